```python
import jax, jax.numpy as jnp
from jax import lax
import numpy as np

D_MODEL = 1024
BATCH = 8
SEQ = 8192
DEPTH = 1

GLA_HEADS = 4
GLA_DK = 64
GLA_DV = 128
GLA_GATE_RANK = 16
GLA_GATE_TAU = 16.0
GLA_CHUNK = 64
GLA_QK = GLA_HEADS * GLA_DK
GLA_V = GLA_HEADS * GLA_DV
ATTN_HEADS = 8
ATTN_HEAD_DIM = 64
ATTN_DIM = ATTN_HEADS * ATTN_HEAD_DIM
DILATED_PAIRS = ((128, 1), (512, 4), (2048, 16))
ATTN_BLOCK = 128
MIX_WIDTH = GLA_V + ATTN_DIM
IN_SPLITS = (GLA_QK, GLA_QK, GLA_V, GLA_V, GLA_GATE_RANK, ATTN_DIM, ATTN_DIM, ATTN_DIM)
IN_WIDTH = sum(IN_SPLITS)
D_FF = 2816
CONV_WIDTH = 3
N_MOD = 6
EPS = 1e-6

kernel_name = "hybrid_gla_dilated_attn_convffn_block"


def rms_norm(x, g):
    xf = x.astype(jnp.float32)
    y = xf * lax.rsqrt(jnp.mean(xf * xf, axis=-1, keepdims=True) + EPS)
    return (y * g.astype(jnp.float32)).astype(x.dtype)


def alibi_slopes(n):
    return jnp.asarray([2.0 ** (-8.0 * (h + 1) / n) for h in range(n)], dtype=jnp.float32)


def gla_group(q, k, v, log_a, r, norm_g):
    B, S, H, _ = q.shape
    n = S // GLA_CHUNK

    def chunks(t):
        return t.reshape(B, n, GLA_CHUNK, H, t.shape[-1]).transpose(1, 0, 3, 2, 4).astype(jnp.float32)

    qc = chunks(q) * (GLA_DK ** -0.5)
    kc, vc, gc = chunks(k), chunks(v), chunks(log_a)
    causal = jnp.tril(jnp.ones((GLA_CHUNK, GLA_CHUNK), dtype=bool))

    def step(state, inp):
        qi, ki, vi, gi = inp
        b = jnp.cumsum(gi, axis=-2)
        b_last = b[..., -1, :]
        o_inter = jnp.einsum('bhck,bhkv->bhcv', qi * jnp.exp(b), state)
        diff = b[:, :, :, None, :] - b[:, :, None, :, :]
        decay = jnp.exp(jnp.where(causal[:, :, None], diff, -jnp.inf))
        scores = jnp.einsum('bhik,bhjk,bhijk->bhij', qi, ki, decay)
        o = o_inter + jnp.einsum('bhij,bhjv->bhiv', scores, vi)
        state = state * jnp.exp(b_last)[..., None] + jnp.einsum(
            'bhck,bhcv->bhkv', ki * jnp.exp(b_last[:, :, None, :] - b), vi)
        return state, o

    state0 = jnp.zeros((B, H, GLA_DK, GLA_DV), jnp.float32)
    _, o = lax.scan(step, state0, (qc, kc, vc, gc))
    o = o.transpose(1, 0, 3, 2, 4).reshape(B, S, H, GLA_DV)
    o = rms_norm(o, norm_g).reshape(B, S, H * GLA_DV)
    return (o * jax.nn.silu(r.astype(jnp.float32))).astype(r.dtype)


def dilated_branch(q, k, v, slopes, window, dilation):
    B, S, H, E = q.shape
    L = S // dilation
    span = window // dilation
    nb = -(-L // ATTN_BLOCK)
    pad = nb * ATTN_BLOCK - L

    def to_blocks(t):
        t = t.reshape(B, L, dilation, H, E).transpose(0, 2, 3, 1, 4)
        t = jnp.pad(t, ((0, 0), (0, 0), (0, 0), (0, pad), (0, 0)))
        return t.reshape(B, dilation, H, nb, ATTN_BLOCK, E)

    def with_prev(t):
        prev = jnp.pad(t, ((0, 0), (0, 0), (0, 0), (1, 0), (0, 0), (0, 0)))[:, :, :, :-1]
        return jnp.concatenate([prev, t], axis=4)

    qb = to_blocks(q)
    kw = with_prev(to_blocks(k))
    vw = with_prev(to_blocks(v))
    s = jnp.einsum('bdhnqe,bdhnke->bdhnqk', qb, kw).astype(jnp.float32) * (E ** -0.5)
    iq = jnp.arange(ATTN_BLOCK)[:, None]
    ik = jnp.arange(2 * ATTN_BLOCK)[None, :]
    rel = iq + ATTN_BLOCK - ik
    key_idx = jnp.arange(nb)[:, None, None] * ATTN_BLOCK + ik - ATTN_BLOCK
    valid = (rel >= 0) & (rel <= span) & (key_idx >= 0)
    alibi = -slopes[:, None, None, None] * (dilation * rel).astype(jnp.float32)
    s = jnp.where(valid, s + alibi, -jnp.inf)
    m = jnp.max(s, axis=-1, keepdims=True)
    p = jnp.exp(s - m)
    den = jnp.sum(p, axis=-1, keepdims=True)
    o = jnp.einsum('bdhnqk,bdhnke->bdhnqe', p, vw.astype(jnp.float32)) / den
    lse = (m + jnp.log(den))[..., 0]
    o = o.reshape(B, dilation, H, nb * ATTN_BLOCK, E)[:, :, :, :L]
    o = o.transpose(0, 3, 1, 2, 4).reshape(B, S, H, E)
    lse = lse.reshape(B, dilation, H, nb * ATTN_BLOCK)[:, :, :, :L]
    lse = lse.transpose(0, 3, 1, 2).reshape(B, S, H)
    return o, lse


def dilated_attention_group(q, k, v):
    slopes = alibi_slopes(ATTN_HEADS)
    outs, lses = [], []
    for window, dilation in DILATED_PAIRS:
        o, lse = dilated_branch(q, k, v, slopes, window, dilation)
        outs.append(o)
        lses.append(lse)
    weights = jax.nn.softmax(jnp.stack(lses, axis=0), axis=0)
    return jnp.einsum('gbsh,gbshe->bshe', weights, jnp.stack(outs, axis=0))


def causal_depthwise_conv(u, w, b):
    K, C = w.shape
    y = lax.conv_general_dilated(u, w[:, None, :], window_strides=(1,), padding=[(K - 1, 0)],
                                 dimension_numbers=('NWC', 'WIO', 'NWC'), feature_group_count=C)
    return y + b


def _fwd_setup_inputs(seed: int = 0) -> dict:
    key = jax.random.key(seed)
    ks = jax.random.split(key, 17)
    f32 = jnp.float32
    L = DEPTH

    def nrm(k, shape, scale):
        return jax.random.normal(k, shape, f32) * scale

    return {
        "x": nrm(ks[0], (BATCH, SEQ, D_MODEL), 1.0),
        "c": nrm(ks[1], (BATCH, D_MODEL), 1.0),
        "w_ada": nrm(ks[2], (L, D_MODEL, N_MOD * D_MODEL), 0.5 * D_MODEL ** -0.5),
        "b_ada": nrm(ks[3], (L, N_MOD * D_MODEL), 0.02),
        "norm1_g": 1.0 + nrm(ks[4], (L, D_MODEL), 0.02),
        "w_in": nrm(ks[5], (L, D_MODEL, IN_WIDTH), D_MODEL ** -0.5),
        "gla_w_gate": nrm(ks[6], (L, GLA_GATE_RANK, GLA_QK), GLA_GATE_RANK ** -0.5),
        "gla_b_gate": nrm(ks[7], (L, GLA_QK), 0.02),
        "gla_norm_g": 1.0 + nrm(ks[8], (L, GLA_DV), 0.02),
        "q_norm_g": 1.0 + nrm(ks[9], (L, ATTN_HEAD_DIM), 0.02),
        "k_norm_g": 1.0 + nrm(ks[10], (L, ATTN_HEAD_DIM), 0.02),
        "w_out": nrm(ks[11], (L, MIX_WIDTH, D_MODEL), MIX_WIDTH ** -0.5),
        "norm2_g": 1.0 + nrm(ks[12], (L, D_MODEL), 0.02),
        "w_up": nrm(ks[13], (L, D_MODEL, 2 * D_FF), D_MODEL ** -0.5),
        "conv_w": nrm(ks[14], (L, CONV_WIDTH, 2 * D_FF), CONV_WIDTH ** -0.5),
        "conv_b": nrm(ks[15], (L, 2 * D_FF), 0.02),
        "w_down": nrm(ks[16], (L, D_FF, D_MODEL), D_FF ** -0.5),
    }


def _fwd_reference(x, c, w_ada, b_ada, norm1_g, w_in, gla_w_gate, gla_b_gate, gla_norm_g,
              q_norm_g, k_norm_g, w_out, norm2_g, w_up, conv_w, conv_b, w_down):
    B, S, _ = x.shape
    cond = jax.nn.silu(c)
    split_at = np.cumsum(IN_SPLITS)[:-1].tolist()
    for l in range(DEPTH):
        mod = (cond @ w_ada[l] + b_ada[l])[:, None, :]
        sh1, sc1, g1, sh2, sc2, g2 = jnp.split(mod, N_MOD, axis=-1)

        h = rms_norm(x, norm1_g[l]) * (1 + sc1) + sh1
        proj = h @ w_in[l]
        gq, gk, gv, gr, glr, aq, ak, av = jnp.split(proj, split_at, axis=-1)
        log_a = jax.nn.log_sigmoid((glr @ gla_w_gate[l] + gla_b_gate[l]).astype(jnp.float32)) / GLA_GATE_TAU
        y_gla = gla_group(gq.reshape(B, S, GLA_HEADS, GLA_DK), gk.reshape(B, S, GLA_HEADS, GLA_DK),
                          gv.reshape(B, S, GLA_HEADS, GLA_DV), log_a.reshape(B, S, GLA_HEADS, GLA_DK),
                          gr, gla_norm_g[l])
        qa = rms_norm(aq.reshape(B, S, ATTN_HEADS, ATTN_HEAD_DIM), q_norm_g[l])
        ka = rms_norm(ak.reshape(B, S, ATTN_HEADS, ATTN_HEAD_DIM), k_norm_g[l])
        y_att = dilated_attention_group(qa, ka, av.reshape(B, S, ATTN_HEADS, ATTN_HEAD_DIM))
        mixed = jnp.concatenate([y_gla.astype(x.dtype), y_att.reshape(B, S, ATTN_DIM).astype(x.dtype)], axis=-1)
        x = x + g1 * (mixed @ w_out[l])

        h = rms_norm(x, norm2_g[l]) * (1 + sc2) + sh2
        u = causal_depthwise_conv(h @ w_up[l], conv_w[l], conv_b[l])
        u_gate, u_val = jnp.split(u, 2, axis=-1)
        x = x + g2 * ((jax.nn.silu(u_gate) * u_val) @ w_down[l])
    return x


import jax as _jax
import jax.numpy as _jnp

TWIN_FORMAT = 'train_step'
FWD_PARAMS = ['x', 'c', 'w_ada', 'b_ada', 'norm1_g', 'w_in', 'gla_w_gate', 'gla_b_gate', 'gla_norm_g', 'q_norm_g', 'k_norm_g', 'w_out', 'norm2_g', 'w_up', 'conv_w', 'conv_b', 'w_down']
TWIN_WEIGHTS = ['w_ada', 'b_ada', 'norm1_g', 'w_in', 'gla_w_gate', 'gla_b_gate', 'gla_norm_g', 'q_norm_g', 'k_norm_g', 'w_out', 'norm2_g', 'w_up', 'conv_w', 'conv_b', 'w_down']
TWIN_DIFF_INPUT = 'x'
TWIN_INPUTS = ['x', 'c', 'w_ada', 'b_ada', 'norm1_g', 'w_in', 'gla_w_gate', 'gla_b_gate', 'gla_norm_g', 'q_norm_g', 'k_norm_g', 'w_out', 'norm2_g', 'w_up', 'conv_w', 'conv_b', 'w_down', 'loss_target', 'm_w_ada', 'm_b_ada', 'm_norm1_g', 'm_w_in', 'm_gla_w_gate', 'm_gla_b_gate', 'm_gla_norm_g', 'm_q_norm_g', 'm_k_norm_g', 'm_w_out', 'm_norm2_g', 'm_w_up', 'm_conv_w', 'm_conv_b', 'm_w_down', 'v_w_ada', 'v_b_ada', 'v_norm1_g', 'v_w_in', 'v_gla_w_gate', 'v_gla_b_gate', 'v_gla_norm_g', 'v_q_norm_g', 'v_k_norm_g', 'v_w_out', 'v_norm2_g', 'v_w_up', 'v_conv_w', 'v_conv_b', 'v_w_down']
TWIN_OUTPUTS = ['loss', 'grad_x', 'grad_w_ada', 'grad_b_ada', 'grad_norm1_g', 'grad_w_in', 'grad_gla_w_gate', 'grad_gla_b_gate', 'grad_gla_norm_g', 'grad_q_norm_g', 'grad_k_norm_g', 'grad_w_out', 'grad_norm2_g', 'grad_w_up', 'grad_conv_w', 'grad_conv_b', 'grad_w_down', 'delta_w_ada', 'delta_b_ada', 'delta_norm1_g', 'delta_w_in', 'delta_gla_w_gate', 'delta_gla_b_gate', 'delta_gla_norm_g', 'delta_q_norm_g', 'delta_k_norm_g', 'delta_w_out', 'delta_norm2_g', 'delta_w_up', 'delta_conv_w', 'delta_conv_b', 'delta_w_down', 'new_m_w_ada', 'new_m_b_ada', 'new_m_norm1_g', 'new_m_w_in', 'new_m_gla_w_gate', 'new_m_gla_b_gate', 'new_m_gla_norm_g', 'new_m_q_norm_g', 'new_m_k_norm_g', 'new_m_w_out', 'new_m_norm2_g', 'new_m_w_up', 'new_m_conv_w', 'new_m_conv_b', 'new_m_w_down', 'new_v_w_ada', 'new_v_b_ada', 'new_v_norm1_g', 'new_v_w_in', 'new_v_gla_w_gate', 'new_v_gla_b_gate', 'new_v_gla_norm_g', 'new_v_q_norm_g', 'new_v_k_norm_g', 'new_v_w_out', 'new_v_norm2_g', 'new_v_w_up', 'new_v_conv_w', 'new_v_conv_b', 'new_v_w_down']
TWIN_LEAF_KINDS = {'loss': 'loss', 'grad_x': 'grad_x', 'grad_w_ada': 'grad_w', 'grad_b_ada': 'grad_w', 'grad_norm1_g': 'grad_w', 'grad_w_in': 'grad_w', 'grad_gla_w_gate': 'grad_w', 'grad_gla_b_gate': 'grad_w', 'grad_gla_norm_g': 'grad_w', 'grad_q_norm_g': 'grad_w', 'grad_k_norm_g': 'grad_w', 'grad_w_out': 'grad_w', 'grad_norm2_g': 'grad_w', 'grad_w_up': 'grad_w', 'grad_conv_w': 'grad_w', 'grad_conv_b': 'grad_w', 'grad_w_down': 'grad_w', 'delta_w_ada': 'delta_w', 'delta_b_ada': 'delta_w', 'delta_norm1_g': 'delta_w', 'delta_w_in': 'delta_w', 'delta_gla_w_gate': 'delta_w', 'delta_gla_b_gate': 'delta_w', 'delta_gla_norm_g': 'delta_w', 'delta_q_norm_g': 'delta_w', 'delta_k_norm_g': 'delta_w', 'delta_w_out': 'delta_w', 'delta_norm2_g': 'delta_w', 'delta_w_up': 'delta_w', 'delta_conv_w': 'delta_w', 'delta_conv_b': 'delta_w', 'delta_w_down': 'delta_w', 'new_m_w_ada': 'new_m', 'new_m_b_ada': 'new_m', 'new_m_norm1_g': 'new_m', 'new_m_w_in': 'new_m', 'new_m_gla_w_gate': 'new_m', 'new_m_gla_b_gate': 'new_m', 'new_m_gla_norm_g': 'new_m', 'new_m_q_norm_g': 'new_m', 'new_m_k_norm_g': 'new_m', 'new_m_w_out': 'new_m', 'new_m_norm2_g': 'new_m', 'new_m_w_up': 'new_m', 'new_m_conv_w': 'new_m', 'new_m_conv_b': 'new_m', 'new_m_w_down': 'new_m', 'new_v_w_ada': 'new_v', 'new_v_b_ada': 'new_v', 'new_v_norm1_g': 'new_v', 'new_v_w_in': 'new_v', 'new_v_gla_w_gate': 'new_v', 'new_v_gla_b_gate': 'new_v', 'new_v_gla_norm_g': 'new_v', 'new_v_q_norm_g': 'new_v', 'new_v_k_norm_g': 'new_v', 'new_v_w_out': 'new_v', 'new_v_norm2_g': 'new_v', 'new_v_w_up': 'new_v', 'new_v_conv_w': 'new_v', 'new_v_conv_b': 'new_v', 'new_v_w_down': 'new_v'}


def _forward(args):
    return _fwd_reference(*[args[k] for k in FWD_PARAMS])


def _output_shape():
    def fwd():
        inp = _fwd_setup_inputs(0)
        return _fwd_reference(*[inp[k] for k in FWD_PARAMS])
    out = _jax.eval_shape(fwd)
    return out.shape, out.dtype

N_MICROBATCH = 1
ADAM_LR = 0.001
ADAM_B1 = 0.9
ADAM_B2 = 0.999
ADAM_EPS = 1e-08
ADAM_WD = 0.01
ADAM_STEP = 10
PER_EXAMPLE_BATCH_AXIS = {'x': 0, 'c': 0, 'loss_target': 0}
SHARED_INPUTS = []
_WEIGHT_DTYPES = {'w_ada': _jnp.float32, 'b_ada': _jnp.float32, 'norm1_g': _jnp.float32, 'w_in': _jnp.float32, 'gla_w_gate': _jnp.float32, 'gla_b_gate': _jnp.float32, 'gla_norm_g': _jnp.float32, 'q_norm_g': _jnp.float32, 'k_norm_g': _jnp.float32, 'w_out': _jnp.float32, 'norm2_g': _jnp.float32, 'w_up': _jnp.float32, 'conv_w': _jnp.float32, 'conv_b': _jnp.float32, 'w_down': _jnp.float32}
MOMENT_SCALE = {'w_ada': 1.373448e+00, 'b_ada': 3.747137e+00, 'norm1_g': 1.673370e+00, 'w_in': 1.414812e-01, 'gla_w_gate': 2.548197e-02, 'gla_b_gate': 7.051496e-02, 'gla_norm_g': 1.011971e+01, 'q_norm_g': 1.176834e+00, 'k_norm_g': 1.174914e+00, 'w_out': 1.567145e-01, 'norm2_g': 6.630490e+00, 'w_up': 1.630962e-01, 'conv_w': 1.008421e+00, 'conv_b': 8.093521e-01, 'w_down': 1.218344e-01}


def _to_microbatches(a, axis):
    t = _jnp.moveaxis(a, axis, 0)
    t = t.reshape((N_MICROBATCH, t.shape[0] // N_MICROBATCH) + t.shape[1:])
    return _jnp.moveaxis(t, 1, axis + 1)


def setup_inputs(seed: int = 0) -> dict:
    inp = _fwd_setup_inputs(seed)
    key = _jax.random.fold_in(_jax.random.key(seed), 7919)
    shape, _ = _output_shape()
    out = dict(inp)
    out["loss_target"] = _jax.random.normal(_jax.random.fold_in(key, 0), shape, _jnp.float32)
    for i, name in enumerate(TWIN_WEIGHTS):
        w = inp[name].astype(_jnp.float32)
        if MOMENT_SCALE is None:
            s = _jnp.sqrt(_jnp.mean(_jnp.square(w)) + 1e-30)
        else:
            s = MOMENT_SCALE[name]
        km, kv = _jax.random.split(_jax.random.fold_in(key, i + 1))
        out[name] = w
        out["m_" + name] = s * _jax.random.normal(km, w.shape, _jnp.float32)
        out["v_" + name] = (s * s) * _jax.random.uniform(kv, w.shape, _jnp.float32, 0.5, 1.5)
    if N_MICROBATCH > 1:
        for name, axis in PER_EXAMPLE_BATCH_AXIS.items():
            out[name] = _to_microbatches(out[name], axis)
    return {'x': out['x'], 'c': out['c'], 'w_ada': out['w_ada'], 'b_ada': out['b_ada'], 'norm1_g': out['norm1_g'], 'w_in': out['w_in'], 'gla_w_gate': out['gla_w_gate'], 'gla_b_gate': out['gla_b_gate'], 'gla_norm_g': out['gla_norm_g'], 'q_norm_g': out['q_norm_g'], 'k_norm_g': out['k_norm_g'], 'w_out': out['w_out'], 'norm2_g': out['norm2_g'], 'w_up': out['w_up'], 'conv_w': out['conv_w'], 'conv_b': out['conv_b'], 'w_down': out['w_down'], 'loss_target': out['loss_target'], 'm_w_ada': out['m_w_ada'], 'm_b_ada': out['m_b_ada'], 'm_norm1_g': out['m_norm1_g'], 'm_w_in': out['m_w_in'], 'm_gla_w_gate': out['m_gla_w_gate'], 'm_gla_b_gate': out['m_gla_b_gate'], 'm_gla_norm_g': out['m_gla_norm_g'], 'm_q_norm_g': out['m_q_norm_g'], 'm_k_norm_g': out['m_k_norm_g'], 'm_w_out': out['m_w_out'], 'm_norm2_g': out['m_norm2_g'], 'm_w_up': out['m_w_up'], 'm_conv_w': out['m_conv_w'], 'm_conv_b': out['m_conv_b'], 'm_w_down': out['m_w_down'], 'v_w_ada': out['v_w_ada'], 'v_b_ada': out['v_b_ada'], 'v_norm1_g': out['v_norm1_g'], 'v_w_in': out['v_w_in'], 'v_gla_w_gate': out['v_gla_w_gate'], 'v_gla_b_gate': out['v_gla_b_gate'], 'v_gla_norm_g': out['v_gla_norm_g'], 'v_q_norm_g': out['v_q_norm_g'], 'v_k_norm_g': out['v_k_norm_g'], 'v_w_out': out['v_w_out'], 'v_norm2_g': out['v_norm2_g'], 'v_w_up': out['v_w_up'], 'v_conv_w': out['v_conv_w'], 'v_conv_b': out['v_conv_b'], 'v_w_down': out['v_w_down']}


def _loss(weights, diff, rest, loss_target):
    with _jax.named_scope("forward"):
        args = {**rest, TWIN_DIFF_INPUT: diff, **{k: w.astype(_WEIGHT_DTYPES[k]) for k, w in weights.items()}}
        y = _forward(args)
    with _jax.named_scope("loss_head"):
        err = _jnp.square(y.astype(_jnp.float32) - loss_target)
        return 0.5 * _jnp.sum(_jnp.mean(err, axis=-1)) if err.ndim else 0.5 * err


def _adamw(w, g, m, v):
    m = ADAM_B1 * m + (1.0 - ADAM_B1) * g
    v = ADAM_B2 * v + (1.0 - ADAM_B2) * _jnp.square(g)
    m_hat = m / (1.0 - ADAM_B1 ** ADAM_STEP)
    v_hat = v / (1.0 - ADAM_B2 ** ADAM_STEP)
    delta = -ADAM_LR * (m_hat / (_jnp.sqrt(v_hat) + ADAM_EPS) + ADAM_WD * w)
    return delta, m, v


def reference(x, c, w_ada, b_ada, norm1_g, w_in, gla_w_gate, gla_b_gate, gla_norm_g, q_norm_g, k_norm_g, w_out, norm2_g, w_up, conv_w, conv_b, w_down, loss_target, m_w_ada, m_b_ada, m_norm1_g, m_w_in, m_gla_w_gate, m_gla_b_gate, m_gla_norm_g, m_q_norm_g, m_k_norm_g, m_w_out, m_norm2_g, m_w_up, m_conv_w, m_conv_b, m_w_down, v_w_ada, v_b_ada, v_norm1_g, v_w_in, v_gla_w_gate, v_gla_b_gate, v_gla_norm_g, v_q_norm_g, v_k_norm_g, v_w_out, v_norm2_g, v_w_up, v_conv_w, v_conv_b, v_w_down):
    given = dict(x=x, c=c, w_ada=w_ada, b_ada=b_ada, norm1_g=norm1_g, w_in=w_in, gla_w_gate=gla_w_gate, gla_b_gate=gla_b_gate, gla_norm_g=gla_norm_g, q_norm_g=q_norm_g, k_norm_g=k_norm_g, w_out=w_out, norm2_g=norm2_g, w_up=w_up, conv_w=conv_w, conv_b=conv_b, w_down=w_down, loss_target=loss_target, m_w_ada=m_w_ada, m_b_ada=m_b_ada, m_norm1_g=m_norm1_g, m_w_in=m_w_in, m_gla_w_gate=m_gla_w_gate, m_gla_b_gate=m_gla_b_gate, m_gla_norm_g=m_gla_norm_g, m_q_norm_g=m_q_norm_g, m_k_norm_g=m_k_norm_g, m_w_out=m_w_out, m_norm2_g=m_norm2_g, m_w_up=m_w_up, m_conv_w=m_conv_w, m_conv_b=m_conv_b, m_w_down=m_w_down, v_w_ada=v_w_ada, v_b_ada=v_b_ada, v_norm1_g=v_norm1_g, v_w_in=v_w_in, v_gla_w_gate=v_gla_w_gate, v_gla_b_gate=v_gla_b_gate, v_gla_norm_g=v_gla_norm_g, v_q_norm_g=v_q_norm_g, v_k_norm_g=v_k_norm_g, v_w_out=v_w_out, v_norm2_g=v_norm2_g, v_w_up=v_w_up, v_conv_w=v_conv_w, v_conv_b=v_conv_b, v_w_down=v_w_down)
    weights = {n: given[n] for n in TWIN_WEIGHTS}
    shared = {n: given[n] for n in SHARED_INPUTS}
    per_example = {n: given[n] for n in ['x', 'c']}
    grad_fn = _jax.value_and_grad(_loss, argnums=(0, 1))

    def one_microbatch(ex, loss_target):
        ex = dict(ex)
        diff = ex.pop(TWIN_DIFF_INPUT)
        return grad_fn(weights, diff, {**shared, **ex}, loss_target)

    if N_MICROBATCH == 1:
        loss, (grad_w, grad_x) = one_microbatch(per_example, given["loss_target"])
    else:
        def body(carry, xs):
            loss_sum, grad_sum = carry
            l_k, (gw_k, gx_k) = one_microbatch(xs[0], xs[1])
            with _jax.named_scope("update"):
                return (loss_sum + l_k, _jax.tree.map(_jnp.add, grad_sum, gw_k)), gx_k

        init = (_jnp.zeros((), _jnp.float32), _jax.tree.map(_jnp.zeros_like, weights))
        (loss, grad_w), grad_x = _jax.lax.scan(body, init, (per_example, given["loss_target"]))
    with _jax.named_scope("update"):
        delta_w, new_m, new_v = {}, {}, {}
        for n in TWIN_WEIGHTS:
            delta_w[n], new_m[n], new_v[n] = _adamw(weights[n], grad_w[n], given["m_" + n], given["v_" + n])
    return (loss, grad_x, *[grad_w[n] for n in TWIN_WEIGHTS], *[delta_w[n] for n in TWIN_WEIGHTS],
            *[new_m[n] for n in TWIN_WEIGHTS], *[new_v[n] for n in TWIN_WEIGHTS])
```

```python
import functools

import jax
import jax.numpy as jnp
from jax import lax
from jax.experimental import pallas as pl
from jax.experimental.pallas import tpu as pltpu

F32, BF16 = jnp.float32, jnp.bfloat16
HIGHEST = lax.Precision.HIGHEST
MESH = pl.DeviceIdType.MESH

D_MODEL = 1024
GLA_CHUNK = 64
GLA_GATE_TAU = 16.0
GLA_GATE_RANK = 16
HEAD_LANES = 128
ATTN_BLOCK = 128
DILATIONS = (1, 4, 16)
ALIBI_SLOPES = tuple(2.0 ** (-(h + 1)) for h in range(8))
D_FF = 2816
EPS = 1e-6
NEG = -1e30
C_GQ, C_GK, C_GV, C_GR, C_AQ, C_AK, C_AV, C_LR, PROJ_W = 0, 256, 512, 1024, 1536, 2048, 2560, 3072, 3200
ADAM_LR, ADAM_B1, ADAM_B2, ADAM_EPS, ADAM_WD, ADAM_STEP = 0.001, 0.9, 0.999, 1e-08, 0.01, 10
VMEM_LIMIT_BYTES = 56 * 1024 * 1024
ROW_TILE = 256


def _params(*sem):
    return pltpu.CompilerParams(dimension_semantics=sem or None, vmem_limit_bytes=VMEM_LIMIT_BYTES)


def _nt(a, b):
    return lax.dot_general(a, b, (((1,), (1,)), ((), ())), preferred_element_type=F32)


def _tn(a, b):
    return lax.dot_general(a, b, (((0,), (0,)), ((), ())), preferred_element_type=F32)


def _nn(a, b, precision=None):
    return jnp.dot(a, b, preferred_element_type=F32, precision=precision)


def _fold8(v):
    return v.reshape(v.shape[0] // 8, 8, v.shape[1]).sum(axis=0)


def _spread_total(ref):
    t = ref[...]
    ref[...] = jnp.broadcast_to(jnp.sum(t, axis=-2, keepdims=True), t.shape)


def _sigmoid(x):
    return 1.0 / (1.0 + jnp.exp(-x))


def _mm(a, b, *, ta=False, tb=False, out_dtype=F32, tm, tn, tk, name):
    (k_a, m) = a.shape if ta else a.shape[::-1]
    (k_b, n) = b.shape[::-1] if tb else b.shape
    assert k_a == k_b and m % tm == 0 and n % tn == 0 and k_a % tk == 0, (name, a.shape, b.shape)
    nk = k_a // tk
    dims = (((0 if ta else 1,), (1 if tb else 0,)), ((), ()))

    def body(a_ref, b_ref, o_ref, acc_ref):
        k = pl.program_id(2)
        part = lax.dot_general(a_ref[...].astype(BF16), b_ref[...].astype(BF16), dims, preferred_element_type=F32)
        if nk == 1:
            o_ref[...] = part.astype(out_dtype)
        else:
            @pl.when(k == 0)
            def _():
                acc_ref[...] = part

            @pl.when(k > 0)
            def _():
                acc_ref[...] += part

            @pl.when(k == nk - 1)
            def _():
                o_ref[...] = acc_ref[...].astype(out_dtype)

    a_spec = pl.BlockSpec((tk, tm), lambda i, j, k: (k, i)) if ta else pl.BlockSpec((tm, tk), lambda i, j, k: (i, k))
    b_spec = pl.BlockSpec((tn, tk), lambda i, j, k: (j, k)) if tb else pl.BlockSpec((tk, tn), lambda i, j, k: (k, j))
    return pl.pallas_call(
        body, name=name, grid=(m // tm, n // tn, nk), in_specs=[a_spec, b_spec],
        out_specs=pl.BlockSpec((tm, tn), lambda i, j, k: (i, j)), out_shape=jax.ShapeDtypeStruct((m, n), out_dtype),
        scratch_shapes=[pltpu.VMEM((tm, tn), F32)], compiler_params=_params("parallel", "parallel", "arbitrary"),
    )(a, b)


def _norm_mod_fwd(x, branch, gate, gain, scale, shift, *, name):
    s, d = x.shape
    tm = ROW_TILE
    has_branch = branch is not None

    def body(*refs):
        if has_branch:
            x_ref, br_ref, gate_ref, gain_ref, sc_ref, sh_ref, x1_ref, h_ref = refs
            xv = x_ref[...] + gate_ref[...] * br_ref[...]
            x1_ref[...] = xv
        else:
            x_ref, gain_ref, sc_ref, sh_ref, h_ref = refs
            xv = x_ref[...]
        r = lax.rsqrt(jnp.mean(xv * xv, axis=-1, keepdims=True) + EPS)
        h_ref[...] = ((xv * r) * gain_ref[...] * (1.0 + sc_ref[...]) + sh_ref[...]).astype(BF16)

    row = pl.BlockSpec((tm, d), lambda i: (i, 0))
    vec = pl.BlockSpec((1, d), lambda i: (0, 0))
    if has_branch:
        return pl.pallas_call(
            body, name=name, grid=(s // tm,), in_specs=[row, row, vec, vec, vec, vec], out_specs=[row, row],
            out_shape=[jax.ShapeDtypeStruct((s, d), F32), jax.ShapeDtypeStruct((s, d), BF16)],
            compiler_params=_params("parallel"))(x, branch, gate, gain, scale, shift)
    h = pl.pallas_call(
        body, name=name, grid=(s // tm,), in_specs=[row, vec, vec, vec], out_specs=row,
        out_shape=jax.ShapeDtypeStruct((s, d), BF16), compiler_params=_params("parallel"))(x, gain, scale, shift)
    return x, h


def _norm_mod_bwd(x, dh, dres, gain, scale, branch, gate, *, name):
    s, d = x.shape
    tm = ROW_TILE
    has_branch = branch is not None

    def body(*refs):
        if has_branch:
            x_ref, dh_ref, dres_ref, gain_ref, sc_ref, br_ref, gate_ref, dx_ref, dbr_ref, sums_ref = refs
        else:
            x_ref, dh_ref, dres_ref, gain_ref, sc_ref, dx_ref, sums_ref = refs
        i = pl.program_id(0)

        @pl.when(i == 0)
        def _():
            sums_ref[...] = jnp.zeros_like(sums_ref)

        xv, dhv = x_ref[...], dh_ref[...]
        r = lax.rsqrt(jnp.mean(xv * xv, axis=-1, keepdims=True) + EPS)
        xn = xv * r
        dxn = dhv * (gain_ref[...] * (1.0 + sc_ref[...]))
        dx = dres_ref[...] + r * (dxn - xn * jnp.mean(dxn * xn, axis=-1, keepdims=True))
        dx_ref[...] = dx
        sums_ref[0] += _fold8(dhv * xn)
        sums_ref[1] += _fold8(dhv)
        if has_branch:
            dbr_ref[...] = (gate_ref[...] * dx).astype(BF16)
            sums_ref[2] += _fold8(dx * br_ref[...])

        @pl.when(i == s // tm - 1)
        def _():
            _spread_total(sums_ref)

    row = pl.BlockSpec((tm, d), lambda i: (i, 0))
    vec = pl.BlockSpec((1, d), lambda i: (0, 0))
    sums = pl.BlockSpec((3, 8, d), lambda i: (0, 0, 0))
    sums_shape = jax.ShapeDtypeStruct((3, 8, d), F32)
    if has_branch:
        return pl.pallas_call(
            body, name=name, grid=(s // tm,), in_specs=[row, row, row, vec, vec, row, vec], out_specs=[row, row, sums],
            out_shape=[jax.ShapeDtypeStruct((s, d), F32), jax.ShapeDtypeStruct((s, d), BF16), sums_shape],
            compiler_params=_params("arbitrary"))(x, dh, dres, gain, scale, branch, gate)
    dx, sm = pl.pallas_call(
        body, name=name, grid=(s // tm,), in_specs=[row, row, row, vec, vec], out_specs=[row, sums],
        out_shape=[jax.ShapeDtypeStruct((s, d), F32), sums_shape],
        compiler_params=_params("arbitrary"))(x, dh, dres, gain, scale)
    return dx, None, sm


GLA_ROWS = 256


def _gla_chunk_setup(lr_ref, wg_ref, bg_ref, rows):
    c = GLA_CHUNK
    ri = lax.broadcasted_iota(jnp.int32, (c, c), 0)
    ci = lax.broadcasted_iota(jnp.int32, (c, c), 1)
    z = _nn(lr_ref[rows, :].astype(BF16), wg_ref[...]) + bg_ref[...]
    g = (jnp.minimum(z, 0.0) - jnp.log(1.0 + jnp.exp(-jnp.abs(z)))) * (1.0 / GLA_GATE_TAU)
    b = _nn((ci <= ri).astype(F32), g, precision=HIGHEST)
    return z, b, ci <= ri


def _last_row(b):
    ri = lax.broadcasted_iota(jnp.int32, b.shape, 0)
    return jnp.sum(jnp.where(ri == b.shape[0] - 1, b, 0.0), axis=0, keepdims=True)


def _gla_fwd(proj, wg, bg, gn, *, name):
    s = proj.shape[0]
    tb, c = GLA_ROWS, GLA_CHUNK
    cb = tb // c

    def body(q_ref, k_ref, v_ref, r_ref, lr_ref, wg_ref, bg_ref, gn_ref, o_ref, y_ref, st_ref, state):
        i = pl.program_id(0)

        @pl.when(i == 0)
        def _():
            state[...] = jnp.zeros_like(state)

        low = lax.broadcasted_iota(jnp.int32, (c, HEAD_LANES), 1) < 64
        for ch in range(cb):
            rows = pl.ds(ch * c, c)
            _, b, causal = _gla_chunk_setup(lr_ref, wg_ref, bg_ref, rows)
            for p in range(2):
                cols = pl.ds(p * HEAD_LANES, HEAD_LANES)
                bp = b[:, p * HEAD_LANES:(p + 1) * HEAD_LANES]
                b_end = _last_row(bp)
                q = q_ref[rows, cols] * 0.125
                k = k_ref[rows, cols]
                q_in = q * jnp.exp(bp)
                k_out = (k * jnp.exp(-bp)).astype(BF16)
                k_end = k * jnp.exp(b_end - bp)
                st = state[p]
                st_ref[ch, p] = st
                st_b = st.astype(BF16)
                upd = jnp.zeros_like(st)
                for e in range(2):
                    msk = low if e == 0 else jnp.logical_not(low)
                    hc = pl.ds((2 * p + e) * HEAD_LANES, HEAD_LANES)
                    qm = jnp.where(msk, q_in, 0.0).astype(BF16)
                    a = jnp.where(causal, _nt(qm, k_out), 0.0)
                    v = v_ref[rows, hc].astype(BF16)
                    o = _nt(qm, st_b) + _nn(a.astype(BF16), v)
                    upd = upd + _tn(v, jnp.where(msk, k_end, 0.0).astype(BF16))
                    o_ref[rows, hc] = o
                    rr = r_ref[rows, hc]
                    on = o * lax.rsqrt(jnp.mean(o * o, axis=-1, keepdims=True) + EPS)
                    y_ref[rows, hc] = (on * gn_ref[...] * (rr * _sigmoid(rr))).astype(BF16)
                state[p] = st * jnp.exp(b_end) + upd

    def col(width, at):
        return pl.BlockSpec((tb, width), lambda i: (i, at // width))

    full = lambda shape: pl.BlockSpec(shape, lambda i: tuple(0 for _ in shape))
    return pl.pallas_call(
        body, name=name, grid=(s // tb,),
        in_specs=[col(256, C_GQ), col(256, C_GK), col(512, C_GV), col(512, C_GR), col(128, C_LR),
                  full((HEAD_LANES, 256)), full((1, 256)), full((1, HEAD_LANES))],
        out_specs=[pl.BlockSpec((tb, 512), lambda i: (i, 0)), pl.BlockSpec((tb, 512), lambda i: (i, 0)),
                   pl.BlockSpec((cb, 2, HEAD_LANES, HEAD_LANES), lambda i: (i, 0, 0, 0))],
        out_shape=[jax.ShapeDtypeStruct((s, 512), F32), jax.ShapeDtypeStruct((s, 512), BF16),
                   jax.ShapeDtypeStruct((s // c, 2, HEAD_LANES, HEAD_LANES), F32)],
        scratch_shapes=[pltpu.VMEM((2, HEAD_LANES, HEAD_LANES), F32)],
        compiler_params=_params("arbitrary"))(proj, proj, proj, proj, proj, wg, bg, gn)


def _gla_bwd(proj, wg, bg, gn, o_raw, states, dmixed, *, name):
    s = proj.shape[0]
    tb, c = GLA_ROWS, GLA_CHUNK
    cb = tb // c
    nblk, nch = s // tb, s // c

    def body(q_ref, k_ref, v_ref, r_ref, lr_ref, wg_ref, bg_ref, gn_ref, o_ref, st_ref, stn_ref, dy_ref,
             dq_ref, dk_ref, dv_ref, dr_ref, dlr_ref, gwg_ref, sums_ref, dstate):
        i = pl.program_id(0)

        @pl.when(i == 0)
        def _():
            dstate[...] = jnp.zeros_like(dstate)
            gwg_ref[...] = jnp.zeros_like(gwg_ref)
            sums_ref[...] = jnp.zeros_like(sums_ref)

        low = lax.broadcasted_iota(jnp.int32, (c, HEAD_LANES), 1) < 64
        for ch in reversed(range(cb)):
            rows = pl.ds(ch * c, c)
            z, b, causal = _gla_chunk_setup(lr_ref, wg_ref, bg_ref, rows)
            upper = jnp.logical_not(causal) | (lax.broadcasted_iota(jnp.int32, (c, c), 0)
                                               == lax.broadcasted_iota(jnp.int32, (c, c), 1))
            lr_b = lr_ref[rows, :].astype(BF16)
            dlr = jnp.zeros((c, HEAD_LANES), F32)
            for p in range(2):
                cols = pl.ds(p * HEAD_LANES, HEAD_LANES)
                sl = slice(p * HEAD_LANES, (p + 1) * HEAD_LANES)
                bp = b[:, sl]
                b_end = _last_row(bp)
                e_in, e_out, e_end = jnp.exp(bp), jnp.exp(-bp), jnp.exp(b_end - bp)
                q = q_ref[rows, cols] * 0.125
                k = k_ref[rows, cols]
                q_in = q * e_in
                k_out = k * e_out
                k_end = k * e_end
                st0 = st_ref[ch, p]
                st1 = st_ref[ch + 1, p] if ch + 1 < cb else stn_ref[0, p]
                dst = dstate[p]
                st0_b, dst_b = st0.astype(BF16), dst.astype(BF16)
                dq_in = jnp.zeros((c, HEAD_LANES), F32)
                dk_out = jnp.zeros((c, HEAD_LANES), F32)
                dk_end = jnp.zeros((c, HEAD_LANES), F32)
                dst_new = dst * jnp.exp(b_end)
                for e in range(2):
                    msk = low if e == 0 else jnp.logical_not(low)
                    hc = pl.ds((2 * p + e) * HEAD_LANES, HEAD_LANES)
                    o = o_ref[rows, hc]
                    rr = r_ref[rows, hc]
                    dy = dy_ref[rows, hc]
                    sg = _sigmoid(rr)
                    rs = lax.rsqrt(jnp.mean(o * o, axis=-1, keepdims=True) + EPS)
                    on = o * rs
                    t = dy * (rr * sg)
                    sums_ref[1, :, hc] += _fold8(t * on)
                    dn = t * gn_ref[...]
                    do = (rs * (dn - on * jnp.mean(dn * on, axis=-1, keepdims=True))).astype(BF16)
                    dr_ref[rows, hc] = (dy * on * gn_ref[...] * (sg * (1.0 + rr * (1.0 - sg)))).astype(BF16)
                    qm = jnp.where(msk, q_in, 0.0).astype(BF16)
                    km_out = jnp.where(msk, k_out, 0.0).astype(BF16)
                    km_end = jnp.where(msk, k_end, 0.0).astype(BF16)
                    v = v_ref[rows, hc].astype(BF16)
                    a = jnp.where(causal, _nt(qm, km_out), 0.0).astype(BF16)
                    da = jnp.where(causal, _nt(do, v), 0.0).astype(BF16)
                    dv_ref[rows, hc] = (_tn(a, do) + _nt(km_end, dst_b)).astype(BF16)
                    dq_in = dq_in + jnp.where(msk, _nn(do, st0_b) + _nn(da, km_out), 0.0)
                    dk_out = dk_out + _tn(da, qm)
                    dk_end = dk_end + jnp.where(msk, _nn(v, dst_b), 0.0)
                    dst_new = dst_new + _tn(do, qm)
                dq = dq_in * e_in
                dk = dk_out * e_out + dk_end * e_end
                dq_ref[rows, cols] = (dq * 0.125).astype(BF16)
                dk_ref[rows, cols] = dk.astype(BF16)
                w = q * dq - k * dk
                dg = _nn(upper.astype(F32), w, precision=HIGHEST) + jnp.sum(dst * st1, axis=0, keepdims=True)
                zp = z[:, sl]
                dz = dg * (1.0 / GLA_GATE_TAU) * _sigmoid(-zp)
                dz_b = dz.astype(BF16)
                sums_ref[0, :, cols] += _fold8(dz)
                dlr = dlr + _nt(dz_b, wg_ref[:, cols])
                gwg_ref[:, cols] += _tn(lr_b, dz_b)
                dstate[p] = dst_new
            dlr_ref[rows, :] = dlr.astype(BF16)

        @pl.when(i == nblk - 1)
        def _():
            _spread_total(sums_ref)

    rev = lambda i: nblk - 1 - i

    def col(width, at):
        return pl.BlockSpec((tb, width), lambda i: (rev(i), at // width))

    full = lambda shape: pl.BlockSpec(shape, lambda i: tuple(0 for _ in shape))
    out_col = lambda width: pl.BlockSpec((tb, width), lambda i: (rev(i), 0))
    return pl.pallas_call(
        body, name=name, grid=(nblk,),
        in_specs=[col(256, C_GQ), col(256, C_GK), col(512, C_GV), col(512, C_GR), col(128, C_LR),
                  full((HEAD_LANES, 256)), full((1, 256)), full((1, HEAD_LANES)),
                  pl.BlockSpec((tb, 512), lambda i: (rev(i), 0)),
                  pl.BlockSpec((cb, 2, HEAD_LANES, HEAD_LANES), lambda i: (rev(i), 0, 0, 0)),
                  pl.BlockSpec((1, 2, HEAD_LANES, HEAD_LANES), lambda i: (jnp.minimum((rev(i) + 1) * cb, nch - 1), 0, 0, 0)),
                  pl.BlockSpec((tb, 512), lambda i: (rev(i), 0))],
        out_specs=[out_col(256), out_col(256), out_col(512), out_col(512), out_col(128),
                   full((HEAD_LANES, 256)), full((2, 8, 512))],
        out_shape=[jax.ShapeDtypeStruct((s, 256), BF16), jax.ShapeDtypeStruct((s, 256), BF16),
                   jax.ShapeDtypeStruct((s, 512), BF16), jax.ShapeDtypeStruct((s, 512), BF16),
                   jax.ShapeDtypeStruct((s, 128), BF16), jax.ShapeDtypeStruct((HEAD_LANES, 256), F32),
                   jax.ShapeDtypeStruct((2, 8, 512), F32)],
        scratch_shapes=[pltpu.VMEM((2, HEAD_LANES, HEAD_LANES), F32)],
        compiler_params=_params("arbitrary"))(proj, proj, proj, proj, proj, wg, bg, gn, o_raw, states, states, dmixed)


def _head_sum_matrix():
    ri = lax.broadcasted_iota(jnp.int32, (512, 512), 0) // 64
    ci = lax.broadcasted_iota(jnp.int32, (512, 512), 1) // 64
    return (ri == ci).astype(F32)


def _attn_prep(proj, qg, kg, *, name):
    s = proj.shape[0]
    tm = ROW_TILE

    def body(q_ref, k_ref, v_ref, qg_ref, kg_ref, qa_ref, ka_ref, va_ref):
        hs = _head_sum_matrix()
        q, k = q_ref[...], k_ref[...]
        qr = lax.rsqrt(_nn(q * q, hs, precision=HIGHEST) * (1.0 / 64) + EPS)
        kr = lax.rsqrt(_nn(k * k, hs, precision=HIGHEST) * (1.0 / 64) + EPS)
        qa_ref[...] = (q * qr * qg_ref[...] * 0.125).astype(BF16)
        ka_ref[...] = (k * kr * kg_ref[...]).astype(BF16)
        va_ref[...] = v_ref[...].astype(BF16)

    col = lambda at: pl.BlockSpec((tm, 512), lambda i: (i, at // 512))
    vec = pl.BlockSpec((1, 512), lambda i: (0, 0))
    out = pl.BlockSpec((tm, 512), lambda i: (i, 0))
    return pl.pallas_call(
        body, name=name, grid=(s // tm,), in_specs=[col(C_AQ), col(C_AK), col(C_AV), vec, vec], out_specs=[out] * 3,
        out_shape=[jax.ShapeDtypeStruct((s, 512), BF16)] * 3, compiler_params=_params("parallel"))(proj, proj, proj, qg, kg)


def _attn_scores(qm, kcat, slope, dil, first):
    blk = ATTN_BLOCK
    iq = lax.broadcasted_iota(jnp.int32, (blk, 2 * blk), 0)
    ik = lax.broadcasted_iota(jnp.int32, (blk, 2 * blk), 1)
    rel = iq + blk - ik
    valid = (rel >= 0) & (rel <= blk) & (jnp.logical_not(first) | (ik >= blk))
    sc = _nt(qm, kcat) - (slope * dil) * rel.astype(F32)
    return jnp.where(valid, sc, NEG), valid


def _dil_attn_fwd(qa, ka, va, dil, *, name):
    s = qa.shape[0]
    l, blk = s // dil, ATTN_BLOCK
    nb = l // blk
    view = lambda t: t.reshape(l, dil * 512)

    def body(q_ref, kp_ref, kc_ref, vp_ref, vc_ref, o_ref, lse_ref):
        first = pl.program_id(1) == 0
        low = lax.broadcasted_iota(jnp.int32, (blk, HEAD_LANES), 1) < 64
        for p in range(4):
            cols = pl.ds(p * HEAD_LANES, HEAD_LANES)
            q2 = q_ref[:, cols]
            kcat = jnp.concatenate([kp_ref[:, cols], kc_ref[:, cols]], axis=0)
            vcat = jnp.concatenate([vp_ref[:, cols], vc_ref[:, cols]], axis=0)
            outs, lses = [], []
            for e in range(2):
                msk = low if e == 0 else jnp.logical_not(low)
                sc, _ = _attn_scores(jnp.where(msk, q2, 0.0).astype(BF16), kcat, ALIBI_SLOPES[2 * p + e], dil, first)
                m = jnp.max(sc, axis=-1, keepdims=True)
                pr = jnp.exp(sc - m)
                den = jnp.sum(pr, axis=-1, keepdims=True)
                outs.append(_nn(pr.astype(BF16), vcat) / den)
                lses.append(m + jnp.log(den))
            o_ref[:, cols] = jnp.where(low, outs[0], outs[1])
            lse_ref[:, cols] = jnp.where(low, lses[0], lses[1])

    cur = pl.BlockSpec((blk, 512), lambda r, n: (n, r))
    prev = pl.BlockSpec((blk, 512), lambda r, n: (jnp.maximum(n - 1, 0), r))
    o, lse = pl.pallas_call(
        body, name=name, grid=(dil, nb), in_specs=[cur, prev, cur, prev, cur], out_specs=[cur, cur],
        out_shape=[jax.ShapeDtypeStruct((l, dil * 512), F32)] * 2,
        compiler_params=_params("parallel", "parallel"))(view(qa), view(ka), view(ka), view(va), view(va))
    return o.reshape(s, 512), lse.reshape(s, 512)


def _attn_merge(branches, y_gla, *, name):
    s = y_gla.shape[0]
    tm = ROW_TILE

    def body(o0, l0, o1, l1, o2, l2, yg_ref, mixed_ref, y_ref, lse_ref):
        m = jnp.maximum(jnp.maximum(l0[...], l1[...]), l2[...])
        w0, w1, w2 = jnp.exp(l0[...] - m), jnp.exp(l1[...] - m), jnp.exp(l2[...] - m)
        zs = w0 + w1 + w2
        y = (w0 * o0[...] + w1 * o1[...] + w2 * o2[...]) / zs
        y_ref[...] = y
        lse_ref[...] = m + jnp.log(zs)
        mixed_ref[:, 0:512] = yg_ref[...]
        mixed_ref[:, 512:1024] = y.astype(BF16)

    blk = pl.BlockSpec((tm, 512), lambda i: (i, 0))
    args = [t for pair in branches for t in pair]
    return pl.pallas_call(
        body, name=name, grid=(s // tm,), in_specs=[blk] * 7,
        out_specs=[pl.BlockSpec((tm, 1024), lambda i: (i, 0)), blk, blk],
        out_shape=[jax.ShapeDtypeStruct((s, 1024), BF16), jax.ShapeDtypeStruct((s, 512), F32),
                   jax.ShapeDtypeStruct((s, 512), F32)],
        compiler_params=_params("parallel"))(*args, y_gla)


def _dil_attn_bwd(qa, ka, va, y_att, lse, dmixed, dil, *, name):
    s = qa.shape[0]
    l, blk = s // dil, ATTN_BLOCK
    nb = l // blk
    view = lambda t: t.reshape(l, dil * t.shape[1])

    def body(q_ref, kp_ref, kc_ref, vp_ref, vc_ref, y_ref, lse_ref, do_ref, dq_ref, dkc_ref, dkp_ref, dvc_ref, dvp_ref):
        first = pl.program_id(1) == 0
        lane = lax.broadcasted_iota(jnp.int32, (blk, HEAD_LANES), 1)
        low = lane < 64
        low_keys = lax.broadcasted_iota(jnp.int32, (2 * blk, HEAD_LANES), 1) < 64
        for p in range(4):
            cols = pl.ds(p * HEAD_LANES, HEAD_LANES)
            q2 = q_ref[:, cols]
            kcat = jnp.concatenate([kp_ref[:, cols], kc_ref[:, cols]], axis=0)
            vcat = jnp.concatenate([vp_ref[:, cols], vc_ref[:, cols]], axis=0)
            do2 = do_ref[:, cols]
            prod = do2 * y_ref[:, cols]
            lse2 = lse_ref[:, cols]
            dq = jnp.zeros((blk, HEAD_LANES), F32)
            dk = jnp.zeros((2 * blk, HEAD_LANES), F32)
            dv = jnp.zeros((2 * blk, HEAD_LANES), F32)
            for e in range(2):
                msk = low if e == 0 else jnp.logical_not(low)
                qm = jnp.where(msk, q2, 0.0).astype(BF16)
                sc, valid = _attn_scores(qm, kcat, ALIBI_SLOPES[2 * p + e], dil, first)
                lse_e = jnp.sum(jnp.where(lane == 64 * e, lse2, 0.0), axis=-1, keepdims=True)
                delta = jnp.sum(jnp.where(msk, prod, 0.0), axis=-1, keepdims=True)
                pr = jnp.where(valid, jnp.exp(sc - lse_e), 0.0)
                dom = jnp.where(msk, do2, 0.0).astype(BF16)
                ds = (pr * (_nt(dom, vcat) - delta)).astype(BF16)
                msk_keys = low_keys if e == 0 else jnp.logical_not(low_keys)
                dq = dq + _nn(ds, jnp.where(msk_keys, kcat, 0.0).astype(BF16))
                dk = dk + _tn(ds, qm)
                dv = dv + _tn(pr.astype(BF16), dom)
            dq_ref[:, cols] = dq
            dkp_ref[:, cols] = dk[0:blk]
            dkc_ref[:, cols] = dk[blk:2 * blk]
            dvp_ref[:, cols] = dv[0:blk]
            dvc_ref[:, cols] = dv[blk:2 * blk]

    cur = pl.BlockSpec((blk, 512), lambda r, n: (n, r))
    prev = pl.BlockSpec((blk, 512), lambda r, n: (jnp.maximum(n - 1, 0), r))
    dmix = pl.BlockSpec((blk, 512), lambda r, n: (n, 2 * r + 1))
    outs = pl.pallas_call(
        body, name=name, grid=(dil, nb), in_specs=[cur, prev, cur, prev, cur, cur, cur, dmix], out_specs=[cur] * 5,
        out_shape=[jax.ShapeDtypeStruct((l, dil * 512), F32)] * 5, compiler_params=_params("parallel", "parallel"),
    )(view(qa), view(ka), view(ka), view(va), view(va), view(y_att), view(lse), view(dmixed))
    return [t.reshape(s, 512) for t in outs]


def _attn_post(parts, proj, qg, kg, *, name):
    s = proj.shape[0]
    tm = ATTN_BLOCK
    nblk = s // tm

    def body(*refs):
        ins, (q_ref, k_ref, qg_ref, kg_ref, dq_out, dk_out, dv_out, sums_ref) = refs[:15], refs[15:]
        i = pl.program_id(0)

        @pl.when(i == 0)
        def _():
            sums_ref[...] = jnp.zeros_like(sums_ref)

        dq = jnp.zeros((tm, 512), F32)
        dk = jnp.zeros((tm, 512), F32)
        dv = jnp.zeros((tm, 512), F32)
        for g, dil in enumerate(DILATIONS):
            dq_r, dkc_r, dkp_r, dvc_r, dvp_r = ins[5 * g:5 * g + 5]
            inside = (i + dil < nblk).astype(F32)
            dq = dq + dq_r[...]
            dk = dk + dkc_r[...] + inside * dkp_r[...]
            dv = dv + dvc_r[...] + inside * dvp_r[...]
        dv_out[...] = dv.astype(BF16)
        hs = _head_sum_matrix()
        for row, (x_ref, g_ref, dy, out, post) in enumerate(((q_ref, qg_ref, dq, dq_out, 0.125), (k_ref, kg_ref, dk, dk_out, 1.0))):
            x = x_ref[...]
            rs = lax.rsqrt(_nn(x * x, hs, precision=HIGHEST) * (1.0 / 64) + EPS)
            xn = x * rs
            dy = dy * post
            sums_ref[row] += _fold8(dy * xn)
            dn = dy * g_ref[...]
            out[...] = (rs * (dn - xn * (_nn(dn * xn, hs, precision=HIGHEST) * (1.0 / 64)))).astype(BF16)

        @pl.when(i == nblk - 1)
        def _():
            _spread_total(sums_ref)

    here = pl.BlockSpec((tm, 512), lambda i: (i, 0))
    specs = []
    for dil in DILATIONS:
        later = pl.BlockSpec((tm, 512), lambda i, dil=dil: (jnp.minimum(i + dil, nblk - 1), 0))
        specs += [here, here, later, here, later]
    col = lambda at: pl.BlockSpec((tm, 512), lambda i: (i, at // 512))
    vec = pl.BlockSpec((1, 512), lambda i: (0, 0))
    return pl.pallas_call(
        body, name=name, grid=(nblk,), in_specs=specs + [col(C_AQ), col(C_AK), vec, vec],
        out_specs=[here, here, here, pl.BlockSpec((2, 8, 512), lambda i: (0, 0, 0))],
        out_shape=[jax.ShapeDtypeStruct((s, 512), BF16)] * 3 + [jax.ShapeDtypeStruct((2, 8, 512), F32)],
        compiler_params=_params("arbitrary"))(*[t for part in parts for t in part], proj, proj, qg, kg)


FFN_TM, FFN_TN = 256, 1408
HALO = 8


def _conv3(u_ref, halo_ref, w_ref, b_ref, first):
    u = u_ref[...]
    ext = jnp.concatenate([jnp.where(first, 0.0, halo_ref[...]), u], axis=0)
    u1 = pltpu.roll(ext, 1, 0)[HALO:]
    u2 = pltpu.roll(ext, 2, 0)[HALO:]
    return b_ref[...] + w_ref[0:1, :] * u2 + w_ref[1:2, :] * u1 + w_ref[2:3, :] * u, u1, u2


def _ffn_specs(tm, tn):
    nj = D_FF // tn
    blk = lambda half: pl.BlockSpec((tm, tn), lambda j, i: (i, j + half * nj))
    halo = lambda half: pl.BlockSpec((HALO, tn), lambda j, i: (jnp.maximum(i * (tm // HALO) - 1, 0), j + half * nj))
    wspec = lambda half: pl.BlockSpec((3, tn), lambda j, i: (0, j + half * nj))
    bspec = lambda half: pl.BlockSpec((1, tn), lambda j, i: (0, j + half * nj))
    return [blk(0), halo(0), blk(1), halo(1), wspec(0), wspec(1), bspec(0), bspec(1)]


def _conv_swiglu_fwd(u, conv_w, conv_b, *, name):
    s = u.shape[0]
    tm, tn = FFN_TM, FFN_TN

    def body(ug_ref, hg_ref, uv_ref, hv_ref, wg_ref, wv_ref, bg_ref, bv_ref, act_ref):
        first = pl.program_id(1) == 0
        cg, _, _ = _conv3(ug_ref, hg_ref, wg_ref, bg_ref, first)
        cv, _, _ = _conv3(uv_ref, hv_ref, wv_ref, bv_ref, first)
        act_ref[...] = (cg * _sigmoid(cg) * cv).astype(BF16)

    return pl.pallas_call(
        body, name=name, grid=(D_FF // tn, s // tm), in_specs=_ffn_specs(tm, tn),
        out_specs=pl.BlockSpec((tm, tn), lambda j, i: (i, j)), out_shape=jax.ShapeDtypeStruct((s, D_FF), BF16),
        compiler_params=_params("parallel", "parallel"))(u, u, u, u, conv_w, conv_w, conv_b, conv_b)


def _conv_swiglu_bwd_pre(u, conv_w, conv_b, dact, *, name):
    s = u.shape[0]
    tm, tn = FFN_TM, FFN_TN

    def body(ug_ref, hg_ref, uv_ref, hv_ref, wg_ref, wv_ref, bg_ref, bv_ref, da_ref, duc_ref, sums_ref):
        i = pl.program_id(1)

        @pl.when(i == 0)
        def _():
            sums_ref[...] = jnp.zeros_like(sums_ref)

        cg, g1, g2 = _conv3(ug_ref, hg_ref, wg_ref, bg_ref, i == 0)
        cv, v1, v2 = _conv3(uv_ref, hv_ref, wv_ref, bv_ref, i == 0)
        da = da_ref[...]
        sg = _sigmoid(cg)
        dg = da * cv * (sg * (1.0 + cg * (1.0 - sg)))
        dv = da * (cg * sg)
        duc_ref[0] = dg
        duc_ref[1] = dv
        for half, (d, taps) in enumerate(((dg, (g2, g1, ug_ref[...])), (dv, (v2, v1, uv_ref[...])))):
            for t, tap in enumerate(taps):
                sums_ref[half, t] += _fold8(d * tap)
            sums_ref[half, 3] += _fold8(d)

        @pl.when(i == s // tm - 1)
        def _():
            _spread_total(sums_ref)

    return pl.pallas_call(
        body, name=name, grid=(D_FF // tn, s // tm),
        in_specs=_ffn_specs(tm, tn) + [pl.BlockSpec((tm, tn), lambda j, i: (i, j))],
        out_specs=[pl.BlockSpec((2, tm, tn), lambda j, i: (0, i, j)), pl.BlockSpec((2, 4, 8, tn), lambda j, i: (0, 0, 0, j))],
        out_shape=[jax.ShapeDtypeStruct((2, s, D_FF), F32), jax.ShapeDtypeStruct((2, 4, 8, D_FF), F32)],
        compiler_params=_params("parallel", "arbitrary"))(u, u, u, u, conv_w, conv_w, conv_b, conv_b, dact)


def _conv_bwd(duc, conv_w, *, name):
    _, s, _ = duc.shape
    tm, tn = FFN_TM, FFN_TN
    nj, ni = D_FF // tn, s // tm

    def body(d_ref, halo_ref, w_ref, du_ref):
        last = pl.program_id(2) == ni - 1
        d = d_ref[0]
        ext = jnp.concatenate([d, jnp.where(last, 0.0, halo_ref[0])], axis=0)
        n = tm + HALO
        d1 = pltpu.roll(ext, n - 1, 0)[:tm]
        d2 = pltpu.roll(ext, n - 2, 0)[:tm]
        du_ref[...] = (w_ref[2:3, :] * d + w_ref[1:2, :] * d1 + w_ref[0:1, :] * d2).astype(BF16)

    return pl.pallas_call(
        body, name=name, grid=(2, nj, ni),
        in_specs=[pl.BlockSpec((1, tm, tn), lambda g, j, i: (g, i, j)),
                  pl.BlockSpec((1, HALO, tn), lambda g, j, i: (g, jnp.minimum((i + 1) * (tm // HALO), s // HALO - 1), j)),
                  pl.BlockSpec((3, tn), lambda g, j, i: (0, g * nj + j))],
        out_specs=pl.BlockSpec((tm, tn), lambda g, j, i: (i, g * nj + j)),
        out_shape=jax.ShapeDtypeStruct((s, 2 * D_FF), BF16),
        compiler_params=_params("parallel", "parallel", "parallel"))(duc, duc, conv_w)


def _loss_head(x1, ffn, gate, target, *, name):
    s, d = x1.shape
    tm = ROW_TILE

    def body(x_ref, f_ref, g_ref, t_ref, dy_ref, df_ref, sums_ref):
        i = pl.program_id(0)

        @pl.when(i == 0)
        def _():
            sums_ref[...] = jnp.zeros_like(sums_ref)

        f = f_ref[...]
        err = x_ref[...] + g_ref[...] * f - t_ref[...]
        dy = err * (1.0 / d)
        dy_ref[...] = dy
        df_ref[...] = (g_ref[...] * dy).astype(BF16)
        sums_ref[0] += _fold8(dy * f)
        sums_ref[1] += _fold8(err * err)

        @pl.when(i == s // tm - 1)
        def _():
            _spread_total(sums_ref)

    row = pl.BlockSpec((tm, d), lambda i: (i, 0))
    return pl.pallas_call(
        body, name=name, grid=(s // tm,), in_specs=[row, row, pl.BlockSpec((1, d), lambda i: (0, 0)), row],
        out_specs=[row, row, pl.BlockSpec((2, 8, d), lambda i: (0, 0, 0))],
        out_shape=[jax.ShapeDtypeStruct((s, d), F32), jax.ShapeDtypeStruct((s, d), BF16), jax.ShapeDtypeStruct((2, 8, d), F32)],
        compiler_params=_params("arbitrary"))(x1, ffn, gate, target)


def _adamw(w, g, m, v, *, name):
    rows, cols = w.shape
    tm = next((t for t in range(ROW_TILE, 7, -8) if rows % t == 0), rows)

    def body(w_ref, g_ref, m_ref, v_ref, d_ref, mo_ref, vo_ref):
        gv = g_ref[...]
        mn = ADAM_B1 * m_ref[...] + (1.0 - ADAM_B1) * gv
        vn = ADAM_B2 * v_ref[...] + (1.0 - ADAM_B2) * (gv * gv)
        m_hat = mn / (1.0 - ADAM_B1 ** ADAM_STEP)
        v_hat = vn / (1.0 - ADAM_B2 ** ADAM_STEP)
        d_ref[...] = -ADAM_LR * (m_hat / (jnp.sqrt(v_hat) + ADAM_EPS) + ADAM_WD * w_ref[...])
        mo_ref[...] = mn
        vo_ref[...] = vn

    blk = pl.BlockSpec((tm, cols), lambda i: (i, 0))
    return pl.pallas_call(
        body, name=name, grid=(rows // tm,), in_specs=[blk] * 4, out_specs=[blk] * 3,
        out_shape=[jax.ShapeDtypeStruct((rows, cols), F32)] * 3, compiler_params=_params("parallel"))(w, g, m, v)


def _colsum(t):
    return t[..., 0, :]


def _in_proj_layout(w_in):
    pad = jnp.zeros((w_in.shape[0], PROJ_W - C_LR - GLA_GATE_RANK), w_in.dtype)
    return jnp.concatenate([w_in[:, :1536], w_in[:, 1552:], w_in[:, 1536:1552], pad], axis=1)


def _in_proj_grad_layout(g):
    return jnp.concatenate([g[:, :1536], g[:, C_LR:C_LR + GLA_GATE_RANK], g[:, 1536:C_LR]], axis=1)


def _gate_layout(gla_w_gate):
    return jnp.pad(gla_w_gate, ((0, HEAD_LANES - GLA_GATE_RANK), (0, 0))).astype(BF16)


def _local_step(x, target, mod, wi, wo, wup, wdown, conv_w, conv_b, wg, bg, gn, qg, kg, n1g, n2g):
    d = D_MODEL
    sh1, sc1, g1, sh2, sc2, g2 = [mod[:, i * d:(i + 1) * d] for i in range(6)]
    qg8, kg8 = jnp.tile(qg, (1, 8)), jnp.tile(kg, (1, 8))

    _, h1 = _norm_mod_fwd(x, None, None, n1g, sc1, sh1, name="norm1_fwd")
    proj = _mm(h1, wi, tm=512, tn=640, tk=d, name="in_proj")
    o_raw, y_gla, states = _gla_fwd(proj, wg, bg, gn, name="gla_fwd")
    qa, ka, va = _attn_prep(proj, qg8, kg8, name="attn_prep")
    branches = [_dil_attn_fwd(qa, ka, va, dil, name=f"attn_fwd_d{dil}") for dil in DILATIONS]
    mixed, y_att, lse = _attn_merge(branches, y_gla, name="attn_merge")
    attn_out = _mm(mixed, wo, tm=512, tn=d, tk=d, name="out_proj")
    x1, h2 = _norm_mod_fwd(x, attn_out, g1, n2g, sc2, sh2, name="norm2_fwd")
    u = _mm(h2, wup, tm=512, tn=1408, tk=d, name="up_proj")
    act = _conv_swiglu_fwd(u, conv_w, conv_b, name="conv_swiglu_fwd")
    ffn = _mm(act, wdown, tm=512, tn=d, tk=1408, name="down_proj")
    dy, dffn, head_sums = _loss_head(x1, ffn, g2, target, name="loss_head")

    dact = _mm(dffn, wdown, tb=True, tm=512, tn=1408, tk=d, name="down_proj_dx")
    g_wdown = _mm(act, dffn, ta=True, tm=1408, tn=d, tk=512, name="down_proj_dw")
    duc, conv_sums = _conv_swiglu_bwd_pre(u, conv_w, conv_b, dact, name="conv_swiglu_bwd")
    du = _conv_bwd(duc, conv_w, name="conv_bwd")
    dh2 = _mm(du, wup, tb=True, tm=512, tn=d, tk=1408, name="up_proj_dx")
    g_wup = _mm(h2, du, ta=True, tm=d, tn=1408, tk=512, name="up_proj_dw")
    dx1, dao, n2_sums = _norm_mod_bwd(x1, dh2, dy, n2g, sc2, attn_out, g1, name="norm2_bwd")

    dmixed = _mm(dao, wo, tb=True, tm=512, tn=d, tk=d, name="out_proj_dx")
    g_wo = _mm(mixed, dao, ta=True, tm=d, tn=d, tk=512, name="out_proj_dw")
    dgq, dgk, dgv, dgr, dlr, g_wg, gla_sums = _gla_bwd(proj, wg, bg, gn, o_raw, states, dmixed, name="gla_bwd")
    parts = [_dil_attn_bwd(qa, ka, va, y_att, lse, dmixed, dil, name=f"attn_bwd_d{dil}") for dil in DILATIONS]
    daq, dak, dav, qk_sums = _attn_post(parts, proj, qg8, kg8, name="attn_post")
    dproj = jnp.concatenate([dgq, dgk, dgv, dgr, daq, dak, dav, dlr], axis=1)
    dh1 = _mm(dproj, wi, tb=True, tm=512, tn=d, tk=640, name="in_proj_dx")
    g_wi = _mm(h1, dproj, ta=True, tm=d, tn=640, tk=512, name="in_proj_dw")
    grad_x, _, n1_sums = _norm_mod_bwd(x, dh1, dx1, n1g, sc1, None, None, name="norm1_bwd")

    n1, n2, hs, cs = _colsum(n1_sums), _colsum(n2_sums), _colsum(head_sums), _colsum(conv_sums)
    gs, qs = _colsum(gla_sums), _colsum(qk_sums)
    dmod = jnp.concatenate([n1[1], n1[0] * n1g[0], n2[2], n2[1], n2[0] * n2g[0], hs[0]])
    small = dict(
        dmod=dmod,
        norm1_g=n1[0] * (1.0 + sc1[0]), norm2_g=n2[0] * (1.0 + sc2[0]),
        gla_w_gate=g_wg[:GLA_GATE_RANK], gla_b_gate=gs[0, :256], gla_norm_g=gs[1].reshape(4, 128).sum(axis=0),
        q_norm_g=qs[0].reshape(8, 64).sum(axis=0), k_norm_g=qs[1].reshape(8, 64).sum(axis=0),
        conv_w=jnp.concatenate([cs[0, :3], cs[1, :3]], axis=1), conv_b=jnp.concatenate([cs[0, 3], cs[1, 3]]),
    )
    return head_sums[1], grad_x, (g_wi, g_wo, g_wup, g_wdown), small


N_DEV, N_CHIP = 8, 4
ANY = pl.BlockSpec(memory_space=pl.ANY)
VMEM_SPEC = pl.BlockSpec(memory_space=pltpu.VMEM)


def _place():
    x, y, c = lax.axis_index("x"), lax.axis_index("y"), lax.axis_index("c")
    other_chips = [(1 - x, y), (x, 1 - y), (1 - x, 1 - y)]
    return x, y, c, (x, y, 1 - c), other_chips


def _all_gather_small(v, *, name):
    m, n = v.shape

    def body(v_ref, out_ref, send_sems, recv_sems, local_sem):
        x, y, c, sibling, chips = _place()
        me = (x, y, c)

        def rows(px, py, pc):
            return out_ref.at[pl.ds((4 * px + 2 * py + pc) * m, m), :]

        def copy(k, block, to, src=None):
            return pltpu.make_async_remote_copy(
                src_ref=rows(*block) if src is None else src, dst_ref=rows(*block), send_sem=send_sems.at[k],
                recv_sem=recv_sems.at[k], device_id=to, device_id_type=MESH)

        mine = pltpu.make_async_copy(v_ref, rows(*me), local_sem)
        mine.start()
        first = [copy(0, me, sibling, src=v_ref)]
        first += [copy(1 + j, me, (*chip, c), src=v_ref) for j, chip in enumerate(chips)]
        for cp in first:
            cp.start()
        passed = [copy(4 + j, (*chip, c), sibling) for j, chip in enumerate(chips)]
        for j, chip in enumerate(chips):
            copy(1 + j, (*chip, c), me).wait_recv()
            passed[j].start()
        copy(0, sibling, me).wait_recv()
        for j, chip in enumerate(chips):
            copy(4 + j, (*chip, 1 - c), me).wait_recv()
        for cp in first + passed:
            cp.wait_send()
        mine.wait()

    return pl.pallas_call(
        body, name=name, out_shape=jax.ShapeDtypeStruct((N_DEV * m, n), v.dtype), in_specs=[VMEM_SPEC], out_specs=VMEM_SPEC,
        scratch_shapes=[pltpu.SemaphoreType.DMA((7,)), pltpu.SemaphoreType.DMA((7,)), pltpu.SemaphoreType.DMA],
    )(v)


def _gather_weight_shards(flat, *, name):
    r, n = flat.shape
    half = r // 2

    def body(src_ref, out_ref, send_sems, recv_sems, local_sem):
        x, y, c, sibling, chips = _place()

        def half_of(chip, core):
            return out_ref.at[2 * chip[0] + chip[1], pl.ds(core * half, half), :]

        def copy(k, dst, to, src):
            return pltpu.make_async_remote_copy(src_ref=src, dst_ref=dst, send_sem=send_sems.at[k], recv_sem=recv_sems.at[k],
                                                device_id=to, device_id_type=MESH)

        mine = pltpu.make_async_copy(src_ref, out_ref.at[2 * x + y], local_sem)
        mine.start()
        my_half = src_ref.at[pl.ds(c * half, half), :]
        first = [copy(k, half_of((x, y), c), (*chip, c), my_half) for k, chip in enumerate(chips)]
        for cp in first:
            cp.start()
        passed = []
        for k, chip in enumerate(chips):
            copy(k, half_of(chip, c), (*chip, c), my_half).wait_recv()
            passed.append(copy(3 + k, half_of(chip, c), sibling, half_of(chip, c)))
            passed[k].start()
        for k, chip in enumerate(chips):
            copy(3 + k, half_of(chip, 1 - c), sibling, my_half).wait_recv()
        for cp in first + passed:
            cp.wait_send()
        mine.wait()

    return pl.pallas_call(
        body, name=name, out_shape=jax.ShapeDtypeStruct((N_CHIP, r, n), flat.dtype), in_specs=[ANY], out_specs=ANY,
        scratch_shapes=[pltpu.SemaphoreType.DMA((6,)), pltpu.SemaphoreType.DMA((6,)), pltpu.SemaphoreType.DMA],
    )(flat)


def _pair_exchange(v, *, name):
    def body(src_ref, out_ref, send_sem, recv_sem):
        _, _, _, sibling, _ = _place()
        cp = pltpu.make_async_remote_copy(src_ref=src_ref, dst_ref=out_ref, send_sem=send_sem, recv_sem=recv_sem,
                                          device_id=sibling, device_id_type=MESH)
        cp.start()
        cp.wait()

    return pl.pallas_call(
        body, name=name, out_shape=jax.ShapeDtypeStruct(v.shape, v.dtype), in_specs=[ANY], out_specs=ANY,
        scratch_shapes=[pltpu.SemaphoreType.DMA, pltpu.SemaphoreType.DMA])(v)


def _chip_scatter(p, *, name):
    _, rows, n = p.shape

    def body(p_ref, out_ref, send_sems, recv_sems):
        _, _, c, _, chips = _place()
        cps = [pltpu.make_async_remote_copy(src_ref=p_ref.at[2 * chip[0] + chip[1]], dst_ref=out_ref.at[k],
                                            send_sem=send_sems.at[k], recv_sem=recv_sems.at[k], device_id=(*chip, c),
                                            device_id_type=MESH) for k, chip in enumerate(chips)]
        for cp in cps:
            cp.start()
        for cp in cps:
            cp.wait()

    return pl.pallas_call(
        body, name=name, out_shape=jax.ShapeDtypeStruct((3, rows, n), p.dtype), in_specs=[ANY], out_specs=ANY,
        scratch_shapes=[pltpu.SemaphoreType.DMA((3,)), pltpu.SemaphoreType.DMA((3,))])(p)


def _add(terms, *, name):
    rows, n = terms[0].shape
    tm = next(t for t in (2512, 2048, 1024, 512, 256, 8) if rows % t == 0)

    def body(*refs):
        acc = refs[0][...]
        for r in refs[1:-1]:
            acc = acc + r[...]
        refs[-1][...] = acc

    blk = pl.BlockSpec((tm, n), lambda i: (i, 0))
    return pl.pallas_call(body, name=name, grid=(rows // tm,), in_specs=[blk] * len(terms), out_specs=blk,
                          out_shape=jax.ShapeDtypeStruct((rows, n), F32), compiler_params=_params("parallel"))(*terms)


def _sum_devices(gathered, *, name):
    _, m, n = gathered.shape

    def body(g_ref, tot_ref, loss_ref):
        tot = g_ref[0]
        for dev in range(1, N_DEV):
            tot = tot + g_ref[dev]
        tot_ref[...] = tot
        loss_ref[...] = jnp.full((8, n), (0.5 / D_MODEL) * jnp.sum(tot[0:8]), F32)

    return pl.pallas_call(body, name=name, in_specs=[VMEM_SPEC], out_specs=[VMEM_SPEC, VMEM_SPEC],
                          out_shape=[jax.ShapeDtypeStruct((m, n), F32), jax.ShapeDtypeStruct((8, n), F32)])(gathered)


def _ada_mod(cond_all, w_ada_shard, *, name):
    tn = 512

    def body(a_ref, b_ref, o_ref):
        o_ref[...] = _nn(a_ref[...], b_ref[...], precision=HIGHEST)

    return pl.pallas_call(
        body, name=name, grid=(w_ada_shard.shape[1] // tn,),
        in_specs=[pl.BlockSpec(cond_all.shape, lambda j: (0, 0)), pl.BlockSpec((D_MODEL, tn), lambda j: (0, j))],
        out_specs=pl.BlockSpec((N_DEV, tn), lambda j: (0, j)),
        out_shape=jax.ShapeDtypeStruct((N_DEV, w_ada_shard.shape[1]), F32), compiler_params=_params("parallel"))(cond_all, w_ada_shard)


def _ada_grad(cond_all, dmod_cols, *, name):
    tm = 256

    def body(a_ref, b_ref, o_ref):
        o_ref[...] = lax.dot_general(a_ref[...], b_ref[...], (((0,), (0,)), ((), ())), precision=HIGHEST,
                                     preferred_element_type=F32)

    return pl.pallas_call(
        body, name=name, grid=(D_MODEL // tm,),
        in_specs=[pl.BlockSpec((N_DEV, tm), lambda i: (0, i)), pl.BlockSpec(dmod_cols.shape, lambda i: (0, 0))],
        out_specs=pl.BlockSpec((tm, dmod_cols.shape[1]), lambda i: (i, 0)),
        out_shape=jax.ShapeDtypeStruct((D_MODEL, dmod_cols.shape[1]), F32), compiler_params=_params("parallel"))(cond_all, dmod_cols)


def _silu_rows(c8, *, name):
    def body(c_ref, o_ref):
        cv = c_ref[...]
        o_ref[...] = cv * _sigmoid(cv)

    return pl.pallas_call(body, name=name, in_specs=[VMEM_SPEC], out_specs=VMEM_SPEC,
                          out_shape=jax.ShapeDtypeStruct(c8.shape, F32))(c8)


def _rows128(t, rows=None):
    flat = t.reshape(-1, 128)
    return flat if rows is None else jnp.pad(flat, ((0, rows - flat.shape[0]), (0, 0)))


def _col_shards(full, n):
    r = full.shape[0]
    return full.reshape(r, N_CHIP, n).transpose(1, 0, 2).reshape(N_CHIP, r * n // 128, 128)


def _from_col_shards(shards, r, n):
    return shards.reshape(N_CHIP, r, n).transpose(1, 0, 2).reshape(r, N_CHIP * n)


SHARD_SHAPES = ((1024, 772), (256, 1024), (1024, 1408), (704, 1024))
SHARD_ROWS = tuple(a * b // 128 for a, b in SHARD_SHAPES)
SHARD_AT = tuple(sum(SHARD_ROWS[:i]) for i in range(5))


def kernel(x, c, w_ada, b_ada, norm1_g, w_in, gla_w_gate, gla_b_gate, gla_norm_g, q_norm_g, k_norm_g, w_out, norm2_g, w_up, conv_w, conv_b, w_down, loss_target, m_w_ada, m_b_ada, m_norm1_g, m_w_in, m_gla_w_gate, m_gla_b_gate, m_gla_norm_g, m_q_norm_g, m_k_norm_g, m_w_out, m_norm2_g, m_w_up, m_conv_w, m_conv_b, m_w_down, v_w_ada, v_b_ada, v_norm1_g, v_w_in, v_gla_w_gate, v_gla_b_gate, v_gla_norm_g, v_q_norm_g, v_k_norm_g, v_w_out, v_norm2_g, v_w_up, v_conv_w, v_conv_b, v_w_down):
    d = D_MODEL
    ax, ay, ac = lax.axis_index("x"), lax.axis_index("y"), lax.axis_index("c")
    chip, dev = 2 * ax + ay, 4 * ax + 2 * ay + ac
    r_all, half = SHARD_AT[4], SHARD_AT[4] // 2

    cond = _silu_rows(jnp.broadcast_to(c, (8, d)), name="cond_silu")[0:1]
    small_in = jnp.concatenate([_rows128(cond), _rows128(conv_w[0]), _rows128(gla_w_gate[0])], axis=0)
    small_in = _rows128(small_in, 56)
    got = _all_gather_small(small_in, name="gather_small").reshape(N_DEV, 56, 128)
    cond_all = got[:, 0:8].reshape(N_DEV, d)
    conv_w_full = _from_col_shards(got[0::2, 8:41].reshape(N_CHIP, 3 * 1408 // 128, 128), 3, 1408)
    gate_full = _from_col_shards(got[0::2, 41:49].reshape(N_CHIP, 16 * 64 // 128, 128), GLA_GATE_RANK, 64)
    mod_part = _ada_mod(cond_all, w_ada[0], name="ada_mod")
    mod_got = _all_gather_small(_rows128(mod_part), name="gather_mod").reshape(N_DEV, N_DEV, 1536)
    mod_all = mod_got[0::2].transpose(1, 0, 2).reshape(N_DEV, 6 * d) + b_ada
    mod = lax.dynamic_slice_in_dim(mod_all, dev, 1, axis=0)

    flat = jnp.concatenate([_rows128(w_in[0]), _rows128(w_out[0]), _rows128(w_up[0]), _rows128(w_down[0])], axis=0)
    shards = _gather_weight_shards(flat.astype(BF16), name="gather_weights")
    part = lambda i: shards[:, SHARD_AT[i]:SHARD_AT[i + 1]]
    w_in_full = _from_col_shards(part(0), d, 772)
    w_out_full = part(1).reshape(d, d)
    w_up_full = _from_col_shards(part(2), d, 1408)
    w_down_full = part(3).reshape(D_FF, d)

    err2, grad_x, (g_wi, g_wo, g_wup, g_wdown), small = _local_step(
        x[0], loss_target[0], mod, _in_proj_layout(w_in_full), w_out_full, w_up_full, w_down_full, conv_w_full, conv_b,
        _gate_layout(gate_full), gla_b_gate, gla_norm_g, q_norm_g, k_norm_g, norm1_g, norm2_g)

    pieces = [err2[0], small["dmod"], small["norm1_g"], small["norm2_g"], small["gla_w_gate"].reshape(-1), small["gla_b_gate"],
              small["gla_norm_g"], small["q_norm_g"], small["k_norm_g"], small["conv_w"].reshape(-1), small["conv_b"]]
    sizes = [p.shape[0] for p in pieces]
    at = [sum(sizes[:i]) for i in range(len(sizes) + 1)]
    vec = _rows128(jnp.concatenate(pieces), 288)
    got = _all_gather_small(vec, name="gather_grads").reshape(N_DEV, 288, 128)
    total, loss8 = _sum_devices(got, name="sum_devices")
    total = total.reshape(-1)
    seg = lambda i: total[at[i]:at[i + 1]]
    dmod_all = got.reshape(N_DEV, -1)[:, at[1]:at[2]]
    g_small = dict(
        b_ada=seg(1)[None], norm1_g=seg(2)[None], norm2_g=seg(3)[None],
        gla_w_gate=lax.dynamic_slice_in_dim(seg(4).reshape(GLA_GATE_RANK, 256), chip * 64, 64, axis=1),
        gla_b_gate=seg(5)[None], gla_norm_g=seg(6)[None], q_norm_g=seg(7)[None], k_norm_g=seg(8)[None],
        conv_w=lax.dynamic_slice_in_dim(seg(9).reshape(3, 2 * D_FF), chip * 1408, 1408, axis=1), conv_b=seg(10)[None])
    dmod_cols = lax.dynamic_slice_in_dim(dmod_all.reshape(N_DEV, 6 * d), chip * 1536, 1536, axis=1)
    g_w_ada = _ada_grad(cond_all, dmod_cols, name="ada_grad")

    g_flat = jnp.concatenate([_col_shards(_in_proj_grad_layout(g_wi), 772), g_wo.reshape(N_CHIP, SHARD_ROWS[1], 128),
                              _col_shards(g_wup, 1408), g_wdown.reshape(N_CHIP, SHARD_ROWS[3], 128)], axis=1)
    keep = lax.dynamic_slice_in_dim(g_flat, ac * half, half, axis=1)
    give = lax.dynamic_slice_in_dim(g_flat, (1 - ac) * half, half, axis=1)
    pair = _add([keep.reshape(-1, 128), _pair_exchange(give, name="reduce_pair").reshape(-1, 128)], name="reduce_pair_add")
    pair = pair.reshape(N_CHIP, half, 128)
    theirs = _chip_scatter(pair, name="reduce_chips")
    mine = lax.dynamic_index_in_dim(pair, chip, axis=0, keepdims=False)
    summed = _add([mine, theirs[0], theirs[1], theirs[2]], name="reduce_chips_add")
    other = _pair_exchange(summed, name="share_pair")
    lo = jnp.where(ac == 0, summed, other)
    hi = jnp.where(ac == 0, other, summed)
    red = jnp.concatenate([lo, hi], axis=0)
    g_big = [red[SHARD_AT[i]:SHARD_AT[i + 1]].reshape(SHARD_SHAPES[i]) for i in range(4)]

    grads = dict(w_ada=g_w_ada, w_in=g_big[0], w_out=g_big[1], w_up=g_big[2], w_down=g_big[3], **g_small)
    names = ["w_ada", "b_ada", "norm1_g", "w_in", "gla_w_gate", "gla_b_gate", "gla_norm_g", "q_norm_g", "k_norm_g", "w_out",
             "norm2_g", "w_up", "conv_w", "conv_b", "w_down"]
    ws = dict(w_ada=w_ada, b_ada=b_ada, norm1_g=norm1_g, w_in=w_in, gla_w_gate=gla_w_gate, gla_b_gate=gla_b_gate,
              gla_norm_g=gla_norm_g, q_norm_g=q_norm_g, k_norm_g=k_norm_g, w_out=w_out, norm2_g=norm2_g, w_up=w_up,
              conv_w=conv_w, conv_b=conv_b, w_down=w_down)
    ms = dict(w_ada=m_w_ada, b_ada=m_b_ada, norm1_g=m_norm1_g, w_in=m_w_in, gla_w_gate=m_gla_w_gate, gla_b_gate=m_gla_b_gate,
              gla_norm_g=m_gla_norm_g, q_norm_g=m_q_norm_g, k_norm_g=m_k_norm_g, w_out=m_w_out, norm2_g=m_norm2_g, w_up=m_w_up,
              conv_w=m_conv_w, conv_b=m_conv_b, w_down=m_w_down)
    vs = dict(w_ada=v_w_ada, b_ada=v_b_ada, norm1_g=v_norm1_g, w_in=v_w_in, gla_w_gate=v_gla_w_gate, gla_b_gate=v_gla_b_gate,
              gla_norm_g=v_gla_norm_g, q_norm_g=v_q_norm_g, k_norm_g=v_k_norm_g, w_out=v_w_out, norm2_g=v_norm2_g, w_up=v_w_up,
              conv_w=v_conv_w, conv_b=v_conv_b, w_down=v_w_down)
    g_out, d_out, m_out, v_out = [], [], [], []
    for nm in names:
        w2 = ws[nm].reshape(ws[nm].shape[-2:])
        g2 = grads[nm].reshape(w2.shape)
        dl, mn, vn = _adamw(w2, g2, ms[nm].reshape(w2.shape), vs[nm].reshape(w2.shape), name=f"adamw_{nm}")
        shape = ws[nm].shape
        g_out.append(g2.reshape(shape))
        d_out.append(dl.reshape(shape))
        m_out.append(mn.reshape(shape))
        v_out.append(vn.reshape(shape))
    return (loss8[0, 0], grad_x[None], *g_out, *d_out, *m_out, *v_out)
```

```python
import functools

import jax
import jax.numpy as jnp
from jax import lax
from jax.experimental import pallas as pl
from jax.experimental.pallas import tpu as pltpu

F32, BF16 = jnp.float32, jnp.bfloat16
HIGHEST = lax.Precision.HIGHEST
MESH = pl.DeviceIdType.MESH

D_MODEL = 1024
GLA_CHUNK = 64
GLA_GATE_TAU = 16.0
GLA_GATE_RANK = 16
HEAD_LANES = 128
ATTN_BLOCK = 128
DILATIONS = (1, 4, 16)
ALIBI_SLOPES = tuple(2.0 ** (-(h + 1)) for h in range(8))
D_FF = 2816
EPS = 1e-6
NEG = -1e30
C_GQ, C_GK, C_GV, C_GR, C_AQ, C_AK, C_AV, C_LR, PROJ_W = 0, 256, 512, 1024, 1536, 2048, 2560, 3072, 3200
ADAM_LR, ADAM_B1, ADAM_B2, ADAM_EPS, ADAM_WD, ADAM_STEP = 0.001, 0.9, 0.999, 1e-08, 0.01, 10
VMEM_LIMIT_BYTES = 56 * 1024 * 1024
ROW_TILE = 256


def _params(*sem):
    return pltpu.CompilerParams(dimension_semantics=sem or None, vmem_limit_bytes=VMEM_LIMIT_BYTES)


def _nt(a, b):
    return lax.dot_general(a, b, (((1,), (1,)), ((), ())), preferred_element_type=F32)


def _tn(a, b):
    return lax.dot_general(a, b, (((0,), (0,)), ((), ())), preferred_element_type=F32)


def _nn(a, b, precision=None):
    return jnp.dot(a, b, preferred_element_type=F32, precision=precision)


def _fold8(v):
    return v.reshape(v.shape[0] // 8, 8, v.shape[1]).sum(axis=0)


def _spread_total(ref):
    t = ref[...]
    ref[...] = jnp.broadcast_to(jnp.sum(t, axis=-2, keepdims=True), t.shape)


def _sigmoid(x):
    return 1.0 / (1.0 + jnp.exp(-x))


def _mm(a, b, *, ta=False, tb=False, out_dtype=F32, tm, tn, tk, name):
    (k_a, m) = a.shape if ta else a.shape[::-1]
    (k_b, n) = b.shape[::-1] if tb else b.shape
    assert k_a == k_b and m % tm == 0 and n % tn == 0 and k_a % tk == 0, (name, a.shape, b.shape)
    nk = k_a // tk
    dims = (((0 if ta else 1,), (1 if tb else 0,)), ((), ()))

    def body(a_ref, b_ref, o_ref, acc_ref):
        k = pl.program_id(2)
        part = lax.dot_general(a_ref[...].astype(BF16), b_ref[...].astype(BF16), dims, preferred_element_type=F32)
        if nk == 1:
            o_ref[...] = part.astype(out_dtype)
        else:
            @pl.when(k == 0)
            def _():
                acc_ref[...] = part

            @pl.when(k > 0)
            def _():
                acc_ref[...] += part

            @pl.when(k == nk - 1)
            def _():
                o_ref[...] = acc_ref[...].astype(out_dtype)

    a_spec = pl.BlockSpec((tk, tm), lambda i, j, k: (k, i)) if ta else pl.BlockSpec((tm, tk), lambda i, j, k: (i, k))
    b_spec = pl.BlockSpec((tn, tk), lambda i, j, k: (j, k)) if tb else pl.BlockSpec((tk, tn), lambda i, j, k: (k, j))
    return pl.pallas_call(
        body, name=name, grid=(m // tm, n // tn, nk), in_specs=[a_spec, b_spec],
        out_specs=pl.BlockSpec((tm, tn), lambda i, j, k: (i, j)), out_shape=jax.ShapeDtypeStruct((m, n), out_dtype),
        scratch_shapes=[pltpu.VMEM((tm, tn), F32)], compiler_params=_params("parallel", "parallel", "arbitrary"),
    )(a, b)


def _norm_mod_fwd(x, branch, gate, gain, scale, shift, *, name):
    s, d = x.shape
    tm = ROW_TILE
    has_branch = branch is not None

    def body(*refs):
        if has_branch:
            x_ref, br_ref, gate_ref, gain_ref, sc_ref, sh_ref, x1_ref, h_ref = refs
            xv = x_ref[...] + gate_ref[...] * br_ref[...]
            x1_ref[...] = xv
        else:
            x_ref, gain_ref, sc_ref, sh_ref, h_ref = refs
            xv = x_ref[...]
        r = lax.rsqrt(jnp.mean(xv * xv, axis=-1, keepdims=True) + EPS)
        h_ref[...] = ((xv * r) * gain_ref[...] * (1.0 + sc_ref[...]) + sh_ref[...]).astype(BF16)

    row = pl.BlockSpec((tm, d), lambda i: (i, 0))
    vec = pl.BlockSpec((1, d), lambda i: (0, 0))
    if has_branch:
        return pl.pallas_call(
            body, name=name, grid=(s // tm,), in_specs=[row, row, vec, vec, vec, vec], out_specs=[row, row],
            out_shape=[jax.ShapeDtypeStruct((s, d), F32), jax.ShapeDtypeStruct((s, d), BF16)],
            compiler_params=_params("parallel"))(x, branch, gate, gain, scale, shift)
    h = pl.pallas_call(
        body, name=name, grid=(s // tm,), in_specs=[row, vec, vec, vec], out_specs=row,
        out_shape=jax.ShapeDtypeStruct((s, d), BF16), compiler_params=_params("parallel"))(x, gain, scale, shift)
    return x, h


def _norm_mod_bwd(x, dh, dres, gain, scale, branch, gate, *, name):
    s, d = x.shape
    tm = ROW_TILE
    has_branch = branch is not None

    def body(*refs):
        if has_branch:
            x_ref, dh_ref, dres_ref, gain_ref, sc_ref, br_ref, gate_ref, dx_ref, dbr_ref, sums_ref = refs
        else:
            x_ref, dh_ref, dres_ref, gain_ref, sc_ref, dx_ref, sums_ref = refs
        i = pl.program_id(0)

        @pl.when(i == 0)
        def _():
            sums_ref[...] = jnp.zeros_like(sums_ref)

        xv, dhv = x_ref[...], dh_ref[...]
        r = lax.rsqrt(jnp.mean(xv * xv, axis=-1, keepdims=True) + EPS)
        xn = xv * r
        dxn = dhv * (gain_ref[...] * (1.0 + sc_ref[...]))
        dx = dres_ref[...] + r * (dxn - xn * jnp.mean(dxn * xn, axis=-1, keepdims=True))
        dx_ref[...] = dx
        sums_ref[0] += _fold8(dhv * xn)
        sums_ref[1] += _fold8(dhv)
        if has_branch:
            dbr_ref[...] = (gate_ref[...] * dx).astype(BF16)
            sums_ref[2] += _fold8(dx * br_ref[...])

        @pl.when(i == s // tm - 1)
        def _():
            _spread_total(sums_ref)

    row = pl.BlockSpec((tm, d), lambda i: (i, 0))
    vec = pl.BlockSpec((1, d), lambda i: (0, 0))
    sums = pl.BlockSpec((3, 8, d), lambda i: (0, 0, 0))
    sums_shape = jax.ShapeDtypeStruct((3, 8, d), F32)
    if has_branch:
        return pl.pallas_call(
            body, name=name, grid=(s // tm,), in_specs=[row, row, row, vec, vec, row, vec], out_specs=[row, row, sums],
            out_shape=[jax.ShapeDtypeStruct((s, d), F32), jax.ShapeDtypeStruct((s, d), BF16), sums_shape],
            compiler_params=_params("arbitrary"))(x, dh, dres, gain, scale, branch, gate)
    dx, sm = pl.pallas_call(
        body, name=name, grid=(s // tm,), in_specs=[row, row, row, vec, vec], out_specs=[row, sums],
        out_shape=[jax.ShapeDtypeStruct((s, d), F32), sums_shape],
        compiler_params=_params("arbitrary"))(x, dh, dres, gain, scale)
    return dx, None, sm


GLA_ROWS = 256


def _gla_chunk_setup(lr_ref, wg_ref, bg_ref, rows):
    c = GLA_CHUNK
    ri = lax.broadcasted_iota(jnp.int32, (c, c), 0)
    ci = lax.broadcasted_iota(jnp.int32, (c, c), 1)
    z = _nn(lr_ref[rows, :].astype(BF16), wg_ref[...]) + bg_ref[...]
    g = (jnp.minimum(z, 0.0) - jnp.log(1.0 + jnp.exp(-jnp.abs(z)))) * (1.0 / GLA_GATE_TAU)
    b = _nn((ci <= ri).astype(F32), g, precision=HIGHEST)
    return z, b, ci <= ri


def _last_row(b):
    ri = lax.broadcasted_iota(jnp.int32, b.shape, 0)
    return jnp.sum(jnp.where(ri == b.shape[0] - 1, b, 0.0), axis=0, keepdims=True)


def _gla_fwd(proj, wg, bg, gn, *, name):
    s = proj.shape[0]
    tb, c = GLA_ROWS, GLA_CHUNK
    cb = tb // c

    def body(q_ref, k_ref, v_ref, r_ref, lr_ref, wg_ref, bg_ref, gn_ref, o_ref, y_ref, st_ref, state):
        i = pl.program_id(0)

        @pl.when(i == 0)
        def _():
            state[...] = jnp.zeros_like(state)

        low = lax.broadcasted_iota(jnp.int32, (c, HEAD_LANES), 1) < 64
        for ch in range(cb):
            rows = pl.ds(ch * c, c)
            _, b, causal = _gla_chunk_setup(lr_ref, wg_ref, bg_ref, rows)
            for p in range(2):
                cols = pl.ds(p * HEAD_LANES, HEAD_LANES)
                bp = b[:, p * HEAD_LANES:(p + 1) * HEAD_LANES]
                b_end = _last_row(bp)
                q = q_ref[rows, cols] * 0.125
                k = k_ref[rows, cols]
                q_in = q * jnp.exp(bp)
                k_out = (k * jnp.exp(-bp)).astype(BF16)
                k_end = k * jnp.exp(b_end - bp)
                st = state[p]
                st_ref[ch, p] = st
                st_b = st.astype(BF16)
                upd = jnp.zeros_like(st)
                for e in range(2):
                    msk = low if e == 0 else jnp.logical_not(low)
                    hc = pl.ds((2 * p + e) * HEAD_LANES, HEAD_LANES)
                    qm = jnp.where(msk, q_in, 0.0).astype(BF16)
                    a = jnp.where(causal, _nt(qm, k_out), 0.0)
                    v = v_ref[rows, hc].astype(BF16)
                    o = _nt(qm, st_b) + _nn(a.astype(BF16), v)
                    upd = upd + _tn(v, jnp.where(msk, k_end, 0.0).astype(BF16))
                    o_ref[rows, hc] = o
                    rr = r_ref[rows, hc]
                    on = o * lax.rsqrt(jnp.mean(o * o, axis=-1, keepdims=True) + EPS)
                    y_ref[rows, hc] = (on * gn_ref[...] * (rr * _sigmoid(rr))).astype(BF16)
                state[p] = st * jnp.exp(b_end) + upd

    def col(width, at):
        return pl.BlockSpec((tb, width), lambda i: (i, at // width))

    full = lambda shape: pl.BlockSpec(shape, lambda i: tuple(0 for _ in shape))
    return pl.pallas_call(
        body, name=name, grid=(s // tb,),
        in_specs=[col(256, C_GQ), col(256, C_GK), col(512, C_GV), col(512, C_GR), col(128, C_LR),
                  full((HEAD_LANES, 256)), full((1, 256)), full((1, HEAD_LANES))],
        out_specs=[pl.BlockSpec((tb, 512), lambda i: (i, 0)), pl.BlockSpec((tb, 512), lambda i: (i, 0)),
                   pl.BlockSpec((cb, 2, HEAD_LANES, HEAD_LANES), lambda i: (i, 0, 0, 0))],
        out_shape=[jax.ShapeDtypeStruct((s, 512), F32), jax.ShapeDtypeStruct((s, 512), BF16),
                   jax.ShapeDtypeStruct((s // c, 2, HEAD_LANES, HEAD_LANES), F32)],
        scratch_shapes=[pltpu.VMEM((2, HEAD_LANES, HEAD_LANES), F32)],
        compiler_params=_params("arbitrary"))(proj, proj, proj, proj, proj, wg, bg, gn)


def _gla_bwd(proj, wg, bg, gn, o_raw, states, dmixed, *, name):
    s = proj.shape[0]
    tb, c = GLA_ROWS, GLA_CHUNK
    cb = tb // c
    nblk, nch = s // tb, s // c

    def body(q_ref, k_ref, v_ref, r_ref, lr_ref, wg_ref, bg_ref, gn_ref, o_ref, st_ref, stn_ref, dy_ref,
             dq_ref, dk_ref, dv_ref, dr_ref, dlr_ref, gwg_ref, sums_ref, dstate):
        i = pl.program_id(0)

        @pl.when(i == 0)
        def _():
            dstate[...] = jnp.zeros_like(dstate)
            gwg_ref[...] = jnp.zeros_like(gwg_ref)
            sums_ref[...] = jnp.zeros_like(sums_ref)

        low = lax.broadcasted_iota(jnp.int32, (c, HEAD_LANES), 1) < 64
        for ch in reversed(range(cb)):
            rows = pl.ds(ch * c, c)
            z, b, causal = _gla_chunk_setup(lr_ref, wg_ref, bg_ref, rows)
            upper = jnp.logical_not(causal) | (lax.broadcasted_iota(jnp.int32, (c, c), 0)
                                               == lax.broadcasted_iota(jnp.int32, (c, c), 1))
            lr_b = lr_ref[rows, :].astype(BF16)
            dlr = jnp.zeros((c, HEAD_LANES), F32)
            for p in range(2):
                cols = pl.ds(p * HEAD_LANES, HEAD_LANES)
                sl = slice(p * HEAD_LANES, (p + 1) * HEAD_LANES)
                bp = b[:, sl]
                b_end = _last_row(bp)
                e_in, e_out, e_end = jnp.exp(bp), jnp.exp(-bp), jnp.exp(b_end - bp)
                q = q_ref[rows, cols] * 0.125
                k = k_ref[rows, cols]
                q_in = q * e_in
                k_out = k * e_out
                k_end = k * e_end
                st0 = st_ref[ch, p]
                st1 = st_ref[ch + 1, p] if ch + 1 < cb else stn_ref[0, p]
                dst = dstate[p]
                st0_b, dst_b = st0.astype(BF16), dst.astype(BF16)
                dq_in = jnp.zeros((c, HEAD_LANES), F32)
                dk_out = jnp.zeros((c, HEAD_LANES), F32)
                dk_end = jnp.zeros((c, HEAD_LANES), F32)
                dst_new = dst * jnp.exp(b_end)
                for e in range(2):
                    msk = low if e == 0 else jnp.logical_not(low)
                    hc = pl.ds((2 * p + e) * HEAD_LANES, HEAD_LANES)
                    o = o_ref[rows, hc]
                    rr = r_ref[rows, hc]
                    dy = dy_ref[rows, hc]
                    sg = _sigmoid(rr)
                    rs = lax.rsqrt(jnp.mean(o * o, axis=-1, keepdims=True) + EPS)
                    on = o * rs
                    t = dy * (rr * sg)
                    sums_ref[1, :, hc] += _fold8(t * on)
                    dn = t * gn_ref[...]
                    do = (rs * (dn - on * jnp.mean(dn * on, axis=-1, keepdims=True))).astype(BF16)
                    dr_ref[rows, hc] = (dy * on * gn_ref[...] * (sg * (1.0 + rr * (1.0 - sg)))).astype(BF16)
                    qm = jnp.where(msk, q_in, 0.0).astype(BF16)
                    km_out = jnp.where(msk, k_out, 0.0).astype(BF16)
                    km_end = jnp.where(msk, k_end, 0.0).astype(BF16)
                    v = v_ref[rows, hc].astype(BF16)
                    a = jnp.where(causal, _nt(qm, km_out), 0.0).astype(BF16)
                    da = jnp.where(causal, _nt(do, v), 0.0).astype(BF16)
                    dv_ref[rows, hc] = (_tn(a, do) + _nt(km_end, dst_b)).astype(BF16)
                    dq_in = dq_in + jnp.where(msk, _nn(do, st0_b) + _nn(da, km_out), 0.0)
                    dk_out = dk_out + _tn(da, qm)
                    dk_end = dk_end + jnp.where(msk, _nn(v, dst_b), 0.0)
                    dst_new = dst_new + _tn(do, qm)
                dq = dq_in * e_in
                dk = dk_out * e_out + dk_end * e_end
                dq_ref[rows, cols] = (dq * 0.125).astype(BF16)
                dk_ref[rows, cols] = dk.astype(BF16)
                w = q * dq - k * dk
                dg = _nn(upper.astype(F32), w, precision=HIGHEST) + jnp.sum(dst * st1, axis=0, keepdims=True)
                zp = z[:, sl]
                dz = dg * (1.0 / GLA_GATE_TAU) * _sigmoid(-zp)
                dz_b = dz.astype(BF16)
                sums_ref[0, :, cols] += _fold8(dz)
                dlr = dlr + _nt(dz_b, wg_ref[:, cols])
                gwg_ref[:, cols] += _tn(lr_b, dz_b)
                dstate[p] = dst_new
            dlr_ref[rows, :] = dlr.astype(BF16)

        @pl.when(i == nblk - 1)
        def _():
            _spread_total(sums_ref)

    rev = lambda i: nblk - 1 - i

    def col(width, at):
        return pl.BlockSpec((tb, width), lambda i: (rev(i), at // width))

    full = lambda shape: pl.BlockSpec(shape, lambda i: tuple(0 for _ in shape))
    out_col = lambda width: pl.BlockSpec((tb, width), lambda i: (rev(i), 0))
    return pl.pallas_call(
        body, name=name, grid=(nblk,),
        in_specs=[col(256, C_GQ), col(256, C_GK), col(512, C_GV), col(512, C_GR), col(128, C_LR),
                  full((HEAD_LANES, 256)), full((1, 256)), full((1, HEAD_LANES)),
                  pl.BlockSpec((tb, 512), lambda i: (rev(i), 0)),
                  pl.BlockSpec((cb, 2, HEAD_LANES, HEAD_LANES), lambda i: (rev(i), 0, 0, 0)),
                  pl.BlockSpec((1, 2, HEAD_LANES, HEAD_LANES), lambda i: (jnp.minimum((rev(i) + 1) * cb, nch - 1), 0, 0, 0)),
                  pl.BlockSpec((tb, 512), lambda i: (rev(i), 0))],
        out_specs=[out_col(256), out_col(256), out_col(512), out_col(512), out_col(128),
                   full((HEAD_LANES, 256)), full((2, 8, 512))],
        out_shape=[jax.ShapeDtypeStruct((s, 256), BF16), jax.ShapeDtypeStruct((s, 256), BF16),
                   jax.ShapeDtypeStruct((s, 512), BF16), jax.ShapeDtypeStruct((s, 512), BF16),
                   jax.ShapeDtypeStruct((s, 128), BF16), jax.ShapeDtypeStruct((HEAD_LANES, 256), F32),
                   jax.ShapeDtypeStruct((2, 8, 512), F32)],
        scratch_shapes=[pltpu.VMEM((2, HEAD_LANES, HEAD_LANES), F32)],
        compiler_params=_params("arbitrary"))(proj, proj, proj, proj, proj, wg, bg, gn, o_raw, states, states, dmixed)


def _head_sum_matrix():
    ri = lax.broadcasted_iota(jnp.int32, (512, 512), 0) // 64
    ci = lax.broadcasted_iota(jnp.int32, (512, 512), 1) // 64
    return (ri == ci).astype(F32)


def _attn_prep(proj, qg, kg, *, name):
    s = proj.shape[0]
    tm = ROW_TILE

    def body(q_ref, k_ref, qg_ref, kg_ref, qa_ref, ka_ref):
        hs = _head_sum_matrix()
        q, k = q_ref[...], k_ref[...]
        qr = lax.rsqrt(_nn(q * q, hs, precision=HIGHEST) * (1.0 / 64) + EPS)
        kr = lax.rsqrt(_nn(k * k, hs, precision=HIGHEST) * (1.0 / 64) + EPS)
        qa_ref[...] = q * qr * qg_ref[...] * 0.125
        ka_ref[...] = k * kr * kg_ref[...]

    col = lambda at: pl.BlockSpec((tm, 512), lambda i: (i, at // 512))
    vec = pl.BlockSpec((1, 512), lambda i: (0, 0))
    out = pl.BlockSpec((tm, 512), lambda i: (i, 0))
    return pl.pallas_call(
        body, name=name, grid=(s // tm,), in_specs=[col(C_AQ), col(C_AK), vec, vec], out_specs=[out] * 2,
        out_shape=[jax.ShapeDtypeStruct((s, 512), F32)] * 2, compiler_params=_params("parallel"))(proj, proj, qg, kg)


def _attn_scores(qm, kcat, slope, dil, first):
    blk = ATTN_BLOCK
    iq = lax.broadcasted_iota(jnp.int32, (blk, 2 * blk), 0)
    ik = lax.broadcasted_iota(jnp.int32, (blk, 2 * blk), 1)
    rel = iq + blk - ik
    valid = (rel >= 0) & (rel <= blk) & (jnp.logical_not(first) | (ik >= blk))
    sc = _nt(qm, kcat) - (slope * dil) * rel.astype(F32)
    return jnp.where(valid, sc, NEG), valid


def _pair_slopes(p):
    if isinstance(p, int):
        return ALIBI_SLOPES[2 * p], ALIBI_SLOPES[2 * p + 1]
    pick = lambda e: jnp.where(p == 0, ALIBI_SLOPES[e], jnp.where(p == 1, ALIBI_SLOPES[2 + e],
                               jnp.where(p == 2, ALIBI_SLOPES[4 + e], ALIBI_SLOPES[6 + e])))
    return pick(0), pick(1)


def _attn_pair_fwd(q2, kcat, vcat, slopes, dil, first):
    low = lax.broadcasted_iota(jnp.int32, (ATTN_BLOCK, HEAD_LANES), 1) < 64
    outs, lses = [], []
    for e in range(2):
        msk = low if e == 0 else jnp.logical_not(low)
        sc, _ = _attn_scores(jnp.where(msk, q2, 0.0).astype(BF16), kcat, slopes[e], dil, first)
        m = jnp.max(sc, axis=-1, keepdims=True)
        pr = jnp.exp(sc - m)
        den = jnp.sum(pr, axis=-1, keepdims=True)
        outs.append(_nn(pr.astype(BF16), vcat) / den)
        lses.append(m + jnp.log(den))
    return jnp.where(low, outs[0], outs[1]), jnp.where(low, lses[0], lses[1])


def _attn_pair_bwd(q2, kcat, vcat, do2, y2, lse2, slopes, dil, first):
    blk = ATTN_BLOCK
    lane = lax.broadcasted_iota(jnp.int32, (blk, HEAD_LANES), 1)
    low = lane < 64
    low_keys = lax.broadcasted_iota(jnp.int32, (2 * blk, HEAD_LANES), 1) < 64
    prod = do2 * y2
    dq = jnp.zeros((blk, HEAD_LANES), F32)
    dk = jnp.zeros((2 * blk, HEAD_LANES), F32)
    dv = jnp.zeros((2 * blk, HEAD_LANES), F32)
    for e in range(2):
        msk = low if e == 0 else jnp.logical_not(low)
        msk_keys = low_keys if e == 0 else jnp.logical_not(low_keys)
        qm = jnp.where(msk, q2, 0.0).astype(BF16)
        sc, valid = _attn_scores(qm, kcat, slopes[e], dil, first)
        lse_e = jnp.sum(jnp.where(lane == 64 * e, lse2, 0.0), axis=-1, keepdims=True)
        delta = jnp.sum(jnp.where(msk, prod, 0.0), axis=-1, keepdims=True)
        pr = jnp.where(valid, jnp.exp(sc - lse_e), 0.0)
        dom = jnp.where(msk, do2, 0.0).astype(BF16)
        ds = (pr * (_nt(dom, vcat) - delta)).astype(BF16)
        dq = dq + _nn(ds, jnp.where(msk_keys, kcat, 0.0).astype(BF16))
        dk = dk + _tn(ds, qm)
        dv = dv + _tn(pr.astype(BF16), dom)
    return dq, dk, dv


def _attn_specs(dil):
    rows = ATTN_BLOCK * dil
    if dil == 1:
        cur = lambda at: pl.BlockSpec((rows, 512), lambda n: (n, at // 512))
        prev = lambda at: pl.BlockSpec((rows, 512), lambda n: (jnp.maximum(n - 1, 0), at // 512))
    else:
        cur = lambda at: pl.BlockSpec((rows, HEAD_LANES), lambda n, p: (n, at // HEAD_LANES + p))
        prev = lambda at: pl.BlockSpec((rows, HEAD_LANES), lambda n, p: (jnp.maximum(n - 1, 0), at // HEAD_LANES + p))
    return cur, prev


def _attn_loop(dil, one_pair):
    if dil == 1:
        for p in range(4):
            one_pair(slice(None), pl.ds(p * HEAD_LANES, HEAD_LANES), p)
    else:
        p = pl.program_id(1)

        def step(r, carry):
            one_pair(pl.ds(r, ATTN_BLOCK, stride=dil), slice(None), p)
            return carry

        lax.fori_loop(0, dil, step, 0)


def _dil_attn_fwd(qa, ka, proj, dil, *, name):
    s = qa.shape[0]

    def body(q_ref, kp_ref, kc_ref, vp_ref, vc_ref, o_ref, lse_ref):
        first = pl.program_id(0) == 0

        def one_pair(rows, cols, p):
            kcat = jnp.concatenate([kp_ref[rows, cols], kc_ref[rows, cols]], axis=0).astype(BF16)
            vcat = jnp.concatenate([vp_ref[rows, cols], vc_ref[rows, cols]], axis=0).astype(BF16)
            o2, lse2 = _attn_pair_fwd(q_ref[rows, cols], kcat, vcat, _pair_slopes(p), dil, first)
            o_ref[rows, cols] = o2
            lse_ref[rows, cols] = lse2

        _attn_loop(dil, one_pair)

    cur, prev = _attn_specs(dil)
    grid = (s // ATTN_BLOCK,) if dil == 1 else (s // (ATTN_BLOCK * dil), 4)
    return pl.pallas_call(
        body, name=name, grid=grid, in_specs=[cur(0), prev(0), cur(0), prev(C_AV), cur(C_AV)], out_specs=[cur(0), cur(0)],
        out_shape=[jax.ShapeDtypeStruct((s, 512), F32)] * 2,
        compiler_params=_params(*["parallel"] * len(grid)))(qa, ka, ka, proj, proj)


def _attn_merge(branches, y_gla, *, name):
    s = y_gla.shape[0]
    tm = ROW_TILE

    def body(o0, l0, o1, l1, o2, l2, yg_ref, mixed_ref, y_ref, lse_ref):
        m = jnp.maximum(jnp.maximum(l0[...], l1[...]), l2[...])
        w0, w1, w2 = jnp.exp(l0[...] - m), jnp.exp(l1[...] - m), jnp.exp(l2[...] - m)
        zs = w0 + w1 + w2
        y = (w0 * o0[...] + w1 * o1[...] + w2 * o2[...]) / zs
        y_ref[...] = y
        lse_ref[...] = m + jnp.log(zs)
        mixed_ref[:, 0:512] = yg_ref[...]
        mixed_ref[:, 512:1024] = y.astype(BF16)

    blk = pl.BlockSpec((tm, 512), lambda i: (i, 0))
    args = [t for pair in branches for t in pair]
    return pl.pallas_call(
        body, name=name, grid=(s // tm,), in_specs=[blk] * 7,
        out_specs=[pl.BlockSpec((tm, 1024), lambda i: (i, 0)), blk, blk],
        out_shape=[jax.ShapeDtypeStruct((s, 1024), BF16), jax.ShapeDtypeStruct((s, 512), F32),
                   jax.ShapeDtypeStruct((s, 512), F32)],
        compiler_params=_params("parallel"))(*args, y_gla)


def _dil_attn_bwd(qa, ka, proj, y_att, lse, dmixed, dil, *, name):
    s = qa.shape[0]
    blk = ATTN_BLOCK

    def body(q_ref, kp_ref, kc_ref, vp_ref, vc_ref, y_ref, lse_ref, do_ref, dq_ref, dkc_ref, dkp_ref, dvc_ref, dvp_ref):
        first = pl.program_id(0) == 0

        def one_pair(rows, cols, p):
            kcat = jnp.concatenate([kp_ref[rows, cols], kc_ref[rows, cols]], axis=0).astype(BF16)
            vcat = jnp.concatenate([vp_ref[rows, cols], vc_ref[rows, cols]], axis=0).astype(BF16)
            dq, dk, dv = _attn_pair_bwd(q_ref[rows, cols], kcat, vcat, do_ref[rows, cols], y_ref[rows, cols],
                                        lse_ref[rows, cols], _pair_slopes(p), dil, first)
            dq_ref[rows, cols] = dq
            dkp_ref[rows, cols] = dk[0:blk]
            dkc_ref[rows, cols] = dk[blk:2 * blk]
            dvp_ref[rows, cols] = dv[0:blk]
            dvc_ref[rows, cols] = dv[blk:2 * blk]

        _attn_loop(dil, one_pair)

    cur, prev = _attn_specs(dil)
    grid = (s // blk,) if dil == 1 else (s // (blk * dil), 4)
    return pl.pallas_call(
        body, name=name, grid=grid,
        in_specs=[cur(0), prev(0), cur(0), prev(C_AV), cur(C_AV), cur(0), cur(0), cur(512)], out_specs=[cur(0)] * 5,
        out_shape=[jax.ShapeDtypeStruct((s, 512), F32)] * 5, compiler_params=_params(*["parallel"] * len(grid)),
    )(qa, ka, ka, proj, proj, y_att, lse, dmixed)


def _attn_post(parts, proj, qg, kg, *, name):
    s = proj.shape[0]
    tm = ATTN_BLOCK
    nblk = s // tm

    def body(*refs):
        ins, (q_ref, k_ref, qg_ref, kg_ref, dq_out, dk_out, dv_out, sums_ref) = refs[:15], refs[15:]
        i = pl.program_id(0)

        @pl.when(i == 0)
        def _():
            sums_ref[...] = jnp.zeros_like(sums_ref)

        dq = jnp.zeros((tm, 512), F32)
        dk = jnp.zeros((tm, 512), F32)
        dv = jnp.zeros((tm, 512), F32)
        for g, dil in enumerate(DILATIONS):
            dq_r, dkc_r, dkp_r, dvc_r, dvp_r = ins[5 * g:5 * g + 5]
            inside = (i + dil < nblk).astype(F32)
            dq = dq + dq_r[...]
            dk = dk + dkc_r[...] + inside * dkp_r[...]
            dv = dv + dvc_r[...] + inside * dvp_r[...]
        dv_out[...] = dv.astype(BF16)
        hs = _head_sum_matrix()
        for row, (x_ref, g_ref, dy, out, post) in enumerate(((q_ref, qg_ref, dq, dq_out, 0.125), (k_ref, kg_ref, dk, dk_out, 1.0))):
            x = x_ref[...]
            rs = lax.rsqrt(_nn(x * x, hs, precision=HIGHEST) * (1.0 / 64) + EPS)
            xn = x * rs
            dy = dy * post
            sums_ref[row] += _fold8(dy * xn)
            dn = dy * g_ref[...]
            out[...] = (rs * (dn - xn * (_nn(dn * xn, hs, precision=HIGHEST) * (1.0 / 64)))).astype(BF16)

        @pl.when(i == nblk - 1)
        def _():
            _spread_total(sums_ref)

    here = pl.BlockSpec((tm, 512), lambda i: (i, 0))
    specs = []
    for dil in DILATIONS:
        later = pl.BlockSpec((tm, 512), lambda i, dil=dil: (jnp.minimum(i + dil, nblk - 1), 0))
        specs += [here, here, later, here, later]
    col = lambda at: pl.BlockSpec((tm, 512), lambda i: (i, at // 512))
    vec = pl.BlockSpec((1, 512), lambda i: (0, 0))
    return pl.pallas_call(
        body, name=name, grid=(nblk,), in_specs=specs + [col(C_AQ), col(C_AK), vec, vec],
        out_specs=[here, here, here, pl.BlockSpec((2, 8, 512), lambda i: (0, 0, 0))],
        out_shape=[jax.ShapeDtypeStruct((s, 512), BF16)] * 3 + [jax.ShapeDtypeStruct((2, 8, 512), F32)],
        compiler_params=_params("arbitrary"))(*[t for part in parts for t in part], proj, proj, qg, kg)


FFN_TM, FFN_TN = 256, 1408
HALO = 16


def _conv3(u_ref, halo_ref, w_ref, b_ref, first):
    u = u_ref[...].astype(F32)
    ext = jnp.concatenate([jnp.where(first, 0.0, halo_ref[...].astype(F32)), u], axis=0)
    u1 = pltpu.roll(ext, 1, 0)[HALO:]
    u2 = pltpu.roll(ext, 2, 0)[HALO:]
    return b_ref[...] + w_ref[0:1, :] * u2 + w_ref[1:2, :] * u1 + w_ref[2:3, :] * u, u, u1, u2


def _ffn_specs(tm, tn):
    nj = D_FF // tn
    blk = lambda half: pl.BlockSpec((tm, tn), lambda j, i: (i, j + half * nj))
    halo = lambda half: pl.BlockSpec((HALO, tn), lambda j, i: (jnp.maximum(i * (tm // HALO) - 1, 0), j + half * nj))
    wspec = lambda half: pl.BlockSpec((3, tn), lambda j, i: (0, j + half * nj))
    bspec = lambda half: pl.BlockSpec((1, tn), lambda j, i: (0, j + half * nj))
    return [blk(0), halo(0), blk(1), halo(1), wspec(0), wspec(1), bspec(0), bspec(1)]


def _conv_swiglu_fwd(u, conv_w, conv_b, *, name):
    s = u.shape[0]
    tm, tn = FFN_TM, FFN_TN

    def body(ug_ref, hg_ref, uv_ref, hv_ref, wg_ref, wv_ref, bg_ref, bv_ref, act_ref):
        first = pl.program_id(1) == 0
        cg = _conv3(ug_ref, hg_ref, wg_ref, bg_ref, first)[0]
        cv = _conv3(uv_ref, hv_ref, wv_ref, bv_ref, first)[0]
        act_ref[...] = (cg * _sigmoid(cg) * cv).astype(BF16)

    return pl.pallas_call(
        body, name=name, grid=(D_FF // tn, s // tm), in_specs=_ffn_specs(tm, tn),
        out_specs=pl.BlockSpec((tm, tn), lambda j, i: (i, j)), out_shape=jax.ShapeDtypeStruct((s, D_FF), BF16),
        compiler_params=_params("parallel", "parallel"))(u, u, u, u, conv_w, conv_w, conv_b, conv_b)


def _conv_swiglu_bwd_pre(u, conv_w, conv_b, dact, *, name):
    s = u.shape[0]
    tm, tn = FFN_TM, FFN_TN

    def body(ug_ref, hg_ref, uv_ref, hv_ref, wg_ref, wv_ref, bg_ref, bv_ref, da_ref, duc_ref, sums_ref):
        i = pl.program_id(1)

        @pl.when(i == 0)
        def _():
            sums_ref[...] = jnp.zeros_like(sums_ref)

        cg, g0, g1, g2 = _conv3(ug_ref, hg_ref, wg_ref, bg_ref, i == 0)
        cv, v0, v1, v2 = _conv3(uv_ref, hv_ref, wv_ref, bv_ref, i == 0)
        da = da_ref[...].astype(F32)
        sg = _sigmoid(cg)
        dg = da * cv * (sg * (1.0 + cg * (1.0 - sg)))
        dv = da * (cg * sg)
        duc_ref[0] = dg.astype(BF16)
        duc_ref[1] = dv.astype(BF16)
        for half, (d, taps) in enumerate(((dg, (g2, g1, g0)), (dv, (v2, v1, v0)))):
            for t, tap in enumerate(taps):
                sums_ref[half, t] += _fold8(d * tap)
            sums_ref[half, 3] += _fold8(d)

        @pl.when(i == s // tm - 1)
        def _():
            _spread_total(sums_ref)

    return pl.pallas_call(
        body, name=name, grid=(D_FF // tn, s // tm),
        in_specs=_ffn_specs(tm, tn) + [pl.BlockSpec((tm, tn), lambda j, i: (i, j))],
        out_specs=[pl.BlockSpec((2, tm, tn), lambda j, i: (0, i, j)), pl.BlockSpec((2, 4, 8, tn), lambda j, i: (0, 0, 0, j))],
        out_shape=[jax.ShapeDtypeStruct((2, s, D_FF), BF16), jax.ShapeDtypeStruct((2, 4, 8, D_FF), F32)],
        compiler_params=_params("parallel", "arbitrary"))(u, u, u, u, conv_w, conv_w, conv_b, conv_b, dact)


def _conv_bwd(duc, conv_w, *, name):
    _, s, _ = duc.shape
    tm, tn = FFN_TM, FFN_TN
    nj, ni = D_FF // tn, s // tm

    def body(d_ref, halo_ref, w_ref, du_ref):
        last = pl.program_id(2) == ni - 1
        d = d_ref[0].astype(F32)
        ext = jnp.concatenate([d, jnp.where(last, 0.0, halo_ref[0].astype(F32))], axis=0)
        n = tm + HALO
        d1 = pltpu.roll(ext, n - 1, 0)[:tm]
        d2 = pltpu.roll(ext, n - 2, 0)[:tm]
        du_ref[...] = (w_ref[2:3, :] * d + w_ref[1:2, :] * d1 + w_ref[0:1, :] * d2).astype(BF16)

    return pl.pallas_call(
        body, name=name, grid=(2, nj, ni),
        in_specs=[pl.BlockSpec((1, tm, tn), lambda g, j, i: (g, i, j)),
                  pl.BlockSpec((1, HALO, tn), lambda g, j, i: (g, jnp.minimum((i + 1) * (tm // HALO), s // HALO - 1), j)),
                  pl.BlockSpec((3, tn), lambda g, j, i: (0, g * nj + j))],
        out_specs=pl.BlockSpec((tm, tn), lambda g, j, i: (i, g * nj + j)),
        out_shape=jax.ShapeDtypeStruct((s, 2 * D_FF), BF16),
        compiler_params=_params("parallel", "parallel", "parallel"))(duc, duc, conv_w)


def _loss_head(x1, ffn, gate, target, *, name):
    s, d = x1.shape
    tm = ROW_TILE

    def body(x_ref, f_ref, g_ref, t_ref, dy_ref, df_ref, sums_ref):
        i = pl.program_id(0)

        @pl.when(i == 0)
        def _():
            sums_ref[...] = jnp.zeros_like(sums_ref)

        f = f_ref[...]
        err = x_ref[...] + g_ref[...] * f - t_ref[...]
        dy = err * (1.0 / d)
        dy_ref[...] = dy
        df_ref[...] = (g_ref[...] * dy).astype(BF16)
        sums_ref[0] += _fold8(dy * f)
        sums_ref[1] += _fold8(err * err)

        @pl.when(i == s // tm - 1)
        def _():
            _spread_total(sums_ref)

    row = pl.BlockSpec((tm, d), lambda i: (i, 0))
    return pl.pallas_call(
        body, name=name, grid=(s // tm,), in_specs=[row, row, pl.BlockSpec((1, d), lambda i: (0, 0)), row],
        out_specs=[row, row, pl.BlockSpec((2, 8, d), lambda i: (0, 0, 0))],
        out_shape=[jax.ShapeDtypeStruct((s, d), F32), jax.ShapeDtypeStruct((s, d), BF16), jax.ShapeDtypeStruct((2, 8, d), F32)],
        compiler_params=_params("arbitrary"))(x1, ffn, gate, target)


def _adamw(w, g, m, v, *, name):
    rows, cols = w.shape
    tm = next((t for t in range(ROW_TILE, 7, -8) if rows % t == 0), rows)

    def body(w_ref, g_ref, m_ref, v_ref, d_ref, mo_ref, vo_ref):
        gv = g_ref[...]
        mn = ADAM_B1 * m_ref[...] + (1.0 - ADAM_B1) * gv
        vn = ADAM_B2 * v_ref[...] + (1.0 - ADAM_B2) * (gv * gv)
        m_hat = mn / (1.0 - ADAM_B1 ** ADAM_STEP)
        v_hat = vn / (1.0 - ADAM_B2 ** ADAM_STEP)
        d_ref[...] = -ADAM_LR * (m_hat / (jnp.sqrt(v_hat) + ADAM_EPS) + ADAM_WD * w_ref[...])
        mo_ref[...] = mn
        vo_ref[...] = vn

    blk = pl.BlockSpec((tm, cols), lambda i: (i, 0))
    return pl.pallas_call(
        body, name=name, grid=(rows // tm,), in_specs=[blk] * 4, out_specs=[blk] * 3,
        out_shape=[jax.ShapeDtypeStruct((rows, cols), F32)] * 3, compiler_params=_params("parallel"))(w, g, m, v)


def _colsum(t):
    return t[..., 0, :]


def _in_proj_layout(w_in):
    pad = jnp.zeros((w_in.shape[0], PROJ_W - C_LR - GLA_GATE_RANK), w_in.dtype)
    return jnp.concatenate([w_in[:, :1536], w_in[:, 1552:], w_in[:, 1536:1552], pad], axis=1)


def _in_proj_grad_layout(g):
    return jnp.concatenate([g[:, :1536], g[:, C_LR:C_LR + GLA_GATE_RANK], g[:, 1536:C_LR]], axis=1)


def _gate_layout(gla_w_gate):
    return jnp.pad(gla_w_gate, ((0, HEAD_LANES - GLA_GATE_RANK), (0, 0))).astype(BF16)


def _local_step(x, target, mod, wi, wo, wup, wdown, conv_w, conv_b, wg, bg, gn, qg, kg, n1g, n2g):
    d = D_MODEL
    sh1, sc1, g1, sh2, sc2, g2 = [mod[:, i * d:(i + 1) * d] for i in range(6)]
    qg8, kg8 = jnp.tile(qg, (1, 8)), jnp.tile(kg, (1, 8))

    _, h1 = _norm_mod_fwd(x, None, None, n1g, sc1, sh1, name="norm1_fwd")
    proj = _mm(h1, wi, tm=512, tn=640, tk=d, name="in_proj")
    o_raw, y_gla, states = _gla_fwd(proj, wg, bg, gn, name="gla_fwd")
    qa, ka = _attn_prep(proj, qg8, kg8, name="attn_prep")
    branches = [_dil_attn_fwd(qa, ka, proj, dil, name=f"attn_fwd_d{dil}") for dil in DILATIONS]
    mixed, y_att, lse = _attn_merge(branches, y_gla, name="attn_merge")
    attn_out = _mm(mixed, wo, tm=512, tn=d, tk=d, name="out_proj")
    x1, h2 = _norm_mod_fwd(x, attn_out, g1, n2g, sc2, sh2, name="norm2_fwd")
    u = _mm(h2, wup, out_dtype=BF16, tm=512, tn=1408, tk=d, name="up_proj")
    act = _conv_swiglu_fwd(u, conv_w, conv_b, name="conv_swiglu_fwd")
    ffn = _mm(act, wdown, tm=512, tn=d, tk=1408, name="down_proj")
    dy, dffn, head_sums = _loss_head(x1, ffn, g2, target, name="loss_head")

    dact = _mm(dffn, wdown, tb=True, out_dtype=BF16, tm=512, tn=1408, tk=d, name="down_proj_dx")
    g_wdown = _mm(act, dffn, ta=True, tm=1408, tn=d, tk=512, name="down_proj_dw")
    duc, conv_sums = _conv_swiglu_bwd_pre(u, conv_w, conv_b, dact, name="conv_swiglu_bwd")
    du = _conv_bwd(duc, conv_w, name="conv_bwd")
    dh2 = _mm(du, wup, tb=True, tm=512, tn=d, tk=1408, name="up_proj_dx")
    g_wup = _mm(h2, du, ta=True, tm=d, tn=1408, tk=512, name="up_proj_dw")
    dx1, dao, n2_sums = _norm_mod_bwd(x1, dh2, dy, n2g, sc2, attn_out, g1, name="norm2_bwd")

    dmixed = _mm(dao, wo, tb=True, tm=512, tn=d, tk=d, name="out_proj_dx")
    g_wo = _mm(mixed, dao, ta=True, tm=d, tn=d, tk=512, name="out_proj_dw")
    dgq, dgk, dgv, dgr, dlr, g_wg, gla_sums = _gla_bwd(proj, wg, bg, gn, o_raw, states, dmixed, name="gla_bwd")
    parts = [_dil_attn_bwd(qa, ka, proj, y_att, lse, dmixed, dil, name=f"attn_bwd_d{dil}") for dil in DILATIONS]
    daq, dak, dav, qk_sums = _attn_post(parts, proj, qg8, kg8, name="attn_post")
    dproj = jnp.concatenate([dgq, dgk, dgv, dgr, daq, dak, dav, dlr], axis=1)
    dh1 = _mm(dproj, wi, tb=True, tm=512, tn=d, tk=640, name="in_proj_dx")
    g_wi = _mm(h1, dproj, ta=True, tm=d, tn=640, tk=512, name="in_proj_dw")
    grad_x, _, n1_sums = _norm_mod_bwd(x, dh1, dx1, n1g, sc1, None, None, name="norm1_bwd")

    n1, n2, hs, cs = _colsum(n1_sums), _colsum(n2_sums), _colsum(head_sums), _colsum(conv_sums)
    gs, qs = _colsum(gla_sums), _colsum(qk_sums)
    dmod = jnp.concatenate([n1[1], n1[0] * n1g[0], n2[2], n2[1], n2[0] * n2g[0], hs[0]])
    small = dict(
        dmod=dmod,
        norm1_g=n1[0] * (1.0 + sc1[0]), norm2_g=n2[0] * (1.0 + sc2[0]),
        gla_w_gate=g_wg[:GLA_GATE_RANK], gla_b_gate=gs[0, :256], gla_norm_g=gs[1].reshape(4, 128).sum(axis=0),
        q_norm_g=qs[0].reshape(8, 64).sum(axis=0), k_norm_g=qs[1].reshape(8, 64).sum(axis=0),
        conv_w=jnp.concatenate([cs[0, :3], cs[1, :3]], axis=1), conv_b=jnp.concatenate([cs[0, 3], cs[1, 3]]),
    )
    return head_sums[1], grad_x, (g_wi, g_wo, g_wup, g_wdown), small


N_DEV, N_CHIP = 8, 4
ANY = pl.BlockSpec(memory_space=pl.ANY)
VMEM_SPEC = pl.BlockSpec(memory_space=pltpu.VMEM)


def _place():
    x, y, c = lax.axis_index("x"), lax.axis_index("y"), lax.axis_index("c")
    other_chips = [(1 - x, y), (x, 1 - y), (1 - x, 1 - y)]
    return x, y, c, (x, y, 1 - c), other_chips


def _all_gather_small(v, *, name):
    m, n = v.shape

    def body(v_ref, out_ref, send_sems, recv_sems, local_sem):
        x, y, c, sibling, chips = _place()
        me = (x, y, c)

        def rows(px, py, pc):
            return out_ref.at[pl.ds((4 * px + 2 * py + pc) * m, m), :]

        def copy(k, block, to, src=None):
            return pltpu.make_async_remote_copy(
                src_ref=rows(*block) if src is None else src, dst_ref=rows(*block), send_sem=send_sems.at[k],
                recv_sem=recv_sems.at[k], device_id=to, device_id_type=MESH)

        mine = pltpu.make_async_copy(v_ref, rows(*me), local_sem)
        mine.start()
        first = [copy(0, me, sibling, src=v_ref)]
        first += [copy(1 + j, me, (*chip, c), src=v_ref) for j, chip in enumerate(chips)]
        for cp in first:
            cp.start()
        passed = [copy(4 + j, (*chip, c), sibling) for j, chip in enumerate(chips)]
        for j, chip in enumerate(chips):
            copy(1 + j, (*chip, c), me).wait_recv()
            passed[j].start()
        copy(0, sibling, me).wait_recv()
        for j, chip in enumerate(chips):
            copy(4 + j, (*chip, 1 - c), me).wait_recv()
        for cp in first + passed:
            cp.wait_send()
        mine.wait()

    return pl.pallas_call(
        body, name=name, out_shape=jax.ShapeDtypeStruct((N_DEV * m, n), v.dtype), in_specs=[VMEM_SPEC], out_specs=VMEM_SPEC,
        scratch_shapes=[pltpu.SemaphoreType.DMA((7,)), pltpu.SemaphoreType.DMA((7,)), pltpu.SemaphoreType.DMA],
    )(v)


def _gather_weight_shards(flat, *, name):
    r, n = flat.shape
    half = r // 2

    def body(src_ref, out_ref, send_sems, recv_sems, local_sem):
        x, y, c, sibling, chips = _place()

        def half_of(chip, core):
            return out_ref.at[2 * chip[0] + chip[1], pl.ds(core * half, half), :]

        def copy(k, dst, to, src):
            return pltpu.make_async_remote_copy(src_ref=src, dst_ref=dst, send_sem=send_sems.at[k], recv_sem=recv_sems.at[k],
                                                device_id=to, device_id_type=MESH)

        mine = pltpu.make_async_copy(src_ref, out_ref.at[2 * x + y], local_sem)
        mine.start()
        my_half = src_ref.at[pl.ds(c * half, half), :]
        first = [copy(k, half_of((x, y), c), (*chip, c), my_half) for k, chip in enumerate(chips)]
        for cp in first:
            cp.start()
        passed = []
        for k, chip in enumerate(chips):
            copy(k, half_of(chip, c), (*chip, c), my_half).wait_recv()
            passed.append(copy(3 + k, half_of(chip, c), sibling, half_of(chip, c)))
            passed[k].start()
        for k, chip in enumerate(chips):
            copy(3 + k, half_of(chip, 1 - c), sibling, my_half).wait_recv()
        for cp in first + passed:
            cp.wait_send()
        mine.wait()

    return pl.pallas_call(
        body, name=name, out_shape=jax.ShapeDtypeStruct((N_CHIP, r, n), flat.dtype), in_specs=[ANY], out_specs=ANY,
        scratch_shapes=[pltpu.SemaphoreType.DMA((6,)), pltpu.SemaphoreType.DMA((6,)), pltpu.SemaphoreType.DMA],
    )(flat)


def _pair_exchange(v, *, name):
    def body(src_ref, out_ref, send_sem, recv_sem):
        _, _, _, sibling, _ = _place()
        cp = pltpu.make_async_remote_copy(src_ref=src_ref, dst_ref=out_ref, send_sem=send_sem, recv_sem=recv_sem,
                                          device_id=sibling, device_id_type=MESH)
        cp.start()
        cp.wait()

    return pl.pallas_call(
        body, name=name, out_shape=jax.ShapeDtypeStruct(v.shape, v.dtype), in_specs=[ANY], out_specs=ANY,
        scratch_shapes=[pltpu.SemaphoreType.DMA, pltpu.SemaphoreType.DMA])(v)


def _chip_scatter(p, *, name):
    _, rows, n = p.shape

    def body(p_ref, out_ref, send_sems, recv_sems):
        _, _, c, _, chips = _place()
        cps = [pltpu.make_async_remote_copy(src_ref=p_ref.at[2 * chip[0] + chip[1]], dst_ref=out_ref.at[k],
                                            send_sem=send_sems.at[k], recv_sem=recv_sems.at[k], device_id=(*chip, c),
                                            device_id_type=MESH) for k, chip in enumerate(chips)]
        for cp in cps:
            cp.start()
        for cp in cps:
            cp.wait()

    return pl.pallas_call(
        body, name=name, out_shape=jax.ShapeDtypeStruct((3, rows, n), p.dtype), in_specs=[ANY], out_specs=ANY,
        scratch_shapes=[pltpu.SemaphoreType.DMA((3,)), pltpu.SemaphoreType.DMA((3,))])(p)


def _add(terms, *, also_bf16=False, name):
    rows, n = terms[0].shape
    tm = next(t for t in (2512, 2048, 1024, 512, 256, 16) if rows % t == 0)
    n_in = len(terms)

    def body(*refs):
        acc = refs[0][...].astype(F32)
        for r in refs[1:n_in]:
            acc = acc + r[...].astype(F32)
        refs[n_in][...] = acc
        if also_bf16:
            refs[n_in + 1][...] = acc.astype(BF16)

    blk = pl.BlockSpec((tm, n), lambda i: (i, 0))
    shapes = [jax.ShapeDtypeStruct((rows, n), F32)] + ([jax.ShapeDtypeStruct((rows, n), BF16)] if also_bf16 else [])
    out = pl.pallas_call(body, name=name, grid=(rows // tm,), in_specs=[blk] * n_in, out_specs=[blk] * len(shapes),
                         out_shape=shapes, compiler_params=_params("parallel"))(*terms)
    return out if also_bf16 else out[0]


def _sum_devices(gathered, *, name):
    _, m, n = gathered.shape

    def body(g_ref, tot_ref, loss_ref):
        tot = g_ref[0]
        for dev in range(1, N_DEV):
            tot = tot + g_ref[dev]
        tot_ref[...] = tot
        loss_ref[...] = jnp.full((8, n), (0.5 / D_MODEL) * jnp.sum(tot[0:8]), F32)

    return pl.pallas_call(body, name=name, in_specs=[VMEM_SPEC], out_specs=[VMEM_SPEC, VMEM_SPEC],
                          out_shape=[jax.ShapeDtypeStruct((m, n), F32), jax.ShapeDtypeStruct((8, n), F32)])(gathered)


def _ada_mod(cond_all, w_ada_shard, *, name):
    tn = 512

    def body(a_ref, b_ref, o_ref):
        o_ref[...] = _nn(a_ref[...], b_ref[...], precision=HIGHEST)

    return pl.pallas_call(
        body, name=name, grid=(w_ada_shard.shape[1] // tn,),
        in_specs=[pl.BlockSpec(cond_all.shape, lambda j: (0, 0)), pl.BlockSpec((D_MODEL, tn), lambda j: (0, j))],
        out_specs=pl.BlockSpec((N_DEV, tn), lambda j: (0, j)),
        out_shape=jax.ShapeDtypeStruct((N_DEV, w_ada_shard.shape[1]), F32), compiler_params=_params("parallel"))(cond_all, w_ada_shard)


def _ada_grad(cond_all, dmod_cols, *, name):
    tm = 256

    def body(a_ref, b_ref, o_ref):
        o_ref[...] = lax.dot_general(a_ref[...], b_ref[...], (((0,), (0,)), ((), ())), precision=HIGHEST,
                                     preferred_element_type=F32)

    return pl.pallas_call(
        body, name=name, grid=(D_MODEL // tm,),
        in_specs=[pl.BlockSpec((N_DEV, tm), lambda i: (0, i)), pl.BlockSpec(dmod_cols.shape, lambda i: (0, 0))],
        out_specs=pl.BlockSpec((tm, dmod_cols.shape[1]), lambda i: (i, 0)),
        out_shape=jax.ShapeDtypeStruct((D_MODEL, dmod_cols.shape[1]), F32), compiler_params=_params("parallel"))(cond_all, dmod_cols)


def _silu_rows(c8, *, name):
    def body(c_ref, o_ref):
        cv = c_ref[...]
        o_ref[...] = cv * _sigmoid(cv)

    return pl.pallas_call(body, name=name, in_specs=[VMEM_SPEC], out_specs=VMEM_SPEC,
                          out_shape=jax.ShapeDtypeStruct(c8.shape, F32))(c8)


def _rows128(t, rows=None):
    flat = t.reshape(-1, 128)
    return flat if rows is None else jnp.pad(flat, ((0, rows - flat.shape[0]), (0, 0)))


def _col_shards(full, n):
    r = full.shape[0]
    return full.reshape(r, N_CHIP, n).transpose(1, 0, 2).reshape(N_CHIP, r * n // 128, 128)


def _from_col_shards(shards, r, n):
    return shards.reshape(N_CHIP, r, n).transpose(1, 0, 2).reshape(r, N_CHIP * n)


SHARD_SHAPES = ((1024, 772), (256, 1024), (1024, 1408), (704, 1024))
SHARD_ROWS = tuple(a * b // 128 for a, b in SHARD_SHAPES)
SHARD_AT = tuple(sum(SHARD_ROWS[:i]) for i in range(5))


def kernel(x, c, w_ada, b_ada, norm1_g, w_in, gla_w_gate, gla_b_gate, gla_norm_g, q_norm_g, k_norm_g, w_out, norm2_g, w_up, conv_w, conv_b, w_down, loss_target, m_w_ada, m_b_ada, m_norm1_g, m_w_in, m_gla_w_gate, m_gla_b_gate, m_gla_norm_g, m_q_norm_g, m_k_norm_g, m_w_out, m_norm2_g, m_w_up, m_conv_w, m_conv_b, m_w_down, v_w_ada, v_b_ada, v_norm1_g, v_w_in, v_gla_w_gate, v_gla_b_gate, v_gla_norm_g, v_q_norm_g, v_k_norm_g, v_w_out, v_norm2_g, v_w_up, v_conv_w, v_conv_b, v_w_down):
    d = D_MODEL
    ax, ay, ac = lax.axis_index("x"), lax.axis_index("y"), lax.axis_index("c")
    chip, dev = 2 * ax + ay, 4 * ax + 2 * ay + ac
    r_all, half = SHARD_AT[4], SHARD_AT[4] // 2

    cond = _silu_rows(jnp.broadcast_to(c, (8, d)), name="cond_silu")[0:1]
    small_in = jnp.concatenate([_rows128(cond), _rows128(conv_w[0]), _rows128(gla_w_gate[0])], axis=0)
    small_in = _rows128(small_in, 56)
    got = _all_gather_small(small_in, name="gather_small").reshape(N_DEV, 56, 128)
    cond_all = got[:, 0:8].reshape(N_DEV, d)
    conv_w_full = _from_col_shards(got[0::2, 8:41].reshape(N_CHIP, 3 * 1408 // 128, 128), 3, 1408)
    gate_full = _from_col_shards(got[0::2, 41:49].reshape(N_CHIP, 16 * 64 // 128, 128), GLA_GATE_RANK, 64)
    mod_part = _ada_mod(cond_all, w_ada[0], name="ada_mod")
    mod_got = _all_gather_small(_rows128(mod_part), name="gather_mod").reshape(N_DEV, N_DEV, 1536)
    mod_all = mod_got[0::2].transpose(1, 0, 2).reshape(N_DEV, 6 * d) + b_ada
    mod = lax.dynamic_slice_in_dim(mod_all, dev, 1, axis=0)

    flat = jnp.concatenate([_rows128(w_in[0]), _rows128(w_out[0]), _rows128(w_up[0]), _rows128(w_down[0])], axis=0)
    shards = _gather_weight_shards(flat.astype(BF16), name="gather_weights")
    part = lambda i: shards[:, SHARD_AT[i]:SHARD_AT[i + 1]]
    w_in_full = _from_col_shards(part(0), d, 772)
    w_out_full = part(1).reshape(d, d)
    w_up_full = _from_col_shards(part(2), d, 1408)
    w_down_full = part(3).reshape(D_FF, d)

    err2, grad_x, (g_wi, g_wo, g_wup, g_wdown), small = _local_step(
        x[0], loss_target[0], mod, _in_proj_layout(w_in_full), w_out_full, w_up_full, w_down_full, conv_w_full, conv_b,
        _gate_layout(gate_full), gla_b_gate, gla_norm_g, q_norm_g, k_norm_g, norm1_g, norm2_g)

    pieces = [err2[0], small["dmod"], small["norm1_g"], small["norm2_g"], small["gla_w_gate"].reshape(-1), small["gla_b_gate"],
              small["gla_norm_g"], small["q_norm_g"], small["k_norm_g"], small["conv_w"].reshape(-1), small["conv_b"]]
    sizes = [p.shape[0] for p in pieces]
    at = [sum(sizes[:i]) for i in range(len(sizes) + 1)]
    vec = _rows128(jnp.concatenate(pieces), 288)
    got = _all_gather_small(vec, name="gather_grads").reshape(N_DEV, 288, 128)
    total, loss8 = _sum_devices(got, name="sum_devices")
    total = total.reshape(-1)
    seg = lambda i: total[at[i]:at[i + 1]]
    dmod_all = got.reshape(N_DEV, -1)[:, at[1]:at[2]]
    g_small = dict(
        b_ada=seg(1)[None], norm1_g=seg(2)[None], norm2_g=seg(3)[None],
        gla_w_gate=lax.dynamic_slice_in_dim(seg(4).reshape(GLA_GATE_RANK, 256), chip * 64, 64, axis=1),
        gla_b_gate=seg(5)[None], gla_norm_g=seg(6)[None], q_norm_g=seg(7)[None], k_norm_g=seg(8)[None],
        conv_w=lax.dynamic_slice_in_dim(seg(9).reshape(3, 2 * D_FF), chip * 1408, 1408, axis=1), conv_b=seg(10)[None])
    dmod_cols = lax.dynamic_slice_in_dim(dmod_all.reshape(N_DEV, 6 * d), chip * 1536, 1536, axis=1)
    g_w_ada = _ada_grad(cond_all, dmod_cols, name="ada_grad")

    g_flat = jnp.concatenate([_col_shards(_in_proj_grad_layout(g_wi), 772), g_wo.reshape(N_CHIP, SHARD_ROWS[1], 128),
                              _col_shards(g_wup, 1408), g_wdown.reshape(N_CHIP, SHARD_ROWS[3], 128)], axis=1)
    keep = lax.dynamic_slice_in_dim(g_flat, ac * half, half, axis=1)
    give = lax.dynamic_slice_in_dim(g_flat, (1 - ac) * half, half, axis=1)
    pair, pair_b = _add([keep.reshape(-1, 128), _pair_exchange(give, name="reduce_pair").reshape(-1, 128)], also_bf16=True,
                        name="reduce_pair_add")
    pair = pair.reshape(N_CHIP, half, 128)
    theirs = _chip_scatter(pair_b.reshape(N_CHIP, half, 128), name="reduce_chips")
    mine = lax.dynamic_index_in_dim(pair, chip, axis=0, keepdims=False)
    summed = _add([mine, theirs[0], theirs[1], theirs[2]], name="reduce_chips_add")
    other = _pair_exchange(summed, name="share_pair")
    lo = jnp.where(ac == 0, summed, other)
    hi = jnp.where(ac == 0, other, summed)
    red = jnp.concatenate([lo, hi], axis=0)
    g_big = [red[SHARD_AT[i]:SHARD_AT[i + 1]].reshape(SHARD_SHAPES[i]) for i in range(4)]

    grads = dict(w_ada=g_w_ada, w_in=g_big[0], w_out=g_big[1], w_up=g_big[2], w_down=g_big[3], **g_small)
    names = ["w_ada", "b_ada", "norm1_g", "w_in", "gla_w_gate", "gla_b_gate", "gla_norm_g", "q_norm_g", "k_norm_g", "w_out",
             "norm2_g", "w_up", "conv_w", "conv_b", "w_down"]
    ws = dict(w_ada=w_ada, b_ada=b_ada, norm1_g=norm1_g, w_in=w_in, gla_w_gate=gla_w_gate, gla_b_gate=gla_b_gate,
              gla_norm_g=gla_norm_g, q_norm_g=q_norm_g, k_norm_g=k_norm_g, w_out=w_out, norm2_g=norm2_g, w_up=w_up,
              conv_w=conv_w, conv_b=conv_b, w_down=w_down)
    ms = dict(w_ada=m_w_ada, b_ada=m_b_ada, norm1_g=m_norm1_g, w_in=m_w_in, gla_w_gate=m_gla_w_gate, gla_b_gate=m_gla_b_gate,
              gla_norm_g=m_gla_norm_g, q_norm_g=m_q_norm_g, k_norm_g=m_k_norm_g, w_out=m_w_out, norm2_g=m_norm2_g, w_up=m_w_up,
              conv_w=m_conv_w, conv_b=m_conv_b, w_down=m_w_down)
    vs = dict(w_ada=v_w_ada, b_ada=v_b_ada, norm1_g=v_norm1_g, w_in=v_w_in, gla_w_gate=v_gla_w_gate, gla_b_gate=v_gla_b_gate,
              gla_norm_g=v_gla_norm_g, q_norm_g=v_q_norm_g, k_norm_g=v_k_norm_g, w_out=v_w_out, norm2_g=v_norm2_g, w_up=v_w_up,
              conv_w=v_conv_w, conv_b=v_conv_b, w_down=v_w_down)
    g_out, d_out, m_out, v_out = [], [], [], []
    for nm in names:
        w2 = ws[nm].reshape(ws[nm].shape[-2:])
        g2 = grads[nm].reshape(w2.shape)
        dl, mn, vn = _adamw(w2, g2, ms[nm].reshape(w2.shape), vs[nm].reshape(w2.shape), name=f"adamw_{nm}")
        shape = ws[nm].shape
        g_out.append(g2.reshape(shape))
        d_out.append(dl.reshape(shape))
        m_out.append(mn.reshape(shape))
        v_out.append(vn.reshape(shape))
    return (loss8[0, 0], grad_x[None], *g_out, *d_out, *m_out, *v_out)
```

```python
import functools

import jax
import jax.numpy as jnp
from jax import lax
from jax.experimental import pallas as pl
from jax.experimental.pallas import tpu as pltpu

F32, BF16 = jnp.float32, jnp.bfloat16
HIGHEST = lax.Precision.HIGHEST
MESH = pl.DeviceIdType.MESH

D_MODEL = 1024
GLA_CHUNK = 64
GLA_GATE_TAU = 16.0
GLA_GATE_RANK = 16
HEAD_LANES = 128
ATTN_BLOCK = 128
DILATIONS = (1, 4, 16)
ALIBI_SLOPES = tuple(2.0 ** (-(h + 1)) for h in range(8))
D_FF = 2816
EPS = 1e-6
C_GQ, C_GK, C_GV, C_GR, C_AQ, C_AK, C_AV, C_LR, PROJ_W = 0, 256, 512, 1024, 1536, 2048, 2560, 3072, 3200
ADAM_LR, ADAM_B1, ADAM_B2, ADAM_EPS, ADAM_WD, ADAM_STEP = 0.001, 0.9, 0.999, 1e-08, 0.01, 10
VMEM_LIMIT_BYTES = 56 * 1024 * 1024
ROW_TILE = 256


def _params(*sem):
    return pltpu.CompilerParams(dimension_semantics=sem or None, vmem_limit_bytes=VMEM_LIMIT_BYTES)


def _nt(a, b):
    return lax.dot_general(a, b, (((1,), (1,)), ((), ())), preferred_element_type=F32)


def _tn(a, b):
    return lax.dot_general(a, b, (((0,), (0,)), ((), ())), preferred_element_type=F32)


def _nn(a, b, precision=None):
    return jnp.dot(a, b, preferred_element_type=F32, precision=precision)


def _fold8(v):
    return v.reshape(v.shape[0] // 8, 8, v.shape[1]).sum(axis=0)


def _spread_total(ref):
    t = ref[...]
    ref[...] = jnp.broadcast_to(jnp.sum(t, axis=-2, keepdims=True), t.shape)


def _sigmoid(x):
    return 1.0 / (1.0 + jnp.exp(-x))


def _mm(a, b, *, ta=False, tb=False, out_dtype=F32, tm, tn, tk, shard_cols=False, name):
    (k_a, m) = a.shape if ta else a.shape[::-1]
    (k_b, n) = b.shape[::-1] if tb else b.shape
    assert k_a == k_b and m % tm == 0 and n % tn == 0 and k_a % tk == 0, (name, a.shape, b.shape)
    nk = k_a // tk
    assert nk == 1 or out_dtype == F32, name
    dims = (((0 if ta else 1,), (1 if tb else 0,)), ((), ()))

    def body(a_ref, b_ref, o_ref):
        k = pl.program_id(2)
        part = lax.dot_general(a_ref[...].astype(BF16), b_ref[...].astype(BF16), dims, preferred_element_type=F32)
        if nk == 1:
            o_ref[...] = part.astype(out_dtype)
        else:
            @pl.when(k == 0)
            def _():
                o_ref[...] = part

            @pl.when(k > 0)
            def _():
                o_ref[...] += part

    a_spec = pl.BlockSpec((tk, tm), lambda i, j, k: (k, i)) if ta else pl.BlockSpec((tm, tk), lambda i, j, k: (i, k))
    b_spec = pl.BlockSpec((tn, tk), lambda i, j, k: (j, k)) if tb else pl.BlockSpec((tk, tn), lambda i, j, k: (k, j))
    if shard_cols:
        o_spec, o_shape = pl.BlockSpec((None, tm, tn), lambda i, j, k: (j, i, 0)), (n // tn, m, tn)
    else:
        o_spec, o_shape = pl.BlockSpec((tm, tn), lambda i, j, k: (i, j)), (m, n)
    return pl.pallas_call(
        body, name=name, grid=(m // tm, n // tn, nk), in_specs=[a_spec, b_spec], out_specs=o_spec,
        out_shape=jax.ShapeDtypeStruct(o_shape, out_dtype), compiler_params=_params("parallel", "parallel", "arbitrary"),
    )(a, b)


def _norm_mod_fwd(x, branch, gate, gain, scale, shift, *, name):
    s, d = x.shape
    tm = ROW_TILE
    has_branch = branch is not None

    def body(*refs):
        if has_branch:
            x_ref, br_ref, gate_ref, gain_ref, sc_ref, sh_ref, x1_ref, h_ref = refs
            xv = x_ref[...] + gate_ref[...] * br_ref[...]
            x1_ref[...] = xv
        else:
            x_ref, gain_ref, sc_ref, sh_ref, h_ref = refs
            xv = x_ref[...]
        r = lax.rsqrt(jnp.mean(xv * xv, axis=-1, keepdims=True) + EPS)
        h_ref[...] = ((xv * r) * gain_ref[...] * (1.0 + sc_ref[...]) + sh_ref[...]).astype(BF16)

    row = pl.BlockSpec((tm, d), lambda i: (i, 0))
    vec = pl.BlockSpec((1, d), lambda i: (0, 0))
    if has_branch:
        return pl.pallas_call(
            body, name=name, grid=(s // tm,), in_specs=[row, row, vec, vec, vec, vec], out_specs=[row, row],
            out_shape=[jax.ShapeDtypeStruct((s, d), F32), jax.ShapeDtypeStruct((s, d), BF16)],
            compiler_params=_params("parallel"))(x, branch, gate, gain, scale, shift)
    h = pl.pallas_call(
        body, name=name, grid=(s // tm,), in_specs=[row, vec, vec, vec], out_specs=row,
        out_shape=jax.ShapeDtypeStruct((s, d), BF16), compiler_params=_params("parallel"))(x, gain, scale, shift)
    return x, h


def _norm_mod_bwd(x, dh, dres, gain, scale, branch, gate, *, name):
    s, d = x.shape
    tm = ROW_TILE
    has_branch = branch is not None

    def body(*refs):
        if has_branch:
            x_ref, dh_ref, dres_ref, gain_ref, sc_ref, br_ref, gate_ref, dx_ref, dbr_ref, sums_ref = refs
        else:
            x_ref, dh_ref, dres_ref, gain_ref, sc_ref, dx_ref, sums_ref = refs
        i = pl.program_id(0)

        @pl.when(i == 0)
        def _():
            sums_ref[...] = jnp.zeros_like(sums_ref)

        xv, dhv = x_ref[...], dh_ref[...]
        r = lax.rsqrt(jnp.mean(xv * xv, axis=-1, keepdims=True) + EPS)
        xn = xv * r
        dxn = dhv * (gain_ref[...] * (1.0 + sc_ref[...]))
        dx = dres_ref[...] + r * (dxn - xn * jnp.mean(dxn * xn, axis=-1, keepdims=True))
        dx_ref[...] = dx
        sums_ref[0] += _fold8(dhv * xn)
        sums_ref[1] += _fold8(dhv)
        if has_branch:
            dbr_ref[...] = (gate_ref[...] * dx).astype(BF16)
            sums_ref[2] += _fold8(dx * br_ref[...])

        @pl.when(i == s // tm - 1)
        def _():
            _spread_total(sums_ref)

    row = pl.BlockSpec((tm, d), lambda i: (i, 0))
    vec = pl.BlockSpec((1, d), lambda i: (0, 0))
    sums = pl.BlockSpec((3, 8, d), lambda i: (0, 0, 0))
    sums_shape = jax.ShapeDtypeStruct((3, 8, d), F32)
    if has_branch:
        return pl.pallas_call(
            body, name=name, grid=(s // tm,), in_specs=[row, row, row, vec, vec, row, vec], out_specs=[row, row, sums],
            out_shape=[jax.ShapeDtypeStruct((s, d), F32), jax.ShapeDtypeStruct((s, d), BF16), sums_shape],
            compiler_params=_params("arbitrary"))(x, dh, dres, gain, scale, branch, gate)
    dx, sm = pl.pallas_call(
        body, name=name, grid=(s // tm,), in_specs=[row, row, row, vec, vec], out_specs=[row, sums],
        out_shape=[jax.ShapeDtypeStruct((s, d), F32), sums_shape],
        compiler_params=_params("arbitrary"))(x, dh, dres, gain, scale)
    return dx, None, sm


GLA_ROWS = 256


def _gla_chunk_setup(lr_ref, wg_ref, bg_ref, rows):
    c = GLA_CHUNK
    ri = lax.broadcasted_iota(jnp.int32, (c, c), 0)
    ci = lax.broadcasted_iota(jnp.int32, (c, c), 1)
    z = _nn(lr_ref[rows, :].astype(BF16), wg_ref[...]) + bg_ref[...]
    g = (jnp.minimum(z, 0.0) - jnp.log(1.0 + jnp.exp(-jnp.abs(z)))) * (1.0 / GLA_GATE_TAU)
    b = _nn((ci <= ri).astype(F32), g, precision=HIGHEST)
    return z, b, ci <= ri


def _last_row(b):
    ri = lax.broadcasted_iota(jnp.int32, b.shape, 0)
    return jnp.sum(jnp.where(ri == b.shape[0] - 1, b, 0.0), axis=0, keepdims=True)


def _gla_fwd(proj, wg, bg, gn, *, name):
    s = proj.shape[0]
    tb, c = GLA_ROWS, GLA_CHUNK
    cb = tb // c

    def body(q_ref, k_ref, v_ref, r_ref, lr_ref, wg_ref, bg_ref, gn_ref, o_ref, y_ref, st_ref, state):
        i = pl.program_id(0)

        @pl.when(i == 0)
        def _():
            state[...] = jnp.zeros_like(state)

        low = lax.broadcasted_iota(jnp.int32, (c, HEAD_LANES), 1) < 64
        for ch in range(cb):
            rows = pl.ds(ch * c, c)
            _, b, causal = _gla_chunk_setup(lr_ref, wg_ref, bg_ref, rows)
            for p in range(2):
                cols = pl.ds(p * HEAD_LANES, HEAD_LANES)
                bp = b[:, p * HEAD_LANES:(p + 1) * HEAD_LANES]
                b_end = _last_row(bp)
                q = q_ref[rows, cols] * 0.125
                k = k_ref[rows, cols]
                q_in = q * jnp.exp(bp)
                k_out = (k * jnp.exp(-bp)).astype(BF16)
                k_end = k * jnp.exp(b_end - bp)
                st = state[p]
                st_ref[ch, p] = st
                st_b = st.astype(BF16)
                upd = jnp.zeros_like(st)
                for e in range(2):
                    msk = low if e == 0 else jnp.logical_not(low)
                    hc = pl.ds((2 * p + e) * HEAD_LANES, HEAD_LANES)
                    qm = jnp.where(msk, q_in, 0.0).astype(BF16)
                    a = jnp.where(causal, _nt(qm, k_out), 0.0)
                    v = v_ref[rows, hc].astype(BF16)
                    o = _nt(qm, st_b) + _nn(a.astype(BF16), v)
                    upd = upd + _tn(v, jnp.where(msk, k_end, 0.0).astype(BF16))
                    o_ref[rows, hc] = o
                    rr = r_ref[rows, hc]
                    on = o * lax.rsqrt(jnp.mean(o * o, axis=-1, keepdims=True) + EPS)
                    y_ref[rows, hc] = (on * gn_ref[...] * (rr * _sigmoid(rr))).astype(BF16)
                state[p] = st * jnp.exp(b_end) + upd

    def col(width, at):
        return pl.BlockSpec((tb, width), lambda i: (i, at // width))

    full = lambda shape: pl.BlockSpec(shape, lambda i: tuple(0 for _ in shape))
    return pl.pallas_call(
        body, name=name, grid=(s // tb,),
        in_specs=[col(256, C_GQ), col(256, C_GK), col(512, C_GV), col(512, C_GR), col(128, C_LR),
                  full((HEAD_LANES, 256)), full((1, 256)), full((1, HEAD_LANES))],
        out_specs=[pl.BlockSpec((tb, 512), lambda i: (i, 0)), pl.BlockSpec((tb, 512), lambda i: (i, 0)),
                   pl.BlockSpec((cb, 2, HEAD_LANES, HEAD_LANES), lambda i: (i, 0, 0, 0))],
        out_shape=[jax.ShapeDtypeStruct((s, 512), F32), jax.ShapeDtypeStruct((s, 512), BF16),
                   jax.ShapeDtypeStruct((s // c, 2, HEAD_LANES, HEAD_LANES), F32)],
        scratch_shapes=[pltpu.VMEM((2, HEAD_LANES, HEAD_LANES), F32)],
        compiler_params=_params("arbitrary"))(proj, proj, proj, proj, proj, wg, bg, gn)


def _gla_bwd(proj, wg, bg, gn, o_raw, states, dmixed, *, name):
    s = proj.shape[0]
    tb, c = GLA_ROWS, GLA_CHUNK
    cb = tb // c
    nblk, nch = s // tb, s // c

    def body(q_ref, k_ref, v_ref, r_ref, lr_ref, wg_ref, bg_ref, gn_ref, o_ref, st_ref, stn_ref, dy_ref,
             dq_ref, dk_ref, dv_ref, dr_ref, dlr_ref, gwg_ref, sums_ref, dstate):
        i = pl.program_id(0)

        @pl.when(i == 0)
        def _():
            dstate[...] = jnp.zeros_like(dstate)
            gwg_ref[...] = jnp.zeros_like(gwg_ref)
            sums_ref[...] = jnp.zeros_like(sums_ref)

        low = lax.broadcasted_iota(jnp.int32, (c, HEAD_LANES), 1) < 64
        for ch in reversed(range(cb)):
            rows = pl.ds(ch * c, c)
            z, b, causal = _gla_chunk_setup(lr_ref, wg_ref, bg_ref, rows)
            upper = jnp.logical_not(causal) | (lax.broadcasted_iota(jnp.int32, (c, c), 0)
                                               == lax.broadcasted_iota(jnp.int32, (c, c), 1))
            lr_b = lr_ref[rows, :].astype(BF16)
            dlr = jnp.zeros((c, HEAD_LANES), F32)
            for p in range(2):
                cols = pl.ds(p * HEAD_LANES, HEAD_LANES)
                sl = slice(p * HEAD_LANES, (p + 1) * HEAD_LANES)
                bp = b[:, sl]
                b_end = _last_row(bp)
                e_in, e_out, e_end = jnp.exp(bp), jnp.exp(-bp), jnp.exp(b_end - bp)
                q = q_ref[rows, cols] * 0.125
                k = k_ref[rows, cols]
                q_in = q * e_in
                k_out = k * e_out
                k_end = k * e_end
                st0 = st_ref[ch, p]
                st1 = st_ref[ch + 1, p] if ch + 1 < cb else stn_ref[0, p]
                dst = dstate[p]
                st0_b, dst_b = st0.astype(BF16), dst.astype(BF16)
                dq_in = jnp.zeros((c, HEAD_LANES), F32)
                dk_out = jnp.zeros((c, HEAD_LANES), F32)
                dk_end = jnp.zeros((c, HEAD_LANES), F32)
                dst_new = dst * jnp.exp(b_end)
                for e in range(2):
                    msk = low if e == 0 else jnp.logical_not(low)
                    hc = pl.ds((2 * p + e) * HEAD_LANES, HEAD_LANES)
                    o = o_ref[rows, hc]
                    rr = r_ref[rows, hc]
                    dy = dy_ref[rows, hc]
                    sg = _sigmoid(rr)
                    rs = lax.rsqrt(jnp.mean(o * o, axis=-1, keepdims=True) + EPS)
                    on = o * rs
                    t = dy * (rr * sg)
                    sums_ref[1, :, hc] += _fold8(t * on)
                    dn = t * gn_ref[...]
                    do = (rs * (dn - on * jnp.mean(dn * on, axis=-1, keepdims=True))).astype(BF16)
                    dr_ref[rows, hc] = (dy * on * gn_ref[...] * (sg * (1.0 + rr * (1.0 - sg)))).astype(BF16)
                    qm = jnp.where(msk, q_in, 0.0).astype(BF16)
                    km_out = jnp.where(msk, k_out, 0.0).astype(BF16)
                    km_end = jnp.where(msk, k_end, 0.0).astype(BF16)
                    v = v_ref[rows, hc].astype(BF16)
                    a = jnp.where(causal, _nt(qm, km_out), 0.0).astype(BF16)
                    da = jnp.where(causal, _nt(do, v), 0.0).astype(BF16)
                    dv_ref[rows, hc] = (_tn(a, do) + _nt(km_end, dst_b)).astype(BF16)
                    dq_in = dq_in + jnp.where(msk, _nn(do, st0_b) + _nn(da, km_out), 0.0)
                    dk_out = dk_out + _tn(da, qm)
                    dk_end = dk_end + jnp.where(msk, _nn(v, dst_b), 0.0)
                    dst_new = dst_new + _tn(do, qm)
                dq = dq_in * e_in
                dk = dk_out * e_out + dk_end * e_end
                dq_ref[rows, cols] = (dq * 0.125).astype(BF16)
                dk_ref[rows, cols] = dk.astype(BF16)
                w = q * dq - k * dk
                dg = _nn(upper.astype(F32), w, precision=HIGHEST) + jnp.sum(dst * st1, axis=0, keepdims=True)
                zp = z[:, sl]
                dz = dg * (1.0 / GLA_GATE_TAU) * _sigmoid(-zp)
                dz_b = dz.astype(BF16)
                sums_ref[0, :, cols] += _fold8(dz)
                dlr = dlr + _nt(dz_b, wg_ref[:, cols])
                gwg_ref[:, cols] += _tn(lr_b, dz_b)
                dstate[p] = dst_new
            dlr_ref[rows, :] = dlr.astype(BF16)

        @pl.when(i == nblk - 1)
        def _():
            _spread_total(sums_ref)

    rev = lambda i: nblk - 1 - i

    def col(width, at):
        return pl.BlockSpec((tb, width), lambda i: (rev(i), at // width))

    full = lambda shape: pl.BlockSpec(shape, lambda i: tuple(0 for _ in shape))
    out_col = lambda width: pl.BlockSpec((tb, width), lambda i: (rev(i), 0))
    return pl.pallas_call(
        body, name=name, grid=(nblk,),
        in_specs=[col(256, C_GQ), col(256, C_GK), col(512, C_GV), col(512, C_GR), col(128, C_LR),
                  full((HEAD_LANES, 256)), full((1, 256)), full((1, HEAD_LANES)),
                  pl.BlockSpec((tb, 512), lambda i: (rev(i), 0)),
                  pl.BlockSpec((cb, 2, HEAD_LANES, HEAD_LANES), lambda i: (rev(i), 0, 0, 0)),
                  pl.BlockSpec((1, 2, HEAD_LANES, HEAD_LANES), lambda i: (jnp.minimum((rev(i) + 1) * cb, nch - 1), 0, 0, 0)),
                  pl.BlockSpec((tb, 512), lambda i: (rev(i), 0))],
        out_specs=[out_col(256), out_col(256), out_col(512), out_col(512), out_col(128),
                   full((HEAD_LANES, 256)), full((2, 8, 512))],
        out_shape=[jax.ShapeDtypeStruct((s, 256), BF16), jax.ShapeDtypeStruct((s, 256), BF16),
                   jax.ShapeDtypeStruct((s, 512), BF16), jax.ShapeDtypeStruct((s, 512), BF16),
                   jax.ShapeDtypeStruct((s, 128), BF16), jax.ShapeDtypeStruct((HEAD_LANES, 256), F32),
                   jax.ShapeDtypeStruct((2, 8, 512), F32)],
        scratch_shapes=[pltpu.VMEM((2, HEAD_LANES, HEAD_LANES), F32)],
        compiler_params=_params("arbitrary"))(proj, proj, proj, proj, proj, wg, bg, gn, o_raw, states, states, dmixed)


def _head_sum_matrix():
    ri = lax.broadcasted_iota(jnp.int32, (512, 512), 0) // 64
    ci = lax.broadcasted_iota(jnp.int32, (512, 512), 1) // 64
    return (ri == ci).astype(F32)


def _attn_prep(proj, qg, kg, *, name):
    s = proj.shape[0]
    tm = ROW_TILE

    def body(q_ref, k_ref, qg_ref, kg_ref, qa_ref, ka_ref):
        hs = _head_sum_matrix()
        q, k = q_ref[...], k_ref[...]
        qr = lax.rsqrt(_nn(q * q, hs, precision=HIGHEST) * (1.0 / 64) + EPS)
        kr = lax.rsqrt(_nn(k * k, hs, precision=HIGHEST) * (1.0 / 64) + EPS)
        qa_ref[...] = q * qr * qg_ref[...] * 0.125
        ka_ref[...] = k * kr * kg_ref[...]

    col = lambda at: pl.BlockSpec((tm, 512), lambda i: (i, at // 512))
    vec = pl.BlockSpec((1, 512), lambda i: (0, 0))
    out = pl.BlockSpec((tm, 512), lambda i: (i, 0))
    return pl.pallas_call(
        body, name=name, grid=(s // tm,), in_specs=[col(C_AQ), col(C_AK), vec, vec], out_specs=[out] * 2,
        out_shape=[jax.ShapeDtypeStruct((s, 512), F32)] * 2, compiler_params=_params("parallel"))(proj, proj, qg, kg)


FAR = 1e30


def _attn_distance(first):
    blk = ATTN_BLOCK
    iq = lax.broadcasted_iota(jnp.int32, (blk, 2 * blk), 0)
    ik = lax.broadcasted_iota(jnp.int32, (blk, 2 * blk), 1)
    rel = iq + blk - ik
    valid = (rel >= 0) & (rel <= blk) & (jnp.logical_not(first) | (ik >= blk))
    return jnp.where(valid, rel.astype(F32), FAR)


def _attn_scores(qm, kcat, slope, dil, dist):
    return _nt(qm, kcat) - (slope * dil) * dist


def _pair_slopes(p):
    if isinstance(p, int):
        return ALIBI_SLOPES[2 * p], ALIBI_SLOPES[2 * p + 1]
    pick = lambda e: jnp.where(p == 0, ALIBI_SLOPES[e], jnp.where(p == 1, ALIBI_SLOPES[2 + e],
                               jnp.where(p == 2, ALIBI_SLOPES[4 + e], ALIBI_SLOPES[6 + e])))
    return pick(0), pick(1)


def _attn_pair_fwd(q2, kcat, vcat, slopes, dil, dist):
    low = lax.broadcasted_iota(jnp.int32, (ATTN_BLOCK, HEAD_LANES), 1) < 64
    outs, lses = [], []
    for e in range(2):
        msk = low if e == 0 else jnp.logical_not(low)
        sc = _attn_scores(jnp.where(msk, q2, 0.0).astype(BF16), kcat, slopes[e], dil, dist)
        m = jnp.max(sc, axis=-1, keepdims=True)
        pr = jnp.exp(sc - m)
        den = jnp.sum(pr, axis=-1, keepdims=True)
        outs.append(_nn(pr.astype(BF16), vcat) / den)
        lses.append(m + jnp.log(den))
    return jnp.where(low, outs[0], outs[1]), jnp.where(low, lses[0], lses[1])


def _attn_pair_bwd(q2, kcat, vcat, do2, y2, lse2, slopes, dil, dist):
    blk = ATTN_BLOCK
    lane = lax.broadcasted_iota(jnp.int32, (blk, HEAD_LANES), 1)
    low = lane < 64
    low_keys = lax.broadcasted_iota(jnp.int32, (2 * blk, HEAD_LANES), 1) < 64
    prod = do2 * y2
    dq = jnp.zeros((blk, HEAD_LANES), F32)
    dk = jnp.zeros((2 * blk, HEAD_LANES), F32)
    dv = jnp.zeros((2 * blk, HEAD_LANES), F32)
    for e in range(2):
        msk = low if e == 0 else jnp.logical_not(low)
        msk_keys = low_keys if e == 0 else jnp.logical_not(low_keys)
        qm = jnp.where(msk, q2, 0.0).astype(BF16)
        sc = _attn_scores(qm, kcat, slopes[e], dil, dist)
        lse_e = jnp.sum(jnp.where(lane == 64 * e, lse2, 0.0), axis=-1, keepdims=True)
        delta = jnp.sum(jnp.where(msk, prod, 0.0), axis=-1, keepdims=True)
        pr = jnp.exp(sc - lse_e)
        dom = jnp.where(msk, do2, 0.0).astype(BF16)
        ds = (pr * (_nt(dom, vcat) - delta)).astype(BF16)
        dq = dq + _nn(ds, jnp.where(msk_keys, kcat, 0.0).astype(BF16))
        dk = dk + _tn(ds, qm)
        dv = dv + _tn(pr.astype(BF16), dom)
    return dq, dk, dv


def _attn_specs(dil):
    rows = ATTN_BLOCK * dil
    if dil == 1:
        cur = lambda at: pl.BlockSpec((rows, 512), lambda n: (n, at // 512))
        prev = lambda at: pl.BlockSpec((rows, 512), lambda n: (jnp.maximum(n - 1, 0), at // 512))
    else:
        cur = lambda at: pl.BlockSpec((rows, HEAD_LANES), lambda n, p: (n, at // HEAD_LANES + p))
        prev = lambda at: pl.BlockSpec((rows, HEAD_LANES), lambda n, p: (jnp.maximum(n - 1, 0), at // HEAD_LANES + p))
    return cur, prev


def _attn_loop(dil, one_pair):
    if dil == 1:
        for p in range(4):
            one_pair(slice(None), pl.ds(p * HEAD_LANES, HEAD_LANES), p)
    else:
        p = pl.program_id(1)

        def step(r, carry):
            one_pair(pl.ds(r, ATTN_BLOCK, stride=dil), slice(None), p)
            return carry

        lax.fori_loop(0, dil, step, 0, unroll=min(dil, 4))


def _dil_attn_fwd(qa, ka, proj, dil, *, name):
    s = qa.shape[0]

    def body(q_ref, kp_ref, kc_ref, vp_ref, vc_ref, o_ref, lse_ref):
        dist = _attn_distance(pl.program_id(0) == 0)

        def one_pair(rows, cols, p):
            kcat = jnp.concatenate([kp_ref[rows, cols], kc_ref[rows, cols]], axis=0).astype(BF16)
            vcat = jnp.concatenate([vp_ref[rows, cols], vc_ref[rows, cols]], axis=0).astype(BF16)
            o2, lse2 = _attn_pair_fwd(q_ref[rows, cols], kcat, vcat, _pair_slopes(p), dil, dist)
            o_ref[rows, cols] = o2
            lse_ref[rows, cols] = lse2

        _attn_loop(dil, one_pair)

    cur, prev = _attn_specs(dil)
    grid = (s // ATTN_BLOCK,) if dil == 1 else (s // (ATTN_BLOCK * dil), 4)
    return pl.pallas_call(
        body, name=name, grid=grid, in_specs=[cur(0), prev(0), cur(0), prev(C_AV), cur(C_AV)], out_specs=[cur(0), cur(0)],
        out_shape=[jax.ShapeDtypeStruct((s, 512), F32)] * 2,
        compiler_params=_params(*["parallel"] * len(grid)))(qa, ka, ka, proj, proj)


def _attn_merge(branches, y_gla, *, name):
    s = y_gla.shape[0]
    tm = ROW_TILE

    def body(o0, l0, o1, l1, o2, l2, yg_ref, mixed_ref, y_ref, lse_ref):
        m = jnp.maximum(jnp.maximum(l0[...], l1[...]), l2[...])
        w0, w1, w2 = jnp.exp(l0[...] - m), jnp.exp(l1[...] - m), jnp.exp(l2[...] - m)
        zs = w0 + w1 + w2
        y = (w0 * o0[...] + w1 * o1[...] + w2 * o2[...]) / zs
        y_ref[...] = y
        lse_ref[...] = m + jnp.log(zs)
        mixed_ref[:, 0:512] = yg_ref[...]
        mixed_ref[:, 512:1024] = y.astype(BF16)

    blk = pl.BlockSpec((tm, 512), lambda i: (i, 0))
    args = [t for pair in branches for t in pair]
    return pl.pallas_call(
        body, name=name, grid=(s // tm,), in_specs=[blk] * 7,
        out_specs=[pl.BlockSpec((tm, 1024), lambda i: (i, 0)), blk, blk],
        out_shape=[jax.ShapeDtypeStruct((s, 1024), BF16), jax.ShapeDtypeStruct((s, 512), F32),
                   jax.ShapeDtypeStruct((s, 512), F32)],
        compiler_params=_params("parallel"))(*args, y_gla)


def _dil_attn_bwd(qa, ka, proj, y_att, lse, dmixed, dil, *, name):
    s = qa.shape[0]
    blk = ATTN_BLOCK

    def body(q_ref, kp_ref, kc_ref, vp_ref, vc_ref, y_ref, lse_ref, do_ref, dq_ref, dkc_ref, dkp_ref, dvc_ref, dvp_ref):
        dist = _attn_distance(pl.program_id(0) == 0)

        def one_pair(rows, cols, p):
            kcat = jnp.concatenate([kp_ref[rows, cols], kc_ref[rows, cols]], axis=0).astype(BF16)
            vcat = jnp.concatenate([vp_ref[rows, cols], vc_ref[rows, cols]], axis=0).astype(BF16)
            dq, dk, dv = _attn_pair_bwd(q_ref[rows, cols], kcat, vcat, do_ref[rows, cols], y_ref[rows, cols],
                                        lse_ref[rows, cols], _pair_slopes(p), dil, dist)
            dq_ref[rows, cols] = dq
            dkp_ref[rows, cols] = dk[0:blk]
            dkc_ref[rows, cols] = dk[blk:2 * blk]
            dvp_ref[rows, cols] = dv[0:blk]
            dvc_ref[rows, cols] = dv[blk:2 * blk]

        _attn_loop(dil, one_pair)

    cur, prev = _attn_specs(dil)
    grid = (s // blk,) if dil == 1 else (s // (blk * dil), 4)
    return pl.pallas_call(
        body, name=name, grid=grid,
        in_specs=[cur(0), prev(0), cur(0), prev(C_AV), cur(C_AV), cur(0), cur(0), cur(512)], out_specs=[cur(0)] * 5,
        out_shape=[jax.ShapeDtypeStruct((s, 512), F32)] * 5, compiler_params=_params(*["parallel"] * len(grid)),
    )(qa, ka, ka, proj, proj, y_att, lse, dmixed)


def _attn_post(parts, proj, qg, kg, *, name):
    s = proj.shape[0]
    tm = ATTN_BLOCK
    nblk = s // tm

    def body(*refs):
        ins, (q_ref, k_ref, qg_ref, kg_ref, dq_out, dk_out, dv_out, sums_ref) = refs[:15], refs[15:]
        i = pl.program_id(0)

        @pl.when(i == 0)
        def _():
            sums_ref[...] = jnp.zeros_like(sums_ref)

        dq = jnp.zeros((tm, 512), F32)
        dk = jnp.zeros((tm, 512), F32)
        dv = jnp.zeros((tm, 512), F32)
        for g, dil in enumerate(DILATIONS):
            dq_r, dkc_r, dkp_r, dvc_r, dvp_r = ins[5 * g:5 * g + 5]
            inside = (i + dil < nblk).astype(F32)
            dq = dq + dq_r[...]
            dk = dk + dkc_r[...] + inside * dkp_r[...]
            dv = dv + dvc_r[...] + inside * dvp_r[...]
        dv_out[...] = dv.astype(BF16)
        hs = _head_sum_matrix()
        for row, (x_ref, g_ref, dy, out, post) in enumerate(((q_ref, qg_ref, dq, dq_out, 0.125), (k_ref, kg_ref, dk, dk_out, 1.0))):
            x = x_ref[...]
            rs = lax.rsqrt(_nn(x * x, hs, precision=HIGHEST) * (1.0 / 64) + EPS)
            xn = x * rs
            dy = dy * post
            sums_ref[row] += _fold8(dy * xn)
            dn = dy * g_ref[...]
            out[...] = (rs * (dn - xn * (_nn(dn * xn, hs, precision=HIGHEST) * (1.0 / 64)))).astype(BF16)

        @pl.when(i == nblk - 1)
        def _():
            _spread_total(sums_ref)

    here = pl.BlockSpec((tm, 512), lambda i: (i, 0))
    specs = []
    for dil in DILATIONS:
        later = pl.BlockSpec((tm, 512), lambda i, dil=dil: (jnp.minimum(i + dil, nblk - 1), 0))
        specs += [here, here, later, here, later]
    col = lambda at: pl.BlockSpec((tm, 512), lambda i: (i, at // 512))
    vec = pl.BlockSpec((1, 512), lambda i: (0, 0))
    return pl.pallas_call(
        body, name=name, grid=(nblk,), in_specs=specs + [col(C_AQ), col(C_AK), vec, vec],
        out_specs=[here, here, here, pl.BlockSpec((2, 8, 512), lambda i: (0, 0, 0))],
        out_shape=[jax.ShapeDtypeStruct((s, 512), BF16)] * 3 + [jax.ShapeDtypeStruct((2, 8, 512), F32)],
        compiler_params=_params("arbitrary"))(*[t for part in parts for t in part], proj, proj, qg, kg)


FFN_TM, FFN_TN = 256, 1408
HALO = 16


def _conv3(u_ref, halo_ref, w_ref, b_ref, first):
    u = u_ref[...].astype(F32)
    ext = jnp.concatenate([jnp.where(first, 0.0, halo_ref[...].astype(F32)), u], axis=0)
    u1 = pltpu.roll(ext, 1, 0)[HALO:]
    u2 = pltpu.roll(ext, 2, 0)[HALO:]
    return b_ref[...] + w_ref[0:1, :] * u2 + w_ref[1:2, :] * u1 + w_ref[2:3, :] * u, u, u1, u2


def _ffn_specs(tm, tn):
    nj = D_FF // tn
    blk = lambda half: pl.BlockSpec((tm, tn), lambda j, i: (i, j + half * nj))
    halo = lambda half: pl.BlockSpec((HALO, tn), lambda j, i: (jnp.maximum(i * (tm // HALO) - 1, 0), j + half * nj))
    wspec = lambda half: pl.BlockSpec((3, tn), lambda j, i: (0, j + half * nj))
    bspec = lambda half: pl.BlockSpec((1, tn), lambda j, i: (0, j + half * nj))
    return [blk(0), halo(0), blk(1), halo(1), wspec(0), wspec(1), bspec(0), bspec(1)]


def _conv_swiglu_fwd(u, conv_w, conv_b, *, name):
    s = u.shape[0]
    tm, tn = FFN_TM, FFN_TN

    def body(ug_ref, hg_ref, uv_ref, hv_ref, wg_ref, wv_ref, bg_ref, bv_ref, act_ref):
        first = pl.program_id(1) == 0
        cg = _conv3(ug_ref, hg_ref, wg_ref, bg_ref, first)[0]
        cv = _conv3(uv_ref, hv_ref, wv_ref, bv_ref, first)[0]
        act_ref[...] = (cg * _sigmoid(cg) * cv).astype(BF16)

    return pl.pallas_call(
        body, name=name, grid=(D_FF // tn, s // tm), in_specs=_ffn_specs(tm, tn),
        out_specs=pl.BlockSpec((tm, tn), lambda j, i: (i, j)), out_shape=jax.ShapeDtypeStruct((s, D_FF), BF16),
        compiler_params=_params("parallel", "parallel"))(u, u, u, u, conv_w, conv_w, conv_b, conv_b)


def _conv_swiglu_bwd_pre(u, conv_w, conv_b, dact, *, name):
    s = u.shape[0]
    tm, tn = FFN_TM, FFN_TN

    def body(ug_ref, hg_ref, uv_ref, hv_ref, wg_ref, wv_ref, bg_ref, bv_ref, da_ref, duc_ref, sums_ref):
        i = pl.program_id(1)

        @pl.when(i == 0)
        def _():
            sums_ref[...] = jnp.zeros_like(sums_ref)

        cg, g0, g1, g2 = _conv3(ug_ref, hg_ref, wg_ref, bg_ref, i == 0)
        cv, v0, v1, v2 = _conv3(uv_ref, hv_ref, wv_ref, bv_ref, i == 0)
        da = da_ref[...].astype(F32)
        sg = _sigmoid(cg)
        dg = da * cv * (sg * (1.0 + cg * (1.0 - sg)))
        dv = da * (cg * sg)
        duc_ref[0] = dg.astype(BF16)
        duc_ref[1] = dv.astype(BF16)
        for half, (d, taps) in enumerate(((dg, (g2, g1, g0)), (dv, (v2, v1, v0)))):
            for t, tap in enumerate(taps):
                sums_ref[half, t] += _fold8(d * tap)
            sums_ref[half, 3] += _fold8(d)

        @pl.when(i == s // tm - 1)
        def _():
            _spread_total(sums_ref)

    return pl.pallas_call(
        body, name=name, grid=(D_FF // tn, s // tm),
        in_specs=_ffn_specs(tm, tn) + [pl.BlockSpec((tm, tn), lambda j, i: (i, j))],
        out_specs=[pl.BlockSpec((2, tm, tn), lambda j, i: (0, i, j)), pl.BlockSpec((2, 4, 8, tn), lambda j, i: (0, 0, 0, j))],
        out_shape=[jax.ShapeDtypeStruct((2, s, D_FF), BF16), jax.ShapeDtypeStruct((2, 4, 8, D_FF), F32)],
        compiler_params=_params("parallel", "arbitrary"))(u, u, u, u, conv_w, conv_w, conv_b, conv_b, dact)


def _conv_bwd(duc, conv_w, *, name):
    _, s, _ = duc.shape
    tm, tn = FFN_TM, FFN_TN
    nj, ni = D_FF // tn, s // tm

    def body(d_ref, halo_ref, w_ref, du_ref):
        last = pl.program_id(2) == ni - 1
        d = d_ref[0].astype(F32)
        ext = jnp.concatenate([d, jnp.where(last, 0.0, halo_ref[0].astype(F32))], axis=0)
        n = tm + HALO
        d1 = pltpu.roll(ext, n - 1, 0)[:tm]
        d2 = pltpu.roll(ext, n - 2, 0)[:tm]
        du_ref[...] = (w_ref[2:3, :] * d + w_ref[1:2, :] * d1 + w_ref[0:1, :] * d2).astype(BF16)

    return pl.pallas_call(
        body, name=name, grid=(2, nj, ni),
        in_specs=[pl.BlockSpec((1, tm, tn), lambda g, j, i: (g, i, j)),
                  pl.BlockSpec((1, HALO, tn), lambda g, j, i: (g, jnp.minimum((i + 1) * (tm // HALO), s // HALO - 1), j)),
                  pl.BlockSpec((3, tn), lambda g, j, i: (0, g * nj + j))],
        out_specs=pl.BlockSpec((tm, tn), lambda g, j, i: (i, g * nj + j)),
        out_shape=jax.ShapeDtypeStruct((s, 2 * D_FF), BF16),
        compiler_params=_params("parallel", "parallel", "parallel"))(duc, duc, conv_w)


def _loss_head(x1, ffn, gate, target, *, name):
    s, d = x1.shape
    tm = ROW_TILE

    def body(x_ref, f_ref, g_ref, t_ref, dy_ref, df_ref, sums_ref):
        i = pl.program_id(0)

        @pl.when(i == 0)
        def _():
            sums_ref[...] = jnp.zeros_like(sums_ref)

        f = f_ref[...]
        err = x_ref[...] + g_ref[...] * f - t_ref[...]
        dy = err * (1.0 / d)
        dy_ref[...] = dy
        df_ref[...] = (g_ref[...] * dy).astype(BF16)
        sums_ref[0] += _fold8(dy * f)
        sums_ref[1] += _fold8(err * err)

        @pl.when(i == s // tm - 1)
        def _():
            _spread_total(sums_ref)

    row = pl.BlockSpec((tm, d), lambda i: (i, 0))
    return pl.pallas_call(
        body, name=name, grid=(s // tm,), in_specs=[row, row, pl.BlockSpec((1, d), lambda i: (0, 0)), row],
        out_specs=[row, row, pl.BlockSpec((2, 8, d), lambda i: (0, 0, 0))],
        out_shape=[jax.ShapeDtypeStruct((s, d), F32), jax.ShapeDtypeStruct((s, d), BF16), jax.ShapeDtypeStruct((2, 8, d), F32)],
        compiler_params=_params("arbitrary"))(x1, ffn, gate, target)


def _adamw(w, g, m, v, *, name):
    rows, cols = w.shape
    tm = next((t for t in range(ROW_TILE, 7, -8) if rows % t == 0), rows)

    def body(w_ref, g_ref, m_ref, v_ref, d_ref, mo_ref, vo_ref):
        gv = g_ref[...]
        mn = ADAM_B1 * m_ref[...] + (1.0 - ADAM_B1) * gv
        vn = ADAM_B2 * v_ref[...] + (1.0 - ADAM_B2) * (gv * gv)
        m_hat = mn / (1.0 - ADAM_B1 ** ADAM_STEP)
        v_hat = vn / (1.0 - ADAM_B2 ** ADAM_STEP)
        d_ref[...] = -ADAM_LR * (m_hat / (jnp.sqrt(v_hat) + ADAM_EPS) + ADAM_WD * w_ref[...])
        mo_ref[...] = mn
        vo_ref[...] = vn

    blk = pl.BlockSpec((tm, cols), lambda i: (i, 0))
    return pl.pallas_call(
        body, name=name, grid=(rows // tm,), in_specs=[blk] * 4, out_specs=[blk] * 3,
        out_shape=[jax.ShapeDtypeStruct((rows, cols), F32)] * 3, compiler_params=_params("parallel"))(w, g, m, v)


def _colsum(t):
    return t[..., 0, :]


def _in_proj_layout(w_in):
    pad = jnp.zeros((w_in.shape[0], PROJ_W - C_LR - GLA_GATE_RANK), w_in.dtype)
    return jnp.concatenate([w_in[:, :1536], w_in[:, 1552:], w_in[:, 1536:1552], pad], axis=1)


def _in_proj_grad_layout(g):
    return jnp.concatenate([g[:, :1536], g[:, C_LR:C_LR + GLA_GATE_RANK], g[:, 1536:C_LR]], axis=1)


def _gate_layout(gla_w_gate):
    return jnp.pad(gla_w_gate, ((0, HEAD_LANES - GLA_GATE_RANK), (0, 0))).astype(BF16)


def _local_step(x, target, mod, wi, wo, wup, wdown, conv_w, conv_b, wg, bg, gn, qg, kg, n1g, n2g):
    d = D_MODEL
    sh1, sc1, g1, sh2, sc2, g2 = [mod[:, i * d:(i + 1) * d] for i in range(6)]
    qg8, kg8 = jnp.tile(qg, (1, 8)), jnp.tile(kg, (1, 8))

    _, h1 = _norm_mod_fwd(x, None, None, n1g, sc1, sh1, name="norm1_fwd")
    proj = _mm(h1, wi, tm=1024, tn=PROJ_W, tk=d, name="in_proj")
    o_raw, y_gla, states = _gla_fwd(proj, wg, bg, gn, name="gla_fwd")
    qa, ka = _attn_prep(proj, qg8, kg8, name="attn_prep")
    branches = [_dil_attn_fwd(qa, ka, proj, dil, name=f"attn_fwd_d{dil}") for dil in DILATIONS]
    mixed, y_att, lse = _attn_merge(branches, y_gla, name="attn_merge")
    attn_out = _mm(mixed, wo, tm=1024, tn=d, tk=d, name="out_proj")
    x1, h2 = _norm_mod_fwd(x, attn_out, g1, n2g, sc2, sh2, name="norm2_fwd")
    u = _mm(h2, wup, out_dtype=BF16, tm=1024, tn=D_FF, tk=d, name="up_proj")
    act = _conv_swiglu_fwd(u, conv_w, conv_b, name="conv_swiglu_fwd")
    ffn = _mm(act, wdown, tm=1024, tn=d, tk=D_FF, name="down_proj")
    dy, dffn, head_sums = _loss_head(x1, ffn, g2, target, name="loss_head")

    dact = _mm(dffn, wdown, tb=True, out_dtype=BF16, tm=1024, tn=D_FF, tk=d, name="down_proj_dx")
    g_wdown = _mm(act, dffn, ta=True, tm=1408, tn=d, tk=1024, name="down_proj_dw")
    duc, conv_sums = _conv_swiglu_bwd_pre(u, conv_w, conv_b, dact, name="conv_swiglu_bwd")
    du = _conv_bwd(duc, conv_w, name="conv_bwd")
    dh2 = _mm(du, wup, tb=True, tm=1024, tn=d, tk=1408, name="up_proj_dx")
    g_wup = _mm(h2, du, ta=True, tm=d, tn=1408, tk=1024, shard_cols=True, name="up_proj_dw")
    dx1, dao, n2_sums = _norm_mod_bwd(x1, dh2, dy, n2g, sc2, attn_out, g1, name="norm2_bwd")

    dmixed = _mm(dao, wo, tb=True, tm=1024, tn=d, tk=d, name="out_proj_dx")
    g_wo = _mm(mixed, dao, ta=True, tm=d, tn=d, tk=1024, name="out_proj_dw")
    dgq, dgk, dgv, dgr, dlr, g_wg, gla_sums = _gla_bwd(proj, wg, bg, gn, o_raw, states, dmixed, name="gla_bwd")
    parts = [_dil_attn_bwd(qa, ka, proj, y_att, lse, dmixed, dil, name=f"attn_bwd_d{dil}") for dil in DILATIONS]
    daq, dak, dav, qk_sums = _attn_post(parts, proj, qg8, kg8, name="attn_post")
    dproj = jnp.concatenate([dgq, dgk, dgv, dgr, daq, dak, dav, dlr], axis=1)
    dh1 = _mm(dproj, wi, tb=True, tm=1024, tn=d, tk=PROJ_W, name="in_proj_dx")
    g_wi = _mm(h1, dproj, ta=True, tm=512, tn=PROJ_W, tk=512, name="in_proj_dw")
    grad_x, _, n1_sums = _norm_mod_bwd(x, dh1, dx1, n1g, sc1, None, None, name="norm1_bwd")

    n1, n2, hs, cs = _colsum(n1_sums), _colsum(n2_sums), _colsum(head_sums), _colsum(conv_sums)
    gs, qs = _colsum(gla_sums), _colsum(qk_sums)
    dmod = jnp.concatenate([n1[1], n1[0] * n1g[0], n2[2], n2[1], n2[0] * n2g[0], hs[0]])
    small = dict(
        dmod=dmod,
        norm1_g=n1[0] * (1.0 + sc1[0]), norm2_g=n2[0] * (1.0 + sc2[0]),
        gla_w_gate=g_wg[:GLA_GATE_RANK], gla_b_gate=gs[0, :256], gla_norm_g=gs[1].reshape(4, 128).sum(axis=0),
        q_norm_g=qs[0].reshape(8, 64).sum(axis=0), k_norm_g=qs[1].reshape(8, 64).sum(axis=0),
        conv_w=jnp.concatenate([cs[0, :3], cs[1, :3]], axis=1), conv_b=jnp.concatenate([cs[0, 3], cs[1, 3]]),
    )
    return head_sums[1], grad_x, (g_wi, g_wo, g_wup, g_wdown), small


N_DEV, N_CHIP = 8, 4
ANY = pl.BlockSpec(memory_space=pl.ANY)
VMEM_SPEC = pl.BlockSpec(memory_space=pltpu.VMEM)


def _place():
    x, y, c = lax.axis_index("x"), lax.axis_index("y"), lax.axis_index("c")
    other_chips = [(1 - x, y), (x, 1 - y), (1 - x, 1 - y)]
    return x, y, c, (x, y, 1 - c), other_chips


def _all_gather_small(v, *, name):
    m, n = v.shape

    def body(v_ref, out_ref, send_sems, recv_sems, local_sem):
        x, y, c, sibling, chips = _place()
        me = (x, y, c)

        def rows(px, py, pc):
            return out_ref.at[pl.ds((4 * px + 2 * py + pc) * m, m), :]

        def copy(k, block, to, src=None):
            return pltpu.make_async_remote_copy(
                src_ref=rows(*block) if src is None else src, dst_ref=rows(*block), send_sem=send_sems.at[k],
                recv_sem=recv_sems.at[k], device_id=to, device_id_type=MESH)

        mine = pltpu.make_async_copy(v_ref, rows(*me), local_sem)
        mine.start()
        first = [copy(0, me, sibling, src=v_ref)]
        first += [copy(1 + j, me, (*chip, c), src=v_ref) for j, chip in enumerate(chips)]
        for cp in first:
            cp.start()
        passed = [copy(4 + j, (*chip, c), sibling) for j, chip in enumerate(chips)]
        for j, chip in enumerate(chips):
            copy(1 + j, (*chip, c), me).wait_recv()
            passed[j].start()
        copy(0, sibling, me).wait_recv()
        for j, chip in enumerate(chips):
            copy(4 + j, (*chip, 1 - c), me).wait_recv()
        for cp in first + passed:
            cp.wait_send()
        mine.wait()

    return pl.pallas_call(
        body, name=name, out_shape=jax.ShapeDtypeStruct((N_DEV * m, n), v.dtype), in_specs=[VMEM_SPEC], out_specs=VMEM_SPEC,
        scratch_shapes=[pltpu.SemaphoreType.DMA((7,)), pltpu.SemaphoreType.DMA((7,)), pltpu.SemaphoreType.DMA],
    )(v)


def _gather_weight_shards(shards, *, name):
    nw = len(shards)

    def body(*refs):
        srcs, outs, (send_sems, recv_sems, local_sems) = refs[:nw], refs[nw:2 * nw], refs[2 * nw:]
        x, y, c, sibling, chips = _place()
        local, first, passed = [], [], []
        for w, (src_ref, out_ref) in enumerate(zip(srcs, outs)):
            half = src_ref.shape[0] // 2

            def half_of(chip, core, out_ref=out_ref, half=half):
                return out_ref.at[2 * chip[0] + chip[1], pl.ds(core * half, half), :]

            def copy(k, dst, to, src, w=w):
                return pltpu.make_async_remote_copy(src_ref=src, dst_ref=dst, send_sem=send_sems.at[6 * w + k],
                                                    recv_sem=recv_sems.at[6 * w + k], device_id=to, device_id_type=MESH)

            my_half = src_ref.at[pl.ds(c * half, half), :]
            local.append(pltpu.make_async_copy(src_ref, out_ref.at[2 * x + y], local_sems.at[w]))
            local[-1].start()
            for k, chip in enumerate(chips):
                first.append(copy(k, half_of((x, y), c), (*chip, c), my_half))
                first[-1].start()
        for w, (src_ref, out_ref) in enumerate(zip(srcs, outs)):
            half = src_ref.shape[0] // 2

            def half_of(chip, core, out_ref=out_ref, half=half):
                return out_ref.at[2 * chip[0] + chip[1], pl.ds(core * half, half), :]

            def copy(k, dst, to, src, w=w):
                return pltpu.make_async_remote_copy(src_ref=src, dst_ref=dst, send_sem=send_sems.at[6 * w + k],
                                                    recv_sem=recv_sems.at[6 * w + k], device_id=to, device_id_type=MESH)

            for k, chip in enumerate(chips):
                copy(k, half_of(chip, c), (*chip, c), half_of(chip, c)).wait_recv()
                passed.append(copy(3 + k, half_of(chip, c), sibling, half_of(chip, c)))
                passed[-1].start()
        for w, (src_ref, out_ref) in enumerate(zip(srcs, outs)):
            half = src_ref.shape[0] // 2
            for k, chip in enumerate(chips):
                got = out_ref.at[2 * chip[0] + chip[1], pl.ds((1 - c) * half, half), :]
                pltpu.make_async_remote_copy(src_ref=got, dst_ref=got, send_sem=send_sems.at[6 * w + 3 + k],
                                             recv_sem=recv_sems.at[6 * w + 3 + k], device_id=sibling,
                                             device_id_type=MESH).wait_recv()
        for cp in first + passed:
            cp.wait_send()
        for cp in local:
            cp.wait()

    return pl.pallas_call(
        body, name=name, out_shape=[jax.ShapeDtypeStruct((N_CHIP, *s.shape), s.dtype) for s in shards],
        in_specs=[ANY] * nw, out_specs=[ANY] * nw,
        scratch_shapes=[pltpu.SemaphoreType.DMA((6 * nw,)), pltpu.SemaphoreType.DMA((6 * nw,)), pltpu.SemaphoreType.DMA((nw,))],
    )(*shards)


def _pair_exchange_halves(grads, *, name):
    nw = len(grads)

    def body(*refs):
        srcs, outs, (send_sems, recv_sems) = refs[:nw], refs[nw:2 * nw], refs[2 * nw:]
        _, _, c, sibling, _ = _place()
        cps = []
        for w, (src_ref, out_ref) in enumerate(zip(srcs, outs)):
            half = src_ref.shape[1] // 2
            cps.append(pltpu.make_async_remote_copy(
                src_ref=src_ref.at[:, pl.ds((1 - c) * half, half), :], dst_ref=out_ref, send_sem=send_sems.at[w],
                recv_sem=recv_sems.at[w], device_id=sibling, device_id_type=MESH))
            cps[-1].start()
        for cp in cps:
            cp.wait()

    return pl.pallas_call(
        body, name=name, out_shape=[jax.ShapeDtypeStruct((N_CHIP, g.shape[1] // 2, g.shape[2]), g.dtype) for g in grads],
        in_specs=[ANY] * nw, out_specs=[ANY] * nw,
        scratch_shapes=[pltpu.SemaphoreType.DMA((nw,)), pltpu.SemaphoreType.DMA((nw,))])(*grads)


def _chip_scatter(pairs, *, name):
    nw = len(pairs)

    def body(*refs):
        srcs, outs, (send_sems, recv_sems) = refs[:nw], refs[nw:2 * nw], refs[2 * nw:]
        _, _, c, _, chips = _place()
        cps = []
        for w, (p_ref, out_ref) in enumerate(zip(srcs, outs)):
            for k, chip in enumerate(chips):
                cps.append(pltpu.make_async_remote_copy(
                    src_ref=p_ref.at[2 * chip[0] + chip[1]], dst_ref=out_ref.at[k], send_sem=send_sems.at[3 * w + k],
                    recv_sem=recv_sems.at[3 * w + k], device_id=(*chip, c), device_id_type=MESH))
                cps[-1].start()
        for cp in cps:
            cp.wait()

    return pl.pallas_call(
        body, name=name, out_shape=[jax.ShapeDtypeStruct((3, *p.shape[1:]), p.dtype) for p in pairs],
        in_specs=[ANY] * nw, out_specs=[ANY] * nw,
        scratch_shapes=[pltpu.SemaphoreType.DMA((3 * nw,)), pltpu.SemaphoreType.DMA((3 * nw,))])(*pairs)


def _share_halves(halves, *, name):
    nw = len(halves)

    def body(*refs):
        srcs, outs, (send_sems, recv_sems, local_sems) = refs[:nw], refs[nw:2 * nw], refs[2 * nw:]
        _, _, c, sibling, _ = _place()
        cps, local = [], []
        for w, (src_ref, out_ref) in enumerate(zip(srcs, outs)):
            half = src_ref.shape[0]
            mine = out_ref.at[pl.ds(c * half, half), :]
            local.append(pltpu.make_async_copy(src_ref, mine, local_sems.at[w]))
            local[-1].start()
            cps.append(pltpu.make_async_remote_copy(src_ref=src_ref, dst_ref=mine, send_sem=send_sems.at[w],
                                                    recv_sem=recv_sems.at[w], device_id=sibling, device_id_type=MESH))
            cps[-1].start()
        for w, (src_ref, out_ref) in enumerate(zip(srcs, outs)):
            half = src_ref.shape[0]
            cps[w].wait_send()
            theirs = out_ref.at[pl.ds((1 - c) * half, half), :]
            pltpu.make_async_remote_copy(src_ref=src_ref, dst_ref=theirs, send_sem=send_sems.at[w], recv_sem=recv_sems.at[w],
                                         device_id=sibling, device_id_type=MESH).wait_recv()
            local[w].wait()

    return pl.pallas_call(
        body, name=name, out_shape=[jax.ShapeDtypeStruct((2 * h.shape[0], h.shape[1]), h.dtype) for h in halves],
        in_specs=[ANY] * nw, out_specs=[ANY] * nw,
        scratch_shapes=[pltpu.SemaphoreType.DMA((nw,)), pltpu.SemaphoreType.DMA((nw,)), pltpu.SemaphoreType.DMA((nw,))])(*halves)


def _row_tile(rows, limit=256):
    return next(t for t in range(limit, 15, -16) if rows % t == 0)


def _pair_add(grad, got, core, *, name):
    _, r, n = grad.shape
    half = r // 2
    tr = _row_tile(half)
    nb = half // tr

    def body(core_ref, g_ref, t_ref, f_ref, b_ref):
        acc = g_ref[...] + t_ref[...]
        f_ref[...] = acc
        b_ref[...] = acc.astype(BF16)

    blk = pl.BlockSpec((1, tr, n), lambda j, i, core_ref: (j, i, 0))
    mine = pl.BlockSpec((1, tr, n), lambda j, i, core_ref: (j, core_ref[0] * nb + i, 0))
    return pl.pallas_call(
        body, name=name,
        grid_spec=pltpu.PrefetchScalarGridSpec(num_scalar_prefetch=1, grid=(N_CHIP, nb), in_specs=[mine, blk], out_specs=[blk, blk]),
        out_shape=[jax.ShapeDtypeStruct((N_CHIP, half, n), F32), jax.ShapeDtypeStruct((N_CHIP, half, n), BF16)],
        compiler_params=_params("parallel", "parallel"))(core, grad, got)


def _chip_add(pair, theirs, chip, *, name):
    _, h, n = pair.shape
    tr = _row_tile(h)

    def body(chip_ref, p_ref, t_ref, o_ref):
        o_ref[...] = ((p_ref[0] + t_ref[0].astype(F32)) + t_ref[1].astype(F32)) + t_ref[2].astype(F32)

    return pl.pallas_call(
        body, name=name,
        grid_spec=pltpu.PrefetchScalarGridSpec(
            num_scalar_prefetch=1, grid=(h // tr,),
            in_specs=[pl.BlockSpec((1, tr, n), lambda i, chip_ref: (chip_ref[0], i, 0)),
                      pl.BlockSpec((3, tr, n), lambda i, chip_ref: (0, i, 0))],
            out_specs=pl.BlockSpec((tr, n), lambda i, chip_ref: (i, 0))),
        out_shape=jax.ShapeDtypeStruct((h, n), F32), compiler_params=_params("parallel"))(chip, pair, theirs)


def _sum_devices(gathered, *, name):
    _, m, n = gathered.shape

    def body(g_ref, tot_ref, loss_ref):
        tot = g_ref[0]
        for dev in range(1, N_DEV):
            tot = tot + g_ref[dev]
        tot_ref[...] = tot
        loss_ref[...] = jnp.full((8, n), (0.5 / D_MODEL) * jnp.sum(tot[0:8]), F32)

    return pl.pallas_call(body, name=name, in_specs=[VMEM_SPEC], out_specs=[VMEM_SPEC, VMEM_SPEC],
                          out_shape=[jax.ShapeDtypeStruct((m, n), F32), jax.ShapeDtypeStruct((8, n), F32)])(gathered)


def _ada_mod(cond_all, w_ada_shard, *, name):
    tn = 512

    def body(a_ref, b_ref, o_ref):
        o_ref[...] = _nn(a_ref[...], b_ref[...], precision=HIGHEST)

    return pl.pallas_call(
        body, name=name, grid=(w_ada_shard.shape[1] // tn,),
        in_specs=[pl.BlockSpec(cond_all.shape, lambda j: (0, 0)), pl.BlockSpec((D_MODEL, tn), lambda j: (0, j))],
        out_specs=pl.BlockSpec((N_DEV, tn), lambda j: (0, j)),
        out_shape=jax.ShapeDtypeStruct((N_DEV, w_ada_shard.shape[1]), F32), compiler_params=_params("parallel"))(cond_all, w_ada_shard)


def _ada_grad(cond_all, dmod_cols, *, name):
    tm = 256

    def body(a_ref, b_ref, o_ref):
        o_ref[...] = lax.dot_general(a_ref[...], b_ref[...], (((0,), (0,)), ((), ())), precision=HIGHEST,
                                     preferred_element_type=F32)

    return pl.pallas_call(
        body, name=name, grid=(D_MODEL // tm,),
        in_specs=[pl.BlockSpec((N_DEV, tm), lambda i: (0, i)), pl.BlockSpec(dmod_cols.shape, lambda i: (0, 0))],
        out_specs=pl.BlockSpec((tm, dmod_cols.shape[1]), lambda i: (i, 0)),
        out_shape=jax.ShapeDtypeStruct((D_MODEL, dmod_cols.shape[1]), F32), compiler_params=_params("parallel"))(cond_all, dmod_cols)


def _silu_rows(c8, *, name):
    def body(c_ref, o_ref):
        cv = c_ref[...]
        o_ref[...] = cv * _sigmoid(cv)

    return pl.pallas_call(body, name=name, in_specs=[VMEM_SPEC], out_specs=VMEM_SPEC,
                          out_shape=jax.ShapeDtypeStruct(c8.shape, F32))(c8)


def _rows128(t, rows=None):
    flat = t.reshape(-1, 128)
    return flat if rows is None else jnp.pad(flat, ((0, rows - flat.shape[0]), (0, 0)))


def _from_col_shards(shards, r, n):
    return shards.reshape(N_CHIP, r, n).transpose(1, 0, 2).reshape(r, N_CHIP * n)


def kernel(x, c, w_ada, b_ada, norm1_g, w_in, gla_w_gate, gla_b_gate, gla_norm_g, q_norm_g, k_norm_g, w_out, norm2_g, w_up, conv_w, conv_b, w_down, loss_target, m_w_ada, m_b_ada, m_norm1_g, m_w_in, m_gla_w_gate, m_gla_b_gate, m_gla_norm_g, m_q_norm_g, m_k_norm_g, m_w_out, m_norm2_g, m_w_up, m_conv_w, m_conv_b, m_w_down, v_w_ada, v_b_ada, v_norm1_g, v_w_in, v_gla_w_gate, v_gla_b_gate, v_gla_norm_g, v_q_norm_g, v_k_norm_g, v_w_out, v_norm2_g, v_w_up, v_conv_w, v_conv_b, v_w_down):
    d = D_MODEL
    ax, ay, ac = lax.axis_index("x"), lax.axis_index("y"), lax.axis_index("c")
    chip, dev = 2 * ax + ay, 4 * ax + 2 * ay + ac

    cond = _silu_rows(jnp.broadcast_to(c, (8, d)), name="cond_silu")[0:1]
    small_in = jnp.concatenate([_rows128(cond), _rows128(conv_w[0]), _rows128(gla_w_gate[0])], axis=0)
    small_in = _rows128(small_in, 56)
    got = _all_gather_small(small_in, name="gather_small").reshape(N_DEV, 56, 128)
    cond_all = got[:, 0:8].reshape(N_DEV, d)
    conv_w_full = _from_col_shards(got[0::2, 8:41].reshape(N_CHIP, 3 * 1408 // 128, 128), 3, 1408)
    gate_full = _from_col_shards(got[0::2, 41:49].reshape(N_CHIP, 16 * 64 // 128, 128), GLA_GATE_RANK, 64)
    mod_part = _ada_mod(cond_all, w_ada[0], name="ada_mod")
    mod_got = _all_gather_small(_rows128(mod_part), name="gather_mod").reshape(N_DEV, N_DEV, 1536)
    mod_all = mod_got[0::2].transpose(1, 0, 2).reshape(N_DEV, 6 * d) + b_ada
    mod = lax.dynamic_slice_in_dim(mod_all, dev, 1, axis=0)

    got_in, got_out, got_up, got_down = _gather_weight_shards(
        [w_in[0].astype(BF16), w_out[0].astype(BF16), w_up[0].astype(BF16), w_down[0].astype(BF16)], name="gather_weights")
    w_in_full = got_in.transpose(1, 0, 2).reshape(d, N_CHIP * 772)
    w_out_full = got_out.reshape(d, d)
    w_up_full = got_up.transpose(1, 0, 2).reshape(d, 2 * D_FF)
    w_down_full = got_down.reshape(D_FF, d)

    err2, grad_x, (g_wi, g_wo, g_wup, g_wdown), small = _local_step(
        x[0], loss_target[0], mod, _in_proj_layout(w_in_full), w_out_full, w_up_full, w_down_full, conv_w_full, conv_b,
        _gate_layout(gate_full), gla_b_gate, gla_norm_g, q_norm_g, k_norm_g, norm1_g, norm2_g)

    pieces = [err2[0], small["dmod"], small["norm1_g"], small["norm2_g"], small["gla_w_gate"].reshape(-1), small["gla_b_gate"],
              small["gla_norm_g"], small["q_norm_g"], small["k_norm_g"], small["conv_w"].reshape(-1), small["conv_b"]]
    sizes = [p.shape[0] for p in pieces]
    at = [sum(sizes[:i]) for i in range(len(sizes) + 1)]
    vec = _rows128(jnp.concatenate(pieces), 288)
    got = _all_gather_small(vec, name="gather_grads").reshape(N_DEV, 288, 128)
    total, loss8 = _sum_devices(got, name="sum_devices")
    total = total.reshape(-1)
    seg = lambda i: total[at[i]:at[i + 1]]
    dmod_all = got.reshape(N_DEV, -1)[:, at[1]:at[2]]
    g_small = dict(
        b_ada=seg(1)[None], norm1_g=seg(2)[None], norm2_g=seg(3)[None],
        gla_w_gate=lax.dynamic_slice_in_dim(seg(4).reshape(GLA_GATE_RANK, 256), chip * 64, 64, axis=1),
        gla_b_gate=seg(5)[None], gla_norm_g=seg(6)[None], q_norm_g=seg(7)[None], k_norm_g=seg(8)[None],
        conv_w=lax.dynamic_slice_in_dim(seg(9).reshape(3, 2 * D_FF), chip * 1408, 1408, axis=1), conv_b=seg(10)[None])
    dmod_cols = lax.dynamic_slice_in_dim(dmod_all.reshape(N_DEV, 6 * d), chip * 1536, 1536, axis=1)
    g_w_ada = _ada_grad(cond_all, dmod_cols, name="ada_grad")

    tags = ("w_in", "w_out", "w_up", "w_down")
    g_parts = [_in_proj_grad_layout(g_wi).reshape(d, N_CHIP, 772).transpose(1, 0, 2), g_wo.reshape(N_CHIP, d // N_CHIP, d),
               g_wup, g_wdown.reshape(N_CHIP, D_FF // N_CHIP, d)]
    core_id, chip_id = jnp.reshape(ac, (1,)).astype(jnp.int32), jnp.reshape(chip, (1,)).astype(jnp.int32)
    got = _pair_exchange_halves(g_parts, name="reduce_pair")
    pairs = [_pair_add(g, t, core_id, name=f"reduce_pair_add_{tag}") for g, t, tag in zip(g_parts, got, tags)]
    theirs = _chip_scatter([pb for _, pb in pairs], name="reduce_chips")
    summed = [_chip_add(pf, t, chip_id, name=f"reduce_chips_add_{tag}") for (pf, _), t, tag in zip(pairs, theirs, tags)]
    g_big = _share_halves(summed, name="share_pair")

    grads = dict(w_ada=g_w_ada, w_in=g_big[0], w_out=g_big[1], w_up=g_big[2], w_down=g_big[3], **g_small)
    names = ["w_ada", "b_ada", "norm1_g", "w_in", "gla_w_gate", "gla_b_gate", "gla_norm_g", "q_norm_g", "k_norm_g", "w_out",
             "norm2_g", "w_up", "conv_w", "conv_b", "w_down"]
    ws = dict(w_ada=w_ada, b_ada=b_ada, norm1_g=norm1_g, w_in=w_in, gla_w_gate=gla_w_gate, gla_b_gate=gla_b_gate,
              gla_norm_g=gla_norm_g, q_norm_g=q_norm_g, k_norm_g=k_norm_g, w_out=w_out, norm2_g=norm2_g, w_up=w_up,
              conv_w=conv_w, conv_b=conv_b, w_down=w_down)
    ms = dict(w_ada=m_w_ada, b_ada=m_b_ada, norm1_g=m_norm1_g, w_in=m_w_in, gla_w_gate=m_gla_w_gate, gla_b_gate=m_gla_b_gate,
              gla_norm_g=m_gla_norm_g, q_norm_g=m_q_norm_g, k_norm_g=m_k_norm_g, w_out=m_w_out, norm2_g=m_norm2_g, w_up=m_w_up,
              conv_w=m_conv_w, conv_b=m_conv_b, w_down=m_w_down)
    vs = dict(w_ada=v_w_ada, b_ada=v_b_ada, norm1_g=v_norm1_g, w_in=v_w_in, gla_w_gate=v_gla_w_gate, gla_b_gate=v_gla_b_gate,
              gla_norm_g=v_gla_norm_g, q_norm_g=v_q_norm_g, k_norm_g=v_k_norm_g, w_out=v_w_out, norm2_g=v_norm2_g, w_up=v_w_up,
              conv_w=v_conv_w, conv_b=v_conv_b, w_down=v_w_down)
    g_out, d_out, m_out, v_out = [], [], [], []
    for nm in names:
        w2 = ws[nm].reshape(ws[nm].shape[-2:])
        g2 = grads[nm].reshape(w2.shape)
        dl, mn, vn = _adamw(w2, g2, ms[nm].reshape(w2.shape), vs[nm].reshape(w2.shape), name=f"adamw_{nm}")
        shape = ws[nm].shape
        g_out.append(g2.reshape(shape))
        d_out.append(dl.reshape(shape))
        m_out.append(mn.reshape(shape))
        v_out.append(vn.reshape(shape))
    return (loss8[0, 0], grad_x[None], *g_out, *d_out, *m_out, *v_out)
```

```python
import functools

import jax
import jax.numpy as jnp
from jax import lax
from jax.experimental import pallas as pl
from jax.experimental.pallas import tpu as pltpu

F32, BF16 = jnp.float32, jnp.bfloat16
HIGHEST = lax.Precision.HIGHEST
MESH = pl.DeviceIdType.MESH

D_MODEL = 1024
GLA_CHUNK = 64
GLA_GATE_TAU = 16.0
GLA_GATE_RANK = 16
HEAD_LANES = 128
ATTN_BLOCK = 128
DILATIONS = (1, 4, 16)
ALIBI_SLOPES = tuple(2.0 ** (-(h + 1)) for h in range(8))
D_FF = 2816
EPS = 1e-6
C_GQ, C_GK, C_GV, C_GR, C_AQ, C_AK, C_AV, C_LR, PROJ_W = 0, 256, 512, 1024, 1536, 2048, 2560, 3072, 3200
ADAM_LR, ADAM_B1, ADAM_B2, ADAM_EPS, ADAM_WD, ADAM_STEP = 0.001, 0.9, 0.999, 1e-08, 0.01, 10
VMEM_LIMIT_BYTES = 56 * 1024 * 1024
ROW_TILE = 256


def _params(*sem):
    return pltpu.CompilerParams(dimension_semantics=sem or None, vmem_limit_bytes=VMEM_LIMIT_BYTES)


def _nt(a, b):
    return lax.dot_general(a, b, (((1,), (1,)), ((), ())), preferred_element_type=F32)


def _tn(a, b):
    return lax.dot_general(a, b, (((0,), (0,)), ((), ())), preferred_element_type=F32)


def _nn(a, b, precision=None):
    return jnp.dot(a, b, preferred_element_type=F32, precision=precision)


def _split3(v):
    hi = v.astype(BF16)
    rest = v - hi.astype(F32)
    mid = rest.astype(BF16)
    return hi, mid, (rest - mid.astype(F32)).astype(BF16)


def _sum_right(v, ones):
    hi, mid, lo = _split3(v)
    return (_nn(lo, ones) + _nn(mid, ones)) + _nn(hi, ones)


def _sum_left(ones, v):
    hi, mid, lo = _split3(v)
    return (_nn(ones, lo) + _nn(ones, mid)) + _nn(ones, hi)


def _fold8(v):
    return v.reshape(v.shape[0] // 8, 8, v.shape[1]).sum(axis=0)


def _spread_total(ref):
    t = ref[...]
    ref[...] = jnp.broadcast_to(jnp.sum(t, axis=-2, keepdims=True), t.shape)


def _sigmoid(x):
    return 1.0 / (1.0 + jnp.exp(-x))


def _mm(a, b, *, ta=False, tb=False, out_dtype=F32, tm, tn, tk, shard_cols=False, name):
    (k_a, m) = a.shape if ta else a.shape[::-1]
    (k_b, n) = b.shape[::-1] if tb else b.shape
    assert k_a == k_b and m % tm == 0 and n % tn == 0 and k_a % tk == 0, (name, a.shape, b.shape)
    nk = k_a // tk
    assert nk == 1 or out_dtype == F32, name
    dims = (((0 if ta else 1,), (1 if tb else 0,)), ((), ()))

    def body(a_ref, b_ref, o_ref):
        k = pl.program_id(2)
        part = lax.dot_general(a_ref[...].astype(BF16), b_ref[...].astype(BF16), dims, preferred_element_type=F32)
        if nk == 1:
            o_ref[...] = part.astype(out_dtype)
        else:
            @pl.when(k == 0)
            def _():
                o_ref[...] = part

            @pl.when(k > 0)
            def _():
                o_ref[...] += part

    a_spec = pl.BlockSpec((tk, tm), lambda i, j, k: (k, i)) if ta else pl.BlockSpec((tm, tk), lambda i, j, k: (i, k))
    b_spec = pl.BlockSpec((tn, tk), lambda i, j, k: (j, k)) if tb else pl.BlockSpec((tk, tn), lambda i, j, k: (k, j))
    if shard_cols:
        o_spec, o_shape = pl.BlockSpec((None, tm, tn), lambda i, j, k: (j, i, 0)), (n // tn, m, tn)
    else:
        o_spec, o_shape = pl.BlockSpec((tm, tn), lambda i, j, k: (i, j)), (m, n)
    return pl.pallas_call(
        body, name=name, grid=(m // tm, n // tn, nk), in_specs=[a_spec, b_spec], out_specs=o_spec,
        out_shape=jax.ShapeDtypeStruct(o_shape, out_dtype), compiler_params=_params("parallel", "parallel", "arbitrary"),
    )(a, b)


def _norm_mod_fwd(x, branch, gate, gain, scale, shift, *, name):
    s, d = x.shape
    tm = ROW_TILE
    has_branch = branch is not None

    def body(*refs):
        if has_branch:
            x_ref, br_ref, gate_ref, gain_ref, sc_ref, sh_ref, x1_ref, h_ref = refs
            xv = x_ref[...] + gate_ref[...] * br_ref[...]
            x1_ref[...] = xv
        else:
            x_ref, gain_ref, sc_ref, sh_ref, h_ref = refs
            xv = x_ref[...]
        r = lax.rsqrt(jnp.mean(xv * xv, axis=-1, keepdims=True) + EPS)
        h_ref[...] = ((xv * r) * gain_ref[...] * (1.0 + sc_ref[...]) + sh_ref[...]).astype(BF16)

    row = pl.BlockSpec((tm, d), lambda i: (i, 0))
    vec = pl.BlockSpec((1, d), lambda i: (0, 0))
    if has_branch:
        return pl.pallas_call(
            body, name=name, grid=(s // tm,), in_specs=[row, row, vec, vec, vec, vec], out_specs=[row, row],
            out_shape=[jax.ShapeDtypeStruct((s, d), F32), jax.ShapeDtypeStruct((s, d), BF16)],
            compiler_params=_params("parallel"))(x, branch, gate, gain, scale, shift)
    h = pl.pallas_call(
        body, name=name, grid=(s // tm,), in_specs=[row, vec, vec, vec], out_specs=row,
        out_shape=jax.ShapeDtypeStruct((s, d), BF16), compiler_params=_params("parallel"))(x, gain, scale, shift)
    return x, h


def _norm_mod_bwd(x, dh, dres, gain, scale, branch, gate, *, name):
    s, d = x.shape
    tm = ROW_TILE
    has_branch = branch is not None

    def body(*refs):
        if has_branch:
            x_ref, dh_ref, dres_ref, gain_ref, sc_ref, br_ref, gate_ref, dx_ref, dbr_ref, sums_ref = refs
        else:
            x_ref, dh_ref, dres_ref, gain_ref, sc_ref, dx_ref, sums_ref = refs
        i = pl.program_id(0)

        @pl.when(i == 0)
        def _():
            sums_ref[...] = jnp.zeros_like(sums_ref)

        xv, dhv = x_ref[...], dh_ref[...]
        r = lax.rsqrt(jnp.mean(xv * xv, axis=-1, keepdims=True) + EPS)
        xn = xv * r
        dxn = dhv * (gain_ref[...] * (1.0 + sc_ref[...]))
        dx = dres_ref[...] + r * (dxn - xn * jnp.mean(dxn * xn, axis=-1, keepdims=True))
        dx_ref[...] = dx
        sums_ref[0] += _fold8(dhv * xn)
        sums_ref[1] += _fold8(dhv)
        if has_branch:
            dbr_ref[...] = (gate_ref[...] * dx).astype(BF16)
            sums_ref[2] += _fold8(dx * br_ref[...])

        @pl.when(i == s // tm - 1)
        def _():
            _spread_total(sums_ref)

    row = pl.BlockSpec((tm, d), lambda i: (i, 0))
    vec = pl.BlockSpec((1, d), lambda i: (0, 0))
    sums = pl.BlockSpec((3, 8, d), lambda i: (0, 0, 0))
    sums_shape = jax.ShapeDtypeStruct((3, 8, d), F32)
    if has_branch:
        return pl.pallas_call(
            body, name=name, grid=(s // tm,), in_specs=[row, row, row, vec, vec, row, vec], out_specs=[row, row, sums],
            out_shape=[jax.ShapeDtypeStruct((s, d), F32), jax.ShapeDtypeStruct((s, d), BF16), sums_shape],
            compiler_params=_params("arbitrary"))(x, dh, dres, gain, scale, branch, gate)
    dx, sm = pl.pallas_call(
        body, name=name, grid=(s // tm,), in_specs=[row, row, row, vec, vec], out_specs=[row, sums],
        out_shape=[jax.ShapeDtypeStruct((s, d), F32), sums_shape],
        compiler_params=_params("arbitrary"))(x, dh, dres, gain, scale)
    return dx, None, sm


GLA_ROWS = 256


def _gla_chunk_setup(lr_ref, wg_ref, bg_ref, rows):
    c = GLA_CHUNK
    ri = lax.broadcasted_iota(jnp.int32, (c, c), 0)
    ci = lax.broadcasted_iota(jnp.int32, (c, c), 1)
    z = _nn(lr_ref[rows, :].astype(BF16), wg_ref[...]) + bg_ref[...]
    g = (jnp.minimum(z, 0.0) - jnp.log(1.0 + jnp.exp(-jnp.abs(z)))) * (1.0 / GLA_GATE_TAU)
    b = _sum_left((ci <= ri).astype(BF16), g)
    return z, b, ci <= ri


def _last_row(b):
    ri = lax.broadcasted_iota(jnp.int32, b.shape, 0)
    return jnp.sum(jnp.where(ri == b.shape[0] - 1, b, 0.0), axis=0, keepdims=True)


def _gla_fwd(proj, wg, bg, gn, *, name):
    s = proj.shape[0]
    tb, c = GLA_ROWS, GLA_CHUNK
    cb = tb // c

    def body(q_ref, k_ref, v_ref, r_ref, lr_ref, wg_ref, bg_ref, gn_ref, o_ref, y_ref, st_ref, state):
        i = pl.program_id(0)

        @pl.when(i == 0)
        def _():
            state[...] = jnp.zeros_like(state)

        low = lax.broadcasted_iota(jnp.int32, (c, HEAD_LANES), 1) < 64
        for ch in range(cb):
            rows = pl.ds(ch * c, c)
            _, b, causal = _gla_chunk_setup(lr_ref, wg_ref, bg_ref, rows)
            for p in range(2):
                cols = pl.ds(p * HEAD_LANES, HEAD_LANES)
                bp = b[:, p * HEAD_LANES:(p + 1) * HEAD_LANES]
                b_end = _last_row(bp)
                q = q_ref[rows, cols] * 0.125
                k = k_ref[rows, cols]
                q_in = q * jnp.exp(bp)
                k_out = (k * jnp.exp(-bp)).astype(BF16)
                k_end = k * jnp.exp(b_end - bp)
                st = state[p]
                st_ref[ch, p] = st
                st_b = st.astype(BF16)
                upd = jnp.zeros_like(st)
                for e in range(2):
                    msk = low if e == 0 else jnp.logical_not(low)
                    hc = pl.ds((2 * p + e) * HEAD_LANES, HEAD_LANES)
                    qm = jnp.where(msk, q_in, 0.0).astype(BF16)
                    a = jnp.where(causal, _nt(qm, k_out), 0.0)
                    v = v_ref[rows, hc].astype(BF16)
                    o = _nt(qm, st_b) + _nn(a.astype(BF16), v)
                    upd = upd + _tn(v, jnp.where(msk, k_end, 0.0).astype(BF16))
                    o_ref[rows, hc] = o
                    rr = r_ref[rows, hc]
                    on = o * lax.rsqrt(jnp.mean(o * o, axis=-1, keepdims=True) + EPS)
                    y_ref[rows, hc] = (on * gn_ref[...] * (rr * _sigmoid(rr))).astype(BF16)
                state[p] = st * jnp.exp(b_end) + upd

    def col(width, at):
        return pl.BlockSpec((tb, width), lambda i: (i, at // width))

    full = lambda shape: pl.BlockSpec(shape, lambda i: tuple(0 for _ in shape))
    return pl.pallas_call(
        body, name=name, grid=(s // tb,),
        in_specs=[col(256, C_GQ), col(256, C_GK), col(512, C_GV), col(512, C_GR), col(128, C_LR),
                  full((HEAD_LANES, 256)), full((1, 256)), full((1, HEAD_LANES))],
        out_specs=[pl.BlockSpec((tb, 512), lambda i: (i, 0)), pl.BlockSpec((tb, 512), lambda i: (i, 0)),
                   pl.BlockSpec((cb, 2, HEAD_LANES, HEAD_LANES), lambda i: (i, 0, 0, 0))],
        out_shape=[jax.ShapeDtypeStruct((s, 512), F32), jax.ShapeDtypeStruct((s, 512), BF16),
                   jax.ShapeDtypeStruct((s // c, 2, HEAD_LANES, HEAD_LANES), F32)],
        scratch_shapes=[pltpu.VMEM((2, HEAD_LANES, HEAD_LANES), F32)],
        compiler_params=_params("arbitrary"))(proj, proj, proj, proj, proj, wg, bg, gn)


def _gla_bwd(proj, wg, bg, gn, o_raw, states, dmixed, *, name):
    s = proj.shape[0]
    tb, c = GLA_ROWS, GLA_CHUNK
    cb = tb // c
    nblk, nch = s // tb, s // c

    def body(q_ref, k_ref, v_ref, r_ref, lr_ref, wg_ref, bg_ref, gn_ref, o_ref, st_ref, stn_ref, dy_ref,
             dq_ref, dk_ref, dv_ref, dr_ref, dlr_ref, gwg_ref, sums_ref, dstate):
        i = pl.program_id(0)

        @pl.when(i == 0)
        def _():
            dstate[...] = jnp.zeros_like(dstate)
            gwg_ref[...] = jnp.zeros_like(gwg_ref)
            sums_ref[...] = jnp.zeros_like(sums_ref)

        low = lax.broadcasted_iota(jnp.int32, (c, HEAD_LANES), 1) < 64
        for ch in reversed(range(cb)):
            rows = pl.ds(ch * c, c)
            z, b, causal = _gla_chunk_setup(lr_ref, wg_ref, bg_ref, rows)
            upper = jnp.logical_not(causal) | (lax.broadcasted_iota(jnp.int32, (c, c), 0)
                                               == lax.broadcasted_iota(jnp.int32, (c, c), 1))
            lr_b = lr_ref[rows, :].astype(BF16)
            dlr = jnp.zeros((c, HEAD_LANES), F32)
            for p in range(2):
                cols = pl.ds(p * HEAD_LANES, HEAD_LANES)
                sl = slice(p * HEAD_LANES, (p + 1) * HEAD_LANES)
                bp = b[:, sl]
                b_end = _last_row(bp)
                e_in, e_out, e_end = jnp.exp(bp), jnp.exp(-bp), jnp.exp(b_end - bp)
                q = q_ref[rows, cols] * 0.125
                k = k_ref[rows, cols]
                q_in = q * e_in
                k_out = k * e_out
                k_end = k * e_end
                st0 = st_ref[ch, p]
                st1 = st_ref[ch + 1, p] if ch + 1 < cb else stn_ref[0, p]
                dst = dstate[p]
                st0_b, dst_b = st0.astype(BF16), dst.astype(BF16)
                dq_in = jnp.zeros((c, HEAD_LANES), F32)
                dk_out = jnp.zeros((c, HEAD_LANES), F32)
                dk_end = jnp.zeros((c, HEAD_LANES), F32)
                dst_new = dst * jnp.exp(b_end)
                for e in range(2):
                    msk = low if e == 0 else jnp.logical_not(low)
                    hc = pl.ds((2 * p + e) * HEAD_LANES, HEAD_LANES)
                    o = o_ref[rows, hc]
                    rr = r_ref[rows, hc]
                    dy = dy_ref[rows, hc]
                    sg = _sigmoid(rr)
                    rs = lax.rsqrt(jnp.mean(o * o, axis=-1, keepdims=True) + EPS)
                    on = o * rs
                    t = dy * (rr * sg)
                    sums_ref[1, :, hc] += _fold8(t * on)
                    dn = t * gn_ref[...]
                    do = (rs * (dn - on * jnp.mean(dn * on, axis=-1, keepdims=True))).astype(BF16)
                    dr_ref[rows, hc] = (dy * on * gn_ref[...] * (sg * (1.0 + rr * (1.0 - sg)))).astype(BF16)
                    qm = jnp.where(msk, q_in, 0.0).astype(BF16)
                    km_out = jnp.where(msk, k_out, 0.0).astype(BF16)
                    km_end = jnp.where(msk, k_end, 0.0).astype(BF16)
                    v = v_ref[rows, hc].astype(BF16)
                    a = jnp.where(causal, _nt(qm, km_out), 0.0).astype(BF16)
                    da = jnp.where(causal, _nt(do, v), 0.0).astype(BF16)
                    dv_ref[rows, hc] = (_tn(a, do) + _nt(km_end, dst_b)).astype(BF16)
                    dq_in = dq_in + jnp.where(msk, _nn(do, st0_b) + _nn(da, km_out), 0.0)
                    dk_out = dk_out + _tn(da, qm)
                    dk_end = dk_end + jnp.where(msk, _nn(v, dst_b), 0.0)
                    dst_new = dst_new + _tn(do, qm)
                dq = dq_in * e_in
                dk = dk_out * e_out + dk_end * e_end
                dq_ref[rows, cols] = (dq * 0.125).astype(BF16)
                dk_ref[rows, cols] = dk.astype(BF16)
                w = q * dq - k * dk
                dg = _sum_left(upper.astype(BF16), w) + jnp.sum(dst * st1, axis=0, keepdims=True)
                zp = z[:, sl]
                dz = dg * (1.0 / GLA_GATE_TAU) * _sigmoid(-zp)
                dz_b = dz.astype(BF16)
                sums_ref[0, :, cols] += _fold8(dz)
                dlr = dlr + _nt(dz_b, wg_ref[:, cols])
                gwg_ref[:, cols] += _tn(lr_b, dz_b)
                dstate[p] = dst_new
            dlr_ref[rows, :] = dlr.astype(BF16)

        @pl.when(i == nblk - 1)
        def _():
            _spread_total(sums_ref)

    rev = lambda i: nblk - 1 - i

    def col(width, at):
        return pl.BlockSpec((tb, width), lambda i: (rev(i), at // width))

    full = lambda shape: pl.BlockSpec(shape, lambda i: tuple(0 for _ in shape))
    out_col = lambda width: pl.BlockSpec((tb, width), lambda i: (rev(i), 0))
    return pl.pallas_call(
        body, name=name, grid=(nblk,),
        in_specs=[col(256, C_GQ), col(256, C_GK), col(512, C_GV), col(512, C_GR), col(128, C_LR),
                  full((HEAD_LANES, 256)), full((1, 256)), full((1, HEAD_LANES)),
                  pl.BlockSpec((tb, 512), lambda i: (rev(i), 0)),
                  pl.BlockSpec((cb, 2, HEAD_LANES, HEAD_LANES), lambda i: (rev(i), 0, 0, 0)),
                  pl.BlockSpec((1, 2, HEAD_LANES, HEAD_LANES), lambda i: (jnp.minimum((rev(i) + 1) * cb, nch - 1), 0, 0, 0)),
                  pl.BlockSpec((tb, 512), lambda i: (rev(i), 0))],
        out_specs=[out_col(256), out_col(256), out_col(512), out_col(512), out_col(128),
                   full((HEAD_LANES, 256)), full((2, 8, 512))],
        out_shape=[jax.ShapeDtypeStruct((s, 256), BF16), jax.ShapeDtypeStruct((s, 256), BF16),
                   jax.ShapeDtypeStruct((s, 512), BF16), jax.ShapeDtypeStruct((s, 512), BF16),
                   jax.ShapeDtypeStruct((s, 128), BF16), jax.ShapeDtypeStruct((HEAD_LANES, 256), F32),
                   jax.ShapeDtypeStruct((2, 8, 512), F32)],
        scratch_shapes=[pltpu.VMEM((2, HEAD_LANES, HEAD_LANES), F32)],
        compiler_params=_params("arbitrary"))(proj, proj, proj, proj, proj, wg, bg, gn, o_raw, states, states, dmixed)


def _head_sum_matrix():
    ri = lax.broadcasted_iota(jnp.int32, (512, 512), 0) // 64
    ci = lax.broadcasted_iota(jnp.int32, (512, 512), 1) // 64
    return (ri == ci).astype(BF16)


def _attn_prep(proj, qg, kg, *, name):
    s = proj.shape[0]
    tm = ROW_TILE

    def body(q_ref, k_ref, qg_ref, kg_ref, qa_ref, ka_ref):
        hs = _head_sum_matrix()
        q, k = q_ref[...], k_ref[...]
        qr = lax.rsqrt(_sum_right(q * q, hs) * (1.0 / 64) + EPS)
        kr = lax.rsqrt(_sum_right(k * k, hs) * (1.0 / 64) + EPS)
        qa_ref[...] = q * qr * qg_ref[...] * 0.125
        ka_ref[...] = k * kr * kg_ref[...]

    col = lambda at: pl.BlockSpec((tm, 512), lambda i: (i, at // 512))
    vec = pl.BlockSpec((1, 512), lambda i: (0, 0))
    out = pl.BlockSpec((tm, 512), lambda i: (i, 0))
    return pl.pallas_call(
        body, name=name, grid=(s // tm,), in_specs=[col(C_AQ), col(C_AK), vec, vec], out_specs=[out] * 2,
        out_shape=[jax.ShapeDtypeStruct((s, 512), F32)] * 2, compiler_params=_params("parallel"))(proj, proj, qg, kg)


FAR = 1e30


def _attn_distance(first):
    blk = ATTN_BLOCK
    iq = lax.broadcasted_iota(jnp.int32, (2 * blk, 2 * blk), 0) & (blk - 1)
    ik = lax.broadcasted_iota(jnp.int32, (2 * blk, 2 * blk), 1)
    rel = iq + blk - ik
    valid = (rel >= 0) & (rel <= blk) & (jnp.logical_not(first) | (ik >= blk))
    return jnp.where(valid, rel.astype(F32), FAR)


def _stack_heads(t2):
    low = lax.broadcasted_iota(jnp.int32, t2.shape, 1) < 64
    return jnp.concatenate([jnp.where(low, t2, 0.0), jnp.where(low, 0.0, t2)], axis=0).astype(BF16)


def _unstack_heads(t):
    blk = ATTN_BLOCK
    low = lax.broadcasted_iota(jnp.int32, (blk, HEAD_LANES), 1) < 64
    return jnp.where(low, t[0:blk], t[blk:2 * blk])


def _attn_scores(qs, kcat, slopes, dil, dist):
    top = lax.broadcasted_iota(jnp.int32, (2 * ATTN_BLOCK, 1), 0) < ATTN_BLOCK
    return _nt(qs, kcat) - jnp.where(top, slopes[0] * dil, slopes[1] * dil) * dist


def _pair_slopes(p):
    if isinstance(p, int):
        return ALIBI_SLOPES[2 * p], ALIBI_SLOPES[2 * p + 1]
    pick = lambda e: jnp.where(p == 0, ALIBI_SLOPES[e], jnp.where(p == 1, ALIBI_SLOPES[2 + e],
                               jnp.where(p == 2, ALIBI_SLOPES[4 + e], ALIBI_SLOPES[6 + e])))
    return pick(0), pick(1)


def _attn_pair_fwd(q2, kcat, vcat, slopes, dil, dist):
    sc = _attn_scores(_stack_heads(q2), kcat, slopes, dil, dist)
    m = jnp.max(sc, axis=-1, keepdims=True)
    pr = jnp.exp(sc - m)
    den = jnp.sum(pr, axis=-1, keepdims=True)
    o = _nn(pr.astype(BF16), vcat) / den
    lse = jnp.broadcast_to(m + jnp.log(den), o.shape)
    return _unstack_heads(o), _unstack_heads(lse)


def _attn_pair_bwd(q2, kcat, vcat, do2, y2, lse2, slopes, dil, dist):
    lane = lax.broadcasted_iota(jnp.int32, (ATTN_BLOCK, HEAD_LANES), 1)
    low = lane < 64
    prod = do2 * y2
    per_head = lambda t, pick: jnp.concatenate([jnp.sum(jnp.where(pick(0), t, 0.0), axis=-1, keepdims=True),
                                                jnp.sum(jnp.where(pick(1), t, 0.0), axis=-1, keepdims=True)], axis=0)
    lse = per_head(lse2, lambda e: lane == 64 * e)
    delta = per_head(prod, lambda e: low if e == 0 else jnp.logical_not(low))
    qs, dos = _stack_heads(q2), _stack_heads(do2)
    pr = jnp.exp(_attn_scores(qs, kcat, slopes, dil, dist) - lse)
    ds = (pr * (_nt(dos, vcat) - delta)).astype(BF16)
    return _unstack_heads(_nn(ds, kcat)), _tn(ds, qs), _tn(pr.astype(BF16), dos)


def _attn_specs(dil):
    rows = ATTN_BLOCK * dil
    if dil == 1:
        cur = lambda at: pl.BlockSpec((rows, 512), lambda n: (n, at // 512))
        prev = lambda at: pl.BlockSpec((rows, 512), lambda n: (jnp.maximum(n - 1, 0), at // 512))
    else:
        cur = lambda at: pl.BlockSpec((rows, HEAD_LANES), lambda n, p: (n, at // HEAD_LANES + p))
        prev = lambda at: pl.BlockSpec((rows, HEAD_LANES), lambda n, p: (jnp.maximum(n - 1, 0), at // HEAD_LANES + p))
    return cur, prev


def _attn_loop(dil, one_pair):
    if dil == 1:
        for p in range(4):
            one_pair(slice(None), pl.ds(p * HEAD_LANES, HEAD_LANES), p)
    else:
        p = pl.program_id(1)

        def step(r, carry):
            one_pair(pl.ds(r, ATTN_BLOCK, stride=dil), slice(None), p)
            return carry

        lax.fori_loop(0, dil, step, 0, unroll=min(dil, 4))


def _dil_attn_fwd(qa, ka, proj, dil, *, name):
    s = qa.shape[0]

    def body(q_ref, kp_ref, kc_ref, vp_ref, vc_ref, o_ref, lse_ref):
        dist = _attn_distance(pl.program_id(0) == 0)

        def one_pair(rows, cols, p):
            kcat = jnp.concatenate([kp_ref[rows, cols], kc_ref[rows, cols]], axis=0).astype(BF16)
            vcat = jnp.concatenate([vp_ref[rows, cols], vc_ref[rows, cols]], axis=0).astype(BF16)
            o2, lse2 = _attn_pair_fwd(q_ref[rows, cols], kcat, vcat, _pair_slopes(p), dil, dist)
            o_ref[rows, cols] = o2
            lse_ref[rows, cols] = lse2

        _attn_loop(dil, one_pair)

    cur, prev = _attn_specs(dil)
    grid = (s // ATTN_BLOCK,) if dil == 1 else (s // (ATTN_BLOCK * dil), 4)
    return pl.pallas_call(
        body, name=name, grid=grid, in_specs=[cur(0), prev(0), cur(0), prev(C_AV), cur(C_AV)], out_specs=[cur(0), cur(0)],
        out_shape=[jax.ShapeDtypeStruct((s, 512), F32)] * 2,
        compiler_params=_params(*["parallel"] * len(grid)))(qa, ka, ka, proj, proj)


def _attn_merge(branches, y_gla, *, name):
    s = y_gla.shape[0]
    tm = ROW_TILE

    def body(o0, l0, o1, l1, o2, l2, yg_ref, mixed_ref, y_ref, lse_ref):
        m = jnp.maximum(jnp.maximum(l0[...], l1[...]), l2[...])
        w0, w1, w2 = jnp.exp(l0[...] - m), jnp.exp(l1[...] - m), jnp.exp(l2[...] - m)
        zs = w0 + w1 + w2
        y = (w0 * o0[...] + w1 * o1[...] + w2 * o2[...]) / zs
        y_ref[...] = y
        lse_ref[...] = m + jnp.log(zs)
        mixed_ref[:, 0:512] = yg_ref[...]
        mixed_ref[:, 512:1024] = y.astype(BF16)

    blk = pl.BlockSpec((tm, 512), lambda i: (i, 0))
    args = [t for pair in branches for t in pair]
    return pl.pallas_call(
        body, name=name, grid=(s // tm,), in_specs=[blk] * 7,
        out_specs=[pl.BlockSpec((tm, 1024), lambda i: (i, 0)), blk, blk],
        out_shape=[jax.ShapeDtypeStruct((s, 1024), BF16), jax.ShapeDtypeStruct((s, 512), F32),
                   jax.ShapeDtypeStruct((s, 512), F32)],
        compiler_params=_params("parallel"))(*args, y_gla)


def _dil_attn_bwd(qa, ka, proj, y_att, lse, dmixed, dil, *, name):
    s = qa.shape[0]
    blk = ATTN_BLOCK

    def body(q_ref, kp_ref, kc_ref, vp_ref, vc_ref, y_ref, lse_ref, do_ref, dq_ref, dkc_ref, dkp_ref, dvc_ref, dvp_ref):
        dist = _attn_distance(pl.program_id(0) == 0)

        def one_pair(rows, cols, p):
            kcat = jnp.concatenate([kp_ref[rows, cols], kc_ref[rows, cols]], axis=0).astype(BF16)
            vcat = jnp.concatenate([vp_ref[rows, cols], vc_ref[rows, cols]], axis=0).astype(BF16)
            dq, dk, dv = _attn_pair_bwd(q_ref[rows, cols], kcat, vcat, do_ref[rows, cols], y_ref[rows, cols],
                                        lse_ref[rows, cols], _pair_slopes(p), dil, dist)
            dq_ref[rows, cols] = dq
            dkp_ref[rows, cols] = dk[0:blk]
            dkc_ref[rows, cols] = dk[blk:2 * blk]
            dvp_ref[rows, cols] = dv[0:blk]
            dvc_ref[rows, cols] = dv[blk:2 * blk]

        _attn_loop(dil, one_pair)

    cur, prev = _attn_specs(dil)
    grid = (s // blk,) if dil == 1 else (s // (blk * dil), 4)
    return pl.pallas_call(
        body, name=name, grid=grid,
        in_specs=[cur(0), prev(0), cur(0), prev(C_AV), cur(C_AV), cur(0), cur(0), cur(512)], out_specs=[cur(0)] * 5,
        out_shape=[jax.ShapeDtypeStruct((s, 512), F32)] * 5, compiler_params=_params(*["parallel"] * len(grid)),
    )(qa, ka, ka, proj, proj, y_att, lse, dmixed)


def _attn_post(parts, proj, qg, kg, *, name):
    s = proj.shape[0]
    tm = ATTN_BLOCK
    nblk = s // tm

    def body(*refs):
        ins, (q_ref, k_ref, qg_ref, kg_ref, dq_out, dk_out, dv_out, sums_ref) = refs[:15], refs[15:]
        i = pl.program_id(0)

        @pl.when(i == 0)
        def _():
            sums_ref[...] = jnp.zeros_like(sums_ref)

        dq = jnp.zeros((tm, 512), F32)
        dk = jnp.zeros((tm, 512), F32)
        dv = jnp.zeros((tm, 512), F32)
        for g, dil in enumerate(DILATIONS):
            dq_r, dkc_r, dkp_r, dvc_r, dvp_r = ins[5 * g:5 * g + 5]
            inside = (i + dil < nblk).astype(F32)
            dq = dq + dq_r[...]
            dk = dk + dkc_r[...] + inside * dkp_r[...]
            dv = dv + dvc_r[...] + inside * dvp_r[...]
        dv_out[...] = dv.astype(BF16)
        hs = _head_sum_matrix()
        for row, (x_ref, g_ref, dy, out, post) in enumerate(((q_ref, qg_ref, dq, dq_out, 0.125), (k_ref, kg_ref, dk, dk_out, 1.0))):
            x = x_ref[...]
            rs = lax.rsqrt(_sum_right(x * x, hs) * (1.0 / 64) + EPS)
            xn = x * rs
            dy = dy * post
            sums_ref[row] += _fold8(dy * xn)
            dn = dy * g_ref[...]
            out[...] = (rs * (dn - xn * (_sum_right(dn * xn, hs) * (1.0 / 64)))).astype(BF16)

        @pl.when(i == nblk - 1)
        def _():
            _spread_total(sums_ref)

    here = pl.BlockSpec((tm, 512), lambda i: (i, 0))
    specs = []
    for dil in DILATIONS:
        later = pl.BlockSpec((tm, 512), lambda i, dil=dil: (jnp.minimum(i + dil, nblk - 1), 0))
        specs += [here, here, later, here, later]
    col = lambda at: pl.BlockSpec((tm, 512), lambda i: (i, at // 512))
    vec = pl.BlockSpec((1, 512), lambda i: (0, 0))
    return pl.pallas_call(
        body, name=name, grid=(nblk,), in_specs=specs + [col(C_AQ), col(C_AK), vec, vec],
        out_specs=[here, here, here, pl.BlockSpec((2, 8, 512), lambda i: (0, 0, 0))],
        out_shape=[jax.ShapeDtypeStruct((s, 512), BF16)] * 3 + [jax.ShapeDtypeStruct((2, 8, 512), F32)],
        compiler_params=_params("arbitrary"))(*[t for part in parts for t in part], proj, proj, qg, kg)


FFN_TM, FFN_TN = 256, 1408
HALO = 16


def _conv3(u_ref, halo_ref, w_ref, b_ref, first):
    u = u_ref[...].astype(F32)
    ext = jnp.concatenate([jnp.where(first, 0.0, halo_ref[...].astype(F32)), u], axis=0)
    u1 = pltpu.roll(ext, 1, 0)[HALO:]
    u2 = pltpu.roll(ext, 2, 0)[HALO:]
    return b_ref[...] + w_ref[0:1, :] * u2 + w_ref[1:2, :] * u1 + w_ref[2:3, :] * u, u, u1, u2


def _ffn_specs(tm, tn):
    nj = D_FF // tn
    blk = lambda half: pl.BlockSpec((tm, tn), lambda j, i: (i, j + half * nj))
    halo = lambda half: pl.BlockSpec((HALO, tn), lambda j, i: (jnp.maximum(i * (tm // HALO) - 1, 0), j + half * nj))
    wspec = lambda half: pl.BlockSpec((3, tn), lambda j, i: (0, j + half * nj))
    bspec = lambda half: pl.BlockSpec((1, tn), lambda j, i: (0, j + half * nj))
    return [blk(0), halo(0), blk(1), halo(1), wspec(0), wspec(1), bspec(0), bspec(1)]


def _conv_swiglu_fwd(u, conv_w, conv_b, *, name):
    s = u.shape[0]
    tm, tn = FFN_TM, FFN_TN

    def body(ug_ref, hg_ref, uv_ref, hv_ref, wg_ref, wv_ref, bg_ref, bv_ref, act_ref):
        first = pl.program_id(1) == 0
        cg = _conv3(ug_ref, hg_ref, wg_ref, bg_ref, first)[0]
        cv = _conv3(uv_ref, hv_ref, wv_ref, bv_ref, first)[0]
        act_ref[...] = (cg * _sigmoid(cg) * cv).astype(BF16)

    return pl.pallas_call(
        body, name=name, grid=(D_FF // tn, s // tm), in_specs=_ffn_specs(tm, tn),
        out_specs=pl.BlockSpec((tm, tn), lambda j, i: (i, j)), out_shape=jax.ShapeDtypeStruct((s, D_FF), BF16),
        compiler_params=_params("parallel", "parallel"))(u, u, u, u, conv_w, conv_w, conv_b, conv_b)


def _conv_swiglu_bwd_pre(u, conv_w, conv_b, dact, *, name):
    s = u.shape[0]
    tm, tn = FFN_TM, FFN_TN

    def body(ug_ref, hg_ref, uv_ref, hv_ref, wg_ref, wv_ref, bg_ref, bv_ref, da_ref, duc_ref, sums_ref):
        i = pl.program_id(1)

        @pl.when(i == 0)
        def _():
            sums_ref[...] = jnp.zeros_like(sums_ref)

        cg, g0, g1, g2 = _conv3(ug_ref, hg_ref, wg_ref, bg_ref, i == 0)
        cv, v0, v1, v2 = _conv3(uv_ref, hv_ref, wv_ref, bv_ref, i == 0)
        da = da_ref[...].astype(F32)
        sg = _sigmoid(cg)
        dg = da * cv * (sg * (1.0 + cg * (1.0 - sg)))
        dv = da * (cg * sg)
        duc_ref[0] = dg.astype(BF16)
        duc_ref[1] = dv.astype(BF16)
        for half, (d, taps) in enumerate(((dg, (g2, g1, g0)), (dv, (v2, v1, v0)))):
            for t, tap in enumerate(taps):
                sums_ref[half, t] += _fold8(d * tap)
            sums_ref[half, 3] += _fold8(d)

        @pl.when(i == s // tm - 1)
        def _():
            _spread_total(sums_ref)

    return pl.pallas_call(
        body, name=name, grid=(D_FF // tn, s // tm),
        in_specs=_ffn_specs(tm, tn) + [pl.BlockSpec((tm, tn), lambda j, i: (i, j))],
        out_specs=[pl.BlockSpec((2, tm, tn), lambda j, i: (0, i, j)), pl.BlockSpec((2, 4, 8, tn), lambda j, i: (0, 0, 0, j))],
        out_shape=[jax.ShapeDtypeStruct((2, s, D_FF), BF16), jax.ShapeDtypeStruct((2, 4, 8, D_FF), F32)],
        compiler_params=_params("parallel", "arbitrary"))(u, u, u, u, conv_w, conv_w, conv_b, conv_b, dact)


def _conv_bwd(duc, conv_w, *, name):
    _, s, _ = duc.shape
    tm, tn = FFN_TM, FFN_TN
    nj, ni = D_FF // tn, s // tm

    def body(d_ref, halo_ref, w_ref, du_ref):
        last = pl.program_id(2) == ni - 1
        d = d_ref[0].astype(F32)
        ext = jnp.concatenate([d, jnp.where(last, 0.0, halo_ref[0].astype(F32))], axis=0)
        n = tm + HALO
        d1 = pltpu.roll(ext, n - 1, 0)[:tm]
        d2 = pltpu.roll(ext, n - 2, 0)[:tm]
        du_ref[...] = (w_ref[2:3, :] * d + w_ref[1:2, :] * d1 + w_ref[0:1, :] * d2).astype(BF16)

    return pl.pallas_call(
        body, name=name, grid=(2, nj, ni),
        in_specs=[pl.BlockSpec((1, tm, tn), lambda g, j, i: (g, i, j)),
                  pl.BlockSpec((1, HALO, tn), lambda g, j, i: (g, jnp.minimum((i + 1) * (tm // HALO), s // HALO - 1), j)),
                  pl.BlockSpec((3, tn), lambda g, j, i: (0, g * nj + j))],
        out_specs=pl.BlockSpec((tm, tn), lambda g, j, i: (i, g * nj + j)),
        out_shape=jax.ShapeDtypeStruct((s, 2 * D_FF), BF16),
        compiler_params=_params("parallel", "parallel", "parallel"))(duc, duc, conv_w)


def _loss_head(x1, ffn, gate, target, *, name):
    s, d = x1.shape
    tm = ROW_TILE

    def body(x_ref, f_ref, g_ref, t_ref, dy_ref, df_ref, sums_ref):
        i = pl.program_id(0)

        @pl.when(i == 0)
        def _():
            sums_ref[...] = jnp.zeros_like(sums_ref)

        f = f_ref[...]
        err = x_ref[...] + g_ref[...] * f - t_ref[...]
        dy = err * (1.0 / d)
        dy_ref[...] = dy
        df_ref[...] = (g_ref[...] * dy).astype(BF16)
        sums_ref[0] += _fold8(dy * f)
        sums_ref[1] += _fold8(err * err)

        @pl.when(i == s // tm - 1)
        def _():
            _spread_total(sums_ref)

    row = pl.BlockSpec((tm, d), lambda i: (i, 0))
    return pl.pallas_call(
        body, name=name, grid=(s // tm,), in_specs=[row, row, pl.BlockSpec((1, d), lambda i: (0, 0)), row],
        out_specs=[row, row, pl.BlockSpec((2, 8, d), lambda i: (0, 0, 0))],
        out_shape=[jax.ShapeDtypeStruct((s, d), F32), jax.ShapeDtypeStruct((s, d), BF16), jax.ShapeDtypeStruct((2, 8, d), F32)],
        compiler_params=_params("arbitrary"))(x1, ffn, gate, target)


def _adamw(w, g, m, v, *, name):
    rows, cols = w.shape
    tm = next((t for t in range(ROW_TILE, 7, -8) if rows % t == 0), rows)

    def body(w_ref, g_ref, m_ref, v_ref, d_ref, mo_ref, vo_ref):
        gv = g_ref[...]
        mn = ADAM_B1 * m_ref[...] + (1.0 - ADAM_B1) * gv
        vn = ADAM_B2 * v_ref[...] + (1.0 - ADAM_B2) * (gv * gv)
        m_hat = mn / (1.0 - ADAM_B1 ** ADAM_STEP)
        v_hat = vn / (1.0 - ADAM_B2 ** ADAM_STEP)
        d_ref[...] = -ADAM_LR * (m_hat / (jnp.sqrt(v_hat) + ADAM_EPS) + ADAM_WD * w_ref[...])
        mo_ref[...] = mn
        vo_ref[...] = vn

    blk = pl.BlockSpec((tm, cols), lambda i: (i, 0))
    return pl.pallas_call(
        body, name=name, grid=(rows // tm,), in_specs=[blk] * 4, out_specs=[blk] * 3,
        out_shape=[jax.ShapeDtypeStruct((rows, cols), F32)] * 3, compiler_params=_params("parallel"))(w, g, m, v)


def _colsum(t):
    return t[..., 0, :]


def _in_proj_layout(w_in):
    pad = jnp.zeros((w_in.shape[0], PROJ_W - C_LR - GLA_GATE_RANK), w_in.dtype)
    return jnp.concatenate([w_in[:, :1536], w_in[:, 1552:], w_in[:, 1536:1552], pad], axis=1)


def _in_proj_grad_layout(g):
    return jnp.concatenate([g[:, :1536], g[:, C_LR:C_LR + GLA_GATE_RANK], g[:, 1536:C_LR]], axis=1)


def _gate_layout(gla_w_gate):
    return jnp.pad(gla_w_gate, ((0, HEAD_LANES - GLA_GATE_RANK), (0, 0))).astype(BF16)


def _local_step(x, target, mod, wi, wo, wup, wdown, conv_w, conv_b, wg, bg, gn, qg, kg, n1g, n2g):
    d = D_MODEL
    sh1, sc1, g1, sh2, sc2, g2 = [mod[:, i * d:(i + 1) * d] for i in range(6)]
    qg8, kg8 = jnp.tile(qg, (1, 8)), jnp.tile(kg, (1, 8))

    _, h1 = _norm_mod_fwd(x, None, None, n1g, sc1, sh1, name="norm1_fwd")
    proj = _mm(h1, wi, tm=1024, tn=PROJ_W, tk=d, name="in_proj")
    o_raw, y_gla, states = _gla_fwd(proj, wg, bg, gn, name="gla_fwd")
    qa, ka = _attn_prep(proj, qg8, kg8, name="attn_prep")
    branches = [_dil_attn_fwd(qa, ka, proj, dil, name=f"attn_fwd_d{dil}") for dil in DILATIONS]
    mixed, y_att, lse = _attn_merge(branches, y_gla, name="attn_merge")
    attn_out = _mm(mixed, wo, tm=1024, tn=d, tk=d, name="out_proj")
    x1, h2 = _norm_mod_fwd(x, attn_out, g1, n2g, sc2, sh2, name="norm2_fwd")
    u = _mm(h2, wup, out_dtype=BF16, tm=1024, tn=D_FF, tk=d, name="up_proj")
    act = _conv_swiglu_fwd(u, conv_w, conv_b, name="conv_swiglu_fwd")
    ffn = _mm(act, wdown, tm=1024, tn=d, tk=D_FF, name="down_proj")
    dy, dffn, head_sums = _loss_head(x1, ffn, g2, target, name="loss_head")

    dact = _mm(dffn, wdown, tb=True, out_dtype=BF16, tm=1024, tn=D_FF, tk=d, name="down_proj_dx")
    g_wdown = _mm(act, dffn, ta=True, tm=1408, tn=d, tk=1024, name="down_proj_dw")
    duc, conv_sums = _conv_swiglu_bwd_pre(u, conv_w, conv_b, dact, name="conv_swiglu_bwd")
    du = _conv_bwd(duc, conv_w, name="conv_bwd")
    dh2 = _mm(du, wup, tb=True, tm=1024, tn=d, tk=1408, name="up_proj_dx")
    g_wup = _mm(h2, du, ta=True, tm=d, tn=1408, tk=1024, shard_cols=True, name="up_proj_dw")
    dx1, dao, n2_sums = _norm_mod_bwd(x1, dh2, dy, n2g, sc2, attn_out, g1, name="norm2_bwd")

    dmixed = _mm(dao, wo, tb=True, tm=1024, tn=d, tk=d, name="out_proj_dx")
    g_wo = _mm(mixed, dao, ta=True, tm=d, tn=d, tk=1024, name="out_proj_dw")
    dgq, dgk, dgv, dgr, dlr, g_wg, gla_sums = _gla_bwd(proj, wg, bg, gn, o_raw, states, dmixed, name="gla_bwd")
    parts = [_dil_attn_bwd(qa, ka, proj, y_att, lse, dmixed, dil, name=f"attn_bwd_d{dil}") for dil in DILATIONS]
    daq, dak, dav, qk_sums = _attn_post(parts, proj, qg8, kg8, name="attn_post")
    dproj = jnp.concatenate([dgq, dgk, dgv, dgr, daq, dak, dav, dlr], axis=1)
    dh1 = _mm(dproj, wi, tb=True, tm=1024, tn=d, tk=PROJ_W, name="in_proj_dx")
    g_wi = _mm(h1, dproj, ta=True, tm=512, tn=PROJ_W, tk=512, name="in_proj_dw")
    grad_x, _, n1_sums = _norm_mod_bwd(x, dh1, dx1, n1g, sc1, None, None, name="norm1_bwd")

    n1, n2, hs, cs = _colsum(n1_sums), _colsum(n2_sums), _colsum(head_sums), _colsum(conv_sums)
    gs, qs = _colsum(gla_sums), _colsum(qk_sums)
    dmod = jnp.concatenate([n1[1], n1[0] * n1g[0], n2[2], n2[1], n2[0] * n2g[0], hs[0]])
    small = dict(
        dmod=dmod,
        norm1_g=n1[0] * (1.0 + sc1[0]), norm2_g=n2[0] * (1.0 + sc2[0]),
        gla_w_gate=g_wg[:GLA_GATE_RANK], gla_b_gate=gs[0, :256], gla_norm_g=gs[1].reshape(4, 128).sum(axis=0),
        q_norm_g=qs[0].reshape(8, 64).sum(axis=0), k_norm_g=qs[1].reshape(8, 64).sum(axis=0),
        conv_w=jnp.concatenate([cs[0, :3], cs[1, :3]], axis=1), conv_b=jnp.concatenate([cs[0, 3], cs[1, 3]]),
    )
    return head_sums[1], grad_x, (g_wi, g_wo, g_wup, g_wdown), small


N_DEV, N_CHIP = 8, 4
ANY = pl.BlockSpec(memory_space=pl.ANY)
VMEM_SPEC = pl.BlockSpec(memory_space=pltpu.VMEM)


def _place():
    x, y, c = lax.axis_index("x"), lax.axis_index("y"), lax.axis_index("c")
    other_chips = [(1 - x, y), (x, 1 - y), (1 - x, 1 - y)]
    return x, y, c, (x, y, 1 - c), other_chips


def _all_gather_small(v, *, name):
    m, n = v.shape

    def body(v_ref, out_ref, send_sems, recv_sems, local_sem):
        x, y, c, sibling, chips = _place()
        me = (x, y, c)

        def rows(px, py, pc):
            return out_ref.at[pl.ds((4 * px + 2 * py + pc) * m, m), :]

        def copy(k, block, to, src=None):
            return pltpu.make_async_remote_copy(
                src_ref=rows(*block) if src is None else src, dst_ref=rows(*block), send_sem=send_sems.at[k],
                recv_sem=recv_sems.at[k], device_id=to, device_id_type=MESH)

        mine = pltpu.make_async_copy(v_ref, rows(*me), local_sem)
        mine.start()
        first = [copy(0, me, sibling, src=v_ref)]
        first += [copy(1 + j, me, (*chip, c), src=v_ref) for j, chip in enumerate(chips)]
        for cp in first:
            cp.start()
        passed = [copy(4 + j, (*chip, c), sibling) for j, chip in enumerate(chips)]
        for j, chip in enumerate(chips):
            copy(1 + j, (*chip, c), me).wait_recv()
            passed[j].start()
        copy(0, sibling, me).wait_recv()
        for j, chip in enumerate(chips):
            copy(4 + j, (*chip, 1 - c), me).wait_recv()
        for cp in first + passed:
            cp.wait_send()
        mine.wait()

    return pl.pallas_call(
        body, name=name, out_shape=jax.ShapeDtypeStruct((N_DEV * m, n), v.dtype), in_specs=[VMEM_SPEC], out_specs=VMEM_SPEC,
        scratch_shapes=[pltpu.SemaphoreType.DMA((7,)), pltpu.SemaphoreType.DMA((7,)), pltpu.SemaphoreType.DMA],
    )(v)


def _gather_weight_shards(shards, *, name):
    nw = len(shards)

    def body(*refs):
        srcs, outs, (send_sems, recv_sems, local_sems) = refs[:nw], refs[nw:2 * nw], refs[2 * nw:]
        x, y, c, sibling, chips = _place()
        index = lambda chip: 2 * chip[0] + chip[1]

        def copy(w, k, src, dst, to):
            return pltpu.make_async_remote_copy(src_ref=src, dst_ref=dst, send_sem=send_sems.at[6 * w + k],
                                                recv_sem=recv_sems.at[6 * w + k], device_id=to, device_id_type=MESH)

        local, sent = [], []
        for w, (src_ref, out_ref) in enumerate(zip(srcs, outs)):
            local.append(pltpu.make_async_copy(src_ref, out_ref.at[2 * x + y], local_sems.at[w]))
            local[-1].start()
            for k, chip in enumerate(chips):
                sent.append(copy(w, k, src_ref.at[c], out_ref.at[2 * x + y, c], (*chip, c)))
                sent[-1].start()
        for w, out_ref in enumerate(outs):
            for k, chip in enumerate(chips):
                landed = out_ref.at[index(chip), c]
                copy(w, k, landed, landed, (*chip, c)).wait_recv()
                sent.append(copy(w, 3 + k, landed, landed, sibling))
                sent[-1].start()
        for w, out_ref in enumerate(outs):
            for k, chip in enumerate(chips):
                passed_on = out_ref.at[index(chip), 1 - c]
                copy(w, 3 + k, passed_on, passed_on, sibling).wait_recv()
        for cp in sent:
            cp.wait_send()
        for cp in local:
            cp.wait()

    return pl.pallas_call(
        body, name=name, out_shape=[jax.ShapeDtypeStruct((N_CHIP, *s.shape), s.dtype) for s in shards],
        in_specs=[ANY] * nw, out_specs=[ANY] * nw,
        scratch_shapes=[pltpu.SemaphoreType.DMA((6 * nw,)), pltpu.SemaphoreType.DMA((6 * nw,)), pltpu.SemaphoreType.DMA((nw,))],
    )(*shards)


def _pair_exchange_halves(grads, *, name):
    nw = len(grads)

    def body(*refs):
        srcs, outs, (send_sems, recv_sems) = refs[:nw], refs[nw:2 * nw], refs[2 * nw:]
        _, _, c, sibling, _ = _place()
        cps = []
        for w, (src_ref, out_ref) in enumerate(zip(srcs, outs)):
            cps.append(pltpu.make_async_remote_copy(
                src_ref=src_ref.at[:, 1 - c], dst_ref=out_ref, send_sem=send_sems.at[w],
                recv_sem=recv_sems.at[w], device_id=sibling, device_id_type=MESH))
            cps[-1].start()
        for cp in cps:
            cp.wait()

    return pl.pallas_call(
        body, name=name, out_shape=[jax.ShapeDtypeStruct((N_CHIP, *g.shape[2:]), g.dtype) for g in grads],
        in_specs=[ANY] * nw, out_specs=[ANY] * nw,
        scratch_shapes=[pltpu.SemaphoreType.DMA((nw,)), pltpu.SemaphoreType.DMA((nw,))])(*grads)


def _chip_scatter(pairs, *, name):
    nw = len(pairs)

    def body(*refs):
        srcs, outs, (send_sems, recv_sems) = refs[:nw], refs[nw:2 * nw], refs[2 * nw:]
        _, _, c, _, chips = _place()
        cps = []
        for w, (p_ref, out_ref) in enumerate(zip(srcs, outs)):
            for k, chip in enumerate(chips):
                cps.append(pltpu.make_async_remote_copy(
                    src_ref=p_ref.at[2 * chip[0] + chip[1]], dst_ref=out_ref.at[k], send_sem=send_sems.at[3 * w + k],
                    recv_sem=recv_sems.at[3 * w + k], device_id=(*chip, c), device_id_type=MESH))
                cps[-1].start()
        for cp in cps:
            cp.wait()

    return pl.pallas_call(
        body, name=name, out_shape=[jax.ShapeDtypeStruct((3, *p.shape[1:]), p.dtype) for p in pairs],
        in_specs=[ANY] * nw, out_specs=[ANY] * nw,
        scratch_shapes=[pltpu.SemaphoreType.DMA((3 * nw,)), pltpu.SemaphoreType.DMA((3 * nw,))])(*pairs)


def _share_halves(halves, *, name):
    nw = len(halves)

    def body(*refs):
        srcs, outs, (send_sems, recv_sems, local_sems) = refs[:nw], refs[nw:2 * nw], refs[2 * nw:]
        _, _, c, sibling, _ = _place()
        cps, local = [], []
        for w, (src_ref, out_ref) in enumerate(zip(srcs, outs)):
            local.append(pltpu.make_async_copy(src_ref, out_ref.at[c], local_sems.at[w]))
            local[-1].start()
            cps.append(pltpu.make_async_remote_copy(src_ref=src_ref, dst_ref=out_ref.at[c], send_sem=send_sems.at[w],
                                                    recv_sem=recv_sems.at[w], device_id=sibling, device_id_type=MESH))
            cps[-1].start()
        for w, (src_ref, out_ref) in enumerate(zip(srcs, outs)):
            cps[w].wait_send()
            pltpu.make_async_remote_copy(src_ref=src_ref, dst_ref=out_ref.at[1 - c], send_sem=send_sems.at[w],
                                         recv_sem=recv_sems.at[w], device_id=sibling, device_id_type=MESH).wait_recv()
            local[w].wait()

    return pl.pallas_call(
        body, name=name, out_shape=[jax.ShapeDtypeStruct((2, *h.shape), h.dtype) for h in halves],
        in_specs=[ANY] * nw, out_specs=[ANY] * nw,
        scratch_shapes=[pltpu.SemaphoreType.DMA((nw,)), pltpu.SemaphoreType.DMA((nw,)), pltpu.SemaphoreType.DMA((nw,))])(*halves)


def _row_tile(rows, limit=256):
    return next(t for t in range(limit, 15, -16) if rows % t == 0)


def _pair_add(grad, got, core, *, name):
    _, r, n = grad.shape
    half = r // 2
    tr = _row_tile(half)
    nb = half // tr

    def body(core_ref, g_ref, t_ref, f_ref, b_ref):
        acc = g_ref[...] + t_ref[...]
        f_ref[...] = acc
        b_ref[...] = acc.astype(BF16)

    blk = pl.BlockSpec((1, tr, n), lambda j, i, core_ref: (j, i, 0))
    mine = pl.BlockSpec((1, tr, n), lambda j, i, core_ref: (j, core_ref[0] * nb + i, 0))
    return pl.pallas_call(
        body, name=name,
        grid_spec=pltpu.PrefetchScalarGridSpec(num_scalar_prefetch=1, grid=(N_CHIP, nb), in_specs=[mine, blk], out_specs=[blk, blk]),
        out_shape=[jax.ShapeDtypeStruct((N_CHIP, half, n), F32), jax.ShapeDtypeStruct((N_CHIP, half, n), BF16)],
        compiler_params=_params("parallel", "parallel"))(core, grad, got)


def _chip_add(pair, theirs, chip, *, name):
    _, h, n = pair.shape
    tr = _row_tile(h)

    def body(chip_ref, p_ref, t_ref, o_ref):
        o_ref[...] = ((p_ref[0] + t_ref[0].astype(F32)) + t_ref[1].astype(F32)) + t_ref[2].astype(F32)

    return pl.pallas_call(
        body, name=name,
        grid_spec=pltpu.PrefetchScalarGridSpec(
            num_scalar_prefetch=1, grid=(h // tr,),
            in_specs=[pl.BlockSpec((1, tr, n), lambda i, chip_ref: (chip_ref[0], i, 0)),
                      pl.BlockSpec((3, tr, n), lambda i, chip_ref: (0, i, 0))],
            out_specs=pl.BlockSpec((tr, n), lambda i, chip_ref: (i, 0))),
        out_shape=jax.ShapeDtypeStruct((h, n), F32), compiler_params=_params("parallel"))(chip, pair, theirs)


def _sum_devices(gathered, *, name):
    _, m, n = gathered.shape

    def body(g_ref, tot_ref, loss_ref):
        tot = g_ref[0]
        for dev in range(1, N_DEV):
            tot = tot + g_ref[dev]
        tot_ref[...] = tot
        loss_ref[...] = jnp.full((8, n), (0.5 / D_MODEL) * jnp.sum(tot[0:8]), F32)

    return pl.pallas_call(body, name=name, in_specs=[VMEM_SPEC], out_specs=[VMEM_SPEC, VMEM_SPEC],
                          out_shape=[jax.ShapeDtypeStruct((m, n), F32), jax.ShapeDtypeStruct((8, n), F32)])(gathered)


def _ada_mod(cond_all, w_ada_shard, *, name):
    tn = 512

    def body(a_ref, b_ref, o_ref):
        o_ref[...] = _nn(a_ref[...], b_ref[...], precision=HIGHEST)

    return pl.pallas_call(
        body, name=name, grid=(w_ada_shard.shape[1] // tn,),
        in_specs=[pl.BlockSpec(cond_all.shape, lambda j: (0, 0)), pl.BlockSpec((D_MODEL, tn), lambda j: (0, j))],
        out_specs=pl.BlockSpec((N_DEV, tn), lambda j: (0, j)),
        out_shape=jax.ShapeDtypeStruct((N_DEV, w_ada_shard.shape[1]), F32), compiler_params=_params("parallel"))(cond_all, w_ada_shard)


def _ada_grad(cond_all, dmod_cols, *, name):
    tm = 256

    def body(a_ref, b_ref, o_ref):
        o_ref[...] = lax.dot_general(a_ref[...], b_ref[...], (((0,), (0,)), ((), ())), precision=HIGHEST,
                                     preferred_element_type=F32)

    return pl.pallas_call(
        body, name=name, grid=(D_MODEL // tm,),
        in_specs=[pl.BlockSpec((N_DEV, tm), lambda i: (0, i)), pl.BlockSpec(dmod_cols.shape, lambda i: (0, 0))],
        out_specs=pl.BlockSpec((tm, dmod_cols.shape[1]), lambda i: (i, 0)),
        out_shape=jax.ShapeDtypeStruct((D_MODEL, dmod_cols.shape[1]), F32), compiler_params=_params("parallel"))(cond_all, dmod_cols)


def _silu_rows(c8, *, name):
    def body(c_ref, o_ref):
        cv = c_ref[...]
        o_ref[...] = cv * _sigmoid(cv)

    return pl.pallas_call(body, name=name, in_specs=[VMEM_SPEC], out_specs=VMEM_SPEC,
                          out_shape=jax.ShapeDtypeStruct(c8.shape, F32))(c8)


def _rows128(t, rows=None):
    flat = t.reshape(-1, 128)
    return flat if rows is None else jnp.pad(flat, ((0, rows - flat.shape[0]), (0, 0)))


def _from_col_shards(shards, r, n):
    return shards.reshape(N_CHIP, r, n).transpose(1, 0, 2).reshape(r, N_CHIP * n)


def kernel(x, c, w_ada, b_ada, norm1_g, w_in, gla_w_gate, gla_b_gate, gla_norm_g, q_norm_g, k_norm_g, w_out, norm2_g, w_up, conv_w, conv_b, w_down, loss_target, m_w_ada, m_b_ada, m_norm1_g, m_w_in, m_gla_w_gate, m_gla_b_gate, m_gla_norm_g, m_q_norm_g, m_k_norm_g, m_w_out, m_norm2_g, m_w_up, m_conv_w, m_conv_b, m_w_down, v_w_ada, v_b_ada, v_norm1_g, v_w_in, v_gla_w_gate, v_gla_b_gate, v_gla_norm_g, v_q_norm_g, v_k_norm_g, v_w_out, v_norm2_g, v_w_up, v_conv_w, v_conv_b, v_w_down):
    d = D_MODEL
    ax, ay, ac = lax.axis_index("x"), lax.axis_index("y"), lax.axis_index("c")
    chip, dev = 2 * ax + ay, 4 * ax + 2 * ay + ac

    cond = _silu_rows(jnp.broadcast_to(c, (8, d)), name="cond_silu")[0:1]
    small_in = jnp.concatenate([_rows128(cond), _rows128(conv_w[0]), _rows128(gla_w_gate[0])], axis=0)
    small_in = _rows128(small_in, 56)
    got = _all_gather_small(small_in, name="gather_small").reshape(N_DEV, 56, 128)
    cond_all = got[:, 0:8].reshape(N_DEV, d)
    conv_w_full = _from_col_shards(got[0::2, 8:41].reshape(N_CHIP, 3 * 1408 // 128, 128), 3, 1408)
    gate_full = _from_col_shards(got[0::2, 41:49].reshape(N_CHIP, 16 * 64 // 128, 128), GLA_GATE_RANK, 64)
    mod_part = _ada_mod(cond_all, w_ada[0], name="ada_mod")
    mod_got = _all_gather_small(_rows128(mod_part), name="gather_mod").reshape(N_DEV, N_DEV, 1536)
    mod_all = mod_got[0::2].transpose(1, 0, 2).reshape(N_DEV, 6 * d) + b_ada
    mod = lax.dynamic_slice_in_dim(mod_all, dev, 1, axis=0)

    halves = lambda w: w[0].astype(BF16).reshape(2, w.shape[1] // 2, w.shape[2])
    got_in, got_out, got_up, got_down = _gather_weight_shards([halves(w_in), halves(w_out), halves(w_up), halves(w_down)],
                                                              name="gather_weights")
    w_in_full = got_in.reshape(N_CHIP, d, 772).transpose(1, 0, 2).reshape(d, N_CHIP * 772)
    w_out_full = got_out.reshape(d, d)
    w_up_full = got_up.reshape(N_CHIP, d, 1408).transpose(1, 0, 2).reshape(d, 2 * D_FF)
    w_down_full = got_down.reshape(D_FF, d)

    err2, grad_x, (g_wi, g_wo, g_wup, g_wdown), small = _local_step(
        x[0], loss_target[0], mod, _in_proj_layout(w_in_full), w_out_full, w_up_full, w_down_full, conv_w_full, conv_b,
        _gate_layout(gate_full), gla_b_gate, gla_norm_g, q_norm_g, k_norm_g, norm1_g, norm2_g)

    pieces = [err2[0], small["dmod"], small["norm1_g"], small["norm2_g"], small["gla_w_gate"].reshape(-1), small["gla_b_gate"],
              small["gla_norm_g"], small["q_norm_g"], small["k_norm_g"], small["conv_w"].reshape(-1), small["conv_b"]]
    sizes = [p.shape[0] for p in pieces]
    at = [sum(sizes[:i]) for i in range(len(sizes) + 1)]
    vec = _rows128(jnp.concatenate(pieces), 288)
    got = _all_gather_small(vec, name="gather_grads").reshape(N_DEV, 288, 128)
    total, loss8 = _sum_devices(got, name="sum_devices")
    total = total.reshape(-1)
    seg = lambda i: total[at[i]:at[i + 1]]
    dmod_all = got.reshape(N_DEV, -1)[:, at[1]:at[2]]
    g_small = dict(
        b_ada=seg(1)[None], norm1_g=seg(2)[None], norm2_g=seg(3)[None],
        gla_w_gate=lax.dynamic_slice_in_dim(seg(4).reshape(GLA_GATE_RANK, 256), chip * 64, 64, axis=1),
        gla_b_gate=seg(5)[None], gla_norm_g=seg(6)[None], q_norm_g=seg(7)[None], k_norm_g=seg(8)[None],
        conv_w=lax.dynamic_slice_in_dim(seg(9).reshape(3, 2 * D_FF), chip * 1408, 1408, axis=1), conv_b=seg(10)[None])
    dmod_cols = lax.dynamic_slice_in_dim(dmod_all.reshape(N_DEV, 6 * d), chip * 1536, 1536, axis=1)
    g_w_ada = _ada_grad(cond_all, dmod_cols, name="ada_grad")

    tags = ("w_in", "w_out", "w_up", "w_down")
    g_parts = [_in_proj_grad_layout(g_wi).reshape(d, N_CHIP, 772).transpose(1, 0, 2), g_wo.reshape(N_CHIP, d // N_CHIP, d),
               g_wup, g_wdown.reshape(N_CHIP, D_FF // N_CHIP, d)]
    core_id, chip_id = jnp.reshape(ac, (1,)).astype(jnp.int32), jnp.reshape(chip, (1,)).astype(jnp.int32)
    got = _pair_exchange_halves([g.reshape(N_CHIP, 2, g.shape[1] // 2, g.shape[2]) for g in g_parts], name="reduce_pair")
    pairs = [_pair_add(g, t, core_id, name=f"reduce_pair_add_{tag}") for g, t, tag in zip(g_parts, got, tags)]
    theirs = _chip_scatter([pb for _, pb in pairs], name="reduce_chips")
    summed = [_chip_add(pf, t, chip_id, name=f"reduce_chips_add_{tag}") for (pf, _), t, tag in zip(pairs, theirs, tags)]
    g_big = [t.reshape(2 * t.shape[1], t.shape[2]) for t in _share_halves(summed, name="share_pair")]

    grads = dict(w_ada=g_w_ada, w_in=g_big[0], w_out=g_big[1], w_up=g_big[2], w_down=g_big[3], **g_small)
    names = ["w_ada", "b_ada", "norm1_g", "w_in", "gla_w_gate", "gla_b_gate", "gla_norm_g", "q_norm_g", "k_norm_g", "w_out",
             "norm2_g", "w_up", "conv_w", "conv_b", "w_down"]
    ws = dict(w_ada=w_ada, b_ada=b_ada, norm1_g=norm1_g, w_in=w_in, gla_w_gate=gla_w_gate, gla_b_gate=gla_b_gate,
              gla_norm_g=gla_norm_g, q_norm_g=q_norm_g, k_norm_g=k_norm_g, w_out=w_out, norm2_g=norm2_g, w_up=w_up,
              conv_w=conv_w, conv_b=conv_b, w_down=w_down)
    ms = dict(w_ada=m_w_ada, b_ada=m_b_ada, norm1_g=m_norm1_g, w_in=m_w_in, gla_w_gate=m_gla_w_gate, gla_b_gate=m_gla_b_gate,
              gla_norm_g=m_gla_norm_g, q_norm_g=m_q_norm_g, k_norm_g=m_k_norm_g, w_out=m_w_out, norm2_g=m_norm2_g, w_up=m_w_up,
              conv_w=m_conv_w, conv_b=m_conv_b, w_down=m_w_down)
    vs = dict(w_ada=v_w_ada, b_ada=v_b_ada, norm1_g=v_norm1_g, w_in=v_w_in, gla_w_gate=v_gla_w_gate, gla_b_gate=v_gla_b_gate,
              gla_norm_g=v_gla_norm_g, q_norm_g=v_q_norm_g, k_norm_g=v_k_norm_g, w_out=v_w_out, norm2_g=v_norm2_g, w_up=v_w_up,
              conv_w=v_conv_w, conv_b=v_conv_b, w_down=v_w_down)
    g_out, d_out, m_out, v_out = [], [], [], []
    for nm in names:
        w2 = ws[nm].reshape(ws[nm].shape[-2:])
        g2 = grads[nm].reshape(w2.shape)
        dl, mn, vn = _adamw(w2, g2, ms[nm].reshape(w2.shape), vs[nm].reshape(w2.shape), name=f"adamw_{nm}")
        shape = ws[nm].shape
        g_out.append(g2.reshape(shape))
        d_out.append(dl.reshape(shape))
        m_out.append(mn.reshape(shape))
        v_out.append(vn.reshape(shape))
    return (loss8[0, 0], grad_x[None], *g_out, *d_out, *m_out, *v_out)
```

```python
import functools

import jax
import jax.numpy as jnp
from jax import lax
from jax.experimental import pallas as pl
from jax.experimental.pallas import tpu as pltpu

F32, BF16 = jnp.float32, jnp.bfloat16
HIGHEST = lax.Precision.HIGHEST
MESH = pl.DeviceIdType.MESH

D_MODEL = 1024
GLA_CHUNK = 64
GLA_GATE_TAU = 16.0
GLA_GATE_RANK = 16
HEAD_LANES = 128
ATTN_BLOCK = 128
DILATIONS = (1, 4, 16)
ALIBI_SLOPES = tuple(2.0 ** (-(h + 1)) for h in range(8))
D_FF = 2816
EPS = 1e-6
C_GQ, C_GK, C_GV, C_GR, C_AQ, C_AK, C_AV, C_LR, PROJ_W = 0, 256, 512, 1024, 1536, 2048, 2560, 3072, 3200
ADAM_LR, ADAM_B1, ADAM_B2, ADAM_EPS, ADAM_WD, ADAM_STEP = 0.001, 0.9, 0.999, 1e-08, 0.01, 10
VMEM_LIMIT_BYTES = 56 * 1024 * 1024
ROW_TILE = 256


def _params(*sem):
    return pltpu.CompilerParams(dimension_semantics=sem or None, vmem_limit_bytes=VMEM_LIMIT_BYTES)


def _nt(a, b):
    return lax.dot_general(a, b, (((1,), (1,)), ((), ())), preferred_element_type=F32)


def _tn(a, b):
    return lax.dot_general(a, b, (((0,), (0,)), ((), ())), preferred_element_type=F32)


def _nn(a, b, precision=None):
    return jnp.dot(a, b, preferred_element_type=F32, precision=precision)


def _split3(v):
    hi = v.astype(BF16)
    rest = v - hi.astype(F32)
    mid = rest.astype(BF16)
    return hi, mid, (rest - mid.astype(F32)).astype(BF16)


def _sum_right(v, ones):
    hi, mid, lo = _split3(v)
    return (_nn(lo, ones) + _nn(mid, ones)) + _nn(hi, ones)


def _sum_left(ones, v):
    hi, mid, lo = _split3(v)
    return (_nn(ones, lo) + _nn(ones, mid)) + _nn(ones, hi)


def _fold8(v):
    return v.reshape(v.shape[0] // 8, 8, v.shape[1]).sum(axis=0)


def _spread_total(ref):
    t = ref[...]
    ref[...] = jnp.broadcast_to(jnp.sum(t, axis=-2, keepdims=True), t.shape)


def _sigmoid(x):
    return 1.0 / (1.0 + jnp.exp(-x))


def _mm(a, b, *, ta=False, tb=False, out_dtype=F32, tm, tn, tk, shard_cols=False, name):
    (k_a, m) = a.shape if ta else a.shape[::-1]
    (k_b, n) = b.shape[::-1] if tb else b.shape
    assert k_a == k_b and m % tm == 0 and n % tn == 0 and k_a % tk == 0, (name, a.shape, b.shape)
    nk = k_a // tk
    assert nk == 1 or out_dtype == F32, name
    dims = (((0 if ta else 1,), (1 if tb else 0,)), ((), ()))

    def body(a_ref, b_ref, o_ref):
        k = pl.program_id(2)
        part = lax.dot_general(a_ref[...].astype(BF16), b_ref[...].astype(BF16), dims, preferred_element_type=F32)
        if nk == 1:
            o_ref[...] = part.astype(out_dtype)
        else:
            @pl.when(k == 0)
            def _():
                o_ref[...] = part

            @pl.when(k > 0)
            def _():
                o_ref[...] += part

    a_spec = pl.BlockSpec((tk, tm), lambda i, j, k: (k, i)) if ta else pl.BlockSpec((tm, tk), lambda i, j, k: (i, k))
    b_spec = pl.BlockSpec((tn, tk), lambda i, j, k: (j, k)) if tb else pl.BlockSpec((tk, tn), lambda i, j, k: (k, j))
    if shard_cols:
        o_spec, o_shape = pl.BlockSpec((None, tm, tn), lambda i, j, k: (j, i, 0)), (n // tn, m, tn)
    else:
        o_spec, o_shape = pl.BlockSpec((tm, tn), lambda i, j, k: (i, j)), (m, n)
    return pl.pallas_call(
        body, name=name, grid=(m // tm, n // tn, nk), in_specs=[a_spec, b_spec], out_specs=o_spec,
        out_shape=jax.ShapeDtypeStruct(o_shape, out_dtype), compiler_params=_params("parallel", "parallel", "arbitrary"),
    )(a, b)


def _norm_mod_fwd(x, branch, gate, gain, scale, shift, *, name):
    s, d = x.shape
    tm = ROW_TILE
    has_branch = branch is not None

    def body(*refs):
        if has_branch:
            x_ref, br_ref, gate_ref, gain_ref, sc_ref, sh_ref, x1_ref, h_ref = refs
            xv = x_ref[...] + gate_ref[...] * br_ref[...]
            x1_ref[...] = xv
        else:
            x_ref, gain_ref, sc_ref, sh_ref, h_ref = refs
            xv = x_ref[...]
        r = lax.rsqrt(jnp.mean(xv * xv, axis=-1, keepdims=True) + EPS)
        h_ref[...] = ((xv * r) * gain_ref[...] * (1.0 + sc_ref[...]) + sh_ref[...]).astype(BF16)

    row = pl.BlockSpec((tm, d), lambda i: (i, 0))
    vec = pl.BlockSpec((1, d), lambda i: (0, 0))
    if has_branch:
        return pl.pallas_call(
            body, name=name, grid=(s // tm,), in_specs=[row, row, vec, vec, vec, vec], out_specs=[row, row],
            out_shape=[jax.ShapeDtypeStruct((s, d), F32), jax.ShapeDtypeStruct((s, d), BF16)],
            compiler_params=_params("parallel"))(x, branch, gate, gain, scale, shift)
    h = pl.pallas_call(
        body, name=name, grid=(s // tm,), in_specs=[row, vec, vec, vec], out_specs=row,
        out_shape=jax.ShapeDtypeStruct((s, d), BF16), compiler_params=_params("parallel"))(x, gain, scale, shift)
    return x, h


def _norm_mod_bwd(x, dh, dres, gain, scale, branch, gate, *, name):
    s, d = x.shape
    tm = ROW_TILE
    has_branch = branch is not None

    def body(*refs):
        if has_branch:
            x_ref, dh_ref, dres_ref, gain_ref, sc_ref, br_ref, gate_ref, dx_ref, dbr_ref, sums_ref = refs
        else:
            x_ref, dh_ref, dres_ref, gain_ref, sc_ref, dx_ref, sums_ref = refs
        i = pl.program_id(0)

        @pl.when(i == 0)
        def _():
            sums_ref[...] = jnp.zeros_like(sums_ref)

        xv, dhv = x_ref[...], dh_ref[...]
        r = lax.rsqrt(jnp.mean(xv * xv, axis=-1, keepdims=True) + EPS)
        xn = xv * r
        dxn = dhv * (gain_ref[...] * (1.0 + sc_ref[...]))
        dx = dres_ref[...] + r * (dxn - xn * jnp.mean(dxn * xn, axis=-1, keepdims=True))
        dx_ref[...] = dx
        sums_ref[0] += _fold8(dhv * xn)
        sums_ref[1] += _fold8(dhv)
        if has_branch:
            dbr_ref[...] = (gate_ref[...] * dx).astype(BF16)
            sums_ref[2] += _fold8(dx * br_ref[...])

        @pl.when(i == s // tm - 1)
        def _():
            _spread_total(sums_ref)

    row = pl.BlockSpec((tm, d), lambda i: (i, 0))
    vec = pl.BlockSpec((1, d), lambda i: (0, 0))
    sums = pl.BlockSpec((3, 8, d), lambda i: (0, 0, 0))
    sums_shape = jax.ShapeDtypeStruct((3, 8, d), F32)
    if has_branch:
        return pl.pallas_call(
            body, name=name, grid=(s // tm,), in_specs=[row, row, row, vec, vec, row, vec], out_specs=[row, row, sums],
            out_shape=[jax.ShapeDtypeStruct((s, d), F32), jax.ShapeDtypeStruct((s, d), BF16), sums_shape],
            compiler_params=_params("arbitrary"))(x, dh, dres, gain, scale, branch, gate)
    dx, sm = pl.pallas_call(
        body, name=name, grid=(s // tm,), in_specs=[row, row, row, vec, vec], out_specs=[row, sums],
        out_shape=[jax.ShapeDtypeStruct((s, d), F32), sums_shape],
        compiler_params=_params("arbitrary"))(x, dh, dres, gain, scale)
    return dx, None, sm


GLA_ROWS = 256


def _gla_chunk_setup(lr_ref, wg_ref, bg_ref, rows):
    c = GLA_CHUNK
    ri = lax.broadcasted_iota(jnp.int32, (c, c), 0)
    ci = lax.broadcasted_iota(jnp.int32, (c, c), 1)
    z = _nn(lr_ref[rows, :].astype(BF16), wg_ref[...]) + bg_ref[...]
    g = (jnp.minimum(z, 0.0) - jnp.log(1.0 + jnp.exp(-jnp.abs(z)))) * (1.0 / GLA_GATE_TAU)
    b = _sum_left((ci <= ri).astype(BF16), g)
    return z, b, ci <= ri


def _last_row(b):
    ri = lax.broadcasted_iota(jnp.int32, b.shape, 0)
    return jnp.sum(jnp.where(ri == b.shape[0] - 1, b, 0.0), axis=0, keepdims=True)


def _gla_fwd(proj, wg, bg, gn, *, name):
    s = proj.shape[0]
    tb, c = GLA_ROWS, GLA_CHUNK
    cb = tb // c

    def body(q_ref, k_ref, v_ref, r_ref, lr_ref, wg_ref, bg_ref, gn_ref, o_ref, y_ref, st_ref, state):
        i = pl.program_id(0)

        @pl.when(i == 0)
        def _():
            state[...] = jnp.zeros_like(state)

        low = lax.broadcasted_iota(jnp.int32, (c, HEAD_LANES), 1) < 64
        for ch in range(cb):
            rows = pl.ds(ch * c, c)
            _, b, causal = _gla_chunk_setup(lr_ref, wg_ref, bg_ref, rows)
            for p in range(2):
                cols = pl.ds(p * HEAD_LANES, HEAD_LANES)
                bp = b[:, p * HEAD_LANES:(p + 1) * HEAD_LANES]
                b_end = _last_row(bp)
                q = q_ref[rows, cols] * 0.125
                k = k_ref[rows, cols]
                q_in = q * jnp.exp(bp)
                k_out = (k * jnp.exp(-bp)).astype(BF16)
                k_end = k * jnp.exp(b_end - bp)
                st = state[p]
                st_ref[ch, p] = st
                st_b = st.astype(BF16)
                upd = jnp.zeros_like(st)
                for e in range(2):
                    msk = low if e == 0 else jnp.logical_not(low)
                    hc = pl.ds((2 * p + e) * HEAD_LANES, HEAD_LANES)
                    qm = jnp.where(msk, q_in, 0.0).astype(BF16)
                    a = jnp.where(causal, _nt(qm, k_out), 0.0)
                    v = v_ref[rows, hc].astype(BF16)
                    o = _nt(qm, st_b) + _nn(a.astype(BF16), v)
                    upd = upd + _tn(v, jnp.where(msk, k_end, 0.0).astype(BF16))
                    o_ref[rows, hc] = o
                    rr = r_ref[rows, hc]
                    on = o * lax.rsqrt(jnp.mean(o * o, axis=-1, keepdims=True) + EPS)
                    y_ref[rows, hc] = (on * gn_ref[...] * (rr * _sigmoid(rr))).astype(BF16)
                state[p] = st * jnp.exp(b_end) + upd

    def col(width, at):
        return pl.BlockSpec((tb, width), lambda i: (i, at // width))

    full = lambda shape: pl.BlockSpec(shape, lambda i: tuple(0 for _ in shape))
    return pl.pallas_call(
        body, name=name, grid=(s // tb,),
        in_specs=[col(256, C_GQ), col(256, C_GK), col(512, C_GV), col(512, C_GR), col(128, C_LR),
                  full((HEAD_LANES, 256)), full((1, 256)), full((1, HEAD_LANES))],
        out_specs=[pl.BlockSpec((tb, 512), lambda i: (i, 0)), pl.BlockSpec((tb, 512), lambda i: (i, 0)),
                   pl.BlockSpec((cb, 2, HEAD_LANES, HEAD_LANES), lambda i: (i, 0, 0, 0))],
        out_shape=[jax.ShapeDtypeStruct((s, 512), F32), jax.ShapeDtypeStruct((s, 512), BF16),
                   jax.ShapeDtypeStruct((s // c, 2, HEAD_LANES, HEAD_LANES), F32)],
        scratch_shapes=[pltpu.VMEM((2, HEAD_LANES, HEAD_LANES), F32)],
        compiler_params=_params("arbitrary"))(proj, proj, proj, proj, proj, wg, bg, gn)


def _gla_bwd(proj, wg, bg, gn, o_raw, states, dmixed, *, name):
    s = proj.shape[0]
    tb, c = GLA_ROWS, GLA_CHUNK
    cb = tb // c
    nblk, nch = s // tb, s // c

    def body(q_ref, k_ref, v_ref, r_ref, lr_ref, wg_ref, bg_ref, gn_ref, o_ref, st_ref, stn_ref, dy_ref,
             dq_ref, dk_ref, dv_ref, dr_ref, dlr_ref, gwg_ref, sums_ref, dstate):
        i = pl.program_id(0)

        @pl.when(i == 0)
        def _():
            dstate[...] = jnp.zeros_like(dstate)
            gwg_ref[...] = jnp.zeros_like(gwg_ref)
            sums_ref[...] = jnp.zeros_like(sums_ref)

        low = lax.broadcasted_iota(jnp.int32, (c, HEAD_LANES), 1) < 64
        for ch in reversed(range(cb)):
            rows = pl.ds(ch * c, c)
            z, b, causal = _gla_chunk_setup(lr_ref, wg_ref, bg_ref, rows)
            upper = jnp.logical_not(causal) | (lax.broadcasted_iota(jnp.int32, (c, c), 0)
                                               == lax.broadcasted_iota(jnp.int32, (c, c), 1))
            lr_b = lr_ref[rows, :].astype(BF16)
            dlr = jnp.zeros((c, HEAD_LANES), F32)
            for p in range(2):
                cols = pl.ds(p * HEAD_LANES, HEAD_LANES)
                sl = slice(p * HEAD_LANES, (p + 1) * HEAD_LANES)
                bp = b[:, sl]
                b_end = _last_row(bp)
                e_in, e_out, e_end = jnp.exp(bp), jnp.exp(-bp), jnp.exp(b_end - bp)
                q = q_ref[rows, cols] * 0.125
                k = k_ref[rows, cols]
                q_in = q * e_in
                k_out = k * e_out
                k_end = k * e_end
                st0 = st_ref[ch, p]
                st1 = st_ref[ch + 1, p] if ch + 1 < cb else stn_ref[0, p]
                dst = dstate[p]
                st0_b, dst_b = st0.astype(BF16), dst.astype(BF16)
                dq_in = jnp.zeros((c, HEAD_LANES), F32)
                dk_out = jnp.zeros((c, HEAD_LANES), F32)
                dk_end = jnp.zeros((c, HEAD_LANES), F32)
                dst_new = dst * jnp.exp(b_end)
                for e in range(2):
                    msk = low if e == 0 else jnp.logical_not(low)
                    hc = pl.ds((2 * p + e) * HEAD_LANES, HEAD_LANES)
                    o = o_ref[rows, hc]
                    rr = r_ref[rows, hc]
                    dy = dy_ref[rows, hc]
                    sg = _sigmoid(rr)
                    rs = lax.rsqrt(jnp.mean(o * o, axis=-1, keepdims=True) + EPS)
                    on = o * rs
                    t = dy * (rr * sg)
                    sums_ref[1, :, hc] += _fold8(t * on)
                    dn = t * gn_ref[...]
                    do = (rs * (dn - on * jnp.mean(dn * on, axis=-1, keepdims=True))).astype(BF16)
                    dr_ref[rows, hc] = (dy * on * gn_ref[...] * (sg * (1.0 + rr * (1.0 - sg)))).astype(BF16)
                    qm = jnp.where(msk, q_in, 0.0).astype(BF16)
                    km_out = jnp.where(msk, k_out, 0.0).astype(BF16)
                    km_end = jnp.where(msk, k_end, 0.0).astype(BF16)
                    v = v_ref[rows, hc].astype(BF16)
                    a = jnp.where(causal, _nt(qm, km_out), 0.0).astype(BF16)
                    da = jnp.where(causal, _nt(do, v), 0.0).astype(BF16)
                    dv_ref[rows, hc] = (_tn(a, do) + _nt(km_end, dst_b)).astype(BF16)
                    dq_in = dq_in + jnp.where(msk, _nn(do, st0_b) + _nn(da, km_out), 0.0)
                    dk_out = dk_out + _tn(da, qm)
                    dk_end = dk_end + jnp.where(msk, _nn(v, dst_b), 0.0)
                    dst_new = dst_new + _tn(do, qm)
                dq = dq_in * e_in
                dk = dk_out * e_out + dk_end * e_end
                dq_ref[rows, cols] = (dq * 0.125).astype(BF16)
                dk_ref[rows, cols] = dk.astype(BF16)
                w = q * dq - k * dk
                dg = _sum_left(upper.astype(BF16), w) + jnp.sum(dst * st1, axis=0, keepdims=True)
                zp = z[:, sl]
                dz = dg * (1.0 / GLA_GATE_TAU) * _sigmoid(-zp)
                dz_b = dz.astype(BF16)
                sums_ref[0, :, cols] += _fold8(dz)
                dlr = dlr + _nt(dz_b, wg_ref[:, cols])
                gwg_ref[:, cols] += _tn(lr_b, dz_b)
                dstate[p] = dst_new
            dlr_ref[rows, :] = dlr.astype(BF16)

        @pl.when(i == nblk - 1)
        def _():
            _spread_total(sums_ref)

    rev = lambda i: nblk - 1 - i

    def col(width, at):
        return pl.BlockSpec((tb, width), lambda i: (rev(i), at // width))

    full = lambda shape: pl.BlockSpec(shape, lambda i: tuple(0 for _ in shape))
    out_col = lambda width: pl.BlockSpec((tb, width), lambda i: (rev(i), 0))
    return pl.pallas_call(
        body, name=name, grid=(nblk,),
        in_specs=[col(256, C_GQ), col(256, C_GK), col(512, C_GV), col(512, C_GR), col(128, C_LR),
                  full((HEAD_LANES, 256)), full((1, 256)), full((1, HEAD_LANES)),
                  pl.BlockSpec((tb, 512), lambda i: (rev(i), 0)),
                  pl.BlockSpec((cb, 2, HEAD_LANES, HEAD_LANES), lambda i: (rev(i), 0, 0, 0)),
                  pl.BlockSpec((1, 2, HEAD_LANES, HEAD_LANES), lambda i: (jnp.minimum((rev(i) + 1) * cb, nch - 1), 0, 0, 0)),
                  pl.BlockSpec((tb, 512), lambda i: (rev(i), 0))],
        out_specs=[out_col(256), out_col(256), out_col(512), out_col(512), out_col(128),
                   full((HEAD_LANES, 256)), full((2, 8, 512))],
        out_shape=[jax.ShapeDtypeStruct((s, 256), BF16), jax.ShapeDtypeStruct((s, 256), BF16),
                   jax.ShapeDtypeStruct((s, 512), BF16), jax.ShapeDtypeStruct((s, 512), BF16),
                   jax.ShapeDtypeStruct((s, 128), BF16), jax.ShapeDtypeStruct((HEAD_LANES, 256), F32),
                   jax.ShapeDtypeStruct((2, 8, 512), F32)],
        scratch_shapes=[pltpu.VMEM((2, HEAD_LANES, HEAD_LANES), F32)],
        compiler_params=_params("arbitrary"))(proj, proj, proj, proj, proj, wg, bg, gn, o_raw, states, states, dmixed)


def _head_sum_matrix():
    ri = lax.broadcasted_iota(jnp.int32, (512, 512), 0) // 64
    ci = lax.broadcasted_iota(jnp.int32, (512, 512), 1) // 64
    return (ri == ci).astype(BF16)


def _attn_prep(proj, qg, kg, *, name):
    s = proj.shape[0]
    tm = ROW_TILE

    def body(q_ref, k_ref, qg_ref, kg_ref, qa_ref, ka_ref):
        hs = _head_sum_matrix()
        q, k = q_ref[...], k_ref[...]
        qr = lax.rsqrt(_sum_right(q * q, hs) * (1.0 / 64) + EPS)
        kr = lax.rsqrt(_sum_right(k * k, hs) * (1.0 / 64) + EPS)
        qa_ref[...] = q * qr * qg_ref[...] * 0.125
        ka_ref[...] = k * kr * kg_ref[...]

    col = lambda at: pl.BlockSpec((tm, 512), lambda i: (i, at // 512))
    vec = pl.BlockSpec((1, 512), lambda i: (0, 0))
    out = pl.BlockSpec((tm, 512), lambda i: (i, 0))
    return pl.pallas_call(
        body, name=name, grid=(s // tm,), in_specs=[col(C_AQ), col(C_AK), vec, vec], out_specs=[out] * 2,
        out_shape=[jax.ShapeDtypeStruct((s, 512), F32)] * 2, compiler_params=_params("parallel"))(proj, proj, qg, kg)


FAR = 1e30


def _attn_distance(first):
    blk = ATTN_BLOCK
    iq = lax.broadcasted_iota(jnp.int32, (2 * blk, 2 * blk), 0) & (blk - 1)
    ik = lax.broadcasted_iota(jnp.int32, (2 * blk, 2 * blk), 1)
    rel = iq + blk - ik
    valid = (rel >= 0) & (rel <= blk) & (jnp.logical_not(first) | (ik >= blk))
    return jnp.where(valid, rel.astype(F32), FAR)


def _stack_heads(t2):
    low = lax.broadcasted_iota(jnp.int32, t2.shape, 1) < 64
    return jnp.concatenate([jnp.where(low, t2, 0.0), jnp.where(low, 0.0, t2)], axis=0).astype(BF16)


def _unstack_heads(t):
    blk = ATTN_BLOCK
    low = lax.broadcasted_iota(jnp.int32, (blk, HEAD_LANES), 1) < 64
    return jnp.where(low, t[0:blk], t[blk:2 * blk])


def _attn_scores(qs, kcat, slopes, dil, dist):
    top = lax.broadcasted_iota(jnp.int32, (2 * ATTN_BLOCK, 1), 0) < ATTN_BLOCK
    return _nt(qs, kcat) - jnp.where(top, slopes[0] * dil, slopes[1] * dil) * dist


def _pair_slopes(p):
    if isinstance(p, int):
        return ALIBI_SLOPES[2 * p], ALIBI_SLOPES[2 * p + 1]
    pick = lambda e: jnp.where(p == 0, ALIBI_SLOPES[e], jnp.where(p == 1, ALIBI_SLOPES[2 + e],
                               jnp.where(p == 2, ALIBI_SLOPES[4 + e], ALIBI_SLOPES[6 + e])))
    return pick(0), pick(1)


def _attn_pair_fwd(q2, kcat, vcat, slopes, dil, dist):
    sc = _attn_scores(_stack_heads(q2), kcat, slopes, dil, dist)
    m = jnp.max(sc, axis=-1, keepdims=True)
    pr = jnp.exp(sc - m)
    den = jnp.sum(pr, axis=-1, keepdims=True)
    o = _nn(pr.astype(BF16), vcat) / den
    lse = jnp.broadcast_to(m + jnp.log(den), o.shape)
    return _unstack_heads(o), _unstack_heads(lse)


def _attn_pair_bwd(q2, kcat, vcat, do2, y2, lse2, slopes, dil, dist):
    lane = lax.broadcasted_iota(jnp.int32, (ATTN_BLOCK, HEAD_LANES), 1)
    low = lane < 64
    prod = do2 * y2
    per_head = lambda t, pick: jnp.concatenate([jnp.sum(jnp.where(pick(0), t, 0.0), axis=-1, keepdims=True),
                                                jnp.sum(jnp.where(pick(1), t, 0.0), axis=-1, keepdims=True)], axis=0)
    lse = per_head(lse2, lambda e: lane == 64 * e)
    delta = per_head(prod, lambda e: low if e == 0 else jnp.logical_not(low))
    qs, dos = _stack_heads(q2), _stack_heads(do2)
    pr = jnp.exp(_attn_scores(qs, kcat, slopes, dil, dist) - lse)
    ds = (pr * (_nt(dos, vcat) - delta)).astype(BF16)
    return _unstack_heads(_nn(ds, kcat)), _tn(ds, qs), _tn(pr.astype(BF16), dos)


def _attn_specs(dil):
    rows = ATTN_BLOCK * dil
    if dil == 1:
        cur = lambda at: pl.BlockSpec((rows, 512), lambda n: (n, at // 512))
        prev = lambda at: pl.BlockSpec((rows, 512), lambda n: (jnp.maximum(n - 1, 0), at // 512))
    else:
        cur = lambda at: pl.BlockSpec((rows, HEAD_LANES), lambda n, p: (n, at // HEAD_LANES + p))
        prev = lambda at: pl.BlockSpec((rows, HEAD_LANES), lambda n, p: (jnp.maximum(n - 1, 0), at // HEAD_LANES + p))
    return cur, prev


def _attn_loop(dil, one_pair):
    if dil == 1:
        for p in range(4):
            one_pair(slice(None), pl.ds(p * HEAD_LANES, HEAD_LANES), p)
    else:
        p = pl.program_id(1)

        def step(r, carry):
            one_pair(pl.ds(r, ATTN_BLOCK, stride=dil), slice(None), p)
            return carry

        lax.fori_loop(0, dil, step, 0, unroll=min(dil, 4))


def _dil_attn_fwd(qa, ka, proj, dil, *, name):
    s = qa.shape[0]

    def body(q_ref, kp_ref, kc_ref, vp_ref, vc_ref, o_ref, lse_ref):
        dist = _attn_distance(pl.program_id(0) == 0)

        def one_pair(rows, cols, p):
            kcat = jnp.concatenate([kp_ref[rows, cols], kc_ref[rows, cols]], axis=0).astype(BF16)
            vcat = jnp.concatenate([vp_ref[rows, cols], vc_ref[rows, cols]], axis=0).astype(BF16)
            o2, lse2 = _attn_pair_fwd(q_ref[rows, cols], kcat, vcat, _pair_slopes(p), dil, dist)
            o_ref[rows, cols] = o2
            lse_ref[rows, cols] = lse2

        _attn_loop(dil, one_pair)

    cur, prev = _attn_specs(dil)
    grid = (s // ATTN_BLOCK,) if dil == 1 else (s // (ATTN_BLOCK * dil), 4)
    return pl.pallas_call(
        body, name=name, grid=grid, in_specs=[cur(0), prev(0), cur(0), prev(C_AV), cur(C_AV)], out_specs=[cur(0), cur(0)],
        out_shape=[jax.ShapeDtypeStruct((s, 512), F32)] * 2,
        compiler_params=_params(*["parallel"] * len(grid)))(qa, ka, ka, proj, proj)


def _attn_merge(branches, y_gla, *, name):
    s = y_gla.shape[0]
    tm = ROW_TILE

    def body(o0, l0, o1, l1, o2, l2, yg_ref, mixed_ref, y_ref, lse_ref):
        m = jnp.maximum(jnp.maximum(l0[...], l1[...]), l2[...])
        w0, w1, w2 = jnp.exp(l0[...] - m), jnp.exp(l1[...] - m), jnp.exp(l2[...] - m)
        zs = w0 + w1 + w2
        y = (w0 * o0[...] + w1 * o1[...] + w2 * o2[...]) / zs
        y_ref[...] = y
        lse_ref[...] = m + jnp.log(zs)
        mixed_ref[:, 0:512] = yg_ref[...]
        mixed_ref[:, 512:1024] = y.astype(BF16)

    blk = pl.BlockSpec((tm, 512), lambda i: (i, 0))
    args = [t for pair in branches for t in pair]
    return pl.pallas_call(
        body, name=name, grid=(s // tm,), in_specs=[blk] * 7,
        out_specs=[pl.BlockSpec((tm, 1024), lambda i: (i, 0)), blk, blk],
        out_shape=[jax.ShapeDtypeStruct((s, 1024), BF16), jax.ShapeDtypeStruct((s, 512), F32),
                   jax.ShapeDtypeStruct((s, 512), F32)],
        compiler_params=_params("parallel"))(*args, y_gla)


def _dil_attn_bwd(qa, ka, proj, y_att, lse, dmixed, dil, *, name):
    s = qa.shape[0]
    blk = ATTN_BLOCK

    def body(q_ref, kp_ref, kc_ref, vp_ref, vc_ref, y_ref, lse_ref, do_ref, dq_ref, dkc_ref, dkp_ref, dvc_ref, dvp_ref):
        dist = _attn_distance(pl.program_id(0) == 0)

        def one_pair(rows, cols, p):
            kcat = jnp.concatenate([kp_ref[rows, cols], kc_ref[rows, cols]], axis=0).astype(BF16)
            vcat = jnp.concatenate([vp_ref[rows, cols], vc_ref[rows, cols]], axis=0).astype(BF16)
            dq, dk, dv = _attn_pair_bwd(q_ref[rows, cols], kcat, vcat, do_ref[rows, cols], y_ref[rows, cols],
                                        lse_ref[rows, cols], _pair_slopes(p), dil, dist)
            dq_ref[rows, cols] = dq
            dkp_ref[rows, cols] = dk[0:blk]
            dkc_ref[rows, cols] = dk[blk:2 * blk]
            dvp_ref[rows, cols] = dv[0:blk]
            dvc_ref[rows, cols] = dv[blk:2 * blk]

        _attn_loop(dil, one_pair)

    cur, prev = _attn_specs(dil)
    grid = (s // blk,) if dil == 1 else (s // (blk * dil), 4)
    return pl.pallas_call(
        body, name=name, grid=grid,
        in_specs=[cur(0), prev(0), cur(0), prev(C_AV), cur(C_AV), cur(0), cur(0), cur(512)], out_specs=[cur(0)] * 5,
        out_shape=[jax.ShapeDtypeStruct((s, 512), F32)] * 5, compiler_params=_params(*["parallel"] * len(grid)),
    )(qa, ka, ka, proj, proj, y_att, lse, dmixed)


def _attn_post(parts, proj, qg, kg, *, name):
    s = proj.shape[0]
    tm = ATTN_BLOCK
    nblk = s // tm

    def body(*refs):
        ins, (q_ref, k_ref, qg_ref, kg_ref, dq_out, dk_out, dv_out, sums_ref) = refs[:15], refs[15:]
        i = pl.program_id(0)

        @pl.when(i == 0)
        def _():
            sums_ref[...] = jnp.zeros_like(sums_ref)

        dq = jnp.zeros((tm, 512), F32)
        dk = jnp.zeros((tm, 512), F32)
        dv = jnp.zeros((tm, 512), F32)
        for g, dil in enumerate(DILATIONS):
            dq_r, dkc_r, dkp_r, dvc_r, dvp_r = ins[5 * g:5 * g + 5]
            inside = (i + dil < nblk).astype(F32)
            dq = dq + dq_r[...]
            dk = dk + dkc_r[...] + inside * dkp_r[...]
            dv = dv + dvc_r[...] + inside * dvp_r[...]
        dv_out[...] = dv.astype(BF16)
        hs = _head_sum_matrix()
        for row, (x_ref, g_ref, dy, out, post) in enumerate(((q_ref, qg_ref, dq, dq_out, 0.125), (k_ref, kg_ref, dk, dk_out, 1.0))):
            x = x_ref[...]
            rs = lax.rsqrt(_sum_right(x * x, hs) * (1.0 / 64) + EPS)
            xn = x * rs
            dy = dy * post
            sums_ref[row] += _fold8(dy * xn)
            dn = dy * g_ref[...]
            out[...] = (rs * (dn - xn * (_sum_right(dn * xn, hs) * (1.0 / 64)))).astype(BF16)

        @pl.when(i == nblk - 1)
        def _():
            _spread_total(sums_ref)

    here = pl.BlockSpec((tm, 512), lambda i: (i, 0))
    specs = []
    for dil in DILATIONS:
        later = pl.BlockSpec((tm, 512), lambda i, dil=dil: (jnp.minimum(i + dil, nblk - 1), 0))
        specs += [here, here, later, here, later]
    col = lambda at: pl.BlockSpec((tm, 512), lambda i: (i, at // 512))
    vec = pl.BlockSpec((1, 512), lambda i: (0, 0))
    return pl.pallas_call(
        body, name=name, grid=(nblk,), in_specs=specs + [col(C_AQ), col(C_AK), vec, vec],
        out_specs=[here, here, here, pl.BlockSpec((2, 8, 512), lambda i: (0, 0, 0))],
        out_shape=[jax.ShapeDtypeStruct((s, 512), BF16)] * 3 + [jax.ShapeDtypeStruct((2, 8, 512), F32)],
        compiler_params=_params("arbitrary"))(*[t for part in parts for t in part], proj, proj, qg, kg)


FFN_TM, FFN_TN = 256, 1408
HALO = 16


def _conv3(u_ref, halo_ref, w_ref, b_ref, first):
    u = u_ref[...].astype(F32)
    ext = jnp.concatenate([jnp.where(first, 0.0, halo_ref[...].astype(F32)), u], axis=0)
    u1 = pltpu.roll(ext, 1, 0)[HALO:]
    u2 = pltpu.roll(ext, 2, 0)[HALO:]
    return b_ref[...] + w_ref[0:1, :] * u2 + w_ref[1:2, :] * u1 + w_ref[2:3, :] * u, u, u1, u2


def _ffn_specs(tm, tn):
    nj = D_FF // tn
    blk = lambda half: pl.BlockSpec((tm, tn), lambda j, i: (i, j + half * nj))
    halo = lambda half: pl.BlockSpec((HALO, tn), lambda j, i: (jnp.maximum(i * (tm // HALO) - 1, 0), j + half * nj))
    wspec = lambda half: pl.BlockSpec((3, tn), lambda j, i: (0, j + half * nj))
    bspec = lambda half: pl.BlockSpec((1, tn), lambda j, i: (0, j + half * nj))
    return [blk(0), halo(0), blk(1), halo(1), wspec(0), wspec(1), bspec(0), bspec(1)]


def _conv_swiglu_fwd(u, conv_w, conv_b, *, name):
    s = u.shape[0]
    tm, tn = FFN_TM, FFN_TN

    def body(ug_ref, hg_ref, uv_ref, hv_ref, wg_ref, wv_ref, bg_ref, bv_ref, act_ref):
        first = pl.program_id(1) == 0
        cg = _conv3(ug_ref, hg_ref, wg_ref, bg_ref, first)[0]
        cv = _conv3(uv_ref, hv_ref, wv_ref, bv_ref, first)[0]
        act_ref[...] = (cg * _sigmoid(cg) * cv).astype(BF16)

    return pl.pallas_call(
        body, name=name, grid=(D_FF // tn, s // tm), in_specs=_ffn_specs(tm, tn),
        out_specs=pl.BlockSpec((tm, tn), lambda j, i: (i, j)), out_shape=jax.ShapeDtypeStruct((s, D_FF), BF16),
        compiler_params=_params("parallel", "parallel"))(u, u, u, u, conv_w, conv_w, conv_b, conv_b)


def _conv_swiglu_bwd_pre(u, conv_w, conv_b, dact, *, name):
    s = u.shape[0]
    tm, tn = FFN_TM, FFN_TN

    def body(ug_ref, hg_ref, uv_ref, hv_ref, wg_ref, wv_ref, bg_ref, bv_ref, da_ref, duc_ref, sums_ref):
        i = pl.program_id(1)

        @pl.when(i == 0)
        def _():
            sums_ref[...] = jnp.zeros_like(sums_ref)

        cg, g0, g1, g2 = _conv3(ug_ref, hg_ref, wg_ref, bg_ref, i == 0)
        cv, v0, v1, v2 = _conv3(uv_ref, hv_ref, wv_ref, bv_ref, i == 0)
        da = da_ref[...].astype(F32)
        sg = _sigmoid(cg)
        dg = da * cv * (sg * (1.0 + cg * (1.0 - sg)))
        dv = da * (cg * sg)
        duc_ref[0] = dg.astype(BF16)
        duc_ref[1] = dv.astype(BF16)
        for half, (d, taps) in enumerate(((dg, (g2, g1, g0)), (dv, (v2, v1, v0)))):
            for t, tap in enumerate(taps):
                sums_ref[half, t] += _fold8(d * tap)
            sums_ref[half, 3] += _fold8(d)

        @pl.when(i == s // tm - 1)
        def _():
            _spread_total(sums_ref)

    return pl.pallas_call(
        body, name=name, grid=(D_FF // tn, s // tm),
        in_specs=_ffn_specs(tm, tn) + [pl.BlockSpec((tm, tn), lambda j, i: (i, j))],
        out_specs=[pl.BlockSpec((2, tm, tn), lambda j, i: (0, i, j)), pl.BlockSpec((2, 4, 8, tn), lambda j, i: (0, 0, 0, j))],
        out_shape=[jax.ShapeDtypeStruct((2, s, D_FF), BF16), jax.ShapeDtypeStruct((2, 4, 8, D_FF), F32)],
        compiler_params=_params("parallel", "arbitrary"))(u, u, u, u, conv_w, conv_w, conv_b, conv_b, dact)


def _conv_bwd(duc, conv_w, *, name):
    _, s, _ = duc.shape
    tm, tn = FFN_TM, FFN_TN
    nj, ni = D_FF // tn, s // tm

    def body(d_ref, halo_ref, w_ref, du_ref):
        last = pl.program_id(2) == ni - 1
        d = d_ref[0].astype(F32)
        ext = jnp.concatenate([d, jnp.where(last, 0.0, halo_ref[0].astype(F32))], axis=0)
        n = tm + HALO
        d1 = pltpu.roll(ext, n - 1, 0)[:tm]
        d2 = pltpu.roll(ext, n - 2, 0)[:tm]
        du_ref[...] = (w_ref[2:3, :] * d + w_ref[1:2, :] * d1 + w_ref[0:1, :] * d2).astype(BF16)

    return pl.pallas_call(
        body, name=name, grid=(2, nj, ni),
        in_specs=[pl.BlockSpec((1, tm, tn), lambda g, j, i: (g, i, j)),
                  pl.BlockSpec((1, HALO, tn), lambda g, j, i: (g, jnp.minimum((i + 1) * (tm // HALO), s // HALO - 1), j)),
                  pl.BlockSpec((3, tn), lambda g, j, i: (0, g * nj + j))],
        out_specs=pl.BlockSpec((tm, tn), lambda g, j, i: (i, g * nj + j)),
        out_shape=jax.ShapeDtypeStruct((s, 2 * D_FF), BF16),
        compiler_params=_params("parallel", "parallel", "parallel"))(duc, duc, conv_w)


def _loss_head(x1, ffn, gate, target, *, name):
    s, d = x1.shape
    tm = ROW_TILE

    def body(x_ref, f_ref, g_ref, t_ref, dy_ref, df_ref, sums_ref):
        i = pl.program_id(0)

        @pl.when(i == 0)
        def _():
            sums_ref[...] = jnp.zeros_like(sums_ref)

        f = f_ref[...]
        err = x_ref[...] + g_ref[...] * f - t_ref[...]
        dy = err * (1.0 / d)
        dy_ref[...] = dy
        df_ref[...] = (g_ref[...] * dy).astype(BF16)
        sums_ref[0] += _fold8(dy * f)
        sums_ref[1] += _fold8(err * err)

        @pl.when(i == s // tm - 1)
        def _():
            _spread_total(sums_ref)

    row = pl.BlockSpec((tm, d), lambda i: (i, 0))
    return pl.pallas_call(
        body, name=name, grid=(s // tm,), in_specs=[row, row, pl.BlockSpec((1, d), lambda i: (0, 0)), row],
        out_specs=[row, row, pl.BlockSpec((2, 8, d), lambda i: (0, 0, 0))],
        out_shape=[jax.ShapeDtypeStruct((s, d), F32), jax.ShapeDtypeStruct((s, d), BF16), jax.ShapeDtypeStruct((2, 8, d), F32)],
        compiler_params=_params("arbitrary"))(x1, ffn, gate, target)


def _adamw(w, g, m, v, *, name):
    rows, cols = w.shape
    tm = next((t for t in range(ROW_TILE, 7, -8) if rows % t == 0), rows)

    def body(w_ref, g_ref, m_ref, v_ref, d_ref, mo_ref, vo_ref):
        gv = g_ref[...]
        mn = ADAM_B1 * m_ref[...] + (1.0 - ADAM_B1) * gv
        vn = ADAM_B2 * v_ref[...] + (1.0 - ADAM_B2) * (gv * gv)
        m_hat = mn / (1.0 - ADAM_B1 ** ADAM_STEP)
        v_hat = vn / (1.0 - ADAM_B2 ** ADAM_STEP)
        d_ref[...] = -ADAM_LR * (m_hat / (jnp.sqrt(v_hat) + ADAM_EPS) + ADAM_WD * w_ref[...])
        mo_ref[...] = mn
        vo_ref[...] = vn

    blk = pl.BlockSpec((tm, cols), lambda i: (i, 0))
    return pl.pallas_call(
        body, name=name, grid=(rows // tm,), in_specs=[blk] * 4, out_specs=[blk] * 3,
        out_shape=[jax.ShapeDtypeStruct((rows, cols), F32)] * 3, compiler_params=_params("parallel"))(w, g, m, v)


def _colsum(t):
    return t[..., 0, :]


def _in_proj_layout(w_in):
    pad = jnp.zeros((w_in.shape[0], PROJ_W - C_LR - GLA_GATE_RANK), w_in.dtype)
    return jnp.concatenate([w_in[:, :1536], w_in[:, 1552:], w_in[:, 1536:1552], pad], axis=1)


def _in_proj_grad_layout(g):
    return jnp.concatenate([g[:, :1536], g[:, C_LR:C_LR + GLA_GATE_RANK], g[:, 1536:C_LR]], axis=1)


def _gate_layout(gla_w_gate):
    return jnp.pad(gla_w_gate, ((0, HEAD_LANES - GLA_GATE_RANK), (0, 0))).astype(BF16)


def _local_step(x, target, mod, wi, wo, wup, wdown, conv_w, conv_b, wg, bg, gn, qg, kg, n1g, n2g):
    d = D_MODEL
    sh1, sc1, g1, sh2, sc2, g2 = [mod[:, i * d:(i + 1) * d] for i in range(6)]
    qg8, kg8 = jnp.tile(qg, (1, 8)), jnp.tile(kg, (1, 8))

    _, h1 = _norm_mod_fwd(x, None, None, n1g, sc1, sh1, name="norm1_fwd")
    proj = _mm(h1, wi, tm=1024, tn=PROJ_W, tk=d, name="in_proj")
    o_raw, y_gla, states = _gla_fwd(proj, wg, bg, gn, name="gla_fwd")
    qa, ka = _attn_prep(proj, qg8, kg8, name="attn_prep")
    branches = [_dil_attn_fwd(qa, ka, proj, dil, name=f"attn_fwd_d{dil}") for dil in DILATIONS]
    mixed, y_att, lse = _attn_merge(branches, y_gla, name="attn_merge")
    attn_out = _mm(mixed, wo, tm=1024, tn=d, tk=d, name="out_proj")
    x1, h2 = _norm_mod_fwd(x, attn_out, g1, n2g, sc2, sh2, name="norm2_fwd")
    u = _mm(h2, wup, out_dtype=BF16, tm=1024, tn=D_FF, tk=d, name="up_proj")
    act = _conv_swiglu_fwd(u, conv_w, conv_b, name="conv_swiglu_fwd")
    ffn = _mm(act, wdown, tm=1024, tn=d, tk=D_FF, name="down_proj")
    dy, dffn, head_sums = _loss_head(x1, ffn, g2, target, name="loss_head")

    dact = _mm(dffn, wdown, tb=True, out_dtype=BF16, tm=1024, tn=D_FF, tk=d, name="down_proj_dx")
    g_wdown = _mm(act, dffn, ta=True, tm=1408, tn=d, tk=1024, name="down_proj_dw")
    duc, conv_sums = _conv_swiglu_bwd_pre(u, conv_w, conv_b, dact, name="conv_swiglu_bwd")
    du = _conv_bwd(duc, conv_w, name="conv_bwd")
    dh2 = _mm(du, wup, tb=True, tm=1024, tn=d, tk=1408, name="up_proj_dx")
    g_wup = _mm(h2, du, ta=True, tm=d, tn=1408, tk=1024, shard_cols=True, name="up_proj_dw")
    dx1, dao, n2_sums = _norm_mod_bwd(x1, dh2, dy, n2g, sc2, attn_out, g1, name="norm2_bwd")

    dmixed = _mm(dao, wo, tb=True, tm=1024, tn=d, tk=d, name="out_proj_dx")
    g_wo = _mm(mixed, dao, ta=True, tm=d, tn=d, tk=1024, name="out_proj_dw")
    dgq, dgk, dgv, dgr, dlr, g_wg, gla_sums = _gla_bwd(proj, wg, bg, gn, o_raw, states, dmixed, name="gla_bwd")
    parts = [_dil_attn_bwd(qa, ka, proj, y_att, lse, dmixed, dil, name=f"attn_bwd_d{dil}") for dil in DILATIONS]
    daq, dak, dav, qk_sums = _attn_post(parts, proj, qg8, kg8, name="attn_post")
    dproj = jnp.concatenate([dgq, dgk, dgv, dgr, daq, dak, dav, dlr], axis=1)
    dh1 = _mm(dproj, wi, tb=True, tm=1024, tn=d, tk=PROJ_W, name="in_proj_dx")
    g_wi = _mm(h1, dproj, ta=True, tm=512, tn=PROJ_W, tk=512, name="in_proj_dw")
    grad_x, _, n1_sums = _norm_mod_bwd(x, dh1, dx1, n1g, sc1, None, None, name="norm1_bwd")

    n1, n2, hs, cs = _colsum(n1_sums), _colsum(n2_sums), _colsum(head_sums), _colsum(conv_sums)
    gs, qs = _colsum(gla_sums), _colsum(qk_sums)
    dmod = jnp.concatenate([n1[1], n1[0] * n1g[0], n2[2], n2[1], n2[0] * n2g[0], hs[0]])
    small = dict(
        dmod=dmod,
        norm1_g=n1[0] * (1.0 + sc1[0]), norm2_g=n2[0] * (1.0 + sc2[0]),
        gla_w_gate=g_wg[:GLA_GATE_RANK], gla_b_gate=gs[0, :256], gla_norm_g=gs[1].reshape(4, 128).sum(axis=0),
        q_norm_g=qs[0].reshape(8, 64).sum(axis=0), k_norm_g=qs[1].reshape(8, 64).sum(axis=0),
        conv_w=jnp.concatenate([cs[0, :3], cs[1, :3]], axis=1), conv_b=jnp.concatenate([cs[0, 3], cs[1, 3]]),
    )
    return head_sums[1], grad_x, (g_wi, g_wo, g_wup, g_wdown), small


N_DEV, N_CHIP = 8, 4
ANY = pl.BlockSpec(memory_space=pl.ANY)
VMEM_SPEC = pl.BlockSpec(memory_space=pltpu.VMEM)


def _place():
    x, y, c = lax.axis_index("x"), lax.axis_index("y"), lax.axis_index("c")
    other_chips = [(1 - x, y), (x, 1 - y), (1 - x, 1 - y)]
    return x, y, c, (x, y, 1 - c), other_chips


def _all_gather_small(v, *, name):
    m, n = v.shape

    def body(v_ref, out_ref, send_sems, recv_sems, local_sem):
        x, y, c, sibling, chips = _place()
        me = (x, y, c)

        def rows(px, py, pc):
            return out_ref.at[pl.ds((4 * px + 2 * py + pc) * m, m), :]

        def copy(k, block, to, src=None):
            return pltpu.make_async_remote_copy(
                src_ref=rows(*block) if src is None else src, dst_ref=rows(*block), send_sem=send_sems.at[k],
                recv_sem=recv_sems.at[k], device_id=to, device_id_type=MESH)

        mine = pltpu.make_async_copy(v_ref, rows(*me), local_sem)
        mine.start()
        first = [copy(0, me, sibling, src=v_ref)]
        first += [copy(1 + j, me, (*chip, c), src=v_ref) for j, chip in enumerate(chips)]
        for cp in first:
            cp.start()
        passed = [copy(4 + j, (*chip, c), sibling) for j, chip in enumerate(chips)]
        for j, chip in enumerate(chips):
            copy(1 + j, (*chip, c), me).wait_recv()
            passed[j].start()
        copy(0, sibling, me).wait_recv()
        for j, chip in enumerate(chips):
            copy(4 + j, (*chip, 1 - c), me).wait_recv()
        for cp in first + passed:
            cp.wait_send()
        mine.wait()

    return pl.pallas_call(
        body, name=name, out_shape=jax.ShapeDtypeStruct((N_DEV * m, n), v.dtype), in_specs=[VMEM_SPEC], out_specs=VMEM_SPEC,
        scratch_shapes=[pltpu.SemaphoreType.DMA((7,)), pltpu.SemaphoreType.DMA((7,)), pltpu.SemaphoreType.DMA],
    )(v)


def _gather_weight_shards(shards, *, name):
    nw = len(shards)

    def body(*refs):
        srcs, outs, (send_sems, recv_sems) = refs[:nw], refs[nw:2 * nw], refs[2 * nw:]
        x, y, c, sibling, chips = _place()
        index = lambda chip: 2 * chip[0] + chip[1]

        def copy(w, k, src, dst, to):
            return pltpu.make_async_remote_copy(src_ref=src, dst_ref=dst, send_sem=send_sems.at[6 * w + k],
                                                recv_sem=recv_sems.at[6 * w + k], device_id=to, device_id_type=MESH)

        sent = []
        for w, (src_ref, out_ref) in enumerate(zip(srcs, outs)):
            for k, chip in enumerate(chips):
                sent.append(copy(w, k, src_ref.at[c], out_ref.at[2 * x + y, c], (*chip, c)))
                sent[-1].start()
        for w, out_ref in enumerate(outs):
            for k, chip in enumerate(chips):
                landed = out_ref.at[index(chip), c]
                copy(w, k, landed, landed, (*chip, c)).wait_recv()
                sent.append(copy(w, 3 + k, landed, landed, sibling))
                sent[-1].start()
        for w, out_ref in enumerate(outs):
            for k, chip in enumerate(chips):
                passed_on = out_ref.at[index(chip), 1 - c]
                copy(w, 3 + k, passed_on, passed_on, sibling).wait_recv()
        for cp in sent:
            cp.wait_send()

    return pl.pallas_call(
        body, name=name, out_shape=[jax.ShapeDtypeStruct((N_CHIP, *s.shape), s.dtype) for s in shards],
        in_specs=[ANY] * nw, out_specs=[ANY] * nw,
        scratch_shapes=[pltpu.SemaphoreType.DMA((6 * nw,)), pltpu.SemaphoreType.DMA((6 * nw,))],
    )(*shards)


def _pair_exchange_halves(grads, *, name):
    nw = len(grads)

    def body(*refs):
        srcs, outs, (send_sems, recv_sems) = refs[:nw], refs[nw:2 * nw], refs[2 * nw:]
        _, _, c, sibling, _ = _place()
        cps = []
        for w, (src_ref, out_ref) in enumerate(zip(srcs, outs)):
            cps.append(pltpu.make_async_remote_copy(
                src_ref=src_ref.at[:, 1 - c], dst_ref=out_ref, send_sem=send_sems.at[w],
                recv_sem=recv_sems.at[w], device_id=sibling, device_id_type=MESH))
            cps[-1].start()
        for cp in cps:
            cp.wait()

    return pl.pallas_call(
        body, name=name, out_shape=[jax.ShapeDtypeStruct((N_CHIP, *g.shape[2:]), g.dtype) for g in grads],
        in_specs=[ANY] * nw, out_specs=[ANY] * nw,
        scratch_shapes=[pltpu.SemaphoreType.DMA((nw,)), pltpu.SemaphoreType.DMA((nw,))])(*grads)


def _chip_scatter(pairs, *, name):
    nw = len(pairs)

    def body(*refs):
        srcs, outs, (send_sems, recv_sems) = refs[:nw], refs[nw:2 * nw], refs[2 * nw:]
        _, _, c, _, chips = _place()
        cps = []
        for w, (p_ref, out_ref) in enumerate(zip(srcs, outs)):
            for k, chip in enumerate(chips):
                cps.append(pltpu.make_async_remote_copy(
                    src_ref=p_ref.at[2 * chip[0] + chip[1]], dst_ref=out_ref.at[k], send_sem=send_sems.at[3 * w + k],
                    recv_sem=recv_sems.at[3 * w + k], device_id=(*chip, c), device_id_type=MESH))
                cps[-1].start()
        for cp in cps:
            cp.wait()

    return pl.pallas_call(
        body, name=name, out_shape=[jax.ShapeDtypeStruct((3, *p.shape[1:]), p.dtype) for p in pairs],
        in_specs=[ANY] * nw, out_specs=[ANY] * nw,
        scratch_shapes=[pltpu.SemaphoreType.DMA((3 * nw,)), pltpu.SemaphoreType.DMA((3 * nw,))])(*pairs)


def _share_halves(halves, *, name):
    nw = len(halves)

    def body(*refs):
        srcs, outs, (send_sems, recv_sems) = refs[:nw], refs[nw:2 * nw], refs[2 * nw:]
        _, _, _, sibling, _ = _place()
        cps = [pltpu.make_async_remote_copy(src_ref=src_ref, dst_ref=out_ref, send_sem=send_sems.at[w], recv_sem=recv_sems.at[w],
                                            device_id=sibling, device_id_type=MESH)
               for w, (src_ref, out_ref) in enumerate(zip(srcs, outs))]
        for cp in cps:
            cp.start()
        for cp in cps:
            cp.wait()

    return pl.pallas_call(
        body, name=name, out_shape=[jax.ShapeDtypeStruct(h.shape, h.dtype) for h in halves],
        in_specs=[ANY] * nw, out_specs=[ANY] * nw,
        scratch_shapes=[pltpu.SemaphoreType.DMA((nw,)), pltpu.SemaphoreType.DMA((nw,))])(*halves)


def _row_tile(rows, limit=256):
    return next(t for t in range(limit, 15, -16) if rows % t == 0)


def _pair_add(grad, got, core, *, name):
    _, r, n = grad.shape
    half = r // 2
    tr = _row_tile(half)
    nb = half // tr

    def body(core_ref, g_ref, t_ref, f_ref, b_ref):
        acc = g_ref[...] + t_ref[...]
        f_ref[...] = acc
        b_ref[...] = acc.astype(BF16)

    blk = pl.BlockSpec((1, tr, n), lambda j, i, core_ref: (j, i, 0))
    mine = pl.BlockSpec((1, tr, n), lambda j, i, core_ref: (j, core_ref[0] * nb + i, 0))
    return pl.pallas_call(
        body, name=name,
        grid_spec=pltpu.PrefetchScalarGridSpec(num_scalar_prefetch=1, grid=(N_CHIP, nb), in_specs=[mine, blk], out_specs=[blk, blk]),
        out_shape=[jax.ShapeDtypeStruct((N_CHIP, half, n), F32), jax.ShapeDtypeStruct((N_CHIP, half, n), BF16)],
        compiler_params=_params("parallel", "parallel"))(core, grad, got)


def _chip_add(pair, theirs, chip, *, name):
    _, h, n = pair.shape
    tr = _row_tile(h)

    def body(chip_ref, p_ref, t_ref, o_ref):
        o_ref[...] = ((p_ref[0] + t_ref[0].astype(F32)) + t_ref[1].astype(F32)) + t_ref[2].astype(F32)

    return pl.pallas_call(
        body, name=name,
        grid_spec=pltpu.PrefetchScalarGridSpec(
            num_scalar_prefetch=1, grid=(h // tr,),
            in_specs=[pl.BlockSpec((1, tr, n), lambda i, chip_ref: (chip_ref[0], i, 0)),
                      pl.BlockSpec((3, tr, n), lambda i, chip_ref: (0, i, 0))],
            out_specs=pl.BlockSpec((tr, n), lambda i, chip_ref: (i, 0))),
        out_shape=jax.ShapeDtypeStruct((h, n), F32), compiler_params=_params("parallel"))(chip, pair, theirs)


def _sum_devices(gathered, *, name):
    _, m, n = gathered.shape

    def body(g_ref, tot_ref, loss_ref):
        tot = g_ref[0]
        for dev in range(1, N_DEV):
            tot = tot + g_ref[dev]
        tot_ref[...] = tot
        loss_ref[...] = jnp.full((8, n), (0.5 / D_MODEL) * jnp.sum(tot[0:8]), F32)

    return pl.pallas_call(body, name=name, in_specs=[VMEM_SPEC], out_specs=[VMEM_SPEC, VMEM_SPEC],
                          out_shape=[jax.ShapeDtypeStruct((m, n), F32), jax.ShapeDtypeStruct((8, n), F32)])(gathered)


def _ada_mod(cond_all, w_ada_shard, *, name):
    tn = 512

    def body(a_ref, b_ref, o_ref):
        o_ref[...] = _nn(a_ref[...], b_ref[...], precision=HIGHEST)

    return pl.pallas_call(
        body, name=name, grid=(w_ada_shard.shape[1] // tn,),
        in_specs=[pl.BlockSpec(cond_all.shape, lambda j: (0, 0)), pl.BlockSpec((D_MODEL, tn), lambda j: (0, j))],
        out_specs=pl.BlockSpec((N_DEV, tn), lambda j: (0, j)),
        out_shape=jax.ShapeDtypeStruct((N_DEV, w_ada_shard.shape[1]), F32), compiler_params=_params("parallel"))(cond_all, w_ada_shard)


def _ada_grad(cond_all, dmod_cols, *, name):
    tm = 256

    def body(a_ref, b_ref, o_ref):
        o_ref[...] = lax.dot_general(a_ref[...], b_ref[...], (((0,), (0,)), ((), ())), precision=HIGHEST,
                                     preferred_element_type=F32)

    return pl.pallas_call(
        body, name=name, grid=(D_MODEL // tm,),
        in_specs=[pl.BlockSpec((N_DEV, tm), lambda i: (0, i)), pl.BlockSpec(dmod_cols.shape, lambda i: (0, 0))],
        out_specs=pl.BlockSpec((tm, dmod_cols.shape[1]), lambda i: (i, 0)),
        out_shape=jax.ShapeDtypeStruct((D_MODEL, dmod_cols.shape[1]), F32), compiler_params=_params("parallel"))(cond_all, dmod_cols)


def _silu_rows(c8, *, name):
    def body(c_ref, o_ref):
        cv = c_ref[...]
        o_ref[...] = cv * _sigmoid(cv)

    return pl.pallas_call(body, name=name, in_specs=[VMEM_SPEC], out_specs=VMEM_SPEC,
                          out_shape=jax.ShapeDtypeStruct(c8.shape, F32))(c8)


def _rows128(t, rows=None):
    flat = t.reshape(-1, 128)
    return flat if rows is None else jnp.pad(flat, ((0, rows - flat.shape[0]), (0, 0)))


def _from_col_shards(shards, r, n):
    return shards.reshape(N_CHIP, r, n).transpose(1, 0, 2).reshape(r, N_CHIP * n)


def kernel(x, c, w_ada, b_ada, norm1_g, w_in, gla_w_gate, gla_b_gate, gla_norm_g, q_norm_g, k_norm_g, w_out, norm2_g, w_up, conv_w, conv_b, w_down, loss_target, m_w_ada, m_b_ada, m_norm1_g, m_w_in, m_gla_w_gate, m_gla_b_gate, m_gla_norm_g, m_q_norm_g, m_k_norm_g, m_w_out, m_norm2_g, m_w_up, m_conv_w, m_conv_b, m_w_down, v_w_ada, v_b_ada, v_norm1_g, v_w_in, v_gla_w_gate, v_gla_b_gate, v_gla_norm_g, v_q_norm_g, v_k_norm_g, v_w_out, v_norm2_g, v_w_up, v_conv_w, v_conv_b, v_w_down):
    d = D_MODEL
    ax, ay, ac = lax.axis_index("x"), lax.axis_index("y"), lax.axis_index("c")
    chip, dev = 2 * ax + ay, 4 * ax + 2 * ay + ac

    cond = _silu_rows(jnp.broadcast_to(c, (8, d)), name="cond_silu")[0:1]
    small_in = jnp.concatenate([_rows128(cond), _rows128(conv_w[0]), _rows128(gla_w_gate[0])], axis=0)
    small_in = _rows128(small_in, 56)
    got = _all_gather_small(small_in, name="gather_small").reshape(N_DEV, 56, 128)
    cond_all = got[:, 0:8].reshape(N_DEV, d)
    conv_w_full = _from_col_shards(got[0::2, 8:41].reshape(N_CHIP, 3 * 1408 // 128, 128), 3, 1408)
    gate_full = _from_col_shards(got[0::2, 41:49].reshape(N_CHIP, 16 * 64 // 128, 128), GLA_GATE_RANK, 64)
    mod_part = _ada_mod(cond_all, w_ada[0], name="ada_mod")
    mod_got = _all_gather_small(_rows128(mod_part), name="gather_mod").reshape(N_DEV, N_DEV, 1536)
    mod_all = mod_got[0::2].transpose(1, 0, 2).reshape(N_DEV, 6 * d) + b_ada
    mod = lax.dynamic_slice_in_dim(mod_all, dev, 1, axis=0)

    own = [w[0].astype(BF16).reshape(2, w.shape[1] // 2, w.shape[2]) for w in (w_in, w_out, w_up, w_down)]
    others = _gather_weight_shards(own, name="gather_weights")
    got_in, got_out, got_up, got_down = [lax.dynamic_update_index_in_dim(t, o, chip, 0) for t, o in zip(others, own)]
    w_in_full = got_in.reshape(N_CHIP, d, 772).transpose(1, 0, 2).reshape(d, N_CHIP * 772)
    w_out_full = got_out.reshape(d, d)
    w_up_full = got_up.reshape(N_CHIP, d, 1408).transpose(1, 0, 2).reshape(d, 2 * D_FF)
    w_down_full = got_down.reshape(D_FF, d)

    err2, grad_x, (g_wi, g_wo, g_wup, g_wdown), small = _local_step(
        x[0], loss_target[0], mod, _in_proj_layout(w_in_full), w_out_full, w_up_full, w_down_full, conv_w_full, conv_b,
        _gate_layout(gate_full), gla_b_gate, gla_norm_g, q_norm_g, k_norm_g, norm1_g, norm2_g)

    pieces = [err2[0], small["dmod"], small["norm1_g"], small["norm2_g"], small["gla_w_gate"].reshape(-1), small["gla_b_gate"],
              small["gla_norm_g"], small["q_norm_g"], small["k_norm_g"], small["conv_w"].reshape(-1), small["conv_b"]]
    sizes = [p.shape[0] for p in pieces]
    at = [sum(sizes[:i]) for i in range(len(sizes) + 1)]
    vec = _rows128(jnp.concatenate(pieces), 288)
    got = _all_gather_small(vec, name="gather_grads").reshape(N_DEV, 288, 128)
    total, loss8 = _sum_devices(got, name="sum_devices")
    total = total.reshape(-1)
    seg = lambda i: total[at[i]:at[i + 1]]
    dmod_all = got.reshape(N_DEV, -1)[:, at[1]:at[2]]
    g_small = dict(
        b_ada=seg(1)[None], norm1_g=seg(2)[None], norm2_g=seg(3)[None],
        gla_w_gate=lax.dynamic_slice_in_dim(seg(4).reshape(GLA_GATE_RANK, 256), chip * 64, 64, axis=1),
        gla_b_gate=seg(5)[None], gla_norm_g=seg(6)[None], q_norm_g=seg(7)[None], k_norm_g=seg(8)[None],
        conv_w=lax.dynamic_slice_in_dim(seg(9).reshape(3, 2 * D_FF), chip * 1408, 1408, axis=1), conv_b=seg(10)[None])
    dmod_cols = lax.dynamic_slice_in_dim(dmod_all.reshape(N_DEV, 6 * d), chip * 1536, 1536, axis=1)
    g_w_ada = _ada_grad(cond_all, dmod_cols, name="ada_grad")

    tags = ("w_in", "w_out", "w_up", "w_down")
    g_parts = [_in_proj_grad_layout(g_wi).reshape(d, N_CHIP, 772).transpose(1, 0, 2), g_wo.reshape(N_CHIP, d // N_CHIP, d),
               g_wup, g_wdown.reshape(N_CHIP, D_FF // N_CHIP, d)]
    core_id, chip_id = jnp.reshape(ac, (1,)).astype(jnp.int32), jnp.reshape(chip, (1,)).astype(jnp.int32)
    got = _pair_exchange_halves([g.reshape(N_CHIP, 2, g.shape[1] // 2, g.shape[2]) for g in g_parts], name="reduce_pair")
    pairs = [_pair_add(g, t, core_id, name=f"reduce_pair_add_{tag}") for g, t, tag in zip(g_parts, got, tags)]
    theirs = _chip_scatter([pb for _, pb in pairs], name="reduce_chips")
    summed = [_chip_add(pf, t, chip_id, name=f"reduce_chips_add_{tag}") for (pf, _), t, tag in zip(pairs, theirs, tags)]
    others = _share_halves(summed, name="share_pair")
    g_big = [jnp.concatenate([jnp.where(ac == 0, mine, other), jnp.where(ac == 0, other, mine)], axis=0)
             for mine, other in zip(summed, others)]

    grads = dict(w_ada=g_w_ada, w_in=g_big[0], w_out=g_big[1], w_up=g_big[2], w_down=g_big[3], **g_small)
    names = ["w_ada", "b_ada", "norm1_g", "w_in", "gla_w_gate", "gla_b_gate", "gla_norm_g", "q_norm_g", "k_norm_g", "w_out",
             "norm2_g", "w_up", "conv_w", "conv_b", "w_down"]
    ws = dict(w_ada=w_ada, b_ada=b_ada, norm1_g=norm1_g, w_in=w_in, gla_w_gate=gla_w_gate, gla_b_gate=gla_b_gate,
              gla_norm_g=gla_norm_g, q_norm_g=q_norm_g, k_norm_g=k_norm_g, w_out=w_out, norm2_g=norm2_g, w_up=w_up,
              conv_w=conv_w, conv_b=conv_b, w_down=w_down)
    ms = dict(w_ada=m_w_ada, b_ada=m_b_ada, norm1_g=m_norm1_g, w_in=m_w_in, gla_w_gate=m_gla_w_gate, gla_b_gate=m_gla_b_gate,
              gla_norm_g=m_gla_norm_g, q_norm_g=m_q_norm_g, k_norm_g=m_k_norm_g, w_out=m_w_out, norm2_g=m_norm2_g, w_up=m_w_up,
              conv_w=m_conv_w, conv_b=m_conv_b, w_down=m_w_down)
    vs = dict(w_ada=v_w_ada, b_ada=v_b_ada, norm1_g=v_norm1_g, w_in=v_w_in, gla_w_gate=v_gla_w_gate, gla_b_gate=v_gla_b_gate,
              gla_norm_g=v_gla_norm_g, q_norm_g=v_q_norm_g, k_norm_g=v_k_norm_g, w_out=v_w_out, norm2_g=v_norm2_g, w_up=v_w_up,
              conv_w=v_conv_w, conv_b=v_conv_b, w_down=v_w_down)
    g_out, d_out, m_out, v_out = [], [], [], []
    for nm in names:
        w2 = ws[nm].reshape(ws[nm].shape[-2:])
        g2 = grads[nm].reshape(w2.shape)
        dl, mn, vn = _adamw(w2, g2, ms[nm].reshape(w2.shape), vs[nm].reshape(w2.shape), name=f"adamw_{nm}")
        shape = ws[nm].shape
        g_out.append(g2.reshape(shape))
        d_out.append(dl.reshape(shape))
        m_out.append(mn.reshape(shape))
        v_out.append(vn.reshape(shape))
    return (loss8[0, 0], grad_x[None], *g_out, *d_out, *m_out, *v_out)
```

```python
import functools

import jax
import jax.numpy as jnp
from jax import lax
from jax.experimental import pallas as pl
from jax.experimental.pallas import tpu as pltpu

F32, BF16 = jnp.float32, jnp.bfloat16
HIGHEST = lax.Precision.HIGHEST
MESH = pl.DeviceIdType.MESH

D_MODEL = 1024
GLA_CHUNK = 64
GLA_GATE_TAU = 16.0
GLA_GATE_RANK = 16
HEAD_LANES = 128
ATTN_BLOCK = 128
DILATIONS = (1, 4, 16)
ALIBI_SLOPES = tuple(2.0 ** (-(h + 1)) for h in range(8))
D_FF = 2816
EPS = 1e-6
C_GQ, C_GK, C_GV, C_GR, C_AQ, C_AK, C_AV, C_LR, PROJ_W = 0, 256, 512, 1024, 1536, 2048, 2560, 3072, 3200
ADAM_LR, ADAM_B1, ADAM_B2, ADAM_EPS, ADAM_WD, ADAM_STEP = 0.001, 0.9, 0.999, 1e-08, 0.01, 10
VMEM_LIMIT_BYTES = 56 * 1024 * 1024
ROW_TILE = 256


def _params(*sem):
    return pltpu.CompilerParams(dimension_semantics=sem or None, vmem_limit_bytes=VMEM_LIMIT_BYTES)


def _nt(a, b):
    return lax.dot_general(a, b, (((1,), (1,)), ((), ())), preferred_element_type=F32)


def _tn(a, b):
    return lax.dot_general(a, b, (((0,), (0,)), ((), ())), preferred_element_type=F32)


def _nn(a, b, precision=None):
    return jnp.dot(a, b, preferred_element_type=F32, precision=precision)


def _split3(v):
    hi = v.astype(BF16)
    rest = v - hi.astype(F32)
    mid = rest.astype(BF16)
    return hi, mid, (rest - mid.astype(F32)).astype(BF16)


def _sum_right(v, ones):
    hi, mid, lo = _split3(v)
    return (_nn(lo, ones) + _nn(mid, ones)) + _nn(hi, ones)


def _sum_left(ones, v):
    hi, mid, lo = _split3(v)
    return (_nn(ones, lo) + _nn(ones, mid)) + _nn(ones, hi)


def _fold8(v):
    return v.reshape(v.shape[0] // 8, 8, v.shape[1]).sum(axis=0)


def _spread_total(ref):
    t = ref[...]
    ref[...] = jnp.broadcast_to(jnp.sum(t, axis=-2, keepdims=True), t.shape)


def _sigmoid(x):
    return 1.0 / (1.0 + jnp.exp(-x))


def _mm(a, b, *, ta=False, tb=False, out_dtype=F32, tm, tn, tk, shard_cols=False, name):
    (k_a, m) = a.shape if ta else a.shape[::-1]
    (k_b, n) = b.shape[::-1] if tb else b.shape
    assert k_a == k_b and m % tm == 0 and n % tn == 0 and k_a % tk == 0, (name, a.shape, b.shape)
    nk = k_a // tk
    assert nk == 1 or out_dtype == F32, name
    dims = (((0 if ta else 1,), (1 if tb else 0,)), ((), ()))

    def body(a_ref, b_ref, o_ref):
        k = pl.program_id(2)
        part = lax.dot_general(a_ref[...].astype(BF16), b_ref[...].astype(BF16), dims, preferred_element_type=F32)
        if nk == 1:
            o_ref[...] = part.astype(out_dtype)
        else:
            @pl.when(k == 0)
            def _():
                o_ref[...] = part

            @pl.when(k > 0)
            def _():
                o_ref[...] += part

    a_spec = pl.BlockSpec((tk, tm), lambda i, j, k: (k, i)) if ta else pl.BlockSpec((tm, tk), lambda i, j, k: (i, k))
    b_spec = pl.BlockSpec((tn, tk), lambda i, j, k: (j, k)) if tb else pl.BlockSpec((tk, tn), lambda i, j, k: (k, j))
    if shard_cols:
        o_spec, o_shape = pl.BlockSpec((None, tm, tn), lambda i, j, k: (j, i, 0)), (n // tn, m, tn)
    else:
        o_spec, o_shape = pl.BlockSpec((tm, tn), lambda i, j, k: (i, j)), (m, n)
    return pl.pallas_call(
        body, name=name, grid=(m // tm, n // tn, nk), in_specs=[a_spec, b_spec], out_specs=o_spec,
        out_shape=jax.ShapeDtypeStruct(o_shape, out_dtype), compiler_params=_params("parallel", "parallel", "arbitrary"),
    )(a, b)


def _norm_mod_fwd(x, branch, gate, gain, scale, shift, *, name):
    s, d = x.shape
    tm = ROW_TILE
    has_branch = branch is not None

    def body(*refs):
        if has_branch:
            x_ref, br_ref, gate_ref, gain_ref, sc_ref, sh_ref, x1_ref, h_ref = refs
            xv = x_ref[...] + gate_ref[...] * br_ref[...]
            x1_ref[...] = xv
        else:
            x_ref, gain_ref, sc_ref, sh_ref, h_ref = refs
            xv = x_ref[...]
        r = lax.rsqrt(jnp.mean(xv * xv, axis=-1, keepdims=True) + EPS)
        h_ref[...] = ((xv * r) * gain_ref[...] * (1.0 + sc_ref[...]) + sh_ref[...]).astype(BF16)

    row = pl.BlockSpec((tm, d), lambda i: (i, 0))
    vec = pl.BlockSpec((1, d), lambda i: (0, 0))
    if has_branch:
        return pl.pallas_call(
            body, name=name, grid=(s // tm,), in_specs=[row, row, vec, vec, vec, vec], out_specs=[row, row],
            out_shape=[jax.ShapeDtypeStruct((s, d), F32), jax.ShapeDtypeStruct((s, d), BF16)],
            compiler_params=_params("parallel"))(x, branch, gate, gain, scale, shift)
    h = pl.pallas_call(
        body, name=name, grid=(s // tm,), in_specs=[row, vec, vec, vec], out_specs=row,
        out_shape=jax.ShapeDtypeStruct((s, d), BF16), compiler_params=_params("parallel"))(x, gain, scale, shift)
    return x, h


def _norm_mod_bwd(x, dh, dres, gain, scale, branch, gate, *, name):
    s, d = x.shape
    tm = ROW_TILE
    has_branch = branch is not None

    def body(*refs):
        if has_branch:
            x_ref, dh_ref, dres_ref, gain_ref, sc_ref, br_ref, gate_ref, dx_ref, dbr_ref, sums_ref = refs
        else:
            x_ref, dh_ref, dres_ref, gain_ref, sc_ref, dx_ref, sums_ref = refs
        i = pl.program_id(0)

        @pl.when(i == 0)
        def _():
            sums_ref[...] = jnp.zeros_like(sums_ref)

        xv, dhv = x_ref[...], dh_ref[...]
        r = lax.rsqrt(jnp.mean(xv * xv, axis=-1, keepdims=True) + EPS)
        xn = xv * r
        dxn = dhv * (gain_ref[...] * (1.0 + sc_ref[...]))
        dx = dres_ref[...] + r * (dxn - xn * jnp.mean(dxn * xn, axis=-1, keepdims=True))
        dx_ref[...] = dx
        sums_ref[0] += _fold8(dhv * xn)
        sums_ref[1] += _fold8(dhv)
        if has_branch:
            dbr_ref[...] = (gate_ref[...] * dx).astype(BF16)
            sums_ref[2] += _fold8(dx * br_ref[...])

        @pl.when(i == s // tm - 1)
        def _():
            _spread_total(sums_ref)

    row = pl.BlockSpec((tm, d), lambda i: (i, 0))
    vec = pl.BlockSpec((1, d), lambda i: (0, 0))
    sums = pl.BlockSpec((3, 8, d), lambda i: (0, 0, 0))
    sums_shape = jax.ShapeDtypeStruct((3, 8, d), F32)
    if has_branch:
        return pl.pallas_call(
            body, name=name, grid=(s // tm,), in_specs=[row, row, row, vec, vec, row, vec], out_specs=[row, row, sums],
            out_shape=[jax.ShapeDtypeStruct((s, d), F32), jax.ShapeDtypeStruct((s, d), BF16), sums_shape],
            compiler_params=_params("arbitrary"))(x, dh, dres, gain, scale, branch, gate)
    dx, sm = pl.pallas_call(
        body, name=name, grid=(s // tm,), in_specs=[row, row, row, vec, vec], out_specs=[row, sums],
        out_shape=[jax.ShapeDtypeStruct((s, d), F32), sums_shape],
        compiler_params=_params("arbitrary"))(x, dh, dres, gain, scale)
    return dx, None, sm


GLA_ROWS = 256


def _gla_chunk_setup(lr_ref, wg_ref, bg_ref, rows):
    c = GLA_CHUNK
    ri = lax.broadcasted_iota(jnp.int32, (c, c), 0)
    ci = lax.broadcasted_iota(jnp.int32, (c, c), 1)
    z = _nn(lr_ref[rows, :].astype(BF16), wg_ref[...]) + bg_ref[...]
    g = (jnp.minimum(z, 0.0) - jnp.log(1.0 + jnp.exp(-jnp.abs(z)))) * (1.0 / GLA_GATE_TAU)
    b = _sum_left((ci <= ri).astype(BF16), g)
    return z, b, ci <= ri


def _last_row(b):
    ri = lax.broadcasted_iota(jnp.int32, b.shape, 0)
    return jnp.sum(jnp.where(ri == b.shape[0] - 1, b, 0.0), axis=0, keepdims=True)


def _gla_fwd(proj, wg, bg, gn, *, name):
    s = proj.shape[0]
    tb, c = GLA_ROWS, GLA_CHUNK
    cb = tb // c

    def body(q_ref, k_ref, v_ref, r_ref, lr_ref, wg_ref, bg_ref, gn_ref, o_ref, y_ref, st_ref, state):
        i = pl.program_id(0)

        @pl.when(i == 0)
        def _():
            state[...] = jnp.zeros_like(state)

        low = lax.broadcasted_iota(jnp.int32, (c, HEAD_LANES), 1) < 64
        for ch in range(cb):
            rows = pl.ds(ch * c, c)
            _, b, causal = _gla_chunk_setup(lr_ref, wg_ref, bg_ref, rows)
            for p in range(2):
                cols = pl.ds(p * HEAD_LANES, HEAD_LANES)
                bp = b[:, p * HEAD_LANES:(p + 1) * HEAD_LANES]
                b_end = _last_row(bp)
                q = q_ref[rows, cols] * 0.125
                k = k_ref[rows, cols]
                q_in = q * jnp.exp(bp)
                k_out = (k * jnp.exp(-bp)).astype(BF16)
                k_end = k * jnp.exp(b_end - bp)
                st = state[p]
                st_ref[ch, p] = st
                st_b = st.astype(BF16)
                upd = jnp.zeros_like(st)
                for e in range(2):
                    msk = low if e == 0 else jnp.logical_not(low)
                    hc = pl.ds((2 * p + e) * HEAD_LANES, HEAD_LANES)
                    qm = jnp.where(msk, q_in, 0.0).astype(BF16)
                    a = jnp.where(causal, _nt(qm, k_out), 0.0)
                    v = v_ref[rows, hc].astype(BF16)
                    o = _nt(qm, st_b) + _nn(a.astype(BF16), v)
                    upd = upd + _tn(v, jnp.where(msk, k_end, 0.0).astype(BF16))
                    o_ref[rows, hc] = o
                    rr = r_ref[rows, hc]
                    on = o * lax.rsqrt(jnp.mean(o * o, axis=-1, keepdims=True) + EPS)
                    y_ref[rows, hc] = (on * gn_ref[...] * (rr * _sigmoid(rr))).astype(BF16)
                state[p] = st * jnp.exp(b_end) + upd

    def col(width, at):
        return pl.BlockSpec((tb, width), lambda i: (i, at // width))

    full = lambda shape: pl.BlockSpec(shape, lambda i: tuple(0 for _ in shape))
    return pl.pallas_call(
        body, name=name, grid=(s // tb,),
        in_specs=[col(256, C_GQ), col(256, C_GK), col(512, C_GV), col(512, C_GR), col(128, C_LR),
                  full((HEAD_LANES, 256)), full((1, 256)), full((1, HEAD_LANES))],
        out_specs=[pl.BlockSpec((tb, 512), lambda i: (i, 0)), pl.BlockSpec((tb, 512), lambda i: (i, 0)),
                   pl.BlockSpec((cb, 2, HEAD_LANES, HEAD_LANES), lambda i: (i, 0, 0, 0))],
        out_shape=[jax.ShapeDtypeStruct((s, 512), F32), jax.ShapeDtypeStruct((s, 512), BF16),
                   jax.ShapeDtypeStruct((s // c, 2, HEAD_LANES, HEAD_LANES), F32)],
        scratch_shapes=[pltpu.VMEM((2, HEAD_LANES, HEAD_LANES), F32)],
        compiler_params=_params("arbitrary"))(proj, proj, proj, proj, proj, wg, bg, gn)


def _gla_bwd(proj, wg, bg, gn, o_raw, states, dmixed, *, name):
    s = proj.shape[0]
    tb, c = GLA_ROWS, GLA_CHUNK
    cb = tb // c
    nblk, nch = s // tb, s // c

    def body(q_ref, k_ref, v_ref, r_ref, lr_ref, wg_ref, bg_ref, gn_ref, o_ref, st_ref, stn_ref, dy_ref,
             dq_ref, dk_ref, dv_ref, dr_ref, dlr_ref, gwg_ref, sums_ref, dstate):
        i = pl.program_id(0)

        @pl.when(i == 0)
        def _():
            dstate[...] = jnp.zeros_like(dstate)
            gwg_ref[...] = jnp.zeros_like(gwg_ref)
            sums_ref[...] = jnp.zeros_like(sums_ref)

        low = lax.broadcasted_iota(jnp.int32, (c, HEAD_LANES), 1) < 64
        for ch in reversed(range(cb)):
            rows = pl.ds(ch * c, c)
            z, b, causal = _gla_chunk_setup(lr_ref, wg_ref, bg_ref, rows)
            upper = jnp.logical_not(causal) | (lax.broadcasted_iota(jnp.int32, (c, c), 0)
                                               == lax.broadcasted_iota(jnp.int32, (c, c), 1))
            lr_b = lr_ref[rows, :].astype(BF16)
            dlr = jnp.zeros((c, HEAD_LANES), F32)
            for p in range(2):
                cols = pl.ds(p * HEAD_LANES, HEAD_LANES)
                sl = slice(p * HEAD_LANES, (p + 1) * HEAD_LANES)
                bp = b[:, sl]
                b_end = _last_row(bp)
                e_in, e_out, e_end = jnp.exp(bp), jnp.exp(-bp), jnp.exp(b_end - bp)
                q = q_ref[rows, cols] * 0.125
                k = k_ref[rows, cols]
                q_in = q * e_in
                k_out = k * e_out
                k_end = k * e_end
                st0 = st_ref[ch, p]
                st1 = st_ref[ch + 1, p] if ch + 1 < cb else stn_ref[0, p]
                dst = dstate[p]
                st0_b, dst_b = st0.astype(BF16), dst.astype(BF16)
                dq_in = jnp.zeros((c, HEAD_LANES), F32)
                dk_out = jnp.zeros((c, HEAD_LANES), F32)
                dk_end = jnp.zeros((c, HEAD_LANES), F32)
                dst_new = dst * jnp.exp(b_end)
                for e in range(2):
                    msk = low if e == 0 else jnp.logical_not(low)
                    hc = pl.ds((2 * p + e) * HEAD_LANES, HEAD_LANES)
                    o = o_ref[rows, hc]
                    rr = r_ref[rows, hc]
                    dy = dy_ref[rows, hc]
                    sg = _sigmoid(rr)
                    rs = lax.rsqrt(jnp.mean(o * o, axis=-1, keepdims=True) + EPS)
                    on = o * rs
                    t = dy * (rr * sg)
                    sums_ref[1, :, hc] += _fold8(t * on)
                    dn = t * gn_ref[...]
                    do = (rs * (dn - on * jnp.mean(dn * on, axis=-1, keepdims=True))).astype(BF16)
                    dr_ref[rows, hc] = (dy * on * gn_ref[...] * (sg * (1.0 + rr * (1.0 - sg)))).astype(BF16)
                    qm = jnp.where(msk, q_in, 0.0).astype(BF16)
                    km_out = jnp.where(msk, k_out, 0.0).astype(BF16)
                    km_end = jnp.where(msk, k_end, 0.0).astype(BF16)
                    v = v_ref[rows, hc].astype(BF16)
                    a = jnp.where(causal, _nt(qm, km_out), 0.0).astype(BF16)
                    da = jnp.where(causal, _nt(do, v), 0.0).astype(BF16)
                    dv_ref[rows, hc] = (_tn(a, do) + _nt(km_end, dst_b)).astype(BF16)
                    dq_in = dq_in + jnp.where(msk, _nn(do, st0_b) + _nn(da, km_out), 0.0)
                    dk_out = dk_out + _tn(da, qm)
                    dk_end = dk_end + jnp.where(msk, _nn(v, dst_b), 0.0)
                    dst_new = dst_new + _tn(do, qm)
                dq = dq_in * e_in
                dk = dk_out * e_out + dk_end * e_end
                dq_ref[rows, cols] = (dq * 0.125).astype(BF16)
                dk_ref[rows, cols] = dk.astype(BF16)
                w = q * dq - k * dk
                dg = _sum_left(upper.astype(BF16), w) + jnp.sum(dst * st1, axis=0, keepdims=True)
                zp = z[:, sl]
                dz = dg * (1.0 / GLA_GATE_TAU) * _sigmoid(-zp)
                dz_b = dz.astype(BF16)
                sums_ref[0, :, cols] += _fold8(dz)
                dlr = dlr + _nt(dz_b, wg_ref[:, cols])
                gwg_ref[:, cols] += _tn(lr_b, dz_b)
                dstate[p] = dst_new
            dlr_ref[rows, :] = dlr.astype(BF16)

        @pl.when(i == nblk - 1)
        def _():
            _spread_total(sums_ref)

    rev = lambda i: nblk - 1 - i

    def col(width, at):
        return pl.BlockSpec((tb, width), lambda i: (rev(i), at // width))

    full = lambda shape: pl.BlockSpec(shape, lambda i: tuple(0 for _ in shape))
    out_col = lambda width: pl.BlockSpec((tb, width), lambda i: (rev(i), 0))
    return pl.pallas_call(
        body, name=name, grid=(nblk,),
        in_specs=[col(256, C_GQ), col(256, C_GK), col(512, C_GV), col(512, C_GR), col(128, C_LR),
                  full((HEAD_LANES, 256)), full((1, 256)), full((1, HEAD_LANES)),
                  pl.BlockSpec((tb, 512), lambda i: (rev(i), 0)),
                  pl.BlockSpec((cb, 2, HEAD_LANES, HEAD_LANES), lambda i: (rev(i), 0, 0, 0)),
                  pl.BlockSpec((1, 2, HEAD_LANES, HEAD_LANES), lambda i: (jnp.minimum((rev(i) + 1) * cb, nch - 1), 0, 0, 0)),
                  pl.BlockSpec((tb, 512), lambda i: (rev(i), 0))],
        out_specs=[out_col(256), out_col(256), out_col(512), out_col(512), out_col(128),
                   full((HEAD_LANES, 256)), full((2, 8, 512))],
        out_shape=[jax.ShapeDtypeStruct((s, 256), BF16), jax.ShapeDtypeStruct((s, 256), BF16),
                   jax.ShapeDtypeStruct((s, 512), BF16), jax.ShapeDtypeStruct((s, 512), BF16),
                   jax.ShapeDtypeStruct((s, 128), BF16), jax.ShapeDtypeStruct((HEAD_LANES, 256), F32),
                   jax.ShapeDtypeStruct((2, 8, 512), F32)],
        scratch_shapes=[pltpu.VMEM((2, HEAD_LANES, HEAD_LANES), F32)],
        compiler_params=_params("arbitrary"))(proj, proj, proj, proj, proj, wg, bg, gn, o_raw, states, states, dmixed)


def _head_sum_matrix():
    ri = lax.broadcasted_iota(jnp.int32, (512, 512), 0) // 64
    ci = lax.broadcasted_iota(jnp.int32, (512, 512), 1) // 64
    return (ri == ci).astype(BF16)


def _attn_prep(proj, qg, kg, *, name):
    s = proj.shape[0]
    tm = ROW_TILE

    def body(q_ref, k_ref, qg_ref, kg_ref, qa_ref, ka_ref):
        hs = _head_sum_matrix()
        q, k = q_ref[...], k_ref[...]
        qr = lax.rsqrt(_sum_right(q * q, hs) * (1.0 / 64) + EPS)
        kr = lax.rsqrt(_sum_right(k * k, hs) * (1.0 / 64) + EPS)
        qa_ref[...] = q * qr * qg_ref[...] * 0.125
        ka_ref[...] = k * kr * kg_ref[...]

    col = lambda at: pl.BlockSpec((tm, 512), lambda i: (i, at // 512))
    vec = pl.BlockSpec((1, 512), lambda i: (0, 0))
    out = pl.BlockSpec((tm, 512), lambda i: (i, 0))
    return pl.pallas_call(
        body, name=name, grid=(s // tm,), in_specs=[col(C_AQ), col(C_AK), vec, vec], out_specs=[out] * 2,
        out_shape=[jax.ShapeDtypeStruct((s, 512), F32)] * 2, compiler_params=_params("parallel"))(proj, proj, qg, kg)


FAR = 1e30


def _attn_distance(first):
    blk = ATTN_BLOCK
    iq = lax.broadcasted_iota(jnp.int32, (2 * blk, 2 * blk), 0) & (blk - 1)
    ik = lax.broadcasted_iota(jnp.int32, (2 * blk, 2 * blk), 1)
    rel = iq + blk - ik
    valid = (rel >= 0) & (rel <= blk) & (jnp.logical_not(first) | (ik >= blk))
    return jnp.where(valid, rel.astype(F32), FAR)


def _stack_heads(t2):
    low = lax.broadcasted_iota(jnp.int32, t2.shape, 1) < 64
    return jnp.concatenate([jnp.where(low, t2, 0.0), jnp.where(low, 0.0, t2)], axis=0).astype(BF16)


def _unstack_heads(t):
    blk = ATTN_BLOCK
    low = lax.broadcasted_iota(jnp.int32, (blk, HEAD_LANES), 1) < 64
    return jnp.where(low, t[0:blk], t[blk:2 * blk])


def _attn_scores(qs, kcat, slopes, dil, dist):
    top = lax.broadcasted_iota(jnp.int32, (2 * ATTN_BLOCK, 1), 0) < ATTN_BLOCK
    return _nt(qs, kcat) - jnp.where(top, slopes[0] * dil, slopes[1] * dil) * dist


def _pair_slopes(p):
    if isinstance(p, int):
        return ALIBI_SLOPES[2 * p], ALIBI_SLOPES[2 * p + 1]
    pick = lambda e: jnp.where(p == 0, ALIBI_SLOPES[e], jnp.where(p == 1, ALIBI_SLOPES[2 + e],
                               jnp.where(p == 2, ALIBI_SLOPES[4 + e], ALIBI_SLOPES[6 + e])))
    return pick(0), pick(1)


def _attn_pair_fwd(q2, kcat, vcat, slopes, dil, dist):
    sc = _attn_scores(_stack_heads(q2), kcat, slopes, dil, dist)
    m = jnp.max(sc, axis=-1, keepdims=True)
    pr = jnp.exp(sc - m)
    den = jnp.sum(pr, axis=-1, keepdims=True)
    o = _nn(pr.astype(BF16), vcat) / den
    lse = jnp.broadcast_to(m + jnp.log(den), o.shape)
    return _unstack_heads(o), _unstack_heads(lse)


def _attn_pair_bwd(q2, kcat, vcat, do2, y2, lse2, slopes, dil, dist):
    lane = lax.broadcasted_iota(jnp.int32, (ATTN_BLOCK, HEAD_LANES), 1)
    low = lane < 64
    prod = do2 * y2
    per_head = lambda t, pick: jnp.concatenate([jnp.sum(jnp.where(pick(0), t, 0.0), axis=-1, keepdims=True),
                                                jnp.sum(jnp.where(pick(1), t, 0.0), axis=-1, keepdims=True)], axis=0)
    lse = per_head(lse2, lambda e: lane == 64 * e)
    delta = per_head(prod, lambda e: low if e == 0 else jnp.logical_not(low))
    qs, dos = _stack_heads(q2), _stack_heads(do2)
    pr = jnp.exp(_attn_scores(qs, kcat, slopes, dil, dist) - lse)
    ds = (pr * (_nt(dos, vcat) - delta)).astype(BF16)
    return _unstack_heads(_nn(ds, kcat)), _tn(ds, qs), _tn(pr.astype(BF16), dos)


def _attn_specs(dil):
    rows = ATTN_BLOCK * dil
    if dil == 1:
        cur = lambda at: pl.BlockSpec((rows, 512), lambda n: (n, at // 512))
        prev = lambda at: pl.BlockSpec((rows, 512), lambda n: (jnp.maximum(n - 1, 0), at // 512))
    else:
        cur = lambda at: pl.BlockSpec((rows, HEAD_LANES), lambda n, p: (n, at // HEAD_LANES + p))
        prev = lambda at: pl.BlockSpec((rows, HEAD_LANES), lambda n, p: (jnp.maximum(n - 1, 0), at // HEAD_LANES + p))
    return cur, prev


def _attn_loop(dil, one_pair):
    if dil == 1:
        for p in range(4):
            one_pair(slice(None), pl.ds(p * HEAD_LANES, HEAD_LANES), p)
    else:
        p = pl.program_id(1)

        def step(r, carry):
            one_pair(pl.ds(r, ATTN_BLOCK, stride=dil), slice(None), p)
            return carry

        lax.fori_loop(0, dil, step, 0, unroll=min(dil, 4))


def _dil_attn_fwd(qa, ka, proj, dil, *, name):
    s = qa.shape[0]

    def body(q_ref, kp_ref, kc_ref, vp_ref, vc_ref, o_ref, lse_ref):
        dist = _attn_distance(pl.program_id(0) == 0)

        def one_pair(rows, cols, p):
            kcat = jnp.concatenate([kp_ref[rows, cols], kc_ref[rows, cols]], axis=0).astype(BF16)
            vcat = jnp.concatenate([vp_ref[rows, cols], vc_ref[rows, cols]], axis=0).astype(BF16)
            o2, lse2 = _attn_pair_fwd(q_ref[rows, cols], kcat, vcat, _pair_slopes(p), dil, dist)
            o_ref[rows, cols] = o2
            lse_ref[rows, cols] = lse2

        _attn_loop(dil, one_pair)

    cur, prev = _attn_specs(dil)
    grid = (s // ATTN_BLOCK,) if dil == 1 else (s // (ATTN_BLOCK * dil), 4)
    return pl.pallas_call(
        body, name=name, grid=grid, in_specs=[cur(0), prev(0), cur(0), prev(C_AV), cur(C_AV)], out_specs=[cur(0), cur(0)],
        out_shape=[jax.ShapeDtypeStruct((s, 512), F32)] * 2,
        compiler_params=_params(*["parallel"] * len(grid)))(qa, ka, ka, proj, proj)


def _attn_merge(branches, y_gla, *, name):
    s = y_gla.shape[0]
    tm = ROW_TILE

    def body(o0, l0, o1, l1, o2, l2, yg_ref, mixed_ref, y_ref, lse_ref):
        m = jnp.maximum(jnp.maximum(l0[...], l1[...]), l2[...])
        w0, w1, w2 = jnp.exp(l0[...] - m), jnp.exp(l1[...] - m), jnp.exp(l2[...] - m)
        zs = w0 + w1 + w2
        y = (w0 * o0[...] + w1 * o1[...] + w2 * o2[...]) / zs
        y_ref[...] = y
        lse_ref[...] = m + jnp.log(zs)
        mixed_ref[:, 0:512] = yg_ref[...]
        mixed_ref[:, 512:1024] = y.astype(BF16)

    blk = pl.BlockSpec((tm, 512), lambda i: (i, 0))
    args = [t for pair in branches for t in pair]
    return pl.pallas_call(
        body, name=name, grid=(s // tm,), in_specs=[blk] * 7,
        out_specs=[pl.BlockSpec((tm, 1024), lambda i: (i, 0)), blk, blk],
        out_shape=[jax.ShapeDtypeStruct((s, 1024), BF16), jax.ShapeDtypeStruct((s, 512), F32),
                   jax.ShapeDtypeStruct((s, 512), F32)],
        compiler_params=_params("parallel"))(*args, y_gla)


def _dil_attn_bwd(qa, ka, proj, y_att, lse, dmixed, dil, *, name):
    s = qa.shape[0]
    blk = ATTN_BLOCK

    def body(q_ref, kp_ref, kc_ref, vp_ref, vc_ref, y_ref, lse_ref, do_ref, dq_ref, dkc_ref, dkp_ref, dvc_ref, dvp_ref):
        dist = _attn_distance(pl.program_id(0) == 0)

        def one_pair(rows, cols, p):
            kcat = jnp.concatenate([kp_ref[rows, cols], kc_ref[rows, cols]], axis=0).astype(BF16)
            vcat = jnp.concatenate([vp_ref[rows, cols], vc_ref[rows, cols]], axis=0).astype(BF16)
            dq, dk, dv = _attn_pair_bwd(q_ref[rows, cols], kcat, vcat, do_ref[rows, cols], y_ref[rows, cols],
                                        lse_ref[rows, cols], _pair_slopes(p), dil, dist)
            dq_ref[rows, cols] = dq
            dkp_ref[rows, cols] = dk[0:blk]
            dkc_ref[rows, cols] = dk[blk:2 * blk]
            dvp_ref[rows, cols] = dv[0:blk]
            dvc_ref[rows, cols] = dv[blk:2 * blk]

        _attn_loop(dil, one_pair)

    cur, prev = _attn_specs(dil)
    grid = (s // blk,) if dil == 1 else (s // (blk * dil), 4)
    return pl.pallas_call(
        body, name=name, grid=grid,
        in_specs=[cur(0), prev(0), cur(0), prev(C_AV), cur(C_AV), cur(0), cur(0), cur(512)], out_specs=[cur(0)] * 5,
        out_shape=[jax.ShapeDtypeStruct((s, 512), F32)] * 5, compiler_params=_params(*["parallel"] * len(grid)),
    )(qa, ka, ka, proj, proj, y_att, lse, dmixed)


def _attn_post(parts, proj, qg, kg, *, name):
    s = proj.shape[0]
    tm = ATTN_BLOCK
    nblk = s // tm

    def body(*refs):
        ins, (q_ref, k_ref, qg_ref, kg_ref, dq_out, dk_out, dv_out, sums_ref) = refs[:15], refs[15:]
        i = pl.program_id(0)

        @pl.when(i == 0)
        def _():
            sums_ref[...] = jnp.zeros_like(sums_ref)

        dq = jnp.zeros((tm, 512), F32)
        dk = jnp.zeros((tm, 512), F32)
        dv = jnp.zeros((tm, 512), F32)
        for g, dil in enumerate(DILATIONS):
            dq_r, dkc_r, dkp_r, dvc_r, dvp_r = ins[5 * g:5 * g + 5]
            inside = (i + dil < nblk).astype(F32)
            dq = dq + dq_r[...]
            dk = dk + dkc_r[...] + inside * dkp_r[...]
            dv = dv + dvc_r[...] + inside * dvp_r[...]
        dv_out[...] = dv.astype(BF16)
        hs = _head_sum_matrix()
        for row, (x_ref, g_ref, dy, out, post) in enumerate(((q_ref, qg_ref, dq, dq_out, 0.125), (k_ref, kg_ref, dk, dk_out, 1.0))):
            x = x_ref[...]
            rs = lax.rsqrt(_sum_right(x * x, hs) * (1.0 / 64) + EPS)
            xn = x * rs
            dy = dy * post
            sums_ref[row] += _fold8(dy * xn)
            dn = dy * g_ref[...]
            out[...] = (rs * (dn - xn * (_sum_right(dn * xn, hs) * (1.0 / 64)))).astype(BF16)

        @pl.when(i == nblk - 1)
        def _():
            _spread_total(sums_ref)

    here = pl.BlockSpec((tm, 512), lambda i: (i, 0))
    specs = []
    for dil in DILATIONS:
        later = pl.BlockSpec((tm, 512), lambda i, dil=dil: (jnp.minimum(i + dil, nblk - 1), 0))
        specs += [here, here, later, here, later]
    col = lambda at: pl.BlockSpec((tm, 512), lambda i: (i, at // 512))
    vec = pl.BlockSpec((1, 512), lambda i: (0, 0))
    return pl.pallas_call(
        body, name=name, grid=(nblk,), in_specs=specs + [col(C_AQ), col(C_AK), vec, vec],
        out_specs=[here, here, here, pl.BlockSpec((2, 8, 512), lambda i: (0, 0, 0))],
        out_shape=[jax.ShapeDtypeStruct((s, 512), BF16)] * 3 + [jax.ShapeDtypeStruct((2, 8, 512), F32)],
        compiler_params=_params("arbitrary"))(*[t for part in parts for t in part], proj, proj, qg, kg)


FFN_TM, FFN_TN = 256, 1408
HALO = 16


def _conv3(u_ref, halo_ref, w_ref, b_ref, first):
    u = u_ref[...].astype(F32)
    ext = jnp.concatenate([jnp.where(first, 0.0, halo_ref[...].astype(F32)), u], axis=0)
    u1 = pltpu.roll(ext, 1, 0)[HALO:]
    u2 = pltpu.roll(ext, 2, 0)[HALO:]
    return b_ref[...] + w_ref[0:1, :] * u2 + w_ref[1:2, :] * u1 + w_ref[2:3, :] * u, u, u1, u2


def _ffn_specs(tm, tn):
    nj = D_FF // tn
    blk = lambda half: pl.BlockSpec((tm, tn), lambda j, i: (i, j + half * nj))
    halo = lambda half: pl.BlockSpec((HALO, tn), lambda j, i: (jnp.maximum(i * (tm // HALO) - 1, 0), j + half * nj))
    wspec = lambda half: pl.BlockSpec((3, tn), lambda j, i: (0, j + half * nj))
    bspec = lambda half: pl.BlockSpec((1, tn), lambda j, i: (0, j + half * nj))
    return [blk(0), halo(0), blk(1), halo(1), wspec(0), wspec(1), bspec(0), bspec(1)]


def _conv_swiglu_fwd(u, conv_w, conv_b, *, name):
    s = u.shape[0]
    tm, tn = FFN_TM, FFN_TN

    def body(ug_ref, hg_ref, uv_ref, hv_ref, wg_ref, wv_ref, bg_ref, bv_ref, act_ref):
        first = pl.program_id(1) == 0
        cg = _conv3(ug_ref, hg_ref, wg_ref, bg_ref, first)[0]
        cv = _conv3(uv_ref, hv_ref, wv_ref, bv_ref, first)[0]
        act_ref[...] = (cg * _sigmoid(cg) * cv).astype(BF16)

    return pl.pallas_call(
        body, name=name, grid=(D_FF // tn, s // tm), in_specs=_ffn_specs(tm, tn),
        out_specs=pl.BlockSpec((tm, tn), lambda j, i: (i, j)), out_shape=jax.ShapeDtypeStruct((s, D_FF), BF16),
        compiler_params=_params("parallel", "parallel"))(u, u, u, u, conv_w, conv_w, conv_b, conv_b)


def _conv_swiglu_bwd_pre(u, conv_w, conv_b, dact, *, name):
    s = u.shape[0]
    tm, tn = FFN_TM, FFN_TN

    def body(ug_ref, hg_ref, uv_ref, hv_ref, wg_ref, wv_ref, bg_ref, bv_ref, da_ref, duc_ref, sums_ref):
        i = pl.program_id(1)

        @pl.when(i == 0)
        def _():
            sums_ref[...] = jnp.zeros_like(sums_ref)

        cg, g0, g1, g2 = _conv3(ug_ref, hg_ref, wg_ref, bg_ref, i == 0)
        cv, v0, v1, v2 = _conv3(uv_ref, hv_ref, wv_ref, bv_ref, i == 0)
        da = da_ref[...].astype(F32)
        sg = _sigmoid(cg)
        dg = da * cv * (sg * (1.0 + cg * (1.0 - sg)))
        dv = da * (cg * sg)
        duc_ref[0] = dg.astype(BF16)
        duc_ref[1] = dv.astype(BF16)
        for half, (d, taps) in enumerate(((dg, (g2, g1, g0)), (dv, (v2, v1, v0)))):
            for t, tap in enumerate(taps):
                sums_ref[half, t] += _fold8(d * tap)
            sums_ref[half, 3] += _fold8(d)

        @pl.when(i == s // tm - 1)
        def _():
            _spread_total(sums_ref)

    return pl.pallas_call(
        body, name=name, grid=(D_FF // tn, s // tm),
        in_specs=_ffn_specs(tm, tn) + [pl.BlockSpec((tm, tn), lambda j, i: (i, j))],
        out_specs=[pl.BlockSpec((2, tm, tn), lambda j, i: (0, i, j)), pl.BlockSpec((2, 4, 8, tn), lambda j, i: (0, 0, 0, j))],
        out_shape=[jax.ShapeDtypeStruct((2, s, D_FF), BF16), jax.ShapeDtypeStruct((2, 4, 8, D_FF), F32)],
        compiler_params=_params("parallel", "arbitrary"))(u, u, u, u, conv_w, conv_w, conv_b, conv_b, dact)


def _conv_bwd(duc, conv_w, *, name):
    _, s, _ = duc.shape
    tm, tn = FFN_TM, FFN_TN
    nj, ni = D_FF // tn, s // tm

    def body(d_ref, halo_ref, w_ref, du_ref):
        last = pl.program_id(2) == ni - 1
        d = d_ref[0].astype(F32)
        ext = jnp.concatenate([d, jnp.where(last, 0.0, halo_ref[0].astype(F32))], axis=0)
        n = tm + HALO
        d1 = pltpu.roll(ext, n - 1, 0)[:tm]
        d2 = pltpu.roll(ext, n - 2, 0)[:tm]
        du_ref[...] = (w_ref[2:3, :] * d + w_ref[1:2, :] * d1 + w_ref[0:1, :] * d2).astype(BF16)

    return pl.pallas_call(
        body, name=name, grid=(2, nj, ni),
        in_specs=[pl.BlockSpec((1, tm, tn), lambda g, j, i: (g, i, j)),
                  pl.BlockSpec((1, HALO, tn), lambda g, j, i: (g, jnp.minimum((i + 1) * (tm // HALO), s // HALO - 1), j)),
                  pl.BlockSpec((3, tn), lambda g, j, i: (0, g * nj + j))],
        out_specs=pl.BlockSpec((tm, tn), lambda g, j, i: (i, g * nj + j)),
        out_shape=jax.ShapeDtypeStruct((s, 2 * D_FF), BF16),
        compiler_params=_params("parallel", "parallel", "parallel"))(duc, duc, conv_w)


def _loss_head(x1, ffn, gate, target, *, name):
    s, d = x1.shape
    tm = ROW_TILE

    def body(x_ref, f_ref, g_ref, t_ref, dy_ref, df_ref, sums_ref):
        i = pl.program_id(0)

        @pl.when(i == 0)
        def _():
            sums_ref[...] = jnp.zeros_like(sums_ref)

        f = f_ref[...]
        err = x_ref[...] + g_ref[...] * f - t_ref[...]
        dy = err * (1.0 / d)
        dy_ref[...] = dy
        df_ref[...] = (g_ref[...] * dy).astype(BF16)
        sums_ref[0] += _fold8(dy * f)
        sums_ref[1] += _fold8(err * err)

        @pl.when(i == s // tm - 1)
        def _():
            _spread_total(sums_ref)

    row = pl.BlockSpec((tm, d), lambda i: (i, 0))
    return pl.pallas_call(
        body, name=name, grid=(s // tm,), in_specs=[row, row, pl.BlockSpec((1, d), lambda i: (0, 0)), row],
        out_specs=[row, row, pl.BlockSpec((2, 8, d), lambda i: (0, 0, 0))],
        out_shape=[jax.ShapeDtypeStruct((s, d), F32), jax.ShapeDtypeStruct((s, d), BF16), jax.ShapeDtypeStruct((2, 8, d), F32)],
        compiler_params=_params("arbitrary"))(x1, ffn, gate, target)


def _adamw(w, g, m, v, *, name):
    rows, cols = w.shape
    tm = next((t for t in range(ROW_TILE, 7, -8) if rows % t == 0), rows)

    def body(w_ref, g_ref, m_ref, v_ref, d_ref, mo_ref, vo_ref):
        gv = g_ref[...]
        mn = ADAM_B1 * m_ref[...] + (1.0 - ADAM_B1) * gv
        vn = ADAM_B2 * v_ref[...] + (1.0 - ADAM_B2) * (gv * gv)
        m_hat = mn / (1.0 - ADAM_B1 ** ADAM_STEP)
        v_hat = vn / (1.0 - ADAM_B2 ** ADAM_STEP)
        d_ref[...] = -ADAM_LR * (m_hat / (jnp.sqrt(v_hat) + ADAM_EPS) + ADAM_WD * w_ref[...])
        mo_ref[...] = mn
        vo_ref[...] = vn

    blk = pl.BlockSpec((tm, cols), lambda i: (i, 0))
    return pl.pallas_call(
        body, name=name, grid=(rows // tm,), in_specs=[blk] * 4, out_specs=[blk] * 3,
        out_shape=[jax.ShapeDtypeStruct((rows, cols), F32)] * 3, compiler_params=_params("parallel"))(w, g, m, v)


def _colsum(t):
    return t[..., 0, :]


def _in_proj_layout(w_in):
    pad = jnp.zeros((w_in.shape[0], PROJ_W - C_LR - GLA_GATE_RANK), w_in.dtype)
    return jnp.concatenate([w_in[:, :1536], w_in[:, 1552:], w_in[:, 1536:1552], pad], axis=1)


def _in_proj_grad_layout(g):
    return jnp.concatenate([g[:, :1536], g[:, C_LR:C_LR + GLA_GATE_RANK], g[:, 1536:C_LR]], axis=1)


def _gate_layout(gla_w_gate):
    return jnp.pad(gla_w_gate, ((0, HEAD_LANES - GLA_GATE_RANK), (0, 0))).astype(BF16)


def _local_step(x, target, mod, wi, wo, ffn_weights, conv_w, conv_b, wg, bg, gn, qg, kg, n1g, n2g):
    d = D_MODEL
    sh1, sc1, g1, sh2, sc2, g2 = [mod[:, i * d:(i + 1) * d] for i in range(6)]
    qg8, kg8 = jnp.tile(qg, (1, 8)), jnp.tile(kg, (1, 8))

    _, h1 = _norm_mod_fwd(x, None, None, n1g, sc1, sh1, name="norm1_fwd")
    proj = _mm(h1, wi, tm=1024, tn=PROJ_W, tk=d, name="in_proj")
    o_raw, y_gla, states = _gla_fwd(proj, wg, bg, gn, name="gla_fwd")
    qa, ka = _attn_prep(proj, qg8, kg8, name="attn_prep")
    branches = [_dil_attn_fwd(qa, ka, proj, dil, name=f"attn_fwd_d{dil}") for dil in DILATIONS]
    mixed, y_att, lse = _attn_merge(branches, y_gla, name="attn_merge")
    attn_out = _mm(mixed, wo, tm=1024, tn=d, tk=d, name="out_proj")
    x1, h2 = _norm_mod_fwd(x, attn_out, g1, n2g, sc2, sh2, name="norm2_fwd")
    wup, wdown = ffn_weights(h2)
    u = _mm(h2, wup, out_dtype=BF16, tm=1024, tn=D_FF, tk=d, name="up_proj")
    act = _conv_swiglu_fwd(u, conv_w, conv_b, name="conv_swiglu_fwd")
    ffn = _mm(act, wdown, tm=1024, tn=d, tk=D_FF, name="down_proj")
    dy, dffn, head_sums = _loss_head(x1, ffn, g2, target, name="loss_head")

    dact = _mm(dffn, wdown, tb=True, out_dtype=BF16, tm=1024, tn=D_FF, tk=d, name="down_proj_dx")
    g_wdown = _mm(act, dffn, ta=True, tm=1408, tn=d, tk=1024, name="down_proj_dw")
    duc, conv_sums = _conv_swiglu_bwd_pre(u, conv_w, conv_b, dact, name="conv_swiglu_bwd")
    du = _conv_bwd(duc, conv_w, name="conv_bwd")
    dh2 = _mm(du, wup, tb=True, tm=1024, tn=d, tk=1408, name="up_proj_dx")
    g_wup = _mm(h2, du, ta=True, tm=d, tn=1408, tk=1024, shard_cols=True, name="up_proj_dw")
    dx1, dao, n2_sums = _norm_mod_bwd(x1, dh2, dy, n2g, sc2, attn_out, g1, name="norm2_bwd")

    dmixed = _mm(dao, wo, tb=True, tm=1024, tn=d, tk=d, name="out_proj_dx")
    g_wo = _mm(mixed, dao, ta=True, tm=d, tn=d, tk=1024, name="out_proj_dw")
    dgq, dgk, dgv, dgr, dlr, g_wg, gla_sums = _gla_bwd(proj, wg, bg, gn, o_raw, states, dmixed, name="gla_bwd")
    parts = [_dil_attn_bwd(qa, ka, proj, y_att, lse, dmixed, dil, name=f"attn_bwd_d{dil}") for dil in DILATIONS]
    daq, dak, dav, qk_sums = _attn_post(parts, proj, qg8, kg8, name="attn_post")
    dproj = jnp.concatenate([dgq, dgk, dgv, dgr, daq, dak, dav, dlr], axis=1)
    dh1 = _mm(dproj, wi, tb=True, tm=1024, tn=d, tk=PROJ_W, name="in_proj_dx")
    g_wi = _mm(h1, dproj, ta=True, tm=512, tn=PROJ_W, tk=512, name="in_proj_dw")
    grad_x, _, n1_sums = _norm_mod_bwd(x, dh1, dx1, n1g, sc1, None, None, name="norm1_bwd")

    n1, n2, hs, cs = _colsum(n1_sums), _colsum(n2_sums), _colsum(head_sums), _colsum(conv_sums)
    gs, qs = _colsum(gla_sums), _colsum(qk_sums)
    dmod = jnp.concatenate([n1[1], n1[0] * n1g[0], n2[2], n2[1], n2[0] * n2g[0], hs[0]])
    small = dict(
        dmod=dmod,
        norm1_g=n1[0] * (1.0 + sc1[0]), norm2_g=n2[0] * (1.0 + sc2[0]),
        gla_w_gate=g_wg[:GLA_GATE_RANK], gla_b_gate=gs[0, :256], gla_norm_g=gs[1].reshape(4, 128).sum(axis=0),
        q_norm_g=qs[0].reshape(8, 64).sum(axis=0), k_norm_g=qs[1].reshape(8, 64).sum(axis=0),
        conv_w=jnp.concatenate([cs[0, :3], cs[1, :3]], axis=1), conv_b=jnp.concatenate([cs[0, 3], cs[1, 3]]),
    )
    return head_sums[1], grad_x, (g_wi, g_wo, g_wup, g_wdown), small


N_DEV, N_CHIP = 8, 4
ANY = pl.BlockSpec(memory_space=pl.ANY)
VMEM_SPEC = pl.BlockSpec(memory_space=pltpu.VMEM)


def _place():
    x, y, c = lax.axis_index("x"), lax.axis_index("y"), lax.axis_index("c")
    other_chips = [(1 - x, y), (x, 1 - y), (1 - x, 1 - y)]
    return x, y, c, (x, y, 1 - c), other_chips


def _all_gather_small(v, *, name):
    m, n = v.shape

    def body(v_ref, out_ref, send_sems, recv_sems, local_sem):
        x, y, c, sibling, chips = _place()
        me = (x, y, c)

        def rows(px, py, pc):
            return out_ref.at[pl.ds((4 * px + 2 * py + pc) * m, m), :]

        def copy(k, block, to, src=None):
            return pltpu.make_async_remote_copy(
                src_ref=rows(*block) if src is None else src, dst_ref=rows(*block), send_sem=send_sems.at[k],
                recv_sem=recv_sems.at[k], device_id=to, device_id_type=MESH)

        mine = pltpu.make_async_copy(v_ref, rows(*me), local_sem)
        mine.start()
        first = [copy(0, me, sibling, src=v_ref)]
        first += [copy(1 + j, me, (*chip, c), src=v_ref) for j, chip in enumerate(chips)]
        for cp in first:
            cp.start()
        passed = [copy(4 + j, (*chip, c), sibling) for j, chip in enumerate(chips)]
        for j, chip in enumerate(chips):
            copy(1 + j, (*chip, c), me).wait_recv()
            passed[j].start()
        copy(0, sibling, me).wait_recv()
        for j, chip in enumerate(chips):
            copy(4 + j, (*chip, 1 - c), me).wait_recv()
        for cp in first + passed:
            cp.wait_send()
        mine.wait()

    return pl.pallas_call(
        body, name=name, out_shape=jax.ShapeDtypeStruct((N_DEV * m, n), v.dtype), in_specs=[VMEM_SPEC], out_specs=VMEM_SPEC,
        scratch_shapes=[pltpu.SemaphoreType.DMA((7,)), pltpu.SemaphoreType.DMA((7,)), pltpu.SemaphoreType.DMA],
    )(v)


def _gather_weight_shards(shards, *, name):
    nw = len(shards)

    def body(*refs):
        srcs, outs, (send_sems, recv_sems) = refs[:nw], refs[nw:2 * nw], refs[2 * nw:]
        x, y, c, sibling, chips = _place()
        index = lambda chip: 2 * chip[0] + chip[1]

        def copy(w, k, src, dst, to):
            return pltpu.make_async_remote_copy(src_ref=src, dst_ref=dst, send_sem=send_sems.at[6 * w + k],
                                                recv_sem=recv_sems.at[6 * w + k], device_id=to, device_id_type=MESH)

        sent = []
        for w, (src_ref, out_ref) in enumerate(zip(srcs, outs)):
            for k, chip in enumerate(chips):
                sent.append(copy(w, k, src_ref.at[c], out_ref.at[2 * x + y, c], (*chip, c)))
                sent[-1].start()
        for w, out_ref in enumerate(outs):
            for k, chip in enumerate(chips):
                landed = out_ref.at[index(chip), c]
                copy(w, k, landed, landed, (*chip, c)).wait_recv()
                sent.append(copy(w, 3 + k, landed, landed, sibling))
                sent[-1].start()
        for w, out_ref in enumerate(outs):
            for k, chip in enumerate(chips):
                passed_on = out_ref.at[index(chip), 1 - c]
                copy(w, 3 + k, passed_on, passed_on, sibling).wait_recv()
        for cp in sent:
            cp.wait_send()

    return pl.pallas_call(
        body, name=name, out_shape=[jax.ShapeDtypeStruct((N_CHIP, *s.shape), s.dtype) for s in shards],
        in_specs=[ANY] * nw, out_specs=[ANY] * nw,
        scratch_shapes=[pltpu.SemaphoreType.DMA((6 * nw,)), pltpu.SemaphoreType.DMA((6 * nw,))],
    )(*shards)


HBM_SPEC = pl.BlockSpec(memory_space=pltpu.HBM)
SEM_SPEC = pl.BlockSpec(memory_space=pltpu.SEMAPHORE)
DATAFLOW_EFFECT = pltpu.SideEffectType.DATAFLOW_SIDE_EFFECTING


def _late_copies(srcs, lands, send_sems, recv_sems):
    x, y, c, _, chips = _place()
    return [pltpu.make_async_remote_copy(
        src_ref=src.at[c], dst_ref=land.at[2 * x + y, c], send_sem=send_sems.at[6 * w + 2 * r + core],
        recv_sem=recv_sems.at[6 * w + 2 * r + c], device_id=(*chip, core), device_id_type=MESH)
        for w, (src, land) in enumerate(zip(srcs, lands)) for r, chip in enumerate(chips) for core in range(2)]


def _gather_late_start(own, *, name):
    nw = len(own)

    def body(*refs):
        srcs, lands, send_sems, recv_sems, token = refs[:nw], refs[nw:2 * nw], refs[2 * nw], refs[2 * nw + 1], refs[-1]
        for cp in _late_copies(srcs, lands, send_sems, recv_sems):
            cp.start()
        token[...] = jnp.zeros_like(token)

    lands = [pltpu.with_memory_space_constraint(lax.empty((N_CHIP, *s.shape), s.dtype), pltpu.HBM) for s in own]
    own = [pltpu.with_memory_space_constraint(s, pltpu.HBM) for s in own]
    out = pl.pallas_call(
        body, name=name,
        out_shape=(pltpu.SemaphoreType.DMA((6 * nw,)), pltpu.SemaphoreType.DMA((6 * nw,)),
                   *[pltpu.HBM(s.shape, s.dtype) for s in own], *[pltpu.HBM(s.shape, s.dtype) for s in lands],
                   jax.ShapeDtypeStruct((8, 128), F32)),
        in_specs=[HBM_SPEC] * (2 * nw), out_specs=(SEM_SPEC, SEM_SPEC, *[HBM_SPEC] * (2 * nw), VMEM_SPEC),
        input_output_aliases={i: 2 + i for i in range(2 * nw)},
        compiler_params=pltpu.CompilerParams(has_side_effects=DATAFLOW_EFFECT))(*own, *lands)
    return out[0], out[1], out[2:2 + nw], out[2 + nw:2 + 2 * nw], out[-1]


def _gather_late_wait(send_sems, recv_sems, own, lands, after, *, name):
    nw = len(own)

    def body(*refs):
        srcs, lands_in, send_sems, recv_sems = refs[:nw], refs[nw:2 * nw], refs[2 * nw], refs[2 * nw + 1]
        x, y, c, _, chips = _place()
        for cp in _late_copies(srcs, lands_in, send_sems, recv_sems):
            cp.wait_send()
        for w, (src, land) in enumerate(zip(srcs, lands_in)):
            for r, chip in enumerate(chips):
                for core in range(2):
                    pltpu.make_async_remote_copy(
                        src_ref=src.at[c], dst_ref=land.at[2 * chip[0] + chip[1], core], send_sem=send_sems.at[6 * w + 2 * r + core],
                        recv_sem=recv_sems.at[6 * w + 2 * r + core], device_id=(*chip, core), device_id_type=MESH).wait_recv()

    out = pl.pallas_call(
        body, name=name, out_shape=(*[pltpu.HBM(s.shape, s.dtype) for s in own], *[pltpu.HBM(s.shape, s.dtype) for s in lands]),
        in_specs=[HBM_SPEC] * (2 * nw) + [SEM_SPEC, SEM_SPEC, ANY], out_specs=tuple([HBM_SPEC] * (2 * nw)),
        input_output_aliases={i: i for i in range(2 * nw)},
        compiler_params=pltpu.CompilerParams(has_side_effects=DATAFLOW_EFFECT))(*own, *lands, send_sems, recv_sems, after)
    return out[:nw], out[nw:]


def _pair_exchange_halves(grads, *, name):
    nw = len(grads)

    def body(*refs):
        srcs, outs, (send_sems, recv_sems) = refs[:nw], refs[nw:2 * nw], refs[2 * nw:]
        _, _, c, sibling, _ = _place()
        cps = []
        for w, (src_ref, out_ref) in enumerate(zip(srcs, outs)):
            cps.append(pltpu.make_async_remote_copy(
                src_ref=src_ref.at[:, 1 - c], dst_ref=out_ref, send_sem=send_sems.at[w],
                recv_sem=recv_sems.at[w], device_id=sibling, device_id_type=MESH))
            cps[-1].start()
        for cp in cps:
            cp.wait()

    return pl.pallas_call(
        body, name=name, out_shape=[jax.ShapeDtypeStruct((N_CHIP, *g.shape[2:]), g.dtype) for g in grads],
        in_specs=[ANY] * nw, out_specs=[ANY] * nw,
        scratch_shapes=[pltpu.SemaphoreType.DMA((nw,)), pltpu.SemaphoreType.DMA((nw,))])(*grads)


def _chip_scatter(pairs, *, name):
    nw = len(pairs)

    def body(*refs):
        srcs, outs, (send_sems, recv_sems) = refs[:nw], refs[nw:2 * nw], refs[2 * nw:]
        _, _, c, _, chips = _place()
        cps = []
        for w, (p_ref, out_ref) in enumerate(zip(srcs, outs)):
            for k, chip in enumerate(chips):
                cps.append(pltpu.make_async_remote_copy(
                    src_ref=p_ref.at[2 * chip[0] + chip[1]], dst_ref=out_ref.at[k], send_sem=send_sems.at[3 * w + k],
                    recv_sem=recv_sems.at[3 * w + k], device_id=(*chip, c), device_id_type=MESH))
                cps[-1].start()
        for cp in cps:
            cp.wait()

    return pl.pallas_call(
        body, name=name, out_shape=[jax.ShapeDtypeStruct((3, *p.shape[1:]), p.dtype) for p in pairs],
        in_specs=[ANY] * nw, out_specs=[ANY] * nw,
        scratch_shapes=[pltpu.SemaphoreType.DMA((3 * nw,)), pltpu.SemaphoreType.DMA((3 * nw,))])(*pairs)


def _share_halves(halves, *, name):
    nw = len(halves)

    def body(*refs):
        srcs, outs, (send_sems, recv_sems) = refs[:nw], refs[nw:2 * nw], refs[2 * nw:]
        _, _, _, sibling, _ = _place()
        cps = [pltpu.make_async_remote_copy(src_ref=src_ref, dst_ref=out_ref, send_sem=send_sems.at[w], recv_sem=recv_sems.at[w],
                                            device_id=sibling, device_id_type=MESH)
               for w, (src_ref, out_ref) in enumerate(zip(srcs, outs))]
        for cp in cps:
            cp.start()
        for cp in cps:
            cp.wait()

    return pl.pallas_call(
        body, name=name, out_shape=[jax.ShapeDtypeStruct(h.shape, h.dtype) for h in halves],
        in_specs=[ANY] * nw, out_specs=[ANY] * nw,
        scratch_shapes=[pltpu.SemaphoreType.DMA((nw,)), pltpu.SemaphoreType.DMA((nw,))])(*halves)


def _row_tile(rows, limit=256):
    return next(t for t in range(limit, 15, -16) if rows % t == 0)


def _pair_add(grad, got, core, *, name):
    _, r, n = grad.shape
    half = r // 2
    tr = _row_tile(half)
    nb = half // tr

    def body(core_ref, g_ref, t_ref, f_ref, b_ref):
        acc = g_ref[...] + t_ref[...]
        f_ref[...] = acc
        b_ref[...] = acc.astype(BF16)

    blk = pl.BlockSpec((1, tr, n), lambda j, i, core_ref: (j, i, 0))
    mine = pl.BlockSpec((1, tr, n), lambda j, i, core_ref: (j, core_ref[0] * nb + i, 0))
    return pl.pallas_call(
        body, name=name,
        grid_spec=pltpu.PrefetchScalarGridSpec(num_scalar_prefetch=1, grid=(N_CHIP, nb), in_specs=[mine, blk], out_specs=[blk, blk]),
        out_shape=[jax.ShapeDtypeStruct((N_CHIP, half, n), F32), jax.ShapeDtypeStruct((N_CHIP, half, n), BF16)],
        compiler_params=_params("parallel", "parallel"))(core, grad, got)


def _chip_add(pair, theirs, chip, *, name):
    _, h, n = pair.shape
    tr = _row_tile(h)

    def body(chip_ref, p_ref, t_ref, o_ref):
        o_ref[...] = ((p_ref[0] + t_ref[0].astype(F32)) + t_ref[1].astype(F32)) + t_ref[2].astype(F32)

    return pl.pallas_call(
        body, name=name,
        grid_spec=pltpu.PrefetchScalarGridSpec(
            num_scalar_prefetch=1, grid=(h // tr,),
            in_specs=[pl.BlockSpec((1, tr, n), lambda i, chip_ref: (chip_ref[0], i, 0)),
                      pl.BlockSpec((3, tr, n), lambda i, chip_ref: (0, i, 0))],
            out_specs=pl.BlockSpec((tr, n), lambda i, chip_ref: (i, 0))),
        out_shape=jax.ShapeDtypeStruct((h, n), F32), compiler_params=_params("parallel"))(chip, pair, theirs)


def _sum_devices(gathered, *, name):
    _, m, n = gathered.shape

    def body(g_ref, tot_ref, loss_ref):
        tot = g_ref[0]
        for dev in range(1, N_DEV):
            tot = tot + g_ref[dev]
        tot_ref[...] = tot
        loss_ref[...] = jnp.full((8, n), (0.5 / D_MODEL) * jnp.sum(tot[0:8]), F32)

    return pl.pallas_call(body, name=name, in_specs=[VMEM_SPEC], out_specs=[VMEM_SPEC, VMEM_SPEC],
                          out_shape=[jax.ShapeDtypeStruct((m, n), F32), jax.ShapeDtypeStruct((8, n), F32)])(gathered)


def _ada_mod(cond_all, w_ada_shard, *, name):
    tn = 512

    def body(a_ref, b_ref, o_ref):
        o_ref[...] = _nn(a_ref[...], b_ref[...], precision=HIGHEST)

    return pl.pallas_call(
        body, name=name, grid=(w_ada_shard.shape[1] // tn,),
        in_specs=[pl.BlockSpec(cond_all.shape, lambda j: (0, 0)), pl.BlockSpec((D_MODEL, tn), lambda j: (0, j))],
        out_specs=pl.BlockSpec((N_DEV, tn), lambda j: (0, j)),
        out_shape=jax.ShapeDtypeStruct((N_DEV, w_ada_shard.shape[1]), F32), compiler_params=_params("parallel"))(cond_all, w_ada_shard)


def _ada_grad(cond_all, dmod_cols, *, name):
    tm = 256

    def body(a_ref, b_ref, o_ref):
        o_ref[...] = lax.dot_general(a_ref[...], b_ref[...], (((0,), (0,)), ((), ())), precision=HIGHEST,
                                     preferred_element_type=F32)

    return pl.pallas_call(
        body, name=name, grid=(D_MODEL // tm,),
        in_specs=[pl.BlockSpec((N_DEV, tm), lambda i: (0, i)), pl.BlockSpec(dmod_cols.shape, lambda i: (0, 0))],
        out_specs=pl.BlockSpec((tm, dmod_cols.shape[1]), lambda i: (i, 0)),
        out_shape=jax.ShapeDtypeStruct((D_MODEL, dmod_cols.shape[1]), F32), compiler_params=_params("parallel"))(cond_all, dmod_cols)


def _silu_rows(c8, *, name):
    def body(c_ref, o_ref):
        cv = c_ref[...]
        o_ref[...] = cv * _sigmoid(cv)

    return pl.pallas_call(body, name=name, in_specs=[VMEM_SPEC], out_specs=VMEM_SPEC,
                          out_shape=jax.ShapeDtypeStruct(c8.shape, F32))(c8)


def _rows128(t, rows=None):
    flat = t.reshape(-1, 128)
    return flat if rows is None else jnp.pad(flat, ((0, rows - flat.shape[0]), (0, 0)))


def _from_col_shards(shards, r, n):
    return shards.reshape(N_CHIP, r, n).transpose(1, 0, 2).reshape(r, N_CHIP * n)


def kernel(x, c, w_ada, b_ada, norm1_g, w_in, gla_w_gate, gla_b_gate, gla_norm_g, q_norm_g, k_norm_g, w_out, norm2_g, w_up, conv_w, conv_b, w_down, loss_target, m_w_ada, m_b_ada, m_norm1_g, m_w_in, m_gla_w_gate, m_gla_b_gate, m_gla_norm_g, m_q_norm_g, m_k_norm_g, m_w_out, m_norm2_g, m_w_up, m_conv_w, m_conv_b, m_w_down, v_w_ada, v_b_ada, v_norm1_g, v_w_in, v_gla_w_gate, v_gla_b_gate, v_gla_norm_g, v_q_norm_g, v_k_norm_g, v_w_out, v_norm2_g, v_w_up, v_conv_w, v_conv_b, v_w_down):
    d = D_MODEL
    ax, ay, ac = lax.axis_index("x"), lax.axis_index("y"), lax.axis_index("c")
    chip, dev = 2 * ax + ay, 4 * ax + 2 * ay + ac

    cond = _silu_rows(jnp.broadcast_to(c, (8, d)), name="cond_silu")[0:1]
    small_in = jnp.concatenate([_rows128(cond), _rows128(conv_w[0]), _rows128(gla_w_gate[0])], axis=0)
    small_in = _rows128(small_in, 56)
    got = _all_gather_small(small_in, name="gather_small").reshape(N_DEV, 56, 128)
    cond_all = got[:, 0:8].reshape(N_DEV, d)
    conv_w_full = _from_col_shards(got[0::2, 8:41].reshape(N_CHIP, 3 * 1408 // 128, 128), 3, 1408)
    gate_full = _from_col_shards(got[0::2, 41:49].reshape(N_CHIP, 16 * 64 // 128, 128), GLA_GATE_RANK, 64)
    mod_part = _ada_mod(cond_all, w_ada[0], name="ada_mod")
    mod_got = _all_gather_small(_rows128(mod_part), name="gather_mod").reshape(N_DEV, N_DEV, 1536)
    mod_all = mod_got[0::2].transpose(1, 0, 2).reshape(N_DEV, 6 * d) + b_ada
    mod = lax.dynamic_slice_in_dim(mod_all, dev, 1, axis=0)

    own = [w[0].astype(BF16).reshape(2, w.shape[1] // 2, w.shape[2]) for w in (w_in, w_out, w_up, w_down)]
    with_own = lambda got, mine: [lax.dynamic_update_index_in_dim(t, o, chip, 0) for t, o in zip(got, mine)]
    got_in, got_out = with_own(_gather_weight_shards(own[:2], name="gather_weights"), own[:2])
    w_in_full = got_in.reshape(N_CHIP, d, 772).transpose(1, 0, 2).reshape(d, N_CHIP * 772)
    w_out_full = got_out.reshape(d, d)
    send_sems, recv_sems, own_thru, lands, token = _gather_late_start(own[2:], name="gather_late_start")
    mod = mod + token[0:1, 0:1]

    def ffn_weights(after):
        mine, landed = _gather_late_wait(send_sems, recv_sems, own_thru, lands, after, name="gather_late_wait")
        got_up, got_down = with_own(landed, mine)
        return got_up.reshape(N_CHIP, d, 1408).transpose(1, 0, 2).reshape(d, 2 * D_FF), got_down.reshape(D_FF, d)

    err2, grad_x, (g_wi, g_wo, g_wup, g_wdown), small = _local_step(
        x[0], loss_target[0], mod, _in_proj_layout(w_in_full), w_out_full, ffn_weights, conv_w_full, conv_b,
        _gate_layout(gate_full), gla_b_gate, gla_norm_g, q_norm_g, k_norm_g, norm1_g, norm2_g)

    pieces = [err2[0], small["dmod"], small["norm1_g"], small["norm2_g"], small["gla_w_gate"].reshape(-1), small["gla_b_gate"],
              small["gla_norm_g"], small["q_norm_g"], small["k_norm_g"], small["conv_w"].reshape(-1), small["conv_b"]]
    sizes = [p.shape[0] for p in pieces]
    at = [sum(sizes[:i]) for i in range(len(sizes) + 1)]
    vec = _rows128(jnp.concatenate(pieces), 288)
    got = _all_gather_small(vec, name="gather_grads").reshape(N_DEV, 288, 128)
    total, loss8 = _sum_devices(got, name="sum_devices")
    total = total.reshape(-1)
    seg = lambda i: total[at[i]:at[i + 1]]
    dmod_all = got.reshape(N_DEV, -1)[:, at[1]:at[2]]
    g_small = dict(
        b_ada=seg(1)[None], norm1_g=seg(2)[None], norm2_g=seg(3)[None],
        gla_w_gate=lax.dynamic_slice_in_dim(seg(4).reshape(GLA_GATE_RANK, 256), chip * 64, 64, axis=1),
        gla_b_gate=seg(5)[None], gla_norm_g=seg(6)[None], q_norm_g=seg(7)[None], k_norm_g=seg(8)[None],
        conv_w=lax.dynamic_slice_in_dim(seg(9).reshape(3, 2 * D_FF), chip * 1408, 1408, axis=1), conv_b=seg(10)[None])
    dmod_cols = lax.dynamic_slice_in_dim(dmod_all.reshape(N_DEV, 6 * d), chip * 1536, 1536, axis=1)
    g_w_ada = _ada_grad(cond_all, dmod_cols, name="ada_grad")

    tags = ("w_in", "w_out", "w_up", "w_down")
    g_parts = [_in_proj_grad_layout(g_wi).reshape(d, N_CHIP, 772).transpose(1, 0, 2), g_wo.reshape(N_CHIP, d // N_CHIP, d),
               g_wup, g_wdown.reshape(N_CHIP, D_FF // N_CHIP, d)]
    core_id, chip_id = jnp.reshape(ac, (1,)).astype(jnp.int32), jnp.reshape(chip, (1,)).astype(jnp.int32)
    got = _pair_exchange_halves([g.reshape(N_CHIP, 2, g.shape[1] // 2, g.shape[2]) for g in g_parts], name="reduce_pair")
    pairs = [_pair_add(g, t, core_id, name=f"reduce_pair_add_{tag}") for g, t, tag in zip(g_parts, got, tags)]
    theirs = _chip_scatter([pb for _, pb in pairs], name="reduce_chips")
    summed = [_chip_add(pf, t, chip_id, name=f"reduce_chips_add_{tag}") for (pf, _), t, tag in zip(pairs, theirs, tags)]
    others = _share_halves(summed, name="share_pair")
    g_big = [jnp.concatenate([jnp.where(ac == 0, mine, other), jnp.where(ac == 0, other, mine)], axis=0)
             for mine, other in zip(summed, others)]

    grads = dict(w_ada=g_w_ada, w_in=g_big[0], w_out=g_big[1], w_up=g_big[2], w_down=g_big[3], **g_small)
    names = ["w_ada", "b_ada", "norm1_g", "w_in", "gla_w_gate", "gla_b_gate", "gla_norm_g", "q_norm_g", "k_norm_g", "w_out",
             "norm2_g", "w_up", "conv_w", "conv_b", "w_down"]
    ws = dict(w_ada=w_ada, b_ada=b_ada, norm1_g=norm1_g, w_in=w_in, gla_w_gate=gla_w_gate, gla_b_gate=gla_b_gate,
              gla_norm_g=gla_norm_g, q_norm_g=q_norm_g, k_norm_g=k_norm_g, w_out=w_out, norm2_g=norm2_g, w_up=w_up,
              conv_w=conv_w, conv_b=conv_b, w_down=w_down)
    ms = dict(w_ada=m_w_ada, b_ada=m_b_ada, norm1_g=m_norm1_g, w_in=m_w_in, gla_w_gate=m_gla_w_gate, gla_b_gate=m_gla_b_gate,
              gla_norm_g=m_gla_norm_g, q_norm_g=m_q_norm_g, k_norm_g=m_k_norm_g, w_out=m_w_out, norm2_g=m_norm2_g, w_up=m_w_up,
              conv_w=m_conv_w, conv_b=m_conv_b, w_down=m_w_down)
    vs = dict(w_ada=v_w_ada, b_ada=v_b_ada, norm1_g=v_norm1_g, w_in=v_w_in, gla_w_gate=v_gla_w_gate, gla_b_gate=v_gla_b_gate,
              gla_norm_g=v_gla_norm_g, q_norm_g=v_q_norm_g, k_norm_g=v_k_norm_g, w_out=v_w_out, norm2_g=v_norm2_g, w_up=v_w_up,
              conv_w=v_conv_w, conv_b=v_conv_b, w_down=v_w_down)
    g_out, d_out, m_out, v_out = [], [], [], []
    for nm in names:
        w2 = ws[nm].reshape(ws[nm].shape[-2:])
        g2 = grads[nm].reshape(w2.shape)
        dl, mn, vn = _adamw(w2, g2, ms[nm].reshape(w2.shape), vs[nm].reshape(w2.shape), name=f"adamw_{nm}")
        shape = ws[nm].shape
        g_out.append(g2.reshape(shape))
        d_out.append(dl.reshape(shape))
        m_out.append(mn.reshape(shape))
        v_out.append(vn.reshape(shape))
    return (loss8[0, 0], grad_x[None], *g_out, *d_out, *m_out, *v_out)
```

```python
import functools

import jax
import jax.numpy as jnp
from jax import lax
from jax.experimental import pallas as pl
from jax.experimental.pallas import tpu as pltpu

F32, BF16 = jnp.float32, jnp.bfloat16
HIGHEST = lax.Precision.HIGHEST
MESH = pl.DeviceIdType.MESH

D_MODEL = 1024
GLA_CHUNK = 64
GLA_GATE_TAU = 16.0
GLA_GATE_RANK = 16
HEAD_LANES = 128
ATTN_BLOCK = 128
DILATIONS = (1, 4, 16)
ALIBI_SLOPES = tuple(2.0 ** (-(h + 1)) for h in range(8))
D_FF = 2816
EPS = 1e-6
C_GQ, C_GK, C_GV, C_GR, C_AQ, C_AK, C_AV, C_LR, PROJ_W = 0, 256, 512, 1024, 1536, 2048, 2560, 3072, 3200
ADAM_LR, ADAM_B1, ADAM_B2, ADAM_EPS, ADAM_WD, ADAM_STEP = 0.001, 0.9, 0.999, 1e-08, 0.01, 10
VMEM_LIMIT_BYTES = 56 * 1024 * 1024
ROW_TILE = 256


def _params(*sem):
    return pltpu.CompilerParams(dimension_semantics=sem or None, vmem_limit_bytes=VMEM_LIMIT_BYTES)


def _nt(a, b):
    return lax.dot_general(a, b, (((1,), (1,)), ((), ())), preferred_element_type=F32)


def _tn(a, b):
    return lax.dot_general(a, b, (((0,), (0,)), ((), ())), preferred_element_type=F32)


def _nn(a, b, precision=None):
    return jnp.dot(a, b, preferred_element_type=F32, precision=precision)


def _split3(v):
    hi = v.astype(BF16)
    rest = v - hi.astype(F32)
    mid = rest.astype(BF16)
    return hi, mid, (rest - mid.astype(F32)).astype(BF16)


def _sum_right(v, ones):
    hi, mid, lo = _split3(v)
    return (_nn(lo, ones) + _nn(mid, ones)) + _nn(hi, ones)


def _sum_left(ones, v):
    hi, mid, lo = _split3(v)
    return (_nn(ones, lo) + _nn(ones, mid)) + _nn(ones, hi)


def _fold8(v):
    return v.reshape(v.shape[0] // 8, 8, v.shape[1]).sum(axis=0)


def _spread_total(ref):
    t = ref[...]
    ref[...] = jnp.broadcast_to(jnp.sum(t, axis=-2, keepdims=True), t.shape)


def _sigmoid(x):
    return 1.0 / (1.0 + jnp.exp(-x))


def _mm(a, b, *, ta=False, tb=False, out_dtype=F32, tm, tn, tk, shard_cols=False, also_bf16=False, name):
    (k_a, m) = a.shape if ta else a.shape[::-1]
    (k_b, n) = b.shape[::-1] if tb else b.shape
    assert k_a == k_b and m % tm == 0 and n % tn == 0 and k_a % tk == 0, (name, a.shape, b.shape)
    nk = k_a // tk
    assert nk == 1 or out_dtype == F32, name
    dims = (((0 if ta else 1,), (1 if tb else 0,)), ((), ()))

    def body(a_ref, b_ref, o_ref, *rounded):
        k = pl.program_id(2)
        part = lax.dot_general(a_ref[...].astype(BF16), b_ref[...].astype(BF16), dims, preferred_element_type=F32)
        if nk == 1:
            o_ref[...] = part.astype(out_dtype)
        else:
            @pl.when(k == 0)
            def _():
                o_ref[...] = part

            @pl.when(k > 0)
            def _():
                o_ref[...] += part

        if also_bf16:
            @pl.when(k == nk - 1)
            def _():
                rounded[0][...] = o_ref[...].astype(BF16)

    a_spec = pl.BlockSpec((tk, tm), lambda i, j, k: (k, i)) if ta else pl.BlockSpec((tm, tk), lambda i, j, k: (i, k))
    b_spec = pl.BlockSpec((tn, tk), lambda i, j, k: (j, k)) if tb else pl.BlockSpec((tk, tn), lambda i, j, k: (k, j))
    if shard_cols:
        o_spec, o_shape = pl.BlockSpec((None, tm, tn), lambda i, j, k: (j, i, 0)), (n // tn, m, tn)
    else:
        o_spec, o_shape = pl.BlockSpec((tm, tn), lambda i, j, k: (i, j)), (m, n)
    shapes = [jax.ShapeDtypeStruct(o_shape, out_dtype)] + ([jax.ShapeDtypeStruct(o_shape, BF16)] if also_bf16 else [])
    out = pl.pallas_call(
        body, name=name, grid=(m // tm, n // tn, nk), in_specs=[a_spec, b_spec], out_specs=[o_spec] * len(shapes),
        out_shape=shapes, compiler_params=_params("parallel", "parallel", "arbitrary"))(a, b)
    return out if also_bf16 else out[0]


def _norm_mod_fwd(x, branch, gate, gain, scale, shift, *, name):
    s, d = x.shape
    tm = ROW_TILE
    has_branch = branch is not None

    def body(*refs):
        if has_branch:
            x_ref, br_ref, gate_ref, gain_ref, sc_ref, sh_ref, x1_ref, h_ref = refs
            xv = x_ref[...] + gate_ref[...] * br_ref[...]
            x1_ref[...] = xv
        else:
            x_ref, gain_ref, sc_ref, sh_ref, h_ref = refs
            xv = x_ref[...]
        r = lax.rsqrt(jnp.mean(xv * xv, axis=-1, keepdims=True) + EPS)
        h_ref[...] = ((xv * r) * gain_ref[...] * (1.0 + sc_ref[...]) + sh_ref[...]).astype(BF16)

    row = pl.BlockSpec((tm, d), lambda i: (i, 0))
    vec = pl.BlockSpec((1, d), lambda i: (0, 0))
    if has_branch:
        return pl.pallas_call(
            body, name=name, grid=(s // tm,), in_specs=[row, row, vec, vec, vec, vec], out_specs=[row, row],
            out_shape=[jax.ShapeDtypeStruct((s, d), F32), jax.ShapeDtypeStruct((s, d), BF16)],
            compiler_params=_params("parallel"))(x, branch, gate, gain, scale, shift)
    h = pl.pallas_call(
        body, name=name, grid=(s // tm,), in_specs=[row, vec, vec, vec], out_specs=row,
        out_shape=jax.ShapeDtypeStruct((s, d), BF16), compiler_params=_params("parallel"))(x, gain, scale, shift)
    return x, h


def _norm_mod_bwd(x, dh, dres, gain, scale, branch, gate, *, name):
    s, d = x.shape
    tm = ROW_TILE
    has_branch = branch is not None

    def body(*refs):
        if has_branch:
            x_ref, dh_ref, dres_ref, gain_ref, sc_ref, br_ref, gate_ref, dx_ref, dbr_ref, sums_ref = refs
        else:
            x_ref, dh_ref, dres_ref, gain_ref, sc_ref, dx_ref, sums_ref = refs
        i = pl.program_id(0)

        @pl.when(i == 0)
        def _():
            sums_ref[...] = jnp.zeros_like(sums_ref)

        xv, dhv = x_ref[...], dh_ref[...]
        r = lax.rsqrt(jnp.mean(xv * xv, axis=-1, keepdims=True) + EPS)
        xn = xv * r
        dxn = dhv * (gain_ref[...] * (1.0 + sc_ref[...]))
        dx = dres_ref[...] + r * (dxn - xn * jnp.mean(dxn * xn, axis=-1, keepdims=True))
        dx_ref[...] = dx
        sums_ref[0] += _fold8(dhv * xn)
        sums_ref[1] += _fold8(dhv)
        if has_branch:
            dbr_ref[...] = (gate_ref[...] * dx).astype(BF16)
            sums_ref[2] += _fold8(dx * br_ref[...])

        @pl.when(i == s // tm - 1)
        def _():
            _spread_total(sums_ref)

    row = pl.BlockSpec((tm, d), lambda i: (i, 0))
    vec = pl.BlockSpec((1, d), lambda i: (0, 0))
    sums = pl.BlockSpec((3, 8, d), lambda i: (0, 0, 0))
    sums_shape = jax.ShapeDtypeStruct((3, 8, d), F32)
    if has_branch:
        return pl.pallas_call(
            body, name=name, grid=(s // tm,), in_specs=[row, row, row, vec, vec, row, vec], out_specs=[row, row, sums],
            out_shape=[jax.ShapeDtypeStruct((s, d), F32), jax.ShapeDtypeStruct((s, d), BF16), sums_shape],
            compiler_params=_params("arbitrary"))(x, dh, dres, gain, scale, branch, gate)
    dx, sm = pl.pallas_call(
        body, name=name, grid=(s // tm,), in_specs=[row, row, row, vec, vec], out_specs=[row, sums],
        out_shape=[jax.ShapeDtypeStruct((s, d), F32), sums_shape],
        compiler_params=_params("arbitrary"))(x, dh, dres, gain, scale)
    return dx, None, sm


GLA_ROWS = 256


def _gla_chunk_setup(lr_ref, wg_ref, bg_ref, rows):
    c = GLA_CHUNK
    ri = lax.broadcasted_iota(jnp.int32, (c, c), 0)
    ci = lax.broadcasted_iota(jnp.int32, (c, c), 1)
    z = _nn(lr_ref[rows, :].astype(BF16), wg_ref[...]) + bg_ref[...]
    g = (jnp.minimum(z, 0.0) - jnp.log(1.0 + jnp.exp(-jnp.abs(z)))) * (1.0 / GLA_GATE_TAU)
    b = _sum_left((ci <= ri).astype(BF16), g)
    return z, b, ci <= ri


def _last_row(b):
    ri = lax.broadcasted_iota(jnp.int32, b.shape, 0)
    return jnp.sum(jnp.where(ri == b.shape[0] - 1, b, 0.0), axis=0, keepdims=True)


def _gla_fwd(proj, wg, bg, gn, *, name):
    s = proj.shape[0]
    tb, c = GLA_ROWS, GLA_CHUNK
    cb = tb // c

    def body(q_ref, k_ref, v_ref, r_ref, lr_ref, wg_ref, bg_ref, gn_ref, o_ref, y_ref, st_ref, state):
        i = pl.program_id(0)

        @pl.when(i == 0)
        def _():
            state[...] = jnp.zeros_like(state)

        low = lax.broadcasted_iota(jnp.int32, (c, HEAD_LANES), 1) < 64
        for ch in range(cb):
            rows = pl.ds(ch * c, c)
            _, b, causal = _gla_chunk_setup(lr_ref, wg_ref, bg_ref, rows)
            for p in range(2):
                cols = pl.ds(p * HEAD_LANES, HEAD_LANES)
                bp = b[:, p * HEAD_LANES:(p + 1) * HEAD_LANES]
                b_end = _last_row(bp)
                q = q_ref[rows, cols] * 0.125
                k = k_ref[rows, cols]
                q_in = q * jnp.exp(bp)
                k_out = (k * jnp.exp(-bp)).astype(BF16)
                k_end = k * jnp.exp(b_end - bp)
                st = state[p]
                st_ref[ch, p] = st
                st_b = st.astype(BF16)
                upd = jnp.zeros_like(st)
                for e in range(2):
                    msk = low if e == 0 else jnp.logical_not(low)
                    hc = pl.ds((2 * p + e) * HEAD_LANES, HEAD_LANES)
                    qm = jnp.where(msk, q_in, 0.0).astype(BF16)
                    a = jnp.where(causal, _nt(qm, k_out), 0.0)
                    v = v_ref[rows, hc].astype(BF16)
                    o = _nt(qm, st_b) + _nn(a.astype(BF16), v)
                    upd = upd + _tn(v, jnp.where(msk, k_end, 0.0).astype(BF16))
                    o_ref[rows, hc] = o
                    rr = r_ref[rows, hc]
                    on = o * lax.rsqrt(jnp.mean(o * o, axis=-1, keepdims=True) + EPS)
                    y_ref[rows, hc] = (on * gn_ref[...] * (rr * _sigmoid(rr))).astype(BF16)
                state[p] = st * jnp.exp(b_end) + upd

    def col(width, at):
        return pl.BlockSpec((tb, width), lambda i: (i, at // width))

    full = lambda shape: pl.BlockSpec(shape, lambda i: tuple(0 for _ in shape))
    return pl.pallas_call(
        body, name=name, grid=(s // tb,),
        in_specs=[col(256, C_GQ), col(256, C_GK), col(512, C_GV), col(512, C_GR), col(128, C_LR),
                  full((HEAD_LANES, 256)), full((1, 256)), full((1, HEAD_LANES))],
        out_specs=[pl.BlockSpec((tb, 512), lambda i: (i, 0)), pl.BlockSpec((tb, 512), lambda i: (i, 0)),
                   pl.BlockSpec((cb, 2, HEAD_LANES, HEAD_LANES), lambda i: (i, 0, 0, 0))],
        out_shape=[jax.ShapeDtypeStruct((s, 512), F32), jax.ShapeDtypeStruct((s, 512), BF16),
                   jax.ShapeDtypeStruct((s // c, 2, HEAD_LANES, HEAD_LANES), F32)],
        scratch_shapes=[pltpu.VMEM((2, HEAD_LANES, HEAD_LANES), F32)],
        compiler_params=_params("arbitrary"))(proj, proj, proj, proj, proj, wg, bg, gn)


def _gla_bwd(proj, wg, bg, gn, o_raw, states, dmixed, *, name):
    s = proj.shape[0]
    tb, c = GLA_ROWS, GLA_CHUNK
    cb = tb // c
    nblk, nch = s // tb, s // c

    def body(q_ref, k_ref, v_ref, r_ref, lr_ref, wg_ref, bg_ref, gn_ref, o_ref, st_ref, stn_ref, dy_ref,
             dq_ref, dk_ref, dv_ref, dr_ref, dlr_ref, gwg_ref, sums_ref, dstate):
        i = pl.program_id(0)

        @pl.when(i == 0)
        def _():
            dstate[...] = jnp.zeros_like(dstate)
            gwg_ref[...] = jnp.zeros_like(gwg_ref)
            sums_ref[...] = jnp.zeros_like(sums_ref)

        low = lax.broadcasted_iota(jnp.int32, (c, HEAD_LANES), 1) < 64
        for ch in reversed(range(cb)):
            rows = pl.ds(ch * c, c)
            z, b, causal = _gla_chunk_setup(lr_ref, wg_ref, bg_ref, rows)
            upper = jnp.logical_not(causal) | (lax.broadcasted_iota(jnp.int32, (c, c), 0)
                                               == lax.broadcasted_iota(jnp.int32, (c, c), 1))
            lr_b = lr_ref[rows, :].astype(BF16)
            dlr = jnp.zeros((c, HEAD_LANES), F32)
            for p in range(2):
                cols = pl.ds(p * HEAD_LANES, HEAD_LANES)
                sl = slice(p * HEAD_LANES, (p + 1) * HEAD_LANES)
                bp = b[:, sl]
                b_end = _last_row(bp)
                e_in, e_out, e_end = jnp.exp(bp), jnp.exp(-bp), jnp.exp(b_end - bp)
                q = q_ref[rows, cols] * 0.125
                k = k_ref[rows, cols]
                q_in = q * e_in
                k_out = k * e_out
                k_end = k * e_end
                st0 = st_ref[ch, p]
                st1 = st_ref[ch + 1, p] if ch + 1 < cb else stn_ref[0, p]
                dst = dstate[p]
                st0_b, dst_b = st0.astype(BF16), dst.astype(BF16)
                dq_in = jnp.zeros((c, HEAD_LANES), F32)
                dk_out = jnp.zeros((c, HEAD_LANES), F32)
                dk_end = jnp.zeros((c, HEAD_LANES), F32)
                dst_new = dst * jnp.exp(b_end)
                for e in range(2):
                    msk = low if e == 0 else jnp.logical_not(low)
                    hc = pl.ds((2 * p + e) * HEAD_LANES, HEAD_LANES)
                    o = o_ref[rows, hc]
                    rr = r_ref[rows, hc]
                    dy = dy_ref[rows, hc]
                    sg = _sigmoid(rr)
                    rs = lax.rsqrt(jnp.mean(o * o, axis=-1, keepdims=True) + EPS)
                    on = o * rs
                    t = dy * (rr * sg)
                    sums_ref[1, :, hc] += _fold8(t * on)
                    dn = t * gn_ref[...]
                    do = (rs * (dn - on * jnp.mean(dn * on, axis=-1, keepdims=True))).astype(BF16)
                    dr_ref[rows, hc] = (dy * on * gn_ref[...] * (sg * (1.0 + rr * (1.0 - sg)))).astype(BF16)
                    qm = jnp.where(msk, q_in, 0.0).astype(BF16)
                    km_out = jnp.where(msk, k_out, 0.0).astype(BF16)
                    km_end = jnp.where(msk, k_end, 0.0).astype(BF16)
                    v = v_ref[rows, hc].astype(BF16)
                    a = jnp.where(causal, _nt(qm, km_out), 0.0).astype(BF16)
                    da = jnp.where(causal, _nt(do, v), 0.0).astype(BF16)
                    dv_ref[rows, hc] = (_tn(a, do) + _nt(km_end, dst_b)).astype(BF16)
                    dq_in = dq_in + jnp.where(msk, _nn(do, st0_b) + _nn(da, km_out), 0.0)
                    dk_out = dk_out + _tn(da, qm)
                    dk_end = dk_end + jnp.where(msk, _nn(v, dst_b), 0.0)
                    dst_new = dst_new + _tn(do, qm)
                dq = dq_in * e_in
                dk = dk_out * e_out + dk_end * e_end
                dq_ref[rows, cols] = (dq * 0.125).astype(BF16)
                dk_ref[rows, cols] = dk.astype(BF16)
                w = q * dq - k * dk
                dg = _sum_left(upper.astype(BF16), w) + jnp.sum(dst * st1, axis=0, keepdims=True)
                zp = z[:, sl]
                dz = dg * (1.0 / GLA_GATE_TAU) * _sigmoid(-zp)
                dz_b = dz.astype(BF16)
                sums_ref[0, :, cols] += _fold8(dz)
                dlr = dlr + _nt(dz_b, wg_ref[:, cols])
                gwg_ref[:, cols] += _tn(lr_b, dz_b)
                dstate[p] = dst_new
            dlr_ref[rows, :] = dlr.astype(BF16)

        @pl.when(i == nblk - 1)
        def _():
            _spread_total(sums_ref)

    rev = lambda i: nblk - 1 - i

    def col(width, at):
        return pl.BlockSpec((tb, width), lambda i: (rev(i), at // width))

    full = lambda shape: pl.BlockSpec(shape, lambda i: tuple(0 for _ in shape))
    out_col = lambda width: pl.BlockSpec((tb, width), lambda i: (rev(i), 0))
    return pl.pallas_call(
        body, name=name, grid=(nblk,),
        in_specs=[col(256, C_GQ), col(256, C_GK), col(512, C_GV), col(512, C_GR), col(128, C_LR),
                  full((HEAD_LANES, 256)), full((1, 256)), full((1, HEAD_LANES)),
                  pl.BlockSpec((tb, 512), lambda i: (rev(i), 0)),
                  pl.BlockSpec((cb, 2, HEAD_LANES, HEAD_LANES), lambda i: (rev(i), 0, 0, 0)),
                  pl.BlockSpec((1, 2, HEAD_LANES, HEAD_LANES), lambda i: (jnp.minimum((rev(i) + 1) * cb, nch - 1), 0, 0, 0)),
                  pl.BlockSpec((tb, 512), lambda i: (rev(i), 0))],
        out_specs=[out_col(256), out_col(256), out_col(512), out_col(512), out_col(128),
                   full((HEAD_LANES, 256)), full((2, 8, 512))],
        out_shape=[jax.ShapeDtypeStruct((s, 256), BF16), jax.ShapeDtypeStruct((s, 256), BF16),
                   jax.ShapeDtypeStruct((s, 512), BF16), jax.ShapeDtypeStruct((s, 512), BF16),
                   jax.ShapeDtypeStruct((s, 128), BF16), jax.ShapeDtypeStruct((HEAD_LANES, 256), F32),
                   jax.ShapeDtypeStruct((2, 8, 512), F32)],
        scratch_shapes=[pltpu.VMEM((2, HEAD_LANES, HEAD_LANES), F32)],
        compiler_params=_params("arbitrary"))(proj, proj, proj, proj, proj, wg, bg, gn, o_raw, states, states, dmixed)


def _head_sum_matrix():
    ri = lax.broadcasted_iota(jnp.int32, (512, 512), 0) // 64
    ci = lax.broadcasted_iota(jnp.int32, (512, 512), 1) // 64
    return (ri == ci).astype(BF16)


def _attn_prep(proj, qg, kg, *, name):
    s = proj.shape[0]
    tm = ROW_TILE

    def body(q_ref, k_ref, qg_ref, kg_ref, qa_ref, ka_ref):
        hs = _head_sum_matrix()
        q, k = q_ref[...], k_ref[...]
        qr = lax.rsqrt(_sum_right(q * q, hs) * (1.0 / 64) + EPS)
        kr = lax.rsqrt(_sum_right(k * k, hs) * (1.0 / 64) + EPS)
        qa_ref[...] = q * qr * qg_ref[...] * 0.125
        ka_ref[...] = k * kr * kg_ref[...]

    col = lambda at: pl.BlockSpec((tm, 512), lambda i: (i, at // 512))
    vec = pl.BlockSpec((1, 512), lambda i: (0, 0))
    out = pl.BlockSpec((tm, 512), lambda i: (i, 0))
    return pl.pallas_call(
        body, name=name, grid=(s // tm,), in_specs=[col(C_AQ), col(C_AK), vec, vec], out_specs=[out] * 2,
        out_shape=[jax.ShapeDtypeStruct((s, 512), F32)] * 2, compiler_params=_params("parallel"))(proj, proj, qg, kg)


FAR = 1e30


def _attn_distance(first):
    blk = ATTN_BLOCK
    iq = lax.broadcasted_iota(jnp.int32, (2 * blk, 2 * blk), 0) & (blk - 1)
    ik = lax.broadcasted_iota(jnp.int32, (2 * blk, 2 * blk), 1)
    rel = iq + blk - ik
    valid = (rel >= 0) & (rel <= blk) & (jnp.logical_not(first) | (ik >= blk))
    return jnp.where(valid, rel.astype(F32), FAR)


def _stack_heads(t2):
    low = lax.broadcasted_iota(jnp.int32, t2.shape, 1) < 64
    return jnp.concatenate([jnp.where(low, t2, 0.0), jnp.where(low, 0.0, t2)], axis=0).astype(BF16)


def _unstack_heads(t):
    blk = ATTN_BLOCK
    low = lax.broadcasted_iota(jnp.int32, (blk, HEAD_LANES), 1) < 64
    return jnp.where(low, t[0:blk], t[blk:2 * blk])


def _attn_scores(qs, kcat, slopes, dil, dist):
    top = lax.broadcasted_iota(jnp.int32, (2 * ATTN_BLOCK, 1), 0) < ATTN_BLOCK
    return _nt(qs, kcat) - jnp.where(top, slopes[0] * dil, slopes[1] * dil) * dist


def _pair_slopes(p):
    if isinstance(p, int):
        return ALIBI_SLOPES[2 * p], ALIBI_SLOPES[2 * p + 1]
    pick = lambda e: jnp.where(p == 0, ALIBI_SLOPES[e], jnp.where(p == 1, ALIBI_SLOPES[2 + e],
                               jnp.where(p == 2, ALIBI_SLOPES[4 + e], ALIBI_SLOPES[6 + e])))
    return pick(0), pick(1)


def _attn_pair_fwd(q2, kcat, vcat, slopes, dil, dist):
    sc = _attn_scores(_stack_heads(q2), kcat, slopes, dil, dist)
    m = jnp.max(sc, axis=-1, keepdims=True)
    pr = jnp.exp(sc - m)
    den = jnp.sum(pr, axis=-1, keepdims=True)
    o = _nn(pr.astype(BF16), vcat) / den
    lse = jnp.broadcast_to(m + jnp.log(den), o.shape)
    return _unstack_heads(o), _unstack_heads(lse)


def _attn_pair_bwd(q2, kcat, vcat, do2, y2, lse2, slopes, dil, dist):
    lane = lax.broadcasted_iota(jnp.int32, (ATTN_BLOCK, HEAD_LANES), 1)
    low = lane < 64
    prod = do2 * y2
    per_head = lambda t, pick: jnp.concatenate([jnp.sum(jnp.where(pick(0), t, 0.0), axis=-1, keepdims=True),
                                                jnp.sum(jnp.where(pick(1), t, 0.0), axis=-1, keepdims=True)], axis=0)
    lse = per_head(lse2, lambda e: lane == 64 * e)
    delta = per_head(prod, lambda e: low if e == 0 else jnp.logical_not(low))
    qs, dos = _stack_heads(q2), _stack_heads(do2)
    pr = jnp.exp(_attn_scores(qs, kcat, slopes, dil, dist) - lse)
    ds = (pr * (_nt(dos, vcat) - delta)).astype(BF16)
    return _unstack_heads(_nn(ds, kcat)), _tn(ds, qs), _tn(pr.astype(BF16), dos)


def _attn_specs(dil):
    rows = ATTN_BLOCK * dil
    if dil == 1:
        cur = lambda at: pl.BlockSpec((rows, 512), lambda n: (n, at // 512))
        prev = lambda at: pl.BlockSpec((rows, 512), lambda n: (jnp.maximum(n - 1, 0), at // 512))
    else:
        cur = lambda at: pl.BlockSpec((rows, HEAD_LANES), lambda n, p: (n, at // HEAD_LANES + p))
        prev = lambda at: pl.BlockSpec((rows, HEAD_LANES), lambda n, p: (jnp.maximum(n - 1, 0), at // HEAD_LANES + p))
    return cur, prev


def _attn_loop(dil, one_pair):
    if dil == 1:
        for p in range(4):
            one_pair(slice(None), pl.ds(p * HEAD_LANES, HEAD_LANES), p)
    else:
        p = pl.program_id(1)

        def step(r, carry):
            one_pair(pl.ds(r, ATTN_BLOCK, stride=dil), slice(None), p)
            return carry

        lax.fori_loop(0, dil, step, 0, unroll=min(dil, 4))


def _dil_attn_fwd(qa, ka, proj, dil, *, name):
    s = qa.shape[0]

    def body(q_ref, kp_ref, kc_ref, vp_ref, vc_ref, o_ref, lse_ref):
        dist = _attn_distance(pl.program_id(0) == 0)

        def one_pair(rows, cols, p):
            kcat = jnp.concatenate([kp_ref[rows, cols], kc_ref[rows, cols]], axis=0).astype(BF16)
            vcat = jnp.concatenate([vp_ref[rows, cols], vc_ref[rows, cols]], axis=0).astype(BF16)
            o2, lse2 = _attn_pair_fwd(q_ref[rows, cols], kcat, vcat, _pair_slopes(p), dil, dist)
            o_ref[rows, cols] = o2
            lse_ref[rows, cols] = lse2

        _attn_loop(dil, one_pair)

    cur, prev = _attn_specs(dil)
    grid = (s // ATTN_BLOCK,) if dil == 1 else (s // (ATTN_BLOCK * dil), 4)
    return pl.pallas_call(
        body, name=name, grid=grid, in_specs=[cur(0), prev(0), cur(0), prev(C_AV), cur(C_AV)], out_specs=[cur(0), cur(0)],
        out_shape=[jax.ShapeDtypeStruct((s, 512), F32)] * 2,
        compiler_params=_params(*["parallel"] * len(grid)))(qa, ka, ka, proj, proj)


def _attn_merge(branches, y_gla, *, name):
    s = y_gla.shape[0]
    tm = ROW_TILE

    def body(o0, l0, o1, l1, o2, l2, yg_ref, mixed_ref, y_ref, lse_ref):
        m = jnp.maximum(jnp.maximum(l0[...], l1[...]), l2[...])
        w0, w1, w2 = jnp.exp(l0[...] - m), jnp.exp(l1[...] - m), jnp.exp(l2[...] - m)
        zs = w0 + w1 + w2
        y = (w0 * o0[...] + w1 * o1[...] + w2 * o2[...]) / zs
        y_ref[...] = y
        lse_ref[...] = m + jnp.log(zs)
        mixed_ref[:, 0:512] = yg_ref[...]
        mixed_ref[:, 512:1024] = y.astype(BF16)

    blk = pl.BlockSpec((tm, 512), lambda i: (i, 0))
    args = [t for pair in branches for t in pair]
    return pl.pallas_call(
        body, name=name, grid=(s // tm,), in_specs=[blk] * 7,
        out_specs=[pl.BlockSpec((tm, 1024), lambda i: (i, 0)), blk, blk],
        out_shape=[jax.ShapeDtypeStruct((s, 1024), BF16), jax.ShapeDtypeStruct((s, 512), F32),
                   jax.ShapeDtypeStruct((s, 512), F32)],
        compiler_params=_params("parallel"))(*args, y_gla)


def _dil_attn_bwd(qa, ka, proj, y_att, lse, dmixed, dil, *, name):
    s = qa.shape[0]
    blk = ATTN_BLOCK

    def body(q_ref, kp_ref, kc_ref, vp_ref, vc_ref, y_ref, lse_ref, do_ref, dq_ref, dkc_ref, dkp_ref, dvc_ref, dvp_ref):
        dist = _attn_distance(pl.program_id(0) == 0)

        def one_pair(rows, cols, p):
            kcat = jnp.concatenate([kp_ref[rows, cols], kc_ref[rows, cols]], axis=0).astype(BF16)
            vcat = jnp.concatenate([vp_ref[rows, cols], vc_ref[rows, cols]], axis=0).astype(BF16)
            dq, dk, dv = _attn_pair_bwd(q_ref[rows, cols], kcat, vcat, do_ref[rows, cols], y_ref[rows, cols],
                                        lse_ref[rows, cols], _pair_slopes(p), dil, dist)
            dq_ref[rows, cols] = dq
            dkp_ref[rows, cols] = dk[0:blk]
            dkc_ref[rows, cols] = dk[blk:2 * blk]
            dvp_ref[rows, cols] = dv[0:blk]
            dvc_ref[rows, cols] = dv[blk:2 * blk]

        _attn_loop(dil, one_pair)

    cur, prev = _attn_specs(dil)
    grid = (s // blk,) if dil == 1 else (s // (blk * dil), 4)
    return pl.pallas_call(
        body, name=name, grid=grid,
        in_specs=[cur(0), prev(0), cur(0), prev(C_AV), cur(C_AV), cur(0), cur(0), cur(512)], out_specs=[cur(0)] * 5,
        out_shape=[jax.ShapeDtypeStruct((s, 512), F32)] * 5, compiler_params=_params(*["parallel"] * len(grid)),
    )(qa, ka, ka, proj, proj, y_att, lse, dmixed)


def _attn_post(parts, proj, qg, kg, *, name):
    s = proj.shape[0]
    tm = ATTN_BLOCK
    nblk = s // tm

    def body(*refs):
        ins, (q_ref, k_ref, qg_ref, kg_ref, dq_out, dk_out, dv_out, sums_ref) = refs[:15], refs[15:]
        i = pl.program_id(0)

        @pl.when(i == 0)
        def _():
            sums_ref[...] = jnp.zeros_like(sums_ref)

        dq = jnp.zeros((tm, 512), F32)
        dk = jnp.zeros((tm, 512), F32)
        dv = jnp.zeros((tm, 512), F32)
        for g, dil in enumerate(DILATIONS):
            dq_r, dkc_r, dkp_r, dvc_r, dvp_r = ins[5 * g:5 * g + 5]
            inside = (i + dil < nblk).astype(F32)
            dq = dq + dq_r[...]
            dk = dk + dkc_r[...] + inside * dkp_r[...]
            dv = dv + dvc_r[...] + inside * dvp_r[...]
        dv_out[...] = dv.astype(BF16)
        hs = _head_sum_matrix()
        for row, (x_ref, g_ref, dy, out, post) in enumerate(((q_ref, qg_ref, dq, dq_out, 0.125), (k_ref, kg_ref, dk, dk_out, 1.0))):
            x = x_ref[...]
            rs = lax.rsqrt(_sum_right(x * x, hs) * (1.0 / 64) + EPS)
            xn = x * rs
            dy = dy * post
            sums_ref[row] += _fold8(dy * xn)
            dn = dy * g_ref[...]
            out[...] = (rs * (dn - xn * (_sum_right(dn * xn, hs) * (1.0 / 64)))).astype(BF16)

        @pl.when(i == nblk - 1)
        def _():
            _spread_total(sums_ref)

    here = pl.BlockSpec((tm, 512), lambda i: (i, 0))
    specs = []
    for dil in DILATIONS:
        later = pl.BlockSpec((tm, 512), lambda i, dil=dil: (jnp.minimum(i + dil, nblk - 1), 0))
        specs += [here, here, later, here, later]
    col = lambda at: pl.BlockSpec((tm, 512), lambda i: (i, at // 512))
    vec = pl.BlockSpec((1, 512), lambda i: (0, 0))
    return pl.pallas_call(
        body, name=name, grid=(nblk,), in_specs=specs + [col(C_AQ), col(C_AK), vec, vec],
        out_specs=[here, here, here, pl.BlockSpec((2, 8, 512), lambda i: (0, 0, 0))],
        out_shape=[jax.ShapeDtypeStruct((s, 512), BF16)] * 3 + [jax.ShapeDtypeStruct((2, 8, 512), F32)],
        compiler_params=_params("arbitrary"))(*[t for part in parts for t in part], proj, proj, qg, kg)


FFN_TM, FFN_TN = 256, 1408
HALO = 16


def _conv3(u_ref, halo_ref, w_ref, b_ref, first):
    u = u_ref[...].astype(F32)
    ext = jnp.concatenate([jnp.where(first, 0.0, halo_ref[...].astype(F32)), u], axis=0)
    u1 = pltpu.roll(ext, 1, 0)[HALO:]
    u2 = pltpu.roll(ext, 2, 0)[HALO:]
    return b_ref[...] + w_ref[0:1, :] * u2 + w_ref[1:2, :] * u1 + w_ref[2:3, :] * u, u, u1, u2


def _ffn_specs(tm, tn):
    nj = D_FF // tn
    blk = lambda half: pl.BlockSpec((tm, tn), lambda j, i: (i, j + half * nj))
    halo = lambda half: pl.BlockSpec((HALO, tn), lambda j, i: (jnp.maximum(i * (tm // HALO) - 1, 0), j + half * nj))
    wspec = lambda half: pl.BlockSpec((3, tn), lambda j, i: (0, j + half * nj))
    bspec = lambda half: pl.BlockSpec((1, tn), lambda j, i: (0, j + half * nj))
    return [blk(0), halo(0), blk(1), halo(1), wspec(0), wspec(1), bspec(0), bspec(1)]


def _conv_swiglu_fwd(u, conv_w, conv_b, *, name):
    s = u.shape[0]
    tm, tn = FFN_TM, FFN_TN

    def body(ug_ref, hg_ref, uv_ref, hv_ref, wg_ref, wv_ref, bg_ref, bv_ref, act_ref):
        first = pl.program_id(1) == 0
        cg = _conv3(ug_ref, hg_ref, wg_ref, bg_ref, first)[0]
        cv = _conv3(uv_ref, hv_ref, wv_ref, bv_ref, first)[0]
        act_ref[...] = (cg * _sigmoid(cg) * cv).astype(BF16)

    return pl.pallas_call(
        body, name=name, grid=(D_FF // tn, s // tm), in_specs=_ffn_specs(tm, tn),
        out_specs=pl.BlockSpec((tm, tn), lambda j, i: (i, j)), out_shape=jax.ShapeDtypeStruct((s, D_FF), BF16),
        compiler_params=_params("parallel", "parallel"))(u, u, u, u, conv_w, conv_w, conv_b, conv_b)


def _conv_swiglu_bwd_pre(u, conv_w, conv_b, dact, *, name):
    s = u.shape[0]
    tm, tn = FFN_TM, FFN_TN

    def body(ug_ref, hg_ref, uv_ref, hv_ref, wg_ref, wv_ref, bg_ref, bv_ref, da_ref, duc_ref, sums_ref):
        i = pl.program_id(1)

        @pl.when(i == 0)
        def _():
            sums_ref[...] = jnp.zeros_like(sums_ref)

        cg, g0, g1, g2 = _conv3(ug_ref, hg_ref, wg_ref, bg_ref, i == 0)
        cv, v0, v1, v2 = _conv3(uv_ref, hv_ref, wv_ref, bv_ref, i == 0)
        da = da_ref[...].astype(F32)
        sg = _sigmoid(cg)
        dg = da * cv * (sg * (1.0 + cg * (1.0 - sg)))
        dv = da * (cg * sg)
        duc_ref[0] = dg.astype(BF16)
        duc_ref[1] = dv.astype(BF16)
        for half, (d, taps) in enumerate(((dg, (g2, g1, g0)), (dv, (v2, v1, v0)))):
            for t, tap in enumerate(taps):
                sums_ref[half, t] += _fold8(d * tap)
            sums_ref[half, 3] += _fold8(d)

        @pl.when(i == s // tm - 1)
        def _():
            _spread_total(sums_ref)

    return pl.pallas_call(
        body, name=name, grid=(D_FF // tn, s // tm),
        in_specs=_ffn_specs(tm, tn) + [pl.BlockSpec((tm, tn), lambda j, i: (i, j))],
        out_specs=[pl.BlockSpec((2, tm, tn), lambda j, i: (0, i, j)), pl.BlockSpec((2, 4, 8, tn), lambda j, i: (0, 0, 0, j))],
        out_shape=[jax.ShapeDtypeStruct((2, s, D_FF), BF16), jax.ShapeDtypeStruct((2, 4, 8, D_FF), F32)],
        compiler_params=_params("parallel", "arbitrary"))(u, u, u, u, conv_w, conv_w, conv_b, conv_b, dact)


def _conv_bwd(duc, conv_w, *, name):
    _, s, _ = duc.shape
    tm, tn = FFN_TM, FFN_TN
    nj, ni = D_FF // tn, s // tm

    def body(d_ref, halo_ref, w_ref, du_ref):
        last = pl.program_id(2) == ni - 1
        d = d_ref[0].astype(F32)
        ext = jnp.concatenate([d, jnp.where(last, 0.0, halo_ref[0].astype(F32))], axis=0)
        n = tm + HALO
        d1 = pltpu.roll(ext, n - 1, 0)[:tm]
        d2 = pltpu.roll(ext, n - 2, 0)[:tm]
        du_ref[...] = (w_ref[2:3, :] * d + w_ref[1:2, :] * d1 + w_ref[0:1, :] * d2).astype(BF16)

    return pl.pallas_call(
        body, name=name, grid=(2, nj, ni),
        in_specs=[pl.BlockSpec((1, tm, tn), lambda g, j, i: (g, i, j)),
                  pl.BlockSpec((1, HALO, tn), lambda g, j, i: (g, jnp.minimum((i + 1) * (tm // HALO), s // HALO - 1), j)),
                  pl.BlockSpec((3, tn), lambda g, j, i: (0, g * nj + j))],
        out_specs=pl.BlockSpec((tm, tn), lambda g, j, i: (i, g * nj + j)),
        out_shape=jax.ShapeDtypeStruct((s, 2 * D_FF), BF16),
        compiler_params=_params("parallel", "parallel", "parallel"))(duc, duc, conv_w)


def _loss_head(x1, ffn, gate, target, *, name):
    s, d = x1.shape
    tm = ROW_TILE

    def body(x_ref, f_ref, g_ref, t_ref, dy_ref, df_ref, sums_ref):
        i = pl.program_id(0)

        @pl.when(i == 0)
        def _():
            sums_ref[...] = jnp.zeros_like(sums_ref)

        f = f_ref[...]
        err = x_ref[...] + g_ref[...] * f - t_ref[...]
        dy = err * (1.0 / d)
        dy_ref[...] = dy
        df_ref[...] = (g_ref[...] * dy).astype(BF16)
        sums_ref[0] += _fold8(dy * f)
        sums_ref[1] += _fold8(err * err)

        @pl.when(i == s // tm - 1)
        def _():
            _spread_total(sums_ref)

    row = pl.BlockSpec((tm, d), lambda i: (i, 0))
    return pl.pallas_call(
        body, name=name, grid=(s // tm,), in_specs=[row, row, pl.BlockSpec((1, d), lambda i: (0, 0)), row],
        out_specs=[row, row, pl.BlockSpec((2, 8, d), lambda i: (0, 0, 0))],
        out_shape=[jax.ShapeDtypeStruct((s, d), F32), jax.ShapeDtypeStruct((s, d), BF16), jax.ShapeDtypeStruct((2, 8, d), F32)],
        compiler_params=_params("arbitrary"))(x1, ffn, gate, target)


def _adamw(w, g, m, v, *, name):
    rows, cols = w.shape
    tm = next((t for t in range(ROW_TILE, 7, -8) if rows % t == 0), rows)

    def body(w_ref, g_ref, m_ref, v_ref, d_ref, mo_ref, vo_ref):
        gv = g_ref[...]
        mn = ADAM_B1 * m_ref[...] + (1.0 - ADAM_B1) * gv
        vn = ADAM_B2 * v_ref[...] + (1.0 - ADAM_B2) * (gv * gv)
        m_hat = mn / (1.0 - ADAM_B1 ** ADAM_STEP)
        v_hat = vn / (1.0 - ADAM_B2 ** ADAM_STEP)
        d_ref[...] = -ADAM_LR * (m_hat / (jnp.sqrt(v_hat) + ADAM_EPS) + ADAM_WD * w_ref[...])
        mo_ref[...] = mn
        vo_ref[...] = vn

    blk = pl.BlockSpec((tm, cols), lambda i: (i, 0))
    return pl.pallas_call(
        body, name=name, grid=(rows // tm,), in_specs=[blk] * 4, out_specs=[blk] * 3,
        out_shape=[jax.ShapeDtypeStruct((rows, cols), F32)] * 3, compiler_params=_params("parallel"))(w, g, m, v)


def _colsum(t):
    return t[..., 0, :]


def _in_proj_layout(w_in):
    pad = jnp.zeros((w_in.shape[0], PROJ_W - C_LR - GLA_GATE_RANK), w_in.dtype)
    return jnp.concatenate([w_in[:, :1536], w_in[:, 1552:], w_in[:, 1536:1552], pad], axis=1)


def _in_proj_grad_layout(g):
    return jnp.concatenate([g[:, :1536], g[:, C_LR:C_LR + GLA_GATE_RANK], g[:, 1536:C_LR]], axis=1)


def _gate_layout(gla_w_gate):
    return jnp.pad(gla_w_gate, ((0, HEAD_LANES - GLA_GATE_RANK), (0, 0))).astype(BF16)


def _local_step(x, target, mod, wi, wo, ffn_weights, ffn_grads_ready, conv_w, conv_b, wg, bg, gn, qg, kg, n1g, n2g):
    d = D_MODEL
    sh1, sc1, g1, sh2, sc2, g2 = [mod[:, i * d:(i + 1) * d] for i in range(6)]
    qg8, kg8 = jnp.tile(qg, (1, 8)), jnp.tile(kg, (1, 8))

    _, h1 = _norm_mod_fwd(x, None, None, n1g, sc1, sh1, name="norm1_fwd")
    proj = _mm(h1, wi, tm=1024, tn=PROJ_W, tk=d, name="in_proj")
    o_raw, y_gla, states = _gla_fwd(proj, wg, bg, gn, name="gla_fwd")
    qa, ka = _attn_prep(proj, qg8, kg8, name="attn_prep")
    branches = [_dil_attn_fwd(qa, ka, proj, dil, name=f"attn_fwd_d{dil}") for dil in DILATIONS]
    mixed, y_att, lse = _attn_merge(branches, y_gla, name="attn_merge")
    attn_out = _mm(mixed, wo, tm=1024, tn=d, tk=d, name="out_proj")
    x1, h2 = _norm_mod_fwd(x, attn_out, g1, n2g, sc2, sh2, name="norm2_fwd")
    wup, wdown = ffn_weights(h2)
    u = _mm(h2, wup, out_dtype=BF16, tm=1024, tn=D_FF, tk=d, name="up_proj")
    act = _conv_swiglu_fwd(u, conv_w, conv_b, name="conv_swiglu_fwd")
    ffn = _mm(act, wdown, tm=1024, tn=d, tk=D_FF, name="down_proj")
    dy, dffn, head_sums = _loss_head(x1, ffn, g2, target, name="loss_head")

    dact = _mm(dffn, wdown, tb=True, out_dtype=BF16, tm=1024, tn=D_FF, tk=d, name="down_proj_dx")
    g_wdown, g_wdown_b = _mm(act, dffn, ta=True, tm=1408, tn=d, tk=1024, also_bf16=True, name="down_proj_dw")
    duc, conv_sums = _conv_swiglu_bwd_pre(u, conv_w, conv_b, dact, name="conv_swiglu_bwd")
    du = _conv_bwd(duc, conv_w, name="conv_bwd")
    dh2 = _mm(du, wup, tb=True, tm=1024, tn=d, tk=1408, name="up_proj_dx")
    g_wup, g_wup_b = _mm(h2, du, ta=True, tm=d, tn=1408, tk=1024, shard_cols=True, also_bf16=True, name="up_proj_dw")
    token = ffn_grads_ready(g_wup_b, g_wdown_b)
    g1_late = g1 if token is None else g1 + token[0:1, 0:1]
    dx1, dao, n2_sums = _norm_mod_bwd(x1, dh2, dy, n2g, sc2, attn_out, g1_late, name="norm2_bwd")

    dmixed = _mm(dao, wo, tb=True, tm=1024, tn=d, tk=d, name="out_proj_dx")
    g_wo = _mm(mixed, dao, ta=True, tm=d, tn=d, tk=1024, name="out_proj_dw")
    dgq, dgk, dgv, dgr, dlr, g_wg, gla_sums = _gla_bwd(proj, wg, bg, gn, o_raw, states, dmixed, name="gla_bwd")
    parts = [_dil_attn_bwd(qa, ka, proj, y_att, lse, dmixed, dil, name=f"attn_bwd_d{dil}") for dil in DILATIONS]
    daq, dak, dav, qk_sums = _attn_post(parts, proj, qg8, kg8, name="attn_post")
    dproj = jnp.concatenate([dgq, dgk, dgv, dgr, daq, dak, dav, dlr], axis=1)
    dh1 = _mm(dproj, wi, tb=True, tm=1024, tn=d, tk=PROJ_W, name="in_proj_dx")
    g_wi = _mm(h1, dproj, ta=True, tm=512, tn=PROJ_W, tk=512, name="in_proj_dw")
    grad_x, _, n1_sums = _norm_mod_bwd(x, dh1, dx1, n1g, sc1, None, None, name="norm1_bwd")

    n1, n2, hs, cs = _colsum(n1_sums), _colsum(n2_sums), _colsum(head_sums), _colsum(conv_sums)
    gs, qs = _colsum(gla_sums), _colsum(qk_sums)
    dmod = jnp.concatenate([n1[1], n1[0] * n1g[0], n2[2], n2[1], n2[0] * n2g[0], hs[0]])
    small = dict(
        dmod=dmod,
        norm1_g=n1[0] * (1.0 + sc1[0]), norm2_g=n2[0] * (1.0 + sc2[0]),
        gla_w_gate=g_wg[:GLA_GATE_RANK], gla_b_gate=gs[0, :256], gla_norm_g=gs[1].reshape(4, 128).sum(axis=0),
        q_norm_g=qs[0].reshape(8, 64).sum(axis=0), k_norm_g=qs[1].reshape(8, 64).sum(axis=0),
        conv_w=jnp.concatenate([cs[0, :3], cs[1, :3]], axis=1), conv_b=jnp.concatenate([cs[0, 3], cs[1, 3]]),
    )
    return head_sums[1], grad_x, (g_wi, g_wo, g_wup, g_wdown), small


N_DEV, N_CHIP = 8, 4
ANY = pl.BlockSpec(memory_space=pl.ANY)
VMEM_SPEC = pl.BlockSpec(memory_space=pltpu.VMEM)


def _place():
    x, y, c = lax.axis_index("x"), lax.axis_index("y"), lax.axis_index("c")
    other_chips = [(1 - x, y), (x, 1 - y), (1 - x, 1 - y)]
    return x, y, c, (x, y, 1 - c), other_chips


def _all_gather_small(v, *, name):
    m, n = v.shape

    def body(v_ref, out_ref, send_sems, recv_sems, local_sem):
        x, y, c, sibling, chips = _place()
        me = (x, y, c)

        def rows(px, py, pc):
            return out_ref.at[pl.ds((4 * px + 2 * py + pc) * m, m), :]

        def copy(k, block, to, src=None):
            return pltpu.make_async_remote_copy(
                src_ref=rows(*block) if src is None else src, dst_ref=rows(*block), send_sem=send_sems.at[k],
                recv_sem=recv_sems.at[k], device_id=to, device_id_type=MESH)

        mine = pltpu.make_async_copy(v_ref, rows(*me), local_sem)
        mine.start()
        first = [copy(0, me, sibling, src=v_ref)]
        first += [copy(1 + j, me, (*chip, c), src=v_ref) for j, chip in enumerate(chips)]
        for cp in first:
            cp.start()
        passed = [copy(4 + j, (*chip, c), sibling) for j, chip in enumerate(chips)]
        for j, chip in enumerate(chips):
            copy(1 + j, (*chip, c), me).wait_recv()
            passed[j].start()
        copy(0, sibling, me).wait_recv()
        for j, chip in enumerate(chips):
            copy(4 + j, (*chip, 1 - c), me).wait_recv()
        for cp in first + passed:
            cp.wait_send()
        mine.wait()

    return pl.pallas_call(
        body, name=name, out_shape=jax.ShapeDtypeStruct((N_DEV * m, n), v.dtype), in_specs=[VMEM_SPEC], out_specs=VMEM_SPEC,
        scratch_shapes=[pltpu.SemaphoreType.DMA((7,)), pltpu.SemaphoreType.DMA((7,)), pltpu.SemaphoreType.DMA],
    )(v)


def _gather_weight_shards(shards, *, name):
    nw = len(shards)

    def body(*refs):
        srcs, outs, (send_sems, recv_sems) = refs[:nw], refs[nw:2 * nw], refs[2 * nw:]
        x, y, c, sibling, chips = _place()
        index = lambda chip: 2 * chip[0] + chip[1]

        def copy(w, k, src, dst, to):
            return pltpu.make_async_remote_copy(src_ref=src, dst_ref=dst, send_sem=send_sems.at[6 * w + k],
                                                recv_sem=recv_sems.at[6 * w + k], device_id=to, device_id_type=MESH)

        sent = []
        for w, (src_ref, out_ref) in enumerate(zip(srcs, outs)):
            for k, chip in enumerate(chips):
                sent.append(copy(w, k, src_ref.at[c], out_ref.at[2 * x + y, c], (*chip, c)))
                sent[-1].start()
        for w, out_ref in enumerate(outs):
            for k, chip in enumerate(chips):
                landed = out_ref.at[index(chip), c]
                copy(w, k, landed, landed, (*chip, c)).wait_recv()
                sent.append(copy(w, 3 + k, landed, landed, sibling))
                sent[-1].start()
        for w, out_ref in enumerate(outs):
            for k, chip in enumerate(chips):
                passed_on = out_ref.at[index(chip), 1 - c]
                copy(w, 3 + k, passed_on, passed_on, sibling).wait_recv()
        for cp in sent:
            cp.wait_send()

    return pl.pallas_call(
        body, name=name, out_shape=[jax.ShapeDtypeStruct((N_CHIP, *s.shape), s.dtype) for s in shards],
        in_specs=[ANY] * nw, out_specs=[ANY] * nw,
        scratch_shapes=[pltpu.SemaphoreType.DMA((6 * nw,)), pltpu.SemaphoreType.DMA((6 * nw,))],
    )(*shards)


HBM_SPEC = pl.BlockSpec(memory_space=pltpu.HBM)
SEM_SPEC = pl.BlockSpec(memory_space=pltpu.SEMAPHORE)
DATAFLOW_EFFECT = pltpu.SideEffectType.DATAFLOW_SIDE_EFFECTING


def _late_copies(srcs, lands, send_sems, recv_sems):
    x, y, c, _, chips = _place()
    return [pltpu.make_async_remote_copy(
        src_ref=src.at[c], dst_ref=land.at[2 * x + y, c], send_sem=send_sems.at[6 * w + 2 * r + core],
        recv_sem=recv_sems.at[6 * w + 2 * r + c], device_id=(*chip, core), device_id_type=MESH)
        for w, (src, land) in enumerate(zip(srcs, lands)) for r, chip in enumerate(chips) for core in range(2)]


def _gather_late_start(own, *, name):
    nw = len(own)

    def body(*refs):
        srcs, lands, send_sems, recv_sems, token = refs[:nw], refs[nw:2 * nw], refs[2 * nw], refs[2 * nw + 1], refs[-1]
        for cp in _late_copies(srcs, lands, send_sems, recv_sems):
            cp.start()
        token[...] = jnp.zeros_like(token)

    lands = [pltpu.with_memory_space_constraint(lax.empty((N_CHIP, *s.shape), s.dtype), pltpu.HBM) for s in own]
    own = [pltpu.with_memory_space_constraint(s, pltpu.HBM) for s in own]
    out = pl.pallas_call(
        body, name=name,
        out_shape=(pltpu.SemaphoreType.DMA((6 * nw,)), pltpu.SemaphoreType.DMA((6 * nw,)),
                   *[pltpu.HBM(s.shape, s.dtype) for s in own], *[pltpu.HBM(s.shape, s.dtype) for s in lands],
                   jax.ShapeDtypeStruct((8, 128), F32)),
        in_specs=[HBM_SPEC] * (2 * nw), out_specs=(SEM_SPEC, SEM_SPEC, *[HBM_SPEC] * (2 * nw), VMEM_SPEC),
        input_output_aliases={i: 2 + i for i in range(2 * nw)},
        compiler_params=pltpu.CompilerParams(has_side_effects=DATAFLOW_EFFECT))(*own, *lands)
    return out[0], out[1], out[2:2 + nw], out[2 + nw:2 + 2 * nw], out[-1]


def _gather_late_wait(send_sems, recv_sems, own, lands, after, *, name):
    nw = len(own)

    def body(*refs):
        srcs, lands_in, send_sems, recv_sems = refs[:nw], refs[nw:2 * nw], refs[2 * nw], refs[2 * nw + 1]
        x, y, c, _, chips = _place()
        for cp in _late_copies(srcs, lands_in, send_sems, recv_sems):
            cp.wait_send()
        for w, (src, land) in enumerate(zip(srcs, lands_in)):
            for r, chip in enumerate(chips):
                for core in range(2):
                    pltpu.make_async_remote_copy(
                        src_ref=src.at[c], dst_ref=land.at[2 * chip[0] + chip[1], core], send_sem=send_sems.at[6 * w + 2 * r + core],
                        recv_sem=recv_sems.at[6 * w + 2 * r + core], device_id=(*chip, core), device_id_type=MESH).wait_recv()

    out = pl.pallas_call(
        body, name=name, out_shape=(*[pltpu.HBM(s.shape, s.dtype) for s in own], *[pltpu.HBM(s.shape, s.dtype) for s in lands]),
        in_specs=[HBM_SPEC] * (2 * nw) + [SEM_SPEC, SEM_SPEC, ANY], out_specs=tuple([HBM_SPEC] * (2 * nw)),
        input_output_aliases={i: i for i in range(2 * nw)},
        compiler_params=pltpu.CompilerParams(has_side_effects=DATAFLOW_EFFECT))(*own, *lands, send_sems, recv_sems, after)
    return out[:nw], out[nw:]


def _direct_reduce_copies(srcs, lands, send_sems, recv_sems):
    x, y, c, _, _ = _place()
    cps = []
    for w, (src, land) in enumerate(zip(srcs, lands)):
        for rel in range(1, N_DEV):
            tx, ty, tc = (1 - x if rel & 4 else x), (1 - y if rel & 2 else y), (1 - c if rel & 1 else c)
            cps.append(pltpu.make_async_remote_copy(
                src_ref=src.at[2 * tx + ty, tc], dst_ref=land.at[rel - 1], send_sem=send_sems.at[7 * w + rel - 1],
                recv_sem=recv_sems.at[7 * w + rel - 1], device_id=(tx, ty, tc), device_id_type=MESH))
    return cps


def _direct_reduce_start(grads, *, name):
    nw = len(grads)

    def body(*refs):
        srcs, lands, send_sems, recv_sems, token = refs[:nw], refs[nw:2 * nw], refs[2 * nw], refs[2 * nw + 1], refs[-1]
        for cp in _direct_reduce_copies(srcs, lands, send_sems, recv_sems):
            cp.start()
        token[...] = jnp.zeros_like(token)

    lands = [pltpu.with_memory_space_constraint(lax.empty((N_DEV - 1, *g.shape[2:]), g.dtype), pltpu.HBM) for g in grads]
    grads = [pltpu.with_memory_space_constraint(g, pltpu.HBM) for g in grads]
    out = pl.pallas_call(
        body, name=name,
        out_shape=(pltpu.SemaphoreType.DMA((7 * nw,)), pltpu.SemaphoreType.DMA((7 * nw,)),
                   *[pltpu.HBM(g.shape, g.dtype) for g in grads], *[pltpu.HBM(t.shape, t.dtype) for t in lands],
                   jax.ShapeDtypeStruct((8, 128), F32)),
        in_specs=[HBM_SPEC] * (2 * nw), out_specs=(SEM_SPEC, SEM_SPEC, *[HBM_SPEC] * (2 * nw), VMEM_SPEC),
        input_output_aliases={i: 2 + i for i in range(2 * nw)},
        compiler_params=pltpu.CompilerParams(has_side_effects=DATAFLOW_EFFECT))(*grads, *lands)
    return out[0], out[1], out[2:2 + nw], out[2 + nw:2 + 2 * nw], out[-1]


def _direct_reduce_wait(send_sems, recv_sems, grads, lands, after, *, name):
    nw = len(grads)

    def body(*refs):
        srcs, lands_in, send_sems, recv_sems = refs[:nw], refs[nw:2 * nw], refs[2 * nw], refs[2 * nw + 1]
        cps = _direct_reduce_copies(srcs, lands_in, send_sems, recv_sems)
        for cp in cps:
            cp.wait_send()
        for cp in cps:
            cp.wait_recv()

    out = pl.pallas_call(
        body, name=name, out_shape=(*[pltpu.HBM(g.shape, g.dtype) for g in grads], *[pltpu.HBM(t.shape, t.dtype) for t in lands]),
        in_specs=[HBM_SPEC] * (2 * nw) + [SEM_SPEC, SEM_SPEC, ANY], out_specs=tuple([HBM_SPEC] * (2 * nw)),
        input_output_aliases={i: i for i in range(2 * nw)},
        compiler_params=pltpu.CompilerParams(has_side_effects=DATAFLOW_EFFECT))(*grads, *lands, send_sems, recv_sems, after)
    return out[nw:]


def _direct_reduce_add(grad, landed, chip, core, *, name):
    _, r, n = grad.shape
    half = r // 2
    tr = _row_tile(half)
    nb = half // tr

    def body(chip_ref, core_ref, g_ref, t_ref, o_ref):
        acc = g_ref[0]
        for k in range(N_DEV - 1):
            acc = acc + t_ref[k].astype(F32)
        o_ref[...] = acc

    return pl.pallas_call(
        body, name=name,
        grid_spec=pltpu.PrefetchScalarGridSpec(
            num_scalar_prefetch=2, grid=(nb,),
            in_specs=[pl.BlockSpec((1, tr, n), lambda i, chip_ref, core_ref: (chip_ref[0], core_ref[0] * nb + i, 0)),
                      pl.BlockSpec((N_DEV - 1, tr, n), lambda i, chip_ref, core_ref: (0, i, 0))],
            out_specs=pl.BlockSpec((tr, n), lambda i, chip_ref, core_ref: (i, 0))),
        out_shape=jax.ShapeDtypeStruct((half, n), F32), compiler_params=_params("parallel"))(chip, core, grad, landed)


def _pair_exchange_halves(grads, *, name):
    nw = len(grads)

    def body(*refs):
        srcs, outs, (send_sems, recv_sems) = refs[:nw], refs[nw:2 * nw], refs[2 * nw:]
        _, _, c, sibling, _ = _place()
        cps = []
        for w, (src_ref, out_ref) in enumerate(zip(srcs, outs)):
            cps.append(pltpu.make_async_remote_copy(
                src_ref=src_ref.at[:, 1 - c], dst_ref=out_ref, send_sem=send_sems.at[w],
                recv_sem=recv_sems.at[w], device_id=sibling, device_id_type=MESH))
            cps[-1].start()
        for cp in cps:
            cp.wait()

    return pl.pallas_call(
        body, name=name, out_shape=[jax.ShapeDtypeStruct((N_CHIP, *g.shape[2:]), g.dtype) for g in grads],
        in_specs=[ANY] * nw, out_specs=[ANY] * nw,
        scratch_shapes=[pltpu.SemaphoreType.DMA((nw,)), pltpu.SemaphoreType.DMA((nw,))])(*grads)


def _chip_scatter(pairs, *, name):
    nw = len(pairs)

    def body(*refs):
        srcs, outs, (send_sems, recv_sems) = refs[:nw], refs[nw:2 * nw], refs[2 * nw:]
        _, _, c, _, chips = _place()
        cps = []
        for w, (p_ref, out_ref) in enumerate(zip(srcs, outs)):
            for k, chip in enumerate(chips):
                cps.append(pltpu.make_async_remote_copy(
                    src_ref=p_ref.at[2 * chip[0] + chip[1]], dst_ref=out_ref.at[k], send_sem=send_sems.at[3 * w + k],
                    recv_sem=recv_sems.at[3 * w + k], device_id=(*chip, c), device_id_type=MESH))
                cps[-1].start()
        for cp in cps:
            cp.wait()

    return pl.pallas_call(
        body, name=name, out_shape=[jax.ShapeDtypeStruct((3, *p.shape[1:]), p.dtype) for p in pairs],
        in_specs=[ANY] * nw, out_specs=[ANY] * nw,
        scratch_shapes=[pltpu.SemaphoreType.DMA((3 * nw,)), pltpu.SemaphoreType.DMA((3 * nw,))])(*pairs)


def _share_halves(halves, *, name):
    nw = len(halves)

    def body(*refs):
        srcs, outs, (send_sems, recv_sems) = refs[:nw], refs[nw:2 * nw], refs[2 * nw:]
        _, _, _, sibling, _ = _place()
        cps = [pltpu.make_async_remote_copy(src_ref=src_ref, dst_ref=out_ref, send_sem=send_sems.at[w], recv_sem=recv_sems.at[w],
                                            device_id=sibling, device_id_type=MESH)
               for w, (src_ref, out_ref) in enumerate(zip(srcs, outs))]
        for cp in cps:
            cp.start()
        for cp in cps:
            cp.wait()

    return pl.pallas_call(
        body, name=name, out_shape=[jax.ShapeDtypeStruct(h.shape, h.dtype) for h in halves],
        in_specs=[ANY] * nw, out_specs=[ANY] * nw,
        scratch_shapes=[pltpu.SemaphoreType.DMA((nw,)), pltpu.SemaphoreType.DMA((nw,))])(*halves)


def _row_tile(rows, limit=256):
    return next(t for t in range(limit, 15, -16) if rows % t == 0)


def _pair_add(grad, got, core, *, name):
    _, r, n = grad.shape
    half = r // 2
    tr = _row_tile(half)
    nb = half // tr

    def body(core_ref, g_ref, t_ref, f_ref, b_ref):
        acc = g_ref[...] + t_ref[...]
        f_ref[...] = acc
        b_ref[...] = acc.astype(BF16)

    blk = pl.BlockSpec((1, tr, n), lambda j, i, core_ref: (j, i, 0))
    mine = pl.BlockSpec((1, tr, n), lambda j, i, core_ref: (j, core_ref[0] * nb + i, 0))
    return pl.pallas_call(
        body, name=name,
        grid_spec=pltpu.PrefetchScalarGridSpec(num_scalar_prefetch=1, grid=(N_CHIP, nb), in_specs=[mine, blk], out_specs=[blk, blk]),
        out_shape=[jax.ShapeDtypeStruct((N_CHIP, half, n), F32), jax.ShapeDtypeStruct((N_CHIP, half, n), BF16)],
        compiler_params=_params("parallel", "parallel"))(core, grad, got)


def _chip_add(pair, theirs, chip, *, name):
    _, h, n = pair.shape
    tr = _row_tile(h)

    def body(chip_ref, p_ref, t_ref, o_ref):
        o_ref[...] = ((p_ref[0] + t_ref[0].astype(F32)) + t_ref[1].astype(F32)) + t_ref[2].astype(F32)

    return pl.pallas_call(
        body, name=name,
        grid_spec=pltpu.PrefetchScalarGridSpec(
            num_scalar_prefetch=1, grid=(h // tr,),
            in_specs=[pl.BlockSpec((1, tr, n), lambda i, chip_ref: (chip_ref[0], i, 0)),
                      pl.BlockSpec((3, tr, n), lambda i, chip_ref: (0, i, 0))],
            out_specs=pl.BlockSpec((tr, n), lambda i, chip_ref: (i, 0))),
        out_shape=jax.ShapeDtypeStruct((h, n), F32), compiler_params=_params("parallel"))(chip, pair, theirs)


def _sum_devices(gathered, *, name):
    _, m, n = gathered.shape

    def body(g_ref, tot_ref, loss_ref):
        tot = g_ref[0]
        for dev in range(1, N_DEV):
            tot = tot + g_ref[dev]
        tot_ref[...] = tot
        loss_ref[...] = jnp.full((8, n), (0.5 / D_MODEL) * jnp.sum(tot[0:8]), F32)

    return pl.pallas_call(body, name=name, in_specs=[VMEM_SPEC], out_specs=[VMEM_SPEC, VMEM_SPEC],
                          out_shape=[jax.ShapeDtypeStruct((m, n), F32), jax.ShapeDtypeStruct((8, n), F32)])(gathered)


def _ada_mod(cond_all, w_ada_shard, *, name):
    tn = 512

    def body(a_ref, b_ref, o_ref):
        o_ref[...] = _nn(a_ref[...], b_ref[...], precision=HIGHEST)

    return pl.pallas_call(
        body, name=name, grid=(w_ada_shard.shape[1] // tn,),
        in_specs=[pl.BlockSpec(cond_all.shape, lambda j: (0, 0)), pl.BlockSpec((D_MODEL, tn), lambda j: (0, j))],
        out_specs=pl.BlockSpec((N_DEV, tn), lambda j: (0, j)),
        out_shape=jax.ShapeDtypeStruct((N_DEV, w_ada_shard.shape[1]), F32), compiler_params=_params("parallel"))(cond_all, w_ada_shard)


def _ada_grad(cond_all, dmod_cols, *, name):
    tm = 256

    def body(a_ref, b_ref, o_ref):
        o_ref[...] = lax.dot_general(a_ref[...], b_ref[...], (((0,), (0,)), ((), ())), precision=HIGHEST,
                                     preferred_element_type=F32)

    return pl.pallas_call(
        body, name=name, grid=(D_MODEL // tm,),
        in_specs=[pl.BlockSpec((N_DEV, tm), lambda i: (0, i)), pl.BlockSpec(dmod_cols.shape, lambda i: (0, 0))],
        out_specs=pl.BlockSpec((tm, dmod_cols.shape[1]), lambda i: (i, 0)),
        out_shape=jax.ShapeDtypeStruct((D_MODEL, dmod_cols.shape[1]), F32), compiler_params=_params("parallel"))(cond_all, dmod_cols)


def _silu_rows(c8, *, name):
    def body(c_ref, o_ref):
        cv = c_ref[...]
        o_ref[...] = cv * _sigmoid(cv)

    return pl.pallas_call(body, name=name, in_specs=[VMEM_SPEC], out_specs=VMEM_SPEC,
                          out_shape=jax.ShapeDtypeStruct(c8.shape, F32))(c8)


def _rows128(t, rows=None):
    flat = t.reshape(-1, 128)
    return flat if rows is None else jnp.pad(flat, ((0, rows - flat.shape[0]), (0, 0)))


def _from_col_shards(shards, r, n):
    return shards.reshape(N_CHIP, r, n).transpose(1, 0, 2).reshape(r, N_CHIP * n)


def kernel(x, c, w_ada, b_ada, norm1_g, w_in, gla_w_gate, gla_b_gate, gla_norm_g, q_norm_g, k_norm_g, w_out, norm2_g, w_up, conv_w, conv_b, w_down, loss_target, m_w_ada, m_b_ada, m_norm1_g, m_w_in, m_gla_w_gate, m_gla_b_gate, m_gla_norm_g, m_q_norm_g, m_k_norm_g, m_w_out, m_norm2_g, m_w_up, m_conv_w, m_conv_b, m_w_down, v_w_ada, v_b_ada, v_norm1_g, v_w_in, v_gla_w_gate, v_gla_b_gate, v_gla_norm_g, v_q_norm_g, v_k_norm_g, v_w_out, v_norm2_g, v_w_up, v_conv_w, v_conv_b, v_w_down):
    d = D_MODEL
    ax, ay, ac = lax.axis_index("x"), lax.axis_index("y"), lax.axis_index("c")
    chip, dev = 2 * ax + ay, 4 * ax + 2 * ay + ac

    cond = _silu_rows(jnp.broadcast_to(c, (8, d)), name="cond_silu")[0:1]
    small_in = jnp.concatenate([_rows128(cond), _rows128(conv_w[0]), _rows128(gla_w_gate[0])], axis=0)
    small_in = _rows128(small_in, 56)
    got = _all_gather_small(small_in, name="gather_small").reshape(N_DEV, 56, 128)
    cond_all = got[:, 0:8].reshape(N_DEV, d)
    conv_w_full = _from_col_shards(got[0::2, 8:41].reshape(N_CHIP, 3 * 1408 // 128, 128), 3, 1408)
    gate_full = _from_col_shards(got[0::2, 41:49].reshape(N_CHIP, 16 * 64 // 128, 128), GLA_GATE_RANK, 64)
    mod_part = _ada_mod(cond_all, w_ada[0], name="ada_mod")
    mod_got = _all_gather_small(_rows128(mod_part), name="gather_mod").reshape(N_DEV, N_DEV, 1536)
    mod_all = mod_got[0::2].transpose(1, 0, 2).reshape(N_DEV, 6 * d) + b_ada
    mod = lax.dynamic_slice_in_dim(mod_all, dev, 1, axis=0)

    own = [w[0].astype(BF16).reshape(2, w.shape[1] // 2, w.shape[2]) for w in (w_in, w_out, w_up, w_down)]
    with_own = lambda got, mine: [lax.dynamic_update_index_in_dim(t, o, chip, 0) for t, o in zip(got, mine)]
    got_in, got_out = with_own(_gather_weight_shards(own[:2], name="gather_weights"), own[:2])
    w_in_full = got_in.reshape(N_CHIP, d, 772).transpose(1, 0, 2).reshape(d, N_CHIP * 772)
    w_out_full = got_out.reshape(d, d)
    send_sems, recv_sems, own_thru, lands, token = _gather_late_start(own[2:], name="gather_late_start")
    mod = mod + token[0:1, 0:1]

    def ffn_weights(after):
        mine, landed = _gather_late_wait(send_sems, recv_sems, own_thru, lands, after, name="gather_late_wait")
        got_up, got_down = with_own(landed, mine)
        return got_up.reshape(N_CHIP, d, 1408).transpose(1, 0, 2).reshape(d, 2 * D_FF), got_down.reshape(D_FF, d)

    late_reduce = []

    def ffn_grads_ready(g_wup_b, g_wdown_b):
        halves_of = lambda g: g.reshape(N_CHIP, 2, g.shape[-2] // 2, g.shape[-1])
        late_reduce.extend(_direct_reduce_start([halves_of(g_wup_b), halves_of(g_wdown_b.reshape(N_CHIP, D_FF // N_CHIP, d))],
                                                name="reduce_late_start"))
        return late_reduce[4]

    err2, grad_x, (g_wi, g_wo, g_wup, g_wdown), small = _local_step(
        x[0], loss_target[0], mod, _in_proj_layout(w_in_full), w_out_full, ffn_weights, ffn_grads_ready, conv_w_full, conv_b,
        _gate_layout(gate_full), gla_b_gate, gla_norm_g, q_norm_g, k_norm_g, norm1_g, norm2_g)

    pieces = [err2[0], small["dmod"], small["norm1_g"], small["norm2_g"], small["gla_w_gate"].reshape(-1), small["gla_b_gate"],
              small["gla_norm_g"], small["q_norm_g"], small["k_norm_g"], small["conv_w"].reshape(-1), small["conv_b"]]
    sizes = [p.shape[0] for p in pieces]
    at = [sum(sizes[:i]) for i in range(len(sizes) + 1)]
    vec = _rows128(jnp.concatenate(pieces), 288)
    got = _all_gather_small(vec, name="gather_grads").reshape(N_DEV, 288, 128)
    total, loss8 = _sum_devices(got, name="sum_devices")
    total = total.reshape(-1)
    seg = lambda i: total[at[i]:at[i + 1]]
    dmod_all = got.reshape(N_DEV, -1)[:, at[1]:at[2]]
    g_small = dict(
        b_ada=seg(1)[None], norm1_g=seg(2)[None], norm2_g=seg(3)[None],
        gla_w_gate=lax.dynamic_slice_in_dim(seg(4).reshape(GLA_GATE_RANK, 256), chip * 64, 64, axis=1),
        gla_b_gate=seg(5)[None], gla_norm_g=seg(6)[None], q_norm_g=seg(7)[None], k_norm_g=seg(8)[None],
        conv_w=lax.dynamic_slice_in_dim(seg(9).reshape(3, 2 * D_FF), chip * 1408, 1408, axis=1), conv_b=seg(10)[None])
    dmod_cols = lax.dynamic_slice_in_dim(dmod_all.reshape(N_DEV, 6 * d), chip * 1536, 1536, axis=1)
    g_w_ada = _ada_grad(cond_all, dmod_cols, name="ada_grad")

    tags = ("w_in", "w_out")
    g_parts = [_in_proj_grad_layout(g_wi).reshape(d, N_CHIP, 772).transpose(1, 0, 2), g_wo.reshape(N_CHIP, d // N_CHIP, d)]
    core_id, chip_id = jnp.reshape(ac, (1,)).astype(jnp.int32), jnp.reshape(chip, (1,)).astype(jnp.int32)
    got = _pair_exchange_halves([g.reshape(N_CHIP, 2, g.shape[1] // 2, g.shape[2]) for g in g_parts], name="reduce_pair")
    pairs = [_pair_add(g, t, core_id, name=f"reduce_pair_add_{tag}") for g, t, tag in zip(g_parts, got, tags)]
    theirs = _chip_scatter([pb for _, pb in pairs], name="reduce_chips")
    summed = [_chip_add(pf, t, chip_id, name=f"reduce_chips_add_{tag}") for (pf, _), t, tag in zip(pairs, theirs, tags)]
    landed = _direct_reduce_wait(*late_reduce[:4], grad_x, name="reduce_late_wait")
    summed += [_direct_reduce_add(g, t, chip_id, core_id, name=f"reduce_late_add_{tag}")
               for g, t, tag in zip((g_wup, g_wdown.reshape(N_CHIP, D_FF // N_CHIP, d)), landed, ("w_up", "w_down"))]
    others = _share_halves(summed, name="share_pair")
    g_big = [jnp.concatenate([jnp.where(ac == 0, mine, other), jnp.where(ac == 0, other, mine)], axis=0)
             for mine, other in zip(summed, others)]

    grads = dict(w_ada=g_w_ada, w_in=g_big[0], w_out=g_big[1], w_up=g_big[2], w_down=g_big[3], **g_small)
    names = ["w_ada", "b_ada", "norm1_g", "w_in", "gla_w_gate", "gla_b_gate", "gla_norm_g", "q_norm_g", "k_norm_g", "w_out",
             "norm2_g", "w_up", "conv_w", "conv_b", "w_down"]
    ws = dict(w_ada=w_ada, b_ada=b_ada, norm1_g=norm1_g, w_in=w_in, gla_w_gate=gla_w_gate, gla_b_gate=gla_b_gate,
              gla_norm_g=gla_norm_g, q_norm_g=q_norm_g, k_norm_g=k_norm_g, w_out=w_out, norm2_g=norm2_g, w_up=w_up,
              conv_w=conv_w, conv_b=conv_b, w_down=w_down)
    ms = dict(w_ada=m_w_ada, b_ada=m_b_ada, norm1_g=m_norm1_g, w_in=m_w_in, gla_w_gate=m_gla_w_gate, gla_b_gate=m_gla_b_gate,
              gla_norm_g=m_gla_norm_g, q_norm_g=m_q_norm_g, k_norm_g=m_k_norm_g, w_out=m_w_out, norm2_g=m_norm2_g, w_up=m_w_up,
              conv_w=m_conv_w, conv_b=m_conv_b, w_down=m_w_down)
    vs = dict(w_ada=v_w_ada, b_ada=v_b_ada, norm1_g=v_norm1_g, w_in=v_w_in, gla_w_gate=v_gla_w_gate, gla_b_gate=v_gla_b_gate,
              gla_norm_g=v_gla_norm_g, q_norm_g=v_q_norm_g, k_norm_g=v_k_norm_g, w_out=v_w_out, norm2_g=v_norm2_g, w_up=v_w_up,
              conv_w=v_conv_w, conv_b=v_conv_b, w_down=v_w_down)
    g_out, d_out, m_out, v_out = [], [], [], []
    for nm in names:
        w2 = ws[nm].reshape(ws[nm].shape[-2:])
        g2 = grads[nm].reshape(w2.shape)
        dl, mn, vn = _adamw(w2, g2, ms[nm].reshape(w2.shape), vs[nm].reshape(w2.shape), name=f"adamw_{nm}")
        shape = ws[nm].shape
        g_out.append(g2.reshape(shape))
        d_out.append(dl.reshape(shape))
        m_out.append(mn.reshape(shape))
        v_out.append(vn.reshape(shape))
    return (loss8[0, 0], grad_x[None], *g_out, *d_out, *m_out, *v_out)
```

```python
import functools

import jax
import jax.numpy as jnp
from jax import lax
from jax.experimental import pallas as pl
from jax.experimental.pallas import tpu as pltpu

F32, BF16 = jnp.float32, jnp.bfloat16
HIGHEST = lax.Precision.HIGHEST
MESH = pl.DeviceIdType.MESH

D_MODEL = 1024
GLA_CHUNK = 64
GLA_GATE_TAU = 16.0
GLA_GATE_RANK = 16
HEAD_LANES = 128
ATTN_BLOCK = 128
DILATIONS = (1, 4, 16)
ALIBI_SLOPES = tuple(2.0 ** (-(h + 1)) for h in range(8))
D_FF = 2816
EPS = 1e-6
C_GQ, C_GK, C_GV, C_GR, C_AQ, C_AK, C_AV, C_LR, PROJ_W = 0, 256, 512, 1024, 1536, 2048, 2560, 3072, 3200
ADAM_LR, ADAM_B1, ADAM_B2, ADAM_EPS, ADAM_WD, ADAM_STEP = 0.001, 0.9, 0.999, 1e-08, 0.01, 10
VMEM_LIMIT_BYTES = 56 * 1024 * 1024
ROW_TILE = 256


def _params(*sem):
    return pltpu.CompilerParams(dimension_semantics=sem or None, vmem_limit_bytes=VMEM_LIMIT_BYTES)


def _nt(a, b):
    return lax.dot_general(a, b, (((1,), (1,)), ((), ())), preferred_element_type=F32)


def _tn(a, b):
    return lax.dot_general(a, b, (((0,), (0,)), ((), ())), preferred_element_type=F32)


def _nn(a, b, precision=None):
    return jnp.dot(a, b, preferred_element_type=F32, precision=precision)


def _split3(v):
    hi = v.astype(BF16)
    rest = v - hi.astype(F32)
    mid = rest.astype(BF16)
    return hi, mid, (rest - mid.astype(F32)).astype(BF16)


def _sum_right(v, ones):
    hi, mid, lo = _split3(v)
    return (_nn(lo, ones) + _nn(mid, ones)) + _nn(hi, ones)


def _sum_left(ones, v):
    hi, mid, lo = _split3(v)
    return (_nn(ones, lo) + _nn(ones, mid)) + _nn(ones, hi)


def _fold8(v):
    return v.reshape(v.shape[0] // 8, 8, v.shape[1]).sum(axis=0)


def _spread_total(ref):
    t = ref[...]
    ref[...] = jnp.broadcast_to(jnp.sum(t, axis=-2, keepdims=True), t.shape)


def _sigmoid(x):
    return 1.0 / (1.0 + jnp.exp(-x))


def _mm(a, b, *, ta=False, tb=False, out_dtype=F32, tm, tn, tk, shard_cols=False, also_bf16=False, name):
    (k_a, m) = a.shape if ta else a.shape[::-1]
    (k_b, n) = b.shape[::-1] if tb else b.shape
    assert k_a == k_b and m % tm == 0 and n % tn == 0 and k_a % tk == 0, (name, a.shape, b.shape)
    nk = k_a // tk
    assert nk == 1 or out_dtype == F32, name
    dims = (((0 if ta else 1,), (1 if tb else 0,)), ((), ()))

    def body(a_ref, b_ref, o_ref, *rounded):
        k = pl.program_id(2)
        part = lax.dot_general(a_ref[...].astype(BF16), b_ref[...].astype(BF16), dims, preferred_element_type=F32)
        if nk == 1:
            o_ref[...] = part.astype(out_dtype)
        else:
            @pl.when(k == 0)
            def _():
                o_ref[...] = part

            @pl.when(k > 0)
            def _():
                o_ref[...] += part

        if also_bf16:
            @pl.when(k == nk - 1)
            def _():
                rounded[0][...] = o_ref[...].astype(BF16)

    a_spec = pl.BlockSpec((tk, tm), lambda i, j, k: (k, i)) if ta else pl.BlockSpec((tm, tk), lambda i, j, k: (i, k))
    b_spec = pl.BlockSpec((tn, tk), lambda i, j, k: (j, k)) if tb else pl.BlockSpec((tk, tn), lambda i, j, k: (k, j))
    if shard_cols:
        o_spec, o_shape = pl.BlockSpec((None, tm, tn), lambda i, j, k: (j, i, 0)), (n // tn, m, tn)
    else:
        o_spec, o_shape = pl.BlockSpec((tm, tn), lambda i, j, k: (i, j)), (m, n)
    shapes = [jax.ShapeDtypeStruct(o_shape, out_dtype)] + ([jax.ShapeDtypeStruct(o_shape, BF16)] if also_bf16 else [])
    out = pl.pallas_call(
        body, name=name, grid=(m // tm, n // tn, nk), in_specs=[a_spec, b_spec], out_specs=[o_spec] * len(shapes),
        out_shape=shapes, compiler_params=_params("parallel", "parallel", "arbitrary"))(a, b)
    return out if also_bf16 else out[0]


def _norm_mod_fwd(x, branch, gate, gain, scale, shift, *, name):
    s, d = x.shape
    tm = ROW_TILE
    has_branch = branch is not None

    def body(*refs):
        if has_branch:
            x_ref, br_ref, gate_ref, gain_ref, sc_ref, sh_ref, x1_ref, h_ref = refs
            xv = x_ref[...] + gate_ref[...] * br_ref[...]
            x1_ref[...] = xv
        else:
            x_ref, gain_ref, sc_ref, sh_ref, h_ref = refs
            xv = x_ref[...]
        r = lax.rsqrt(jnp.mean(xv * xv, axis=-1, keepdims=True) + EPS)
        h_ref[...] = ((xv * r) * gain_ref[...] * (1.0 + sc_ref[...]) + sh_ref[...]).astype(BF16)

    row = pl.BlockSpec((tm, d), lambda i: (i, 0))
    vec = pl.BlockSpec((1, d), lambda i: (0, 0))
    if has_branch:
        return pl.pallas_call(
            body, name=name, grid=(s // tm,), in_specs=[row, row, vec, vec, vec, vec], out_specs=[row, row],
            out_shape=[jax.ShapeDtypeStruct((s, d), F32), jax.ShapeDtypeStruct((s, d), BF16)],
            compiler_params=_params("parallel"))(x, branch, gate, gain, scale, shift)
    h = pl.pallas_call(
        body, name=name, grid=(s // tm,), in_specs=[row, vec, vec, vec], out_specs=row,
        out_shape=jax.ShapeDtypeStruct((s, d), BF16), compiler_params=_params("parallel"))(x, gain, scale, shift)
    return x, h


def _norm_mod_bwd(x, dh, dres, gain, scale, branch, gate, *, name):
    s, d = x.shape
    tm = ROW_TILE
    has_branch = branch is not None

    def body(*refs):
        if has_branch:
            x_ref, dh_ref, dres_ref, gain_ref, sc_ref, br_ref, gate_ref, dx_ref, dbr_ref, sums_ref = refs
        else:
            x_ref, dh_ref, dres_ref, gain_ref, sc_ref, dx_ref, sums_ref = refs
        i = pl.program_id(0)

        @pl.when(i == 0)
        def _():
            sums_ref[...] = jnp.zeros_like(sums_ref)

        xv, dhv = x_ref[...], dh_ref[...]
        r = lax.rsqrt(jnp.mean(xv * xv, axis=-1, keepdims=True) + EPS)
        xn = xv * r
        dxn = dhv * (gain_ref[...] * (1.0 + sc_ref[...]))
        dx = dres_ref[...] + r * (dxn - xn * jnp.mean(dxn * xn, axis=-1, keepdims=True))
        dx_ref[...] = dx
        sums_ref[0] += _fold8(dhv * xn)
        sums_ref[1] += _fold8(dhv)
        if has_branch:
            dbr_ref[...] = (gate_ref[...] * dx).astype(BF16)
            sums_ref[2] += _fold8(dx * br_ref[...])

        @pl.when(i == s // tm - 1)
        def _():
            _spread_total(sums_ref)

    row = pl.BlockSpec((tm, d), lambda i: (i, 0))
    vec = pl.BlockSpec((1, d), lambda i: (0, 0))
    sums = pl.BlockSpec((3, 8, d), lambda i: (0, 0, 0))
    sums_shape = jax.ShapeDtypeStruct((3, 8, d), F32)
    if has_branch:
        return pl.pallas_call(
            body, name=name, grid=(s // tm,), in_specs=[row, row, row, vec, vec, row, vec], out_specs=[row, row, sums],
            out_shape=[jax.ShapeDtypeStruct((s, d), F32), jax.ShapeDtypeStruct((s, d), BF16), sums_shape],
            compiler_params=_params("arbitrary"))(x, dh, dres, gain, scale, branch, gate)
    dx, sm = pl.pallas_call(
        body, name=name, grid=(s // tm,), in_specs=[row, row, row, vec, vec], out_specs=[row, sums],
        out_shape=[jax.ShapeDtypeStruct((s, d), F32), sums_shape],
        compiler_params=_params("arbitrary"))(x, dh, dres, gain, scale)
    return dx, None, sm


GLA_ROWS = 256


def _gla_chunk_setup(lr_ref, wg_ref, bg_ref, rows):
    c = GLA_CHUNK
    ri = lax.broadcasted_iota(jnp.int32, (c, c), 0)
    ci = lax.broadcasted_iota(jnp.int32, (c, c), 1)
    z = _nn(lr_ref[rows, :].astype(BF16), wg_ref[...]) + bg_ref[...]
    g = (jnp.minimum(z, 0.0) - jnp.log(1.0 + jnp.exp(-jnp.abs(z)))) * (1.0 / GLA_GATE_TAU)
    b = _sum_left((ci <= ri).astype(BF16), g)
    return z, b, ci <= ri


def _last_row(b):
    ri = lax.broadcasted_iota(jnp.int32, b.shape, 0)
    return jnp.sum(jnp.where(ri == b.shape[0] - 1, b, 0.0), axis=0, keepdims=True)


def _gla_fwd(proj, wg, bg, gn, *, name):
    s = proj.shape[0]
    tb, c = GLA_ROWS, GLA_CHUNK
    cb = tb // c

    def body(q_ref, k_ref, v_ref, r_ref, lr_ref, wg_ref, bg_ref, gn_ref, o_ref, y_ref, st_ref, state):
        i = pl.program_id(0)

        @pl.when(i == 0)
        def _():
            state[...] = jnp.zeros_like(state)

        low = lax.broadcasted_iota(jnp.int32, (c, HEAD_LANES), 1) < 64
        for ch in range(cb):
            rows = pl.ds(ch * c, c)
            _, b, causal = _gla_chunk_setup(lr_ref, wg_ref, bg_ref, rows)
            for p in range(2):
                cols = pl.ds(p * HEAD_LANES, HEAD_LANES)
                bp = b[:, p * HEAD_LANES:(p + 1) * HEAD_LANES]
                b_end = _last_row(bp)
                q = q_ref[rows, cols] * 0.125
                k = k_ref[rows, cols]
                q_in = q * jnp.exp(bp)
                k_out = (k * jnp.exp(-bp)).astype(BF16)
                k_end = k * jnp.exp(b_end - bp)
                st = state[p]
                st_ref[ch, p] = st
                st_b = st.astype(BF16)
                upd = jnp.zeros_like(st)
                for e in range(2):
                    msk = low if e == 0 else jnp.logical_not(low)
                    hc = pl.ds((2 * p + e) * HEAD_LANES, HEAD_LANES)
                    qm = jnp.where(msk, q_in, 0.0).astype(BF16)
                    a = jnp.where(causal, _nt(qm, k_out), 0.0)
                    v = v_ref[rows, hc].astype(BF16)
                    o = _nt(qm, st_b) + _nn(a.astype(BF16), v)
                    upd = upd + _tn(v, jnp.where(msk, k_end, 0.0).astype(BF16))
                    o_ref[rows, hc] = o
                    rr = r_ref[rows, hc]
                    on = o * lax.rsqrt(jnp.mean(o * o, axis=-1, keepdims=True) + EPS)
                    y_ref[rows, hc] = (on * gn_ref[...] * (rr * _sigmoid(rr))).astype(BF16)
                state[p] = st * jnp.exp(b_end) + upd

    def col(width, at):
        return pl.BlockSpec((tb, width), lambda i: (i, at // width))

    full = lambda shape: pl.BlockSpec(shape, lambda i: tuple(0 for _ in shape))
    return pl.pallas_call(
        body, name=name, grid=(s // tb,),
        in_specs=[col(256, C_GQ), col(256, C_GK), col(512, C_GV), col(512, C_GR), col(128, C_LR),
                  full((HEAD_LANES, 256)), full((1, 256)), full((1, HEAD_LANES))],
        out_specs=[pl.BlockSpec((tb, 512), lambda i: (i, 0)), pl.BlockSpec((tb, 512), lambda i: (i, 0)),
                   pl.BlockSpec((cb, 2, HEAD_LANES, HEAD_LANES), lambda i: (i, 0, 0, 0))],
        out_shape=[jax.ShapeDtypeStruct((s, 512), F32), jax.ShapeDtypeStruct((s, 512), BF16),
                   jax.ShapeDtypeStruct((s // c, 2, HEAD_LANES, HEAD_LANES), F32)],
        scratch_shapes=[pltpu.VMEM((2, HEAD_LANES, HEAD_LANES), F32)],
        compiler_params=_params("arbitrary"))(proj, proj, proj, proj, proj, wg, bg, gn)


def _gla_bwd(proj, wg, bg, gn, o_raw, states, dmixed, *, name):
    s = proj.shape[0]
    tb, c = GLA_ROWS, GLA_CHUNK
    cb = tb // c
    nblk, nch = s // tb, s // c

    def body(q_ref, k_ref, v_ref, r_ref, lr_ref, wg_ref, bg_ref, gn_ref, o_ref, st_ref, stn_ref, dy_ref,
             dq_ref, dk_ref, dv_ref, dr_ref, dlr_ref, gwg_ref, sums_ref, dstate):
        i = pl.program_id(0)

        @pl.when(i == 0)
        def _():
            dstate[...] = jnp.zeros_like(dstate)
            gwg_ref[...] = jnp.zeros_like(gwg_ref)
            sums_ref[...] = jnp.zeros_like(sums_ref)

        low = lax.broadcasted_iota(jnp.int32, (c, HEAD_LANES), 1) < 64
        for ch in reversed(range(cb)):
            rows = pl.ds(ch * c, c)
            z, b, causal = _gla_chunk_setup(lr_ref, wg_ref, bg_ref, rows)
            upper = jnp.logical_not(causal) | (lax.broadcasted_iota(jnp.int32, (c, c), 0)
                                               == lax.broadcasted_iota(jnp.int32, (c, c), 1))
            lr_b = lr_ref[rows, :].astype(BF16)
            dlr = jnp.zeros((c, HEAD_LANES), F32)
            for p in range(2):
                cols = pl.ds(p * HEAD_LANES, HEAD_LANES)
                sl = slice(p * HEAD_LANES, (p + 1) * HEAD_LANES)
                bp = b[:, sl]
                b_end = _last_row(bp)
                e_in, e_out, e_end = jnp.exp(bp), jnp.exp(-bp), jnp.exp(b_end - bp)
                q = q_ref[rows, cols] * 0.125
                k = k_ref[rows, cols]
                q_in = q * e_in
                k_out = k * e_out
                k_end = k * e_end
                st0 = st_ref[ch, p]
                st1 = st_ref[ch + 1, p] if ch + 1 < cb else stn_ref[0, p]
                dst = dstate[p]
                st0_b, dst_b = st0.astype(BF16), dst.astype(BF16)
                dq_in = jnp.zeros((c, HEAD_LANES), F32)
                dk_out = jnp.zeros((c, HEAD_LANES), F32)
                dk_end = jnp.zeros((c, HEAD_LANES), F32)
                dst_new = dst * jnp.exp(b_end)
                for e in range(2):
                    msk = low if e == 0 else jnp.logical_not(low)
                    hc = pl.ds((2 * p + e) * HEAD_LANES, HEAD_LANES)
                    o = o_ref[rows, hc]
                    rr = r_ref[rows, hc]
                    dy = dy_ref[rows, hc]
                    sg = _sigmoid(rr)
                    rs = lax.rsqrt(jnp.mean(o * o, axis=-1, keepdims=True) + EPS)
                    on = o * rs
                    t = dy * (rr * sg)
                    sums_ref[1, :, hc] += _fold8(t * on)
                    dn = t * gn_ref[...]
                    do = (rs * (dn - on * jnp.mean(dn * on, axis=-1, keepdims=True))).astype(BF16)
                    dr_ref[rows, hc] = (dy * on * gn_ref[...] * (sg * (1.0 + rr * (1.0 - sg)))).astype(BF16)
                    qm = jnp.where(msk, q_in, 0.0).astype(BF16)
                    km_out = jnp.where(msk, k_out, 0.0).astype(BF16)
                    km_end = jnp.where(msk, k_end, 0.0).astype(BF16)
                    v = v_ref[rows, hc].astype(BF16)
                    a = jnp.where(causal, _nt(qm, km_out), 0.0).astype(BF16)
                    da = jnp.where(causal, _nt(do, v), 0.0).astype(BF16)
                    dv_ref[rows, hc] = (_tn(a, do) + _nt(km_end, dst_b)).astype(BF16)
                    dq_in = dq_in + jnp.where(msk, _nn(do, st0_b) + _nn(da, km_out), 0.0)
                    dk_out = dk_out + _tn(da, qm)
                    dk_end = dk_end + jnp.where(msk, _nn(v, dst_b), 0.0)
                    dst_new = dst_new + _tn(do, qm)
                dq = dq_in * e_in
                dk = dk_out * e_out + dk_end * e_end
                dq_ref[rows, cols] = (dq * 0.125).astype(BF16)
                dk_ref[rows, cols] = dk.astype(BF16)
                w = q * dq - k * dk
                dg = _sum_left(upper.astype(BF16), w) + jnp.sum(dst * st1, axis=0, keepdims=True)
                zp = z[:, sl]
                dz = dg * (1.0 / GLA_GATE_TAU) * _sigmoid(-zp)
                dz_b = dz.astype(BF16)
                sums_ref[0, :, cols] += _fold8(dz)
                dlr = dlr + _nt(dz_b, wg_ref[:, cols])
                gwg_ref[:, cols] += _tn(lr_b, dz_b)
                dstate[p] = dst_new
            dlr_ref[rows, :] = dlr.astype(BF16)

        @pl.when(i == nblk - 1)
        def _():
            _spread_total(sums_ref)

    rev = lambda i: nblk - 1 - i

    def col(width, at):
        return pl.BlockSpec((tb, width), lambda i: (rev(i), at // width))

    full = lambda shape: pl.BlockSpec(shape, lambda i: tuple(0 for _ in shape))
    out_col = lambda width: pl.BlockSpec((tb, width), lambda i: (rev(i), 0))
    return pl.pallas_call(
        body, name=name, grid=(nblk,),
        in_specs=[col(256, C_GQ), col(256, C_GK), col(512, C_GV), col(512, C_GR), col(128, C_LR),
                  full((HEAD_LANES, 256)), full((1, 256)), full((1, HEAD_LANES)),
                  pl.BlockSpec((tb, 512), lambda i: (rev(i), 0)),
                  pl.BlockSpec((cb, 2, HEAD_LANES, HEAD_LANES), lambda i: (rev(i), 0, 0, 0)),
                  pl.BlockSpec((1, 2, HEAD_LANES, HEAD_LANES), lambda i: (jnp.minimum((rev(i) + 1) * cb, nch - 1), 0, 0, 0)),
                  pl.BlockSpec((tb, 512), lambda i: (rev(i), 0))],
        out_specs=[out_col(256), out_col(256), out_col(512), out_col(512), out_col(128),
                   full((HEAD_LANES, 256)), full((2, 8, 512))],
        out_shape=[jax.ShapeDtypeStruct((s, 256), BF16), jax.ShapeDtypeStruct((s, 256), BF16),
                   jax.ShapeDtypeStruct((s, 512), BF16), jax.ShapeDtypeStruct((s, 512), BF16),
                   jax.ShapeDtypeStruct((s, 128), BF16), jax.ShapeDtypeStruct((HEAD_LANES, 256), F32),
                   jax.ShapeDtypeStruct((2, 8, 512), F32)],
        scratch_shapes=[pltpu.VMEM((2, HEAD_LANES, HEAD_LANES), F32)],
        compiler_params=_params("arbitrary"))(proj, proj, proj, proj, proj, wg, bg, gn, o_raw, states, states, dmixed)


def _head_sum_matrix():
    ri = lax.broadcasted_iota(jnp.int32, (512, 512), 0) // 64
    ci = lax.broadcasted_iota(jnp.int32, (512, 512), 1) // 64
    return (ri == ci).astype(BF16)


def _attn_prep(proj, qg, kg, *, name):
    s = proj.shape[0]
    tm = ROW_TILE

    def body(q_ref, k_ref, qg_ref, kg_ref, qa_ref, ka_ref):
        hs = _head_sum_matrix()
        q, k = q_ref[...], k_ref[...]
        qr = lax.rsqrt(_sum_right(q * q, hs) * (1.0 / 64) + EPS)
        kr = lax.rsqrt(_sum_right(k * k, hs) * (1.0 / 64) + EPS)
        qa_ref[...] = q * qr * qg_ref[...] * 0.125
        ka_ref[...] = k * kr * kg_ref[...]

    col = lambda at: pl.BlockSpec((tm, 512), lambda i: (i, at // 512))
    vec = pl.BlockSpec((1, 512), lambda i: (0, 0))
    out = pl.BlockSpec((tm, 512), lambda i: (i, 0))
    return pl.pallas_call(
        body, name=name, grid=(s // tm,), in_specs=[col(C_AQ), col(C_AK), vec, vec], out_specs=[out] * 2,
        out_shape=[jax.ShapeDtypeStruct((s, 512), F32)] * 2, compiler_params=_params("parallel"))(proj, proj, qg, kg)


FAR = 1e30


def _attn_distance(first):
    blk = ATTN_BLOCK
    iq = lax.broadcasted_iota(jnp.int32, (2 * blk, 2 * blk), 0) & (blk - 1)
    ik = lax.broadcasted_iota(jnp.int32, (2 * blk, 2 * blk), 1)
    rel = iq + blk - ik
    valid = (rel >= 0) & (rel <= blk) & (jnp.logical_not(first) | (ik >= blk))
    return jnp.where(valid, rel.astype(F32), FAR)


def _stack_heads(t2):
    low = lax.broadcasted_iota(jnp.int32, t2.shape, 1) < 64
    return jnp.concatenate([jnp.where(low, t2, 0.0), jnp.where(low, 0.0, t2)], axis=0).astype(BF16)


def _unstack_heads(t):
    blk = ATTN_BLOCK
    low = lax.broadcasted_iota(jnp.int32, (blk, HEAD_LANES), 1) < 64
    return jnp.where(low, t[0:blk], t[blk:2 * blk])


def _attn_scores(qs, kcat, slopes, dil, dist):
    top = lax.broadcasted_iota(jnp.int32, (2 * ATTN_BLOCK, 1), 0) < ATTN_BLOCK
    return _nt(qs, kcat) - jnp.where(top, slopes[0] * dil, slopes[1] * dil) * dist


def _pair_slopes(p):
    if isinstance(p, int):
        return ALIBI_SLOPES[2 * p], ALIBI_SLOPES[2 * p + 1]
    pick = lambda e: jnp.where(p == 0, ALIBI_SLOPES[e], jnp.where(p == 1, ALIBI_SLOPES[2 + e],
                               jnp.where(p == 2, ALIBI_SLOPES[4 + e], ALIBI_SLOPES[6 + e])))
    return pick(0), pick(1)


def _attn_pair_fwd(q2, kcat, vcat, slopes, dil, dist):
    sc = _attn_scores(_stack_heads(q2), kcat, slopes, dil, dist)
    m = jnp.max(sc, axis=-1, keepdims=True)
    pr = jnp.exp(sc - m)
    den = jnp.sum(pr, axis=-1, keepdims=True)
    o = _nn(pr.astype(BF16), vcat) / den
    lse = jnp.broadcast_to(m + jnp.log(den), o.shape)
    return _unstack_heads(o), _unstack_heads(lse)


def _attn_pair_bwd(q2, kcat, vcat, do2, y2, lse2, slopes, dil, dist):
    lane = lax.broadcasted_iota(jnp.int32, (ATTN_BLOCK, HEAD_LANES), 1)
    low = lane < 64
    prod = do2 * y2
    per_head = lambda t, pick: jnp.concatenate([jnp.sum(jnp.where(pick(0), t, 0.0), axis=-1, keepdims=True),
                                                jnp.sum(jnp.where(pick(1), t, 0.0), axis=-1, keepdims=True)], axis=0)
    lse = per_head(lse2, lambda e: lane == 64 * e)
    delta = per_head(prod, lambda e: low if e == 0 else jnp.logical_not(low))
    qs, dos = _stack_heads(q2), _stack_heads(do2)
    pr = jnp.exp(_attn_scores(qs, kcat, slopes, dil, dist) - lse)
    ds = (pr * (_nt(dos, vcat) - delta)).astype(BF16)
    return _unstack_heads(_nn(ds, kcat)), _tn(ds, qs), _tn(pr.astype(BF16), dos)


def _attn_specs(dil):
    rows = ATTN_BLOCK * dil
    if dil == 1:
        cur = lambda at: pl.BlockSpec((rows, 512), lambda n: (n, at // 512))
        prev = lambda at: pl.BlockSpec((rows, 512), lambda n: (jnp.maximum(n - 1, 0), at // 512))
    else:
        cur = lambda at: pl.BlockSpec((rows, HEAD_LANES), lambda n, p: (n, at // HEAD_LANES + p))
        prev = lambda at: pl.BlockSpec((rows, HEAD_LANES), lambda n, p: (jnp.maximum(n - 1, 0), at // HEAD_LANES + p))
    return cur, prev


def _attn_loop(dil, one_pair):
    if dil == 1:
        for p in range(4):
            one_pair(slice(None), pl.ds(p * HEAD_LANES, HEAD_LANES), p)
    else:
        p = pl.program_id(1)

        def step(r, carry):
            one_pair(pl.ds(r, ATTN_BLOCK, stride=dil), slice(None), p)
            return carry

        lax.fori_loop(0, dil, step, 0, unroll=min(dil, 4))


def _dil_attn_fwd(qa, ka, proj, dil, *, name):
    s = qa.shape[0]

    def body(q_ref, kp_ref, kc_ref, vp_ref, vc_ref, o_ref, lse_ref):
        dist = _attn_distance(pl.program_id(0) == 0)

        def one_pair(rows, cols, p):
            kcat = jnp.concatenate([kp_ref[rows, cols], kc_ref[rows, cols]], axis=0).astype(BF16)
            vcat = jnp.concatenate([vp_ref[rows, cols], vc_ref[rows, cols]], axis=0).astype(BF16)
            o2, lse2 = _attn_pair_fwd(q_ref[rows, cols], kcat, vcat, _pair_slopes(p), dil, dist)
            o_ref[rows, cols] = o2
            lse_ref[rows, cols] = lse2

        _attn_loop(dil, one_pair)

    cur, prev = _attn_specs(dil)
    grid = (s // ATTN_BLOCK,) if dil == 1 else (s // (ATTN_BLOCK * dil), 4)
    return pl.pallas_call(
        body, name=name, grid=grid, in_specs=[cur(0), prev(0), cur(0), prev(C_AV), cur(C_AV)], out_specs=[cur(0), cur(0)],
        out_shape=[jax.ShapeDtypeStruct((s, 512), F32)] * 2,
        compiler_params=_params(*["parallel"] * len(grid)))(qa, ka, ka, proj, proj)


def _attn_merge(branches, y_gla, *, name):
    s = y_gla.shape[0]
    tm = ROW_TILE

    def body(o0, l0, o1, l1, o2, l2, yg_ref, mixed_ref, y_ref, lse_ref):
        m = jnp.maximum(jnp.maximum(l0[...], l1[...]), l2[...])
        w0, w1, w2 = jnp.exp(l0[...] - m), jnp.exp(l1[...] - m), jnp.exp(l2[...] - m)
        zs = w0 + w1 + w2
        y = (w0 * o0[...] + w1 * o1[...] + w2 * o2[...]) / zs
        y_ref[...] = y
        lse_ref[...] = m + jnp.log(zs)
        mixed_ref[:, 0:512] = yg_ref[...]
        mixed_ref[:, 512:1024] = y.astype(BF16)

    blk = pl.BlockSpec((tm, 512), lambda i: (i, 0))
    args = [t for pair in branches for t in pair]
    return pl.pallas_call(
        body, name=name, grid=(s // tm,), in_specs=[blk] * 7,
        out_specs=[pl.BlockSpec((tm, 1024), lambda i: (i, 0)), blk, blk],
        out_shape=[jax.ShapeDtypeStruct((s, 1024), BF16), jax.ShapeDtypeStruct((s, 512), F32),
                   jax.ShapeDtypeStruct((s, 512), F32)],
        compiler_params=_params("parallel"))(*args, y_gla)


def _dil_attn_bwd(qa, ka, proj, y_att, lse, dmixed, dil, *, name):
    s = qa.shape[0]
    blk = ATTN_BLOCK

    def body(q_ref, kp_ref, kc_ref, vp_ref, vc_ref, y_ref, lse_ref, do_ref, dq_ref, dkc_ref, dkp_ref, dvc_ref, dvp_ref):
        dist = _attn_distance(pl.program_id(0) == 0)

        def one_pair(rows, cols, p):
            kcat = jnp.concatenate([kp_ref[rows, cols], kc_ref[rows, cols]], axis=0).astype(BF16)
            vcat = jnp.concatenate([vp_ref[rows, cols], vc_ref[rows, cols]], axis=0).astype(BF16)
            dq, dk, dv = _attn_pair_bwd(q_ref[rows, cols], kcat, vcat, do_ref[rows, cols], y_ref[rows, cols],
                                        lse_ref[rows, cols], _pair_slopes(p), dil, dist)
            dq_ref[rows, cols] = dq
            dkp_ref[rows, cols] = dk[0:blk]
            dkc_ref[rows, cols] = dk[blk:2 * blk]
            dvp_ref[rows, cols] = dv[0:blk]
            dvc_ref[rows, cols] = dv[blk:2 * blk]

        _attn_loop(dil, one_pair)

    cur, prev = _attn_specs(dil)
    grid = (s // blk,) if dil == 1 else (s // (blk * dil), 4)
    return pl.pallas_call(
        body, name=name, grid=grid,
        in_specs=[cur(0), prev(0), cur(0), prev(C_AV), cur(C_AV), cur(0), cur(0), cur(512)], out_specs=[cur(0)] * 5,
        out_shape=[jax.ShapeDtypeStruct((s, 512), F32)] * 5, compiler_params=_params(*["parallel"] * len(grid)),
    )(qa, ka, ka, proj, proj, y_att, lse, dmixed)


def _attn_post(parts, proj, qg, kg, *, name):
    s = proj.shape[0]
    tm = ATTN_BLOCK
    nblk = s // tm

    def body(*refs):
        ins, (q_ref, k_ref, qg_ref, kg_ref, dq_out, dk_out, dv_out, sums_ref) = refs[:15], refs[15:]
        i = pl.program_id(0)

        @pl.when(i == 0)
        def _():
            sums_ref[...] = jnp.zeros_like(sums_ref)

        dq = jnp.zeros((tm, 512), F32)
        dk = jnp.zeros((tm, 512), F32)
        dv = jnp.zeros((tm, 512), F32)
        for g, dil in enumerate(DILATIONS):
            dq_r, dkc_r, dkp_r, dvc_r, dvp_r = ins[5 * g:5 * g + 5]
            inside = (i + dil < nblk).astype(F32)
            dq = dq + dq_r[...]
            dk = dk + dkc_r[...] + inside * dkp_r[...]
            dv = dv + dvc_r[...] + inside * dvp_r[...]
        dv_out[...] = dv.astype(BF16)
        hs = _head_sum_matrix()
        for row, (x_ref, g_ref, dy, out, post) in enumerate(((q_ref, qg_ref, dq, dq_out, 0.125), (k_ref, kg_ref, dk, dk_out, 1.0))):
            x = x_ref[...]
            rs = lax.rsqrt(_sum_right(x * x, hs) * (1.0 / 64) + EPS)
            xn = x * rs
            dy = dy * post
            sums_ref[row] += _fold8(dy * xn)
            dn = dy * g_ref[...]
            out[...] = (rs * (dn - xn * (_sum_right(dn * xn, hs) * (1.0 / 64)))).astype(BF16)

        @pl.when(i == nblk - 1)
        def _():
            _spread_total(sums_ref)

    here = pl.BlockSpec((tm, 512), lambda i: (i, 0))
    specs = []
    for dil in DILATIONS:
        later = pl.BlockSpec((tm, 512), lambda i, dil=dil: (jnp.minimum(i + dil, nblk - 1), 0))
        specs += [here, here, later, here, later]
    col = lambda at: pl.BlockSpec((tm, 512), lambda i: (i, at // 512))
    vec = pl.BlockSpec((1, 512), lambda i: (0, 0))
    return pl.pallas_call(
        body, name=name, grid=(nblk,), in_specs=specs + [col(C_AQ), col(C_AK), vec, vec],
        out_specs=[here, here, here, pl.BlockSpec((2, 8, 512), lambda i: (0, 0, 0))],
        out_shape=[jax.ShapeDtypeStruct((s, 512), BF16)] * 3 + [jax.ShapeDtypeStruct((2, 8, 512), F32)],
        compiler_params=_params("arbitrary"))(*[t for part in parts for t in part], proj, proj, qg, kg)


FFN_TM, FFN_TN = 256, 1408
HALO = 16


def _conv3(u_ref, halo_ref, w_ref, b_ref, first):
    u = u_ref[...].astype(F32)
    ext = jnp.concatenate([jnp.where(first, 0.0, halo_ref[...].astype(F32)), u], axis=0)
    u1 = pltpu.roll(ext, 1, 0)[HALO:]
    u2 = pltpu.roll(ext, 2, 0)[HALO:]
    return b_ref[...] + w_ref[0:1, :] * u2 + w_ref[1:2, :] * u1 + w_ref[2:3, :] * u, u, u1, u2


def _ffn_specs(tm, tn):
    nj = D_FF // tn
    blk = lambda half: pl.BlockSpec((tm, tn), lambda j, i: (i, j + half * nj))
    halo = lambda half: pl.BlockSpec((HALO, tn), lambda j, i: (jnp.maximum(i * (tm // HALO) - 1, 0), j + half * nj))
    wspec = lambda half: pl.BlockSpec((3, tn), lambda j, i: (0, j + half * nj))
    bspec = lambda half: pl.BlockSpec((1, tn), lambda j, i: (0, j + half * nj))
    return [blk(0), halo(0), blk(1), halo(1), wspec(0), wspec(1), bspec(0), bspec(1)]


def _conv_swiglu_fwd(u, conv_w, conv_b, *, name):
    s = u.shape[0]
    tm, tn = FFN_TM, FFN_TN

    def body(ug_ref, hg_ref, uv_ref, hv_ref, wg_ref, wv_ref, bg_ref, bv_ref, act_ref):
        first = pl.program_id(1) == 0
        cg = _conv3(ug_ref, hg_ref, wg_ref, bg_ref, first)[0]
        cv = _conv3(uv_ref, hv_ref, wv_ref, bv_ref, first)[0]
        act_ref[...] = (cg * _sigmoid(cg) * cv).astype(BF16)

    return pl.pallas_call(
        body, name=name, grid=(D_FF // tn, s // tm), in_specs=_ffn_specs(tm, tn),
        out_specs=pl.BlockSpec((tm, tn), lambda j, i: (i, j)), out_shape=jax.ShapeDtypeStruct((s, D_FF), BF16),
        compiler_params=_params("parallel", "parallel"))(u, u, u, u, conv_w, conv_w, conv_b, conv_b)


def _conv_swiglu_bwd_pre(u, conv_w, conv_b, dact, *, name):
    s = u.shape[0]
    tm, tn = FFN_TM, FFN_TN

    def body(ug_ref, hg_ref, uv_ref, hv_ref, wg_ref, wv_ref, bg_ref, bv_ref, da_ref, duc_ref, sums_ref):
        i = pl.program_id(1)

        @pl.when(i == 0)
        def _():
            sums_ref[...] = jnp.zeros_like(sums_ref)

        cg, g0, g1, g2 = _conv3(ug_ref, hg_ref, wg_ref, bg_ref, i == 0)
        cv, v0, v1, v2 = _conv3(uv_ref, hv_ref, wv_ref, bv_ref, i == 0)
        da = da_ref[...].astype(F32)
        sg = _sigmoid(cg)
        dg = da * cv * (sg * (1.0 + cg * (1.0 - sg)))
        dv = da * (cg * sg)
        duc_ref[0] = dg.astype(BF16)
        duc_ref[1] = dv.astype(BF16)
        for half, (d, taps) in enumerate(((dg, (g2, g1, g0)), (dv, (v2, v1, v0)))):
            for t, tap in enumerate(taps):
                sums_ref[half, t] += _fold8(d * tap)
            sums_ref[half, 3] += _fold8(d)

        @pl.when(i == s // tm - 1)
        def _():
            _spread_total(sums_ref)

    return pl.pallas_call(
        body, name=name, grid=(D_FF // tn, s // tm),
        in_specs=_ffn_specs(tm, tn) + [pl.BlockSpec((tm, tn), lambda j, i: (i, j))],
        out_specs=[pl.BlockSpec((2, tm, tn), lambda j, i: (0, i, j)), pl.BlockSpec((2, 4, 8, tn), lambda j, i: (0, 0, 0, j))],
        out_shape=[jax.ShapeDtypeStruct((2, s, D_FF), BF16), jax.ShapeDtypeStruct((2, 4, 8, D_FF), F32)],
        compiler_params=_params("parallel", "arbitrary"))(u, u, u, u, conv_w, conv_w, conv_b, conv_b, dact)


def _conv_bwd(duc, conv_w, *, name):
    _, s, _ = duc.shape
    tm, tn = FFN_TM, FFN_TN
    nj, ni = D_FF // tn, s // tm

    def body(d_ref, halo_ref, w_ref, du_ref):
        last = pl.program_id(2) == ni - 1
        d = d_ref[0].astype(F32)
        ext = jnp.concatenate([d, jnp.where(last, 0.0, halo_ref[0].astype(F32))], axis=0)
        n = tm + HALO
        d1 = pltpu.roll(ext, n - 1, 0)[:tm]
        d2 = pltpu.roll(ext, n - 2, 0)[:tm]
        du_ref[...] = (w_ref[2:3, :] * d + w_ref[1:2, :] * d1 + w_ref[0:1, :] * d2).astype(BF16)

    return pl.pallas_call(
        body, name=name, grid=(2, nj, ni),
        in_specs=[pl.BlockSpec((1, tm, tn), lambda g, j, i: (g, i, j)),
                  pl.BlockSpec((1, HALO, tn), lambda g, j, i: (g, jnp.minimum((i + 1) * (tm // HALO), s // HALO - 1), j)),
                  pl.BlockSpec((3, tn), lambda g, j, i: (0, g * nj + j))],
        out_specs=pl.BlockSpec((tm, tn), lambda g, j, i: (i, g * nj + j)),
        out_shape=jax.ShapeDtypeStruct((s, 2 * D_FF), BF16),
        compiler_params=_params("parallel", "parallel", "parallel"))(duc, duc, conv_w)


def _loss_head(x1, ffn, gate, target, *, name):
    s, d = x1.shape
    tm = ROW_TILE

    def body(x_ref, f_ref, g_ref, t_ref, dy_ref, df_ref, sums_ref):
        i = pl.program_id(0)

        @pl.when(i == 0)
        def _():
            sums_ref[...] = jnp.zeros_like(sums_ref)

        f = f_ref[...]
        err = x_ref[...] + g_ref[...] * f - t_ref[...]
        dy = err * (1.0 / d)
        dy_ref[...] = dy
        df_ref[...] = (g_ref[...] * dy).astype(BF16)
        sums_ref[0] += _fold8(dy * f)
        sums_ref[1] += _fold8(err * err)

        @pl.when(i == s // tm - 1)
        def _():
            _spread_total(sums_ref)

    row = pl.BlockSpec((tm, d), lambda i: (i, 0))
    return pl.pallas_call(
        body, name=name, grid=(s // tm,), in_specs=[row, row, pl.BlockSpec((1, d), lambda i: (0, 0)), row],
        out_specs=[row, row, pl.BlockSpec((2, 8, d), lambda i: (0, 0, 0))],
        out_shape=[jax.ShapeDtypeStruct((s, d), F32), jax.ShapeDtypeStruct((s, d), BF16), jax.ShapeDtypeStruct((2, 8, d), F32)],
        compiler_params=_params("arbitrary"))(x1, ffn, gate, target)


def _adamw(w, g, m, v, *, name):
    rows, cols = w.shape
    tm = next((t for t in range(ROW_TILE, 7, -8) if rows % t == 0), rows)

    def body(w_ref, g_ref, m_ref, v_ref, d_ref, mo_ref, vo_ref):
        gv = g_ref[...]
        mn = ADAM_B1 * m_ref[...] + (1.0 - ADAM_B1) * gv
        vn = ADAM_B2 * v_ref[...] + (1.0 - ADAM_B2) * (gv * gv)
        m_hat = mn / (1.0 - ADAM_B1 ** ADAM_STEP)
        v_hat = vn / (1.0 - ADAM_B2 ** ADAM_STEP)
        d_ref[...] = -ADAM_LR * (m_hat / (jnp.sqrt(v_hat) + ADAM_EPS) + ADAM_WD * w_ref[...])
        mo_ref[...] = mn
        vo_ref[...] = vn

    blk = pl.BlockSpec((tm, cols), lambda i: (i, 0))
    return pl.pallas_call(
        body, name=name, grid=(rows // tm,), in_specs=[blk] * 4, out_specs=[blk] * 3,
        out_shape=[jax.ShapeDtypeStruct((rows, cols), F32)] * 3, compiler_params=_params("parallel"))(w, g, m, v)


def _colsum(t):
    return t[..., 0, :]


def _in_proj_layout(w_in):
    pad = jnp.zeros((w_in.shape[0], PROJ_W - C_LR - GLA_GATE_RANK), w_in.dtype)
    return jnp.concatenate([w_in[:, :1536], w_in[:, 1552:], w_in[:, 1536:1552], pad], axis=1)


def _in_proj_grad_layout(g):
    return jnp.concatenate([g[:, :1536], g[:, C_LR:C_LR + GLA_GATE_RANK], g[:, 1536:C_LR]], axis=1)


def _gate_layout(gla_w_gate):
    return jnp.pad(gla_w_gate, ((0, HEAD_LANES - GLA_GATE_RANK), (0, 0))).astype(BF16)


def _local_step(x, target, mod, wi, wo, ffn_weights, ffn_grads_ready, conv_w, conv_b, wg, bg, gn, qg, kg, n1g, n2g):
    d = D_MODEL
    sh1, sc1, g1, sh2, sc2, g2 = [mod[:, i * d:(i + 1) * d] for i in range(6)]
    qg8, kg8 = jnp.tile(qg, (1, 8)), jnp.tile(kg, (1, 8))

    _, h1 = _norm_mod_fwd(x, None, None, n1g, sc1, sh1, name="norm1_fwd")
    proj = _mm(h1, wi, tm=1024, tn=PROJ_W, tk=d, name="in_proj")
    o_raw, y_gla, states = _gla_fwd(proj, wg, bg, gn, name="gla_fwd")
    qa, ka = _attn_prep(proj, qg8, kg8, name="attn_prep")
    branches = [_dil_attn_fwd(qa, ka, proj, dil, name=f"attn_fwd_d{dil}") for dil in DILATIONS]
    mixed, y_att, lse = _attn_merge(branches, y_gla, name="attn_merge")
    attn_out = _mm(mixed, wo, tm=1024, tn=d, tk=d, name="out_proj")
    x1, h2 = _norm_mod_fwd(x, attn_out, g1, n2g, sc2, sh2, name="norm2_fwd")
    wup, wdown = ffn_weights(h2)
    u = _mm(h2, wup, out_dtype=BF16, tm=1024, tn=D_FF, tk=d, name="up_proj")
    act = _conv_swiglu_fwd(u, conv_w, conv_b, name="conv_swiglu_fwd")
    ffn = _mm(act, wdown, tm=1024, tn=d, tk=D_FF, name="down_proj")
    dy, dffn, head_sums = _loss_head(x1, ffn, g2, target, name="loss_head")

    dact = _mm(dffn, wdown, tb=True, out_dtype=BF16, tm=1024, tn=D_FF, tk=d, name="down_proj_dx")
    g_wdown, g_wdown_b = _mm(act, dffn, ta=True, tm=1408, tn=d, tk=1024, also_bf16=True, name="down_proj_dw")
    duc, conv_sums = _conv_swiglu_bwd_pre(u, conv_w, conv_b, dact, name="conv_swiglu_bwd")
    du = _conv_bwd(duc, conv_w, name="conv_bwd")
    dh2 = _mm(du, wup, tb=True, tm=1024, tn=d, tk=1408, name="up_proj_dx")
    g_wup, g_wup_b = _mm(h2, du, ta=True, tm=d, tn=1408, tk=1024, shard_cols=True, also_bf16=True, name="up_proj_dw")
    token = ffn_grads_ready(g_wup_b, g_wdown_b)
    g1_late = g1 if token is None else g1 + token[0:1, 0:1]
    dx1, dao, n2_sums = _norm_mod_bwd(x1, dh2, dy, n2g, sc2, attn_out, g1_late, name="norm2_bwd")

    dmixed = _mm(dao, wo, tb=True, tm=1024, tn=d, tk=d, name="out_proj_dx")
    g_wo = _mm(mixed, dao, ta=True, tm=d, tn=d, tk=1024, name="out_proj_dw")
    dgq, dgk, dgv, dgr, dlr, g_wg, gla_sums = _gla_bwd(proj, wg, bg, gn, o_raw, states, dmixed, name="gla_bwd")
    parts = [_dil_attn_bwd(qa, ka, proj, y_att, lse, dmixed, dil, name=f"attn_bwd_d{dil}") for dil in DILATIONS]
    daq, dak, dav, qk_sums = _attn_post(parts, proj, qg8, kg8, name="attn_post")
    dproj = jnp.concatenate([dgq, dgk, dgv, dgr, daq, dak, dav, dlr], axis=1)
    dh1 = _mm(dproj, wi, tb=True, tm=1024, tn=d, tk=PROJ_W, name="in_proj_dx")
    g_wi = _mm(h1, dproj, ta=True, tm=512, tn=PROJ_W, tk=512, name="in_proj_dw")
    grad_x, _, n1_sums = _norm_mod_bwd(x, dh1, dx1, n1g, sc1, None, None, name="norm1_bwd")

    n1, n2, hs, cs = _colsum(n1_sums), _colsum(n2_sums), _colsum(head_sums), _colsum(conv_sums)
    gs, qs = _colsum(gla_sums), _colsum(qk_sums)
    dmod = jnp.concatenate([n1[1], n1[0] * n1g[0], n2[2], n2[1], n2[0] * n2g[0], hs[0]])
    small = dict(
        dmod=dmod,
        norm1_g=n1[0] * (1.0 + sc1[0]), norm2_g=n2[0] * (1.0 + sc2[0]),
        gla_w_gate=g_wg[:GLA_GATE_RANK], gla_b_gate=gs[0, :256], gla_norm_g=gs[1].reshape(4, 128).sum(axis=0),
        q_norm_g=qs[0].reshape(8, 64).sum(axis=0), k_norm_g=qs[1].reshape(8, 64).sum(axis=0),
        conv_w=jnp.concatenate([cs[0, :3], cs[1, :3]], axis=1), conv_b=jnp.concatenate([cs[0, 3], cs[1, 3]]),
    )
    return head_sums[1], grad_x, (g_wi, g_wo, g_wup, g_wdown), small


N_DEV, N_CHIP = 8, 4
ANY = pl.BlockSpec(memory_space=pl.ANY)
VMEM_SPEC = pl.BlockSpec(memory_space=pltpu.VMEM)


def _place():
    x, y, c = lax.axis_index("x"), lax.axis_index("y"), lax.axis_index("c")
    other_chips = [(1 - x, y), (x, 1 - y), (1 - x, 1 - y)]
    return x, y, c, (x, y, 1 - c), other_chips


def _all_gather_small(v, *, name):
    m, n = v.shape

    def body(v_ref, out_ref, send_sems, recv_sems, local_sem):
        x, y, c, sibling, chips = _place()
        me = (x, y, c)

        def rows(px, py, pc):
            return out_ref.at[pl.ds((4 * px + 2 * py + pc) * m, m), :]

        def copy(k, block, to, src=None):
            return pltpu.make_async_remote_copy(
                src_ref=rows(*block) if src is None else src, dst_ref=rows(*block), send_sem=send_sems.at[k],
                recv_sem=recv_sems.at[k], device_id=to, device_id_type=MESH)

        mine = pltpu.make_async_copy(v_ref, rows(*me), local_sem)
        mine.start()
        first = [copy(0, me, sibling, src=v_ref)]
        first += [copy(1 + j, me, (*chip, c), src=v_ref) for j, chip in enumerate(chips)]
        for cp in first:
            cp.start()
        passed = [copy(4 + j, (*chip, c), sibling) for j, chip in enumerate(chips)]
        for j, chip in enumerate(chips):
            copy(1 + j, (*chip, c), me).wait_recv()
            passed[j].start()
        copy(0, sibling, me).wait_recv()
        for j, chip in enumerate(chips):
            copy(4 + j, (*chip, 1 - c), me).wait_recv()
        for cp in first + passed:
            cp.wait_send()
        mine.wait()

    return pl.pallas_call(
        body, name=name, out_shape=jax.ShapeDtypeStruct((N_DEV * m, n), v.dtype), in_specs=[VMEM_SPEC], out_specs=VMEM_SPEC,
        scratch_shapes=[pltpu.SemaphoreType.DMA((7,)), pltpu.SemaphoreType.DMA((7,)), pltpu.SemaphoreType.DMA],
    )(v)


def _gather_weight_shards(shards, *, name):
    nw = len(shards)

    def body(*refs):
        srcs, outs, (send_sems, recv_sems) = refs[:nw], refs[nw:2 * nw], refs[2 * nw:]
        x, y, c, sibling, chips = _place()
        index = lambda chip: 2 * chip[0] + chip[1]

        def copy(w, k, src, dst, to):
            return pltpu.make_async_remote_copy(src_ref=src, dst_ref=dst, send_sem=send_sems.at[6 * w + k],
                                                recv_sem=recv_sems.at[6 * w + k], device_id=to, device_id_type=MESH)

        sent = []
        for w, (src_ref, out_ref) in enumerate(zip(srcs, outs)):
            for k, chip in enumerate(chips):
                sent.append(copy(w, k, src_ref.at[c], out_ref.at[2 * x + y, c], (*chip, c)))
                sent[-1].start()
        for w, out_ref in enumerate(outs):
            for k, chip in enumerate(chips):
                landed = out_ref.at[index(chip), c]
                copy(w, k, landed, landed, (*chip, c)).wait_recv()
                sent.append(copy(w, 3 + k, landed, landed, sibling))
                sent[-1].start()
        for w, out_ref in enumerate(outs):
            for k, chip in enumerate(chips):
                passed_on = out_ref.at[index(chip), 1 - c]
                copy(w, 3 + k, passed_on, passed_on, sibling).wait_recv()
        for cp in sent:
            cp.wait_send()

    return pl.pallas_call(
        body, name=name, out_shape=[jax.ShapeDtypeStruct((N_CHIP, *s.shape), s.dtype) for s in shards],
        in_specs=[ANY] * nw, out_specs=[ANY] * nw,
        scratch_shapes=[pltpu.SemaphoreType.DMA((6 * nw,)), pltpu.SemaphoreType.DMA((6 * nw,))],
    )(*shards)


HBM_SPEC = pl.BlockSpec(memory_space=pltpu.HBM)
SEM_SPEC = pl.BlockSpec(memory_space=pltpu.SEMAPHORE)
DATAFLOW_EFFECT = pltpu.SideEffectType.DATAFLOW_SIDE_EFFECTING


def _late_copies(srcs, lands, send_sems, recv_sems):
    x, y, c, _, chips = _place()
    return [pltpu.make_async_remote_copy(
        src_ref=src.at[c], dst_ref=land.at[2 * x + y, c], send_sem=send_sems.at[6 * w + 2 * r + core],
        recv_sem=recv_sems.at[6 * w + 2 * r + c], device_id=(*chip, core), device_id_type=MESH)
        for w, (src, land) in enumerate(zip(srcs, lands)) for r, chip in enumerate(chips) for core in range(2)]


def _gather_late_start(own, after, *, name):
    nw = len(own)

    def body(*refs):
        srcs, lands, send_sems, recv_sems, token = refs[:nw], refs[nw:2 * nw], refs[2 * nw + 1], refs[2 * nw + 2], refs[-1]
        for cp in _late_copies(srcs, lands, send_sems, recv_sems):
            cp.start()
        token[...] = jnp.zeros_like(token)

    lands = [pltpu.with_memory_space_constraint(lax.empty((N_CHIP, *s.shape), s.dtype), pltpu.HBM) for s in own]
    own = [pltpu.with_memory_space_constraint(s, pltpu.HBM) for s in own]
    out = pl.pallas_call(
        body, name=name,
        out_shape=(pltpu.SemaphoreType.DMA((6 * nw,)), pltpu.SemaphoreType.DMA((6 * nw,)),
                   *[pltpu.HBM(s.shape, s.dtype) for s in own], *[pltpu.HBM(s.shape, s.dtype) for s in lands],
                   jax.ShapeDtypeStruct((8, 128), F32)),
        in_specs=[HBM_SPEC] * (2 * nw) + [ANY], out_specs=(SEM_SPEC, SEM_SPEC, *[HBM_SPEC] * (2 * nw), VMEM_SPEC),
        input_output_aliases={i: 2 + i for i in range(2 * nw)},
        compiler_params=pltpu.CompilerParams(has_side_effects=DATAFLOW_EFFECT))(*own, *lands, after)
    return out[0], out[1], out[2:2 + nw], out[2 + nw:2 + 2 * nw], out[-1]


def _gather_late_wait(send_sems, recv_sems, own, lands, after, *, name):
    nw = len(own)

    def body(*refs):
        srcs, lands_in, send_sems, recv_sems = refs[:nw], refs[nw:2 * nw], refs[2 * nw], refs[2 * nw + 1]
        x, y, c, _, chips = _place()
        for cp in _late_copies(srcs, lands_in, send_sems, recv_sems):
            cp.wait_send()
        for w, (src, land) in enumerate(zip(srcs, lands_in)):
            for r, chip in enumerate(chips):
                for core in range(2):
                    pltpu.make_async_remote_copy(
                        src_ref=src.at[c], dst_ref=land.at[2 * chip[0] + chip[1], core], send_sem=send_sems.at[6 * w + 2 * r + core],
                        recv_sem=recv_sems.at[6 * w + 2 * r + core], device_id=(*chip, core), device_id_type=MESH).wait_recv()

    out = pl.pallas_call(
        body, name=name, out_shape=(*[pltpu.HBM(s.shape, s.dtype) for s in own], *[pltpu.HBM(s.shape, s.dtype) for s in lands]),
        in_specs=[HBM_SPEC] * (2 * nw) + [SEM_SPEC, SEM_SPEC, ANY], out_specs=tuple([HBM_SPEC] * (2 * nw)),
        input_output_aliases={i: i for i in range(2 * nw)},
        compiler_params=pltpu.CompilerParams(has_side_effects=DATAFLOW_EFFECT))(*own, *lands, send_sems, recv_sems, after)
    return out[:nw], out[nw:]


def _direct_reduce_copies(srcs, lands, send_sems, recv_sems):
    x, y, c, _, _ = _place()
    cps = []
    for w, (src, land) in enumerate(zip(srcs, lands)):
        for rel in range(1, N_DEV):
            tx, ty, tc = (1 - x if rel & 4 else x), (1 - y if rel & 2 else y), (1 - c if rel & 1 else c)
            cps.append(pltpu.make_async_remote_copy(
                src_ref=src.at[2 * tx + ty, tc], dst_ref=land.at[rel - 1], send_sem=send_sems.at[7 * w + rel - 1],
                recv_sem=recv_sems.at[7 * w + rel - 1], device_id=(tx, ty, tc), device_id_type=MESH))
    return cps


def _direct_reduce_start(grads, *, name):
    nw = len(grads)

    def body(*refs):
        srcs, lands, send_sems, recv_sems, token = refs[:nw], refs[nw:2 * nw], refs[2 * nw], refs[2 * nw + 1], refs[-1]
        for cp in _direct_reduce_copies(srcs, lands, send_sems, recv_sems):
            cp.start()
        token[...] = jnp.zeros_like(token)

    lands = [pltpu.with_memory_space_constraint(lax.empty((N_DEV - 1, *g.shape[2:]), g.dtype), pltpu.HBM) for g in grads]
    grads = [pltpu.with_memory_space_constraint(g, pltpu.HBM) for g in grads]
    out = pl.pallas_call(
        body, name=name,
        out_shape=(pltpu.SemaphoreType.DMA((7 * nw,)), pltpu.SemaphoreType.DMA((7 * nw,)),
                   *[pltpu.HBM(g.shape, g.dtype) for g in grads], *[pltpu.HBM(t.shape, t.dtype) for t in lands],
                   jax.ShapeDtypeStruct((8, 128), F32)),
        in_specs=[HBM_SPEC] * (2 * nw), out_specs=(SEM_SPEC, SEM_SPEC, *[HBM_SPEC] * (2 * nw), VMEM_SPEC),
        input_output_aliases={i: 2 + i for i in range(2 * nw)},
        compiler_params=pltpu.CompilerParams(has_side_effects=DATAFLOW_EFFECT))(*grads, *lands)
    return out[0], out[1], out[2:2 + nw], out[2 + nw:2 + 2 * nw], out[-1]


def _direct_reduce_wait(send_sems, recv_sems, grads, lands, after, *, name):
    nw = len(grads)

    def body(*refs):
        srcs, lands_in, send_sems, recv_sems = refs[:nw], refs[nw:2 * nw], refs[2 * nw], refs[2 * nw + 1]
        cps = _direct_reduce_copies(srcs, lands_in, send_sems, recv_sems)
        for cp in cps:
            cp.wait_send()
        for cp in cps:
            cp.wait_recv()

    out = pl.pallas_call(
        body, name=name, out_shape=(*[pltpu.HBM(g.shape, g.dtype) for g in grads], *[pltpu.HBM(t.shape, t.dtype) for t in lands]),
        in_specs=[HBM_SPEC] * (2 * nw) + [SEM_SPEC, SEM_SPEC, ANY], out_specs=tuple([HBM_SPEC] * (2 * nw)),
        input_output_aliases={i: i for i in range(2 * nw)},
        compiler_params=pltpu.CompilerParams(has_side_effects=DATAFLOW_EFFECT))(*grads, *lands, send_sems, recv_sems, after)
    return out[nw:]


def _direct_reduce_add(grad, landed, chip, core, *, name):
    _, r, n = grad.shape
    half = r // 2
    tr = _row_tile(half)
    nb = half // tr

    def body(chip_ref, core_ref, g_ref, t_ref, o_ref):
        acc = g_ref[0]
        for k in range(N_DEV - 1):
            acc = acc + t_ref[k].astype(F32)
        o_ref[...] = acc

    return pl.pallas_call(
        body, name=name,
        grid_spec=pltpu.PrefetchScalarGridSpec(
            num_scalar_prefetch=2, grid=(nb,),
            in_specs=[pl.BlockSpec((1, tr, n), lambda i, chip_ref, core_ref: (chip_ref[0], core_ref[0] * nb + i, 0)),
                      pl.BlockSpec((N_DEV - 1, tr, n), lambda i, chip_ref, core_ref: (0, i, 0))],
            out_specs=pl.BlockSpec((tr, n), lambda i, chip_ref, core_ref: (i, 0))),
        out_shape=jax.ShapeDtypeStruct((half, n), F32), compiler_params=_params("parallel"))(chip, core, grad, landed)


def _pair_exchange_halves(grads, *, name):
    nw = len(grads)

    def body(*refs):
        srcs, outs, (send_sems, recv_sems) = refs[:nw], refs[nw:2 * nw], refs[2 * nw:]
        _, _, c, sibling, _ = _place()
        cps = []
        for w, (src_ref, out_ref) in enumerate(zip(srcs, outs)):
            cps.append(pltpu.make_async_remote_copy(
                src_ref=src_ref.at[:, 1 - c], dst_ref=out_ref, send_sem=send_sems.at[w],
                recv_sem=recv_sems.at[w], device_id=sibling, device_id_type=MESH))
            cps[-1].start()
        for cp in cps:
            cp.wait()

    return pl.pallas_call(
        body, name=name, out_shape=[jax.ShapeDtypeStruct((N_CHIP, *g.shape[2:]), g.dtype) for g in grads],
        in_specs=[ANY] * nw, out_specs=[ANY] * nw,
        scratch_shapes=[pltpu.SemaphoreType.DMA((nw,)), pltpu.SemaphoreType.DMA((nw,))])(*grads)


def _chip_scatter(pairs, *, name):
    nw = len(pairs)

    def body(*refs):
        srcs, outs, (send_sems, recv_sems) = refs[:nw], refs[nw:2 * nw], refs[2 * nw:]
        _, _, c, _, chips = _place()
        cps = []
        for w, (p_ref, out_ref) in enumerate(zip(srcs, outs)):
            for k, chip in enumerate(chips):
                cps.append(pltpu.make_async_remote_copy(
                    src_ref=p_ref.at[2 * chip[0] + chip[1]], dst_ref=out_ref.at[k], send_sem=send_sems.at[3 * w + k],
                    recv_sem=recv_sems.at[3 * w + k], device_id=(*chip, c), device_id_type=MESH))
                cps[-1].start()
        for cp in cps:
            cp.wait()

    return pl.pallas_call(
        body, name=name, out_shape=[jax.ShapeDtypeStruct((3, *p.shape[1:]), p.dtype) for p in pairs],
        in_specs=[ANY] * nw, out_specs=[ANY] * nw,
        scratch_shapes=[pltpu.SemaphoreType.DMA((3 * nw,)), pltpu.SemaphoreType.DMA((3 * nw,))])(*pairs)


def _share_halves(halves, *, name):
    nw = len(halves)

    def body(*refs):
        srcs, outs, (send_sems, recv_sems) = refs[:nw], refs[nw:2 * nw], refs[2 * nw:]
        _, _, _, sibling, _ = _place()
        cps = [pltpu.make_async_remote_copy(src_ref=src_ref, dst_ref=out_ref, send_sem=send_sems.at[w], recv_sem=recv_sems.at[w],
                                            device_id=sibling, device_id_type=MESH)
               for w, (src_ref, out_ref) in enumerate(zip(srcs, outs))]
        for cp in cps:
            cp.start()
        for cp in cps:
            cp.wait()

    return pl.pallas_call(
        body, name=name, out_shape=[jax.ShapeDtypeStruct(h.shape, h.dtype) for h in halves],
        in_specs=[ANY] * nw, out_specs=[ANY] * nw,
        scratch_shapes=[pltpu.SemaphoreType.DMA((nw,)), pltpu.SemaphoreType.DMA((nw,))])(*halves)


def _row_tile(rows, limit=256):
    return next(t for t in range(limit, 15, -16) if rows % t == 0)


def _pair_add(grad, got, core, *, name):
    _, r, n = grad.shape
    half = r // 2
    tr = _row_tile(half)
    nb = half // tr

    def body(core_ref, g_ref, t_ref, f_ref, b_ref):
        acc = g_ref[...] + t_ref[...]
        f_ref[...] = acc
        b_ref[...] = acc.astype(BF16)

    blk = pl.BlockSpec((1, tr, n), lambda j, i, core_ref: (j, i, 0))
    mine = pl.BlockSpec((1, tr, n), lambda j, i, core_ref: (j, core_ref[0] * nb + i, 0))
    return pl.pallas_call(
        body, name=name,
        grid_spec=pltpu.PrefetchScalarGridSpec(num_scalar_prefetch=1, grid=(N_CHIP, nb), in_specs=[mine, blk], out_specs=[blk, blk]),
        out_shape=[jax.ShapeDtypeStruct((N_CHIP, half, n), F32), jax.ShapeDtypeStruct((N_CHIP, half, n), BF16)],
        compiler_params=_params("parallel", "parallel"))(core, grad, got)


def _chip_add(pair, theirs, chip, *, name):
    _, h, n = pair.shape
    tr = _row_tile(h)

    def body(chip_ref, p_ref, t_ref, o_ref):
        o_ref[...] = ((p_ref[0] + t_ref[0].astype(F32)) + t_ref[1].astype(F32)) + t_ref[2].astype(F32)

    return pl.pallas_call(
        body, name=name,
        grid_spec=pltpu.PrefetchScalarGridSpec(
            num_scalar_prefetch=1, grid=(h // tr,),
            in_specs=[pl.BlockSpec((1, tr, n), lambda i, chip_ref: (chip_ref[0], i, 0)),
                      pl.BlockSpec((3, tr, n), lambda i, chip_ref: (0, i, 0))],
            out_specs=pl.BlockSpec((tr, n), lambda i, chip_ref: (i, 0))),
        out_shape=jax.ShapeDtypeStruct((h, n), F32), compiler_params=_params("parallel"))(chip, pair, theirs)


def _sum_devices(gathered, *, name):
    _, m, n = gathered.shape

    def body(g_ref, tot_ref, loss_ref):
        tot = g_ref[0]
        for dev in range(1, N_DEV):
            tot = tot + g_ref[dev]
        tot_ref[...] = tot
        loss_ref[...] = jnp.full((8, n), (0.5 / D_MODEL) * jnp.sum(tot[0:8]), F32)

    return pl.pallas_call(body, name=name, in_specs=[VMEM_SPEC], out_specs=[VMEM_SPEC, VMEM_SPEC],
                          out_shape=[jax.ShapeDtypeStruct((m, n), F32), jax.ShapeDtypeStruct((8, n), F32)])(gathered)


def _ada_mod(cond_all, w_ada_shard, *, name):
    tn = 512

    def body(a_ref, b_ref, o_ref):
        o_ref[...] = _nn(a_ref[...], b_ref[...], precision=HIGHEST)

    return pl.pallas_call(
        body, name=name, grid=(w_ada_shard.shape[1] // tn,),
        in_specs=[pl.BlockSpec(cond_all.shape, lambda j: (0, 0)), pl.BlockSpec((D_MODEL, tn), lambda j: (0, j))],
        out_specs=pl.BlockSpec((N_DEV, tn), lambda j: (0, j)),
        out_shape=jax.ShapeDtypeStruct((N_DEV, w_ada_shard.shape[1]), F32), compiler_params=_params("parallel"))(cond_all, w_ada_shard)


def _ada_grad(cond_all, dmod_cols, *, name):
    tm = 256

    def body(a_ref, b_ref, o_ref):
        o_ref[...] = lax.dot_general(a_ref[...], b_ref[...], (((0,), (0,)), ((), ())), precision=HIGHEST,
                                     preferred_element_type=F32)

    return pl.pallas_call(
        body, name=name, grid=(D_MODEL // tm,),
        in_specs=[pl.BlockSpec((N_DEV, tm), lambda i: (0, i)), pl.BlockSpec(dmod_cols.shape, lambda i: (0, 0))],
        out_specs=pl.BlockSpec((tm, dmod_cols.shape[1]), lambda i: (i, 0)),
        out_shape=jax.ShapeDtypeStruct((D_MODEL, dmod_cols.shape[1]), F32), compiler_params=_params("parallel"))(cond_all, dmod_cols)


def _silu_rows(c8, *, name):
    def body(c_ref, o_ref):
        cv = c_ref[...]
        o_ref[...] = cv * _sigmoid(cv)

    return pl.pallas_call(body, name=name, in_specs=[VMEM_SPEC], out_specs=VMEM_SPEC,
                          out_shape=jax.ShapeDtypeStruct(c8.shape, F32))(c8)


def _rows128(t, rows=None):
    flat = t.reshape(-1, 128)
    return flat if rows is None else jnp.pad(flat, ((0, rows - flat.shape[0]), (0, 0)))


def _from_col_shards(shards, r, n):
    return shards.reshape(N_CHIP, r, n).transpose(1, 0, 2).reshape(r, N_CHIP * n)


def kernel(x, c, w_ada, b_ada, norm1_g, w_in, gla_w_gate, gla_b_gate, gla_norm_g, q_norm_g, k_norm_g, w_out, norm2_g, w_up, conv_w, conv_b, w_down, loss_target, m_w_ada, m_b_ada, m_norm1_g, m_w_in, m_gla_w_gate, m_gla_b_gate, m_gla_norm_g, m_q_norm_g, m_k_norm_g, m_w_out, m_norm2_g, m_w_up, m_conv_w, m_conv_b, m_w_down, v_w_ada, v_b_ada, v_norm1_g, v_w_in, v_gla_w_gate, v_gla_b_gate, v_gla_norm_g, v_q_norm_g, v_k_norm_g, v_w_out, v_norm2_g, v_w_up, v_conv_w, v_conv_b, v_w_down):
    d = D_MODEL
    ax, ay, ac = lax.axis_index("x"), lax.axis_index("y"), lax.axis_index("c")
    chip, dev = 2 * ax + ay, 4 * ax + 2 * ay + ac

    cond = _silu_rows(jnp.broadcast_to(c, (8, d)), name="cond_silu")[0:1]
    small_in = jnp.concatenate([_rows128(cond), _rows128(conv_w[0]), _rows128(gla_w_gate[0])], axis=0)
    small_in = _rows128(small_in, 56)
    got = _all_gather_small(small_in, name="gather_small").reshape(N_DEV, 56, 128)
    cond_all = got[:, 0:8].reshape(N_DEV, d)
    conv_w_full = _from_col_shards(got[0::2, 8:41].reshape(N_CHIP, 3 * 1408 // 128, 128), 3, 1408)
    gate_full = _from_col_shards(got[0::2, 41:49].reshape(N_CHIP, 16 * 64 // 128, 128), GLA_GATE_RANK, 64)
    mod_part = _ada_mod(cond_all, w_ada[0], name="ada_mod")
    mod_got = _all_gather_small(_rows128(mod_part), name="gather_mod").reshape(N_DEV, N_DEV, 1536)
    mod_all = mod_got[0::2].transpose(1, 0, 2).reshape(N_DEV, 6 * d) + b_ada
    mod = lax.dynamic_slice_in_dim(mod_all, dev, 1, axis=0)

    own = [w[0].astype(BF16).reshape(2, w.shape[1] // 2, w.shape[2]) for w in (w_in, w_out, w_up, w_down)]
    with_own = lambda got, mine: [lax.dynamic_update_index_in_dim(t, o, chip, 0) for t, o in zip(got, mine)]
    got_in, got_out = with_own(_gather_weight_shards(own[:2], name="gather_weights"), own[:2])
    w_in_full = got_in.reshape(N_CHIP, d, 772).transpose(1, 0, 2).reshape(d, N_CHIP * 772)
    w_out_full = got_out.reshape(d, d)
    exchanged = mod_all[0:1, 0:1] + got_in[0, 0, 0:1, 0:1].astype(F32)
    send_sems, recv_sems, own_thru, lands, token = _gather_late_start(own[2:], exchanged, name="gather_late_start")
    mod = mod + token[0:1, 0:1]

    def ffn_weights(after):
        mine, landed = _gather_late_wait(send_sems, recv_sems, own_thru, lands, after, name="gather_late_wait")
        got_up, got_down = with_own(landed, mine)
        return got_up.reshape(N_CHIP, d, 1408).transpose(1, 0, 2).reshape(d, 2 * D_FF), got_down.reshape(D_FF, d)

    late_reduce = []

    def ffn_grads_ready(g_wup_b, g_wdown_b):
        halves_of = lambda g: g.reshape(N_CHIP, 2, g.shape[-2] // 2, g.shape[-1])
        late_reduce.extend(_direct_reduce_start([halves_of(g_wup_b), halves_of(g_wdown_b.reshape(N_CHIP, D_FF // N_CHIP, d))],
                                                name="reduce_late_start"))
        return late_reduce[4]

    err2, grad_x, (g_wi, g_wo, g_wup, g_wdown), small = _local_step(
        x[0], loss_target[0], mod, _in_proj_layout(w_in_full), w_out_full, ffn_weights, ffn_grads_ready, conv_w_full, conv_b,
        _gate_layout(gate_full), gla_b_gate, gla_norm_g, q_norm_g, k_norm_g, norm1_g, norm2_g)

    pieces = [err2[0], small["dmod"], small["norm1_g"], small["norm2_g"], small["gla_w_gate"].reshape(-1), small["gla_b_gate"],
              small["gla_norm_g"], small["q_norm_g"], small["k_norm_g"], small["conv_w"].reshape(-1), small["conv_b"]]
    sizes = [p.shape[0] for p in pieces]
    at = [sum(sizes[:i]) for i in range(len(sizes) + 1)]
    vec = _rows128(jnp.concatenate(pieces), 288)
    got = _all_gather_small(vec, name="gather_grads").reshape(N_DEV, 288, 128)
    total, loss8 = _sum_devices(got, name="sum_devices")
    total = total.reshape(-1)
    seg = lambda i: total[at[i]:at[i + 1]]
    dmod_all = got.reshape(N_DEV, -1)[:, at[1]:at[2]]
    g_small = dict(
        b_ada=seg(1)[None], norm1_g=seg(2)[None], norm2_g=seg(3)[None],
        gla_w_gate=lax.dynamic_slice_in_dim(seg(4).reshape(GLA_GATE_RANK, 256), chip * 64, 64, axis=1),
        gla_b_gate=seg(5)[None], gla_norm_g=seg(6)[None], q_norm_g=seg(7)[None], k_norm_g=seg(8)[None],
        conv_w=lax.dynamic_slice_in_dim(seg(9).reshape(3, 2 * D_FF), chip * 1408, 1408, axis=1), conv_b=seg(10)[None])
    dmod_cols = lax.dynamic_slice_in_dim(dmod_all.reshape(N_DEV, 6 * d), chip * 1536, 1536, axis=1)
    g_w_ada = _ada_grad(cond_all, dmod_cols, name="ada_grad")

    tags = ("w_in", "w_out")
    g_parts = [_in_proj_grad_layout(g_wi).reshape(d, N_CHIP, 772).transpose(1, 0, 2), g_wo.reshape(N_CHIP, d // N_CHIP, d)]
    core_id, chip_id = jnp.reshape(ac, (1,)).astype(jnp.int32), jnp.reshape(chip, (1,)).astype(jnp.int32)
    got = _pair_exchange_halves([g.reshape(N_CHIP, 2, g.shape[1] // 2, g.shape[2]) for g in g_parts], name="reduce_pair")
    pairs = [_pair_add(g, t, core_id, name=f"reduce_pair_add_{tag}") for g, t, tag in zip(g_parts, got, tags)]
    theirs = _chip_scatter([pb for _, pb in pairs], name="reduce_chips")
    summed = [_chip_add(pf, t, chip_id, name=f"reduce_chips_add_{tag}") for (pf, _), t, tag in zip(pairs, theirs, tags)]
    landed = _direct_reduce_wait(*late_reduce[:4], grad_x, name="reduce_late_wait")
    summed += [_direct_reduce_add(g, t, chip_id, core_id, name=f"reduce_late_add_{tag}")
               for g, t, tag in zip((g_wup, g_wdown.reshape(N_CHIP, D_FF // N_CHIP, d)), landed, ("w_up", "w_down"))]
    others = _share_halves(summed, name="share_pair")
    g_big = [jnp.concatenate([jnp.where(ac == 0, mine, other), jnp.where(ac == 0, other, mine)], axis=0)
             for mine, other in zip(summed, others)]

    grads = dict(w_ada=g_w_ada, w_in=g_big[0], w_out=g_big[1], w_up=g_big[2], w_down=g_big[3], **g_small)
    names = ["w_ada", "b_ada", "norm1_g", "w_in", "gla_w_gate", "gla_b_gate", "gla_norm_g", "q_norm_g", "k_norm_g", "w_out",
             "norm2_g", "w_up", "conv_w", "conv_b", "w_down"]
    ws = dict(w_ada=w_ada, b_ada=b_ada, norm1_g=norm1_g, w_in=w_in, gla_w_gate=gla_w_gate, gla_b_gate=gla_b_gate,
              gla_norm_g=gla_norm_g, q_norm_g=q_norm_g, k_norm_g=k_norm_g, w_out=w_out, norm2_g=norm2_g, w_up=w_up,
              conv_w=conv_w, conv_b=conv_b, w_down=w_down)
    ms = dict(w_ada=m_w_ada, b_ada=m_b_ada, norm1_g=m_norm1_g, w_in=m_w_in, gla_w_gate=m_gla_w_gate, gla_b_gate=m_gla_b_gate,
              gla_norm_g=m_gla_norm_g, q_norm_g=m_q_norm_g, k_norm_g=m_k_norm_g, w_out=m_w_out, norm2_g=m_norm2_g, w_up=m_w_up,
              conv_w=m_conv_w, conv_b=m_conv_b, w_down=m_w_down)
    vs = dict(w_ada=v_w_ada, b_ada=v_b_ada, norm1_g=v_norm1_g, w_in=v_w_in, gla_w_gate=v_gla_w_gate, gla_b_gate=v_gla_b_gate,
              gla_norm_g=v_gla_norm_g, q_norm_g=v_q_norm_g, k_norm_g=v_k_norm_g, w_out=v_w_out, norm2_g=v_norm2_g, w_up=v_w_up,
              conv_w=v_conv_w, conv_b=v_conv_b, w_down=v_w_down)
    g_out, d_out, m_out, v_out = [], [], [], []
    for nm in names:
        w2 = ws[nm].reshape(ws[nm].shape[-2:])
        g2 = grads[nm].reshape(w2.shape)
        dl, mn, vn = _adamw(w2, g2, ms[nm].reshape(w2.shape), vs[nm].reshape(w2.shape), name=f"adamw_{nm}")
        shape = ws[nm].shape
        g_out.append(g2.reshape(shape))
        d_out.append(dl.reshape(shape))
        m_out.append(mn.reshape(shape))
        v_out.append(vn.reshape(shape))
    return (loss8[0, 0], grad_x[None], *g_out, *d_out, *m_out, *v_out)
```

```python
import functools

import jax
import jax.numpy as jnp
from jax import lax
from jax.experimental import pallas as pl
from jax.experimental.pallas import tpu as pltpu

F32, BF16 = jnp.float32, jnp.bfloat16
HIGHEST = lax.Precision.HIGHEST
MESH = pl.DeviceIdType.MESH

D_MODEL = 1024
GLA_CHUNK = 64
GLA_GATE_TAU = 16.0
GLA_GATE_RANK = 16
HEAD_LANES = 128
ATTN_BLOCK = 128
DILATIONS = (1, 4, 16)
ALIBI_SLOPES = tuple(2.0 ** (-(h + 1)) for h in range(8))
D_FF = 2816
EPS = 1e-6
C_GQ, C_GK, C_GV, C_GR, C_AQ, C_AK, C_AV, C_LR, PROJ_W = 0, 256, 512, 1024, 1536, 2048, 2560, 3072, 3200
ADAM_LR, ADAM_B1, ADAM_B2, ADAM_EPS, ADAM_WD, ADAM_STEP = 0.001, 0.9, 0.999, 1e-08, 0.01, 10
VMEM_LIMIT_BYTES = 56 * 1024 * 1024
ROW_TILE = 256


def _params(*sem):
    return pltpu.CompilerParams(dimension_semantics=sem or None, vmem_limit_bytes=VMEM_LIMIT_BYTES)


def _nt(a, b):
    return lax.dot_general(a, b, (((1,), (1,)), ((), ())), preferred_element_type=F32)


def _tn(a, b):
    return lax.dot_general(a, b, (((0,), (0,)), ((), ())), preferred_element_type=F32)


def _nn(a, b, precision=None):
    return jnp.dot(a, b, preferred_element_type=F32, precision=precision)


def _split3(v):
    hi = v.astype(BF16)
    rest = v - hi.astype(F32)
    mid = rest.astype(BF16)
    return hi, mid, (rest - mid.astype(F32)).astype(BF16)


def _sum_right(v, ones):
    hi, mid, lo = _split3(v)
    return (_nn(lo, ones) + _nn(mid, ones)) + _nn(hi, ones)


def _sum_left(ones, v):
    hi, mid, lo = _split3(v)
    return (_nn(ones, lo) + _nn(ones, mid)) + _nn(ones, hi)


def _fold8(v):
    return v.reshape(v.shape[0] // 8, 8, v.shape[1]).sum(axis=0)


def _spread_total(ref):
    t = ref[...]
    ref[...] = jnp.broadcast_to(jnp.sum(t, axis=-2, keepdims=True), t.shape)


def _sigmoid(x):
    return 1.0 / (1.0 + jnp.exp(-x))


def _mm(a, b, *, ta=False, tb=False, out_dtype=F32, tm, tn, tk, shard_cols=False, also_bf16=False, name):
    (k_a, m) = a.shape if ta else a.shape[::-1]
    (k_b, n) = b.shape[::-1] if tb else b.shape
    assert k_a == k_b and m % tm == 0 and n % tn == 0 and k_a % tk == 0, (name, a.shape, b.shape)
    nk = k_a // tk
    assert nk == 1 or out_dtype == F32, name
    dims = (((0 if ta else 1,), (1 if tb else 0,)), ((), ()))

    def body(a_ref, b_ref, o_ref, *rounded):
        k = pl.program_id(2)
        part = lax.dot_general(a_ref[...].astype(BF16), b_ref[...].astype(BF16), dims, preferred_element_type=F32)
        if nk == 1:
            o_ref[...] = part.astype(out_dtype)
        else:
            @pl.when(k == 0)
            def _():
                o_ref[...] = part

            @pl.when(k > 0)
            def _():
                o_ref[...] += part

        if also_bf16:
            @pl.when(k == nk - 1)
            def _():
                rounded[0][...] = o_ref[...].astype(BF16)

    a_spec = pl.BlockSpec((tk, tm), lambda i, j, k: (k, i)) if ta else pl.BlockSpec((tm, tk), lambda i, j, k: (i, k))
    b_spec = pl.BlockSpec((tn, tk), lambda i, j, k: (j, k)) if tb else pl.BlockSpec((tk, tn), lambda i, j, k: (k, j))
    if shard_cols:
        o_spec, o_shape = pl.BlockSpec((None, tm, tn), lambda i, j, k: (j, i, 0)), (n // tn, m, tn)
    else:
        o_spec, o_shape = pl.BlockSpec((tm, tn), lambda i, j, k: (i, j)), (m, n)
    shapes = [jax.ShapeDtypeStruct(o_shape, out_dtype)] + ([jax.ShapeDtypeStruct(o_shape, BF16)] if also_bf16 else [])
    out = pl.pallas_call(
        body, name=name, grid=(m // tm, n // tn, nk), in_specs=[a_spec, b_spec], out_specs=[o_spec] * len(shapes),
        out_shape=shapes, compiler_params=_params("parallel", "parallel", "arbitrary"))(a, b)
    return out if also_bf16 else out[0]


def _norm_mod_fwd(x, branch, gate, gain, scale, shift, *, name):
    s, d = x.shape
    tm = ROW_TILE
    has_branch = branch is not None

    def body(*refs):
        if has_branch:
            x_ref, br_ref, gate_ref, gain_ref, sc_ref, sh_ref, x1_ref, h_ref = refs
            xv = x_ref[...] + gate_ref[...] * br_ref[...]
            x1_ref[...] = xv
        else:
            x_ref, gain_ref, sc_ref, sh_ref, h_ref = refs
            xv = x_ref[...]
        r = lax.rsqrt(jnp.mean(xv * xv, axis=-1, keepdims=True) + EPS)
        h_ref[...] = ((xv * r) * gain_ref[...] * (1.0 + sc_ref[...]) + sh_ref[...]).astype(BF16)

    row = pl.BlockSpec((tm, d), lambda i: (i, 0))
    vec = pl.BlockSpec((1, d), lambda i: (0, 0))
    if has_branch:
        return pl.pallas_call(
            body, name=name, grid=(s // tm,), in_specs=[row, row, vec, vec, vec, vec], out_specs=[row, row],
            out_shape=[jax.ShapeDtypeStruct((s, d), F32), jax.ShapeDtypeStruct((s, d), BF16)],
            compiler_params=_params("parallel"))(x, branch, gate, gain, scale, shift)
    h = pl.pallas_call(
        body, name=name, grid=(s // tm,), in_specs=[row, vec, vec, vec], out_specs=row,
        out_shape=jax.ShapeDtypeStruct((s, d), BF16), compiler_params=_params("parallel"))(x, gain, scale, shift)
    return x, h


def _norm_mod_bwd(x, dh, dres, gain, scale, branch, gate, *, name):
    s, d = x.shape
    tm = ROW_TILE
    has_branch = branch is not None

    def body(*refs):
        if has_branch:
            x_ref, dh_ref, dres_ref, gain_ref, sc_ref, br_ref, gate_ref, dx_ref, dbr_ref, sums_ref = refs
        else:
            x_ref, dh_ref, dres_ref, gain_ref, sc_ref, dx_ref, sums_ref = refs
        i = pl.program_id(0)

        @pl.when(i == 0)
        def _():
            sums_ref[...] = jnp.zeros_like(sums_ref)

        xv, dhv = x_ref[...], dh_ref[...]
        r = lax.rsqrt(jnp.mean(xv * xv, axis=-1, keepdims=True) + EPS)
        xn = xv * r
        dxn = dhv * (gain_ref[...] * (1.0 + sc_ref[...]))
        dx = dres_ref[...] + r * (dxn - xn * jnp.mean(dxn * xn, axis=-1, keepdims=True))
        dx_ref[...] = dx
        sums_ref[0] += _fold8(dhv * xn)
        sums_ref[1] += _fold8(dhv)
        if has_branch:
            dbr_ref[...] = (gate_ref[...] * dx).astype(BF16)
            sums_ref[2] += _fold8(dx * br_ref[...])

        @pl.when(i == s // tm - 1)
        def _():
            _spread_total(sums_ref)

    row = pl.BlockSpec((tm, d), lambda i: (i, 0))
    vec = pl.BlockSpec((1, d), lambda i: (0, 0))
    sums = pl.BlockSpec((3, 8, d), lambda i: (0, 0, 0))
    sums_shape = jax.ShapeDtypeStruct((3, 8, d), F32)
    if has_branch:
        return pl.pallas_call(
            body, name=name, grid=(s // tm,), in_specs=[row, row, row, vec, vec, row, vec], out_specs=[row, row, sums],
            out_shape=[jax.ShapeDtypeStruct((s, d), F32), jax.ShapeDtypeStruct((s, d), BF16), sums_shape],
            compiler_params=_params("arbitrary"))(x, dh, dres, gain, scale, branch, gate)
    dx, sm = pl.pallas_call(
        body, name=name, grid=(s // tm,), in_specs=[row, row, row, vec, vec], out_specs=[row, sums],
        out_shape=[jax.ShapeDtypeStruct((s, d), F32), sums_shape],
        compiler_params=_params("arbitrary"))(x, dh, dres, gain, scale)
    return dx, None, sm


GLA_ROWS = 256


def _gla_block_setup(lr_ref, wg_ref, bg_ref):
    t, c = GLA_ROWS, GLA_CHUNK
    ri = lax.broadcasted_iota(jnp.int32, (t, t), 0)
    ci = lax.broadcasted_iota(jnp.int32, (t, t), 1)
    same = (ri // c) == (ci // c)
    causal, upper = same & (ci <= ri), same & (ci >= ri)
    z = _nn(lr_ref[...].astype(BF16), wg_ref[...]) + bg_ref[...]
    g = (jnp.minimum(z, 0.0) - jnp.log(1.0 + jnp.exp(-jnp.abs(z)))) * (1.0 / GLA_GATE_TAU)
    hi, mid, lo = _split3(g)
    total = lambda ones: (_nn(ones, lo) + _nn(ones, mid)) + _nn(ones, hi)
    return z, total(causal.astype(BF16)), total(same.astype(BF16)), causal, upper


def _chunks(t):
    return [t[i * GLA_CHUNK:(i + 1) * GLA_CHUNK] for i in range(GLA_ROWS // GLA_CHUNK)]


def _gla_fwd(proj, wg, bg, gn, *, name):
    s = proj.shape[0]
    tb, c = GLA_ROWS, GLA_CHUNK
    cb = tb // c

    def body(q_ref, k_ref, v_ref, r_ref, lr_ref, wg_ref, bg_ref, gn_ref, o_ref, y_ref, st_ref, state):
        i = pl.program_id(0)

        @pl.when(i == 0)
        def _():
            state[...] = jnp.zeros_like(state)

        low = lax.broadcasted_iota(jnp.int32, (tb, HEAD_LANES), 1) < 64
        masks = (low, jnp.logical_not(low))
        _, b, b_end, causal, _ = _gla_block_setup(lr_ref, wg_ref, bg_ref)
        for p in range(2):
            cols = pl.ds(p * HEAD_LANES, HEAD_LANES)
            bp, bep = (t[:, p * HEAD_LANES:(p + 1) * HEAD_LANES] for t in (b, b_end))
            k = k_ref[:, cols]
            q_in = q_ref[:, cols] * 0.125 * jnp.exp(bp)
            k_out = (k * jnp.exp(-bp)).astype(BF16)
            k_end = k * jnp.exp(bep - bp)
            qms = [jnp.where(m, q_in, 0.0).astype(BF16) for m in masks]
            kes = [jnp.where(m, k_end, 0.0).astype(BF16) for m in masks]
            vs = [v_ref[:, pl.ds((2 * p + e) * HEAD_LANES, HEAD_LANES)].astype(BF16) for e in range(2)]
            grow = [_tn(v0, k0) + _tn(v1, k1) for v0, k0, v1, k1 in zip(_chunks(vs[0]), _chunks(kes[0]), _chunks(vs[1]), _chunks(kes[1]))]
            st, entering = state[p], []
            for ch in range(cb):
                entering.append(st)
                st_ref[ch, p] = st
                st = st * jnp.exp(bep[ch * c:ch * c + 1, :]) + grow[ch]
            state[p] = st
            for e in range(2):
                hc = pl.ds((2 * p + e) * HEAD_LANES, HEAD_LANES)
                a = jnp.where(causal, _nt(qms[e], k_out), 0.0).astype(BF16)
                carried = jnp.concatenate([_nt(qc, sc.astype(BF16)) for qc, sc in zip(_chunks(qms[e]), entering)], axis=0)
                o = _nn(a, vs[e]) + carried
                o_ref[:, hc] = o
                rr = r_ref[:, hc]
                on = o * lax.rsqrt(jnp.mean(o * o, axis=-1, keepdims=True) + EPS)
                y_ref[:, hc] = (on * gn_ref[...] * (rr * _sigmoid(rr))).astype(BF16)

    def col(width, at):
        return pl.BlockSpec((tb, width), lambda i: (i, at // width))

    full = lambda shape: pl.BlockSpec(shape, lambda i: tuple(0 for _ in shape))
    return pl.pallas_call(
        body, name=name, grid=(s // tb,),
        in_specs=[col(256, C_GQ), col(256, C_GK), col(512, C_GV), col(512, C_GR), col(128, C_LR),
                  full((HEAD_LANES, 256)), full((1, 256)), full((1, HEAD_LANES))],
        out_specs=[pl.BlockSpec((tb, 512), lambda i: (i, 0)), pl.BlockSpec((tb, 512), lambda i: (i, 0)),
                   pl.BlockSpec((cb, 2, HEAD_LANES, HEAD_LANES), lambda i: (i, 0, 0, 0))],
        out_shape=[jax.ShapeDtypeStruct((s, 512), F32), jax.ShapeDtypeStruct((s, 512), BF16),
                   jax.ShapeDtypeStruct((s // c, 2, HEAD_LANES, HEAD_LANES), F32)],
        scratch_shapes=[pltpu.VMEM((2, HEAD_LANES, HEAD_LANES), F32)],
        compiler_params=_params("arbitrary"))(proj, proj, proj, proj, proj, wg, bg, gn)


def _gla_bwd(proj, wg, bg, gn, o_raw, states, dmixed, *, name):
    s = proj.shape[0]
    tb, c = GLA_ROWS, GLA_CHUNK
    cb = tb // c
    nblk, nch = s // tb, s // c

    def body(q_ref, k_ref, v_ref, r_ref, lr_ref, wg_ref, bg_ref, gn_ref, o_ref, st_ref, stn_ref, dy_ref,
             dq_ref, dk_ref, dv_ref, dr_ref, dlr_ref, gwg_ref, sums_ref, dstate):
        i = pl.program_id(0)

        @pl.when(i == 0)
        def _():
            dstate[...] = jnp.zeros_like(dstate)
            gwg_ref[...] = jnp.zeros_like(gwg_ref)
            sums_ref[...] = jnp.zeros_like(sums_ref)

        low = lax.broadcasted_iota(jnp.int32, (tb, HEAD_LANES), 1) < 64
        masks = (low, jnp.logical_not(low))
        z, b, b_end, causal, upper = _gla_block_setup(lr_ref, wg_ref, bg_ref)
        lr_b = lr_ref[...].astype(BF16)
        dlr = jnp.zeros((tb, HEAD_LANES), F32)
        for p in range(2):
            cols = pl.ds(p * HEAD_LANES, HEAD_LANES)
            sl = slice(p * HEAD_LANES, (p + 1) * HEAD_LANES)
            bp, bep = b[:, sl], b_end[:, sl]
            e_in, e_out, e_end = jnp.exp(bp), jnp.exp(-bp), jnp.exp(bep - bp)
            q = q_ref[:, cols] * 0.125
            k = k_ref[:, cols]
            q_in, k_out, k_end = q * e_in, k * e_out, k * e_end
            qms = [jnp.where(m, q_in, 0.0).astype(BF16) for m in masks]
            kms_out = [jnp.where(m, k_out, 0.0).astype(BF16) for m in masks]
            kms_end = [jnp.where(m, k_end, 0.0).astype(BF16) for m in masks]
            vs, dos = [], []
            for e in range(2):
                hc = pl.ds((2 * p + e) * HEAD_LANES, HEAD_LANES)
                o, rr, dy = o_ref[:, hc], r_ref[:, hc], dy_ref[:, hc]
                sg = _sigmoid(rr)
                rs = lax.rsqrt(jnp.mean(o * o, axis=-1, keepdims=True) + EPS)
                on = o * rs
                t = dy * (rr * sg)
                sums_ref[1, :, hc] += _fold8(t * on)
                dn = t * gn_ref[...]
                dos.append((rs * (dn - on * jnp.mean(dn * on, axis=-1, keepdims=True))).astype(BF16))
                dr_ref[:, hc] = (dy * on * gn_ref[...] * (sg * (1.0 + rr * (1.0 - sg)))).astype(BF16)
                vs.append(v_ref[:, hc].astype(BF16))
            grow = [_tn(d0, q0) + _tn(d1, q1) for d0, q0, d1, q1 in zip(_chunks(dos[0]), _chunks(qms[0]), _chunks(dos[1]), _chunks(qms[1]))]
            entering = [st_ref[ch, p] for ch in range(cb)]
            dst, leaving_grad = dstate[p], [None] * cb
            for ch in reversed(range(cb)):
                leaving_grad[ch] = dst
                dst = dst * jnp.exp(bep[ch * c:ch * c + 1, :]) + grow[ch]
            dstate[p] = dst
            leaving = entering[1:] + [stn_ref[0, p]]
            felt = jnp.concatenate([jnp.broadcast_to(jnp.sum(dg_st * st, axis=0, keepdims=True), (c, HEAD_LANES))
                                    for dg_st, st in zip(leaving_grad, leaving)], axis=0)
            per_chunk = lambda rows, mats, fn: jnp.concatenate([fn(r, m.astype(BF16)) for r, m in zip(_chunks(rows), mats)], axis=0)
            dq_in = jnp.zeros((tb, HEAD_LANES), F32)
            dk_out = jnp.zeros((tb, HEAD_LANES), F32)
            dk_end = jnp.zeros((tb, HEAD_LANES), F32)
            for e in range(2):
                hc = pl.ds((2 * p + e) * HEAD_LANES, HEAD_LANES)
                a = jnp.where(causal, _nt(qms[e], kms_out[e]), 0.0).astype(BF16)
                da = jnp.where(causal, _nt(dos[e], vs[e]), 0.0).astype(BF16)
                dv_ref[:, hc] = (_tn(a, dos[e]) + per_chunk(kms_end[e], leaving_grad, _nt)).astype(BF16)
                dq_in = dq_in + jnp.where(masks[e], per_chunk(dos[e], entering, _nn) + _nn(da, kms_out[e]), 0.0)
                dk_out = dk_out + _tn(da, qms[e])
                dk_end = dk_end + jnp.where(masks[e], per_chunk(vs[e], leaving_grad, _nn), 0.0)
            dq = dq_in * e_in
            dk = dk_out * e_out + dk_end * e_end
            dq_ref[:, cols] = (dq * 0.125).astype(BF16)
            dk_ref[:, cols] = dk.astype(BF16)
            dg = _sum_left(upper.astype(BF16), q * dq - k * dk) + felt
            dz = dg * (1.0 / GLA_GATE_TAU) * _sigmoid(-z[:, sl])
            dz_b = dz.astype(BF16)
            sums_ref[0, :, cols] += _fold8(dz)
            dlr = dlr + _nt(dz_b, wg_ref[:, cols])
            gwg_ref[:, cols] += _tn(lr_b, dz_b)
        dlr_ref[...] = dlr.astype(BF16)

        @pl.when(i == nblk - 1)
        def _():
            _spread_total(sums_ref)

    rev = lambda i: nblk - 1 - i

    def col(width, at):
        return pl.BlockSpec((tb, width), lambda i: (rev(i), at // width))

    full = lambda shape: pl.BlockSpec(shape, lambda i: tuple(0 for _ in shape))
    out_col = lambda width: pl.BlockSpec((tb, width), lambda i: (rev(i), 0))
    return pl.pallas_call(
        body, name=name, grid=(nblk,),
        in_specs=[col(256, C_GQ), col(256, C_GK), col(512, C_GV), col(512, C_GR), col(128, C_LR),
                  full((HEAD_LANES, 256)), full((1, 256)), full((1, HEAD_LANES)),
                  pl.BlockSpec((tb, 512), lambda i: (rev(i), 0)),
                  pl.BlockSpec((cb, 2, HEAD_LANES, HEAD_LANES), lambda i: (rev(i), 0, 0, 0)),
                  pl.BlockSpec((1, 2, HEAD_LANES, HEAD_LANES), lambda i: (jnp.minimum((rev(i) + 1) * cb, nch - 1), 0, 0, 0)),
                  pl.BlockSpec((tb, 512), lambda i: (rev(i), 0))],
        out_specs=[out_col(256), out_col(256), out_col(512), out_col(512), out_col(128),
                   full((HEAD_LANES, 256)), full((2, 8, 512))],
        out_shape=[jax.ShapeDtypeStruct((s, 256), BF16), jax.ShapeDtypeStruct((s, 256), BF16),
                   jax.ShapeDtypeStruct((s, 512), BF16), jax.ShapeDtypeStruct((s, 512), BF16),
                   jax.ShapeDtypeStruct((s, 128), BF16), jax.ShapeDtypeStruct((HEAD_LANES, 256), F32),
                   jax.ShapeDtypeStruct((2, 8, 512), F32)],
        scratch_shapes=[pltpu.VMEM((2, HEAD_LANES, HEAD_LANES), F32)],
        compiler_params=_params("arbitrary"))(proj, proj, proj, proj, proj, wg, bg, gn, o_raw, states, states, dmixed)


def _head_sum_matrix():
    ri = lax.broadcasted_iota(jnp.int32, (512, 512), 0) // 64
    ci = lax.broadcasted_iota(jnp.int32, (512, 512), 1) // 64
    return (ri == ci).astype(BF16)


def _attn_prep(proj, qg, kg, *, name):
    s = proj.shape[0]
    tm = ROW_TILE

    def body(q_ref, k_ref, qg_ref, kg_ref, qa_ref, ka_ref):
        hs = _head_sum_matrix()
        q, k = q_ref[...], k_ref[...]
        qr = lax.rsqrt(_sum_right(q * q, hs) * (1.0 / 64) + EPS)
        kr = lax.rsqrt(_sum_right(k * k, hs) * (1.0 / 64) + EPS)
        qa_ref[...] = q * qr * qg_ref[...] * 0.125
        ka_ref[...] = k * kr * kg_ref[...]

    col = lambda at: pl.BlockSpec((tm, 512), lambda i: (i, at // 512))
    vec = pl.BlockSpec((1, 512), lambda i: (0, 0))
    out = pl.BlockSpec((tm, 512), lambda i: (i, 0))
    return pl.pallas_call(
        body, name=name, grid=(s // tm,), in_specs=[col(C_AQ), col(C_AK), vec, vec], out_specs=[out] * 2,
        out_shape=[jax.ShapeDtypeStruct((s, 512), F32)] * 2, compiler_params=_params("parallel"))(proj, proj, qg, kg)


FAR = 1e30


def _attn_distance(first):
    blk = ATTN_BLOCK
    iq = lax.broadcasted_iota(jnp.int32, (2 * blk, 2 * blk), 0) & (blk - 1)
    ik = lax.broadcasted_iota(jnp.int32, (2 * blk, 2 * blk), 1)
    rel = iq + blk - ik
    valid = (rel >= 0) & (rel <= blk) & (jnp.logical_not(first) | (ik >= blk))
    return jnp.where(valid, rel.astype(F32), FAR)


def _stack_heads(t2):
    low = lax.broadcasted_iota(jnp.int32, t2.shape, 1) < 64
    return jnp.concatenate([jnp.where(low, t2, 0.0), jnp.where(low, 0.0, t2)], axis=0).astype(BF16)


def _unstack_heads(t):
    blk = ATTN_BLOCK
    low = lax.broadcasted_iota(jnp.int32, (blk, HEAD_LANES), 1) < 64
    return jnp.where(low, t[0:blk], t[blk:2 * blk])


def _attn_scores(qs, kcat, slopes, dil, dist):
    top = lax.broadcasted_iota(jnp.int32, (2 * ATTN_BLOCK, 1), 0) < ATTN_BLOCK
    return _nt(qs, kcat) - jnp.where(top, slopes[0] * dil, slopes[1] * dil) * dist


def _pair_slopes(p):
    if isinstance(p, int):
        return ALIBI_SLOPES[2 * p], ALIBI_SLOPES[2 * p + 1]
    pick = lambda e: jnp.where(p == 0, ALIBI_SLOPES[e], jnp.where(p == 1, ALIBI_SLOPES[2 + e],
                               jnp.where(p == 2, ALIBI_SLOPES[4 + e], ALIBI_SLOPES[6 + e])))
    return pick(0), pick(1)


def _attn_pair_fwd(q2, kcat, vcat, slopes, dil, dist):
    sc = _attn_scores(_stack_heads(q2), kcat, slopes, dil, dist)
    m = jnp.max(sc, axis=-1, keepdims=True)
    pr = jnp.exp(sc - m)
    den = jnp.sum(pr, axis=-1, keepdims=True)
    o = _nn(pr.astype(BF16), vcat) / den
    lse = jnp.broadcast_to(m + jnp.log(den), o.shape)
    return _unstack_heads(o), _unstack_heads(lse)


def _attn_pair_bwd(q2, kcat, vcat, do2, y2, lse2, slopes, dil, dist):
    lane = lax.broadcasted_iota(jnp.int32, (ATTN_BLOCK, HEAD_LANES), 1)
    low = lane < 64
    prod = do2 * y2
    per_head = lambda t, pick: jnp.concatenate([jnp.sum(jnp.where(pick(0), t, 0.0), axis=-1, keepdims=True),
                                                jnp.sum(jnp.where(pick(1), t, 0.0), axis=-1, keepdims=True)], axis=0)
    lse = per_head(lse2, lambda e: lane == 64 * e)
    delta = per_head(prod, lambda e: low if e == 0 else jnp.logical_not(low))
    qs, dos = _stack_heads(q2), _stack_heads(do2)
    pr = jnp.exp(_attn_scores(qs, kcat, slopes, dil, dist) - lse)
    ds = (pr * (_nt(dos, vcat) - delta)).astype(BF16)
    return _unstack_heads(_nn(ds, kcat)), _tn(ds, qs), _tn(pr.astype(BF16), dos)


def _attn_specs(dil):
    rows = ATTN_BLOCK * dil
    if dil == 1:
        cur = lambda at: pl.BlockSpec((rows, 512), lambda n: (n, at // 512))
        prev = lambda at: pl.BlockSpec((rows, 512), lambda n: (jnp.maximum(n - 1, 0), at // 512))
    else:
        cur = lambda at: pl.BlockSpec((rows, HEAD_LANES), lambda n, p: (n, at // HEAD_LANES + p))
        prev = lambda at: pl.BlockSpec((rows, HEAD_LANES), lambda n, p: (jnp.maximum(n - 1, 0), at // HEAD_LANES + p))
    return cur, prev


def _attn_loop(dil, one_pair):
    if dil == 1:
        for p in range(4):
            one_pair(slice(None), pl.ds(p * HEAD_LANES, HEAD_LANES), p)
    else:
        p = pl.program_id(1)

        def step(r, carry):
            one_pair(pl.ds(r, ATTN_BLOCK, stride=dil), slice(None), p)
            return carry

        lax.fori_loop(0, dil, step, 0, unroll=min(dil, 4))


def _dil_attn_fwd(qa, ka, proj, dil, *, name):
    s = qa.shape[0]

    def body(q_ref, kp_ref, kc_ref, vp_ref, vc_ref, o_ref, lse_ref):
        dist = _attn_distance(pl.program_id(0) == 0)

        def one_pair(rows, cols, p):
            kcat = jnp.concatenate([kp_ref[rows, cols], kc_ref[rows, cols]], axis=0).astype(BF16)
            vcat = jnp.concatenate([vp_ref[rows, cols], vc_ref[rows, cols]], axis=0).astype(BF16)
            o2, lse2 = _attn_pair_fwd(q_ref[rows, cols], kcat, vcat, _pair_slopes(p), dil, dist)
            o_ref[rows, cols] = o2
            lse_ref[rows, cols] = lse2

        _attn_loop(dil, one_pair)

    cur, prev = _attn_specs(dil)
    grid = (s // ATTN_BLOCK,) if dil == 1 else (s // (ATTN_BLOCK * dil), 4)
    return pl.pallas_call(
        body, name=name, grid=grid, in_specs=[cur(0), prev(0), cur(0), prev(C_AV), cur(C_AV)], out_specs=[cur(0), cur(0)],
        out_shape=[jax.ShapeDtypeStruct((s, 512), F32)] * 2,
        compiler_params=_params(*["parallel"] * len(grid)))(qa, ka, ka, proj, proj)


def _attn_merge(branches, y_gla, *, name):
    s = y_gla.shape[0]
    tm = ROW_TILE

    def body(o0, l0, o1, l1, o2, l2, yg_ref, mixed_ref, y_ref, lse_ref):
        m = jnp.maximum(jnp.maximum(l0[...], l1[...]), l2[...])
        w0, w1, w2 = jnp.exp(l0[...] - m), jnp.exp(l1[...] - m), jnp.exp(l2[...] - m)
        zs = w0 + w1 + w2
        y = (w0 * o0[...] + w1 * o1[...] + w2 * o2[...]) / zs
        y_ref[...] = y
        lse_ref[...] = m + jnp.log(zs)
        mixed_ref[:, 0:512] = yg_ref[...]
        mixed_ref[:, 512:1024] = y.astype(BF16)

    blk = pl.BlockSpec((tm, 512), lambda i: (i, 0))
    args = [t for pair in branches for t in pair]
    return pl.pallas_call(
        body, name=name, grid=(s // tm,), in_specs=[blk] * 7,
        out_specs=[pl.BlockSpec((tm, 1024), lambda i: (i, 0)), blk, blk],
        out_shape=[jax.ShapeDtypeStruct((s, 1024), BF16), jax.ShapeDtypeStruct((s, 512), F32),
                   jax.ShapeDtypeStruct((s, 512), F32)],
        compiler_params=_params("parallel"))(*args, y_gla)


def _dil_attn_bwd(qa, ka, proj, y_att, lse, dmixed, dil, *, name):
    s = qa.shape[0]
    blk = ATTN_BLOCK

    def body(q_ref, kp_ref, kc_ref, vp_ref, vc_ref, y_ref, lse_ref, do_ref, dq_ref, dkc_ref, dkp_ref, dvc_ref, dvp_ref):
        dist = _attn_distance(pl.program_id(0) == 0)

        def one_pair(rows, cols, p):
            kcat = jnp.concatenate([kp_ref[rows, cols], kc_ref[rows, cols]], axis=0).astype(BF16)
            vcat = jnp.concatenate([vp_ref[rows, cols], vc_ref[rows, cols]], axis=0).astype(BF16)
            dq, dk, dv = _attn_pair_bwd(q_ref[rows, cols], kcat, vcat, do_ref[rows, cols], y_ref[rows, cols],
                                        lse_ref[rows, cols], _pair_slopes(p), dil, dist)
            dq_ref[rows, cols] = dq
            dkp_ref[rows, cols] = dk[0:blk]
            dkc_ref[rows, cols] = dk[blk:2 * blk]
            dvp_ref[rows, cols] = dv[0:blk]
            dvc_ref[rows, cols] = dv[blk:2 * blk]

        _attn_loop(dil, one_pair)

    cur, prev = _attn_specs(dil)
    grid = (s // blk,) if dil == 1 else (s // (blk * dil), 4)
    return pl.pallas_call(
        body, name=name, grid=grid,
        in_specs=[cur(0), prev(0), cur(0), prev(C_AV), cur(C_AV), cur(0), cur(0), cur(512)], out_specs=[cur(0)] * 5,
        out_shape=[jax.ShapeDtypeStruct((s, 512), F32)] * 5, compiler_params=_params(*["parallel"] * len(grid)),
    )(qa, ka, ka, proj, proj, y_att, lse, dmixed)


def _attn_post(parts, proj, qg, kg, *, name):
    s = proj.shape[0]
    tm = ATTN_BLOCK
    nblk = s // tm

    def body(*refs):
        ins, (q_ref, k_ref, qg_ref, kg_ref, dq_out, dk_out, dv_out, sums_ref) = refs[:15], refs[15:]
        i = pl.program_id(0)

        @pl.when(i == 0)
        def _():
            sums_ref[...] = jnp.zeros_like(sums_ref)

        dq = jnp.zeros((tm, 512), F32)
        dk = jnp.zeros((tm, 512), F32)
        dv = jnp.zeros((tm, 512), F32)
        for g, dil in enumerate(DILATIONS):
            dq_r, dkc_r, dkp_r, dvc_r, dvp_r = ins[5 * g:5 * g + 5]
            inside = (i + dil < nblk).astype(F32)
            dq = dq + dq_r[...]
            dk = dk + dkc_r[...] + inside * dkp_r[...]
            dv = dv + dvc_r[...] + inside * dvp_r[...]
        dv_out[...] = dv.astype(BF16)
        hs = _head_sum_matrix()
        for row, (x_ref, g_ref, dy, out, post) in enumerate(((q_ref, qg_ref, dq, dq_out, 0.125), (k_ref, kg_ref, dk, dk_out, 1.0))):
            x = x_ref[...]
            rs = lax.rsqrt(_sum_right(x * x, hs) * (1.0 / 64) + EPS)
            xn = x * rs
            dy = dy * post
            sums_ref[row] += _fold8(dy * xn)
            dn = dy * g_ref[...]
            out[...] = (rs * (dn - xn * (_sum_right(dn * xn, hs) * (1.0 / 64)))).astype(BF16)

        @pl.when(i == nblk - 1)
        def _():
            _spread_total(sums_ref)

    here = pl.BlockSpec((tm, 512), lambda i: (i, 0))
    specs = []
    for dil in DILATIONS:
        later = pl.BlockSpec((tm, 512), lambda i, dil=dil: (jnp.minimum(i + dil, nblk - 1), 0))
        specs += [here, here, later, here, later]
    col = lambda at: pl.BlockSpec((tm, 512), lambda i: (i, at // 512))
    vec = pl.BlockSpec((1, 512), lambda i: (0, 0))
    return pl.pallas_call(
        body, name=name, grid=(nblk,), in_specs=specs + [col(C_AQ), col(C_AK), vec, vec],
        out_specs=[here, here, here, pl.BlockSpec((2, 8, 512), lambda i: (0, 0, 0))],
        out_shape=[jax.ShapeDtypeStruct((s, 512), BF16)] * 3 + [jax.ShapeDtypeStruct((2, 8, 512), F32)],
        compiler_params=_params("arbitrary"))(*[t for part in parts for t in part], proj, proj, qg, kg)


FFN_TM, FFN_TN = 256, 1408
HALO = 16


def _conv3(u_ref, halo_ref, w_ref, b_ref, first):
    u = u_ref[...].astype(F32)
    ext = jnp.concatenate([jnp.where(first, 0.0, halo_ref[...].astype(F32)), u], axis=0)
    u1 = pltpu.roll(ext, 1, 0)[HALO:]
    u2 = pltpu.roll(ext, 2, 0)[HALO:]
    return b_ref[...] + w_ref[0:1, :] * u2 + w_ref[1:2, :] * u1 + w_ref[2:3, :] * u, u, u1, u2


def _ffn_specs(tm, tn):
    nj = D_FF // tn
    blk = lambda half: pl.BlockSpec((tm, tn), lambda j, i: (i, j + half * nj))
    halo = lambda half: pl.BlockSpec((HALO, tn), lambda j, i: (jnp.maximum(i * (tm // HALO) - 1, 0), j + half * nj))
    wspec = lambda half: pl.BlockSpec((3, tn), lambda j, i: (0, j + half * nj))
    bspec = lambda half: pl.BlockSpec((1, tn), lambda j, i: (0, j + half * nj))
    return [blk(0), halo(0), blk(1), halo(1), wspec(0), wspec(1), bspec(0), bspec(1)]


def _conv_swiglu_fwd(u, conv_w, conv_b, *, name):
    s = u.shape[0]
    tm, tn = FFN_TM, FFN_TN

    def body(ug_ref, hg_ref, uv_ref, hv_ref, wg_ref, wv_ref, bg_ref, bv_ref, act_ref):
        first = pl.program_id(1) == 0
        cg = _conv3(ug_ref, hg_ref, wg_ref, bg_ref, first)[0]
        cv = _conv3(uv_ref, hv_ref, wv_ref, bv_ref, first)[0]
        act_ref[...] = (cg * _sigmoid(cg) * cv).astype(BF16)

    return pl.pallas_call(
        body, name=name, grid=(D_FF // tn, s // tm), in_specs=_ffn_specs(tm, tn),
        out_specs=pl.BlockSpec((tm, tn), lambda j, i: (i, j)), out_shape=jax.ShapeDtypeStruct((s, D_FF), BF16),
        compiler_params=_params("parallel", "parallel"))(u, u, u, u, conv_w, conv_w, conv_b, conv_b)


def _conv_swiglu_bwd_pre(u, conv_w, conv_b, dact, *, name):
    s = u.shape[0]
    tm, tn = FFN_TM, FFN_TN

    def body(ug_ref, hg_ref, uv_ref, hv_ref, wg_ref, wv_ref, bg_ref, bv_ref, da_ref, duc_ref, sums_ref):
        i = pl.program_id(1)

        @pl.when(i == 0)
        def _():
            sums_ref[...] = jnp.zeros_like(sums_ref)

        cg, g0, g1, g2 = _conv3(ug_ref, hg_ref, wg_ref, bg_ref, i == 0)
        cv, v0, v1, v2 = _conv3(uv_ref, hv_ref, wv_ref, bv_ref, i == 0)
        da = da_ref[...].astype(F32)
        sg = _sigmoid(cg)
        dg = da * cv * (sg * (1.0 + cg * (1.0 - sg)))
        dv = da * (cg * sg)
        duc_ref[0] = dg.astype(BF16)
        duc_ref[1] = dv.astype(BF16)
        for half, (d, taps) in enumerate(((dg, (g2, g1, g0)), (dv, (v2, v1, v0)))):
            for t, tap in enumerate(taps):
                sums_ref[half, t] += _fold8(d * tap)
            sums_ref[half, 3] += _fold8(d)

        @pl.when(i == s // tm - 1)
        def _():
            _spread_total(sums_ref)

    return pl.pallas_call(
        body, name=name, grid=(D_FF // tn, s // tm),
        in_specs=_ffn_specs(tm, tn) + [pl.BlockSpec((tm, tn), lambda j, i: (i, j))],
        out_specs=[pl.BlockSpec((2, tm, tn), lambda j, i: (0, i, j)), pl.BlockSpec((2, 4, 8, tn), lambda j, i: (0, 0, 0, j))],
        out_shape=[jax.ShapeDtypeStruct((2, s, D_FF), BF16), jax.ShapeDtypeStruct((2, 4, 8, D_FF), F32)],
        compiler_params=_params("parallel", "arbitrary"))(u, u, u, u, conv_w, conv_w, conv_b, conv_b, dact)


def _conv_bwd(duc, conv_w, *, name):
    _, s, _ = duc.shape
    tm, tn = FFN_TM, FFN_TN
    nj, ni = D_FF // tn, s // tm

    def body(d_ref, halo_ref, w_ref, du_ref):
        last = pl.program_id(2) == ni - 1
        d = d_ref[0].astype(F32)
        ext = jnp.concatenate([d, jnp.where(last, 0.0, halo_ref[0].astype(F32))], axis=0)
        n = tm + HALO
        d1 = pltpu.roll(ext, n - 1, 0)[:tm]
        d2 = pltpu.roll(ext, n - 2, 0)[:tm]
        du_ref[...] = (w_ref[2:3, :] * d + w_ref[1:2, :] * d1 + w_ref[0:1, :] * d2).astype(BF16)

    return pl.pallas_call(
        body, name=name, grid=(2, nj, ni),
        in_specs=[pl.BlockSpec((1, tm, tn), lambda g, j, i: (g, i, j)),
                  pl.BlockSpec((1, HALO, tn), lambda g, j, i: (g, jnp.minimum((i + 1) * (tm // HALO), s // HALO - 1), j)),
                  pl.BlockSpec((3, tn), lambda g, j, i: (0, g * nj + j))],
        out_specs=pl.BlockSpec((tm, tn), lambda g, j, i: (i, g * nj + j)),
        out_shape=jax.ShapeDtypeStruct((s, 2 * D_FF), BF16),
        compiler_params=_params("parallel", "parallel", "parallel"))(duc, duc, conv_w)


def _loss_head(x1, ffn, gate, target, *, name):
    s, d = x1.shape
    tm = ROW_TILE

    def body(x_ref, f_ref, g_ref, t_ref, dy_ref, df_ref, sums_ref):
        i = pl.program_id(0)

        @pl.when(i == 0)
        def _():
            sums_ref[...] = jnp.zeros_like(sums_ref)

        f = f_ref[...]
        err = x_ref[...] + g_ref[...] * f - t_ref[...]
        dy = err * (1.0 / d)
        dy_ref[...] = dy
        df_ref[...] = (g_ref[...] * dy).astype(BF16)
        sums_ref[0] += _fold8(dy * f)
        sums_ref[1] += _fold8(err * err)

        @pl.when(i == s // tm - 1)
        def _():
            _spread_total(sums_ref)

    row = pl.BlockSpec((tm, d), lambda i: (i, 0))
    return pl.pallas_call(
        body, name=name, grid=(s // tm,), in_specs=[row, row, pl.BlockSpec((1, d), lambda i: (0, 0)), row],
        out_specs=[row, row, pl.BlockSpec((2, 8, d), lambda i: (0, 0, 0))],
        out_shape=[jax.ShapeDtypeStruct((s, d), F32), jax.ShapeDtypeStruct((s, d), BF16), jax.ShapeDtypeStruct((2, 8, d), F32)],
        compiler_params=_params("arbitrary"))(x1, ffn, gate, target)


def _adamw(w, g, m, v, *, name):
    rows, cols = w.shape
    tm = next((t for t in range(ROW_TILE, 7, -8) if rows % t == 0), rows)

    def body(w_ref, g_ref, m_ref, v_ref, d_ref, mo_ref, vo_ref):
        gv = g_ref[...]
        mn = ADAM_B1 * m_ref[...] + (1.0 - ADAM_B1) * gv
        vn = ADAM_B2 * v_ref[...] + (1.0 - ADAM_B2) * (gv * gv)
        m_hat = mn / (1.0 - ADAM_B1 ** ADAM_STEP)
        v_hat = vn / (1.0 - ADAM_B2 ** ADAM_STEP)
        d_ref[...] = -ADAM_LR * (m_hat / (jnp.sqrt(v_hat) + ADAM_EPS) + ADAM_WD * w_ref[...])
        mo_ref[...] = mn
        vo_ref[...] = vn

    blk = pl.BlockSpec((tm, cols), lambda i: (i, 0))
    return pl.pallas_call(
        body, name=name, grid=(rows // tm,), in_specs=[blk] * 4, out_specs=[blk] * 3,
        out_shape=[jax.ShapeDtypeStruct((rows, cols), F32)] * 3, compiler_params=_params("parallel"))(w, g, m, v)


def _colsum(t):
    return t[..., 0, :]


def _in_proj_layout(w_in):
    pad = jnp.zeros((w_in.shape[0], PROJ_W - C_LR - GLA_GATE_RANK), w_in.dtype)
    return jnp.concatenate([w_in[:, :1536], w_in[:, 1552:], w_in[:, 1536:1552], pad], axis=1)


def _in_proj_grad_layout(g):
    return jnp.concatenate([g[:, :1536], g[:, C_LR:C_LR + GLA_GATE_RANK], g[:, 1536:C_LR]], axis=1)


def _gate_layout(gla_w_gate):
    return jnp.pad(gla_w_gate, ((0, HEAD_LANES - GLA_GATE_RANK), (0, 0))).astype(BF16)


def _local_step(x, target, mod, wi, wo, ffn_weights, ffn_grads_ready, conv_w, conv_b, wg, bg, gn, qg, kg, n1g, n2g):
    d = D_MODEL
    sh1, sc1, g1, sh2, sc2, g2 = [mod[:, i * d:(i + 1) * d] for i in range(6)]
    qg8, kg8 = jnp.tile(qg, (1, 8)), jnp.tile(kg, (1, 8))

    _, h1 = _norm_mod_fwd(x, None, None, n1g, sc1, sh1, name="norm1_fwd")
    proj = _mm(h1, wi, tm=1024, tn=PROJ_W, tk=d, name="in_proj")
    o_raw, y_gla, states = _gla_fwd(proj, wg, bg, gn, name="gla_fwd")
    qa, ka = _attn_prep(proj, qg8, kg8, name="attn_prep")
    branches = [_dil_attn_fwd(qa, ka, proj, dil, name=f"attn_fwd_d{dil}") for dil in DILATIONS]
    mixed, y_att, lse = _attn_merge(branches, y_gla, name="attn_merge")
    attn_out = _mm(mixed, wo, tm=1024, tn=d, tk=d, name="out_proj")
    x1, h2 = _norm_mod_fwd(x, attn_out, g1, n2g, sc2, sh2, name="norm2_fwd")
    wup, wdown = ffn_weights(h2)
    u = _mm(h2, wup, out_dtype=BF16, tm=1024, tn=D_FF, tk=d, name="up_proj")
    act = _conv_swiglu_fwd(u, conv_w, conv_b, name="conv_swiglu_fwd")
    ffn = _mm(act, wdown, tm=1024, tn=d, tk=D_FF, name="down_proj")
    dy, dffn, head_sums = _loss_head(x1, ffn, g2, target, name="loss_head")

    dact = _mm(dffn, wdown, tb=True, out_dtype=BF16, tm=1024, tn=D_FF, tk=d, name="down_proj_dx")
    g_wdown, g_wdown_b = _mm(act, dffn, ta=True, tm=1408, tn=d, tk=1024, also_bf16=True, name="down_proj_dw")
    duc, conv_sums = _conv_swiglu_bwd_pre(u, conv_w, conv_b, dact, name="conv_swiglu_bwd")
    du = _conv_bwd(duc, conv_w, name="conv_bwd")
    dh2 = _mm(du, wup, tb=True, tm=1024, tn=d, tk=1408, name="up_proj_dx")
    g_wup, g_wup_b = _mm(h2, du, ta=True, tm=d, tn=1408, tk=1024, shard_cols=True, also_bf16=True, name="up_proj_dw")
    token = ffn_grads_ready(g_wup_b, g_wdown_b)
    g1_late = g1 if token is None else g1 + token[0:1, 0:1]
    dx1, dao, n2_sums = _norm_mod_bwd(x1, dh2, dy, n2g, sc2, attn_out, g1_late, name="norm2_bwd")

    dmixed = _mm(dao, wo, tb=True, tm=1024, tn=d, tk=d, name="out_proj_dx")
    g_wo = _mm(mixed, dao, ta=True, tm=d, tn=d, tk=1024, name="out_proj_dw")
    dgq, dgk, dgv, dgr, dlr, g_wg, gla_sums = _gla_bwd(proj, wg, bg, gn, o_raw, states, dmixed, name="gla_bwd")
    parts = [_dil_attn_bwd(qa, ka, proj, y_att, lse, dmixed, dil, name=f"attn_bwd_d{dil}") for dil in DILATIONS]
    daq, dak, dav, qk_sums = _attn_post(parts, proj, qg8, kg8, name="attn_post")
    dproj = jnp.concatenate([dgq, dgk, dgv, dgr, daq, dak, dav, dlr], axis=1)
    dh1 = _mm(dproj, wi, tb=True, tm=1024, tn=d, tk=PROJ_W, name="in_proj_dx")
    g_wi = _mm(h1, dproj, ta=True, tm=512, tn=PROJ_W, tk=512, name="in_proj_dw")
    grad_x, _, n1_sums = _norm_mod_bwd(x, dh1, dx1, n1g, sc1, None, None, name="norm1_bwd")

    n1, n2, hs, cs = _colsum(n1_sums), _colsum(n2_sums), _colsum(head_sums), _colsum(conv_sums)
    gs, qs = _colsum(gla_sums), _colsum(qk_sums)
    dmod = jnp.concatenate([n1[1], n1[0] * n1g[0], n2[2], n2[1], n2[0] * n2g[0], hs[0]])
    small = dict(
        dmod=dmod,
        norm1_g=n1[0] * (1.0 + sc1[0]), norm2_g=n2[0] * (1.0 + sc2[0]),
        gla_w_gate=g_wg[:GLA_GATE_RANK], gla_b_gate=gs[0, :256], gla_norm_g=gs[1].reshape(4, 128).sum(axis=0),
        q_norm_g=qs[0].reshape(8, 64).sum(axis=0), k_norm_g=qs[1].reshape(8, 64).sum(axis=0),
        conv_w=jnp.concatenate([cs[0, :3], cs[1, :3]], axis=1), conv_b=jnp.concatenate([cs[0, 3], cs[1, 3]]),
    )
    return head_sums[1], grad_x, (g_wi, g_wo, g_wup, g_wdown), small


N_DEV, N_CHIP = 8, 4
ANY = pl.BlockSpec(memory_space=pl.ANY)
VMEM_SPEC = pl.BlockSpec(memory_space=pltpu.VMEM)


def _place():
    x, y, c = lax.axis_index("x"), lax.axis_index("y"), lax.axis_index("c")
    other_chips = [(1 - x, y), (x, 1 - y), (1 - x, 1 - y)]
    return x, y, c, (x, y, 1 - c), other_chips


def _all_gather_small(v, *, name):
    m, n = v.shape

    def body(v_ref, out_ref, send_sems, recv_sems, local_sem):
        x, y, c, sibling, chips = _place()
        me = (x, y, c)

        def rows(px, py, pc):
            return out_ref.at[pl.ds((4 * px + 2 * py + pc) * m, m), :]

        def copy(k, block, to, src=None):
            return pltpu.make_async_remote_copy(
                src_ref=rows(*block) if src is None else src, dst_ref=rows(*block), send_sem=send_sems.at[k],
                recv_sem=recv_sems.at[k], device_id=to, device_id_type=MESH)

        mine = pltpu.make_async_copy(v_ref, rows(*me), local_sem)
        mine.start()
        first = [copy(0, me, sibling, src=v_ref)]
        first += [copy(1 + j, me, (*chip, c), src=v_ref) for j, chip in enumerate(chips)]
        for cp in first:
            cp.start()
        passed = [copy(4 + j, (*chip, c), sibling) for j, chip in enumerate(chips)]
        for j, chip in enumerate(chips):
            copy(1 + j, (*chip, c), me).wait_recv()
            passed[j].start()
        copy(0, sibling, me).wait_recv()
        for j, chip in enumerate(chips):
            copy(4 + j, (*chip, 1 - c), me).wait_recv()
        for cp in first + passed:
            cp.wait_send()
        mine.wait()

    return pl.pallas_call(
        body, name=name, out_shape=jax.ShapeDtypeStruct((N_DEV * m, n), v.dtype), in_specs=[VMEM_SPEC], out_specs=VMEM_SPEC,
        scratch_shapes=[pltpu.SemaphoreType.DMA((7,)), pltpu.SemaphoreType.DMA((7,)), pltpu.SemaphoreType.DMA],
    )(v)


def _gather_weight_shards(shards, *, name):
    nw = len(shards)

    def body(*refs):
        srcs, outs, (send_sems, recv_sems) = refs[:nw], refs[nw:2 * nw], refs[2 * nw:]
        x, y, c, sibling, chips = _place()
        index = lambda chip: 2 * chip[0] + chip[1]

        def copy(w, k, src, dst, to):
            return pltpu.make_async_remote_copy(src_ref=src, dst_ref=dst, send_sem=send_sems.at[6 * w + k],
                                                recv_sem=recv_sems.at[6 * w + k], device_id=to, device_id_type=MESH)

        sent = []
        for w, (src_ref, out_ref) in enumerate(zip(srcs, outs)):
            for k, chip in enumerate(chips):
                sent.append(copy(w, k, src_ref.at[c], out_ref.at[2 * x + y, c], (*chip, c)))
                sent[-1].start()
        for w, out_ref in enumerate(outs):
            for k, chip in enumerate(chips):
                landed = out_ref.at[index(chip), c]
                copy(w, k, landed, landed, (*chip, c)).wait_recv()
                sent.append(copy(w, 3 + k, landed, landed, sibling))
                sent[-1].start()
        for w, out_ref in enumerate(outs):
            for k, chip in enumerate(chips):
                passed_on = out_ref.at[index(chip), 1 - c]
                copy(w, 3 + k, passed_on, passed_on, sibling).wait_recv()
        for cp in sent:
            cp.wait_send()

    return pl.pallas_call(
        body, name=name, out_shape=[jax.ShapeDtypeStruct((N_CHIP, *s.shape), s.dtype) for s in shards],
        in_specs=[ANY] * nw, out_specs=[ANY] * nw,
        scratch_shapes=[pltpu.SemaphoreType.DMA((6 * nw,)), pltpu.SemaphoreType.DMA((6 * nw,))],
    )(*shards)


HBM_SPEC = pl.BlockSpec(memory_space=pltpu.HBM)
SEM_SPEC = pl.BlockSpec(memory_space=pltpu.SEMAPHORE)
DATAFLOW_EFFECT = pltpu.SideEffectType.DATAFLOW_SIDE_EFFECTING


def _late_copies(srcs, lands, send_sems, recv_sems):
    x, y, c, _, chips = _place()
    return [pltpu.make_async_remote_copy(
        src_ref=src.at[c], dst_ref=land.at[2 * x + y, c], send_sem=send_sems.at[6 * w + 2 * r + core],
        recv_sem=recv_sems.at[6 * w + 2 * r + c], device_id=(*chip, core), device_id_type=MESH)
        for w, (src, land) in enumerate(zip(srcs, lands)) for r, chip in enumerate(chips) for core in range(2)]


def _gather_late_start(own, after, *, name):
    nw = len(own)

    def body(*refs):
        srcs, lands, send_sems, recv_sems, token = refs[:nw], refs[nw:2 * nw], refs[2 * nw + 1], refs[2 * nw + 2], refs[-1]
        for cp in _late_copies(srcs, lands, send_sems, recv_sems):
            cp.start()
        token[...] = jnp.zeros_like(token)

    lands = [pltpu.with_memory_space_constraint(lax.empty((N_CHIP, *s.shape), s.dtype), pltpu.HBM) for s in own]
    own = [pltpu.with_memory_space_constraint(s, pltpu.HBM) for s in own]
    out = pl.pallas_call(
        body, name=name,
        out_shape=(pltpu.SemaphoreType.DMA((6 * nw,)), pltpu.SemaphoreType.DMA((6 * nw,)),
                   *[pltpu.HBM(s.shape, s.dtype) for s in own], *[pltpu.HBM(s.shape, s.dtype) for s in lands],
                   jax.ShapeDtypeStruct((8, 128), F32)),
        in_specs=[HBM_SPEC] * (2 * nw) + [ANY], out_specs=(SEM_SPEC, SEM_SPEC, *[HBM_SPEC] * (2 * nw), VMEM_SPEC),
        input_output_aliases={i: 2 + i for i in range(2 * nw)},
        compiler_params=pltpu.CompilerParams(has_side_effects=DATAFLOW_EFFECT))(*own, *lands, after)
    return out[0], out[1], out[2:2 + nw], out[2 + nw:2 + 2 * nw], out[-1]


def _gather_late_wait(send_sems, recv_sems, own, lands, after, *, name):
    nw = len(own)

    def body(*refs):
        srcs, lands_in, send_sems, recv_sems = refs[:nw], refs[nw:2 * nw], refs[2 * nw], refs[2 * nw + 1]
        x, y, c, _, chips = _place()
        for cp in _late_copies(srcs, lands_in, send_sems, recv_sems):
            cp.wait_send()
        for w, (src, land) in enumerate(zip(srcs, lands_in)):
            for r, chip in enumerate(chips):
                for core in range(2):
                    pltpu.make_async_remote_copy(
                        src_ref=src.at[c], dst_ref=land.at[2 * chip[0] + chip[1], core], send_sem=send_sems.at[6 * w + 2 * r + core],
                        recv_sem=recv_sems.at[6 * w + 2 * r + core], device_id=(*chip, core), device_id_type=MESH).wait_recv()

    out = pl.pallas_call(
        body, name=name, out_shape=(*[pltpu.HBM(s.shape, s.dtype) for s in own], *[pltpu.HBM(s.shape, s.dtype) for s in lands]),
        in_specs=[HBM_SPEC] * (2 * nw) + [SEM_SPEC, SEM_SPEC, ANY], out_specs=tuple([HBM_SPEC] * (2 * nw)),
        input_output_aliases={i: i for i in range(2 * nw)},
        compiler_params=pltpu.CompilerParams(has_side_effects=DATAFLOW_EFFECT))(*own, *lands, send_sems, recv_sems, after)
    return out[:nw], out[nw:]


def _direct_reduce_copies(srcs, lands, send_sems, recv_sems):
    x, y, c, _, _ = _place()
    cps = []
    for w, (src, land) in enumerate(zip(srcs, lands)):
        for rel in range(1, N_DEV):
            tx, ty, tc = (1 - x if rel & 4 else x), (1 - y if rel & 2 else y), (1 - c if rel & 1 else c)
            cps.append(pltpu.make_async_remote_copy(
                src_ref=src.at[2 * tx + ty, tc], dst_ref=land.at[rel - 1], send_sem=send_sems.at[7 * w + rel - 1],
                recv_sem=recv_sems.at[7 * w + rel - 1], device_id=(tx, ty, tc), device_id_type=MESH))
    return cps


def _direct_reduce_start(grads, *, name):
    nw = len(grads)

    def body(*refs):
        srcs, lands, send_sems, recv_sems, token = refs[:nw], refs[nw:2 * nw], refs[2 * nw], refs[2 * nw + 1], refs[-1]
        for cp in _direct_reduce_copies(srcs, lands, send_sems, recv_sems):
            cp.start()
        token[...] = jnp.zeros_like(token)

    lands = [pltpu.with_memory_space_constraint(lax.empty((N_DEV - 1, *g.shape[2:]), g.dtype), pltpu.HBM) for g in grads]
    grads = [pltpu.with_memory_space_constraint(g, pltpu.HBM) for g in grads]
    out = pl.pallas_call(
        body, name=name,
        out_shape=(pltpu.SemaphoreType.DMA((7 * nw,)), pltpu.SemaphoreType.DMA((7 * nw,)),
                   *[pltpu.HBM(g.shape, g.dtype) for g in grads], *[pltpu.HBM(t.shape, t.dtype) for t in lands],
                   jax.ShapeDtypeStruct((8, 128), F32)),
        in_specs=[HBM_SPEC] * (2 * nw), out_specs=(SEM_SPEC, SEM_SPEC, *[HBM_SPEC] * (2 * nw), VMEM_SPEC),
        input_output_aliases={i: 2 + i for i in range(2 * nw)},
        compiler_params=pltpu.CompilerParams(has_side_effects=DATAFLOW_EFFECT))(*grads, *lands)
    return out[0], out[1], out[2:2 + nw], out[2 + nw:2 + 2 * nw], out[-1]


def _direct_reduce_wait(send_sems, recv_sems, grads, lands, after, *, name):
    nw = len(grads)

    def body(*refs):
        srcs, lands_in, send_sems, recv_sems = refs[:nw], refs[nw:2 * nw], refs[2 * nw], refs[2 * nw + 1]
        cps = _direct_reduce_copies(srcs, lands_in, send_sems, recv_sems)
        for cp in cps:
            cp.wait_send()
        for cp in cps:
            cp.wait_recv()

    out = pl.pallas_call(
        body, name=name, out_shape=(*[pltpu.HBM(g.shape, g.dtype) for g in grads], *[pltpu.HBM(t.shape, t.dtype) for t in lands]),
        in_specs=[HBM_SPEC] * (2 * nw) + [SEM_SPEC, SEM_SPEC, ANY], out_specs=tuple([HBM_SPEC] * (2 * nw)),
        input_output_aliases={i: i for i in range(2 * nw)},
        compiler_params=pltpu.CompilerParams(has_side_effects=DATAFLOW_EFFECT))(*grads, *lands, send_sems, recv_sems, after)
    return out[nw:]


def _direct_reduce_add(grad, landed, chip, core, *, name):
    _, r, n = grad.shape
    half = r // 2
    tr = _row_tile(half)
    nb = half // tr

    def body(chip_ref, core_ref, g_ref, t_ref, o_ref):
        acc = g_ref[0]
        for k in range(N_DEV - 1):
            acc = acc + t_ref[k].astype(F32)
        o_ref[...] = acc

    return pl.pallas_call(
        body, name=name,
        grid_spec=pltpu.PrefetchScalarGridSpec(
            num_scalar_prefetch=2, grid=(nb,),
            in_specs=[pl.BlockSpec((1, tr, n), lambda i, chip_ref, core_ref: (chip_ref[0], core_ref[0] * nb + i, 0)),
                      pl.BlockSpec((N_DEV - 1, tr, n), lambda i, chip_ref, core_ref: (0, i, 0))],
            out_specs=pl.BlockSpec((tr, n), lambda i, chip_ref, core_ref: (i, 0))),
        out_shape=jax.ShapeDtypeStruct((half, n), F32), compiler_params=_params("parallel"))(chip, core, grad, landed)


def _pair_exchange_halves(grads, *, name):
    nw = len(grads)

    def body(*refs):
        srcs, outs, (send_sems, recv_sems) = refs[:nw], refs[nw:2 * nw], refs[2 * nw:]
        _, _, c, sibling, _ = _place()
        cps = []
        for w, (src_ref, out_ref) in enumerate(zip(srcs, outs)):
            cps.append(pltpu.make_async_remote_copy(
                src_ref=src_ref.at[:, 1 - c], dst_ref=out_ref, send_sem=send_sems.at[w],
                recv_sem=recv_sems.at[w], device_id=sibling, device_id_type=MESH))
            cps[-1].start()
        for cp in cps:
            cp.wait()

    return pl.pallas_call(
        body, name=name, out_shape=[jax.ShapeDtypeStruct((N_CHIP, *g.shape[2:]), g.dtype) for g in grads],
        in_specs=[ANY] * nw, out_specs=[ANY] * nw,
        scratch_shapes=[pltpu.SemaphoreType.DMA((nw,)), pltpu.SemaphoreType.DMA((nw,))])(*grads)


def _chip_scatter(pairs, *, name):
    nw = len(pairs)

    def body(*refs):
        srcs, outs, (send_sems, recv_sems) = refs[:nw], refs[nw:2 * nw], refs[2 * nw:]
        _, _, c, _, chips = _place()
        cps = []
        for w, (p_ref, out_ref) in enumerate(zip(srcs, outs)):
            for k, chip in enumerate(chips):
                cps.append(pltpu.make_async_remote_copy(
                    src_ref=p_ref.at[2 * chip[0] + chip[1]], dst_ref=out_ref.at[k], send_sem=send_sems.at[3 * w + k],
                    recv_sem=recv_sems.at[3 * w + k], device_id=(*chip, c), device_id_type=MESH))
                cps[-1].start()
        for cp in cps:
            cp.wait()

    return pl.pallas_call(
        body, name=name, out_shape=[jax.ShapeDtypeStruct((3, *p.shape[1:]), p.dtype) for p in pairs],
        in_specs=[ANY] * nw, out_specs=[ANY] * nw,
        scratch_shapes=[pltpu.SemaphoreType.DMA((3 * nw,)), pltpu.SemaphoreType.DMA((3 * nw,))])(*pairs)


def _share_halves(halves, *, name):
    nw = len(halves)

    def body(*refs):
        srcs, outs, (send_sems, recv_sems) = refs[:nw], refs[nw:2 * nw], refs[2 * nw:]
        _, _, _, sibling, _ = _place()
        cps = [pltpu.make_async_remote_copy(src_ref=src_ref, dst_ref=out_ref, send_sem=send_sems.at[w], recv_sem=recv_sems.at[w],
                                            device_id=sibling, device_id_type=MESH)
               for w, (src_ref, out_ref) in enumerate(zip(srcs, outs))]
        for cp in cps:
            cp.start()
        for cp in cps:
            cp.wait()

    return pl.pallas_call(
        body, name=name, out_shape=[jax.ShapeDtypeStruct(h.shape, h.dtype) for h in halves],
        in_specs=[ANY] * nw, out_specs=[ANY] * nw,
        scratch_shapes=[pltpu.SemaphoreType.DMA((nw,)), pltpu.SemaphoreType.DMA((nw,))])(*halves)


def _row_tile(rows, limit=256):
    return next(t for t in range(limit, 15, -16) if rows % t == 0)


def _pair_add(grad, got, core, *, name):
    _, r, n = grad.shape
    half = r // 2
    tr = _row_tile(half)
    nb = half // tr

    def body(core_ref, g_ref, t_ref, f_ref, b_ref):
        acc = g_ref[...] + t_ref[...]
        f_ref[...] = acc
        b_ref[...] = acc.astype(BF16)

    blk = pl.BlockSpec((1, tr, n), lambda j, i, core_ref: (j, i, 0))
    mine = pl.BlockSpec((1, tr, n), lambda j, i, core_ref: (j, core_ref[0] * nb + i, 0))
    return pl.pallas_call(
        body, name=name,
        grid_spec=pltpu.PrefetchScalarGridSpec(num_scalar_prefetch=1, grid=(N_CHIP, nb), in_specs=[mine, blk], out_specs=[blk, blk]),
        out_shape=[jax.ShapeDtypeStruct((N_CHIP, half, n), F32), jax.ShapeDtypeStruct((N_CHIP, half, n), BF16)],
        compiler_params=_params("parallel", "parallel"))(core, grad, got)


def _chip_add(pair, theirs, chip, *, name):
    _, h, n = pair.shape
    tr = _row_tile(h)

    def body(chip_ref, p_ref, t_ref, o_ref):
        o_ref[...] = ((p_ref[0] + t_ref[0].astype(F32)) + t_ref[1].astype(F32)) + t_ref[2].astype(F32)

    return pl.pallas_call(
        body, name=name,
        grid_spec=pltpu.PrefetchScalarGridSpec(
            num_scalar_prefetch=1, grid=(h // tr,),
            in_specs=[pl.BlockSpec((1, tr, n), lambda i, chip_ref: (chip_ref[0], i, 0)),
                      pl.BlockSpec((3, tr, n), lambda i, chip_ref: (0, i, 0))],
            out_specs=pl.BlockSpec((tr, n), lambda i, chip_ref: (i, 0))),
        out_shape=jax.ShapeDtypeStruct((h, n), F32), compiler_params=_params("parallel"))(chip, pair, theirs)


def _sum_devices(gathered, *, name):
    _, m, n = gathered.shape

    def body(g_ref, tot_ref, loss_ref):
        tot = g_ref[0]
        for dev in range(1, N_DEV):
            tot = tot + g_ref[dev]
        tot_ref[...] = tot
        loss_ref[...] = jnp.full((8, n), (0.5 / D_MODEL) * jnp.sum(tot[0:8]), F32)

    return pl.pallas_call(body, name=name, in_specs=[VMEM_SPEC], out_specs=[VMEM_SPEC, VMEM_SPEC],
                          out_shape=[jax.ShapeDtypeStruct((m, n), F32), jax.ShapeDtypeStruct((8, n), F32)])(gathered)


def _ada_mod(cond_all, w_ada_shard, *, name):
    tn = 512

    def body(a_ref, b_ref, o_ref):
        o_ref[...] = _nn(a_ref[...], b_ref[...], precision=HIGHEST)

    return pl.pallas_call(
        body, name=name, grid=(w_ada_shard.shape[1] // tn,),
        in_specs=[pl.BlockSpec(cond_all.shape, lambda j: (0, 0)), pl.BlockSpec((D_MODEL, tn), lambda j: (0, j))],
        out_specs=pl.BlockSpec((N_DEV, tn), lambda j: (0, j)),
        out_shape=jax.ShapeDtypeStruct((N_DEV, w_ada_shard.shape[1]), F32), compiler_params=_params("parallel"))(cond_all, w_ada_shard)


def _ada_grad(cond_all, dmod_cols, *, name):
    tm = 256

    def body(a_ref, b_ref, o_ref):
        o_ref[...] = lax.dot_general(a_ref[...], b_ref[...], (((0,), (0,)), ((), ())), precision=HIGHEST,
                                     preferred_element_type=F32)

    return pl.pallas_call(
        body, name=name, grid=(D_MODEL // tm,),
        in_specs=[pl.BlockSpec((N_DEV, tm), lambda i: (0, i)), pl.BlockSpec(dmod_cols.shape, lambda i: (0, 0))],
        out_specs=pl.BlockSpec((tm, dmod_cols.shape[1]), lambda i: (i, 0)),
        out_shape=jax.ShapeDtypeStruct((D_MODEL, dmod_cols.shape[1]), F32), compiler_params=_params("parallel"))(cond_all, dmod_cols)


def _silu_rows(c8, *, name):
    def body(c_ref, o_ref):
        cv = c_ref[...]
        o_ref[...] = cv * _sigmoid(cv)

    return pl.pallas_call(body, name=name, in_specs=[VMEM_SPEC], out_specs=VMEM_SPEC,
                          out_shape=jax.ShapeDtypeStruct(c8.shape, F32))(c8)


def _rows128(t, rows=None):
    flat = t.reshape(-1, 128)
    return flat if rows is None else jnp.pad(flat, ((0, rows - flat.shape[0]), (0, 0)))


def _from_col_shards(shards, r, n):
    return shards.reshape(N_CHIP, r, n).transpose(1, 0, 2).reshape(r, N_CHIP * n)


def kernel(x, c, w_ada, b_ada, norm1_g, w_in, gla_w_gate, gla_b_gate, gla_norm_g, q_norm_g, k_norm_g, w_out, norm2_g, w_up, conv_w, conv_b, w_down, loss_target, m_w_ada, m_b_ada, m_norm1_g, m_w_in, m_gla_w_gate, m_gla_b_gate, m_gla_norm_g, m_q_norm_g, m_k_norm_g, m_w_out, m_norm2_g, m_w_up, m_conv_w, m_conv_b, m_w_down, v_w_ada, v_b_ada, v_norm1_g, v_w_in, v_gla_w_gate, v_gla_b_gate, v_gla_norm_g, v_q_norm_g, v_k_norm_g, v_w_out, v_norm2_g, v_w_up, v_conv_w, v_conv_b, v_w_down):
    d = D_MODEL
    ax, ay, ac = lax.axis_index("x"), lax.axis_index("y"), lax.axis_index("c")
    chip, dev = 2 * ax + ay, 4 * ax + 2 * ay + ac

    cond = _silu_rows(jnp.broadcast_to(c, (8, d)), name="cond_silu")[0:1]
    small_in = jnp.concatenate([_rows128(cond), _rows128(conv_w[0]), _rows128(gla_w_gate[0])], axis=0)
    small_in = _rows128(small_in, 56)
    got = _all_gather_small(small_in, name="gather_small").reshape(N_DEV, 56, 128)
    cond_all = got[:, 0:8].reshape(N_DEV, d)
    conv_w_full = _from_col_shards(got[0::2, 8:41].reshape(N_CHIP, 3 * 1408 // 128, 128), 3, 1408)
    gate_full = _from_col_shards(got[0::2, 41:49].reshape(N_CHIP, 16 * 64 // 128, 128), GLA_GATE_RANK, 64)
    mod_part = _ada_mod(cond_all, w_ada[0], name="ada_mod")
    mod_got = _all_gather_small(_rows128(mod_part), name="gather_mod").reshape(N_DEV, N_DEV, 1536)
    mod_all = mod_got[0::2].transpose(1, 0, 2).reshape(N_DEV, 6 * d) + b_ada
    mod = lax.dynamic_slice_in_dim(mod_all, dev, 1, axis=0)

    own = [w[0].astype(BF16).reshape(2, w.shape[1] // 2, w.shape[2]) for w in (w_in, w_out, w_up, w_down)]
    with_own = lambda got, mine: [lax.dynamic_update_index_in_dim(t, o, chip, 0) for t, o in zip(got, mine)]
    got_in, got_out = with_own(_gather_weight_shards(own[:2], name="gather_weights"), own[:2])
    w_in_full = got_in.reshape(N_CHIP, d, 772).transpose(1, 0, 2).reshape(d, N_CHIP * 772)
    w_out_full = got_out.reshape(d, d)
    exchanged = mod_all[0:1, 0:1] + got_in[0, 0, 0:1, 0:1].astype(F32)
    send_sems, recv_sems, own_thru, lands, token = _gather_late_start(own[2:], exchanged, name="gather_late_start")
    mod = mod + token[0:1, 0:1]

    def ffn_weights(after):
        mine, landed = _gather_late_wait(send_sems, recv_sems, own_thru, lands, after, name="gather_late_wait")
        got_up, got_down = with_own(landed, mine)
        return got_up.reshape(N_CHIP, d, 1408).transpose(1, 0, 2).reshape(d, 2 * D_FF), got_down.reshape(D_FF, d)

    late_reduce = []

    def ffn_grads_ready(g_wup_b, g_wdown_b):
        halves_of = lambda g: g.reshape(N_CHIP, 2, g.shape[-2] // 2, g.shape[-1])
        late_reduce.extend(_direct_reduce_start([halves_of(g_wup_b), halves_of(g_wdown_b.reshape(N_CHIP, D_FF // N_CHIP, d))],
                                                name="reduce_late_start"))
        return late_reduce[4]

    err2, grad_x, (g_wi, g_wo, g_wup, g_wdown), small = _local_step(
        x[0], loss_target[0], mod, _in_proj_layout(w_in_full), w_out_full, ffn_weights, ffn_grads_ready, conv_w_full, conv_b,
        _gate_layout(gate_full), gla_b_gate, gla_norm_g, q_norm_g, k_norm_g, norm1_g, norm2_g)

    pieces = [err2[0], small["dmod"], small["norm1_g"], small["norm2_g"], small["gla_w_gate"].reshape(-1), small["gla_b_gate"],
              small["gla_norm_g"], small["q_norm_g"], small["k_norm_g"], small["conv_w"].reshape(-1), small["conv_b"]]
    sizes = [p.shape[0] for p in pieces]
    at = [sum(sizes[:i]) for i in range(len(sizes) + 1)]
    vec = _rows128(jnp.concatenate(pieces), 288)
    got = _all_gather_small(vec, name="gather_grads").reshape(N_DEV, 288, 128)
    total, loss8 = _sum_devices(got, name="sum_devices")
    total = total.reshape(-1)
    seg = lambda i: total[at[i]:at[i + 1]]
    dmod_all = got.reshape(N_DEV, -1)[:, at[1]:at[2]]
    g_small = dict(
        b_ada=seg(1)[None], norm1_g=seg(2)[None], norm2_g=seg(3)[None],
        gla_w_gate=lax.dynamic_slice_in_dim(seg(4).reshape(GLA_GATE_RANK, 256), chip * 64, 64, axis=1),
        gla_b_gate=seg(5)[None], gla_norm_g=seg(6)[None], q_norm_g=seg(7)[None], k_norm_g=seg(8)[None],
        conv_w=lax.dynamic_slice_in_dim(seg(9).reshape(3, 2 * D_FF), chip * 1408, 1408, axis=1), conv_b=seg(10)[None])
    dmod_cols = lax.dynamic_slice_in_dim(dmod_all.reshape(N_DEV, 6 * d), chip * 1536, 1536, axis=1)
    g_w_ada = _ada_grad(cond_all, dmod_cols, name="ada_grad")

    tags = ("w_in", "w_out")
    g_parts = [_in_proj_grad_layout(g_wi).reshape(d, N_CHIP, 772).transpose(1, 0, 2), g_wo.reshape(N_CHIP, d // N_CHIP, d)]
    core_id, chip_id = jnp.reshape(ac, (1,)).astype(jnp.int32), jnp.reshape(chip, (1,)).astype(jnp.int32)
    got = _pair_exchange_halves([g.reshape(N_CHIP, 2, g.shape[1] // 2, g.shape[2]) for g in g_parts], name="reduce_pair")
    pairs = [_pair_add(g, t, core_id, name=f"reduce_pair_add_{tag}") for g, t, tag in zip(g_parts, got, tags)]
    theirs = _chip_scatter([pb for _, pb in pairs], name="reduce_chips")
    summed = [_chip_add(pf, t, chip_id, name=f"reduce_chips_add_{tag}") for (pf, _), t, tag in zip(pairs, theirs, tags)]
    landed = _direct_reduce_wait(*late_reduce[:4], grad_x, name="reduce_late_wait")
    summed += [_direct_reduce_add(g, t, chip_id, core_id, name=f"reduce_late_add_{tag}")
               for g, t, tag in zip((g_wup, g_wdown.reshape(N_CHIP, D_FF // N_CHIP, d)), landed, ("w_up", "w_down"))]
    others = _share_halves(summed, name="share_pair")
    g_big = [jnp.concatenate([jnp.where(ac == 0, mine, other), jnp.where(ac == 0, other, mine)], axis=0)
             for mine, other in zip(summed, others)]

    grads = dict(w_ada=g_w_ada, w_in=g_big[0], w_out=g_big[1], w_up=g_big[2], w_down=g_big[3], **g_small)
    names = ["w_ada", "b_ada", "norm1_g", "w_in", "gla_w_gate", "gla_b_gate", "gla_norm_g", "q_norm_g", "k_norm_g", "w_out",
             "norm2_g", "w_up", "conv_w", "conv_b", "w_down"]
    ws = dict(w_ada=w_ada, b_ada=b_ada, norm1_g=norm1_g, w_in=w_in, gla_w_gate=gla_w_gate, gla_b_gate=gla_b_gate,
              gla_norm_g=gla_norm_g, q_norm_g=q_norm_g, k_norm_g=k_norm_g, w_out=w_out, norm2_g=norm2_g, w_up=w_up,
              conv_w=conv_w, conv_b=conv_b, w_down=w_down)
    ms = dict(w_ada=m_w_ada, b_ada=m_b_ada, norm1_g=m_norm1_g, w_in=m_w_in, gla_w_gate=m_gla_w_gate, gla_b_gate=m_gla_b_gate,
              gla_norm_g=m_gla_norm_g, q_norm_g=m_q_norm_g, k_norm_g=m_k_norm_g, w_out=m_w_out, norm2_g=m_norm2_g, w_up=m_w_up,
              conv_w=m_conv_w, conv_b=m_conv_b, w_down=m_w_down)
    vs = dict(w_ada=v_w_ada, b_ada=v_b_ada, norm1_g=v_norm1_g, w_in=v_w_in, gla_w_gate=v_gla_w_gate, gla_b_gate=v_gla_b_gate,
              gla_norm_g=v_gla_norm_g, q_norm_g=v_q_norm_g, k_norm_g=v_k_norm_g, w_out=v_w_out, norm2_g=v_norm2_g, w_up=v_w_up,
              conv_w=v_conv_w, conv_b=v_conv_b, w_down=v_w_down)
    g_out, d_out, m_out, v_out = [], [], [], []
    for nm in names:
        w2 = ws[nm].reshape(ws[nm].shape[-2:])
        g2 = grads[nm].reshape(w2.shape)
        dl, mn, vn = _adamw(w2, g2, ms[nm].reshape(w2.shape), vs[nm].reshape(w2.shape), name=f"adamw_{nm}")
        shape = ws[nm].shape
        g_out.append(g2.reshape(shape))
        d_out.append(dl.reshape(shape))
        m_out.append(mn.reshape(shape))
        v_out.append(vn.reshape(shape))
    return (loss8[0, 0], grad_x[None], *g_out, *d_out, *m_out, *v_out)
```

```python
import functools

import jax
import jax.numpy as jnp
from jax import lax
from jax.experimental import pallas as pl
from jax.experimental.pallas import tpu as pltpu

F32, BF16 = jnp.float32, jnp.bfloat16
HIGHEST = lax.Precision.HIGHEST
MESH = pl.DeviceIdType.MESH

D_MODEL = 1024
GLA_CHUNK = 64
GLA_GATE_TAU = 16.0
GLA_GATE_RANK = 16
HEAD_LANES = 128
ATTN_BLOCK = 128
DILATIONS = (1, 4, 16)
ALIBI_SLOPES = tuple(2.0 ** (-(h + 1)) for h in range(8))
D_FF = 2816
EPS = 1e-6
C_GQ, C_GK, C_GV, C_GR, C_AQ, C_AK, C_AV, C_LR, PROJ_W = 0, 256, 512, 1024, 1536, 2048, 2560, 3072, 3200
ADAM_LR, ADAM_B1, ADAM_B2, ADAM_EPS, ADAM_WD, ADAM_STEP = 0.001, 0.9, 0.999, 1e-08, 0.01, 10
VMEM_LIMIT_BYTES = 56 * 1024 * 1024
ROW_TILE = 256


def _params(*sem):
    return pltpu.CompilerParams(dimension_semantics=sem or None, vmem_limit_bytes=VMEM_LIMIT_BYTES)


def _nt(a, b):
    return lax.dot_general(a, b, (((1,), (1,)), ((), ())), preferred_element_type=F32)


def _tn(a, b):
    return lax.dot_general(a, b, (((0,), (0,)), ((), ())), preferred_element_type=F32)


def _nn(a, b, precision=None):
    return jnp.dot(a, b, preferred_element_type=F32, precision=precision)


def _split3(v):
    hi = v.astype(BF16)
    rest = v - hi.astype(F32)
    mid = rest.astype(BF16)
    return hi, mid, (rest - mid.astype(F32)).astype(BF16)


def _sum_right(v, ones):
    hi, mid, lo = _split3(v)
    return (_nn(lo, ones) + _nn(mid, ones)) + _nn(hi, ones)


def _sum_left(ones, v):
    hi, mid, lo = _split3(v)
    return (_nn(ones, lo) + _nn(ones, mid)) + _nn(ones, hi)


def _fold8(v):
    return v.reshape(v.shape[0] // 8, 8, v.shape[1]).sum(axis=0)


def _spread_total(ref):
    t = ref[...]
    ref[...] = jnp.broadcast_to(jnp.sum(t, axis=-2, keepdims=True), t.shape)


def _sigmoid(x):
    return 1.0 / (1.0 + jnp.exp(-x))


def _mm(a, b, *, ta=False, tb=False, out_dtype=F32, tm, tn, tk, shard_cols=False, also_bf16=False, name):
    (k_a, m) = a.shape if ta else a.shape[::-1]
    (k_b, n) = b.shape[::-1] if tb else b.shape
    assert k_a == k_b and m % tm == 0 and n % tn == 0 and k_a % tk == 0, (name, a.shape, b.shape)
    nk = k_a // tk
    assert nk == 1 or out_dtype == F32, name
    dims = (((0 if ta else 1,), (1 if tb else 0,)), ((), ()))

    def body(a_ref, b_ref, o_ref, *rounded):
        k = pl.program_id(2)
        part = lax.dot_general(a_ref[...].astype(BF16), b_ref[...].astype(BF16), dims, preferred_element_type=F32)
        if nk == 1:
            o_ref[...] = part.astype(out_dtype)
        else:
            @pl.when(k == 0)
            def _():
                o_ref[...] = part

            @pl.when(k > 0)
            def _():
                o_ref[...] += part

        if also_bf16:
            @pl.when(k == nk - 1)
            def _():
                rounded[0][...] = o_ref[...].astype(BF16)

    a_spec = pl.BlockSpec((tk, tm), lambda i, j, k: (k, i)) if ta else pl.BlockSpec((tm, tk), lambda i, j, k: (i, k))
    b_spec = pl.BlockSpec((tn, tk), lambda i, j, k: (j, k)) if tb else pl.BlockSpec((tk, tn), lambda i, j, k: (k, j))
    if shard_cols:
        o_spec, o_shape = pl.BlockSpec((None, tm, tn), lambda i, j, k: (j, i, 0)), (n // tn, m, tn)
    else:
        o_spec, o_shape = pl.BlockSpec((tm, tn), lambda i, j, k: (i, j)), (m, n)
    shapes = [jax.ShapeDtypeStruct(o_shape, out_dtype)] + ([jax.ShapeDtypeStruct(o_shape, BF16)] if also_bf16 else [])
    out = pl.pallas_call(
        body, name=name, grid=(m // tm, n // tn, nk), in_specs=[a_spec, b_spec], out_specs=[o_spec] * len(shapes),
        out_shape=shapes, compiler_params=_params("parallel", "parallel", "arbitrary"))(a, b)
    return out if also_bf16 else out[0]


def _norm_mod_fwd(x, branch, gate, gain, scale, shift, *, name):
    s, d = x.shape
    tm = ROW_TILE
    has_branch = branch is not None

    def body(*refs):
        if has_branch:
            x_ref, br_ref, gate_ref, gain_ref, sc_ref, sh_ref, x1_ref, h_ref, ht_ref = refs
            xv = x_ref[...] + gate_ref[...] * br_ref[...]
            x1_ref[...] = xv
        else:
            x_ref, gain_ref, sc_ref, sh_ref, h_ref, ht_ref = refs
            xv = x_ref[...]
        r = lax.rsqrt(jnp.mean(xv * xv, axis=-1, keepdims=True) + EPS)
        h = (xv * r) * gain_ref[...] * (1.0 + sc_ref[...]) + sh_ref[...]
        h_ref[...] = h.astype(BF16)
        ht_ref[...] = h.T.astype(BF16)

    row = pl.BlockSpec((tm, d), lambda i: (i, 0))
    col = pl.BlockSpec((d, tm), lambda i: (0, i))
    vec = pl.BlockSpec((1, d), lambda i: (0, 0))
    h_shapes = [jax.ShapeDtypeStruct((s, d), BF16), jax.ShapeDtypeStruct((d, s), BF16)]
    if has_branch:
        return pl.pallas_call(
            body, name=name, grid=(s // tm,), in_specs=[row, row, vec, vec, vec, vec], out_specs=[row, row, col],
            out_shape=[jax.ShapeDtypeStruct((s, d), F32)] + h_shapes,
            compiler_params=_params("parallel"))(x, branch, gate, gain, scale, shift)
    h, ht = pl.pallas_call(
        body, name=name, grid=(s // tm,), in_specs=[row, vec, vec, vec], out_specs=[row, col],
        out_shape=h_shapes, compiler_params=_params("parallel"))(x, gain, scale, shift)
    return x, h, ht


def _norm_mod_bwd(x, dh, dres, gain, scale, branch, gate, *, name):
    s, d = x.shape
    tm = ROW_TILE
    has_branch = branch is not None

    def body(*refs):
        if has_branch:
            x_ref, dh_ref, dres_ref, gain_ref, sc_ref, br_ref, gate_ref, dx_ref, dbr_ref, sums_ref = refs
        else:
            x_ref, dh_ref, dres_ref, gain_ref, sc_ref, dx_ref, sums_ref = refs
        i = pl.program_id(0)

        @pl.when(i == 0)
        def _():
            sums_ref[...] = jnp.zeros_like(sums_ref)

        xv, dhv = x_ref[...], dh_ref[...]
        r = lax.rsqrt(jnp.mean(xv * xv, axis=-1, keepdims=True) + EPS)
        xn = xv * r
        dxn = dhv * (gain_ref[...] * (1.0 + sc_ref[...]))
        dx = dres_ref[...] + r * (dxn - xn * jnp.mean(dxn * xn, axis=-1, keepdims=True))
        dx_ref[...] = dx
        sums_ref[0] += _fold8(dhv * xn)
        sums_ref[1] += _fold8(dhv)
        if has_branch:
            dbr_ref[...] = (gate_ref[...] * dx).astype(BF16)
            sums_ref[2] += _fold8(dx * br_ref[...])

        @pl.when(i == s // tm - 1)
        def _():
            _spread_total(sums_ref)

    row = pl.BlockSpec((tm, d), lambda i: (i, 0))
    vec = pl.BlockSpec((1, d), lambda i: (0, 0))
    sums = pl.BlockSpec((3, 8, d), lambda i: (0, 0, 0))
    sums_shape = jax.ShapeDtypeStruct((3, 8, d), F32)
    if has_branch:
        return pl.pallas_call(
            body, name=name, grid=(s // tm,), in_specs=[row, row, row, vec, vec, row, vec], out_specs=[row, row, sums],
            out_shape=[jax.ShapeDtypeStruct((s, d), F32), jax.ShapeDtypeStruct((s, d), BF16), sums_shape],
            compiler_params=_params("arbitrary"))(x, dh, dres, gain, scale, branch, gate)
    dx, sm = pl.pallas_call(
        body, name=name, grid=(s // tm,), in_specs=[row, row, row, vec, vec], out_specs=[row, sums],
        out_shape=[jax.ShapeDtypeStruct((s, d), F32), sums_shape],
        compiler_params=_params("arbitrary"))(x, dh, dres, gain, scale)
    return dx, None, sm


GLA_ROWS = 256


def _gla_block_setup(lr_ref, wg_ref, bg_ref):
    t, c = GLA_ROWS, GLA_CHUNK
    ri = lax.broadcasted_iota(jnp.int32, (t, t), 0)
    ci = lax.broadcasted_iota(jnp.int32, (t, t), 1)
    same = (ri // c) == (ci // c)
    causal, upper = same & (ci <= ri), same & (ci >= ri)
    z = _nn(lr_ref[...].astype(BF16), wg_ref[...]) + bg_ref[...]
    g = (jnp.minimum(z, 0.0) - jnp.log(1.0 + jnp.exp(-jnp.abs(z)))) * (1.0 / GLA_GATE_TAU)
    hi, mid, lo = _split3(g)
    total = lambda ones: (_nn(ones, lo) + _nn(ones, mid)) + _nn(ones, hi)
    return z, total(causal.astype(BF16)), total(same.astype(BF16)), causal, upper


def _chunks(t):
    return [t[i * GLA_CHUNK:(i + 1) * GLA_CHUNK] for i in range(GLA_ROWS // GLA_CHUNK)]


def _gla_fwd(proj, wg, bg, gn, *, name):
    s = proj.shape[0]
    tb, c = GLA_ROWS, GLA_CHUNK
    cb = tb // c

    def body(q_ref, k_ref, v_ref, r_ref, lr_ref, wg_ref, bg_ref, gn_ref, o_ref, y_ref, st_ref, state):
        i = pl.program_id(0)

        @pl.when(i == 0)
        def _():
            state[...] = jnp.zeros_like(state)

        low = lax.broadcasted_iota(jnp.int32, (tb, HEAD_LANES), 1) < 64
        masks = (low, jnp.logical_not(low))
        _, b, b_end, causal, _ = _gla_block_setup(lr_ref, wg_ref, bg_ref)
        for p in range(2):
            cols = pl.ds(p * HEAD_LANES, HEAD_LANES)
            bp, bep = (t[:, p * HEAD_LANES:(p + 1) * HEAD_LANES] for t in (b, b_end))
            k = k_ref[:, cols]
            q_in = q_ref[:, cols] * 0.125 * jnp.exp(bp)
            k_out = (k * jnp.exp(-bp)).astype(BF16)
            k_end = k * jnp.exp(bep - bp)
            qms = [jnp.where(m, q_in, 0.0).astype(BF16) for m in masks]
            kes = [jnp.where(m, k_end, 0.0).astype(BF16) for m in masks]
            vs = [v_ref[:, pl.ds((2 * p + e) * HEAD_LANES, HEAD_LANES)].astype(BF16) for e in range(2)]
            grow = [_tn(v0, k0) + _tn(v1, k1) for v0, k0, v1, k1 in zip(_chunks(vs[0]), _chunks(kes[0]), _chunks(vs[1]), _chunks(kes[1]))]
            st, entering = state[p], []
            for ch in range(cb):
                entering.append(st)
                st_ref[ch, p] = st
                st = st * jnp.exp(bep[ch * c:ch * c + 1, :]) + grow[ch]
            state[p] = st
            for e in range(2):
                hc = pl.ds((2 * p + e) * HEAD_LANES, HEAD_LANES)
                a = jnp.where(causal, _nt(qms[e], k_out), 0.0).astype(BF16)
                carried = jnp.concatenate([_nt(qc, sc.astype(BF16)) for qc, sc in zip(_chunks(qms[e]), entering)], axis=0)
                o = _nn(a, vs[e]) + carried
                o_ref[:, hc] = o
                rr = r_ref[:, hc]
                on = o * lax.rsqrt(jnp.mean(o * o, axis=-1, keepdims=True) + EPS)
                y_ref[:, hc] = (on * gn_ref[...] * (rr * _sigmoid(rr))).astype(BF16)

    def col(width, at):
        return pl.BlockSpec((tb, width), lambda i: (i, at // width))

    full = lambda shape: pl.BlockSpec(shape, lambda i: tuple(0 for _ in shape))
    return pl.pallas_call(
        body, name=name, grid=(s // tb,),
        in_specs=[col(256, C_GQ), col(256, C_GK), col(512, C_GV), col(512, C_GR), col(128, C_LR),
                  full((HEAD_LANES, 256)), full((1, 256)), full((1, HEAD_LANES))],
        out_specs=[pl.BlockSpec((tb, 512), lambda i: (i, 0)), pl.BlockSpec((tb, 512), lambda i: (i, 0)),
                   pl.BlockSpec((cb, 2, HEAD_LANES, HEAD_LANES), lambda i: (i, 0, 0, 0))],
        out_shape=[jax.ShapeDtypeStruct((s, 512), F32), jax.ShapeDtypeStruct((s, 512), BF16),
                   jax.ShapeDtypeStruct((s // c, 2, HEAD_LANES, HEAD_LANES), F32)],
        scratch_shapes=[pltpu.VMEM((2, HEAD_LANES, HEAD_LANES), F32)],
        compiler_params=_params("arbitrary"))(proj, proj, proj, proj, proj, wg, bg, gn)


def _gla_bwd(proj, wg, bg, gn, o_raw, states, dmixed, *, name):
    s = proj.shape[0]
    tb, c = GLA_ROWS, GLA_CHUNK
    cb = tb // c
    nblk, nch = s // tb, s // c

    def body(q_ref, k_ref, v_ref, r_ref, lr_ref, wg_ref, bg_ref, gn_ref, o_ref, st_ref, stn_ref, dy_ref,
             dq_ref, dk_ref, dv_ref, dr_ref, dlr_ref, gwg_ref, sums_ref, dstate):
        i = pl.program_id(0)

        @pl.when(i == 0)
        def _():
            dstate[...] = jnp.zeros_like(dstate)
            gwg_ref[...] = jnp.zeros_like(gwg_ref)
            sums_ref[...] = jnp.zeros_like(sums_ref)

        low = lax.broadcasted_iota(jnp.int32, (tb, HEAD_LANES), 1) < 64
        masks = (low, jnp.logical_not(low))
        z, b, b_end, causal, upper = _gla_block_setup(lr_ref, wg_ref, bg_ref)
        lr_b = lr_ref[...].astype(BF16)
        dlr = jnp.zeros((tb, HEAD_LANES), F32)
        for p in range(2):
            cols = pl.ds(p * HEAD_LANES, HEAD_LANES)
            sl = slice(p * HEAD_LANES, (p + 1) * HEAD_LANES)
            bp, bep = b[:, sl], b_end[:, sl]
            e_in, e_out, e_end = jnp.exp(bp), jnp.exp(-bp), jnp.exp(bep - bp)
            q = q_ref[:, cols] * 0.125
            k = k_ref[:, cols]
            q_in, k_out, k_end = q * e_in, k * e_out, k * e_end
            qms = [jnp.where(m, q_in, 0.0).astype(BF16) for m in masks]
            kms_out = [jnp.where(m, k_out, 0.0).astype(BF16) for m in masks]
            kms_end = [jnp.where(m, k_end, 0.0).astype(BF16) for m in masks]
            vs, dos = [], []
            for e in range(2):
                hc = pl.ds((2 * p + e) * HEAD_LANES, HEAD_LANES)
                o, rr, dy = o_ref[:, hc], r_ref[:, hc], dy_ref[:, hc]
                sg = _sigmoid(rr)
                rs = lax.rsqrt(jnp.mean(o * o, axis=-1, keepdims=True) + EPS)
                on = o * rs
                t = dy * (rr * sg)
                sums_ref[1, :, hc] += _fold8(t * on)
                dn = t * gn_ref[...]
                dos.append((rs * (dn - on * jnp.mean(dn * on, axis=-1, keepdims=True))).astype(BF16))
                dr_ref[:, hc] = (dy * on * gn_ref[...] * (sg * (1.0 + rr * (1.0 - sg)))).astype(BF16)
                vs.append(v_ref[:, hc].astype(BF16))
            grow = [_tn(d0, q0) + _tn(d1, q1) for d0, q0, d1, q1 in zip(_chunks(dos[0]), _chunks(qms[0]), _chunks(dos[1]), _chunks(qms[1]))]
            entering = [st_ref[ch, p] for ch in range(cb)]
            dst, leaving_grad = dstate[p], [None] * cb
            for ch in reversed(range(cb)):
                leaving_grad[ch] = dst
                dst = dst * jnp.exp(bep[ch * c:ch * c + 1, :]) + grow[ch]
            dstate[p] = dst
            leaving = entering[1:] + [stn_ref[0, p]]
            felt = jnp.concatenate([jnp.broadcast_to(jnp.sum(dg_st * st, axis=0, keepdims=True), (c, HEAD_LANES))
                                    for dg_st, st in zip(leaving_grad, leaving)], axis=0)
            per_chunk = lambda rows, mats, fn: jnp.concatenate([fn(r, m.astype(BF16)) for r, m in zip(_chunks(rows), mats)], axis=0)
            dq_in = jnp.zeros((tb, HEAD_LANES), F32)
            dk_out = jnp.zeros((tb, HEAD_LANES), F32)
            dk_end = jnp.zeros((tb, HEAD_LANES), F32)
            for e in range(2):
                hc = pl.ds((2 * p + e) * HEAD_LANES, HEAD_LANES)
                a = jnp.where(causal, _nt(qms[e], kms_out[e]), 0.0).astype(BF16)
                da = jnp.where(causal, _nt(dos[e], vs[e]), 0.0).astype(BF16)
                dv_ref[:, hc] = (_tn(a, dos[e]) + per_chunk(kms_end[e], leaving_grad, _nt)).astype(BF16)
                dq_in = dq_in + jnp.where(masks[e], per_chunk(dos[e], entering, _nn) + _nn(da, kms_out[e]), 0.0)
                dk_out = dk_out + _tn(da, qms[e])
                dk_end = dk_end + jnp.where(masks[e], per_chunk(vs[e], leaving_grad, _nn), 0.0)
            dq = dq_in * e_in
            dk = dk_out * e_out + dk_end * e_end
            dq_ref[:, cols] = (dq * 0.125).astype(BF16)
            dk_ref[:, cols] = dk.astype(BF16)
            dg = _sum_left(upper.astype(BF16), q * dq - k * dk) + felt
            dz = dg * (1.0 / GLA_GATE_TAU) * _sigmoid(-z[:, sl])
            dz_b = dz.astype(BF16)
            sums_ref[0, :, cols] += _fold8(dz)
            dlr = dlr + _nt(dz_b, wg_ref[:, cols])
            gwg_ref[:, cols] += _tn(lr_b, dz_b)
        dlr_ref[...] = dlr.astype(BF16)

        @pl.when(i == nblk - 1)
        def _():
            _spread_total(sums_ref)

    rev = lambda i: nblk - 1 - i

    def col(width, at):
        return pl.BlockSpec((tb, width), lambda i: (rev(i), at // width))

    full = lambda shape: pl.BlockSpec(shape, lambda i: tuple(0 for _ in shape))
    out_col = lambda width: pl.BlockSpec((tb, width), lambda i: (rev(i), 0))
    return pl.pallas_call(
        body, name=name, grid=(nblk,),
        in_specs=[col(256, C_GQ), col(256, C_GK), col(512, C_GV), col(512, C_GR), col(128, C_LR),
                  full((HEAD_LANES, 256)), full((1, 256)), full((1, HEAD_LANES)),
                  pl.BlockSpec((tb, 512), lambda i: (rev(i), 0)),
                  pl.BlockSpec((cb, 2, HEAD_LANES, HEAD_LANES), lambda i: (rev(i), 0, 0, 0)),
                  pl.BlockSpec((1, 2, HEAD_LANES, HEAD_LANES), lambda i: (jnp.minimum((rev(i) + 1) * cb, nch - 1), 0, 0, 0)),
                  pl.BlockSpec((tb, 512), lambda i: (rev(i), 0))],
        out_specs=[out_col(256), out_col(256), out_col(512), out_col(512), out_col(128),
                   full((HEAD_LANES, 256)), full((2, 8, 512))],
        out_shape=[jax.ShapeDtypeStruct((s, 256), BF16), jax.ShapeDtypeStruct((s, 256), BF16),
                   jax.ShapeDtypeStruct((s, 512), BF16), jax.ShapeDtypeStruct((s, 512), BF16),
                   jax.ShapeDtypeStruct((s, 128), BF16), jax.ShapeDtypeStruct((HEAD_LANES, 256), F32),
                   jax.ShapeDtypeStruct((2, 8, 512), F32)],
        scratch_shapes=[pltpu.VMEM((2, HEAD_LANES, HEAD_LANES), F32)],
        compiler_params=_params("arbitrary"))(proj, proj, proj, proj, proj, wg, bg, gn, o_raw, states, states, dmixed)


def _head_sum_matrix():
    ri = lax.broadcasted_iota(jnp.int32, (512, 512), 0) // 64
    ci = lax.broadcasted_iota(jnp.int32, (512, 512), 1) // 64
    return (ri == ci).astype(BF16)


def _attn_prep(proj, qg, kg, *, name):
    s = proj.shape[0]
    tm = ROW_TILE

    def body(q_ref, k_ref, qg_ref, kg_ref, qa_ref, ka_ref):
        hs = _head_sum_matrix()
        q, k = q_ref[...], k_ref[...]
        qr = lax.rsqrt(_sum_right(q * q, hs) * (1.0 / 64) + EPS)
        kr = lax.rsqrt(_sum_right(k * k, hs) * (1.0 / 64) + EPS)
        qa_ref[...] = q * qr * qg_ref[...] * 0.125
        ka_ref[...] = k * kr * kg_ref[...]

    col = lambda at: pl.BlockSpec((tm, 512), lambda i: (i, at // 512))
    vec = pl.BlockSpec((1, 512), lambda i: (0, 0))
    out = pl.BlockSpec((tm, 512), lambda i: (i, 0))
    return pl.pallas_call(
        body, name=name, grid=(s // tm,), in_specs=[col(C_AQ), col(C_AK), vec, vec], out_specs=[out] * 2,
        out_shape=[jax.ShapeDtypeStruct((s, 512), F32)] * 2, compiler_params=_params("parallel"))(proj, proj, qg, kg)


FAR = 1e30
LOG2E, LN2 = 1.4426950408889634, 0.6931471805599453


def _attn_distance(first):
    blk = ATTN_BLOCK
    iq = lax.broadcasted_iota(jnp.int32, (2 * blk, 2 * blk), 0) & (blk - 1)
    ik = lax.broadcasted_iota(jnp.int32, (2 * blk, 2 * blk), 1)
    rel = iq + blk - ik
    valid = (rel >= 0) & (rel <= blk) & (jnp.logical_not(first) | (ik >= blk))
    return jnp.where(valid, rel.astype(F32), FAR)


def _stack_heads(t2):
    low = lax.broadcasted_iota(jnp.int32, t2.shape, 1) < 64
    return jnp.concatenate([jnp.where(low, t2, 0.0), jnp.where(low, 0.0, t2)], axis=0).astype(BF16)


def _unstack_heads(t):
    blk = ATTN_BLOCK
    low = lax.broadcasted_iota(jnp.int32, (blk, HEAD_LANES), 1) < 64
    return jnp.where(low, t[0:blk], t[blk:2 * blk])


def _attn_scores(qs, kcat, slopes, dil, dist):
    top = lax.broadcasted_iota(jnp.int32, (2 * ATTN_BLOCK, 1), 0) < ATTN_BLOCK
    return _nt(qs, kcat) - jnp.where(top, slopes[0] * (dil * LOG2E), slopes[1] * (dil * LOG2E)) * dist


def _pair_slopes(p):
    if isinstance(p, int):
        return ALIBI_SLOPES[2 * p], ALIBI_SLOPES[2 * p + 1]
    pick = lambda e: jnp.where(p == 0, ALIBI_SLOPES[e], jnp.where(p == 1, ALIBI_SLOPES[2 + e],
                               jnp.where(p == 2, ALIBI_SLOPES[4 + e], ALIBI_SLOPES[6 + e])))
    return pick(0), pick(1)


def _attn_pair_fwd(q2, kcat, vcat, slopes, dil, dist):
    sc = _attn_scores(_stack_heads(q2 * LOG2E), kcat, slopes, dil, dist)
    m = jnp.max(sc, axis=-1, keepdims=True)
    pr = jnp.exp2(sc - m)
    den = jnp.sum(pr, axis=-1, keepdims=True)
    o = _nn(pr.astype(BF16), vcat) / den
    lse = jnp.broadcast_to(m + jnp.log2(den), o.shape)
    return _unstack_heads(o), _unstack_heads(lse)


def _attn_pair_bwd(q2, kcat, vcat, do2, y2, lse2, slopes, dil, dist):
    lane = lax.broadcasted_iota(jnp.int32, (ATTN_BLOCK, HEAD_LANES), 1)
    low = lane < 64
    prod = do2 * y2
    per_head = lambda t, pick: jnp.concatenate([jnp.sum(jnp.where(pick(0), t, 0.0), axis=-1, keepdims=True),
                                                jnp.sum(jnp.where(pick(1), t, 0.0), axis=-1, keepdims=True)], axis=0)
    lse = per_head(lse2, lambda e: lane == 64 * e)
    delta = per_head(prod, lambda e: low if e == 0 else jnp.logical_not(low))
    qs, dos = _stack_heads(q2 * LOG2E), _stack_heads(do2)
    pr = jnp.exp2(_attn_scores(qs, kcat, slopes, dil, dist) - lse)
    ds = (pr * (_nt(dos, vcat) - delta)).astype(BF16)
    return _unstack_heads(_nn(ds, kcat)), _tn(ds, qs) * LN2, _tn(pr.astype(BF16), dos)


def _attn_specs(dil):
    rows = ATTN_BLOCK * dil
    if dil == 1:
        cur = lambda at: pl.BlockSpec((rows, 512), lambda n: (n, at // 512))
        prev = lambda at: pl.BlockSpec((rows, 512), lambda n: (jnp.maximum(n - 1, 0), at // 512))
    else:
        cur = lambda at: pl.BlockSpec((rows, HEAD_LANES), lambda n, p: (n, at // HEAD_LANES + p))
        prev = lambda at: pl.BlockSpec((rows, HEAD_LANES), lambda n, p: (jnp.maximum(n - 1, 0), at // HEAD_LANES + p))
    return cur, prev


def _attn_loop(dil, one_pair):
    if dil == 1:
        for p in range(4):
            one_pair(slice(None), pl.ds(p * HEAD_LANES, HEAD_LANES), p)
    else:
        p = pl.program_id(1)

        def step(r, carry):
            one_pair(pl.ds(r, ATTN_BLOCK, stride=dil), slice(None), p)
            return carry

        lax.fori_loop(0, dil, step, 0, unroll=min(dil, 4))


def _dil_attn_fwd(qa, ka, proj, dil, *, name):
    s = qa.shape[0]

    def body(q_ref, kp_ref, kc_ref, vp_ref, vc_ref, o_ref, lse_ref):
        dist = _attn_distance(pl.program_id(0) == 0)

        def one_pair(rows, cols, p):
            kcat = jnp.concatenate([kp_ref[rows, cols], kc_ref[rows, cols]], axis=0).astype(BF16)
            vcat = jnp.concatenate([vp_ref[rows, cols], vc_ref[rows, cols]], axis=0).astype(BF16)
            o2, lse2 = _attn_pair_fwd(q_ref[rows, cols], kcat, vcat, _pair_slopes(p), dil, dist)
            o_ref[rows, cols] = o2
            lse_ref[rows, cols] = lse2

        _attn_loop(dil, one_pair)

    cur, prev = _attn_specs(dil)
    grid = (s // ATTN_BLOCK,) if dil == 1 else (s // (ATTN_BLOCK * dil), 4)
    return pl.pallas_call(
        body, name=name, grid=grid, in_specs=[cur(0), prev(0), cur(0), prev(C_AV), cur(C_AV)], out_specs=[cur(0), cur(0)],
        out_shape=[jax.ShapeDtypeStruct((s, 512), F32)] * 2,
        compiler_params=_params(*["parallel"] * len(grid)))(qa, ka, ka, proj, proj)


def _attn_merge(branches, y_gla, *, name):
    s = y_gla.shape[0]
    tm = ROW_TILE

    def body(o0, l0, o1, l1, o2, l2, yg_ref, mixed_ref, y_ref, lse_ref):
        m = jnp.maximum(jnp.maximum(l0[...], l1[...]), l2[...])
        w0, w1, w2 = jnp.exp2(l0[...] - m), jnp.exp2(l1[...] - m), jnp.exp2(l2[...] - m)
        zs = w0 + w1 + w2
        y = (w0 * o0[...] + w1 * o1[...] + w2 * o2[...]) / zs
        y_ref[...] = y
        lse_ref[...] = m + jnp.log2(zs)
        mixed_ref[:, 0:512] = yg_ref[...]
        mixed_ref[:, 512:1024] = y.astype(BF16)

    blk = pl.BlockSpec((tm, 512), lambda i: (i, 0))
    args = [t for pair in branches for t in pair]
    return pl.pallas_call(
        body, name=name, grid=(s // tm,), in_specs=[blk] * 7,
        out_specs=[pl.BlockSpec((tm, 1024), lambda i: (i, 0)), blk, blk],
        out_shape=[jax.ShapeDtypeStruct((s, 1024), BF16), jax.ShapeDtypeStruct((s, 512), F32),
                   jax.ShapeDtypeStruct((s, 512), F32)],
        compiler_params=_params("parallel"))(*args, y_gla)


def _dil_attn_bwd(qa, ka, proj, y_att, lse, dmixed, dil, *, name):
    s = qa.shape[0]
    blk = ATTN_BLOCK

    def body(q_ref, kp_ref, kc_ref, vp_ref, vc_ref, y_ref, lse_ref, do_ref, dq_ref, dkc_ref, dkp_ref, dvc_ref, dvp_ref):
        dist = _attn_distance(pl.program_id(0) == 0)

        def one_pair(rows, cols, p):
            kcat = jnp.concatenate([kp_ref[rows, cols], kc_ref[rows, cols]], axis=0).astype(BF16)
            vcat = jnp.concatenate([vp_ref[rows, cols], vc_ref[rows, cols]], axis=0).astype(BF16)
            dq, dk, dv = _attn_pair_bwd(q_ref[rows, cols], kcat, vcat, do_ref[rows, cols], y_ref[rows, cols],
                                        lse_ref[rows, cols], _pair_slopes(p), dil, dist)
            dq_ref[rows, cols] = dq
            dkp_ref[rows, cols] = dk[0:blk]
            dkc_ref[rows, cols] = dk[blk:2 * blk]
            dvp_ref[rows, cols] = dv[0:blk]
            dvc_ref[rows, cols] = dv[blk:2 * blk]

        _attn_loop(dil, one_pair)

    cur, prev = _attn_specs(dil)
    grid = (s // blk,) if dil == 1 else (s // (blk * dil), 4)
    return pl.pallas_call(
        body, name=name, grid=grid,
        in_specs=[cur(0), prev(0), cur(0), prev(C_AV), cur(C_AV), cur(0), cur(0), cur(512)], out_specs=[cur(0)] * 5,
        out_shape=[jax.ShapeDtypeStruct((s, 512), F32)] * 5, compiler_params=_params(*["parallel"] * len(grid)),
    )(qa, ka, ka, proj, proj, y_att, lse, dmixed)


def _attn_post(parts, proj, qg, kg, *, name):
    s = proj.shape[0]
    tm = ATTN_BLOCK
    nblk = s // tm

    def body(*refs):
        ins, (q_ref, k_ref, qg_ref, kg_ref, dq_out, dk_out, dv_out, sums_ref) = refs[:15], refs[15:]
        i = pl.program_id(0)

        @pl.when(i == 0)
        def _():
            sums_ref[...] = jnp.zeros_like(sums_ref)

        dq = jnp.zeros((tm, 512), F32)
        dk = jnp.zeros((tm, 512), F32)
        dv = jnp.zeros((tm, 512), F32)
        for g, dil in enumerate(DILATIONS):
            dq_r, dkc_r, dkp_r, dvc_r, dvp_r = ins[5 * g:5 * g + 5]
            inside = (i + dil < nblk).astype(F32)
            dq = dq + dq_r[...]
            dk = dk + dkc_r[...] + inside * dkp_r[...]
            dv = dv + dvc_r[...] + inside * dvp_r[...]
        dv_out[...] = dv.astype(BF16)
        hs = _head_sum_matrix()
        for row, (x_ref, g_ref, dy, out, post) in enumerate(((q_ref, qg_ref, dq, dq_out, 0.125), (k_ref, kg_ref, dk, dk_out, 1.0))):
            x = x_ref[...]
            rs = lax.rsqrt(_sum_right(x * x, hs) * (1.0 / 64) + EPS)
            xn = x * rs
            dy = dy * post
            sums_ref[row] += _fold8(dy * xn)
            dn = dy * g_ref[...]
            out[...] = (rs * (dn - xn * (_sum_right(dn * xn, hs) * (1.0 / 64)))).astype(BF16)

        @pl.when(i == nblk - 1)
        def _():
            _spread_total(sums_ref)

    here = pl.BlockSpec((tm, 512), lambda i: (i, 0))
    specs = []
    for dil in DILATIONS:
        later = pl.BlockSpec((tm, 512), lambda i, dil=dil: (jnp.minimum(i + dil, nblk - 1), 0))
        specs += [here, here, later, here, later]
    col = lambda at: pl.BlockSpec((tm, 512), lambda i: (i, at // 512))
    vec = pl.BlockSpec((1, 512), lambda i: (0, 0))
    return pl.pallas_call(
        body, name=name, grid=(nblk,), in_specs=specs + [col(C_AQ), col(C_AK), vec, vec],
        out_specs=[here, here, here, pl.BlockSpec((2, 8, 512), lambda i: (0, 0, 0))],
        out_shape=[jax.ShapeDtypeStruct((s, 512), BF16)] * 3 + [jax.ShapeDtypeStruct((2, 8, 512), F32)],
        compiler_params=_params("arbitrary"))(*[t for part in parts for t in part], proj, proj, qg, kg)


FFN_TM, FFN_TN = 256, 1408
HALO = 16


def _conv3(u_ref, halo_ref, w_ref, b_ref, first):
    u = u_ref[...].astype(F32)
    ext = jnp.concatenate([jnp.where(first, 0.0, halo_ref[...].astype(F32)), u], axis=0)
    u1 = pltpu.roll(ext, 1, 0)[HALO:]
    u2 = pltpu.roll(ext, 2, 0)[HALO:]
    return b_ref[...] + w_ref[0:1, :] * u2 + w_ref[1:2, :] * u1 + w_ref[2:3, :] * u, u, u1, u2


def _ffn_specs(tm, tn):
    nj = D_FF // tn
    blk = lambda half: pl.BlockSpec((tm, tn), lambda j, i: (i, j + half * nj))
    halo = lambda half: pl.BlockSpec((HALO, tn), lambda j, i: (jnp.maximum(i * (tm // HALO) - 1, 0), j + half * nj))
    wspec = lambda half: pl.BlockSpec((3, tn), lambda j, i: (0, j + half * nj))
    bspec = lambda half: pl.BlockSpec((1, tn), lambda j, i: (0, j + half * nj))
    return [blk(0), halo(0), blk(1), halo(1), wspec(0), wspec(1), bspec(0), bspec(1)]


def _conv_swiglu_fwd(u, conv_w, conv_b, *, name):
    s = u.shape[0]
    tm, tn = FFN_TM, FFN_TN

    def body(ug_ref, hg_ref, uv_ref, hv_ref, wg_ref, wv_ref, bg_ref, bv_ref, act_ref):
        first = pl.program_id(1) == 0
        cg = _conv3(ug_ref, hg_ref, wg_ref, bg_ref, first)[0]
        cv = _conv3(uv_ref, hv_ref, wv_ref, bv_ref, first)[0]
        act_ref[...] = (cg * _sigmoid(cg) * cv).astype(BF16)

    return pl.pallas_call(
        body, name=name, grid=(D_FF // tn, s // tm), in_specs=_ffn_specs(tm, tn),
        out_specs=pl.BlockSpec((tm, tn), lambda j, i: (i, j)), out_shape=jax.ShapeDtypeStruct((s, D_FF), BF16),
        compiler_params=_params("parallel", "parallel"))(u, u, u, u, conv_w, conv_w, conv_b, conv_b)


def _conv_swiglu_bwd_pre(u, conv_w, conv_b, dact, *, name):
    s = u.shape[0]
    tm, tn = FFN_TM, FFN_TN

    def body(ug_ref, hg_ref, uv_ref, hv_ref, wg_ref, wv_ref, bg_ref, bv_ref, da_ref, duc_ref, sums_ref):
        i = pl.program_id(1)

        @pl.when(i == 0)
        def _():
            sums_ref[...] = jnp.zeros_like(sums_ref)

        cg, g0, g1, g2 = _conv3(ug_ref, hg_ref, wg_ref, bg_ref, i == 0)
        cv, v0, v1, v2 = _conv3(uv_ref, hv_ref, wv_ref, bv_ref, i == 0)
        da = da_ref[...].astype(F32)
        sg = _sigmoid(cg)
        dg = da * cv * (sg * (1.0 + cg * (1.0 - sg)))
        dv = da * (cg * sg)
        duc_ref[0] = dg.astype(BF16)
        duc_ref[1] = dv.astype(BF16)
        for half, (d, taps) in enumerate(((dg, (g2, g1, g0)), (dv, (v2, v1, v0)))):
            for t, tap in enumerate(taps):
                sums_ref[half, t] += _fold8(d * tap)
            sums_ref[half, 3] += _fold8(d)

        @pl.when(i == s // tm - 1)
        def _():
            _spread_total(sums_ref)

    return pl.pallas_call(
        body, name=name, grid=(D_FF // tn, s // tm),
        in_specs=_ffn_specs(tm, tn) + [pl.BlockSpec((tm, tn), lambda j, i: (i, j))],
        out_specs=[pl.BlockSpec((2, tm, tn), lambda j, i: (0, i, j)), pl.BlockSpec((2, 4, 8, tn), lambda j, i: (0, 0, 0, j))],
        out_shape=[jax.ShapeDtypeStruct((2, s, D_FF), BF16), jax.ShapeDtypeStruct((2, 4, 8, D_FF), F32)],
        compiler_params=_params("parallel", "arbitrary"))(u, u, u, u, conv_w, conv_w, conv_b, conv_b, dact)


def _conv_bwd(duc, conv_w, *, name):
    _, s, _ = duc.shape
    tm, tn = FFN_TM, FFN_TN
    nj, ni = D_FF // tn, s // tm

    def body(d_ref, halo_ref, w_ref, du_ref):
        last = pl.program_id(2) == ni - 1
        d = d_ref[0].astype(F32)
        ext = jnp.concatenate([d, jnp.where(last, 0.0, halo_ref[0].astype(F32))], axis=0)
        n = tm + HALO
        d1 = pltpu.roll(ext, n - 1, 0)[:tm]
        d2 = pltpu.roll(ext, n - 2, 0)[:tm]
        du_ref[...] = (w_ref[2:3, :] * d + w_ref[1:2, :] * d1 + w_ref[0:1, :] * d2).astype(BF16)

    return pl.pallas_call(
        body, name=name, grid=(2, nj, ni),
        in_specs=[pl.BlockSpec((1, tm, tn), lambda g, j, i: (g, i, j)),
                  pl.BlockSpec((1, HALO, tn), lambda g, j, i: (g, jnp.minimum((i + 1) * (tm // HALO), s // HALO - 1), j)),
                  pl.BlockSpec((3, tn), lambda g, j, i: (0, g * nj + j))],
        out_specs=pl.BlockSpec((tm, tn), lambda g, j, i: (i, g * nj + j)),
        out_shape=jax.ShapeDtypeStruct((s, 2 * D_FF), BF16),
        compiler_params=_params("parallel", "parallel", "parallel"))(duc, duc, conv_w)


def _loss_head(x1, ffn, gate, target, *, name):
    s, d = x1.shape
    tm = ROW_TILE

    def body(x_ref, f_ref, g_ref, t_ref, dy_ref, df_ref, sums_ref):
        i = pl.program_id(0)

        @pl.when(i == 0)
        def _():
            sums_ref[...] = jnp.zeros_like(sums_ref)

        f = f_ref[...]
        err = x_ref[...] + g_ref[...] * f - t_ref[...]
        dy = err * (1.0 / d)
        dy_ref[...] = dy
        df_ref[...] = (g_ref[...] * dy).astype(BF16)
        sums_ref[0] += _fold8(dy * f)
        sums_ref[1] += _fold8(err * err)

        @pl.when(i == s // tm - 1)
        def _():
            _spread_total(sums_ref)

    row = pl.BlockSpec((tm, d), lambda i: (i, 0))
    return pl.pallas_call(
        body, name=name, grid=(s // tm,), in_specs=[row, row, pl.BlockSpec((1, d), lambda i: (0, 0)), row],
        out_specs=[row, row, pl.BlockSpec((2, 8, d), lambda i: (0, 0, 0))],
        out_shape=[jax.ShapeDtypeStruct((s, d), F32), jax.ShapeDtypeStruct((s, d), BF16), jax.ShapeDtypeStruct((2, 8, d), F32)],
        compiler_params=_params("arbitrary"))(x1, ffn, gate, target)


def _adamw(w, g, m, v, *, name):
    rows, cols = w.shape
    tm = next((t for t in range(ROW_TILE, 7, -8) if rows % t == 0), rows)

    def body(w_ref, g_ref, m_ref, v_ref, d_ref, mo_ref, vo_ref):
        gv = g_ref[...]
        mn = ADAM_B1 * m_ref[...] + (1.0 - ADAM_B1) * gv
        vn = ADAM_B2 * v_ref[...] + (1.0 - ADAM_B2) * (gv * gv)
        m_hat = mn / (1.0 - ADAM_B1 ** ADAM_STEP)
        v_hat = vn / (1.0 - ADAM_B2 ** ADAM_STEP)
        d_ref[...] = -ADAM_LR * (m_hat / (jnp.sqrt(v_hat) + ADAM_EPS) + ADAM_WD * w_ref[...])
        mo_ref[...] = mn
        vo_ref[...] = vn

    blk = pl.BlockSpec((tm, cols), lambda i: (i, 0))
    return pl.pallas_call(
        body, name=name, grid=(rows // tm,), in_specs=[blk] * 4, out_specs=[blk] * 3,
        out_shape=[jax.ShapeDtypeStruct((rows, cols), F32)] * 3, compiler_params=_params("parallel"))(w, g, m, v)


def _colsum(t):
    return t[..., 0, :]


def _in_proj_layout(w_in):
    pad = jnp.zeros((w_in.shape[0], PROJ_W - C_LR - GLA_GATE_RANK), w_in.dtype)
    return jnp.concatenate([w_in[:, :1536], w_in[:, 1552:], w_in[:, 1536:1552], pad], axis=1)


def _in_proj_grad_layout(g):
    return jnp.concatenate([g[:, :1536], g[:, C_LR:C_LR + GLA_GATE_RANK], g[:, 1536:C_LR]], axis=1)


def _gate_layout(gla_w_gate):
    return jnp.pad(gla_w_gate, ((0, HEAD_LANES - GLA_GATE_RANK), (0, 0))).astype(BF16)


def _local_step(x, target, mod, wi, wo, ffn_weights, ffn_grads_ready, conv_w, conv_b, wg, bg, gn, qg, kg, n1g, n2g):
    d = D_MODEL
    sh1, sc1, g1, sh2, sc2, g2 = [mod[:, i * d:(i + 1) * d] for i in range(6)]
    qg8, kg8 = jnp.tile(qg, (1, 8)), jnp.tile(kg, (1, 8))

    _, h1, h1_t = _norm_mod_fwd(x, None, None, n1g, sc1, sh1, name="norm1_fwd")
    proj = _mm(h1, wi, tm=1024, tn=PROJ_W, tk=d, name="in_proj")
    o_raw, y_gla, states = _gla_fwd(proj, wg, bg, gn, name="gla_fwd")
    qa, ka = _attn_prep(proj, qg8, kg8, name="attn_prep")
    branches = [_dil_attn_fwd(qa, ka, proj, dil, name=f"attn_fwd_d{dil}") for dil in DILATIONS]
    mixed, y_att, lse = _attn_merge(branches, y_gla, name="attn_merge")
    attn_out = _mm(mixed, wo, tm=1024, tn=d, tk=d, name="out_proj")
    x1, h2, h2_t = _norm_mod_fwd(x, attn_out, g1, n2g, sc2, sh2, name="norm2_fwd")
    wup, wdown = ffn_weights(h2)
    u = _mm(h2, wup, out_dtype=BF16, tm=1024, tn=D_FF, tk=d, name="up_proj")
    act = _conv_swiglu_fwd(u, conv_w, conv_b, name="conv_swiglu_fwd")
    ffn = _mm(act, wdown, tm=1024, tn=d, tk=D_FF, name="down_proj")
    dy, dffn, head_sums = _loss_head(x1, ffn, g2, target, name="loss_head")

    dact = _mm(dffn, wdown, tb=True, out_dtype=BF16, tm=1024, tn=D_FF, tk=d, name="down_proj_dx")
    g_wdown, g_wdown_b = _mm(act, dffn, ta=True, tm=1408, tn=d, tk=1024, also_bf16=True, name="down_proj_dw")
    duc, conv_sums = _conv_swiglu_bwd_pre(u, conv_w, conv_b, dact, name="conv_swiglu_bwd")
    du = _conv_bwd(duc, conv_w, name="conv_bwd")
    dh2 = _mm(du, wup, tb=True, tm=1024, tn=d, tk=D_FF, name="up_proj_dx")
    g_wup, g_wup_b = _mm(h2_t, du, tm=d, tn=1408, tk=1024, shard_cols=True, also_bf16=True, name="up_proj_dw")
    token = ffn_grads_ready(g_wup_b, g_wdown_b)
    g1_late = g1 if token is None else g1 + token[0:1, 0:1]
    dx1, dao, n2_sums = _norm_mod_bwd(x1, dh2, dy, n2g, sc2, attn_out, g1_late, name="norm2_bwd")

    dmixed = _mm(dao, wo, tb=True, tm=1024, tn=d, tk=d, name="out_proj_dx")
    g_wo = _mm(mixed, dao, ta=True, tm=d, tn=d, tk=1024, name="out_proj_dw")
    dgq, dgk, dgv, dgr, dlr, g_wg, gla_sums = _gla_bwd(proj, wg, bg, gn, o_raw, states, dmixed, name="gla_bwd")
    parts = [_dil_attn_bwd(qa, ka, proj, y_att, lse, dmixed, dil, name=f"attn_bwd_d{dil}") for dil in DILATIONS]
    daq, dak, dav, qk_sums = _attn_post(parts, proj, qg8, kg8, name="attn_post")
    dproj = jnp.concatenate([dgq, dgk, dgv, dgr, daq, dak, dav, dlr], axis=1)
    dh1 = _mm(dproj, wi, tb=True, tm=1024, tn=d, tk=PROJ_W, name="in_proj_dx")
    g_wi = _mm(h1_t, dproj, tm=512, tn=PROJ_W, tk=1024, name="in_proj_dw")
    grad_x, _, n1_sums = _norm_mod_bwd(x, dh1, dx1, n1g, sc1, None, None, name="norm1_bwd")

    n1, n2, hs, cs = _colsum(n1_sums), _colsum(n2_sums), _colsum(head_sums), _colsum(conv_sums)
    gs, qs = _colsum(gla_sums), _colsum(qk_sums)
    dmod = jnp.concatenate([n1[1], n1[0] * n1g[0], n2[2], n2[1], n2[0] * n2g[0], hs[0]])
    small = dict(
        dmod=dmod,
        norm1_g=n1[0] * (1.0 + sc1[0]), norm2_g=n2[0] * (1.0 + sc2[0]),
        gla_w_gate=g_wg[:GLA_GATE_RANK], gla_b_gate=gs[0, :256], gla_norm_g=gs[1].reshape(4, 128).sum(axis=0),
        q_norm_g=qs[0].reshape(8, 64).sum(axis=0), k_norm_g=qs[1].reshape(8, 64).sum(axis=0),
        conv_w=jnp.concatenate([cs[0, :3], cs[1, :3]], axis=1), conv_b=jnp.concatenate([cs[0, 3], cs[1, 3]]),
    )
    return head_sums[1], grad_x, (g_wi, g_wo, g_wup, g_wdown), small


N_DEV, N_CHIP = 8, 4
ANY = pl.BlockSpec(memory_space=pl.ANY)
VMEM_SPEC = pl.BlockSpec(memory_space=pltpu.VMEM)


def _place():
    x, y, c = lax.axis_index("x"), lax.axis_index("y"), lax.axis_index("c")
    other_chips = [(1 - x, y), (x, 1 - y), (1 - x, 1 - y)]
    return x, y, c, (x, y, 1 - c), other_chips


def _all_gather_small(v, *, name):
    m, n = v.shape

    def body(v_ref, out_ref, send_sems, recv_sems, local_sem):
        x, y, c, sibling, chips = _place()
        me = (x, y, c)

        def rows(px, py, pc):
            return out_ref.at[pl.ds((4 * px + 2 * py + pc) * m, m), :]

        def copy(k, block, to, src=None):
            return pltpu.make_async_remote_copy(
                src_ref=rows(*block) if src is None else src, dst_ref=rows(*block), send_sem=send_sems.at[k],
                recv_sem=recv_sems.at[k], device_id=to, device_id_type=MESH)

        mine = pltpu.make_async_copy(v_ref, rows(*me), local_sem)
        mine.start()
        first = [copy(0, me, sibling, src=v_ref)]
        first += [copy(1 + j, me, (*chip, c), src=v_ref) for j, chip in enumerate(chips)]
        for cp in first:
            cp.start()
        passed = [copy(4 + j, (*chip, c), sibling) for j, chip in enumerate(chips)]
        for j, chip in enumerate(chips):
            copy(1 + j, (*chip, c), me).wait_recv()
            passed[j].start()
        copy(0, sibling, me).wait_recv()
        for j, chip in enumerate(chips):
            copy(4 + j, (*chip, 1 - c), me).wait_recv()
        for cp in first + passed:
            cp.wait_send()
        mine.wait()

    return pl.pallas_call(
        body, name=name, out_shape=jax.ShapeDtypeStruct((N_DEV * m, n), v.dtype), in_specs=[VMEM_SPEC], out_specs=VMEM_SPEC,
        scratch_shapes=[pltpu.SemaphoreType.DMA((7,)), pltpu.SemaphoreType.DMA((7,)), pltpu.SemaphoreType.DMA],
    )(v)


def _gather_weight_shards(shards, *, name):
    nw = len(shards)

    def body(*refs):
        srcs, outs, (send_sems, recv_sems) = refs[:nw], refs[nw:2 * nw], refs[2 * nw:]
        x, y, c, sibling, chips = _place()
        index = lambda chip: 2 * chip[0] + chip[1]

        def copy(w, k, src, dst, to):
            return pltpu.make_async_remote_copy(src_ref=src, dst_ref=dst, send_sem=send_sems.at[6 * w + k],
                                                recv_sem=recv_sems.at[6 * w + k], device_id=to, device_id_type=MESH)

        sent = []
        for w, (src_ref, out_ref) in enumerate(zip(srcs, outs)):
            for k, chip in enumerate(chips):
                sent.append(copy(w, k, src_ref.at[c], out_ref.at[2 * x + y, c], (*chip, c)))
                sent[-1].start()
        for w, out_ref in enumerate(outs):
            for k, chip in enumerate(chips):
                landed = out_ref.at[index(chip), c]
                copy(w, k, landed, landed, (*chip, c)).wait_recv()
                sent.append(copy(w, 3 + k, landed, landed, sibling))
                sent[-1].start()
        for w, out_ref in enumerate(outs):
            for k, chip in enumerate(chips):
                passed_on = out_ref.at[index(chip), 1 - c]
                copy(w, 3 + k, passed_on, passed_on, sibling).wait_recv()
        for cp in sent:
            cp.wait_send()

    return pl.pallas_call(
        body, name=name, out_shape=[jax.ShapeDtypeStruct((N_CHIP, *s.shape), s.dtype) for s in shards],
        in_specs=[ANY] * nw, out_specs=[ANY] * nw,
        scratch_shapes=[pltpu.SemaphoreType.DMA((6 * nw,)), pltpu.SemaphoreType.DMA((6 * nw,))],
    )(*shards)


HBM_SPEC = pl.BlockSpec(memory_space=pltpu.HBM)
SEM_SPEC = pl.BlockSpec(memory_space=pltpu.SEMAPHORE)
DATAFLOW_EFFECT = pltpu.SideEffectType.DATAFLOW_SIDE_EFFECTING


def _late_copies(srcs, lands, send_sems, recv_sems):
    x, y, c, _, chips = _place()
    return [pltpu.make_async_remote_copy(
        src_ref=src.at[c], dst_ref=land.at[2 * x + y, c], send_sem=send_sems.at[6 * w + 2 * r + core],
        recv_sem=recv_sems.at[6 * w + 2 * r + c], device_id=(*chip, core), device_id_type=MESH)
        for w, (src, land) in enumerate(zip(srcs, lands)) for r, chip in enumerate(chips) for core in range(2)]


def _gather_late_start(own, after, *, name):
    nw = len(own)

    def body(*refs):
        srcs, lands, send_sems, recv_sems, token = refs[:nw], refs[nw:2 * nw], refs[2 * nw + 1], refs[2 * nw + 2], refs[-1]
        for cp in _late_copies(srcs, lands, send_sems, recv_sems):
            cp.start()
        token[...] = jnp.zeros_like(token)

    lands = [pltpu.with_memory_space_constraint(lax.empty((N_CHIP, *s.shape), s.dtype), pltpu.HBM) for s in own]
    own = [pltpu.with_memory_space_constraint(s, pltpu.HBM) for s in own]
    out = pl.pallas_call(
        body, name=name,
        out_shape=(pltpu.SemaphoreType.DMA((6 * nw,)), pltpu.SemaphoreType.DMA((6 * nw,)),
                   *[pltpu.HBM(s.shape, s.dtype) for s in own], *[pltpu.HBM(s.shape, s.dtype) for s in lands],
                   jax.ShapeDtypeStruct((8, 128), F32)),
        in_specs=[HBM_SPEC] * (2 * nw) + [ANY], out_specs=(SEM_SPEC, SEM_SPEC, *[HBM_SPEC] * (2 * nw), VMEM_SPEC),
        input_output_aliases={i: 2 + i for i in range(2 * nw)},
        compiler_params=pltpu.CompilerParams(has_side_effects=DATAFLOW_EFFECT))(*own, *lands, after)
    return out[0], out[1], out[2:2 + nw], out[2 + nw:2 + 2 * nw], out[-1]


def _gather_late_wait(send_sems, recv_sems, own, lands, after, *, name):
    nw = len(own)

    def body(*refs):
        srcs, lands_in, send_sems, recv_sems = refs[:nw], refs[nw:2 * nw], refs[2 * nw], refs[2 * nw + 1]
        x, y, c, _, chips = _place()
        for cp in _late_copies(srcs, lands_in, send_sems, recv_sems):
            cp.wait_send()
        for w, (src, land) in enumerate(zip(srcs, lands_in)):
            for r, chip in enumerate(chips):
                for core in range(2):
                    pltpu.make_async_remote_copy(
                        src_ref=src.at[c], dst_ref=land.at[2 * chip[0] + chip[1], core], send_sem=send_sems.at[6 * w + 2 * r + core],
                        recv_sem=recv_sems.at[6 * w + 2 * r + core], device_id=(*chip, core), device_id_type=MESH).wait_recv()

    out = pl.pallas_call(
        body, name=name, out_shape=(*[pltpu.HBM(s.shape, s.dtype) for s in own], *[pltpu.HBM(s.shape, s.dtype) for s in lands]),
        in_specs=[HBM_SPEC] * (2 * nw) + [SEM_SPEC, SEM_SPEC, ANY], out_specs=tuple([HBM_SPEC] * (2 * nw)),
        input_output_aliases={i: i for i in range(2 * nw)},
        compiler_params=pltpu.CompilerParams(has_side_effects=DATAFLOW_EFFECT))(*own, *lands, send_sems, recv_sems, after)
    return out[:nw], out[nw:]


def _direct_reduce_copies(srcs, lands, send_sems, recv_sems):
    x, y, c, _, _ = _place()
    cps = []
    for w, (src, land) in enumerate(zip(srcs, lands)):
        for rel in range(1, N_DEV):
            tx, ty, tc = (1 - x if rel & 4 else x), (1 - y if rel & 2 else y), (1 - c if rel & 1 else c)
            cps.append(pltpu.make_async_remote_copy(
                src_ref=src.at[2 * tx + ty, tc], dst_ref=land.at[rel - 1], send_sem=send_sems.at[7 * w + rel - 1],
                recv_sem=recv_sems.at[7 * w + rel - 1], device_id=(tx, ty, tc), device_id_type=MESH))
    return cps


def _direct_reduce_start(grads, *, name):
    nw = len(grads)

    def body(*refs):
        srcs, lands, send_sems, recv_sems, token = refs[:nw], refs[nw:2 * nw], refs[2 * nw], refs[2 * nw + 1], refs[-1]
        for cp in _direct_reduce_copies(srcs, lands, send_sems, recv_sems):
            cp.start()
        token[...] = jnp.zeros_like(token)

    lands = [pltpu.with_memory_space_constraint(lax.empty((N_DEV - 1, *g.shape[2:]), g.dtype), pltpu.HBM) for g in grads]
    grads = [pltpu.with_memory_space_constraint(g, pltpu.HBM) for g in grads]
    out = pl.pallas_call(
        body, name=name,
        out_shape=(pltpu.SemaphoreType.DMA((7 * nw,)), pltpu.SemaphoreType.DMA((7 * nw,)),
                   *[pltpu.HBM(g.shape, g.dtype) for g in grads], *[pltpu.HBM(t.shape, t.dtype) for t in lands],
                   jax.ShapeDtypeStruct((8, 128), F32)),
        in_specs=[HBM_SPEC] * (2 * nw), out_specs=(SEM_SPEC, SEM_SPEC, *[HBM_SPEC] * (2 * nw), VMEM_SPEC),
        input_output_aliases={i: 2 + i for i in range(2 * nw)},
        compiler_params=pltpu.CompilerParams(has_side_effects=DATAFLOW_EFFECT))(*grads, *lands)
    return out[0], out[1], out[2:2 + nw], out[2 + nw:2 + 2 * nw], out[-1]


def _direct_reduce_wait(send_sems, recv_sems, grads, lands, after, *, name):
    nw = len(grads)

    def body(*refs):
        srcs, lands_in, send_sems, recv_sems = refs[:nw], refs[nw:2 * nw], refs[2 * nw], refs[2 * nw + 1]
        cps = _direct_reduce_copies(srcs, lands_in, send_sems, recv_sems)
        for cp in cps:
            cp.wait_send()
        for cp in cps:
            cp.wait_recv()

    out = pl.pallas_call(
        body, name=name, out_shape=(*[pltpu.HBM(g.shape, g.dtype) for g in grads], *[pltpu.HBM(t.shape, t.dtype) for t in lands]),
        in_specs=[HBM_SPEC] * (2 * nw) + [SEM_SPEC, SEM_SPEC, ANY], out_specs=tuple([HBM_SPEC] * (2 * nw)),
        input_output_aliases={i: i for i in range(2 * nw)},
        compiler_params=pltpu.CompilerParams(has_side_effects=DATAFLOW_EFFECT))(*grads, *lands, send_sems, recv_sems, after)
    return out[nw:]


def _direct_reduce_add(grad, landed, chip, core, *, name):
    _, r, n = grad.shape
    half = r // 2
    tr = _row_tile(half)
    nb = half // tr

    def body(chip_ref, core_ref, g_ref, t_ref, o_ref):
        acc = g_ref[0]
        for k in range(N_DEV - 1):
            acc = acc + t_ref[k].astype(F32)
        o_ref[...] = acc

    return pl.pallas_call(
        body, name=name,
        grid_spec=pltpu.PrefetchScalarGridSpec(
            num_scalar_prefetch=2, grid=(nb,),
            in_specs=[pl.BlockSpec((1, tr, n), lambda i, chip_ref, core_ref: (chip_ref[0], core_ref[0] * nb + i, 0)),
                      pl.BlockSpec((N_DEV - 1, tr, n), lambda i, chip_ref, core_ref: (0, i, 0))],
            out_specs=pl.BlockSpec((tr, n), lambda i, chip_ref, core_ref: (i, 0))),
        out_shape=jax.ShapeDtypeStruct((half, n), F32), compiler_params=_params("parallel"))(chip, core, grad, landed)


def _pair_exchange_halves(grads, *, name):
    nw = len(grads)

    def body(*refs):
        srcs, outs, (send_sems, recv_sems) = refs[:nw], refs[nw:2 * nw], refs[2 * nw:]
        _, _, c, sibling, _ = _place()
        cps = []
        for w, (src_ref, out_ref) in enumerate(zip(srcs, outs)):
            cps.append(pltpu.make_async_remote_copy(
                src_ref=src_ref.at[:, 1 - c], dst_ref=out_ref, send_sem=send_sems.at[w],
                recv_sem=recv_sems.at[w], device_id=sibling, device_id_type=MESH))
            cps[-1].start()
        for cp in cps:
            cp.wait()

    return pl.pallas_call(
        body, name=name, out_shape=[jax.ShapeDtypeStruct((N_CHIP, *g.shape[2:]), g.dtype) for g in grads],
        in_specs=[ANY] * nw, out_specs=[ANY] * nw,
        scratch_shapes=[pltpu.SemaphoreType.DMA((nw,)), pltpu.SemaphoreType.DMA((nw,))])(*grads)


def _chip_scatter(pairs, *, name):
    nw = len(pairs)

    def body(*refs):
        srcs, outs, (send_sems, recv_sems) = refs[:nw], refs[nw:2 * nw], refs[2 * nw:]
        _, _, c, _, chips = _place()
        cps = []
        for w, (p_ref, out_ref) in enumerate(zip(srcs, outs)):
            for k, chip in enumerate(chips):
                cps.append(pltpu.make_async_remote_copy(
                    src_ref=p_ref.at[2 * chip[0] + chip[1]], dst_ref=out_ref.at[k], send_sem=send_sems.at[3 * w + k],
                    recv_sem=recv_sems.at[3 * w + k], device_id=(*chip, c), device_id_type=MESH))
                cps[-1].start()
        for cp in cps:
            cp.wait()

    return pl.pallas_call(
        body, name=name, out_shape=[jax.ShapeDtypeStruct((3, *p.shape[1:]), p.dtype) for p in pairs],
        in_specs=[ANY] * nw, out_specs=[ANY] * nw,
        scratch_shapes=[pltpu.SemaphoreType.DMA((3 * nw,)), pltpu.SemaphoreType.DMA((3 * nw,))])(*pairs)


def _share_halves(halves, *, name):
    nw = len(halves)

    def body(*refs):
        srcs, outs, (send_sems, recv_sems) = refs[:nw], refs[nw:2 * nw], refs[2 * nw:]
        _, _, _, sibling, _ = _place()
        cps = [pltpu.make_async_remote_copy(src_ref=src_ref, dst_ref=out_ref, send_sem=send_sems.at[w], recv_sem=recv_sems.at[w],
                                            device_id=sibling, device_id_type=MESH)
               for w, (src_ref, out_ref) in enumerate(zip(srcs, outs))]
        for cp in cps:
            cp.start()
        for cp in cps:
            cp.wait()

    return pl.pallas_call(
        body, name=name, out_shape=[jax.ShapeDtypeStruct(h.shape, h.dtype) for h in halves],
        in_specs=[ANY] * nw, out_specs=[ANY] * nw,
        scratch_shapes=[pltpu.SemaphoreType.DMA((nw,)), pltpu.SemaphoreType.DMA((nw,))])(*halves)


def _row_tile(rows, limit=256):
    return next(t for t in range(limit, 15, -16) if rows % t == 0)


def _pair_add(grad, got, core, *, name):
    _, r, n = grad.shape
    half = r // 2
    tr = _row_tile(half)
    nb = half // tr

    def body(core_ref, g_ref, t_ref, f_ref, b_ref):
        acc = g_ref[...] + t_ref[...]
        f_ref[...] = acc
        b_ref[...] = acc.astype(BF16)

    blk = pl.BlockSpec((1, tr, n), lambda j, i, core_ref: (j, i, 0))
    mine = pl.BlockSpec((1, tr, n), lambda j, i, core_ref: (j, core_ref[0] * nb + i, 0))
    return pl.pallas_call(
        body, name=name,
        grid_spec=pltpu.PrefetchScalarGridSpec(num_scalar_prefetch=1, grid=(N_CHIP, nb), in_specs=[mine, blk], out_specs=[blk, blk]),
        out_shape=[jax.ShapeDtypeStruct((N_CHIP, half, n), F32), jax.ShapeDtypeStruct((N_CHIP, half, n), BF16)],
        compiler_params=_params("parallel", "parallel"))(core, grad, got)


def _chip_add(pair, theirs, chip, *, name):
    _, h, n = pair.shape
    tr = _row_tile(h)

    def body(chip_ref, p_ref, t_ref, o_ref):
        o_ref[...] = ((p_ref[0] + t_ref[0].astype(F32)) + t_ref[1].astype(F32)) + t_ref[2].astype(F32)

    return pl.pallas_call(
        body, name=name,
        grid_spec=pltpu.PrefetchScalarGridSpec(
            num_scalar_prefetch=1, grid=(h // tr,),
            in_specs=[pl.BlockSpec((1, tr, n), lambda i, chip_ref: (chip_ref[0], i, 0)),
                      pl.BlockSpec((3, tr, n), lambda i, chip_ref: (0, i, 0))],
            out_specs=pl.BlockSpec((tr, n), lambda i, chip_ref: (i, 0))),
        out_shape=jax.ShapeDtypeStruct((h, n), F32), compiler_params=_params("parallel"))(chip, pair, theirs)


def _sum_devices(gathered, *, name):
    _, m, n = gathered.shape

    def body(g_ref, tot_ref, loss_ref):
        tot = g_ref[0]
        for dev in range(1, N_DEV):
            tot = tot + g_ref[dev]
        tot_ref[...] = tot
        loss_ref[...] = jnp.full((8, n), (0.5 / D_MODEL) * jnp.sum(tot[0:8]), F32)

    return pl.pallas_call(body, name=name, in_specs=[VMEM_SPEC], out_specs=[VMEM_SPEC, VMEM_SPEC],
                          out_shape=[jax.ShapeDtypeStruct((m, n), F32), jax.ShapeDtypeStruct((8, n), F32)])(gathered)


def _ada_mod(cond_all, w_ada_shard, *, name):
    tn = 512

    def body(a_ref, b_ref, o_ref):
        o_ref[...] = _nn(a_ref[...], b_ref[...], precision=HIGHEST)

    return pl.pallas_call(
        body, name=name, grid=(w_ada_shard.shape[1] // tn,),
        in_specs=[pl.BlockSpec(cond_all.shape, lambda j: (0, 0)), pl.BlockSpec((D_MODEL, tn), lambda j: (0, j))],
        out_specs=pl.BlockSpec((N_DEV, tn), lambda j: (0, j)),
        out_shape=jax.ShapeDtypeStruct((N_DEV, w_ada_shard.shape[1]), F32), compiler_params=_params("parallel"))(cond_all, w_ada_shard)


def _ada_grad(cond_all, dmod_cols, *, name):
    tm = 256

    def body(a_ref, b_ref, o_ref):
        o_ref[...] = lax.dot_general(a_ref[...], b_ref[...], (((0,), (0,)), ((), ())), precision=HIGHEST,
                                     preferred_element_type=F32)

    return pl.pallas_call(
        body, name=name, grid=(D_MODEL // tm,),
        in_specs=[pl.BlockSpec((N_DEV, tm), lambda i: (0, i)), pl.BlockSpec(dmod_cols.shape, lambda i: (0, 0))],
        out_specs=pl.BlockSpec((tm, dmod_cols.shape[1]), lambda i: (i, 0)),
        out_shape=jax.ShapeDtypeStruct((D_MODEL, dmod_cols.shape[1]), F32), compiler_params=_params("parallel"))(cond_all, dmod_cols)


def _silu_rows(c8, *, name):
    def body(c_ref, o_ref):
        cv = c_ref[...]
        o_ref[...] = cv * _sigmoid(cv)

    return pl.pallas_call(body, name=name, in_specs=[VMEM_SPEC], out_specs=VMEM_SPEC,
                          out_shape=jax.ShapeDtypeStruct(c8.shape, F32))(c8)


def _rows128(t, rows=None):
    flat = t.reshape(-1, 128)
    return flat if rows is None else jnp.pad(flat, ((0, rows - flat.shape[0]), (0, 0)))


def _from_col_shards(shards, r, n):
    return shards.reshape(N_CHIP, r, n).transpose(1, 0, 2).reshape(r, N_CHIP * n)


def kernel(x, c, w_ada, b_ada, norm1_g, w_in, gla_w_gate, gla_b_gate, gla_norm_g, q_norm_g, k_norm_g, w_out, norm2_g, w_up, conv_w, conv_b, w_down, loss_target, m_w_ada, m_b_ada, m_norm1_g, m_w_in, m_gla_w_gate, m_gla_b_gate, m_gla_norm_g, m_q_norm_g, m_k_norm_g, m_w_out, m_norm2_g, m_w_up, m_conv_w, m_conv_b, m_w_down, v_w_ada, v_b_ada, v_norm1_g, v_w_in, v_gla_w_gate, v_gla_b_gate, v_gla_norm_g, v_q_norm_g, v_k_norm_g, v_w_out, v_norm2_g, v_w_up, v_conv_w, v_conv_b, v_w_down):
    d = D_MODEL
    ax, ay, ac = lax.axis_index("x"), lax.axis_index("y"), lax.axis_index("c")
    chip, dev = 2 * ax + ay, 4 * ax + 2 * ay + ac

    cond = _silu_rows(jnp.broadcast_to(c, (8, d)), name="cond_silu")[0:1]
    small_in = jnp.concatenate([_rows128(cond), _rows128(conv_w[0]), _rows128(gla_w_gate[0])], axis=0)
    small_in = _rows128(small_in, 56)
    got = _all_gather_small(small_in, name="gather_small").reshape(N_DEV, 56, 128)
    cond_all = got[:, 0:8].reshape(N_DEV, d)
    conv_w_full = _from_col_shards(got[0::2, 8:41].reshape(N_CHIP, 3 * 1408 // 128, 128), 3, 1408)
    gate_full = _from_col_shards(got[0::2, 41:49].reshape(N_CHIP, 16 * 64 // 128, 128), GLA_GATE_RANK, 64)
    mod_part = _ada_mod(cond_all, w_ada[0], name="ada_mod")
    mod_got = _all_gather_small(_rows128(mod_part), name="gather_mod").reshape(N_DEV, N_DEV, 1536)
    mod_all = mod_got[0::2].transpose(1, 0, 2).reshape(N_DEV, 6 * d) + b_ada
    mod = lax.dynamic_slice_in_dim(mod_all, dev, 1, axis=0)

    own = [w[0].astype(BF16).reshape(2, w.shape[1] // 2, w.shape[2]) for w in (w_in, w_out, w_up, w_down)]
    with_own = lambda got, mine: [lax.dynamic_update_index_in_dim(t, o, chip, 0) for t, o in zip(got, mine)]
    got_in, got_out = with_own(_gather_weight_shards(own[:2], name="gather_weights"), own[:2])
    w_in_full = got_in.reshape(N_CHIP, d, 772).transpose(1, 0, 2).reshape(d, N_CHIP * 772)
    w_out_full = got_out.reshape(d, d)
    exchanged = mod_all[0:1, 0:1] + got_in[0, 0, 0:1, 0:1].astype(F32)
    send_sems, recv_sems, own_thru, lands, token = _gather_late_start(own[2:], exchanged, name="gather_late_start")
    mod = mod + token[0:1, 0:1]

    def ffn_weights(after):
        mine, landed = _gather_late_wait(send_sems, recv_sems, own_thru, lands, after, name="gather_late_wait")
        got_up, got_down = with_own(landed, mine)
        return got_up.reshape(N_CHIP, d, 1408).transpose(1, 0, 2).reshape(d, 2 * D_FF), got_down.reshape(D_FF, d)

    late_reduce = []

    def ffn_grads_ready(g_wup_b, g_wdown_b):
        halves_of = lambda g: g.reshape(N_CHIP, 2, g.shape[-2] // 2, g.shape[-1])
        late_reduce.extend(_direct_reduce_start([halves_of(g_wup_b), halves_of(g_wdown_b.reshape(N_CHIP, D_FF // N_CHIP, d))],
                                                name="reduce_late_start"))
        return late_reduce[4]

    err2, grad_x, (g_wi, g_wo, g_wup, g_wdown), small = _local_step(
        x[0], loss_target[0], mod, _in_proj_layout(w_in_full), w_out_full, ffn_weights, ffn_grads_ready, conv_w_full, conv_b,
        _gate_layout(gate_full), gla_b_gate, gla_norm_g, q_norm_g, k_norm_g, norm1_g, norm2_g)

    pieces = [err2[0], small["dmod"], small["norm1_g"], small["norm2_g"], small["gla_w_gate"].reshape(-1), small["gla_b_gate"],
              small["gla_norm_g"], small["q_norm_g"], small["k_norm_g"], small["conv_w"].reshape(-1), small["conv_b"]]
    sizes = [p.shape[0] for p in pieces]
    at = [sum(sizes[:i]) for i in range(len(sizes) + 1)]
    vec = _rows128(jnp.concatenate(pieces), 288)
    got = _all_gather_small(vec, name="gather_grads").reshape(N_DEV, 288, 128)
    total, loss8 = _sum_devices(got, name="sum_devices")
    total = total.reshape(-1)
    seg = lambda i: total[at[i]:at[i + 1]]
    dmod_all = got.reshape(N_DEV, -1)[:, at[1]:at[2]]
    g_small = dict(
        b_ada=seg(1)[None], norm1_g=seg(2)[None], norm2_g=seg(3)[None],
        gla_w_gate=lax.dynamic_slice_in_dim(seg(4).reshape(GLA_GATE_RANK, 256), chip * 64, 64, axis=1),
        gla_b_gate=seg(5)[None], gla_norm_g=seg(6)[None], q_norm_g=seg(7)[None], k_norm_g=seg(8)[None],
        conv_w=lax.dynamic_slice_in_dim(seg(9).reshape(3, 2 * D_FF), chip * 1408, 1408, axis=1), conv_b=seg(10)[None])
    dmod_cols = lax.dynamic_slice_in_dim(dmod_all.reshape(N_DEV, 6 * d), chip * 1536, 1536, axis=1)
    g_w_ada = _ada_grad(cond_all, dmod_cols, name="ada_grad")

    tags = ("w_in", "w_out")
    g_parts = [_in_proj_grad_layout(g_wi).reshape(d, N_CHIP, 772).transpose(1, 0, 2), g_wo.reshape(N_CHIP, d // N_CHIP, d)]
    core_id, chip_id = jnp.reshape(ac, (1,)).astype(jnp.int32), jnp.reshape(chip, (1,)).astype(jnp.int32)
    got = _pair_exchange_halves([g.reshape(N_CHIP, 2, g.shape[1] // 2, g.shape[2]) for g in g_parts], name="reduce_pair")
    pairs = [_pair_add(g, t, core_id, name=f"reduce_pair_add_{tag}") for g, t, tag in zip(g_parts, got, tags)]
    theirs = _chip_scatter([pb for _, pb in pairs], name="reduce_chips")
    summed = [_chip_add(pf, t, chip_id, name=f"reduce_chips_add_{tag}") for (pf, _), t, tag in zip(pairs, theirs, tags)]
    landed = _direct_reduce_wait(*late_reduce[:4], grad_x, name="reduce_late_wait")
    summed += [_direct_reduce_add(g, t, chip_id, core_id, name=f"reduce_late_add_{tag}")
               for g, t, tag in zip((g_wup, g_wdown.reshape(N_CHIP, D_FF // N_CHIP, d)), landed, ("w_up", "w_down"))]
    others = _share_halves(summed, name="share_pair")
    g_big = [jnp.concatenate([jnp.where(ac == 0, mine, other), jnp.where(ac == 0, other, mine)], axis=0)
             for mine, other in zip(summed, others)]

    grads = dict(w_ada=g_w_ada, w_in=g_big[0], w_out=g_big[1], w_up=g_big[2], w_down=g_big[3], **g_small)
    names = ["w_ada", "b_ada", "norm1_g", "w_in", "gla_w_gate", "gla_b_gate", "gla_norm_g", "q_norm_g", "k_norm_g", "w_out",
             "norm2_g", "w_up", "conv_w", "conv_b", "w_down"]
    ws = dict(w_ada=w_ada, b_ada=b_ada, norm1_g=norm1_g, w_in=w_in, gla_w_gate=gla_w_gate, gla_b_gate=gla_b_gate,
              gla_norm_g=gla_norm_g, q_norm_g=q_norm_g, k_norm_g=k_norm_g, w_out=w_out, norm2_g=norm2_g, w_up=w_up,
              conv_w=conv_w, conv_b=conv_b, w_down=w_down)
    ms = dict(w_ada=m_w_ada, b_ada=m_b_ada, norm1_g=m_norm1_g, w_in=m_w_in, gla_w_gate=m_gla_w_gate, gla_b_gate=m_gla_b_gate,
              gla_norm_g=m_gla_norm_g, q_norm_g=m_q_norm_g, k_norm_g=m_k_norm_g, w_out=m_w_out, norm2_g=m_norm2_g, w_up=m_w_up,
              conv_w=m_conv_w, conv_b=m_conv_b, w_down=m_w_down)
    vs = dict(w_ada=v_w_ada, b_ada=v_b_ada, norm1_g=v_norm1_g, w_in=v_w_in, gla_w_gate=v_gla_w_gate, gla_b_gate=v_gla_b_gate,
              gla_norm_g=v_gla_norm_g, q_norm_g=v_q_norm_g, k_norm_g=v_k_norm_g, w_out=v_w_out, norm2_g=v_norm2_g, w_up=v_w_up,
              conv_w=v_conv_w, conv_b=v_conv_b, w_down=v_w_down)
    g_out, d_out, m_out, v_out = [], [], [], []
    for nm in names:
        w2 = ws[nm].reshape(ws[nm].shape[-2:])
        g2 = grads[nm].reshape(w2.shape)
        dl, mn, vn = _adamw(w2, g2, ms[nm].reshape(w2.shape), vs[nm].reshape(w2.shape), name=f"adamw_{nm}")
        shape = ws[nm].shape
        g_out.append(g2.reshape(shape))
        d_out.append(dl.reshape(shape))
        m_out.append(mn.reshape(shape))
        v_out.append(vn.reshape(shape))
    return (loss8[0, 0], grad_x[None], *g_out, *d_out, *m_out, *v_out)
```

```python
import functools

import jax
import jax.numpy as jnp
from jax import lax
from jax.experimental import pallas as pl
from jax.experimental.pallas import tpu as pltpu

F32, BF16 = jnp.float32, jnp.bfloat16
HIGHEST = lax.Precision.HIGHEST
MESH = pl.DeviceIdType.MESH

D_MODEL = 1024
GLA_CHUNK = 64
GLA_GATE_TAU = 16.0
GLA_GATE_RANK = 16
HEAD_LANES = 128
ATTN_BLOCK = 128
DILATIONS = (1, 4, 16)
ALIBI_SLOPES = tuple(2.0 ** (-(h + 1)) for h in range(8))
D_FF = 2816
EPS = 1e-6
C_GQ, C_GK, C_GV, C_GR, C_AQ, C_AK, C_AV, C_LR, PROJ_W = 0, 256, 512, 1024, 1536, 2048, 2560, 3072, 3200
ADAM_LR, ADAM_B1, ADAM_B2, ADAM_EPS, ADAM_WD, ADAM_STEP = 0.001, 0.9, 0.999, 1e-08, 0.01, 10
VMEM_LIMIT_BYTES = 56 * 1024 * 1024
ROW_TILE = 256


def _params(*sem):
    return pltpu.CompilerParams(dimension_semantics=sem or None, vmem_limit_bytes=VMEM_LIMIT_BYTES)


def _nt(a, b):
    return lax.dot_general(a, b, (((1,), (1,)), ((), ())), preferred_element_type=F32)


def _tn(a, b):
    return lax.dot_general(a, b, (((0,), (0,)), ((), ())), preferred_element_type=F32)


def _nn(a, b, precision=None):
    return jnp.dot(a, b, preferred_element_type=F32, precision=precision)


def _split3(v):
    hi = v.astype(BF16)
    rest = v - hi.astype(F32)
    mid = rest.astype(BF16)
    return hi, mid, (rest - mid.astype(F32)).astype(BF16)


def _sum_right(v, ones):
    hi, mid, lo = _split3(v)
    return (_nn(lo, ones) + _nn(mid, ones)) + _nn(hi, ones)


def _sum_left(ones, v):
    hi, mid, lo = _split3(v)
    return (_nn(ones, lo) + _nn(ones, mid)) + _nn(ones, hi)


def _fold8(v):
    return v.reshape(v.shape[0] // 8, 8, v.shape[1]).sum(axis=0)


def _spread_total(ref):
    t = ref[...]
    ref[...] = jnp.broadcast_to(jnp.sum(t, axis=-2, keepdims=True), t.shape)


def _sigmoid(x):
    return 1.0 / (1.0 + jnp.exp(-x))


def _mm(a, b, *, ta=False, tb=False, out_dtype=F32, tm, tn, tk, shard_cols=False, also_bf16=False, name):
    (k_a, m) = a.shape if ta else a.shape[::-1]
    (k_b, n) = b.shape[::-1] if tb else b.shape
    assert k_a == k_b and m % tm == 0 and n % tn == 0 and k_a % tk == 0, (name, a.shape, b.shape)
    nk = k_a // tk
    assert nk == 1 or out_dtype == F32, name
    dims = (((0 if ta else 1,), (1 if tb else 0,)), ((), ()))

    def body(a_ref, b_ref, o_ref, *rounded):
        k = pl.program_id(2)
        part = lax.dot_general(a_ref[...].astype(BF16), b_ref[...].astype(BF16), dims, preferred_element_type=F32)
        if nk == 1:
            o_ref[...] = part.astype(out_dtype)
        else:
            @pl.when(k == 0)
            def _():
                o_ref[...] = part

            @pl.when(k > 0)
            def _():
                o_ref[...] += part

        if also_bf16:
            @pl.when(k == nk - 1)
            def _():
                rounded[0][...] = o_ref[...].astype(BF16)

    a_spec = pl.BlockSpec((tk, tm), lambda i, j, k: (k, i)) if ta else pl.BlockSpec((tm, tk), lambda i, j, k: (i, k))
    b_spec = pl.BlockSpec((tn, tk), lambda i, j, k: (j, k)) if tb else pl.BlockSpec((tk, tn), lambda i, j, k: (k, j))
    if shard_cols:
        o_spec, o_shape = pl.BlockSpec((None, tm, tn), lambda i, j, k: (j, i, 0)), (n // tn, m, tn)
    else:
        o_spec, o_shape = pl.BlockSpec((tm, tn), lambda i, j, k: (i, j)), (m, n)
    shapes = [jax.ShapeDtypeStruct(o_shape, out_dtype)] + ([jax.ShapeDtypeStruct(o_shape, BF16)] if also_bf16 else [])
    out = pl.pallas_call(
        body, name=name, grid=(m // tm, n // tn, nk), in_specs=[a_spec, b_spec], out_specs=[o_spec] * len(shapes),
        out_shape=shapes, compiler_params=_params("parallel", "parallel", "arbitrary"))(a, b)
    return out if also_bf16 else out[0]


def _norm_mod_fwd(x, branch, gate, gain, scale, shift, *, name):
    s, d = x.shape
    tm = ROW_TILE
    has_branch = branch is not None

    def body(*refs):
        if has_branch:
            x_ref, br_ref, gate_ref, gain_ref, sc_ref, sh_ref, x1_ref, h_ref, ht_ref = refs
            xv = x_ref[...] + gate_ref[...] * br_ref[...]
            x1_ref[...] = xv
        else:
            x_ref, gain_ref, sc_ref, sh_ref, h_ref, ht_ref = refs
            xv = x_ref[...]
        r = lax.rsqrt(jnp.mean(xv * xv, axis=-1, keepdims=True) + EPS)
        h = (xv * r) * gain_ref[...] * (1.0 + sc_ref[...]) + sh_ref[...]
        h_ref[...] = h.astype(BF16)
        ht_ref[...] = h.T.astype(BF16)

    row = pl.BlockSpec((tm, d), lambda i: (i, 0))
    col = pl.BlockSpec((d, tm), lambda i: (0, i))
    vec = pl.BlockSpec((1, d), lambda i: (0, 0))
    h_shapes = [jax.ShapeDtypeStruct((s, d), BF16), jax.ShapeDtypeStruct((d, s), BF16)]
    if has_branch:
        return pl.pallas_call(
            body, name=name, grid=(s // tm,), in_specs=[row, row, vec, vec, vec, vec], out_specs=[row, row, col],
            out_shape=[jax.ShapeDtypeStruct((s, d), F32)] + h_shapes,
            compiler_params=_params("parallel"))(x, branch, gate, gain, scale, shift)
    h, ht = pl.pallas_call(
        body, name=name, grid=(s // tm,), in_specs=[row, vec, vec, vec], out_specs=[row, col],
        out_shape=h_shapes, compiler_params=_params("parallel"))(x, gain, scale, shift)
    return x, h, ht


def _norm_mod_bwd(x, dh, dres, gain, scale, branch, gate, *, name):
    s, d = x.shape
    tm = ROW_TILE
    has_branch = branch is not None

    def body(*refs):
        if has_branch:
            x_ref, dh_ref, dres_ref, gain_ref, sc_ref, br_ref, gate_ref, dx_ref, dbr_ref, sums_ref = refs
        else:
            x_ref, dh_ref, dres_ref, gain_ref, sc_ref, dx_ref, sums_ref = refs
        i = pl.program_id(0)

        @pl.when(i == 0)
        def _():
            sums_ref[...] = jnp.zeros_like(sums_ref)

        xv, dhv = x_ref[...], dh_ref[...]
        r = lax.rsqrt(jnp.mean(xv * xv, axis=-1, keepdims=True) + EPS)
        xn = xv * r
        dxn = dhv * (gain_ref[...] * (1.0 + sc_ref[...]))
        dx = dres_ref[...] + r * (dxn - xn * jnp.mean(dxn * xn, axis=-1, keepdims=True))
        dx_ref[...] = dx
        sums_ref[0] += _fold8(dhv * xn)
        sums_ref[1] += _fold8(dhv)
        if has_branch:
            dbr_ref[...] = (gate_ref[...] * dx).astype(BF16)
            sums_ref[2] += _fold8(dx * br_ref[...])

        @pl.when(i == s // tm - 1)
        def _():
            _spread_total(sums_ref)

    row = pl.BlockSpec((tm, d), lambda i: (i, 0))
    vec = pl.BlockSpec((1, d), lambda i: (0, 0))
    sums = pl.BlockSpec((3, 8, d), lambda i: (0, 0, 0))
    sums_shape = jax.ShapeDtypeStruct((3, 8, d), F32)
    if has_branch:
        return pl.pallas_call(
            body, name=name, grid=(s // tm,), in_specs=[row, row, row, vec, vec, row, vec], out_specs=[row, row, sums],
            out_shape=[jax.ShapeDtypeStruct((s, d), F32), jax.ShapeDtypeStruct((s, d), BF16), sums_shape],
            compiler_params=_params("arbitrary"))(x, dh, dres, gain, scale, branch, gate)
    dx, sm = pl.pallas_call(
        body, name=name, grid=(s // tm,), in_specs=[row, row, row, vec, vec], out_specs=[row, sums],
        out_shape=[jax.ShapeDtypeStruct((s, d), F32), sums_shape],
        compiler_params=_params("arbitrary"))(x, dh, dres, gain, scale)
    return dx, None, sm


GLA_ROWS = 256


def _gla_block_setup(lr_ref, wg_ref, bg_ref):
    t, c = GLA_ROWS, GLA_CHUNK
    ri = lax.broadcasted_iota(jnp.int32, (t, t), 0)
    ci = lax.broadcasted_iota(jnp.int32, (t, t), 1)
    same = (ri // c) == (ci // c)
    causal, upper = same & (ci <= ri), same & (ci >= ri)
    z = _nn(lr_ref[...].astype(BF16), wg_ref[...]) + bg_ref[...]
    g = (jnp.minimum(z, 0.0) - jnp.log(1.0 + jnp.exp(-jnp.abs(z)))) * (1.0 / GLA_GATE_TAU)
    hi, mid, lo = _split3(g)
    total = lambda ones: (_nn(ones, lo) + _nn(ones, mid)) + _nn(ones, hi)
    return z, total(causal.astype(BF16)), total(same.astype(BF16)), causal, upper


def _chunks(t):
    return [t[i * GLA_CHUNK:(i + 1) * GLA_CHUNK] for i in range(GLA_ROWS // GLA_CHUNK)]


def _gla_fwd(proj, wg, bg, gn, *, name):
    s = proj.shape[0]
    tb, c = GLA_ROWS, GLA_CHUNK
    cb = tb // c

    def body(q_ref, k_ref, v_ref, r_ref, lr_ref, wg_ref, bg_ref, gn_ref, o_ref, y_ref, st_ref, state):
        i = pl.program_id(0)

        @pl.when(i == 0)
        def _():
            state[...] = jnp.zeros_like(state)

        low = lax.broadcasted_iota(jnp.int32, (tb, HEAD_LANES), 1) < 64
        masks = (low, jnp.logical_not(low))
        _, b, b_end, causal, _ = _gla_block_setup(lr_ref, wg_ref, bg_ref)
        for p in range(2):
            cols = pl.ds(p * HEAD_LANES, HEAD_LANES)
            bp, bep = (t[:, p * HEAD_LANES:(p + 1) * HEAD_LANES] for t in (b, b_end))
            k = k_ref[:, cols]
            q_in = q_ref[:, cols] * 0.125 * jnp.exp(bp)
            k_out = (k * jnp.exp(-bp)).astype(BF16)
            k_end = k * jnp.exp(bep - bp)
            qms = [jnp.where(m, q_in, 0.0).astype(BF16) for m in masks]
            kes = [jnp.where(m, k_end, 0.0).astype(BF16) for m in masks]
            vs = [v_ref[:, pl.ds((2 * p + e) * HEAD_LANES, HEAD_LANES)].astype(BF16) for e in range(2)]
            grow = [_tn(v0, k0) + _tn(v1, k1) for v0, k0, v1, k1 in zip(_chunks(vs[0]), _chunks(kes[0]), _chunks(vs[1]), _chunks(kes[1]))]
            st, entering = state[p], []
            for ch in range(cb):
                entering.append(st)
                st_ref[ch, p] = st
                st = st * jnp.exp(bep[ch * c:ch * c + 1, :]) + grow[ch]
            state[p] = st
            for e in range(2):
                hc = pl.ds((2 * p + e) * HEAD_LANES, HEAD_LANES)
                a = jnp.where(causal, _nt(qms[e], k_out), 0.0).astype(BF16)
                carried = jnp.concatenate([_nt(qc, sc.astype(BF16)) for qc, sc in zip(_chunks(qms[e]), entering)], axis=0)
                o = _nn(a, vs[e]) + carried
                o_ref[:, hc] = o
                rr = r_ref[:, hc]
                on = o * lax.rsqrt(jnp.mean(o * o, axis=-1, keepdims=True) + EPS)
                y_ref[:, hc] = (on * gn_ref[...] * (rr * _sigmoid(rr))).astype(BF16)

    def col(width, at):
        return pl.BlockSpec((tb, width), lambda i: (i, at // width))

    full = lambda shape: pl.BlockSpec(shape, lambda i: tuple(0 for _ in shape))
    return pl.pallas_call(
        body, name=name, grid=(s // tb,),
        in_specs=[col(256, C_GQ), col(256, C_GK), col(512, C_GV), col(512, C_GR), col(128, C_LR),
                  full((HEAD_LANES, 256)), full((1, 256)), full((1, HEAD_LANES))],
        out_specs=[pl.BlockSpec((tb, 512), lambda i: (i, 0)), pl.BlockSpec((tb, 512), lambda i: (i, 0)),
                   pl.BlockSpec((cb, 2, HEAD_LANES, HEAD_LANES), lambda i: (i, 0, 0, 0))],
        out_shape=[jax.ShapeDtypeStruct((s, 512), F32), jax.ShapeDtypeStruct((s, 512), BF16),
                   jax.ShapeDtypeStruct((s // c, 2, HEAD_LANES, HEAD_LANES), F32)],
        scratch_shapes=[pltpu.VMEM((2, HEAD_LANES, HEAD_LANES), F32)],
        compiler_params=_params("arbitrary"))(proj, proj, proj, proj, proj, wg, bg, gn)


def _gla_bwd(proj, wg, bg, gn, o_raw, states, dmixed, *, name):
    s = proj.shape[0]
    tb, c = GLA_ROWS, GLA_CHUNK
    cb = tb // c
    nblk, nch = s // tb, s // c

    def body(q_ref, k_ref, v_ref, r_ref, lr_ref, wg_ref, bg_ref, gn_ref, o_ref, st_ref, stn_ref, dy_ref,
             dq_ref, dk_ref, dv_ref, dr_ref, dlr_ref, gwg_ref, sums_ref, dstate):
        i = pl.program_id(0)

        @pl.when(i == 0)
        def _():
            dstate[...] = jnp.zeros_like(dstate)
            gwg_ref[...] = jnp.zeros_like(gwg_ref)
            sums_ref[...] = jnp.zeros_like(sums_ref)

        low = lax.broadcasted_iota(jnp.int32, (tb, HEAD_LANES), 1) < 64
        masks = (low, jnp.logical_not(low))
        z, b, b_end, causal, upper = _gla_block_setup(lr_ref, wg_ref, bg_ref)
        lr_b = lr_ref[...].astype(BF16)
        dlr = jnp.zeros((tb, HEAD_LANES), F32)
        for p in range(2):
            cols = pl.ds(p * HEAD_LANES, HEAD_LANES)
            sl = slice(p * HEAD_LANES, (p + 1) * HEAD_LANES)
            bp, bep = b[:, sl], b_end[:, sl]
            e_in, e_out, e_end = jnp.exp(bp), jnp.exp(-bp), jnp.exp(bep - bp)
            q = q_ref[:, cols] * 0.125
            k = k_ref[:, cols]
            q_in, k_out, k_end = q * e_in, k * e_out, k * e_end
            qms = [jnp.where(m, q_in, 0.0).astype(BF16) for m in masks]
            kms_out = [jnp.where(m, k_out, 0.0).astype(BF16) for m in masks]
            kms_end = [jnp.where(m, k_end, 0.0).astype(BF16) for m in masks]
            vs, dos = [], []
            for e in range(2):
                hc = pl.ds((2 * p + e) * HEAD_LANES, HEAD_LANES)
                o, rr, dy = o_ref[:, hc], r_ref[:, hc], dy_ref[:, hc]
                sg = _sigmoid(rr)
                rs = lax.rsqrt(jnp.mean(o * o, axis=-1, keepdims=True) + EPS)
                on = o * rs
                t = dy * (rr * sg)
                sums_ref[1, :, hc] += _fold8(t * on)
                dn = t * gn_ref[...]
                dos.append((rs * (dn - on * jnp.mean(dn * on, axis=-1, keepdims=True))).astype(BF16))
                dr_ref[:, hc] = (dy * on * gn_ref[...] * (sg * (1.0 + rr * (1.0 - sg)))).astype(BF16)
                vs.append(v_ref[:, hc].astype(BF16))
            grow = [_tn(d0, q0) + _tn(d1, q1) for d0, q0, d1, q1 in zip(_chunks(dos[0]), _chunks(qms[0]), _chunks(dos[1]), _chunks(qms[1]))]
            entering = [st_ref[ch, p] for ch in range(cb)]
            dst, leaving_grad = dstate[p], [None] * cb
            for ch in reversed(range(cb)):
                leaving_grad[ch] = dst
                dst = dst * jnp.exp(bep[ch * c:ch * c + 1, :]) + grow[ch]
            dstate[p] = dst
            leaving = entering[1:] + [stn_ref[0, p]]
            felt = jnp.concatenate([jnp.broadcast_to(jnp.sum(dg_st * st, axis=0, keepdims=True), (c, HEAD_LANES))
                                    for dg_st, st in zip(leaving_grad, leaving)], axis=0)
            per_chunk = lambda rows, mats, fn: jnp.concatenate([fn(r, m.astype(BF16)) for r, m in zip(_chunks(rows), mats)], axis=0)
            dq_in = jnp.zeros((tb, HEAD_LANES), F32)
            dk_out = jnp.zeros((tb, HEAD_LANES), F32)
            dk_end = jnp.zeros((tb, HEAD_LANES), F32)
            for e in range(2):
                hc = pl.ds((2 * p + e) * HEAD_LANES, HEAD_LANES)
                a = jnp.where(causal, _nt(qms[e], kms_out[e]), 0.0).astype(BF16)
                da = jnp.where(causal, _nt(dos[e], vs[e]), 0.0).astype(BF16)
                dv_ref[:, hc] = (_tn(a, dos[e]) + per_chunk(kms_end[e], leaving_grad, _nt)).astype(BF16)
                dq_in = dq_in + jnp.where(masks[e], per_chunk(dos[e], entering, _nn) + _nn(da, kms_out[e]), 0.0)
                dk_out = dk_out + _tn(da, qms[e])
                dk_end = dk_end + jnp.where(masks[e], per_chunk(vs[e], leaving_grad, _nn), 0.0)
            dq = dq_in * e_in
            dk = dk_out * e_out + dk_end * e_end
            dq_ref[:, cols] = (dq * 0.125).astype(BF16)
            dk_ref[:, cols] = dk.astype(BF16)
            dg = _sum_left(upper.astype(BF16), q * dq - k * dk) + felt
            dz = dg * (1.0 / GLA_GATE_TAU) * _sigmoid(-z[:, sl])
            dz_b = dz.astype(BF16)
            sums_ref[0, :, cols] += _fold8(dz)
            dlr = dlr + _nt(dz_b, wg_ref[:, cols])
            gwg_ref[:, cols] += _tn(lr_b, dz_b)
        dlr_ref[...] = dlr.astype(BF16)

        @pl.when(i == nblk - 1)
        def _():
            _spread_total(sums_ref)

    rev = lambda i: nblk - 1 - i

    def col(width, at):
        return pl.BlockSpec((tb, width), lambda i: (rev(i), at // width))

    full = lambda shape: pl.BlockSpec(shape, lambda i: tuple(0 for _ in shape))
    out_col = lambda width: pl.BlockSpec((tb, width), lambda i: (rev(i), 0))
    return pl.pallas_call(
        body, name=name, grid=(nblk,),
        in_specs=[col(256, C_GQ), col(256, C_GK), col(512, C_GV), col(512, C_GR), col(128, C_LR),
                  full((HEAD_LANES, 256)), full((1, 256)), full((1, HEAD_LANES)),
                  pl.BlockSpec((tb, 512), lambda i: (rev(i), 0)),
                  pl.BlockSpec((cb, 2, HEAD_LANES, HEAD_LANES), lambda i: (rev(i), 0, 0, 0)),
                  pl.BlockSpec((1, 2, HEAD_LANES, HEAD_LANES), lambda i: (jnp.minimum((rev(i) + 1) * cb, nch - 1), 0, 0, 0)),
                  pl.BlockSpec((tb, 512), lambda i: (rev(i), 0))],
        out_specs=[out_col(256), out_col(256), out_col(512), out_col(512), out_col(128),
                   full((HEAD_LANES, 256)), full((2, 8, 512))],
        out_shape=[jax.ShapeDtypeStruct((s, 256), BF16), jax.ShapeDtypeStruct((s, 256), BF16),
                   jax.ShapeDtypeStruct((s, 512), BF16), jax.ShapeDtypeStruct((s, 512), BF16),
                   jax.ShapeDtypeStruct((s, 128), BF16), jax.ShapeDtypeStruct((HEAD_LANES, 256), F32),
                   jax.ShapeDtypeStruct((2, 8, 512), F32)],
        scratch_shapes=[pltpu.VMEM((2, HEAD_LANES, HEAD_LANES), F32)],
        compiler_params=_params("arbitrary"))(proj, proj, proj, proj, proj, wg, bg, gn, o_raw, states, states, dmixed)


def _head_sum_matrix():
    ri = lax.broadcasted_iota(jnp.int32, (512, 512), 0) // 64
    ci = lax.broadcasted_iota(jnp.int32, (512, 512), 1) // 64
    return (ri == ci).astype(BF16)


def _attn_prep(proj, qg, kg, *, name):
    s = proj.shape[0]
    tm = ROW_TILE

    def body(q_ref, k_ref, qg_ref, kg_ref, qa_ref, ka_ref):
        hs = _head_sum_matrix()
        q, k = q_ref[...], k_ref[...]
        qr = lax.rsqrt(_sum_right(q * q, hs) * (1.0 / 64) + EPS)
        kr = lax.rsqrt(_sum_right(k * k, hs) * (1.0 / 64) + EPS)
        qa_ref[...] = q * qr * qg_ref[...] * 0.125
        ka_ref[...] = k * kr * kg_ref[...]

    col = lambda at: pl.BlockSpec((tm, 512), lambda i: (i, at // 512))
    vec = pl.BlockSpec((1, 512), lambda i: (0, 0))
    out = pl.BlockSpec((tm, 512), lambda i: (i, 0))
    return pl.pallas_call(
        body, name=name, grid=(s // tm,), in_specs=[col(C_AQ), col(C_AK), vec, vec], out_specs=[out] * 2,
        out_shape=[jax.ShapeDtypeStruct((s, 512), F32)] * 2, compiler_params=_params("parallel"))(proj, proj, qg, kg)


FAR = 1e30
LOG2E, LN2 = 1.4426950408889634, 0.6931471805599453


def _attn_distance(first):
    blk = ATTN_BLOCK
    iq = lax.broadcasted_iota(jnp.int32, (2 * blk, 2 * blk), 0) & (blk - 1)
    ik = lax.broadcasted_iota(jnp.int32, (2 * blk, 2 * blk), 1)
    rel = iq + blk - ik
    valid = (rel >= 0) & (rel <= blk) & (jnp.logical_not(first) | (ik >= blk))
    return jnp.where(valid, rel.astype(F32), FAR)


def _stack_heads(t2):
    low = lax.broadcasted_iota(jnp.int32, t2.shape, 1) < 64
    return jnp.concatenate([jnp.where(low, t2, 0.0), jnp.where(low, 0.0, t2)], axis=0).astype(BF16)


def _unstack_heads(t):
    blk = ATTN_BLOCK
    low = lax.broadcasted_iota(jnp.int32, (blk, HEAD_LANES), 1) < 64
    return jnp.where(low, t[0:blk], t[blk:2 * blk])


def _attn_scores(qs, kcat, slopes, dil, dist):
    top = lax.broadcasted_iota(jnp.int32, (2 * ATTN_BLOCK, 1), 0) < ATTN_BLOCK
    return _nt(qs, kcat) - jnp.where(top, slopes[0] * (dil * LOG2E), slopes[1] * (dil * LOG2E)) * dist


def _pair_slopes(p):
    if isinstance(p, int):
        return ALIBI_SLOPES[2 * p], ALIBI_SLOPES[2 * p + 1]
    pick = lambda e: jnp.where(p == 0, ALIBI_SLOPES[e], jnp.where(p == 1, ALIBI_SLOPES[2 + e],
                               jnp.where(p == 2, ALIBI_SLOPES[4 + e], ALIBI_SLOPES[6 + e])))
    return pick(0), pick(1)


ATTN_GROUP = 4


def _each(fn, *lists):
    return [fn(*args) for args in zip(*lists)]


def _attn_group_fwd(q2s, kcats, vcats, slopes, dil, dist):
    qs = _each(lambda q2: _stack_heads(q2 * LOG2E), q2s)
    sc = _each(lambda q, k, sl: _attn_scores(q, k, sl, dil, dist), qs, kcats, slopes)
    m = _each(lambda s: jnp.max(s, axis=-1, keepdims=True), sc)
    pr = _each(lambda s, mx: jnp.exp2(s - mx), sc, m)
    den = _each(lambda p: jnp.sum(p, axis=-1, keepdims=True), pr)
    o = _each(lambda p, v, d: _nn(p.astype(BF16), v) / d, pr, vcats, den)
    lse = _each(lambda mx, d, t: jnp.broadcast_to(mx + jnp.log2(d), t.shape), m, den, o)
    return _each(lambda t, l: (_unstack_heads(t), _unstack_heads(l)), o, lse)


def _attn_group_bwd(q2s, kcats, vcats, do2s, y2s, lse2s, slopes, dil, dist):
    lane = lax.broadcasted_iota(jnp.int32, (ATTN_BLOCK, HEAD_LANES), 1)
    low = lane < 64
    per_head = lambda t, pick: jnp.concatenate([jnp.sum(jnp.where(pick(0), t, 0.0), axis=-1, keepdims=True),
                                                jnp.sum(jnp.where(pick(1), t, 0.0), axis=-1, keepdims=True)], axis=0)
    lse = _each(lambda l: per_head(l, lambda e: lane == 64 * e), lse2s)
    delta = _each(lambda d, y: per_head(d * y, lambda e: low if e == 0 else jnp.logical_not(low)), do2s, y2s)
    qs = _each(lambda q2: _stack_heads(q2 * LOG2E), q2s)
    dos = _each(_stack_heads, do2s)
    sc = _each(lambda q, k, sl: _attn_scores(q, k, sl, dil, dist), qs, kcats, slopes)
    pr = _each(lambda s, l: jnp.exp2(s - l), sc, lse)
    dp = _each(_nt, dos, vcats)
    ds = _each(lambda p, d, dl: (p * (d - dl)).astype(BF16), pr, dp, delta)
    dq = _each(lambda d, k: _unstack_heads(_nn(d, k)), ds, kcats)
    dk = _each(lambda d, q: _tn(d, q) * LN2, ds, qs)
    dv = _each(lambda p, d: _tn(p.astype(BF16), d), pr, dos)
    return list(zip(dq, dk, dv))


def _attn_specs(dil):
    rows = ATTN_BLOCK * dil
    if dil == 1:
        cur = lambda at: pl.BlockSpec((rows, 512), lambda n: (n, at // 512))
        prev = lambda at: pl.BlockSpec((rows, 512), lambda n: (jnp.maximum(n - 1, 0), at // 512))
    else:
        cur = lambda at: pl.BlockSpec((rows, HEAD_LANES), lambda n, p: (n, at // HEAD_LANES + p))
        prev = lambda at: pl.BlockSpec((rows, HEAD_LANES), lambda n, p: (jnp.maximum(n - 1, 0), at // HEAD_LANES + p))
    return cur, prev


def _attn_loop(dil, one_group):
    if dil == 1:
        one_group([(slice(None), pl.ds(p * HEAD_LANES, HEAD_LANES), p) for p in range(ATTN_GROUP)])
    else:
        p = pl.program_id(1)

        def step(g, carry):
            one_group([(pl.ds(g * ATTN_GROUP + j, ATTN_BLOCK, stride=dil), slice(None), p) for j in range(ATTN_GROUP)])
            return carry

        if dil == ATTN_GROUP:
            step(0, 0)
        else:
            lax.fori_loop(0, dil // ATTN_GROUP, step, 0)


def _dil_attn_fwd(qa, ka, proj, dil, *, name):
    s = qa.shape[0]

    def body(q_ref, kp_ref, kc_ref, vp_ref, vc_ref, o_ref, lse_ref):
        dist = _attn_distance(pl.program_id(0) == 0)

        def one_group(items):
            both = lambda a, b: [jnp.concatenate([a[rows, cols], b[rows, cols]], axis=0).astype(BF16) for rows, cols, _ in items]
            outs = _attn_group_fwd([q_ref[rows, cols] for rows, cols, _ in items], both(kp_ref, kc_ref), both(vp_ref, vc_ref),
                                   [_pair_slopes(p) for _, _, p in items], dil, dist)
            for (rows, cols, _), (o2, lse2) in zip(items, outs):
                o_ref[rows, cols] = o2
                lse_ref[rows, cols] = lse2

        _attn_loop(dil, one_group)

    cur, prev = _attn_specs(dil)
    grid = (s // ATTN_BLOCK,) if dil == 1 else (s // (ATTN_BLOCK * dil), 4)
    return pl.pallas_call(
        body, name=name, grid=grid, in_specs=[cur(0), prev(0), cur(0), prev(C_AV), cur(C_AV)], out_specs=[cur(0), cur(0)],
        out_shape=[jax.ShapeDtypeStruct((s, 512), F32)] * 2,
        compiler_params=_params(*["parallel"] * len(grid)))(qa, ka, ka, proj, proj)


def _attn_merge(branches, y_gla, *, name):
    s = y_gla.shape[0]
    tm = ROW_TILE

    def body(o0, l0, o1, l1, o2, l2, yg_ref, mixed_ref, y_ref, lse_ref):
        m = jnp.maximum(jnp.maximum(l0[...], l1[...]), l2[...])
        w0, w1, w2 = jnp.exp2(l0[...] - m), jnp.exp2(l1[...] - m), jnp.exp2(l2[...] - m)
        zs = w0 + w1 + w2
        y = (w0 * o0[...] + w1 * o1[...] + w2 * o2[...]) / zs
        y_ref[...] = y
        lse_ref[...] = m + jnp.log2(zs)
        mixed_ref[:, 0:512] = yg_ref[...]
        mixed_ref[:, 512:1024] = y.astype(BF16)

    blk = pl.BlockSpec((tm, 512), lambda i: (i, 0))
    args = [t for pair in branches for t in pair]
    return pl.pallas_call(
        body, name=name, grid=(s // tm,), in_specs=[blk] * 7,
        out_specs=[pl.BlockSpec((tm, 1024), lambda i: (i, 0)), blk, blk],
        out_shape=[jax.ShapeDtypeStruct((s, 1024), BF16), jax.ShapeDtypeStruct((s, 512), F32),
                   jax.ShapeDtypeStruct((s, 512), F32)],
        compiler_params=_params("parallel"))(*args, y_gla)


def _dil_attn_bwd(qa, ka, proj, y_att, lse, dmixed, dil, *, name):
    s = qa.shape[0]
    blk = ATTN_BLOCK

    def body(q_ref, kp_ref, kc_ref, vp_ref, vc_ref, y_ref, lse_ref, do_ref, dq_ref, dkc_ref, dkp_ref, dvc_ref, dvp_ref):
        dist = _attn_distance(pl.program_id(0) == 0)

        def one_group(items):
            both = lambda a, b: [jnp.concatenate([a[rows, cols], b[rows, cols]], axis=0).astype(BF16) for rows, cols, _ in items]
            at = lambda ref: [ref[rows, cols] for rows, cols, _ in items]
            outs = _attn_group_bwd(at(q_ref), both(kp_ref, kc_ref), both(vp_ref, vc_ref), at(do_ref), at(y_ref), at(lse_ref),
                                   [_pair_slopes(p) for _, _, p in items], dil, dist)
            for (rows, cols, _), (dq, dk, dv) in zip(items, outs):
                dq_ref[rows, cols] = dq
                dkp_ref[rows, cols] = dk[0:blk]
                dkc_ref[rows, cols] = dk[blk:2 * blk]
                dvp_ref[rows, cols] = dv[0:blk]
                dvc_ref[rows, cols] = dv[blk:2 * blk]

        _attn_loop(dil, one_group)

    cur, prev = _attn_specs(dil)
    grid = (s // blk,) if dil == 1 else (s // (blk * dil), 4)
    return pl.pallas_call(
        body, name=name, grid=grid,
        in_specs=[cur(0), prev(0), cur(0), prev(C_AV), cur(C_AV), cur(0), cur(0), cur(512)], out_specs=[cur(0)] * 5,
        out_shape=[jax.ShapeDtypeStruct((s, 512), F32)] * 5, compiler_params=_params(*["parallel"] * len(grid)),
    )(qa, ka, ka, proj, proj, y_att, lse, dmixed)


def _attn_post(parts, proj, qg, kg, *, name):
    s = proj.shape[0]
    tm = ATTN_BLOCK
    nblk = s // tm

    def body(*refs):
        ins, (q_ref, k_ref, qg_ref, kg_ref, dq_out, dk_out, dv_out, sums_ref) = refs[:15], refs[15:]
        i = pl.program_id(0)

        @pl.when(i == 0)
        def _():
            sums_ref[...] = jnp.zeros_like(sums_ref)

        dq = jnp.zeros((tm, 512), F32)
        dk = jnp.zeros((tm, 512), F32)
        dv = jnp.zeros((tm, 512), F32)
        for g, dil in enumerate(DILATIONS):
            dq_r, dkc_r, dkp_r, dvc_r, dvp_r = ins[5 * g:5 * g + 5]
            inside = (i + dil < nblk).astype(F32)
            dq = dq + dq_r[...]
            dk = dk + dkc_r[...] + inside * dkp_r[...]
            dv = dv + dvc_r[...] + inside * dvp_r[...]
        dv_out[...] = dv.astype(BF16)
        hs = _head_sum_matrix()
        for row, (x_ref, g_ref, dy, out, post) in enumerate(((q_ref, qg_ref, dq, dq_out, 0.125), (k_ref, kg_ref, dk, dk_out, 1.0))):
            x = x_ref[...]
            rs = lax.rsqrt(_sum_right(x * x, hs) * (1.0 / 64) + EPS)
            xn = x * rs
            dy = dy * post
            sums_ref[row] += _fold8(dy * xn)
            dn = dy * g_ref[...]
            out[...] = (rs * (dn - xn * (_sum_right(dn * xn, hs) * (1.0 / 64)))).astype(BF16)

        @pl.when(i == nblk - 1)
        def _():
            _spread_total(sums_ref)

    here = pl.BlockSpec((tm, 512), lambda i: (i, 0))
    specs = []
    for dil in DILATIONS:
        later = pl.BlockSpec((tm, 512), lambda i, dil=dil: (jnp.minimum(i + dil, nblk - 1), 0))
        specs += [here, here, later, here, later]
    col = lambda at: pl.BlockSpec((tm, 512), lambda i: (i, at // 512))
    vec = pl.BlockSpec((1, 512), lambda i: (0, 0))
    return pl.pallas_call(
        body, name=name, grid=(nblk,), in_specs=specs + [col(C_AQ), col(C_AK), vec, vec],
        out_specs=[here, here, here, pl.BlockSpec((2, 8, 512), lambda i: (0, 0, 0))],
        out_shape=[jax.ShapeDtypeStruct((s, 512), BF16)] * 3 + [jax.ShapeDtypeStruct((2, 8, 512), F32)],
        compiler_params=_params("arbitrary"))(*[t for part in parts for t in part], proj, proj, qg, kg)


FFN_TM, FFN_TN = 256, 1408
HALO = 16


def _conv3(u_ref, halo_ref, w_ref, b_ref, first):
    u = u_ref[...].astype(F32)
    ext = jnp.concatenate([jnp.where(first, 0.0, halo_ref[...].astype(F32)), u], axis=0)
    u1 = pltpu.roll(ext, 1, 0)[HALO:]
    u2 = pltpu.roll(ext, 2, 0)[HALO:]
    return b_ref[...] + w_ref[0:1, :] * u2 + w_ref[1:2, :] * u1 + w_ref[2:3, :] * u, u, u1, u2


def _ffn_specs(tm, tn):
    nj = D_FF // tn
    blk = lambda half: pl.BlockSpec((tm, tn), lambda j, i: (i, j + half * nj))
    halo = lambda half: pl.BlockSpec((HALO, tn), lambda j, i: (jnp.maximum(i * (tm // HALO) - 1, 0), j + half * nj))
    wspec = lambda half: pl.BlockSpec((3, tn), lambda j, i: (0, j + half * nj))
    bspec = lambda half: pl.BlockSpec((1, tn), lambda j, i: (0, j + half * nj))
    return [blk(0), halo(0), blk(1), halo(1), wspec(0), wspec(1), bspec(0), bspec(1)]


def _conv_swiglu_fwd(u, conv_w, conv_b, *, name):
    s = u.shape[0]
    tm, tn = FFN_TM, FFN_TN

    def body(ug_ref, hg_ref, uv_ref, hv_ref, wg_ref, wv_ref, bg_ref, bv_ref, act_ref):
        first = pl.program_id(1) == 0
        cg = _conv3(ug_ref, hg_ref, wg_ref, bg_ref, first)[0]
        cv = _conv3(uv_ref, hv_ref, wv_ref, bv_ref, first)[0]
        act_ref[...] = (cg * _sigmoid(cg) * cv).astype(BF16)

    return pl.pallas_call(
        body, name=name, grid=(D_FF // tn, s // tm), in_specs=_ffn_specs(tm, tn),
        out_specs=pl.BlockSpec((tm, tn), lambda j, i: (i, j)), out_shape=jax.ShapeDtypeStruct((s, D_FF), BF16),
        compiler_params=_params("parallel", "parallel"))(u, u, u, u, conv_w, conv_w, conv_b, conv_b)


def _conv_swiglu_bwd_pre(u, conv_w, conv_b, dact, *, name):
    s = u.shape[0]
    tm, tn = FFN_TM, FFN_TN

    def body(ug_ref, hg_ref, uv_ref, hv_ref, wg_ref, wv_ref, bg_ref, bv_ref, da_ref, duc_ref, sums_ref):
        i = pl.program_id(1)

        @pl.when(i == 0)
        def _():
            sums_ref[...] = jnp.zeros_like(sums_ref)

        cg, g0, g1, g2 = _conv3(ug_ref, hg_ref, wg_ref, bg_ref, i == 0)
        cv, v0, v1, v2 = _conv3(uv_ref, hv_ref, wv_ref, bv_ref, i == 0)
        da = da_ref[...].astype(F32)
        sg = _sigmoid(cg)
        dg = da * cv * (sg * (1.0 + cg * (1.0 - sg)))
        dv = da * (cg * sg)
        duc_ref[0] = dg.astype(BF16)
        duc_ref[1] = dv.astype(BF16)
        for half, (d, taps) in enumerate(((dg, (g2, g1, g0)), (dv, (v2, v1, v0)))):
            for t, tap in enumerate(taps):
                sums_ref[half, t] += _fold8(d * tap)
            sums_ref[half, 3] += _fold8(d)

        @pl.when(i == s // tm - 1)
        def _():
            _spread_total(sums_ref)

    return pl.pallas_call(
        body, name=name, grid=(D_FF // tn, s // tm),
        in_specs=_ffn_specs(tm, tn) + [pl.BlockSpec((tm, tn), lambda j, i: (i, j))],
        out_specs=[pl.BlockSpec((2, tm, tn), lambda j, i: (0, i, j)), pl.BlockSpec((2, 4, 8, tn), lambda j, i: (0, 0, 0, j))],
        out_shape=[jax.ShapeDtypeStruct((2, s, D_FF), BF16), jax.ShapeDtypeStruct((2, 4, 8, D_FF), F32)],
        compiler_params=_params("parallel", "arbitrary"))(u, u, u, u, conv_w, conv_w, conv_b, conv_b, dact)


def _conv_bwd(duc, conv_w, *, name):
    _, s, _ = duc.shape
    tm, tn = FFN_TM, FFN_TN
    nj, ni = D_FF // tn, s // tm

    def body(d_ref, halo_ref, w_ref, du_ref):
        last = pl.program_id(2) == ni - 1
        d = d_ref[0].astype(F32)
        ext = jnp.concatenate([d, jnp.where(last, 0.0, halo_ref[0].astype(F32))], axis=0)
        n = tm + HALO
        d1 = pltpu.roll(ext, n - 1, 0)[:tm]
        d2 = pltpu.roll(ext, n - 2, 0)[:tm]
        du_ref[...] = (w_ref[2:3, :] * d + w_ref[1:2, :] * d1 + w_ref[0:1, :] * d2).astype(BF16)

    return pl.pallas_call(
        body, name=name, grid=(2, nj, ni),
        in_specs=[pl.BlockSpec((1, tm, tn), lambda g, j, i: (g, i, j)),
                  pl.BlockSpec((1, HALO, tn), lambda g, j, i: (g, jnp.minimum((i + 1) * (tm // HALO), s // HALO - 1), j)),
                  pl.BlockSpec((3, tn), lambda g, j, i: (0, g * nj + j))],
        out_specs=pl.BlockSpec((tm, tn), lambda g, j, i: (i, g * nj + j)),
        out_shape=jax.ShapeDtypeStruct((s, 2 * D_FF), BF16),
        compiler_params=_params("parallel", "parallel", "parallel"))(duc, duc, conv_w)


def _loss_head(x1, ffn, gate, target, *, name):
    s, d = x1.shape
    tm = ROW_TILE

    def body(x_ref, f_ref, g_ref, t_ref, dy_ref, df_ref, sums_ref):
        i = pl.program_id(0)

        @pl.when(i == 0)
        def _():
            sums_ref[...] = jnp.zeros_like(sums_ref)

        f = f_ref[...]
        err = x_ref[...] + g_ref[...] * f - t_ref[...]
        dy = err * (1.0 / d)
        dy_ref[...] = dy
        df_ref[...] = (g_ref[...] * dy).astype(BF16)
        sums_ref[0] += _fold8(dy * f)
        sums_ref[1] += _fold8(err * err)

        @pl.when(i == s // tm - 1)
        def _():
            _spread_total(sums_ref)

    row = pl.BlockSpec((tm, d), lambda i: (i, 0))
    return pl.pallas_call(
        body, name=name, grid=(s // tm,), in_specs=[row, row, pl.BlockSpec((1, d), lambda i: (0, 0)), row],
        out_specs=[row, row, pl.BlockSpec((2, 8, d), lambda i: (0, 0, 0))],
        out_shape=[jax.ShapeDtypeStruct((s, d), F32), jax.ShapeDtypeStruct((s, d), BF16), jax.ShapeDtypeStruct((2, 8, d), F32)],
        compiler_params=_params("arbitrary"))(x1, ffn, gate, target)


def _adamw(w, g, m, v, *, name):
    rows, cols = w.shape
    tm = next((t for t in range(ROW_TILE, 7, -8) if rows % t == 0), rows)

    def body(w_ref, g_ref, m_ref, v_ref, d_ref, mo_ref, vo_ref):
        gv = g_ref[...]
        mn = ADAM_B1 * m_ref[...] + (1.0 - ADAM_B1) * gv
        vn = ADAM_B2 * v_ref[...] + (1.0 - ADAM_B2) * (gv * gv)
        m_hat = mn / (1.0 - ADAM_B1 ** ADAM_STEP)
        v_hat = vn / (1.0 - ADAM_B2 ** ADAM_STEP)
        d_ref[...] = -ADAM_LR * (m_hat / (jnp.sqrt(v_hat) + ADAM_EPS) + ADAM_WD * w_ref[...])
        mo_ref[...] = mn
        vo_ref[...] = vn

    blk = pl.BlockSpec((tm, cols), lambda i: (i, 0))
    return pl.pallas_call(
        body, name=name, grid=(rows // tm,), in_specs=[blk] * 4, out_specs=[blk] * 3,
        out_shape=[jax.ShapeDtypeStruct((rows, cols), F32)] * 3, compiler_params=_params("parallel"))(w, g, m, v)


def _colsum(t):
    return t[..., 0, :]


def _in_proj_layout(w_in):
    pad = jnp.zeros((w_in.shape[0], PROJ_W - C_LR - GLA_GATE_RANK), w_in.dtype)
    return jnp.concatenate([w_in[:, :1536], w_in[:, 1552:], w_in[:, 1536:1552], pad], axis=1)


def _in_proj_grad_layout(g):
    return jnp.concatenate([g[:, :1536], g[:, C_LR:C_LR + GLA_GATE_RANK], g[:, 1536:C_LR]], axis=1)


def _gate_layout(gla_w_gate):
    return jnp.pad(gla_w_gate, ((0, HEAD_LANES - GLA_GATE_RANK), (0, 0))).astype(BF16)


def _local_step(x, target, mod, wi, wo, ffn_weights, ffn_grads_ready, conv_w, conv_b, wg, bg, gn, qg, kg, n1g, n2g):
    d = D_MODEL
    sh1, sc1, g1, sh2, sc2, g2 = [mod[:, i * d:(i + 1) * d] for i in range(6)]
    qg8, kg8 = jnp.tile(qg, (1, 8)), jnp.tile(kg, (1, 8))

    _, h1, h1_t = _norm_mod_fwd(x, None, None, n1g, sc1, sh1, name="norm1_fwd")
    proj = _mm(h1, wi, tm=1024, tn=PROJ_W, tk=d, name="in_proj")
    o_raw, y_gla, states = _gla_fwd(proj, wg, bg, gn, name="gla_fwd")
    qa, ka = _attn_prep(proj, qg8, kg8, name="attn_prep")
    branches = [_dil_attn_fwd(qa, ka, proj, dil, name=f"attn_fwd_d{dil}") for dil in DILATIONS]
    mixed, y_att, lse = _attn_merge(branches, y_gla, name="attn_merge")
    attn_out = _mm(mixed, wo, tm=1024, tn=d, tk=d, name="out_proj")
    x1, h2, h2_t = _norm_mod_fwd(x, attn_out, g1, n2g, sc2, sh2, name="norm2_fwd")
    wup, wdown = ffn_weights(h2)
    u = _mm(h2, wup, out_dtype=BF16, tm=1024, tn=D_FF, tk=d, name="up_proj")
    act = _conv_swiglu_fwd(u, conv_w, conv_b, name="conv_swiglu_fwd")
    ffn = _mm(act, wdown, tm=1024, tn=d, tk=D_FF, name="down_proj")
    dy, dffn, head_sums = _loss_head(x1, ffn, g2, target, name="loss_head")

    dact = _mm(dffn, wdown, tb=True, out_dtype=BF16, tm=1024, tn=D_FF, tk=d, name="down_proj_dx")
    g_wdown, g_wdown_b = _mm(act, dffn, ta=True, tm=1408, tn=d, tk=1024, also_bf16=True, name="down_proj_dw")
    duc, conv_sums = _conv_swiglu_bwd_pre(u, conv_w, conv_b, dact, name="conv_swiglu_bwd")
    du = _conv_bwd(duc, conv_w, name="conv_bwd")
    dh2 = _mm(du, wup, tb=True, tm=1024, tn=d, tk=D_FF, name="up_proj_dx")
    g_wup, g_wup_b = _mm(h2_t, du, tm=d, tn=1408, tk=1024, shard_cols=True, also_bf16=True, name="up_proj_dw")
    token = ffn_grads_ready(g_wup_b, g_wdown_b)
    g1_late = g1 if token is None else g1 + token[0:1, 0:1]
    dx1, dao, n2_sums = _norm_mod_bwd(x1, dh2, dy, n2g, sc2, attn_out, g1_late, name="norm2_bwd")

    dmixed = _mm(dao, wo, tb=True, tm=1024, tn=d, tk=d, name="out_proj_dx")
    g_wo = _mm(mixed, dao, ta=True, tm=d, tn=d, tk=1024, name="out_proj_dw")
    dgq, dgk, dgv, dgr, dlr, g_wg, gla_sums = _gla_bwd(proj, wg, bg, gn, o_raw, states, dmixed, name="gla_bwd")
    parts = [_dil_attn_bwd(qa, ka, proj, y_att, lse, dmixed, dil, name=f"attn_bwd_d{dil}") for dil in DILATIONS]
    daq, dak, dav, qk_sums = _attn_post(parts, proj, qg8, kg8, name="attn_post")
    dproj = jnp.concatenate([dgq, dgk, dgv, dgr, daq, dak, dav, dlr], axis=1)
    dh1 = _mm(dproj, wi, tb=True, tm=1024, tn=d, tk=PROJ_W, name="in_proj_dx")
    g_wi = _mm(h1_t, dproj, tm=512, tn=PROJ_W, tk=1024, name="in_proj_dw")
    grad_x, _, n1_sums = _norm_mod_bwd(x, dh1, dx1, n1g, sc1, None, None, name="norm1_bwd")

    n1, n2, hs, cs = _colsum(n1_sums), _colsum(n2_sums), _colsum(head_sums), _colsum(conv_sums)
    gs, qs = _colsum(gla_sums), _colsum(qk_sums)
    dmod = jnp.concatenate([n1[1], n1[0] * n1g[0], n2[2], n2[1], n2[0] * n2g[0], hs[0]])
    small = dict(
        dmod=dmod,
        norm1_g=n1[0] * (1.0 + sc1[0]), norm2_g=n2[0] * (1.0 + sc2[0]),
        gla_w_gate=g_wg[:GLA_GATE_RANK], gla_b_gate=gs[0, :256], gla_norm_g=gs[1].reshape(4, 128).sum(axis=0),
        q_norm_g=qs[0].reshape(8, 64).sum(axis=0), k_norm_g=qs[1].reshape(8, 64).sum(axis=0),
        conv_w=jnp.concatenate([cs[0, :3], cs[1, :3]], axis=1), conv_b=jnp.concatenate([cs[0, 3], cs[1, 3]]),
    )
    return head_sums[1], grad_x, (g_wi, g_wo, g_wup, g_wdown), small


N_DEV, N_CHIP = 8, 4
ANY = pl.BlockSpec(memory_space=pl.ANY)
VMEM_SPEC = pl.BlockSpec(memory_space=pltpu.VMEM)


def _place():
    x, y, c = lax.axis_index("x"), lax.axis_index("y"), lax.axis_index("c")
    other_chips = [(1 - x, y), (x, 1 - y), (1 - x, 1 - y)]
    return x, y, c, (x, y, 1 - c), other_chips


def _all_gather_small(v, *, name):
    m, n = v.shape

    def body(v_ref, out_ref, send_sems, recv_sems, local_sem):
        x, y, c, sibling, chips = _place()
        me = (x, y, c)

        def rows(px, py, pc):
            return out_ref.at[pl.ds((4 * px + 2 * py + pc) * m, m), :]

        def copy(k, block, to, src=None):
            return pltpu.make_async_remote_copy(
                src_ref=rows(*block) if src is None else src, dst_ref=rows(*block), send_sem=send_sems.at[k],
                recv_sem=recv_sems.at[k], device_id=to, device_id_type=MESH)

        mine = pltpu.make_async_copy(v_ref, rows(*me), local_sem)
        mine.start()
        first = [copy(0, me, sibling, src=v_ref)]
        first += [copy(1 + j, me, (*chip, c), src=v_ref) for j, chip in enumerate(chips)]
        for cp in first:
            cp.start()
        passed = [copy(4 + j, (*chip, c), sibling) for j, chip in enumerate(chips)]
        for j, chip in enumerate(chips):
            copy(1 + j, (*chip, c), me).wait_recv()
            passed[j].start()
        copy(0, sibling, me).wait_recv()
        for j, chip in enumerate(chips):
            copy(4 + j, (*chip, 1 - c), me).wait_recv()
        for cp in first + passed:
            cp.wait_send()
        mine.wait()

    return pl.pallas_call(
        body, name=name, out_shape=jax.ShapeDtypeStruct((N_DEV * m, n), v.dtype), in_specs=[VMEM_SPEC], out_specs=VMEM_SPEC,
        scratch_shapes=[pltpu.SemaphoreType.DMA((7,)), pltpu.SemaphoreType.DMA((7,)), pltpu.SemaphoreType.DMA],
    )(v)


def _gather_weight_shards(shards, *, name):
    nw = len(shards)

    def body(*refs):
        srcs, outs, (send_sems, recv_sems) = refs[:nw], refs[nw:2 * nw], refs[2 * nw:]
        x, y, c, sibling, chips = _place()
        index = lambda chip: 2 * chip[0] + chip[1]

        def copy(w, k, src, dst, to):
            return pltpu.make_async_remote_copy(src_ref=src, dst_ref=dst, send_sem=send_sems.at[6 * w + k],
                                                recv_sem=recv_sems.at[6 * w + k], device_id=to, device_id_type=MESH)

        sent = []
        for w, (src_ref, out_ref) in enumerate(zip(srcs, outs)):
            for k, chip in enumerate(chips):
                sent.append(copy(w, k, src_ref.at[c], out_ref.at[2 * x + y, c], (*chip, c)))
                sent[-1].start()
        for w, out_ref in enumerate(outs):
            for k, chip in enumerate(chips):
                landed = out_ref.at[index(chip), c]
                copy(w, k, landed, landed, (*chip, c)).wait_recv()
                sent.append(copy(w, 3 + k, landed, landed, sibling))
                sent[-1].start()
        for w, out_ref in enumerate(outs):
            for k, chip in enumerate(chips):
                passed_on = out_ref.at[index(chip), 1 - c]
                copy(w, 3 + k, passed_on, passed_on, sibling).wait_recv()
        for cp in sent:
            cp.wait_send()

    return pl.pallas_call(
        body, name=name, out_shape=[jax.ShapeDtypeStruct((N_CHIP, *s.shape), s.dtype) for s in shards],
        in_specs=[ANY] * nw, out_specs=[ANY] * nw,
        scratch_shapes=[pltpu.SemaphoreType.DMA((6 * nw,)), pltpu.SemaphoreType.DMA((6 * nw,))],
    )(*shards)


HBM_SPEC = pl.BlockSpec(memory_space=pltpu.HBM)
SEM_SPEC = pl.BlockSpec(memory_space=pltpu.SEMAPHORE)
DATAFLOW_EFFECT = pltpu.SideEffectType.DATAFLOW_SIDE_EFFECTING


def _late_copies(srcs, lands, send_sems, recv_sems):
    x, y, c, _, chips = _place()
    return [pltpu.make_async_remote_copy(
        src_ref=src.at[c], dst_ref=land.at[2 * x + y, c], send_sem=send_sems.at[6 * w + 2 * r + core],
        recv_sem=recv_sems.at[6 * w + 2 * r + c], device_id=(*chip, core), device_id_type=MESH)
        for w, (src, land) in enumerate(zip(srcs, lands)) for r, chip in enumerate(chips) for core in range(2)]


def _gather_late_start(own, after, *, name):
    nw = len(own)

    def body(*refs):
        srcs, lands, send_sems, recv_sems, token = refs[:nw], refs[nw:2 * nw], refs[2 * nw + 1], refs[2 * nw + 2], refs[-1]
        for cp in _late_copies(srcs, lands, send_sems, recv_sems):
            cp.start()
        token[...] = jnp.zeros_like(token)

    lands = [pltpu.with_memory_space_constraint(lax.empty((N_CHIP, *s.shape), s.dtype), pltpu.HBM) for s in own]
    own = [pltpu.with_memory_space_constraint(s, pltpu.HBM) for s in own]
    out = pl.pallas_call(
        body, name=name,
        out_shape=(pltpu.SemaphoreType.DMA((6 * nw,)), pltpu.SemaphoreType.DMA((6 * nw,)),
                   *[pltpu.HBM(s.shape, s.dtype) for s in own], *[pltpu.HBM(s.shape, s.dtype) for s in lands],
                   jax.ShapeDtypeStruct((8, 128), F32)),
        in_specs=[HBM_SPEC] * (2 * nw) + [ANY], out_specs=(SEM_SPEC, SEM_SPEC, *[HBM_SPEC] * (2 * nw), VMEM_SPEC),
        input_output_aliases={i: 2 + i for i in range(2 * nw)},
        compiler_params=pltpu.CompilerParams(has_side_effects=DATAFLOW_EFFECT))(*own, *lands, after)
    return out[0], out[1], out[2:2 + nw], out[2 + nw:2 + 2 * nw], out[-1]


def _gather_late_wait(send_sems, recv_sems, own, lands, after, *, name):
    nw = len(own)

    def body(*refs):
        srcs, lands_in, send_sems, recv_sems = refs[:nw], refs[nw:2 * nw], refs[2 * nw], refs[2 * nw + 1]
        x, y, c, _, chips = _place()
        for cp in _late_copies(srcs, lands_in, send_sems, recv_sems):
            cp.wait_send()
        for w, (src, land) in enumerate(zip(srcs, lands_in)):
            for r, chip in enumerate(chips):
                for core in range(2):
                    pltpu.make_async_remote_copy(
                        src_ref=src.at[c], dst_ref=land.at[2 * chip[0] + chip[1], core], send_sem=send_sems.at[6 * w + 2 * r + core],
                        recv_sem=recv_sems.at[6 * w + 2 * r + core], device_id=(*chip, core), device_id_type=MESH).wait_recv()

    out = pl.pallas_call(
        body, name=name, out_shape=(*[pltpu.HBM(s.shape, s.dtype) for s in own], *[pltpu.HBM(s.shape, s.dtype) for s in lands]),
        in_specs=[HBM_SPEC] * (2 * nw) + [SEM_SPEC, SEM_SPEC, ANY], out_specs=tuple([HBM_SPEC] * (2 * nw)),
        input_output_aliases={i: i for i in range(2 * nw)},
        compiler_params=pltpu.CompilerParams(has_side_effects=DATAFLOW_EFFECT))(*own, *lands, send_sems, recv_sems, after)
    return out[:nw], out[nw:]


def _direct_reduce_copies(srcs, lands, send_sems, recv_sems):
    x, y, c, _, _ = _place()
    cps = []
    for w, (src, land) in enumerate(zip(srcs, lands)):
        for rel in range(1, N_DEV):
            tx, ty, tc = (1 - x if rel & 4 else x), (1 - y if rel & 2 else y), (1 - c if rel & 1 else c)
            cps.append(pltpu.make_async_remote_copy(
                src_ref=src.at[2 * tx + ty, tc], dst_ref=land.at[rel - 1], send_sem=send_sems.at[7 * w + rel - 1],
                recv_sem=recv_sems.at[7 * w + rel - 1], device_id=(tx, ty, tc), device_id_type=MESH))
    return cps


def _direct_reduce_start(grads, *, name):
    nw = len(grads)

    def body(*refs):
        srcs, lands, send_sems, recv_sems, token = refs[:nw], refs[nw:2 * nw], refs[2 * nw], refs[2 * nw + 1], refs[-1]
        for cp in _direct_reduce_copies(srcs, lands, send_sems, recv_sems):
            cp.start()
        token[...] = jnp.zeros_like(token)

    lands = [pltpu.with_memory_space_constraint(lax.empty((N_DEV - 1, *g.shape[2:]), g.dtype), pltpu.HBM) for g in grads]
    grads = [pltpu.with_memory_space_constraint(g, pltpu.HBM) for g in grads]
    out = pl.pallas_call(
        body, name=name,
        out_shape=(pltpu.SemaphoreType.DMA((7 * nw,)), pltpu.SemaphoreType.DMA((7 * nw,)),
                   *[pltpu.HBM(g.shape, g.dtype) for g in grads], *[pltpu.HBM(t.shape, t.dtype) for t in lands],
                   jax.ShapeDtypeStruct((8, 128), F32)),
        in_specs=[HBM_SPEC] * (2 * nw), out_specs=(SEM_SPEC, SEM_SPEC, *[HBM_SPEC] * (2 * nw), VMEM_SPEC),
        input_output_aliases={i: 2 + i for i in range(2 * nw)},
        compiler_params=pltpu.CompilerParams(has_side_effects=DATAFLOW_EFFECT))(*grads, *lands)
    return out[0], out[1], out[2:2 + nw], out[2 + nw:2 + 2 * nw], out[-1]


def _direct_reduce_wait(send_sems, recv_sems, grads, lands, after, *, name):
    nw = len(grads)

    def body(*refs):
        srcs, lands_in, send_sems, recv_sems = refs[:nw], refs[nw:2 * nw], refs[2 * nw], refs[2 * nw + 1]
        cps = _direct_reduce_copies(srcs, lands_in, send_sems, recv_sems)
        for cp in cps:
            cp.wait_send()
        for cp in cps:
            cp.wait_recv()

    out = pl.pallas_call(
        body, name=name, out_shape=(*[pltpu.HBM(g.shape, g.dtype) for g in grads], *[pltpu.HBM(t.shape, t.dtype) for t in lands]),
        in_specs=[HBM_SPEC] * (2 * nw) + [SEM_SPEC, SEM_SPEC, ANY], out_specs=tuple([HBM_SPEC] * (2 * nw)),
        input_output_aliases={i: i for i in range(2 * nw)},
        compiler_params=pltpu.CompilerParams(has_side_effects=DATAFLOW_EFFECT))(*grads, *lands, send_sems, recv_sems, after)
    return out[nw:]


def _direct_reduce_add(grad, landed, chip, core, *, name):
    _, r, n = grad.shape
    half = r // 2
    tr = _row_tile(half)
    nb = half // tr

    def body(chip_ref, core_ref, g_ref, t_ref, o_ref):
        acc = g_ref[0]
        for k in range(N_DEV - 1):
            acc = acc + t_ref[k].astype(F32)
        o_ref[...] = acc

    return pl.pallas_call(
        body, name=name,
        grid_spec=pltpu.PrefetchScalarGridSpec(
            num_scalar_prefetch=2, grid=(nb,),
            in_specs=[pl.BlockSpec((1, tr, n), lambda i, chip_ref, core_ref: (chip_ref[0], core_ref[0] * nb + i, 0)),
                      pl.BlockSpec((N_DEV - 1, tr, n), lambda i, chip_ref, core_ref: (0, i, 0))],
            out_specs=pl.BlockSpec((tr, n), lambda i, chip_ref, core_ref: (i, 0))),
        out_shape=jax.ShapeDtypeStruct((half, n), F32), compiler_params=_params("parallel"))(chip, core, grad, landed)


def _pair_exchange_halves(grads, *, name):
    nw = len(grads)

    def body(*refs):
        srcs, outs, (send_sems, recv_sems) = refs[:nw], refs[nw:2 * nw], refs[2 * nw:]
        _, _, c, sibling, _ = _place()
        cps = []
        for w, (src_ref, out_ref) in enumerate(zip(srcs, outs)):
            cps.append(pltpu.make_async_remote_copy(
                src_ref=src_ref.at[:, 1 - c], dst_ref=out_ref, send_sem=send_sems.at[w],
                recv_sem=recv_sems.at[w], device_id=sibling, device_id_type=MESH))
            cps[-1].start()
        for cp in cps:
            cp.wait()

    return pl.pallas_call(
        body, name=name, out_shape=[jax.ShapeDtypeStruct((N_CHIP, *g.shape[2:]), g.dtype) for g in grads],
        in_specs=[ANY] * nw, out_specs=[ANY] * nw,
        scratch_shapes=[pltpu.SemaphoreType.DMA((nw,)), pltpu.SemaphoreType.DMA((nw,))])(*grads)


def _chip_scatter(pairs, *, name):
    nw = len(pairs)

    def body(*refs):
        srcs, outs, (send_sems, recv_sems) = refs[:nw], refs[nw:2 * nw], refs[2 * nw:]
        _, _, c, _, chips = _place()
        cps = []
        for w, (p_ref, out_ref) in enumerate(zip(srcs, outs)):
            for k, chip in enumerate(chips):
                cps.append(pltpu.make_async_remote_copy(
                    src_ref=p_ref.at[2 * chip[0] + chip[1]], dst_ref=out_ref.at[k], send_sem=send_sems.at[3 * w + k],
                    recv_sem=recv_sems.at[3 * w + k], device_id=(*chip, c), device_id_type=MESH))
                cps[-1].start()
        for cp in cps:
            cp.wait()

    return pl.pallas_call(
        body, name=name, out_shape=[jax.ShapeDtypeStruct((3, *p.shape[1:]), p.dtype) for p in pairs],
        in_specs=[ANY] * nw, out_specs=[ANY] * nw,
        scratch_shapes=[pltpu.SemaphoreType.DMA((3 * nw,)), pltpu.SemaphoreType.DMA((3 * nw,))])(*pairs)


def _share_halves(halves, *, name):
    nw = len(halves)

    def body(*refs):
        srcs, outs, (send_sems, recv_sems) = refs[:nw], refs[nw:2 * nw], refs[2 * nw:]
        _, _, _, sibling, _ = _place()
        cps = [pltpu.make_async_remote_copy(src_ref=src_ref, dst_ref=out_ref, send_sem=send_sems.at[w], recv_sem=recv_sems.at[w],
                                            device_id=sibling, device_id_type=MESH)
               for w, (src_ref, out_ref) in enumerate(zip(srcs, outs))]
        for cp in cps:
            cp.start()
        for cp in cps:
            cp.wait()

    return pl.pallas_call(
        body, name=name, out_shape=[jax.ShapeDtypeStruct(h.shape, h.dtype) for h in halves],
        in_specs=[ANY] * nw, out_specs=[ANY] * nw,
        scratch_shapes=[pltpu.SemaphoreType.DMA((nw,)), pltpu.SemaphoreType.DMA((nw,))])(*halves)


def _row_tile(rows, limit=256):
    return next(t for t in range(limit, 15, -16) if rows % t == 0)


def _pair_add(grad, got, core, *, name):
    _, r, n = grad.shape
    half = r // 2
    tr = _row_tile(half)
    nb = half // tr

    def body(core_ref, g_ref, t_ref, f_ref, b_ref):
        acc = g_ref[...] + t_ref[...]
        f_ref[...] = acc
        b_ref[...] = acc.astype(BF16)

    blk = pl.BlockSpec((1, tr, n), lambda j, i, core_ref: (j, i, 0))
    mine = pl.BlockSpec((1, tr, n), lambda j, i, core_ref: (j, core_ref[0] * nb + i, 0))
    return pl.pallas_call(
        body, name=name,
        grid_spec=pltpu.PrefetchScalarGridSpec(num_scalar_prefetch=1, grid=(N_CHIP, nb), in_specs=[mine, blk], out_specs=[blk, blk]),
        out_shape=[jax.ShapeDtypeStruct((N_CHIP, half, n), F32), jax.ShapeDtypeStruct((N_CHIP, half, n), BF16)],
        compiler_params=_params("parallel", "parallel"))(core, grad, got)


def _chip_add(pair, theirs, chip, *, name):
    _, h, n = pair.shape
    tr = _row_tile(h)

    def body(chip_ref, p_ref, t_ref, o_ref):
        o_ref[...] = ((p_ref[0] + t_ref[0].astype(F32)) + t_ref[1].astype(F32)) + t_ref[2].astype(F32)

    return pl.pallas_call(
        body, name=name,
        grid_spec=pltpu.PrefetchScalarGridSpec(
            num_scalar_prefetch=1, grid=(h // tr,),
            in_specs=[pl.BlockSpec((1, tr, n), lambda i, chip_ref: (chip_ref[0], i, 0)),
                      pl.BlockSpec((3, tr, n), lambda i, chip_ref: (0, i, 0))],
            out_specs=pl.BlockSpec((tr, n), lambda i, chip_ref: (i, 0))),
        out_shape=jax.ShapeDtypeStruct((h, n), F32), compiler_params=_params("parallel"))(chip, pair, theirs)


def _sum_devices(gathered, *, name):
    _, m, n = gathered.shape

    def body(g_ref, tot_ref, loss_ref):
        tot = g_ref[0]
        for dev in range(1, N_DEV):
            tot = tot + g_ref[dev]
        tot_ref[...] = tot
        loss_ref[...] = jnp.full((8, n), (0.5 / D_MODEL) * jnp.sum(tot[0:8]), F32)

    return pl.pallas_call(body, name=name, in_specs=[VMEM_SPEC], out_specs=[VMEM_SPEC, VMEM_SPEC],
                          out_shape=[jax.ShapeDtypeStruct((m, n), F32), jax.ShapeDtypeStruct((8, n), F32)])(gathered)


def _ada_mod(cond_all, w_ada_shard, *, name):
    tn = 512

    def body(a_ref, b_ref, o_ref):
        o_ref[...] = _nn(a_ref[...], b_ref[...], precision=HIGHEST)

    return pl.pallas_call(
        body, name=name, grid=(w_ada_shard.shape[1] // tn,),
        in_specs=[pl.BlockSpec(cond_all.shape, lambda j: (0, 0)), pl.BlockSpec((D_MODEL, tn), lambda j: (0, j))],
        out_specs=pl.BlockSpec((N_DEV, tn), lambda j: (0, j)),
        out_shape=jax.ShapeDtypeStruct((N_DEV, w_ada_shard.shape[1]), F32), compiler_params=_params("parallel"))(cond_all, w_ada_shard)


def _ada_grad(cond_all, dmod_cols, *, name):
    tm = 256

    def body(a_ref, b_ref, o_ref):
        o_ref[...] = lax.dot_general(a_ref[...], b_ref[...], (((0,), (0,)), ((), ())), precision=HIGHEST,
                                     preferred_element_type=F32)

    return pl.pallas_call(
        body, name=name, grid=(D_MODEL // tm,),
        in_specs=[pl.BlockSpec((N_DEV, tm), lambda i: (0, i)), pl.BlockSpec(dmod_cols.shape, lambda i: (0, 0))],
        out_specs=pl.BlockSpec((tm, dmod_cols.shape[1]), lambda i: (i, 0)),
        out_shape=jax.ShapeDtypeStruct((D_MODEL, dmod_cols.shape[1]), F32), compiler_params=_params("parallel"))(cond_all, dmod_cols)


def _silu_rows(c8, *, name):
    def body(c_ref, o_ref):
        cv = c_ref[...]
        o_ref[...] = cv * _sigmoid(cv)

    return pl.pallas_call(body, name=name, in_specs=[VMEM_SPEC], out_specs=VMEM_SPEC,
                          out_shape=jax.ShapeDtypeStruct(c8.shape, F32))(c8)


def _rows128(t, rows=None):
    flat = t.reshape(-1, 128)
    return flat if rows is None else jnp.pad(flat, ((0, rows - flat.shape[0]), (0, 0)))


def _from_col_shards(shards, r, n):
    return shards.reshape(N_CHIP, r, n).transpose(1, 0, 2).reshape(r, N_CHIP * n)


def kernel(x, c, w_ada, b_ada, norm1_g, w_in, gla_w_gate, gla_b_gate, gla_norm_g, q_norm_g, k_norm_g, w_out, norm2_g, w_up, conv_w, conv_b, w_down, loss_target, m_w_ada, m_b_ada, m_norm1_g, m_w_in, m_gla_w_gate, m_gla_b_gate, m_gla_norm_g, m_q_norm_g, m_k_norm_g, m_w_out, m_norm2_g, m_w_up, m_conv_w, m_conv_b, m_w_down, v_w_ada, v_b_ada, v_norm1_g, v_w_in, v_gla_w_gate, v_gla_b_gate, v_gla_norm_g, v_q_norm_g, v_k_norm_g, v_w_out, v_norm2_g, v_w_up, v_conv_w, v_conv_b, v_w_down):
    d = D_MODEL
    ax, ay, ac = lax.axis_index("x"), lax.axis_index("y"), lax.axis_index("c")
    chip, dev = 2 * ax + ay, 4 * ax + 2 * ay + ac

    cond = _silu_rows(jnp.broadcast_to(c, (8, d)), name="cond_silu")[0:1]
    small_in = jnp.concatenate([_rows128(cond), _rows128(conv_w[0]), _rows128(gla_w_gate[0])], axis=0)
    small_in = _rows128(small_in, 56)
    got = _all_gather_small(small_in, name="gather_small").reshape(N_DEV, 56, 128)
    cond_all = got[:, 0:8].reshape(N_DEV, d)
    conv_w_full = _from_col_shards(got[0::2, 8:41].reshape(N_CHIP, 3 * 1408 // 128, 128), 3, 1408)
    gate_full = _from_col_shards(got[0::2, 41:49].reshape(N_CHIP, 16 * 64 // 128, 128), GLA_GATE_RANK, 64)
    mod_part = _ada_mod(cond_all, w_ada[0], name="ada_mod")
    mod_got = _all_gather_small(_rows128(mod_part), name="gather_mod").reshape(N_DEV, N_DEV, 1536)
    mod_all = mod_got[0::2].transpose(1, 0, 2).reshape(N_DEV, 6 * d) + b_ada
    mod = lax.dynamic_slice_in_dim(mod_all, dev, 1, axis=0)

    own = [w[0].astype(BF16).reshape(2, w.shape[1] // 2, w.shape[2]) for w in (w_in, w_out, w_up, w_down)]
    with_own = lambda got, mine: [lax.dynamic_update_index_in_dim(t, o, chip, 0) for t, o in zip(got, mine)]
    got_in, got_out = with_own(_gather_weight_shards(own[:2], name="gather_weights"), own[:2])
    w_in_full = got_in.reshape(N_CHIP, d, 772).transpose(1, 0, 2).reshape(d, N_CHIP * 772)
    w_out_full = got_out.reshape(d, d)
    exchanged = mod_all[0:1, 0:1] + got_in[0, 0, 0:1, 0:1].astype(F32)
    send_sems, recv_sems, own_thru, lands, token = _gather_late_start(own[2:], exchanged, name="gather_late_start")
    mod = mod + token[0:1, 0:1]

    def ffn_weights(after):
        mine, landed = _gather_late_wait(send_sems, recv_sems, own_thru, lands, after, name="gather_late_wait")
        got_up, got_down = with_own(landed, mine)
        return got_up.reshape(N_CHIP, d, 1408).transpose(1, 0, 2).reshape(d, 2 * D_FF), got_down.reshape(D_FF, d)

    late_reduce = []

    def ffn_grads_ready(g_wup_b, g_wdown_b):
        halves_of = lambda g: g.reshape(N_CHIP, 2, g.shape[-2] // 2, g.shape[-1])
        late_reduce.extend(_direct_reduce_start([halves_of(g_wup_b), halves_of(g_wdown_b.reshape(N_CHIP, D_FF // N_CHIP, d))],
                                                name="reduce_late_start"))
        return late_reduce[4]

    err2, grad_x, (g_wi, g_wo, g_wup, g_wdown), small = _local_step(
        x[0], loss_target[0], mod, _in_proj_layout(w_in_full), w_out_full, ffn_weights, ffn_grads_ready, conv_w_full, conv_b,
        _gate_layout(gate_full), gla_b_gate, gla_norm_g, q_norm_g, k_norm_g, norm1_g, norm2_g)

    pieces = [err2[0], small["dmod"], small["norm1_g"], small["norm2_g"], small["gla_w_gate"].reshape(-1), small["gla_b_gate"],
              small["gla_norm_g"], small["q_norm_g"], small["k_norm_g"], small["conv_w"].reshape(-1), small["conv_b"]]
    sizes = [p.shape[0] for p in pieces]
    at = [sum(sizes[:i]) for i in range(len(sizes) + 1)]
    vec = _rows128(jnp.concatenate(pieces), 288)
    got = _all_gather_small(vec, name="gather_grads").reshape(N_DEV, 288, 128)
    total, loss8 = _sum_devices(got, name="sum_devices")
    total = total.reshape(-1)
    seg = lambda i: total[at[i]:at[i + 1]]
    dmod_all = got.reshape(N_DEV, -1)[:, at[1]:at[2]]
    g_small = dict(
        b_ada=seg(1)[None], norm1_g=seg(2)[None], norm2_g=seg(3)[None],
        gla_w_gate=lax.dynamic_slice_in_dim(seg(4).reshape(GLA_GATE_RANK, 256), chip * 64, 64, axis=1),
        gla_b_gate=seg(5)[None], gla_norm_g=seg(6)[None], q_norm_g=seg(7)[None], k_norm_g=seg(8)[None],
        conv_w=lax.dynamic_slice_in_dim(seg(9).reshape(3, 2 * D_FF), chip * 1408, 1408, axis=1), conv_b=seg(10)[None])
    dmod_cols = lax.dynamic_slice_in_dim(dmod_all.reshape(N_DEV, 6 * d), chip * 1536, 1536, axis=1)
    g_w_ada = _ada_grad(cond_all, dmod_cols, name="ada_grad")

    tags = ("w_in", "w_out")
    g_parts = [_in_proj_grad_layout(g_wi).reshape(d, N_CHIP, 772).transpose(1, 0, 2), g_wo.reshape(N_CHIP, d // N_CHIP, d)]
    core_id, chip_id = jnp.reshape(ac, (1,)).astype(jnp.int32), jnp.reshape(chip, (1,)).astype(jnp.int32)
    got = _pair_exchange_halves([g.reshape(N_CHIP, 2, g.shape[1] // 2, g.shape[2]) for g in g_parts], name="reduce_pair")
    pairs = [_pair_add(g, t, core_id, name=f"reduce_pair_add_{tag}") for g, t, tag in zip(g_parts, got, tags)]
    theirs = _chip_scatter([pb for _, pb in pairs], name="reduce_chips")
    summed = [_chip_add(pf, t, chip_id, name=f"reduce_chips_add_{tag}") for (pf, _), t, tag in zip(pairs, theirs, tags)]
    landed = _direct_reduce_wait(*late_reduce[:4], grad_x, name="reduce_late_wait")
    summed += [_direct_reduce_add(g, t, chip_id, core_id, name=f"reduce_late_add_{tag}")
               for g, t, tag in zip((g_wup, g_wdown.reshape(N_CHIP, D_FF // N_CHIP, d)), landed, ("w_up", "w_down"))]
    others = _share_halves(summed, name="share_pair")
    g_big = [jnp.concatenate([jnp.where(ac == 0, mine, other), jnp.where(ac == 0, other, mine)], axis=0)
             for mine, other in zip(summed, others)]

    grads = dict(w_ada=g_w_ada, w_in=g_big[0], w_out=g_big[1], w_up=g_big[2], w_down=g_big[3], **g_small)
    names = ["w_ada", "b_ada", "norm1_g", "w_in", "gla_w_gate", "gla_b_gate", "gla_norm_g", "q_norm_g", "k_norm_g", "w_out",
             "norm2_g", "w_up", "conv_w", "conv_b", "w_down"]
    ws = dict(w_ada=w_ada, b_ada=b_ada, norm1_g=norm1_g, w_in=w_in, gla_w_gate=gla_w_gate, gla_b_gate=gla_b_gate,
              gla_norm_g=gla_norm_g, q_norm_g=q_norm_g, k_norm_g=k_norm_g, w_out=w_out, norm2_g=norm2_g, w_up=w_up,
              conv_w=conv_w, conv_b=conv_b, w_down=w_down)
    ms = dict(w_ada=m_w_ada, b_ada=m_b_ada, norm1_g=m_norm1_g, w_in=m_w_in, gla_w_gate=m_gla_w_gate, gla_b_gate=m_gla_b_gate,
              gla_norm_g=m_gla_norm_g, q_norm_g=m_q_norm_g, k_norm_g=m_k_norm_g, w_out=m_w_out, norm2_g=m_norm2_g, w_up=m_w_up,
              conv_w=m_conv_w, conv_b=m_conv_b, w_down=m_w_down)
    vs = dict(w_ada=v_w_ada, b_ada=v_b_ada, norm1_g=v_norm1_g, w_in=v_w_in, gla_w_gate=v_gla_w_gate, gla_b_gate=v_gla_b_gate,
              gla_norm_g=v_gla_norm_g, q_norm_g=v_q_norm_g, k_norm_g=v_k_norm_g, w_out=v_w_out, norm2_g=v_norm2_g, w_up=v_w_up,
              conv_w=v_conv_w, conv_b=v_conv_b, w_down=v_w_down)
    g_out, d_out, m_out, v_out = [], [], [], []
    for nm in names:
        w2 = ws[nm].reshape(ws[nm].shape[-2:])
        g2 = grads[nm].reshape(w2.shape)
        dl, mn, vn = _adamw(w2, g2, ms[nm].reshape(w2.shape), vs[nm].reshape(w2.shape), name=f"adamw_{nm}")
        shape = ws[nm].shape
        g_out.append(g2.reshape(shape))
        d_out.append(dl.reshape(shape))
        m_out.append(mn.reshape(shape))
        v_out.append(vn.reshape(shape))
    return (loss8[0, 0], grad_x[None], *g_out, *d_out, *m_out, *v_out)
```

```python
import functools

import jax
import jax.numpy as jnp
from jax import lax
from jax.experimental import pallas as pl
from jax.experimental.pallas import tpu as pltpu

F32, BF16 = jnp.float32, jnp.bfloat16
HIGHEST = lax.Precision.HIGHEST
MESH = pl.DeviceIdType.MESH

D_MODEL = 1024
GLA_CHUNK = 64
GLA_GATE_TAU = 16.0
GLA_GATE_RANK = 16
HEAD_LANES = 128
ATTN_BLOCK = 128
DILATIONS = (1, 4, 16)
ALIBI_SLOPES = tuple(2.0 ** (-(h + 1)) for h in range(8))
D_FF = 2816
EPS = 1e-6
C_GQ, C_GK, C_GV, C_GR, C_AQ, C_AK, C_AV, C_LR, PROJ_W = 0, 256, 512, 1024, 1536, 2048, 2560, 3072, 3200
ADAM_LR, ADAM_B1, ADAM_B2, ADAM_EPS, ADAM_WD, ADAM_STEP = 0.001, 0.9, 0.999, 1e-08, 0.01, 10
VMEM_LIMIT_BYTES = 56 * 1024 * 1024
ROW_TILE = 256


def _params(*sem):
    return pltpu.CompilerParams(dimension_semantics=sem or None, vmem_limit_bytes=VMEM_LIMIT_BYTES)


def _nt(a, b):
    return lax.dot_general(a, b, (((1,), (1,)), ((), ())), preferred_element_type=F32)


def _tn(a, b):
    return lax.dot_general(a, b, (((0,), (0,)), ((), ())), preferred_element_type=F32)


def _nn(a, b, precision=None):
    return jnp.dot(a, b, preferred_element_type=F32, precision=precision)


def _split3(v):
    hi = v.astype(BF16)
    rest = v - hi.astype(F32)
    mid = rest.astype(BF16)
    return hi, mid, (rest - mid.astype(F32)).astype(BF16)


def _sum_right(v, ones):
    hi, mid, lo = _split3(v)
    return (_nn(lo, ones) + _nn(mid, ones)) + _nn(hi, ones)


def _sum_left(ones, v):
    hi, mid, lo = _split3(v)
    return (_nn(ones, lo) + _nn(ones, mid)) + _nn(ones, hi)


def _fold8(v):
    return v.reshape(v.shape[0] // 8, 8, v.shape[1]).sum(axis=0)


def _spread_total(ref):
    t = ref[...]
    ref[...] = jnp.broadcast_to(jnp.sum(t, axis=-2, keepdims=True), t.shape)


def _sigmoid(x):
    return 1.0 / (1.0 + jnp.exp(-x))


def _mm(a, b, *, ta=False, tb=False, out_dtype=F32, tm, tn, tk, shard_cols=False, also_bf16=False, name):
    (k_a, m) = a.shape if ta else a.shape[::-1]
    (k_b, n) = b.shape[::-1] if tb else b.shape
    assert k_a == k_b and m % tm == 0 and n % tn == 0 and k_a % tk == 0, (name, a.shape, b.shape)
    nk = k_a // tk
    assert nk == 1 or out_dtype == F32, name
    dims = (((0 if ta else 1,), (1 if tb else 0,)), ((), ()))

    def body(a_ref, b_ref, o_ref, *rounded):
        k = pl.program_id(2)
        part = lax.dot_general(a_ref[...].astype(BF16), b_ref[...].astype(BF16), dims, preferred_element_type=F32)
        if nk == 1:
            o_ref[...] = part.astype(out_dtype)
        else:
            @pl.when(k == 0)
            def _():
                o_ref[...] = part

            @pl.when(k > 0)
            def _():
                o_ref[...] += part

        if also_bf16:
            @pl.when(k == nk - 1)
            def _():
                rounded[0][...] = o_ref[...].astype(BF16)

    a_spec = pl.BlockSpec((tk, tm), lambda i, j, k: (k, i)) if ta else pl.BlockSpec((tm, tk), lambda i, j, k: (i, k))
    b_spec = pl.BlockSpec((tn, tk), lambda i, j, k: (j, k)) if tb else pl.BlockSpec((tk, tn), lambda i, j, k: (k, j))
    if shard_cols:
        o_spec, o_shape = pl.BlockSpec((None, tm, tn), lambda i, j, k: (j, i, 0)), (n // tn, m, tn)
    else:
        o_spec, o_shape = pl.BlockSpec((tm, tn), lambda i, j, k: (i, j)), (m, n)
    shapes = [jax.ShapeDtypeStruct(o_shape, out_dtype)] + ([jax.ShapeDtypeStruct(o_shape, BF16)] if also_bf16 else [])
    out = pl.pallas_call(
        body, name=name, grid=(m // tm, n // tn, nk), in_specs=[a_spec, b_spec], out_specs=[o_spec] * len(shapes),
        out_shape=shapes, compiler_params=_params("parallel", "parallel", "arbitrary"))(a, b)
    return out if also_bf16 else out[0]


def _norm_mod_fwd(x, branch, gate, gain, scale, shift, *, name):
    s, d = x.shape
    tm = ROW_TILE
    has_branch = branch is not None

    def body(*refs):
        if has_branch:
            x_ref, br_ref, gate_ref, gain_ref, sc_ref, sh_ref, x1_ref, h_ref, ht_ref = refs
            xv = x_ref[...] + gate_ref[...] * br_ref[...]
            x1_ref[...] = xv
        else:
            x_ref, gain_ref, sc_ref, sh_ref, h_ref, ht_ref = refs
            xv = x_ref[...]
        r = lax.rsqrt(jnp.mean(xv * xv, axis=-1, keepdims=True) + EPS)
        h = (xv * r) * gain_ref[...] * (1.0 + sc_ref[...]) + sh_ref[...]
        h_ref[...] = h.astype(BF16)
        ht_ref[...] = h.T.astype(BF16)

    row = pl.BlockSpec((tm, d), lambda i: (i, 0))
    col = pl.BlockSpec((d, tm), lambda i: (0, i))
    vec = pl.BlockSpec((1, d), lambda i: (0, 0))
    h_shapes = [jax.ShapeDtypeStruct((s, d), BF16), jax.ShapeDtypeStruct((d, s), BF16)]
    if has_branch:
        return pl.pallas_call(
            body, name=name, grid=(s // tm,), in_specs=[row, row, vec, vec, vec, vec], out_specs=[row, row, col],
            out_shape=[jax.ShapeDtypeStruct((s, d), F32)] + h_shapes,
            compiler_params=_params("parallel"))(x, branch, gate, gain, scale, shift)
    h, ht = pl.pallas_call(
        body, name=name, grid=(s // tm,), in_specs=[row, vec, vec, vec], out_specs=[row, col],
        out_shape=h_shapes, compiler_params=_params("parallel"))(x, gain, scale, shift)
    return x, h, ht


def _norm_mod_bwd(x, dh, dres, gain, scale, branch, gate, *, name):
    s, d = x.shape
    tm = ROW_TILE
    has_branch = branch is not None

    def body(*refs):
        if has_branch:
            x_ref, dh_ref, dres_ref, gain_ref, sc_ref, br_ref, gate_ref, dx_ref, dbr_ref, sums_ref = refs
        else:
            x_ref, dh_ref, dres_ref, gain_ref, sc_ref, dx_ref, sums_ref = refs
        i = pl.program_id(0)

        @pl.when(i == 0)
        def _():
            sums_ref[...] = jnp.zeros_like(sums_ref)

        xv, dhv = x_ref[...], dh_ref[...]
        r = lax.rsqrt(jnp.mean(xv * xv, axis=-1, keepdims=True) + EPS)
        xn = xv * r
        dxn = dhv * (gain_ref[...] * (1.0 + sc_ref[...]))
        dx = dres_ref[...] + r * (dxn - xn * jnp.mean(dxn * xn, axis=-1, keepdims=True))
        dx_ref[...] = dx
        sums_ref[0] += _fold8(dhv * xn)
        sums_ref[1] += _fold8(dhv)
        if has_branch:
            dbr_ref[...] = (gate_ref[...] * dx).astype(BF16)
            sums_ref[2] += _fold8(dx * br_ref[...])

        @pl.when(i == s // tm - 1)
        def _():
            _spread_total(sums_ref)

    row = pl.BlockSpec((tm, d), lambda i: (i, 0))
    vec = pl.BlockSpec((1, d), lambda i: (0, 0))
    sums = pl.BlockSpec((3, 8, d), lambda i: (0, 0, 0))
    sums_shape = jax.ShapeDtypeStruct((3, 8, d), F32)
    if has_branch:
        return pl.pallas_call(
            body, name=name, grid=(s // tm,), in_specs=[row, row, row, vec, vec, row, vec], out_specs=[row, row, sums],
            out_shape=[jax.ShapeDtypeStruct((s, d), F32), jax.ShapeDtypeStruct((s, d), BF16), sums_shape],
            compiler_params=_params("arbitrary"))(x, dh, dres, gain, scale, branch, gate)
    dx, sm = pl.pallas_call(
        body, name=name, grid=(s // tm,), in_specs=[row, row, row, vec, vec], out_specs=[row, sums],
        out_shape=[jax.ShapeDtypeStruct((s, d), F32), sums_shape],
        compiler_params=_params("arbitrary"))(x, dh, dres, gain, scale)
    return dx, None, sm


GLA_ROWS = 256


def _gla_block_setup(lr_ref, wg_ref, bg_ref):
    t, c = GLA_ROWS, GLA_CHUNK
    ri = lax.broadcasted_iota(jnp.int32, (t, t), 0)
    ci = lax.broadcasted_iota(jnp.int32, (t, t), 1)
    same = (ri // c) == (ci // c)
    causal, upper = same & (ci <= ri), same & (ci >= ri)
    z = _nn(lr_ref[...].astype(BF16), wg_ref[...]) + bg_ref[...]
    g = (jnp.minimum(z, 0.0) - jnp.log(1.0 + jnp.exp(-jnp.abs(z)))) * (1.0 / GLA_GATE_TAU)
    hi, mid, lo = _split3(g)
    total = lambda ones: (_nn(ones, lo) + _nn(ones, mid)) + _nn(ones, hi)
    return z, total(causal.astype(BF16)), total(same.astype(BF16)), causal, upper


def _chunks(t):
    return [t[i * GLA_CHUNK:(i + 1) * GLA_CHUNK] for i in range(GLA_ROWS // GLA_CHUNK)]


def _gla_fwd(proj, wg, bg, gn, *, name):
    s = proj.shape[0]
    tb, c = GLA_ROWS, GLA_CHUNK
    cb = tb // c

    def body(q_ref, k_ref, v_ref, r_ref, lr_ref, wg_ref, bg_ref, gn_ref, o_ref, y_ref, st_ref, state):
        i = pl.program_id(0)

        @pl.when(i == 0)
        def _():
            state[...] = jnp.zeros_like(state)

        low = lax.broadcasted_iota(jnp.int32, (tb, HEAD_LANES), 1) < 64
        masks = (low, jnp.logical_not(low))
        _, b, b_end, causal, _ = _gla_block_setup(lr_ref, wg_ref, bg_ref)
        for p in range(2):
            cols = pl.ds(p * HEAD_LANES, HEAD_LANES)
            bp, bep = (t[:, p * HEAD_LANES:(p + 1) * HEAD_LANES] for t in (b, b_end))
            k = k_ref[:, cols]
            q_in = q_ref[:, cols] * 0.125 * jnp.exp(bp)
            k_out = (k * jnp.exp(-bp)).astype(BF16)
            k_end = k * jnp.exp(bep - bp)
            qms = [jnp.where(m, q_in, 0.0).astype(BF16) for m in masks]
            kes = [jnp.where(m, k_end, 0.0).astype(BF16) for m in masks]
            vs = [v_ref[:, pl.ds((2 * p + e) * HEAD_LANES, HEAD_LANES)].astype(BF16) for e in range(2)]
            grow = [_tn(v0, k0) + _tn(v1, k1) for v0, k0, v1, k1 in zip(_chunks(vs[0]), _chunks(kes[0]), _chunks(vs[1]), _chunks(kes[1]))]
            st, entering = state[p], []
            for ch in range(cb):
                entering.append(st)
                st_ref[ch, p] = st
                st = st * jnp.exp(bep[ch * c:ch * c + 1, :]) + grow[ch]
            state[p] = st
            for e in range(2):
                hc = pl.ds((2 * p + e) * HEAD_LANES, HEAD_LANES)
                a = jnp.where(causal, _nt(qms[e], k_out), 0.0).astype(BF16)
                carried = jnp.concatenate([_nt(qc, sc.astype(BF16)) for qc, sc in zip(_chunks(qms[e]), entering)], axis=0)
                o = _nn(a, vs[e]) + carried
                o_ref[:, hc] = o
                rr = r_ref[:, hc]
                on = o * lax.rsqrt(jnp.mean(o * o, axis=-1, keepdims=True) + EPS)
                y_ref[:, hc] = (on * gn_ref[...] * (rr * _sigmoid(rr))).astype(BF16)

    def col(width, at):
        return pl.BlockSpec((tb, width), lambda i: (i, at // width))

    full = lambda shape: pl.BlockSpec(shape, lambda i: tuple(0 for _ in shape))
    return pl.pallas_call(
        body, name=name, grid=(s // tb,),
        in_specs=[col(256, C_GQ), col(256, C_GK), col(512, C_GV), col(512, C_GR), col(128, C_LR),
                  full((HEAD_LANES, 256)), full((1, 256)), full((1, HEAD_LANES))],
        out_specs=[pl.BlockSpec((tb, 512), lambda i: (i, 0)), pl.BlockSpec((tb, 512), lambda i: (i, 0)),
                   pl.BlockSpec((cb, 2, HEAD_LANES, HEAD_LANES), lambda i: (i, 0, 0, 0))],
        out_shape=[jax.ShapeDtypeStruct((s, 512), F32), jax.ShapeDtypeStruct((s, 512), BF16),
                   jax.ShapeDtypeStruct((s // c, 2, HEAD_LANES, HEAD_LANES), F32)],
        scratch_shapes=[pltpu.VMEM((2, HEAD_LANES, HEAD_LANES), F32)],
        compiler_params=_params("arbitrary"))(proj, proj, proj, proj, proj, wg, bg, gn)


def _gla_bwd(proj, wg, bg, gn, o_raw, states, dmixed, *, name):
    s = proj.shape[0]
    tb, c = GLA_ROWS, GLA_CHUNK
    cb = tb // c
    nblk, nch = s // tb, s // c

    def body(q_ref, k_ref, v_ref, r_ref, lr_ref, wg_ref, bg_ref, gn_ref, o_ref, st_ref, stn_ref, dy_ref,
             dq_ref, dk_ref, dv_ref, dr_ref, dlr_ref, gwg_ref, sums_ref, dstate):
        i = pl.program_id(0)

        @pl.when(i == 0)
        def _():
            dstate[...] = jnp.zeros_like(dstate)
            gwg_ref[...] = jnp.zeros_like(gwg_ref)
            sums_ref[...] = jnp.zeros_like(sums_ref)

        low = lax.broadcasted_iota(jnp.int32, (tb, HEAD_LANES), 1) < 64
        masks = (low, jnp.logical_not(low))
        z, b, b_end, causal, upper = _gla_block_setup(lr_ref, wg_ref, bg_ref)
        lr_b = lr_ref[...].astype(BF16)
        dlr = jnp.zeros((tb, HEAD_LANES), F32)
        for p in range(2):
            cols = pl.ds(p * HEAD_LANES, HEAD_LANES)
            sl = slice(p * HEAD_LANES, (p + 1) * HEAD_LANES)
            bp, bep = b[:, sl], b_end[:, sl]
            e_in, e_out, e_end = jnp.exp(bp), jnp.exp(-bp), jnp.exp(bep - bp)
            q = q_ref[:, cols] * 0.125
            k = k_ref[:, cols]
            q_in, k_out, k_end = q * e_in, k * e_out, k * e_end
            qms = [jnp.where(m, q_in, 0.0).astype(BF16) for m in masks]
            kms_out = [jnp.where(m, k_out, 0.0).astype(BF16) for m in masks]
            kms_end = [jnp.where(m, k_end, 0.0).astype(BF16) for m in masks]
            vs, dos = [], []
            for e in range(2):
                hc = pl.ds((2 * p + e) * HEAD_LANES, HEAD_LANES)
                o, rr, dy = o_ref[:, hc], r_ref[:, hc], dy_ref[:, hc]
                sg = _sigmoid(rr)
                rs = lax.rsqrt(jnp.mean(o * o, axis=-1, keepdims=True) + EPS)
                on = o * rs
                t = dy * (rr * sg)
                sums_ref[1, :, hc] += _fold8(t * on)
                dn = t * gn_ref[...]
                dos.append((rs * (dn - on * jnp.mean(dn * on, axis=-1, keepdims=True))).astype(BF16))
                dr_ref[:, hc] = (dy * on * gn_ref[...] * (sg * (1.0 + rr * (1.0 - sg)))).astype(BF16)
                vs.append(v_ref[:, hc].astype(BF16))
            grow = [_tn(d0, q0) + _tn(d1, q1) for d0, q0, d1, q1 in zip(_chunks(dos[0]), _chunks(qms[0]), _chunks(dos[1]), _chunks(qms[1]))]
            entering = [st_ref[ch, p] for ch in range(cb)]
            dst, leaving_grad = dstate[p], [None] * cb
            for ch in reversed(range(cb)):
                leaving_grad[ch] = dst
                dst = dst * jnp.exp(bep[ch * c:ch * c + 1, :]) + grow[ch]
            dstate[p] = dst
            leaving = entering[1:] + [stn_ref[0, p]]
            felt = jnp.concatenate([jnp.broadcast_to(jnp.sum(dg_st * st, axis=0, keepdims=True), (c, HEAD_LANES))
                                    for dg_st, st in zip(leaving_grad, leaving)], axis=0)
            per_chunk = lambda rows, mats, fn: jnp.concatenate([fn(r, m.astype(BF16)) for r, m in zip(_chunks(rows), mats)], axis=0)
            dq_in = jnp.zeros((tb, HEAD_LANES), F32)
            dk_out = jnp.zeros((tb, HEAD_LANES), F32)
            dk_end = jnp.zeros((tb, HEAD_LANES), F32)
            for e in range(2):
                hc = pl.ds((2 * p + e) * HEAD_LANES, HEAD_LANES)
                a = jnp.where(causal, _nt(qms[e], kms_out[e]), 0.0).astype(BF16)
                da = jnp.where(causal, _nt(dos[e], vs[e]), 0.0).astype(BF16)
                dv_ref[:, hc] = (_tn(a, dos[e]) + per_chunk(kms_end[e], leaving_grad, _nt)).astype(BF16)
                dq_in = dq_in + jnp.where(masks[e], per_chunk(dos[e], entering, _nn) + _nn(da, kms_out[e]), 0.0)
                dk_out = dk_out + _tn(da, qms[e])
                dk_end = dk_end + jnp.where(masks[e], per_chunk(vs[e], leaving_grad, _nn), 0.0)
            dq = dq_in * e_in
            dk = dk_out * e_out + dk_end * e_end
            dq_ref[:, cols] = (dq * 0.125).astype(BF16)
            dk_ref[:, cols] = dk.astype(BF16)
            dg = _sum_left(upper.astype(BF16), q * dq - k * dk) + felt
            dz = dg * (1.0 / GLA_GATE_TAU) * _sigmoid(-z[:, sl])
            dz_b = dz.astype(BF16)
            sums_ref[0, :, cols] += _fold8(dz)
            dlr = dlr + _nt(dz_b, wg_ref[:, cols])
            gwg_ref[:, cols] += _tn(lr_b, dz_b)
        dlr_ref[...] = dlr.astype(BF16)

        @pl.when(i == nblk - 1)
        def _():
            _spread_total(sums_ref)

    rev = lambda i: nblk - 1 - i

    def col(width, at):
        return pl.BlockSpec((tb, width), lambda i: (rev(i), at // width))

    full = lambda shape: pl.BlockSpec(shape, lambda i: tuple(0 for _ in shape))
    out_col = lambda width: pl.BlockSpec((tb, width), lambda i: (rev(i), 0))
    return pl.pallas_call(
        body, name=name, grid=(nblk,),
        in_specs=[col(256, C_GQ), col(256, C_GK), col(512, C_GV), col(512, C_GR), col(128, C_LR),
                  full((HEAD_LANES, 256)), full((1, 256)), full((1, HEAD_LANES)),
                  pl.BlockSpec((tb, 512), lambda i: (rev(i), 0)),
                  pl.BlockSpec((cb, 2, HEAD_LANES, HEAD_LANES), lambda i: (rev(i), 0, 0, 0)),
                  pl.BlockSpec((1, 2, HEAD_LANES, HEAD_LANES), lambda i: (jnp.minimum((rev(i) + 1) * cb, nch - 1), 0, 0, 0)),
                  pl.BlockSpec((tb, 512), lambda i: (rev(i), 0))],
        out_specs=[out_col(256), out_col(256), out_col(512), out_col(512), out_col(128),
                   full((HEAD_LANES, 256)), full((2, 8, 512))],
        out_shape=[jax.ShapeDtypeStruct((s, 256), BF16), jax.ShapeDtypeStruct((s, 256), BF16),
                   jax.ShapeDtypeStruct((s, 512), BF16), jax.ShapeDtypeStruct((s, 512), BF16),
                   jax.ShapeDtypeStruct((s, 128), BF16), jax.ShapeDtypeStruct((HEAD_LANES, 256), F32),
                   jax.ShapeDtypeStruct((2, 8, 512), F32)],
        scratch_shapes=[pltpu.VMEM((2, HEAD_LANES, HEAD_LANES), F32)],
        compiler_params=_params("arbitrary"))(proj, proj, proj, proj, proj, wg, bg, gn, o_raw, states, states, dmixed)


def _head_sums(v):
    ri = lax.broadcasted_iota(jnp.int32, (HEAD_LANES, HEAD_LANES), 0) // 64
    ci = lax.broadcasted_iota(jnp.int32, (HEAD_LANES, HEAD_LANES), 1) // 64
    ones = (ri == ci).astype(BF16)
    return jnp.concatenate([_sum_right(v[:, p * HEAD_LANES:(p + 1) * HEAD_LANES], ones) for p in range(4)], axis=1)


def _attn_prep(proj, qg, kg, *, name):
    s = proj.shape[0]
    tm = ROW_TILE

    def body(q_ref, k_ref, qg_ref, kg_ref, qa_ref, ka_ref):
        q, k = q_ref[...], k_ref[...]
        qr = lax.rsqrt(_head_sums(q * q) * (1.0 / 64) + EPS)
        kr = lax.rsqrt(_head_sums(k * k) * (1.0 / 64) + EPS)
        qa_ref[...] = q * qr * qg_ref[...] * 0.125
        ka_ref[...] = k * kr * kg_ref[...]

    col = lambda at: pl.BlockSpec((tm, 512), lambda i: (i, at // 512))
    vec = pl.BlockSpec((1, 512), lambda i: (0, 0))
    out = pl.BlockSpec((tm, 512), lambda i: (i, 0))
    return pl.pallas_call(
        body, name=name, grid=(s // tm,), in_specs=[col(C_AQ), col(C_AK), vec, vec], out_specs=[out] * 2,
        out_shape=[jax.ShapeDtypeStruct((s, 512), F32)] * 2, compiler_params=_params("parallel"))(proj, proj, qg, kg)


FAR = 1e30
LOG2E, LN2 = 1.4426950408889634, 0.6931471805599453


def _attn_distance(first):
    blk = ATTN_BLOCK
    iq = lax.broadcasted_iota(jnp.int32, (2 * blk, 2 * blk), 0) & (blk - 1)
    ik = lax.broadcasted_iota(jnp.int32, (2 * blk, 2 * blk), 1)
    rel = iq + blk - ik
    valid = (rel >= 0) & (rel <= blk) & (jnp.logical_not(first) | (ik >= blk))
    return jnp.where(valid, rel.astype(F32), FAR)


def _stack_heads(t2):
    low = lax.broadcasted_iota(jnp.int32, t2.shape, 1) < 64
    return jnp.concatenate([jnp.where(low, t2, 0.0), jnp.where(low, 0.0, t2)], axis=0).astype(BF16)


def _unstack_heads(t):
    blk = ATTN_BLOCK
    low = lax.broadcasted_iota(jnp.int32, (blk, HEAD_LANES), 1) < 64
    return jnp.where(low, t[0:blk], t[blk:2 * blk])


def _attn_scores(qs, kcat, slopes, dil, dist):
    top = lax.broadcasted_iota(jnp.int32, (2 * ATTN_BLOCK, 1), 0) < ATTN_BLOCK
    return _nt(qs, kcat) - jnp.where(top, slopes[0] * (dil * LOG2E), slopes[1] * (dil * LOG2E)) * dist


def _pair_slopes(p):
    if isinstance(p, int):
        return ALIBI_SLOPES[2 * p], ALIBI_SLOPES[2 * p + 1]
    pick = lambda e: jnp.where(p == 0, ALIBI_SLOPES[e], jnp.where(p == 1, ALIBI_SLOPES[2 + e],
                               jnp.where(p == 2, ALIBI_SLOPES[4 + e], ALIBI_SLOPES[6 + e])))
    return pick(0), pick(1)


ATTN_GROUP = 4


def _each(fn, *lists):
    return [fn(*args) for args in zip(*lists)]


def _attn_group_fwd(q2s, kcats, vcats, slopes, dil, dist):
    qs = _each(lambda q2: _stack_heads(q2 * LOG2E), q2s)
    sc = _each(lambda q, k, sl: _attn_scores(q, k, sl, dil, dist), qs, kcats, slopes)
    m = _each(lambda s: jnp.max(s, axis=-1, keepdims=True), sc)
    pr = _each(lambda s, mx: jnp.exp2(s - mx), sc, m)
    den = _each(lambda p: jnp.sum(p, axis=-1, keepdims=True), pr)
    o = _each(lambda p, v, d: _nn(p.astype(BF16), v) / d, pr, vcats, den)
    lse = _each(lambda mx, d, t: jnp.broadcast_to(mx + jnp.log2(d), t.shape), m, den, o)
    return _each(lambda t, l: (_unstack_heads(t), _unstack_heads(l)), o, lse)


def _attn_group_bwd(q2s, kcats, vcats, do2s, y2s, lse2s, slopes, dil, dist):
    lane = lax.broadcasted_iota(jnp.int32, (ATTN_BLOCK, HEAD_LANES), 1)
    low = lane < 64
    per_head = lambda t, pick: jnp.concatenate([jnp.sum(jnp.where(pick(0), t, 0.0), axis=-1, keepdims=True),
                                                jnp.sum(jnp.where(pick(1), t, 0.0), axis=-1, keepdims=True)], axis=0)
    lse = _each(lambda l: per_head(l, lambda e: lane == 64 * e), lse2s)
    delta = _each(lambda d, y: per_head(d * y, lambda e: low if e == 0 else jnp.logical_not(low)), do2s, y2s)
    qs = _each(lambda q2: _stack_heads(q2 * LOG2E), q2s)
    dos = _each(_stack_heads, do2s)
    sc = _each(lambda q, k, sl: _attn_scores(q, k, sl, dil, dist), qs, kcats, slopes)
    pr = _each(lambda s, l: jnp.exp2(s - l), sc, lse)
    dp = _each(_nt, dos, vcats)
    ds = _each(lambda p, d, dl: (p * (d - dl)).astype(BF16), pr, dp, delta)
    dq = _each(lambda d, k: _unstack_heads(_nn(d, k)), ds, kcats)
    dk = _each(lambda d, q: _tn(d, q) * LN2, ds, qs)
    dv = _each(lambda p, d: _tn(p.astype(BF16), d), pr, dos)
    return list(zip(dq, dk, dv))


def _attn_specs(dil):
    rows = ATTN_BLOCK * dil
    if dil == 1:
        cur = lambda at: pl.BlockSpec((rows, 512), lambda n: (n, at // 512))
        prev = lambda at: pl.BlockSpec((rows, 512), lambda n: (jnp.maximum(n - 1, 0), at // 512))
    else:
        cur = lambda at: pl.BlockSpec((rows, HEAD_LANES), lambda n, p: (n, at // HEAD_LANES + p))
        prev = lambda at: pl.BlockSpec((rows, HEAD_LANES), lambda n, p: (jnp.maximum(n - 1, 0), at // HEAD_LANES + p))
    return cur, prev


def _attn_loop(dil, one_group):
    if dil == 1:
        one_group([(slice(None), pl.ds(p * HEAD_LANES, HEAD_LANES), p) for p in range(ATTN_GROUP)])
    else:
        p = pl.program_id(1)

        def step(g, carry):
            one_group([(pl.ds(g * ATTN_GROUP + j, ATTN_BLOCK, stride=dil), slice(None), p) for j in range(ATTN_GROUP)])
            return carry

        if dil == ATTN_GROUP:
            step(0, 0)
        else:
            lax.fori_loop(0, dil // ATTN_GROUP, step, 0)


def _dil_attn_fwd(qa, ka, proj, dil, *, name):
    s = qa.shape[0]

    def body(q_ref, kp_ref, kc_ref, vp_ref, vc_ref, o_ref, lse_ref):
        dist = _attn_distance(pl.program_id(0) == 0)

        def one_group(items):
            both = lambda a, b: [jnp.concatenate([a[rows, cols], b[rows, cols]], axis=0).astype(BF16) for rows, cols, _ in items]
            outs = _attn_group_fwd([q_ref[rows, cols] for rows, cols, _ in items], both(kp_ref, kc_ref), both(vp_ref, vc_ref),
                                   [_pair_slopes(p) for _, _, p in items], dil, dist)
            for (rows, cols, _), (o2, lse2) in zip(items, outs):
                o_ref[rows, cols] = o2
                lse_ref[rows, cols] = lse2

        _attn_loop(dil, one_group)

    cur, prev = _attn_specs(dil)
    grid = (s // ATTN_BLOCK,) if dil == 1 else (s // (ATTN_BLOCK * dil), 4)
    return pl.pallas_call(
        body, name=name, grid=grid, in_specs=[cur(0), prev(0), cur(0), prev(C_AV), cur(C_AV)], out_specs=[cur(0), cur(0)],
        out_shape=[jax.ShapeDtypeStruct((s, 512), F32)] * 2,
        compiler_params=_params(*["parallel"] * len(grid)))(qa, ka, ka, proj, proj)


def _attn_merge(branches, y_gla, *, name):
    s = y_gla.shape[0]
    tm = ROW_TILE

    def body(o0, l0, o1, l1, o2, l2, yg_ref, mixed_ref, y_ref, lse_ref):
        m = jnp.maximum(jnp.maximum(l0[...], l1[...]), l2[...])
        w0, w1, w2 = jnp.exp2(l0[...] - m), jnp.exp2(l1[...] - m), jnp.exp2(l2[...] - m)
        zs = w0 + w1 + w2
        y = (w0 * o0[...] + w1 * o1[...] + w2 * o2[...]) / zs
        y_ref[...] = y
        lse_ref[...] = m + jnp.log2(zs)
        mixed_ref[:, 0:512] = yg_ref[...]
        mixed_ref[:, 512:1024] = y.astype(BF16)

    blk = pl.BlockSpec((tm, 512), lambda i: (i, 0))
    args = [t for pair in branches for t in pair]
    return pl.pallas_call(
        body, name=name, grid=(s // tm,), in_specs=[blk] * 7,
        out_specs=[pl.BlockSpec((tm, 1024), lambda i: (i, 0)), blk, blk],
        out_shape=[jax.ShapeDtypeStruct((s, 1024), BF16), jax.ShapeDtypeStruct((s, 512), F32),
                   jax.ShapeDtypeStruct((s, 512), F32)],
        compiler_params=_params("parallel"))(*args, y_gla)


def _dil_attn_bwd(qa, ka, proj, y_att, lse, dmixed, dil, *, name):
    s = qa.shape[0]
    blk = ATTN_BLOCK

    def body(q_ref, kp_ref, kc_ref, vp_ref, vc_ref, y_ref, lse_ref, do_ref, dq_ref, dkc_ref, dkp_ref, dvc_ref, dvp_ref):
        dist = _attn_distance(pl.program_id(0) == 0)

        def one_group(items):
            both = lambda a, b: [jnp.concatenate([a[rows, cols], b[rows, cols]], axis=0).astype(BF16) for rows, cols, _ in items]
            at = lambda ref: [ref[rows, cols] for rows, cols, _ in items]
            outs = _attn_group_bwd(at(q_ref), both(kp_ref, kc_ref), both(vp_ref, vc_ref), at(do_ref), at(y_ref), at(lse_ref),
                                   [_pair_slopes(p) for _, _, p in items], dil, dist)
            for (rows, cols, _), (dq, dk, dv) in zip(items, outs):
                dq_ref[rows, cols] = dq
                dkp_ref[rows, cols] = dk[0:blk]
                dkc_ref[rows, cols] = dk[blk:2 * blk]
                dvp_ref[rows, cols] = dv[0:blk]
                dvc_ref[rows, cols] = dv[blk:2 * blk]

        _attn_loop(dil, one_group)

    cur, prev = _attn_specs(dil)
    grid = (s // blk,) if dil == 1 else (s // (blk * dil), 4)
    return pl.pallas_call(
        body, name=name, grid=grid,
        in_specs=[cur(0), prev(0), cur(0), prev(C_AV), cur(C_AV), cur(0), cur(0), cur(512)], out_specs=[cur(0)] * 5,
        out_shape=[jax.ShapeDtypeStruct((s, 512), F32)] * 5, compiler_params=_params(*["parallel"] * len(grid)),
    )(qa, ka, ka, proj, proj, y_att, lse, dmixed)


def _attn_post(parts, proj, qg, kg, *, name):
    s = proj.shape[0]
    tm = ATTN_BLOCK
    nblk = s // tm

    def body(*refs):
        ins, (q_ref, k_ref, qg_ref, kg_ref, dq_out, dk_out, dv_out, sums_ref) = refs[:15], refs[15:]
        i = pl.program_id(0)

        @pl.when(i == 0)
        def _():
            sums_ref[...] = jnp.zeros_like(sums_ref)

        dq = jnp.zeros((tm, 512), F32)
        dk = jnp.zeros((tm, 512), F32)
        dv = jnp.zeros((tm, 512), F32)
        for g, dil in enumerate(DILATIONS):
            dq_r, dkc_r, dkp_r, dvc_r, dvp_r = ins[5 * g:5 * g + 5]
            inside = (i + dil < nblk).astype(F32)
            dq = dq + dq_r[...]
            dk = dk + dkc_r[...] + inside * dkp_r[...]
            dv = dv + dvc_r[...] + inside * dvp_r[...]
        dv_out[...] = dv.astype(BF16)
        for row, (x_ref, g_ref, dy, out, post) in enumerate(((q_ref, qg_ref, dq, dq_out, 0.125), (k_ref, kg_ref, dk, dk_out, 1.0))):
            x = x_ref[...]
            rs = lax.rsqrt(_head_sums(x * x) * (1.0 / 64) + EPS)
            xn = x * rs
            dy = dy * post
            sums_ref[row] += _fold8(dy * xn)
            dn = dy * g_ref[...]
            out[...] = (rs * (dn - xn * (_head_sums(dn * xn) * (1.0 / 64)))).astype(BF16)

        @pl.when(i == nblk - 1)
        def _():
            _spread_total(sums_ref)

    here = pl.BlockSpec((tm, 512), lambda i: (i, 0))
    specs = []
    for dil in DILATIONS:
        later = pl.BlockSpec((tm, 512), lambda i, dil=dil: (jnp.minimum(i + dil, nblk - 1), 0))
        specs += [here, here, later, here, later]
    col = lambda at: pl.BlockSpec((tm, 512), lambda i: (i, at // 512))
    vec = pl.BlockSpec((1, 512), lambda i: (0, 0))
    return pl.pallas_call(
        body, name=name, grid=(nblk,), in_specs=specs + [col(C_AQ), col(C_AK), vec, vec],
        out_specs=[here, here, here, pl.BlockSpec((2, 8, 512), lambda i: (0, 0, 0))],
        out_shape=[jax.ShapeDtypeStruct((s, 512), BF16)] * 3 + [jax.ShapeDtypeStruct((2, 8, 512), F32)],
        compiler_params=_params("arbitrary"))(*[t for part in parts for t in part], proj, proj, qg, kg)


FFN_TM, FFN_TN = 256, 1408
HALO = 16


def _conv3(u_ref, halo_ref, w_ref, b_ref, first):
    u = u_ref[...].astype(F32)
    ext = jnp.concatenate([jnp.where(first, 0.0, halo_ref[...].astype(F32)), u], axis=0)
    u1 = pltpu.roll(ext, 1, 0)[HALO:]
    u2 = pltpu.roll(ext, 2, 0)[HALO:]
    return b_ref[...] + w_ref[0:1, :] * u2 + w_ref[1:2, :] * u1 + w_ref[2:3, :] * u, u, u1, u2


def _ffn_specs(tm, tn):
    nj = D_FF // tn
    blk = lambda half: pl.BlockSpec((tm, tn), lambda j, i: (i, j + half * nj))
    halo = lambda half: pl.BlockSpec((HALO, tn), lambda j, i: (jnp.maximum(i * (tm // HALO) - 1, 0), j + half * nj))
    wspec = lambda half: pl.BlockSpec((3, tn), lambda j, i: (0, j + half * nj))
    bspec = lambda half: pl.BlockSpec((1, tn), lambda j, i: (0, j + half * nj))
    return [blk(0), halo(0), blk(1), halo(1), wspec(0), wspec(1), bspec(0), bspec(1)]


def _conv_swiglu_fwd(u, conv_w, conv_b, *, name):
    s = u.shape[0]
    tm, tn = FFN_TM, FFN_TN

    def body(ug_ref, hg_ref, uv_ref, hv_ref, wg_ref, wv_ref, bg_ref, bv_ref, act_ref, uc_ref):
        first = pl.program_id(1) == 0
        cg = _conv3(ug_ref, hg_ref, wg_ref, bg_ref, first)[0]
        cv = _conv3(uv_ref, hv_ref, wv_ref, bv_ref, first)[0]
        act_ref[...] = (cg * _sigmoid(cg) * cv).astype(BF16)
        uc_ref[0] = cg.astype(BF16)
        uc_ref[1] = cv.astype(BF16)

    return pl.pallas_call(
        body, name=name, grid=(D_FF // tn, s // tm), in_specs=_ffn_specs(tm, tn),
        out_specs=[pl.BlockSpec((tm, tn), lambda j, i: (i, j)), pl.BlockSpec((2, tm, tn), lambda j, i: (0, i, j))],
        out_shape=[jax.ShapeDtypeStruct((s, D_FF), BF16), jax.ShapeDtypeStruct((2, s, D_FF), BF16)],
        compiler_params=_params("parallel", "parallel"))(u, u, u, u, conv_w, conv_w, conv_b, conv_b)


def _swiglu_bwd(uc, dact, *, name):
    _, s, _ = uc.shape
    tm, tn = FFN_TM, FFN_TN

    def body(uc_ref, da_ref, duc_ref, sums_ref):
        i = pl.program_id(1)

        @pl.when(i == 0)
        def _():
            sums_ref[...] = jnp.zeros_like(sums_ref)

        cg, cv, da = uc_ref[0].astype(F32), uc_ref[1].astype(F32), da_ref[...].astype(F32)
        sg = _sigmoid(cg)
        dg = da * cv * (sg * (1.0 + cg * (1.0 - sg)))
        dv = da * (cg * sg)
        duc_ref[0] = dg.astype(BF16)
        duc_ref[1] = dv.astype(BF16)
        sums_ref[0] += _fold8(dg)
        sums_ref[1] += _fold8(dv)

        @pl.when(i == s // tm - 1)
        def _():
            _spread_total(sums_ref)

    pair = pl.BlockSpec((2, tm, tn), lambda j, i: (0, i, j))
    return pl.pallas_call(
        body, name=name, grid=(D_FF // tn, s // tm), in_specs=[pair, pl.BlockSpec((tm, tn), lambda j, i: (i, j))],
        out_specs=[pair, pl.BlockSpec((2, 8, tn), lambda j, i: (0, 0, j))],
        out_shape=[jax.ShapeDtypeStruct((2, s, D_FF), BF16), jax.ShapeDtypeStruct((2, 8, D_FF), F32)],
        compiler_params=_params("parallel", "arbitrary"))(uc, dact)


def _conv_bwd(duc, u, conv_w, *, name):
    _, s, _ = duc.shape
    tm, tn = FFN_TM, FFN_TN
    nj, ni = D_FF // tn, s // tm

    def body(d_ref, halo_ref, u_ref, w_ref, du_ref, sums_ref):
        i = pl.program_id(2)

        @pl.when(i == 0)
        def _():
            sums_ref[...] = jnp.zeros_like(sums_ref)

        d = d_ref[0].astype(F32)
        ext = jnp.concatenate([d, jnp.where(i == ni - 1, 0.0, halo_ref[0].astype(F32))], axis=0)
        n = tm + HALO
        d1 = pltpu.roll(ext, n - 1, 0)[:tm]
        d2 = pltpu.roll(ext, n - 2, 0)[:tm]
        du_ref[...] = (w_ref[2:3, :] * d + w_ref[1:2, :] * d1 + w_ref[0:1, :] * d2).astype(BF16)
        uv = u_ref[...].astype(F32)
        for t, shifted in enumerate((d2, d1, d)):
            sums_ref[0, t] += _fold8(shifted * uv)

        @pl.when(i == ni - 1)
        def _():
            _spread_total(sums_ref)

    return pl.pallas_call(
        body, name=name, grid=(2, nj, ni),
        in_specs=[pl.BlockSpec((1, tm, tn), lambda g, j, i: (g, i, j)),
                  pl.BlockSpec((1, HALO, tn), lambda g, j, i: (g, jnp.minimum((i + 1) * (tm // HALO), s // HALO - 1), j)),
                  pl.BlockSpec((tm, tn), lambda g, j, i: (i, g * nj + j)),
                  pl.BlockSpec((3, tn), lambda g, j, i: (0, g * nj + j))],
        out_specs=[pl.BlockSpec((tm, tn), lambda g, j, i: (i, g * nj + j)),
                   pl.BlockSpec((1, 3, 8, tn), lambda g, j, i: (g, 0, 0, j))],
        out_shape=[jax.ShapeDtypeStruct((s, 2 * D_FF), BF16), jax.ShapeDtypeStruct((2, 3, 8, D_FF), F32)],
        compiler_params=_params("parallel", "parallel", "arbitrary"))(duc, duc, u, conv_w)


def _loss_head(x1, ffn, gate, target, *, name):
    s, d = x1.shape
    tm = ROW_TILE

    def body(x_ref, f_ref, g_ref, t_ref, dy_ref, df_ref, sums_ref):
        i = pl.program_id(0)

        @pl.when(i == 0)
        def _():
            sums_ref[...] = jnp.zeros_like(sums_ref)

        f = f_ref[...]
        err = x_ref[...] + g_ref[...] * f - t_ref[...]
        dy = err * (1.0 / d)
        dy_ref[...] = dy
        df_ref[...] = (g_ref[...] * dy).astype(BF16)
        sums_ref[0] += _fold8(dy * f)
        sums_ref[1] += _fold8(err * err)

        @pl.when(i == s // tm - 1)
        def _():
            _spread_total(sums_ref)

    row = pl.BlockSpec((tm, d), lambda i: (i, 0))
    return pl.pallas_call(
        body, name=name, grid=(s // tm,), in_specs=[row, row, pl.BlockSpec((1, d), lambda i: (0, 0)), row],
        out_specs=[row, row, pl.BlockSpec((2, 8, d), lambda i: (0, 0, 0))],
        out_shape=[jax.ShapeDtypeStruct((s, d), F32), jax.ShapeDtypeStruct((s, d), BF16), jax.ShapeDtypeStruct((2, 8, d), F32)],
        compiler_params=_params("arbitrary"))(x1, ffn, gate, target)


def _adamw(w, g, m, v, *, name):
    rows, cols = w.shape
    tm = next((t for t in range(ROW_TILE, 7, -8) if rows % t == 0), rows)

    def body(w_ref, g_ref, m_ref, v_ref, d_ref, mo_ref, vo_ref):
        gv = g_ref[...]
        mn = ADAM_B1 * m_ref[...] + (1.0 - ADAM_B1) * gv
        vn = ADAM_B2 * v_ref[...] + (1.0 - ADAM_B2) * (gv * gv)
        m_hat = mn / (1.0 - ADAM_B1 ** ADAM_STEP)
        v_hat = vn / (1.0 - ADAM_B2 ** ADAM_STEP)
        d_ref[...] = -ADAM_LR * (m_hat / (jnp.sqrt(v_hat) + ADAM_EPS) + ADAM_WD * w_ref[...])
        mo_ref[...] = mn
        vo_ref[...] = vn

    blk = pl.BlockSpec((tm, cols), lambda i: (i, 0))
    return pl.pallas_call(
        body, name=name, grid=(rows // tm,), in_specs=[blk] * 4, out_specs=[blk] * 3,
        out_shape=[jax.ShapeDtypeStruct((rows, cols), F32)] * 3, compiler_params=_params("parallel"))(w, g, m, v)


def _colsum(t):
    return t[..., 0, :]


def _in_proj_layout(w_in):
    pad = jnp.zeros((w_in.shape[0], PROJ_W - C_LR - GLA_GATE_RANK), w_in.dtype)
    return jnp.concatenate([w_in[:, :1536], w_in[:, 1552:], w_in[:, 1536:1552], pad], axis=1)


def _in_proj_grad_layout(g):
    return jnp.concatenate([g[:, :1536], g[:, C_LR:C_LR + GLA_GATE_RANK], g[:, 1536:C_LR]], axis=1)


def _gate_layout(gla_w_gate):
    return jnp.pad(gla_w_gate, ((0, HEAD_LANES - GLA_GATE_RANK), (0, 0))).astype(BF16)


def _local_step(x, target, mod, wi, wo, ffn_weights, ffn_grads_ready, conv_w, conv_b, wg, bg, gn, qg, kg, n1g, n2g):
    d = D_MODEL
    sh1, sc1, g1, sh2, sc2, g2 = [mod[:, i * d:(i + 1) * d] for i in range(6)]
    qg8, kg8 = jnp.tile(qg, (1, 8)), jnp.tile(kg, (1, 8))

    _, h1, h1_t = _norm_mod_fwd(x, None, None, n1g, sc1, sh1, name="norm1_fwd")
    proj = _mm(h1, wi, tm=1024, tn=PROJ_W, tk=d, name="in_proj")
    o_raw, y_gla, states = _gla_fwd(proj, wg, bg, gn, name="gla_fwd")
    qa, ka = _attn_prep(proj, qg8, kg8, name="attn_prep")
    branches = [_dil_attn_fwd(qa, ka, proj, dil, name=f"attn_fwd_d{dil}") for dil in DILATIONS]
    mixed, y_att, lse = _attn_merge(branches, y_gla, name="attn_merge")
    attn_out = _mm(mixed, wo, tm=1024, tn=d, tk=d, name="out_proj")
    x1, h2, h2_t = _norm_mod_fwd(x, attn_out, g1, n2g, sc2, sh2, name="norm2_fwd")
    wup, wdown = ffn_weights(h2)
    u = _mm(h2, wup, out_dtype=BF16, tm=1024, tn=D_FF, tk=d, name="up_proj")
    act, uc = _conv_swiglu_fwd(u, conv_w, conv_b, name="conv_swiglu_fwd")
    ffn = _mm(act, wdown, tm=1024, tn=d, tk=D_FF, name="down_proj")
    dy, dffn, head_sums = _loss_head(x1, ffn, g2, target, name="loss_head")

    dact = _mm(dffn, wdown, tb=True, out_dtype=BF16, tm=1024, tn=D_FF, tk=d, name="down_proj_dx")
    g_wdown, g_wdown_b = _mm(act, dffn, ta=True, tm=1408, tn=d, tk=1024, also_bf16=True, name="down_proj_dw")
    duc, bias_sums = _swiglu_bwd(uc, dact, name="swiglu_bwd")
    du, tap_sums = _conv_bwd(duc, u, conv_w, name="conv_bwd")
    dh2 = _mm(du, wup, tb=True, tm=1024, tn=d, tk=D_FF, name="up_proj_dx")
    g_wup, g_wup_b = _mm(h2_t, du, tm=d, tn=1408, tk=1024, shard_cols=True, also_bf16=True, name="up_proj_dw")
    token = ffn_grads_ready(g_wup_b, g_wdown_b)
    g1_late = g1 if token is None else g1 + token[0:1, 0:1]
    dx1, dao, n2_sums = _norm_mod_bwd(x1, dh2, dy, n2g, sc2, attn_out, g1_late, name="norm2_bwd")

    dmixed = _mm(dao, wo, tb=True, tm=1024, tn=d, tk=d, name="out_proj_dx")
    g_wo = _mm(mixed, dao, ta=True, tm=d, tn=d, tk=1024, name="out_proj_dw")
    dgq, dgk, dgv, dgr, dlr, g_wg, gla_sums = _gla_bwd(proj, wg, bg, gn, o_raw, states, dmixed, name="gla_bwd")
    parts = [_dil_attn_bwd(qa, ka, proj, y_att, lse, dmixed, dil, name=f"attn_bwd_d{dil}") for dil in DILATIONS]
    daq, dak, dav, qk_sums = _attn_post(parts, proj, qg8, kg8, name="attn_post")
    dproj = jnp.concatenate([dgq, dgk, dgv, dgr, daq, dak, dav, dlr], axis=1)
    dh1 = _mm(dproj, wi, tb=True, tm=1024, tn=d, tk=PROJ_W, name="in_proj_dx")
    g_wi = _mm(h1_t, dproj, tm=512, tn=PROJ_W, tk=1024, name="in_proj_dw")
    grad_x, _, n1_sums = _norm_mod_bwd(x, dh1, dx1, n1g, sc1, None, None, name="norm1_bwd")

    n1, n2, hs, taps, cb = _colsum(n1_sums), _colsum(n2_sums), _colsum(head_sums), _colsum(tap_sums), _colsum(bias_sums)
    gs, qs = _colsum(gla_sums), _colsum(qk_sums)
    dmod = jnp.concatenate([n1[1], n1[0] * n1g[0], n2[2], n2[1], n2[0] * n2g[0], hs[0]])
    small = dict(
        dmod=dmod,
        norm1_g=n1[0] * (1.0 + sc1[0]), norm2_g=n2[0] * (1.0 + sc2[0]),
        gla_w_gate=g_wg[:GLA_GATE_RANK], gla_b_gate=gs[0, :256], gla_norm_g=gs[1].reshape(4, 128).sum(axis=0),
        q_norm_g=qs[0].reshape(8, 64).sum(axis=0), k_norm_g=qs[1].reshape(8, 64).sum(axis=0),
        conv_w=jnp.concatenate([taps[0], taps[1]], axis=1), conv_b=jnp.concatenate([cb[0], cb[1]]),
    )
    return head_sums[1], grad_x, (g_wi, g_wo, g_wup, g_wdown), small


N_DEV, N_CHIP = 8, 4
ANY = pl.BlockSpec(memory_space=pl.ANY)
VMEM_SPEC = pl.BlockSpec(memory_space=pltpu.VMEM)


def _place():
    x, y, c = lax.axis_index("x"), lax.axis_index("y"), lax.axis_index("c")
    other_chips = [(1 - x, y), (x, 1 - y), (1 - x, 1 - y)]
    return x, y, c, (x, y, 1 - c), other_chips


def _all_gather_small(v, *, name):
    m, n = v.shape

    def body(v_ref, out_ref, send_sems, recv_sems, local_sem):
        x, y, c, sibling, chips = _place()
        me = (x, y, c)

        def rows(px, py, pc):
            return out_ref.at[pl.ds((4 * px + 2 * py + pc) * m, m), :]

        def copy(k, block, to, src=None):
            return pltpu.make_async_remote_copy(
                src_ref=rows(*block) if src is None else src, dst_ref=rows(*block), send_sem=send_sems.at[k],
                recv_sem=recv_sems.at[k], device_id=to, device_id_type=MESH)

        mine = pltpu.make_async_copy(v_ref, rows(*me), local_sem)
        mine.start()
        first = [copy(0, me, sibling, src=v_ref)]
        first += [copy(1 + j, me, (*chip, c), src=v_ref) for j, chip in enumerate(chips)]
        for cp in first:
            cp.start()
        passed = [copy(4 + j, (*chip, c), sibling) for j, chip in enumerate(chips)]
        for j, chip in enumerate(chips):
            copy(1 + j, (*chip, c), me).wait_recv()
            passed[j].start()
        copy(0, sibling, me).wait_recv()
        for j, chip in enumerate(chips):
            copy(4 + j, (*chip, 1 - c), me).wait_recv()
        for cp in first + passed:
            cp.wait_send()
        mine.wait()

    return pl.pallas_call(
        body, name=name, out_shape=jax.ShapeDtypeStruct((N_DEV * m, n), v.dtype), in_specs=[VMEM_SPEC], out_specs=VMEM_SPEC,
        scratch_shapes=[pltpu.SemaphoreType.DMA((7,)), pltpu.SemaphoreType.DMA((7,)), pltpu.SemaphoreType.DMA],
    )(v)


def _gather_weight_shards(shards, *, name):
    nw = len(shards)

    def body(*refs):
        srcs, outs, (send_sems, recv_sems) = refs[:nw], refs[nw:2 * nw], refs[2 * nw:]
        x, y, c, sibling, chips = _place()
        index = lambda chip: 2 * chip[0] + chip[1]

        def copy(w, k, src, dst, to):
            return pltpu.make_async_remote_copy(src_ref=src, dst_ref=dst, send_sem=send_sems.at[6 * w + k],
                                                recv_sem=recv_sems.at[6 * w + k], device_id=to, device_id_type=MESH)

        sent = []
        for w, (src_ref, out_ref) in enumerate(zip(srcs, outs)):
            for k, chip in enumerate(chips):
                sent.append(copy(w, k, src_ref.at[c], out_ref.at[2 * x + y, c], (*chip, c)))
                sent[-1].start()
        for w, out_ref in enumerate(outs):
            for k, chip in enumerate(chips):
                landed = out_ref.at[index(chip), c]
                copy(w, k, landed, landed, (*chip, c)).wait_recv()
                sent.append(copy(w, 3 + k, landed, landed, sibling))
                sent[-1].start()
        for w, out_ref in enumerate(outs):
            for k, chip in enumerate(chips):
                passed_on = out_ref.at[index(chip), 1 - c]
                copy(w, 3 + k, passed_on, passed_on, sibling).wait_recv()
        for cp in sent:
            cp.wait_send()

    return pl.pallas_call(
        body, name=name, out_shape=[jax.ShapeDtypeStruct((N_CHIP, *s.shape), s.dtype) for s in shards],
        in_specs=[ANY] * nw, out_specs=[ANY] * nw,
        scratch_shapes=[pltpu.SemaphoreType.DMA((6 * nw,)), pltpu.SemaphoreType.DMA((6 * nw,))],
    )(*shards)


HBM_SPEC = pl.BlockSpec(memory_space=pltpu.HBM)
SEM_SPEC = pl.BlockSpec(memory_space=pltpu.SEMAPHORE)
DATAFLOW_EFFECT = pltpu.SideEffectType.DATAFLOW_SIDE_EFFECTING


def _late_copies(srcs, lands, send_sems, recv_sems):
    x, y, c, _, chips = _place()
    return [pltpu.make_async_remote_copy(
        src_ref=src.at[c], dst_ref=land.at[2 * x + y, c], send_sem=send_sems.at[6 * w + 2 * r + core],
        recv_sem=recv_sems.at[6 * w + 2 * r + c], device_id=(*chip, core), device_id_type=MESH)
        for w, (src, land) in enumerate(zip(srcs, lands)) for r, chip in enumerate(chips) for core in range(2)]


def _gather_late_start(own, after, *, name):
    nw = len(own)

    def body(*refs):
        srcs, lands, send_sems, recv_sems, token = refs[:nw], refs[nw:2 * nw], refs[2 * nw + 1], refs[2 * nw + 2], refs[-1]
        for cp in _late_copies(srcs, lands, send_sems, recv_sems):
            cp.start()
        token[...] = jnp.zeros_like(token)

    lands = [pltpu.with_memory_space_constraint(lax.empty((N_CHIP, *s.shape), s.dtype), pltpu.HBM) for s in own]
    own = [pltpu.with_memory_space_constraint(s, pltpu.HBM) for s in own]
    out = pl.pallas_call(
        body, name=name,
        out_shape=(pltpu.SemaphoreType.DMA((6 * nw,)), pltpu.SemaphoreType.DMA((6 * nw,)),
                   *[pltpu.HBM(s.shape, s.dtype) for s in own], *[pltpu.HBM(s.shape, s.dtype) for s in lands],
                   jax.ShapeDtypeStruct((8, 128), F32)),
        in_specs=[HBM_SPEC] * (2 * nw) + [ANY], out_specs=(SEM_SPEC, SEM_SPEC, *[HBM_SPEC] * (2 * nw), VMEM_SPEC),
        input_output_aliases={i: 2 + i for i in range(2 * nw)},
        compiler_params=pltpu.CompilerParams(has_side_effects=DATAFLOW_EFFECT))(*own, *lands, after)
    return out[0], out[1], out[2:2 + nw], out[2 + nw:2 + 2 * nw], out[-1]


def _gather_late_wait(send_sems, recv_sems, own, lands, after, *, name):
    nw = len(own)

    def body(*refs):
        srcs, lands_in, send_sems, recv_sems = refs[:nw], refs[nw:2 * nw], refs[2 * nw], refs[2 * nw + 1]
        x, y, c, _, chips = _place()
        for cp in _late_copies(srcs, lands_in, send_sems, recv_sems):
            cp.wait_send()
        for w, (src, land) in enumerate(zip(srcs, lands_in)):
            for r, chip in enumerate(chips):
                for core in range(2):
                    pltpu.make_async_remote_copy(
                        src_ref=src.at[c], dst_ref=land.at[2 * chip[0] + chip[1], core], send_sem=send_sems.at[6 * w + 2 * r + core],
                        recv_sem=recv_sems.at[6 * w + 2 * r + core], device_id=(*chip, core), device_id_type=MESH).wait_recv()

    out = pl.pallas_call(
        body, name=name, out_shape=(*[pltpu.HBM(s.shape, s.dtype) for s in own], *[pltpu.HBM(s.shape, s.dtype) for s in lands]),
        in_specs=[HBM_SPEC] * (2 * nw) + [SEM_SPEC, SEM_SPEC, ANY], out_specs=tuple([HBM_SPEC] * (2 * nw)),
        input_output_aliases={i: i for i in range(2 * nw)},
        compiler_params=pltpu.CompilerParams(has_side_effects=DATAFLOW_EFFECT))(*own, *lands, send_sems, recv_sems, after)
    return out[:nw], out[nw:]


def _direct_reduce_copies(srcs, lands, send_sems, recv_sems):
    x, y, c, _, _ = _place()
    cps = []
    for w, (src, land) in enumerate(zip(srcs, lands)):
        for rel in range(1, N_DEV):
            tx, ty, tc = (1 - x if rel & 4 else x), (1 - y if rel & 2 else y), (1 - c if rel & 1 else c)
            cps.append(pltpu.make_async_remote_copy(
                src_ref=src.at[2 * tx + ty, tc], dst_ref=land.at[rel - 1], send_sem=send_sems.at[7 * w + rel - 1],
                recv_sem=recv_sems.at[7 * w + rel - 1], device_id=(tx, ty, tc), device_id_type=MESH))
    return cps


def _direct_reduce_start(grads, *, name):
    nw = len(grads)

    def body(*refs):
        srcs, lands, send_sems, recv_sems, token = refs[:nw], refs[nw:2 * nw], refs[2 * nw], refs[2 * nw + 1], refs[-1]
        for cp in _direct_reduce_copies(srcs, lands, send_sems, recv_sems):
            cp.start()
        token[...] = jnp.zeros_like(token)

    lands = [pltpu.with_memory_space_constraint(lax.empty((N_DEV - 1, *g.shape[2:]), g.dtype), pltpu.HBM) for g in grads]
    grads = [pltpu.with_memory_space_constraint(g, pltpu.HBM) for g in grads]
    out = pl.pallas_call(
        body, name=name,
        out_shape=(pltpu.SemaphoreType.DMA((7 * nw,)), pltpu.SemaphoreType.DMA((7 * nw,)),
                   *[pltpu.HBM(g.shape, g.dtype) for g in grads], *[pltpu.HBM(t.shape, t.dtype) for t in lands],
                   jax.ShapeDtypeStruct((8, 128), F32)),
        in_specs=[HBM_SPEC] * (2 * nw), out_specs=(SEM_SPEC, SEM_SPEC, *[HBM_SPEC] * (2 * nw), VMEM_SPEC),
        input_output_aliases={i: 2 + i for i in range(2 * nw)},
        compiler_params=pltpu.CompilerParams(has_side_effects=DATAFLOW_EFFECT))(*grads, *lands)
    return out[0], out[1], out[2:2 + nw], out[2 + nw:2 + 2 * nw], out[-1]


def _direct_reduce_wait(send_sems, recv_sems, grads, lands, after, *, name):
    nw = len(grads)

    def body(*refs):
        srcs, lands_in, send_sems, recv_sems = refs[:nw], refs[nw:2 * nw], refs[2 * nw], refs[2 * nw + 1]
        cps = _direct_reduce_copies(srcs, lands_in, send_sems, recv_sems)
        for cp in cps:
            cp.wait_send()
        for cp in cps:
            cp.wait_recv()

    out = pl.pallas_call(
        body, name=name, out_shape=(*[pltpu.HBM(g.shape, g.dtype) for g in grads], *[pltpu.HBM(t.shape, t.dtype) for t in lands]),
        in_specs=[HBM_SPEC] * (2 * nw) + [SEM_SPEC, SEM_SPEC, ANY], out_specs=tuple([HBM_SPEC] * (2 * nw)),
        input_output_aliases={i: i for i in range(2 * nw)},
        compiler_params=pltpu.CompilerParams(has_side_effects=DATAFLOW_EFFECT))(*grads, *lands, send_sems, recv_sems, after)
    return out[nw:]


def _direct_reduce_add(grad, landed, chip, core, *, name):
    _, r, n = grad.shape
    half = r // 2
    tr = _row_tile(half)
    nb = half // tr

    def body(chip_ref, core_ref, g_ref, t_ref, o_ref):
        acc = g_ref[0]
        for k in range(N_DEV - 1):
            acc = acc + t_ref[k].astype(F32)
        o_ref[...] = acc

    return pl.pallas_call(
        body, name=name,
        grid_spec=pltpu.PrefetchScalarGridSpec(
            num_scalar_prefetch=2, grid=(nb,),
            in_specs=[pl.BlockSpec((1, tr, n), lambda i, chip_ref, core_ref: (chip_ref[0], core_ref[0] * nb + i, 0)),
                      pl.BlockSpec((N_DEV - 1, tr, n), lambda i, chip_ref, core_ref: (0, i, 0))],
            out_specs=pl.BlockSpec((tr, n), lambda i, chip_ref, core_ref: (i, 0))),
        out_shape=jax.ShapeDtypeStruct((half, n), F32), compiler_params=_params("parallel"))(chip, core, grad, landed)


def _pair_exchange_halves(grads, *, name):
    nw = len(grads)

    def body(*refs):
        srcs, outs, (send_sems, recv_sems) = refs[:nw], refs[nw:2 * nw], refs[2 * nw:]
        _, _, c, sibling, _ = _place()
        cps = []
        for w, (src_ref, out_ref) in enumerate(zip(srcs, outs)):
            cps.append(pltpu.make_async_remote_copy(
                src_ref=src_ref.at[:, 1 - c], dst_ref=out_ref, send_sem=send_sems.at[w],
                recv_sem=recv_sems.at[w], device_id=sibling, device_id_type=MESH))
            cps[-1].start()
        for cp in cps:
            cp.wait()

    return pl.pallas_call(
        body, name=name, out_shape=[jax.ShapeDtypeStruct((N_CHIP, *g.shape[2:]), g.dtype) for g in grads],
        in_specs=[ANY] * nw, out_specs=[ANY] * nw,
        scratch_shapes=[pltpu.SemaphoreType.DMA((nw,)), pltpu.SemaphoreType.DMA((nw,))])(*grads)


def _chip_scatter(pairs, *, name):
    nw = len(pairs)

    def body(*refs):
        srcs, outs, (send_sems, recv_sems) = refs[:nw], refs[nw:2 * nw], refs[2 * nw:]
        _, _, c, _, chips = _place()
        cps = []
        for w, (p_ref, out_ref) in enumerate(zip(srcs, outs)):
            for k, chip in enumerate(chips):
                cps.append(pltpu.make_async_remote_copy(
                    src_ref=p_ref.at[2 * chip[0] + chip[1]], dst_ref=out_ref.at[k], send_sem=send_sems.at[3 * w + k],
                    recv_sem=recv_sems.at[3 * w + k], device_id=(*chip, c), device_id_type=MESH))
                cps[-1].start()
        for cp in cps:
            cp.wait()

    return pl.pallas_call(
        body, name=name, out_shape=[jax.ShapeDtypeStruct((3, *p.shape[1:]), p.dtype) for p in pairs],
        in_specs=[ANY] * nw, out_specs=[ANY] * nw,
        scratch_shapes=[pltpu.SemaphoreType.DMA((3 * nw,)), pltpu.SemaphoreType.DMA((3 * nw,))])(*pairs)


def _share_halves(halves, *, name):
    nw = len(halves)

    def body(*refs):
        srcs, outs, (send_sems, recv_sems) = refs[:nw], refs[nw:2 * nw], refs[2 * nw:]
        _, _, _, sibling, _ = _place()
        cps = [pltpu.make_async_remote_copy(src_ref=src_ref, dst_ref=out_ref, send_sem=send_sems.at[w], recv_sem=recv_sems.at[w],
                                            device_id=sibling, device_id_type=MESH)
               for w, (src_ref, out_ref) in enumerate(zip(srcs, outs))]
        for cp in cps:
            cp.start()
        for cp in cps:
            cp.wait()

    return pl.pallas_call(
        body, name=name, out_shape=[jax.ShapeDtypeStruct(h.shape, h.dtype) for h in halves],
        in_specs=[ANY] * nw, out_specs=[ANY] * nw,
        scratch_shapes=[pltpu.SemaphoreType.DMA((nw,)), pltpu.SemaphoreType.DMA((nw,))])(*halves)


def _row_tile(rows, limit=256):
    return next(t for t in range(limit, 15, -16) if rows % t == 0)


def _pair_add(grad, got, core, *, name):
    _, r, n = grad.shape
    half = r // 2
    tr = _row_tile(half)
    nb = half // tr

    def body(core_ref, g_ref, t_ref, f_ref, b_ref):
        acc = g_ref[...] + t_ref[...]
        f_ref[...] = acc
        b_ref[...] = acc.astype(BF16)

    blk = pl.BlockSpec((1, tr, n), lambda j, i, core_ref: (j, i, 0))
    mine = pl.BlockSpec((1, tr, n), lambda j, i, core_ref: (j, core_ref[0] * nb + i, 0))
    return pl.pallas_call(
        body, name=name,
        grid_spec=pltpu.PrefetchScalarGridSpec(num_scalar_prefetch=1, grid=(N_CHIP, nb), in_specs=[mine, blk], out_specs=[blk, blk]),
        out_shape=[jax.ShapeDtypeStruct((N_CHIP, half, n), F32), jax.ShapeDtypeStruct((N_CHIP, half, n), BF16)],
        compiler_params=_params("parallel", "parallel"))(core, grad, got)


def _chip_add(pair, theirs, chip, *, name):
    _, h, n = pair.shape
    tr = _row_tile(h)

    def body(chip_ref, p_ref, t_ref, o_ref):
        o_ref[...] = ((p_ref[0] + t_ref[0].astype(F32)) + t_ref[1].astype(F32)) + t_ref[2].astype(F32)

    return pl.pallas_call(
        body, name=name,
        grid_spec=pltpu.PrefetchScalarGridSpec(
            num_scalar_prefetch=1, grid=(h // tr,),
            in_specs=[pl.BlockSpec((1, tr, n), lambda i, chip_ref: (chip_ref[0], i, 0)),
                      pl.BlockSpec((3, tr, n), lambda i, chip_ref: (0, i, 0))],
            out_specs=pl.BlockSpec((tr, n), lambda i, chip_ref: (i, 0))),
        out_shape=jax.ShapeDtypeStruct((h, n), F32), compiler_params=_params("parallel"))(chip, pair, theirs)


def _sum_devices(gathered, *, name):
    _, m, n = gathered.shape

    def body(g_ref, tot_ref, loss_ref):
        tot = g_ref[0]
        for dev in range(1, N_DEV):
            tot = tot + g_ref[dev]
        tot_ref[...] = tot
        loss_ref[...] = jnp.full((8, n), (0.5 / D_MODEL) * jnp.sum(tot[0:8]), F32)

    return pl.pallas_call(body, name=name, in_specs=[VMEM_SPEC], out_specs=[VMEM_SPEC, VMEM_SPEC],
                          out_shape=[jax.ShapeDtypeStruct((m, n), F32), jax.ShapeDtypeStruct((8, n), F32)])(gathered)


def _ada_mod(cond_all, w_ada_shard, *, name):
    tn = 512

    def body(a_ref, b_ref, o_ref):
        o_ref[...] = _nn(a_ref[...], b_ref[...], precision=HIGHEST)

    return pl.pallas_call(
        body, name=name, grid=(w_ada_shard.shape[1] // tn,),
        in_specs=[pl.BlockSpec(cond_all.shape, lambda j: (0, 0)), pl.BlockSpec((D_MODEL, tn), lambda j: (0, j))],
        out_specs=pl.BlockSpec((N_DEV, tn), lambda j: (0, j)),
        out_shape=jax.ShapeDtypeStruct((N_DEV, w_ada_shard.shape[1]), F32), compiler_params=_params("parallel"))(cond_all, w_ada_shard)


def _ada_grad(cond_all, dmod_cols, *, name):
    tm = 256

    def body(a_ref, b_ref, o_ref):
        o_ref[...] = lax.dot_general(a_ref[...], b_ref[...], (((0,), (0,)), ((), ())), precision=HIGHEST,
                                     preferred_element_type=F32)

    return pl.pallas_call(
        body, name=name, grid=(D_MODEL // tm,),
        in_specs=[pl.BlockSpec((N_DEV, tm), lambda i: (0, i)), pl.BlockSpec(dmod_cols.shape, lambda i: (0, 0))],
        out_specs=pl.BlockSpec((tm, dmod_cols.shape[1]), lambda i: (i, 0)),
        out_shape=jax.ShapeDtypeStruct((D_MODEL, dmod_cols.shape[1]), F32), compiler_params=_params("parallel"))(cond_all, dmod_cols)


def _silu_rows(c8, *, name):
    def body(c_ref, o_ref):
        cv = c_ref[...]
        o_ref[...] = cv * _sigmoid(cv)

    return pl.pallas_call(body, name=name, in_specs=[VMEM_SPEC], out_specs=VMEM_SPEC,
                          out_shape=jax.ShapeDtypeStruct(c8.shape, F32))(c8)


def _rows128(t, rows=None):
    flat = t.reshape(-1, 128)
    return flat if rows is None else jnp.pad(flat, ((0, rows - flat.shape[0]), (0, 0)))


def _from_col_shards(shards, r, n):
    return shards.reshape(N_CHIP, r, n).transpose(1, 0, 2).reshape(r, N_CHIP * n)


def kernel(x, c, w_ada, b_ada, norm1_g, w_in, gla_w_gate, gla_b_gate, gla_norm_g, q_norm_g, k_norm_g, w_out, norm2_g, w_up, conv_w, conv_b, w_down, loss_target, m_w_ada, m_b_ada, m_norm1_g, m_w_in, m_gla_w_gate, m_gla_b_gate, m_gla_norm_g, m_q_norm_g, m_k_norm_g, m_w_out, m_norm2_g, m_w_up, m_conv_w, m_conv_b, m_w_down, v_w_ada, v_b_ada, v_norm1_g, v_w_in, v_gla_w_gate, v_gla_b_gate, v_gla_norm_g, v_q_norm_g, v_k_norm_g, v_w_out, v_norm2_g, v_w_up, v_conv_w, v_conv_b, v_w_down):
    d = D_MODEL
    ax, ay, ac = lax.axis_index("x"), lax.axis_index("y"), lax.axis_index("c")
    chip, dev = 2 * ax + ay, 4 * ax + 2 * ay + ac

    cond = _silu_rows(jnp.broadcast_to(c, (8, d)), name="cond_silu")[0:1]
    small_in = jnp.concatenate([_rows128(cond), _rows128(conv_w[0]), _rows128(gla_w_gate[0])], axis=0)
    small_in = _rows128(small_in, 56)
    got = _all_gather_small(small_in, name="gather_small").reshape(N_DEV, 56, 128)
    cond_all = got[:, 0:8].reshape(N_DEV, d)
    conv_w_full = _from_col_shards(got[0::2, 8:41].reshape(N_CHIP, 3 * 1408 // 128, 128), 3, 1408)
    gate_full = _from_col_shards(got[0::2, 41:49].reshape(N_CHIP, 16 * 64 // 128, 128), GLA_GATE_RANK, 64)
    mod_part = _ada_mod(cond_all, w_ada[0], name="ada_mod")
    mod_got = _all_gather_small(_rows128(mod_part), name="gather_mod").reshape(N_DEV, N_DEV, 1536)
    mod_all = mod_got[0::2].transpose(1, 0, 2).reshape(N_DEV, 6 * d) + b_ada
    mod = lax.dynamic_slice_in_dim(mod_all, dev, 1, axis=0)

    own = [w[0].astype(BF16).reshape(2, w.shape[1] // 2, w.shape[2]) for w in (w_in, w_out, w_up, w_down)]
    with_own = lambda got, mine: [lax.dynamic_update_index_in_dim(t, o, chip, 0) for t, o in zip(got, mine)]
    got_in, got_out = with_own(_gather_weight_shards(own[:2], name="gather_weights"), own[:2])
    w_in_full = got_in.reshape(N_CHIP, d, 772).transpose(1, 0, 2).reshape(d, N_CHIP * 772)
    w_out_full = got_out.reshape(d, d)
    exchanged = mod_all[0:1, 0:1] + got_in[0, 0, 0:1, 0:1].astype(F32)
    send_sems, recv_sems, own_thru, lands, token = _gather_late_start(own[2:], exchanged, name="gather_late_start")
    mod = mod + token[0:1, 0:1]

    def ffn_weights(after):
        mine, landed = _gather_late_wait(send_sems, recv_sems, own_thru, lands, after, name="gather_late_wait")
        got_up, got_down = with_own(landed, mine)
        return got_up.reshape(N_CHIP, d, 1408).transpose(1, 0, 2).reshape(d, 2 * D_FF), got_down.reshape(D_FF, d)

    late_reduce = []

    def ffn_grads_ready(g_wup_b, g_wdown_b):
        halves_of = lambda g: g.reshape(N_CHIP, 2, g.shape[-2] // 2, g.shape[-1])
        late_reduce.extend(_direct_reduce_start([halves_of(g_wup_b), halves_of(g_wdown_b.reshape(N_CHIP, D_FF // N_CHIP, d))],
                                                name="reduce_late_start"))
        return late_reduce[4]

    err2, grad_x, (g_wi, g_wo, g_wup, g_wdown), small = _local_step(
        x[0], loss_target[0], mod, _in_proj_layout(w_in_full), w_out_full, ffn_weights, ffn_grads_ready, conv_w_full, conv_b,
        _gate_layout(gate_full), gla_b_gate, gla_norm_g, q_norm_g, k_norm_g, norm1_g, norm2_g)

    pieces = [err2[0], small["dmod"], small["norm1_g"], small["norm2_g"], small["gla_w_gate"].reshape(-1), small["gla_b_gate"],
              small["gla_norm_g"], small["q_norm_g"], small["k_norm_g"], small["conv_w"].reshape(-1), small["conv_b"]]
    sizes = [p.shape[0] for p in pieces]
    at = [sum(sizes[:i]) for i in range(len(sizes) + 1)]
    vec = _rows128(jnp.concatenate(pieces), 288)
    got = _all_gather_small(vec, name="gather_grads").reshape(N_DEV, 288, 128)
    total, loss8 = _sum_devices(got, name="sum_devices")
    total = total.reshape(-1)
    seg = lambda i: total[at[i]:at[i + 1]]
    dmod_all = got.reshape(N_DEV, -1)[:, at[1]:at[2]]
    g_small = dict(
        b_ada=seg(1)[None], norm1_g=seg(2)[None], norm2_g=seg(3)[None],
        gla_w_gate=lax.dynamic_slice_in_dim(seg(4).reshape(GLA_GATE_RANK, 256), chip * 64, 64, axis=1),
        gla_b_gate=seg(5)[None], gla_norm_g=seg(6)[None], q_norm_g=seg(7)[None], k_norm_g=seg(8)[None],
        conv_w=lax.dynamic_slice_in_dim(seg(9).reshape(3, 2 * D_FF), chip * 1408, 1408, axis=1), conv_b=seg(10)[None])
    dmod_cols = lax.dynamic_slice_in_dim(dmod_all.reshape(N_DEV, 6 * d), chip * 1536, 1536, axis=1)
    g_w_ada = _ada_grad(cond_all, dmod_cols, name="ada_grad")

    tags = ("w_in", "w_out")
    g_parts = [_in_proj_grad_layout(g_wi).reshape(d, N_CHIP, 772).transpose(1, 0, 2), g_wo.reshape(N_CHIP, d // N_CHIP, d)]
    core_id, chip_id = jnp.reshape(ac, (1,)).astype(jnp.int32), jnp.reshape(chip, (1,)).astype(jnp.int32)
    got = _pair_exchange_halves([g.reshape(N_CHIP, 2, g.shape[1] // 2, g.shape[2]) for g in g_parts], name="reduce_pair")
    pairs = [_pair_add(g, t, core_id, name=f"reduce_pair_add_{tag}") for g, t, tag in zip(g_parts, got, tags)]
    theirs = _chip_scatter([pb for _, pb in pairs], name="reduce_chips")
    summed = [_chip_add(pf, t, chip_id, name=f"reduce_chips_add_{tag}") for (pf, _), t, tag in zip(pairs, theirs, tags)]
    landed = _direct_reduce_wait(*late_reduce[:4], grad_x, name="reduce_late_wait")
    summed += [_direct_reduce_add(g, t, chip_id, core_id, name=f"reduce_late_add_{tag}")
               for g, t, tag in zip((g_wup, g_wdown.reshape(N_CHIP, D_FF // N_CHIP, d)), landed, ("w_up", "w_down"))]
    others = _share_halves(summed, name="share_pair")
    g_big = [jnp.concatenate([jnp.where(ac == 0, mine, other), jnp.where(ac == 0, other, mine)], axis=0)
             for mine, other in zip(summed, others)]

    grads = dict(w_ada=g_w_ada, w_in=g_big[0], w_out=g_big[1], w_up=g_big[2], w_down=g_big[3], **g_small)
    names = ["w_ada", "b_ada", "norm1_g", "w_in", "gla_w_gate", "gla_b_gate", "gla_norm_g", "q_norm_g", "k_norm_g", "w_out",
             "norm2_g", "w_up", "conv_w", "conv_b", "w_down"]
    ws = dict(w_ada=w_ada, b_ada=b_ada, norm1_g=norm1_g, w_in=w_in, gla_w_gate=gla_w_gate, gla_b_gate=gla_b_gate,
              gla_norm_g=gla_norm_g, q_norm_g=q_norm_g, k_norm_g=k_norm_g, w_out=w_out, norm2_g=norm2_g, w_up=w_up,
              conv_w=conv_w, conv_b=conv_b, w_down=w_down)
    ms = dict(w_ada=m_w_ada, b_ada=m_b_ada, norm1_g=m_norm1_g, w_in=m_w_in, gla_w_gate=m_gla_w_gate, gla_b_gate=m_gla_b_gate,
              gla_norm_g=m_gla_norm_g, q_norm_g=m_q_norm_g, k_norm_g=m_k_norm_g, w_out=m_w_out, norm2_g=m_norm2_g, w_up=m_w_up,
              conv_w=m_conv_w, conv_b=m_conv_b, w_down=m_w_down)
    vs = dict(w_ada=v_w_ada, b_ada=v_b_ada, norm1_g=v_norm1_g, w_in=v_w_in, gla_w_gate=v_gla_w_gate, gla_b_gate=v_gla_b_gate,
              gla_norm_g=v_gla_norm_g, q_norm_g=v_q_norm_g, k_norm_g=v_k_norm_g, w_out=v_w_out, norm2_g=v_norm2_g, w_up=v_w_up,
              conv_w=v_conv_w, conv_b=v_conv_b, w_down=v_w_down)
    g_out, d_out, m_out, v_out = [], [], [], []
    for nm in names:
        w2 = ws[nm].reshape(ws[nm].shape[-2:])
        g2 = grads[nm].reshape(w2.shape)
        dl, mn, vn = _adamw(w2, g2, ms[nm].reshape(w2.shape), vs[nm].reshape(w2.shape), name=f"adamw_{nm}")
        shape = ws[nm].shape
        g_out.append(g2.reshape(shape))
        d_out.append(dl.reshape(shape))
        m_out.append(mn.reshape(shape))
        v_out.append(vn.reshape(shape))
    return (loss8[0, 0], grad_x[None], *g_out, *d_out, *m_out, *v_out)
```

```python
import functools

import jax
import jax.numpy as jnp
from jax import lax
from jax.experimental import pallas as pl
from jax.experimental.pallas import tpu as pltpu

F32, BF16 = jnp.float32, jnp.bfloat16
HIGHEST = lax.Precision.HIGHEST
MESH = pl.DeviceIdType.MESH

D_MODEL = 1024
GLA_CHUNK = 64
GLA_GATE_TAU = 16.0
GLA_GATE_RANK = 16
HEAD_LANES = 128
ATTN_BLOCK = 128
DILATIONS = (1, 4, 16)
ALIBI_SLOPES = tuple(2.0 ** (-(h + 1)) for h in range(8))
D_FF = 2816
EPS = 1e-6
C_GQ, C_GK, C_GV, C_GR, C_AQ, C_AK, C_AV, C_LR, PROJ_W = 0, 256, 512, 1024, 1536, 2048, 2560, 3072, 3200
ADAM_LR, ADAM_B1, ADAM_B2, ADAM_EPS, ADAM_WD, ADAM_STEP = 0.001, 0.9, 0.999, 1e-08, 0.01, 10
VMEM_LIMIT_BYTES = 56 * 1024 * 1024
ROW_TILE = 256


def _params(*sem):
    return pltpu.CompilerParams(dimension_semantics=sem or None, vmem_limit_bytes=VMEM_LIMIT_BYTES)


def _nt(a, b):
    return lax.dot_general(a, b, (((1,), (1,)), ((), ())), preferred_element_type=F32)


def _tn(a, b):
    return lax.dot_general(a, b, (((0,), (0,)), ((), ())), preferred_element_type=F32)


def _nn(a, b, precision=None):
    return jnp.dot(a, b, preferred_element_type=F32, precision=precision)


def _split3(v):
    hi = v.astype(BF16)
    rest = v - hi.astype(F32)
    mid = rest.astype(BF16)
    return hi, mid, (rest - mid.astype(F32)).astype(BF16)


def _sum_right(v, ones):
    hi, mid, lo = _split3(v)
    return (_nn(lo, ones) + _nn(mid, ones)) + _nn(hi, ones)


def _sum_left(ones, v):
    hi, mid, lo = _split3(v)
    return (_nn(ones, lo) + _nn(ones, mid)) + _nn(ones, hi)


def _fold8(v):
    return v.reshape(v.shape[0] // 8, 8, v.shape[1]).sum(axis=0)


def _spread_total(ref):
    t = ref[...]
    ref[...] = jnp.broadcast_to(jnp.sum(t, axis=-2, keepdims=True), t.shape)


def _sigmoid(x):
    return 1.0 / (1.0 + jnp.exp(-x))


def _mm(a, b, *, ta=False, tb=False, out_dtype=F32, tm, tn, tk, shard_cols=False, also_bf16=False, name):
    (k_a, m) = a.shape if ta else a.shape[::-1]
    (k_b, n) = b.shape[::-1] if tb else b.shape
    assert k_a == k_b and m % tm == 0 and n % tn == 0 and k_a % tk == 0, (name, a.shape, b.shape)
    nk = k_a // tk
    assert nk == 1 or out_dtype == F32, name
    dims = (((0 if ta else 1,), (1 if tb else 0,)), ((), ()))

    def body(a_ref, b_ref, o_ref, *rounded):
        k = pl.program_id(2)
        part = lax.dot_general(a_ref[...].astype(BF16), b_ref[...].astype(BF16), dims, preferred_element_type=F32)
        if nk == 1:
            o_ref[...] = part.astype(out_dtype)
        else:
            @pl.when(k == 0)
            def _():
                o_ref[...] = part

            @pl.when(k > 0)
            def _():
                o_ref[...] += part

        if also_bf16:
            @pl.when(k == nk - 1)
            def _():
                rounded[0][...] = o_ref[...].astype(BF16)

    a_spec = pl.BlockSpec((tk, tm), lambda i, j, k: (k, i)) if ta else pl.BlockSpec((tm, tk), lambda i, j, k: (i, k))
    b_spec = pl.BlockSpec((tn, tk), lambda i, j, k: (j, k)) if tb else pl.BlockSpec((tk, tn), lambda i, j, k: (k, j))
    if shard_cols:
        o_spec, o_shape = pl.BlockSpec((None, tm, tn), lambda i, j, k: (j, i, 0)), (n // tn, m, tn)
    else:
        o_spec, o_shape = pl.BlockSpec((tm, tn), lambda i, j, k: (i, j)), (m, n)
    shapes = [jax.ShapeDtypeStruct(o_shape, out_dtype)] + ([jax.ShapeDtypeStruct(o_shape, BF16)] if also_bf16 else [])
    out = pl.pallas_call(
        body, name=name, grid=(m // tm, n // tn, nk), in_specs=[a_spec, b_spec], out_specs=[o_spec] * len(shapes),
        out_shape=shapes, compiler_params=_params("parallel", "parallel", "arbitrary"))(a, b)
    return out if also_bf16 else out[0]


def _norm_mod_fwd(x, branch, gate, gain, scale, shift, *, name):
    s, d = x.shape
    tm = ROW_TILE
    has_branch = branch is not None

    def body(*refs):
        if has_branch:
            x_ref, br_ref, gate_ref, gain_ref, sc_ref, sh_ref, x1_ref, h_ref, ht_ref = refs
            xv = x_ref[...] + gate_ref[...] * br_ref[...]
            x1_ref[...] = xv
        else:
            x_ref, gain_ref, sc_ref, sh_ref, h_ref, ht_ref = refs
            xv = x_ref[...]
        r = lax.rsqrt(jnp.mean(xv * xv, axis=-1, keepdims=True) + EPS)
        h = (xv * r) * gain_ref[...] * (1.0 + sc_ref[...]) + sh_ref[...]
        h_ref[...] = h.astype(BF16)
        ht_ref[...] = h.T.astype(BF16)

    row = pl.BlockSpec((tm, d), lambda i: (i, 0))
    col = pl.BlockSpec((d, tm), lambda i: (0, i))
    vec = pl.BlockSpec((1, d), lambda i: (0, 0))
    h_shapes = [jax.ShapeDtypeStruct((s, d), BF16), jax.ShapeDtypeStruct((d, s), BF16)]
    if has_branch:
        return pl.pallas_call(
            body, name=name, grid=(s // tm,), in_specs=[row, row, vec, vec, vec, vec], out_specs=[row, row, col],
            out_shape=[jax.ShapeDtypeStruct((s, d), F32)] + h_shapes,
            compiler_params=_params("parallel"))(x, branch, gate, gain, scale, shift)
    h, ht = pl.pallas_call(
        body, name=name, grid=(s // tm,), in_specs=[row, vec, vec, vec], out_specs=[row, col],
        out_shape=h_shapes, compiler_params=_params("parallel"))(x, gain, scale, shift)
    return x, h, ht


def _norm_mod_bwd(x, dh, dres, gain, scale, branch, gate, *, name):
    s, d = x.shape
    tm = ROW_TILE
    has_branch = branch is not None

    def body(*refs):
        if has_branch:
            x_ref, dh_ref, dres_ref, gain_ref, sc_ref, br_ref, gate_ref, dx_ref, dbr_ref, sums_ref = refs
        else:
            x_ref, dh_ref, dres_ref, gain_ref, sc_ref, dx_ref, sums_ref = refs
        i = pl.program_id(0)

        @pl.when(i == 0)
        def _():
            sums_ref[...] = jnp.zeros_like(sums_ref)

        xv, dhv = x_ref[...], dh_ref[...]
        r = lax.rsqrt(jnp.mean(xv * xv, axis=-1, keepdims=True) + EPS)
        xn = xv * r
        dxn = dhv * (gain_ref[...] * (1.0 + sc_ref[...]))
        dx = dres_ref[...] + r * (dxn - xn * jnp.mean(dxn * xn, axis=-1, keepdims=True))
        dx_ref[...] = dx
        sums_ref[0] += _fold8(dhv * xn)
        sums_ref[1] += _fold8(dhv)
        if has_branch:
            dbr_ref[...] = (gate_ref[...] * dx).astype(BF16)
            sums_ref[2] += _fold8(dx * br_ref[...])

        @pl.when(i == s // tm - 1)
        def _():
            _spread_total(sums_ref)

    row = pl.BlockSpec((tm, d), lambda i: (i, 0))
    vec = pl.BlockSpec((1, d), lambda i: (0, 0))
    sums = pl.BlockSpec((3, 8, d), lambda i: (0, 0, 0))
    sums_shape = jax.ShapeDtypeStruct((3, 8, d), F32)
    if has_branch:
        return pl.pallas_call(
            body, name=name, grid=(s // tm,), in_specs=[row, row, row, vec, vec, row, vec], out_specs=[row, row, sums],
            out_shape=[jax.ShapeDtypeStruct((s, d), F32), jax.ShapeDtypeStruct((s, d), BF16), sums_shape],
            compiler_params=_params("arbitrary"))(x, dh, dres, gain, scale, branch, gate)
    dx, sm = pl.pallas_call(
        body, name=name, grid=(s // tm,), in_specs=[row, row, row, vec, vec], out_specs=[row, sums],
        out_shape=[jax.ShapeDtypeStruct((s, d), F32), sums_shape],
        compiler_params=_params("arbitrary"))(x, dh, dres, gain, scale)
    return dx, None, sm


GLA_ROWS = 256


def _gla_block_setup(lr_ref, wg_ref, bg_ref):
    t, c = GLA_ROWS, GLA_CHUNK
    ri = lax.broadcasted_iota(jnp.int32, (t, t), 0)
    ci = lax.broadcasted_iota(jnp.int32, (t, t), 1)
    same = (ri // c) == (ci // c)
    causal, upper = same & (ci <= ri), same & (ci >= ri)
    z = _nn(lr_ref[...].astype(BF16), wg_ref[...]) + bg_ref[...]
    g = (jnp.minimum(z, 0.0) - jnp.log(1.0 + jnp.exp(-jnp.abs(z)))) * (1.0 / GLA_GATE_TAU)
    hi, mid, lo = _split3(g)
    total = lambda ones: (_nn(ones, lo) + _nn(ones, mid)) + _nn(ones, hi)
    return z, total(causal.astype(BF16)), total(same.astype(BF16)), causal, upper


def _chunks(t):
    return [t[i * GLA_CHUNK:(i + 1) * GLA_CHUNK] for i in range(GLA_ROWS // GLA_CHUNK)]


def _gla_fwd(proj, wg, bg, gn, *, name):
    s = proj.shape[0]
    tb, c = GLA_ROWS, GLA_CHUNK
    cb = tb // c

    def body(q_ref, k_ref, v_ref, r_ref, lr_ref, wg_ref, bg_ref, gn_ref, o_ref, y_ref, st_ref, state):
        i = pl.program_id(0)

        @pl.when(i == 0)
        def _():
            state[...] = jnp.zeros_like(state)

        low = lax.broadcasted_iota(jnp.int32, (tb, HEAD_LANES), 1) < 64
        masks = (low, jnp.logical_not(low))
        _, b, b_end, causal, _ = _gla_block_setup(lr_ref, wg_ref, bg_ref)
        for p in range(2):
            cols = pl.ds(p * HEAD_LANES, HEAD_LANES)
            bp, bep = (t[:, p * HEAD_LANES:(p + 1) * HEAD_LANES] for t in (b, b_end))
            k = k_ref[:, cols]
            q_in = q_ref[:, cols] * 0.125 * jnp.exp(bp)
            k_out = (k * jnp.exp(-bp)).astype(BF16)
            k_end = k * jnp.exp(bep - bp)
            qms = [jnp.where(m, q_in, 0.0).astype(BF16) for m in masks]
            kes = [jnp.where(m, k_end, 0.0).astype(BF16) for m in masks]
            vs = [v_ref[:, pl.ds((2 * p + e) * HEAD_LANES, HEAD_LANES)].astype(BF16) for e in range(2)]
            grow = [_tn(v0, k0) + _tn(v1, k1) for v0, k0, v1, k1 in zip(_chunks(vs[0]), _chunks(kes[0]), _chunks(vs[1]), _chunks(kes[1]))]
            st, entering = state[p], []
            for ch in range(cb):
                entering.append(st)
                st_ref[ch, p] = st
                st = st * jnp.exp(bep[ch * c:ch * c + 1, :]) + grow[ch]
            state[p] = st
            for e in range(2):
                hc = pl.ds((2 * p + e) * HEAD_LANES, HEAD_LANES)
                a = jnp.where(causal, _nt(qms[e], k_out), 0.0).astype(BF16)
                carried = jnp.concatenate([_nt(qc, sc.astype(BF16)) for qc, sc in zip(_chunks(qms[e]), entering)], axis=0)
                o = _nn(a, vs[e]) + carried
                o_ref[:, hc] = o
                rr = r_ref[:, hc]
                on = o * lax.rsqrt(jnp.mean(o * o, axis=-1, keepdims=True) + EPS)
                y_ref[:, hc] = (on * gn_ref[...] * (rr * _sigmoid(rr))).astype(BF16)

    def col(width, at):
        return pl.BlockSpec((tb, width), lambda i: (i, at // width))

    full = lambda shape: pl.BlockSpec(shape, lambda i: tuple(0 for _ in shape))
    return pl.pallas_call(
        body, name=name, grid=(s // tb,),
        in_specs=[col(256, C_GQ), col(256, C_GK), col(512, C_GV), col(512, C_GR), col(128, C_LR),
                  full((HEAD_LANES, 256)), full((1, 256)), full((1, HEAD_LANES))],
        out_specs=[pl.BlockSpec((tb, 512), lambda i: (i, 0)), pl.BlockSpec((tb, 512), lambda i: (i, 0)),
                   pl.BlockSpec((cb, 2, HEAD_LANES, HEAD_LANES), lambda i: (i, 0, 0, 0))],
        out_shape=[jax.ShapeDtypeStruct((s, 512), F32), jax.ShapeDtypeStruct((s, 512), BF16),
                   jax.ShapeDtypeStruct((s // c, 2, HEAD_LANES, HEAD_LANES), F32)],
        scratch_shapes=[pltpu.VMEM((2, HEAD_LANES, HEAD_LANES), F32)],
        compiler_params=_params("arbitrary"))(proj, proj, proj, proj, proj, wg, bg, gn)


def _gla_bwd(proj, wg, bg, gn, o_raw, states, dmixed, *, name):
    s = proj.shape[0]
    tb, c = GLA_ROWS, GLA_CHUNK
    cb = tb // c
    nblk, nch = s // tb, s // c

    def body(q_ref, k_ref, v_ref, r_ref, lr_ref, wg_ref, bg_ref, gn_ref, o_ref, st_ref, stn_ref, dy_ref,
             dq_ref, dk_ref, dv_ref, dr_ref, dlr_ref, gwg_ref, sums_ref, dstate):
        i = pl.program_id(0)

        @pl.when(i == 0)
        def _():
            dstate[...] = jnp.zeros_like(dstate)
            gwg_ref[...] = jnp.zeros_like(gwg_ref)
            sums_ref[...] = jnp.zeros_like(sums_ref)

        low = lax.broadcasted_iota(jnp.int32, (tb, HEAD_LANES), 1) < 64
        masks = (low, jnp.logical_not(low))
        z, b, b_end, causal, upper = _gla_block_setup(lr_ref, wg_ref, bg_ref)
        lr_b = lr_ref[...].astype(BF16)
        dlr = jnp.zeros((tb, HEAD_LANES), F32)
        for p in range(2):
            cols = pl.ds(p * HEAD_LANES, HEAD_LANES)
            sl = slice(p * HEAD_LANES, (p + 1) * HEAD_LANES)
            bp, bep = b[:, sl], b_end[:, sl]
            e_in, e_out, e_end = jnp.exp(bp), jnp.exp(-bp), jnp.exp(bep - bp)
            q = q_ref[:, cols] * 0.125
            k = k_ref[:, cols]
            q_in, k_out, k_end = q * e_in, k * e_out, k * e_end
            qms = [jnp.where(m, q_in, 0.0).astype(BF16) for m in masks]
            kms_out = [jnp.where(m, k_out, 0.0).astype(BF16) for m in masks]
            kms_end = [jnp.where(m, k_end, 0.0).astype(BF16) for m in masks]
            vs, dos = [], []
            for e in range(2):
                hc = pl.ds((2 * p + e) * HEAD_LANES, HEAD_LANES)
                o, rr, dy = o_ref[:, hc], r_ref[:, hc], dy_ref[:, hc]
                sg = _sigmoid(rr)
                rs = lax.rsqrt(jnp.mean(o * o, axis=-1, keepdims=True) + EPS)
                on = o * rs
                t = dy * (rr * sg)
                sums_ref[1, :, hc] += _fold8(t * on)
                dn = t * gn_ref[...]
                dos.append((rs * (dn - on * jnp.mean(dn * on, axis=-1, keepdims=True))).astype(BF16))
                dr_ref[:, hc] = (dy * on * gn_ref[...] * (sg * (1.0 + rr * (1.0 - sg)))).astype(BF16)
                vs.append(v_ref[:, hc].astype(BF16))
            grow = [_tn(d0, q0) + _tn(d1, q1) for d0, q0, d1, q1 in zip(_chunks(dos[0]), _chunks(qms[0]), _chunks(dos[1]), _chunks(qms[1]))]
            entering = [st_ref[ch, p] for ch in range(cb)]
            dst, leaving_grad = dstate[p], [None] * cb
            for ch in reversed(range(cb)):
                leaving_grad[ch] = dst
                dst = dst * jnp.exp(bep[ch * c:ch * c + 1, :]) + grow[ch]
            dstate[p] = dst
            leaving = entering[1:] + [stn_ref[0, p]]
            felt = jnp.concatenate([jnp.broadcast_to(jnp.sum(dg_st * st, axis=0, keepdims=True), (c, HEAD_LANES))
                                    for dg_st, st in zip(leaving_grad, leaving)], axis=0)
            per_chunk = lambda rows, mats, fn: jnp.concatenate([fn(r, m.astype(BF16)) for r, m in zip(_chunks(rows), mats)], axis=0)
            dq_in = jnp.zeros((tb, HEAD_LANES), F32)
            dk_out = jnp.zeros((tb, HEAD_LANES), F32)
            dk_end = jnp.zeros((tb, HEAD_LANES), F32)
            for e in range(2):
                hc = pl.ds((2 * p + e) * HEAD_LANES, HEAD_LANES)
                a = jnp.where(causal, _nt(qms[e], kms_out[e]), 0.0).astype(BF16)
                da = jnp.where(causal, _nt(dos[e], vs[e]), 0.0).astype(BF16)
                dv_ref[:, hc] = (_tn(a, dos[e]) + per_chunk(kms_end[e], leaving_grad, _nt)).astype(BF16)
                dq_in = dq_in + jnp.where(masks[e], per_chunk(dos[e], entering, _nn) + _nn(da, kms_out[e]), 0.0)
                dk_out = dk_out + _tn(da, qms[e])
                dk_end = dk_end + jnp.where(masks[e], per_chunk(vs[e], leaving_grad, _nn), 0.0)
            dq = dq_in * e_in
            dk = dk_out * e_out + dk_end * e_end
            dq_ref[:, cols] = (dq * 0.125).astype(BF16)
            dk_ref[:, cols] = dk.astype(BF16)
            dg = _sum_left(upper.astype(BF16), q * dq - k * dk) + felt
            dz = dg * (1.0 / GLA_GATE_TAU) * _sigmoid(-z[:, sl])
            dz_b = dz.astype(BF16)
            sums_ref[0, :, cols] += _fold8(dz)
            dlr = dlr + _nt(dz_b, wg_ref[:, cols])
            gwg_ref[:, cols] += _tn(lr_b, dz_b)
        dlr_ref[...] = dlr.astype(BF16)

        @pl.when(i == nblk - 1)
        def _():
            _spread_total(sums_ref)

    rev = lambda i: nblk - 1 - i

    def col(width, at):
        return pl.BlockSpec((tb, width), lambda i: (rev(i), at // width))

    full = lambda shape: pl.BlockSpec(shape, lambda i: tuple(0 for _ in shape))
    out_col = lambda width: pl.BlockSpec((tb, width), lambda i: (rev(i), 0))
    return pl.pallas_call(
        body, name=name, grid=(nblk,),
        in_specs=[col(256, C_GQ), col(256, C_GK), col(512, C_GV), col(512, C_GR), col(128, C_LR),
                  full((HEAD_LANES, 256)), full((1, 256)), full((1, HEAD_LANES)),
                  pl.BlockSpec((tb, 512), lambda i: (rev(i), 0)),
                  pl.BlockSpec((cb, 2, HEAD_LANES, HEAD_LANES), lambda i: (rev(i), 0, 0, 0)),
                  pl.BlockSpec((1, 2, HEAD_LANES, HEAD_LANES), lambda i: (jnp.minimum((rev(i) + 1) * cb, nch - 1), 0, 0, 0)),
                  pl.BlockSpec((tb, 512), lambda i: (rev(i), 0))],
        out_specs=[out_col(256), out_col(256), out_col(512), out_col(512), out_col(128),
                   full((HEAD_LANES, 256)), full((2, 8, 512))],
        out_shape=[jax.ShapeDtypeStruct((s, 256), BF16), jax.ShapeDtypeStruct((s, 256), BF16),
                   jax.ShapeDtypeStruct((s, 512), BF16), jax.ShapeDtypeStruct((s, 512), BF16),
                   jax.ShapeDtypeStruct((s, 128), BF16), jax.ShapeDtypeStruct((HEAD_LANES, 256), F32),
                   jax.ShapeDtypeStruct((2, 8, 512), F32)],
        scratch_shapes=[pltpu.VMEM((2, HEAD_LANES, HEAD_LANES), F32)],
        compiler_params=_params("arbitrary"))(proj, proj, proj, proj, proj, wg, bg, gn, o_raw, states, states, dmixed)


def _head_sums(v):
    ri = lax.broadcasted_iota(jnp.int32, (HEAD_LANES, HEAD_LANES), 0) // 64
    ci = lax.broadcasted_iota(jnp.int32, (HEAD_LANES, HEAD_LANES), 1) // 64
    ones = (ri == ci).astype(BF16)
    return jnp.concatenate([_sum_right(v[:, p * HEAD_LANES:(p + 1) * HEAD_LANES], ones) for p in range(4)], axis=1)


def _attn_prep(proj, qg, kg, *, name):
    s = proj.shape[0]
    tm = ROW_TILE

    def body(q_ref, k_ref, qg_ref, kg_ref, qa_ref, ka_ref):
        q, k = q_ref[...], k_ref[...]
        qr = lax.rsqrt(_head_sums(q * q) * (1.0 / 64) + EPS)
        kr = lax.rsqrt(_head_sums(k * k) * (1.0 / 64) + EPS)
        qa_ref[...] = q * qr * qg_ref[...] * 0.125
        ka_ref[...] = k * kr * kg_ref[...]

    col = lambda at: pl.BlockSpec((tm, 512), lambda i: (i, at // 512))
    vec = pl.BlockSpec((1, 512), lambda i: (0, 0))
    out = pl.BlockSpec((tm, 512), lambda i: (i, 0))
    return pl.pallas_call(
        body, name=name, grid=(s // tm,), in_specs=[col(C_AQ), col(C_AK), vec, vec], out_specs=[out] * 2,
        out_shape=[jax.ShapeDtypeStruct((s, 512), F32)] * 2, compiler_params=_params("parallel"))(proj, proj, qg, kg)


FAR = 1e30
LOG2E, LN2 = 1.4426950408889634, 0.6931471805599453


def _attn_distance(first):
    blk = ATTN_BLOCK
    iq = lax.broadcasted_iota(jnp.int32, (2 * blk, 2 * blk), 0) & (blk - 1)
    ik = lax.broadcasted_iota(jnp.int32, (2 * blk, 2 * blk), 1)
    rel = iq + blk - ik
    valid = (rel >= 0) & (rel <= blk) & (jnp.logical_not(first) | (ik >= blk))
    return jnp.where(valid, rel.astype(F32), FAR)


def _stack_heads(t2):
    low = lax.broadcasted_iota(jnp.int32, t2.shape, 1) < 64
    return jnp.concatenate([jnp.where(low, t2, 0.0), jnp.where(low, 0.0, t2)], axis=0).astype(BF16)


def _unstack_heads(t):
    blk = ATTN_BLOCK
    low = lax.broadcasted_iota(jnp.int32, (blk, HEAD_LANES), 1) < 64
    return jnp.where(low, t[0:blk], t[blk:2 * blk])


def _attn_scores(qs, kcat, slopes, dil, dist):
    top = lax.broadcasted_iota(jnp.int32, (2 * ATTN_BLOCK, 1), 0) < ATTN_BLOCK
    return _nt(qs, kcat) - jnp.where(top, slopes[0] * (dil * LOG2E), slopes[1] * (dil * LOG2E)) * dist


def _pair_slopes(p):
    if isinstance(p, int):
        return ALIBI_SLOPES[2 * p], ALIBI_SLOPES[2 * p + 1]
    pick = lambda e: jnp.where(p == 0, ALIBI_SLOPES[e], jnp.where(p == 1, ALIBI_SLOPES[2 + e],
                               jnp.where(p == 2, ALIBI_SLOPES[4 + e], ALIBI_SLOPES[6 + e])))
    return pick(0), pick(1)


ATTN_GROUP = 4


def _each(fn, *lists):
    return [fn(*args) for args in zip(*lists)]


def _attn_group_fwd(q2s, kcats, vcats, slopes, dil, dist):
    qs = _each(lambda q2: _stack_heads(q2 * LOG2E), q2s)
    sc = _each(lambda q, k, sl: _attn_scores(q, k, sl, dil, dist), qs, kcats, slopes)
    m = _each(lambda s: jnp.max(s, axis=-1, keepdims=True), sc)
    pr = _each(lambda s, mx: jnp.exp2(s - mx), sc, m)
    den = _each(lambda p: jnp.sum(p, axis=-1, keepdims=True), pr)
    o = _each(lambda p, v, d: _nn(p.astype(BF16), v) / d, pr, vcats, den)
    lse = _each(lambda mx, d, t: jnp.broadcast_to(mx + jnp.log2(d), t.shape), m, den, o)
    return _each(lambda t, l: (_unstack_heads(t), _unstack_heads(l)), o, lse)


def _attn_group_bwd(q2s, kcats, vcats, do2s, y2s, lse2s, slopes, dil, dist):
    lane = lax.broadcasted_iota(jnp.int32, (ATTN_BLOCK, HEAD_LANES), 1)
    low = lane < 64
    per_head = lambda t, pick: jnp.concatenate([jnp.sum(jnp.where(pick(0), t, 0.0), axis=-1, keepdims=True),
                                                jnp.sum(jnp.where(pick(1), t, 0.0), axis=-1, keepdims=True)], axis=0)
    lse = _each(lambda l: per_head(l, lambda e: lane == 64 * e), lse2s)
    delta = _each(lambda d, y: per_head(d * y, lambda e: low if e == 0 else jnp.logical_not(low)), do2s, y2s)
    qs = _each(lambda q2: _stack_heads(q2 * LOG2E), q2s)
    dos = _each(_stack_heads, do2s)
    sc = _each(lambda q, k, sl: _attn_scores(q, k, sl, dil, dist), qs, kcats, slopes)
    pr = _each(lambda s, l: jnp.exp2(s - l), sc, lse)
    dp = _each(_nt, dos, vcats)
    ds = _each(lambda p, d, dl: (p * (d - dl)).astype(BF16), pr, dp, delta)
    dq = _each(lambda d, k: _unstack_heads(_nn(d, k)), ds, kcats)
    dk = _each(lambda d, q: _tn(d, q) * LN2, ds, qs)
    dv = _each(lambda p, d: _tn(p.astype(BF16), d), pr, dos)
    return list(zip(dq, dk, dv))


def _attn_specs(dil):
    rows = ATTN_BLOCK * dil
    if dil == 1:
        cur = lambda at: pl.BlockSpec((rows, 512), lambda n: (n, at // 512))
        prev = lambda at: pl.BlockSpec((rows, 512), lambda n: (jnp.maximum(n - 1, 0), at // 512))
    else:
        cur = lambda at: pl.BlockSpec((rows, HEAD_LANES), lambda n, p: (n, at // HEAD_LANES + p))
        prev = lambda at: pl.BlockSpec((rows, HEAD_LANES), lambda n, p: (jnp.maximum(n - 1, 0), at // HEAD_LANES + p))
    return cur, prev


def _attn_loop(dil, one_group):
    if dil == 1:
        one_group([(slice(None), pl.ds(p * HEAD_LANES, HEAD_LANES), p) for p in range(ATTN_GROUP)])
    else:
        p = pl.program_id(1)

        def step(g, carry):
            one_group([(pl.ds(g * ATTN_GROUP + j, ATTN_BLOCK, stride=dil), slice(None), p) for j in range(ATTN_GROUP)])
            return carry

        if dil == ATTN_GROUP:
            step(0, 0)
        else:
            lax.fori_loop(0, dil // ATTN_GROUP, step, 0)


def _dil_attn_fwd(qa, ka, proj, dil, *, name):
    s = qa.shape[0]

    def body(q_ref, kp_ref, kc_ref, vp_ref, vc_ref, o_ref, lse_ref):
        dist = _attn_distance(pl.program_id(0) == 0)

        def one_group(items):
            both = lambda a, b: [jnp.concatenate([a[rows, cols], b[rows, cols]], axis=0).astype(BF16) for rows, cols, _ in items]
            outs = _attn_group_fwd([q_ref[rows, cols] for rows, cols, _ in items], both(kp_ref, kc_ref), both(vp_ref, vc_ref),
                                   [_pair_slopes(p) for _, _, p in items], dil, dist)
            for (rows, cols, _), (o2, lse2) in zip(items, outs):
                o_ref[rows, cols] = o2
                lse_ref[rows, cols] = lse2

        _attn_loop(dil, one_group)

    cur, prev = _attn_specs(dil)
    grid = (s // ATTN_BLOCK,) if dil == 1 else (s // (ATTN_BLOCK * dil), 4)
    return pl.pallas_call(
        body, name=name, grid=grid, in_specs=[cur(0), prev(0), cur(0), prev(C_AV), cur(C_AV)], out_specs=[cur(0), cur(0)],
        out_shape=[jax.ShapeDtypeStruct((s, 512), F32)] * 2,
        compiler_params=_params(*["parallel"] * len(grid)))(qa, ka, ka, proj, proj)


def _attn_merge(branches, y_gla, *, name):
    s = y_gla.shape[0]
    tm = ROW_TILE

    def body(o0, l0, o1, l1, o2, l2, yg_ref, mixed_ref, y_ref, lse_ref):
        m = jnp.maximum(jnp.maximum(l0[...], l1[...]), l2[...])
        w0, w1, w2 = jnp.exp2(l0[...] - m), jnp.exp2(l1[...] - m), jnp.exp2(l2[...] - m)
        zs = w0 + w1 + w2
        y = (w0 * o0[...] + w1 * o1[...] + w2 * o2[...]) / zs
        y_ref[...] = y
        lse_ref[...] = m + jnp.log2(zs)
        mixed_ref[:, 0:512] = yg_ref[...]
        mixed_ref[:, 512:1024] = y.astype(BF16)

    blk = pl.BlockSpec((tm, 512), lambda i: (i, 0))
    args = [t for pair in branches for t in pair]
    return pl.pallas_call(
        body, name=name, grid=(s // tm,), in_specs=[blk] * 7,
        out_specs=[pl.BlockSpec((tm, 1024), lambda i: (i, 0)), blk, blk],
        out_shape=[jax.ShapeDtypeStruct((s, 1024), BF16), jax.ShapeDtypeStruct((s, 512), F32),
                   jax.ShapeDtypeStruct((s, 512), F32)],
        compiler_params=_params("parallel"))(*args, y_gla)


def _dil_attn_bwd(qa, ka, proj, y_att, lse, dmixed, dil, *, name):
    s = qa.shape[0]
    blk = ATTN_BLOCK

    def body(q_ref, kp_ref, kc_ref, vp_ref, vc_ref, y_ref, lse_ref, do_ref, dq_ref, dkc_ref, dkp_ref, dvc_ref, dvp_ref):
        dist = _attn_distance(pl.program_id(0) == 0)

        def one_group(items):
            both = lambda a, b: [jnp.concatenate([a[rows, cols], b[rows, cols]], axis=0).astype(BF16) for rows, cols, _ in items]
            at = lambda ref: [ref[rows, cols] for rows, cols, _ in items]
            outs = _attn_group_bwd(at(q_ref), both(kp_ref, kc_ref), both(vp_ref, vc_ref), at(do_ref), at(y_ref), at(lse_ref),
                                   [_pair_slopes(p) for _, _, p in items], dil, dist)
            for (rows, cols, _), (dq, dk, dv) in zip(items, outs):
                dq_ref[rows, cols] = dq
                dkp_ref[rows, cols] = dk[0:blk]
                dkc_ref[rows, cols] = dk[blk:2 * blk]
                dvp_ref[rows, cols] = dv[0:blk]
                dvc_ref[rows, cols] = dv[blk:2 * blk]

        _attn_loop(dil, one_group)

    cur, prev = _attn_specs(dil)
    grid = (s // blk,) if dil == 1 else (s // (blk * dil), 4)
    return pl.pallas_call(
        body, name=name, grid=grid,
        in_specs=[cur(0), prev(0), cur(0), prev(C_AV), cur(C_AV), cur(0), cur(0), cur(512)], out_specs=[cur(0)] * 5,
        out_shape=[jax.ShapeDtypeStruct((s, 512), F32)] * 5, compiler_params=_params(*["parallel"] * len(grid)),
    )(qa, ka, ka, proj, proj, y_att, lse, dmixed)


def _attn_post(parts, proj, qg, kg, *, name):
    s = proj.shape[0]
    tm = ATTN_BLOCK
    nblk = s // tm

    def body(*refs):
        ins, (q_ref, k_ref, qg_ref, kg_ref, dq_out, dk_out, dv_out, sums_ref) = refs[:15], refs[15:]
        i = pl.program_id(0)

        @pl.when(i == 0)
        def _():
            sums_ref[...] = jnp.zeros_like(sums_ref)

        dq = jnp.zeros((tm, 512), F32)
        dk = jnp.zeros((tm, 512), F32)
        dv = jnp.zeros((tm, 512), F32)
        for g, dil in enumerate(DILATIONS):
            dq_r, dkc_r, dkp_r, dvc_r, dvp_r = ins[5 * g:5 * g + 5]
            inside = (i + dil < nblk).astype(F32)
            dq = dq + dq_r[...]
            dk = dk + dkc_r[...] + inside * dkp_r[...]
            dv = dv + dvc_r[...] + inside * dvp_r[...]
        dv_out[...] = dv.astype(BF16)
        for row, (x_ref, g_ref, dy, out, post) in enumerate(((q_ref, qg_ref, dq, dq_out, 0.125), (k_ref, kg_ref, dk, dk_out, 1.0))):
            x = x_ref[...]
            rs = lax.rsqrt(_head_sums(x * x) * (1.0 / 64) + EPS)
            xn = x * rs
            dy = dy * post
            sums_ref[row] += _fold8(dy * xn)
            dn = dy * g_ref[...]
            out[...] = (rs * (dn - xn * (_head_sums(dn * xn) * (1.0 / 64)))).astype(BF16)

        @pl.when(i == nblk - 1)
        def _():
            _spread_total(sums_ref)

    here = pl.BlockSpec((tm, 512), lambda i: (i, 0))
    specs = []
    for dil in DILATIONS:
        later = pl.BlockSpec((tm, 512), lambda i, dil=dil: (jnp.minimum(i + dil, nblk - 1), 0))
        specs += [here, here, later, here, later]
    col = lambda at: pl.BlockSpec((tm, 512), lambda i: (i, at // 512))
    vec = pl.BlockSpec((1, 512), lambda i: (0, 0))
    return pl.pallas_call(
        body, name=name, grid=(nblk,), in_specs=specs + [col(C_AQ), col(C_AK), vec, vec],
        out_specs=[here, here, here, pl.BlockSpec((2, 8, 512), lambda i: (0, 0, 0))],
        out_shape=[jax.ShapeDtypeStruct((s, 512), BF16)] * 3 + [jax.ShapeDtypeStruct((2, 8, 512), F32)],
        compiler_params=_params("arbitrary"))(*[t for part in parts for t in part], proj, proj, qg, kg)


FFN_TM, FFN_TN = 256, 1408
HALO = 16


def _conv3(u_ref, halo_ref, w_ref, b_ref, first):
    u = u_ref[...].astype(F32)
    ext = jnp.concatenate([jnp.where(first, 0.0, halo_ref[...].astype(F32)), u], axis=0)
    u1 = pltpu.roll(ext, 1, 0)[HALO:]
    u2 = pltpu.roll(ext, 2, 0)[HALO:]
    return b_ref[...] + w_ref[0:1, :] * u2 + w_ref[1:2, :] * u1 + w_ref[2:3, :] * u, u, u1, u2


def _ffn_specs(tm, tn):
    nj = D_FF // tn
    blk = lambda half: pl.BlockSpec((tm, tn), lambda j, i: (i, j + half * nj))
    halo = lambda half: pl.BlockSpec((HALO, tn), lambda j, i: (jnp.maximum(i * (tm // HALO) - 1, 0), j + half * nj))
    wspec = lambda half: pl.BlockSpec((3, tn), lambda j, i: (0, j + half * nj))
    bspec = lambda half: pl.BlockSpec((1, tn), lambda j, i: (0, j + half * nj))
    return [blk(0), halo(0), blk(1), halo(1), wspec(0), wspec(1), bspec(0), bspec(1)]


def _conv_swiglu_fwd(u, conv_w, conv_b, *, name):
    s = u.shape[0]
    tm, tn = FFN_TM, FFN_TN

    def body(ug_ref, hg_ref, uv_ref, hv_ref, wg_ref, wv_ref, bg_ref, bv_ref, act_ref, uc_ref):
        first = pl.program_id(1) == 0
        cg = _conv3(ug_ref, hg_ref, wg_ref, bg_ref, first)[0]
        cv = _conv3(uv_ref, hv_ref, wv_ref, bv_ref, first)[0]
        act_ref[...] = (cg * _sigmoid(cg) * cv).astype(BF16)
        uc_ref[0] = cg.astype(BF16)
        uc_ref[1] = cv.astype(BF16)

    return pl.pallas_call(
        body, name=name, grid=(D_FF // tn, s // tm), in_specs=_ffn_specs(tm, tn),
        out_specs=[pl.BlockSpec((tm, tn), lambda j, i: (i, j)), pl.BlockSpec((2, tm, tn), lambda j, i: (0, i, j))],
        out_shape=[jax.ShapeDtypeStruct((s, D_FF), BF16), jax.ShapeDtypeStruct((2, s, D_FF), BF16)],
        compiler_params=_params("parallel", "parallel"))(u, u, u, u, conv_w, conv_w, conv_b, conv_b)


def _swiglu_bwd(uc, dact, *, name):
    _, s, _ = uc.shape
    tm, tn = FFN_TM, FFN_TN

    def body(uc_ref, da_ref, duc_ref, sums_ref):
        i = pl.program_id(1)

        @pl.when(i == 0)
        def _():
            sums_ref[...] = jnp.zeros_like(sums_ref)

        cg, cv, da = uc_ref[0].astype(F32), uc_ref[1].astype(F32), da_ref[...].astype(F32)
        sg = _sigmoid(cg)
        dg = da * cv * (sg * (1.0 + cg * (1.0 - sg)))
        dv = da * (cg * sg)
        duc_ref[0] = dg.astype(BF16)
        duc_ref[1] = dv.astype(BF16)
        sums_ref[0] += _fold8(dg)
        sums_ref[1] += _fold8(dv)

        @pl.when(i == s // tm - 1)
        def _():
            _spread_total(sums_ref)

    pair = pl.BlockSpec((2, tm, tn), lambda j, i: (0, i, j))
    return pl.pallas_call(
        body, name=name, grid=(D_FF // tn, s // tm), in_specs=[pair, pl.BlockSpec((tm, tn), lambda j, i: (i, j))],
        out_specs=[pair, pl.BlockSpec((2, 8, tn), lambda j, i: (0, 0, j))],
        out_shape=[jax.ShapeDtypeStruct((2, s, D_FF), BF16), jax.ShapeDtypeStruct((2, 8, D_FF), F32)],
        compiler_params=_params("parallel", "arbitrary"))(uc, dact)


def _conv_bwd(duc, u, conv_w, *, name):
    _, s, _ = duc.shape
    tm, tn = FFN_TM, FFN_TN
    nj, ni = D_FF // tn, s // tm

    def body(d_ref, halo_ref, u_ref, w_ref, du_ref, sums_ref):
        i = pl.program_id(2)

        @pl.when(i == 0)
        def _():
            sums_ref[...] = jnp.zeros_like(sums_ref)

        d = d_ref[0].astype(F32)
        ext = jnp.concatenate([d, jnp.where(i == ni - 1, 0.0, halo_ref[0].astype(F32))], axis=0)
        n = tm + HALO
        d1 = pltpu.roll(ext, n - 1, 0)[:tm]
        d2 = pltpu.roll(ext, n - 2, 0)[:tm]
        du_ref[...] = (w_ref[2:3, :] * d + w_ref[1:2, :] * d1 + w_ref[0:1, :] * d2).astype(BF16)
        uv = u_ref[...].astype(F32)
        for t, shifted in enumerate((d2, d1, d)):
            sums_ref[0, t] += _fold8(shifted * uv)

        @pl.when(i == ni - 1)
        def _():
            _spread_total(sums_ref)

    return pl.pallas_call(
        body, name=name, grid=(2, nj, ni),
        in_specs=[pl.BlockSpec((1, tm, tn), lambda g, j, i: (g, i, j)),
                  pl.BlockSpec((1, HALO, tn), lambda g, j, i: (g, jnp.minimum((i + 1) * (tm // HALO), s // HALO - 1), j)),
                  pl.BlockSpec((tm, tn), lambda g, j, i: (i, g * nj + j)),
                  pl.BlockSpec((3, tn), lambda g, j, i: (0, g * nj + j))],
        out_specs=[pl.BlockSpec((tm, tn), lambda g, j, i: (i, g * nj + j)),
                   pl.BlockSpec((1, 3, 8, tn), lambda g, j, i: (g, 0, 0, j))],
        out_shape=[jax.ShapeDtypeStruct((s, 2 * D_FF), BF16), jax.ShapeDtypeStruct((2, 3, 8, D_FF), F32)],
        compiler_params=_params("parallel", "parallel", "arbitrary"))(duc, duc, u, conv_w)


def _loss_head(x1, ffn, gate, target, *, name):
    s, d = x1.shape
    tm = ROW_TILE

    def body(x_ref, f_ref, g_ref, t_ref, dy_ref, df_ref, sums_ref):
        i = pl.program_id(0)

        @pl.when(i == 0)
        def _():
            sums_ref[...] = jnp.zeros_like(sums_ref)

        f = f_ref[...]
        err = x_ref[...] + g_ref[...] * f - t_ref[...]
        dy = err * (1.0 / d)
        dy_ref[...] = dy
        df_ref[...] = (g_ref[...] * dy).astype(BF16)
        sums_ref[0] += _fold8(dy * f)
        sums_ref[1] += _fold8(err * err)

        @pl.when(i == s // tm - 1)
        def _():
            _spread_total(sums_ref)

    row = pl.BlockSpec((tm, d), lambda i: (i, 0))
    return pl.pallas_call(
        body, name=name, grid=(s // tm,), in_specs=[row, row, pl.BlockSpec((1, d), lambda i: (0, 0)), row],
        out_specs=[row, row, pl.BlockSpec((2, 8, d), lambda i: (0, 0, 0))],
        out_shape=[jax.ShapeDtypeStruct((s, d), F32), jax.ShapeDtypeStruct((s, d), BF16), jax.ShapeDtypeStruct((2, 8, d), F32)],
        compiler_params=_params("arbitrary"))(x1, ffn, gate, target)


def _adamw(w, g, m, v, *, name):
    rows, cols = w.shape
    tm = next((t for t in range(ROW_TILE, 7, -8) if rows % t == 0), rows)

    def body(w_ref, g_ref, m_ref, v_ref, d_ref, mo_ref, vo_ref):
        gv = g_ref[...]
        mn = ADAM_B1 * m_ref[...] + (1.0 - ADAM_B1) * gv
        vn = ADAM_B2 * v_ref[...] + (1.0 - ADAM_B2) * (gv * gv)
        m_hat = mn / (1.0 - ADAM_B1 ** ADAM_STEP)
        v_hat = vn / (1.0 - ADAM_B2 ** ADAM_STEP)
        d_ref[...] = -ADAM_LR * (m_hat / (jnp.sqrt(v_hat) + ADAM_EPS) + ADAM_WD * w_ref[...])
        mo_ref[...] = mn
        vo_ref[...] = vn

    blk = pl.BlockSpec((tm, cols), lambda i: (i, 0))
    return pl.pallas_call(
        body, name=name, grid=(rows // tm,), in_specs=[blk] * 4, out_specs=[blk] * 3,
        out_shape=[jax.ShapeDtypeStruct((rows, cols), F32)] * 3, compiler_params=_params("parallel"))(w, g, m, v)


def _colsum(t):
    return t[..., 0, :]


def _in_proj_layout(w_in):
    pad = jnp.zeros((w_in.shape[0], PROJ_W - C_LR - GLA_GATE_RANK), w_in.dtype)
    return jnp.concatenate([w_in[:, :1536], w_in[:, 1552:], w_in[:, 1536:1552], pad], axis=1)


def _in_proj_grad_layout(g):
    return jnp.concatenate([g[:, :1536], g[:, C_LR:C_LR + GLA_GATE_RANK], g[:, 1536:C_LR]], axis=1)


def _gate_layout(gla_w_gate):
    return jnp.pad(gla_w_gate, ((0, HEAD_LANES - GLA_GATE_RANK), (0, 0))).astype(BF16)


def _local_step(x, target, mod, wi, wo, ffn_weights, ffn_grads_ready, attn_grads_ready, conv_w, conv_b, wg, bg, gn, qg, kg, n1g, n2g):
    d = D_MODEL
    sh1, sc1, g1, sh2, sc2, g2 = [mod[:, i * d:(i + 1) * d] for i in range(6)]
    qg8, kg8 = jnp.tile(qg, (1, 8)), jnp.tile(kg, (1, 8))

    _, h1, h1_t = _norm_mod_fwd(x, None, None, n1g, sc1, sh1, name="norm1_fwd")
    proj = _mm(h1, wi, tm=1024, tn=PROJ_W, tk=d, name="in_proj")
    o_raw, y_gla, states = _gla_fwd(proj, wg, bg, gn, name="gla_fwd")
    qa, ka = _attn_prep(proj, qg8, kg8, name="attn_prep")
    branches = [_dil_attn_fwd(qa, ka, proj, dil, name=f"attn_fwd_d{dil}") for dil in DILATIONS]
    mixed, y_att, lse = _attn_merge(branches, y_gla, name="attn_merge")
    attn_out = _mm(mixed, wo, tm=1024, tn=d, tk=d, name="out_proj")
    x1, h2, h2_t = _norm_mod_fwd(x, attn_out, g1, n2g, sc2, sh2, name="norm2_fwd")
    wup, wdown = ffn_weights(h2)
    u = _mm(h2, wup, out_dtype=BF16, tm=1024, tn=D_FF, tk=d, name="up_proj")
    act, uc = _conv_swiglu_fwd(u, conv_w, conv_b, name="conv_swiglu_fwd")
    ffn = _mm(act, wdown, tm=1024, tn=d, tk=D_FF, name="down_proj")
    dy, dffn, head_sums = _loss_head(x1, ffn, g2, target, name="loss_head")

    dact = _mm(dffn, wdown, tb=True, out_dtype=BF16, tm=1024, tn=D_FF, tk=d, name="down_proj_dx")
    g_wdown, g_wdown_b = _mm(act, dffn, ta=True, tm=1408, tn=d, tk=1024, also_bf16=True, name="down_proj_dw")
    duc, bias_sums = _swiglu_bwd(uc, dact, name="swiglu_bwd")
    du, tap_sums = _conv_bwd(duc, u, conv_w, name="conv_bwd")
    dh2 = _mm(du, wup, tb=True, tm=1024, tn=d, tk=D_FF, name="up_proj_dx")
    g_wup, g_wup_b = _mm(h2_t, du, tm=d, tn=1408, tk=1024, shard_cols=True, also_bf16=True, name="up_proj_dw")
    token = ffn_grads_ready(g_wup_b, g_wdown_b)
    g1_late = g1 if token is None else g1 + token[0:1, 0:1]
    dx1, dao, n2_sums = _norm_mod_bwd(x1, dh2, dy, n2g, sc2, attn_out, g1_late, name="norm2_bwd")

    dmixed = _mm(dao, wo, tb=True, tm=1024, tn=d, tk=d, name="out_proj_dx")
    g_wo = _mm(mixed, dao, ta=True, tm=d, tn=d, tk=1024, name="out_proj_dw")
    dgq, dgk, dgv, dgr, dlr, g_wg, gla_sums = _gla_bwd(proj, wg, bg, gn, o_raw, states, dmixed, name="gla_bwd")
    parts = [_dil_attn_bwd(qa, ka, proj, y_att, lse, dmixed, dil, name=f"attn_bwd_d{dil}") for dil in DILATIONS]
    daq, dak, dav, qk_sums = _attn_post(parts, proj, qg8, kg8, name="attn_post")
    dproj = jnp.concatenate([dgq, dgk, dgv, dgr, daq, dak, dav, dlr], axis=1)
    g_wi = _mm(h1_t, dproj, tm=512, tn=PROJ_W, tk=1024, name="in_proj_dw")
    token = attn_grads_ready(g_wi, g_wo)
    sc1_late = sc1 if token is None else sc1 + token[0:1, 0:1]
    dh1 = _mm(dproj, wi, tb=True, tm=1024, tn=d, tk=PROJ_W, name="in_proj_dx")
    grad_x, _, n1_sums = _norm_mod_bwd(x, dh1, dx1, n1g, sc1_late, None, None, name="norm1_bwd")

    n1, n2, hs, taps, cb = _colsum(n1_sums), _colsum(n2_sums), _colsum(head_sums), _colsum(tap_sums), _colsum(bias_sums)
    gs, qs = _colsum(gla_sums), _colsum(qk_sums)
    dmod = jnp.concatenate([n1[1], n1[0] * n1g[0], n2[2], n2[1], n2[0] * n2g[0], hs[0]])
    small = dict(
        dmod=dmod,
        norm1_g=n1[0] * (1.0 + sc1[0]), norm2_g=n2[0] * (1.0 + sc2[0]),
        gla_w_gate=g_wg[:GLA_GATE_RANK], gla_b_gate=gs[0, :256], gla_norm_g=gs[1].reshape(4, 128).sum(axis=0),
        q_norm_g=qs[0].reshape(8, 64).sum(axis=0), k_norm_g=qs[1].reshape(8, 64).sum(axis=0),
        conv_w=jnp.concatenate([taps[0], taps[1]], axis=1), conv_b=jnp.concatenate([cb[0], cb[1]]),
    )
    return head_sums[1], grad_x, (g_wi, g_wo, g_wup, g_wdown), small


N_DEV, N_CHIP = 8, 4
ANY = pl.BlockSpec(memory_space=pl.ANY)
VMEM_SPEC = pl.BlockSpec(memory_space=pltpu.VMEM)


def _place():
    x, y, c = lax.axis_index("x"), lax.axis_index("y"), lax.axis_index("c")
    other_chips = [(1 - x, y), (x, 1 - y), (1 - x, 1 - y)]
    return x, y, c, (x, y, 1 - c), other_chips


def _all_gather_small(v, *, name):
    m, n = v.shape

    def body(v_ref, out_ref, send_sems, recv_sems, local_sem):
        x, y, c, sibling, chips = _place()
        me = (x, y, c)

        def rows(px, py, pc):
            return out_ref.at[pl.ds((4 * px + 2 * py + pc) * m, m), :]

        def copy(k, block, to, src=None):
            return pltpu.make_async_remote_copy(
                src_ref=rows(*block) if src is None else src, dst_ref=rows(*block), send_sem=send_sems.at[k],
                recv_sem=recv_sems.at[k], device_id=to, device_id_type=MESH)

        mine = pltpu.make_async_copy(v_ref, rows(*me), local_sem)
        mine.start()
        first = [copy(0, me, sibling, src=v_ref)]
        first += [copy(1 + j, me, (*chip, c), src=v_ref) for j, chip in enumerate(chips)]
        for cp in first:
            cp.start()
        passed = [copy(4 + j, (*chip, c), sibling) for j, chip in enumerate(chips)]
        for j, chip in enumerate(chips):
            copy(1 + j, (*chip, c), me).wait_recv()
            passed[j].start()
        copy(0, sibling, me).wait_recv()
        for j, chip in enumerate(chips):
            copy(4 + j, (*chip, 1 - c), me).wait_recv()
        for cp in first + passed:
            cp.wait_send()
        mine.wait()

    return pl.pallas_call(
        body, name=name, out_shape=jax.ShapeDtypeStruct((N_DEV * m, n), v.dtype), in_specs=[VMEM_SPEC], out_specs=VMEM_SPEC,
        scratch_shapes=[pltpu.SemaphoreType.DMA((7,)), pltpu.SemaphoreType.DMA((7,)), pltpu.SemaphoreType.DMA],
    )(v)


def _gather_weight_shards(shards, *, name):
    nw = len(shards)

    def body(*refs):
        srcs, outs, (send_sems, recv_sems) = refs[:nw], refs[nw:2 * nw], refs[2 * nw:]
        x, y, c, sibling, chips = _place()
        index = lambda chip: 2 * chip[0] + chip[1]

        def copy(w, k, src, dst, to):
            return pltpu.make_async_remote_copy(src_ref=src, dst_ref=dst, send_sem=send_sems.at[6 * w + k],
                                                recv_sem=recv_sems.at[6 * w + k], device_id=to, device_id_type=MESH)

        sent = []
        for w, (src_ref, out_ref) in enumerate(zip(srcs, outs)):
            for k, chip in enumerate(chips):
                sent.append(copy(w, k, src_ref.at[c], out_ref.at[2 * x + y, c], (*chip, c)))
                sent[-1].start()
        for w, out_ref in enumerate(outs):
            for k, chip in enumerate(chips):
                landed = out_ref.at[index(chip), c]
                copy(w, k, landed, landed, (*chip, c)).wait_recv()
                sent.append(copy(w, 3 + k, landed, landed, sibling))
                sent[-1].start()
        for w, out_ref in enumerate(outs):
            for k, chip in enumerate(chips):
                passed_on = out_ref.at[index(chip), 1 - c]
                copy(w, 3 + k, passed_on, passed_on, sibling).wait_recv()
        for cp in sent:
            cp.wait_send()

    return pl.pallas_call(
        body, name=name, out_shape=[jax.ShapeDtypeStruct((N_CHIP, *s.shape), s.dtype) for s in shards],
        in_specs=[ANY] * nw, out_specs=[ANY] * nw,
        scratch_shapes=[pltpu.SemaphoreType.DMA((6 * nw,)), pltpu.SemaphoreType.DMA((6 * nw,))],
    )(*shards)


HBM_SPEC = pl.BlockSpec(memory_space=pltpu.HBM)
SEM_SPEC = pl.BlockSpec(memory_space=pltpu.SEMAPHORE)
DATAFLOW_EFFECT = pltpu.SideEffectType.DATAFLOW_SIDE_EFFECTING


def _late_copies(srcs, lands, send_sems, recv_sems):
    x, y, c, _, chips = _place()
    return [pltpu.make_async_remote_copy(
        src_ref=src.at[c], dst_ref=land.at[2 * x + y, c], send_sem=send_sems.at[6 * w + 2 * r + core],
        recv_sem=recv_sems.at[6 * w + 2 * r + c], device_id=(*chip, core), device_id_type=MESH)
        for w, (src, land) in enumerate(zip(srcs, lands)) for r, chip in enumerate(chips) for core in range(2)]


def _gather_late_start(own, after, *, name):
    nw = len(own)

    def body(*refs):
        srcs, lands, send_sems, recv_sems, token = refs[:nw], refs[nw:2 * nw], refs[2 * nw + 1], refs[2 * nw + 2], refs[-1]
        for cp in _late_copies(srcs, lands, send_sems, recv_sems):
            cp.start()
        token[...] = jnp.zeros_like(token)

    lands = [pltpu.with_memory_space_constraint(lax.empty((N_CHIP, *s.shape), s.dtype), pltpu.HBM) for s in own]
    own = [pltpu.with_memory_space_constraint(s, pltpu.HBM) for s in own]
    out = pl.pallas_call(
        body, name=name,
        out_shape=(pltpu.SemaphoreType.DMA((6 * nw,)), pltpu.SemaphoreType.DMA((6 * nw,)),
                   *[pltpu.HBM(s.shape, s.dtype) for s in own], *[pltpu.HBM(s.shape, s.dtype) for s in lands],
                   jax.ShapeDtypeStruct((8, 128), F32)),
        in_specs=[HBM_SPEC] * (2 * nw) + [ANY], out_specs=(SEM_SPEC, SEM_SPEC, *[HBM_SPEC] * (2 * nw), VMEM_SPEC),
        input_output_aliases={i: 2 + i for i in range(2 * nw)},
        compiler_params=pltpu.CompilerParams(has_side_effects=DATAFLOW_EFFECT))(*own, *lands, after)
    return out[0], out[1], out[2:2 + nw], out[2 + nw:2 + 2 * nw], out[-1]


def _gather_late_wait(send_sems, recv_sems, own, lands, after, *, name):
    nw = len(own)

    def body(*refs):
        srcs, lands_in, send_sems, recv_sems = refs[:nw], refs[nw:2 * nw], refs[2 * nw], refs[2 * nw + 1]
        x, y, c, _, chips = _place()
        for cp in _late_copies(srcs, lands_in, send_sems, recv_sems):
            cp.wait_send()
        for w, (src, land) in enumerate(zip(srcs, lands_in)):
            for r, chip in enumerate(chips):
                for core in range(2):
                    pltpu.make_async_remote_copy(
                        src_ref=src.at[c], dst_ref=land.at[2 * chip[0] + chip[1], core], send_sem=send_sems.at[6 * w + 2 * r + core],
                        recv_sem=recv_sems.at[6 * w + 2 * r + core], device_id=(*chip, core), device_id_type=MESH).wait_recv()

    out = pl.pallas_call(
        body, name=name, out_shape=(*[pltpu.HBM(s.shape, s.dtype) for s in own], *[pltpu.HBM(s.shape, s.dtype) for s in lands]),
        in_specs=[HBM_SPEC] * (2 * nw) + [SEM_SPEC, SEM_SPEC, ANY], out_specs=tuple([HBM_SPEC] * (2 * nw)),
        input_output_aliases={i: i for i in range(2 * nw)},
        compiler_params=pltpu.CompilerParams(has_side_effects=DATAFLOW_EFFECT))(*own, *lands, send_sems, recv_sems, after)
    return out[:nw], out[nw:]


def _direct_reduce_copies(srcs, lands, send_sems, recv_sems):
    x, y, c, _, _ = _place()
    cps = []
    for w, (src, land) in enumerate(zip(srcs, lands)):
        for rel in range(1, N_DEV):
            tx, ty, tc = (1 - x if rel & 4 else x), (1 - y if rel & 2 else y), (1 - c if rel & 1 else c)
            cps.append(pltpu.make_async_remote_copy(
                src_ref=src.at[2 * tx + ty, tc], dst_ref=land.at[rel - 1], send_sem=send_sems.at[7 * w + rel - 1],
                recv_sem=recv_sems.at[7 * w + rel - 1], device_id=(tx, ty, tc), device_id_type=MESH))
    return cps


def _direct_reduce_start(grads, *, name):
    nw = len(grads)

    def body(*refs):
        srcs, lands, send_sems, recv_sems, token = refs[:nw], refs[nw:2 * nw], refs[2 * nw], refs[2 * nw + 1], refs[-1]
        for cp in _direct_reduce_copies(srcs, lands, send_sems, recv_sems):
            cp.start()
        token[...] = jnp.zeros_like(token)

    lands = [pltpu.with_memory_space_constraint(lax.empty((N_DEV - 1, *g.shape[2:]), g.dtype), pltpu.HBM) for g in grads]
    grads = [pltpu.with_memory_space_constraint(g, pltpu.HBM) for g in grads]
    out = pl.pallas_call(
        body, name=name,
        out_shape=(pltpu.SemaphoreType.DMA((7 * nw,)), pltpu.SemaphoreType.DMA((7 * nw,)),
                   *[pltpu.HBM(g.shape, g.dtype) for g in grads], *[pltpu.HBM(t.shape, t.dtype) for t in lands],
                   jax.ShapeDtypeStruct((8, 128), F32)),
        in_specs=[HBM_SPEC] * (2 * nw), out_specs=(SEM_SPEC, SEM_SPEC, *[HBM_SPEC] * (2 * nw), VMEM_SPEC),
        input_output_aliases={i: 2 + i for i in range(2 * nw)},
        compiler_params=pltpu.CompilerParams(has_side_effects=DATAFLOW_EFFECT))(*grads, *lands)
    return out[0], out[1], out[2:2 + nw], out[2 + nw:2 + 2 * nw], out[-1]


def _direct_reduce_wait(send_sems, recv_sems, grads, lands, after, *, name):
    nw = len(grads)

    def body(*refs):
        srcs, lands_in, send_sems, recv_sems = refs[:nw], refs[nw:2 * nw], refs[2 * nw], refs[2 * nw + 1]
        cps = _direct_reduce_copies(srcs, lands_in, send_sems, recv_sems)
        for cp in cps:
            cp.wait_send()
        for cp in cps:
            cp.wait_recv()

    out = pl.pallas_call(
        body, name=name, out_shape=(*[pltpu.HBM(g.shape, g.dtype) for g in grads], *[pltpu.HBM(t.shape, t.dtype) for t in lands]),
        in_specs=[HBM_SPEC] * (2 * nw) + [SEM_SPEC, SEM_SPEC, ANY], out_specs=tuple([HBM_SPEC] * (2 * nw)),
        input_output_aliases={i: i for i in range(2 * nw)},
        compiler_params=pltpu.CompilerParams(has_side_effects=DATAFLOW_EFFECT))(*grads, *lands, send_sems, recv_sems, after)
    return out[nw:]


def _direct_reduce_add(grad, landed, chip, core, *, name):
    _, r, n = grad.shape
    half = r // 2
    tr = _row_tile(half)
    nb = half // tr

    def body(chip_ref, core_ref, g_ref, t_ref, o_ref):
        acc = g_ref[0]
        for k in range(N_DEV - 1):
            acc = acc + t_ref[k].astype(F32)
        o_ref[...] = acc

    return pl.pallas_call(
        body, name=name,
        grid_spec=pltpu.PrefetchScalarGridSpec(
            num_scalar_prefetch=2, grid=(nb,),
            in_specs=[pl.BlockSpec((1, tr, n), lambda i, chip_ref, core_ref: (chip_ref[0], core_ref[0] * nb + i, 0)),
                      pl.BlockSpec((N_DEV - 1, tr, n), lambda i, chip_ref, core_ref: (0, i, 0))],
            out_specs=pl.BlockSpec((tr, n), lambda i, chip_ref, core_ref: (i, 0))),
        out_shape=jax.ShapeDtypeStruct((half, n), F32), compiler_params=_params("parallel"))(chip, core, grad, landed)


def _share_halves(halves, *, name):
    nw = len(halves)

    def body(*refs):
        srcs, outs, (send_sems, recv_sems) = refs[:nw], refs[nw:2 * nw], refs[2 * nw:]
        _, _, _, sibling, _ = _place()
        cps = [pltpu.make_async_remote_copy(src_ref=src_ref, dst_ref=out_ref, send_sem=send_sems.at[w], recv_sem=recv_sems.at[w],
                                            device_id=sibling, device_id_type=MESH)
               for w, (src_ref, out_ref) in enumerate(zip(srcs, outs))]
        for cp in cps:
            cp.start()
        for cp in cps:
            cp.wait()

    return pl.pallas_call(
        body, name=name, out_shape=[jax.ShapeDtypeStruct(h.shape, h.dtype) for h in halves],
        in_specs=[ANY] * nw, out_specs=[ANY] * nw,
        scratch_shapes=[pltpu.SemaphoreType.DMA((nw,)), pltpu.SemaphoreType.DMA((nw,))])(*halves)


def _row_tile(rows, limit=256):
    return next(t for t in range(limit, 15, -16) if rows % t == 0)


def _sum_devices(gathered, *, name):
    _, m, n = gathered.shape

    def body(g_ref, tot_ref, loss_ref):
        tot = g_ref[0]
        for dev in range(1, N_DEV):
            tot = tot + g_ref[dev]
        tot_ref[...] = tot
        loss_ref[...] = jnp.full((8, n), (0.5 / D_MODEL) * jnp.sum(tot[0:8]), F32)

    return pl.pallas_call(body, name=name, in_specs=[VMEM_SPEC], out_specs=[VMEM_SPEC, VMEM_SPEC],
                          out_shape=[jax.ShapeDtypeStruct((m, n), F32), jax.ShapeDtypeStruct((8, n), F32)])(gathered)


def _ada_mod(cond_all, w_ada_shard, *, name):
    tn = 512

    def body(a_ref, b_ref, o_ref):
        o_ref[...] = _nn(a_ref[...], b_ref[...], precision=HIGHEST)

    return pl.pallas_call(
        body, name=name, grid=(w_ada_shard.shape[1] // tn,),
        in_specs=[pl.BlockSpec(cond_all.shape, lambda j: (0, 0)), pl.BlockSpec((D_MODEL, tn), lambda j: (0, j))],
        out_specs=pl.BlockSpec((N_DEV, tn), lambda j: (0, j)),
        out_shape=jax.ShapeDtypeStruct((N_DEV, w_ada_shard.shape[1]), F32), compiler_params=_params("parallel"))(cond_all, w_ada_shard)


def _ada_grad(cond_all, dmod_cols, *, name):
    tm = 256

    def body(a_ref, b_ref, o_ref):
        o_ref[...] = lax.dot_general(a_ref[...], b_ref[...], (((0,), (0,)), ((), ())), precision=HIGHEST,
                                     preferred_element_type=F32)

    return pl.pallas_call(
        body, name=name, grid=(D_MODEL // tm,),
        in_specs=[pl.BlockSpec((N_DEV, tm), lambda i: (0, i)), pl.BlockSpec(dmod_cols.shape, lambda i: (0, 0))],
        out_specs=pl.BlockSpec((tm, dmod_cols.shape[1]), lambda i: (i, 0)),
        out_shape=jax.ShapeDtypeStruct((D_MODEL, dmod_cols.shape[1]), F32), compiler_params=_params("parallel"))(cond_all, dmod_cols)


def _silu_rows(c8, *, name):
    def body(c_ref, o_ref):
        cv = c_ref[...]
        o_ref[...] = cv * _sigmoid(cv)

    return pl.pallas_call(body, name=name, in_specs=[VMEM_SPEC], out_specs=VMEM_SPEC,
                          out_shape=jax.ShapeDtypeStruct(c8.shape, F32))(c8)


def _rows128(t, rows=None):
    flat = t.reshape(-1, 128)
    return flat if rows is None else jnp.pad(flat, ((0, rows - flat.shape[0]), (0, 0)))


def _from_col_shards(shards, r, n):
    return shards.reshape(N_CHIP, r, n).transpose(1, 0, 2).reshape(r, N_CHIP * n)


def kernel(x, c, w_ada, b_ada, norm1_g, w_in, gla_w_gate, gla_b_gate, gla_norm_g, q_norm_g, k_norm_g, w_out, norm2_g, w_up, conv_w, conv_b, w_down, loss_target, m_w_ada, m_b_ada, m_norm1_g, m_w_in, m_gla_w_gate, m_gla_b_gate, m_gla_norm_g, m_q_norm_g, m_k_norm_g, m_w_out, m_norm2_g, m_w_up, m_conv_w, m_conv_b, m_w_down, v_w_ada, v_b_ada, v_norm1_g, v_w_in, v_gla_w_gate, v_gla_b_gate, v_gla_norm_g, v_q_norm_g, v_k_norm_g, v_w_out, v_norm2_g, v_w_up, v_conv_w, v_conv_b, v_w_down):
    d = D_MODEL
    ax, ay, ac = lax.axis_index("x"), lax.axis_index("y"), lax.axis_index("c")
    chip, dev = 2 * ax + ay, 4 * ax + 2 * ay + ac

    cond = _silu_rows(jnp.broadcast_to(c, (8, d)), name="cond_silu")[0:1]
    small_in = jnp.concatenate([_rows128(cond), _rows128(conv_w[0]), _rows128(gla_w_gate[0])], axis=0)
    small_in = _rows128(small_in, 56)
    got = _all_gather_small(small_in, name="gather_small").reshape(N_DEV, 56, 128)
    cond_all = got[:, 0:8].reshape(N_DEV, d)
    conv_w_full = _from_col_shards(got[0::2, 8:41].reshape(N_CHIP, 3 * 1408 // 128, 128), 3, 1408)
    gate_full = _from_col_shards(got[0::2, 41:49].reshape(N_CHIP, 16 * 64 // 128, 128), GLA_GATE_RANK, 64)
    mod_part = _ada_mod(cond_all, w_ada[0], name="ada_mod")
    mod_got = _all_gather_small(_rows128(mod_part), name="gather_mod").reshape(N_DEV, N_DEV, 1536)
    mod_all = mod_got[0::2].transpose(1, 0, 2).reshape(N_DEV, 6 * d) + b_ada
    mod = lax.dynamic_slice_in_dim(mod_all, dev, 1, axis=0)

    own = [w[0].astype(BF16).reshape(2, w.shape[1] // 2, w.shape[2]) for w in (w_in, w_out, w_up, w_down)]
    with_own = lambda got, mine: [lax.dynamic_update_index_in_dim(t, o, chip, 0) for t, o in zip(got, mine)]
    got_in, got_out = with_own(_gather_weight_shards(own[:2], name="gather_weights"), own[:2])
    w_in_full = got_in.reshape(N_CHIP, d, 772).transpose(1, 0, 2).reshape(d, N_CHIP * 772)
    w_out_full = got_out.reshape(d, d)
    exchanged = mod_all[0:1, 0:1] + got_in[0, 0, 0:1, 0:1].astype(F32)
    send_sems, recv_sems, own_thru, lands, token = _gather_late_start(own[2:], exchanged, name="gather_late_start")
    mod = mod + token[0:1, 0:1]

    def ffn_weights(after):
        mine, landed = _gather_late_wait(send_sems, recv_sems, own_thru, lands, after, name="gather_late_wait")
        got_up, got_down = with_own(landed, mine)
        return got_up.reshape(N_CHIP, d, 1408).transpose(1, 0, 2).reshape(d, 2 * D_FF), got_down.reshape(D_FF, d)

    ffn_reduce, attn_reduce, attn_parts = [], [], []
    halves_of = lambda g: g.reshape(N_CHIP, 2, g.shape[-2] // 2, g.shape[-1])

    def ffn_grads_ready(g_wup_b, g_wdown_b):
        ffn_reduce.extend(_direct_reduce_start([halves_of(g_wup_b), halves_of(g_wdown_b.reshape(N_CHIP, D_FF // N_CHIP, d))],
                                               name="reduce_ffn_start"))
        return ffn_reduce[4]

    def attn_grads_ready(g_wi, g_wo):
        attn_parts.extend([_in_proj_grad_layout(g_wi).reshape(d, N_CHIP, 772).transpose(1, 0, 2), g_wo.reshape(N_CHIP, d // N_CHIP, d)])
        attn_reduce.extend(_direct_reduce_start([halves_of(g.astype(BF16)) for g in attn_parts], name="reduce_attn_start"))
        return attn_reduce[4]

    err2, grad_x, (g_wi, g_wo, g_wup, g_wdown), small = _local_step(
        x[0], loss_target[0], mod, _in_proj_layout(w_in_full), w_out_full, ffn_weights, ffn_grads_ready, attn_grads_ready,
        conv_w_full, conv_b,
        _gate_layout(gate_full), gla_b_gate, gla_norm_g, q_norm_g, k_norm_g, norm1_g, norm2_g)

    pieces = [err2[0], small["dmod"], small["norm1_g"], small["norm2_g"], small["gla_w_gate"].reshape(-1), small["gla_b_gate"],
              small["gla_norm_g"], small["q_norm_g"], small["k_norm_g"], small["conv_w"].reshape(-1), small["conv_b"]]
    sizes = [p.shape[0] for p in pieces]
    at = [sum(sizes[:i]) for i in range(len(sizes) + 1)]
    vec = _rows128(jnp.concatenate(pieces), 288)
    got = _all_gather_small(vec, name="gather_grads").reshape(N_DEV, 288, 128)
    total, loss8 = _sum_devices(got, name="sum_devices")
    total = total.reshape(-1)
    seg = lambda i: total[at[i]:at[i + 1]]
    dmod_all = got.reshape(N_DEV, -1)[:, at[1]:at[2]]
    g_small = dict(
        b_ada=seg(1)[None], norm1_g=seg(2)[None], norm2_g=seg(3)[None],
        gla_w_gate=lax.dynamic_slice_in_dim(seg(4).reshape(GLA_GATE_RANK, 256), chip * 64, 64, axis=1),
        gla_b_gate=seg(5)[None], gla_norm_g=seg(6)[None], q_norm_g=seg(7)[None], k_norm_g=seg(8)[None],
        conv_w=lax.dynamic_slice_in_dim(seg(9).reshape(3, 2 * D_FF), chip * 1408, 1408, axis=1), conv_b=seg(10)[None])
    dmod_cols = lax.dynamic_slice_in_dim(dmod_all.reshape(N_DEV, 6 * d), chip * 1536, 1536, axis=1)
    g_w_ada = _ada_grad(cond_all, dmod_cols, name="ada_grad")

    core_id, chip_id = jnp.reshape(ac, (1,)).astype(jnp.int32), jnp.reshape(chip, (1,)).astype(jnp.int32)
    landed = (_direct_reduce_wait(*attn_reduce[:4], grad_x, name="reduce_attn_wait")
              + _direct_reduce_wait(*ffn_reduce[:4], grad_x, name="reduce_ffn_wait"))
    own = attn_parts + [g_wup, g_wdown.reshape(N_CHIP, D_FF // N_CHIP, d)]
    summed = [_direct_reduce_add(g, t, chip_id, core_id, name=f"reduce_add_{tag}")
              for g, t, tag in zip(own, landed, ("w_in", "w_out", "w_up", "w_down"))]
    others = _share_halves(summed, name="share_pair")
    g_big = [jnp.concatenate([jnp.where(ac == 0, mine, other), jnp.where(ac == 0, other, mine)], axis=0)
             for mine, other in zip(summed, others)]

    grads = dict(w_ada=g_w_ada, w_in=g_big[0], w_out=g_big[1], w_up=g_big[2], w_down=g_big[3], **g_small)
    names = ["w_ada", "b_ada", "norm1_g", "w_in", "gla_w_gate", "gla_b_gate", "gla_norm_g", "q_norm_g", "k_norm_g", "w_out",
             "norm2_g", "w_up", "conv_w", "conv_b", "w_down"]
    ws = dict(w_ada=w_ada, b_ada=b_ada, norm1_g=norm1_g, w_in=w_in, gla_w_gate=gla_w_gate, gla_b_gate=gla_b_gate,
              gla_norm_g=gla_norm_g, q_norm_g=q_norm_g, k_norm_g=k_norm_g, w_out=w_out, norm2_g=norm2_g, w_up=w_up,
              conv_w=conv_w, conv_b=conv_b, w_down=w_down)
    ms = dict(w_ada=m_w_ada, b_ada=m_b_ada, norm1_g=m_norm1_g, w_in=m_w_in, gla_w_gate=m_gla_w_gate, gla_b_gate=m_gla_b_gate,
              gla_norm_g=m_gla_norm_g, q_norm_g=m_q_norm_g, k_norm_g=m_k_norm_g, w_out=m_w_out, norm2_g=m_norm2_g, w_up=m_w_up,
              conv_w=m_conv_w, conv_b=m_conv_b, w_down=m_w_down)
    vs = dict(w_ada=v_w_ada, b_ada=v_b_ada, norm1_g=v_norm1_g, w_in=v_w_in, gla_w_gate=v_gla_w_gate, gla_b_gate=v_gla_b_gate,
              gla_norm_g=v_gla_norm_g, q_norm_g=v_q_norm_g, k_norm_g=v_k_norm_g, w_out=v_w_out, norm2_g=v_norm2_g, w_up=v_w_up,
              conv_w=v_conv_w, conv_b=v_conv_b, w_down=v_w_down)
    g_out, d_out, m_out, v_out = [], [], [], []
    for nm in names:
        w2 = ws[nm].reshape(ws[nm].shape[-2:])
        g2 = grads[nm].reshape(w2.shape)
        dl, mn, vn = _adamw(w2, g2, ms[nm].reshape(w2.shape), vs[nm].reshape(w2.shape), name=f"adamw_{nm}")
        shape = ws[nm].shape
        g_out.append(g2.reshape(shape))
        d_out.append(dl.reshape(shape))
        m_out.append(mn.reshape(shape))
        v_out.append(vn.reshape(shape))
    return (loss8[0, 0], grad_x[None], *g_out, *d_out, *m_out, *v_out)
```

```python
import functools

import jax
import jax.numpy as jnp
from jax import lax
from jax.experimental import pallas as pl
from jax.experimental.pallas import tpu as pltpu

F32, BF16 = jnp.float32, jnp.bfloat16
HIGHEST = lax.Precision.HIGHEST
MESH = pl.DeviceIdType.MESH

D_MODEL = 1024
GLA_CHUNK = 64
GLA_GATE_TAU = 16.0
GLA_GATE_RANK = 16
HEAD_LANES = 128
ATTN_BLOCK = 128
DILATIONS = (1, 4, 16)
ALIBI_SLOPES = tuple(2.0 ** (-(h + 1)) for h in range(8))
D_FF = 2816
EPS = 1e-6
C_GQ, C_GK, C_GV, C_GR, C_AQ, C_AK, C_AV, C_LR, PROJ_W = 0, 256, 512, 1024, 1536, 2048, 2560, 3072, 3200
ADAM_LR, ADAM_B1, ADAM_B2, ADAM_EPS, ADAM_WD, ADAM_STEP = 0.001, 0.9, 0.999, 1e-08, 0.01, 10
VMEM_LIMIT_BYTES = 56 * 1024 * 1024
ROW_TILE = 256


def _params(*sem):
    return pltpu.CompilerParams(dimension_semantics=sem or None, vmem_limit_bytes=VMEM_LIMIT_BYTES)


def _nt(a, b):
    return lax.dot_general(a, b, (((1,), (1,)), ((), ())), preferred_element_type=F32)


def _tn(a, b):
    return lax.dot_general(a, b, (((0,), (0,)), ((), ())), preferred_element_type=F32)


def _nn(a, b, precision=None):
    return jnp.dot(a, b, preferred_element_type=F32, precision=precision)


def _split3(v):
    hi = v.astype(BF16)
    rest = v - hi.astype(F32)
    mid = rest.astype(BF16)
    return hi, mid, (rest - mid.astype(F32)).astype(BF16)


def _sum_right(v, ones):
    hi, mid, lo = _split3(v)
    return (_nn(lo, ones) + _nn(mid, ones)) + _nn(hi, ones)


def _sum_left(ones, v):
    hi, mid, lo = _split3(v)
    return (_nn(ones, lo) + _nn(ones, mid)) + _nn(ones, hi)


def _fold8(v):
    return v.reshape(v.shape[0] // 8, 8, v.shape[1]).sum(axis=0)


def _spread_total(ref):
    t = ref[...]
    ref[...] = jnp.broadcast_to(jnp.sum(t, axis=-2, keepdims=True), t.shape)


def _sigmoid(x):
    return 1.0 / (1.0 + jnp.exp(-x))


def _mm(a, b, *, ta=False, tb=False, out_dtype=F32, tm, tn, tk, shard_cols=False, also_bf16=False, name):
    (k_a, m) = a.shape if ta else a.shape[::-1]
    (k_b, n) = b.shape[::-1] if tb else b.shape
    assert k_a == k_b and m % tm == 0 and n % tn == 0 and k_a % tk == 0, (name, a.shape, b.shape)
    nk = k_a // tk
    assert nk == 1 or out_dtype == F32, name
    dims = (((0 if ta else 1,), (1 if tb else 0,)), ((), ()))

    def body(a_ref, b_ref, o_ref, *rounded):
        k = pl.program_id(2)
        part = lax.dot_general(a_ref[...].astype(BF16), b_ref[...].astype(BF16), dims, preferred_element_type=F32)
        if nk == 1:
            o_ref[...] = part.astype(out_dtype)
        else:
            @pl.when(k == 0)
            def _():
                o_ref[...] = part

            @pl.when(k > 0)
            def _():
                o_ref[...] += part

        if also_bf16:
            @pl.when(k == nk - 1)
            def _():
                rounded[0][...] = o_ref[...].astype(BF16)

    a_spec = pl.BlockSpec((tk, tm), lambda i, j, k: (k, i)) if ta else pl.BlockSpec((tm, tk), lambda i, j, k: (i, k))
    b_spec = pl.BlockSpec((tn, tk), lambda i, j, k: (j, k)) if tb else pl.BlockSpec((tk, tn), lambda i, j, k: (k, j))
    if shard_cols:
        o_spec, o_shape = pl.BlockSpec((None, tm, tn), lambda i, j, k: (j, i, 0)), (n // tn, m, tn)
    else:
        o_spec, o_shape = pl.BlockSpec((tm, tn), lambda i, j, k: (i, j)), (m, n)
    shapes = [jax.ShapeDtypeStruct(o_shape, out_dtype)] + ([jax.ShapeDtypeStruct(o_shape, BF16)] if also_bf16 else [])
    out = pl.pallas_call(
        body, name=name, grid=(m // tm, n // tn, nk), in_specs=[a_spec, b_spec], out_specs=[o_spec] * len(shapes),
        out_shape=shapes, compiler_params=_params("parallel", "parallel", "arbitrary"))(a, b)
    return out if also_bf16 else out[0]


def _norm_mod_fwd(x, branch, gate, gain, scale, shift, *, name):
    s, d = x.shape
    tm = ROW_TILE
    has_branch = branch is not None

    def body(*refs):
        if has_branch:
            x_ref, br_ref, gate_ref, gain_ref, sc_ref, sh_ref, x1_ref, h_ref, ht_ref = refs
            xv = x_ref[...] + gate_ref[...] * br_ref[...]
            x1_ref[...] = xv
        else:
            x_ref, gain_ref, sc_ref, sh_ref, h_ref, ht_ref = refs
            xv = x_ref[...]
        r = lax.rsqrt(jnp.mean(xv * xv, axis=-1, keepdims=True) + EPS)
        h = (xv * r) * gain_ref[...] * (1.0 + sc_ref[...]) + sh_ref[...]
        h_ref[...] = h.astype(BF16)
        ht_ref[...] = h.T.astype(BF16)

    row = pl.BlockSpec((tm, d), lambda i: (i, 0))
    col = pl.BlockSpec((d, tm), lambda i: (0, i))
    vec = pl.BlockSpec((1, d), lambda i: (0, 0))
    h_shapes = [jax.ShapeDtypeStruct((s, d), BF16), jax.ShapeDtypeStruct((d, s), BF16)]
    if has_branch:
        return pl.pallas_call(
            body, name=name, grid=(s // tm,), in_specs=[row, row, vec, vec, vec, vec], out_specs=[row, row, col],
            out_shape=[jax.ShapeDtypeStruct((s, d), F32)] + h_shapes,
            compiler_params=_params("parallel"))(x, branch, gate, gain, scale, shift)
    h, ht = pl.pallas_call(
        body, name=name, grid=(s // tm,), in_specs=[row, vec, vec, vec], out_specs=[row, col],
        out_shape=h_shapes, compiler_params=_params("parallel"))(x, gain, scale, shift)
    return x, h, ht


def _norm_mod_bwd(x, dh, dres, gain, scale, branch, gate, *, name):
    s, d = x.shape
    tm = ROW_TILE
    has_branch = branch is not None

    def body(*refs):
        if has_branch:
            x_ref, dh_ref, dres_ref, gain_ref, sc_ref, br_ref, gate_ref, dx_ref, dbr_ref, sums_ref = refs
        else:
            x_ref, dh_ref, dres_ref, gain_ref, sc_ref, dx_ref, sums_ref = refs
        i = pl.program_id(0)

        @pl.when(i == 0)
        def _():
            sums_ref[...] = jnp.zeros_like(sums_ref)

        xv, dhv = x_ref[...], dh_ref[...]
        r = lax.rsqrt(jnp.mean(xv * xv, axis=-1, keepdims=True) + EPS)
        xn = xv * r
        dxn = dhv * (gain_ref[...] * (1.0 + sc_ref[...]))
        dx = dres_ref[...] + r * (dxn - xn * jnp.mean(dxn * xn, axis=-1, keepdims=True))
        dx_ref[...] = dx
        sums_ref[0] += _fold8(dhv * xn)
        sums_ref[1] += _fold8(dhv)
        if has_branch:
            dbr_ref[...] = (gate_ref[...] * dx).astype(BF16)
            sums_ref[2] += _fold8(dx * br_ref[...])

        @pl.when(i == s // tm - 1)
        def _():
            _spread_total(sums_ref)

    row = pl.BlockSpec((tm, d), lambda i: (i, 0))
    vec = pl.BlockSpec((1, d), lambda i: (0, 0))
    sums = pl.BlockSpec((3, 8, d), lambda i: (0, 0, 0))
    sums_shape = jax.ShapeDtypeStruct((3, 8, d), F32)
    if has_branch:
        return pl.pallas_call(
            body, name=name, grid=(s // tm,), in_specs=[row, row, row, vec, vec, row, vec], out_specs=[row, row, sums],
            out_shape=[jax.ShapeDtypeStruct((s, d), F32), jax.ShapeDtypeStruct((s, d), BF16), sums_shape],
            compiler_params=_params("arbitrary"))(x, dh, dres, gain, scale, branch, gate)
    dx, sm = pl.pallas_call(
        body, name=name, grid=(s // tm,), in_specs=[row, row, row, vec, vec], out_specs=[row, sums],
        out_shape=[jax.ShapeDtypeStruct((s, d), F32), sums_shape],
        compiler_params=_params("arbitrary"))(x, dh, dres, gain, scale)
    return dx, None, sm


GLA_ROWS = 256


def _gla_block_setup(lr_ref, wg_ref, bg_ref):
    t, c = GLA_ROWS, GLA_CHUNK
    ri = lax.broadcasted_iota(jnp.int32, (t, t), 0)
    ci = lax.broadcasted_iota(jnp.int32, (t, t), 1)
    same = (ri // c) == (ci // c)
    causal, upper = same & (ci <= ri), same & (ci >= ri)
    z = _nn(lr_ref[...].astype(BF16), wg_ref[...]) + bg_ref[...]
    g = (jnp.minimum(z, 0.0) - jnp.log(1.0 + jnp.exp(-jnp.abs(z)))) * (1.0 / GLA_GATE_TAU)
    hi, mid, lo = _split3(g)
    total = lambda ones: (_nn(ones, lo) + _nn(ones, mid)) + _nn(ones, hi)
    return z, total(causal.astype(BF16)), total(same.astype(BF16)), causal, upper


def _chunks(t):
    return [t[i * GLA_CHUNK:(i + 1) * GLA_CHUNK] for i in range(GLA_ROWS // GLA_CHUNK)]


def _gla_fwd(proj, wg, bg, gn, *, name):
    s = proj.shape[0]
    tb, c = GLA_ROWS, GLA_CHUNK
    cb = tb // c

    def body(q_ref, k_ref, v_ref, r_ref, lr_ref, wg_ref, bg_ref, gn_ref, o_ref, y_ref, st_ref, state):
        i = pl.program_id(0)

        @pl.when(i == 0)
        def _():
            state[...] = jnp.zeros_like(state)

        low = lax.broadcasted_iota(jnp.int32, (tb, HEAD_LANES), 1) < 64
        masks = (low, jnp.logical_not(low))
        _, b, b_end, causal, _ = _gla_block_setup(lr_ref, wg_ref, bg_ref)
        pairs = []
        for p in range(2):
            cols = pl.ds(p * HEAD_LANES, HEAD_LANES)
            bp, bep = (t[:, p * HEAD_LANES:(p + 1) * HEAD_LANES] for t in (b, b_end))
            k = k_ref[:, cols]
            q_in = q_ref[:, cols] * 0.125 * jnp.exp(bp)
            k_out = (k * jnp.exp(-bp)).astype(BF16)
            k_end = k * jnp.exp(bep - bp)
            qms = [jnp.where(m, q_in, 0.0).astype(BF16) for m in masks]
            kes = [jnp.where(m, k_end, 0.0).astype(BF16) for m in masks]
            vs = [v_ref[:, pl.ds((2 * p + e) * HEAD_LANES, HEAD_LANES)].astype(BF16) for e in range(2)]
            grow = [_tn(v0, k0) + _tn(v1, k1) for v0, k0, v1, k1 in zip(_chunks(vs[0]), _chunks(kes[0]), _chunks(vs[1]), _chunks(kes[1]))]
            pairs.append((bep, k_out, qms, vs, grow))
        entering = [[], []]
        for p, (bep, _, _, _, grow) in enumerate(pairs):
            st = state[p]
            for ch in range(cb):
                entering[p].append(st)
                st_ref[ch, p] = st
                st = st * jnp.exp(bep[ch * c:ch * c + 1, :]) + grow[ch]
            state[p] = st
        for p, (_, k_out, qms, vs, _) in enumerate(pairs):
            for e in range(2):
                hc = pl.ds((2 * p + e) * HEAD_LANES, HEAD_LANES)
                a = jnp.where(causal, _nt(qms[e], k_out), 0.0).astype(BF16)
                carried = jnp.concatenate([_nt(qc, sc.astype(BF16)) for qc, sc in zip(_chunks(qms[e]), entering[p])], axis=0)
                o = _nn(a, vs[e]) + carried
                o_ref[:, hc] = o
                rr = r_ref[:, hc]
                on = o * lax.rsqrt(jnp.mean(o * o, axis=-1, keepdims=True) + EPS)
                y_ref[:, hc] = (on * gn_ref[...] * (rr * _sigmoid(rr))).astype(BF16)

    def col(width, at):
        return pl.BlockSpec((tb, width), lambda i: (i, at // width))

    full = lambda shape: pl.BlockSpec(shape, lambda i: tuple(0 for _ in shape))
    return pl.pallas_call(
        body, name=name, grid=(s // tb,),
        in_specs=[col(256, C_GQ), col(256, C_GK), col(512, C_GV), col(512, C_GR), col(128, C_LR),
                  full((HEAD_LANES, 256)), full((1, 256)), full((1, HEAD_LANES))],
        out_specs=[pl.BlockSpec((tb, 512), lambda i: (i, 0)), pl.BlockSpec((tb, 512), lambda i: (i, 0)),
                   pl.BlockSpec((cb, 2, HEAD_LANES, HEAD_LANES), lambda i: (i, 0, 0, 0))],
        out_shape=[jax.ShapeDtypeStruct((s, 512), F32), jax.ShapeDtypeStruct((s, 512), BF16),
                   jax.ShapeDtypeStruct((s // c, 2, HEAD_LANES, HEAD_LANES), F32)],
        scratch_shapes=[pltpu.VMEM((2, HEAD_LANES, HEAD_LANES), F32)],
        compiler_params=_params("arbitrary"))(proj, proj, proj, proj, proj, wg, bg, gn)


def _gla_bwd(proj, wg, bg, gn, o_raw, states, dmixed, *, name):
    s = proj.shape[0]
    tb, c = GLA_ROWS, GLA_CHUNK
    cb = tb // c
    nblk, nch = s // tb, s // c

    def body(q_ref, k_ref, v_ref, r_ref, lr_ref, wg_ref, bg_ref, gn_ref, o_ref, st_ref, stn_ref, dy_ref,
             dq_ref, dk_ref, dv_ref, dr_ref, dlr_ref, gwg_ref, sums_ref, dstate):
        i = pl.program_id(0)

        @pl.when(i == 0)
        def _():
            dstate[...] = jnp.zeros_like(dstate)
            gwg_ref[...] = jnp.zeros_like(gwg_ref)
            sums_ref[...] = jnp.zeros_like(sums_ref)

        low = lax.broadcasted_iota(jnp.int32, (tb, HEAD_LANES), 1) < 64
        masks = (low, jnp.logical_not(low))
        z, b, b_end, causal, upper = _gla_block_setup(lr_ref, wg_ref, bg_ref)
        lr_b = lr_ref[...].astype(BF16)
        dlr = jnp.zeros((tb, HEAD_LANES), F32)
        per_chunk = lambda rows, mats, fn: jnp.concatenate([fn(r, m.astype(BF16)) for r, m in zip(_chunks(rows), mats)], axis=0)
        pairs = []
        for p in range(2):
            cols = pl.ds(p * HEAD_LANES, HEAD_LANES)
            sl = slice(p * HEAD_LANES, (p + 1) * HEAD_LANES)
            bp, bep = b[:, sl], b_end[:, sl]
            e_in, e_out, e_end = jnp.exp(bp), jnp.exp(-bp), jnp.exp(bep - bp)
            q = q_ref[:, cols] * 0.125
            k = k_ref[:, cols]
            q_in, k_out, k_end = q * e_in, k * e_out, k * e_end
            qms = [jnp.where(m, q_in, 0.0).astype(BF16) for m in masks]
            kms_out = [jnp.where(m, k_out, 0.0).astype(BF16) for m in masks]
            kms_end = [jnp.where(m, k_end, 0.0).astype(BF16) for m in masks]
            vs, dos = [], []
            for e in range(2):
                hc = pl.ds((2 * p + e) * HEAD_LANES, HEAD_LANES)
                o, rr, dy = o_ref[:, hc], r_ref[:, hc], dy_ref[:, hc]
                sg = _sigmoid(rr)
                rs = lax.rsqrt(jnp.mean(o * o, axis=-1, keepdims=True) + EPS)
                on = o * rs
                t = dy * (rr * sg)
                sums_ref[1, :, hc] += _fold8(t * on)
                dn = t * gn_ref[...]
                dos.append((rs * (dn - on * jnp.mean(dn * on, axis=-1, keepdims=True))).astype(BF16))
                dr_ref[:, hc] = (dy * on * gn_ref[...] * (sg * (1.0 + rr * (1.0 - sg)))).astype(BF16)
                vs.append(v_ref[:, hc].astype(BF16))
            grow = [_tn(d0, q0) + _tn(d1, q1) for d0, q0, d1, q1 in zip(_chunks(dos[0]), _chunks(qms[0]), _chunks(dos[1]), _chunks(qms[1]))]
            pairs.append((bep, e_in, e_out, e_end, q, k, qms, kms_out, kms_end, vs, dos, grow))
        chains = []
        for p in range(2):
            bep, grow = pairs[p][0], pairs[p][-1]
            entering = [st_ref[ch, p] for ch in range(cb)]
            dst, leaving_grad = dstate[p], [None] * cb
            for ch in reversed(range(cb)):
                leaving_grad[ch] = dst
                dst = dst * jnp.exp(bep[ch * c:ch * c + 1, :]) + grow[ch]
            dstate[p] = dst
            chains.append((entering, leaving_grad))
        for p in range(2):
            cols = pl.ds(p * HEAD_LANES, HEAD_LANES)
            sl = slice(p * HEAD_LANES, (p + 1) * HEAD_LANES)
            _, e_in, e_out, e_end, q, k, qms, kms_out, kms_end, vs, dos, _ = pairs[p]
            entering, leaving_grad = chains[p]
            leaving = entering[1:] + [stn_ref[0, p]]
            felt = jnp.concatenate([jnp.broadcast_to(jnp.sum(dg_st * st, axis=0, keepdims=True), (c, HEAD_LANES))
                                    for dg_st, st in zip(leaving_grad, leaving)], axis=0)
            dq_in = jnp.zeros((tb, HEAD_LANES), F32)
            dk_out = jnp.zeros((tb, HEAD_LANES), F32)
            dk_end = jnp.zeros((tb, HEAD_LANES), F32)
            for e in range(2):
                hc = pl.ds((2 * p + e) * HEAD_LANES, HEAD_LANES)
                a = jnp.where(causal, _nt(qms[e], kms_out[e]), 0.0).astype(BF16)
                da = jnp.where(causal, _nt(dos[e], vs[e]), 0.0).astype(BF16)
                dv_ref[:, hc] = (_tn(a, dos[e]) + per_chunk(kms_end[e], leaving_grad, _nt)).astype(BF16)
                dq_in = dq_in + jnp.where(masks[e], per_chunk(dos[e], entering, _nn) + _nn(da, kms_out[e]), 0.0)
                dk_out = dk_out + _tn(da, qms[e])
                dk_end = dk_end + jnp.where(masks[e], per_chunk(vs[e], leaving_grad, _nn), 0.0)
            dq = dq_in * e_in
            dk = dk_out * e_out + dk_end * e_end
            dq_ref[:, cols] = (dq * 0.125).astype(BF16)
            dk_ref[:, cols] = dk.astype(BF16)
            dg = _sum_left(upper.astype(BF16), q * dq - k * dk) + felt
            dz = dg * (1.0 / GLA_GATE_TAU) * _sigmoid(-z[:, sl])
            dz_b = dz.astype(BF16)
            sums_ref[0, :, cols] += _fold8(dz)
            dlr = dlr + _nt(dz_b, wg_ref[:, cols])
            gwg_ref[:, cols] += _tn(lr_b, dz_b)
        dlr_ref[...] = dlr.astype(BF16)

        @pl.when(i == nblk - 1)
        def _():
            _spread_total(sums_ref)

    rev = lambda i: nblk - 1 - i

    def col(width, at):
        return pl.BlockSpec((tb, width), lambda i: (rev(i), at // width))

    full = lambda shape: pl.BlockSpec(shape, lambda i: tuple(0 for _ in shape))
    out_col = lambda width: pl.BlockSpec((tb, width), lambda i: (rev(i), 0))
    return pl.pallas_call(
        body, name=name, grid=(nblk,),
        in_specs=[col(256, C_GQ), col(256, C_GK), col(512, C_GV), col(512, C_GR), col(128, C_LR),
                  full((HEAD_LANES, 256)), full((1, 256)), full((1, HEAD_LANES)),
                  pl.BlockSpec((tb, 512), lambda i: (rev(i), 0)),
                  pl.BlockSpec((cb, 2, HEAD_LANES, HEAD_LANES), lambda i: (rev(i), 0, 0, 0)),
                  pl.BlockSpec((1, 2, HEAD_LANES, HEAD_LANES), lambda i: (jnp.minimum((rev(i) + 1) * cb, nch - 1), 0, 0, 0)),
                  pl.BlockSpec((tb, 512), lambda i: (rev(i), 0))],
        out_specs=[out_col(256), out_col(256), out_col(512), out_col(512), out_col(128),
                   full((HEAD_LANES, 256)), full((2, 8, 512))],
        out_shape=[jax.ShapeDtypeStruct((s, 256), BF16), jax.ShapeDtypeStruct((s, 256), BF16),
                   jax.ShapeDtypeStruct((s, 512), BF16), jax.ShapeDtypeStruct((s, 512), BF16),
                   jax.ShapeDtypeStruct((s, 128), BF16), jax.ShapeDtypeStruct((HEAD_LANES, 256), F32),
                   jax.ShapeDtypeStruct((2, 8, 512), F32)],
        scratch_shapes=[pltpu.VMEM((2, HEAD_LANES, HEAD_LANES), F32)],
        compiler_params=_params("arbitrary"))(proj, proj, proj, proj, proj, wg, bg, gn, o_raw, states, states, dmixed)


def _head_sums(v):
    ri = lax.broadcasted_iota(jnp.int32, (HEAD_LANES, HEAD_LANES), 0) // 64
    ci = lax.broadcasted_iota(jnp.int32, (HEAD_LANES, HEAD_LANES), 1) // 64
    ones = (ri == ci).astype(BF16)
    return jnp.concatenate([_sum_right(v[:, p * HEAD_LANES:(p + 1) * HEAD_LANES], ones) for p in range(4)], axis=1)


def _attn_prep(proj, qg, kg, *, name):
    s = proj.shape[0]
    tm = ROW_TILE

    def body(q_ref, k_ref, qg_ref, kg_ref, qa_ref, ka_ref):
        q, k = q_ref[...], k_ref[...]
        qr = lax.rsqrt(_head_sums(q * q) * (1.0 / 64) + EPS)
        kr = lax.rsqrt(_head_sums(k * k) * (1.0 / 64) + EPS)
        qa_ref[...] = q * qr * qg_ref[...] * 0.125
        ka_ref[...] = k * kr * kg_ref[...]

    col = lambda at: pl.BlockSpec((tm, 512), lambda i: (i, at // 512))
    vec = pl.BlockSpec((1, 512), lambda i: (0, 0))
    out = pl.BlockSpec((tm, 512), lambda i: (i, 0))
    return pl.pallas_call(
        body, name=name, grid=(s // tm,), in_specs=[col(C_AQ), col(C_AK), vec, vec], out_specs=[out] * 2,
        out_shape=[jax.ShapeDtypeStruct((s, 512), F32)] * 2, compiler_params=_params("parallel"))(proj, proj, qg, kg)


FAR = 1e30
LOG2E, LN2 = 1.4426950408889634, 0.6931471805599453


def _attn_distance(first):
    blk = ATTN_BLOCK
    iq = lax.broadcasted_iota(jnp.int32, (2 * blk, 2 * blk), 0) & (blk - 1)
    ik = lax.broadcasted_iota(jnp.int32, (2 * blk, 2 * blk), 1)
    rel = iq + blk - ik
    valid = (rel >= 0) & (rel <= blk) & (jnp.logical_not(first) | (ik >= blk))
    return jnp.where(valid, rel.astype(F32), FAR)


def _stack_heads(t2):
    low = lax.broadcasted_iota(jnp.int32, t2.shape, 1) < 64
    return jnp.concatenate([jnp.where(low, t2, 0.0), jnp.where(low, 0.0, t2)], axis=0).astype(BF16)


def _unstack_heads(t):
    blk = ATTN_BLOCK
    low = lax.broadcasted_iota(jnp.int32, (blk, HEAD_LANES), 1) < 64
    return jnp.where(low, t[0:blk], t[blk:2 * blk])


def _attn_scores(qs, kcat, slopes, dil, dist):
    top = lax.broadcasted_iota(jnp.int32, (2 * ATTN_BLOCK, 1), 0) < ATTN_BLOCK
    return _nt(qs, kcat) - jnp.where(top, slopes[0] * (dil * LOG2E), slopes[1] * (dil * LOG2E)) * dist


def _pair_slopes(p):
    if isinstance(p, int):
        return ALIBI_SLOPES[2 * p], ALIBI_SLOPES[2 * p + 1]
    pick = lambda e: jnp.where(p == 0, ALIBI_SLOPES[e], jnp.where(p == 1, ALIBI_SLOPES[2 + e],
                               jnp.where(p == 2, ALIBI_SLOPES[4 + e], ALIBI_SLOPES[6 + e])))
    return pick(0), pick(1)


ATTN_GROUP = 4


def _each(fn, *lists):
    return [fn(*args) for args in zip(*lists)]


def _attn_group_fwd(q2s, kcats, vcats, slopes, dil, dist):
    qs = _each(lambda q2: _stack_heads(q2 * LOG2E), q2s)
    sc = _each(lambda q, k, sl: _attn_scores(q, k, sl, dil, dist), qs, kcats, slopes)
    m = _each(lambda s: jnp.max(s, axis=-1, keepdims=True), sc)
    pr = _each(lambda s, mx: jnp.exp2(s - mx), sc, m)
    den = _each(lambda p: jnp.sum(p, axis=-1, keepdims=True), pr)
    o = _each(lambda p, v, d: _nn(p.astype(BF16), v) / d, pr, vcats, den)
    lse = _each(lambda mx, d, t: jnp.broadcast_to(mx + jnp.log2(d), t.shape), m, den, o)
    return _each(lambda t, l: (_unstack_heads(t), _unstack_heads(l)), o, lse)


def _attn_group_bwd(q2s, kcats, vcats, do2s, y2s, lse2s, slopes, dil, dist):
    lane = lax.broadcasted_iota(jnp.int32, (ATTN_BLOCK, HEAD_LANES), 1)
    low = lane < 64
    per_head = lambda t, pick: jnp.concatenate([jnp.sum(jnp.where(pick(0), t, 0.0), axis=-1, keepdims=True),
                                                jnp.sum(jnp.where(pick(1), t, 0.0), axis=-1, keepdims=True)], axis=0)
    lse = _each(lambda l: per_head(l, lambda e: lane == 64 * e), lse2s)
    delta = _each(lambda d, y: per_head(d * y, lambda e: low if e == 0 else jnp.logical_not(low)), do2s, y2s)
    qs = _each(lambda q2: _stack_heads(q2 * LOG2E), q2s)
    dos = _each(_stack_heads, do2s)
    sc = _each(lambda q, k, sl: _attn_scores(q, k, sl, dil, dist), qs, kcats, slopes)
    pr = _each(lambda s, l: jnp.exp2(s - l), sc, lse)
    dp = _each(_nt, dos, vcats)
    ds = _each(lambda p, d, dl: (p * (d - dl)).astype(BF16), pr, dp, delta)
    dq = _each(lambda d, k: _unstack_heads(_nn(d, k)), ds, kcats)
    dk = _each(lambda d, q: _tn(d, q) * LN2, ds, qs)
    dv = _each(lambda p, d: _tn(p.astype(BF16), d), pr, dos)
    return list(zip(dq, dk, dv))


def _attn_specs(dil):
    rows = ATTN_BLOCK * dil
    if dil == 1:
        cur = lambda at: pl.BlockSpec((rows, 512), lambda n: (n, at // 512))
        prev = lambda at: pl.BlockSpec((rows, 512), lambda n: (jnp.maximum(n - 1, 0), at // 512))
    else:
        cur = lambda at: pl.BlockSpec((rows, HEAD_LANES), lambda n, p: (n, at // HEAD_LANES + p))
        prev = lambda at: pl.BlockSpec((rows, HEAD_LANES), lambda n, p: (jnp.maximum(n - 1, 0), at // HEAD_LANES + p))
    return cur, prev


def _attn_loop(dil, one_group):
    if dil == 1:
        one_group([(slice(None), pl.ds(p * HEAD_LANES, HEAD_LANES), p) for p in range(ATTN_GROUP)])
    else:
        p = pl.program_id(1)

        def step(g, carry):
            one_group([(pl.ds(g * ATTN_GROUP + j, ATTN_BLOCK, stride=dil), slice(None), p) for j in range(ATTN_GROUP)])
            return carry

        if dil == ATTN_GROUP:
            step(0, 0)
        else:
            lax.fori_loop(0, dil // ATTN_GROUP, step, 0)


def _dil_attn_fwd(qa, ka, proj, dil, *, name):
    s = qa.shape[0]

    def body(q_ref, kp_ref, kc_ref, vp_ref, vc_ref, o_ref, lse_ref):
        dist = _attn_distance(pl.program_id(0) == 0)

        def one_group(items):
            both = lambda a, b: [jnp.concatenate([a[rows, cols], b[rows, cols]], axis=0).astype(BF16) for rows, cols, _ in items]
            outs = _attn_group_fwd([q_ref[rows, cols] for rows, cols, _ in items], both(kp_ref, kc_ref), both(vp_ref, vc_ref),
                                   [_pair_slopes(p) for _, _, p in items], dil, dist)
            for (rows, cols, _), (o2, lse2) in zip(items, outs):
                o_ref[rows, cols] = o2
                lse_ref[rows, cols] = lse2

        _attn_loop(dil, one_group)

    cur, prev = _attn_specs(dil)
    grid = (s // ATTN_BLOCK,) if dil == 1 else (s // (ATTN_BLOCK * dil), 4)
    return pl.pallas_call(
        body, name=name, grid=grid, in_specs=[cur(0), prev(0), cur(0), prev(C_AV), cur(C_AV)], out_specs=[cur(0), cur(0)],
        out_shape=[jax.ShapeDtypeStruct((s, 512), F32)] * 2,
        compiler_params=_params(*["parallel"] * len(grid)))(qa, ka, ka, proj, proj)


def _attn_merge(branches, y_gla, *, name):
    s = y_gla.shape[0]
    tm = ROW_TILE

    def body(o0, l0, o1, l1, o2, l2, yg_ref, mixed_ref, y_ref, lse_ref):
        m = jnp.maximum(jnp.maximum(l0[...], l1[...]), l2[...])
        w0, w1, w2 = jnp.exp2(l0[...] - m), jnp.exp2(l1[...] - m), jnp.exp2(l2[...] - m)
        zs = w0 + w1 + w2
        y = (w0 * o0[...] + w1 * o1[...] + w2 * o2[...]) / zs
        y_ref[...] = y
        lse_ref[...] = m + jnp.log2(zs)
        mixed_ref[:, 0:512] = yg_ref[...]
        mixed_ref[:, 512:1024] = y.astype(BF16)

    blk = pl.BlockSpec((tm, 512), lambda i: (i, 0))
    args = [t for pair in branches for t in pair]
    return pl.pallas_call(
        body, name=name, grid=(s // tm,), in_specs=[blk] * 7,
        out_specs=[pl.BlockSpec((tm, 1024), lambda i: (i, 0)), blk, blk],
        out_shape=[jax.ShapeDtypeStruct((s, 1024), BF16), jax.ShapeDtypeStruct((s, 512), F32),
                   jax.ShapeDtypeStruct((s, 512), F32)],
        compiler_params=_params("parallel"))(*args, y_gla)


def _dil_attn_bwd(qa, ka, proj, y_att, lse, dmixed, dil, *, name):
    s = qa.shape[0]
    blk = ATTN_BLOCK

    def body(q_ref, kp_ref, kc_ref, vp_ref, vc_ref, y_ref, lse_ref, do_ref, dq_ref, dkc_ref, dkp_ref, dvc_ref, dvp_ref):
        dist = _attn_distance(pl.program_id(0) == 0)

        def one_group(items):
            both = lambda a, b: [jnp.concatenate([a[rows, cols], b[rows, cols]], axis=0).astype(BF16) for rows, cols, _ in items]
            at = lambda ref: [ref[rows, cols] for rows, cols, _ in items]
            outs = _attn_group_bwd(at(q_ref), both(kp_ref, kc_ref), both(vp_ref, vc_ref), at(do_ref), at(y_ref), at(lse_ref),
                                   [_pair_slopes(p) for _, _, p in items], dil, dist)
            for (rows, cols, _), (dq, dk, dv) in zip(items, outs):
                dq_ref[rows, cols] = dq
                dkp_ref[rows, cols] = dk[0:blk]
                dkc_ref[rows, cols] = dk[blk:2 * blk]
                dvp_ref[rows, cols] = dv[0:blk]
                dvc_ref[rows, cols] = dv[blk:2 * blk]

        _attn_loop(dil, one_group)

    cur, prev = _attn_specs(dil)
    grid = (s // blk,) if dil == 1 else (s // (blk * dil), 4)
    return pl.pallas_call(
        body, name=name, grid=grid,
        in_specs=[cur(0), prev(0), cur(0), prev(C_AV), cur(C_AV), cur(0), cur(0), cur(512)], out_specs=[cur(0)] * 5,
        out_shape=[jax.ShapeDtypeStruct((s, 512), F32)] * 5, compiler_params=_params(*["parallel"] * len(grid)),
    )(qa, ka, ka, proj, proj, y_att, lse, dmixed)


def _attn_post(parts, proj, qg, kg, *, name):
    s = proj.shape[0]
    tm = ATTN_BLOCK
    nblk = s // tm

    def body(*refs):
        ins, (q_ref, k_ref, qg_ref, kg_ref, dq_out, dk_out, dv_out, sums_ref) = refs[:15], refs[15:]
        i = pl.program_id(0)

        @pl.when(i == 0)
        def _():
            sums_ref[...] = jnp.zeros_like(sums_ref)

        dq = jnp.zeros((tm, 512), F32)
        dk = jnp.zeros((tm, 512), F32)
        dv = jnp.zeros((tm, 512), F32)
        for g, dil in enumerate(DILATIONS):
            dq_r, dkc_r, dkp_r, dvc_r, dvp_r = ins[5 * g:5 * g + 5]
            inside = (i + dil < nblk).astype(F32)
            dq = dq + dq_r[...]
            dk = dk + dkc_r[...] + inside * dkp_r[...]
            dv = dv + dvc_r[...] + inside * dvp_r[...]
        dv_out[...] = dv.astype(BF16)
        for row, (x_ref, g_ref, dy, out, post) in enumerate(((q_ref, qg_ref, dq, dq_out, 0.125), (k_ref, kg_ref, dk, dk_out, 1.0))):
            x = x_ref[...]
            rs = lax.rsqrt(_head_sums(x * x) * (1.0 / 64) + EPS)
            xn = x * rs
            dy = dy * post
            sums_ref[row] += _fold8(dy * xn)
            dn = dy * g_ref[...]
            out[...] = (rs * (dn - xn * (_head_sums(dn * xn) * (1.0 / 64)))).astype(BF16)

        @pl.when(i == nblk - 1)
        def _():
            _spread_total(sums_ref)

    here = pl.BlockSpec((tm, 512), lambda i: (i, 0))
    specs = []
    for dil in DILATIONS:
        later = pl.BlockSpec((tm, 512), lambda i, dil=dil: (jnp.minimum(i + dil, nblk - 1), 0))
        specs += [here, here, later, here, later]
    col = lambda at: pl.BlockSpec((tm, 512), lambda i: (i, at // 512))
    vec = pl.BlockSpec((1, 512), lambda i: (0, 0))
    return pl.pallas_call(
        body, name=name, grid=(nblk,), in_specs=specs + [col(C_AQ), col(C_AK), vec, vec],
        out_specs=[here, here, here, pl.BlockSpec((2, 8, 512), lambda i: (0, 0, 0))],
        out_shape=[jax.ShapeDtypeStruct((s, 512), BF16)] * 3 + [jax.ShapeDtypeStruct((2, 8, 512), F32)],
        compiler_params=_params("arbitrary"))(*[t for part in parts for t in part], proj, proj, qg, kg)


FFN_TM, FFN_TN = 256, 1408
HALO = 16


def _conv3(u_ref, halo_ref, w_ref, b_ref, first):
    u = u_ref[...].astype(F32)
    ext = jnp.concatenate([jnp.where(first, 0.0, halo_ref[...].astype(F32)), u], axis=0)
    u1 = pltpu.roll(ext, 1, 0)[HALO:]
    u2 = pltpu.roll(ext, 2, 0)[HALO:]
    return b_ref[...] + w_ref[0:1, :] * u2 + w_ref[1:2, :] * u1 + w_ref[2:3, :] * u, u, u1, u2


def _ffn_specs(tm, tn):
    nj = D_FF // tn
    blk = lambda half: pl.BlockSpec((tm, tn), lambda j, i: (i, j + half * nj))
    halo = lambda half: pl.BlockSpec((HALO, tn), lambda j, i: (jnp.maximum(i * (tm // HALO) - 1, 0), j + half * nj))
    wspec = lambda half: pl.BlockSpec((3, tn), lambda j, i: (0, j + half * nj))
    bspec = lambda half: pl.BlockSpec((1, tn), lambda j, i: (0, j + half * nj))
    return [blk(0), halo(0), blk(1), halo(1), wspec(0), wspec(1), bspec(0), bspec(1)]


def _conv_swiglu_fwd(u, conv_w, conv_b, *, name):
    s = u.shape[0]
    tm, tn = FFN_TM, FFN_TN

    def body(ug_ref, hg_ref, uv_ref, hv_ref, wg_ref, wv_ref, bg_ref, bv_ref, act_ref, uc_ref):
        first = pl.program_id(1) == 0
        cg = _conv3(ug_ref, hg_ref, wg_ref, bg_ref, first)[0]
        cv = _conv3(uv_ref, hv_ref, wv_ref, bv_ref, first)[0]
        act_ref[...] = (cg * _sigmoid(cg) * cv).astype(BF16)
        uc_ref[0] = cg.astype(BF16)
        uc_ref[1] = cv.astype(BF16)

    return pl.pallas_call(
        body, name=name, grid=(D_FF // tn, s // tm), in_specs=_ffn_specs(tm, tn),
        out_specs=[pl.BlockSpec((tm, tn), lambda j, i: (i, j)), pl.BlockSpec((2, tm, tn), lambda j, i: (0, i, j))],
        out_shape=[jax.ShapeDtypeStruct((s, D_FF), BF16), jax.ShapeDtypeStruct((2, s, D_FF), BF16)],
        compiler_params=_params("parallel", "parallel"))(u, u, u, u, conv_w, conv_w, conv_b, conv_b)


def _swiglu_bwd(uc, dact, *, name):
    _, s, _ = uc.shape
    tm, tn = FFN_TM, FFN_TN

    def body(uc_ref, da_ref, duc_ref, sums_ref):
        i = pl.program_id(1)

        @pl.when(i == 0)
        def _():
            sums_ref[...] = jnp.zeros_like(sums_ref)

        cg, cv, da = uc_ref[0].astype(F32), uc_ref[1].astype(F32), da_ref[...].astype(F32)
        sg = _sigmoid(cg)
        dg = da * cv * (sg * (1.0 + cg * (1.0 - sg)))
        dv = da * (cg * sg)
        duc_ref[0] = dg.astype(BF16)
        duc_ref[1] = dv.astype(BF16)
        sums_ref[0] += _fold8(dg)
        sums_ref[1] += _fold8(dv)

        @pl.when(i == s // tm - 1)
        def _():
            _spread_total(sums_ref)

    pair = pl.BlockSpec((2, tm, tn), lambda j, i: (0, i, j))
    return pl.pallas_call(
        body, name=name, grid=(D_FF // tn, s // tm), in_specs=[pair, pl.BlockSpec((tm, tn), lambda j, i: (i, j))],
        out_specs=[pair, pl.BlockSpec((2, 8, tn), lambda j, i: (0, 0, j))],
        out_shape=[jax.ShapeDtypeStruct((2, s, D_FF), BF16), jax.ShapeDtypeStruct((2, 8, D_FF), F32)],
        compiler_params=_params("parallel", "arbitrary"))(uc, dact)


def _conv_bwd(duc, u, conv_w, *, name):
    _, s, _ = duc.shape
    tm, tn = FFN_TM, FFN_TN
    nj, ni = D_FF // tn, s // tm

    def body(d_ref, halo_ref, u_ref, w_ref, du_ref, sums_ref):
        i = pl.program_id(2)

        @pl.when(i == 0)
        def _():
            sums_ref[...] = jnp.zeros_like(sums_ref)

        d = d_ref[0].astype(F32)
        ext = jnp.concatenate([d, jnp.where(i == ni - 1, 0.0, halo_ref[0].astype(F32))], axis=0)
        n = tm + HALO
        d1 = pltpu.roll(ext, n - 1, 0)[:tm]
        d2 = pltpu.roll(ext, n - 2, 0)[:tm]
        du_ref[...] = (w_ref[2:3, :] * d + w_ref[1:2, :] * d1 + w_ref[0:1, :] * d2).astype(BF16)
        uv = u_ref[...].astype(F32)
        for t, shifted in enumerate((d2, d1, d)):
            sums_ref[0, t] += _fold8(shifted * uv)

        @pl.when(i == ni - 1)
        def _():
            _spread_total(sums_ref)

    return pl.pallas_call(
        body, name=name, grid=(2, nj, ni),
        in_specs=[pl.BlockSpec((1, tm, tn), lambda g, j, i: (g, i, j)),
                  pl.BlockSpec((1, HALO, tn), lambda g, j, i: (g, jnp.minimum((i + 1) * (tm // HALO), s // HALO - 1), j)),
                  pl.BlockSpec((tm, tn), lambda g, j, i: (i, g * nj + j)),
                  pl.BlockSpec((3, tn), lambda g, j, i: (0, g * nj + j))],
        out_specs=[pl.BlockSpec((tm, tn), lambda g, j, i: (i, g * nj + j)),
                   pl.BlockSpec((1, 3, 8, tn), lambda g, j, i: (g, 0, 0, j))],
        out_shape=[jax.ShapeDtypeStruct((s, 2 * D_FF), BF16), jax.ShapeDtypeStruct((2, 3, 8, D_FF), F32)],
        compiler_params=_params("parallel", "parallel", "arbitrary"))(duc, duc, u, conv_w)


def _loss_head(x1, ffn, gate, target, *, name):
    s, d = x1.shape
    tm = ROW_TILE

    def body(x_ref, f_ref, g_ref, t_ref, dy_ref, df_ref, sums_ref):
        i = pl.program_id(0)

        @pl.when(i == 0)
        def _():
            sums_ref[...] = jnp.zeros_like(sums_ref)

        f = f_ref[...]
        err = x_ref[...] + g_ref[...] * f - t_ref[...]
        dy = err * (1.0 / d)
        dy_ref[...] = dy
        df_ref[...] = (g_ref[...] * dy).astype(BF16)
        sums_ref[0] += _fold8(dy * f)
        sums_ref[1] += _fold8(err * err)

        @pl.when(i == s // tm - 1)
        def _():
            _spread_total(sums_ref)

    row = pl.BlockSpec((tm, d), lambda i: (i, 0))
    return pl.pallas_call(
        body, name=name, grid=(s // tm,), in_specs=[row, row, pl.BlockSpec((1, d), lambda i: (0, 0)), row],
        out_specs=[row, row, pl.BlockSpec((2, 8, d), lambda i: (0, 0, 0))],
        out_shape=[jax.ShapeDtypeStruct((s, d), F32), jax.ShapeDtypeStruct((s, d), BF16), jax.ShapeDtypeStruct((2, 8, d), F32)],
        compiler_params=_params("arbitrary"))(x1, ffn, gate, target)


def _adamw(w, g, m, v, *, name):
    rows, cols = w.shape
    tm = next((t for t in range(ROW_TILE, 7, -8) if rows % t == 0), rows)

    def body(w_ref, g_ref, m_ref, v_ref, d_ref, mo_ref, vo_ref):
        gv = g_ref[...]
        mn = ADAM_B1 * m_ref[...] + (1.0 - ADAM_B1) * gv
        vn = ADAM_B2 * v_ref[...] + (1.0 - ADAM_B2) * (gv * gv)
        m_hat = mn / (1.0 - ADAM_B1 ** ADAM_STEP)
        v_hat = vn / (1.0 - ADAM_B2 ** ADAM_STEP)
        d_ref[...] = -ADAM_LR * (m_hat / (jnp.sqrt(v_hat) + ADAM_EPS) + ADAM_WD * w_ref[...])
        mo_ref[...] = mn
        vo_ref[...] = vn

    blk = pl.BlockSpec((tm, cols), lambda i: (i, 0))
    return pl.pallas_call(
        body, name=name, grid=(rows // tm,), in_specs=[blk] * 4, out_specs=[blk] * 3,
        out_shape=[jax.ShapeDtypeStruct((rows, cols), F32)] * 3, compiler_params=_params("parallel"))(w, g, m, v)


def _colsum(t):
    return t[..., 0, :]


def _in_proj_layout(w_in):
    pad = jnp.zeros((w_in.shape[0], PROJ_W - C_LR - GLA_GATE_RANK), w_in.dtype)
    return jnp.concatenate([w_in[:, :1536], w_in[:, 1552:], w_in[:, 1536:1552], pad], axis=1)


def _in_proj_grad_layout(g):
    return jnp.concatenate([g[:, :1536], g[:, C_LR:C_LR + GLA_GATE_RANK], g[:, 1536:C_LR]], axis=1)


def _gate_layout(gla_w_gate):
    return jnp.pad(gla_w_gate, ((0, HEAD_LANES - GLA_GATE_RANK), (0, 0))).astype(BF16)


def _local_step(x, target, mod, wi, wo, ffn_weights, ffn_grads_ready, attn_grads_ready, conv_w, conv_b, wg, bg, gn, qg, kg, n1g, n2g):
    d = D_MODEL
    sh1, sc1, g1, sh2, sc2, g2 = [mod[:, i * d:(i + 1) * d] for i in range(6)]
    qg8, kg8 = jnp.tile(qg, (1, 8)), jnp.tile(kg, (1, 8))

    _, h1, h1_t = _norm_mod_fwd(x, None, None, n1g, sc1, sh1, name="norm1_fwd")
    proj = _mm(h1, wi, tm=1024, tn=PROJ_W, tk=d, name="in_proj")
    o_raw, y_gla, states = _gla_fwd(proj, wg, bg, gn, name="gla_fwd")
    qa, ka = _attn_prep(proj, qg8, kg8, name="attn_prep")
    branches = [_dil_attn_fwd(qa, ka, proj, dil, name=f"attn_fwd_d{dil}") for dil in DILATIONS]
    mixed, y_att, lse = _attn_merge(branches, y_gla, name="attn_merge")
    attn_out = _mm(mixed, wo, tm=1024, tn=d, tk=d, name="out_proj")
    x1, h2, h2_t = _norm_mod_fwd(x, attn_out, g1, n2g, sc2, sh2, name="norm2_fwd")
    wup, wdown = ffn_weights(h2)
    u = _mm(h2, wup, out_dtype=BF16, tm=1024, tn=D_FF, tk=d, name="up_proj")
    act, uc = _conv_swiglu_fwd(u, conv_w, conv_b, name="conv_swiglu_fwd")
    ffn = _mm(act, wdown, tm=1024, tn=d, tk=D_FF, name="down_proj")
    dy, dffn, head_sums = _loss_head(x1, ffn, g2, target, name="loss_head")

    dact = _mm(dffn, wdown, tb=True, out_dtype=BF16, tm=1024, tn=D_FF, tk=d, name="down_proj_dx")
    g_wdown, g_wdown_b = _mm(act, dffn, ta=True, tm=1408, tn=d, tk=2048, also_bf16=True, name="down_proj_dw")
    duc, bias_sums = _swiglu_bwd(uc, dact, name="swiglu_bwd")
    du, tap_sums = _conv_bwd(duc, u, conv_w, name="conv_bwd")
    dh2 = _mm(du, wup, tb=True, tm=1024, tn=d, tk=D_FF, name="up_proj_dx")
    g_wup, g_wup_b = _mm(h2_t, du, tm=d, tn=1408, tk=2048, shard_cols=True, also_bf16=True, name="up_proj_dw")
    token = ffn_grads_ready(g_wup_b, g_wdown_b)
    g1_late = g1 if token is None else g1 + token[0:1, 0:1]
    dx1, dao, n2_sums = _norm_mod_bwd(x1, dh2, dy, n2g, sc2, attn_out, g1_late, name="norm2_bwd")

    dmixed = _mm(dao, wo, tb=True, tm=1024, tn=d, tk=d, name="out_proj_dx")
    g_wo = _mm(mixed, dao, ta=True, tm=d, tn=d, tk=1024, name="out_proj_dw")
    dgq, dgk, dgv, dgr, dlr, g_wg, gla_sums = _gla_bwd(proj, wg, bg, gn, o_raw, states, dmixed, name="gla_bwd")
    parts = [_dil_attn_bwd(qa, ka, proj, y_att, lse, dmixed, dil, name=f"attn_bwd_d{dil}") for dil in DILATIONS]
    daq, dak, dav, qk_sums = _attn_post(parts, proj, qg8, kg8, name="attn_post")
    dproj = jnp.concatenate([dgq, dgk, dgv, dgr, daq, dak, dav, dlr], axis=1)
    g_wi = _mm(h1_t, dproj, tm=512, tn=PROJ_W, tk=2048, name="in_proj_dw")
    token = attn_grads_ready(g_wi, g_wo)
    sc1_late = sc1 if token is None else sc1 + token[0:1, 0:1]
    dh1 = _mm(dproj, wi, tb=True, tm=1024, tn=d, tk=PROJ_W, name="in_proj_dx")
    grad_x, _, n1_sums = _norm_mod_bwd(x, dh1, dx1, n1g, sc1_late, None, None, name="norm1_bwd")

    n1, n2, hs, taps, cb = _colsum(n1_sums), _colsum(n2_sums), _colsum(head_sums), _colsum(tap_sums), _colsum(bias_sums)
    gs, qs = _colsum(gla_sums), _colsum(qk_sums)
    dmod = jnp.concatenate([n1[1], n1[0] * n1g[0], n2[2], n2[1], n2[0] * n2g[0], hs[0]])
    small = dict(
        dmod=dmod,
        norm1_g=n1[0] * (1.0 + sc1[0]), norm2_g=n2[0] * (1.0 + sc2[0]),
        gla_w_gate=g_wg[:GLA_GATE_RANK], gla_b_gate=gs[0, :256], gla_norm_g=gs[1].reshape(4, 128).sum(axis=0),
        q_norm_g=qs[0].reshape(8, 64).sum(axis=0), k_norm_g=qs[1].reshape(8, 64).sum(axis=0),
        conv_w=jnp.concatenate([taps[0], taps[1]], axis=1), conv_b=jnp.concatenate([cb[0], cb[1]]),
    )
    return head_sums[1], grad_x, (g_wi, g_wo, g_wup, g_wdown), small


N_DEV, N_CHIP = 8, 4
ANY = pl.BlockSpec(memory_space=pl.ANY)
VMEM_SPEC = pl.BlockSpec(memory_space=pltpu.VMEM)


def _place():
    x, y, c = lax.axis_index("x"), lax.axis_index("y"), lax.axis_index("c")
    other_chips = [(1 - x, y), (x, 1 - y), (1 - x, 1 - y)]
    return x, y, c, (x, y, 1 - c), other_chips


def _all_gather_small(v, *, name):
    m, n = v.shape

    def body(v_ref, out_ref, send_sems, recv_sems, local_sem):
        x, y, c, sibling, chips = _place()
        me = (x, y, c)

        def rows(px, py, pc):
            return out_ref.at[pl.ds((4 * px + 2 * py + pc) * m, m), :]

        def copy(k, block, to, src=None):
            return pltpu.make_async_remote_copy(
                src_ref=rows(*block) if src is None else src, dst_ref=rows(*block), send_sem=send_sems.at[k],
                recv_sem=recv_sems.at[k], device_id=to, device_id_type=MESH)

        mine = pltpu.make_async_copy(v_ref, rows(*me), local_sem)
        mine.start()
        first = [copy(0, me, sibling, src=v_ref)]
        first += [copy(1 + j, me, (*chip, c), src=v_ref) for j, chip in enumerate(chips)]
        for cp in first:
            cp.start()
        passed = [copy(4 + j, (*chip, c), sibling) for j, chip in enumerate(chips)]
        for j, chip in enumerate(chips):
            copy(1 + j, (*chip, c), me).wait_recv()
            passed[j].start()
        copy(0, sibling, me).wait_recv()
        for j, chip in enumerate(chips):
            copy(4 + j, (*chip, 1 - c), me).wait_recv()
        for cp in first + passed:
            cp.wait_send()
        mine.wait()

    return pl.pallas_call(
        body, name=name, out_shape=jax.ShapeDtypeStruct((N_DEV * m, n), v.dtype), in_specs=[VMEM_SPEC], out_specs=VMEM_SPEC,
        scratch_shapes=[pltpu.SemaphoreType.DMA((7,)), pltpu.SemaphoreType.DMA((7,)), pltpu.SemaphoreType.DMA],
    )(v)


def _gather_weight_shards(shards, *, name):
    nw = len(shards)

    def body(*refs):
        srcs, outs, (send_sems, recv_sems) = refs[:nw], refs[nw:2 * nw], refs[2 * nw:]
        x, y, c, sibling, chips = _place()
        index = lambda chip: 2 * chip[0] + chip[1]

        def copy(w, k, src, dst, to):
            return pltpu.make_async_remote_copy(src_ref=src, dst_ref=dst, send_sem=send_sems.at[6 * w + k],
                                                recv_sem=recv_sems.at[6 * w + k], device_id=to, device_id_type=MESH)

        sent = []
        for w, (src_ref, out_ref) in enumerate(zip(srcs, outs)):
            for k, chip in enumerate(chips):
                sent.append(copy(w, k, src_ref.at[c], out_ref.at[2 * x + y, c], (*chip, c)))
                sent[-1].start()
        for w, out_ref in enumerate(outs):
            for k, chip in enumerate(chips):
                landed = out_ref.at[index(chip), c]
                copy(w, k, landed, landed, (*chip, c)).wait_recv()
                sent.append(copy(w, 3 + k, landed, landed, sibling))
                sent[-1].start()
        for w, out_ref in enumerate(outs):
            for k, chip in enumerate(chips):
                passed_on = out_ref.at[index(chip), 1 - c]
                copy(w, 3 + k, passed_on, passed_on, sibling).wait_recv()
        for cp in sent:
            cp.wait_send()

    return pl.pallas_call(
        body, name=name, out_shape=[jax.ShapeDtypeStruct((N_CHIP, *s.shape), s.dtype) for s in shards],
        in_specs=[ANY] * nw, out_specs=[ANY] * nw,
        scratch_shapes=[pltpu.SemaphoreType.DMA((6 * nw,)), pltpu.SemaphoreType.DMA((6 * nw,))],
    )(*shards)


HBM_SPEC = pl.BlockSpec(memory_space=pltpu.HBM)
SEM_SPEC = pl.BlockSpec(memory_space=pltpu.SEMAPHORE)
DATAFLOW_EFFECT = pltpu.SideEffectType.DATAFLOW_SIDE_EFFECTING


def _late_copies(srcs, lands, send_sems, recv_sems):
    x, y, c, _, chips = _place()
    return [pltpu.make_async_remote_copy(
        src_ref=src.at[c], dst_ref=land.at[2 * x + y, c], send_sem=send_sems.at[6 * w + 2 * r + core],
        recv_sem=recv_sems.at[6 * w + 2 * r + c], device_id=(*chip, core), device_id_type=MESH)
        for w, (src, land) in enumerate(zip(srcs, lands)) for r, chip in enumerate(chips) for core in range(2)]


def _gather_late_start(own, after, *, name):
    nw = len(own)

    def body(*refs):
        srcs, lands, send_sems, recv_sems, token = refs[:nw], refs[nw:2 * nw], refs[2 * nw + 1], refs[2 * nw + 2], refs[-1]
        for cp in _late_copies(srcs, lands, send_sems, recv_sems):
            cp.start()
        token[...] = jnp.zeros_like(token)

    lands = [pltpu.with_memory_space_constraint(lax.empty((N_CHIP, *s.shape), s.dtype), pltpu.HBM) for s in own]
    own = [pltpu.with_memory_space_constraint(s, pltpu.HBM) for s in own]
    out = pl.pallas_call(
        body, name=name,
        out_shape=(pltpu.SemaphoreType.DMA((6 * nw,)), pltpu.SemaphoreType.DMA((6 * nw,)),
                   *[pltpu.HBM(s.shape, s.dtype) for s in own], *[pltpu.HBM(s.shape, s.dtype) for s in lands],
                   jax.ShapeDtypeStruct((8, 128), F32)),
        in_specs=[HBM_SPEC] * (2 * nw) + [ANY], out_specs=(SEM_SPEC, SEM_SPEC, *[HBM_SPEC] * (2 * nw), VMEM_SPEC),
        input_output_aliases={i: 2 + i for i in range(2 * nw)},
        compiler_params=pltpu.CompilerParams(has_side_effects=DATAFLOW_EFFECT))(*own, *lands, after)
    return out[0], out[1], out[2:2 + nw], out[2 + nw:2 + 2 * nw], out[-1]


def _gather_late_wait(send_sems, recv_sems, own, lands, after, *, name):
    nw = len(own)

    def body(*refs):
        srcs, lands_in, send_sems, recv_sems = refs[:nw], refs[nw:2 * nw], refs[2 * nw], refs[2 * nw + 1]
        x, y, c, _, chips = _place()
        for cp in _late_copies(srcs, lands_in, send_sems, recv_sems):
            cp.wait_send()
        for w, (src, land) in enumerate(zip(srcs, lands_in)):
            for r, chip in enumerate(chips):
                for core in range(2):
                    pltpu.make_async_remote_copy(
                        src_ref=src.at[c], dst_ref=land.at[2 * chip[0] + chip[1], core], send_sem=send_sems.at[6 * w + 2 * r + core],
                        recv_sem=recv_sems.at[6 * w + 2 * r + core], device_id=(*chip, core), device_id_type=MESH).wait_recv()

    out = pl.pallas_call(
        body, name=name, out_shape=(*[pltpu.HBM(s.shape, s.dtype) for s in own], *[pltpu.HBM(s.shape, s.dtype) for s in lands]),
        in_specs=[HBM_SPEC] * (2 * nw) + [SEM_SPEC, SEM_SPEC, ANY], out_specs=tuple([HBM_SPEC] * (2 * nw)),
        input_output_aliases={i: i for i in range(2 * nw)},
        compiler_params=pltpu.CompilerParams(has_side_effects=DATAFLOW_EFFECT))(*own, *lands, send_sems, recv_sems, after)
    return out[:nw], out[nw:]


def _direct_reduce_copies(srcs, lands, send_sems, recv_sems):
    x, y, c, _, _ = _place()
    cps = []
    for w, (src, land) in enumerate(zip(srcs, lands)):
        for rel in range(1, N_DEV):
            tx, ty, tc = (1 - x if rel & 4 else x), (1 - y if rel & 2 else y), (1 - c if rel & 1 else c)
            cps.append(pltpu.make_async_remote_copy(
                src_ref=src.at[2 * tx + ty, tc], dst_ref=land.at[rel - 1], send_sem=send_sems.at[7 * w + rel - 1],
                recv_sem=recv_sems.at[7 * w + rel - 1], device_id=(tx, ty, tc), device_id_type=MESH))
    return cps


def _direct_reduce_start(grads, *, name):
    nw = len(grads)

    def body(*refs):
        srcs, lands, send_sems, recv_sems, token = refs[:nw], refs[nw:2 * nw], refs[2 * nw], refs[2 * nw + 1], refs[-1]
        for cp in _direct_reduce_copies(srcs, lands, send_sems, recv_sems):
            cp.start()
        token[...] = jnp.zeros_like(token)

    lands = [pltpu.with_memory_space_constraint(lax.empty((N_DEV - 1, *g.shape[2:]), g.dtype), pltpu.HBM) for g in grads]
    grads = [pltpu.with_memory_space_constraint(g, pltpu.HBM) for g in grads]
    out = pl.pallas_call(
        body, name=name,
        out_shape=(pltpu.SemaphoreType.DMA((7 * nw,)), pltpu.SemaphoreType.DMA((7 * nw,)),
                   *[pltpu.HBM(g.shape, g.dtype) for g in grads], *[pltpu.HBM(t.shape, t.dtype) for t in lands],
                   jax.ShapeDtypeStruct((8, 128), F32)),
        in_specs=[HBM_SPEC] * (2 * nw), out_specs=(SEM_SPEC, SEM_SPEC, *[HBM_SPEC] * (2 * nw), VMEM_SPEC),
        input_output_aliases={i: 2 + i for i in range(2 * nw)},
        compiler_params=pltpu.CompilerParams(has_side_effects=DATAFLOW_EFFECT))(*grads, *lands)
    return out[0], out[1], out[2:2 + nw], out[2 + nw:2 + 2 * nw], out[-1]


def _direct_reduce_wait(send_sems, recv_sems, grads, lands, after, *, name):
    nw = len(grads)

    def body(*refs):
        srcs, lands_in, send_sems, recv_sems = refs[:nw], refs[nw:2 * nw], refs[2 * nw], refs[2 * nw + 1]
        cps = _direct_reduce_copies(srcs, lands_in, send_sems, recv_sems)
        for cp in cps:
            cp.wait_send()
        for cp in cps:
            cp.wait_recv()

    out = pl.pallas_call(
        body, name=name, out_shape=(*[pltpu.HBM(g.shape, g.dtype) for g in grads], *[pltpu.HBM(t.shape, t.dtype) for t in lands]),
        in_specs=[HBM_SPEC] * (2 * nw) + [SEM_SPEC, SEM_SPEC, ANY], out_specs=tuple([HBM_SPEC] * (2 * nw)),
        input_output_aliases={i: i for i in range(2 * nw)},
        compiler_params=pltpu.CompilerParams(has_side_effects=DATAFLOW_EFFECT))(*grads, *lands, send_sems, recv_sems, after)
    return out[nw:]


def _direct_reduce_add(grad, landed, chip, core, *, name):
    _, r, n = grad.shape
    half = r // 2
    tr = _row_tile(half)
    nb = half // tr

    def body(chip_ref, core_ref, g_ref, t_ref, o_ref):
        acc = g_ref[0]
        for k in range(N_DEV - 1):
            acc = acc + t_ref[k].astype(F32)
        o_ref[...] = acc

    return pl.pallas_call(
        body, name=name,
        grid_spec=pltpu.PrefetchScalarGridSpec(
            num_scalar_prefetch=2, grid=(nb,),
            in_specs=[pl.BlockSpec((1, tr, n), lambda i, chip_ref, core_ref: (chip_ref[0], core_ref[0] * nb + i, 0)),
                      pl.BlockSpec((N_DEV - 1, tr, n), lambda i, chip_ref, core_ref: (0, i, 0))],
            out_specs=pl.BlockSpec((tr, n), lambda i, chip_ref, core_ref: (i, 0))),
        out_shape=jax.ShapeDtypeStruct((half, n), F32), compiler_params=_params("parallel"))(chip, core, grad, landed)


def _share_halves(halves, *, name):
    nw = len(halves)

    def body(*refs):
        srcs, outs, (send_sems, recv_sems) = refs[:nw], refs[nw:2 * nw], refs[2 * nw:]
        _, _, _, sibling, _ = _place()
        cps = [pltpu.make_async_remote_copy(src_ref=src_ref, dst_ref=out_ref, send_sem=send_sems.at[w], recv_sem=recv_sems.at[w],
                                            device_id=sibling, device_id_type=MESH)
               for w, (src_ref, out_ref) in enumerate(zip(srcs, outs))]
        for cp in cps:
            cp.start()
        for cp in cps:
            cp.wait()

    return pl.pallas_call(
        body, name=name, out_shape=[jax.ShapeDtypeStruct(h.shape, h.dtype) for h in halves],
        in_specs=[ANY] * nw, out_specs=[ANY] * nw,
        scratch_shapes=[pltpu.SemaphoreType.DMA((nw,)), pltpu.SemaphoreType.DMA((nw,))])(*halves)


def _row_tile(rows, limit=256):
    return next(t for t in range(limit, 15, -16) if rows % t == 0)


def _sum_devices(gathered, *, name):
    _, m, n = gathered.shape

    def body(g_ref, tot_ref, loss_ref):
        tot = g_ref[0]
        for dev in range(1, N_DEV):
            tot = tot + g_ref[dev]
        tot_ref[...] = tot
        loss_ref[...] = jnp.full((8, n), (0.5 / D_MODEL) * jnp.sum(tot[0:8]), F32)

    return pl.pallas_call(body, name=name, in_specs=[VMEM_SPEC], out_specs=[VMEM_SPEC, VMEM_SPEC],
                          out_shape=[jax.ShapeDtypeStruct((m, n), F32), jax.ShapeDtypeStruct((8, n), F32)])(gathered)


def _ada_mod(cond_all, w_ada_shard, *, name):
    tn = 512

    def body(a_ref, b_ref, o_ref):
        o_ref[...] = _nn(a_ref[...], b_ref[...], precision=HIGHEST)

    return pl.pallas_call(
        body, name=name, grid=(w_ada_shard.shape[1] // tn,),
        in_specs=[pl.BlockSpec(cond_all.shape, lambda j: (0, 0)), pl.BlockSpec((D_MODEL, tn), lambda j: (0, j))],
        out_specs=pl.BlockSpec((N_DEV, tn), lambda j: (0, j)),
        out_shape=jax.ShapeDtypeStruct((N_DEV, w_ada_shard.shape[1]), F32), compiler_params=_params("parallel"))(cond_all, w_ada_shard)


def _ada_grad(cond_all, dmod_cols, *, name):
    tm = 256

    def body(a_ref, b_ref, o_ref):
        o_ref[...] = lax.dot_general(a_ref[...], b_ref[...], (((0,), (0,)), ((), ())), precision=HIGHEST,
                                     preferred_element_type=F32)

    return pl.pallas_call(
        body, name=name, grid=(D_MODEL // tm,),
        in_specs=[pl.BlockSpec((N_DEV, tm), lambda i: (0, i)), pl.BlockSpec(dmod_cols.shape, lambda i: (0, 0))],
        out_specs=pl.BlockSpec((tm, dmod_cols.shape[1]), lambda i: (i, 0)),
        out_shape=jax.ShapeDtypeStruct((D_MODEL, dmod_cols.shape[1]), F32), compiler_params=_params("parallel"))(cond_all, dmod_cols)


def _silu_rows(c8, *, name):
    def body(c_ref, o_ref):
        cv = c_ref[...]
        o_ref[...] = cv * _sigmoid(cv)

    return pl.pallas_call(body, name=name, in_specs=[VMEM_SPEC], out_specs=VMEM_SPEC,
                          out_shape=jax.ShapeDtypeStruct(c8.shape, F32))(c8)


def _rows128(t, rows=None):
    flat = t.reshape(-1, 128)
    return flat if rows is None else jnp.pad(flat, ((0, rows - flat.shape[0]), (0, 0)))


def _from_col_shards(shards, r, n):
    return shards.reshape(N_CHIP, r, n).transpose(1, 0, 2).reshape(r, N_CHIP * n)


def kernel(x, c, w_ada, b_ada, norm1_g, w_in, gla_w_gate, gla_b_gate, gla_norm_g, q_norm_g, k_norm_g, w_out, norm2_g, w_up, conv_w, conv_b, w_down, loss_target, m_w_ada, m_b_ada, m_norm1_g, m_w_in, m_gla_w_gate, m_gla_b_gate, m_gla_norm_g, m_q_norm_g, m_k_norm_g, m_w_out, m_norm2_g, m_w_up, m_conv_w, m_conv_b, m_w_down, v_w_ada, v_b_ada, v_norm1_g, v_w_in, v_gla_w_gate, v_gla_b_gate, v_gla_norm_g, v_q_norm_g, v_k_norm_g, v_w_out, v_norm2_g, v_w_up, v_conv_w, v_conv_b, v_w_down):
    d = D_MODEL
    ax, ay, ac = lax.axis_index("x"), lax.axis_index("y"), lax.axis_index("c")
    chip, dev = 2 * ax + ay, 4 * ax + 2 * ay + ac

    cond = _silu_rows(jnp.broadcast_to(c, (8, d)), name="cond_silu")[0:1]
    small_in = jnp.concatenate([_rows128(cond), _rows128(conv_w[0]), _rows128(gla_w_gate[0])], axis=0)
    small_in = _rows128(small_in, 56)
    got = _all_gather_small(small_in, name="gather_small").reshape(N_DEV, 56, 128)
    cond_all = got[:, 0:8].reshape(N_DEV, d)
    conv_w_full = _from_col_shards(got[0::2, 8:41].reshape(N_CHIP, 3 * 1408 // 128, 128), 3, 1408)
    gate_full = _from_col_shards(got[0::2, 41:49].reshape(N_CHIP, 16 * 64 // 128, 128), GLA_GATE_RANK, 64)
    mod_part = _ada_mod(cond_all, w_ada[0], name="ada_mod")
    mod_got = _all_gather_small(_rows128(mod_part), name="gather_mod").reshape(N_DEV, N_DEV, 1536)
    mod_all = mod_got[0::2].transpose(1, 0, 2).reshape(N_DEV, 6 * d) + b_ada
    mod = lax.dynamic_slice_in_dim(mod_all, dev, 1, axis=0)

    own = [w[0].astype(BF16).reshape(2, w.shape[1] // 2, w.shape[2]) for w in (w_in, w_out, w_up, w_down)]
    with_own = lambda got, mine: [lax.dynamic_update_index_in_dim(t, o, chip, 0) for t, o in zip(got, mine)]
    got_in, got_out = with_own(_gather_weight_shards(own[:2], name="gather_weights"), own[:2])
    w_in_full = got_in.reshape(N_CHIP, d, 772).transpose(1, 0, 2).reshape(d, N_CHIP * 772)
    w_out_full = got_out.reshape(d, d)
    exchanged = mod_all[0:1, 0:1] + got_in[0, 0, 0:1, 0:1].astype(F32)
    send_sems, recv_sems, own_thru, lands, token = _gather_late_start(own[2:], exchanged, name="gather_late_start")
    mod = mod + token[0:1, 0:1]

    def ffn_weights(after):
        mine, landed = _gather_late_wait(send_sems, recv_sems, own_thru, lands, after, name="gather_late_wait")
        got_up, got_down = with_own(landed, mine)
        return got_up.reshape(N_CHIP, d, 1408).transpose(1, 0, 2).reshape(d, 2 * D_FF), got_down.reshape(D_FF, d)

    ffn_reduce, attn_reduce, attn_parts = [], [], []
    halves_of = lambda g: g.reshape(N_CHIP, 2, g.shape[-2] // 2, g.shape[-1])

    def ffn_grads_ready(g_wup_b, g_wdown_b):
        ffn_reduce.extend(_direct_reduce_start([halves_of(g_wup_b), halves_of(g_wdown_b.reshape(N_CHIP, D_FF // N_CHIP, d))],
                                               name="reduce_ffn_start"))
        return ffn_reduce[4]

    def attn_grads_ready(g_wi, g_wo):
        attn_parts.extend([_in_proj_grad_layout(g_wi).reshape(d, N_CHIP, 772).transpose(1, 0, 2), g_wo.reshape(N_CHIP, d // N_CHIP, d)])
        attn_reduce.extend(_direct_reduce_start([halves_of(g.astype(BF16)) for g in attn_parts], name="reduce_attn_start"))
        return attn_reduce[4]

    err2, grad_x, (g_wi, g_wo, g_wup, g_wdown), small = _local_step(
        x[0], loss_target[0], mod, _in_proj_layout(w_in_full), w_out_full, ffn_weights, ffn_grads_ready, attn_grads_ready,
        conv_w_full, conv_b,
        _gate_layout(gate_full), gla_b_gate, gla_norm_g, q_norm_g, k_norm_g, norm1_g, norm2_g)

    pieces = [err2[0], small["dmod"], small["norm1_g"], small["norm2_g"], small["gla_w_gate"].reshape(-1), small["gla_b_gate"],
              small["gla_norm_g"], small["q_norm_g"], small["k_norm_g"], small["conv_w"].reshape(-1), small["conv_b"]]
    sizes = [p.shape[0] for p in pieces]
    at = [sum(sizes[:i]) for i in range(len(sizes) + 1)]
    vec = _rows128(jnp.concatenate(pieces), 288)
    got = _all_gather_small(vec, name="gather_grads").reshape(N_DEV, 288, 128)
    total, loss8 = _sum_devices(got, name="sum_devices")
    total = total.reshape(-1)
    seg = lambda i: total[at[i]:at[i + 1]]
    dmod_all = got.reshape(N_DEV, -1)[:, at[1]:at[2]]
    g_small = dict(
        b_ada=seg(1)[None], norm1_g=seg(2)[None], norm2_g=seg(3)[None],
        gla_w_gate=lax.dynamic_slice_in_dim(seg(4).reshape(GLA_GATE_RANK, 256), chip * 64, 64, axis=1),
        gla_b_gate=seg(5)[None], gla_norm_g=seg(6)[None], q_norm_g=seg(7)[None], k_norm_g=seg(8)[None],
        conv_w=lax.dynamic_slice_in_dim(seg(9).reshape(3, 2 * D_FF), chip * 1408, 1408, axis=1), conv_b=seg(10)[None])
    dmod_cols = lax.dynamic_slice_in_dim(dmod_all.reshape(N_DEV, 6 * d), chip * 1536, 1536, axis=1)
    g_w_ada = _ada_grad(cond_all, dmod_cols, name="ada_grad")

    core_id, chip_id = jnp.reshape(ac, (1,)).astype(jnp.int32), jnp.reshape(chip, (1,)).astype(jnp.int32)
    landed = (_direct_reduce_wait(*attn_reduce[:4], grad_x, name="reduce_attn_wait")
              + _direct_reduce_wait(*ffn_reduce[:4], grad_x, name="reduce_ffn_wait"))
    own = attn_parts + [g_wup, g_wdown.reshape(N_CHIP, D_FF // N_CHIP, d)]
    summed = [_direct_reduce_add(g, t, chip_id, core_id, name=f"reduce_add_{tag}")
              for g, t, tag in zip(own, landed, ("w_in", "w_out", "w_up", "w_down"))]
    others = _share_halves(summed, name="share_pair")
    g_big = [jnp.concatenate([jnp.where(ac == 0, mine, other), jnp.where(ac == 0, other, mine)], axis=0)
             for mine, other in zip(summed, others)]

    grads = dict(w_ada=g_w_ada, w_in=g_big[0], w_out=g_big[1], w_up=g_big[2], w_down=g_big[3], **g_small)
    names = ["w_ada", "b_ada", "norm1_g", "w_in", "gla_w_gate", "gla_b_gate", "gla_norm_g", "q_norm_g", "k_norm_g", "w_out",
             "norm2_g", "w_up", "conv_w", "conv_b", "w_down"]
    ws = dict(w_ada=w_ada, b_ada=b_ada, norm1_g=norm1_g, w_in=w_in, gla_w_gate=gla_w_gate, gla_b_gate=gla_b_gate,
              gla_norm_g=gla_norm_g, q_norm_g=q_norm_g, k_norm_g=k_norm_g, w_out=w_out, norm2_g=norm2_g, w_up=w_up,
              conv_w=conv_w, conv_b=conv_b, w_down=w_down)
    ms = dict(w_ada=m_w_ada, b_ada=m_b_ada, norm1_g=m_norm1_g, w_in=m_w_in, gla_w_gate=m_gla_w_gate, gla_b_gate=m_gla_b_gate,
              gla_norm_g=m_gla_norm_g, q_norm_g=m_q_norm_g, k_norm_g=m_k_norm_g, w_out=m_w_out, norm2_g=m_norm2_g, w_up=m_w_up,
              conv_w=m_conv_w, conv_b=m_conv_b, w_down=m_w_down)
    vs = dict(w_ada=v_w_ada, b_ada=v_b_ada, norm1_g=v_norm1_g, w_in=v_w_in, gla_w_gate=v_gla_w_gate, gla_b_gate=v_gla_b_gate,
              gla_norm_g=v_gla_norm_g, q_norm_g=v_q_norm_g, k_norm_g=v_k_norm_g, w_out=v_w_out, norm2_g=v_norm2_g, w_up=v_w_up,
              conv_w=v_conv_w, conv_b=v_conv_b, w_down=v_w_down)
    g_out, d_out, m_out, v_out = [], [], [], []
    for nm in names:
        w2 = ws[nm].reshape(ws[nm].shape[-2:])
        g2 = grads[nm].reshape(w2.shape)
        dl, mn, vn = _adamw(w2, g2, ms[nm].reshape(w2.shape), vs[nm].reshape(w2.shape), name=f"adamw_{nm}")
        shape = ws[nm].shape
        g_out.append(g2.reshape(shape))
        d_out.append(dl.reshape(shape))
        m_out.append(mn.reshape(shape))
        v_out.append(vn.reshape(shape))
    return (loss8[0, 0], grad_x[None], *g_out, *d_out, *m_out, *v_out)
```

```python
import functools

import jax
import jax.numpy as jnp
from jax import lax
from jax.experimental import pallas as pl
from jax.experimental.pallas import tpu as pltpu

F32, BF16 = jnp.float32, jnp.bfloat16
HIGHEST = lax.Precision.HIGHEST
MESH = pl.DeviceIdType.MESH

D_MODEL = 1024
GLA_CHUNK = 64
GLA_GATE_TAU = 16.0
GLA_GATE_RANK = 16
HEAD_LANES = 128
ATTN_BLOCK = 128
DILATIONS = (1, 4, 16)
ALIBI_SLOPES = tuple(2.0 ** (-(h + 1)) for h in range(8))
D_FF = 2816
EPS = 1e-6
C_GQ, C_GK, C_GV, C_GR, C_AQ, C_AK, C_AV, C_LR, PROJ_W = 0, 256, 512, 1024, 1536, 2048, 2560, 3072, 3200
ADAM_LR, ADAM_B1, ADAM_B2, ADAM_EPS, ADAM_WD, ADAM_STEP = 0.001, 0.9, 0.999, 1e-08, 0.01, 10
VMEM_LIMIT_BYTES = 56 * 1024 * 1024
ROW_TILE = 256


def _params(*sem):
    return pltpu.CompilerParams(dimension_semantics=sem or None, vmem_limit_bytes=VMEM_LIMIT_BYTES)


def _nt(a, b):
    return lax.dot_general(a, b, (((1,), (1,)), ((), ())), preferred_element_type=F32)


def _tn(a, b):
    return lax.dot_general(a, b, (((0,), (0,)), ((), ())), preferred_element_type=F32)


def _nn(a, b, precision=None):
    return jnp.dot(a, b, preferred_element_type=F32, precision=precision)


def _split3(v):
    hi = v.astype(BF16)
    rest = v - hi.astype(F32)
    mid = rest.astype(BF16)
    return hi, mid, (rest - mid.astype(F32)).astype(BF16)


def _sum_right(v, ones):
    hi, mid, lo = _split3(v)
    return (_nn(lo, ones) + _nn(mid, ones)) + _nn(hi, ones)


def _sum_left(ones, v):
    hi, mid, lo = _split3(v)
    return (_nn(ones, lo) + _nn(ones, mid)) + _nn(ones, hi)


def _fold8(v):
    return v.reshape(v.shape[0] // 8, 8, v.shape[1]).sum(axis=0)


def _spread_total(ref):
    t = ref[...]
    ref[...] = jnp.broadcast_to(jnp.sum(t, axis=-2, keepdims=True), t.shape)


def _sigmoid(x):
    return 1.0 / (1.0 + jnp.exp(-x))


def _mm(a, b, *, ta=False, tb=False, out_dtype=F32, tm, tn, tk, shard_cols=False, also_bf16=False, name):
    (k_a, m) = a.shape if ta else a.shape[::-1]
    (k_b, n) = b.shape[::-1] if tb else b.shape
    assert k_a == k_b and m % tm == 0 and n % tn == 0 and k_a % tk == 0, (name, a.shape, b.shape)
    nk = k_a // tk
    assert nk == 1 or out_dtype == F32, name
    dims = (((0 if ta else 1,), (1 if tb else 0,)), ((), ()))

    def body(a_ref, b_ref, o_ref, *rounded):
        k = pl.program_id(2)
        part = lax.dot_general(a_ref[...].astype(BF16), b_ref[...].astype(BF16), dims, preferred_element_type=F32)
        if nk == 1:
            o_ref[...] = part.astype(out_dtype)
        else:
            @pl.when(k == 0)
            def _():
                o_ref[...] = part

            @pl.when(k > 0)
            def _():
                o_ref[...] += part

        if also_bf16:
            @pl.when(k == nk - 1)
            def _():
                rounded[0][...] = o_ref[...].astype(BF16)

    a_spec = pl.BlockSpec((tk, tm), lambda i, j, k: (k, i)) if ta else pl.BlockSpec((tm, tk), lambda i, j, k: (i, k))
    b_spec = pl.BlockSpec((tn, tk), lambda i, j, k: (j, k)) if tb else pl.BlockSpec((tk, tn), lambda i, j, k: (k, j))
    if shard_cols:
        o_spec, o_shape = pl.BlockSpec((None, tm, tn), lambda i, j, k: (j, i, 0)), (n // tn, m, tn)
    else:
        o_spec, o_shape = pl.BlockSpec((tm, tn), lambda i, j, k: (i, j)), (m, n)
    shapes = [jax.ShapeDtypeStruct(o_shape, out_dtype)] + ([jax.ShapeDtypeStruct(o_shape, BF16)] if also_bf16 else [])
    out = pl.pallas_call(
        body, name=name, grid=(m // tm, n // tn, nk), in_specs=[a_spec, b_spec], out_specs=[o_spec] * len(shapes),
        out_shape=shapes, compiler_params=_params("parallel", "parallel", "arbitrary"))(a, b)
    return out if also_bf16 else out[0]


def _norm_mod_fwd(x, branch, gate, gain, scale, shift, *, name):
    s, d = x.shape
    tm = ROW_TILE
    has_branch = branch is not None

    def body(*refs):
        if has_branch:
            x_ref, br_ref, gate_ref, gain_ref, sc_ref, sh_ref, x1_ref, h_ref, ht_ref = refs
            xv = x_ref[...] + gate_ref[...] * br_ref[...]
            x1_ref[...] = xv
        else:
            x_ref, gain_ref, sc_ref, sh_ref, h_ref, ht_ref = refs
            xv = x_ref[...]
        r = lax.rsqrt(jnp.mean(xv * xv, axis=-1, keepdims=True) + EPS)
        h = (xv * r) * gain_ref[...] * (1.0 + sc_ref[...]) + sh_ref[...]
        h_ref[...] = h.astype(BF16)
        ht_ref[...] = h.T.astype(BF16)

    row = pl.BlockSpec((tm, d), lambda i: (i, 0))
    col = pl.BlockSpec((d, tm), lambda i: (0, i))
    vec = pl.BlockSpec((1, d), lambda i: (0, 0))
    h_shapes = [jax.ShapeDtypeStruct((s, d), BF16), jax.ShapeDtypeStruct((d, s), BF16)]
    if has_branch:
        return pl.pallas_call(
            body, name=name, grid=(s // tm,), in_specs=[row, row, vec, vec, vec, vec], out_specs=[row, row, col],
            out_shape=[jax.ShapeDtypeStruct((s, d), F32)] + h_shapes,
            compiler_params=_params("parallel"))(x, branch, gate, gain, scale, shift)
    h, ht = pl.pallas_call(
        body, name=name, grid=(s // tm,), in_specs=[row, vec, vec, vec], out_specs=[row, col],
        out_shape=h_shapes, compiler_params=_params("parallel"))(x, gain, scale, shift)
    return x, h, ht


def _norm_mod_bwd(x, dh, dres, gain, scale, branch, gate, *, name):
    s, d = x.shape
    tm = ROW_TILE
    has_branch = branch is not None

    def body(*refs):
        if has_branch:
            x_ref, dh_ref, dres_ref, gain_ref, sc_ref, br_ref, gate_ref, dx_ref, dbr_ref, sums_ref = refs
        else:
            x_ref, dh_ref, dres_ref, gain_ref, sc_ref, dx_ref, sums_ref = refs
        i = pl.program_id(0)

        @pl.when(i == 0)
        def _():
            sums_ref[...] = jnp.zeros_like(sums_ref)

        xv, dhv = x_ref[...], dh_ref[...]
        r = lax.rsqrt(jnp.mean(xv * xv, axis=-1, keepdims=True) + EPS)
        xn = xv * r
        dxn = dhv * (gain_ref[...] * (1.0 + sc_ref[...]))
        dx = dres_ref[...] + r * (dxn - xn * jnp.mean(dxn * xn, axis=-1, keepdims=True))
        dx_ref[...] = dx
        sums_ref[0] += _fold8(dhv * xn)
        sums_ref[1] += _fold8(dhv)
        if has_branch:
            dbr_ref[...] = (gate_ref[...] * dx).astype(BF16)
            sums_ref[2] += _fold8(dx * br_ref[...])

        @pl.when(i == s // tm - 1)
        def _():
            _spread_total(sums_ref)

    row = pl.BlockSpec((tm, d), lambda i: (i, 0))
    vec = pl.BlockSpec((1, d), lambda i: (0, 0))
    sums = pl.BlockSpec((3, 8, d), lambda i: (0, 0, 0))
    sums_shape = jax.ShapeDtypeStruct((3, 8, d), F32)
    if has_branch:
        return pl.pallas_call(
            body, name=name, grid=(s // tm,), in_specs=[row, row, row, vec, vec, row, vec], out_specs=[row, row, sums],
            out_shape=[jax.ShapeDtypeStruct((s, d), F32), jax.ShapeDtypeStruct((s, d), BF16), sums_shape],
            compiler_params=_params("arbitrary"))(x, dh, dres, gain, scale, branch, gate)
    dx, sm = pl.pallas_call(
        body, name=name, grid=(s // tm,), in_specs=[row, row, row, vec, vec], out_specs=[row, sums],
        out_shape=[jax.ShapeDtypeStruct((s, d), F32), sums_shape],
        compiler_params=_params("arbitrary"))(x, dh, dres, gain, scale)
    return dx, None, sm


GLA_ROWS = 256


def _gla_block_setup(lr_ref, wg_ref, bg_ref):
    t, c = GLA_ROWS, GLA_CHUNK
    ri = lax.broadcasted_iota(jnp.int32, (t, t), 0)
    ci = lax.broadcasted_iota(jnp.int32, (t, t), 1)
    same = (ri // c) == (ci // c)
    causal, upper = same & (ci <= ri), same & (ci >= ri)
    z = _nn(lr_ref[...].astype(BF16), wg_ref[...]) + bg_ref[...]
    g = (jnp.minimum(z, 0.0) - jnp.log(1.0 + jnp.exp(-jnp.abs(z)))) * (1.0 / GLA_GATE_TAU)
    hi, mid, lo = _split3(g)
    total = lambda ones: (_nn(ones, lo) + _nn(ones, mid)) + _nn(ones, hi)
    return z, total(causal.astype(BF16)), total(same.astype(BF16)), causal, upper


def _chunks(t):
    return [t[i * GLA_CHUNK:(i + 1) * GLA_CHUNK] for i in range(GLA_ROWS // GLA_CHUNK)]


def _gla_fwd(proj, wg, bg, gn, *, name):
    s = proj.shape[0]
    tb, c = GLA_ROWS, GLA_CHUNK
    cb = tb // c

    def body(q_ref, k_ref, v_ref, r_ref, lr_ref, wg_ref, bg_ref, gn_ref, o_ref, y_ref, st_ref, state):
        i = pl.program_id(0)

        @pl.when(i == 0)
        def _():
            state[...] = jnp.zeros_like(state)

        low = lax.broadcasted_iota(jnp.int32, (tb, HEAD_LANES), 1) < 64
        masks = (low, jnp.logical_not(low))
        _, b, b_end, causal, _ = _gla_block_setup(lr_ref, wg_ref, bg_ref)
        pairs = []
        for p in range(2):
            cols = pl.ds(p * HEAD_LANES, HEAD_LANES)
            bp, bep = (t[:, p * HEAD_LANES:(p + 1) * HEAD_LANES] for t in (b, b_end))
            k = k_ref[:, cols]
            q_in = q_ref[:, cols] * 0.125 * jnp.exp(bp)
            k_out = (k * jnp.exp(-bp)).astype(BF16)
            k_end = k * jnp.exp(bep - bp)
            qms = [jnp.where(m, q_in, 0.0).astype(BF16) for m in masks]
            kes = [jnp.where(m, k_end, 0.0).astype(BF16) for m in masks]
            vs = [v_ref[:, pl.ds((2 * p + e) * HEAD_LANES, HEAD_LANES)].astype(BF16) for e in range(2)]
            grow = [_tn(v0, k0) + _tn(v1, k1) for v0, k0, v1, k1 in zip(_chunks(vs[0]), _chunks(kes[0]), _chunks(vs[1]), _chunks(kes[1]))]
            pairs.append((bep, k_out, qms, vs, grow))
        entering = [[], []]
        for p, (bep, _, _, _, grow) in enumerate(pairs):
            st = state[p]
            for ch in range(cb):
                entering[p].append(st)
                st_ref[ch, p] = st
                st = st * jnp.exp(bep[ch * c:ch * c + 1, :]) + grow[ch]
            state[p] = st
        for p, (_, k_out, qms, vs, _) in enumerate(pairs):
            for e in range(2):
                hc = pl.ds((2 * p + e) * HEAD_LANES, HEAD_LANES)
                a = jnp.where(causal, _nt(qms[e], k_out), 0.0).astype(BF16)
                carried = jnp.concatenate([_nt(qc, sc.astype(BF16)) for qc, sc in zip(_chunks(qms[e]), entering[p])], axis=0)
                o = _nn(a, vs[e]) + carried
                o_ref[:, hc] = o
                rr = r_ref[:, hc]
                on = o * lax.rsqrt(jnp.mean(o * o, axis=-1, keepdims=True) + EPS)
                y_ref[:, hc] = (on * gn_ref[...] * (rr * _sigmoid(rr))).astype(BF16)

    def col(width, at):
        return pl.BlockSpec((tb, width), lambda i: (i, at // width))

    full = lambda shape: pl.BlockSpec(shape, lambda i: tuple(0 for _ in shape))
    return pl.pallas_call(
        body, name=name, grid=(s // tb,),
        in_specs=[col(256, C_GQ), col(256, C_GK), col(512, C_GV), col(512, C_GR), col(128, C_LR),
                  full((HEAD_LANES, 256)), full((1, 256)), full((1, HEAD_LANES))],
        out_specs=[pl.BlockSpec((tb, 512), lambda i: (i, 0)), pl.BlockSpec((tb, 512), lambda i: (i, 0)),
                   pl.BlockSpec((cb, 2, HEAD_LANES, HEAD_LANES), lambda i: (i, 0, 0, 0))],
        out_shape=[jax.ShapeDtypeStruct((s, 512), F32), jax.ShapeDtypeStruct((s, 512), BF16),
                   jax.ShapeDtypeStruct((s // c, 2, HEAD_LANES, HEAD_LANES), F32)],
        scratch_shapes=[pltpu.VMEM((2, HEAD_LANES, HEAD_LANES), F32)],
        compiler_params=_params("arbitrary"))(proj, proj, proj, proj, proj, wg, bg, gn)


def _gla_bwd(proj, wg, bg, gn, o_raw, states, dmixed, *, name):
    s = proj.shape[0]
    tb, c = GLA_ROWS, GLA_CHUNK
    cb = tb // c
    nblk, nch = s // tb, s // c

    def body(q_ref, k_ref, v_ref, r_ref, lr_ref, wg_ref, bg_ref, gn_ref, o_ref, st_ref, stn_ref, dy_ref,
             dq_ref, dk_ref, dv_ref, dr_ref, dlr_ref, gwg_ref, sums_ref, dstate):
        i = pl.program_id(0)

        @pl.when(i == 0)
        def _():
            dstate[...] = jnp.zeros_like(dstate)
            gwg_ref[...] = jnp.zeros_like(gwg_ref)
            sums_ref[...] = jnp.zeros_like(sums_ref)

        low = lax.broadcasted_iota(jnp.int32, (tb, HEAD_LANES), 1) < 64
        masks = (low, jnp.logical_not(low))
        z, b, b_end, causal, upper = _gla_block_setup(lr_ref, wg_ref, bg_ref)
        lr_b = lr_ref[...].astype(BF16)
        dlr = jnp.zeros((tb, HEAD_LANES), F32)
        per_chunk = lambda rows, mats, fn: jnp.concatenate([fn(r, m.astype(BF16)) for r, m in zip(_chunks(rows), mats)], axis=0)
        pairs = []
        for p in range(2):
            cols = pl.ds(p * HEAD_LANES, HEAD_LANES)
            sl = slice(p * HEAD_LANES, (p + 1) * HEAD_LANES)
            bp, bep = b[:, sl], b_end[:, sl]
            e_in, e_out, e_end = jnp.exp(bp), jnp.exp(-bp), jnp.exp(bep - bp)
            q = q_ref[:, cols] * 0.125
            k = k_ref[:, cols]
            q_in, k_out, k_end = q * e_in, k * e_out, k * e_end
            qms = [jnp.where(m, q_in, 0.0).astype(BF16) for m in masks]
            kms_out = [jnp.where(m, k_out, 0.0).astype(BF16) for m in masks]
            kms_end = [jnp.where(m, k_end, 0.0).astype(BF16) for m in masks]
            vs, dos = [], []
            for e in range(2):
                hc = pl.ds((2 * p + e) * HEAD_LANES, HEAD_LANES)
                o, rr, dy = o_ref[:, hc], r_ref[:, hc], dy_ref[:, hc]
                sg = _sigmoid(rr)
                rs = lax.rsqrt(jnp.mean(o * o, axis=-1, keepdims=True) + EPS)
                on = o * rs
                t = dy * (rr * sg)
                sums_ref[1, :, hc] += _fold8(t * on)
                dn = t * gn_ref[...]
                dos.append((rs * (dn - on * jnp.mean(dn * on, axis=-1, keepdims=True))).astype(BF16))
                dr_ref[:, hc] = (dy * on * gn_ref[...] * (sg * (1.0 + rr * (1.0 - sg)))).astype(BF16)
                vs.append(v_ref[:, hc].astype(BF16))
            grow = [_tn(d0, q0) + _tn(d1, q1) for d0, q0, d1, q1 in zip(_chunks(dos[0]), _chunks(qms[0]), _chunks(dos[1]), _chunks(qms[1]))]
            pairs.append((bep, e_in, e_out, e_end, q, k, qms, kms_out, kms_end, vs, dos, grow))
        chains = []
        for p in range(2):
            bep, grow = pairs[p][0], pairs[p][-1]
            entering = [st_ref[ch, p] for ch in range(cb)]
            dst, leaving_grad = dstate[p], [None] * cb
            for ch in reversed(range(cb)):
                leaving_grad[ch] = dst
                dst = dst * jnp.exp(bep[ch * c:ch * c + 1, :]) + grow[ch]
            dstate[p] = dst
            chains.append((entering, leaving_grad))
        for p in range(2):
            cols = pl.ds(p * HEAD_LANES, HEAD_LANES)
            sl = slice(p * HEAD_LANES, (p + 1) * HEAD_LANES)
            _, e_in, e_out, e_end, q, k, qms, kms_out, kms_end, vs, dos, _ = pairs[p]
            entering, leaving_grad = chains[p]
            leaving = entering[1:] + [stn_ref[0, p]]
            felt = jnp.concatenate([jnp.broadcast_to(jnp.sum(dg_st * st, axis=0, keepdims=True), (c, HEAD_LANES))
                                    for dg_st, st in zip(leaving_grad, leaving)], axis=0)
            dq_in = jnp.zeros((tb, HEAD_LANES), F32)
            dk_out = jnp.zeros((tb, HEAD_LANES), F32)
            dk_end = jnp.zeros((tb, HEAD_LANES), F32)
            for e in range(2):
                hc = pl.ds((2 * p + e) * HEAD_LANES, HEAD_LANES)
                a = jnp.where(causal, _nt(qms[e], kms_out[e]), 0.0).astype(BF16)
                da = jnp.where(causal, _nt(dos[e], vs[e]), 0.0).astype(BF16)
                dv_ref[:, hc] = (_tn(a, dos[e]) + per_chunk(kms_end[e], leaving_grad, _nt)).astype(BF16)
                dq_in = dq_in + jnp.where(masks[e], per_chunk(dos[e], entering, _nn) + _nn(da, kms_out[e]), 0.0)
                dk_out = dk_out + _tn(da, qms[e])
                dk_end = dk_end + jnp.where(masks[e], per_chunk(vs[e], leaving_grad, _nn), 0.0)
            dq = dq_in * e_in
            dk = dk_out * e_out + dk_end * e_end
            dq_ref[:, cols] = (dq * 0.125).astype(BF16)
            dk_ref[:, cols] = dk.astype(BF16)
            dg = _sum_left(upper.astype(BF16), q * dq - k * dk) + felt
            dz = dg * (1.0 / GLA_GATE_TAU) * _sigmoid(-z[:, sl])
            dz_b = dz.astype(BF16)
            sums_ref[0, :, cols] += _fold8(dz)
            dlr = dlr + _nt(dz_b, wg_ref[:, cols])
            gwg_ref[:, cols] += _tn(lr_b, dz_b)
        dlr_ref[...] = dlr.astype(BF16)

        @pl.when(i == nblk - 1)
        def _():
            _spread_total(sums_ref)

    rev = lambda i: nblk - 1 - i

    def col(width, at):
        return pl.BlockSpec((tb, width), lambda i: (rev(i), at // width))

    full = lambda shape: pl.BlockSpec(shape, lambda i: tuple(0 for _ in shape))
    out_col = lambda width: pl.BlockSpec((tb, width), lambda i: (rev(i), 0))
    return pl.pallas_call(
        body, name=name, grid=(nblk,),
        in_specs=[col(256, C_GQ), col(256, C_GK), col(512, C_GV), col(512, C_GR), col(128, C_LR),
                  full((HEAD_LANES, 256)), full((1, 256)), full((1, HEAD_LANES)),
                  pl.BlockSpec((tb, 512), lambda i: (rev(i), 0)),
                  pl.BlockSpec((cb, 2, HEAD_LANES, HEAD_LANES), lambda i: (rev(i), 0, 0, 0)),
                  pl.BlockSpec((1, 2, HEAD_LANES, HEAD_LANES), lambda i: (jnp.minimum((rev(i) + 1) * cb, nch - 1), 0, 0, 0)),
                  pl.BlockSpec((tb, 512), lambda i: (rev(i), 0))],
        out_specs=[out_col(256), out_col(256), out_col(512), out_col(512), out_col(128),
                   full((HEAD_LANES, 256)), full((2, 8, 512))],
        out_shape=[jax.ShapeDtypeStruct((s, 256), BF16), jax.ShapeDtypeStruct((s, 256), BF16),
                   jax.ShapeDtypeStruct((s, 512), BF16), jax.ShapeDtypeStruct((s, 512), BF16),
                   jax.ShapeDtypeStruct((s, 128), BF16), jax.ShapeDtypeStruct((HEAD_LANES, 256), F32),
                   jax.ShapeDtypeStruct((2, 8, 512), F32)],
        scratch_shapes=[pltpu.VMEM((2, HEAD_LANES, HEAD_LANES), F32)],
        compiler_params=_params("arbitrary"))(proj, proj, proj, proj, proj, wg, bg, gn, o_raw, states, states, dmixed)


def _head_sums(v):
    ri = lax.broadcasted_iota(jnp.int32, (HEAD_LANES, HEAD_LANES), 0) // 64
    ci = lax.broadcasted_iota(jnp.int32, (HEAD_LANES, HEAD_LANES), 1) // 64
    ones = (ri == ci).astype(BF16)
    return jnp.concatenate([_sum_right(v[:, p * HEAD_LANES:(p + 1) * HEAD_LANES], ones) for p in range(4)], axis=1)


def _attn_prep(proj, qg, kg, *, name):
    s = proj.shape[0]
    tm = ROW_TILE

    def body(q_ref, k_ref, qg_ref, kg_ref, qa_ref, ka_ref):
        q, k = q_ref[...], k_ref[...]
        qr = lax.rsqrt(_head_sums(q * q) * (1.0 / 64) + EPS)
        kr = lax.rsqrt(_head_sums(k * k) * (1.0 / 64) + EPS)
        qa_ref[...] = q * qr * qg_ref[...] * 0.125
        ka_ref[...] = k * kr * kg_ref[...]

    col = lambda at: pl.BlockSpec((tm, 512), lambda i: (i, at // 512))
    vec = pl.BlockSpec((1, 512), lambda i: (0, 0))
    out = pl.BlockSpec((tm, 512), lambda i: (i, 0))
    return pl.pallas_call(
        body, name=name, grid=(s // tm,), in_specs=[col(C_AQ), col(C_AK), vec, vec], out_specs=[out] * 2,
        out_shape=[jax.ShapeDtypeStruct((s, 512), F32)] * 2, compiler_params=_params("parallel"))(proj, proj, qg, kg)


FAR = 1e30
LOG2E, LN2 = 1.4426950408889634, 0.6931471805599453


def _attn_distance(first):
    blk = ATTN_BLOCK
    iq = lax.broadcasted_iota(jnp.int32, (2 * blk, 2 * blk), 0) & (blk - 1)
    ik = lax.broadcasted_iota(jnp.int32, (2 * blk, 2 * blk), 1)
    rel = iq + blk - ik
    valid = (rel >= 0) & (rel <= blk) & (jnp.logical_not(first) | (ik >= blk))
    return jnp.where(valid, rel.astype(F32), FAR)


def _stack_heads(t2):
    low = lax.broadcasted_iota(jnp.int32, t2.shape, 1) < 64
    return jnp.concatenate([jnp.where(low, t2, 0.0), jnp.where(low, 0.0, t2)], axis=0).astype(BF16)


def _unstack_heads(t):
    blk = ATTN_BLOCK
    low = lax.broadcasted_iota(jnp.int32, (blk, HEAD_LANES), 1) < 64
    return jnp.where(low, t[0:blk], t[blk:2 * blk])


def _attn_scores(qs, kcat, slopes, dil, dist):
    top = lax.broadcasted_iota(jnp.int32, (2 * ATTN_BLOCK, 1), 0) < ATTN_BLOCK
    return _nt(qs, kcat) - jnp.where(top, slopes[0] * (dil * LOG2E), slopes[1] * (dil * LOG2E)) * dist


def _pair_slopes(p):
    if isinstance(p, int):
        return ALIBI_SLOPES[2 * p], ALIBI_SLOPES[2 * p + 1]
    pick = lambda e: jnp.where(p == 0, ALIBI_SLOPES[e], jnp.where(p == 1, ALIBI_SLOPES[2 + e],
                               jnp.where(p == 2, ALIBI_SLOPES[4 + e], ALIBI_SLOPES[6 + e])))
    return pick(0), pick(1)


ATTN_GROUP = 4


def _each(fn, *lists):
    return [fn(*args) for args in zip(*lists)]


def _attn_group_fwd(q2s, kcats, vcats, slopes, dil, dist):
    qs = _each(lambda q2: _stack_heads(q2 * LOG2E), q2s)
    sc = _each(lambda q, k, sl: _attn_scores(q, k, sl, dil, dist), qs, kcats, slopes)
    m = _each(lambda s: jnp.max(s, axis=-1, keepdims=True), sc)
    pr = _each(lambda s, mx: jnp.exp2(s - mx), sc, m)
    den = _each(lambda p: jnp.sum(p, axis=-1, keepdims=True), pr)
    o = _each(lambda p, v, d: _nn(p.astype(BF16), v) / d, pr, vcats, den)
    lse = _each(lambda mx, d, t: jnp.broadcast_to(mx + jnp.log2(d), t.shape), m, den, o)
    return _each(lambda t, l: (_unstack_heads(t), _unstack_heads(l)), o, lse)


def _attn_group_bwd(q2s, kcats, vcats, do2s, y2s, lse2s, slopes, dil, dist):
    lane = lax.broadcasted_iota(jnp.int32, (ATTN_BLOCK, HEAD_LANES), 1)
    low = lane < 64
    per_head = lambda t, pick: jnp.concatenate([jnp.sum(jnp.where(pick(0), t, 0.0), axis=-1, keepdims=True),
                                                jnp.sum(jnp.where(pick(1), t, 0.0), axis=-1, keepdims=True)], axis=0)
    lse = _each(lambda l: per_head(l, lambda e: lane == 64 * e), lse2s)
    delta = _each(lambda d, y: per_head(d * y, lambda e: low if e == 0 else jnp.logical_not(low)), do2s, y2s)
    qs = _each(lambda q2: _stack_heads(q2 * LOG2E), q2s)
    dos = _each(_stack_heads, do2s)
    sc = _each(lambda q, k, sl: _attn_scores(q, k, sl, dil, dist), qs, kcats, slopes)
    pr = _each(lambda s, l: jnp.exp2(s - l), sc, lse)
    dp = _each(_nt, dos, vcats)
    ds = _each(lambda p, d, dl: (p * (d - dl)).astype(BF16), pr, dp, delta)
    dq = _each(lambda d, k: _unstack_heads(_nn(d, k)), ds, kcats)
    dk = _each(lambda d, q: _tn(d, q) * LN2, ds, qs)
    dv = _each(lambda p, d: _tn(p.astype(BF16), d), pr, dos)
    return list(zip(dq, dk, dv))


def _attn_specs(dil):
    rows = ATTN_BLOCK * dil
    if dil == 1:
        cur = lambda at: pl.BlockSpec((rows, 512), lambda n: (n, at // 512))
        prev = lambda at: pl.BlockSpec((rows, 512), lambda n: (jnp.maximum(n - 1, 0), at // 512))
    else:
        cur = lambda at: pl.BlockSpec((rows, HEAD_LANES), lambda n, p: (n, at // HEAD_LANES + p))
        prev = lambda at: pl.BlockSpec((rows, HEAD_LANES), lambda n, p: (jnp.maximum(n - 1, 0), at // HEAD_LANES + p))
    return cur, prev


def _attn_loop(dil, one_group):
    if dil == 1:
        one_group([(slice(None), pl.ds(p * HEAD_LANES, HEAD_LANES), p) for p in range(ATTN_GROUP)])
    else:
        p = pl.program_id(1)

        def step(g, carry):
            one_group([(pl.ds(g * ATTN_GROUP + j, ATTN_BLOCK, stride=dil), slice(None), p) for j in range(ATTN_GROUP)])
            return carry

        if dil == ATTN_GROUP:
            step(0, 0)
        else:
            lax.fori_loop(0, dil // ATTN_GROUP, step, 0)


def _dil_attn_fwd(qa, ka, proj, dil, *, name):
    s = qa.shape[0]

    def body(q_ref, kp_ref, kc_ref, vp_ref, vc_ref, o_ref, lse_ref):
        dist = _attn_distance(pl.program_id(0) == 0)

        def one_group(items):
            both = lambda a, b: [jnp.concatenate([a[rows, cols], b[rows, cols]], axis=0).astype(BF16) for rows, cols, _ in items]
            outs = _attn_group_fwd([q_ref[rows, cols] for rows, cols, _ in items], both(kp_ref, kc_ref), both(vp_ref, vc_ref),
                                   [_pair_slopes(p) for _, _, p in items], dil, dist)
            for (rows, cols, _), (o2, lse2) in zip(items, outs):
                o_ref[rows, cols] = o2
                lse_ref[rows, cols] = lse2

        _attn_loop(dil, one_group)

    cur, prev = _attn_specs(dil)
    grid = (s // ATTN_BLOCK,) if dil == 1 else (s // (ATTN_BLOCK * dil), 4)
    return pl.pallas_call(
        body, name=name, grid=grid, in_specs=[cur(0), prev(0), cur(0), prev(C_AV), cur(C_AV)], out_specs=[cur(0), cur(0)],
        out_shape=[jax.ShapeDtypeStruct((s, 512), F32)] * 2,
        compiler_params=_params(*["parallel"] * len(grid)))(qa, ka, ka, proj, proj)


def _attn_merge(branches, y_gla, *, name):
    s = y_gla.shape[0]
    tm = ROW_TILE

    def body(o0, l0, o1, l1, o2, l2, yg_ref, mixed_ref, y_ref, lse_ref):
        m = jnp.maximum(jnp.maximum(l0[...], l1[...]), l2[...])
        w0, w1, w2 = jnp.exp2(l0[...] - m), jnp.exp2(l1[...] - m), jnp.exp2(l2[...] - m)
        zs = w0 + w1 + w2
        y = (w0 * o0[...] + w1 * o1[...] + w2 * o2[...]) / zs
        y_ref[...] = y
        lse_ref[...] = m + jnp.log2(zs)
        mixed_ref[:, 0:512] = yg_ref[...]
        mixed_ref[:, 512:1024] = y.astype(BF16)

    blk = pl.BlockSpec((tm, 512), lambda i: (i, 0))
    args = [t for pair in branches for t in pair]
    return pl.pallas_call(
        body, name=name, grid=(s // tm,), in_specs=[blk] * 7,
        out_specs=[pl.BlockSpec((tm, 1024), lambda i: (i, 0)), blk, blk],
        out_shape=[jax.ShapeDtypeStruct((s, 1024), BF16), jax.ShapeDtypeStruct((s, 512), F32),
                   jax.ShapeDtypeStruct((s, 512), F32)],
        compiler_params=_params("parallel"))(*args, y_gla)


def _dil_attn_bwd(qa, ka, proj, y_att, lse, dmixed, dil, *, name):
    s = qa.shape[0]
    blk = ATTN_BLOCK

    def body(q_ref, kp_ref, kc_ref, vp_ref, vc_ref, y_ref, lse_ref, do_ref, dq_ref, dkc_ref, dkp_ref, dvc_ref, dvp_ref):
        dist = _attn_distance(pl.program_id(0) == 0)

        def one_group(items):
            both = lambda a, b: [jnp.concatenate([a[rows, cols], b[rows, cols]], axis=0).astype(BF16) for rows, cols, _ in items]
            at = lambda ref: [ref[rows, cols] for rows, cols, _ in items]
            outs = _attn_group_bwd(at(q_ref), both(kp_ref, kc_ref), both(vp_ref, vc_ref), at(do_ref), at(y_ref), at(lse_ref),
                                   [_pair_slopes(p) for _, _, p in items], dil, dist)
            for (rows, cols, _), (dq, dk, dv) in zip(items, outs):
                dq_ref[rows, cols] = dq
                dkp_ref[rows, cols] = dk[0:blk]
                dkc_ref[rows, cols] = dk[blk:2 * blk]
                dvp_ref[rows, cols] = dv[0:blk]
                dvc_ref[rows, cols] = dv[blk:2 * blk]

        _attn_loop(dil, one_group)

    cur, prev = _attn_specs(dil)
    grid = (s // blk,) if dil == 1 else (s // (blk * dil), 4)
    return pl.pallas_call(
        body, name=name, grid=grid,
        in_specs=[cur(0), prev(0), cur(0), prev(C_AV), cur(C_AV), cur(0), cur(0), cur(512)], out_specs=[cur(0)] * 5,
        out_shape=[jax.ShapeDtypeStruct((s, 512), F32)] * 5, compiler_params=_params(*["parallel"] * len(grid)),
    )(qa, ka, ka, proj, proj, y_att, lse, dmixed)


def _attn_post(parts, proj, qg, kg, *, name):
    s = proj.shape[0]
    tm = ATTN_BLOCK
    nblk = s // tm

    def body(*refs):
        ins, (q_ref, k_ref, qg_ref, kg_ref, dq_out, dk_out, dv_out, sums_ref) = refs[:15], refs[15:]
        i = pl.program_id(0)

        @pl.when(i == 0)
        def _():
            sums_ref[...] = jnp.zeros_like(sums_ref)

        dq = jnp.zeros((tm, 512), F32)
        dk = jnp.zeros((tm, 512), F32)
        dv = jnp.zeros((tm, 512), F32)
        for g, dil in enumerate(DILATIONS):
            dq_r, dkc_r, dkp_r, dvc_r, dvp_r = ins[5 * g:5 * g + 5]
            inside = (i + dil < nblk).astype(F32)
            dq = dq + dq_r[...]
            dk = dk + dkc_r[...] + inside * dkp_r[...]
            dv = dv + dvc_r[...] + inside * dvp_r[...]
        dv_out[...] = dv.astype(BF16)
        for row, (x_ref, g_ref, dy, out, post) in enumerate(((q_ref, qg_ref, dq, dq_out, 0.125), (k_ref, kg_ref, dk, dk_out, 1.0))):
            x = x_ref[...]
            rs = lax.rsqrt(_head_sums(x * x) * (1.0 / 64) + EPS)
            xn = x * rs
            dy = dy * post
            sums_ref[row] += _fold8(dy * xn)
            dn = dy * g_ref[...]
            out[...] = (rs * (dn - xn * (_head_sums(dn * xn) * (1.0 / 64)))).astype(BF16)

        @pl.when(i == nblk - 1)
        def _():
            _spread_total(sums_ref)

    here = pl.BlockSpec((tm, 512), lambda i: (i, 0))
    specs = []
    for dil in DILATIONS:
        later = pl.BlockSpec((tm, 512), lambda i, dil=dil: (jnp.minimum(i + dil, nblk - 1), 0))
        specs += [here, here, later, here, later]
    col = lambda at: pl.BlockSpec((tm, 512), lambda i: (i, at // 512))
    vec = pl.BlockSpec((1, 512), lambda i: (0, 0))
    return pl.pallas_call(
        body, name=name, grid=(nblk,), in_specs=specs + [col(C_AQ), col(C_AK), vec, vec],
        out_specs=[here, here, here, pl.BlockSpec((2, 8, 512), lambda i: (0, 0, 0))],
        out_shape=[jax.ShapeDtypeStruct((s, 512), BF16)] * 3 + [jax.ShapeDtypeStruct((2, 8, 512), F32)],
        compiler_params=_params("arbitrary"))(*[t for part in parts for t in part], proj, proj, qg, kg)


FFN_TM, FFN_TN = 256, 1408
HALO = 16


def _conv3(u_ref, halo_ref, w_ref, b_ref, first):
    u = u_ref[...].astype(F32)
    ext = jnp.concatenate([jnp.where(first, 0.0, halo_ref[...].astype(F32)), u], axis=0)
    u1 = pltpu.roll(ext, 1, 0)[HALO:]
    u2 = pltpu.roll(ext, 2, 0)[HALO:]
    return b_ref[...] + w_ref[0:1, :] * u2 + w_ref[1:2, :] * u1 + w_ref[2:3, :] * u, u, u1, u2


def _ffn_specs(tm, tn):
    nj = D_FF // tn
    blk = lambda half: pl.BlockSpec((tm, tn), lambda j, i: (i, j + half * nj))
    halo = lambda half: pl.BlockSpec((HALO, tn), lambda j, i: (jnp.maximum(i * (tm // HALO) - 1, 0), j + half * nj))
    wspec = lambda half: pl.BlockSpec((3, tn), lambda j, i: (0, j + half * nj))
    bspec = lambda half: pl.BlockSpec((1, tn), lambda j, i: (0, j + half * nj))
    return [blk(0), halo(0), blk(1), halo(1), wspec(0), wspec(1), bspec(0), bspec(1)]


def _conv_swiglu_fwd(u, conv_w, conv_b, *, name):
    s = u.shape[0]
    tm, tn = FFN_TM, FFN_TN

    def body(ug_ref, hg_ref, uv_ref, hv_ref, wg_ref, wv_ref, bg_ref, bv_ref, act_ref, uc_ref):
        first = pl.program_id(1) == 0
        cg = _conv3(ug_ref, hg_ref, wg_ref, bg_ref, first)[0]
        cv = _conv3(uv_ref, hv_ref, wv_ref, bv_ref, first)[0]
        act_ref[...] = (cg * _sigmoid(cg) * cv).astype(BF16)
        uc_ref[0] = cg.astype(BF16)
        uc_ref[1] = cv.astype(BF16)

    return pl.pallas_call(
        body, name=name, grid=(D_FF // tn, s // tm), in_specs=_ffn_specs(tm, tn),
        out_specs=[pl.BlockSpec((tm, tn), lambda j, i: (i, j)), pl.BlockSpec((2, tm, tn), lambda j, i: (0, i, j))],
        out_shape=[jax.ShapeDtypeStruct((s, D_FF), BF16), jax.ShapeDtypeStruct((2, s, D_FF), BF16)],
        compiler_params=_params("parallel", "parallel"))(u, u, u, u, conv_w, conv_w, conv_b, conv_b)


def _swiglu_bwd(uc, dact, *, name):
    _, s, _ = uc.shape
    tm, tn = FFN_TM, FFN_TN

    def body(uc_ref, da_ref, duc_ref, sums_ref):
        i = pl.program_id(1)

        @pl.when(i == 0)
        def _():
            sums_ref[...] = jnp.zeros_like(sums_ref)

        cg, cv, da = uc_ref[0].astype(F32), uc_ref[1].astype(F32), da_ref[...].astype(F32)
        sg = _sigmoid(cg)
        dg = da * cv * (sg * (1.0 + cg * (1.0 - sg)))
        dv = da * (cg * sg)
        duc_ref[0] = dg.astype(BF16)
        duc_ref[1] = dv.astype(BF16)
        sums_ref[0] += _fold8(dg)
        sums_ref[1] += _fold8(dv)

        @pl.when(i == s // tm - 1)
        def _():
            _spread_total(sums_ref)

    pair = pl.BlockSpec((2, tm, tn), lambda j, i: (0, i, j))
    return pl.pallas_call(
        body, name=name, grid=(D_FF // tn, s // tm), in_specs=[pair, pl.BlockSpec((tm, tn), lambda j, i: (i, j))],
        out_specs=[pair, pl.BlockSpec((2, 8, tn), lambda j, i: (0, 0, j))],
        out_shape=[jax.ShapeDtypeStruct((2, s, D_FF), BF16), jax.ShapeDtypeStruct((2, 8, D_FF), F32)],
        compiler_params=_params("parallel", "arbitrary"))(uc, dact)


def _conv_bwd(duc, u, conv_w, *, name):
    _, s, _ = duc.shape
    tm, tn = FFN_TM, FFN_TN
    nj, ni = D_FF // tn, s // tm

    def body(d_ref, halo_ref, u_ref, w_ref, du_ref, sums_ref):
        i = pl.program_id(2)

        @pl.when(i == 0)
        def _():
            sums_ref[...] = jnp.zeros_like(sums_ref)

        d = d_ref[0].astype(F32)
        ext = jnp.concatenate([d, jnp.where(i == ni - 1, 0.0, halo_ref[0].astype(F32))], axis=0)
        n = tm + HALO
        d1 = pltpu.roll(ext, n - 1, 0)[:tm]
        d2 = pltpu.roll(ext, n - 2, 0)[:tm]
        du_ref[...] = (w_ref[2:3, :] * d + w_ref[1:2, :] * d1 + w_ref[0:1, :] * d2).astype(BF16)
        uv = u_ref[...].astype(F32)
        for t, shifted in enumerate((d2, d1, d)):
            sums_ref[0, t] += _fold8(shifted * uv)

        @pl.when(i == ni - 1)
        def _():
            _spread_total(sums_ref)

    return pl.pallas_call(
        body, name=name, grid=(2, nj, ni),
        in_specs=[pl.BlockSpec((1, tm, tn), lambda g, j, i: (g, i, j)),
                  pl.BlockSpec((1, HALO, tn), lambda g, j, i: (g, jnp.minimum((i + 1) * (tm // HALO), s // HALO - 1), j)),
                  pl.BlockSpec((tm, tn), lambda g, j, i: (i, g * nj + j)),
                  pl.BlockSpec((3, tn), lambda g, j, i: (0, g * nj + j))],
        out_specs=[pl.BlockSpec((tm, tn), lambda g, j, i: (i, g * nj + j)),
                   pl.BlockSpec((1, 3, 8, tn), lambda g, j, i: (g, 0, 0, j))],
        out_shape=[jax.ShapeDtypeStruct((s, 2 * D_FF), BF16), jax.ShapeDtypeStruct((2, 3, 8, D_FF), F32)],
        compiler_params=_params("parallel", "parallel", "arbitrary"))(duc, duc, u, conv_w)


def _loss_head(x1, ffn, gate, target, *, name):
    s, d = x1.shape
    tm = ROW_TILE

    def body(x_ref, f_ref, g_ref, t_ref, dy_ref, df_ref, sums_ref):
        i = pl.program_id(0)

        @pl.when(i == 0)
        def _():
            sums_ref[...] = jnp.zeros_like(sums_ref)

        f = f_ref[...]
        err = x_ref[...] + g_ref[...] * f - t_ref[...]
        dy = err * (1.0 / d)
        dy_ref[...] = dy
        df_ref[...] = (g_ref[...] * dy).astype(BF16)
        sums_ref[0] += _fold8(dy * f)
        sums_ref[1] += _fold8(err * err)

        @pl.when(i == s // tm - 1)
        def _():
            _spread_total(sums_ref)

    row = pl.BlockSpec((tm, d), lambda i: (i, 0))
    return pl.pallas_call(
        body, name=name, grid=(s // tm,), in_specs=[row, row, pl.BlockSpec((1, d), lambda i: (0, 0)), row],
        out_specs=[row, row, pl.BlockSpec((2, 8, d), lambda i: (0, 0, 0))],
        out_shape=[jax.ShapeDtypeStruct((s, d), F32), jax.ShapeDtypeStruct((s, d), BF16), jax.ShapeDtypeStruct((2, 8, d), F32)],
        compiler_params=_params("arbitrary"))(x1, ffn, gate, target)


def _adamw(w, g, m, v, *, name):
    rows, cols = w.shape
    if rows % 8 == 0 or rows <= ROW_TILE:
        tm = next((t for t in range(ROW_TILE, 7, -8) if rows % t == 0), rows)
        blk, grid = pl.BlockSpec((tm, cols), lambda i: (i, 0)), (rows // tm,)
    else:
        blk, grid = pl.BlockSpec((rows, ROW_TILE), lambda i: (0, i)), (cols // ROW_TILE,)

    def body(w_ref, g_ref, m_ref, v_ref, d_ref, mo_ref, vo_ref):
        gv = g_ref[...]
        mn = ADAM_B1 * m_ref[...] + (1.0 - ADAM_B1) * gv
        vn = ADAM_B2 * v_ref[...] + (1.0 - ADAM_B2) * (gv * gv)
        m_hat = mn / (1.0 - ADAM_B1 ** ADAM_STEP)
        v_hat = vn / (1.0 - ADAM_B2 ** ADAM_STEP)
        d_ref[...] = -ADAM_LR * (m_hat / (jnp.sqrt(v_hat) + ADAM_EPS) + ADAM_WD * w_ref[...])
        mo_ref[...] = mn
        vo_ref[...] = vn

    return pl.pallas_call(
        body, name=name, grid=grid, in_specs=[blk] * 4, out_specs=[blk] * 3,
        out_shape=[jax.ShapeDtypeStruct((rows, cols), F32)] * 3, compiler_params=_params("parallel"))(w, g, m, v)


def _colsum(t):
    return t[..., 0, :]


def _in_proj_layout(w_in):
    pad = jnp.zeros((w_in.shape[0], PROJ_W - C_LR - GLA_GATE_RANK), w_in.dtype)
    return jnp.concatenate([w_in[:, :1536], w_in[:, 1552:], w_in[:, 1536:1552], pad], axis=1)


def _in_proj_grad_layout(g):
    return jnp.concatenate([g[:, :1536], g[:, C_LR:C_LR + GLA_GATE_RANK], g[:, 1536:C_LR]], axis=1)


def _gate_layout(gla_w_gate):
    return jnp.pad(gla_w_gate, ((0, HEAD_LANES - GLA_GATE_RANK), (0, 0))).astype(BF16)


def _local_step(x, target, mod, wi, wo, ffn_weights, ffn_grads_ready, attn_grads_ready, conv_w, conv_b, wg, bg, gn, qg, kg, n1g, n2g):
    d = D_MODEL
    sh1, sc1, g1, sh2, sc2, g2 = [mod[:, i * d:(i + 1) * d] for i in range(6)]
    qg8, kg8 = jnp.tile(qg, (1, 8)), jnp.tile(kg, (1, 8))

    _, h1, h1_t = _norm_mod_fwd(x, None, None, n1g, sc1, sh1, name="norm1_fwd")
    proj = _mm(h1, wi, tm=1024, tn=PROJ_W, tk=d, name="in_proj")
    o_raw, y_gla, states = _gla_fwd(proj, wg, bg, gn, name="gla_fwd")
    qa, ka = _attn_prep(proj, qg8, kg8, name="attn_prep")
    branches = [_dil_attn_fwd(qa, ka, proj, dil, name=f"attn_fwd_d{dil}") for dil in DILATIONS]
    mixed, y_att, lse = _attn_merge(branches, y_gla, name="attn_merge")
    attn_out = _mm(mixed, wo, tm=1024, tn=d, tk=d, name="out_proj")
    x1, h2, h2_t = _norm_mod_fwd(x, attn_out, g1, n2g, sc2, sh2, name="norm2_fwd")
    wup, wdown = ffn_weights(h2)
    u = _mm(h2, wup, out_dtype=BF16, tm=1024, tn=D_FF, tk=d, name="up_proj")
    act, uc = _conv_swiglu_fwd(u, conv_w, conv_b, name="conv_swiglu_fwd")
    ffn = _mm(act, wdown, tm=1024, tn=d, tk=D_FF, name="down_proj")
    dy, dffn, head_sums = _loss_head(x1, ffn, g2, target, name="loss_head")

    dact = _mm(dffn, wdown, tb=True, out_dtype=BF16, tm=1024, tn=D_FF, tk=d, name="down_proj_dx")
    g_wdown, g_wdown_b = _mm(act, dffn, ta=True, tm=1408, tn=d, tk=2048, also_bf16=True, name="down_proj_dw")
    duc, bias_sums = _swiglu_bwd(uc, dact, name="swiglu_bwd")
    du, tap_sums = _conv_bwd(duc, u, conv_w, name="conv_bwd")
    dh2 = _mm(du, wup, tb=True, tm=1024, tn=d, tk=D_FF, name="up_proj_dx")
    g_wup, g_wup_b = _mm(h2_t, du, tm=d, tn=1408, tk=2048, shard_cols=True, also_bf16=True, name="up_proj_dw")
    token = ffn_grads_ready(g_wup_b, g_wdown_b)
    g1_late = g1 if token is None else g1 + token[0:1, 0:1]
    dx1, dao, n2_sums = _norm_mod_bwd(x1, dh2, dy, n2g, sc2, attn_out, g1_late, name="norm2_bwd")

    dmixed = _mm(dao, wo, tb=True, tm=1024, tn=d, tk=d, name="out_proj_dx")
    g_wo = _mm(mixed, dao, ta=True, tm=d, tn=d, tk=1024, name="out_proj_dw")
    dgq, dgk, dgv, dgr, dlr, g_wg, gla_sums = _gla_bwd(proj, wg, bg, gn, o_raw, states, dmixed, name="gla_bwd")
    parts = [_dil_attn_bwd(qa, ka, proj, y_att, lse, dmixed, dil, name=f"attn_bwd_d{dil}") for dil in DILATIONS]
    daq, dak, dav, qk_sums = _attn_post(parts, proj, qg8, kg8, name="attn_post")
    dproj = jnp.concatenate([dgq, dgk, dgv, dgr, daq, dak, dav, dlr], axis=1)
    g_wi = _mm(h1_t, dproj, tm=512, tn=PROJ_W, tk=2048, name="in_proj_dw")
    token = attn_grads_ready(g_wi, g_wo)
    wi_late = wi if token is None else wi + token[0:1, 0:1].astype(BF16)
    dh1 = _mm(dproj, wi_late, tb=True, tm=1024, tn=d, tk=PROJ_W, name="in_proj_dx")
    grad_x, _, n1_sums = _norm_mod_bwd(x, dh1, dx1, n1g, sc1, None, None, name="norm1_bwd")

    n1, n2, hs, taps, cb = _colsum(n1_sums), _colsum(n2_sums), _colsum(head_sums), _colsum(tap_sums), _colsum(bias_sums)
    gs, qs = _colsum(gla_sums), _colsum(qk_sums)
    dmod = jnp.concatenate([n1[1], n1[0] * n1g[0], n2[2], n2[1], n2[0] * n2g[0], hs[0]])
    small = dict(
        dmod=dmod,
        norm1_g=n1[0] * (1.0 + sc1[0]), norm2_g=n2[0] * (1.0 + sc2[0]),
        gla_w_gate=g_wg[:GLA_GATE_RANK], gla_b_gate=gs[0, :256], gla_norm_g=gs[1].reshape(4, 128).sum(axis=0),
        q_norm_g=qs[0].reshape(8, 64).sum(axis=0), k_norm_g=qs[1].reshape(8, 64).sum(axis=0),
        conv_w=jnp.concatenate([taps[0], taps[1]], axis=1), conv_b=jnp.concatenate([cb[0], cb[1]]),
    )
    return head_sums[1], grad_x, (g_wi, g_wo, g_wup, g_wdown), small


N_DEV, N_CHIP = 8, 4
ANY = pl.BlockSpec(memory_space=pl.ANY)
VMEM_SPEC = pl.BlockSpec(memory_space=pltpu.VMEM)


def _place():
    x, y, c = lax.axis_index("x"), lax.axis_index("y"), lax.axis_index("c")
    other_chips = [(1 - x, y), (x, 1 - y), (1 - x, 1 - y)]
    return x, y, c, (x, y, 1 - c), other_chips


def _all_gather_small(v, *, name):
    m, n = v.shape

    def body(v_ref, out_ref, send_sems, recv_sems, local_sem):
        x, y, c, sibling, chips = _place()
        me = (x, y, c)

        def rows(px, py, pc):
            return out_ref.at[pl.ds((4 * px + 2 * py + pc) * m, m), :]

        def copy(k, block, to, src=None):
            return pltpu.make_async_remote_copy(
                src_ref=rows(*block) if src is None else src, dst_ref=rows(*block), send_sem=send_sems.at[k],
                recv_sem=recv_sems.at[k], device_id=to, device_id_type=MESH)

        mine = pltpu.make_async_copy(v_ref, rows(*me), local_sem)
        mine.start()
        first = [copy(0, me, sibling, src=v_ref)]
        first += [copy(1 + j, me, (*chip, c), src=v_ref) for j, chip in enumerate(chips)]
        for cp in first:
            cp.start()
        passed = [copy(4 + j, (*chip, c), sibling) for j, chip in enumerate(chips)]
        for j, chip in enumerate(chips):
            copy(1 + j, (*chip, c), me).wait_recv()
            passed[j].start()
        copy(0, sibling, me).wait_recv()
        for j, chip in enumerate(chips):
            copy(4 + j, (*chip, 1 - c), me).wait_recv()
        for cp in first + passed:
            cp.wait_send()
        mine.wait()

    return pl.pallas_call(
        body, name=name, out_shape=jax.ShapeDtypeStruct((N_DEV * m, n), v.dtype), in_specs=[VMEM_SPEC], out_specs=VMEM_SPEC,
        scratch_shapes=[pltpu.SemaphoreType.DMA((7,)), pltpu.SemaphoreType.DMA((7,)), pltpu.SemaphoreType.DMA],
    )(v)


def _gather_weight_shards(shards, *, name):
    nw = len(shards)

    def body(*refs):
        srcs, outs, (send_sems, recv_sems) = refs[:nw], refs[nw:2 * nw], refs[2 * nw:]
        x, y, c, sibling, chips = _place()
        index = lambda chip: 2 * chip[0] + chip[1]

        def copy(w, k, src, dst, to):
            return pltpu.make_async_remote_copy(src_ref=src, dst_ref=dst, send_sem=send_sems.at[6 * w + k],
                                                recv_sem=recv_sems.at[6 * w + k], device_id=to, device_id_type=MESH)

        sent = []
        for w, (src_ref, out_ref) in enumerate(zip(srcs, outs)):
            for k, chip in enumerate(chips):
                sent.append(copy(w, k, src_ref.at[c], out_ref.at[2 * x + y, c], (*chip, c)))
                sent[-1].start()
        for w, out_ref in enumerate(outs):
            for k, chip in enumerate(chips):
                landed = out_ref.at[index(chip), c]
                copy(w, k, landed, landed, (*chip, c)).wait_recv()
                sent.append(copy(w, 3 + k, landed, landed, sibling))
                sent[-1].start()
        for w, out_ref in enumerate(outs):
            for k, chip in enumerate(chips):
                passed_on = out_ref.at[index(chip), 1 - c]
                copy(w, 3 + k, passed_on, passed_on, sibling).wait_recv()
        for cp in sent:
            cp.wait_send()

    return pl.pallas_call(
        body, name=name, out_shape=[jax.ShapeDtypeStruct((N_CHIP, *s.shape), s.dtype) for s in shards],
        in_specs=[ANY] * nw, out_specs=[ANY] * nw,
        scratch_shapes=[pltpu.SemaphoreType.DMA((6 * nw,)), pltpu.SemaphoreType.DMA((6 * nw,))],
    )(*shards)


HBM_SPEC = pl.BlockSpec(memory_space=pltpu.HBM)
SEM_SPEC = pl.BlockSpec(memory_space=pltpu.SEMAPHORE)
DATAFLOW_EFFECT = pltpu.SideEffectType.DATAFLOW_SIDE_EFFECTING


def _late_copies(srcs, lands, send_sems, recv_sems):
    x, y, c, _, chips = _place()
    return [pltpu.make_async_remote_copy(
        src_ref=src.at[c], dst_ref=land.at[2 * x + y, c], send_sem=send_sems.at[6 * w + 2 * r + core],
        recv_sem=recv_sems.at[6 * w + 2 * r + c], device_id=(*chip, core), device_id_type=MESH)
        for w, (src, land) in enumerate(zip(srcs, lands)) for r, chip in enumerate(chips) for core in range(2)]


def _gather_late_start(own, after, *, name):
    nw = len(own)

    def body(*refs):
        srcs, lands, send_sems, recv_sems, token = refs[:nw], refs[nw:2 * nw], refs[2 * nw + 1], refs[2 * nw + 2], refs[-1]
        for cp in _late_copies(srcs, lands, send_sems, recv_sems):
            cp.start()
        token[...] = jnp.zeros_like(token)

    lands = [pltpu.with_memory_space_constraint(lax.empty((N_CHIP, *s.shape), s.dtype), pltpu.HBM) for s in own]
    own = [pltpu.with_memory_space_constraint(s, pltpu.HBM) for s in own]
    out = pl.pallas_call(
        body, name=name,
        out_shape=(pltpu.SemaphoreType.DMA((6 * nw,)), pltpu.SemaphoreType.DMA((6 * nw,)),
                   *[pltpu.HBM(s.shape, s.dtype) for s in own], *[pltpu.HBM(s.shape, s.dtype) for s in lands],
                   jax.ShapeDtypeStruct((8, 128), F32)),
        in_specs=[HBM_SPEC] * (2 * nw) + [ANY], out_specs=(SEM_SPEC, SEM_SPEC, *[HBM_SPEC] * (2 * nw), VMEM_SPEC),
        input_output_aliases={i: 2 + i for i in range(2 * nw)},
        compiler_params=pltpu.CompilerParams(has_side_effects=DATAFLOW_EFFECT))(*own, *lands, after)
    return out[0], out[1], out[2:2 + nw], out[2 + nw:2 + 2 * nw], out[-1]


def _gather_late_wait(send_sems, recv_sems, own, lands, after, *, name):
    nw = len(own)

    def body(*refs):
        srcs, lands_in, send_sems, recv_sems = refs[:nw], refs[nw:2 * nw], refs[2 * nw], refs[2 * nw + 1]
        x, y, c, _, chips = _place()
        for cp in _late_copies(srcs, lands_in, send_sems, recv_sems):
            cp.wait_send()
        for w, (src, land) in enumerate(zip(srcs, lands_in)):
            for r, chip in enumerate(chips):
                for core in range(2):
                    pltpu.make_async_remote_copy(
                        src_ref=src.at[c], dst_ref=land.at[2 * chip[0] + chip[1], core], send_sem=send_sems.at[6 * w + 2 * r + core],
                        recv_sem=recv_sems.at[6 * w + 2 * r + core], device_id=(*chip, core), device_id_type=MESH).wait_recv()

    out = pl.pallas_call(
        body, name=name, out_shape=(*[pltpu.HBM(s.shape, s.dtype) for s in own], *[pltpu.HBM(s.shape, s.dtype) for s in lands]),
        in_specs=[HBM_SPEC] * (2 * nw) + [SEM_SPEC, SEM_SPEC, ANY], out_specs=tuple([HBM_SPEC] * (2 * nw)),
        input_output_aliases={i: i for i in range(2 * nw)},
        compiler_params=pltpu.CompilerParams(has_side_effects=DATAFLOW_EFFECT))(*own, *lands, send_sems, recv_sems, after)
    return out[:nw], out[nw:]


def _direct_reduce_copies(srcs, lands, send_sems, recv_sems):
    x, y, c, _, _ = _place()
    cps = []
    for w, (src, land) in enumerate(zip(srcs, lands)):
        for rel in range(1, N_DEV):
            tx, ty, tc = (1 - x if rel & 4 else x), (1 - y if rel & 2 else y), (1 - c if rel & 1 else c)
            cps.append(pltpu.make_async_remote_copy(
                src_ref=src.at[2 * tx + ty, tc], dst_ref=land.at[rel - 1], send_sem=send_sems.at[7 * w + rel - 1],
                recv_sem=recv_sems.at[7 * w + rel - 1], device_id=(tx, ty, tc), device_id_type=MESH))
    return cps


def _direct_reduce_start(grads, *, name):
    nw = len(grads)

    def body(*refs):
        srcs, lands, send_sems, recv_sems, token = refs[:nw], refs[nw:2 * nw], refs[2 * nw], refs[2 * nw + 1], refs[-1]
        for cp in _direct_reduce_copies(srcs, lands, send_sems, recv_sems):
            cp.start()
        token[...] = jnp.zeros_like(token)

    lands = [pltpu.with_memory_space_constraint(lax.empty((N_DEV - 1, *g.shape[2:]), g.dtype), pltpu.HBM) for g in grads]
    grads = [pltpu.with_memory_space_constraint(g, pltpu.HBM) for g in grads]
    out = pl.pallas_call(
        body, name=name,
        out_shape=(pltpu.SemaphoreType.DMA((7 * nw,)), pltpu.SemaphoreType.DMA((7 * nw,)),
                   *[pltpu.HBM(g.shape, g.dtype) for g in grads], *[pltpu.HBM(t.shape, t.dtype) for t in lands],
                   jax.ShapeDtypeStruct((8, 128), F32)),
        in_specs=[HBM_SPEC] * (2 * nw), out_specs=(SEM_SPEC, SEM_SPEC, *[HBM_SPEC] * (2 * nw), VMEM_SPEC),
        input_output_aliases={i: 2 + i for i in range(2 * nw)},
        compiler_params=pltpu.CompilerParams(has_side_effects=DATAFLOW_EFFECT))(*grads, *lands)
    return out[0], out[1], out[2:2 + nw], out[2 + nw:2 + 2 * nw], out[-1]


def _direct_reduce_wait(send_sems, recv_sems, grads, lands, after, *, name):
    nw = len(grads)

    def body(*refs):
        srcs, lands_in, send_sems, recv_sems = refs[:nw], refs[nw:2 * nw], refs[2 * nw], refs[2 * nw + 1]
        cps = _direct_reduce_copies(srcs, lands_in, send_sems, recv_sems)
        for cp in cps:
            cp.wait_send()
        for cp in cps:
            cp.wait_recv()

    out = pl.pallas_call(
        body, name=name, out_shape=(*[pltpu.HBM(g.shape, g.dtype) for g in grads], *[pltpu.HBM(t.shape, t.dtype) for t in lands]),
        in_specs=[HBM_SPEC] * (2 * nw) + [SEM_SPEC, SEM_SPEC, ANY], out_specs=tuple([HBM_SPEC] * (2 * nw)),
        input_output_aliases={i: i for i in range(2 * nw)},
        compiler_params=pltpu.CompilerParams(has_side_effects=DATAFLOW_EFFECT))(*grads, *lands, send_sems, recv_sems, after)
    return out[nw:]


def _direct_reduce_add(grad, landed, chip, core, *, name):
    _, r, n = grad.shape
    half = r // 2
    tr = _row_tile(half)
    nb = half // tr

    def body(chip_ref, core_ref, g_ref, t_ref, o_ref):
        acc = g_ref[0]
        for k in range(N_DEV - 1):
            acc = acc + t_ref[k].astype(F32)
        o_ref[...] = acc

    return pl.pallas_call(
        body, name=name,
        grid_spec=pltpu.PrefetchScalarGridSpec(
            num_scalar_prefetch=2, grid=(nb,),
            in_specs=[pl.BlockSpec((1, tr, n), lambda i, chip_ref, core_ref: (chip_ref[0], core_ref[0] * nb + i, 0)),
                      pl.BlockSpec((N_DEV - 1, tr, n), lambda i, chip_ref, core_ref: (0, i, 0))],
            out_specs=pl.BlockSpec((tr, n), lambda i, chip_ref, core_ref: (i, 0))),
        out_shape=jax.ShapeDtypeStruct((half, n), F32), compiler_params=_params("parallel"))(chip, core, grad, landed)


def _share_halves(halves, *, name):
    nw = len(halves)

    def body(*refs):
        srcs, outs, (send_sems, recv_sems) = refs[:nw], refs[nw:2 * nw], refs[2 * nw:]
        _, _, _, sibling, _ = _place()
        cps = [pltpu.make_async_remote_copy(src_ref=src_ref, dst_ref=out_ref, send_sem=send_sems.at[w], recv_sem=recv_sems.at[w],
                                            device_id=sibling, device_id_type=MESH)
               for w, (src_ref, out_ref) in enumerate(zip(srcs, outs))]
        for cp in cps:
            cp.start()
        for cp in cps:
            cp.wait()

    return pl.pallas_call(
        body, name=name, out_shape=[jax.ShapeDtypeStruct(h.shape, h.dtype) for h in halves],
        in_specs=[ANY] * nw, out_specs=[ANY] * nw,
        scratch_shapes=[pltpu.SemaphoreType.DMA((nw,)), pltpu.SemaphoreType.DMA((nw,))])(*halves)


def _row_tile(rows, limit=256):
    return next(t for t in range(limit, 15, -16) if rows % t == 0)


def _sum_devices(gathered, *, name):
    _, m, n = gathered.shape

    def body(g_ref, tot_ref, loss_ref):
        tot = g_ref[0]
        for dev in range(1, N_DEV):
            tot = tot + g_ref[dev]
        tot_ref[...] = tot
        loss_ref[...] = jnp.full((8, n), (0.5 / D_MODEL) * jnp.sum(tot[0:8]), F32)

    return pl.pallas_call(body, name=name, in_specs=[VMEM_SPEC], out_specs=[VMEM_SPEC, VMEM_SPEC],
                          out_shape=[jax.ShapeDtypeStruct((m, n), F32), jax.ShapeDtypeStruct((8, n), F32)])(gathered)


def _ada_mod(cond_all, w_ada_shard, *, name):
    tn = 512

    def body(a_ref, b_ref, o_ref):
        o_ref[...] = _nn(a_ref[...], b_ref[...], precision=HIGHEST)

    return pl.pallas_call(
        body, name=name, grid=(w_ada_shard.shape[1] // tn,),
        in_specs=[pl.BlockSpec(cond_all.shape, lambda j: (0, 0)), pl.BlockSpec((D_MODEL, tn), lambda j: (0, j))],
        out_specs=pl.BlockSpec((N_DEV, tn), lambda j: (0, j)),
        out_shape=jax.ShapeDtypeStruct((N_DEV, w_ada_shard.shape[1]), F32), compiler_params=_params("parallel"))(cond_all, w_ada_shard)


def _ada_grad(cond_all, dmod_cols, *, name):
    tm = 256

    def body(a_ref, b_ref, o_ref):
        o_ref[...] = lax.dot_general(a_ref[...], b_ref[...], (((0,), (0,)), ((), ())), precision=HIGHEST,
                                     preferred_element_type=F32)

    return pl.pallas_call(
        body, name=name, grid=(D_MODEL // tm,),
        in_specs=[pl.BlockSpec((N_DEV, tm), lambda i: (0, i)), pl.BlockSpec(dmod_cols.shape, lambda i: (0, 0))],
        out_specs=pl.BlockSpec((tm, dmod_cols.shape[1]), lambda i: (i, 0)),
        out_shape=jax.ShapeDtypeStruct((D_MODEL, dmod_cols.shape[1]), F32), compiler_params=_params("parallel"))(cond_all, dmod_cols)


def _silu_rows(c8, *, name):
    def body(c_ref, o_ref):
        cv = c_ref[...]
        o_ref[...] = cv * _sigmoid(cv)

    return pl.pallas_call(body, name=name, in_specs=[VMEM_SPEC], out_specs=VMEM_SPEC,
                          out_shape=jax.ShapeDtypeStruct(c8.shape, F32))(c8)


def _rows128(t, rows=None):
    flat = t.reshape(-1, 128)
    return flat if rows is None else jnp.pad(flat, ((0, rows - flat.shape[0]), (0, 0)))


def _from_col_shards(shards, r, n):
    return shards.reshape(N_CHIP, r, n).transpose(1, 0, 2).reshape(r, N_CHIP * n)


def kernel(x, c, w_ada, b_ada, norm1_g, w_in, gla_w_gate, gla_b_gate, gla_norm_g, q_norm_g, k_norm_g, w_out, norm2_g, w_up, conv_w, conv_b, w_down, loss_target, m_w_ada, m_b_ada, m_norm1_g, m_w_in, m_gla_w_gate, m_gla_b_gate, m_gla_norm_g, m_q_norm_g, m_k_norm_g, m_w_out, m_norm2_g, m_w_up, m_conv_w, m_conv_b, m_w_down, v_w_ada, v_b_ada, v_norm1_g, v_w_in, v_gla_w_gate, v_gla_b_gate, v_gla_norm_g, v_q_norm_g, v_k_norm_g, v_w_out, v_norm2_g, v_w_up, v_conv_w, v_conv_b, v_w_down):
    d = D_MODEL
    ax, ay, ac = lax.axis_index("x"), lax.axis_index("y"), lax.axis_index("c")
    chip, dev = 2 * ax + ay, 4 * ax + 2 * ay + ac

    cond = _silu_rows(jnp.broadcast_to(c, (8, d)), name="cond_silu")[0:1]
    small_in = jnp.concatenate([_rows128(cond), _rows128(conv_w[0]), _rows128(gla_w_gate[0])], axis=0)
    small_in = _rows128(small_in, 56)
    got = _all_gather_small(small_in, name="gather_small").reshape(N_DEV, 56, 128)
    cond_all = got[:, 0:8].reshape(N_DEV, d)
    conv_w_full = _from_col_shards(got[0::2, 8:41].reshape(N_CHIP, 3 * 1408 // 128, 128), 3, 1408)
    gate_full = _from_col_shards(got[0::2, 41:49].reshape(N_CHIP, 16 * 64 // 128, 128), GLA_GATE_RANK, 64)
    mod_part = _ada_mod(cond_all, w_ada[0], name="ada_mod")
    mod_got = _all_gather_small(_rows128(mod_part), name="gather_mod").reshape(N_DEV, N_DEV, 1536)
    mod_all = mod_got[0::2].transpose(1, 0, 2).reshape(N_DEV, 6 * d) + b_ada
    mod = lax.dynamic_slice_in_dim(mod_all, dev, 1, axis=0)

    own = [w[0].astype(BF16).reshape(2, w.shape[1] // 2, w.shape[2]) for w in (w_in, w_out, w_up, w_down)]
    with_own = lambda got, mine: [lax.dynamic_update_index_in_dim(t, o, chip, 0) for t, o in zip(got, mine)]
    got_in, got_out = with_own(_gather_weight_shards(own[:2], name="gather_weights"), own[:2])
    w_in_full = got_in.reshape(N_CHIP, d, 772).transpose(1, 0, 2).reshape(d, N_CHIP * 772)
    w_out_full = got_out.reshape(d, d)
    exchanged = mod_all[0:1, 0:1] + got_in[0, 0, 0:1, 0:1].astype(F32)
    send_sems, recv_sems, own_thru, lands, token = _gather_late_start(own[2:], exchanged, name="gather_late_start")
    mod = mod + token[0:1, 0:1]

    def ffn_weights(after):
        mine, landed = _gather_late_wait(send_sems, recv_sems, own_thru, lands, after, name="gather_late_wait")
        got_up, got_down = with_own(landed, mine)
        return got_up.reshape(N_CHIP, d, 1408).transpose(1, 0, 2).reshape(d, 2 * D_FF), got_down.reshape(D_FF, d)

    ffn_reduce, attn_reduce, attn_parts = [], [], []
    halves_of = lambda g: g.reshape(N_CHIP, 2, g.shape[-2] // 2, g.shape[-1])

    def ffn_grads_ready(g_wup_b, g_wdown_b):
        ffn_reduce.extend(_direct_reduce_start([halves_of(g_wup_b), halves_of(g_wdown_b.reshape(N_CHIP, D_FF // N_CHIP, d))],
                                               name="reduce_ffn_start"))
        return ffn_reduce[4]

    def attn_grads_ready(g_wi, g_wo):
        attn_parts.extend([_in_proj_grad_layout(g_wi).reshape(d, N_CHIP, 772).transpose(1, 0, 2), g_wo.reshape(N_CHIP, d // N_CHIP, d)])
        attn_reduce.extend(_direct_reduce_start([halves_of(g.astype(BF16)) for g in attn_parts], name="reduce_attn_start"))
        return attn_reduce[4]

    err2, grad_x, (g_wi, g_wo, g_wup, g_wdown), small = _local_step(
        x[0], loss_target[0], mod, _in_proj_layout(w_in_full), w_out_full, ffn_weights, ffn_grads_ready, attn_grads_ready,
        conv_w_full, conv_b,
        _gate_layout(gate_full), gla_b_gate, gla_norm_g, q_norm_g, k_norm_g, norm1_g, norm2_g)

    pieces = [err2[0], small["dmod"], small["norm1_g"], small["norm2_g"], small["gla_w_gate"].reshape(-1), small["gla_b_gate"],
              small["gla_norm_g"], small["q_norm_g"], small["k_norm_g"], small["conv_w"].reshape(-1), small["conv_b"]]
    sizes = [p.shape[0] for p in pieces]
    at = [sum(sizes[:i]) for i in range(len(sizes) + 1)]
    vec = _rows128(jnp.concatenate(pieces), 288)
    got = _all_gather_small(vec, name="gather_grads").reshape(N_DEV, 288, 128)
    total, loss8 = _sum_devices(got, name="sum_devices")
    total = total.reshape(-1)
    seg = lambda i: total[at[i]:at[i + 1]]
    dmod_all = got.reshape(N_DEV, -1)[:, at[1]:at[2]]
    g_small = dict(
        b_ada=seg(1)[None], norm1_g=seg(2)[None], norm2_g=seg(3)[None],
        gla_w_gate=lax.dynamic_slice_in_dim(seg(4).reshape(GLA_GATE_RANK, 256), chip * 64, 64, axis=1),
        gla_b_gate=seg(5)[None], gla_norm_g=seg(6)[None], q_norm_g=seg(7)[None], k_norm_g=seg(8)[None],
        conv_w=lax.dynamic_slice_in_dim(seg(9).reshape(3, 2 * D_FF), chip * 1408, 1408, axis=1), conv_b=seg(10)[None])
    dmod_cols = lax.dynamic_slice_in_dim(dmod_all.reshape(N_DEV, 6 * d), chip * 1536, 1536, axis=1)
    g_w_ada = _ada_grad(cond_all, dmod_cols, name="ada_grad")

    core_id, chip_id = jnp.reshape(ac, (1,)).astype(jnp.int32), jnp.reshape(chip, (1,)).astype(jnp.int32)
    landed = (_direct_reduce_wait(*attn_reduce[:4], grad_x, name="reduce_attn_wait")
              + _direct_reduce_wait(*ffn_reduce[:4], grad_x, name="reduce_ffn_wait"))
    own = attn_parts + [g_wup, g_wdown.reshape(N_CHIP, D_FF // N_CHIP, d)]
    summed = [_direct_reduce_add(g, t, chip_id, core_id, name=f"reduce_add_{tag}")
              for g, t, tag in zip(own, landed, ("w_in", "w_out", "w_up", "w_down"))]
    others = _share_halves(summed, name="share_pair")
    g_big = [jnp.concatenate([jnp.where(ac == 0, mine, other), jnp.where(ac == 0, other, mine)], axis=0)
             for mine, other in zip(summed, others)]

    grads = dict(w_ada=g_w_ada, w_in=g_big[0], w_out=g_big[1], w_up=g_big[2], w_down=g_big[3], **g_small)
    names = ["w_ada", "b_ada", "norm1_g", "w_in", "gla_w_gate", "gla_b_gate", "gla_norm_g", "q_norm_g", "k_norm_g", "w_out",
             "norm2_g", "w_up", "conv_w", "conv_b", "w_down"]
    ws = dict(w_ada=w_ada, b_ada=b_ada, norm1_g=norm1_g, w_in=w_in, gla_w_gate=gla_w_gate, gla_b_gate=gla_b_gate,
              gla_norm_g=gla_norm_g, q_norm_g=q_norm_g, k_norm_g=k_norm_g, w_out=w_out, norm2_g=norm2_g, w_up=w_up,
              conv_w=conv_w, conv_b=conv_b, w_down=w_down)
    ms = dict(w_ada=m_w_ada, b_ada=m_b_ada, norm1_g=m_norm1_g, w_in=m_w_in, gla_w_gate=m_gla_w_gate, gla_b_gate=m_gla_b_gate,
              gla_norm_g=m_gla_norm_g, q_norm_g=m_q_norm_g, k_norm_g=m_k_norm_g, w_out=m_w_out, norm2_g=m_norm2_g, w_up=m_w_up,
              conv_w=m_conv_w, conv_b=m_conv_b, w_down=m_w_down)
    vs = dict(w_ada=v_w_ada, b_ada=v_b_ada, norm1_g=v_norm1_g, w_in=v_w_in, gla_w_gate=v_gla_w_gate, gla_b_gate=v_gla_b_gate,
              gla_norm_g=v_gla_norm_g, q_norm_g=v_q_norm_g, k_norm_g=v_k_norm_g, w_out=v_w_out, norm2_g=v_norm2_g, w_up=v_w_up,
              conv_w=v_conv_w, conv_b=v_conv_b, w_down=v_w_down)
    g_out, d_out, m_out, v_out = [], [], [], []
    for nm in names:
        shape = ws[nm].shape
        flip = (lambda t: t.T) if shape[-1] % 128 and shape[-2] % 128 == 0 else (lambda t: t)
        w2 = flip(ws[nm].reshape(shape[-2:]))
        g2 = flip(grads[nm].reshape(shape[-2:]))
        dl, mn, vn = _adamw(w2, g2, flip(ms[nm].reshape(shape[-2:])), flip(vs[nm].reshape(shape[-2:])), name=f"adamw_{nm}")
        for outs, t in ((g_out, g2), (d_out, dl), (m_out, mn), (v_out, vn)):
            outs.append(flip(t).reshape(shape))
    return (loss8[0, 0], grad_x[None], *g_out, *d_out, *m_out, *v_out)
```

```python
import functools

import jax
import jax.numpy as jnp
from jax import lax
from jax.experimental import pallas as pl
from jax.experimental.pallas import tpu as pltpu

F32, BF16 = jnp.float32, jnp.bfloat16
HIGHEST = lax.Precision.HIGHEST
MESH = pl.DeviceIdType.MESH

D_MODEL = 1024
GLA_CHUNK = 64
GLA_GATE_TAU = 16.0
GLA_GATE_RANK = 16
HEAD_LANES = 128
ATTN_BLOCK = 128
DILATIONS = (1, 4, 16)
ALIBI_SLOPES = tuple(2.0 ** (-(h + 1)) for h in range(8))
D_FF = 2816
EPS = 1e-6
C_GQ, C_GK, C_GV, C_GR, C_AQ, C_AK, C_AV, C_LR, PROJ_W = 0, 256, 512, 1024, 1536, 2048, 2560, 3072, 3200
ADAM_LR, ADAM_B1, ADAM_B2, ADAM_EPS, ADAM_WD, ADAM_STEP = 0.001, 0.9, 0.999, 1e-08, 0.01, 10
VMEM_LIMIT_BYTES = 56 * 1024 * 1024
ROW_TILE = 256


def _params(*sem):
    return pltpu.CompilerParams(dimension_semantics=sem or None, vmem_limit_bytes=VMEM_LIMIT_BYTES)


def _nt(a, b):
    return lax.dot_general(a, b, (((1,), (1,)), ((), ())), preferred_element_type=F32)


def _tn(a, b):
    return lax.dot_general(a, b, (((0,), (0,)), ((), ())), preferred_element_type=F32)


def _nn(a, b, precision=None):
    return jnp.dot(a, b, preferred_element_type=F32, precision=precision)


def _split3(v):
    hi = v.astype(BF16)
    rest = v - hi.astype(F32)
    mid = rest.astype(BF16)
    return hi, mid, (rest - mid.astype(F32)).astype(BF16)


def _sum_right(v, ones):
    hi, mid, lo = _split3(v)
    return (_nn(lo, ones) + _nn(mid, ones)) + _nn(hi, ones)


def _sum_left(ones, v):
    hi, mid, lo = _split3(v)
    return (_nn(ones, lo) + _nn(ones, mid)) + _nn(ones, hi)


def _fold8(v):
    return v.reshape(v.shape[0] // 8, 8, v.shape[1]).sum(axis=0)


def _spread_total(ref):
    t = ref[...]
    ref[...] = jnp.broadcast_to(jnp.sum(t, axis=-2, keepdims=True), t.shape)


def _sigmoid(x):
    return 1.0 / (1.0 + jnp.exp(-x))


def _mm(a, b, *, ta=False, tb=False, out_dtype=F32, tm, tn, tk, shard_cols=False, also_bf16=False, name):
    (k_a, m) = a.shape if ta else a.shape[::-1]
    (k_b, n) = b.shape[::-1] if tb else b.shape
    assert k_a == k_b and m % tm == 0 and n % tn == 0 and k_a % tk == 0, (name, a.shape, b.shape)
    nk = k_a // tk
    assert nk == 1 or out_dtype == F32, name
    dims = (((0 if ta else 1,), (1 if tb else 0,)), ((), ()))

    def body(a_ref, b_ref, o_ref, *rounded):
        k = pl.program_id(2)
        part = lax.dot_general(a_ref[...].astype(BF16), b_ref[...].astype(BF16), dims, preferred_element_type=F32)
        if nk == 1:
            o_ref[...] = part.astype(out_dtype)
        else:
            @pl.when(k == 0)
            def _():
                o_ref[...] = part

            @pl.when(k > 0)
            def _():
                o_ref[...] += part

        if also_bf16:
            @pl.when(k == nk - 1)
            def _():
                rounded[0][...] = o_ref[...].astype(BF16)

    a_spec = pl.BlockSpec((tk, tm), lambda i, j, k: (k, i)) if ta else pl.BlockSpec((tm, tk), lambda i, j, k: (i, k))
    b_spec = pl.BlockSpec((tn, tk), lambda i, j, k: (j, k)) if tb else pl.BlockSpec((tk, tn), lambda i, j, k: (k, j))
    if shard_cols:
        o_spec, o_shape = pl.BlockSpec((None, tm, tn), lambda i, j, k: (j, i, 0)), (n // tn, m, tn)
    else:
        o_spec, o_shape = pl.BlockSpec((tm, tn), lambda i, j, k: (i, j)), (m, n)
    shapes = [jax.ShapeDtypeStruct(o_shape, out_dtype)] + ([jax.ShapeDtypeStruct(o_shape, BF16)] if also_bf16 else [])
    out = pl.pallas_call(
        body, name=name, grid=(m // tm, n // tn, nk), in_specs=[a_spec, b_spec], out_specs=[o_spec] * len(shapes),
        out_shape=shapes, compiler_params=_params("parallel", "parallel", "arbitrary"))(a, b)
    return out if also_bf16 else out[0]


def _norm_mod_fwd(x, branch, gate, gain, scale, shift, *, name):
    s, d = x.shape
    tm = ROW_TILE
    has_branch = branch is not None

    def body(*refs):
        if has_branch:
            x_ref, br_ref, gate_ref, gain_ref, sc_ref, sh_ref, x1_ref, h_ref, ht_ref = refs
            xv = x_ref[...] + gate_ref[...] * br_ref[...]
            x1_ref[...] = xv
        else:
            x_ref, gain_ref, sc_ref, sh_ref, h_ref, ht_ref = refs
            xv = x_ref[...]
        r = lax.rsqrt(jnp.mean(xv * xv, axis=-1, keepdims=True) + EPS)
        h = (xv * r) * gain_ref[...] * (1.0 + sc_ref[...]) + sh_ref[...]
        h_ref[...] = h.astype(BF16)
        ht_ref[...] = h.T.astype(BF16)

    row = pl.BlockSpec((tm, d), lambda i: (i, 0))
    col = pl.BlockSpec((d, tm), lambda i: (0, i))
    vec = pl.BlockSpec((1, d), lambda i: (0, 0))
    h_shapes = [jax.ShapeDtypeStruct((s, d), BF16), jax.ShapeDtypeStruct((d, s), BF16)]
    if has_branch:
        return pl.pallas_call(
            body, name=name, grid=(s // tm,), in_specs=[row, row, vec, vec, vec, vec], out_specs=[row, row, col],
            out_shape=[jax.ShapeDtypeStruct((s, d), F32)] + h_shapes,
            compiler_params=_params("parallel"))(x, branch, gate, gain, scale, shift)
    h, ht = pl.pallas_call(
        body, name=name, grid=(s // tm,), in_specs=[row, vec, vec, vec], out_specs=[row, col],
        out_shape=h_shapes, compiler_params=_params("parallel"))(x, gain, scale, shift)
    return x, h, ht


def _norm_mod_bwd(x, dh, dres, gain, scale, branch, gate, *, name):
    s, d = x.shape
    tm = ROW_TILE
    has_branch = branch is not None

    def body(*refs):
        if has_branch:
            x_ref, dh_ref, dres_ref, gain_ref, sc_ref, br_ref, gate_ref, dx_ref, dbr_ref, sums_ref = refs
        else:
            x_ref, dh_ref, dres_ref, gain_ref, sc_ref, dx_ref, sums_ref = refs
        i = pl.program_id(0)

        @pl.when(i == 0)
        def _():
            sums_ref[...] = jnp.zeros_like(sums_ref)

        xv, dhv = x_ref[...], dh_ref[...]
        r = lax.rsqrt(jnp.mean(xv * xv, axis=-1, keepdims=True) + EPS)
        xn = xv * r
        dxn = dhv * (gain_ref[...] * (1.0 + sc_ref[...]))
        dx = dres_ref[...] + r * (dxn - xn * jnp.mean(dxn * xn, axis=-1, keepdims=True))
        dx_ref[...] = dx
        sums_ref[0] += _fold8(dhv * xn)
        sums_ref[1] += _fold8(dhv)
        if has_branch:
            dbr_ref[...] = (gate_ref[...] * dx).astype(BF16)
            sums_ref[2] += _fold8(dx * br_ref[...])

        @pl.when(i == s // tm - 1)
        def _():
            _spread_total(sums_ref)

    row = pl.BlockSpec((tm, d), lambda i: (i, 0))
    vec = pl.BlockSpec((1, d), lambda i: (0, 0))
    sums = pl.BlockSpec((3, 8, d), lambda i: (0, 0, 0))
    sums_shape = jax.ShapeDtypeStruct((3, 8, d), F32)
    if has_branch:
        return pl.pallas_call(
            body, name=name, grid=(s // tm,), in_specs=[row, row, row, vec, vec, row, vec], out_specs=[row, row, sums],
            out_shape=[jax.ShapeDtypeStruct((s, d), F32), jax.ShapeDtypeStruct((s, d), BF16), sums_shape],
            compiler_params=_params("arbitrary"))(x, dh, dres, gain, scale, branch, gate)
    dx, sm = pl.pallas_call(
        body, name=name, grid=(s // tm,), in_specs=[row, row, row, vec, vec], out_specs=[row, sums],
        out_shape=[jax.ShapeDtypeStruct((s, d), F32), sums_shape],
        compiler_params=_params("arbitrary"))(x, dh, dres, gain, scale)
    return dx, None, sm


GLA_ROWS = 256


def _gla_block_setup(lr_ref, wg_ref, bg_ref):
    t, c = GLA_ROWS, GLA_CHUNK
    ri = lax.broadcasted_iota(jnp.int32, (t, t), 0)
    ci = lax.broadcasted_iota(jnp.int32, (t, t), 1)
    same = (ri // c) == (ci // c)
    causal, upper = same & (ci <= ri), same & (ci >= ri)
    z = _nn(lr_ref[...].astype(BF16), wg_ref[...]) + bg_ref[...]
    g = (jnp.minimum(z, 0.0) - jnp.log(1.0 + jnp.exp(-jnp.abs(z)))) * (1.0 / GLA_GATE_TAU)
    hi, mid, lo = _split3(g)
    total = lambda ones: (_nn(ones, lo) + _nn(ones, mid)) + _nn(ones, hi)
    return z, total(causal.astype(BF16)), total(same.astype(BF16)), causal, upper


def _chunks(t):
    return [t[i * GLA_CHUNK:(i + 1) * GLA_CHUNK] for i in range(GLA_ROWS // GLA_CHUNK)]


def _gla_fwd(proj, wg, bg, gn, *, name):
    s = proj.shape[0]
    tb, c = GLA_ROWS, GLA_CHUNK
    cb = tb // c

    def body(q_ref, k_ref, v_ref, r_ref, lr_ref, wg_ref, bg_ref, gn_ref, o_ref, y_ref, st_ref, state):
        i = pl.program_id(0)

        @pl.when(i == 0)
        def _():
            state[...] = jnp.zeros_like(state)

        low = lax.broadcasted_iota(jnp.int32, (tb, HEAD_LANES), 1) < 64
        masks = (low, jnp.logical_not(low))
        _, b, b_end, causal, _ = _gla_block_setup(lr_ref, wg_ref, bg_ref)
        pairs = []
        for p in range(2):
            cols = pl.ds(p * HEAD_LANES, HEAD_LANES)
            bp, bep = (t[:, p * HEAD_LANES:(p + 1) * HEAD_LANES] for t in (b, b_end))
            k = k_ref[:, cols]
            q_in = q_ref[:, cols] * 0.125 * jnp.exp(bp)
            k_out = (k * jnp.exp(-bp)).astype(BF16)
            k_end = k * jnp.exp(bep - bp)
            qms = [jnp.where(m, q_in, 0.0).astype(BF16) for m in masks]
            kes = [jnp.where(m, k_end, 0.0).astype(BF16) for m in masks]
            vs = [v_ref[:, pl.ds((2 * p + e) * HEAD_LANES, HEAD_LANES)].astype(BF16) for e in range(2)]
            grow = [_tn(v0, k0) + _tn(v1, k1) for v0, k0, v1, k1 in zip(_chunks(vs[0]), _chunks(kes[0]), _chunks(vs[1]), _chunks(kes[1]))]
            pairs.append((bep, k_out, qms, vs, grow))
        entering = [[], []]
        for p, (bep, _, _, _, grow) in enumerate(pairs):
            st = state[p]
            for ch in range(cb):
                entering[p].append(st)
                st_ref[ch, p] = st
                st = st * jnp.exp(bep[ch * c:ch * c + 1, :]) + grow[ch]
            state[p] = st
        for p, (_, k_out, qms, vs, _) in enumerate(pairs):
            for e in range(2):
                hc = pl.ds((2 * p + e) * HEAD_LANES, HEAD_LANES)
                a = jnp.where(causal, _nt(qms[e], k_out), 0.0).astype(BF16)
                carried = jnp.concatenate([_nt(qc, sc.astype(BF16)) for qc, sc in zip(_chunks(qms[e]), entering[p])], axis=0)
                o = _nn(a, vs[e]) + carried
                o_ref[:, hc] = o
                rr = r_ref[:, hc]
                on = o * lax.rsqrt(jnp.mean(o * o, axis=-1, keepdims=True) + EPS)
                y_ref[:, hc] = (on * gn_ref[...] * (rr * _sigmoid(rr))).astype(BF16)

    def col(width, at):
        return pl.BlockSpec((tb, width), lambda i: (i, at // width))

    full = lambda shape: pl.BlockSpec(shape, lambda i: tuple(0 for _ in shape))
    return pl.pallas_call(
        body, name=name, grid=(s // tb,),
        in_specs=[col(256, C_GQ), col(256, C_GK), col(512, C_GV), col(512, C_GR), col(128, C_LR),
                  full((HEAD_LANES, 256)), full((1, 256)), full((1, HEAD_LANES))],
        out_specs=[pl.BlockSpec((tb, 512), lambda i: (i, 0)), pl.BlockSpec((tb, 512), lambda i: (i, 0)),
                   pl.BlockSpec((cb, 2, HEAD_LANES, HEAD_LANES), lambda i: (i, 0, 0, 0))],
        out_shape=[jax.ShapeDtypeStruct((s, 512), F32), jax.ShapeDtypeStruct((s, 512), BF16),
                   jax.ShapeDtypeStruct((s // c, 2, HEAD_LANES, HEAD_LANES), F32)],
        scratch_shapes=[pltpu.VMEM((2, HEAD_LANES, HEAD_LANES), F32)],
        compiler_params=_params("arbitrary"))(proj, proj, proj, proj, proj, wg, bg, gn)


def _gla_bwd(proj, wg, bg, gn, o_raw, states, dmixed, *, name):
    s = proj.shape[0]
    tb, c = GLA_ROWS, GLA_CHUNK
    cb = tb // c
    nblk, nch = s // tb, s // c

    def body(q_ref, k_ref, v_ref, r_ref, lr_ref, wg_ref, bg_ref, gn_ref, o_ref, st_ref, stn_ref, dy_ref,
             dq_ref, dk_ref, dv_ref, dr_ref, dlr_ref, gwg_ref, sums_ref, dstate):
        i = pl.program_id(0)

        @pl.when(i == 0)
        def _():
            dstate[...] = jnp.zeros_like(dstate)
            gwg_ref[...] = jnp.zeros_like(gwg_ref)
            sums_ref[...] = jnp.zeros_like(sums_ref)

        low = lax.broadcasted_iota(jnp.int32, (tb, HEAD_LANES), 1) < 64
        masks = (low, jnp.logical_not(low))
        z, b, b_end, causal, upper = _gla_block_setup(lr_ref, wg_ref, bg_ref)
        lr_b = lr_ref[...].astype(BF16)
        dlr = jnp.zeros((tb, HEAD_LANES), F32)
        per_chunk = lambda rows, mats, fn: jnp.concatenate([fn(r, m.astype(BF16)) for r, m in zip(_chunks(rows), mats)], axis=0)
        pairs = []
        for p in range(2):
            cols = pl.ds(p * HEAD_LANES, HEAD_LANES)
            sl = slice(p * HEAD_LANES, (p + 1) * HEAD_LANES)
            bp, bep = b[:, sl], b_end[:, sl]
            e_in, e_out, e_end = jnp.exp(bp), jnp.exp(-bp), jnp.exp(bep - bp)
            q = q_ref[:, cols] * 0.125
            k = k_ref[:, cols]
            q_in, k_out, k_end = q * e_in, k * e_out, k * e_end
            qms = [jnp.where(m, q_in, 0.0).astype(BF16) for m in masks]
            kms_out = [jnp.where(m, k_out, 0.0).astype(BF16) for m in masks]
            kms_end = [jnp.where(m, k_end, 0.0).astype(BF16) for m in masks]
            vs, dos = [], []
            for e in range(2):
                hc = pl.ds((2 * p + e) * HEAD_LANES, HEAD_LANES)
                o, rr, dy = o_ref[:, hc], r_ref[:, hc], dy_ref[:, hc]
                sg = _sigmoid(rr)
                rs = lax.rsqrt(jnp.mean(o * o, axis=-1, keepdims=True) + EPS)
                on = o * rs
                t = dy * (rr * sg)
                sums_ref[1, :, hc] += _fold8(t * on)
                dn = t * gn_ref[...]
                dos.append((rs * (dn - on * jnp.mean(dn * on, axis=-1, keepdims=True))).astype(BF16))
                dr_ref[:, hc] = (dy * on * gn_ref[...] * (sg * (1.0 + rr * (1.0 - sg)))).astype(BF16)
                vs.append(v_ref[:, hc].astype(BF16))
            grow = [_tn(d0, q0) + _tn(d1, q1) for d0, q0, d1, q1 in zip(_chunks(dos[0]), _chunks(qms[0]), _chunks(dos[1]), _chunks(qms[1]))]
            pairs.append((bep, e_in, e_out, e_end, q, k, qms, kms_out, kms_end, vs, dos, grow))
        chains = []
        for p in range(2):
            bep, grow = pairs[p][0], pairs[p][-1]
            entering = [st_ref[ch, p] for ch in range(cb)]
            dst, leaving_grad = dstate[p], [None] * cb
            for ch in reversed(range(cb)):
                leaving_grad[ch] = dst
                dst = dst * jnp.exp(bep[ch * c:ch * c + 1, :]) + grow[ch]
            dstate[p] = dst
            chains.append((entering, leaving_grad))
        for p in range(2):
            cols = pl.ds(p * HEAD_LANES, HEAD_LANES)
            sl = slice(p * HEAD_LANES, (p + 1) * HEAD_LANES)
            _, e_in, e_out, e_end, q, k, qms, kms_out, kms_end, vs, dos, _ = pairs[p]
            entering, leaving_grad = chains[p]
            leaving = entering[1:] + [stn_ref[0, p]]
            felt = jnp.concatenate([jnp.broadcast_to(jnp.sum(dg_st * st, axis=0, keepdims=True), (c, HEAD_LANES))
                                    for dg_st, st in zip(leaving_grad, leaving)], axis=0)
            dq_in = jnp.zeros((tb, HEAD_LANES), F32)
            dk_out = jnp.zeros((tb, HEAD_LANES), F32)
            dk_end = jnp.zeros((tb, HEAD_LANES), F32)
            for e in range(2):
                hc = pl.ds((2 * p + e) * HEAD_LANES, HEAD_LANES)
                a = jnp.where(causal, _nt(qms[e], kms_out[e]), 0.0).astype(BF16)
                da = jnp.where(causal, _nt(dos[e], vs[e]), 0.0).astype(BF16)
                dv_ref[:, hc] = (_tn(a, dos[e]) + per_chunk(kms_end[e], leaving_grad, _nt)).astype(BF16)
                dq_in = dq_in + jnp.where(masks[e], per_chunk(dos[e], entering, _nn) + _nn(da, kms_out[e]), 0.0)
                dk_out = dk_out + _tn(da, qms[e])
                dk_end = dk_end + jnp.where(masks[e], per_chunk(vs[e], leaving_grad, _nn), 0.0)
            dq = dq_in * e_in
            dk = dk_out * e_out + dk_end * e_end
            dq_ref[:, cols] = (dq * 0.125).astype(BF16)
            dk_ref[:, cols] = dk.astype(BF16)
            dg = _sum_left(upper.astype(BF16), q * dq - k * dk) + felt
            dz = dg * (1.0 / GLA_GATE_TAU) * _sigmoid(-z[:, sl])
            dz_b = dz.astype(BF16)
            sums_ref[0, :, cols] += _fold8(dz)
            dlr = dlr + _nt(dz_b, wg_ref[:, cols])
            gwg_ref[:, cols] += _tn(lr_b, dz_b)
        dlr_ref[...] = dlr.astype(BF16)

        @pl.when(i == nblk - 1)
        def _():
            _spread_total(sums_ref)

    rev = lambda i: nblk - 1 - i

    def col(width, at):
        return pl.BlockSpec((tb, width), lambda i: (rev(i), at // width))

    full = lambda shape: pl.BlockSpec(shape, lambda i: tuple(0 for _ in shape))
    out_col = lambda width: pl.BlockSpec((tb, width), lambda i: (rev(i), 0))
    return pl.pallas_call(
        body, name=name, grid=(nblk,),
        in_specs=[col(256, C_GQ), col(256, C_GK), col(512, C_GV), col(512, C_GR), col(128, C_LR),
                  full((HEAD_LANES, 256)), full((1, 256)), full((1, HEAD_LANES)),
                  pl.BlockSpec((tb, 512), lambda i: (rev(i), 0)),
                  pl.BlockSpec((cb, 2, HEAD_LANES, HEAD_LANES), lambda i: (rev(i), 0, 0, 0)),
                  pl.BlockSpec((1, 2, HEAD_LANES, HEAD_LANES), lambda i: (jnp.minimum((rev(i) + 1) * cb, nch - 1), 0, 0, 0)),
                  pl.BlockSpec((tb, 512), lambda i: (rev(i), 0))],
        out_specs=[out_col(256), out_col(256), out_col(512), out_col(512), out_col(128),
                   full((HEAD_LANES, 256)), full((2, 8, 512))],
        out_shape=[jax.ShapeDtypeStruct((s, 256), BF16), jax.ShapeDtypeStruct((s, 256), BF16),
                   jax.ShapeDtypeStruct((s, 512), BF16), jax.ShapeDtypeStruct((s, 512), BF16),
                   jax.ShapeDtypeStruct((s, 128), BF16), jax.ShapeDtypeStruct((HEAD_LANES, 256), F32),
                   jax.ShapeDtypeStruct((2, 8, 512), F32)],
        scratch_shapes=[pltpu.VMEM((2, HEAD_LANES, HEAD_LANES), F32)],
        compiler_params=_params("arbitrary"))(proj, proj, proj, proj, proj, wg, bg, gn, o_raw, states, states, dmixed)


def _head_sums(v):
    ri = lax.broadcasted_iota(jnp.int32, (HEAD_LANES, HEAD_LANES), 0) // 64
    ci = lax.broadcasted_iota(jnp.int32, (HEAD_LANES, HEAD_LANES), 1) // 64
    ones = (ri == ci).astype(BF16)
    return jnp.concatenate([_sum_right(v[:, p * HEAD_LANES:(p + 1) * HEAD_LANES], ones) for p in range(4)], axis=1)


def _attn_prep(proj, qg, kg, *, name):
    s = proj.shape[0]
    tm = ROW_TILE

    def body(q_ref, k_ref, qg_ref, kg_ref, qa_ref, ka_ref):
        q, k = q_ref[...], k_ref[...]
        qr = lax.rsqrt(_head_sums(q * q) * (1.0 / 64) + EPS)
        kr = lax.rsqrt(_head_sums(k * k) * (1.0 / 64) + EPS)
        qa_ref[...] = q * qr * qg_ref[...] * 0.125
        ka_ref[...] = k * kr * kg_ref[...]

    col = lambda at: pl.BlockSpec((tm, 512), lambda i: (i, at // 512))
    vec = pl.BlockSpec((1, 512), lambda i: (0, 0))
    out = pl.BlockSpec((tm, 512), lambda i: (i, 0))
    return pl.pallas_call(
        body, name=name, grid=(s // tm,), in_specs=[col(C_AQ), col(C_AK), vec, vec], out_specs=[out] * 2,
        out_shape=[jax.ShapeDtypeStruct((s, 512), F32)] * 2, compiler_params=_params("parallel"))(proj, proj, qg, kg)


FAR = 1e30
LOG2E, LN2 = 1.4426950408889634, 0.6931471805599453


def _attn_distance(first):
    blk = ATTN_BLOCK
    iq = lax.broadcasted_iota(jnp.int32, (2 * blk, 2 * blk), 0) & (blk - 1)
    ik = lax.broadcasted_iota(jnp.int32, (2 * blk, 2 * blk), 1)
    rel = iq + blk - ik
    valid = (rel >= 0) & (rel <= blk) & (jnp.logical_not(first) | (ik >= blk))
    return jnp.where(valid, rel.astype(F32), FAR)


def _stack_heads(t2):
    low = lax.broadcasted_iota(jnp.int32, t2.shape, 1) < 64
    return jnp.concatenate([jnp.where(low, t2, 0.0), jnp.where(low, 0.0, t2)], axis=0).astype(BF16)


def _unstack_heads(t):
    blk = ATTN_BLOCK
    low = lax.broadcasted_iota(jnp.int32, (blk, HEAD_LANES), 1) < 64
    return jnp.where(low, t[0:blk], t[blk:2 * blk])


def _attn_scores(qs, kcat, slopes, dil, dist):
    top = lax.broadcasted_iota(jnp.int32, (2 * ATTN_BLOCK, 1), 0) < ATTN_BLOCK
    return _nt(qs, kcat) - jnp.where(top, slopes[0] * (dil * LOG2E), slopes[1] * (dil * LOG2E)) * dist


def _pair_slopes(p):
    if isinstance(p, int):
        return ALIBI_SLOPES[2 * p], ALIBI_SLOPES[2 * p + 1]
    pick = lambda e: jnp.where(p == 0, ALIBI_SLOPES[e], jnp.where(p == 1, ALIBI_SLOPES[2 + e],
                               jnp.where(p == 2, ALIBI_SLOPES[4 + e], ALIBI_SLOPES[6 + e])))
    return pick(0), pick(1)


ATTN_GROUP = 4


def _each(fn, *lists):
    return [fn(*args) for args in zip(*lists)]


def _attn_group_fwd(q2s, kcats, vcats, slopes, dil, dist):
    qs = _each(lambda q2: _stack_heads(q2 * LOG2E), q2s)
    sc = _each(lambda q, k, sl: _attn_scores(q, k, sl, dil, dist), qs, kcats, slopes)
    m = _each(lambda s: jnp.max(s, axis=-1, keepdims=True), sc)
    pr = _each(lambda s, mx: jnp.exp2(s - mx), sc, m)
    den = _each(lambda p: jnp.sum(p, axis=-1, keepdims=True), pr)
    o = _each(lambda p, v, d: _nn(p.astype(BF16), v) / d, pr, vcats, den)
    lse = _each(lambda mx, d, t: jnp.broadcast_to(mx + jnp.log2(d), t.shape), m, den, o)
    return _each(lambda t, l: (_unstack_heads(t), _unstack_heads(l)), o, lse)


def _attn_group_bwd(q2s, kcats, vcats, do2s, y2s, lse2s, slopes, dil, dist):
    lane = lax.broadcasted_iota(jnp.int32, (ATTN_BLOCK, HEAD_LANES), 1)
    low = lane < 64
    per_head = lambda t, pick: jnp.concatenate([jnp.sum(jnp.where(pick(0), t, 0.0), axis=-1, keepdims=True),
                                                jnp.sum(jnp.where(pick(1), t, 0.0), axis=-1, keepdims=True)], axis=0)
    lse = _each(lambda l: per_head(l, lambda e: lane == 64 * e), lse2s)
    delta = _each(lambda d, y: per_head(d * y, lambda e: low if e == 0 else jnp.logical_not(low)), do2s, y2s)
    qs = _each(lambda q2: _stack_heads(q2 * LOG2E), q2s)
    dos = _each(_stack_heads, do2s)
    sc = _each(lambda q, k, sl: _attn_scores(q, k, sl, dil, dist), qs, kcats, slopes)
    pr = _each(lambda s, l: jnp.exp2(s - l), sc, lse)
    dp = _each(_nt, dos, vcats)
    ds = _each(lambda p, d, dl: (p * (d - dl)).astype(BF16), pr, dp, delta)
    dq = _each(lambda d, k: _unstack_heads(_nn(d, k)), ds, kcats)
    dk = _each(lambda d, q: _tn(d, q) * LN2, ds, qs)
    dv = _each(lambda p, d: _tn(p.astype(BF16), d), pr, dos)
    return list(zip(dq, dk, dv))


def _attn_specs(dil):
    rows = ATTN_BLOCK * dil
    if dil == 1:
        cur = lambda at: pl.BlockSpec((rows, 512), lambda n: (n, at // 512))
        prev = lambda at: pl.BlockSpec((rows, 512), lambda n: (jnp.maximum(n - 1, 0), at // 512))
    else:
        cur = lambda at: pl.BlockSpec((rows, HEAD_LANES), lambda n, p: (n, at // HEAD_LANES + p))
        prev = lambda at: pl.BlockSpec((rows, HEAD_LANES), lambda n, p: (jnp.maximum(n - 1, 0), at // HEAD_LANES + p))
    return cur, prev


def _attn_loop(dil, one_group, p):
    if dil == 1:
        one_group([(slice(None), pl.ds(p * HEAD_LANES, HEAD_LANES), p) for p in range(ATTN_GROUP)])
    else:
        group = min(dil, ATTN_GROUP)

        def step(g, carry):
            one_group([(pl.ds(g * group + j, ATTN_BLOCK, stride=dil), slice(None), p) for j in range(group)])
            return carry

        if dil == group:
            step(0, 0)
        else:
            lax.fori_loop(0, dil // group, step, 0)


def _dil_attn_fwd(qa, ka, proj, dil, *, name):
    s = qa.shape[0]

    def body(q_ref, kp_ref, kc_ref, vp_ref, vc_ref, o_ref, lse_ref):
        dist = _attn_distance(pl.program_id(0) == 0)
        pair = None if dil == 1 else pl.program_id(1)

        def one_group(items):
            both = lambda a, b: [jnp.concatenate([a[rows, cols], b[rows, cols]], axis=0).astype(BF16) for rows, cols, _ in items]
            outs = _attn_group_fwd([q_ref[rows, cols] for rows, cols, _ in items], both(kp_ref, kc_ref), both(vp_ref, vc_ref),
                                   [_pair_slopes(p) for _, _, p in items], dil, dist)
            for (rows, cols, _), (o2, lse2) in zip(items, outs):
                o_ref[rows, cols] = o2
                lse_ref[rows, cols] = lse2

        _attn_loop(dil, one_group, pair)

    cur, prev = _attn_specs(dil)
    grid = (s // ATTN_BLOCK,) if dil == 1 else (s // (ATTN_BLOCK * dil), 4)
    return pl.pallas_call(
        body, name=name, grid=grid, in_specs=[cur(0), prev(0), cur(0), prev(C_AV), cur(C_AV)], out_specs=[cur(0), cur(0)],
        out_shape=[jax.ShapeDtypeStruct((s, 512), F32)] * 2,
        compiler_params=_params(*["parallel"] * len(grid)))(qa, ka, ka, proj, proj)


def _attn_merge(branches, y_gla, *, name):
    s = y_gla.shape[0]
    tm = ROW_TILE

    def body(o0, l0, o1, l1, o2, l2, yg_ref, mixed_ref, y_ref, lse_ref):
        m = jnp.maximum(jnp.maximum(l0[...], l1[...]), l2[...])
        w0, w1, w2 = jnp.exp2(l0[...] - m), jnp.exp2(l1[...] - m), jnp.exp2(l2[...] - m)
        zs = w0 + w1 + w2
        y = (w0 * o0[...] + w1 * o1[...] + w2 * o2[...]) / zs
        y_ref[...] = y
        lse_ref[...] = m + jnp.log2(zs)
        mixed_ref[:, 0:512] = yg_ref[...]
        mixed_ref[:, 512:1024] = y.astype(BF16)

    blk = pl.BlockSpec((tm, 512), lambda i: (i, 0))
    args = [t for pair in branches for t in pair]
    return pl.pallas_call(
        body, name=name, grid=(s // tm,), in_specs=[blk] * 7,
        out_specs=[pl.BlockSpec((tm, 1024), lambda i: (i, 0)), blk, blk],
        out_shape=[jax.ShapeDtypeStruct((s, 1024), BF16), jax.ShapeDtypeStruct((s, 512), F32),
                   jax.ShapeDtypeStruct((s, 512), F32)],
        compiler_params=_params("parallel"))(*args, y_gla)


def _dil_attn_bwd(qa, ka, proj, y_att, lse, dmixed, dil, *, name):
    s = qa.shape[0]
    blk, rows_per_step = ATTN_BLOCK, ATTN_BLOCK * dil
    nb = s // rows_per_step
    step_axis = 0 if dil == 1 else 1

    def body(q_ref, kp_ref, kc_ref, vp_ref, vc_ref, y_ref, lse_ref, do_ref, dq_ref, dk_ref, dv_ref, dk_own, dv_own):
        n = pl.program_id(step_axis)
        pair = None if dil == 1 else pl.program_id(0)
        dist = _attn_distance(n == 0)

        @pl.when(n == 0)
        def _():
            dk_own[...] = jnp.zeros_like(dk_own)
            dv_own[...] = jnp.zeros_like(dv_own)

        @pl.when(n < nb)
        def _():
            def one_group(items):
                both = lambda a, b: [jnp.concatenate([a[rows, cols], b[rows, cols]], axis=0).astype(BF16) for rows, cols, _ in items]
                at = lambda ref: [ref[rows, cols] for rows, cols, _ in items]
                outs = _attn_group_bwd(at(q_ref), both(kp_ref, kc_ref), both(vp_ref, vc_ref), at(do_ref), at(y_ref), at(lse_ref),
                                       [_pair_slopes(p) for _, _, p in items], dil, dist)
                for (rows, cols, _), (dq, dk, dv) in zip(items, outs):
                    dq_ref[rows, cols] = dq
                    dk_ref[rows, cols] = dk_own[rows, cols] + dk[0:blk]
                    dv_ref[rows, cols] = dv_own[rows, cols] + dv[0:blk]
                    dk_own[rows, cols] = dk[blk:2 * blk]
                    dv_own[rows, cols] = dv[blk:2 * blk]

            _attn_loop(dil, one_group, pair)

        @pl.when(n == nb)
        def _():
            dk_ref[...] = dk_own[...]
            dv_ref[...] = dv_own[...]

    width = 512 if dil == 1 else HEAD_LANES

    def spec(at, row_of):
        if dil == 1:
            return pl.BlockSpec((rows_per_step, width), lambda n: (row_of(n), at // width))
        return pl.BlockSpec((rows_per_step, width), lambda p, n: (row_of(n), at // width + p))

    cur = lambda at: spec(at, lambda n: jnp.minimum(n, nb - 1))
    prev = lambda at: spec(at, lambda n: jnp.maximum(n - 1, 0))
    grid = (nb + 1,) if dil == 1 else (4, nb + 1)
    sems = ("arbitrary",) if dil == 1 else ("parallel", "arbitrary")
    return pl.pallas_call(
        body, name=name, grid=grid,
        in_specs=[cur(0), prev(0), cur(0), prev(C_AV), cur(C_AV), cur(0), cur(0), cur(512)], out_specs=[cur(0), prev(0), prev(0)],
        out_shape=[jax.ShapeDtypeStruct((s, 512), F32)] * 3,
        scratch_shapes=[pltpu.VMEM((rows_per_step, width), F32)] * 2, compiler_params=_params(*sems),
    )(qa, ka, ka, proj, proj, y_att, lse, dmixed)


def _attn_post(parts, proj, qg, kg, *, name):
    s = proj.shape[0]
    tm = ROW_TILE
    nblk = s // tm

    def body(*refs):
        ins, (q_ref, k_ref, qg_ref, kg_ref, dq_out, dk_out, dv_out, sums_ref) = refs[:9], refs[9:]
        i = pl.program_id(0)

        @pl.when(i == 0)
        def _():
            sums_ref[...] = jnp.zeros_like(sums_ref)

        dq = (ins[0][...] + ins[3][...]) + ins[6][...]
        dk = (ins[1][...] + ins[4][...]) + ins[7][...]
        dv = (ins[2][...] + ins[5][...]) + ins[8][...]
        dv_out[...] = dv.astype(BF16)
        for row, (x_ref, g_ref, dy, out, post) in enumerate(((q_ref, qg_ref, dq, dq_out, 0.125), (k_ref, kg_ref, dk, dk_out, 1.0))):
            x = x_ref[...]
            rs = lax.rsqrt(_head_sums(x * x) * (1.0 / 64) + EPS)
            xn = x * rs
            dy = dy * post
            sums_ref[row] += _fold8(dy * xn)
            dn = dy * g_ref[...]
            out[...] = (rs * (dn - xn * (_head_sums(dn * xn) * (1.0 / 64)))).astype(BF16)

        @pl.when(i == nblk - 1)
        def _():
            _spread_total(sums_ref)

    here = pl.BlockSpec((tm, 512), lambda i: (i, 0))
    col = lambda at: pl.BlockSpec((tm, 512), lambda i: (i, at // 512))
    vec = pl.BlockSpec((1, 512), lambda i: (0, 0))
    return pl.pallas_call(
        body, name=name, grid=(nblk,), in_specs=[here] * 9 + [col(C_AQ), col(C_AK), vec, vec],
        out_specs=[here, here, here, pl.BlockSpec((2, 8, 512), lambda i: (0, 0, 0))],
        out_shape=[jax.ShapeDtypeStruct((s, 512), BF16)] * 3 + [jax.ShapeDtypeStruct((2, 8, 512), F32)],
        compiler_params=_params("arbitrary"))(*[t for part in parts for t in part], proj, proj, qg, kg)


FFN_TM, FFN_TN = 256, 1408
HALO = 16


def _conv3(u_ref, halo_ref, w_ref, b_ref, first):
    u = u_ref[...].astype(F32)
    ext = jnp.concatenate([jnp.where(first, 0.0, halo_ref[...].astype(F32)), u], axis=0)
    u1 = pltpu.roll(ext, 1, 0)[HALO:]
    u2 = pltpu.roll(ext, 2, 0)[HALO:]
    return b_ref[...] + w_ref[0:1, :] * u2 + w_ref[1:2, :] * u1 + w_ref[2:3, :] * u, u, u1, u2


def _ffn_specs(tm, tn):
    nj = D_FF // tn
    blk = lambda half: pl.BlockSpec((tm, tn), lambda j, i: (i, j + half * nj))
    halo = lambda half: pl.BlockSpec((HALO, tn), lambda j, i: (jnp.maximum(i * (tm // HALO) - 1, 0), j + half * nj))
    wspec = lambda half: pl.BlockSpec((3, tn), lambda j, i: (0, j + half * nj))
    bspec = lambda half: pl.BlockSpec((1, tn), lambda j, i: (0, j + half * nj))
    return [blk(0), halo(0), blk(1), halo(1), wspec(0), wspec(1), bspec(0), bspec(1)]


def _conv_swiglu_fwd(u, conv_w, conv_b, *, name):
    s = u.shape[0]
    tm, tn = FFN_TM, FFN_TN

    def body(ug_ref, hg_ref, uv_ref, hv_ref, wg_ref, wv_ref, bg_ref, bv_ref, act_ref, uc_ref):
        first = pl.program_id(1) == 0
        cg = _conv3(ug_ref, hg_ref, wg_ref, bg_ref, first)[0]
        cv = _conv3(uv_ref, hv_ref, wv_ref, bv_ref, first)[0]
        act_ref[...] = (cg * _sigmoid(cg) * cv).astype(BF16)
        uc_ref[0] = cg.astype(BF16)
        uc_ref[1] = cv.astype(BF16)

    return pl.pallas_call(
        body, name=name, grid=(D_FF // tn, s // tm), in_specs=_ffn_specs(tm, tn),
        out_specs=[pl.BlockSpec((tm, tn), lambda j, i: (i, j)), pl.BlockSpec((2, tm, tn), lambda j, i: (0, i, j))],
        out_shape=[jax.ShapeDtypeStruct((s, D_FF), BF16), jax.ShapeDtypeStruct((2, s, D_FF), BF16)],
        compiler_params=_params("parallel", "parallel"))(u, u, u, u, conv_w, conv_w, conv_b, conv_b)


def _swiglu_bwd(uc, dact, *, name):
    _, s, _ = uc.shape
    tm, tn = FFN_TM, FFN_TN

    def body(uc_ref, da_ref, duc_ref, sums_ref):
        i = pl.program_id(1)

        @pl.when(i == 0)
        def _():
            sums_ref[...] = jnp.zeros_like(sums_ref)

        cg, cv, da = uc_ref[0].astype(F32), uc_ref[1].astype(F32), da_ref[...].astype(F32)
        sg = _sigmoid(cg)
        dg = da * cv * (sg * (1.0 + cg * (1.0 - sg)))
        dv = da * (cg * sg)
        duc_ref[0] = dg.astype(BF16)
        duc_ref[1] = dv.astype(BF16)
        sums_ref[0] += _fold8(dg)
        sums_ref[1] += _fold8(dv)

        @pl.when(i == s // tm - 1)
        def _():
            _spread_total(sums_ref)

    pair = pl.BlockSpec((2, tm, tn), lambda j, i: (0, i, j))
    return pl.pallas_call(
        body, name=name, grid=(D_FF // tn, s // tm), in_specs=[pair, pl.BlockSpec((tm, tn), lambda j, i: (i, j))],
        out_specs=[pair, pl.BlockSpec((2, 8, tn), lambda j, i: (0, 0, j))],
        out_shape=[jax.ShapeDtypeStruct((2, s, D_FF), BF16), jax.ShapeDtypeStruct((2, 8, D_FF), F32)],
        compiler_params=_params("parallel", "arbitrary"))(uc, dact)


def _conv_bwd(duc, u, conv_w, *, name):
    _, s, _ = duc.shape
    tm, tn = FFN_TM, FFN_TN
    nj, ni = D_FF // tn, s // tm

    def body(d_ref, halo_ref, u_ref, w_ref, du_ref, sums_ref):
        i = pl.program_id(2)

        @pl.when(i == 0)
        def _():
            sums_ref[...] = jnp.zeros_like(sums_ref)

        d = d_ref[0].astype(F32)
        ext = jnp.concatenate([d, jnp.where(i == ni - 1, 0.0, halo_ref[0].astype(F32))], axis=0)
        n = tm + HALO
        d1 = pltpu.roll(ext, n - 1, 0)[:tm]
        d2 = pltpu.roll(ext, n - 2, 0)[:tm]
        du_ref[...] = (w_ref[2:3, :] * d + w_ref[1:2, :] * d1 + w_ref[0:1, :] * d2).astype(BF16)
        uv = u_ref[...].astype(F32)
        for t, shifted in enumerate((d2, d1, d)):
            sums_ref[0, t] += _fold8(shifted * uv)

        @pl.when(i == ni - 1)
        def _():
            _spread_total(sums_ref)

    return pl.pallas_call(
        body, name=name, grid=(2, nj, ni),
        in_specs=[pl.BlockSpec((1, tm, tn), lambda g, j, i: (g, i, j)),
                  pl.BlockSpec((1, HALO, tn), lambda g, j, i: (g, jnp.minimum((i + 1) * (tm // HALO), s // HALO - 1), j)),
                  pl.BlockSpec((tm, tn), lambda g, j, i: (i, g * nj + j)),
                  pl.BlockSpec((3, tn), lambda g, j, i: (0, g * nj + j))],
        out_specs=[pl.BlockSpec((tm, tn), lambda g, j, i: (i, g * nj + j)),
                   pl.BlockSpec((1, 3, 8, tn), lambda g, j, i: (g, 0, 0, j))],
        out_shape=[jax.ShapeDtypeStruct((s, 2 * D_FF), BF16), jax.ShapeDtypeStruct((2, 3, 8, D_FF), F32)],
        compiler_params=_params("parallel", "parallel", "arbitrary"))(duc, duc, u, conv_w)


def _loss_head(x1, ffn, gate, target, *, name):
    s, d = x1.shape
    tm = ROW_TILE

    def body(x_ref, f_ref, g_ref, t_ref, dy_ref, df_ref, sums_ref):
        i = pl.program_id(0)

        @pl.when(i == 0)
        def _():
            sums_ref[...] = jnp.zeros_like(sums_ref)

        f = f_ref[...]
        err = x_ref[...] + g_ref[...] * f - t_ref[...]
        dy = err * (1.0 / d)
        dy_ref[...] = dy
        df_ref[...] = (g_ref[...] * dy).astype(BF16)
        sums_ref[0] += _fold8(dy * f)
        sums_ref[1] += _fold8(err * err)

        @pl.when(i == s // tm - 1)
        def _():
            _spread_total(sums_ref)

    row = pl.BlockSpec((tm, d), lambda i: (i, 0))
    return pl.pallas_call(
        body, name=name, grid=(s // tm,), in_specs=[row, row, pl.BlockSpec((1, d), lambda i: (0, 0)), row],
        out_specs=[row, row, pl.BlockSpec((2, 8, d), lambda i: (0, 0, 0))],
        out_shape=[jax.ShapeDtypeStruct((s, d), F32), jax.ShapeDtypeStruct((s, d), BF16), jax.ShapeDtypeStruct((2, 8, d), F32)],
        compiler_params=_params("arbitrary"))(x1, ffn, gate, target)


def _adamw(w, g, m, v, *, name):
    rows, cols = w.shape
    if rows % 8 == 0 or rows <= ROW_TILE:
        tm = next((t for t in range(ROW_TILE, 7, -8) if rows % t == 0), rows)
        blk, grid = pl.BlockSpec((tm, cols), lambda i: (i, 0)), (rows // tm,)
    else:
        blk, grid = pl.BlockSpec((rows, ROW_TILE), lambda i: (0, i)), (cols // ROW_TILE,)

    def body(w_ref, g_ref, m_ref, v_ref, d_ref, mo_ref, vo_ref):
        gv = g_ref[...]
        mn = ADAM_B1 * m_ref[...] + (1.0 - ADAM_B1) * gv
        vn = ADAM_B2 * v_ref[...] + (1.0 - ADAM_B2) * (gv * gv)
        m_hat = mn / (1.0 - ADAM_B1 ** ADAM_STEP)
        v_hat = vn / (1.0 - ADAM_B2 ** ADAM_STEP)
        d_ref[...] = -ADAM_LR * (m_hat / (jnp.sqrt(v_hat) + ADAM_EPS) + ADAM_WD * w_ref[...])
        mo_ref[...] = mn
        vo_ref[...] = vn

    return pl.pallas_call(
        body, name=name, grid=grid, in_specs=[blk] * 4, out_specs=[blk] * 3,
        out_shape=[jax.ShapeDtypeStruct((rows, cols), F32)] * 3, compiler_params=_params("parallel"))(w, g, m, v)


def _colsum(t):
    return t[..., 0, :]


def _in_proj_layout(w_in):
    pad = jnp.zeros((w_in.shape[0], PROJ_W - C_LR - GLA_GATE_RANK), w_in.dtype)
    return jnp.concatenate([w_in[:, :1536], w_in[:, 1552:], w_in[:, 1536:1552], pad], axis=1)


def _in_proj_grad_layout(g):
    return jnp.concatenate([g[:, :1536], g[:, C_LR:C_LR + GLA_GATE_RANK], g[:, 1536:C_LR]], axis=1)


def _gate_layout(gla_w_gate):
    return jnp.pad(gla_w_gate, ((0, HEAD_LANES - GLA_GATE_RANK), (0, 0))).astype(BF16)


def _local_step(x, target, mod, wi, wo, ffn_weights, ffn_grads_ready, attn_grads_ready, conv_w, conv_b, wg, bg, gn, qg, kg, n1g, n2g):
    d = D_MODEL
    sh1, sc1, g1, sh2, sc2, g2 = [mod[:, i * d:(i + 1) * d] for i in range(6)]
    qg8, kg8 = jnp.tile(qg, (1, 8)), jnp.tile(kg, (1, 8))

    _, h1, h1_t = _norm_mod_fwd(x, None, None, n1g, sc1, sh1, name="norm1_fwd")
    proj = _mm(h1, wi, tm=1024, tn=PROJ_W, tk=d, name="in_proj")
    o_raw, y_gla, states = _gla_fwd(proj, wg, bg, gn, name="gla_fwd")
    qa, ka = _attn_prep(proj, qg8, kg8, name="attn_prep")
    branches = [_dil_attn_fwd(qa, ka, proj, dil, name=f"attn_fwd_d{dil}") for dil in DILATIONS]
    mixed, y_att, lse = _attn_merge(branches, y_gla, name="attn_merge")
    attn_out = _mm(mixed, wo, tm=1024, tn=d, tk=d, name="out_proj")
    x1, h2, h2_t = _norm_mod_fwd(x, attn_out, g1, n2g, sc2, sh2, name="norm2_fwd")
    wup, wdown = ffn_weights(h2)
    u = _mm(h2, wup, out_dtype=BF16, tm=1024, tn=D_FF, tk=d, name="up_proj")
    act, uc = _conv_swiglu_fwd(u, conv_w, conv_b, name="conv_swiglu_fwd")
    ffn = _mm(act, wdown, tm=1024, tn=d, tk=D_FF, name="down_proj")
    dy, dffn, head_sums = _loss_head(x1, ffn, g2, target, name="loss_head")

    dact = _mm(dffn, wdown, tb=True, out_dtype=BF16, tm=1024, tn=D_FF, tk=d, name="down_proj_dx")
    g_wdown, g_wdown_b = _mm(act, dffn, ta=True, tm=1408, tn=d, tk=2048, also_bf16=True, name="down_proj_dw")
    duc, bias_sums = _swiglu_bwd(uc, dact, name="swiglu_bwd")
    du, tap_sums = _conv_bwd(duc, u, conv_w, name="conv_bwd")
    dh2 = _mm(du, wup, tb=True, tm=1024, tn=d, tk=D_FF, name="up_proj_dx")
    g_wup, g_wup_b = _mm(h2_t, du, tm=d, tn=1408, tk=2048, shard_cols=True, also_bf16=True, name="up_proj_dw")
    token = ffn_grads_ready(g_wup_b, g_wdown_b)
    g1_late = g1 if token is None else g1 + token[0:1, 0:1]
    dx1, dao, n2_sums = _norm_mod_bwd(x1, dh2, dy, n2g, sc2, attn_out, g1_late, name="norm2_bwd")

    dmixed = _mm(dao, wo, tb=True, tm=1024, tn=d, tk=d, name="out_proj_dx")
    g_wo = _mm(mixed, dao, ta=True, tm=d, tn=d, tk=1024, name="out_proj_dw")
    dgq, dgk, dgv, dgr, dlr, g_wg, gla_sums = _gla_bwd(proj, wg, bg, gn, o_raw, states, dmixed, name="gla_bwd")
    parts = [_dil_attn_bwd(qa, ka, proj, y_att, lse, dmixed, dil, name=f"attn_bwd_d{dil}") for dil in DILATIONS]
    daq, dak, dav, qk_sums = _attn_post(parts, proj, qg8, kg8, name="attn_post")
    dproj = jnp.concatenate([dgq, dgk, dgv, dgr, daq, dak, dav, dlr], axis=1)
    g_wi = _mm(h1_t, dproj, tm=512, tn=PROJ_W, tk=2048, name="in_proj_dw")
    token = attn_grads_ready(g_wi, g_wo)
    wi_late = wi if token is None else wi + token[0:1, 0:1].astype(BF16)
    dh1 = _mm(dproj, wi_late, tb=True, tm=1024, tn=d, tk=PROJ_W, name="in_proj_dx")
    grad_x, _, n1_sums = _norm_mod_bwd(x, dh1, dx1, n1g, sc1, None, None, name="norm1_bwd")

    n1, n2, hs, taps, cb = _colsum(n1_sums), _colsum(n2_sums), _colsum(head_sums), _colsum(tap_sums), _colsum(bias_sums)
    gs, qs = _colsum(gla_sums), _colsum(qk_sums)
    dmod = jnp.concatenate([n1[1], n1[0] * n1g[0], n2[2], n2[1], n2[0] * n2g[0], hs[0]])
    small = dict(
        dmod=dmod,
        norm1_g=n1[0] * (1.0 + sc1[0]), norm2_g=n2[0] * (1.0 + sc2[0]),
        gla_w_gate=g_wg[:GLA_GATE_RANK], gla_b_gate=gs[0, :256], gla_norm_g=gs[1].reshape(4, 128).sum(axis=0),
        q_norm_g=qs[0].reshape(8, 64).sum(axis=0), k_norm_g=qs[1].reshape(8, 64).sum(axis=0),
        conv_w=jnp.concatenate([taps[0], taps[1]], axis=1), conv_b=jnp.concatenate([cb[0], cb[1]]),
    )
    return head_sums[1], grad_x, (g_wi, g_wo, g_wup, g_wdown), small


N_DEV, N_CHIP = 8, 4
ANY = pl.BlockSpec(memory_space=pl.ANY)
VMEM_SPEC = pl.BlockSpec(memory_space=pltpu.VMEM)


def _place():
    x, y, c = lax.axis_index("x"), lax.axis_index("y"), lax.axis_index("c")
    other_chips = [(1 - x, y), (x, 1 - y), (1 - x, 1 - y)]
    return x, y, c, (x, y, 1 - c), other_chips


def _all_gather_small(v, *, name):
    m, n = v.shape

    def body(v_ref, out_ref, send_sems, recv_sems, local_sem):
        x, y, c, sibling, chips = _place()
        me = (x, y, c)

        def rows(px, py, pc):
            return out_ref.at[pl.ds((4 * px + 2 * py + pc) * m, m), :]

        def copy(k, block, to, src=None):
            return pltpu.make_async_remote_copy(
                src_ref=rows(*block) if src is None else src, dst_ref=rows(*block), send_sem=send_sems.at[k],
                recv_sem=recv_sems.at[k], device_id=to, device_id_type=MESH)

        mine = pltpu.make_async_copy(v_ref, rows(*me), local_sem)
        mine.start()
        first = [copy(0, me, sibling, src=v_ref)]
        first += [copy(1 + j, me, (*chip, c), src=v_ref) for j, chip in enumerate(chips)]
        for cp in first:
            cp.start()
        passed = [copy(4 + j, (*chip, c), sibling) for j, chip in enumerate(chips)]
        for j, chip in enumerate(chips):
            copy(1 + j, (*chip, c), me).wait_recv()
            passed[j].start()
        copy(0, sibling, me).wait_recv()
        for j, chip in enumerate(chips):
            copy(4 + j, (*chip, 1 - c), me).wait_recv()
        for cp in first + passed:
            cp.wait_send()
        mine.wait()

    return pl.pallas_call(
        body, name=name, out_shape=jax.ShapeDtypeStruct((N_DEV * m, n), v.dtype), in_specs=[VMEM_SPEC], out_specs=VMEM_SPEC,
        scratch_shapes=[pltpu.SemaphoreType.DMA((7,)), pltpu.SemaphoreType.DMA((7,)), pltpu.SemaphoreType.DMA],
    )(v)


def _gather_weight_shards(shards, *, name):
    nw = len(shards)

    def body(*refs):
        srcs, outs, (send_sems, recv_sems) = refs[:nw], refs[nw:2 * nw], refs[2 * nw:]
        x, y, c, sibling, chips = _place()
        index = lambda chip: 2 * chip[0] + chip[1]

        def copy(w, k, src, dst, to):
            return pltpu.make_async_remote_copy(src_ref=src, dst_ref=dst, send_sem=send_sems.at[6 * w + k],
                                                recv_sem=recv_sems.at[6 * w + k], device_id=to, device_id_type=MESH)

        sent = []
        for w, (src_ref, out_ref) in enumerate(zip(srcs, outs)):
            for k, chip in enumerate(chips):
                sent.append(copy(w, k, src_ref.at[c], out_ref.at[2 * x + y, c], (*chip, c)))
                sent[-1].start()
        for w, out_ref in enumerate(outs):
            for k, chip in enumerate(chips):
                landed = out_ref.at[index(chip), c]
                copy(w, k, landed, landed, (*chip, c)).wait_recv()
                sent.append(copy(w, 3 + k, landed, landed, sibling))
                sent[-1].start()
        for w, out_ref in enumerate(outs):
            for k, chip in enumerate(chips):
                passed_on = out_ref.at[index(chip), 1 - c]
                copy(w, 3 + k, passed_on, passed_on, sibling).wait_recv()
        for cp in sent:
            cp.wait_send()

    return pl.pallas_call(
        body, name=name, out_shape=[jax.ShapeDtypeStruct((N_CHIP, *s.shape), s.dtype) for s in shards],
        in_specs=[ANY] * nw, out_specs=[ANY] * nw,
        scratch_shapes=[pltpu.SemaphoreType.DMA((6 * nw,)), pltpu.SemaphoreType.DMA((6 * nw,))],
    )(*shards)


HBM_SPEC = pl.BlockSpec(memory_space=pltpu.HBM)
SEM_SPEC = pl.BlockSpec(memory_space=pltpu.SEMAPHORE)
DATAFLOW_EFFECT = pltpu.SideEffectType.DATAFLOW_SIDE_EFFECTING


def _late_copies(srcs, lands, send_sems, recv_sems):
    x, y, c, _, chips = _place()
    return [pltpu.make_async_remote_copy(
        src_ref=src.at[c], dst_ref=land.at[2 * x + y, c], send_sem=send_sems.at[6 * w + 2 * r + core],
        recv_sem=recv_sems.at[6 * w + 2 * r + c], device_id=(*chip, core), device_id_type=MESH)
        for w, (src, land) in enumerate(zip(srcs, lands)) for r, chip in enumerate(chips) for core in range(2)]


def _gather_late_start(own, after, *, name):
    nw = len(own)

    def body(*refs):
        srcs, lands, send_sems, recv_sems, token = refs[:nw], refs[nw:2 * nw], refs[2 * nw + 1], refs[2 * nw + 2], refs[-1]
        for cp in _late_copies(srcs, lands, send_sems, recv_sems):
            cp.start()
        token[...] = jnp.zeros_like(token)

    lands = [pltpu.with_memory_space_constraint(lax.empty((N_CHIP, *s.shape), s.dtype), pltpu.HBM) for s in own]
    own = [pltpu.with_memory_space_constraint(s, pltpu.HBM) for s in own]
    out = pl.pallas_call(
        body, name=name,
        out_shape=(pltpu.SemaphoreType.DMA((6 * nw,)), pltpu.SemaphoreType.DMA((6 * nw,)),
                   *[pltpu.HBM(s.shape, s.dtype) for s in own], *[pltpu.HBM(s.shape, s.dtype) for s in lands],
                   jax.ShapeDtypeStruct((8, 128), F32)),
        in_specs=[HBM_SPEC] * (2 * nw) + [ANY], out_specs=(SEM_SPEC, SEM_SPEC, *[HBM_SPEC] * (2 * nw), VMEM_SPEC),
        input_output_aliases={i: 2 + i for i in range(2 * nw)},
        compiler_params=pltpu.CompilerParams(has_side_effects=DATAFLOW_EFFECT))(*own, *lands, after)
    return out[0], out[1], out[2:2 + nw], out[2 + nw:2 + 2 * nw], out[-1]


def _gather_late_wait(send_sems, recv_sems, own, lands, after, *, name):
    nw = len(own)

    def body(*refs):
        srcs, lands_in, send_sems, recv_sems = refs[:nw], refs[nw:2 * nw], refs[2 * nw], refs[2 * nw + 1]
        x, y, c, _, chips = _place()
        for cp in _late_copies(srcs, lands_in, send_sems, recv_sems):
            cp.wait_send()
        for w, (src, land) in enumerate(zip(srcs, lands_in)):
            for r, chip in enumerate(chips):
                for core in range(2):
                    pltpu.make_async_remote_copy(
                        src_ref=src.at[c], dst_ref=land.at[2 * chip[0] + chip[1], core], send_sem=send_sems.at[6 * w + 2 * r + core],
                        recv_sem=recv_sems.at[6 * w + 2 * r + core], device_id=(*chip, core), device_id_type=MESH).wait_recv()

    out = pl.pallas_call(
        body, name=name, out_shape=(*[pltpu.HBM(s.shape, s.dtype) for s in own], *[pltpu.HBM(s.shape, s.dtype) for s in lands]),
        in_specs=[HBM_SPEC] * (2 * nw) + [SEM_SPEC, SEM_SPEC, ANY], out_specs=tuple([HBM_SPEC] * (2 * nw)),
        input_output_aliases={i: i for i in range(2 * nw)},
        compiler_params=pltpu.CompilerParams(has_side_effects=DATAFLOW_EFFECT))(*own, *lands, send_sems, recv_sems, after)
    return out[:nw], out[nw:]


def _direct_reduce_copies(srcs, lands, send_sems, recv_sems):
    x, y, c, _, _ = _place()
    cps = []
    for w, (src, land) in enumerate(zip(srcs, lands)):
        for rel in range(1, N_DEV):
            tx, ty, tc = (1 - x if rel & 4 else x), (1 - y if rel & 2 else y), (1 - c if rel & 1 else c)
            cps.append(pltpu.make_async_remote_copy(
                src_ref=src.at[2 * tx + ty, tc], dst_ref=land.at[rel - 1], send_sem=send_sems.at[7 * w + rel - 1],
                recv_sem=recv_sems.at[7 * w + rel - 1], device_id=(tx, ty, tc), device_id_type=MESH))
    return cps


def _direct_reduce_start(grads, *, name):
    nw = len(grads)

    def body(*refs):
        srcs, lands, send_sems, recv_sems, token = refs[:nw], refs[nw:2 * nw], refs[2 * nw], refs[2 * nw + 1], refs[-1]
        for cp in _direct_reduce_copies(srcs, lands, send_sems, recv_sems):
            cp.start()
        token[...] = jnp.zeros_like(token)

    lands = [pltpu.with_memory_space_constraint(lax.empty((N_DEV - 1, *g.shape[2:]), g.dtype), pltpu.HBM) for g in grads]
    grads = [pltpu.with_memory_space_constraint(g, pltpu.HBM) for g in grads]
    out = pl.pallas_call(
        body, name=name,
        out_shape=(pltpu.SemaphoreType.DMA((7 * nw,)), pltpu.SemaphoreType.DMA((7 * nw,)),
                   *[pltpu.HBM(g.shape, g.dtype) for g in grads], *[pltpu.HBM(t.shape, t.dtype) for t in lands],
                   jax.ShapeDtypeStruct((8, 128), F32)),
        in_specs=[HBM_SPEC] * (2 * nw), out_specs=(SEM_SPEC, SEM_SPEC, *[HBM_SPEC] * (2 * nw), VMEM_SPEC),
        input_output_aliases={i: 2 + i for i in range(2 * nw)},
        compiler_params=pltpu.CompilerParams(has_side_effects=DATAFLOW_EFFECT))(*grads, *lands)
    return out[0], out[1], out[2:2 + nw], out[2 + nw:2 + 2 * nw], out[-1]


def _direct_reduce_wait(send_sems, recv_sems, grads, lands, after, *, name):
    nw = len(grads)

    def body(*refs):
        srcs, lands_in, send_sems, recv_sems = refs[:nw], refs[nw:2 * nw], refs[2 * nw], refs[2 * nw + 1]
        cps = _direct_reduce_copies(srcs, lands_in, send_sems, recv_sems)
        for cp in cps:
            cp.wait_send()
        for cp in cps:
            cp.wait_recv()

    out = pl.pallas_call(
        body, name=name, out_shape=(*[pltpu.HBM(g.shape, g.dtype) for g in grads], *[pltpu.HBM(t.shape, t.dtype) for t in lands]),
        in_specs=[HBM_SPEC] * (2 * nw) + [SEM_SPEC, SEM_SPEC, ANY], out_specs=tuple([HBM_SPEC] * (2 * nw)),
        input_output_aliases={i: i for i in range(2 * nw)},
        compiler_params=pltpu.CompilerParams(has_side_effects=DATAFLOW_EFFECT))(*grads, *lands, send_sems, recv_sems, after)
    return out[nw:]


def _direct_reduce_add(grad, landed, chip, core, *, name):
    _, r, n = grad.shape
    half = r // 2
    tr = _row_tile(half)
    nb = half // tr

    def body(chip_ref, core_ref, g_ref, t_ref, o_ref):
        acc = g_ref[0]
        for k in range(N_DEV - 1):
            acc = acc + t_ref[k].astype(F32)
        o_ref[...] = acc

    return pl.pallas_call(
        body, name=name,
        grid_spec=pltpu.PrefetchScalarGridSpec(
            num_scalar_prefetch=2, grid=(nb,),
            in_specs=[pl.BlockSpec((1, tr, n), lambda i, chip_ref, core_ref: (chip_ref[0], core_ref[0] * nb + i, 0)),
                      pl.BlockSpec((N_DEV - 1, tr, n), lambda i, chip_ref, core_ref: (0, i, 0))],
            out_specs=pl.BlockSpec((tr, n), lambda i, chip_ref, core_ref: (i, 0))),
        out_shape=jax.ShapeDtypeStruct((half, n), F32), compiler_params=_params("parallel"))(chip, core, grad, landed)


def _share_halves(halves, *, name):
    nw = len(halves)

    def body(*refs):
        srcs, outs, (send_sems, recv_sems) = refs[:nw], refs[nw:2 * nw], refs[2 * nw:]
        _, _, _, sibling, _ = _place()
        cps = [pltpu.make_async_remote_copy(src_ref=src_ref, dst_ref=out_ref, send_sem=send_sems.at[w], recv_sem=recv_sems.at[w],
                                            device_id=sibling, device_id_type=MESH)
               for w, (src_ref, out_ref) in enumerate(zip(srcs, outs))]
        for cp in cps:
            cp.start()
        for cp in cps:
            cp.wait()

    return pl.pallas_call(
        body, name=name, out_shape=[jax.ShapeDtypeStruct(h.shape, h.dtype) for h in halves],
        in_specs=[ANY] * nw, out_specs=[ANY] * nw,
        scratch_shapes=[pltpu.SemaphoreType.DMA((nw,)), pltpu.SemaphoreType.DMA((nw,))])(*halves)


def _row_tile(rows, limit=256):
    return next(t for t in range(limit, 15, -16) if rows % t == 0)


def _sum_devices(gathered, *, name):
    _, m, n = gathered.shape

    def body(g_ref, tot_ref, loss_ref):
        tot = g_ref[0]
        for dev in range(1, N_DEV):
            tot = tot + g_ref[dev]
        tot_ref[...] = tot
        loss_ref[...] = jnp.full((8, n), (0.5 / D_MODEL) * jnp.sum(tot[0:8]), F32)

    return pl.pallas_call(body, name=name, in_specs=[VMEM_SPEC], out_specs=[VMEM_SPEC, VMEM_SPEC],
                          out_shape=[jax.ShapeDtypeStruct((m, n), F32), jax.ShapeDtypeStruct((8, n), F32)])(gathered)


def _ada_mod(cond_all, w_ada_shard, *, name):
    tn = 512

    def body(a_ref, b_ref, o_ref):
        o_ref[...] = _nn(a_ref[...], b_ref[...], precision=HIGHEST)

    return pl.pallas_call(
        body, name=name, grid=(w_ada_shard.shape[1] // tn,),
        in_specs=[pl.BlockSpec(cond_all.shape, lambda j: (0, 0)), pl.BlockSpec((D_MODEL, tn), lambda j: (0, j))],
        out_specs=pl.BlockSpec((N_DEV, tn), lambda j: (0, j)),
        out_shape=jax.ShapeDtypeStruct((N_DEV, w_ada_shard.shape[1]), F32), compiler_params=_params("parallel"))(cond_all, w_ada_shard)


def _ada_grad(cond_all, dmod_cols, *, name):
    tm = 256

    def body(a_ref, b_ref, o_ref):
        o_ref[...] = lax.dot_general(a_ref[...], b_ref[...], (((0,), (0,)), ((), ())), precision=HIGHEST,
                                     preferred_element_type=F32)

    return pl.pallas_call(
        body, name=name, grid=(D_MODEL // tm,),
        in_specs=[pl.BlockSpec((N_DEV, tm), lambda i: (0, i)), pl.BlockSpec(dmod_cols.shape, lambda i: (0, 0))],
        out_specs=pl.BlockSpec((tm, dmod_cols.shape[1]), lambda i: (i, 0)),
        out_shape=jax.ShapeDtypeStruct((D_MODEL, dmod_cols.shape[1]), F32), compiler_params=_params("parallel"))(cond_all, dmod_cols)


def _silu_rows(c8, *, name):
    def body(c_ref, o_ref):
        cv = c_ref[...]
        o_ref[...] = cv * _sigmoid(cv)

    return pl.pallas_call(body, name=name, in_specs=[VMEM_SPEC], out_specs=VMEM_SPEC,
                          out_shape=jax.ShapeDtypeStruct(c8.shape, F32))(c8)


def _rows128(t, rows=None):
    flat = t.reshape(-1, 128)
    return flat if rows is None else jnp.pad(flat, ((0, rows - flat.shape[0]), (0, 0)))


def _from_col_shards(shards, r, n):
    return shards.reshape(N_CHIP, r, n).transpose(1, 0, 2).reshape(r, N_CHIP * n)


def kernel(x, c, w_ada, b_ada, norm1_g, w_in, gla_w_gate, gla_b_gate, gla_norm_g, q_norm_g, k_norm_g, w_out, norm2_g, w_up, conv_w, conv_b, w_down, loss_target, m_w_ada, m_b_ada, m_norm1_g, m_w_in, m_gla_w_gate, m_gla_b_gate, m_gla_norm_g, m_q_norm_g, m_k_norm_g, m_w_out, m_norm2_g, m_w_up, m_conv_w, m_conv_b, m_w_down, v_w_ada, v_b_ada, v_norm1_g, v_w_in, v_gla_w_gate, v_gla_b_gate, v_gla_norm_g, v_q_norm_g, v_k_norm_g, v_w_out, v_norm2_g, v_w_up, v_conv_w, v_conv_b, v_w_down):
    d = D_MODEL
    ax, ay, ac = lax.axis_index("x"), lax.axis_index("y"), lax.axis_index("c")
    chip, dev = 2 * ax + ay, 4 * ax + 2 * ay + ac

    cond = _silu_rows(jnp.broadcast_to(c, (8, d)), name="cond_silu")[0:1]
    small_in = jnp.concatenate([_rows128(cond), _rows128(conv_w[0]), _rows128(gla_w_gate[0])], axis=0)
    small_in = _rows128(small_in, 56)
    got = _all_gather_small(small_in, name="gather_small").reshape(N_DEV, 56, 128)
    cond_all = got[:, 0:8].reshape(N_DEV, d)
    conv_w_full = _from_col_shards(got[0::2, 8:41].reshape(N_CHIP, 3 * 1408 // 128, 128), 3, 1408)
    gate_full = _from_col_shards(got[0::2, 41:49].reshape(N_CHIP, 16 * 64 // 128, 128), GLA_GATE_RANK, 64)
    mod_part = _ada_mod(cond_all, w_ada[0], name="ada_mod")
    mod_got = _all_gather_small(_rows128(mod_part), name="gather_mod").reshape(N_DEV, N_DEV, 1536)
    mod_all = mod_got[0::2].transpose(1, 0, 2).reshape(N_DEV, 6 * d) + b_ada
    mod = lax.dynamic_slice_in_dim(mod_all, dev, 1, axis=0)

    own = [w[0].astype(BF16).reshape(2, w.shape[1] // 2, w.shape[2]) for w in (w_in, w_out, w_up, w_down)]
    with_own = lambda got, mine: [lax.dynamic_update_index_in_dim(t, o, chip, 0) for t, o in zip(got, mine)]
    got_in, got_out = with_own(_gather_weight_shards(own[:2], name="gather_weights"), own[:2])
    w_in_full = got_in.reshape(N_CHIP, d, 772).transpose(1, 0, 2).reshape(d, N_CHIP * 772)
    w_out_full = got_out.reshape(d, d)
    exchanged = mod_all[0:1, 0:1] + got_in[0, 0, 0:1, 0:1].astype(F32)
    send_sems, recv_sems, own_thru, lands, token = _gather_late_start(own[2:], exchanged, name="gather_late_start")
    mod = mod + token[0:1, 0:1]

    def ffn_weights(after):
        mine, landed = _gather_late_wait(send_sems, recv_sems, own_thru, lands, after, name="gather_late_wait")
        got_up, got_down = with_own(landed, mine)
        return got_up.reshape(N_CHIP, d, 1408).transpose(1, 0, 2).reshape(d, 2 * D_FF), got_down.reshape(D_FF, d)

    ffn_reduce, attn_reduce, attn_parts = [], [], []
    halves_of = lambda g: g.reshape(N_CHIP, 2, g.shape[-2] // 2, g.shape[-1])

    def ffn_grads_ready(g_wup_b, g_wdown_b):
        ffn_reduce.extend(_direct_reduce_start([halves_of(g_wup_b), halves_of(g_wdown_b.reshape(N_CHIP, D_FF // N_CHIP, d))],
                                               name="reduce_ffn_start"))
        return ffn_reduce[4]

    def attn_grads_ready(g_wi, g_wo):
        attn_parts.extend([_in_proj_grad_layout(g_wi).reshape(d, N_CHIP, 772).transpose(1, 0, 2), g_wo.reshape(N_CHIP, d // N_CHIP, d)])
        attn_reduce.extend(_direct_reduce_start([halves_of(g.astype(BF16)) for g in attn_parts], name="reduce_attn_start"))
        return attn_reduce[4]

    err2, grad_x, (g_wi, g_wo, g_wup, g_wdown), small = _local_step(
        x[0], loss_target[0], mod, _in_proj_layout(w_in_full), w_out_full, ffn_weights, ffn_grads_ready, attn_grads_ready,
        conv_w_full, conv_b,
        _gate_layout(gate_full), gla_b_gate, gla_norm_g, q_norm_g, k_norm_g, norm1_g, norm2_g)

    pieces = [err2[0], small["dmod"], small["norm1_g"], small["norm2_g"], small["gla_w_gate"].reshape(-1), small["gla_b_gate"],
              small["gla_norm_g"], small["q_norm_g"], small["k_norm_g"], small["conv_w"].reshape(-1), small["conv_b"]]
    sizes = [p.shape[0] for p in pieces]
    at = [sum(sizes[:i]) for i in range(len(sizes) + 1)]
    vec = _rows128(jnp.concatenate(pieces), 288)
    got = _all_gather_small(vec, name="gather_grads").reshape(N_DEV, 288, 128)
    total, loss8 = _sum_devices(got, name="sum_devices")
    total = total.reshape(-1)
    seg = lambda i: total[at[i]:at[i + 1]]
    dmod_all = got.reshape(N_DEV, -1)[:, at[1]:at[2]]
    g_small = dict(
        b_ada=seg(1)[None], norm1_g=seg(2)[None], norm2_g=seg(3)[None],
        gla_w_gate=lax.dynamic_slice_in_dim(seg(4).reshape(GLA_GATE_RANK, 256), chip * 64, 64, axis=1),
        gla_b_gate=seg(5)[None], gla_norm_g=seg(6)[None], q_norm_g=seg(7)[None], k_norm_g=seg(8)[None],
        conv_w=lax.dynamic_slice_in_dim(seg(9).reshape(3, 2 * D_FF), chip * 1408, 1408, axis=1), conv_b=seg(10)[None])
    dmod_cols = lax.dynamic_slice_in_dim(dmod_all.reshape(N_DEV, 6 * d), chip * 1536, 1536, axis=1)
    g_w_ada = _ada_grad(cond_all, dmod_cols, name="ada_grad")

    core_id, chip_id = jnp.reshape(ac, (1,)).astype(jnp.int32), jnp.reshape(chip, (1,)).astype(jnp.int32)
    landed = (_direct_reduce_wait(*attn_reduce[:4], grad_x, name="reduce_attn_wait")
              + _direct_reduce_wait(*ffn_reduce[:4], grad_x, name="reduce_ffn_wait"))
    own = attn_parts + [g_wup, g_wdown.reshape(N_CHIP, D_FF // N_CHIP, d)]
    summed = [_direct_reduce_add(g, t, chip_id, core_id, name=f"reduce_add_{tag}")
              for g, t, tag in zip(own, landed, ("w_in", "w_out", "w_up", "w_down"))]
    others = _share_halves(summed, name="share_pair")
    g_big = [jnp.concatenate([jnp.where(ac == 0, mine, other), jnp.where(ac == 0, other, mine)], axis=0)
             for mine, other in zip(summed, others)]

    grads = dict(w_ada=g_w_ada, w_in=g_big[0], w_out=g_big[1], w_up=g_big[2], w_down=g_big[3], **g_small)
    names = ["w_ada", "b_ada", "norm1_g", "w_in", "gla_w_gate", "gla_b_gate", "gla_norm_g", "q_norm_g", "k_norm_g", "w_out",
             "norm2_g", "w_up", "conv_w", "conv_b", "w_down"]
    ws = dict(w_ada=w_ada, b_ada=b_ada, norm1_g=norm1_g, w_in=w_in, gla_w_gate=gla_w_gate, gla_b_gate=gla_b_gate,
              gla_norm_g=gla_norm_g, q_norm_g=q_norm_g, k_norm_g=k_norm_g, w_out=w_out, norm2_g=norm2_g, w_up=w_up,
              conv_w=conv_w, conv_b=conv_b, w_down=w_down)
    ms = dict(w_ada=m_w_ada, b_ada=m_b_ada, norm1_g=m_norm1_g, w_in=m_w_in, gla_w_gate=m_gla_w_gate, gla_b_gate=m_gla_b_gate,
              gla_norm_g=m_gla_norm_g, q_norm_g=m_q_norm_g, k_norm_g=m_k_norm_g, w_out=m_w_out, norm2_g=m_norm2_g, w_up=m_w_up,
              conv_w=m_conv_w, conv_b=m_conv_b, w_down=m_w_down)
    vs = dict(w_ada=v_w_ada, b_ada=v_b_ada, norm1_g=v_norm1_g, w_in=v_w_in, gla_w_gate=v_gla_w_gate, gla_b_gate=v_gla_b_gate,
              gla_norm_g=v_gla_norm_g, q_norm_g=v_q_norm_g, k_norm_g=v_k_norm_g, w_out=v_w_out, norm2_g=v_norm2_g, w_up=v_w_up,
              conv_w=v_conv_w, conv_b=v_conv_b, w_down=v_w_down)
    g_out, d_out, m_out, v_out = [], [], [], []
    for nm in names:
        shape = ws[nm].shape
        flip = (lambda t: t.T) if shape[-1] % 128 and shape[-2] % 128 == 0 else (lambda t: t)
        w2 = flip(ws[nm].reshape(shape[-2:]))
        g2 = flip(grads[nm].reshape(shape[-2:]))
        dl, mn, vn = _adamw(w2, g2, flip(ms[nm].reshape(shape[-2:])), flip(vs[nm].reshape(shape[-2:])), name=f"adamw_{nm}")
        for outs, t in ((g_out, g2), (d_out, dl), (m_out, mn), (v_out, vn)):
            outs.append(flip(t).reshape(shape))
    return (loss8[0, 0], grad_x[None], *g_out, *d_out, *m_out, *v_out)
```

```python
import functools

import jax
import jax.numpy as jnp
from jax import lax
from jax.experimental import pallas as pl
from jax.experimental.pallas import tpu as pltpu

F32, BF16 = jnp.float32, jnp.bfloat16
HIGHEST = lax.Precision.HIGHEST
MESH = pl.DeviceIdType.MESH

D_MODEL = 1024
GLA_CHUNK = 64
GLA_GATE_TAU = 16.0
GLA_GATE_RANK = 16
HEAD_LANES = 128
ATTN_BLOCK = 128
DILATIONS = (1, 4, 16)
ALIBI_SLOPES = tuple(2.0 ** (-(h + 1)) for h in range(8))
D_FF = 2816
EPS = 1e-6
C_GQ, C_GK, C_GV, C_GR, C_AQ, C_AK, C_AV, C_LR, PROJ_W = 0, 256, 512, 1024, 1536, 2048, 2560, 3072, 3200
ADAM_LR, ADAM_B1, ADAM_B2, ADAM_EPS, ADAM_WD, ADAM_STEP = 0.001, 0.9, 0.999, 1e-08, 0.01, 10
VMEM_LIMIT_BYTES = 56 * 1024 * 1024
ROW_TILE = 256


def _params(*sem):
    return pltpu.CompilerParams(dimension_semantics=sem or None, vmem_limit_bytes=VMEM_LIMIT_BYTES)


def _nt(a, b):
    return lax.dot_general(a, b, (((1,), (1,)), ((), ())), preferred_element_type=F32)


def _tn(a, b):
    return lax.dot_general(a, b, (((0,), (0,)), ((), ())), preferred_element_type=F32)


def _nn(a, b, precision=None):
    return jnp.dot(a, b, preferred_element_type=F32, precision=precision)


def _split3(v):
    hi = v.astype(BF16)
    rest = v - hi.astype(F32)
    mid = rest.astype(BF16)
    return hi, mid, (rest - mid.astype(F32)).astype(BF16)


def _sum_right(v, ones):
    hi, mid, lo = _split3(v)
    return (_nn(lo, ones) + _nn(mid, ones)) + _nn(hi, ones)


def _sum_left(ones, v):
    hi, mid, lo = _split3(v)
    return (_nn(ones, lo) + _nn(ones, mid)) + _nn(ones, hi)


def _fold8(v):
    return v.reshape(v.shape[0] // 8, 8, v.shape[1]).sum(axis=0)


def _spread_total(ref):
    t = ref[...]
    ref[...] = jnp.broadcast_to(jnp.sum(t, axis=-2, keepdims=True), t.shape)


def _sigmoid(x):
    return 1.0 / (1.0 + jnp.exp(-x))


def _mm(a, b, *, ta=False, tb=False, out_dtype=F32, tm, tn, tk, shard_cols=False, also_bf16=False, name):
    (k_a, m) = a.shape if ta else a.shape[::-1]
    (k_b, n) = b.shape[::-1] if tb else b.shape
    assert k_a == k_b and m % tm == 0 and n % tn == 0 and k_a % tk == 0, (name, a.shape, b.shape)
    nk = k_a // tk
    assert nk == 1 or out_dtype == F32, name
    dims = (((0 if ta else 1,), (1 if tb else 0,)), ((), ()))

    def body(a_ref, b_ref, o_ref, *rounded):
        k = pl.program_id(2)
        part = lax.dot_general(a_ref[...].astype(BF16), b_ref[...].astype(BF16), dims, preferred_element_type=F32)
        if nk == 1:
            o_ref[...] = part.astype(out_dtype)
        else:
            @pl.when(k == 0)
            def _():
                o_ref[...] = part

            @pl.when(k > 0)
            def _():
                o_ref[...] += part

        if also_bf16:
            @pl.when(k == nk - 1)
            def _():
                rounded[0][...] = o_ref[...].astype(BF16)

    a_spec = pl.BlockSpec((tk, tm), lambda i, j, k: (k, i)) if ta else pl.BlockSpec((tm, tk), lambda i, j, k: (i, k))
    b_spec = pl.BlockSpec((tn, tk), lambda i, j, k: (j, k)) if tb else pl.BlockSpec((tk, tn), lambda i, j, k: (k, j))
    if shard_cols:
        o_spec, o_shape = pl.BlockSpec((None, tm, tn), lambda i, j, k: (j, i, 0)), (n // tn, m, tn)
    else:
        o_spec, o_shape = pl.BlockSpec((tm, tn), lambda i, j, k: (i, j)), (m, n)
    shapes = [jax.ShapeDtypeStruct(o_shape, out_dtype)] + ([jax.ShapeDtypeStruct(o_shape, BF16)] if also_bf16 else [])
    out = pl.pallas_call(
        body, name=name, grid=(m // tm, n // tn, nk), in_specs=[a_spec, b_spec], out_specs=[o_spec] * len(shapes),
        out_shape=shapes, compiler_params=_params("parallel", "parallel", "arbitrary"))(a, b)
    return out if also_bf16 else out[0]


def _norm_mod_fwd(x, branch, gate, gain, scale, shift, *, name):
    s, d = x.shape
    tm = ROW_TILE
    has_branch = branch is not None

    def body(*refs):
        if has_branch:
            x_ref, br_ref, gate_ref, gain_ref, sc_ref, sh_ref, x1_ref, h_ref, ht_ref = refs
            xv = x_ref[...] + gate_ref[...] * br_ref[...]
            x1_ref[...] = xv
        else:
            x_ref, gain_ref, sc_ref, sh_ref, h_ref, ht_ref = refs
            xv = x_ref[...]
        r = lax.rsqrt(jnp.mean(xv * xv, axis=-1, keepdims=True) + EPS)
        h = (xv * r) * gain_ref[...] * (1.0 + sc_ref[...]) + sh_ref[...]
        h_ref[...] = h.astype(BF16)
        ht_ref[...] = h.T.astype(BF16)

    row = pl.BlockSpec((tm, d), lambda i: (i, 0))
    col = pl.BlockSpec((d, tm), lambda i: (0, i))
    vec = pl.BlockSpec((1, d), lambda i: (0, 0))
    h_shapes = [jax.ShapeDtypeStruct((s, d), BF16), jax.ShapeDtypeStruct((d, s), BF16)]
    if has_branch:
        return pl.pallas_call(
            body, name=name, grid=(s // tm,), in_specs=[row, row, vec, vec, vec, vec], out_specs=[row, row, col],
            out_shape=[jax.ShapeDtypeStruct((s, d), F32)] + h_shapes,
            compiler_params=_params("parallel"))(x, branch, gate, gain, scale, shift)
    h, ht = pl.pallas_call(
        body, name=name, grid=(s // tm,), in_specs=[row, vec, vec, vec], out_specs=[row, col],
        out_shape=h_shapes, compiler_params=_params("parallel"))(x, gain, scale, shift)
    return x, h, ht


def _norm_mod_bwd(x, dh, dres, gain, scale, branch, gate, *, name):
    s, d = x.shape
    tm = ROW_TILE
    has_branch = branch is not None

    def body(*refs):
        if has_branch:
            x_ref, dh_ref, dres_ref, gain_ref, sc_ref, br_ref, gate_ref, dx_ref, dbr_ref, sums_ref = refs
        else:
            x_ref, dh_ref, dres_ref, gain_ref, sc_ref, dx_ref, sums_ref = refs
        i = pl.program_id(0)

        @pl.when(i == 0)
        def _():
            sums_ref[...] = jnp.zeros_like(sums_ref)

        xv, dhv = x_ref[...], dh_ref[...]
        r = lax.rsqrt(jnp.mean(xv * xv, axis=-1, keepdims=True) + EPS)
        xn = xv * r
        dxn = dhv * (gain_ref[...] * (1.0 + sc_ref[...]))
        dx = dres_ref[...] + r * (dxn - xn * jnp.mean(dxn * xn, axis=-1, keepdims=True))
        dx_ref[...] = dx
        sums_ref[0] += _fold8(dhv * xn)
        sums_ref[1] += _fold8(dhv)
        if has_branch:
            dbr_ref[...] = (gate_ref[...] * dx).astype(BF16)
            sums_ref[2] += _fold8(dx * br_ref[...])

        @pl.when(i == s // tm - 1)
        def _():
            _spread_total(sums_ref)

    row = pl.BlockSpec((tm, d), lambda i: (i, 0))
    vec = pl.BlockSpec((1, d), lambda i: (0, 0))
    sums = pl.BlockSpec((3, 8, d), lambda i: (0, 0, 0))
    sums_shape = jax.ShapeDtypeStruct((3, 8, d), F32)
    if has_branch:
        return pl.pallas_call(
            body, name=name, grid=(s // tm,), in_specs=[row, row, row, vec, vec, row, vec], out_specs=[row, row, sums],
            out_shape=[jax.ShapeDtypeStruct((s, d), F32), jax.ShapeDtypeStruct((s, d), BF16), sums_shape],
            compiler_params=_params("arbitrary"))(x, dh, dres, gain, scale, branch, gate)
    dx, sm = pl.pallas_call(
        body, name=name, grid=(s // tm,), in_specs=[row, row, row, vec, vec], out_specs=[row, sums],
        out_shape=[jax.ShapeDtypeStruct((s, d), F32), sums_shape],
        compiler_params=_params("arbitrary"))(x, dh, dres, gain, scale)
    return dx, None, sm


GLA_ROWS = 256


def _gla_block_setup(lr_ref, wg_ref, bg_ref):
    t, c = GLA_ROWS, GLA_CHUNK
    ri = lax.broadcasted_iota(jnp.int32, (t, t), 0)
    ci = lax.broadcasted_iota(jnp.int32, (t, t), 1)
    same = (ri // c) == (ci // c)
    causal, upper = same & (ci <= ri), same & (ci >= ri)
    z = _nn(lr_ref[...].astype(BF16), wg_ref[...]) + bg_ref[...]
    g = (jnp.minimum(z, 0.0) - jnp.log(1.0 + jnp.exp(-jnp.abs(z)))) * (1.0 / GLA_GATE_TAU)
    hi, mid, lo = _split3(g)
    total = lambda ones: (_nn(ones, lo) + _nn(ones, mid)) + _nn(ones, hi)
    return z, total(causal.astype(BF16)), total(same.astype(BF16)), causal, upper


def _chunks(t):
    return [t[i * GLA_CHUNK:(i + 1) * GLA_CHUNK] for i in range(GLA_ROWS // GLA_CHUNK)]


def _gla_fwd(proj, wg, bg, gn, *, name):
    s = proj.shape[0]
    tb, c = GLA_ROWS, GLA_CHUNK
    cb = tb // c

    def body(q_ref, k_ref, v_ref, r_ref, lr_ref, wg_ref, bg_ref, gn_ref, o_ref, y_ref, st_ref, state):
        i = pl.program_id(0)

        @pl.when(i == 0)
        def _():
            state[...] = jnp.zeros_like(state)

        low = lax.broadcasted_iota(jnp.int32, (tb, HEAD_LANES), 1) < 64
        masks = (low, jnp.logical_not(low))
        _, b, b_end, causal, _ = _gla_block_setup(lr_ref, wg_ref, bg_ref)
        pairs = []
        for p in range(2):
            cols = pl.ds(p * HEAD_LANES, HEAD_LANES)
            bp, bep = (t[:, p * HEAD_LANES:(p + 1) * HEAD_LANES] for t in (b, b_end))
            k = k_ref[:, cols]
            q_in = q_ref[:, cols] * 0.125 * jnp.exp(bp)
            k_out = (k * jnp.exp(-bp)).astype(BF16)
            k_end = k * jnp.exp(bep - bp)
            qms = [jnp.where(m, q_in, 0.0).astype(BF16) for m in masks]
            kes = [jnp.where(m, k_end, 0.0).astype(BF16) for m in masks]
            vs = [v_ref[:, pl.ds((2 * p + e) * HEAD_LANES, HEAD_LANES)].astype(BF16) for e in range(2)]
            grow = [_tn(v0, k0) + _tn(v1, k1) for v0, k0, v1, k1 in zip(_chunks(vs[0]), _chunks(kes[0]), _chunks(vs[1]), _chunks(kes[1]))]
            pairs.append((bep, k_out, qms, vs, grow))
        entering = [[], []]
        for p, (bep, _, _, _, grow) in enumerate(pairs):
            st = state[p]
            for ch in range(cb):
                entering[p].append(st)
                st_ref[ch, p] = st
                st = st * jnp.exp(bep[ch * c:ch * c + 1, :]) + grow[ch]
            state[p] = st
        for p, (_, k_out, qms, vs, _) in enumerate(pairs):
            for e in range(2):
                hc = pl.ds((2 * p + e) * HEAD_LANES, HEAD_LANES)
                a = jnp.where(causal, _nt(qms[e], k_out), 0.0).astype(BF16)
                carried = jnp.concatenate([_nt(qc, sc.astype(BF16)) for qc, sc in zip(_chunks(qms[e]), entering[p])], axis=0)
                o = _nn(a, vs[e]) + carried
                o_ref[:, hc] = o
                rr = r_ref[:, hc]
                on = o * lax.rsqrt(jnp.mean(o * o, axis=-1, keepdims=True) + EPS)
                y_ref[:, hc] = (on * gn_ref[...] * (rr * _sigmoid(rr))).astype(BF16)

    def col(width, at):
        return pl.BlockSpec((tb, width), lambda i: (i, at // width))

    full = lambda shape: pl.BlockSpec(shape, lambda i: tuple(0 for _ in shape))
    return pl.pallas_call(
        body, name=name, grid=(s // tb,),
        in_specs=[col(256, C_GQ), col(256, C_GK), col(512, C_GV), col(512, C_GR), col(128, C_LR),
                  full((HEAD_LANES, 256)), full((1, 256)), full((1, HEAD_LANES))],
        out_specs=[pl.BlockSpec((tb, 512), lambda i: (i, 0)), pl.BlockSpec((tb, 512), lambda i: (i, 0)),
                   pl.BlockSpec((cb, 2, HEAD_LANES, HEAD_LANES), lambda i: (i, 0, 0, 0))],
        out_shape=[jax.ShapeDtypeStruct((s, 512), F32), jax.ShapeDtypeStruct((s, 512), BF16),
                   jax.ShapeDtypeStruct((s // c, 2, HEAD_LANES, HEAD_LANES), F32)],
        scratch_shapes=[pltpu.VMEM((2, HEAD_LANES, HEAD_LANES), F32)],
        compiler_params=_params("arbitrary"))(proj, proj, proj, proj, proj, wg, bg, gn)


def _gla_bwd(proj, wg, bg, gn, o_raw, states, dmixed, *, name):
    s = proj.shape[0]
    tb, c = GLA_ROWS, GLA_CHUNK
    cb = tb // c
    nblk, nch = s // tb, s // c

    def body(q_ref, k_ref, v_ref, r_ref, lr_ref, wg_ref, bg_ref, gn_ref, o_ref, st_ref, stn_ref, dy_ref,
             dq_ref, dk_ref, dv_ref, dr_ref, dlr_ref, gwg_ref, sums_ref, dstate):
        i = pl.program_id(0)

        @pl.when(i == 0)
        def _():
            dstate[...] = jnp.zeros_like(dstate)
            gwg_ref[...] = jnp.zeros_like(gwg_ref)
            sums_ref[...] = jnp.zeros_like(sums_ref)

        low = lax.broadcasted_iota(jnp.int32, (tb, HEAD_LANES), 1) < 64
        masks = (low, jnp.logical_not(low))
        z, b, b_end, causal, upper = _gla_block_setup(lr_ref, wg_ref, bg_ref)
        lr_b = lr_ref[...].astype(BF16)
        dlr = jnp.zeros((tb, HEAD_LANES), F32)
        per_chunk = lambda rows, mats, fn: jnp.concatenate([fn(r, m.astype(BF16)) for r, m in zip(_chunks(rows), mats)], axis=0)
        pairs = []
        for p in range(2):
            cols = pl.ds(p * HEAD_LANES, HEAD_LANES)
            sl = slice(p * HEAD_LANES, (p + 1) * HEAD_LANES)
            bp, bep = b[:, sl], b_end[:, sl]
            e_in, e_out, e_end = jnp.exp(bp), jnp.exp(-bp), jnp.exp(bep - bp)
            q = q_ref[:, cols] * 0.125
            k = k_ref[:, cols]
            q_in, k_out, k_end = q * e_in, k * e_out, k * e_end
            qms = [jnp.where(m, q_in, 0.0).astype(BF16) for m in masks]
            kms_out = [jnp.where(m, k_out, 0.0).astype(BF16) for m in masks]
            kms_end = [jnp.where(m, k_end, 0.0).astype(BF16) for m in masks]
            vs, dos = [], []
            for e in range(2):
                hc = pl.ds((2 * p + e) * HEAD_LANES, HEAD_LANES)
                o, rr, dy = o_ref[:, hc], r_ref[:, hc], dy_ref[:, hc]
                sg = _sigmoid(rr)
                rs = lax.rsqrt(jnp.mean(o * o, axis=-1, keepdims=True) + EPS)
                on = o * rs
                t = dy * (rr * sg)
                sums_ref[1, :, hc] += _fold8(t * on)
                dn = t * gn_ref[...]
                dos.append((rs * (dn - on * jnp.mean(dn * on, axis=-1, keepdims=True))).astype(BF16))
                dr_ref[:, hc] = (dy * on * gn_ref[...] * (sg * (1.0 + rr * (1.0 - sg)))).astype(BF16)
                vs.append(v_ref[:, hc].astype(BF16))
            grow = [_tn(d0, q0) + _tn(d1, q1) for d0, q0, d1, q1 in zip(_chunks(dos[0]), _chunks(qms[0]), _chunks(dos[1]), _chunks(qms[1]))]
            pairs.append((bep, e_in, e_out, e_end, q, k, qms, kms_out, kms_end, vs, dos, grow))
        chains = []
        for p in range(2):
            bep, grow = pairs[p][0], pairs[p][-1]
            entering = [st_ref[ch, p] for ch in range(cb)]
            dst, leaving_grad = dstate[p], [None] * cb
            for ch in reversed(range(cb)):
                leaving_grad[ch] = dst
                dst = dst * jnp.exp(bep[ch * c:ch * c + 1, :]) + grow[ch]
            dstate[p] = dst
            chains.append((entering, leaving_grad))
        for p in range(2):
            cols = pl.ds(p * HEAD_LANES, HEAD_LANES)
            sl = slice(p * HEAD_LANES, (p + 1) * HEAD_LANES)
            _, e_in, e_out, e_end, q, k, qms, kms_out, kms_end, vs, dos, _ = pairs[p]
            entering, leaving_grad = chains[p]
            leaving = entering[1:] + [stn_ref[0, p]]
            felt = jnp.concatenate([jnp.broadcast_to(jnp.sum(dg_st * st, axis=0, keepdims=True), (c, HEAD_LANES))
                                    for dg_st, st in zip(leaving_grad, leaving)], axis=0)
            dq_in = jnp.zeros((tb, HEAD_LANES), F32)
            dk_out = jnp.zeros((tb, HEAD_LANES), F32)
            dk_end = jnp.zeros((tb, HEAD_LANES), F32)
            for e in range(2):
                hc = pl.ds((2 * p + e) * HEAD_LANES, HEAD_LANES)
                a = jnp.where(causal, _nt(qms[e], kms_out[e]), 0.0).astype(BF16)
                da = jnp.where(causal, _nt(dos[e], vs[e]), 0.0).astype(BF16)
                dv_ref[:, hc] = (_tn(a, dos[e]) + per_chunk(kms_end[e], leaving_grad, _nt)).astype(BF16)
                dq_in = dq_in + jnp.where(masks[e], per_chunk(dos[e], entering, _nn) + _nn(da, kms_out[e]), 0.0)
                dk_out = dk_out + _tn(da, qms[e])
                dk_end = dk_end + jnp.where(masks[e], per_chunk(vs[e], leaving_grad, _nn), 0.0)
            dq = dq_in * e_in
            dk = dk_out * e_out + dk_end * e_end
            dq_ref[:, cols] = (dq * 0.125).astype(BF16)
            dk_ref[:, cols] = dk.astype(BF16)
            dg = _sum_left(upper.astype(BF16), q * dq - k * dk) + felt
            dz = dg * (1.0 / GLA_GATE_TAU) * _sigmoid(-z[:, sl])
            dz_b = dz.astype(BF16)
            sums_ref[0, :, cols] += _fold8(dz)
            dlr = dlr + _nt(dz_b, wg_ref[:, cols])
            gwg_ref[:, cols] += _tn(lr_b, dz_b)
        dlr_ref[...] = dlr.astype(BF16)

        @pl.when(i == nblk - 1)
        def _():
            _spread_total(sums_ref)

    rev = lambda i: nblk - 1 - i

    def col(width, at):
        return pl.BlockSpec((tb, width), lambda i: (rev(i), at // width))

    full = lambda shape: pl.BlockSpec(shape, lambda i: tuple(0 for _ in shape))
    out_col = lambda width: pl.BlockSpec((tb, width), lambda i: (rev(i), 0))
    return pl.pallas_call(
        body, name=name, grid=(nblk,),
        in_specs=[col(256, C_GQ), col(256, C_GK), col(512, C_GV), col(512, C_GR), col(128, C_LR),
                  full((HEAD_LANES, 256)), full((1, 256)), full((1, HEAD_LANES)),
                  pl.BlockSpec((tb, 512), lambda i: (rev(i), 0)),
                  pl.BlockSpec((cb, 2, HEAD_LANES, HEAD_LANES), lambda i: (rev(i), 0, 0, 0)),
                  pl.BlockSpec((1, 2, HEAD_LANES, HEAD_LANES), lambda i: (jnp.minimum((rev(i) + 1) * cb, nch - 1), 0, 0, 0)),
                  pl.BlockSpec((tb, 512), lambda i: (rev(i), 0))],
        out_specs=[out_col(256), out_col(256), out_col(512), out_col(512), out_col(128),
                   full((HEAD_LANES, 256)), full((2, 8, 512))],
        out_shape=[jax.ShapeDtypeStruct((s, 256), BF16), jax.ShapeDtypeStruct((s, 256), BF16),
                   jax.ShapeDtypeStruct((s, 512), BF16), jax.ShapeDtypeStruct((s, 512), BF16),
                   jax.ShapeDtypeStruct((s, 128), BF16), jax.ShapeDtypeStruct((HEAD_LANES, 256), F32),
                   jax.ShapeDtypeStruct((2, 8, 512), F32)],
        scratch_shapes=[pltpu.VMEM((2, HEAD_LANES, HEAD_LANES), F32)],
        compiler_params=_params("arbitrary"))(proj, proj, proj, proj, proj, wg, bg, gn, o_raw, states, states, dmixed)


def _head_sums(v):
    ri = lax.broadcasted_iota(jnp.int32, (HEAD_LANES, HEAD_LANES), 0) // 64
    ci = lax.broadcasted_iota(jnp.int32, (HEAD_LANES, HEAD_LANES), 1) // 64
    ones = (ri == ci).astype(BF16)
    return jnp.concatenate([_sum_right(v[:, p * HEAD_LANES:(p + 1) * HEAD_LANES], ones) for p in range(4)], axis=1)


def _attn_prep(proj, qg, kg, *, name):
    s = proj.shape[0]
    tm = ROW_TILE

    def body(q_ref, k_ref, qg_ref, kg_ref, qa_ref, ka_ref):
        q, k = q_ref[...], k_ref[...]
        qr = lax.rsqrt(_head_sums(q * q) * (1.0 / 64) + EPS)
        kr = lax.rsqrt(_head_sums(k * k) * (1.0 / 64) + EPS)
        qa_ref[...] = q * qr * qg_ref[...] * 0.125
        ka_ref[...] = k * kr * kg_ref[...]

    col = lambda at: pl.BlockSpec((tm, 512), lambda i: (i, at // 512))
    vec = pl.BlockSpec((1, 512), lambda i: (0, 0))
    out = pl.BlockSpec((tm, 512), lambda i: (i, 0))
    return pl.pallas_call(
        body, name=name, grid=(s // tm,), in_specs=[col(C_AQ), col(C_AK), vec, vec], out_specs=[out] * 2,
        out_shape=[jax.ShapeDtypeStruct((s, 512), F32)] * 2, compiler_params=_params("parallel"))(proj, proj, qg, kg)


FAR = 1e30
LOG2E, LN2 = 1.4426950408889634, 0.6931471805599453


def _attn_distance(first):
    blk = ATTN_BLOCK
    iq = lax.broadcasted_iota(jnp.int32, (2 * blk, 2 * blk), 0) & (blk - 1)
    ik = lax.broadcasted_iota(jnp.int32, (2 * blk, 2 * blk), 1)
    rel = iq + blk - ik
    valid = (rel >= 0) & (rel <= blk) & (jnp.logical_not(first) | (ik >= blk))
    return jnp.where(valid, rel.astype(F32), FAR)


def _stack_heads(t2):
    low = lax.broadcasted_iota(jnp.int32, t2.shape, 1) < 64
    return jnp.concatenate([jnp.where(low, t2, 0.0), jnp.where(low, 0.0, t2)], axis=0).astype(BF16)


def _unstack_heads(t):
    blk = ATTN_BLOCK
    low = lax.broadcasted_iota(jnp.int32, (blk, HEAD_LANES), 1) < 64
    return jnp.where(low, t[0:blk], t[blk:2 * blk])


def _attn_scores(qs, kcat, slopes, dil, dist):
    top = lax.broadcasted_iota(jnp.int32, (2 * ATTN_BLOCK, 1), 0) < ATTN_BLOCK
    return _nt(qs, kcat) - jnp.where(top, slopes[0] * (dil * LOG2E), slopes[1] * (dil * LOG2E)) * dist


def _pair_slopes(p):
    if isinstance(p, int):
        return ALIBI_SLOPES[2 * p], ALIBI_SLOPES[2 * p + 1]
    pick = lambda e: jnp.where(p == 0, ALIBI_SLOPES[e], jnp.where(p == 1, ALIBI_SLOPES[2 + e],
                               jnp.where(p == 2, ALIBI_SLOPES[4 + e], ALIBI_SLOPES[6 + e])))
    return pick(0), pick(1)


ATTN_GROUP = 4


def _each(fn, *lists):
    return [fn(*args) for args in zip(*lists)]


def _attn_group_fwd(q2s, kcats, vcats, slopes, dil, dist):
    qs = _each(lambda q2: _stack_heads(q2 * LOG2E), q2s)
    sc = _each(lambda q, k, sl: _attn_scores(q, k, sl, dil, dist), qs, kcats, slopes)
    m = _each(lambda s: jnp.max(s, axis=-1, keepdims=True), sc)
    pr = _each(lambda s, mx: jnp.exp2(s - mx), sc, m)
    den = _each(lambda p: jnp.sum(p, axis=-1, keepdims=True), pr)
    o = _each(lambda p, v, d: _nn(p.astype(BF16), v) / d, pr, vcats, den)
    lse = _each(lambda mx, d, t: jnp.broadcast_to(mx + jnp.log2(d), t.shape), m, den, o)
    return _each(lambda t, l: (_unstack_heads(t), _unstack_heads(l)), o, lse)


def _attn_group_bwd(q2s, kcats, vcats, do2s, y2s, lse2s, slopes, dil, dist):
    lane = lax.broadcasted_iota(jnp.int32, (ATTN_BLOCK, HEAD_LANES), 1)
    low = lane < 64
    per_head = lambda t, pick: jnp.concatenate([jnp.sum(jnp.where(pick(0), t, 0.0), axis=-1, keepdims=True),
                                                jnp.sum(jnp.where(pick(1), t, 0.0), axis=-1, keepdims=True)], axis=0)
    lse = _each(lambda l: per_head(l, lambda e: lane == 64 * e), lse2s)
    delta = _each(lambda d, y: per_head(d * y, lambda e: low if e == 0 else jnp.logical_not(low)), do2s, y2s)
    qs = _each(lambda q2: _stack_heads(q2 * LOG2E), q2s)
    dos = _each(_stack_heads, do2s)
    sc = _each(lambda q, k, sl: _attn_scores(q, k, sl, dil, dist), qs, kcats, slopes)
    pr = _each(lambda s, l: jnp.exp2(s - l), sc, lse)
    dp = _each(_nt, dos, vcats)
    ds = _each(lambda p, d, dl: (p * (d - dl)).astype(BF16), pr, dp, delta)
    dq = _each(lambda d, k: _unstack_heads(_nn(d, k)), ds, kcats)
    dk = _each(lambda d, q: _tn(d, q) * LN2, ds, qs)
    dv = _each(lambda p, d: _tn(p.astype(BF16), d), pr, dos)
    return list(zip(dq, dk, dv))


def _attn_specs(dil):
    rows = ATTN_BLOCK * dil
    if dil == 1:
        cur = lambda at: pl.BlockSpec((rows, 512), lambda n: (n, at // 512))
        prev = lambda at: pl.BlockSpec((rows, 512), lambda n: (jnp.maximum(n - 1, 0), at // 512))
    else:
        cur = lambda at: pl.BlockSpec((rows, HEAD_LANES), lambda n, p: (n, at // HEAD_LANES + p))
        prev = lambda at: pl.BlockSpec((rows, HEAD_LANES), lambda n, p: (jnp.maximum(n - 1, 0), at // HEAD_LANES + p))
    return cur, prev


def _attn_loop(dil, one_group, p):
    if dil == 1:
        one_group([(slice(None), pl.ds(p * HEAD_LANES, HEAD_LANES), p) for p in range(ATTN_GROUP)])
    else:
        group = min(dil, ATTN_GROUP)

        def step(g, carry):
            one_group([(pl.ds(g * group + j, ATTN_BLOCK, stride=dil), slice(None), p) for j in range(group)])
            return carry

        if dil == group:
            step(0, 0)
        else:
            lax.fori_loop(0, dil // group, step, 0)


def _dil_attn_fwd(qa, ka, proj, dil, *, name):
    s = qa.shape[0]

    def body(q_ref, kp_ref, kc_ref, vp_ref, vc_ref, o_ref, lse_ref):
        dist = _attn_distance(pl.program_id(0) == 0)
        pair = None if dil == 1 else pl.program_id(1)

        def one_group(items):
            both = lambda a, b: [jnp.concatenate([a[rows, cols], b[rows, cols]], axis=0).astype(BF16) for rows, cols, _ in items]
            outs = _attn_group_fwd([q_ref[rows, cols] for rows, cols, _ in items], both(kp_ref, kc_ref), both(vp_ref, vc_ref),
                                   [_pair_slopes(p) for _, _, p in items], dil, dist)
            for (rows, cols, _), (o2, lse2) in zip(items, outs):
                o_ref[rows, cols] = o2
                lse_ref[rows, cols] = lse2

        _attn_loop(dil, one_group, pair)

    cur, prev = _attn_specs(dil)
    grid = (s // ATTN_BLOCK,) if dil == 1 else (s // (ATTN_BLOCK * dil), 4)
    return pl.pallas_call(
        body, name=name, grid=grid, in_specs=[cur(0), prev(0), cur(0), prev(C_AV), cur(C_AV)], out_specs=[cur(0), cur(0)],
        out_shape=[jax.ShapeDtypeStruct((s, 512), F32)] * 2,
        compiler_params=_params(*["parallel"] * len(grid)))(qa, ka, ka, proj, proj)


def _attn_merge(branches, y_gla, *, name):
    s = y_gla.shape[0]
    tm = ROW_TILE

    def body(o0, l0, o1, l1, o2, l2, yg_ref, mixed_ref, y_ref, lse_ref):
        m = jnp.maximum(jnp.maximum(l0[...], l1[...]), l2[...])
        w0, w1, w2 = jnp.exp2(l0[...] - m), jnp.exp2(l1[...] - m), jnp.exp2(l2[...] - m)
        zs = w0 + w1 + w2
        y = (w0 * o0[...] + w1 * o1[...] + w2 * o2[...]) / zs
        y_ref[...] = y
        lse_ref[...] = m + jnp.log2(zs)
        mixed_ref[:, 0:512] = yg_ref[...]
        mixed_ref[:, 512:1024] = y.astype(BF16)

    blk = pl.BlockSpec((tm, 512), lambda i: (i, 0))
    args = [t for pair in branches for t in pair]
    return pl.pallas_call(
        body, name=name, grid=(s // tm,), in_specs=[blk] * 7,
        out_specs=[pl.BlockSpec((tm, 1024), lambda i: (i, 0)), blk, blk],
        out_shape=[jax.ShapeDtypeStruct((s, 1024), BF16), jax.ShapeDtypeStruct((s, 512), F32),
                   jax.ShapeDtypeStruct((s, 512), F32)],
        compiler_params=_params("parallel"))(*args, y_gla)


def _dil_attn_bwd(qa, ka, proj, y_att, lse, dmixed, dil, *, name):
    s = qa.shape[0]
    blk, rows_per_step = ATTN_BLOCK, ATTN_BLOCK * dil
    nb = s // rows_per_step
    step_axis = 0 if dil == 1 else 1

    def body(q_ref, kp_ref, kc_ref, vp_ref, vc_ref, y_ref, lse_ref, do_ref, dq_ref, dk_ref, dv_ref, dk_own, dv_own):
        n = pl.program_id(step_axis)
        pair = None if dil == 1 else pl.program_id(0)
        dist = _attn_distance(n == 0)

        @pl.when(n == 0)
        def _():
            dk_own[...] = jnp.zeros_like(dk_own)
            dv_own[...] = jnp.zeros_like(dv_own)

        def one_group(items):
            both = lambda a, b: [jnp.concatenate([a[rows, cols], b[rows, cols]], axis=0).astype(BF16) for rows, cols, _ in items]
            at = lambda ref: [ref[rows, cols] for rows, cols, _ in items]
            outs = _attn_group_bwd(at(q_ref), both(kp_ref, kc_ref), both(vp_ref, vc_ref), at(do_ref), at(y_ref), at(lse_ref),
                                   [_pair_slopes(p) for _, _, p in items], dil, dist)
            for (rows, cols, _), (dq, dk, dv) in zip(items, outs):
                dq_ref[rows, cols] = dq
                dk_ref[rows, cols] = dk_own[rows, cols] + dk[0:blk]
                dv_ref[rows, cols] = dv_own[rows, cols] + dv[0:blk]
                dk_own[rows, cols] = dk[blk:2 * blk]
                dv_own[rows, cols] = dv[blk:2 * blk]

        _attn_loop(dil, one_group, pair)

    width = 512 if dil == 1 else HEAD_LANES

    def spec(at, row_of):
        if dil == 1:
            return pl.BlockSpec((rows_per_step, width), lambda n: (row_of(n), at // width))
        return pl.BlockSpec((rows_per_step, width), lambda p, n: (row_of(n), at // width + p))

    cur = lambda at: spec(at, lambda n: n)
    prev = lambda at: spec(at, lambda n: jnp.maximum(n - 1, 0))
    own = spec(0, lambda n: 0)
    grid = (nb,) if dil == 1 else (4, nb)
    sems = ("arbitrary",) if dil == 1 else ("parallel", "arbitrary")
    dq, dk, dv, dk_last, dv_last = pl.pallas_call(
        body, name=name, grid=grid,
        in_specs=[cur(0), prev(0), cur(0), prev(C_AV), cur(C_AV), cur(0), cur(0), cur(512)],
        out_specs=[cur(0), prev(0), prev(0), own, own],
        out_shape=[jax.ShapeDtypeStruct((s, 512), F32)] * 3 + [jax.ShapeDtypeStruct((rows_per_step, 512), F32)] * 2,
        compiler_params=_params(*sems),
    )(qa, ka, ka, proj, proj, y_att, lse, dmixed)
    return dq, dk.at[s - rows_per_step:].set(dk_last), dv.at[s - rows_per_step:].set(dv_last)


def _attn_post(parts, proj, qg, kg, *, name):
    s = proj.shape[0]
    tm = ROW_TILE
    nblk = s // tm

    def body(*refs):
        ins, (q_ref, k_ref, qg_ref, kg_ref, dq_out, dk_out, dv_out, sums_ref) = refs[:9], refs[9:]
        i = pl.program_id(0)

        @pl.when(i == 0)
        def _():
            sums_ref[...] = jnp.zeros_like(sums_ref)

        dq = (ins[0][...] + ins[3][...]) + ins[6][...]
        dk = (ins[1][...] + ins[4][...]) + ins[7][...]
        dv = (ins[2][...] + ins[5][...]) + ins[8][...]
        dv_out[...] = dv.astype(BF16)
        for row, (x_ref, g_ref, dy, out, post) in enumerate(((q_ref, qg_ref, dq, dq_out, 0.125), (k_ref, kg_ref, dk, dk_out, 1.0))):
            x = x_ref[...]
            rs = lax.rsqrt(_head_sums(x * x) * (1.0 / 64) + EPS)
            xn = x * rs
            dy = dy * post
            sums_ref[row] += _fold8(dy * xn)
            dn = dy * g_ref[...]
            out[...] = (rs * (dn - xn * (_head_sums(dn * xn) * (1.0 / 64)))).astype(BF16)

        @pl.when(i == nblk - 1)
        def _():
            _spread_total(sums_ref)

    here = pl.BlockSpec((tm, 512), lambda i: (i, 0))
    col = lambda at: pl.BlockSpec((tm, 512), lambda i: (i, at // 512))
    vec = pl.BlockSpec((1, 512), lambda i: (0, 0))
    return pl.pallas_call(
        body, name=name, grid=(nblk,), in_specs=[here] * 9 + [col(C_AQ), col(C_AK), vec, vec],
        out_specs=[here, here, here, pl.BlockSpec((2, 8, 512), lambda i: (0, 0, 0))],
        out_shape=[jax.ShapeDtypeStruct((s, 512), BF16)] * 3 + [jax.ShapeDtypeStruct((2, 8, 512), F32)],
        compiler_params=_params("arbitrary"))(*[t for part in parts for t in part], proj, proj, qg, kg)


FFN_TM, FFN_TN = 256, 1408
HALO = 16


def _conv3(u_ref, halo_ref, w_ref, b_ref, first):
    u = u_ref[...].astype(F32)
    ext = jnp.concatenate([jnp.where(first, 0.0, halo_ref[...].astype(F32)), u], axis=0)
    u1 = pltpu.roll(ext, 1, 0)[HALO:]
    u2 = pltpu.roll(ext, 2, 0)[HALO:]
    return b_ref[...] + w_ref[0:1, :] * u2 + w_ref[1:2, :] * u1 + w_ref[2:3, :] * u


def _ffn_specs(tm, tn):
    nj = D_FF // tn
    blk = lambda half: pl.BlockSpec((tm, tn), lambda j, i: (i, j + half * nj))
    halo = lambda half: pl.BlockSpec((HALO, tn), lambda j, i: (jnp.maximum(i * (tm // HALO) - 1, 0), j + half * nj))
    wspec = lambda half: pl.BlockSpec((3, tn), lambda j, i: (0, j + half * nj))
    bspec = lambda half: pl.BlockSpec((1, tn), lambda j, i: (0, j + half * nj))
    return [blk(0), halo(0), blk(1), halo(1), wspec(0), wspec(1), bspec(0), bspec(1)]


def _conv_swiglu_fwd(u, conv_w, conv_b, *, name):
    s = u.shape[0]
    tm, tn = FFN_TM, FFN_TN

    def body(ug_ref, hg_ref, uv_ref, hv_ref, wg_ref, wv_ref, bg_ref, bv_ref, act_ref, uc_ref):
        first = pl.program_id(1) == 0
        cg = _conv3(ug_ref, hg_ref, wg_ref, bg_ref, first)
        cv = _conv3(uv_ref, hv_ref, wv_ref, bv_ref, first)
        act_ref[...] = (cg * _sigmoid(cg) * cv).astype(BF16)
        uc_ref[0] = cg.astype(BF16)
        uc_ref[1] = cv.astype(BF16)

    return pl.pallas_call(
        body, name=name, grid=(D_FF // tn, s // tm), in_specs=_ffn_specs(tm, tn),
        out_specs=[pl.BlockSpec((tm, tn), lambda j, i: (i, j)), pl.BlockSpec((2, tm, tn), lambda j, i: (0, i, j))],
        out_shape=[jax.ShapeDtypeStruct((s, D_FF), BF16), jax.ShapeDtypeStruct((2, s, D_FF), BF16)],
        compiler_params=_params("parallel", "parallel"))(u, u, u, u, conv_w, conv_w, conv_b, conv_b)


def _swiglu_bwd(uc, dact, *, name):
    _, s, _ = uc.shape
    tm, tn = FFN_TM, FFN_TN

    def body(uc_ref, da_ref, duc_ref, sums_ref):
        i = pl.program_id(1)

        @pl.when(i == 0)
        def _():
            sums_ref[...] = jnp.zeros_like(sums_ref)

        cg, cv, da = uc_ref[0].astype(F32), uc_ref[1].astype(F32), da_ref[...].astype(F32)
        sg = _sigmoid(cg)
        dg = da * cv * (sg * (1.0 + cg * (1.0 - sg)))
        dv = da * (cg * sg)
        duc_ref[0] = dg.astype(BF16)
        duc_ref[1] = dv.astype(BF16)
        sums_ref[0] += _fold8(dg)
        sums_ref[1] += _fold8(dv)

        @pl.when(i == s // tm - 1)
        def _():
            _spread_total(sums_ref)

    pair = pl.BlockSpec((2, tm, tn), lambda j, i: (0, i, j))
    return pl.pallas_call(
        body, name=name, grid=(D_FF // tn, s // tm), in_specs=[pair, pl.BlockSpec((tm, tn), lambda j, i: (i, j))],
        out_specs=[pair, pl.BlockSpec((2, 8, tn), lambda j, i: (0, 0, j))],
        out_shape=[jax.ShapeDtypeStruct((2, s, D_FF), BF16), jax.ShapeDtypeStruct((2, 8, D_FF), F32)],
        compiler_params=_params("parallel", "arbitrary"))(uc, dact)


def _conv_bwd(duc, u, conv_w, *, name):
    _, s, _ = duc.shape
    tm, tn = FFN_TM, FFN_TN
    nj, ni = D_FF // tn, s // tm

    def body(d_ref, halo_ref, u_ref, w_ref, du_ref, sums_ref):
        i = pl.program_id(2)

        @pl.when(i == 0)
        def _():
            sums_ref[...] = jnp.zeros_like(sums_ref)

        d = d_ref[0].astype(F32)
        ext = jnp.concatenate([d, jnp.where(i == ni - 1, 0.0, halo_ref[0].astype(F32))], axis=0)
        n = tm + HALO
        d1 = pltpu.roll(ext, n - 1, 0)[:tm]
        d2 = pltpu.roll(ext, n - 2, 0)[:tm]
        du_ref[...] = (w_ref[2:3, :] * d + w_ref[1:2, :] * d1 + w_ref[0:1, :] * d2).astype(BF16)
        uv = u_ref[...].astype(F32)
        for t, shifted in enumerate((d2, d1, d)):
            sums_ref[0, t] += _fold8(shifted * uv)

        @pl.when(i == ni - 1)
        def _():
            _spread_total(sums_ref)

    return pl.pallas_call(
        body, name=name, grid=(2, nj, ni),
        in_specs=[pl.BlockSpec((1, tm, tn), lambda g, j, i: (g, i, j)),
                  pl.BlockSpec((1, HALO, tn), lambda g, j, i: (g, jnp.minimum((i + 1) * (tm // HALO), s // HALO - 1), j)),
                  pl.BlockSpec((tm, tn), lambda g, j, i: (i, g * nj + j)),
                  pl.BlockSpec((3, tn), lambda g, j, i: (0, g * nj + j))],
        out_specs=[pl.BlockSpec((tm, tn), lambda g, j, i: (i, g * nj + j)),
                   pl.BlockSpec((1, 3, 8, tn), lambda g, j, i: (g, 0, 0, j))],
        out_shape=[jax.ShapeDtypeStruct((s, 2 * D_FF), BF16), jax.ShapeDtypeStruct((2, 3, 8, D_FF), F32)],
        compiler_params=_params("parallel", "parallel", "arbitrary"))(duc, duc, u, conv_w)


def _loss_head(x1, ffn, gate, target, *, name):
    s, d = x1.shape
    tm = ROW_TILE

    def body(x_ref, f_ref, g_ref, t_ref, dy_ref, df_ref, sums_ref):
        i = pl.program_id(0)

        @pl.when(i == 0)
        def _():
            sums_ref[...] = jnp.zeros_like(sums_ref)

        f = f_ref[...]
        err = x_ref[...] + g_ref[...] * f - t_ref[...]
        dy = err * (1.0 / d)
        dy_ref[...] = dy
        df_ref[...] = (g_ref[...] * dy).astype(BF16)
        sums_ref[0] += _fold8(dy * f)
        sums_ref[1] += _fold8(err * err)

        @pl.when(i == s // tm - 1)
        def _():
            _spread_total(sums_ref)

    row = pl.BlockSpec((tm, d), lambda i: (i, 0))
    return pl.pallas_call(
        body, name=name, grid=(s // tm,), in_specs=[row, row, pl.BlockSpec((1, d), lambda i: (0, 0)), row],
        out_specs=[row, row, pl.BlockSpec((2, 8, d), lambda i: (0, 0, 0))],
        out_shape=[jax.ShapeDtypeStruct((s, d), F32), jax.ShapeDtypeStruct((s, d), BF16), jax.ShapeDtypeStruct((2, 8, d), F32)],
        compiler_params=_params("arbitrary"))(x1, ffn, gate, target)


def _adamw(w, g, m, v, *, name):
    rows, cols = w.shape
    if rows % 8 == 0 or rows <= ROW_TILE:
        tm = next((t for t in range(ROW_TILE, 7, -8) if rows % t == 0), rows)
        blk, grid = pl.BlockSpec((tm, cols), lambda i: (i, 0)), (rows // tm,)
    else:
        blk, grid = pl.BlockSpec((rows, ROW_TILE), lambda i: (0, i)), (cols // ROW_TILE,)

    def body(w_ref, g_ref, m_ref, v_ref, d_ref, mo_ref, vo_ref):
        gv = g_ref[...]
        mn = ADAM_B1 * m_ref[...] + (1.0 - ADAM_B1) * gv
        vn = ADAM_B2 * v_ref[...] + (1.0 - ADAM_B2) * (gv * gv)
        m_hat = mn / (1.0 - ADAM_B1 ** ADAM_STEP)
        v_hat = vn / (1.0 - ADAM_B2 ** ADAM_STEP)
        d_ref[...] = -ADAM_LR * (m_hat / (jnp.sqrt(v_hat) + ADAM_EPS) + ADAM_WD * w_ref[...])
        mo_ref[...] = mn
        vo_ref[...] = vn

    return pl.pallas_call(
        body, name=name, grid=grid, in_specs=[blk] * 4, out_specs=[blk] * 3,
        out_shape=[jax.ShapeDtypeStruct((rows, cols), F32)] * 3, compiler_params=_params("parallel"))(w, g, m, v)


def _colsum(t):
    return t[..., 0, :]


def _in_proj_layout(w_in):
    pad = jnp.zeros((w_in.shape[0], PROJ_W - C_LR - GLA_GATE_RANK), w_in.dtype)
    return jnp.concatenate([w_in[:, :1536], w_in[:, 1552:], w_in[:, 1536:1552], pad], axis=1)


def _in_proj_grad_layout(g):
    return jnp.concatenate([g[:, :1536], g[:, C_LR:C_LR + GLA_GATE_RANK], g[:, 1536:C_LR]], axis=1)


def _gate_layout(gla_w_gate):
    return jnp.pad(gla_w_gate, ((0, HEAD_LANES - GLA_GATE_RANK), (0, 0))).astype(BF16)


def _local_step(x, target, mod, wi, wo, ffn_weights, ffn_grads_ready, attn_grads_ready, conv_w, conv_b, wg, bg, gn, qg, kg, n1g, n2g):
    d = D_MODEL
    sh1, sc1, g1, sh2, sc2, g2 = [mod[:, i * d:(i + 1) * d] for i in range(6)]
    qg8, kg8 = jnp.tile(qg, (1, 8)), jnp.tile(kg, (1, 8))

    _, h1, h1_t = _norm_mod_fwd(x, None, None, n1g, sc1, sh1, name="norm1_fwd")
    proj = _mm(h1, wi, tm=1024, tn=PROJ_W, tk=d, name="in_proj")
    o_raw, y_gla, states = _gla_fwd(proj, wg, bg, gn, name="gla_fwd")
    qa, ka = _attn_prep(proj, qg8, kg8, name="attn_prep")
    branches = [_dil_attn_fwd(qa, ka, proj, dil, name=f"attn_fwd_d{dil}") for dil in DILATIONS]
    mixed, y_att, lse = _attn_merge(branches, y_gla, name="attn_merge")
    attn_out = _mm(mixed, wo, tm=1024, tn=d, tk=d, name="out_proj")
    x1, h2, h2_t = _norm_mod_fwd(x, attn_out, g1, n2g, sc2, sh2, name="norm2_fwd")
    wup, wdown = ffn_weights(h2)
    u = _mm(h2, wup, out_dtype=BF16, tm=1024, tn=D_FF, tk=d, name="up_proj")
    act, uc = _conv_swiglu_fwd(u, conv_w, conv_b, name="conv_swiglu_fwd")
    ffn = _mm(act, wdown, tm=1024, tn=d, tk=D_FF, name="down_proj")
    dy, dffn, head_sums = _loss_head(x1, ffn, g2, target, name="loss_head")

    dact = _mm(dffn, wdown, tb=True, out_dtype=BF16, tm=1024, tn=D_FF, tk=d, name="down_proj_dx")
    g_wdown, g_wdown_b = _mm(act, dffn, ta=True, tm=1408, tn=d, tk=2048, also_bf16=True, name="down_proj_dw")
    duc, bias_sums = _swiglu_bwd(uc, dact, name="swiglu_bwd")
    du, tap_sums = _conv_bwd(duc, u, conv_w, name="conv_bwd")
    dh2 = _mm(du, wup, tb=True, tm=1024, tn=d, tk=D_FF, name="up_proj_dx")
    g_wup, g_wup_b = _mm(h2_t, du, tm=d, tn=1408, tk=2048, shard_cols=True, also_bf16=True, name="up_proj_dw")
    token = ffn_grads_ready(g_wup_b, g_wdown_b)
    g1_late = g1 if token is None else g1 + token[0:1, 0:1]
    dx1, dao, n2_sums = _norm_mod_bwd(x1, dh2, dy, n2g, sc2, attn_out, g1_late, name="norm2_bwd")

    dmixed = _mm(dao, wo, tb=True, tm=1024, tn=d, tk=d, name="out_proj_dx")
    g_wo = _mm(mixed, dao, ta=True, tm=d, tn=d, tk=1024, name="out_proj_dw")
    dgq, dgk, dgv, dgr, dlr, g_wg, gla_sums = _gla_bwd(proj, wg, bg, gn, o_raw, states, dmixed, name="gla_bwd")
    parts = [_dil_attn_bwd(qa, ka, proj, y_att, lse, dmixed, dil, name=f"attn_bwd_d{dil}") for dil in DILATIONS]
    daq, dak, dav, qk_sums = _attn_post(parts, proj, qg8, kg8, name="attn_post")
    dproj = jnp.concatenate([dgq, dgk, dgv, dgr, daq, dak, dav, dlr], axis=1)
    g_wi = _mm(h1_t, dproj, tm=512, tn=PROJ_W, tk=2048, name="in_proj_dw")
    token = attn_grads_ready(g_wi, g_wo)
    wi_late = wi if token is None else wi + token[0:1, 0:1].astype(BF16)
    dh1 = _mm(dproj, wi_late, tb=True, tm=1024, tn=d, tk=PROJ_W, name="in_proj_dx")
    grad_x, _, n1_sums = _norm_mod_bwd(x, dh1, dx1, n1g, sc1, None, None, name="norm1_bwd")

    n1, n2, hs, taps, cb = _colsum(n1_sums), _colsum(n2_sums), _colsum(head_sums), _colsum(tap_sums), _colsum(bias_sums)
    gs, qs = _colsum(gla_sums), _colsum(qk_sums)
    dmod = jnp.concatenate([n1[1], n1[0] * n1g[0], n2[2], n2[1], n2[0] * n2g[0], hs[0]])
    small = dict(
        dmod=dmod,
        norm1_g=n1[0] * (1.0 + sc1[0]), norm2_g=n2[0] * (1.0 + sc2[0]),
        gla_w_gate=g_wg[:GLA_GATE_RANK], gla_b_gate=gs[0, :256], gla_norm_g=gs[1].reshape(4, 128).sum(axis=0),
        q_norm_g=qs[0].reshape(8, 64).sum(axis=0), k_norm_g=qs[1].reshape(8, 64).sum(axis=0),
        conv_w=jnp.concatenate([taps[0], taps[1]], axis=1), conv_b=jnp.concatenate([cb[0], cb[1]]),
    )
    return head_sums[1], grad_x, (g_wi, g_wo, g_wup, g_wdown), small


N_DEV, N_CHIP = 8, 4
ANY = pl.BlockSpec(memory_space=pl.ANY)
VMEM_SPEC = pl.BlockSpec(memory_space=pltpu.VMEM)


def _place():
    x, y, c = lax.axis_index("x"), lax.axis_index("y"), lax.axis_index("c")
    other_chips = [(1 - x, y), (x, 1 - y), (1 - x, 1 - y)]
    return x, y, c, (x, y, 1 - c), other_chips


def _all_gather_small(v, *, name):
    m, n = v.shape

    def body(v_ref, out_ref, send_sems, recv_sems, local_sem):
        x, y, c, sibling, chips = _place()
        me = (x, y, c)

        def rows(px, py, pc):
            return out_ref.at[pl.ds((4 * px + 2 * py + pc) * m, m), :]

        def copy(k, block, to, src=None):
            return pltpu.make_async_remote_copy(
                src_ref=rows(*block) if src is None else src, dst_ref=rows(*block), send_sem=send_sems.at[k],
                recv_sem=recv_sems.at[k], device_id=to, device_id_type=MESH)

        mine = pltpu.make_async_copy(v_ref, rows(*me), local_sem)
        mine.start()
        first = [copy(0, me, sibling, src=v_ref)]
        first += [copy(1 + j, me, (*chip, c), src=v_ref) for j, chip in enumerate(chips)]
        for cp in first:
            cp.start()
        passed = [copy(4 + j, (*chip, c), sibling) for j, chip in enumerate(chips)]
        for j, chip in enumerate(chips):
            copy(1 + j, (*chip, c), me).wait_recv()
            passed[j].start()
        copy(0, sibling, me).wait_recv()
        for j, chip in enumerate(chips):
            copy(4 + j, (*chip, 1 - c), me).wait_recv()
        for cp in first + passed:
            cp.wait_send()
        mine.wait()

    return pl.pallas_call(
        body, name=name, out_shape=jax.ShapeDtypeStruct((N_DEV * m, n), v.dtype), in_specs=[VMEM_SPEC], out_specs=VMEM_SPEC,
        scratch_shapes=[pltpu.SemaphoreType.DMA((7,)), pltpu.SemaphoreType.DMA((7,)), pltpu.SemaphoreType.DMA],
    )(v)


def _gather_weight_shards(shards, *, name):
    nw = len(shards)

    def body(*refs):
        srcs, outs, (send_sems, recv_sems) = refs[:nw], refs[nw:2 * nw], refs[2 * nw:]
        x, y, c, sibling, chips = _place()
        index = lambda chip: 2 * chip[0] + chip[1]

        def copy(w, k, src, dst, to):
            return pltpu.make_async_remote_copy(src_ref=src, dst_ref=dst, send_sem=send_sems.at[6 * w + k],
                                                recv_sem=recv_sems.at[6 * w + k], device_id=to, device_id_type=MESH)

        sent = []
        for w, (src_ref, out_ref) in enumerate(zip(srcs, outs)):
            for k, chip in enumerate(chips):
                sent.append(copy(w, k, src_ref.at[c], out_ref.at[2 * x + y, c], (*chip, c)))
                sent[-1].start()
        for w, out_ref in enumerate(outs):
            for k, chip in enumerate(chips):
                landed = out_ref.at[index(chip), c]
                copy(w, k, landed, landed, (*chip, c)).wait_recv()
                sent.append(copy(w, 3 + k, landed, landed, sibling))
                sent[-1].start()
        for w, out_ref in enumerate(outs):
            for k, chip in enumerate(chips):
                passed_on = out_ref.at[index(chip), 1 - c]
                copy(w, 3 + k, passed_on, passed_on, sibling).wait_recv()
        for cp in sent:
            cp.wait_send()

    return pl.pallas_call(
        body, name=name, out_shape=[jax.ShapeDtypeStruct((N_CHIP, *s.shape), s.dtype) for s in shards],
        in_specs=[ANY] * nw, out_specs=[ANY] * nw,
        scratch_shapes=[pltpu.SemaphoreType.DMA((6 * nw,)), pltpu.SemaphoreType.DMA((6 * nw,))],
    )(*shards)


HBM_SPEC = pl.BlockSpec(memory_space=pltpu.HBM)
SEM_SPEC = pl.BlockSpec(memory_space=pltpu.SEMAPHORE)
DATAFLOW_EFFECT = pltpu.SideEffectType.DATAFLOW_SIDE_EFFECTING


def _late_copies(srcs, lands, send_sems, recv_sems):
    x, y, c, _, chips = _place()
    return [pltpu.make_async_remote_copy(
        src_ref=src.at[c], dst_ref=land.at[2 * x + y, c], send_sem=send_sems.at[6 * w + 2 * r + core],
        recv_sem=recv_sems.at[6 * w + 2 * r + c], device_id=(*chip, core), device_id_type=MESH)
        for w, (src, land) in enumerate(zip(srcs, lands)) for r, chip in enumerate(chips) for core in range(2)]


def _gather_late_start(own, after, *, name):
    nw = len(own)

    def body(*refs):
        srcs, lands, send_sems, recv_sems, token = refs[:nw], refs[nw:2 * nw], refs[2 * nw + 1], refs[2 * nw + 2], refs[-1]
        for cp in _late_copies(srcs, lands, send_sems, recv_sems):
            cp.start()
        token[...] = jnp.zeros_like(token)

    lands = [pltpu.with_memory_space_constraint(lax.empty((N_CHIP, *s.shape), s.dtype), pltpu.HBM) for s in own]
    own = [pltpu.with_memory_space_constraint(s, pltpu.HBM) for s in own]
    out = pl.pallas_call(
        body, name=name,
        out_shape=(pltpu.SemaphoreType.DMA((6 * nw,)), pltpu.SemaphoreType.DMA((6 * nw,)),
                   *[pltpu.HBM(s.shape, s.dtype) for s in own], *[pltpu.HBM(s.shape, s.dtype) for s in lands],
                   jax.ShapeDtypeStruct((8, 128), F32)),
        in_specs=[HBM_SPEC] * (2 * nw) + [ANY], out_specs=(SEM_SPEC, SEM_SPEC, *[HBM_SPEC] * (2 * nw), VMEM_SPEC),
        input_output_aliases={i: 2 + i for i in range(2 * nw)},
        compiler_params=pltpu.CompilerParams(has_side_effects=DATAFLOW_EFFECT))(*own, *lands, after)
    return out[0], out[1], out[2:2 + nw], out[2 + nw:2 + 2 * nw], out[-1]


def _gather_late_wait(send_sems, recv_sems, own, lands, after, *, name):
    nw = len(own)

    def body(*refs):
        srcs, lands_in, send_sems, recv_sems = refs[:nw], refs[nw:2 * nw], refs[2 * nw], refs[2 * nw + 1]
        x, y, c, _, chips = _place()
        for cp in _late_copies(srcs, lands_in, send_sems, recv_sems):
            cp.wait_send()
        for w, (src, land) in enumerate(zip(srcs, lands_in)):
            for r, chip in enumerate(chips):
                for core in range(2):
                    pltpu.make_async_remote_copy(
                        src_ref=src.at[c], dst_ref=land.at[2 * chip[0] + chip[1], core], send_sem=send_sems.at[6 * w + 2 * r + core],
                        recv_sem=recv_sems.at[6 * w + 2 * r + core], device_id=(*chip, core), device_id_type=MESH).wait_recv()

    out = pl.pallas_call(
        body, name=name, out_shape=(*[pltpu.HBM(s.shape, s.dtype) for s in own], *[pltpu.HBM(s.shape, s.dtype) for s in lands]),
        in_specs=[HBM_SPEC] * (2 * nw) + [SEM_SPEC, SEM_SPEC, ANY], out_specs=tuple([HBM_SPEC] * (2 * nw)),
        input_output_aliases={i: i for i in range(2 * nw)},
        compiler_params=pltpu.CompilerParams(has_side_effects=DATAFLOW_EFFECT))(*own, *lands, send_sems, recv_sems, after)
    return out[:nw], out[nw:]


def _direct_reduce_copies(srcs, lands, send_sems, recv_sems):
    x, y, c, _, _ = _place()
    cps = []
    for w, (src, land) in enumerate(zip(srcs, lands)):
        for rel in range(1, N_DEV):
            tx, ty, tc = (1 - x if rel & 4 else x), (1 - y if rel & 2 else y), (1 - c if rel & 1 else c)
            cps.append(pltpu.make_async_remote_copy(
                src_ref=src.at[2 * tx + ty, tc], dst_ref=land.at[rel - 1], send_sem=send_sems.at[7 * w + rel - 1],
                recv_sem=recv_sems.at[7 * w + rel - 1], device_id=(tx, ty, tc), device_id_type=MESH))
    return cps


def _direct_reduce_start(grads, *, name):
    nw = len(grads)

    def body(*refs):
        srcs, lands, send_sems, recv_sems, token = refs[:nw], refs[nw:2 * nw], refs[2 * nw], refs[2 * nw + 1], refs[-1]
        for cp in _direct_reduce_copies(srcs, lands, send_sems, recv_sems):
            cp.start()
        token[...] = jnp.zeros_like(token)

    lands = [pltpu.with_memory_space_constraint(lax.empty((N_DEV - 1, *g.shape[2:]), g.dtype), pltpu.HBM) for g in grads]
    grads = [pltpu.with_memory_space_constraint(g, pltpu.HBM) for g in grads]
    out = pl.pallas_call(
        body, name=name,
        out_shape=(pltpu.SemaphoreType.DMA((7 * nw,)), pltpu.SemaphoreType.DMA((7 * nw,)),
                   *[pltpu.HBM(g.shape, g.dtype) for g in grads], *[pltpu.HBM(t.shape, t.dtype) for t in lands],
                   jax.ShapeDtypeStruct((8, 128), F32)),
        in_specs=[HBM_SPEC] * (2 * nw), out_specs=(SEM_SPEC, SEM_SPEC, *[HBM_SPEC] * (2 * nw), VMEM_SPEC),
        input_output_aliases={i: 2 + i for i in range(2 * nw)},
        compiler_params=pltpu.CompilerParams(has_side_effects=DATAFLOW_EFFECT))(*grads, *lands)
    return out[0], out[1], out[2:2 + nw], out[2 + nw:2 + 2 * nw], out[-1]


def _direct_reduce_wait(send_sems, recv_sems, grads, lands, after, *, name):
    nw = len(grads)

    def body(*refs):
        srcs, lands_in, send_sems, recv_sems = refs[:nw], refs[nw:2 * nw], refs[2 * nw], refs[2 * nw + 1]
        cps = _direct_reduce_copies(srcs, lands_in, send_sems, recv_sems)
        for cp in cps:
            cp.wait_send()
        for cp in cps:
            cp.wait_recv()

    out = pl.pallas_call(
        body, name=name, out_shape=(*[pltpu.HBM(g.shape, g.dtype) for g in grads], *[pltpu.HBM(t.shape, t.dtype) for t in lands]),
        in_specs=[HBM_SPEC] * (2 * nw) + [SEM_SPEC, SEM_SPEC, ANY], out_specs=tuple([HBM_SPEC] * (2 * nw)),
        input_output_aliases={i: i for i in range(2 * nw)},
        compiler_params=pltpu.CompilerParams(has_side_effects=DATAFLOW_EFFECT))(*grads, *lands, send_sems, recv_sems, after)
    return out[nw:]


def _direct_reduce_add(grad, landed, chip, core, *, name):
    _, r, n = grad.shape
    half = r // 2
    tr = _row_tile(half)
    nb = half // tr

    def body(chip_ref, core_ref, g_ref, t_ref, o_ref):
        acc = g_ref[0]
        for k in range(N_DEV - 1):
            acc = acc + t_ref[k].astype(F32)
        o_ref[...] = acc

    return pl.pallas_call(
        body, name=name,
        grid_spec=pltpu.PrefetchScalarGridSpec(
            num_scalar_prefetch=2, grid=(nb,),
            in_specs=[pl.BlockSpec((1, tr, n), lambda i, chip_ref, core_ref: (chip_ref[0], core_ref[0] * nb + i, 0)),
                      pl.BlockSpec((N_DEV - 1, tr, n), lambda i, chip_ref, core_ref: (0, i, 0))],
            out_specs=pl.BlockSpec((tr, n), lambda i, chip_ref, core_ref: (i, 0))),
        out_shape=jax.ShapeDtypeStruct((half, n), F32), compiler_params=_params("parallel"))(chip, core, grad, landed)


def _share_halves(halves, *, name):
    nw = len(halves)

    def body(*refs):
        srcs, outs, (send_sems, recv_sems) = refs[:nw], refs[nw:2 * nw], refs[2 * nw:]
        _, _, _, sibling, _ = _place()
        cps = [pltpu.make_async_remote_copy(src_ref=src_ref, dst_ref=out_ref, send_sem=send_sems.at[w], recv_sem=recv_sems.at[w],
                                            device_id=sibling, device_id_type=MESH)
               for w, (src_ref, out_ref) in enumerate(zip(srcs, outs))]
        for cp in cps:
            cp.start()
        for cp in cps:
            cp.wait()

    return pl.pallas_call(
        body, name=name, out_shape=[jax.ShapeDtypeStruct(h.shape, h.dtype) for h in halves],
        in_specs=[ANY] * nw, out_specs=[ANY] * nw,
        scratch_shapes=[pltpu.SemaphoreType.DMA((nw,)), pltpu.SemaphoreType.DMA((nw,))])(*halves)


def _row_tile(rows, limit=256):
    return next(t for t in range(limit, 15, -16) if rows % t == 0)


def _sum_devices(gathered, *, name):
    _, m, n = gathered.shape

    def body(g_ref, tot_ref, loss_ref):
        tot = g_ref[0]
        for dev in range(1, N_DEV):
            tot = tot + g_ref[dev]
        tot_ref[...] = tot
        loss_ref[...] = jnp.full((8, n), (0.5 / D_MODEL) * jnp.sum(tot[0:8]), F32)

    return pl.pallas_call(body, name=name, in_specs=[VMEM_SPEC], out_specs=[VMEM_SPEC, VMEM_SPEC],
                          out_shape=[jax.ShapeDtypeStruct((m, n), F32), jax.ShapeDtypeStruct((8, n), F32)])(gathered)


def _ada_mod(cond_all, w_ada_shard, *, name):
    tn = 512

    def body(a_ref, b_ref, o_ref):
        o_ref[...] = _nn(a_ref[...], b_ref[...], precision=HIGHEST)

    return pl.pallas_call(
        body, name=name, grid=(w_ada_shard.shape[1] // tn,),
        in_specs=[pl.BlockSpec(cond_all.shape, lambda j: (0, 0)), pl.BlockSpec((D_MODEL, tn), lambda j: (0, j))],
        out_specs=pl.BlockSpec((N_DEV, tn), lambda j: (0, j)),
        out_shape=jax.ShapeDtypeStruct((N_DEV, w_ada_shard.shape[1]), F32), compiler_params=_params("parallel"))(cond_all, w_ada_shard)


def _ada_grad(cond_all, dmod_cols, *, name):
    tm = 256

    def body(a_ref, b_ref, o_ref):
        o_ref[...] = lax.dot_general(a_ref[...], b_ref[...], (((0,), (0,)), ((), ())), precision=HIGHEST,
                                     preferred_element_type=F32)

    return pl.pallas_call(
        body, name=name, grid=(D_MODEL // tm,),
        in_specs=[pl.BlockSpec((N_DEV, tm), lambda i: (0, i)), pl.BlockSpec(dmod_cols.shape, lambda i: (0, 0))],
        out_specs=pl.BlockSpec((tm, dmod_cols.shape[1]), lambda i: (i, 0)),
        out_shape=jax.ShapeDtypeStruct((D_MODEL, dmod_cols.shape[1]), F32), compiler_params=_params("parallel"))(cond_all, dmod_cols)


def _silu_rows(c8, *, name):
    def body(c_ref, o_ref):
        cv = c_ref[...]
        o_ref[...] = cv * _sigmoid(cv)

    return pl.pallas_call(body, name=name, in_specs=[VMEM_SPEC], out_specs=VMEM_SPEC,
                          out_shape=jax.ShapeDtypeStruct(c8.shape, F32))(c8)


def _rows128(t, rows=None):
    flat = t.reshape(-1, 128)
    return flat if rows is None else jnp.pad(flat, ((0, rows - flat.shape[0]), (0, 0)))


def _from_col_shards(shards, r, n):
    return shards.reshape(N_CHIP, r, n).transpose(1, 0, 2).reshape(r, N_CHIP * n)


def kernel(x, c, w_ada, b_ada, norm1_g, w_in, gla_w_gate, gla_b_gate, gla_norm_g, q_norm_g, k_norm_g, w_out, norm2_g, w_up, conv_w, conv_b, w_down, loss_target, m_w_ada, m_b_ada, m_norm1_g, m_w_in, m_gla_w_gate, m_gla_b_gate, m_gla_norm_g, m_q_norm_g, m_k_norm_g, m_w_out, m_norm2_g, m_w_up, m_conv_w, m_conv_b, m_w_down, v_w_ada, v_b_ada, v_norm1_g, v_w_in, v_gla_w_gate, v_gla_b_gate, v_gla_norm_g, v_q_norm_g, v_k_norm_g, v_w_out, v_norm2_g, v_w_up, v_conv_w, v_conv_b, v_w_down):
    d = D_MODEL
    ax, ay, ac = lax.axis_index("x"), lax.axis_index("y"), lax.axis_index("c")
    chip, dev = 2 * ax + ay, 4 * ax + 2 * ay + ac

    cond = _silu_rows(jnp.broadcast_to(c, (8, d)), name="cond_silu")[0:1]
    small_in = jnp.concatenate([_rows128(cond), _rows128(conv_w[0]), _rows128(gla_w_gate[0])], axis=0)
    small_in = _rows128(small_in, 56)
    got = _all_gather_small(small_in, name="gather_small").reshape(N_DEV, 56, 128)
    cond_all = got[:, 0:8].reshape(N_DEV, d)
    conv_w_full = _from_col_shards(got[0::2, 8:41].reshape(N_CHIP, 3 * 1408 // 128, 128), 3, 1408)
    gate_full = _from_col_shards(got[0::2, 41:49].reshape(N_CHIP, 16 * 64 // 128, 128), GLA_GATE_RANK, 64)
    mod_part = _ada_mod(cond_all, w_ada[0], name="ada_mod")
    mod_got = _all_gather_small(_rows128(mod_part), name="gather_mod").reshape(N_DEV, N_DEV, 1536)
    mod_all = mod_got[0::2].transpose(1, 0, 2).reshape(N_DEV, 6 * d) + b_ada
    mod = lax.dynamic_slice_in_dim(mod_all, dev, 1, axis=0)

    own = [w[0].astype(BF16).reshape(2, w.shape[1] // 2, w.shape[2]) for w in (w_in, w_out, w_up, w_down)]
    with_own = lambda got, mine: [lax.dynamic_update_index_in_dim(t, o, chip, 0) for t, o in zip(got, mine)]
    got_in, got_out = with_own(_gather_weight_shards(own[:2], name="gather_weights"), own[:2])
    w_in_full = got_in.reshape(N_CHIP, d, 772).transpose(1, 0, 2).reshape(d, N_CHIP * 772)
    w_out_full = got_out.reshape(d, d)
    exchanged = mod_all[0:1, 0:1] + got_in[0, 0, 0:1, 0:1].astype(F32)
    send_sems, recv_sems, own_thru, lands, token = _gather_late_start(own[2:], exchanged, name="gather_late_start")
    mod = mod + token[0:1, 0:1]

    def ffn_weights(after):
        mine, landed = _gather_late_wait(send_sems, recv_sems, own_thru, lands, after, name="gather_late_wait")
        got_up, got_down = with_own(landed, mine)
        return got_up.reshape(N_CHIP, d, 1408).transpose(1, 0, 2).reshape(d, 2 * D_FF), got_down.reshape(D_FF, d)

    ffn_reduce, attn_reduce, attn_parts = [], [], []
    halves_of = lambda g: g.reshape(N_CHIP, 2, g.shape[-2] // 2, g.shape[-1])

    def ffn_grads_ready(g_wup_b, g_wdown_b):
        ffn_reduce.extend(_direct_reduce_start([halves_of(g_wup_b), halves_of(g_wdown_b.reshape(N_CHIP, D_FF // N_CHIP, d))],
                                               name="reduce_ffn_start"))
        return ffn_reduce[4]

    def attn_grads_ready(g_wi, g_wo):
        attn_parts.extend([_in_proj_grad_layout(g_wi).reshape(d, N_CHIP, 772).transpose(1, 0, 2), g_wo.reshape(N_CHIP, d // N_CHIP, d)])
        attn_reduce.extend(_direct_reduce_start([halves_of(g.astype(BF16)) for g in attn_parts], name="reduce_attn_start"))
        return attn_reduce[4]

    err2, grad_x, (g_wi, g_wo, g_wup, g_wdown), small = _local_step(
        x[0], loss_target[0], mod, _in_proj_layout(w_in_full), w_out_full, ffn_weights, ffn_grads_ready, attn_grads_ready,
        conv_w_full, conv_b,
        _gate_layout(gate_full), gla_b_gate, gla_norm_g, q_norm_g, k_norm_g, norm1_g, norm2_g)

    pieces = [err2[0], small["dmod"], small["norm1_g"], small["norm2_g"], small["gla_w_gate"].reshape(-1), small["gla_b_gate"],
              small["gla_norm_g"], small["q_norm_g"], small["k_norm_g"], small["conv_w"].reshape(-1), small["conv_b"]]
    sizes = [p.shape[0] for p in pieces]
    at = [sum(sizes[:i]) for i in range(len(sizes) + 1)]
    vec = _rows128(jnp.concatenate(pieces), 288)
    got = _all_gather_small(vec, name="gather_grads").reshape(N_DEV, 288, 128)
    total, loss8 = _sum_devices(got, name="sum_devices")
    total = total.reshape(-1)
    seg = lambda i: total[at[i]:at[i + 1]]
    dmod_all = got.reshape(N_DEV, -1)[:, at[1]:at[2]]
    g_small = dict(
        b_ada=seg(1)[None], norm1_g=seg(2)[None], norm2_g=seg(3)[None],
        gla_w_gate=lax.dynamic_slice_in_dim(seg(4).reshape(GLA_GATE_RANK, 256), chip * 64, 64, axis=1),
        gla_b_gate=seg(5)[None], gla_norm_g=seg(6)[None], q_norm_g=seg(7)[None], k_norm_g=seg(8)[None],
        conv_w=lax.dynamic_slice_in_dim(seg(9).reshape(3, 2 * D_FF), chip * 1408, 1408, axis=1), conv_b=seg(10)[None])
    dmod_cols = lax.dynamic_slice_in_dim(dmod_all.reshape(N_DEV, 6 * d), chip * 1536, 1536, axis=1)
    g_w_ada = _ada_grad(cond_all, dmod_cols, name="ada_grad")

    core_id, chip_id = jnp.reshape(ac, (1,)).astype(jnp.int32), jnp.reshape(chip, (1,)).astype(jnp.int32)
    landed = (_direct_reduce_wait(*attn_reduce[:4], grad_x, name="reduce_attn_wait")
              + _direct_reduce_wait(*ffn_reduce[:4], grad_x, name="reduce_ffn_wait"))
    own = attn_parts + [g_wup, g_wdown.reshape(N_CHIP, D_FF // N_CHIP, d)]
    summed = [_direct_reduce_add(g, t, chip_id, core_id, name=f"reduce_add_{tag}")
              for g, t, tag in zip(own, landed, ("w_in", "w_out", "w_up", "w_down"))]
    others = _share_halves(summed, name="share_pair")
    g_big = [jnp.concatenate([jnp.where(ac == 0, mine, other), jnp.where(ac == 0, other, mine)], axis=0)
             for mine, other in zip(summed, others)]

    grads = dict(w_ada=g_w_ada, w_in=g_big[0], w_out=g_big[1], w_up=g_big[2], w_down=g_big[3], **g_small)
    names = ["w_ada", "b_ada", "norm1_g", "w_in", "gla_w_gate", "gla_b_gate", "gla_norm_g", "q_norm_g", "k_norm_g", "w_out",
             "norm2_g", "w_up", "conv_w", "conv_b", "w_down"]
    ws = dict(w_ada=w_ada, b_ada=b_ada, norm1_g=norm1_g, w_in=w_in, gla_w_gate=gla_w_gate, gla_b_gate=gla_b_gate,
              gla_norm_g=gla_norm_g, q_norm_g=q_norm_g, k_norm_g=k_norm_g, w_out=w_out, norm2_g=norm2_g, w_up=w_up,
              conv_w=conv_w, conv_b=conv_b, w_down=w_down)
    ms = dict(w_ada=m_w_ada, b_ada=m_b_ada, norm1_g=m_norm1_g, w_in=m_w_in, gla_w_gate=m_gla_w_gate, gla_b_gate=m_gla_b_gate,
              gla_norm_g=m_gla_norm_g, q_norm_g=m_q_norm_g, k_norm_g=m_k_norm_g, w_out=m_w_out, norm2_g=m_norm2_g, w_up=m_w_up,
              conv_w=m_conv_w, conv_b=m_conv_b, w_down=m_w_down)
    vs = dict(w_ada=v_w_ada, b_ada=v_b_ada, norm1_g=v_norm1_g, w_in=v_w_in, gla_w_gate=v_gla_w_gate, gla_b_gate=v_gla_b_gate,
              gla_norm_g=v_gla_norm_g, q_norm_g=v_q_norm_g, k_norm_g=v_k_norm_g, w_out=v_w_out, norm2_g=v_norm2_g, w_up=v_w_up,
              conv_w=v_conv_w, conv_b=v_conv_b, w_down=v_w_down)
    g_out, d_out, m_out, v_out = [], [], [], []
    for nm in names:
        shape = ws[nm].shape
        flip = (lambda t: t.T) if shape[-1] % 128 and shape[-2] % 128 == 0 else (lambda t: t)
        w2 = flip(ws[nm].reshape(shape[-2:]))
        g2 = flip(grads[nm].reshape(shape[-2:]))
        dl, mn, vn = _adamw(w2, g2, flip(ms[nm].reshape(shape[-2:])), flip(vs[nm].reshape(shape[-2:])), name=f"adamw_{nm}")
        for outs, t in ((g_out, g2), (d_out, dl), (m_out, mn), (v_out, vn)):
            outs.append(flip(t).reshape(shape))
    return (loss8[0, 0], grad_x[None], *g_out, *d_out, *m_out, *v_out)
```

```python
import functools

import jax
import jax.numpy as jnp
from jax import lax
from jax.experimental import pallas as pl
from jax.experimental.pallas import tpu as pltpu

F32, BF16 = jnp.float32, jnp.bfloat16
HIGHEST = lax.Precision.HIGHEST
MESH = pl.DeviceIdType.MESH

D_MODEL = 1024
GLA_CHUNK = 64
GLA_GATE_TAU = 16.0
GLA_GATE_RANK = 16
HEAD_LANES = 128
ATTN_BLOCK = 128
DILATIONS = (1, 4, 16)
ALIBI_SLOPES = tuple(2.0 ** (-(h + 1)) for h in range(8))
D_FF = 2816
EPS = 1e-6
C_GQ, C_GK, C_GV, C_GR, C_AQ, C_AK, C_AV, C_LR, PROJ_W = 0, 256, 512, 1024, 1536, 2048, 2560, 3072, 3200
ADAM_LR, ADAM_B1, ADAM_B2, ADAM_EPS, ADAM_WD, ADAM_STEP = 0.001, 0.9, 0.999, 1e-08, 0.01, 10
VMEM_LIMIT_BYTES = 56 * 1024 * 1024
ROW_TILE = 256


def _params(*sem):
    return pltpu.CompilerParams(dimension_semantics=sem or None, vmem_limit_bytes=VMEM_LIMIT_BYTES)


def _nt(a, b):
    return lax.dot_general(a, b, (((1,), (1,)), ((), ())), preferred_element_type=F32)


def _tn(a, b):
    return lax.dot_general(a, b, (((0,), (0,)), ((), ())), preferred_element_type=F32)


def _nn(a, b, precision=None):
    return jnp.dot(a, b, preferred_element_type=F32, precision=precision)


def _split3(v):
    hi = v.astype(BF16)
    rest = v - hi.astype(F32)
    mid = rest.astype(BF16)
    return hi, mid, (rest - mid.astype(F32)).astype(BF16)


def _sum_right(v, ones):
    hi, mid, lo = _split3(v)
    return (_nn(lo, ones) + _nn(mid, ones)) + _nn(hi, ones)


def _sum_left(ones, v):
    hi, mid, lo = _split3(v)
    return (_nn(ones, lo) + _nn(ones, mid)) + _nn(ones, hi)


def _fold8(v):
    return v.reshape(v.shape[0] // 8, 8, v.shape[1]).sum(axis=0)


def _spread_total(ref):
    t = ref[...]
    ref[...] = jnp.broadcast_to(jnp.sum(t, axis=-2, keepdims=True), t.shape)


def _sigmoid(x):
    return 1.0 / (1.0 + jnp.exp(-x))


def _mm(a, b, *, ta=False, tb=False, out_dtype=F32, tm, tn, tk, shard_cols=False, also_bf16=False, name):
    (k_a, m) = a.shape if ta else a.shape[::-1]
    (k_b, n) = b.shape[::-1] if tb else b.shape
    assert k_a == k_b and m % tm == 0 and n % tn == 0 and k_a % tk == 0, (name, a.shape, b.shape)
    nk = k_a // tk
    assert nk == 1 or out_dtype == F32, name
    dims = (((0 if ta else 1,), (1 if tb else 0,)), ((), ()))

    def body(a_ref, b_ref, o_ref, *rounded):
        k = pl.program_id(2)
        part = lax.dot_general(a_ref[...].astype(BF16), b_ref[...].astype(BF16), dims, preferred_element_type=F32)
        if nk == 1:
            o_ref[...] = part.astype(out_dtype)
        else:
            @pl.when(k == 0)
            def _():
                o_ref[...] = part

            @pl.when(k > 0)
            def _():
                o_ref[...] += part

        if also_bf16:
            @pl.when(k == nk - 1)
            def _():
                rounded[0][...] = o_ref[...].astype(BF16)

    a_spec = pl.BlockSpec((tk, tm), lambda i, j, k: (k, i)) if ta else pl.BlockSpec((tm, tk), lambda i, j, k: (i, k))
    b_spec = pl.BlockSpec((tn, tk), lambda i, j, k: (j, k)) if tb else pl.BlockSpec((tk, tn), lambda i, j, k: (k, j))
    if shard_cols:
        o_spec, o_shape = pl.BlockSpec((None, tm, tn), lambda i, j, k: (j, i, 0)), (n // tn, m, tn)
    else:
        o_spec, o_shape = pl.BlockSpec((tm, tn), lambda i, j, k: (i, j)), (m, n)
    shapes = [jax.ShapeDtypeStruct(o_shape, out_dtype)] + ([jax.ShapeDtypeStruct(o_shape, BF16)] if also_bf16 else [])
    out = pl.pallas_call(
        body, name=name, grid=(m // tm, n // tn, nk), in_specs=[a_spec, b_spec], out_specs=[o_spec] * len(shapes),
        out_shape=shapes, compiler_params=_params("parallel", "parallel", "arbitrary"))(a, b)
    return out if also_bf16 else out[0]


def _norm_mod_fwd(x, branch, gate, gain, scale, shift, *, name):
    s, d = x.shape
    tm = ROW_TILE
    has_branch = branch is not None

    def body(*refs):
        if has_branch:
            x_ref, br_ref, gate_ref, gain_ref, sc_ref, sh_ref, x1_ref, h_ref, ht_ref = refs
            xv = x_ref[...] + gate_ref[...] * br_ref[...]
            x1_ref[...] = xv
        else:
            x_ref, gain_ref, sc_ref, sh_ref, h_ref, ht_ref = refs
            xv = x_ref[...]
        r = lax.rsqrt(jnp.mean(xv * xv, axis=-1, keepdims=True) + EPS)
        h = (xv * r) * gain_ref[...] * (1.0 + sc_ref[...]) + sh_ref[...]
        h_ref[...] = h.astype(BF16)
        ht_ref[...] = h.T.astype(BF16)

    row = pl.BlockSpec((tm, d), lambda i: (i, 0))
    col = pl.BlockSpec((d, tm), lambda i: (0, i))
    vec = pl.BlockSpec((1, d), lambda i: (0, 0))
    h_shapes = [jax.ShapeDtypeStruct((s, d), BF16), jax.ShapeDtypeStruct((d, s), BF16)]
    if has_branch:
        return pl.pallas_call(
            body, name=name, grid=(s // tm,), in_specs=[row, row, vec, vec, vec, vec], out_specs=[row, row, col],
            out_shape=[jax.ShapeDtypeStruct((s, d), F32)] + h_shapes,
            compiler_params=_params("parallel"))(x, branch, gate, gain, scale, shift)
    h, ht = pl.pallas_call(
        body, name=name, grid=(s // tm,), in_specs=[row, vec, vec, vec], out_specs=[row, col],
        out_shape=h_shapes, compiler_params=_params("parallel"))(x, gain, scale, shift)
    return x, h, ht


def _norm_mod_bwd(x, dh, dres, gain, scale, branch, gate, *, name):
    s, d = x.shape
    tm = ROW_TILE
    has_branch = branch is not None

    def body(*refs):
        if has_branch:
            x_ref, dh_ref, dres_ref, gain_ref, sc_ref, br_ref, gate_ref, dx_ref, dbr_ref, sums_ref = refs
        else:
            x_ref, dh_ref, dres_ref, gain_ref, sc_ref, dx_ref, sums_ref = refs
        i = pl.program_id(0)

        @pl.when(i == 0)
        def _():
            sums_ref[...] = jnp.zeros_like(sums_ref)

        xv, dhv = x_ref[...], dh_ref[...]
        r = lax.rsqrt(jnp.mean(xv * xv, axis=-1, keepdims=True) + EPS)
        xn = xv * r
        dxn = dhv * (gain_ref[...] * (1.0 + sc_ref[...]))
        dx = dres_ref[...] + r * (dxn - xn * jnp.mean(dxn * xn, axis=-1, keepdims=True))
        dx_ref[...] = dx
        sums_ref[0] += _fold8(dhv * xn)
        sums_ref[1] += _fold8(dhv)
        if has_branch:
            dbr_ref[...] = (gate_ref[...] * dx).astype(BF16)
            sums_ref[2] += _fold8(dx * br_ref[...])

        @pl.when(i == s // tm - 1)
        def _():
            _spread_total(sums_ref)

    row = pl.BlockSpec((tm, d), lambda i: (i, 0))
    vec = pl.BlockSpec((1, d), lambda i: (0, 0))
    sums = pl.BlockSpec((3, 8, d), lambda i: (0, 0, 0))
    sums_shape = jax.ShapeDtypeStruct((3, 8, d), F32)
    if has_branch:
        return pl.pallas_call(
            body, name=name, grid=(s // tm,), in_specs=[row, row, row, vec, vec, row, vec], out_specs=[row, row, sums],
            out_shape=[jax.ShapeDtypeStruct((s, d), F32), jax.ShapeDtypeStruct((s, d), BF16), sums_shape],
            compiler_params=_params("arbitrary"))(x, dh, dres, gain, scale, branch, gate)
    dx, sm = pl.pallas_call(
        body, name=name, grid=(s // tm,), in_specs=[row, row, row, vec, vec], out_specs=[row, sums],
        out_shape=[jax.ShapeDtypeStruct((s, d), F32), sums_shape],
        compiler_params=_params("arbitrary"))(x, dh, dres, gain, scale)
    return dx, None, sm


GLA_ROWS = 256


def _gla_block_setup(lr_ref, wg_ref, bg_ref):
    t, c = GLA_ROWS, GLA_CHUNK
    ri = lax.broadcasted_iota(jnp.int32, (t, t), 0)
    ci = lax.broadcasted_iota(jnp.int32, (t, t), 1)
    same = (ri // c) == (ci // c)
    causal, upper = same & (ci <= ri), same & (ci >= ri)
    z = _nn(lr_ref[...].astype(BF16), wg_ref[...]) + bg_ref[...]
    g = (jnp.minimum(z, 0.0) - jnp.log(1.0 + jnp.exp(-jnp.abs(z)))) * (1.0 / GLA_GATE_TAU)
    hi, mid, lo = _split3(g)
    total = lambda ones: (_nn(ones, lo) + _nn(ones, mid)) + _nn(ones, hi)
    return z, total(causal.astype(BF16)), total(same.astype(BF16)), causal, upper


def _chunks(t):
    return [t[i * GLA_CHUNK:(i + 1) * GLA_CHUNK] for i in range(GLA_ROWS // GLA_CHUNK)]


def _gla_fwd(proj, wg, bg, gn, *, name):
    s = proj.shape[0]
    tb, c = GLA_ROWS, GLA_CHUNK
    cb = tb // c

    def body(q_ref, k_ref, v_ref, r_ref, lr_ref, wg_ref, bg_ref, gn_ref, o_ref, y_ref, st_ref, state):
        i = pl.program_id(0)

        @pl.when(i == 0)
        def _():
            state[...] = jnp.zeros_like(state)

        low = lax.broadcasted_iota(jnp.int32, (tb, HEAD_LANES), 1) < 64
        masks = (low, jnp.logical_not(low))
        _, b, b_end, causal, _ = _gla_block_setup(lr_ref, wg_ref, bg_ref)
        pairs = []
        for p in range(2):
            cols = pl.ds(p * HEAD_LANES, HEAD_LANES)
            bp, bep = (t[:, p * HEAD_LANES:(p + 1) * HEAD_LANES] for t in (b, b_end))
            k = k_ref[:, cols]
            q_in = q_ref[:, cols] * 0.125 * jnp.exp(bp)
            k_out = (k * jnp.exp(-bp)).astype(BF16)
            k_end = k * jnp.exp(bep - bp)
            qms = [jnp.where(m, q_in, 0.0).astype(BF16) for m in masks]
            kes = [jnp.where(m, k_end, 0.0).astype(BF16) for m in masks]
            vs = [v_ref[:, pl.ds((2 * p + e) * HEAD_LANES, HEAD_LANES)].astype(BF16) for e in range(2)]
            grow = [_tn(v0, k0) + _tn(v1, k1) for v0, k0, v1, k1 in zip(_chunks(vs[0]), _chunks(kes[0]), _chunks(vs[1]), _chunks(kes[1]))]
            pairs.append((bep, k_out, qms, vs, grow))
        entering = [[], []]
        for p, (bep, _, _, _, grow) in enumerate(pairs):
            st = state[p]
            for ch in range(cb):
                entering[p].append(st)
                st_ref[ch, p] = st
                st = st * jnp.exp(bep[ch * c:ch * c + 1, :]) + grow[ch]
            state[p] = st
        for p, (_, k_out, qms, vs, _) in enumerate(pairs):
            for e in range(2):
                hc = pl.ds((2 * p + e) * HEAD_LANES, HEAD_LANES)
                a = jnp.where(causal, _nt(qms[e], k_out), 0.0).astype(BF16)
                carried = jnp.concatenate([_nt(qc, sc.astype(BF16)) for qc, sc in zip(_chunks(qms[e]), entering[p])], axis=0)
                o = _nn(a, vs[e]) + carried
                o_ref[:, hc] = o
                rr = r_ref[:, hc]
                on = o * lax.rsqrt(jnp.mean(o * o, axis=-1, keepdims=True) + EPS)
                y_ref[:, hc] = (on * gn_ref[...] * (rr * _sigmoid(rr))).astype(BF16)

    def col(width, at):
        return pl.BlockSpec((tb, width), lambda i: (i, at // width))

    full = lambda shape: pl.BlockSpec(shape, lambda i: tuple(0 for _ in shape))
    return pl.pallas_call(
        body, name=name, grid=(s // tb,),
        in_specs=[col(256, C_GQ), col(256, C_GK), col(512, C_GV), col(512, C_GR), col(128, C_LR),
                  full((HEAD_LANES, 256)), full((1, 256)), full((1, HEAD_LANES))],
        out_specs=[pl.BlockSpec((tb, 512), lambda i: (i, 0)), pl.BlockSpec((tb, 512), lambda i: (i, 0)),
                   pl.BlockSpec((cb, 2, HEAD_LANES, HEAD_LANES), lambda i: (i, 0, 0, 0))],
        out_shape=[jax.ShapeDtypeStruct((s, 512), F32), jax.ShapeDtypeStruct((s, 512), BF16),
                   jax.ShapeDtypeStruct((s // c, 2, HEAD_LANES, HEAD_LANES), F32)],
        scratch_shapes=[pltpu.VMEM((2, HEAD_LANES, HEAD_LANES), F32)],
        compiler_params=_params("arbitrary"))(proj, proj, proj, proj, proj, wg, bg, gn)


def _gla_bwd(proj, wg, bg, gn, o_raw, states, dmixed, *, name):
    s = proj.shape[0]
    tb, c = GLA_ROWS, GLA_CHUNK
    cb = tb // c
    nblk, nch = s // tb, s // c

    def body(q_ref, k_ref, v_ref, r_ref, lr_ref, wg_ref, bg_ref, gn_ref, o_ref, st_ref, stn_ref, dy_ref,
             dq_ref, dk_ref, dv_ref, dr_ref, dlr_ref, gwg_ref, sums_ref, dstate):
        i = pl.program_id(0)

        @pl.when(i == 0)
        def _():
            dstate[...] = jnp.zeros_like(dstate)
            gwg_ref[...] = jnp.zeros_like(gwg_ref)
            sums_ref[...] = jnp.zeros_like(sums_ref)

        low = lax.broadcasted_iota(jnp.int32, (tb, HEAD_LANES), 1) < 64
        masks = (low, jnp.logical_not(low))
        z, b, b_end, causal, upper = _gla_block_setup(lr_ref, wg_ref, bg_ref)
        lr_b = lr_ref[...].astype(BF16)
        dlr = jnp.zeros((tb, HEAD_LANES), F32)
        per_chunk = lambda rows, mats, fn: jnp.concatenate([fn(r, m.astype(BF16)) for r, m in zip(_chunks(rows), mats)], axis=0)
        pairs = []
        for p in range(2):
            cols = pl.ds(p * HEAD_LANES, HEAD_LANES)
            sl = slice(p * HEAD_LANES, (p + 1) * HEAD_LANES)
            bp, bep = b[:, sl], b_end[:, sl]
            e_in, e_out, e_end = jnp.exp(bp), jnp.exp(-bp), jnp.exp(bep - bp)
            q = q_ref[:, cols] * 0.125
            k = k_ref[:, cols]
            q_in, k_out, k_end = q * e_in, k * e_out, k * e_end
            qms = [jnp.where(m, q_in, 0.0).astype(BF16) for m in masks]
            kms_out = [jnp.where(m, k_out, 0.0).astype(BF16) for m in masks]
            kms_end = [jnp.where(m, k_end, 0.0).astype(BF16) for m in masks]
            vs, dos = [], []
            for e in range(2):
                hc = pl.ds((2 * p + e) * HEAD_LANES, HEAD_LANES)
                o, rr, dy = o_ref[:, hc], r_ref[:, hc], dy_ref[:, hc]
                sg = _sigmoid(rr)
                rs = lax.rsqrt(jnp.mean(o * o, axis=-1, keepdims=True) + EPS)
                on = o * rs
                t = dy * (rr * sg)
                sums_ref[1, :, hc] += _fold8(t * on)
                dn = t * gn_ref[...]
                dos.append((rs * (dn - on * jnp.mean(dn * on, axis=-1, keepdims=True))).astype(BF16))
                dr_ref[:, hc] = (dy * on * gn_ref[...] * (sg * (1.0 + rr * (1.0 - sg)))).astype(BF16)
                vs.append(v_ref[:, hc].astype(BF16))
            grow = [_tn(d0, q0) + _tn(d1, q1) for d0, q0, d1, q1 in zip(_chunks(dos[0]), _chunks(qms[0]), _chunks(dos[1]), _chunks(qms[1]))]
            pairs.append((bep, e_in, e_out, e_end, q, k, qms, kms_out, kms_end, vs, dos, grow))
        chains = []
        for p in range(2):
            bep, grow = pairs[p][0], pairs[p][-1]
            entering = [st_ref[ch, p] for ch in range(cb)]
            dst, leaving_grad = dstate[p], [None] * cb
            for ch in reversed(range(cb)):
                leaving_grad[ch] = dst
                dst = dst * jnp.exp(bep[ch * c:ch * c + 1, :]) + grow[ch]
            dstate[p] = dst
            chains.append((entering, leaving_grad))
        for p in range(2):
            cols = pl.ds(p * HEAD_LANES, HEAD_LANES)
            sl = slice(p * HEAD_LANES, (p + 1) * HEAD_LANES)
            _, e_in, e_out, e_end, q, k, qms, kms_out, kms_end, vs, dos, _ = pairs[p]
            entering, leaving_grad = chains[p]
            leaving = entering[1:] + [stn_ref[0, p]]
            felt = jnp.concatenate([jnp.broadcast_to(jnp.sum(dg_st * st, axis=0, keepdims=True), (c, HEAD_LANES))
                                    for dg_st, st in zip(leaving_grad, leaving)], axis=0)
            dq_in = jnp.zeros((tb, HEAD_LANES), F32)
            dk_out = jnp.zeros((tb, HEAD_LANES), F32)
            dk_end = jnp.zeros((tb, HEAD_LANES), F32)
            for e in range(2):
                hc = pl.ds((2 * p + e) * HEAD_LANES, HEAD_LANES)
                a = jnp.where(causal, _nt(qms[e], kms_out[e]), 0.0).astype(BF16)
                da = jnp.where(causal, _nt(dos[e], vs[e]), 0.0).astype(BF16)
                dv_ref[:, hc] = (_tn(a, dos[e]) + per_chunk(kms_end[e], leaving_grad, _nt)).astype(BF16)
                dq_in = dq_in + jnp.where(masks[e], per_chunk(dos[e], entering, _nn) + _nn(da, kms_out[e]), 0.0)
                dk_out = dk_out + _tn(da, qms[e])
                dk_end = dk_end + jnp.where(masks[e], per_chunk(vs[e], leaving_grad, _nn), 0.0)
            dq = dq_in * e_in
            dk = dk_out * e_out + dk_end * e_end
            dq_ref[:, cols] = (dq * 0.125).astype(BF16)
            dk_ref[:, cols] = dk.astype(BF16)
            dg = _sum_left(upper.astype(BF16), q * dq - k * dk) + felt
            dz = dg * (1.0 / GLA_GATE_TAU) * _sigmoid(-z[:, sl])
            dz_b = dz.astype(BF16)
            sums_ref[0, :, cols] += _fold8(dz)
            dlr = dlr + _nt(dz_b, wg_ref[:, cols])
            gwg_ref[:, cols] += _tn(lr_b, dz_b)
        dlr_ref[...] = dlr.astype(BF16)

        @pl.when(i == nblk - 1)
        def _():
            _spread_total(sums_ref)

    rev = lambda i: nblk - 1 - i

    def col(width, at):
        return pl.BlockSpec((tb, width), lambda i: (rev(i), at // width))

    full = lambda shape: pl.BlockSpec(shape, lambda i: tuple(0 for _ in shape))
    out_col = lambda width: pl.BlockSpec((tb, width), lambda i: (rev(i), 0))
    return pl.pallas_call(
        body, name=name, grid=(nblk,),
        in_specs=[col(256, C_GQ), col(256, C_GK), col(512, C_GV), col(512, C_GR), col(128, C_LR),
                  full((HEAD_LANES, 256)), full((1, 256)), full((1, HEAD_LANES)),
                  pl.BlockSpec((tb, 512), lambda i: (rev(i), 0)),
                  pl.BlockSpec((cb, 2, HEAD_LANES, HEAD_LANES), lambda i: (rev(i), 0, 0, 0)),
                  pl.BlockSpec((1, 2, HEAD_LANES, HEAD_LANES), lambda i: (jnp.minimum((rev(i) + 1) * cb, nch - 1), 0, 0, 0)),
                  pl.BlockSpec((tb, 512), lambda i: (rev(i), 0))],
        out_specs=[out_col(256), out_col(256), out_col(512), out_col(512), out_col(128),
                   full((HEAD_LANES, 256)), full((2, 8, 512))],
        out_shape=[jax.ShapeDtypeStruct((s, 256), BF16), jax.ShapeDtypeStruct((s, 256), BF16),
                   jax.ShapeDtypeStruct((s, 512), BF16), jax.ShapeDtypeStruct((s, 512), BF16),
                   jax.ShapeDtypeStruct((s, 128), BF16), jax.ShapeDtypeStruct((HEAD_LANES, 256), F32),
                   jax.ShapeDtypeStruct((2, 8, 512), F32)],
        scratch_shapes=[pltpu.VMEM((2, HEAD_LANES, HEAD_LANES), F32)],
        compiler_params=_params("arbitrary"))(proj, proj, proj, proj, proj, wg, bg, gn, o_raw, states, states, dmixed)


def _head_sums(v):
    ri = lax.broadcasted_iota(jnp.int32, (HEAD_LANES, HEAD_LANES), 0) // 64
    ci = lax.broadcasted_iota(jnp.int32, (HEAD_LANES, HEAD_LANES), 1) // 64
    ones = (ri == ci).astype(BF16)
    return jnp.concatenate([_sum_right(v[:, p * HEAD_LANES:(p + 1) * HEAD_LANES], ones) for p in range(4)], axis=1)


def _attn_prep(proj, qg, kg, *, name):
    s = proj.shape[0]
    tm = ROW_TILE

    def body(q_ref, k_ref, qg_ref, kg_ref, qa_ref, ka_ref):
        q, k = q_ref[...], k_ref[...]
        qr = lax.rsqrt(_head_sums(q * q) * (1.0 / 64) + EPS)
        kr = lax.rsqrt(_head_sums(k * k) * (1.0 / 64) + EPS)
        qa_ref[...] = q * qr * qg_ref[...] * 0.125
        ka_ref[...] = k * kr * kg_ref[...]

    col = lambda at: pl.BlockSpec((tm, 512), lambda i: (i, at // 512))
    vec = pl.BlockSpec((1, 512), lambda i: (0, 0))
    out = pl.BlockSpec((tm, 512), lambda i: (i, 0))
    return pl.pallas_call(
        body, name=name, grid=(s // tm,), in_specs=[col(C_AQ), col(C_AK), vec, vec], out_specs=[out] * 2,
        out_shape=[jax.ShapeDtypeStruct((s, 512), F32)] * 2, compiler_params=_params("parallel"))(proj, proj, qg, kg)


FAR = 1e30
LOG2E, LN2 = 1.4426950408889634, 0.6931471805599453


def _attn_distance(first):
    blk = ATTN_BLOCK
    iq = lax.broadcasted_iota(jnp.int32, (2 * blk, 2 * blk), 0) & (blk - 1)
    ik = lax.broadcasted_iota(jnp.int32, (2 * blk, 2 * blk), 1)
    rel = iq + blk - ik
    valid = (rel >= 0) & (rel <= blk) & (jnp.logical_not(first) | (ik >= blk))
    return jnp.where(valid, rel.astype(F32), FAR)


def _stack_heads(t2):
    low = lax.broadcasted_iota(jnp.int32, t2.shape, 1) < 64
    return jnp.concatenate([jnp.where(low, t2, 0.0), jnp.where(low, 0.0, t2)], axis=0).astype(BF16)


def _unstack_heads(t):
    blk = ATTN_BLOCK
    low = lax.broadcasted_iota(jnp.int32, (blk, HEAD_LANES), 1) < 64
    return jnp.where(low, t[0:blk], t[blk:2 * blk])


def _attn_scores(qs, kcat, slopes, dil, dist):
    top = lax.broadcasted_iota(jnp.int32, (2 * ATTN_BLOCK, 1), 0) < ATTN_BLOCK
    return _nt(qs, kcat) - jnp.where(top, slopes[0] * (dil * LOG2E), slopes[1] * (dil * LOG2E)) * dist


def _pair_slopes(p):
    if isinstance(p, int):
        return ALIBI_SLOPES[2 * p], ALIBI_SLOPES[2 * p + 1]
    pick = lambda e: jnp.where(p == 0, ALIBI_SLOPES[e], jnp.where(p == 1, ALIBI_SLOPES[2 + e],
                               jnp.where(p == 2, ALIBI_SLOPES[4 + e], ALIBI_SLOPES[6 + e])))
    return pick(0), pick(1)


ATTN_GROUP = 4


def _each(fn, *lists):
    return [fn(*args) for args in zip(*lists)]


def _attn_group_fwd(q2s, kcats, vcats, slopes, dil, dist):
    qs = _each(lambda q2: _stack_heads(q2 * LOG2E), q2s)
    sc = _each(lambda q, k, sl: _attn_scores(q, k, sl, dil, dist), qs, kcats, slopes)
    m = _each(lambda s: jnp.max(s, axis=-1, keepdims=True), sc)
    pr = _each(lambda s, mx: jnp.exp2(s - mx), sc, m)
    den = _each(lambda p: jnp.sum(p, axis=-1, keepdims=True), pr)
    o = _each(lambda p, v, d: _nn(p.astype(BF16), v) / d, pr, vcats, den)
    lse = _each(lambda mx, d, t: jnp.broadcast_to(mx + jnp.log2(d), t.shape), m, den, o)
    return _each(lambda t, l: (_unstack_heads(t), _unstack_heads(l)), o, lse)


def _attn_group_bwd(q2s, kcats, vcats, do2s, y2s, lse2s, slopes, dil, dist):
    lane = lax.broadcasted_iota(jnp.int32, (ATTN_BLOCK, HEAD_LANES), 1)
    low = lane < 64
    per_head = lambda t, pick: jnp.concatenate([jnp.sum(jnp.where(pick(0), t, 0.0), axis=-1, keepdims=True),
                                                jnp.sum(jnp.where(pick(1), t, 0.0), axis=-1, keepdims=True)], axis=0)
    lse = _each(lambda l: per_head(l, lambda e: lane == 64 * e), lse2s)
    delta = _each(lambda d, y: per_head(d * y, lambda e: low if e == 0 else jnp.logical_not(low)), do2s, y2s)
    qs = _each(lambda q2: _stack_heads(q2 * LOG2E), q2s)
    dos = _each(_stack_heads, do2s)
    sc = _each(lambda q, k, sl: _attn_scores(q, k, sl, dil, dist), qs, kcats, slopes)
    pr = _each(lambda s, l: jnp.exp2(s - l), sc, lse)
    dp = _each(_nt, dos, vcats)
    ds = _each(lambda p, d, dl: (p * (d - dl)).astype(BF16), pr, dp, delta)
    dq = _each(lambda d, k: _unstack_heads(_nn(d, k)), ds, kcats)
    dk = _each(lambda d, q: _tn(d, q) * LN2, ds, qs)
    dv = _each(lambda p, d: _tn(p.astype(BF16), d), pr, dos)
    return list(zip(dq, dk, dv))


def _attn_specs(dil):
    rows = ATTN_BLOCK * dil
    if dil == 1:
        cur = lambda at: pl.BlockSpec((rows, 512), lambda n: (n, at // 512))
        prev = lambda at: pl.BlockSpec((rows, 512), lambda n: (jnp.maximum(n - 1, 0), at // 512))
    else:
        cur = lambda at: pl.BlockSpec((rows, HEAD_LANES), lambda n, p: (n, at // HEAD_LANES + p))
        prev = lambda at: pl.BlockSpec((rows, HEAD_LANES), lambda n, p: (jnp.maximum(n - 1, 0), at // HEAD_LANES + p))
    return cur, prev


def _attn_loop(dil, one_group, p):
    if dil == 1:
        one_group([(slice(None), pl.ds(p * HEAD_LANES, HEAD_LANES), p) for p in range(ATTN_GROUP)])
    else:
        group = min(dil, ATTN_GROUP)

        def step(g, carry):
            one_group([(pl.ds(g * group + j, ATTN_BLOCK, stride=dil), slice(None), p) for j in range(group)])
            return carry

        if dil == group:
            step(0, 0)
        else:
            lax.fori_loop(0, dil // group, step, 0)


def _dil_attn_fwd(qa, ka, proj, dil, *, name):
    s = qa.shape[0]

    def body(q_ref, kp_ref, kc_ref, vp_ref, vc_ref, o_ref, lse_ref):
        dist = _attn_distance(pl.program_id(0) == 0)
        pair = None if dil == 1 else pl.program_id(1)

        def one_group(items):
            both = lambda a, b: [jnp.concatenate([a[rows, cols], b[rows, cols]], axis=0).astype(BF16) for rows, cols, _ in items]
            outs = _attn_group_fwd([q_ref[rows, cols] for rows, cols, _ in items], both(kp_ref, kc_ref), both(vp_ref, vc_ref),
                                   [_pair_slopes(p) for _, _, p in items], dil, dist)
            for (rows, cols, _), (o2, lse2) in zip(items, outs):
                o_ref[rows, cols] = o2
                lse_ref[rows, cols] = lse2

        _attn_loop(dil, one_group, pair)

    cur, prev = _attn_specs(dil)
    grid = (s // ATTN_BLOCK,) if dil == 1 else (s // (ATTN_BLOCK * dil), 4)
    return pl.pallas_call(
        body, name=name, grid=grid, in_specs=[cur(0), prev(0), cur(0), prev(C_AV), cur(C_AV)], out_specs=[cur(0), cur(0)],
        out_shape=[jax.ShapeDtypeStruct((s, 512), F32)] * 2,
        compiler_params=_params(*["parallel"] * len(grid)))(qa, ka, ka, proj, proj)


def _attn_merge(branches, y_gla, *, name):
    s = y_gla.shape[0]
    tm = ROW_TILE

    def body(o0, l0, o1, l1, o2, l2, yg_ref, mixed_ref, y_ref, lse_ref):
        m = jnp.maximum(jnp.maximum(l0[...], l1[...]), l2[...])
        w0, w1, w2 = jnp.exp2(l0[...] - m), jnp.exp2(l1[...] - m), jnp.exp2(l2[...] - m)
        zs = w0 + w1 + w2
        y = (w0 * o0[...] + w1 * o1[...] + w2 * o2[...]) / zs
        y_ref[...] = y
        lse_ref[...] = m + jnp.log2(zs)
        mixed_ref[:, 0:512] = yg_ref[...]
        mixed_ref[:, 512:1024] = y.astype(BF16)

    blk = pl.BlockSpec((tm, 512), lambda i: (i, 0))
    args = [t for pair in branches for t in pair]
    return pl.pallas_call(
        body, name=name, grid=(s // tm,), in_specs=[blk] * 7,
        out_specs=[pl.BlockSpec((tm, 1024), lambda i: (i, 0)), blk, blk],
        out_shape=[jax.ShapeDtypeStruct((s, 1024), BF16), jax.ShapeDtypeStruct((s, 512), F32),
                   jax.ShapeDtypeStruct((s, 512), F32)],
        compiler_params=_params("parallel"))(*args, y_gla)


def _dil_attn_bwd(qa, ka, proj, y_att, lse, dmixed, dil, *, name):
    s = qa.shape[0]
    blk, rows_per_step = ATTN_BLOCK, ATTN_BLOCK * dil
    nb = s // rows_per_step
    step_axis = 0 if dil == 1 else 1

    def body(q_ref, kp_ref, kc_ref, vp_ref, vc_ref, y_ref, lse_ref, do_ref, dq_ref, dk_ref, dv_ref, dk_own, dv_own):
        n = pl.program_id(step_axis)
        pair = None if dil == 1 else pl.program_id(0)
        dist = _attn_distance(n == 0)

        @pl.when(n == 0)
        def _():
            dk_own[...] = jnp.zeros_like(dk_own)
            dv_own[...] = jnp.zeros_like(dv_own)

        def one_group(items):
            both = lambda a, b: [jnp.concatenate([a[rows, cols], b[rows, cols]], axis=0).astype(BF16) for rows, cols, _ in items]
            at = lambda ref: [ref[rows, cols] for rows, cols, _ in items]
            outs = _attn_group_bwd(at(q_ref), both(kp_ref, kc_ref), both(vp_ref, vc_ref), at(do_ref), at(y_ref), at(lse_ref),
                                   [_pair_slopes(p) for _, _, p in items], dil, dist)
            for (rows, cols, _), (dq, dk, dv) in zip(items, outs):
                dq_ref[rows, cols] = dq
                dk_ref[rows, cols] = dk_own[rows, cols] + dk[0:blk]
                dv_ref[rows, cols] = dv_own[rows, cols] + dv[0:blk]
                dk_own[rows, cols] = dk[blk:2 * blk]
                dv_own[rows, cols] = dv[blk:2 * blk]

        _attn_loop(dil, one_group, pair)

    width = 512 if dil == 1 else HEAD_LANES

    def spec(at, row_of):
        if dil == 1:
            return pl.BlockSpec((rows_per_step, width), lambda n: (row_of(n), at // width))
        return pl.BlockSpec((rows_per_step, width), lambda p, n: (row_of(n), at // width + p))

    cur = lambda at: spec(at, lambda n: n)
    prev = lambda at: spec(at, lambda n: jnp.maximum(n - 1, 0))
    own = spec(0, lambda n: 0)
    grid = (nb,) if dil == 1 else (4, nb)
    sems = ("arbitrary",) if dil == 1 else ("parallel", "arbitrary")
    dq, dk, dv, dk_last, dv_last = pl.pallas_call(
        body, name=name, grid=grid,
        in_specs=[cur(0), prev(0), cur(0), prev(C_AV), cur(C_AV), cur(0), cur(0), cur(512)],
        out_specs=[cur(0), prev(0), prev(0), own, own],
        out_shape=[jax.ShapeDtypeStruct((s, 512), F32)] * 3 + [jax.ShapeDtypeStruct((rows_per_step, 512), F32)] * 2,
        compiler_params=_params(*sems),
    )(qa, ka, ka, proj, proj, y_att, lse, dmixed)
    return dq, dk.at[s - rows_per_step:].set(dk_last), dv.at[s - rows_per_step:].set(dv_last)


def _attn_post(parts, proj, qg, kg, *, name):
    s = proj.shape[0]
    tm = ROW_TILE
    nblk = s // tm

    def body(*refs):
        ins, (q_ref, k_ref, qg_ref, kg_ref, dq_out, dk_out, dv_out, sums_ref) = refs[:9], refs[9:]
        i = pl.program_id(0)

        @pl.when(i == 0)
        def _():
            sums_ref[...] = jnp.zeros_like(sums_ref)

        dq = (ins[0][...] + ins[3][...]) + ins[6][...]
        dk = (ins[1][...] + ins[4][...]) + ins[7][...]
        dv = (ins[2][...] + ins[5][...]) + ins[8][...]
        dv_out[...] = dv.astype(BF16)
        for row, (x_ref, g_ref, dy, out, post) in enumerate(((q_ref, qg_ref, dq, dq_out, 0.125), (k_ref, kg_ref, dk, dk_out, 1.0))):
            x = x_ref[...]
            rs = lax.rsqrt(_head_sums(x * x) * (1.0 / 64) + EPS)
            xn = x * rs
            dy = dy * post
            sums_ref[row] += _fold8(dy * xn)
            dn = dy * g_ref[...]
            out[...] = (rs * (dn - xn * (_head_sums(dn * xn) * (1.0 / 64)))).astype(BF16)

        @pl.when(i == nblk - 1)
        def _():
            _spread_total(sums_ref)

    here = pl.BlockSpec((tm, 512), lambda i: (i, 0))
    col = lambda at: pl.BlockSpec((tm, 512), lambda i: (i, at // 512))
    vec = pl.BlockSpec((1, 512), lambda i: (0, 0))
    return pl.pallas_call(
        body, name=name, grid=(nblk,), in_specs=[here] * 9 + [col(C_AQ), col(C_AK), vec, vec],
        out_specs=[here, here, here, pl.BlockSpec((2, 8, 512), lambda i: (0, 0, 0))],
        out_shape=[jax.ShapeDtypeStruct((s, 512), BF16)] * 3 + [jax.ShapeDtypeStruct((2, 8, 512), F32)],
        compiler_params=_params("arbitrary"))(*[t for part in parts for t in part], proj, proj, qg, kg)


FFN_TM, FFN_TN = 256, 1408
HALO = 16


def _conv3(u_ref, halo_ref, w_ref, b_ref, first):
    u = u_ref[...].astype(F32)
    ext = jnp.concatenate([jnp.where(first, 0.0, halo_ref[...].astype(F32)), u], axis=0)
    u1 = pltpu.roll(ext, 1, 0)[HALO:]
    u2 = pltpu.roll(ext, 2, 0)[HALO:]
    return b_ref[...] + w_ref[0:1, :] * u2 + w_ref[1:2, :] * u1 + w_ref[2:3, :] * u


def _ffn_specs(tm, tn):
    nj = D_FF // tn
    blk = lambda half: pl.BlockSpec((tm, tn), lambda j, i: (i, j + half * nj))
    halo = lambda half: pl.BlockSpec((HALO, tn), lambda j, i: (jnp.maximum(i * (tm // HALO) - 1, 0), j + half * nj))
    wspec = lambda half: pl.BlockSpec((3, tn), lambda j, i: (0, j + half * nj))
    bspec = lambda half: pl.BlockSpec((1, tn), lambda j, i: (0, j + half * nj))
    return [blk(0), halo(0), blk(1), halo(1), wspec(0), wspec(1), bspec(0), bspec(1)]


def _conv_swiglu_fwd(u, conv_w, conv_b, *, name):
    s = u.shape[0]
    tm, tn = FFN_TM, FFN_TN

    def body(ug_ref, hg_ref, uv_ref, hv_ref, wg_ref, wv_ref, bg_ref, bv_ref, act_ref, uc_ref):
        first = pl.program_id(1) == 0
        cg = _conv3(ug_ref, hg_ref, wg_ref, bg_ref, first)
        cv = _conv3(uv_ref, hv_ref, wv_ref, bv_ref, first)
        act_ref[...] = (cg * _sigmoid(cg) * cv).astype(BF16)
        uc_ref[0] = cg.astype(BF16)
        uc_ref[1] = cv.astype(BF16)

    return pl.pallas_call(
        body, name=name, grid=(D_FF // tn, s // tm), in_specs=_ffn_specs(tm, tn),
        out_specs=[pl.BlockSpec((tm, tn), lambda j, i: (i, j)), pl.BlockSpec((2, tm, tn), lambda j, i: (0, i, j))],
        out_shape=[jax.ShapeDtypeStruct((s, D_FF), BF16), jax.ShapeDtypeStruct((2, s, D_FF), BF16)],
        compiler_params=_params("parallel", "parallel"))(u, u, u, u, conv_w, conv_w, conv_b, conv_b)


def _swiglu_bwd(uc, dact, *, name):
    _, s, _ = uc.shape
    tm, tn = FFN_TM, FFN_TN

    def body(uc_ref, da_ref, duc_ref, sums_ref):
        i = pl.program_id(1)

        @pl.when(i == 0)
        def _():
            sums_ref[...] = jnp.zeros_like(sums_ref)

        cg, cv, da = uc_ref[0].astype(F32), uc_ref[1].astype(F32), da_ref[...].astype(F32)
        sg = _sigmoid(cg)
        dg = da * cv * (sg * (1.0 + cg * (1.0 - sg)))
        dv = da * (cg * sg)
        duc_ref[0] = dg.astype(BF16)
        duc_ref[1] = dv.astype(BF16)
        sums_ref[0] += _fold8(dg)
        sums_ref[1] += _fold8(dv)

        @pl.when(i == s // tm - 1)
        def _():
            _spread_total(sums_ref)

    pair = pl.BlockSpec((2, tm, tn), lambda j, i: (0, i, j))
    return pl.pallas_call(
        body, name=name, grid=(D_FF // tn, s // tm), in_specs=[pair, pl.BlockSpec((tm, tn), lambda j, i: (i, j))],
        out_specs=[pair, pl.BlockSpec((2, 8, tn), lambda j, i: (0, 0, j))],
        out_shape=[jax.ShapeDtypeStruct((2, s, D_FF), BF16), jax.ShapeDtypeStruct((2, 8, D_FF), F32)],
        compiler_params=_params("parallel", "arbitrary"))(uc, dact)


def _conv_bwd(duc, u, conv_w, *, name):
    _, s, _ = duc.shape
    tm, tn = FFN_TM, FFN_TN
    nj, ni = D_FF // tn, s // tm

    def body(d_ref, halo_ref, u_ref, w_ref, du_ref, sums_ref):
        i = pl.program_id(2)

        @pl.when(i == 0)
        def _():
            sums_ref[...] = jnp.zeros_like(sums_ref)

        d = d_ref[0].astype(F32)
        ext = jnp.concatenate([d, jnp.where(i == ni - 1, 0.0, halo_ref[0].astype(F32))], axis=0)
        n = tm + HALO
        d1 = pltpu.roll(ext, n - 1, 0)[:tm]
        d2 = pltpu.roll(ext, n - 2, 0)[:tm]
        du_ref[...] = (w_ref[2:3, :] * d + w_ref[1:2, :] * d1 + w_ref[0:1, :] * d2).astype(BF16)
        uv = u_ref[...].astype(F32)
        for t, shifted in enumerate((d2, d1, d)):
            sums_ref[0, t] += _fold8(shifted * uv)

        @pl.when(i == ni - 1)
        def _():
            _spread_total(sums_ref)

    return pl.pallas_call(
        body, name=name, grid=(2, nj, ni),
        in_specs=[pl.BlockSpec((1, tm, tn), lambda g, j, i: (g, i, j)),
                  pl.BlockSpec((1, HALO, tn), lambda g, j, i: (g, jnp.minimum((i + 1) * (tm // HALO), s // HALO - 1), j)),
                  pl.BlockSpec((tm, tn), lambda g, j, i: (i, g * nj + j)),
                  pl.BlockSpec((3, tn), lambda g, j, i: (0, g * nj + j))],
        out_specs=[pl.BlockSpec((tm, tn), lambda g, j, i: (i, g * nj + j)),
                   pl.BlockSpec((1, 3, 8, tn), lambda g, j, i: (g, 0, 0, j))],
        out_shape=[jax.ShapeDtypeStruct((s, 2 * D_FF), BF16), jax.ShapeDtypeStruct((2, 3, 8, D_FF), F32)],
        compiler_params=_params("parallel", "parallel", "arbitrary"))(duc, duc, u, conv_w)


def _loss_head(x1, ffn, gate, target, *, name):
    s, d = x1.shape
    tm = ROW_TILE

    def body(x_ref, f_ref, g_ref, t_ref, dy_ref, df_ref, sums_ref):
        i = pl.program_id(0)

        @pl.when(i == 0)
        def _():
            sums_ref[...] = jnp.zeros_like(sums_ref)

        f = f_ref[...]
        err = x_ref[...] + g_ref[...] * f - t_ref[...]
        dy = err * (1.0 / d)
        dy_ref[...] = dy
        df_ref[...] = (g_ref[...] * dy).astype(BF16)
        sums_ref[0] += _fold8(dy * f)
        sums_ref[1] += _fold8(err * err)

        @pl.when(i == s // tm - 1)
        def _():
            _spread_total(sums_ref)

    row = pl.BlockSpec((tm, d), lambda i: (i, 0))
    return pl.pallas_call(
        body, name=name, grid=(s // tm,), in_specs=[row, row, pl.BlockSpec((1, d), lambda i: (0, 0)), row],
        out_specs=[row, row, pl.BlockSpec((2, 8, d), lambda i: (0, 0, 0))],
        out_shape=[jax.ShapeDtypeStruct((s, d), F32), jax.ShapeDtypeStruct((s, d), BF16), jax.ShapeDtypeStruct((2, 8, d), F32)],
        compiler_params=_params("arbitrary"))(x1, ffn, gate, target)


def _adamw(w, g, m, v, *, name):
    rows, cols = w.shape
    if rows % 8 == 0 or rows <= ROW_TILE:
        tm = next((t for t in range(ROW_TILE, 7, -8) if rows % t == 0), rows)
        blk, grid = pl.BlockSpec((tm, cols), lambda i: (i, 0)), (rows // tm,)
    else:
        blk, grid = pl.BlockSpec((rows, ROW_TILE), lambda i: (0, i)), (cols // ROW_TILE,)

    def body(w_ref, g_ref, m_ref, v_ref, d_ref, mo_ref, vo_ref):
        gv = g_ref[...]
        mn = ADAM_B1 * m_ref[...] + (1.0 - ADAM_B1) * gv
        vn = ADAM_B2 * v_ref[...] + (1.0 - ADAM_B2) * (gv * gv)
        m_hat = mn / (1.0 - ADAM_B1 ** ADAM_STEP)
        v_hat = vn / (1.0 - ADAM_B2 ** ADAM_STEP)
        d_ref[...] = -ADAM_LR * (m_hat / (jnp.sqrt(v_hat) + ADAM_EPS) + ADAM_WD * w_ref[...])
        mo_ref[...] = mn
        vo_ref[...] = vn

    return pl.pallas_call(
        body, name=name, grid=grid, in_specs=[blk] * 4, out_specs=[blk] * 3,
        out_shape=[jax.ShapeDtypeStruct((rows, cols), F32)] * 3, compiler_params=_params("parallel"))(w, g, m, v)


def _colsum(t):
    return t[..., 0, :]


def _in_proj_layout(w_in):
    pad = jnp.zeros((w_in.shape[0], PROJ_W - C_LR - GLA_GATE_RANK), w_in.dtype)
    return jnp.concatenate([w_in[:, :1536], w_in[:, 1552:], w_in[:, 1536:1552], pad], axis=1)


def _in_proj_grad_layout(g):
    return jnp.concatenate([g[:, :1536], g[:, C_LR:C_LR + GLA_GATE_RANK], g[:, 1536:C_LR]], axis=1)


def _gate_layout(gla_w_gate):
    return jnp.pad(gla_w_gate, ((0, HEAD_LANES - GLA_GATE_RANK), (0, 0))).astype(BF16)


def _local_step(x, target, mod, wi, late_weights, ffn_grads_ready, attn_grads_ready, conv_w, conv_b, wg, bg, gn, qg, kg, n1g, n2g):
    d = D_MODEL
    sh1, sc1, g1, sh2, sc2, g2 = [mod[:, i * d:(i + 1) * d] for i in range(6)]
    qg8, kg8 = jnp.tile(qg, (1, 8)), jnp.tile(kg, (1, 8))

    _, h1, h1_t = _norm_mod_fwd(x, None, None, n1g, sc1, sh1, name="norm1_fwd")
    proj = _mm(h1, wi, tm=1024, tn=PROJ_W, tk=d, name="in_proj")
    o_raw, y_gla, states = _gla_fwd(proj, wg, bg, gn, name="gla_fwd")
    qa, ka = _attn_prep(proj, qg8, kg8, name="attn_prep")
    branches = [_dil_attn_fwd(qa, ka, proj, dil, name=f"attn_fwd_d{dil}") for dil in DILATIONS]
    mixed, y_att, lse = _attn_merge(branches, y_gla, name="attn_merge")
    wo, wup, wdown = late_weights(mixed)
    attn_out = _mm(mixed, wo, tm=1024, tn=d, tk=d, name="out_proj")
    x1, h2, h2_t = _norm_mod_fwd(x, attn_out, g1, n2g, sc2, sh2, name="norm2_fwd")
    u = _mm(h2, wup, out_dtype=BF16, tm=1024, tn=D_FF, tk=d, name="up_proj")
    act, uc = _conv_swiglu_fwd(u, conv_w, conv_b, name="conv_swiglu_fwd")
    ffn = _mm(act, wdown, tm=1024, tn=d, tk=D_FF, name="down_proj")
    dy, dffn, head_sums = _loss_head(x1, ffn, g2, target, name="loss_head")

    dact = _mm(dffn, wdown, tb=True, out_dtype=BF16, tm=1024, tn=D_FF, tk=d, name="down_proj_dx")
    g_wdown, g_wdown_b = _mm(act, dffn, ta=True, tm=1408, tn=d, tk=2048, also_bf16=True, name="down_proj_dw")
    duc, bias_sums = _swiglu_bwd(uc, dact, name="swiglu_bwd")
    du, tap_sums = _conv_bwd(duc, u, conv_w, name="conv_bwd")
    dh2 = _mm(du, wup, tb=True, tm=1024, tn=d, tk=D_FF, name="up_proj_dx")
    g_wup, g_wup_b = _mm(h2_t, du, tm=d, tn=1408, tk=2048, shard_cols=True, also_bf16=True, name="up_proj_dw")
    token = ffn_grads_ready(g_wup_b, g_wdown_b)
    g1_late = g1 if token is None else g1 + token[0:1, 0:1]
    dx1, dao, n2_sums = _norm_mod_bwd(x1, dh2, dy, n2g, sc2, attn_out, g1_late, name="norm2_bwd")

    dmixed = _mm(dao, wo, tb=True, tm=1024, tn=d, tk=d, name="out_proj_dx")
    g_wo = _mm(mixed, dao, ta=True, tm=d, tn=d, tk=1024, name="out_proj_dw")
    dgq, dgk, dgv, dgr, dlr, g_wg, gla_sums = _gla_bwd(proj, wg, bg, gn, o_raw, states, dmixed, name="gla_bwd")
    parts = [_dil_attn_bwd(qa, ka, proj, y_att, lse, dmixed, dil, name=f"attn_bwd_d{dil}") for dil in DILATIONS]
    daq, dak, dav, qk_sums = _attn_post(parts, proj, qg8, kg8, name="attn_post")
    dproj = jnp.concatenate([dgq, dgk, dgv, dgr, daq, dak, dav, dlr], axis=1)
    g_wi = _mm(h1_t, dproj, tm=512, tn=PROJ_W, tk=2048, name="in_proj_dw")
    token = attn_grads_ready(g_wi, g_wo)
    wi_late = wi if token is None else wi + token[0:1, 0:1].astype(BF16)
    dh1 = _mm(dproj, wi_late, tb=True, tm=1024, tn=d, tk=PROJ_W, name="in_proj_dx")
    grad_x, _, n1_sums = _norm_mod_bwd(x, dh1, dx1, n1g, sc1, None, None, name="norm1_bwd")

    n1, n2, hs, taps, cb = _colsum(n1_sums), _colsum(n2_sums), _colsum(head_sums), _colsum(tap_sums), _colsum(bias_sums)
    gs, qs = _colsum(gla_sums), _colsum(qk_sums)
    dmod = jnp.concatenate([n1[1], n1[0] * n1g[0], n2[2], n2[1], n2[0] * n2g[0], hs[0]])
    small = dict(
        dmod=dmod,
        norm1_g=n1[0] * (1.0 + sc1[0]), norm2_g=n2[0] * (1.0 + sc2[0]),
        gla_w_gate=g_wg[:GLA_GATE_RANK], gla_b_gate=gs[0, :256], gla_norm_g=gs[1].reshape(4, 128).sum(axis=0),
        q_norm_g=qs[0].reshape(8, 64).sum(axis=0), k_norm_g=qs[1].reshape(8, 64).sum(axis=0),
        conv_w=jnp.concatenate([taps[0], taps[1]], axis=1), conv_b=jnp.concatenate([cb[0], cb[1]]),
    )
    return head_sums[1], grad_x, (g_wi, g_wo, g_wup, g_wdown), small


N_DEV, N_CHIP = 8, 4
ANY = pl.BlockSpec(memory_space=pl.ANY)
VMEM_SPEC = pl.BlockSpec(memory_space=pltpu.VMEM)


def _place():
    x, y, c = lax.axis_index("x"), lax.axis_index("y"), lax.axis_index("c")
    other_chips = [(1 - x, y), (x, 1 - y), (1 - x, 1 - y)]
    return x, y, c, (x, y, 1 - c), other_chips


def _all_gather_small(v, *, name):
    m, n = v.shape

    def body(v_ref, out_ref, send_sems, recv_sems, local_sem):
        x, y, c, sibling, chips = _place()
        me = (x, y, c)

        def rows(px, py, pc):
            return out_ref.at[pl.ds((4 * px + 2 * py + pc) * m, m), :]

        def copy(k, block, to, src=None):
            return pltpu.make_async_remote_copy(
                src_ref=rows(*block) if src is None else src, dst_ref=rows(*block), send_sem=send_sems.at[k],
                recv_sem=recv_sems.at[k], device_id=to, device_id_type=MESH)

        mine = pltpu.make_async_copy(v_ref, rows(*me), local_sem)
        mine.start()
        first = [copy(0, me, sibling, src=v_ref)]
        first += [copy(1 + j, me, (*chip, c), src=v_ref) for j, chip in enumerate(chips)]
        for cp in first:
            cp.start()
        passed = [copy(4 + j, (*chip, c), sibling) for j, chip in enumerate(chips)]
        for j, chip in enumerate(chips):
            copy(1 + j, (*chip, c), me).wait_recv()
            passed[j].start()
        copy(0, sibling, me).wait_recv()
        for j, chip in enumerate(chips):
            copy(4 + j, (*chip, 1 - c), me).wait_recv()
        for cp in first + passed:
            cp.wait_send()
        mine.wait()

    return pl.pallas_call(
        body, name=name, out_shape=jax.ShapeDtypeStruct((N_DEV * m, n), v.dtype), in_specs=[VMEM_SPEC], out_specs=VMEM_SPEC,
        scratch_shapes=[pltpu.SemaphoreType.DMA((7,)), pltpu.SemaphoreType.DMA((7,)), pltpu.SemaphoreType.DMA],
    )(v)


def _gather_weight_shards(shards, *, name):
    nw = len(shards)

    def body(*refs):
        srcs, outs, (send_sems, recv_sems) = refs[:nw], refs[nw:2 * nw], refs[2 * nw:]
        x, y, c, sibling, chips = _place()
        index = lambda chip: 2 * chip[0] + chip[1]

        def copy(w, k, src, dst, to):
            return pltpu.make_async_remote_copy(src_ref=src, dst_ref=dst, send_sem=send_sems.at[6 * w + k],
                                                recv_sem=recv_sems.at[6 * w + k], device_id=to, device_id_type=MESH)

        sent = []
        for w, (src_ref, out_ref) in enumerate(zip(srcs, outs)):
            for k, chip in enumerate(chips):
                sent.append(copy(w, k, src_ref.at[c], out_ref.at[2 * x + y, c], (*chip, c)))
                sent[-1].start()
        for w, out_ref in enumerate(outs):
            for k, chip in enumerate(chips):
                landed = out_ref.at[index(chip), c]
                copy(w, k, landed, landed, (*chip, c)).wait_recv()
                sent.append(copy(w, 3 + k, landed, landed, sibling))
                sent[-1].start()
        for w, out_ref in enumerate(outs):
            for k, chip in enumerate(chips):
                passed_on = out_ref.at[index(chip), 1 - c]
                copy(w, 3 + k, passed_on, passed_on, sibling).wait_recv()
        for cp in sent:
            cp.wait_send()

    return pl.pallas_call(
        body, name=name, out_shape=[jax.ShapeDtypeStruct((N_CHIP, *s.shape), s.dtype) for s in shards],
        in_specs=[ANY] * nw, out_specs=[ANY] * nw,
        scratch_shapes=[pltpu.SemaphoreType.DMA((6 * nw,)), pltpu.SemaphoreType.DMA((6 * nw,))],
    )(*shards)


HBM_SPEC = pl.BlockSpec(memory_space=pltpu.HBM)
SEM_SPEC = pl.BlockSpec(memory_space=pltpu.SEMAPHORE)
DATAFLOW_EFFECT = pltpu.SideEffectType.DATAFLOW_SIDE_EFFECTING


def _late_copies(srcs, lands, send_sems, recv_sems):
    x, y, c, _, chips = _place()
    return [pltpu.make_async_remote_copy(
        src_ref=src.at[c], dst_ref=land.at[2 * x + y, c], send_sem=send_sems.at[6 * w + 2 * r + core],
        recv_sem=recv_sems.at[6 * w + 2 * r + c], device_id=(*chip, core), device_id_type=MESH)
        for w, (src, land) in enumerate(zip(srcs, lands)) for r, chip in enumerate(chips) for core in range(2)]


def _gather_late_start(own, after, *, name):
    nw = len(own)

    def body(*refs):
        srcs, lands, send_sems, recv_sems, token = refs[:nw], refs[nw:2 * nw], refs[2 * nw + 1], refs[2 * nw + 2], refs[-1]
        for cp in _late_copies(srcs, lands, send_sems, recv_sems):
            cp.start()
        token[...] = jnp.zeros_like(token)

    lands = [pltpu.with_memory_space_constraint(lax.empty((N_CHIP, *s.shape), s.dtype), pltpu.HBM) for s in own]
    own = [pltpu.with_memory_space_constraint(s, pltpu.HBM) for s in own]
    out = pl.pallas_call(
        body, name=name,
        out_shape=(pltpu.SemaphoreType.DMA((6 * nw,)), pltpu.SemaphoreType.DMA((6 * nw,)),
                   *[pltpu.HBM(s.shape, s.dtype) for s in own], *[pltpu.HBM(s.shape, s.dtype) for s in lands],
                   jax.ShapeDtypeStruct((8, 128), F32)),
        in_specs=[HBM_SPEC] * (2 * nw) + [ANY], out_specs=(SEM_SPEC, SEM_SPEC, *[HBM_SPEC] * (2 * nw), VMEM_SPEC),
        input_output_aliases={i: 2 + i for i in range(2 * nw)},
        compiler_params=pltpu.CompilerParams(has_side_effects=DATAFLOW_EFFECT))(*own, *lands, after)
    return out[0], out[1], out[2:2 + nw], out[2 + nw:2 + 2 * nw], out[-1]


def _gather_late_wait(send_sems, recv_sems, own, lands, after, *, name):
    nw = len(own)

    def body(*refs):
        srcs, lands_in, send_sems, recv_sems = refs[:nw], refs[nw:2 * nw], refs[2 * nw], refs[2 * nw + 1]
        x, y, c, _, chips = _place()
        for cp in _late_copies(srcs, lands_in, send_sems, recv_sems):
            cp.wait_send()
        for w, (src, land) in enumerate(zip(srcs, lands_in)):
            for r, chip in enumerate(chips):
                for core in range(2):
                    pltpu.make_async_remote_copy(
                        src_ref=src.at[c], dst_ref=land.at[2 * chip[0] + chip[1], core], send_sem=send_sems.at[6 * w + 2 * r + core],
                        recv_sem=recv_sems.at[6 * w + 2 * r + core], device_id=(*chip, core), device_id_type=MESH).wait_recv()

    out = pl.pallas_call(
        body, name=name, out_shape=(*[pltpu.HBM(s.shape, s.dtype) for s in own], *[pltpu.HBM(s.shape, s.dtype) for s in lands]),
        in_specs=[HBM_SPEC] * (2 * nw) + [SEM_SPEC, SEM_SPEC, ANY], out_specs=tuple([HBM_SPEC] * (2 * nw)),
        input_output_aliases={i: i for i in range(2 * nw)},
        compiler_params=pltpu.CompilerParams(has_side_effects=DATAFLOW_EFFECT))(*own, *lands, send_sems, recv_sems, after)
    return out[:nw], out[nw:]


def _direct_reduce_copies(srcs, lands, send_sems, recv_sems):
    x, y, c, _, _ = _place()
    cps = []
    for w, (src, land) in enumerate(zip(srcs, lands)):
        for rel in range(1, N_DEV):
            tx, ty, tc = (1 - x if rel & 4 else x), (1 - y if rel & 2 else y), (1 - c if rel & 1 else c)
            cps.append(pltpu.make_async_remote_copy(
                src_ref=src.at[2 * tx + ty, tc], dst_ref=land.at[rel - 1], send_sem=send_sems.at[7 * w + rel - 1],
                recv_sem=recv_sems.at[7 * w + rel - 1], device_id=(tx, ty, tc), device_id_type=MESH))
    return cps


def _direct_reduce_start(grads, *, name):
    nw = len(grads)

    def body(*refs):
        srcs, lands, send_sems, recv_sems, token = refs[:nw], refs[nw:2 * nw], refs[2 * nw], refs[2 * nw + 1], refs[-1]
        for cp in _direct_reduce_copies(srcs, lands, send_sems, recv_sems):
            cp.start()
        token[...] = jnp.zeros_like(token)

    lands = [pltpu.with_memory_space_constraint(lax.empty((N_DEV - 1, *g.shape[2:]), g.dtype), pltpu.HBM) for g in grads]
    grads = [pltpu.with_memory_space_constraint(g, pltpu.HBM) for g in grads]
    out = pl.pallas_call(
        body, name=name,
        out_shape=(pltpu.SemaphoreType.DMA((7 * nw,)), pltpu.SemaphoreType.DMA((7 * nw,)),
                   *[pltpu.HBM(g.shape, g.dtype) for g in grads], *[pltpu.HBM(t.shape, t.dtype) for t in lands],
                   jax.ShapeDtypeStruct((8, 128), F32)),
        in_specs=[HBM_SPEC] * (2 * nw), out_specs=(SEM_SPEC, SEM_SPEC, *[HBM_SPEC] * (2 * nw), VMEM_SPEC),
        input_output_aliases={i: 2 + i for i in range(2 * nw)},
        compiler_params=pltpu.CompilerParams(has_side_effects=DATAFLOW_EFFECT))(*grads, *lands)
    return out[0], out[1], out[2:2 + nw], out[2 + nw:2 + 2 * nw], out[-1]


def _direct_reduce_wait(send_sems, recv_sems, grads, lands, after, *, name):
    nw = len(grads)

    def body(*refs):
        srcs, lands_in, send_sems, recv_sems = refs[:nw], refs[nw:2 * nw], refs[2 * nw], refs[2 * nw + 1]
        cps = _direct_reduce_copies(srcs, lands_in, send_sems, recv_sems)
        for cp in cps:
            cp.wait_send()
        for cp in cps:
            cp.wait_recv()

    out = pl.pallas_call(
        body, name=name, out_shape=(*[pltpu.HBM(g.shape, g.dtype) for g in grads], *[pltpu.HBM(t.shape, t.dtype) for t in lands]),
        in_specs=[HBM_SPEC] * (2 * nw) + [SEM_SPEC, SEM_SPEC, ANY], out_specs=tuple([HBM_SPEC] * (2 * nw)),
        input_output_aliases={i: i for i in range(2 * nw)},
        compiler_params=pltpu.CompilerParams(has_side_effects=DATAFLOW_EFFECT))(*grads, *lands, send_sems, recv_sems, after)
    return out[nw:]


def _direct_reduce_add(grad, landed, chip, core, *, name):
    _, r, n = grad.shape
    half = r // 2
    tr = _row_tile(half)
    nb = half // tr

    def body(chip_ref, core_ref, g_ref, t_ref, o_ref):
        acc = g_ref[0]
        for k in range(N_DEV - 1):
            acc = acc + t_ref[k].astype(F32)
        o_ref[...] = acc

    return pl.pallas_call(
        body, name=name,
        grid_spec=pltpu.PrefetchScalarGridSpec(
            num_scalar_prefetch=2, grid=(nb,),
            in_specs=[pl.BlockSpec((1, tr, n), lambda i, chip_ref, core_ref: (chip_ref[0], core_ref[0] * nb + i, 0)),
                      pl.BlockSpec((N_DEV - 1, tr, n), lambda i, chip_ref, core_ref: (0, i, 0))],
            out_specs=pl.BlockSpec((tr, n), lambda i, chip_ref, core_ref: (i, 0))),
        out_shape=jax.ShapeDtypeStruct((half, n), F32), compiler_params=_params("parallel"))(chip, core, grad, landed)


def _share_halves(halves, *, name):
    nw = len(halves)

    def body(*refs):
        srcs, outs, (send_sems, recv_sems) = refs[:nw], refs[nw:2 * nw], refs[2 * nw:]
        _, _, _, sibling, _ = _place()
        cps = [pltpu.make_async_remote_copy(src_ref=src_ref, dst_ref=out_ref, send_sem=send_sems.at[w], recv_sem=recv_sems.at[w],
                                            device_id=sibling, device_id_type=MESH)
               for w, (src_ref, out_ref) in enumerate(zip(srcs, outs))]
        for cp in cps:
            cp.start()
        for cp in cps:
            cp.wait()

    return pl.pallas_call(
        body, name=name, out_shape=[jax.ShapeDtypeStruct(h.shape, h.dtype) for h in halves],
        in_specs=[ANY] * nw, out_specs=[ANY] * nw,
        scratch_shapes=[pltpu.SemaphoreType.DMA((nw,)), pltpu.SemaphoreType.DMA((nw,))])(*halves)


def _row_tile(rows, limit=256):
    return next(t for t in range(limit, 15, -16) if rows % t == 0)


def _sum_devices(gathered, *, name):
    _, m, n = gathered.shape

    def body(g_ref, tot_ref, loss_ref):
        tot = g_ref[0]
        for dev in range(1, N_DEV):
            tot = tot + g_ref[dev]
        tot_ref[...] = tot
        loss_ref[...] = jnp.full((8, n), (0.5 / D_MODEL) * jnp.sum(tot[0:8]), F32)

    return pl.pallas_call(body, name=name, in_specs=[VMEM_SPEC], out_specs=[VMEM_SPEC, VMEM_SPEC],
                          out_shape=[jax.ShapeDtypeStruct((m, n), F32), jax.ShapeDtypeStruct((8, n), F32)])(gathered)


def _ada_mod(cond_all, w_ada_shard, *, name):
    tn = 512

    def body(a_ref, b_ref, o_ref):
        o_ref[...] = _nn(a_ref[...], b_ref[...], precision=HIGHEST)

    return pl.pallas_call(
        body, name=name, grid=(w_ada_shard.shape[1] // tn,),
        in_specs=[pl.BlockSpec(cond_all.shape, lambda j: (0, 0)), pl.BlockSpec((D_MODEL, tn), lambda j: (0, j))],
        out_specs=pl.BlockSpec((N_DEV, tn), lambda j: (0, j)),
        out_shape=jax.ShapeDtypeStruct((N_DEV, w_ada_shard.shape[1]), F32), compiler_params=_params("parallel"))(cond_all, w_ada_shard)


def _ada_grad(cond_all, dmod_cols, *, name):
    tm = 256

    def body(a_ref, b_ref, o_ref):
        o_ref[...] = lax.dot_general(a_ref[...], b_ref[...], (((0,), (0,)), ((), ())), precision=HIGHEST,
                                     preferred_element_type=F32)

    return pl.pallas_call(
        body, name=name, grid=(D_MODEL // tm,),
        in_specs=[pl.BlockSpec((N_DEV, tm), lambda i: (0, i)), pl.BlockSpec(dmod_cols.shape, lambda i: (0, 0))],
        out_specs=pl.BlockSpec((tm, dmod_cols.shape[1]), lambda i: (i, 0)),
        out_shape=jax.ShapeDtypeStruct((D_MODEL, dmod_cols.shape[1]), F32), compiler_params=_params("parallel"))(cond_all, dmod_cols)


def _silu_rows(c8, *, name):
    def body(c_ref, o_ref):
        cv = c_ref[...]
        o_ref[...] = cv * _sigmoid(cv)

    return pl.pallas_call(body, name=name, in_specs=[VMEM_SPEC], out_specs=VMEM_SPEC,
                          out_shape=jax.ShapeDtypeStruct(c8.shape, F32))(c8)


def _rows128(t, rows=None):
    flat = t.reshape(-1, 128)
    return flat if rows is None else jnp.pad(flat, ((0, rows - flat.shape[0]), (0, 0)))


def _from_col_shards(shards, r, n):
    return shards.reshape(N_CHIP, r, n).transpose(1, 0, 2).reshape(r, N_CHIP * n)


def kernel(x, c, w_ada, b_ada, norm1_g, w_in, gla_w_gate, gla_b_gate, gla_norm_g, q_norm_g, k_norm_g, w_out, norm2_g, w_up, conv_w, conv_b, w_down, loss_target, m_w_ada, m_b_ada, m_norm1_g, m_w_in, m_gla_w_gate, m_gla_b_gate, m_gla_norm_g, m_q_norm_g, m_k_norm_g, m_w_out, m_norm2_g, m_w_up, m_conv_w, m_conv_b, m_w_down, v_w_ada, v_b_ada, v_norm1_g, v_w_in, v_gla_w_gate, v_gla_b_gate, v_gla_norm_g, v_q_norm_g, v_k_norm_g, v_w_out, v_norm2_g, v_w_up, v_conv_w, v_conv_b, v_w_down):
    d = D_MODEL
    ax, ay, ac = lax.axis_index("x"), lax.axis_index("y"), lax.axis_index("c")
    chip, dev = 2 * ax + ay, 4 * ax + 2 * ay + ac

    cond = _silu_rows(jnp.broadcast_to(c, (8, d)), name="cond_silu")[0:1]
    small_in = jnp.concatenate([_rows128(cond), _rows128(conv_w[0]), _rows128(gla_w_gate[0])], axis=0)
    small_in = _rows128(small_in, 56)
    got = _all_gather_small(small_in, name="gather_small").reshape(N_DEV, 56, 128)
    cond_all = got[:, 0:8].reshape(N_DEV, d)
    conv_w_full = _from_col_shards(got[0::2, 8:41].reshape(N_CHIP, 3 * 1408 // 128, 128), 3, 1408)
    gate_full = _from_col_shards(got[0::2, 41:49].reshape(N_CHIP, 16 * 64 // 128, 128), GLA_GATE_RANK, 64)
    mod_part = _ada_mod(cond_all, w_ada[0], name="ada_mod")
    mod_got = _all_gather_small(_rows128(mod_part), name="gather_mod").reshape(N_DEV, N_DEV, 1536)
    mod_all = mod_got[0::2].transpose(1, 0, 2).reshape(N_DEV, 6 * d) + b_ada
    mod = lax.dynamic_slice_in_dim(mod_all, dev, 1, axis=0)

    own = [w[0].astype(BF16).reshape(2, w.shape[1] // 2, w.shape[2]) for w in (w_in, w_out, w_up, w_down)]
    with_own = lambda got, mine: [lax.dynamic_update_index_in_dim(t, o, chip, 0) for t, o in zip(got, mine)]
    got_in, = with_own(_gather_weight_shards(own[:1], name="gather_weights"), own[:1])
    w_in_full = got_in.reshape(N_CHIP, d, 772).transpose(1, 0, 2).reshape(d, N_CHIP * 772)
    exchanged = mod_all[0:1, 0:1] + got_in[0, 0, 0:1, 0:1].astype(F32)
    send_sems, recv_sems, own_thru, lands, token = _gather_late_start(own[1:], exchanged, name="gather_late_start")
    mod = mod + token[0:1, 0:1]

    def late_weights(after):
        mine, landed = _gather_late_wait(send_sems, recv_sems, own_thru, lands, after, name="gather_late_wait")
        got_out, got_up, got_down = with_own(landed, mine)
        return (got_out.reshape(d, d), got_up.reshape(N_CHIP, d, 1408).transpose(1, 0, 2).reshape(d, 2 * D_FF),
                got_down.reshape(D_FF, d))

    ffn_reduce, attn_reduce, attn_parts = [], [], []
    halves_of = lambda g: g.reshape(N_CHIP, 2, g.shape[-2] // 2, g.shape[-1])

    def ffn_grads_ready(g_wup_b, g_wdown_b):
        ffn_reduce.extend(_direct_reduce_start([halves_of(g_wup_b), halves_of(g_wdown_b.reshape(N_CHIP, D_FF // N_CHIP, d))],
                                               name="reduce_ffn_start"))
        return ffn_reduce[4]

    def attn_grads_ready(g_wi, g_wo):
        attn_parts.extend([_in_proj_grad_layout(g_wi).reshape(d, N_CHIP, 772).transpose(1, 0, 2), g_wo.reshape(N_CHIP, d // N_CHIP, d)])
        attn_reduce.extend(_direct_reduce_start([halves_of(g.astype(BF16)) for g in attn_parts], name="reduce_attn_start"))
        return attn_reduce[4]

    err2, grad_x, (g_wi, g_wo, g_wup, g_wdown), small = _local_step(
        x[0], loss_target[0], mod, _in_proj_layout(w_in_full), late_weights, ffn_grads_ready, attn_grads_ready,
        conv_w_full, conv_b,
        _gate_layout(gate_full), gla_b_gate, gla_norm_g, q_norm_g, k_norm_g, norm1_g, norm2_g)

    pieces = [err2[0], small["dmod"], small["norm1_g"], small["norm2_g"], small["gla_w_gate"].reshape(-1), small["gla_b_gate"],
              small["gla_norm_g"], small["q_norm_g"], small["k_norm_g"], small["conv_w"].reshape(-1), small["conv_b"]]
    sizes = [p.shape[0] for p in pieces]
    at = [sum(sizes[:i]) for i in range(len(sizes) + 1)]
    vec = _rows128(jnp.concatenate(pieces), 288)
    got = _all_gather_small(vec, name="gather_grads").reshape(N_DEV, 288, 128)
    total, loss8 = _sum_devices(got, name="sum_devices")
    total = total.reshape(-1)
    seg = lambda i: total[at[i]:at[i + 1]]
    dmod_all = got.reshape(N_DEV, -1)[:, at[1]:at[2]]
    g_small = dict(
        b_ada=seg(1)[None], norm1_g=seg(2)[None], norm2_g=seg(3)[None],
        gla_w_gate=lax.dynamic_slice_in_dim(seg(4).reshape(GLA_GATE_RANK, 256), chip * 64, 64, axis=1),
        gla_b_gate=seg(5)[None], gla_norm_g=seg(6)[None], q_norm_g=seg(7)[None], k_norm_g=seg(8)[None],
        conv_w=lax.dynamic_slice_in_dim(seg(9).reshape(3, 2 * D_FF), chip * 1408, 1408, axis=1), conv_b=seg(10)[None])
    dmod_cols = lax.dynamic_slice_in_dim(dmod_all.reshape(N_DEV, 6 * d), chip * 1536, 1536, axis=1)
    g_w_ada = _ada_grad(cond_all, dmod_cols, name="ada_grad")

    core_id, chip_id = jnp.reshape(ac, (1,)).astype(jnp.int32), jnp.reshape(chip, (1,)).astype(jnp.int32)
    landed = (_direct_reduce_wait(*attn_reduce[:4], grad_x, name="reduce_attn_wait")
              + _direct_reduce_wait(*ffn_reduce[:4], grad_x, name="reduce_ffn_wait"))
    own = attn_parts + [g_wup, g_wdown.reshape(N_CHIP, D_FF // N_CHIP, d)]
    summed = [_direct_reduce_add(g, t, chip_id, core_id, name=f"reduce_add_{tag}")
              for g, t, tag in zip(own, landed, ("w_in", "w_out", "w_up", "w_down"))]
    others = _share_halves(summed, name="share_pair")
    g_big = [jnp.concatenate([jnp.where(ac == 0, mine, other), jnp.where(ac == 0, other, mine)], axis=0)
             for mine, other in zip(summed, others)]

    grads = dict(w_ada=g_w_ada, w_in=g_big[0], w_out=g_big[1], w_up=g_big[2], w_down=g_big[3], **g_small)
    names = ["w_ada", "b_ada", "norm1_g", "w_in", "gla_w_gate", "gla_b_gate", "gla_norm_g", "q_norm_g", "k_norm_g", "w_out",
             "norm2_g", "w_up", "conv_w", "conv_b", "w_down"]
    ws = dict(w_ada=w_ada, b_ada=b_ada, norm1_g=norm1_g, w_in=w_in, gla_w_gate=gla_w_gate, gla_b_gate=gla_b_gate,
              gla_norm_g=gla_norm_g, q_norm_g=q_norm_g, k_norm_g=k_norm_g, w_out=w_out, norm2_g=norm2_g, w_up=w_up,
              conv_w=conv_w, conv_b=conv_b, w_down=w_down)
    ms = dict(w_ada=m_w_ada, b_ada=m_b_ada, norm1_g=m_norm1_g, w_in=m_w_in, gla_w_gate=m_gla_w_gate, gla_b_gate=m_gla_b_gate,
              gla_norm_g=m_gla_norm_g, q_norm_g=m_q_norm_g, k_norm_g=m_k_norm_g, w_out=m_w_out, norm2_g=m_norm2_g, w_up=m_w_up,
              conv_w=m_conv_w, conv_b=m_conv_b, w_down=m_w_down)
    vs = dict(w_ada=v_w_ada, b_ada=v_b_ada, norm1_g=v_norm1_g, w_in=v_w_in, gla_w_gate=v_gla_w_gate, gla_b_gate=v_gla_b_gate,
              gla_norm_g=v_gla_norm_g, q_norm_g=v_q_norm_g, k_norm_g=v_k_norm_g, w_out=v_w_out, norm2_g=v_norm2_g, w_up=v_w_up,
              conv_w=v_conv_w, conv_b=v_conv_b, w_down=v_w_down)
    g_out, d_out, m_out, v_out = [], [], [], []
    for nm in names:
        shape = ws[nm].shape
        flip = (lambda t: t.T) if shape[-1] % 128 and shape[-2] % 128 == 0 else (lambda t: t)
        w2 = flip(ws[nm].reshape(shape[-2:]))
        g2 = flip(grads[nm].reshape(shape[-2:]))
        dl, mn, vn = _adamw(w2, g2, flip(ms[nm].reshape(shape[-2:])), flip(vs[nm].reshape(shape[-2:])), name=f"adamw_{nm}")
        for outs, t in ((g_out, g2), (d_out, dl), (m_out, mn), (v_out, vn)):
            outs.append(flip(t).reshape(shape))
    return (loss8[0, 0], grad_x[None], *g_out, *d_out, *m_out, *v_out)
```

```python
import functools

import jax
import jax.numpy as jnp
from jax import lax
from jax.experimental import pallas as pl
from jax.experimental.pallas import tpu as pltpu

F32, BF16 = jnp.float32, jnp.bfloat16
HIGHEST = lax.Precision.HIGHEST
MESH = pl.DeviceIdType.MESH

D_MODEL = 1024
GLA_CHUNK = 64
GLA_GATE_TAU = 16.0
GLA_GATE_RANK = 16
HEAD_LANES = 128
ATTN_BLOCK = 128
DILATIONS = (1, 4, 16)
ALIBI_SLOPES = tuple(2.0 ** (-(h + 1)) for h in range(8))
D_FF = 2816
EPS = 1e-6
C_GQ, C_GK, C_GV, C_GR, C_AQ, C_AK, C_AV, C_LR, PROJ_W = 0, 256, 512, 1024, 1536, 2048, 2560, 3072, 3200
ADAM_LR, ADAM_B1, ADAM_B2, ADAM_EPS, ADAM_WD, ADAM_STEP = 0.001, 0.9, 0.999, 1e-08, 0.01, 10
VMEM_LIMIT_BYTES = 56 * 1024 * 1024
ROW_TILE = 256


def _params(*sem):
    return pltpu.CompilerParams(dimension_semantics=sem or None, vmem_limit_bytes=VMEM_LIMIT_BYTES)


def _nt(a, b):
    return lax.dot_general(a, b, (((1,), (1,)), ((), ())), preferred_element_type=F32)


def _tn(a, b):
    return lax.dot_general(a, b, (((0,), (0,)), ((), ())), preferred_element_type=F32)


def _nn(a, b, precision=None):
    return jnp.dot(a, b, preferred_element_type=F32, precision=precision)


def _split3(v):
    hi = v.astype(BF16)
    rest = v - hi.astype(F32)
    mid = rest.astype(BF16)
    return hi, mid, (rest - mid.astype(F32)).astype(BF16)


def _sum_right(v, ones):
    hi, mid, lo = _split3(v)
    return (_nn(lo, ones) + _nn(mid, ones)) + _nn(hi, ones)


def _sum_left(ones, v):
    hi, mid, lo = _split3(v)
    return (_nn(ones, lo) + _nn(ones, mid)) + _nn(ones, hi)


def _fold8(v):
    return v.reshape(v.shape[0] // 8, 8, v.shape[1]).sum(axis=0)


def _spread_total(ref):
    t = ref[...]
    ref[...] = jnp.broadcast_to(jnp.sum(t, axis=-2, keepdims=True), t.shape)


def _sigmoid(x):
    return 1.0 / (1.0 + jnp.exp(-x))


def _mm(a, b, *, ta=False, tb=False, out_dtype=F32, tm, tn, tk, shard_cols=False, also_bf16=False, name):
    (k_a, m) = a.shape if ta else a.shape[::-1]
    (k_b, n) = b.shape[::-1] if tb else b.shape
    assert k_a == k_b and m % tm == 0 and n % tn == 0 and k_a % tk == 0, (name, a.shape, b.shape)
    nk = k_a // tk
    assert nk == 1 or out_dtype == F32, name
    dims = (((0 if ta else 1,), (1 if tb else 0,)), ((), ()))

    def body(a_ref, b_ref, o_ref, *rounded):
        k = pl.program_id(2)
        part = lax.dot_general(a_ref[...].astype(BF16), b_ref[...].astype(BF16), dims, preferred_element_type=F32)
        if nk == 1:
            o_ref[...] = part.astype(out_dtype)
        else:
            @pl.when(k == 0)
            def _():
                o_ref[...] = part

            @pl.when(k > 0)
            def _():
                o_ref[...] += part

        if also_bf16:
            @pl.when(k == nk - 1)
            def _():
                rounded[0][...] = o_ref[...].astype(BF16)

    a_spec = pl.BlockSpec((tk, tm), lambda i, j, k: (k, i)) if ta else pl.BlockSpec((tm, tk), lambda i, j, k: (i, k))
    b_spec = pl.BlockSpec((tn, tk), lambda i, j, k: (j, k)) if tb else pl.BlockSpec((tk, tn), lambda i, j, k: (k, j))
    if shard_cols:
        o_spec, o_shape = pl.BlockSpec((None, tm, tn), lambda i, j, k: (j, i, 0)), (n // tn, m, tn)
    else:
        o_spec, o_shape = pl.BlockSpec((tm, tn), lambda i, j, k: (i, j)), (m, n)
    shapes = [jax.ShapeDtypeStruct(o_shape, out_dtype)] + ([jax.ShapeDtypeStruct(o_shape, BF16)] if also_bf16 else [])
    out = pl.pallas_call(
        body, name=name, grid=(m // tm, n // tn, nk), in_specs=[a_spec, b_spec], out_specs=[o_spec] * len(shapes),
        out_shape=shapes, compiler_params=_params("parallel", "parallel", "arbitrary"))(a, b)
    return out if also_bf16 else out[0]


def _norm_mod_fwd(x, branch, gate, gain, scale, shift, *, name):
    s, d = x.shape
    tm = ROW_TILE
    has_branch = branch is not None

    def body(*refs):
        if has_branch:
            x_ref, br_ref, gate_ref, gain_ref, sc_ref, sh_ref, x1_ref, h_ref, ht_ref = refs
            xv = x_ref[...] + gate_ref[...] * br_ref[...]
            x1_ref[...] = xv
        else:
            x_ref, gain_ref, sc_ref, sh_ref, h_ref, ht_ref = refs
            xv = x_ref[...]
        r = lax.rsqrt(jnp.mean(xv * xv, axis=-1, keepdims=True) + EPS)
        h = (xv * r) * gain_ref[...] * (1.0 + sc_ref[...]) + sh_ref[...]
        h_ref[...] = h.astype(BF16)
        ht_ref[...] = h.T.astype(BF16)

    row = pl.BlockSpec((tm, d), lambda i: (i, 0))
    col = pl.BlockSpec((d, tm), lambda i: (0, i))
    vec = pl.BlockSpec((1, d), lambda i: (0, 0))
    h_shapes = [jax.ShapeDtypeStruct((s, d), BF16), jax.ShapeDtypeStruct((d, s), BF16)]
    if has_branch:
        return pl.pallas_call(
            body, name=name, grid=(s // tm,), in_specs=[row, row, vec, vec, vec, vec], out_specs=[row, row, col],
            out_shape=[jax.ShapeDtypeStruct((s, d), F32)] + h_shapes,
            compiler_params=_params("parallel"))(x, branch, gate, gain, scale, shift)
    h, ht = pl.pallas_call(
        body, name=name, grid=(s // tm,), in_specs=[row, vec, vec, vec], out_specs=[row, col],
        out_shape=h_shapes, compiler_params=_params("parallel"))(x, gain, scale, shift)
    return x, h, ht


def _norm_mod_bwd(x, dh, dres, gain, scale, branch, gate, *, name):
    s, d = x.shape
    tm = ROW_TILE
    has_branch = branch is not None

    def body(*refs):
        if has_branch:
            x_ref, dh_ref, dres_ref, gain_ref, sc_ref, br_ref, gate_ref, dx_ref, dbr_ref, sums_ref = refs
        else:
            x_ref, dh_ref, dres_ref, gain_ref, sc_ref, dx_ref, sums_ref = refs
        i = pl.program_id(0)

        @pl.when(i == 0)
        def _():
            sums_ref[...] = jnp.zeros_like(sums_ref)

        xv, dhv = x_ref[...], dh_ref[...]
        r = lax.rsqrt(jnp.mean(xv * xv, axis=-1, keepdims=True) + EPS)
        xn = xv * r
        dxn = dhv * (gain_ref[...] * (1.0 + sc_ref[...]))
        dx = dres_ref[...] + r * (dxn - xn * jnp.mean(dxn * xn, axis=-1, keepdims=True))
        dx_ref[...] = dx
        sums_ref[0] += _fold8(dhv * xn)
        sums_ref[1] += _fold8(dhv)
        if has_branch:
            dbr_ref[...] = (gate_ref[...] * dx).astype(BF16)
            sums_ref[2] += _fold8(dx * br_ref[...])

        @pl.when(i == s // tm - 1)
        def _():
            _spread_total(sums_ref)

    row = pl.BlockSpec((tm, d), lambda i: (i, 0))
    vec = pl.BlockSpec((1, d), lambda i: (0, 0))
    sums = pl.BlockSpec((3, 8, d), lambda i: (0, 0, 0))
    sums_shape = jax.ShapeDtypeStruct((3, 8, d), F32)
    if has_branch:
        return pl.pallas_call(
            body, name=name, grid=(s // tm,), in_specs=[row, row, row, vec, vec, row, vec], out_specs=[row, row, sums],
            out_shape=[jax.ShapeDtypeStruct((s, d), F32), jax.ShapeDtypeStruct((s, d), BF16), sums_shape],
            compiler_params=_params("arbitrary"))(x, dh, dres, gain, scale, branch, gate)
    dx, sm = pl.pallas_call(
        body, name=name, grid=(s // tm,), in_specs=[row, row, row, vec, vec], out_specs=[row, sums],
        out_shape=[jax.ShapeDtypeStruct((s, d), F32), sums_shape],
        compiler_params=_params("arbitrary"))(x, dh, dres, gain, scale)
    return dx, None, sm


GLA_ROWS = 256


def _gla_block_setup(lr_ref, wg_ref, bg_ref):
    t, c = GLA_ROWS, GLA_CHUNK
    ri = lax.broadcasted_iota(jnp.int32, (t, t), 0)
    ci = lax.broadcasted_iota(jnp.int32, (t, t), 1)
    same = (ri // c) == (ci // c)
    causal, upper = same & (ci <= ri), same & (ci >= ri)
    z = _nn(lr_ref[...].astype(BF16), wg_ref[...]) + bg_ref[...]
    g = (jnp.minimum(z, 0.0) - jnp.log(1.0 + jnp.exp(-jnp.abs(z)))) * (1.0 / GLA_GATE_TAU)
    hi, mid, lo = _split3(g)
    total = lambda ones: (_nn(ones, lo) + _nn(ones, mid)) + _nn(ones, hi)
    return z, total(causal.astype(BF16)), total(same.astype(BF16)), causal, upper


def _chunks(t):
    return [t[i * GLA_CHUNK:(i + 1) * GLA_CHUNK] for i in range(GLA_ROWS // GLA_CHUNK)]


def _gla_fwd(proj, wg, bg, gn, *, name):
    s = proj.shape[0]
    tb, c = GLA_ROWS, GLA_CHUNK
    cb = tb // c

    def body(q_ref, k_ref, v_ref, r_ref, lr_ref, wg_ref, bg_ref, gn_ref, o_ref, y_ref, st_ref, state):
        i = pl.program_id(0)

        @pl.when(i == 0)
        def _():
            state[...] = jnp.zeros_like(state)

        low = lax.broadcasted_iota(jnp.int32, (tb, HEAD_LANES), 1) < 64
        masks = (low, jnp.logical_not(low))
        _, b, b_end, causal, _ = _gla_block_setup(lr_ref, wg_ref, bg_ref)
        pairs = []
        for p in range(2):
            cols = pl.ds(p * HEAD_LANES, HEAD_LANES)
            bp, bep = (t[:, p * HEAD_LANES:(p + 1) * HEAD_LANES] for t in (b, b_end))
            k = k_ref[:, cols]
            q_in = q_ref[:, cols] * 0.125 * jnp.exp(bp)
            k_out = (k * jnp.exp(-bp)).astype(BF16)
            k_end = k * jnp.exp(bep - bp)
            qms = [jnp.where(m, q_in, 0.0).astype(BF16) for m in masks]
            kes = [jnp.where(m, k_end, 0.0).astype(BF16) for m in masks]
            vs = [v_ref[:, pl.ds((2 * p + e) * HEAD_LANES, HEAD_LANES)].astype(BF16) for e in range(2)]
            grow = [_tn(v0, k0) + _tn(v1, k1) for v0, k0, v1, k1 in zip(_chunks(vs[0]), _chunks(kes[0]), _chunks(vs[1]), _chunks(kes[1]))]
            pairs.append((bep, k_out, qms, vs, grow))
        entering = [[], []]
        for p, (bep, _, _, _, grow) in enumerate(pairs):
            st = state[p]
            for ch in range(cb):
                entering[p].append(st)
                st_ref[ch, p] = st
                st = st * jnp.exp(bep[ch * c:ch * c + 1, :]) + grow[ch]
            state[p] = st
        for p, (_, k_out, qms, vs, _) in enumerate(pairs):
            for e in range(2):
                hc = pl.ds((2 * p + e) * HEAD_LANES, HEAD_LANES)
                a = jnp.where(causal, _nt(qms[e], k_out), 0.0).astype(BF16)
                carried = jnp.concatenate([_nt(qc, sc.astype(BF16)) for qc, sc in zip(_chunks(qms[e]), entering[p])], axis=0)
                o = _nn(a, vs[e]) + carried
                o_ref[:, hc] = o
                rr = r_ref[:, hc]
                on = o * lax.rsqrt(jnp.mean(o * o, axis=-1, keepdims=True) + EPS)
                y_ref[:, hc] = (on * gn_ref[...] * (rr * _sigmoid(rr))).astype(BF16)

    def col(width, at):
        return pl.BlockSpec((tb, width), lambda i: (i, at // width))

    full = lambda shape: pl.BlockSpec(shape, lambda i: tuple(0 for _ in shape))
    return pl.pallas_call(
        body, name=name, grid=(s // tb,),
        in_specs=[col(256, C_GQ), col(256, C_GK), col(512, C_GV), col(512, C_GR), col(128, C_LR),
                  full((HEAD_LANES, 256)), full((1, 256)), full((1, HEAD_LANES))],
        out_specs=[pl.BlockSpec((tb, 512), lambda i: (i, 0)), pl.BlockSpec((tb, 512), lambda i: (i, 0)),
                   pl.BlockSpec((cb, 2, HEAD_LANES, HEAD_LANES), lambda i: (i, 0, 0, 0))],
        out_shape=[jax.ShapeDtypeStruct((s, 512), F32), jax.ShapeDtypeStruct((s, 512), BF16),
                   jax.ShapeDtypeStruct((s // c, 2, HEAD_LANES, HEAD_LANES), F32)],
        scratch_shapes=[pltpu.VMEM((2, HEAD_LANES, HEAD_LANES), F32)],
        compiler_params=_params("arbitrary"))(proj, proj, proj, proj, proj, wg, bg, gn)


def _gla_bwd(proj, wg, bg, gn, o_raw, states, dmixed, *, name):
    s = proj.shape[0]
    tb, c = GLA_ROWS, GLA_CHUNK
    cb = tb // c
    nblk, nch = s // tb, s // c

    def body(q_ref, k_ref, v_ref, r_ref, lr_ref, wg_ref, bg_ref, gn_ref, o_ref, st_ref, stn_ref, dy_ref,
             dq_ref, dk_ref, dv_ref, dr_ref, dlr_ref, gwg_ref, sums_ref, dstate):
        i = pl.program_id(0)

        @pl.when(i == 0)
        def _():
            dstate[...] = jnp.zeros_like(dstate)
            gwg_ref[...] = jnp.zeros_like(gwg_ref)
            sums_ref[...] = jnp.zeros_like(sums_ref)

        low = lax.broadcasted_iota(jnp.int32, (tb, HEAD_LANES), 1) < 64
        masks = (low, jnp.logical_not(low))
        z, b, b_end, causal, upper = _gla_block_setup(lr_ref, wg_ref, bg_ref)
        lr_b = lr_ref[...].astype(BF16)
        dlr = jnp.zeros((tb, HEAD_LANES), F32)
        per_chunk = lambda rows, mats, fn: jnp.concatenate([fn(r, m.astype(BF16)) for r, m in zip(_chunks(rows), mats)], axis=0)
        pairs = []
        for p in range(2):
            cols = pl.ds(p * HEAD_LANES, HEAD_LANES)
            sl = slice(p * HEAD_LANES, (p + 1) * HEAD_LANES)
            bp, bep = b[:, sl], b_end[:, sl]
            e_in, e_out, e_end = jnp.exp(bp), jnp.exp(-bp), jnp.exp(bep - bp)
            q = q_ref[:, cols] * 0.125
            k = k_ref[:, cols]
            q_in, k_out, k_end = q * e_in, k * e_out, k * e_end
            qms = [jnp.where(m, q_in, 0.0).astype(BF16) for m in masks]
            kms_out = [jnp.where(m, k_out, 0.0).astype(BF16) for m in masks]
            kms_end = [jnp.where(m, k_end, 0.0).astype(BF16) for m in masks]
            vs, dos = [], []
            for e in range(2):
                hc = pl.ds((2 * p + e) * HEAD_LANES, HEAD_LANES)
                o, rr, dy = o_ref[:, hc], r_ref[:, hc], dy_ref[:, hc]
                sg = _sigmoid(rr)
                rs = lax.rsqrt(jnp.mean(o * o, axis=-1, keepdims=True) + EPS)
                on = o * rs
                t = dy * (rr * sg)
                sums_ref[1, :, hc] += _fold8(t * on)
                dn = t * gn_ref[...]
                dos.append((rs * (dn - on * jnp.mean(dn * on, axis=-1, keepdims=True))).astype(BF16))
                dr_ref[:, hc] = (dy * on * gn_ref[...] * (sg * (1.0 + rr * (1.0 - sg)))).astype(BF16)
                vs.append(v_ref[:, hc].astype(BF16))
            grow = [_tn(d0, q0) + _tn(d1, q1) for d0, q0, d1, q1 in zip(_chunks(dos[0]), _chunks(qms[0]), _chunks(dos[1]), _chunks(qms[1]))]
            pairs.append((bep, e_in, e_out, e_end, q, k, qms, kms_out, kms_end, vs, dos, grow))
        chains = []
        for p in range(2):
            bep, grow = pairs[p][0], pairs[p][-1]
            entering = [st_ref[ch, p] for ch in range(cb)]
            dst, leaving_grad = dstate[p], [None] * cb
            for ch in reversed(range(cb)):
                leaving_grad[ch] = dst
                dst = dst * jnp.exp(bep[ch * c:ch * c + 1, :]) + grow[ch]
            dstate[p] = dst
            chains.append((entering, leaving_grad))
        for p in range(2):
            cols = pl.ds(p * HEAD_LANES, HEAD_LANES)
            sl = slice(p * HEAD_LANES, (p + 1) * HEAD_LANES)
            _, e_in, e_out, e_end, q, k, qms, kms_out, kms_end, vs, dos, _ = pairs[p]
            entering, leaving_grad = chains[p]
            leaving = entering[1:] + [stn_ref[0, p]]
            felt = jnp.concatenate([jnp.broadcast_to(jnp.sum(dg_st * st, axis=0, keepdims=True), (c, HEAD_LANES))
                                    for dg_st, st in zip(leaving_grad, leaving)], axis=0)
            dq_in = jnp.zeros((tb, HEAD_LANES), F32)
            dk_out = jnp.zeros((tb, HEAD_LANES), F32)
            dk_end = jnp.zeros((tb, HEAD_LANES), F32)
            for e in range(2):
                hc = pl.ds((2 * p + e) * HEAD_LANES, HEAD_LANES)
                a = jnp.where(causal, _nt(qms[e], kms_out[e]), 0.0).astype(BF16)
                da = jnp.where(causal, _nt(dos[e], vs[e]), 0.0).astype(BF16)
                dv_ref[:, hc] = (_tn(a, dos[e]) + per_chunk(kms_end[e], leaving_grad, _nt)).astype(BF16)
                dq_in = dq_in + jnp.where(masks[e], per_chunk(dos[e], entering, _nn) + _nn(da, kms_out[e]), 0.0)
                dk_out = dk_out + _tn(da, qms[e])
                dk_end = dk_end + jnp.where(masks[e], per_chunk(vs[e], leaving_grad, _nn), 0.0)
            dq = dq_in * e_in
            dk = dk_out * e_out + dk_end * e_end
            dq_ref[:, cols] = (dq * 0.125).astype(BF16)
            dk_ref[:, cols] = dk.astype(BF16)
            dg = _sum_left(upper.astype(BF16), q * dq - k * dk) + felt
            dz = dg * (1.0 / GLA_GATE_TAU) * _sigmoid(-z[:, sl])
            dz_b = dz.astype(BF16)
            sums_ref[0, :, cols] += _fold8(dz)
            dlr = dlr + _nt(dz_b, wg_ref[:, cols])
            gwg_ref[:, cols] += _tn(lr_b, dz_b)
        dlr_ref[...] = dlr.astype(BF16)

        @pl.when(i == nblk - 1)
        def _():
            _spread_total(sums_ref)

    rev = lambda i: nblk - 1 - i

    def col(width, at):
        return pl.BlockSpec((tb, width), lambda i: (rev(i), at // width))

    full = lambda shape: pl.BlockSpec(shape, lambda i: tuple(0 for _ in shape))
    out_col = lambda width: pl.BlockSpec((tb, width), lambda i: (rev(i), 0))
    return pl.pallas_call(
        body, name=name, grid=(nblk,),
        in_specs=[col(256, C_GQ), col(256, C_GK), col(512, C_GV), col(512, C_GR), col(128, C_LR),
                  full((HEAD_LANES, 256)), full((1, 256)), full((1, HEAD_LANES)),
                  pl.BlockSpec((tb, 512), lambda i: (rev(i), 0)),
                  pl.BlockSpec((cb, 2, HEAD_LANES, HEAD_LANES), lambda i: (rev(i), 0, 0, 0)),
                  pl.BlockSpec((1, 2, HEAD_LANES, HEAD_LANES), lambda i: (jnp.minimum((rev(i) + 1) * cb, nch - 1), 0, 0, 0)),
                  pl.BlockSpec((tb, 512), lambda i: (rev(i), 0))],
        out_specs=[out_col(256), out_col(256), out_col(512), out_col(512), out_col(128),
                   full((HEAD_LANES, 256)), full((2, 8, 512))],
        out_shape=[jax.ShapeDtypeStruct((s, 256), BF16), jax.ShapeDtypeStruct((s, 256), BF16),
                   jax.ShapeDtypeStruct((s, 512), BF16), jax.ShapeDtypeStruct((s, 512), BF16),
                   jax.ShapeDtypeStruct((s, 128), BF16), jax.ShapeDtypeStruct((HEAD_LANES, 256), F32),
                   jax.ShapeDtypeStruct((2, 8, 512), F32)],
        scratch_shapes=[pltpu.VMEM((2, HEAD_LANES, HEAD_LANES), F32)],
        compiler_params=_params("arbitrary"))(proj, proj, proj, proj, proj, wg, bg, gn, o_raw, states, states, dmixed)


def _head_sums(v):
    ri = lax.broadcasted_iota(jnp.int32, (HEAD_LANES, HEAD_LANES), 0) // 64
    ci = lax.broadcasted_iota(jnp.int32, (HEAD_LANES, HEAD_LANES), 1) // 64
    ones = (ri == ci).astype(BF16)
    return jnp.concatenate([_sum_right(v[:, p * HEAD_LANES:(p + 1) * HEAD_LANES], ones) for p in range(4)], axis=1)


def _attn_prep(proj, qg, kg, *, name):
    s = proj.shape[0]
    tm = ROW_TILE

    def body(q_ref, k_ref, qg_ref, kg_ref, qa_ref, ka_ref):
        q, k = q_ref[...], k_ref[...]
        qr = lax.rsqrt(_head_sums(q * q) * (1.0 / 64) + EPS)
        kr = lax.rsqrt(_head_sums(k * k) * (1.0 / 64) + EPS)
        qa_ref[...] = q * qr * qg_ref[...] * 0.125
        ka_ref[...] = k * kr * kg_ref[...]

    col = lambda at: pl.BlockSpec((tm, 512), lambda i: (i, at // 512))
    vec = pl.BlockSpec((1, 512), lambda i: (0, 0))
    out = pl.BlockSpec((tm, 512), lambda i: (i, 0))
    return pl.pallas_call(
        body, name=name, grid=(s // tm,), in_specs=[col(C_AQ), col(C_AK), vec, vec], out_specs=[out] * 2,
        out_shape=[jax.ShapeDtypeStruct((s, 512), F32)] * 2, compiler_params=_params("parallel"))(proj, proj, qg, kg)


FAR = 1e30
LOG2E, LN2 = 1.4426950408889634, 0.6931471805599453


def _attn_distance(first):
    blk = ATTN_BLOCK
    iq = lax.broadcasted_iota(jnp.int32, (2 * blk, 2 * blk), 0) & (blk - 1)
    ik = lax.broadcasted_iota(jnp.int32, (2 * blk, 2 * blk), 1)
    rel = iq + blk - ik
    valid = (rel >= 0) & (rel <= blk) & (jnp.logical_not(first) | (ik >= blk))
    return jnp.where(valid, rel.astype(F32), FAR)


def _stack_heads(t2):
    low = lax.broadcasted_iota(jnp.int32, t2.shape, 1) < 64
    return jnp.concatenate([jnp.where(low, t2, 0.0), jnp.where(low, 0.0, t2)], axis=0).astype(BF16)


def _unstack_heads(t):
    blk = ATTN_BLOCK
    low = lax.broadcasted_iota(jnp.int32, (blk, HEAD_LANES), 1) < 64
    return jnp.where(low, t[0:blk], t[blk:2 * blk])


def _attn_scores(qs, kcat, slopes, dil, dist):
    top = lax.broadcasted_iota(jnp.int32, (2 * ATTN_BLOCK, 1), 0) < ATTN_BLOCK
    return _nt(qs, kcat) - jnp.where(top, slopes[0] * (dil * LOG2E), slopes[1] * (dil * LOG2E)) * dist


def _pair_slopes(p):
    if isinstance(p, int):
        return ALIBI_SLOPES[2 * p], ALIBI_SLOPES[2 * p + 1]
    pick = lambda e: jnp.where(p == 0, ALIBI_SLOPES[e], jnp.where(p == 1, ALIBI_SLOPES[2 + e],
                               jnp.where(p == 2, ALIBI_SLOPES[4 + e], ALIBI_SLOPES[6 + e])))
    return pick(0), pick(1)


ATTN_GROUP = 4


def _each(fn, *lists):
    return [fn(*args) for args in zip(*lists)]


def _attn_group_fwd(q2s, kcats, vcats, slopes, dil, dist):
    qs = _each(lambda q2: _stack_heads(q2 * LOG2E), q2s)
    sc = _each(lambda q, k, sl: _attn_scores(q, k, sl, dil, dist), qs, kcats, slopes)
    m = _each(lambda s: jnp.max(s, axis=-1, keepdims=True), sc)
    pr = _each(lambda s, mx: jnp.exp2(s - mx), sc, m)
    den = _each(lambda p: jnp.sum(p, axis=-1, keepdims=True), pr)
    o = _each(lambda p, v, d: _nn(p.astype(BF16), v) / d, pr, vcats, den)
    lse = _each(lambda mx, d, t: jnp.broadcast_to(mx + jnp.log2(d), t.shape), m, den, o)
    return _each(lambda t, l: (_unstack_heads(t), _unstack_heads(l)), o, lse)


def _attn_group_bwd(q2s, kcats, vcats, do2s, y2s, lse2s, slopes, dil, dist):
    lane = lax.broadcasted_iota(jnp.int32, (ATTN_BLOCK, HEAD_LANES), 1)
    low = lane < 64
    per_head = lambda t, pick: jnp.concatenate([jnp.sum(jnp.where(pick(0), t, 0.0), axis=-1, keepdims=True),
                                                jnp.sum(jnp.where(pick(1), t, 0.0), axis=-1, keepdims=True)], axis=0)
    lse = _each(lambda l: per_head(l, lambda e: lane == 64 * e), lse2s)
    delta = _each(lambda d, y: per_head(d * y, lambda e: low if e == 0 else jnp.logical_not(low)), do2s, y2s)
    qs = _each(lambda q2: _stack_heads(q2 * LOG2E), q2s)
    dos = _each(_stack_heads, do2s)
    sc = _each(lambda q, k, sl: _attn_scores(q, k, sl, dil, dist), qs, kcats, slopes)
    pr = _each(lambda s, l: jnp.exp2(s - l), sc, lse)
    dp = _each(_nt, dos, vcats)
    ds = _each(lambda p, d, dl: (p * (d - dl)).astype(BF16), pr, dp, delta)
    dq = _each(lambda d, k: _unstack_heads(_nn(d, k)), ds, kcats)
    dk = _each(lambda d, q: _tn(d, q) * LN2, ds, qs)
    dv = _each(lambda p, d: _tn(p.astype(BF16), d), pr, dos)
    return list(zip(dq, dk, dv))


def _attn_specs(dil):
    rows = ATTN_BLOCK * dil
    if dil == 1:
        cur = lambda at: pl.BlockSpec((rows, 512), lambda n: (n, at // 512))
        prev = lambda at: pl.BlockSpec((rows, 512), lambda n: (jnp.maximum(n - 1, 0), at // 512))
    else:
        cur = lambda at: pl.BlockSpec((rows, HEAD_LANES), lambda n, p: (n, at // HEAD_LANES + p))
        prev = lambda at: pl.BlockSpec((rows, HEAD_LANES), lambda n, p: (jnp.maximum(n - 1, 0), at // HEAD_LANES + p))
    return cur, prev


def _attn_loop(dil, one_group, p):
    if dil == 1:
        one_group([(slice(None), pl.ds(p * HEAD_LANES, HEAD_LANES), p) for p in range(ATTN_GROUP)])
    else:
        group = min(dil, ATTN_GROUP)

        def step(g, carry):
            one_group([(pl.ds(g * group + j, ATTN_BLOCK, stride=dil), slice(None), p) for j in range(group)])
            return carry

        if dil == group:
            step(0, 0)
        else:
            lax.fori_loop(0, dil // group, step, 0)


def _dil_attn_fwd(qa, ka, proj, dil, *, name):
    s = qa.shape[0]

    def body(q_ref, kp_ref, kc_ref, vp_ref, vc_ref, o_ref, lse_ref):
        dist = _attn_distance(pl.program_id(0) == 0)
        pair = None if dil == 1 else pl.program_id(1)

        def one_group(items):
            both = lambda a, b: [jnp.concatenate([a[rows, cols], b[rows, cols]], axis=0).astype(BF16) for rows, cols, _ in items]
            outs = _attn_group_fwd([q_ref[rows, cols] for rows, cols, _ in items], both(kp_ref, kc_ref), both(vp_ref, vc_ref),
                                   [_pair_slopes(p) for _, _, p in items], dil, dist)
            for (rows, cols, _), (o2, lse2) in zip(items, outs):
                o_ref[rows, cols] = o2
                lse_ref[rows, cols] = lse2

        _attn_loop(dil, one_group, pair)

    cur, prev = _attn_specs(dil)
    grid = (s // ATTN_BLOCK,) if dil == 1 else (s // (ATTN_BLOCK * dil), 4)
    return pl.pallas_call(
        body, name=name, grid=grid, in_specs=[cur(0), prev(0), cur(0), prev(C_AV), cur(C_AV)], out_specs=[cur(0), cur(0)],
        out_shape=[jax.ShapeDtypeStruct((s, 512), F32)] * 2,
        compiler_params=_params(*["parallel"] * len(grid)))(qa, ka, ka, proj, proj)


def _dense_attn_fwd_merge(qa, ka, proj, others, y_gla, *, name):
    s = qa.shape[0]
    blk = ATTN_BLOCK

    def body(q_ref, kp_ref, kc_ref, vp_ref, vc_ref, oa_ref, la_ref, ob_ref, lb_ref, yg_ref, mixed_ref, y_ref, lse_ref):
        dist = _attn_distance(pl.program_id(0) == 0)
        mixed_ref[:, 0:512] = yg_ref[...]

        def one_group(items):
            both = lambda a, b: [jnp.concatenate([a[rows, cols], b[rows, cols]], axis=0).astype(BF16) for rows, cols, _ in items]
            outs = _attn_group_fwd([q_ref[rows, cols] for rows, cols, _ in items], both(kp_ref, kc_ref), both(vp_ref, vc_ref),
                                   [_pair_slopes(p) for _, _, p in items], 1, dist)
            for (_, cols, p), (o2, l2) in zip(items, outs):
                la, lb = la_ref[:, cols], lb_ref[:, cols]
                m = jnp.maximum(jnp.maximum(l2, la), lb)
                w0, wa, wb = jnp.exp2(l2 - m), jnp.exp2(la - m), jnp.exp2(lb - m)
                zs = w0 + wa + wb
                y = (w0 * o2 + wa * oa_ref[:, cols] + wb * ob_ref[:, cols]) / zs
                y_ref[:, cols] = y
                lse_ref[:, cols] = m + jnp.log2(zs)
                mixed_ref[:, pl.ds(512 + p * HEAD_LANES, HEAD_LANES)] = y.astype(BF16)

        _attn_loop(1, one_group, None)

    cur, prev = _attn_specs(1)
    here = pl.BlockSpec((blk, 512), lambda n: (n, 0))
    (oa, la), (ob, lb) = others
    return pl.pallas_call(
        body, name=name, grid=(s // blk,),
        in_specs=[cur(0), prev(0), cur(0), prev(C_AV), cur(C_AV)] + [here] * 5,
        out_specs=[pl.BlockSpec((blk, 1024), lambda n: (n, 0)), here, here],
        out_shape=[jax.ShapeDtypeStruct((s, 1024), BF16), jax.ShapeDtypeStruct((s, 512), F32),
                   jax.ShapeDtypeStruct((s, 512), F32)],
        compiler_params=_params("parallel"))(qa, ka, ka, proj, proj, oa, la, ob, lb, y_gla)


def _dil_attn_bwd(qa, ka, proj, y_att, lse, dmixed, dil, *, name):
    s = qa.shape[0]
    blk, rows_per_step = ATTN_BLOCK, ATTN_BLOCK * dil
    nb = s // rows_per_step
    step_axis = 0 if dil == 1 else 1

    def body(q_ref, kp_ref, kc_ref, vp_ref, vc_ref, y_ref, lse_ref, do_ref, dq_ref, dk_ref, dv_ref, dk_own, dv_own):
        n = pl.program_id(step_axis)
        pair = None if dil == 1 else pl.program_id(0)
        dist = _attn_distance(n == 0)

        @pl.when(n == 0)
        def _():
            dk_own[...] = jnp.zeros_like(dk_own)
            dv_own[...] = jnp.zeros_like(dv_own)

        def one_group(items):
            both = lambda a, b: [jnp.concatenate([a[rows, cols], b[rows, cols]], axis=0).astype(BF16) for rows, cols, _ in items]
            at = lambda ref: [ref[rows, cols] for rows, cols, _ in items]
            outs = _attn_group_bwd(at(q_ref), both(kp_ref, kc_ref), both(vp_ref, vc_ref), at(do_ref), at(y_ref), at(lse_ref),
                                   [_pair_slopes(p) for _, _, p in items], dil, dist)
            for (rows, cols, _), (dq, dk, dv) in zip(items, outs):
                dq_ref[rows, cols] = dq
                dk_ref[rows, cols] = dk_own[rows, cols] + dk[0:blk]
                dv_ref[rows, cols] = dv_own[rows, cols] + dv[0:blk]
                dk_own[rows, cols] = dk[blk:2 * blk]
                dv_own[rows, cols] = dv[blk:2 * blk]

        _attn_loop(dil, one_group, pair)

    width = 512 if dil == 1 else HEAD_LANES

    def spec(at, row_of):
        if dil == 1:
            return pl.BlockSpec((rows_per_step, width), lambda n: (row_of(n), at // width))
        return pl.BlockSpec((rows_per_step, width), lambda p, n: (row_of(n), at // width + p))

    cur = lambda at: spec(at, lambda n: n)
    prev = lambda at: spec(at, lambda n: jnp.maximum(n - 1, 0))
    own = spec(0, lambda n: 0)
    grid = (nb,) if dil == 1 else (4, nb)
    sems = ("arbitrary",) if dil == 1 else ("parallel", "arbitrary")
    dq, dk, dv, dk_last, dv_last = pl.pallas_call(
        body, name=name, grid=grid,
        in_specs=[cur(0), prev(0), cur(0), prev(C_AV), cur(C_AV), cur(0), cur(0), cur(512)],
        out_specs=[cur(0), prev(0), prev(0), own, own],
        out_shape=[jax.ShapeDtypeStruct((s, 512), F32)] * 3 + [jax.ShapeDtypeStruct((rows_per_step, 512), F32)] * 2,
        compiler_params=_params(*sems),
    )(qa, ka, ka, proj, proj, y_att, lse, dmixed)
    return dq, dk.at[s - rows_per_step:].set(dk_last), dv.at[s - rows_per_step:].set(dv_last)


def _attn_post(parts, proj, qg, kg, *, name):
    s = proj.shape[0]
    tm = ROW_TILE
    nblk = s // tm

    def body(*refs):
        ins, (q_ref, k_ref, qg_ref, kg_ref, dq_out, dk_out, dv_out, sums_ref) = refs[:9], refs[9:]
        i = pl.program_id(0)

        @pl.when(i == 0)
        def _():
            sums_ref[...] = jnp.zeros_like(sums_ref)

        dq = (ins[0][...] + ins[3][...]) + ins[6][...]
        dk = (ins[1][...] + ins[4][...]) + ins[7][...]
        dv = (ins[2][...] + ins[5][...]) + ins[8][...]
        dv_out[...] = dv.astype(BF16)
        for row, (x_ref, g_ref, dy, out, post) in enumerate(((q_ref, qg_ref, dq, dq_out, 0.125), (k_ref, kg_ref, dk, dk_out, 1.0))):
            x = x_ref[...]
            rs = lax.rsqrt(_head_sums(x * x) * (1.0 / 64) + EPS)
            xn = x * rs
            dy = dy * post
            sums_ref[row] += _fold8(dy * xn)
            dn = dy * g_ref[...]
            out[...] = (rs * (dn - xn * (_head_sums(dn * xn) * (1.0 / 64)))).astype(BF16)

        @pl.when(i == nblk - 1)
        def _():
            _spread_total(sums_ref)

    here = pl.BlockSpec((tm, 512), lambda i: (i, 0))
    col = lambda at: pl.BlockSpec((tm, 512), lambda i: (i, at // 512))
    vec = pl.BlockSpec((1, 512), lambda i: (0, 0))
    return pl.pallas_call(
        body, name=name, grid=(nblk,), in_specs=[here] * 9 + [col(C_AQ), col(C_AK), vec, vec],
        out_specs=[here, here, here, pl.BlockSpec((2, 8, 512), lambda i: (0, 0, 0))],
        out_shape=[jax.ShapeDtypeStruct((s, 512), BF16)] * 3 + [jax.ShapeDtypeStruct((2, 8, 512), F32)],
        compiler_params=_params("arbitrary"))(*[t for part in parts for t in part], proj, proj, qg, kg)


FFN_TM, FFN_TN = 256, 1408
HALO = 16


def _conv3(u_ref, halo_ref, w_ref, b_ref, first):
    u = u_ref[...].astype(F32)
    ext = jnp.concatenate([jnp.where(first, 0.0, halo_ref[...].astype(F32)), u], axis=0)
    u1 = pltpu.roll(ext, 1, 0)[HALO:]
    u2 = pltpu.roll(ext, 2, 0)[HALO:]
    return b_ref[...] + w_ref[0:1, :] * u2 + w_ref[1:2, :] * u1 + w_ref[2:3, :] * u


def _ffn_specs(tm, tn):
    nj = D_FF // tn
    blk = lambda half: pl.BlockSpec((tm, tn), lambda j, i: (i, j + half * nj))
    halo = lambda half: pl.BlockSpec((HALO, tn), lambda j, i: (jnp.maximum(i * (tm // HALO) - 1, 0), j + half * nj))
    wspec = lambda half: pl.BlockSpec((3, tn), lambda j, i: (0, j + half * nj))
    bspec = lambda half: pl.BlockSpec((1, tn), lambda j, i: (0, j + half * nj))
    return [blk(0), halo(0), blk(1), halo(1), wspec(0), wspec(1), bspec(0), bspec(1)]


def _conv_swiglu_fwd(u, conv_w, conv_b, *, name):
    s = u.shape[0]
    tm, tn = FFN_TM, FFN_TN

    def body(ug_ref, hg_ref, uv_ref, hv_ref, wg_ref, wv_ref, bg_ref, bv_ref, act_ref, uc_ref):
        first = pl.program_id(1) == 0
        cg = _conv3(ug_ref, hg_ref, wg_ref, bg_ref, first)
        cv = _conv3(uv_ref, hv_ref, wv_ref, bv_ref, first)
        act_ref[...] = (cg * _sigmoid(cg) * cv).astype(BF16)
        uc_ref[0] = cg.astype(BF16)
        uc_ref[1] = cv.astype(BF16)

    return pl.pallas_call(
        body, name=name, grid=(D_FF // tn, s // tm), in_specs=_ffn_specs(tm, tn),
        out_specs=[pl.BlockSpec((tm, tn), lambda j, i: (i, j)), pl.BlockSpec((2, tm, tn), lambda j, i: (0, i, j))],
        out_shape=[jax.ShapeDtypeStruct((s, D_FF), BF16), jax.ShapeDtypeStruct((2, s, D_FF), BF16)],
        compiler_params=_params("parallel", "parallel"))(u, u, u, u, conv_w, conv_w, conv_b, conv_b)


def _swiglu_bwd(uc, dact, *, name):
    _, s, _ = uc.shape
    tm, tn = FFN_TM, FFN_TN

    def body(uc_ref, da_ref, duc_ref, sums_ref):
        i = pl.program_id(1)

        @pl.when(i == 0)
        def _():
            sums_ref[...] = jnp.zeros_like(sums_ref)

        cg, cv, da = uc_ref[0].astype(F32), uc_ref[1].astype(F32), da_ref[...].astype(F32)
        sg = _sigmoid(cg)
        dg = da * cv * (sg * (1.0 + cg * (1.0 - sg)))
        dv = da * (cg * sg)
        duc_ref[0] = dg.astype(BF16)
        duc_ref[1] = dv.astype(BF16)
        sums_ref[0] += _fold8(dg)
        sums_ref[1] += _fold8(dv)

        @pl.when(i == s // tm - 1)
        def _():
            _spread_total(sums_ref)

    pair = pl.BlockSpec((2, tm, tn), lambda j, i: (0, i, j))
    return pl.pallas_call(
        body, name=name, grid=(D_FF // tn, s // tm), in_specs=[pair, pl.BlockSpec((tm, tn), lambda j, i: (i, j))],
        out_specs=[pair, pl.BlockSpec((2, 8, tn), lambda j, i: (0, 0, j))],
        out_shape=[jax.ShapeDtypeStruct((2, s, D_FF), BF16), jax.ShapeDtypeStruct((2, 8, D_FF), F32)],
        compiler_params=_params("parallel", "arbitrary"))(uc, dact)


def _conv_bwd(duc, u, conv_w, *, name):
    _, s, _ = duc.shape
    tm, tn = FFN_TM, FFN_TN
    nj, ni = D_FF // tn, s // tm

    def body(d_ref, halo_ref, u_ref, w_ref, du_ref, sums_ref):
        i = pl.program_id(2)

        @pl.when(i == 0)
        def _():
            sums_ref[...] = jnp.zeros_like(sums_ref)

        d = d_ref[0].astype(F32)
        ext = jnp.concatenate([d, jnp.where(i == ni - 1, 0.0, halo_ref[0].astype(F32))], axis=0)
        n = tm + HALO
        d1 = pltpu.roll(ext, n - 1, 0)[:tm]
        d2 = pltpu.roll(ext, n - 2, 0)[:tm]
        du_ref[...] = (w_ref[2:3, :] * d + w_ref[1:2, :] * d1 + w_ref[0:1, :] * d2).astype(BF16)
        uv = u_ref[...].astype(F32)
        for t, shifted in enumerate((d2, d1, d)):
            sums_ref[0, t] += _fold8(shifted * uv)

        @pl.when(i == ni - 1)
        def _():
            _spread_total(sums_ref)

    return pl.pallas_call(
        body, name=name, grid=(2, nj, ni),
        in_specs=[pl.BlockSpec((1, tm, tn), lambda g, j, i: (g, i, j)),
                  pl.BlockSpec((1, HALO, tn), lambda g, j, i: (g, jnp.minimum((i + 1) * (tm // HALO), s // HALO - 1), j)),
                  pl.BlockSpec((tm, tn), lambda g, j, i: (i, g * nj + j)),
                  pl.BlockSpec((3, tn), lambda g, j, i: (0, g * nj + j))],
        out_specs=[pl.BlockSpec((tm, tn), lambda g, j, i: (i, g * nj + j)),
                   pl.BlockSpec((1, 3, 8, tn), lambda g, j, i: (g, 0, 0, j))],
        out_shape=[jax.ShapeDtypeStruct((s, 2 * D_FF), BF16), jax.ShapeDtypeStruct((2, 3, 8, D_FF), F32)],
        compiler_params=_params("parallel", "parallel", "arbitrary"))(duc, duc, u, conv_w)


def _loss_head(x1, ffn, gate, target, *, name):
    s, d = x1.shape
    tm = ROW_TILE

    def body(x_ref, f_ref, g_ref, t_ref, dy_ref, df_ref, sums_ref):
        i = pl.program_id(0)

        @pl.when(i == 0)
        def _():
            sums_ref[...] = jnp.zeros_like(sums_ref)

        f = f_ref[...]
        err = x_ref[...] + g_ref[...] * f - t_ref[...]
        dy = err * (1.0 / d)
        dy_ref[...] = dy
        df_ref[...] = (g_ref[...] * dy).astype(BF16)
        sums_ref[0] += _fold8(dy * f)
        sums_ref[1] += _fold8(err * err)

        @pl.when(i == s // tm - 1)
        def _():
            _spread_total(sums_ref)

    row = pl.BlockSpec((tm, d), lambda i: (i, 0))
    return pl.pallas_call(
        body, name=name, grid=(s // tm,), in_specs=[row, row, pl.BlockSpec((1, d), lambda i: (0, 0)), row],
        out_specs=[row, row, pl.BlockSpec((2, 8, d), lambda i: (0, 0, 0))],
        out_shape=[jax.ShapeDtypeStruct((s, d), F32), jax.ShapeDtypeStruct((s, d), BF16), jax.ShapeDtypeStruct((2, 8, d), F32)],
        compiler_params=_params("arbitrary"))(x1, ffn, gate, target)


def _adamw(w, g, m, v, *, name):
    rows, cols = w.shape
    if rows % 8 == 0 or rows <= ROW_TILE:
        tm = next((t for t in range(ROW_TILE, 7, -8) if rows % t == 0), rows)
        blk, grid = pl.BlockSpec((tm, cols), lambda i: (i, 0)), (rows // tm,)
    else:
        blk, grid = pl.BlockSpec((rows, ROW_TILE), lambda i: (0, i)), (cols // ROW_TILE,)

    def body(w_ref, g_ref, m_ref, v_ref, d_ref, mo_ref, vo_ref):
        gv = g_ref[...]
        mn = ADAM_B1 * m_ref[...] + (1.0 - ADAM_B1) * gv
        vn = ADAM_B2 * v_ref[...] + (1.0 - ADAM_B2) * (gv * gv)
        m_hat = mn / (1.0 - ADAM_B1 ** ADAM_STEP)
        v_hat = vn / (1.0 - ADAM_B2 ** ADAM_STEP)
        d_ref[...] = -ADAM_LR * (m_hat / (jnp.sqrt(v_hat) + ADAM_EPS) + ADAM_WD * w_ref[...])
        mo_ref[...] = mn
        vo_ref[...] = vn

    return pl.pallas_call(
        body, name=name, grid=grid, in_specs=[blk] * 4, out_specs=[blk] * 3,
        out_shape=[jax.ShapeDtypeStruct((rows, cols), F32)] * 3, compiler_params=_params("parallel"))(w, g, m, v)


def _colsum(t):
    return t[..., 0, :]


def _in_proj_layout(w_in):
    pad = jnp.zeros((w_in.shape[0], PROJ_W - C_LR - GLA_GATE_RANK), w_in.dtype)
    return jnp.concatenate([w_in[:, :1536], w_in[:, 1552:], w_in[:, 1536:1552], pad], axis=1)


def _in_proj_grad_layout(g):
    return jnp.concatenate([g[:, :1536], g[:, C_LR:C_LR + GLA_GATE_RANK], g[:, 1536:C_LR]], axis=1)


def _gate_layout(gla_w_gate):
    return jnp.pad(gla_w_gate, ((0, HEAD_LANES - GLA_GATE_RANK), (0, 0))).astype(BF16)


def _local_step(x, target, mod, wi, wo, ffn_weights, ffn_grads_ready, attn_grads_ready, conv_w, conv_b, wg, bg, gn, qg, kg, n1g, n2g):
    d = D_MODEL
    sh1, sc1, g1, sh2, sc2, g2 = [mod[:, i * d:(i + 1) * d] for i in range(6)]
    qg8, kg8 = jnp.tile(qg, (1, 8)), jnp.tile(kg, (1, 8))

    _, h1, h1_t = _norm_mod_fwd(x, None, None, n1g, sc1, sh1, name="norm1_fwd")
    proj = _mm(h1, wi, tm=1024, tn=PROJ_W, tk=d, name="in_proj")
    o_raw, y_gla, states = _gla_fwd(proj, wg, bg, gn, name="gla_fwd")
    qa, ka = _attn_prep(proj, qg8, kg8, name="attn_prep")
    sparse = [_dil_attn_fwd(qa, ka, proj, dil, name=f"attn_fwd_d{dil}") for dil in DILATIONS[1:]]
    mixed, y_att, lse = _dense_attn_fwd_merge(qa, ka, proj, sparse, y_gla, name="attn_fwd_d1_merge")
    attn_out = _mm(mixed, wo, tm=1024, tn=d, tk=d, name="out_proj")
    x1, h2, h2_t = _norm_mod_fwd(x, attn_out, g1, n2g, sc2, sh2, name="norm2_fwd")
    wup, wdown = ffn_weights(h2)
    u = _mm(h2, wup, out_dtype=BF16, tm=1024, tn=D_FF, tk=d, name="up_proj")
    act, uc = _conv_swiglu_fwd(u, conv_w, conv_b, name="conv_swiglu_fwd")
    ffn = _mm(act, wdown, tm=1024, tn=d, tk=D_FF, name="down_proj")
    dy, dffn, head_sums = _loss_head(x1, ffn, g2, target, name="loss_head")

    dact = _mm(dffn, wdown, tb=True, out_dtype=BF16, tm=1024, tn=D_FF, tk=d, name="down_proj_dx")
    g_wdown, g_wdown_b = _mm(act, dffn, ta=True, tm=1408, tn=d, tk=2048, also_bf16=True, name="down_proj_dw")
    duc, bias_sums = _swiglu_bwd(uc, dact, name="swiglu_bwd")
    du, tap_sums = _conv_bwd(duc, u, conv_w, name="conv_bwd")
    dh2 = _mm(du, wup, tb=True, tm=1024, tn=d, tk=D_FF, name="up_proj_dx")
    g_wup, g_wup_b = _mm(h2_t, du, tm=d, tn=1408, tk=2048, shard_cols=True, also_bf16=True, name="up_proj_dw")
    token = ffn_grads_ready(g_wup_b, g_wdown_b)
    g1_late = g1 if token is None else g1 + token[0:1, 0:1]
    dx1, dao, n2_sums = _norm_mod_bwd(x1, dh2, dy, n2g, sc2, attn_out, g1_late, name="norm2_bwd")

    dmixed = _mm(dao, wo, tb=True, tm=1024, tn=d, tk=d, name="out_proj_dx")
    g_wo = _mm(mixed, dao, ta=True, tm=d, tn=d, tk=1024, name="out_proj_dw")
    dgq, dgk, dgv, dgr, dlr, g_wg, gla_sums = _gla_bwd(proj, wg, bg, gn, o_raw, states, dmixed, name="gla_bwd")
    parts = [_dil_attn_bwd(qa, ka, proj, y_att, lse, dmixed, dil, name=f"attn_bwd_d{dil}") for dil in DILATIONS]
    daq, dak, dav, qk_sums = _attn_post(parts, proj, qg8, kg8, name="attn_post")
    dproj = jnp.concatenate([dgq, dgk, dgv, dgr, daq, dak, dav, dlr], axis=1)
    g_wi = _mm(h1_t, dproj, tm=512, tn=PROJ_W, tk=2048, name="in_proj_dw")
    token = attn_grads_ready(g_wi, g_wo)
    wi_late = wi if token is None else wi + token[0:1, 0:1].astype(BF16)
    dh1 = _mm(dproj, wi_late, tb=True, tm=1024, tn=d, tk=PROJ_W, name="in_proj_dx")
    grad_x, _, n1_sums = _norm_mod_bwd(x, dh1, dx1, n1g, sc1, None, None, name="norm1_bwd")

    n1, n2, hs, taps, cb = _colsum(n1_sums), _colsum(n2_sums), _colsum(head_sums), _colsum(tap_sums), _colsum(bias_sums)
    gs, qs = _colsum(gla_sums), _colsum(qk_sums)
    dmod = jnp.concatenate([n1[1], n1[0] * n1g[0], n2[2], n2[1], n2[0] * n2g[0], hs[0]])
    small = dict(
        dmod=dmod,
        norm1_g=n1[0] * (1.0 + sc1[0]), norm2_g=n2[0] * (1.0 + sc2[0]),
        gla_w_gate=g_wg[:GLA_GATE_RANK], gla_b_gate=gs[0, :256], gla_norm_g=gs[1].reshape(4, 128).sum(axis=0),
        q_norm_g=qs[0].reshape(8, 64).sum(axis=0), k_norm_g=qs[1].reshape(8, 64).sum(axis=0),
        conv_w=jnp.concatenate([taps[0], taps[1]], axis=1), conv_b=jnp.concatenate([cb[0], cb[1]]),
    )
    return head_sums[1], grad_x, (g_wi, g_wo, g_wup, g_wdown), small


N_DEV, N_CHIP = 8, 4
ANY = pl.BlockSpec(memory_space=pl.ANY)
VMEM_SPEC = pl.BlockSpec(memory_space=pltpu.VMEM)


def _place():
    x, y, c = lax.axis_index("x"), lax.axis_index("y"), lax.axis_index("c")
    other_chips = [(1 - x, y), (x, 1 - y), (1 - x, 1 - y)]
    return x, y, c, (x, y, 1 - c), other_chips


def _all_gather_small(v, *, name):
    m, n = v.shape

    def body(v_ref, out_ref, send_sems, recv_sems, local_sem):
        x, y, c, sibling, chips = _place()
        me = (x, y, c)

        def rows(px, py, pc):
            return out_ref.at[pl.ds((4 * px + 2 * py + pc) * m, m), :]

        def copy(k, block, to, src=None):
            return pltpu.make_async_remote_copy(
                src_ref=rows(*block) if src is None else src, dst_ref=rows(*block), send_sem=send_sems.at[k],
                recv_sem=recv_sems.at[k], device_id=to, device_id_type=MESH)

        mine = pltpu.make_async_copy(v_ref, rows(*me), local_sem)
        mine.start()
        first = [copy(0, me, sibling, src=v_ref)]
        first += [copy(1 + j, me, (*chip, c), src=v_ref) for j, chip in enumerate(chips)]
        for cp in first:
            cp.start()
        passed = [copy(4 + j, (*chip, c), sibling) for j, chip in enumerate(chips)]
        for j, chip in enumerate(chips):
            copy(1 + j, (*chip, c), me).wait_recv()
            passed[j].start()
        copy(0, sibling, me).wait_recv()
        for j, chip in enumerate(chips):
            copy(4 + j, (*chip, 1 - c), me).wait_recv()
        for cp in first + passed:
            cp.wait_send()
        mine.wait()

    return pl.pallas_call(
        body, name=name, out_shape=jax.ShapeDtypeStruct((N_DEV * m, n), v.dtype), in_specs=[VMEM_SPEC], out_specs=VMEM_SPEC,
        scratch_shapes=[pltpu.SemaphoreType.DMA((7,)), pltpu.SemaphoreType.DMA((7,)), pltpu.SemaphoreType.DMA],
    )(v)


def _gather_weight_shards(shards, *, name):
    nw = len(shards)

    def body(*refs):
        srcs, outs, (send_sems, recv_sems) = refs[:nw], refs[nw:2 * nw], refs[2 * nw:]
        x, y, c, sibling, chips = _place()
        index = lambda chip: 2 * chip[0] + chip[1]

        def copy(w, k, src, dst, to):
            return pltpu.make_async_remote_copy(src_ref=src, dst_ref=dst, send_sem=send_sems.at[6 * w + k],
                                                recv_sem=recv_sems.at[6 * w + k], device_id=to, device_id_type=MESH)

        sent = []
        for w, (src_ref, out_ref) in enumerate(zip(srcs, outs)):
            for k, chip in enumerate(chips):
                sent.append(copy(w, k, src_ref.at[c], out_ref.at[2 * x + y, c], (*chip, c)))
                sent[-1].start()
        for w, out_ref in enumerate(outs):
            for k, chip in enumerate(chips):
                landed = out_ref.at[index(chip), c]
                copy(w, k, landed, landed, (*chip, c)).wait_recv()
                sent.append(copy(w, 3 + k, landed, landed, sibling))
                sent[-1].start()
        for w, out_ref in enumerate(outs):
            for k, chip in enumerate(chips):
                passed_on = out_ref.at[index(chip), 1 - c]
                copy(w, 3 + k, passed_on, passed_on, sibling).wait_recv()
        for cp in sent:
            cp.wait_send()

    return pl.pallas_call(
        body, name=name, out_shape=[jax.ShapeDtypeStruct((N_CHIP, *s.shape), s.dtype) for s in shards],
        in_specs=[ANY] * nw, out_specs=[ANY] * nw,
        scratch_shapes=[pltpu.SemaphoreType.DMA((6 * nw,)), pltpu.SemaphoreType.DMA((6 * nw,))],
    )(*shards)


HBM_SPEC = pl.BlockSpec(memory_space=pltpu.HBM)
SEM_SPEC = pl.BlockSpec(memory_space=pltpu.SEMAPHORE)
DATAFLOW_EFFECT = pltpu.SideEffectType.DATAFLOW_SIDE_EFFECTING


def _late_copies(srcs, lands, send_sems, recv_sems):
    x, y, c, _, chips = _place()
    return [pltpu.make_async_remote_copy(
        src_ref=src.at[c], dst_ref=land.at[2 * x + y, c], send_sem=send_sems.at[6 * w + 2 * r + core],
        recv_sem=recv_sems.at[6 * w + 2 * r + c], device_id=(*chip, core), device_id_type=MESH)
        for w, (src, land) in enumerate(zip(srcs, lands)) for r, chip in enumerate(chips) for core in range(2)]


def _gather_late_start(own, after, *, name):
    nw = len(own)

    def body(*refs):
        srcs, lands, send_sems, recv_sems, token = refs[:nw], refs[nw:2 * nw], refs[2 * nw + 1], refs[2 * nw + 2], refs[-1]
        for cp in _late_copies(srcs, lands, send_sems, recv_sems):
            cp.start()
        token[...] = jnp.zeros_like(token)

    lands = [pltpu.with_memory_space_constraint(lax.empty((N_CHIP, *s.shape), s.dtype), pltpu.HBM) for s in own]
    own = [pltpu.with_memory_space_constraint(s, pltpu.HBM) for s in own]
    out = pl.pallas_call(
        body, name=name,
        out_shape=(pltpu.SemaphoreType.DMA((6 * nw,)), pltpu.SemaphoreType.DMA((6 * nw,)),
                   *[pltpu.HBM(s.shape, s.dtype) for s in own], *[pltpu.HBM(s.shape, s.dtype) for s in lands],
                   jax.ShapeDtypeStruct((8, 128), F32)),
        in_specs=[HBM_SPEC] * (2 * nw) + [ANY], out_specs=(SEM_SPEC, SEM_SPEC, *[HBM_SPEC] * (2 * nw), VMEM_SPEC),
        input_output_aliases={i: 2 + i for i in range(2 * nw)},
        compiler_params=pltpu.CompilerParams(has_side_effects=DATAFLOW_EFFECT))(*own, *lands, after)
    return out[0], out[1], out[2:2 + nw], out[2 + nw:2 + 2 * nw], out[-1]


def _gather_late_wait(send_sems, recv_sems, own, lands, after, *, name):
    nw = len(own)

    def body(*refs):
        srcs, lands_in, send_sems, recv_sems = refs[:nw], refs[nw:2 * nw], refs[2 * nw], refs[2 * nw + 1]
        x, y, c, _, chips = _place()
        for cp in _late_copies(srcs, lands_in, send_sems, recv_sems):
            cp.wait_send()
        for w, (src, land) in enumerate(zip(srcs, lands_in)):
            for r, chip in enumerate(chips):
                for core in range(2):
                    pltpu.make_async_remote_copy(
                        src_ref=src.at[c], dst_ref=land.at[2 * chip[0] + chip[1], core], send_sem=send_sems.at[6 * w + 2 * r + core],
                        recv_sem=recv_sems.at[6 * w + 2 * r + core], device_id=(*chip, core), device_id_type=MESH).wait_recv()

    out = pl.pallas_call(
        body, name=name, out_shape=(*[pltpu.HBM(s.shape, s.dtype) for s in own], *[pltpu.HBM(s.shape, s.dtype) for s in lands]),
        in_specs=[HBM_SPEC] * (2 * nw) + [SEM_SPEC, SEM_SPEC, ANY], out_specs=tuple([HBM_SPEC] * (2 * nw)),
        input_output_aliases={i: i for i in range(2 * nw)},
        compiler_params=pltpu.CompilerParams(has_side_effects=DATAFLOW_EFFECT))(*own, *lands, send_sems, recv_sems, after)
    return out[:nw], out[nw:]


def _direct_reduce_copies(srcs, lands, send_sems, recv_sems):
    x, y, c, _, _ = _place()
    cps = []
    for w, (src, land) in enumerate(zip(srcs, lands)):
        for rel in range(1, N_DEV):
            tx, ty, tc = (1 - x if rel & 4 else x), (1 - y if rel & 2 else y), (1 - c if rel & 1 else c)
            cps.append(pltpu.make_async_remote_copy(
                src_ref=src.at[2 * tx + ty, tc], dst_ref=land.at[rel - 1], send_sem=send_sems.at[7 * w + rel - 1],
                recv_sem=recv_sems.at[7 * w + rel - 1], device_id=(tx, ty, tc), device_id_type=MESH))
    return cps


def _direct_reduce_start(grads, *, name):
    nw = len(grads)

    def body(*refs):
        srcs, lands, send_sems, recv_sems, token = refs[:nw], refs[nw:2 * nw], refs[2 * nw], refs[2 * nw + 1], refs[-1]
        for cp in _direct_reduce_copies(srcs, lands, send_sems, recv_sems):
            cp.start()
        token[...] = jnp.zeros_like(token)

    lands = [pltpu.with_memory_space_constraint(lax.empty((N_DEV - 1, *g.shape[2:]), g.dtype), pltpu.HBM) for g in grads]
    grads = [pltpu.with_memory_space_constraint(g, pltpu.HBM) for g in grads]
    out = pl.pallas_call(
        body, name=name,
        out_shape=(pltpu.SemaphoreType.DMA((7 * nw,)), pltpu.SemaphoreType.DMA((7 * nw,)),
                   *[pltpu.HBM(g.shape, g.dtype) for g in grads], *[pltpu.HBM(t.shape, t.dtype) for t in lands],
                   jax.ShapeDtypeStruct((8, 128), F32)),
        in_specs=[HBM_SPEC] * (2 * nw), out_specs=(SEM_SPEC, SEM_SPEC, *[HBM_SPEC] * (2 * nw), VMEM_SPEC),
        input_output_aliases={i: 2 + i for i in range(2 * nw)},
        compiler_params=pltpu.CompilerParams(has_side_effects=DATAFLOW_EFFECT))(*grads, *lands)
    return out[0], out[1], out[2:2 + nw], out[2 + nw:2 + 2 * nw], out[-1]


def _direct_reduce_wait(send_sems, recv_sems, grads, lands, after, *, name):
    nw = len(grads)

    def body(*refs):
        srcs, lands_in, send_sems, recv_sems = refs[:nw], refs[nw:2 * nw], refs[2 * nw], refs[2 * nw + 1]
        cps = _direct_reduce_copies(srcs, lands_in, send_sems, recv_sems)
        for cp in cps:
            cp.wait_send()
        for cp in cps:
            cp.wait_recv()

    out = pl.pallas_call(
        body, name=name, out_shape=(*[pltpu.HBM(g.shape, g.dtype) for g in grads], *[pltpu.HBM(t.shape, t.dtype) for t in lands]),
        in_specs=[HBM_SPEC] * (2 * nw) + [SEM_SPEC, SEM_SPEC, ANY], out_specs=tuple([HBM_SPEC] * (2 * nw)),
        input_output_aliases={i: i for i in range(2 * nw)},
        compiler_params=pltpu.CompilerParams(has_side_effects=DATAFLOW_EFFECT))(*grads, *lands, send_sems, recv_sems, after)
    return out[nw:]


def _direct_reduce_add(grad, landed, chip, core, *, name):
    _, r, n = grad.shape
    half = r // 2
    tr = _row_tile(half)
    nb = half // tr

    def body(chip_ref, core_ref, g_ref, t_ref, o_ref):
        acc = g_ref[0]
        for k in range(N_DEV - 1):
            acc = acc + t_ref[k].astype(F32)
        o_ref[...] = acc

    return pl.pallas_call(
        body, name=name,
        grid_spec=pltpu.PrefetchScalarGridSpec(
            num_scalar_prefetch=2, grid=(nb,),
            in_specs=[pl.BlockSpec((1, tr, n), lambda i, chip_ref, core_ref: (chip_ref[0], core_ref[0] * nb + i, 0)),
                      pl.BlockSpec((N_DEV - 1, tr, n), lambda i, chip_ref, core_ref: (0, i, 0))],
            out_specs=pl.BlockSpec((tr, n), lambda i, chip_ref, core_ref: (i, 0))),
        out_shape=jax.ShapeDtypeStruct((half, n), F32), compiler_params=_params("parallel"))(chip, core, grad, landed)


def _share_halves(halves, *, name):
    nw = len(halves)

    def body(*refs):
        srcs, outs, (send_sems, recv_sems) = refs[:nw], refs[nw:2 * nw], refs[2 * nw:]
        _, _, _, sibling, _ = _place()
        cps = [pltpu.make_async_remote_copy(src_ref=src_ref, dst_ref=out_ref, send_sem=send_sems.at[w], recv_sem=recv_sems.at[w],
                                            device_id=sibling, device_id_type=MESH)
               for w, (src_ref, out_ref) in enumerate(zip(srcs, outs))]
        for cp in cps:
            cp.start()
        for cp in cps:
            cp.wait()

    return pl.pallas_call(
        body, name=name, out_shape=[jax.ShapeDtypeStruct(h.shape, h.dtype) for h in halves],
        in_specs=[ANY] * nw, out_specs=[ANY] * nw,
        scratch_shapes=[pltpu.SemaphoreType.DMA((nw,)), pltpu.SemaphoreType.DMA((nw,))])(*halves)


def _row_tile(rows, limit=256):
    return next(t for t in range(limit, 15, -16) if rows % t == 0)


def _sum_devices(gathered, *, name):
    _, m, n = gathered.shape

    def body(g_ref, tot_ref, loss_ref):
        tot = g_ref[0]
        for dev in range(1, N_DEV):
            tot = tot + g_ref[dev]
        tot_ref[...] = tot
        loss_ref[...] = jnp.full((8, n), (0.5 / D_MODEL) * jnp.sum(tot[0:8]), F32)

    return pl.pallas_call(body, name=name, in_specs=[VMEM_SPEC], out_specs=[VMEM_SPEC, VMEM_SPEC],
                          out_shape=[jax.ShapeDtypeStruct((m, n), F32), jax.ShapeDtypeStruct((8, n), F32)])(gathered)


def _ada_mod(cond_all, w_ada_shard, *, name):
    tn = 512

    def body(a_ref, b_ref, o_ref):
        o_ref[...] = _nn(a_ref[...], b_ref[...], precision=HIGHEST)

    return pl.pallas_call(
        body, name=name, grid=(w_ada_shard.shape[1] // tn,),
        in_specs=[pl.BlockSpec(cond_all.shape, lambda j: (0, 0)), pl.BlockSpec((D_MODEL, tn), lambda j: (0, j))],
        out_specs=pl.BlockSpec((N_DEV, tn), lambda j: (0, j)),
        out_shape=jax.ShapeDtypeStruct((N_DEV, w_ada_shard.shape[1]), F32), compiler_params=_params("parallel"))(cond_all, w_ada_shard)


def _ada_grad(cond_all, dmod_cols, *, name):
    tm = 256

    def body(a_ref, b_ref, o_ref):
        o_ref[...] = lax.dot_general(a_ref[...], b_ref[...], (((0,), (0,)), ((), ())), precision=HIGHEST,
                                     preferred_element_type=F32)

    return pl.pallas_call(
        body, name=name, grid=(D_MODEL // tm,),
        in_specs=[pl.BlockSpec((N_DEV, tm), lambda i: (0, i)), pl.BlockSpec(dmod_cols.shape, lambda i: (0, 0))],
        out_specs=pl.BlockSpec((tm, dmod_cols.shape[1]), lambda i: (i, 0)),
        out_shape=jax.ShapeDtypeStruct((D_MODEL, dmod_cols.shape[1]), F32), compiler_params=_params("parallel"))(cond_all, dmod_cols)


def _silu_rows(c8, *, name):
    def body(c_ref, o_ref):
        cv = c_ref[...]
        o_ref[...] = cv * _sigmoid(cv)

    return pl.pallas_call(body, name=name, in_specs=[VMEM_SPEC], out_specs=VMEM_SPEC,
                          out_shape=jax.ShapeDtypeStruct(c8.shape, F32))(c8)


def _rows128(t, rows=None):
    flat = t.reshape(-1, 128)
    return flat if rows is None else jnp.pad(flat, ((0, rows - flat.shape[0]), (0, 0)))


def _from_col_shards(shards, r, n):
    return shards.reshape(N_CHIP, r, n).transpose(1, 0, 2).reshape(r, N_CHIP * n)


def kernel(x, c, w_ada, b_ada, norm1_g, w_in, gla_w_gate, gla_b_gate, gla_norm_g, q_norm_g, k_norm_g, w_out, norm2_g, w_up, conv_w, conv_b, w_down, loss_target, m_w_ada, m_b_ada, m_norm1_g, m_w_in, m_gla_w_gate, m_gla_b_gate, m_gla_norm_g, m_q_norm_g, m_k_norm_g, m_w_out, m_norm2_g, m_w_up, m_conv_w, m_conv_b, m_w_down, v_w_ada, v_b_ada, v_norm1_g, v_w_in, v_gla_w_gate, v_gla_b_gate, v_gla_norm_g, v_q_norm_g, v_k_norm_g, v_w_out, v_norm2_g, v_w_up, v_conv_w, v_conv_b, v_w_down):
    d = D_MODEL
    ax, ay, ac = lax.axis_index("x"), lax.axis_index("y"), lax.axis_index("c")
    chip, dev = 2 * ax + ay, 4 * ax + 2 * ay + ac

    cond = _silu_rows(jnp.broadcast_to(c, (8, d)), name="cond_silu")[0:1]
    small_in = jnp.concatenate([_rows128(cond), _rows128(conv_w[0]), _rows128(gla_w_gate[0])], axis=0)
    small_in = _rows128(small_in, 56)
    got = _all_gather_small(small_in, name="gather_small").reshape(N_DEV, 56, 128)
    cond_all = got[:, 0:8].reshape(N_DEV, d)
    conv_w_full = _from_col_shards(got[0::2, 8:41].reshape(N_CHIP, 3 * 1408 // 128, 128), 3, 1408)
    gate_full = _from_col_shards(got[0::2, 41:49].reshape(N_CHIP, 16 * 64 // 128, 128), GLA_GATE_RANK, 64)
    mod_part = _ada_mod(cond_all, w_ada[0], name="ada_mod")
    mod_got = _all_gather_small(_rows128(mod_part), name="gather_mod").reshape(N_DEV, N_DEV, 1536)
    mod_all = mod_got[0::2].transpose(1, 0, 2).reshape(N_DEV, 6 * d) + b_ada
    mod = lax.dynamic_slice_in_dim(mod_all, dev, 1, axis=0)

    own = [w[0].astype(BF16).reshape(2, w.shape[1] // 2, w.shape[2]) for w in (w_in, w_out, w_up, w_down)]
    with_own = lambda got, mine: [lax.dynamic_update_index_in_dim(t, o, chip, 0) for t, o in zip(got, mine)]
    got_in, got_out = with_own(_gather_weight_shards(own[:2], name="gather_weights"), own[:2])
    w_in_full = got_in.reshape(N_CHIP, d, 772).transpose(1, 0, 2).reshape(d, N_CHIP * 772)
    w_out_full = got_out.reshape(d, d)
    exchanged = mod_all[0:1, 0:1] + got_in[0, 0, 0:1, 0:1].astype(F32)
    send_sems, recv_sems, own_thru, lands, token = _gather_late_start(own[2:], exchanged, name="gather_late_start")
    mod = mod + token[0:1, 0:1]

    def ffn_weights(after):
        mine, landed = _gather_late_wait(send_sems, recv_sems, own_thru, lands, after, name="gather_late_wait")
        got_up, got_down = with_own(landed, mine)
        return got_up.reshape(N_CHIP, d, 1408).transpose(1, 0, 2).reshape(d, 2 * D_FF), got_down.reshape(D_FF, d)

    ffn_reduce, attn_reduce, attn_parts = [], [], []
    halves_of = lambda g: g.reshape(N_CHIP, 2, g.shape[-2] // 2, g.shape[-1])

    def ffn_grads_ready(g_wup_b, g_wdown_b):
        ffn_reduce.extend(_direct_reduce_start([halves_of(g_wup_b), halves_of(g_wdown_b.reshape(N_CHIP, D_FF // N_CHIP, d))],
                                               name="reduce_ffn_start"))
        return ffn_reduce[4]

    def attn_grads_ready(g_wi, g_wo):
        attn_parts.extend([_in_proj_grad_layout(g_wi).reshape(d, N_CHIP, 772).transpose(1, 0, 2), g_wo.reshape(N_CHIP, d // N_CHIP, d)])
        attn_reduce.extend(_direct_reduce_start([halves_of(g.astype(BF16)) for g in attn_parts], name="reduce_attn_start"))
        return attn_reduce[4]

    err2, grad_x, (g_wi, g_wo, g_wup, g_wdown), small = _local_step(
        x[0], loss_target[0], mod, _in_proj_layout(w_in_full), w_out_full, ffn_weights, ffn_grads_ready, attn_grads_ready,
        conv_w_full, conv_b,
        _gate_layout(gate_full), gla_b_gate, gla_norm_g, q_norm_g, k_norm_g, norm1_g, norm2_g)

    pieces = [err2[0], small["dmod"], small["norm1_g"], small["norm2_g"], small["gla_w_gate"].reshape(-1), small["gla_b_gate"],
              small["gla_norm_g"], small["q_norm_g"], small["k_norm_g"], small["conv_w"].reshape(-1), small["conv_b"]]
    sizes = [p.shape[0] for p in pieces]
    at = [sum(sizes[:i]) for i in range(len(sizes) + 1)]
    vec = _rows128(jnp.concatenate(pieces), 288)
    got = _all_gather_small(vec, name="gather_grads").reshape(N_DEV, 288, 128)
    total, loss8 = _sum_devices(got, name="sum_devices")
    total = total.reshape(-1)
    seg = lambda i: total[at[i]:at[i + 1]]
    dmod_all = got.reshape(N_DEV, -1)[:, at[1]:at[2]]
    g_small = dict(
        b_ada=seg(1)[None], norm1_g=seg(2)[None], norm2_g=seg(3)[None],
        gla_w_gate=lax.dynamic_slice_in_dim(seg(4).reshape(GLA_GATE_RANK, 256), chip * 64, 64, axis=1),
        gla_b_gate=seg(5)[None], gla_norm_g=seg(6)[None], q_norm_g=seg(7)[None], k_norm_g=seg(8)[None],
        conv_w=lax.dynamic_slice_in_dim(seg(9).reshape(3, 2 * D_FF), chip * 1408, 1408, axis=1), conv_b=seg(10)[None])
    dmod_cols = lax.dynamic_slice_in_dim(dmod_all.reshape(N_DEV, 6 * d), chip * 1536, 1536, axis=1)
    g_w_ada = _ada_grad(cond_all, dmod_cols, name="ada_grad")

    core_id, chip_id = jnp.reshape(ac, (1,)).astype(jnp.int32), jnp.reshape(chip, (1,)).astype(jnp.int32)
    landed = (_direct_reduce_wait(*attn_reduce[:4], grad_x, name="reduce_attn_wait")
              + _direct_reduce_wait(*ffn_reduce[:4], grad_x, name="reduce_ffn_wait"))
    own = attn_parts + [g_wup, g_wdown.reshape(N_CHIP, D_FF // N_CHIP, d)]
    summed = [_direct_reduce_add(g, t, chip_id, core_id, name=f"reduce_add_{tag}")
              for g, t, tag in zip(own, landed, ("w_in", "w_out", "w_up", "w_down"))]
    others = _share_halves(summed, name="share_pair")
    g_big = [jnp.concatenate([jnp.where(ac == 0, mine, other), jnp.where(ac == 0, other, mine)], axis=0)
             for mine, other in zip(summed, others)]

    grads = dict(w_ada=g_w_ada, w_in=g_big[0], w_out=g_big[1], w_up=g_big[2], w_down=g_big[3], **g_small)
    names = ["w_ada", "b_ada", "norm1_g", "w_in", "gla_w_gate", "gla_b_gate", "gla_norm_g", "q_norm_g", "k_norm_g", "w_out",
             "norm2_g", "w_up", "conv_w", "conv_b", "w_down"]
    ws = dict(w_ada=w_ada, b_ada=b_ada, norm1_g=norm1_g, w_in=w_in, gla_w_gate=gla_w_gate, gla_b_gate=gla_b_gate,
              gla_norm_g=gla_norm_g, q_norm_g=q_norm_g, k_norm_g=k_norm_g, w_out=w_out, norm2_g=norm2_g, w_up=w_up,
              conv_w=conv_w, conv_b=conv_b, w_down=w_down)
    ms = dict(w_ada=m_w_ada, b_ada=m_b_ada, norm1_g=m_norm1_g, w_in=m_w_in, gla_w_gate=m_gla_w_gate, gla_b_gate=m_gla_b_gate,
              gla_norm_g=m_gla_norm_g, q_norm_g=m_q_norm_g, k_norm_g=m_k_norm_g, w_out=m_w_out, norm2_g=m_norm2_g, w_up=m_w_up,
              conv_w=m_conv_w, conv_b=m_conv_b, w_down=m_w_down)
    vs = dict(w_ada=v_w_ada, b_ada=v_b_ada, norm1_g=v_norm1_g, w_in=v_w_in, gla_w_gate=v_gla_w_gate, gla_b_gate=v_gla_b_gate,
              gla_norm_g=v_gla_norm_g, q_norm_g=v_q_norm_g, k_norm_g=v_k_norm_g, w_out=v_w_out, norm2_g=v_norm2_g, w_up=v_w_up,
              conv_w=v_conv_w, conv_b=v_conv_b, w_down=v_w_down)
    g_out, d_out, m_out, v_out = [], [], [], []
    for nm in names:
        shape = ws[nm].shape
        flip = (lambda t: t.T) if shape[-1] % 128 and shape[-2] % 128 == 0 else (lambda t: t)
        w2 = flip(ws[nm].reshape(shape[-2:]))
        g2 = flip(grads[nm].reshape(shape[-2:]))
        dl, mn, vn = _adamw(w2, g2, flip(ms[nm].reshape(shape[-2:])), flip(vs[nm].reshape(shape[-2:])), name=f"adamw_{nm}")
        for outs, t in ((g_out, g2), (d_out, dl), (m_out, mn), (v_out, vn)):
            outs.append(flip(t).reshape(shape))
    return (loss8[0, 0], grad_x[None], *g_out, *d_out, *m_out, *v_out)
```

```python
import functools

import jax
import jax.numpy as jnp
from jax import lax
from jax.experimental import pallas as pl
from jax.experimental.pallas import tpu as pltpu

F32, BF16 = jnp.float32, jnp.bfloat16
HIGHEST = lax.Precision.HIGHEST
MESH = pl.DeviceIdType.MESH

D_MODEL = 1024
GLA_CHUNK = 64
GLA_GATE_TAU = 16.0
GLA_GATE_RANK = 16
HEAD_LANES = 128
ATTN_BLOCK = 128
DILATIONS = (1, 4, 16)
ALIBI_SLOPES = tuple(2.0 ** (-(h + 1)) for h in range(8))
D_FF = 2816
EPS = 1e-6
C_GQ, C_GK, C_GV, C_GR, C_AQ, C_AK, C_AV, C_LR, PROJ_W = 0, 256, 512, 1024, 1536, 2048, 2560, 3072, 3200
ADAM_LR, ADAM_B1, ADAM_B2, ADAM_EPS, ADAM_WD, ADAM_STEP = 0.001, 0.9, 0.999, 1e-08, 0.01, 10
VMEM_LIMIT_BYTES = 56 * 1024 * 1024
ROW_TILE = 256


def _params(*sem):
    return pltpu.CompilerParams(dimension_semantics=sem or None, vmem_limit_bytes=VMEM_LIMIT_BYTES)


def _nt(a, b):
    return lax.dot_general(a, b, (((1,), (1,)), ((), ())), preferred_element_type=F32)


def _tn(a, b):
    return lax.dot_general(a, b, (((0,), (0,)), ((), ())), preferred_element_type=F32)


def _nn(a, b, precision=None):
    return jnp.dot(a, b, preferred_element_type=F32, precision=precision)


def _split3(v):
    hi = v.astype(BF16)
    rest = v - hi.astype(F32)
    mid = rest.astype(BF16)
    return hi, mid, (rest - mid.astype(F32)).astype(BF16)


def _sum_right(v, ones):
    hi, mid, lo = _split3(v)
    return (_nn(lo, ones) + _nn(mid, ones)) + _nn(hi, ones)


def _sum_left(ones, v):
    hi, mid, lo = _split3(v)
    return (_nn(ones, lo) + _nn(ones, mid)) + _nn(ones, hi)


def _fold8(v):
    return v.reshape(v.shape[0] // 8, 8, v.shape[1]).sum(axis=0)


def _spread_total(ref):
    t = ref[...]
    ref[...] = jnp.broadcast_to(jnp.sum(t, axis=-2, keepdims=True), t.shape)


def _sigmoid(x):
    return 1.0 / (1.0 + jnp.exp(-x))


def _mm(a, b, *, ta=False, tb=False, out_dtype=F32, tm, tn, tk, shard_cols=False, also_bf16=False, name):
    (k_a, m) = a.shape if ta else a.shape[::-1]
    (k_b, n) = b.shape[::-1] if tb else b.shape
    assert k_a == k_b and m % tm == 0 and n % tn == 0 and k_a % tk == 0, (name, a.shape, b.shape)
    nk = k_a // tk
    assert nk == 1 or out_dtype == F32, name
    dims = (((0 if ta else 1,), (1 if tb else 0,)), ((), ()))

    def body(a_ref, b_ref, o_ref, *rounded):
        k = pl.program_id(2)
        part = lax.dot_general(a_ref[...].astype(BF16), b_ref[...].astype(BF16), dims, preferred_element_type=F32)
        if nk == 1:
            o_ref[...] = part.astype(out_dtype)
        else:
            @pl.when(k == 0)
            def _():
                o_ref[...] = part

            @pl.when(k > 0)
            def _():
                o_ref[...] += part

        if also_bf16:
            @pl.when(k == nk - 1)
            def _():
                rounded[0][...] = o_ref[...].astype(BF16)

    a_spec = pl.BlockSpec((tk, tm), lambda i, j, k: (k, i)) if ta else pl.BlockSpec((tm, tk), lambda i, j, k: (i, k))
    b_spec = pl.BlockSpec((tn, tk), lambda i, j, k: (j, k)) if tb else pl.BlockSpec((tk, tn), lambda i, j, k: (k, j))
    if shard_cols:
        o_spec, o_shape = pl.BlockSpec((None, tm, tn), lambda i, j, k: (j, i, 0)), (n // tn, m, tn)
    else:
        o_spec, o_shape = pl.BlockSpec((tm, tn), lambda i, j, k: (i, j)), (m, n)
    shapes = [jax.ShapeDtypeStruct(o_shape, out_dtype)] + ([jax.ShapeDtypeStruct(o_shape, BF16)] if also_bf16 else [])
    out = pl.pallas_call(
        body, name=name, grid=(m // tm, n // tn, nk), in_specs=[a_spec, b_spec], out_specs=[o_spec] * len(shapes),
        out_shape=shapes, compiler_params=_params("parallel", "parallel", "arbitrary"))(a, b)
    return out if also_bf16 else out[0]


def _norm_mod_fwd(x, branch, gate, gain, scale, shift, *, name):
    s, d = x.shape
    tm = ROW_TILE
    has_branch = branch is not None

    def body(*refs):
        if has_branch:
            x_ref, br_ref, gate_ref, gain_ref, sc_ref, sh_ref, x1_ref, h_ref, ht_ref = refs
            xv = x_ref[...] + gate_ref[...] * br_ref[...]
            x1_ref[...] = xv
        else:
            x_ref, gain_ref, sc_ref, sh_ref, h_ref, ht_ref = refs
            xv = x_ref[...]
        r = lax.rsqrt(jnp.mean(xv * xv, axis=-1, keepdims=True) + EPS)
        h = (xv * r) * gain_ref[...] * (1.0 + sc_ref[...]) + sh_ref[...]
        h_ref[...] = h.astype(BF16)
        ht_ref[...] = h.T.astype(BF16)

    row = pl.BlockSpec((tm, d), lambda i: (i, 0))
    col = pl.BlockSpec((d, tm), lambda i: (0, i))
    vec = pl.BlockSpec((1, d), lambda i: (0, 0))
    h_shapes = [jax.ShapeDtypeStruct((s, d), BF16), jax.ShapeDtypeStruct((d, s), BF16)]
    if has_branch:
        return pl.pallas_call(
            body, name=name, grid=(s // tm,), in_specs=[row, row, vec, vec, vec, vec], out_specs=[row, row, col],
            out_shape=[jax.ShapeDtypeStruct((s, d), F32)] + h_shapes,
            compiler_params=_params("parallel"))(x, branch, gate, gain, scale, shift)
    h, ht = pl.pallas_call(
        body, name=name, grid=(s // tm,), in_specs=[row, vec, vec, vec], out_specs=[row, col],
        out_shape=h_shapes, compiler_params=_params("parallel"))(x, gain, scale, shift)
    return x, h, ht


def _norm_mod_bwd(x, dh, dres, gain, scale, branch, gate, *, name):
    s, d = x.shape
    tm = ROW_TILE
    has_branch = branch is not None

    def body(*refs):
        if has_branch:
            x_ref, dh_ref, dres_ref, gain_ref, sc_ref, br_ref, gate_ref, dx_ref, dbr_ref, sums_ref = refs
        else:
            x_ref, dh_ref, dres_ref, gain_ref, sc_ref, dx_ref, sums_ref = refs
        i = pl.program_id(0)

        @pl.when(i == 0)
        def _():
            sums_ref[...] = jnp.zeros_like(sums_ref)

        xv, dhv = x_ref[...], dh_ref[...]
        r = lax.rsqrt(jnp.mean(xv * xv, axis=-1, keepdims=True) + EPS)
        xn = xv * r
        dxn = dhv * (gain_ref[...] * (1.0 + sc_ref[...]))
        dx = dres_ref[...] + r * (dxn - xn * jnp.mean(dxn * xn, axis=-1, keepdims=True))
        dx_ref[...] = dx
        sums_ref[0] += _fold8(dhv * xn)
        sums_ref[1] += _fold8(dhv)
        if has_branch:
            dbr_ref[...] = (gate_ref[...] * dx).astype(BF16)
            sums_ref[2] += _fold8(dx * br_ref[...])

        @pl.when(i == s // tm - 1)
        def _():
            _spread_total(sums_ref)

    row = pl.BlockSpec((tm, d), lambda i: (i, 0))
    vec = pl.BlockSpec((1, d), lambda i: (0, 0))
    sums = pl.BlockSpec((3, 8, d), lambda i: (0, 0, 0))
    sums_shape = jax.ShapeDtypeStruct((3, 8, d), F32)
    if has_branch:
        return pl.pallas_call(
            body, name=name, grid=(s // tm,), in_specs=[row, row, row, vec, vec, row, vec], out_specs=[row, row, sums],
            out_shape=[jax.ShapeDtypeStruct((s, d), F32), jax.ShapeDtypeStruct((s, d), BF16), sums_shape],
            compiler_params=_params("arbitrary"))(x, dh, dres, gain, scale, branch, gate)
    dx, sm = pl.pallas_call(
        body, name=name, grid=(s // tm,), in_specs=[row, row, row, vec, vec], out_specs=[row, sums],
        out_shape=[jax.ShapeDtypeStruct((s, d), F32), sums_shape],
        compiler_params=_params("arbitrary"))(x, dh, dres, gain, scale)
    return dx, None, sm


GLA_ROWS = 256


def _gla_block_setup(lr_ref, wg_ref, bg_ref):
    t, c = GLA_ROWS, GLA_CHUNK
    ri = lax.broadcasted_iota(jnp.int32, (t, t), 0)
    ci = lax.broadcasted_iota(jnp.int32, (t, t), 1)
    same = (ri // c) == (ci // c)
    causal, upper = same & (ci <= ri), same & (ci >= ri)
    z = _nn(lr_ref[...].astype(BF16), wg_ref[...]) + bg_ref[...]
    g = (jnp.minimum(z, 0.0) - jnp.log(1.0 + jnp.exp(-jnp.abs(z)))) * (1.0 / GLA_GATE_TAU)
    hi, mid, lo = _split3(g)
    total = lambda ones: (_nn(ones, lo) + _nn(ones, mid)) + _nn(ones, hi)
    return z, total(causal.astype(BF16)), total(same.astype(BF16)), causal, upper


def _chunks(t):
    return [t[i * GLA_CHUNK:(i + 1) * GLA_CHUNK] for i in range(GLA_ROWS // GLA_CHUNK)]


def _gla_fwd(proj, wg, bg, gn, *, name):
    s = proj.shape[0]
    tb, c = GLA_ROWS, GLA_CHUNK
    cb = tb // c

    def body(q_ref, k_ref, v_ref, r_ref, lr_ref, wg_ref, bg_ref, gn_ref, o_ref, y_ref, st_ref, state):
        i = pl.program_id(0)

        @pl.when(i == 0)
        def _():
            state[...] = jnp.zeros_like(state)

        low = lax.broadcasted_iota(jnp.int32, (tb, HEAD_LANES), 1) < 64
        masks = (low, jnp.logical_not(low))
        _, b, b_end, causal, _ = _gla_block_setup(lr_ref, wg_ref, bg_ref)
        pairs = []
        for p in range(2):
            cols = pl.ds(p * HEAD_LANES, HEAD_LANES)
            bp, bep = (t[:, p * HEAD_LANES:(p + 1) * HEAD_LANES] for t in (b, b_end))
            k = k_ref[:, cols]
            q_in = q_ref[:, cols] * 0.125 * jnp.exp(bp)
            k_out = (k * jnp.exp(-bp)).astype(BF16)
            k_end = k * jnp.exp(bep - bp)
            qms = [jnp.where(m, q_in, 0.0).astype(BF16) for m in masks]
            kes = [jnp.where(m, k_end, 0.0).astype(BF16) for m in masks]
            vs = [v_ref[:, pl.ds((2 * p + e) * HEAD_LANES, HEAD_LANES)].astype(BF16) for e in range(2)]
            grow = [_tn(v0, k0) + _tn(v1, k1) for v0, k0, v1, k1 in zip(_chunks(vs[0]), _chunks(kes[0]), _chunks(vs[1]), _chunks(kes[1]))]
            pairs.append((bep, k_out, qms, vs, grow))
        entering = [[], []]
        for p, (bep, _, _, _, grow) in enumerate(pairs):
            st = state[p]
            for ch in range(cb):
                entering[p].append(st)
                st_ref[ch, p] = st
                st = st * jnp.exp(bep[ch * c:ch * c + 1, :]) + grow[ch]
            state[p] = st
        for p, (_, k_out, qms, vs, _) in enumerate(pairs):
            for e in range(2):
                hc = pl.ds((2 * p + e) * HEAD_LANES, HEAD_LANES)
                a = jnp.where(causal, _nt(qms[e], k_out), 0.0).astype(BF16)
                carried = jnp.concatenate([_nt(qc, sc.astype(BF16)) for qc, sc in zip(_chunks(qms[e]), entering[p])], axis=0)
                o = _nn(a, vs[e]) + carried
                o_ref[:, hc] = o
                rr = r_ref[:, hc]
                on = o * lax.rsqrt(jnp.mean(o * o, axis=-1, keepdims=True) + EPS)
                y_ref[:, hc] = (on * gn_ref[...] * (rr * _sigmoid(rr))).astype(BF16)

    def col(width, at):
        return pl.BlockSpec((tb, width), lambda i: (i, at // width))

    full = lambda shape: pl.BlockSpec(shape, lambda i: tuple(0 for _ in shape))
    return pl.pallas_call(
        body, name=name, grid=(s // tb,),
        in_specs=[col(256, C_GQ), col(256, C_GK), col(512, C_GV), col(512, C_GR), col(128, C_LR),
                  full((HEAD_LANES, 256)), full((1, 256)), full((1, HEAD_LANES))],
        out_specs=[pl.BlockSpec((tb, 512), lambda i: (i, 0)), pl.BlockSpec((tb, 512), lambda i: (i, 0)),
                   pl.BlockSpec((cb, 2, HEAD_LANES, HEAD_LANES), lambda i: (i, 0, 0, 0))],
        out_shape=[jax.ShapeDtypeStruct((s, 512), F32), jax.ShapeDtypeStruct((s, 512), BF16),
                   jax.ShapeDtypeStruct((s // c, 2, HEAD_LANES, HEAD_LANES), F32)],
        scratch_shapes=[pltpu.VMEM((2, HEAD_LANES, HEAD_LANES), F32)],
        compiler_params=_params("arbitrary"))(proj, proj, proj, proj, proj, wg, bg, gn)


def _gla_bwd(proj, wg, bg, gn, o_raw, states, dmixed, *, name):
    s = proj.shape[0]
    tb, c = GLA_ROWS, GLA_CHUNK
    cb = tb // c
    nblk, nch = s // tb, s // c

    def body(q_ref, k_ref, v_ref, r_ref, lr_ref, wg_ref, bg_ref, gn_ref, o_ref, st_ref, stn_ref, dy_ref,
             dq_ref, dk_ref, dv_ref, dr_ref, dlr_ref, gwg_ref, sums_ref, dstate):
        i = pl.program_id(0)

        @pl.when(i == 0)
        def _():
            dstate[...] = jnp.zeros_like(dstate)
            gwg_ref[...] = jnp.zeros_like(gwg_ref)
            sums_ref[...] = jnp.zeros_like(sums_ref)

        low = lax.broadcasted_iota(jnp.int32, (tb, HEAD_LANES), 1) < 64
        masks = (low, jnp.logical_not(low))
        z, b, b_end, causal, upper = _gla_block_setup(lr_ref, wg_ref, bg_ref)
        lr_b = lr_ref[...].astype(BF16)
        dlr = jnp.zeros((tb, HEAD_LANES), F32)
        per_chunk = lambda rows, mats, fn: jnp.concatenate([fn(r, m.astype(BF16)) for r, m in zip(_chunks(rows), mats)], axis=0)
        pairs = []
        for p in range(2):
            cols = pl.ds(p * HEAD_LANES, HEAD_LANES)
            sl = slice(p * HEAD_LANES, (p + 1) * HEAD_LANES)
            bp, bep = b[:, sl], b_end[:, sl]
            e_in, e_out, e_end = jnp.exp(bp), jnp.exp(-bp), jnp.exp(bep - bp)
            q = q_ref[:, cols] * 0.125
            k = k_ref[:, cols]
            q_in, k_out, k_end = q * e_in, k * e_out, k * e_end
            qms = [jnp.where(m, q_in, 0.0).astype(BF16) for m in masks]
            kms_out = [jnp.where(m, k_out, 0.0).astype(BF16) for m in masks]
            kms_end = [jnp.where(m, k_end, 0.0).astype(BF16) for m in masks]
            vs, dos = [], []
            for e in range(2):
                hc = pl.ds((2 * p + e) * HEAD_LANES, HEAD_LANES)
                o, rr, dy = o_ref[:, hc], r_ref[:, hc], dy_ref[:, hc]
                sg = _sigmoid(rr)
                rs = lax.rsqrt(jnp.mean(o * o, axis=-1, keepdims=True) + EPS)
                on = o * rs
                t = dy * (rr * sg)
                sums_ref[1, :, hc] += _fold8(t * on)
                dn = t * gn_ref[...]
                dos.append((rs * (dn - on * jnp.mean(dn * on, axis=-1, keepdims=True))).astype(BF16))
                dr_ref[:, hc] = (dy * on * gn_ref[...] * (sg * (1.0 + rr * (1.0 - sg)))).astype(BF16)
                vs.append(v_ref[:, hc].astype(BF16))
            grow = [_tn(d0, q0) + _tn(d1, q1) for d0, q0, d1, q1 in zip(_chunks(dos[0]), _chunks(qms[0]), _chunks(dos[1]), _chunks(qms[1]))]
            pairs.append((bep, e_in, e_out, e_end, q, k, qms, kms_out, kms_end, vs, dos, grow))
        chains = []
        for p in range(2):
            bep, grow = pairs[p][0], pairs[p][-1]
            entering = [st_ref[ch, p] for ch in range(cb)]
            dst, leaving_grad = dstate[p], [None] * cb
            for ch in reversed(range(cb)):
                leaving_grad[ch] = dst
                dst = dst * jnp.exp(bep[ch * c:ch * c + 1, :]) + grow[ch]
            dstate[p] = dst
            chains.append((entering, leaving_grad))
        for p in range(2):
            cols = pl.ds(p * HEAD_LANES, HEAD_LANES)
            sl = slice(p * HEAD_LANES, (p + 1) * HEAD_LANES)
            _, e_in, e_out, e_end, q, k, qms, kms_out, kms_end, vs, dos, _ = pairs[p]
            entering, leaving_grad = chains[p]
            leaving = entering[1:] + [stn_ref[0, p]]
            felt = jnp.concatenate([jnp.broadcast_to(jnp.sum(dg_st * st, axis=0, keepdims=True), (c, HEAD_LANES))
                                    for dg_st, st in zip(leaving_grad, leaving)], axis=0)
            dq_in = jnp.zeros((tb, HEAD_LANES), F32)
            dk_out = jnp.zeros((tb, HEAD_LANES), F32)
            dk_end = jnp.zeros((tb, HEAD_LANES), F32)
            for e in range(2):
                hc = pl.ds((2 * p + e) * HEAD_LANES, HEAD_LANES)
                a = jnp.where(causal, _nt(qms[e], kms_out[e]), 0.0).astype(BF16)
                da = jnp.where(causal, _nt(dos[e], vs[e]), 0.0).astype(BF16)
                dv_ref[:, hc] = (_tn(a, dos[e]) + per_chunk(kms_end[e], leaving_grad, _nt)).astype(BF16)
                dq_in = dq_in + jnp.where(masks[e], per_chunk(dos[e], entering, _nn) + _nn(da, kms_out[e]), 0.0)
                dk_out = dk_out + _tn(da, qms[e])
                dk_end = dk_end + jnp.where(masks[e], per_chunk(vs[e], leaving_grad, _nn), 0.0)
            dq = dq_in * e_in
            dk = dk_out * e_out + dk_end * e_end
            dq_ref[:, cols] = (dq * 0.125).astype(BF16)
            dk_ref[:, cols] = dk.astype(BF16)
            dg = _sum_left(upper.astype(BF16), q * dq - k * dk) + felt
            dz = dg * (1.0 / GLA_GATE_TAU) * _sigmoid(-z[:, sl])
            dz_b = dz.astype(BF16)
            sums_ref[0, :, cols] += _fold8(dz)
            dlr = dlr + _nt(dz_b, wg_ref[:, cols])
            gwg_ref[:, cols] += _tn(lr_b, dz_b)
        dlr_ref[...] = dlr.astype(BF16)

        @pl.when(i == nblk - 1)
        def _():
            _spread_total(sums_ref)

    rev = lambda i: nblk - 1 - i

    def col(width, at):
        return pl.BlockSpec((tb, width), lambda i: (rev(i), at // width))

    full = lambda shape: pl.BlockSpec(shape, lambda i: tuple(0 for _ in shape))
    out_col = lambda width: pl.BlockSpec((tb, width), lambda i: (rev(i), 0))
    return pl.pallas_call(
        body, name=name, grid=(nblk,),
        in_specs=[col(256, C_GQ), col(256, C_GK), col(512, C_GV), col(512, C_GR), col(128, C_LR),
                  full((HEAD_LANES, 256)), full((1, 256)), full((1, HEAD_LANES)),
                  pl.BlockSpec((tb, 512), lambda i: (rev(i), 0)),
                  pl.BlockSpec((cb, 2, HEAD_LANES, HEAD_LANES), lambda i: (rev(i), 0, 0, 0)),
                  pl.BlockSpec((1, 2, HEAD_LANES, HEAD_LANES), lambda i: (jnp.minimum((rev(i) + 1) * cb, nch - 1), 0, 0, 0)),
                  pl.BlockSpec((tb, 512), lambda i: (rev(i), 0))],
        out_specs=[out_col(256), out_col(256), out_col(512), out_col(512), out_col(128),
                   full((HEAD_LANES, 256)), full((2, 8, 512))],
        out_shape=[jax.ShapeDtypeStruct((s, 256), BF16), jax.ShapeDtypeStruct((s, 256), BF16),
                   jax.ShapeDtypeStruct((s, 512), BF16), jax.ShapeDtypeStruct((s, 512), BF16),
                   jax.ShapeDtypeStruct((s, 128), BF16), jax.ShapeDtypeStruct((HEAD_LANES, 256), F32),
                   jax.ShapeDtypeStruct((2, 8, 512), F32)],
        scratch_shapes=[pltpu.VMEM((2, HEAD_LANES, HEAD_LANES), F32)],
        compiler_params=_params("arbitrary"))(proj, proj, proj, proj, proj, wg, bg, gn, o_raw, states, states, dmixed)


def _head_sums(v):
    ri = lax.broadcasted_iota(jnp.int32, (HEAD_LANES, HEAD_LANES), 0) // 64
    ci = lax.broadcasted_iota(jnp.int32, (HEAD_LANES, HEAD_LANES), 1) // 64
    ones = (ri == ci).astype(BF16)
    return jnp.concatenate([_sum_right(v[:, p * HEAD_LANES:(p + 1) * HEAD_LANES], ones) for p in range(4)], axis=1)


def _attn_prep(proj, qg, kg, *, name):
    s = proj.shape[0]
    tm = ROW_TILE

    def body(q_ref, k_ref, qg_ref, kg_ref, qa_ref, ka_ref):
        q, k = q_ref[...], k_ref[...]
        qr = lax.rsqrt(_head_sums(q * q) * (1.0 / 64) + EPS)
        kr = lax.rsqrt(_head_sums(k * k) * (1.0 / 64) + EPS)
        qa_ref[...] = q * qr * qg_ref[...] * 0.125
        ka_ref[...] = k * kr * kg_ref[...]

    col = lambda at: pl.BlockSpec((tm, 512), lambda i: (i, at // 512))
    vec = pl.BlockSpec((1, 512), lambda i: (0, 0))
    out = pl.BlockSpec((tm, 512), lambda i: (i, 0))
    return pl.pallas_call(
        body, name=name, grid=(s // tm,), in_specs=[col(C_AQ), col(C_AK), vec, vec], out_specs=[out] * 2,
        out_shape=[jax.ShapeDtypeStruct((s, 512), F32)] * 2, compiler_params=_params("parallel"))(proj, proj, qg, kg)


FAR = 1e30
LOG2E, LN2 = 1.4426950408889634, 0.6931471805599453


def _attn_distance(first):
    blk = ATTN_BLOCK
    iq = lax.broadcasted_iota(jnp.int32, (2 * blk, 2 * blk), 0) & (blk - 1)
    ik = lax.broadcasted_iota(jnp.int32, (2 * blk, 2 * blk), 1)
    rel = iq + blk - ik
    valid = (rel >= 0) & (rel <= blk) & (jnp.logical_not(first) | (ik >= blk))
    return jnp.where(valid, rel.astype(F32), FAR)


def _stack_heads(t2):
    low = lax.broadcasted_iota(jnp.int32, t2.shape, 1) < 64
    return jnp.concatenate([jnp.where(low, t2, 0.0), jnp.where(low, 0.0, t2)], axis=0).astype(BF16)


def _unstack_heads(t):
    blk = ATTN_BLOCK
    low = lax.broadcasted_iota(jnp.int32, (blk, HEAD_LANES), 1) < 64
    return jnp.where(low, t[0:blk], t[blk:2 * blk])


def _attn_scores(qs, kcat, slopes, dil, dist):
    top = lax.broadcasted_iota(jnp.int32, (2 * ATTN_BLOCK, 1), 0) < ATTN_BLOCK
    return _nt(qs, kcat) - jnp.where(top, slopes[0] * (dil * LOG2E), slopes[1] * (dil * LOG2E)) * dist


def _pair_slopes(p):
    if isinstance(p, int):
        return ALIBI_SLOPES[2 * p], ALIBI_SLOPES[2 * p + 1]
    pick = lambda e: jnp.where(p == 0, ALIBI_SLOPES[e], jnp.where(p == 1, ALIBI_SLOPES[2 + e],
                               jnp.where(p == 2, ALIBI_SLOPES[4 + e], ALIBI_SLOPES[6 + e])))
    return pick(0), pick(1)


ATTN_GROUP = 4


def _each(fn, *lists):
    return [fn(*args) for args in zip(*lists)]


def _attn_group_fwd(q2s, kcats, vcats, slopes, dil, dist):
    qs = _each(lambda q2: _stack_heads(q2 * LOG2E), q2s)
    sc = _each(lambda q, k, sl: _attn_scores(q, k, sl, dil, dist), qs, kcats, slopes)
    m = _each(lambda s: jnp.max(s, axis=-1, keepdims=True), sc)
    pr = _each(lambda s, mx: jnp.exp2(s - mx), sc, m)
    den = _each(lambda p: jnp.sum(p, axis=-1, keepdims=True), pr)
    o = _each(lambda p, v, d: _nn(p.astype(BF16), v) / d, pr, vcats, den)
    lse = _each(lambda mx, d, t: jnp.broadcast_to(mx + jnp.log2(d), t.shape), m, den, o)
    return _each(lambda t, l: (_unstack_heads(t), _unstack_heads(l)), o, lse)


def _attn_group_bwd(q2s, kcats, vcats, do2s, y2s, lse2s, slopes, dil, dist):
    lane = lax.broadcasted_iota(jnp.int32, (ATTN_BLOCK, HEAD_LANES), 1)
    low = lane < 64
    per_head = lambda t, pick: jnp.concatenate([jnp.sum(jnp.where(pick(0), t, 0.0), axis=-1, keepdims=True),
                                                jnp.sum(jnp.where(pick(1), t, 0.0), axis=-1, keepdims=True)], axis=0)
    lse = _each(lambda l: per_head(l, lambda e: lane == 64 * e), lse2s)
    delta = _each(lambda d, y: per_head(d * y, lambda e: low if e == 0 else jnp.logical_not(low)), do2s, y2s)
    qs = _each(lambda q2: _stack_heads(q2 * LOG2E), q2s)
    dos = _each(_stack_heads, do2s)
    sc = _each(lambda q, k, sl: _attn_scores(q, k, sl, dil, dist), qs, kcats, slopes)
    pr = _each(lambda s, l: jnp.exp2(s - l), sc, lse)
    dp = _each(_nt, dos, vcats)
    ds = _each(lambda p, d, dl: (p * (d - dl)).astype(BF16), pr, dp, delta)
    dq = _each(lambda d, k: _unstack_heads(_nn(d, k)), ds, kcats)
    dk = _each(lambda d, q: _tn(d, q) * LN2, ds, qs)
    dv = _each(lambda p, d: _tn(p.astype(BF16), d), pr, dos)
    return list(zip(dq, dk, dv))


def _attn_specs(dil):
    rows = ATTN_BLOCK * dil
    if dil == 1:
        cur = lambda at: pl.BlockSpec((rows, 512), lambda n: (n, at // 512))
        prev = lambda at: pl.BlockSpec((rows, 512), lambda n: (jnp.maximum(n - 1, 0), at // 512))
    else:
        cur = lambda at: pl.BlockSpec((rows, HEAD_LANES), lambda n, p: (n, at // HEAD_LANES + p))
        prev = lambda at: pl.BlockSpec((rows, HEAD_LANES), lambda n, p: (jnp.maximum(n - 1, 0), at // HEAD_LANES + p))
    return cur, prev


def _attn_loop(dil, one_group, p):
    if dil == 1:
        one_group([(slice(None), pl.ds(p * HEAD_LANES, HEAD_LANES), p) for p in range(ATTN_GROUP)])
    else:
        group = min(dil, ATTN_GROUP)

        def step(g, carry):
            one_group([(pl.ds(g * group + j, ATTN_BLOCK, stride=dil), slice(None), p) for j in range(group)])
            return carry

        if dil == group:
            step(0, 0)
        else:
            lax.fori_loop(0, dil // group, step, 0)


def _dil_attn_fwd(qa, ka, proj, dil, *, name):
    s = qa.shape[0]

    def body(q_ref, kp_ref, kc_ref, vp_ref, vc_ref, o_ref, lse_ref):
        dist = _attn_distance(pl.program_id(0) == 0)
        pair = None if dil == 1 else pl.program_id(1)

        def one_group(items):
            both = lambda a, b: [jnp.concatenate([a[rows, cols], b[rows, cols]], axis=0).astype(BF16) for rows, cols, _ in items]
            outs = _attn_group_fwd([q_ref[rows, cols] for rows, cols, _ in items], both(kp_ref, kc_ref), both(vp_ref, vc_ref),
                                   [_pair_slopes(p) for _, _, p in items], dil, dist)
            for (rows, cols, _), (o2, lse2) in zip(items, outs):
                o_ref[rows, cols] = o2
                lse_ref[rows, cols] = lse2

        _attn_loop(dil, one_group, pair)

    cur, prev = _attn_specs(dil)
    grid = (s // ATTN_BLOCK,) if dil == 1 else (s // (ATTN_BLOCK * dil), 4)
    return pl.pallas_call(
        body, name=name, grid=grid, in_specs=[cur(0), prev(0), cur(0), prev(C_AV), cur(C_AV)], out_specs=[cur(0), cur(0)],
        out_shape=[jax.ShapeDtypeStruct((s, 512), F32)] * 2,
        compiler_params=_params(*["parallel"] * len(grid)))(qa, ka, ka, proj, proj)


def _dense_attn_fwd_merge(qa, ka, proj, others, y_gla, *, name):
    s = qa.shape[0]
    blk = ATTN_BLOCK

    def body(q_ref, kp_ref, kc_ref, vp_ref, vc_ref, oa_ref, la_ref, ob_ref, lb_ref, yg_ref, mixed_ref, y_ref, lse_ref):
        dist = _attn_distance(pl.program_id(0) == 0)
        mixed_ref[:, 0:512] = yg_ref[...]

        def one_group(items):
            both = lambda a, b: [jnp.concatenate([a[rows, cols], b[rows, cols]], axis=0).astype(BF16) for rows, cols, _ in items]
            outs = _attn_group_fwd([q_ref[rows, cols] for rows, cols, _ in items], both(kp_ref, kc_ref), both(vp_ref, vc_ref),
                                   [_pair_slopes(p) for _, _, p in items], 1, dist)
            for (_, cols, p), (o2, l2) in zip(items, outs):
                la, lb = la_ref[:, cols], lb_ref[:, cols]
                m = jnp.maximum(jnp.maximum(l2, la), lb)
                w0, wa, wb = jnp.exp2(l2 - m), jnp.exp2(la - m), jnp.exp2(lb - m)
                zs = w0 + wa + wb
                y = (w0 * o2 + wa * oa_ref[:, cols] + wb * ob_ref[:, cols]) / zs
                y_ref[:, cols] = y
                lse_ref[:, cols] = m + jnp.log2(zs)
                mixed_ref[:, pl.ds(512 + p * HEAD_LANES, HEAD_LANES)] = y.astype(BF16)

        _attn_loop(1, one_group, None)

    cur, prev = _attn_specs(1)
    here = pl.BlockSpec((blk, 512), lambda n: (n, 0))
    (oa, la), (ob, lb) = others
    return pl.pallas_call(
        body, name=name, grid=(s // blk,),
        in_specs=[cur(0), prev(0), cur(0), prev(C_AV), cur(C_AV)] + [here] * 5,
        out_specs=[pl.BlockSpec((blk, 1024), lambda n: (n, 0)), here, here],
        out_shape=[jax.ShapeDtypeStruct((s, 1024), BF16), jax.ShapeDtypeStruct((s, 512), F32),
                   jax.ShapeDtypeStruct((s, 512), F32)],
        compiler_params=_params("parallel"))(qa, ka, ka, proj, proj, oa, la, ob, lb, y_gla)


def _dil_attn_bwd(qa, ka, proj, y_att, lse, dmixed, dil, *, name):
    s = qa.shape[0]
    blk, rows_per_step = ATTN_BLOCK, ATTN_BLOCK * dil
    nb = s // rows_per_step
    step_axis = 0 if dil == 1 else 1

    def body(q_ref, kp_ref, kc_ref, vp_ref, vc_ref, y_ref, lse_ref, do_ref, dq_ref, dk_ref, dv_ref, dk_own, dv_own):
        n = pl.program_id(step_axis)
        pair = None if dil == 1 else pl.program_id(0)
        dist = _attn_distance(n == 0)

        @pl.when(n == 0)
        def _():
            dk_own[...] = jnp.zeros_like(dk_own)
            dv_own[...] = jnp.zeros_like(dv_own)

        def one_group(items):
            both = lambda a, b: [jnp.concatenate([a[rows, cols], b[rows, cols]], axis=0).astype(BF16) for rows, cols, _ in items]
            at = lambda ref: [ref[rows, cols] for rows, cols, _ in items]
            outs = _attn_group_bwd(at(q_ref), both(kp_ref, kc_ref), both(vp_ref, vc_ref), at(do_ref), at(y_ref), at(lse_ref),
                                   [_pair_slopes(p) for _, _, p in items], dil, dist)
            for (rows, cols, _), (dq, dk, dv) in zip(items, outs):
                dq_ref[rows, cols] = dq
                dk_ref[rows, cols] = dk_own[rows, cols] + dk[0:blk]
                dv_ref[rows, cols] = dv_own[rows, cols] + dv[0:blk]
                dk_own[rows, cols] = dk[blk:2 * blk]
                dv_own[rows, cols] = dv[blk:2 * blk]

        _attn_loop(dil, one_group, pair)

    width = 512 if dil == 1 else HEAD_LANES

    def spec(at, row_of):
        if dil == 1:
            return pl.BlockSpec((rows_per_step, width), lambda n: (row_of(n), at // width))
        return pl.BlockSpec((rows_per_step, width), lambda p, n: (row_of(n), at // width + p))

    cur = lambda at: spec(at, lambda n: n)
    prev = lambda at: spec(at, lambda n: jnp.maximum(n - 1, 0))
    own = spec(0, lambda n: 0)
    grid = (nb,) if dil == 1 else (4, nb)
    sems = ("arbitrary",) if dil == 1 else ("parallel", "arbitrary")
    dq, dk, dv, dk_last, dv_last = pl.pallas_call(
        body, name=name, grid=grid,
        in_specs=[cur(0), prev(0), cur(0), prev(C_AV), cur(C_AV), cur(0), cur(0), cur(512)],
        out_specs=[cur(0), prev(0), prev(0), own, own],
        out_shape=[jax.ShapeDtypeStruct((s, 512), F32)] * 3 + [jax.ShapeDtypeStruct((rows_per_step, 512), F32)] * 2,
        compiler_params=_params(*sems),
    )(qa, ka, ka, proj, proj, y_att, lse, dmixed)
    return dq, dk.at[s - rows_per_step:].set(dk_last), dv.at[s - rows_per_step:].set(dv_last)


def _attn_post(parts, proj, qg, kg, *, name):
    s = proj.shape[0]
    tm = ROW_TILE
    nblk = s // tm

    def body(*refs):
        ins, (q_ref, k_ref, qg_ref, kg_ref, dq_out, dk_out, dv_out, sums_ref) = refs[:9], refs[9:]
        i = pl.program_id(0)

        @pl.when(i == 0)
        def _():
            sums_ref[...] = jnp.zeros_like(sums_ref)

        dq = (ins[0][...] + ins[3][...]) + ins[6][...]
        dk = (ins[1][...] + ins[4][...]) + ins[7][...]
        dv = (ins[2][...] + ins[5][...]) + ins[8][...]
        dv_out[...] = dv.astype(BF16)
        for row, (x_ref, g_ref, dy, out, post) in enumerate(((q_ref, qg_ref, dq, dq_out, 0.125), (k_ref, kg_ref, dk, dk_out, 1.0))):
            x = x_ref[...]
            rs = lax.rsqrt(_head_sums(x * x) * (1.0 / 64) + EPS)
            xn = x * rs
            dy = dy * post
            sums_ref[row] += _fold8(dy * xn)
            dn = dy * g_ref[...]
            out[...] = (rs * (dn - xn * (_head_sums(dn * xn) * (1.0 / 64)))).astype(BF16)

        @pl.when(i == nblk - 1)
        def _():
            _spread_total(sums_ref)

    here = pl.BlockSpec((tm, 512), lambda i: (i, 0))
    col = lambda at: pl.BlockSpec((tm, 512), lambda i: (i, at // 512))
    vec = pl.BlockSpec((1, 512), lambda i: (0, 0))
    return pl.pallas_call(
        body, name=name, grid=(nblk,), in_specs=[here] * 9 + [col(C_AQ), col(C_AK), vec, vec],
        out_specs=[here, here, here, pl.BlockSpec((2, 8, 512), lambda i: (0, 0, 0))],
        out_shape=[jax.ShapeDtypeStruct((s, 512), BF16)] * 3 + [jax.ShapeDtypeStruct((2, 8, 512), F32)],
        compiler_params=_params("arbitrary"))(*[t for part in parts for t in part], proj, proj, qg, kg)


FFN_TM, FFN_TN = 256, 1408
HALO = 16


def _conv3(u_ref, halo_ref, w_ref, b_ref, first):
    u = u_ref[...].astype(F32)
    ext = jnp.concatenate([jnp.where(first, 0.0, halo_ref[...].astype(F32)), u], axis=0)
    u1 = pltpu.roll(ext, 1, 0)[HALO:]
    u2 = pltpu.roll(ext, 2, 0)[HALO:]
    return b_ref[...] + w_ref[0:1, :] * u2 + w_ref[1:2, :] * u1 + w_ref[2:3, :] * u


def _ffn_specs(tm, tn):
    nj = D_FF // tn
    blk = lambda half: pl.BlockSpec((tm, tn), lambda j, i: (i, j + half * nj))
    halo = lambda half: pl.BlockSpec((HALO, tn), lambda j, i: (jnp.maximum(i * (tm // HALO) - 1, 0), j + half * nj))
    wspec = lambda half: pl.BlockSpec((3, tn), lambda j, i: (0, j + half * nj))
    bspec = lambda half: pl.BlockSpec((1, tn), lambda j, i: (0, j + half * nj))
    return [blk(0), halo(0), blk(1), halo(1), wspec(0), wspec(1), bspec(0), bspec(1)]


def _conv_swiglu_fwd(u, conv_w, conv_b, *, name):
    s = u.shape[0]
    tm, tn = FFN_TM, FFN_TN

    def body(ug_ref, hg_ref, uv_ref, hv_ref, wg_ref, wv_ref, bg_ref, bv_ref, act_ref, uc_ref):
        first = pl.program_id(1) == 0
        cg = _conv3(ug_ref, hg_ref, wg_ref, bg_ref, first)
        cv = _conv3(uv_ref, hv_ref, wv_ref, bv_ref, first)
        act_ref[...] = (cg * _sigmoid(cg) * cv).astype(BF16)
        uc_ref[0] = cg.astype(BF16)
        uc_ref[1] = cv.astype(BF16)

    return pl.pallas_call(
        body, name=name, grid=(D_FF // tn, s // tm), in_specs=_ffn_specs(tm, tn),
        out_specs=[pl.BlockSpec((tm, tn), lambda j, i: (i, j)), pl.BlockSpec((2, tm, tn), lambda j, i: (0, i, j))],
        out_shape=[jax.ShapeDtypeStruct((s, D_FF), BF16), jax.ShapeDtypeStruct((2, s, D_FF), BF16)],
        compiler_params=_params("parallel", "parallel"))(u, u, u, u, conv_w, conv_w, conv_b, conv_b)


def _swiglu_bwd(uc, dact, *, name):
    _, s, _ = uc.shape
    tm, tn = FFN_TM, FFN_TN

    def body(uc_ref, da_ref, duc_ref, sums_ref):
        i = pl.program_id(1)

        @pl.when(i == 0)
        def _():
            sums_ref[...] = jnp.zeros_like(sums_ref)

        cg, cv, da = uc_ref[0].astype(F32), uc_ref[1].astype(F32), da_ref[...].astype(F32)
        sg = _sigmoid(cg)
        dg = da * cv * (sg * (1.0 + cg * (1.0 - sg)))
        dv = da * (cg * sg)
        duc_ref[0] = dg.astype(BF16)
        duc_ref[1] = dv.astype(BF16)
        sums_ref[0] += _fold8(dg)
        sums_ref[1] += _fold8(dv)

        @pl.when(i == s // tm - 1)
        def _():
            _spread_total(sums_ref)

    pair = pl.BlockSpec((2, tm, tn), lambda j, i: (0, i, j))
    return pl.pallas_call(
        body, name=name, grid=(D_FF // tn, s // tm), in_specs=[pair, pl.BlockSpec((tm, tn), lambda j, i: (i, j))],
        out_specs=[pair, pl.BlockSpec((2, 8, tn), lambda j, i: (0, 0, j))],
        out_shape=[jax.ShapeDtypeStruct((2, s, D_FF), BF16), jax.ShapeDtypeStruct((2, 8, D_FF), F32)],
        compiler_params=_params("parallel", "arbitrary"))(uc, dact)


def _conv_bwd(duc, u, conv_w, *, name):
    _, s, _ = duc.shape
    tm, tn = FFN_TM, FFN_TN
    nj, ni = D_FF // tn, s // tm

    def body(d_ref, halo_ref, u_ref, w_ref, du_ref, sums_ref):
        i = pl.program_id(2)

        @pl.when(i == 0)
        def _():
            sums_ref[...] = jnp.zeros_like(sums_ref)

        d = d_ref[0].astype(F32)
        ext = jnp.concatenate([d, jnp.where(i == ni - 1, 0.0, halo_ref[0].astype(F32))], axis=0)
        n = tm + HALO
        d1 = pltpu.roll(ext, n - 1, 0)[:tm]
        d2 = pltpu.roll(ext, n - 2, 0)[:tm]
        du_ref[...] = (w_ref[2:3, :] * d + w_ref[1:2, :] * d1 + w_ref[0:1, :] * d2).astype(BF16)
        uv = u_ref[...].astype(F32)
        for t, shifted in enumerate((d2, d1, d)):
            sums_ref[0, t] += _fold8(shifted * uv)

        @pl.when(i == ni - 1)
        def _():
            _spread_total(sums_ref)

    return pl.pallas_call(
        body, name=name, grid=(2, nj, ni),
        in_specs=[pl.BlockSpec((1, tm, tn), lambda g, j, i: (g, i, j)),
                  pl.BlockSpec((1, HALO, tn), lambda g, j, i: (g, jnp.minimum((i + 1) * (tm // HALO), s // HALO - 1), j)),
                  pl.BlockSpec((tm, tn), lambda g, j, i: (i, g * nj + j)),
                  pl.BlockSpec((3, tn), lambda g, j, i: (0, g * nj + j))],
        out_specs=[pl.BlockSpec((tm, tn), lambda g, j, i: (i, g * nj + j)),
                   pl.BlockSpec((1, 3, 8, tn), lambda g, j, i: (g, 0, 0, j))],
        out_shape=[jax.ShapeDtypeStruct((s, 2 * D_FF), BF16), jax.ShapeDtypeStruct((2, 3, 8, D_FF), F32)],
        compiler_params=_params("parallel", "parallel", "arbitrary"))(duc, duc, u, conv_w)


def _loss_head(x1, ffn, gate, target, *, name):
    s, d = x1.shape
    tm = ROW_TILE

    def body(x_ref, f_ref, g_ref, t_ref, dy_ref, df_ref, sums_ref):
        i = pl.program_id(0)

        @pl.when(i == 0)
        def _():
            sums_ref[...] = jnp.zeros_like(sums_ref)

        f = f_ref[...]
        err = x_ref[...] + g_ref[...] * f - t_ref[...]
        dy = err * (1.0 / d)
        dy_ref[...] = dy
        df_ref[...] = (g_ref[...] * dy).astype(BF16)
        sums_ref[0] += _fold8(dy * f)
        sums_ref[1] += _fold8(err * err)

        @pl.when(i == s // tm - 1)
        def _():
            _spread_total(sums_ref)

    row = pl.BlockSpec((tm, d), lambda i: (i, 0))
    return pl.pallas_call(
        body, name=name, grid=(s // tm,), in_specs=[row, row, pl.BlockSpec((1, d), lambda i: (0, 0)), row],
        out_specs=[row, row, pl.BlockSpec((2, 8, d), lambda i: (0, 0, 0))],
        out_shape=[jax.ShapeDtypeStruct((s, d), F32), jax.ShapeDtypeStruct((s, d), BF16), jax.ShapeDtypeStruct((2, 8, d), F32)],
        compiler_params=_params("arbitrary"))(x1, ffn, gate, target)


def _adamw(w, g, m, v, *, name):
    rows, cols = w.shape
    split = isinstance(g, tuple)
    if rows % 8 == 0 or rows <= ROW_TILE:
        span = rows // 2 if split else rows
        tm = next((t for t in range(ROW_TILE, 7, -8) if span % t == 0), span)
        shape, at, steps, per_half = (tm, cols), (lambda i: (i, 0)), rows // tm, span // tm
    else:
        shape, at, steps, per_half = (rows, ROW_TILE), (lambda i: (0, i)), cols // ROW_TILE, cols // ROW_TILE // 2

    def update(gv, w_ref, m_ref, v_ref, d_ref, mo_ref, vo_ref):
        mn = ADAM_B1 * m_ref[...] + (1.0 - ADAM_B1) * gv
        vn = ADAM_B2 * v_ref[...] + (1.0 - ADAM_B2) * (gv * gv)
        m_hat = mn / (1.0 - ADAM_B1 ** ADAM_STEP)
        v_hat = vn / (1.0 - ADAM_B2 ** ADAM_STEP)
        d_ref[...] = -ADAM_LR * (m_hat / (jnp.sqrt(v_hat) + ADAM_EPS) + ADAM_WD * w_ref[...])
        mo_ref[...] = mn
        vo_ref[...] = vn

    out_shape = [jax.ShapeDtypeStruct((rows, cols), F32)] * (4 if split else 3)
    if not split:
        def body(w_ref, g_ref, m_ref, v_ref, d_ref, mo_ref, vo_ref):
            update(g_ref[...], w_ref, m_ref, v_ref, d_ref, mo_ref, vo_ref)

        blk = pl.BlockSpec(shape, at)
        return pl.pallas_call(body, name=name, grid=(steps,), in_specs=[blk] * 4, out_specs=[blk] * 3, out_shape=out_shape,
                              compiler_params=_params("parallel"))(w, g, m, v)

    mine, other, core = g

    def body(core_ref, w_ref, mine_ref, other_ref, m_ref, v_ref, d_ref, mo_ref, vo_ref, g_ref):
        gv = jnp.where(pl.program_id(0) // per_half == core_ref[0], mine_ref[...], other_ref[...])
        g_ref[...] = gv
        update(gv, w_ref, m_ref, v_ref, d_ref, mo_ref, vo_ref)

    blk = pl.BlockSpec(shape, lambda i, core_ref: at(i))
    half = pl.BlockSpec(shape, lambda i, core_ref: at(i % per_half))
    return pl.pallas_call(
        body, name=name, out_shape=out_shape, compiler_params=_params("parallel"),
        grid_spec=pltpu.PrefetchScalarGridSpec(num_scalar_prefetch=1, grid=(steps,), in_specs=[blk, half, half, blk, blk],
                                               out_specs=[blk] * 4))(core, w, mine, other, m, v)


def _colsum(t):
    return t[..., 0, :]


def _in_proj_layout(w_in):
    pad = jnp.zeros((w_in.shape[0], PROJ_W - C_LR - GLA_GATE_RANK), w_in.dtype)
    return jnp.concatenate([w_in[:, :1536], w_in[:, 1552:], w_in[:, 1536:1552], pad], axis=1)


def _in_proj_grad_layout(g):
    return jnp.concatenate([g[:, :1536], g[:, C_LR:C_LR + GLA_GATE_RANK], g[:, 1536:C_LR]], axis=1)


def _gate_layout(gla_w_gate):
    return jnp.pad(gla_w_gate, ((0, HEAD_LANES - GLA_GATE_RANK), (0, 0))).astype(BF16)


def _local_step(x, target, mod, wi, wo, ffn_weights, ffn_grads_ready, attn_grads_ready, conv_w, conv_b, wg, bg, gn, qg, kg, n1g, n2g):
    d = D_MODEL
    sh1, sc1, g1, sh2, sc2, g2 = [mod[:, i * d:(i + 1) * d] for i in range(6)]
    qg8, kg8 = jnp.tile(qg, (1, 8)), jnp.tile(kg, (1, 8))

    _, h1, h1_t = _norm_mod_fwd(x, None, None, n1g, sc1, sh1, name="norm1_fwd")
    proj = _mm(h1, wi, tm=1024, tn=PROJ_W, tk=d, name="in_proj")
    o_raw, y_gla, states = _gla_fwd(proj, wg, bg, gn, name="gla_fwd")
    qa, ka = _attn_prep(proj, qg8, kg8, name="attn_prep")
    sparse = [_dil_attn_fwd(qa, ka, proj, dil, name=f"attn_fwd_d{dil}") for dil in DILATIONS[1:]]
    mixed, y_att, lse = _dense_attn_fwd_merge(qa, ka, proj, sparse, y_gla, name="attn_fwd_d1_merge")
    attn_out = _mm(mixed, wo, tm=1024, tn=d, tk=d, name="out_proj")
    x1, h2, h2_t = _norm_mod_fwd(x, attn_out, g1, n2g, sc2, sh2, name="norm2_fwd")
    wup, wdown = ffn_weights(h2)
    u = _mm(h2, wup, out_dtype=BF16, tm=1024, tn=D_FF, tk=d, name="up_proj")
    act, uc = _conv_swiglu_fwd(u, conv_w, conv_b, name="conv_swiglu_fwd")
    ffn = _mm(act, wdown, tm=1024, tn=d, tk=D_FF, name="down_proj")
    dy, dffn, head_sums = _loss_head(x1, ffn, g2, target, name="loss_head")

    dact = _mm(dffn, wdown, tb=True, out_dtype=BF16, tm=1024, tn=D_FF, tk=d, name="down_proj_dx")
    g_wdown, g_wdown_b = _mm(act, dffn, ta=True, tm=1408, tn=d, tk=2048, also_bf16=True, name="down_proj_dw")
    duc, bias_sums = _swiglu_bwd(uc, dact, name="swiglu_bwd")
    du, tap_sums = _conv_bwd(duc, u, conv_w, name="conv_bwd")
    dh2 = _mm(du, wup, tb=True, tm=1024, tn=d, tk=D_FF, name="up_proj_dx")
    g_wup, g_wup_b = _mm(h2_t, du, tm=d, tn=1408, tk=2048, shard_cols=True, also_bf16=True, name="up_proj_dw")
    token = ffn_grads_ready(g_wup_b, g_wdown_b)
    g1_late = g1 if token is None else g1 + token[0:1, 0:1]
    dx1, dao, n2_sums = _norm_mod_bwd(x1, dh2, dy, n2g, sc2, attn_out, g1_late, name="norm2_bwd")

    dmixed = _mm(dao, wo, tb=True, tm=1024, tn=d, tk=d, name="out_proj_dx")
    g_wo = _mm(mixed, dao, ta=True, tm=d, tn=d, tk=1024, name="out_proj_dw")
    dgq, dgk, dgv, dgr, dlr, g_wg, gla_sums = _gla_bwd(proj, wg, bg, gn, o_raw, states, dmixed, name="gla_bwd")
    parts = [_dil_attn_bwd(qa, ka, proj, y_att, lse, dmixed, dil, name=f"attn_bwd_d{dil}") for dil in DILATIONS]
    daq, dak, dav, qk_sums = _attn_post(parts, proj, qg8, kg8, name="attn_post")
    dproj = jnp.concatenate([dgq, dgk, dgv, dgr, daq, dak, dav, dlr], axis=1)
    g_wi = _mm(h1_t, dproj, tm=512, tn=PROJ_W, tk=2048, name="in_proj_dw")
    token = attn_grads_ready(g_wi, g_wo)
    wi_late = wi if token is None else wi + token[0:1, 0:1].astype(BF16)
    dh1 = _mm(dproj, wi_late, tb=True, tm=1024, tn=d, tk=PROJ_W, name="in_proj_dx")
    grad_x, _, n1_sums = _norm_mod_bwd(x, dh1, dx1, n1g, sc1, None, None, name="norm1_bwd")

    n1, n2, hs, taps, cb = _colsum(n1_sums), _colsum(n2_sums), _colsum(head_sums), _colsum(tap_sums), _colsum(bias_sums)
    gs, qs = _colsum(gla_sums), _colsum(qk_sums)
    dmod = jnp.concatenate([n1[1], n1[0] * n1g[0], n2[2], n2[1], n2[0] * n2g[0], hs[0]])
    small = dict(
        dmod=dmod,
        norm1_g=n1[0] * (1.0 + sc1[0]), norm2_g=n2[0] * (1.0 + sc2[0]),
        gla_w_gate=g_wg[:GLA_GATE_RANK], gla_b_gate=gs[0, :256], gla_norm_g=gs[1].reshape(4, 128).sum(axis=0),
        q_norm_g=qs[0].reshape(8, 64).sum(axis=0), k_norm_g=qs[1].reshape(8, 64).sum(axis=0),
        conv_w=jnp.concatenate([taps[0], taps[1]], axis=1), conv_b=jnp.concatenate([cb[0], cb[1]]),
    )
    return head_sums[1], grad_x, (g_wi, g_wo, g_wup, g_wdown), small


N_DEV, N_CHIP = 8, 4
ANY = pl.BlockSpec(memory_space=pl.ANY)
VMEM_SPEC = pl.BlockSpec(memory_space=pltpu.VMEM)


def _place():
    x, y, c = lax.axis_index("x"), lax.axis_index("y"), lax.axis_index("c")
    other_chips = [(1 - x, y), (x, 1 - y), (1 - x, 1 - y)]
    return x, y, c, (x, y, 1 - c), other_chips


def _all_gather_small(v, *, name):
    m, n = v.shape

    def body(v_ref, out_ref, send_sems, recv_sems, local_sem):
        x, y, c, sibling, chips = _place()
        me = (x, y, c)

        def rows(px, py, pc):
            return out_ref.at[pl.ds((4 * px + 2 * py + pc) * m, m), :]

        def copy(k, block, to, src=None):
            return pltpu.make_async_remote_copy(
                src_ref=rows(*block) if src is None else src, dst_ref=rows(*block), send_sem=send_sems.at[k],
                recv_sem=recv_sems.at[k], device_id=to, device_id_type=MESH)

        mine = pltpu.make_async_copy(v_ref, rows(*me), local_sem)
        mine.start()
        first = [copy(0, me, sibling, src=v_ref)]
        first += [copy(1 + j, me, (*chip, c), src=v_ref) for j, chip in enumerate(chips)]
        for cp in first:
            cp.start()
        passed = [copy(4 + j, (*chip, c), sibling) for j, chip in enumerate(chips)]
        for j, chip in enumerate(chips):
            copy(1 + j, (*chip, c), me).wait_recv()
            passed[j].start()
        copy(0, sibling, me).wait_recv()
        for j, chip in enumerate(chips):
            copy(4 + j, (*chip, 1 - c), me).wait_recv()
        for cp in first + passed:
            cp.wait_send()
        mine.wait()

    return pl.pallas_call(
        body, name=name, out_shape=jax.ShapeDtypeStruct((N_DEV * m, n), v.dtype), in_specs=[VMEM_SPEC], out_specs=VMEM_SPEC,
        scratch_shapes=[pltpu.SemaphoreType.DMA((7,)), pltpu.SemaphoreType.DMA((7,)), pltpu.SemaphoreType.DMA],
    )(v)


def _gather_weight_shards(shards, *, name):
    nw = len(shards)

    def body(*refs):
        srcs, outs, (send_sems, recv_sems) = refs[:nw], refs[nw:2 * nw], refs[2 * nw:]
        x, y, c, sibling, chips = _place()
        index = lambda chip: 2 * chip[0] + chip[1]

        def copy(w, k, src, dst, to):
            return pltpu.make_async_remote_copy(src_ref=src, dst_ref=dst, send_sem=send_sems.at[6 * w + k],
                                                recv_sem=recv_sems.at[6 * w + k], device_id=to, device_id_type=MESH)

        sent = []
        for w, (src_ref, out_ref) in enumerate(zip(srcs, outs)):
            for k, chip in enumerate(chips):
                sent.append(copy(w, k, src_ref.at[c], out_ref.at[2 * x + y, c], (*chip, c)))
                sent[-1].start()
        for w, out_ref in enumerate(outs):
            for k, chip in enumerate(chips):
                landed = out_ref.at[index(chip), c]
                copy(w, k, landed, landed, (*chip, c)).wait_recv()
                sent.append(copy(w, 3 + k, landed, landed, sibling))
                sent[-1].start()
        for w, out_ref in enumerate(outs):
            for k, chip in enumerate(chips):
                passed_on = out_ref.at[index(chip), 1 - c]
                copy(w, 3 + k, passed_on, passed_on, sibling).wait_recv()
        for cp in sent:
            cp.wait_send()

    return pl.pallas_call(
        body, name=name, out_shape=[jax.ShapeDtypeStruct((N_CHIP, *s.shape), s.dtype) for s in shards],
        in_specs=[ANY] * nw, out_specs=[ANY] * nw,
        scratch_shapes=[pltpu.SemaphoreType.DMA((6 * nw,)), pltpu.SemaphoreType.DMA((6 * nw,))],
    )(*shards)


HBM_SPEC = pl.BlockSpec(memory_space=pltpu.HBM)
SEM_SPEC = pl.BlockSpec(memory_space=pltpu.SEMAPHORE)
DATAFLOW_EFFECT = pltpu.SideEffectType.DATAFLOW_SIDE_EFFECTING


def _late_copies(srcs, lands, send_sems, recv_sems):
    x, y, c, _, chips = _place()
    return [pltpu.make_async_remote_copy(
        src_ref=src.at[c], dst_ref=land.at[2 * x + y, c], send_sem=send_sems.at[6 * w + 2 * r + core],
        recv_sem=recv_sems.at[6 * w + 2 * r + c], device_id=(*chip, core), device_id_type=MESH)
        for w, (src, land) in enumerate(zip(srcs, lands)) for r, chip in enumerate(chips) for core in range(2)]


def _gather_late_start(own, after, *, name):
    nw = len(own)

    def body(*refs):
        srcs, lands, send_sems, recv_sems, token = refs[:nw], refs[nw:2 * nw], refs[2 * nw + 1], refs[2 * nw + 2], refs[-1]
        for cp in _late_copies(srcs, lands, send_sems, recv_sems):
            cp.start()
        token[...] = jnp.zeros_like(token)

    lands = [pltpu.with_memory_space_constraint(lax.empty((N_CHIP, *s.shape), s.dtype), pltpu.HBM) for s in own]
    own = [pltpu.with_memory_space_constraint(s, pltpu.HBM) for s in own]
    out = pl.pallas_call(
        body, name=name,
        out_shape=(pltpu.SemaphoreType.DMA((6 * nw,)), pltpu.SemaphoreType.DMA((6 * nw,)),
                   *[pltpu.HBM(s.shape, s.dtype) for s in own], *[pltpu.HBM(s.shape, s.dtype) for s in lands],
                   jax.ShapeDtypeStruct((8, 128), F32)),
        in_specs=[HBM_SPEC] * (2 * nw) + [ANY], out_specs=(SEM_SPEC, SEM_SPEC, *[HBM_SPEC] * (2 * nw), VMEM_SPEC),
        input_output_aliases={i: 2 + i for i in range(2 * nw)},
        compiler_params=pltpu.CompilerParams(has_side_effects=DATAFLOW_EFFECT))(*own, *lands, after)
    return out[0], out[1], out[2:2 + nw], out[2 + nw:2 + 2 * nw], out[-1]


def _gather_late_wait(send_sems, recv_sems, own, lands, after, *, name):
    nw = len(own)

    def body(*refs):
        srcs, lands_in, send_sems, recv_sems = refs[:nw], refs[nw:2 * nw], refs[2 * nw], refs[2 * nw + 1]
        x, y, c, _, chips = _place()
        for cp in _late_copies(srcs, lands_in, send_sems, recv_sems):
            cp.wait_send()
        for w, (src, land) in enumerate(zip(srcs, lands_in)):
            for r, chip in enumerate(chips):
                for core in range(2):
                    pltpu.make_async_remote_copy(
                        src_ref=src.at[c], dst_ref=land.at[2 * chip[0] + chip[1], core], send_sem=send_sems.at[6 * w + 2 * r + core],
                        recv_sem=recv_sems.at[6 * w + 2 * r + core], device_id=(*chip, core), device_id_type=MESH).wait_recv()

    out = pl.pallas_call(
        body, name=name, out_shape=(*[pltpu.HBM(s.shape, s.dtype) for s in own], *[pltpu.HBM(s.shape, s.dtype) for s in lands]),
        in_specs=[HBM_SPEC] * (2 * nw) + [SEM_SPEC, SEM_SPEC, ANY], out_specs=tuple([HBM_SPEC] * (2 * nw)),
        input_output_aliases={i: i for i in range(2 * nw)},
        compiler_params=pltpu.CompilerParams(has_side_effects=DATAFLOW_EFFECT))(*own, *lands, send_sems, recv_sems, after)
    return out[:nw], out[nw:]


def _direct_reduce_copies(srcs, lands, send_sems, recv_sems):
    x, y, c, _, _ = _place()
    cps = []
    for w, (src, land) in enumerate(zip(srcs, lands)):
        for rel in range(1, N_DEV):
            tx, ty, tc = (1 - x if rel & 4 else x), (1 - y if rel & 2 else y), (1 - c if rel & 1 else c)
            cps.append(pltpu.make_async_remote_copy(
                src_ref=src.at[2 * tx + ty, tc], dst_ref=land.at[rel - 1], send_sem=send_sems.at[7 * w + rel - 1],
                recv_sem=recv_sems.at[7 * w + rel - 1], device_id=(tx, ty, tc), device_id_type=MESH))
    return cps


def _direct_reduce_start(grads, *, name):
    nw = len(grads)

    def body(*refs):
        srcs, lands, send_sems, recv_sems, token = refs[:nw], refs[nw:2 * nw], refs[2 * nw], refs[2 * nw + 1], refs[-1]
        for cp in _direct_reduce_copies(srcs, lands, send_sems, recv_sems):
            cp.start()
        token[...] = jnp.zeros_like(token)

    lands = [pltpu.with_memory_space_constraint(lax.empty((N_DEV - 1, *g.shape[2:]), g.dtype), pltpu.HBM) for g in grads]
    grads = [pltpu.with_memory_space_constraint(g, pltpu.HBM) for g in grads]
    out = pl.pallas_call(
        body, name=name,
        out_shape=(pltpu.SemaphoreType.DMA((7 * nw,)), pltpu.SemaphoreType.DMA((7 * nw,)),
                   *[pltpu.HBM(g.shape, g.dtype) for g in grads], *[pltpu.HBM(t.shape, t.dtype) for t in lands],
                   jax.ShapeDtypeStruct((8, 128), F32)),
        in_specs=[HBM_SPEC] * (2 * nw), out_specs=(SEM_SPEC, SEM_SPEC, *[HBM_SPEC] * (2 * nw), VMEM_SPEC),
        input_output_aliases={i: 2 + i for i in range(2 * nw)},
        compiler_params=pltpu.CompilerParams(has_side_effects=DATAFLOW_EFFECT))(*grads, *lands)
    return out[0], out[1], out[2:2 + nw], out[2 + nw:2 + 2 * nw], out[-1]


def _direct_reduce_wait(send_sems, recv_sems, grads, lands, after, *, name):
    nw = len(grads)

    def body(*refs):
        srcs, lands_in, send_sems, recv_sems = refs[:nw], refs[nw:2 * nw], refs[2 * nw], refs[2 * nw + 1]
        cps = _direct_reduce_copies(srcs, lands_in, send_sems, recv_sems)
        for cp in cps:
            cp.wait_send()
        for cp in cps:
            cp.wait_recv()

    out = pl.pallas_call(
        body, name=name, out_shape=(*[pltpu.HBM(g.shape, g.dtype) for g in grads], *[pltpu.HBM(t.shape, t.dtype) for t in lands]),
        in_specs=[HBM_SPEC] * (2 * nw) + [SEM_SPEC, SEM_SPEC, ANY], out_specs=tuple([HBM_SPEC] * (2 * nw)),
        input_output_aliases={i: i for i in range(2 * nw)},
        compiler_params=pltpu.CompilerParams(has_side_effects=DATAFLOW_EFFECT))(*grads, *lands, send_sems, recv_sems, after)
    return out[nw:]


def _direct_reduce_add(grad, landed, chip, core, *, name):
    _, r, n = grad.shape
    half = r // 2
    tr = _row_tile(half)
    nb = half // tr

    def body(chip_ref, core_ref, g_ref, t_ref, o_ref):
        acc = g_ref[0]
        for k in range(N_DEV - 1):
            acc = acc + t_ref[k].astype(F32)
        o_ref[...] = acc

    return pl.pallas_call(
        body, name=name,
        grid_spec=pltpu.PrefetchScalarGridSpec(
            num_scalar_prefetch=2, grid=(nb,),
            in_specs=[pl.BlockSpec((1, tr, n), lambda i, chip_ref, core_ref: (chip_ref[0], core_ref[0] * nb + i, 0)),
                      pl.BlockSpec((N_DEV - 1, tr, n), lambda i, chip_ref, core_ref: (0, i, 0))],
            out_specs=pl.BlockSpec((tr, n), lambda i, chip_ref, core_ref: (i, 0))),
        out_shape=jax.ShapeDtypeStruct((half, n), F32), compiler_params=_params("parallel"))(chip, core, grad, landed)


def _share_halves(halves, *, name):
    nw = len(halves)

    def body(*refs):
        srcs, outs, (send_sems, recv_sems) = refs[:nw], refs[nw:2 * nw], refs[2 * nw:]
        _, _, _, sibling, _ = _place()
        cps = [pltpu.make_async_remote_copy(src_ref=src_ref, dst_ref=out_ref, send_sem=send_sems.at[w], recv_sem=recv_sems.at[w],
                                            device_id=sibling, device_id_type=MESH)
               for w, (src_ref, out_ref) in enumerate(zip(srcs, outs))]
        for cp in cps:
            cp.start()
        for cp in cps:
            cp.wait()

    return pl.pallas_call(
        body, name=name, out_shape=[jax.ShapeDtypeStruct(h.shape, h.dtype) for h in halves],
        in_specs=[ANY] * nw, out_specs=[ANY] * nw,
        scratch_shapes=[pltpu.SemaphoreType.DMA((nw,)), pltpu.SemaphoreType.DMA((nw,))])(*halves)


def _row_tile(rows, limit=256):
    return next(t for t in range(limit, 15, -16) if rows % t == 0)


def _sum_devices(gathered, *, name):
    _, m, n = gathered.shape

    def body(g_ref, tot_ref, loss_ref):
        tot = g_ref[0]
        for dev in range(1, N_DEV):
            tot = tot + g_ref[dev]
        tot_ref[...] = tot
        loss_ref[...] = jnp.full((8, n), (0.5 / D_MODEL) * jnp.sum(tot[0:8]), F32)

    return pl.pallas_call(body, name=name, in_specs=[VMEM_SPEC], out_specs=[VMEM_SPEC, VMEM_SPEC],
                          out_shape=[jax.ShapeDtypeStruct((m, n), F32), jax.ShapeDtypeStruct((8, n), F32)])(gathered)


def _ada_mod(cond_all, w_ada_shard, *, name):
    tn = 512

    def body(a_ref, b_ref, o_ref):
        o_ref[...] = _nn(a_ref[...], b_ref[...], precision=HIGHEST)

    return pl.pallas_call(
        body, name=name, grid=(w_ada_shard.shape[1] // tn,),
        in_specs=[pl.BlockSpec(cond_all.shape, lambda j: (0, 0)), pl.BlockSpec((D_MODEL, tn), lambda j: (0, j))],
        out_specs=pl.BlockSpec((N_DEV, tn), lambda j: (0, j)),
        out_shape=jax.ShapeDtypeStruct((N_DEV, w_ada_shard.shape[1]), F32), compiler_params=_params("parallel"))(cond_all, w_ada_shard)


def _ada_grad(cond_all, dmod_cols, *, name):
    tm = 256

    def body(a_ref, b_ref, o_ref):
        o_ref[...] = lax.dot_general(a_ref[...], b_ref[...], (((0,), (0,)), ((), ())), precision=HIGHEST,
                                     preferred_element_type=F32)

    return pl.pallas_call(
        body, name=name, grid=(D_MODEL // tm,),
        in_specs=[pl.BlockSpec((N_DEV, tm), lambda i: (0, i)), pl.BlockSpec(dmod_cols.shape, lambda i: (0, 0))],
        out_specs=pl.BlockSpec((tm, dmod_cols.shape[1]), lambda i: (i, 0)),
        out_shape=jax.ShapeDtypeStruct((D_MODEL, dmod_cols.shape[1]), F32), compiler_params=_params("parallel"))(cond_all, dmod_cols)


def _silu_rows(c8, *, name):
    def body(c_ref, o_ref):
        cv = c_ref[...]
        o_ref[...] = cv * _sigmoid(cv)

    return pl.pallas_call(body, name=name, in_specs=[VMEM_SPEC], out_specs=VMEM_SPEC,
                          out_shape=jax.ShapeDtypeStruct(c8.shape, F32))(c8)


def _rows128(t, rows=None):
    flat = t.reshape(-1, 128)
    return flat if rows is None else jnp.pad(flat, ((0, rows - flat.shape[0]), (0, 0)))


def _from_col_shards(shards, r, n):
    return shards.reshape(N_CHIP, r, n).transpose(1, 0, 2).reshape(r, N_CHIP * n)


def kernel(x, c, w_ada, b_ada, norm1_g, w_in, gla_w_gate, gla_b_gate, gla_norm_g, q_norm_g, k_norm_g, w_out, norm2_g, w_up, conv_w, conv_b, w_down, loss_target, m_w_ada, m_b_ada, m_norm1_g, m_w_in, m_gla_w_gate, m_gla_b_gate, m_gla_norm_g, m_q_norm_g, m_k_norm_g, m_w_out, m_norm2_g, m_w_up, m_conv_w, m_conv_b, m_w_down, v_w_ada, v_b_ada, v_norm1_g, v_w_in, v_gla_w_gate, v_gla_b_gate, v_gla_norm_g, v_q_norm_g, v_k_norm_g, v_w_out, v_norm2_g, v_w_up, v_conv_w, v_conv_b, v_w_down):
    d = D_MODEL
    ax, ay, ac = lax.axis_index("x"), lax.axis_index("y"), lax.axis_index("c")
    chip, dev = 2 * ax + ay, 4 * ax + 2 * ay + ac

    cond = _silu_rows(jnp.broadcast_to(c, (8, d)), name="cond_silu")[0:1]
    small_in = jnp.concatenate([_rows128(cond), _rows128(conv_w[0]), _rows128(gla_w_gate[0])], axis=0)
    small_in = _rows128(small_in, 56)
    got = _all_gather_small(small_in, name="gather_small").reshape(N_DEV, 56, 128)
    cond_all = got[:, 0:8].reshape(N_DEV, d)
    conv_w_full = _from_col_shards(got[0::2, 8:41].reshape(N_CHIP, 3 * 1408 // 128, 128), 3, 1408)
    gate_full = _from_col_shards(got[0::2, 41:49].reshape(N_CHIP, 16 * 64 // 128, 128), GLA_GATE_RANK, 64)
    mod_part = _ada_mod(cond_all, w_ada[0], name="ada_mod")
    mod_got = _all_gather_small(_rows128(mod_part), name="gather_mod").reshape(N_DEV, N_DEV, 1536)
    mod_all = mod_got[0::2].transpose(1, 0, 2).reshape(N_DEV, 6 * d) + b_ada
    mod = lax.dynamic_slice_in_dim(mod_all, dev, 1, axis=0)

    own = [w[0].astype(BF16).reshape(2, w.shape[1] // 2, w.shape[2]) for w in (w_in, w_out, w_up, w_down)]
    with_own = lambda got, mine: [lax.dynamic_update_index_in_dim(t, o, chip, 0) for t, o in zip(got, mine)]
    got_in, got_out = with_own(_gather_weight_shards(own[:2], name="gather_weights"), own[:2])
    w_in_full = got_in.reshape(N_CHIP, d, 772).transpose(1, 0, 2).reshape(d, N_CHIP * 772)
    w_out_full = got_out.reshape(d, d)
    exchanged = mod_all[0:1, 0:1] + got_in[0, 0, 0:1, 0:1].astype(F32)
    send_sems, recv_sems, own_thru, lands, token = _gather_late_start(own[2:], exchanged, name="gather_late_start")
    mod = mod + token[0:1, 0:1]

    def ffn_weights(after):
        mine, landed = _gather_late_wait(send_sems, recv_sems, own_thru, lands, after, name="gather_late_wait")
        got_up, got_down = with_own(landed, mine)
        return got_up.reshape(N_CHIP, d, 1408).transpose(1, 0, 2).reshape(d, 2 * D_FF), got_down.reshape(D_FF, d)

    ffn_reduce, attn_reduce, attn_parts = [], [], []
    halves_of = lambda g: g.reshape(N_CHIP, 2, g.shape[-2] // 2, g.shape[-1])

    def ffn_grads_ready(g_wup_b, g_wdown_b):
        ffn_reduce.extend(_direct_reduce_start([halves_of(g_wup_b), halves_of(g_wdown_b.reshape(N_CHIP, D_FF // N_CHIP, d))],
                                               name="reduce_ffn_start"))
        return ffn_reduce[4]

    def attn_grads_ready(g_wi, g_wo):
        attn_parts.extend([_in_proj_grad_layout(g_wi).reshape(d, N_CHIP, 772).transpose(1, 0, 2), g_wo.reshape(N_CHIP, d // N_CHIP, d)])
        attn_reduce.extend(_direct_reduce_start([halves_of(g.astype(BF16)) for g in attn_parts], name="reduce_attn_start"))
        return attn_reduce[4]

    err2, grad_x, (g_wi, g_wo, g_wup, g_wdown), small = _local_step(
        x[0], loss_target[0], mod, _in_proj_layout(w_in_full), w_out_full, ffn_weights, ffn_grads_ready, attn_grads_ready,
        conv_w_full, conv_b,
        _gate_layout(gate_full), gla_b_gate, gla_norm_g, q_norm_g, k_norm_g, norm1_g, norm2_g)

    pieces = [err2[0], small["dmod"], small["norm1_g"], small["norm2_g"], small["gla_w_gate"].reshape(-1), small["gla_b_gate"],
              small["gla_norm_g"], small["q_norm_g"], small["k_norm_g"], small["conv_w"].reshape(-1), small["conv_b"]]
    sizes = [p.shape[0] for p in pieces]
    at = [sum(sizes[:i]) for i in range(len(sizes) + 1)]
    vec = _rows128(jnp.concatenate(pieces), 288)
    got = _all_gather_small(vec, name="gather_grads").reshape(N_DEV, 288, 128)
    total, loss8 = _sum_devices(got, name="sum_devices")
    total = total.reshape(-1)
    seg = lambda i: total[at[i]:at[i + 1]]
    dmod_all = got.reshape(N_DEV, -1)[:, at[1]:at[2]]
    g_small = dict(
        b_ada=seg(1)[None], norm1_g=seg(2)[None], norm2_g=seg(3)[None],
        gla_w_gate=lax.dynamic_slice_in_dim(seg(4).reshape(GLA_GATE_RANK, 256), chip * 64, 64, axis=1),
        gla_b_gate=seg(5)[None], gla_norm_g=seg(6)[None], q_norm_g=seg(7)[None], k_norm_g=seg(8)[None],
        conv_w=lax.dynamic_slice_in_dim(seg(9).reshape(3, 2 * D_FF), chip * 1408, 1408, axis=1), conv_b=seg(10)[None])
    dmod_cols = lax.dynamic_slice_in_dim(dmod_all.reshape(N_DEV, 6 * d), chip * 1536, 1536, axis=1)
    g_w_ada = _ada_grad(cond_all, dmod_cols, name="ada_grad")

    core_id, chip_id = jnp.reshape(ac, (1,)).astype(jnp.int32), jnp.reshape(chip, (1,)).astype(jnp.int32)
    landed = (_direct_reduce_wait(*attn_reduce[:4], grad_x, name="reduce_attn_wait")
              + _direct_reduce_wait(*ffn_reduce[:4], grad_x, name="reduce_ffn_wait"))
    own = attn_parts + [g_wup, g_wdown.reshape(N_CHIP, D_FF // N_CHIP, d)]
    summed = [_direct_reduce_add(g, t, chip_id, core_id, name=f"reduce_add_{tag}")
              for g, t, tag in zip(own, landed, ("w_in", "w_out", "w_up", "w_down"))]
    others = _share_halves(summed, name="share_pair")

    grads = dict(w_ada=g_w_ada, **g_small, **dict(zip(("w_in", "w_out", "w_up", "w_down"), zip(summed, others))))
    names = ["w_ada", "b_ada", "norm1_g", "w_in", "gla_w_gate", "gla_b_gate", "gla_norm_g", "q_norm_g", "k_norm_g", "w_out",
             "norm2_g", "w_up", "conv_w", "conv_b", "w_down"]
    ws = dict(w_ada=w_ada, b_ada=b_ada, norm1_g=norm1_g, w_in=w_in, gla_w_gate=gla_w_gate, gla_b_gate=gla_b_gate,
              gla_norm_g=gla_norm_g, q_norm_g=q_norm_g, k_norm_g=k_norm_g, w_out=w_out, norm2_g=norm2_g, w_up=w_up,
              conv_w=conv_w, conv_b=conv_b, w_down=w_down)
    ms = dict(w_ada=m_w_ada, b_ada=m_b_ada, norm1_g=m_norm1_g, w_in=m_w_in, gla_w_gate=m_gla_w_gate, gla_b_gate=m_gla_b_gate,
              gla_norm_g=m_gla_norm_g, q_norm_g=m_q_norm_g, k_norm_g=m_k_norm_g, w_out=m_w_out, norm2_g=m_norm2_g, w_up=m_w_up,
              conv_w=m_conv_w, conv_b=m_conv_b, w_down=m_w_down)
    vs = dict(w_ada=v_w_ada, b_ada=v_b_ada, norm1_g=v_norm1_g, w_in=v_w_in, gla_w_gate=v_gla_w_gate, gla_b_gate=v_gla_b_gate,
              gla_norm_g=v_gla_norm_g, q_norm_g=v_q_norm_g, k_norm_g=v_k_norm_g, w_out=v_w_out, norm2_g=v_norm2_g, w_up=v_w_up,
              conv_w=v_conv_w, conv_b=v_conv_b, w_down=v_w_down)
    g_out, d_out, m_out, v_out = [], [], [], []
    for nm in names:
        shape = ws[nm].shape
        flip = (lambda t: t.T) if shape[-1] % 128 and shape[-2] % 128 == 0 else (lambda t: t)
        w2 = flip(ws[nm].reshape(shape[-2:]))
        if isinstance(grads[nm], tuple):
            mine, other = grads[nm]
            dl, mn, vn, g2 = _adamw(w2, (flip(mine), flip(other), core_id), flip(ms[nm].reshape(shape[-2:])),
                                    flip(vs[nm].reshape(shape[-2:])), name=f"adamw_{nm}")
        else:
            g2 = flip(grads[nm].reshape(shape[-2:]))
            dl, mn, vn = _adamw(w2, g2, flip(ms[nm].reshape(shape[-2:])), flip(vs[nm].reshape(shape[-2:])), name=f"adamw_{nm}")
        for outs, t in ((g_out, g2), (d_out, dl), (m_out, mn), (v_out, vn)):
            outs.append(flip(t).reshape(shape))
    return (loss8[0, 0], grad_x[None], *g_out, *d_out, *m_out, *v_out)
```

```python
import functools

import jax
import jax.numpy as jnp
from jax import lax
from jax.experimental import pallas as pl
from jax.experimental.pallas import tpu as pltpu

F32, BF16 = jnp.float32, jnp.bfloat16
HIGHEST = lax.Precision.HIGHEST
MESH = pl.DeviceIdType.MESH

D_MODEL = 1024
GLA_CHUNK = 64
GLA_GATE_TAU = 16.0
GLA_GATE_RANK = 16
HEAD_LANES = 128
ATTN_BLOCK = 128
DILATIONS = (1, 4, 16)
ALIBI_SLOPES = tuple(2.0 ** (-(h + 1)) for h in range(8))
D_FF = 2816
EPS = 1e-6
C_GQ, C_GK, C_GV, C_GR, C_AQ, C_AK, C_AV, C_LR, PROJ_W = 0, 256, 512, 1024, 1536, 2048, 2560, 3072, 3200
ADAM_LR, ADAM_B1, ADAM_B2, ADAM_EPS, ADAM_WD, ADAM_STEP = 0.001, 0.9, 0.999, 1e-08, 0.01, 10
VMEM_LIMIT_BYTES = 56 * 1024 * 1024
ROW_TILE = 512
ADAM_TILE = 256


def _params(*sem):
    return pltpu.CompilerParams(dimension_semantics=sem or None, vmem_limit_bytes=VMEM_LIMIT_BYTES)


def _nt(a, b):
    return lax.dot_general(a, b, (((1,), (1,)), ((), ())), preferred_element_type=F32)


def _tn(a, b):
    return lax.dot_general(a, b, (((0,), (0,)), ((), ())), preferred_element_type=F32)


def _nn(a, b, precision=None):
    return jnp.dot(a, b, preferred_element_type=F32, precision=precision)


def _split3(v):
    hi = v.astype(BF16)
    rest = v - hi.astype(F32)
    mid = rest.astype(BF16)
    return hi, mid, (rest - mid.astype(F32)).astype(BF16)


def _sum_right(v, ones):
    hi, mid, lo = _split3(v)
    return (_nn(lo, ones) + _nn(mid, ones)) + _nn(hi, ones)


def _sum_left(ones, v):
    hi, mid, lo = _split3(v)
    return (_nn(ones, lo) + _nn(ones, mid)) + _nn(ones, hi)


def _fold8(v):
    return v.reshape(v.shape[0] // 8, 8, v.shape[1]).sum(axis=0)


def _spread_total(ref):
    t = ref[...]
    ref[...] = jnp.broadcast_to(jnp.sum(t, axis=-2, keepdims=True), t.shape)


def _sigmoid(x):
    return 1.0 / (1.0 + jnp.exp(-x))


def _mm(a, b, *, ta=False, tb=False, out_dtype=F32, tm, tn, tk, shard_cols=False, also_bf16=False, name):
    (k_a, m) = a.shape if ta else a.shape[::-1]
    (k_b, n) = b.shape[::-1] if tb else b.shape
    assert k_a == k_b and m % tm == 0 and n % tn == 0 and k_a % tk == 0, (name, a.shape, b.shape)
    nk = k_a // tk
    assert nk == 1 or out_dtype == F32, name
    dims = (((0 if ta else 1,), (1 if tb else 0,)), ((), ()))

    def body(a_ref, b_ref, o_ref, *rounded):
        k = pl.program_id(2)
        part = lax.dot_general(a_ref[...].astype(BF16), b_ref[...].astype(BF16), dims, preferred_element_type=F32)
        if nk == 1:
            o_ref[...] = part.astype(out_dtype)
        else:
            @pl.when(k == 0)
            def _():
                o_ref[...] = part

            @pl.when(k > 0)
            def _():
                o_ref[...] += part

        if also_bf16:
            @pl.when(k == nk - 1)
            def _():
                rounded[0][...] = o_ref[...].astype(BF16)

    a_spec = pl.BlockSpec((tk, tm), lambda i, j, k: (k, i)) if ta else pl.BlockSpec((tm, tk), lambda i, j, k: (i, k))
    b_spec = pl.BlockSpec((tn, tk), lambda i, j, k: (j, k)) if tb else pl.BlockSpec((tk, tn), lambda i, j, k: (k, j))
    if shard_cols:
        o_spec, o_shape = pl.BlockSpec((None, tm, tn), lambda i, j, k: (j, i, 0)), (n // tn, m, tn)
    else:
        o_spec, o_shape = pl.BlockSpec((tm, tn), lambda i, j, k: (i, j)), (m, n)
    shapes = [jax.ShapeDtypeStruct(o_shape, out_dtype)] + ([jax.ShapeDtypeStruct(o_shape, BF16)] if also_bf16 else [])
    out = pl.pallas_call(
        body, name=name, grid=(m // tm, n // tn, nk), in_specs=[a_spec, b_spec], out_specs=[o_spec] * len(shapes),
        out_shape=shapes, compiler_params=_params("parallel", "parallel", "arbitrary"))(a, b)
    return out if also_bf16 else out[0]


def _norm_mod_fwd(x, branch, gate, gain, scale, shift, *, name):
    s, d = x.shape
    tm = ROW_TILE
    has_branch = branch is not None

    def body(*refs):
        if has_branch:
            x_ref, br_ref, gate_ref, gain_ref, sc_ref, sh_ref, x1_ref, h_ref, ht_ref = refs
            xv = x_ref[...] + gate_ref[...] * br_ref[...]
            x1_ref[...] = xv
        else:
            x_ref, gain_ref, sc_ref, sh_ref, h_ref, ht_ref = refs
            xv = x_ref[...]
        r = lax.rsqrt(jnp.mean(xv * xv, axis=-1, keepdims=True) + EPS)
        h = (xv * r) * gain_ref[...] * (1.0 + sc_ref[...]) + sh_ref[...]
        h_ref[...] = h.astype(BF16)
        ht_ref[...] = h.T.astype(BF16)

    row = pl.BlockSpec((tm, d), lambda i: (i, 0))
    col = pl.BlockSpec((d, tm), lambda i: (0, i))
    vec = pl.BlockSpec((1, d), lambda i: (0, 0))
    h_shapes = [jax.ShapeDtypeStruct((s, d), BF16), jax.ShapeDtypeStruct((d, s), BF16)]
    if has_branch:
        return pl.pallas_call(
            body, name=name, grid=(s // tm,), in_specs=[row, row, vec, vec, vec, vec], out_specs=[row, row, col],
            out_shape=[jax.ShapeDtypeStruct((s, d), F32)] + h_shapes,
            compiler_params=_params("parallel"))(x, branch, gate, gain, scale, shift)
    h, ht = pl.pallas_call(
        body, name=name, grid=(s // tm,), in_specs=[row, vec, vec, vec], out_specs=[row, col],
        out_shape=h_shapes, compiler_params=_params("parallel"))(x, gain, scale, shift)
    return x, h, ht


def _norm_mod_bwd(x, dh, dres, gain, scale, branch, gate, *, name):
    s, d = x.shape
    tm = ROW_TILE
    has_branch = branch is not None

    def body(*refs):
        if has_branch:
            x_ref, dh_ref, dres_ref, gain_ref, sc_ref, br_ref, gate_ref, dx_ref, dbr_ref, sums_ref = refs
        else:
            x_ref, dh_ref, dres_ref, gain_ref, sc_ref, dx_ref, sums_ref = refs
        i = pl.program_id(0)

        @pl.when(i == 0)
        def _():
            sums_ref[...] = jnp.zeros_like(sums_ref)

        xv, dhv = x_ref[...], dh_ref[...]
        r = lax.rsqrt(jnp.mean(xv * xv, axis=-1, keepdims=True) + EPS)
        xn = xv * r
        dxn = dhv * (gain_ref[...] * (1.0 + sc_ref[...]))
        dx = dres_ref[...] + r * (dxn - xn * jnp.mean(dxn * xn, axis=-1, keepdims=True))
        dx_ref[...] = dx
        sums_ref[0] += _fold8(dhv * xn)
        sums_ref[1] += _fold8(dhv)
        if has_branch:
            dbr_ref[...] = (gate_ref[...] * dx).astype(BF16)
            sums_ref[2] += _fold8(dx * br_ref[...])

        @pl.when(i == s // tm - 1)
        def _():
            _spread_total(sums_ref)

    row = pl.BlockSpec((tm, d), lambda i: (i, 0))
    vec = pl.BlockSpec((1, d), lambda i: (0, 0))
    sums = pl.BlockSpec((3, 8, d), lambda i: (0, 0, 0))
    sums_shape = jax.ShapeDtypeStruct((3, 8, d), F32)
    if has_branch:
        return pl.pallas_call(
            body, name=name, grid=(s // tm,), in_specs=[row, row, row, vec, vec, row, vec], out_specs=[row, row, sums],
            out_shape=[jax.ShapeDtypeStruct((s, d), F32), jax.ShapeDtypeStruct((s, d), BF16), sums_shape],
            compiler_params=_params("arbitrary"))(x, dh, dres, gain, scale, branch, gate)
    dx, sm = pl.pallas_call(
        body, name=name, grid=(s // tm,), in_specs=[row, row, row, vec, vec], out_specs=[row, sums],
        out_shape=[jax.ShapeDtypeStruct((s, d), F32), sums_shape],
        compiler_params=_params("arbitrary"))(x, dh, dres, gain, scale)
    return dx, None, sm


GLA_ROWS = 256


def _gla_block_setup(lr_ref, wg_ref, bg_ref):
    t, c = GLA_ROWS, GLA_CHUNK
    ri = lax.broadcasted_iota(jnp.int32, (t, t), 0)
    ci = lax.broadcasted_iota(jnp.int32, (t, t), 1)
    same = (ri // c) == (ci // c)
    causal, upper = same & (ci <= ri), same & (ci >= ri)
    z = _nn(lr_ref[...].astype(BF16), wg_ref[...]) + bg_ref[...]
    g = (jnp.minimum(z, 0.0) - jnp.log(1.0 + jnp.exp(-jnp.abs(z)))) * (1.0 / GLA_GATE_TAU)
    hi, mid, lo = _split3(g)
    total = lambda ones: (_nn(ones, lo) + _nn(ones, mid)) + _nn(ones, hi)
    return z, total(causal.astype(BF16)), total(same.astype(BF16)), causal, upper


def _chunks(t):
    return [t[i * GLA_CHUNK:(i + 1) * GLA_CHUNK] for i in range(GLA_ROWS // GLA_CHUNK)]


def _gla_fwd(proj, wg, bg, gn, *, name):
    s = proj.shape[0]
    tb, c = GLA_ROWS, GLA_CHUNK
    cb = tb // c

    def body(q_ref, k_ref, v_ref, r_ref, lr_ref, wg_ref, bg_ref, gn_ref, o_ref, y_ref, st_ref, state):
        i = pl.program_id(0)

        @pl.when(i == 0)
        def _():
            state[...] = jnp.zeros_like(state)

        low = lax.broadcasted_iota(jnp.int32, (tb, HEAD_LANES), 1) < 64
        masks = (low, jnp.logical_not(low))
        _, b, b_end, causal, _ = _gla_block_setup(lr_ref, wg_ref, bg_ref)
        pairs = []
        for p in range(2):
            cols = pl.ds(p * HEAD_LANES, HEAD_LANES)
            bp, bep = (t[:, p * HEAD_LANES:(p + 1) * HEAD_LANES] for t in (b, b_end))
            k = k_ref[:, cols]
            q_in = q_ref[:, cols] * 0.125 * jnp.exp(bp)
            k_out = (k * jnp.exp(-bp)).astype(BF16)
            k_end = k * jnp.exp(bep - bp)
            qms = [jnp.where(m, q_in, 0.0).astype(BF16) for m in masks]
            kes = [jnp.where(m, k_end, 0.0).astype(BF16) for m in masks]
            vs = [v_ref[:, pl.ds((2 * p + e) * HEAD_LANES, HEAD_LANES)].astype(BF16) for e in range(2)]
            grow = [_tn(v0, k0) + _tn(v1, k1) for v0, k0, v1, k1 in zip(_chunks(vs[0]), _chunks(kes[0]), _chunks(vs[1]), _chunks(kes[1]))]
            pairs.append((bep, k_out, qms, vs, grow))
        entering = [[], []]
        for p, (bep, _, _, _, grow) in enumerate(pairs):
            st = state[p]
            for ch in range(cb):
                entering[p].append(st)
                st_ref[ch, p] = st
                st = st * jnp.exp(bep[ch * c:ch * c + 1, :]) + grow[ch]
            state[p] = st
        for p, (_, k_out, qms, vs, _) in enumerate(pairs):
            for e in range(2):
                hc = pl.ds((2 * p + e) * HEAD_LANES, HEAD_LANES)
                a = jnp.where(causal, _nt(qms[e], k_out), 0.0).astype(BF16)
                carried = jnp.concatenate([_nt(qc, sc.astype(BF16)) for qc, sc in zip(_chunks(qms[e]), entering[p])], axis=0)
                o = _nn(a, vs[e]) + carried
                o_ref[:, hc] = o
                rr = r_ref[:, hc]
                on = o * lax.rsqrt(jnp.mean(o * o, axis=-1, keepdims=True) + EPS)
                y_ref[:, hc] = (on * gn_ref[...] * (rr * _sigmoid(rr))).astype(BF16)

    def col(width, at):
        return pl.BlockSpec((tb, width), lambda i: (i, at // width))

    full = lambda shape: pl.BlockSpec(shape, lambda i: tuple(0 for _ in shape))
    return pl.pallas_call(
        body, name=name, grid=(s // tb,),
        in_specs=[col(256, C_GQ), col(256, C_GK), col(512, C_GV), col(512, C_GR), col(128, C_LR),
                  full((HEAD_LANES, 256)), full((1, 256)), full((1, HEAD_LANES))],
        out_specs=[pl.BlockSpec((tb, 512), lambda i: (i, 0)), pl.BlockSpec((tb, 512), lambda i: (i, 0)),
                   pl.BlockSpec((cb, 2, HEAD_LANES, HEAD_LANES), lambda i: (i, 0, 0, 0))],
        out_shape=[jax.ShapeDtypeStruct((s, 512), F32), jax.ShapeDtypeStruct((s, 512), BF16),
                   jax.ShapeDtypeStruct((s // c, 2, HEAD_LANES, HEAD_LANES), F32)],
        scratch_shapes=[pltpu.VMEM((2, HEAD_LANES, HEAD_LANES), F32)],
        compiler_params=_params("arbitrary"))(proj, proj, proj, proj, proj, wg, bg, gn)


def _gla_bwd(proj, wg, bg, gn, o_raw, states, dmixed, *, name):
    s = proj.shape[0]
    tb, c = GLA_ROWS, GLA_CHUNK
    cb = tb // c
    nblk, nch = s // tb, s // c

    def body(q_ref, k_ref, v_ref, r_ref, lr_ref, wg_ref, bg_ref, gn_ref, o_ref, st_ref, stn_ref, dy_ref,
             dq_ref, dk_ref, dv_ref, dr_ref, dlr_ref, gwg_ref, sums_ref, dstate):
        i = pl.program_id(0)

        @pl.when(i == 0)
        def _():
            dstate[...] = jnp.zeros_like(dstate)
            gwg_ref[...] = jnp.zeros_like(gwg_ref)
            sums_ref[...] = jnp.zeros_like(sums_ref)

        low = lax.broadcasted_iota(jnp.int32, (tb, HEAD_LANES), 1) < 64
        masks = (low, jnp.logical_not(low))
        z, b, b_end, causal, upper = _gla_block_setup(lr_ref, wg_ref, bg_ref)
        lr_b = lr_ref[...].astype(BF16)
        dlr = jnp.zeros((tb, HEAD_LANES), F32)
        per_chunk = lambda rows, mats, fn: jnp.concatenate([fn(r, m.astype(BF16)) for r, m in zip(_chunks(rows), mats)], axis=0)
        pairs = []
        for p in range(2):
            cols = pl.ds(p * HEAD_LANES, HEAD_LANES)
            sl = slice(p * HEAD_LANES, (p + 1) * HEAD_LANES)
            bp, bep = b[:, sl], b_end[:, sl]
            e_in, e_out, e_end = jnp.exp(bp), jnp.exp(-bp), jnp.exp(bep - bp)
            q = q_ref[:, cols] * 0.125
            k = k_ref[:, cols]
            q_in, k_out, k_end = q * e_in, k * e_out, k * e_end
            qms = [jnp.where(m, q_in, 0.0).astype(BF16) for m in masks]
            kms_out = [jnp.where(m, k_out, 0.0).astype(BF16) for m in masks]
            kms_end = [jnp.where(m, k_end, 0.0).astype(BF16) for m in masks]
            vs, dos = [], []
            for e in range(2):
                hc = pl.ds((2 * p + e) * HEAD_LANES, HEAD_LANES)
                o, rr, dy = o_ref[:, hc], r_ref[:, hc], dy_ref[:, hc]
                sg = _sigmoid(rr)
                rs = lax.rsqrt(jnp.mean(o * o, axis=-1, keepdims=True) + EPS)
                on = o * rs
                t = dy * (rr * sg)
                sums_ref[1, :, hc] += _fold8(t * on)
                dn = t * gn_ref[...]
                dos.append((rs * (dn - on * jnp.mean(dn * on, axis=-1, keepdims=True))).astype(BF16))
                dr_ref[:, hc] = (dy * on * gn_ref[...] * (sg * (1.0 + rr * (1.0 - sg)))).astype(BF16)
                vs.append(v_ref[:, hc].astype(BF16))
            grow = [_tn(d0, q0) + _tn(d1, q1) for d0, q0, d1, q1 in zip(_chunks(dos[0]), _chunks(qms[0]), _chunks(dos[1]), _chunks(qms[1]))]
            pairs.append((bep, e_in, e_out, e_end, q, k, qms, kms_out, kms_end, vs, dos, grow))
        chains = []
        for p in range(2):
            bep, grow = pairs[p][0], pairs[p][-1]
            entering = [st_ref[ch, p] for ch in range(cb)]
            dst, leaving_grad = dstate[p], [None] * cb
            for ch in reversed(range(cb)):
                leaving_grad[ch] = dst
                dst = dst * jnp.exp(bep[ch * c:ch * c + 1, :]) + grow[ch]
            dstate[p] = dst
            chains.append((entering, leaving_grad))
        for p in range(2):
            cols = pl.ds(p * HEAD_LANES, HEAD_LANES)
            sl = slice(p * HEAD_LANES, (p + 1) * HEAD_LANES)
            _, e_in, e_out, e_end, q, k, qms, kms_out, kms_end, vs, dos, _ = pairs[p]
            entering, leaving_grad = chains[p]
            leaving = entering[1:] + [stn_ref[0, p]]
            felt = jnp.concatenate([jnp.broadcast_to(jnp.sum(dg_st * st, axis=0, keepdims=True), (c, HEAD_LANES))
                                    for dg_st, st in zip(leaving_grad, leaving)], axis=0)
            dq_in = jnp.zeros((tb, HEAD_LANES), F32)
            dk_out = jnp.zeros((tb, HEAD_LANES), F32)
            dk_end = jnp.zeros((tb, HEAD_LANES), F32)
            for e in range(2):
                hc = pl.ds((2 * p + e) * HEAD_LANES, HEAD_LANES)
                a = jnp.where(causal, _nt(qms[e], kms_out[e]), 0.0).astype(BF16)
                da = jnp.where(causal, _nt(dos[e], vs[e]), 0.0).astype(BF16)
                dv_ref[:, hc] = (_tn(a, dos[e]) + per_chunk(kms_end[e], leaving_grad, _nt)).astype(BF16)
                dq_in = dq_in + jnp.where(masks[e], per_chunk(dos[e], entering, _nn) + _nn(da, kms_out[e]), 0.0)
                dk_out = dk_out + _tn(da, qms[e])
                dk_end = dk_end + jnp.where(masks[e], per_chunk(vs[e], leaving_grad, _nn), 0.0)
            dq = dq_in * e_in
            dk = dk_out * e_out + dk_end * e_end
            dq_ref[:, cols] = (dq * 0.125).astype(BF16)
            dk_ref[:, cols] = dk.astype(BF16)
            dg = _sum_left(upper.astype(BF16), q * dq - k * dk) + felt
            dz = dg * (1.0 / GLA_GATE_TAU) * _sigmoid(-z[:, sl])
            dz_b = dz.astype(BF16)
            sums_ref[0, :, cols] += _fold8(dz)
            dlr = dlr + _nt(dz_b, wg_ref[:, cols])
            gwg_ref[:, cols] += _tn(lr_b, dz_b)
        dlr_ref[...] = dlr.astype(BF16)

        @pl.when(i == nblk - 1)
        def _():
            _spread_total(sums_ref)

    rev = lambda i: nblk - 1 - i

    def col(width, at):
        return pl.BlockSpec((tb, width), lambda i: (rev(i), at // width))

    full = lambda shape: pl.BlockSpec(shape, lambda i: tuple(0 for _ in shape))
    out_col = lambda width: pl.BlockSpec((tb, width), lambda i: (rev(i), 0))
    return pl.pallas_call(
        body, name=name, grid=(nblk,),
        in_specs=[col(256, C_GQ), col(256, C_GK), col(512, C_GV), col(512, C_GR), col(128, C_LR),
                  full((HEAD_LANES, 256)), full((1, 256)), full((1, HEAD_LANES)),
                  pl.BlockSpec((tb, 512), lambda i: (rev(i), 0)),
                  pl.BlockSpec((cb, 2, HEAD_LANES, HEAD_LANES), lambda i: (rev(i), 0, 0, 0)),
                  pl.BlockSpec((1, 2, HEAD_LANES, HEAD_LANES), lambda i: (jnp.minimum((rev(i) + 1) * cb, nch - 1), 0, 0, 0)),
                  pl.BlockSpec((tb, 512), lambda i: (rev(i), 0))],
        out_specs=[out_col(256), out_col(256), out_col(512), out_col(512), out_col(128),
                   full((HEAD_LANES, 256)), full((2, 8, 512))],
        out_shape=[jax.ShapeDtypeStruct((s, 256), BF16), jax.ShapeDtypeStruct((s, 256), BF16),
                   jax.ShapeDtypeStruct((s, 512), BF16), jax.ShapeDtypeStruct((s, 512), BF16),
                   jax.ShapeDtypeStruct((s, 128), BF16), jax.ShapeDtypeStruct((HEAD_LANES, 256), F32),
                   jax.ShapeDtypeStruct((2, 8, 512), F32)],
        scratch_shapes=[pltpu.VMEM((2, HEAD_LANES, HEAD_LANES), F32)],
        compiler_params=_params("arbitrary"))(proj, proj, proj, proj, proj, wg, bg, gn, o_raw, states, states, dmixed)


def _head_sums(v):
    ri = lax.broadcasted_iota(jnp.int32, (HEAD_LANES, HEAD_LANES), 0) // 64
    ci = lax.broadcasted_iota(jnp.int32, (HEAD_LANES, HEAD_LANES), 1) // 64
    ones = (ri == ci).astype(BF16)
    return jnp.concatenate([_sum_right(v[:, p * HEAD_LANES:(p + 1) * HEAD_LANES], ones) for p in range(4)], axis=1)


def _attn_prep(proj, qg, kg, *, name):
    s = proj.shape[0]
    tm = ROW_TILE

    def body(q_ref, k_ref, qg_ref, kg_ref, qa_ref, ka_ref):
        q, k = q_ref[...], k_ref[...]
        qr = lax.rsqrt(_head_sums(q * q) * (1.0 / 64) + EPS)
        kr = lax.rsqrt(_head_sums(k * k) * (1.0 / 64) + EPS)
        qa_ref[...] = q * qr * qg_ref[...] * 0.125
        ka_ref[...] = k * kr * kg_ref[...]

    col = lambda at: pl.BlockSpec((tm, 512), lambda i: (i, at // 512))
    vec = pl.BlockSpec((1, 512), lambda i: (0, 0))
    out = pl.BlockSpec((tm, 512), lambda i: (i, 0))
    return pl.pallas_call(
        body, name=name, grid=(s // tm,), in_specs=[col(C_AQ), col(C_AK), vec, vec], out_specs=[out] * 2,
        out_shape=[jax.ShapeDtypeStruct((s, 512), F32)] * 2, compiler_params=_params("parallel"))(proj, proj, qg, kg)


FAR = 1e30
LOG2E, LN2 = 1.4426950408889634, 0.6931471805599453


def _attn_distance(first):
    blk = ATTN_BLOCK
    iq = lax.broadcasted_iota(jnp.int32, (2 * blk, 2 * blk), 0) & (blk - 1)
    ik = lax.broadcasted_iota(jnp.int32, (2 * blk, 2 * blk), 1)
    rel = iq + blk - ik
    valid = (rel >= 0) & (rel <= blk) & (jnp.logical_not(first) | (ik >= blk))
    return jnp.where(valid, rel.astype(F32), FAR)


def _stack_heads(t2):
    low = lax.broadcasted_iota(jnp.int32, t2.shape, 1) < 64
    return jnp.concatenate([jnp.where(low, t2, 0.0), jnp.where(low, 0.0, t2)], axis=0).astype(BF16)


def _unstack_heads(t):
    blk = ATTN_BLOCK
    low = lax.broadcasted_iota(jnp.int32, (blk, HEAD_LANES), 1) < 64
    return jnp.where(low, t[0:blk], t[blk:2 * blk])


def _attn_scores(qs, kcat, slopes, dil, dist):
    top = lax.broadcasted_iota(jnp.int32, (2 * ATTN_BLOCK, 1), 0) < ATTN_BLOCK
    return _nt(qs, kcat) - jnp.where(top, slopes[0] * (dil * LOG2E), slopes[1] * (dil * LOG2E)) * dist


def _pair_slopes(p):
    if isinstance(p, int):
        return ALIBI_SLOPES[2 * p], ALIBI_SLOPES[2 * p + 1]
    pick = lambda e: jnp.where(p == 0, ALIBI_SLOPES[e], jnp.where(p == 1, ALIBI_SLOPES[2 + e],
                               jnp.where(p == 2, ALIBI_SLOPES[4 + e], ALIBI_SLOPES[6 + e])))
    return pick(0), pick(1)


ATTN_GROUP = 4


def _each(fn, *lists):
    return [fn(*args) for args in zip(*lists)]


def _attn_group_fwd(q2s, kcats, vcats, slopes, dil, dist):
    qs = _each(lambda q2: _stack_heads(q2 * LOG2E), q2s)
    sc = _each(lambda q, k, sl: _attn_scores(q, k, sl, dil, dist), qs, kcats, slopes)
    m = _each(lambda s: jnp.max(s, axis=-1, keepdims=True), sc)
    pr = _each(lambda s, mx: jnp.exp2(s - mx), sc, m)
    den = _each(lambda p: jnp.sum(p, axis=-1, keepdims=True), pr)
    o = _each(lambda p, v, d: _nn(p.astype(BF16), v) / d, pr, vcats, den)
    lse = _each(lambda mx, d, t: jnp.broadcast_to(mx + jnp.log2(d), t.shape), m, den, o)
    return _each(lambda t, l: (_unstack_heads(t), _unstack_heads(l)), o, lse)


def _attn_group_bwd(q2s, kcats, vcats, do2s, y2s, lse2s, slopes, dil, dist):
    lane = lax.broadcasted_iota(jnp.int32, (ATTN_BLOCK, HEAD_LANES), 1)
    low = lane < 64
    per_head = lambda t, pick: jnp.concatenate([jnp.sum(jnp.where(pick(0), t, 0.0), axis=-1, keepdims=True),
                                                jnp.sum(jnp.where(pick(1), t, 0.0), axis=-1, keepdims=True)], axis=0)
    lse = _each(lambda l: per_head(l, lambda e: lane == 64 * e), lse2s)
    delta = _each(lambda d, y: per_head(d * y, lambda e: low if e == 0 else jnp.logical_not(low)), do2s, y2s)
    qs = _each(lambda q2: _stack_heads(q2 * LOG2E), q2s)
    dos = _each(_stack_heads, do2s)
    sc = _each(lambda q, k, sl: _attn_scores(q, k, sl, dil, dist), qs, kcats, slopes)
    pr = _each(lambda s, l: jnp.exp2(s - l), sc, lse)
    dp = _each(_nt, dos, vcats)
    ds = _each(lambda p, d, dl: (p * (d - dl)).astype(BF16), pr, dp, delta)
    dq = _each(lambda d, k: _unstack_heads(_nn(d, k)), ds, kcats)
    dk = _each(lambda d, q: _tn(d, q) * LN2, ds, qs)
    dv = _each(lambda p, d: _tn(p.astype(BF16), d), pr, dos)
    return list(zip(dq, dk, dv))


def _attn_specs(dil):
    rows = ATTN_BLOCK * dil
    if dil == 1:
        cur = lambda at: pl.BlockSpec((rows, 512), lambda n: (n, at // 512))
        prev = lambda at: pl.BlockSpec((rows, 512), lambda n: (jnp.maximum(n - 1, 0), at // 512))
    else:
        cur = lambda at: pl.BlockSpec((rows, HEAD_LANES), lambda n, p: (n, at // HEAD_LANES + p))
        prev = lambda at: pl.BlockSpec((rows, HEAD_LANES), lambda n, p: (jnp.maximum(n - 1, 0), at // HEAD_LANES + p))
    return cur, prev


def _attn_loop(dil, one_group, p):
    if dil == 1:
        one_group([(slice(None), pl.ds(p * HEAD_LANES, HEAD_LANES), p) for p in range(ATTN_GROUP)])
    else:
        group = min(dil, ATTN_GROUP)

        def step(g, carry):
            one_group([(pl.ds(g * group + j, ATTN_BLOCK, stride=dil), slice(None), p) for j in range(group)])
            return carry

        if dil == group:
            step(0, 0)
        else:
            lax.fori_loop(0, dil // group, step, 0)


def _dil_attn_fwd(qa, ka, proj, dil, *, name):
    s = qa.shape[0]

    def body(q_ref, kp_ref, kc_ref, vp_ref, vc_ref, o_ref, lse_ref):
        dist = _attn_distance(pl.program_id(0) == 0)
        pair = None if dil == 1 else pl.program_id(1)

        def one_group(items):
            both = lambda a, b: [jnp.concatenate([a[rows, cols], b[rows, cols]], axis=0).astype(BF16) for rows, cols, _ in items]
            outs = _attn_group_fwd([q_ref[rows, cols] for rows, cols, _ in items], both(kp_ref, kc_ref), both(vp_ref, vc_ref),
                                   [_pair_slopes(p) for _, _, p in items], dil, dist)
            for (rows, cols, _), (o2, lse2) in zip(items, outs):
                o_ref[rows, cols] = o2
                lse_ref[rows, cols] = lse2

        _attn_loop(dil, one_group, pair)

    cur, prev = _attn_specs(dil)
    grid = (s // ATTN_BLOCK,) if dil == 1 else (s // (ATTN_BLOCK * dil), 4)
    return pl.pallas_call(
        body, name=name, grid=grid, in_specs=[cur(0), prev(0), cur(0), prev(C_AV), cur(C_AV)], out_specs=[cur(0), cur(0)],
        out_shape=[jax.ShapeDtypeStruct((s, 512), F32)] * 2,
        compiler_params=_params(*["parallel"] * len(grid)))(qa, ka, ka, proj, proj)


def _dense_attn_fwd_merge(qa, ka, proj, others, y_gla, *, name):
    s = qa.shape[0]
    blk = ATTN_BLOCK

    def body(q_ref, kp_ref, kc_ref, vp_ref, vc_ref, oa_ref, la_ref, ob_ref, lb_ref, yg_ref, mixed_ref, y_ref, lse_ref):
        dist = _attn_distance(pl.program_id(0) == 0)
        mixed_ref[:, 0:512] = yg_ref[...]

        def one_group(items):
            both = lambda a, b: [jnp.concatenate([a[rows, cols], b[rows, cols]], axis=0).astype(BF16) for rows, cols, _ in items]
            outs = _attn_group_fwd([q_ref[rows, cols] for rows, cols, _ in items], both(kp_ref, kc_ref), both(vp_ref, vc_ref),
                                   [_pair_slopes(p) for _, _, p in items], 1, dist)
            for (_, cols, p), (o2, l2) in zip(items, outs):
                la, lb = la_ref[:, cols], lb_ref[:, cols]
                m = jnp.maximum(jnp.maximum(l2, la), lb)
                w0, wa, wb = jnp.exp2(l2 - m), jnp.exp2(la - m), jnp.exp2(lb - m)
                zs = w0 + wa + wb
                y = (w0 * o2 + wa * oa_ref[:, cols] + wb * ob_ref[:, cols]) / zs
                y_ref[:, cols] = y
                lse_ref[:, cols] = m + jnp.log2(zs)
                mixed_ref[:, pl.ds(512 + p * HEAD_LANES, HEAD_LANES)] = y.astype(BF16)

        _attn_loop(1, one_group, None)

    cur, prev = _attn_specs(1)
    here = pl.BlockSpec((blk, 512), lambda n: (n, 0))
    (oa, la), (ob, lb) = others
    return pl.pallas_call(
        body, name=name, grid=(s // blk,),
        in_specs=[cur(0), prev(0), cur(0), prev(C_AV), cur(C_AV)] + [here] * 5,
        out_specs=[pl.BlockSpec((blk, 1024), lambda n: (n, 0)), here, here],
        out_shape=[jax.ShapeDtypeStruct((s, 1024), BF16), jax.ShapeDtypeStruct((s, 512), F32),
                   jax.ShapeDtypeStruct((s, 512), F32)],
        compiler_params=_params("parallel"))(qa, ka, ka, proj, proj, oa, la, ob, lb, y_gla)


def _dil_attn_bwd(qa, ka, proj, y_att, lse, dmixed, dil, *, name):
    s = qa.shape[0]
    blk, rows_per_step = ATTN_BLOCK, ATTN_BLOCK * dil
    nb = s // rows_per_step
    step_axis = 0 if dil == 1 else 1

    def body(q_ref, kp_ref, kc_ref, vp_ref, vc_ref, y_ref, lse_ref, do_ref, dq_ref, dk_ref, dv_ref, dk_own, dv_own):
        n = pl.program_id(step_axis)
        pair = None if dil == 1 else pl.program_id(0)
        dist = _attn_distance(n == 0)

        @pl.when(n == 0)
        def _():
            dk_own[...] = jnp.zeros_like(dk_own)
            dv_own[...] = jnp.zeros_like(dv_own)

        def one_group(items):
            both = lambda a, b: [jnp.concatenate([a[rows, cols], b[rows, cols]], axis=0).astype(BF16) for rows, cols, _ in items]
            at = lambda ref: [ref[rows, cols] for rows, cols, _ in items]
            outs = _attn_group_bwd(at(q_ref), both(kp_ref, kc_ref), both(vp_ref, vc_ref), at(do_ref), at(y_ref), at(lse_ref),
                                   [_pair_slopes(p) for _, _, p in items], dil, dist)
            for (rows, cols, _), (dq, dk, dv) in zip(items, outs):
                dq_ref[rows, cols] = dq
                dk_ref[rows, cols] = dk_own[rows, cols] + dk[0:blk]
                dv_ref[rows, cols] = dv_own[rows, cols] + dv[0:blk]
                dk_own[rows, cols] = dk[blk:2 * blk]
                dv_own[rows, cols] = dv[blk:2 * blk]

        _attn_loop(dil, one_group, pair)

    width = 512 if dil == 1 else HEAD_LANES

    def spec(at, row_of):
        if dil == 1:
            return pl.BlockSpec((rows_per_step, width), lambda n: (row_of(n), at // width))
        return pl.BlockSpec((rows_per_step, width), lambda p, n: (row_of(n), at // width + p))

    cur = lambda at: spec(at, lambda n: n)
    prev = lambda at: spec(at, lambda n: jnp.maximum(n - 1, 0))
    own = spec(0, lambda n: 0)
    grid = (nb,) if dil == 1 else (4, nb)
    sems = ("arbitrary",) if dil == 1 else ("parallel", "arbitrary")
    dq, dk, dv, dk_last, dv_last = pl.pallas_call(
        body, name=name, grid=grid,
        in_specs=[cur(0), prev(0), cur(0), prev(C_AV), cur(C_AV), cur(0), cur(0), cur(512)],
        out_specs=[cur(0), prev(0), prev(0), own, own],
        out_shape=[jax.ShapeDtypeStruct((s, 512), F32)] * 3 + [jax.ShapeDtypeStruct((rows_per_step, 512), F32)] * 2,
        compiler_params=_params(*sems),
    )(qa, ka, ka, proj, proj, y_att, lse, dmixed)
    return dq, dk.at[s - rows_per_step:].set(dk_last), dv.at[s - rows_per_step:].set(dv_last)


def _attn_post(parts, proj, qg, kg, *, name):
    s = proj.shape[0]
    tm = ROW_TILE
    nblk = s // tm

    def body(*refs):
        ins, (q_ref, k_ref, qg_ref, kg_ref, dq_out, dk_out, dv_out, sums_ref) = refs[:9], refs[9:]
        i = pl.program_id(0)

        @pl.when(i == 0)
        def _():
            sums_ref[...] = jnp.zeros_like(sums_ref)

        dq = (ins[0][...] + ins[3][...]) + ins[6][...]
        dk = (ins[1][...] + ins[4][...]) + ins[7][...]
        dv = (ins[2][...] + ins[5][...]) + ins[8][...]
        dv_out[...] = dv.astype(BF16)
        for row, (x_ref, g_ref, dy, out, post) in enumerate(((q_ref, qg_ref, dq, dq_out, 0.125), (k_ref, kg_ref, dk, dk_out, 1.0))):
            x = x_ref[...]
            rs = lax.rsqrt(_head_sums(x * x) * (1.0 / 64) + EPS)
            xn = x * rs
            dy = dy * post
            sums_ref[row] += _fold8(dy * xn)
            dn = dy * g_ref[...]
            out[...] = (rs * (dn - xn * (_head_sums(dn * xn) * (1.0 / 64)))).astype(BF16)

        @pl.when(i == nblk - 1)
        def _():
            _spread_total(sums_ref)

    here = pl.BlockSpec((tm, 512), lambda i: (i, 0))
    col = lambda at: pl.BlockSpec((tm, 512), lambda i: (i, at // 512))
    vec = pl.BlockSpec((1, 512), lambda i: (0, 0))
    return pl.pallas_call(
        body, name=name, grid=(nblk,), in_specs=[here] * 9 + [col(C_AQ), col(C_AK), vec, vec],
        out_specs=[here, here, here, pl.BlockSpec((2, 8, 512), lambda i: (0, 0, 0))],
        out_shape=[jax.ShapeDtypeStruct((s, 512), BF16)] * 3 + [jax.ShapeDtypeStruct((2, 8, 512), F32)],
        compiler_params=_params("arbitrary"))(*[t for part in parts for t in part], proj, proj, qg, kg)


FFN_TM, FFN_TN = 256, 1408
HALO = 16


def _conv3(u_ref, halo_ref, w_ref, b_ref, first):
    u = u_ref[...].astype(F32)
    ext = jnp.concatenate([jnp.where(first, 0.0, halo_ref[...].astype(F32)), u], axis=0)
    u1 = pltpu.roll(ext, 1, 0)[HALO:]
    u2 = pltpu.roll(ext, 2, 0)[HALO:]
    return b_ref[...] + w_ref[0:1, :] * u2 + w_ref[1:2, :] * u1 + w_ref[2:3, :] * u


def _ffn_specs(tm, tn):
    nj = D_FF // tn
    blk = lambda half: pl.BlockSpec((tm, tn), lambda j, i: (i, j + half * nj))
    halo = lambda half: pl.BlockSpec((HALO, tn), lambda j, i: (jnp.maximum(i * (tm // HALO) - 1, 0), j + half * nj))
    wspec = lambda half: pl.BlockSpec((3, tn), lambda j, i: (0, j + half * nj))
    bspec = lambda half: pl.BlockSpec((1, tn), lambda j, i: (0, j + half * nj))
    return [blk(0), halo(0), blk(1), halo(1), wspec(0), wspec(1), bspec(0), bspec(1)]


def _conv_swiglu_fwd(u, conv_w, conv_b, *, name):
    s = u.shape[0]
    tm, tn = FFN_TM, FFN_TN

    def body(ug_ref, hg_ref, uv_ref, hv_ref, wg_ref, wv_ref, bg_ref, bv_ref, act_ref, uc_ref):
        first = pl.program_id(1) == 0
        cg = _conv3(ug_ref, hg_ref, wg_ref, bg_ref, first)
        cv = _conv3(uv_ref, hv_ref, wv_ref, bv_ref, first)
        act_ref[...] = (cg * _sigmoid(cg) * cv).astype(BF16)
        uc_ref[0] = cg.astype(BF16)
        uc_ref[1] = cv.astype(BF16)

    return pl.pallas_call(
        body, name=name, grid=(D_FF // tn, s // tm), in_specs=_ffn_specs(tm, tn),
        out_specs=[pl.BlockSpec((tm, tn), lambda j, i: (i, j)), pl.BlockSpec((2, tm, tn), lambda j, i: (0, i, j))],
        out_shape=[jax.ShapeDtypeStruct((s, D_FF), BF16), jax.ShapeDtypeStruct((2, s, D_FF), BF16)],
        compiler_params=_params("parallel", "parallel"))(u, u, u, u, conv_w, conv_w, conv_b, conv_b)


def _swiglu_bwd(uc, dact, *, name):
    _, s, _ = uc.shape
    tm, tn = FFN_TM, FFN_TN

    def body(uc_ref, da_ref, duc_ref, sums_ref):
        i = pl.program_id(1)

        @pl.when(i == 0)
        def _():
            sums_ref[...] = jnp.zeros_like(sums_ref)

        cg, cv, da = uc_ref[0].astype(F32), uc_ref[1].astype(F32), da_ref[...].astype(F32)
        sg = _sigmoid(cg)
        dg = da * cv * (sg * (1.0 + cg * (1.0 - sg)))
        dv = da * (cg * sg)
        duc_ref[0] = dg.astype(BF16)
        duc_ref[1] = dv.astype(BF16)
        sums_ref[0] += _fold8(dg)
        sums_ref[1] += _fold8(dv)

        @pl.when(i == s // tm - 1)
        def _():
            _spread_total(sums_ref)

    pair = pl.BlockSpec((2, tm, tn), lambda j, i: (0, i, j))
    return pl.pallas_call(
        body, name=name, grid=(D_FF // tn, s // tm), in_specs=[pair, pl.BlockSpec((tm, tn), lambda j, i: (i, j))],
        out_specs=[pair, pl.BlockSpec((2, 8, tn), lambda j, i: (0, 0, j))],
        out_shape=[jax.ShapeDtypeStruct((2, s, D_FF), BF16), jax.ShapeDtypeStruct((2, 8, D_FF), F32)],
        compiler_params=_params("parallel", "arbitrary"))(uc, dact)


def _conv_bwd(duc, u, conv_w, *, name):
    _, s, _ = duc.shape
    tm, tn = FFN_TM, FFN_TN
    nj, ni = D_FF // tn, s // tm

    def body(d_ref, halo_ref, u_ref, w_ref, du_ref, sums_ref):
        i = pl.program_id(2)

        @pl.when(i == 0)
        def _():
            sums_ref[...] = jnp.zeros_like(sums_ref)

        d = d_ref[0].astype(F32)
        ext = jnp.concatenate([d, jnp.where(i == ni - 1, 0.0, halo_ref[0].astype(F32))], axis=0)
        n = tm + HALO
        d1 = pltpu.roll(ext, n - 1, 0)[:tm]
        d2 = pltpu.roll(ext, n - 2, 0)[:tm]
        du_ref[...] = (w_ref[2:3, :] * d + w_ref[1:2, :] * d1 + w_ref[0:1, :] * d2).astype(BF16)
        uv = u_ref[...].astype(F32)
        for t, shifted in enumerate((d2, d1, d)):
            sums_ref[0, t] += _fold8(shifted * uv)

        @pl.when(i == ni - 1)
        def _():
            _spread_total(sums_ref)

    return pl.pallas_call(
        body, name=name, grid=(2, nj, ni),
        in_specs=[pl.BlockSpec((1, tm, tn), lambda g, j, i: (g, i, j)),
                  pl.BlockSpec((1, HALO, tn), lambda g, j, i: (g, jnp.minimum((i + 1) * (tm // HALO), s // HALO - 1), j)),
                  pl.BlockSpec((tm, tn), lambda g, j, i: (i, g * nj + j)),
                  pl.BlockSpec((3, tn), lambda g, j, i: (0, g * nj + j))],
        out_specs=[pl.BlockSpec((tm, tn), lambda g, j, i: (i, g * nj + j)),
                   pl.BlockSpec((1, 3, 8, tn), lambda g, j, i: (g, 0, 0, j))],
        out_shape=[jax.ShapeDtypeStruct((s, 2 * D_FF), BF16), jax.ShapeDtypeStruct((2, 3, 8, D_FF), F32)],
        compiler_params=_params("parallel", "parallel", "arbitrary"))(duc, duc, u, conv_w)


def _loss_head(x1, ffn, gate, target, *, name):
    s, d = x1.shape
    tm = ROW_TILE

    def body(x_ref, f_ref, g_ref, t_ref, dy_ref, df_ref, sums_ref):
        i = pl.program_id(0)

        @pl.when(i == 0)
        def _():
            sums_ref[...] = jnp.zeros_like(sums_ref)

        f = f_ref[...]
        err = x_ref[...] + g_ref[...] * f - t_ref[...]
        dy = err * (1.0 / d)
        dy_ref[...] = dy
        df_ref[...] = (g_ref[...] * dy).astype(BF16)
        sums_ref[0] += _fold8(dy * f)
        sums_ref[1] += _fold8(err * err)

        @pl.when(i == s // tm - 1)
        def _():
            _spread_total(sums_ref)

    row = pl.BlockSpec((tm, d), lambda i: (i, 0))
    return pl.pallas_call(
        body, name=name, grid=(s // tm,), in_specs=[row, row, pl.BlockSpec((1, d), lambda i: (0, 0)), row],
        out_specs=[row, row, pl.BlockSpec((2, 8, d), lambda i: (0, 0, 0))],
        out_shape=[jax.ShapeDtypeStruct((s, d), F32), jax.ShapeDtypeStruct((s, d), BF16), jax.ShapeDtypeStruct((2, 8, d), F32)],
        compiler_params=_params("arbitrary"))(x1, ffn, gate, target)


def _adamw(w, g, m, v, *, name):
    rows, cols = w.shape
    split = isinstance(g, tuple)
    if rows % 8 == 0 or rows <= ADAM_TILE:
        span = rows // 2 if split else rows
        tm = next((t for t in range(ADAM_TILE, 7, -8) if span % t == 0), span)
        shape, at, steps, per_half = (tm, cols), (lambda i: (i, 0)), rows // tm, span // tm
    else:
        shape, at, steps, per_half = (rows, ADAM_TILE), (lambda i: (0, i)), cols // ADAM_TILE, cols // ADAM_TILE // 2

    def update(gv, w_ref, m_ref, v_ref, d_ref, mo_ref, vo_ref):
        mn = ADAM_B1 * m_ref[...] + (1.0 - ADAM_B1) * gv
        vn = ADAM_B2 * v_ref[...] + (1.0 - ADAM_B2) * (gv * gv)
        m_hat = mn / (1.0 - ADAM_B1 ** ADAM_STEP)
        v_hat = vn / (1.0 - ADAM_B2 ** ADAM_STEP)
        d_ref[...] = -ADAM_LR * (m_hat / (jnp.sqrt(v_hat) + ADAM_EPS) + ADAM_WD * w_ref[...])
        mo_ref[...] = mn
        vo_ref[...] = vn

    out_shape = [jax.ShapeDtypeStruct((rows, cols), F32)] * (4 if split else 3)
    if not split:
        def body(w_ref, g_ref, m_ref, v_ref, d_ref, mo_ref, vo_ref):
            update(g_ref[...], w_ref, m_ref, v_ref, d_ref, mo_ref, vo_ref)

        blk = pl.BlockSpec(shape, at)
        return pl.pallas_call(body, name=name, grid=(steps,), in_specs=[blk] * 4, out_specs=[blk] * 3, out_shape=out_shape,
                              compiler_params=_params("parallel"))(w, g, m, v)

    mine, other, core = g

    def body(core_ref, w_ref, mine_ref, other_ref, m_ref, v_ref, d_ref, mo_ref, vo_ref, g_ref):
        gv = jnp.where(pl.program_id(0) // per_half == core_ref[0], mine_ref[...], other_ref[...])
        g_ref[...] = gv
        update(gv, w_ref, m_ref, v_ref, d_ref, mo_ref, vo_ref)

    blk = pl.BlockSpec(shape, lambda i, core_ref: at(i))
    half = pl.BlockSpec(shape, lambda i, core_ref: at(i % per_half))
    return pl.pallas_call(
        body, name=name, out_shape=out_shape, compiler_params=_params("parallel"),
        grid_spec=pltpu.PrefetchScalarGridSpec(num_scalar_prefetch=1, grid=(steps,), in_specs=[blk, half, half, blk, blk],
                                               out_specs=[blk] * 4))(core, w, mine, other, m, v)


def _colsum(t):
    return t[..., 0, :]


def _in_proj_layout(w_in):
    pad = jnp.zeros((w_in.shape[0], PROJ_W - C_LR - GLA_GATE_RANK), w_in.dtype)
    return jnp.concatenate([w_in[:, :1536], w_in[:, 1552:], w_in[:, 1536:1552], pad], axis=1)


def _in_proj_grad_layout(g):
    return jnp.concatenate([g[:, :1536], g[:, C_LR:C_LR + GLA_GATE_RANK], g[:, 1536:C_LR]], axis=1)


def _gate_layout(gla_w_gate):
    return jnp.pad(gla_w_gate, ((0, HEAD_LANES - GLA_GATE_RANK), (0, 0))).astype(BF16)


def _local_step(x, target, mod, wi, wo, ffn_weights, ffn_grads_ready, attn_grads_ready, conv_w, conv_b, wg, bg, gn, qg, kg, n1g, n2g):
    d = D_MODEL
    sh1, sc1, g1, sh2, sc2, g2 = [mod[:, i * d:(i + 1) * d] for i in range(6)]
    qg8, kg8 = jnp.tile(qg, (1, 8)), jnp.tile(kg, (1, 8))

    _, h1, h1_t = _norm_mod_fwd(x, None, None, n1g, sc1, sh1, name="norm1_fwd")
    proj = _mm(h1, wi, tm=1024, tn=PROJ_W, tk=d, name="in_proj")
    o_raw, y_gla, states = _gla_fwd(proj, wg, bg, gn, name="gla_fwd")
    qa, ka = _attn_prep(proj, qg8, kg8, name="attn_prep")
    sparse = [_dil_attn_fwd(qa, ka, proj, dil, name=f"attn_fwd_d{dil}") for dil in DILATIONS[1:]]
    mixed, y_att, lse = _dense_attn_fwd_merge(qa, ka, proj, sparse, y_gla, name="attn_fwd_d1_merge")
    attn_out = _mm(mixed, wo, tm=1024, tn=d, tk=d, name="out_proj")
    x1, h2, h2_t = _norm_mod_fwd(x, attn_out, g1, n2g, sc2, sh2, name="norm2_fwd")
    wup, wdown = ffn_weights(h2)
    u = _mm(h2, wup, out_dtype=BF16, tm=1024, tn=D_FF, tk=d, name="up_proj")
    act, uc = _conv_swiglu_fwd(u, conv_w, conv_b, name="conv_swiglu_fwd")
    ffn = _mm(act, wdown, tm=1024, tn=d, tk=D_FF, name="down_proj")
    dy, dffn, head_sums = _loss_head(x1, ffn, g2, target, name="loss_head")

    dact = _mm(dffn, wdown, tb=True, out_dtype=BF16, tm=1024, tn=D_FF, tk=d, name="down_proj_dx")
    g_wdown, g_wdown_b = _mm(act, dffn, ta=True, tm=1408, tn=d, tk=2048, also_bf16=True, name="down_proj_dw")
    duc, bias_sums = _swiglu_bwd(uc, dact, name="swiglu_bwd")
    du, tap_sums = _conv_bwd(duc, u, conv_w, name="conv_bwd")
    dh2 = _mm(du, wup, tb=True, tm=1024, tn=d, tk=D_FF, name="up_proj_dx")
    g_wup, g_wup_b = _mm(h2_t, du, tm=d, tn=1408, tk=2048, shard_cols=True, also_bf16=True, name="up_proj_dw")
    token = ffn_grads_ready(g_wup_b, g_wdown_b)
    g1_late = g1 if token is None else g1 + token[0:1, 0:1]
    dx1, dao, n2_sums = _norm_mod_bwd(x1, dh2, dy, n2g, sc2, attn_out, g1_late, name="norm2_bwd")

    dmixed = _mm(dao, wo, tb=True, tm=1024, tn=d, tk=d, name="out_proj_dx")
    g_wo = _mm(mixed, dao, ta=True, tm=d, tn=d, tk=1024, name="out_proj_dw")
    dgq, dgk, dgv, dgr, dlr, g_wg, gla_sums = _gla_bwd(proj, wg, bg, gn, o_raw, states, dmixed, name="gla_bwd")
    parts = [_dil_attn_bwd(qa, ka, proj, y_att, lse, dmixed, dil, name=f"attn_bwd_d{dil}") for dil in DILATIONS]
    daq, dak, dav, qk_sums = _attn_post(parts, proj, qg8, kg8, name="attn_post")
    dproj = jnp.concatenate([dgq, dgk, dgv, dgr, daq, dak, dav, dlr], axis=1)
    g_wi = _mm(h1_t, dproj, tm=512, tn=PROJ_W, tk=2048, name="in_proj_dw")
    token = attn_grads_ready(g_wi, g_wo)
    wi_late = wi if token is None else wi + token[0:1, 0:1].astype(BF16)
    dh1 = _mm(dproj, wi_late, tb=True, tm=1024, tn=d, tk=PROJ_W, name="in_proj_dx")
    grad_x, _, n1_sums = _norm_mod_bwd(x, dh1, dx1, n1g, sc1, None, None, name="norm1_bwd")

    n1, n2, hs, taps, cb = _colsum(n1_sums), _colsum(n2_sums), _colsum(head_sums), _colsum(tap_sums), _colsum(bias_sums)
    gs, qs = _colsum(gla_sums), _colsum(qk_sums)
    dmod = jnp.concatenate([n1[1], n1[0] * n1g[0], n2[2], n2[1], n2[0] * n2g[0], hs[0]])
    small = dict(
        dmod=dmod,
        norm1_g=n1[0] * (1.0 + sc1[0]), norm2_g=n2[0] * (1.0 + sc2[0]),
        gla_w_gate=g_wg[:GLA_GATE_RANK], gla_b_gate=gs[0, :256], gla_norm_g=gs[1].reshape(4, 128).sum(axis=0),
        q_norm_g=qs[0].reshape(8, 64).sum(axis=0), k_norm_g=qs[1].reshape(8, 64).sum(axis=0),
        conv_w=jnp.concatenate([taps[0], taps[1]], axis=1), conv_b=jnp.concatenate([cb[0], cb[1]]),
    )
    return head_sums[1], grad_x, (g_wi, g_wo, g_wup, g_wdown), small


N_DEV, N_CHIP = 8, 4
ANY = pl.BlockSpec(memory_space=pl.ANY)
VMEM_SPEC = pl.BlockSpec(memory_space=pltpu.VMEM)


def _place():
    x, y, c = lax.axis_index("x"), lax.axis_index("y"), lax.axis_index("c")
    other_chips = [(1 - x, y), (x, 1 - y), (1 - x, 1 - y)]
    return x, y, c, (x, y, 1 - c), other_chips


def _all_gather_small(v, *, name):
    m, n = v.shape

    def body(v_ref, out_ref, send_sems, recv_sems, local_sem):
        x, y, c, sibling, chips = _place()
        me = (x, y, c)

        def rows(px, py, pc):
            return out_ref.at[pl.ds((4 * px + 2 * py + pc) * m, m), :]

        def copy(k, block, to, src=None):
            return pltpu.make_async_remote_copy(
                src_ref=rows(*block) if src is None else src, dst_ref=rows(*block), send_sem=send_sems.at[k],
                recv_sem=recv_sems.at[k], device_id=to, device_id_type=MESH)

        mine = pltpu.make_async_copy(v_ref, rows(*me), local_sem)
        mine.start()
        first = [copy(0, me, sibling, src=v_ref)]
        first += [copy(1 + j, me, (*chip, c), src=v_ref) for j, chip in enumerate(chips)]
        for cp in first:
            cp.start()
        passed = [copy(4 + j, (*chip, c), sibling) for j, chip in enumerate(chips)]
        for j, chip in enumerate(chips):
            copy(1 + j, (*chip, c), me).wait_recv()
            passed[j].start()
        copy(0, sibling, me).wait_recv()
        for j, chip in enumerate(chips):
            copy(4 + j, (*chip, 1 - c), me).wait_recv()
        for cp in first + passed:
            cp.wait_send()
        mine.wait()

    return pl.pallas_call(
        body, name=name, out_shape=jax.ShapeDtypeStruct((N_DEV * m, n), v.dtype), in_specs=[VMEM_SPEC], out_specs=VMEM_SPEC,
        scratch_shapes=[pltpu.SemaphoreType.DMA((7,)), pltpu.SemaphoreType.DMA((7,)), pltpu.SemaphoreType.DMA],
    )(v)


def _gather_weight_shards(shards, *, name):
    nw = len(shards)

    def body(*refs):
        srcs, outs, (send_sems, recv_sems) = refs[:nw], refs[nw:2 * nw], refs[2 * nw:]
        x, y, c, sibling, chips = _place()
        index = lambda chip: 2 * chip[0] + chip[1]

        def copy(w, k, src, dst, to):
            return pltpu.make_async_remote_copy(src_ref=src, dst_ref=dst, send_sem=send_sems.at[6 * w + k],
                                                recv_sem=recv_sems.at[6 * w + k], device_id=to, device_id_type=MESH)

        sent = []
        for w, (src_ref, out_ref) in enumerate(zip(srcs, outs)):
            for k, chip in enumerate(chips):
                sent.append(copy(w, k, src_ref.at[c], out_ref.at[2 * x + y, c], (*chip, c)))
                sent[-1].start()
        for w, out_ref in enumerate(outs):
            for k, chip in enumerate(chips):
                landed = out_ref.at[index(chip), c]
                copy(w, k, landed, landed, (*chip, c)).wait_recv()
                sent.append(copy(w, 3 + k, landed, landed, sibling))
                sent[-1].start()
        for w, out_ref in enumerate(outs):
            for k, chip in enumerate(chips):
                passed_on = out_ref.at[index(chip), 1 - c]
                copy(w, 3 + k, passed_on, passed_on, sibling).wait_recv()
        for cp in sent:
            cp.wait_send()

    return pl.pallas_call(
        body, name=name, out_shape=[jax.ShapeDtypeStruct((N_CHIP, *s.shape), s.dtype) for s in shards],
        in_specs=[ANY] * nw, out_specs=[ANY] * nw,
        scratch_shapes=[pltpu.SemaphoreType.DMA((6 * nw,)), pltpu.SemaphoreType.DMA((6 * nw,))],
    )(*shards)


HBM_SPEC = pl.BlockSpec(memory_space=pltpu.HBM)
SEM_SPEC = pl.BlockSpec(memory_space=pltpu.SEMAPHORE)
DATAFLOW_EFFECT = pltpu.SideEffectType.DATAFLOW_SIDE_EFFECTING


def _late_copies(srcs, lands, send_sems, recv_sems):
    x, y, c, _, chips = _place()
    return [pltpu.make_async_remote_copy(
        src_ref=src.at[c], dst_ref=land.at[2 * x + y, c], send_sem=send_sems.at[6 * w + 2 * r + core],
        recv_sem=recv_sems.at[6 * w + 2 * r + c], device_id=(*chip, core), device_id_type=MESH)
        for w, (src, land) in enumerate(zip(srcs, lands)) for r, chip in enumerate(chips) for core in range(2)]


def _gather_late_start(own, after, *, name):
    nw = len(own)

    def body(*refs):
        srcs, lands, send_sems, recv_sems, token = refs[:nw], refs[nw:2 * nw], refs[2 * nw + 1], refs[2 * nw + 2], refs[-1]
        for cp in _late_copies(srcs, lands, send_sems, recv_sems):
            cp.start()
        token[...] = jnp.zeros_like(token)

    lands = [pltpu.with_memory_space_constraint(lax.empty((N_CHIP, *s.shape), s.dtype), pltpu.HBM) for s in own]
    own = [pltpu.with_memory_space_constraint(s, pltpu.HBM) for s in own]
    out = pl.pallas_call(
        body, name=name,
        out_shape=(pltpu.SemaphoreType.DMA((6 * nw,)), pltpu.SemaphoreType.DMA((6 * nw,)),
                   *[pltpu.HBM(s.shape, s.dtype) for s in own], *[pltpu.HBM(s.shape, s.dtype) for s in lands],
                   jax.ShapeDtypeStruct((8, 128), F32)),
        in_specs=[HBM_SPEC] * (2 * nw) + [ANY], out_specs=(SEM_SPEC, SEM_SPEC, *[HBM_SPEC] * (2 * nw), VMEM_SPEC),
        input_output_aliases={i: 2 + i for i in range(2 * nw)},
        compiler_params=pltpu.CompilerParams(has_side_effects=DATAFLOW_EFFECT))(*own, *lands, after)
    return out[0], out[1], out[2:2 + nw], out[2 + nw:2 + 2 * nw], out[-1]


def _gather_late_wait(send_sems, recv_sems, own, lands, after, *, name):
    nw = len(own)

    def body(*refs):
        srcs, lands_in, send_sems, recv_sems = refs[:nw], refs[nw:2 * nw], refs[2 * nw], refs[2 * nw + 1]
        x, y, c, _, chips = _place()
        for cp in _late_copies(srcs, lands_in, send_sems, recv_sems):
            cp.wait_send()
        for w, (src, land) in enumerate(zip(srcs, lands_in)):
            for r, chip in enumerate(chips):
                for core in range(2):
                    pltpu.make_async_remote_copy(
                        src_ref=src.at[c], dst_ref=land.at[2 * chip[0] + chip[1], core], send_sem=send_sems.at[6 * w + 2 * r + core],
                        recv_sem=recv_sems.at[6 * w + 2 * r + core], device_id=(*chip, core), device_id_type=MESH).wait_recv()

    out = pl.pallas_call(
        body, name=name, out_shape=(*[pltpu.HBM(s.shape, s.dtype) for s in own], *[pltpu.HBM(s.shape, s.dtype) for s in lands]),
        in_specs=[HBM_SPEC] * (2 * nw) + [SEM_SPEC, SEM_SPEC, ANY], out_specs=tuple([HBM_SPEC] * (2 * nw)),
        input_output_aliases={i: i for i in range(2 * nw)},
        compiler_params=pltpu.CompilerParams(has_side_effects=DATAFLOW_EFFECT))(*own, *lands, send_sems, recv_sems, after)
    return out[:nw], out[nw:]


def _direct_reduce_copies(srcs, lands, send_sems, recv_sems):
    x, y, c, _, _ = _place()
    cps = []
    for w, (src, land) in enumerate(zip(srcs, lands)):
        for rel in range(1, N_DEV):
            tx, ty, tc = (1 - x if rel & 4 else x), (1 - y if rel & 2 else y), (1 - c if rel & 1 else c)
            cps.append(pltpu.make_async_remote_copy(
                src_ref=src.at[2 * tx + ty, tc], dst_ref=land.at[rel - 1], send_sem=send_sems.at[7 * w + rel - 1],
                recv_sem=recv_sems.at[7 * w + rel - 1], device_id=(tx, ty, tc), device_id_type=MESH))
    return cps


def _direct_reduce_start(grads, *, name):
    nw = len(grads)

    def body(*refs):
        srcs, lands, send_sems, recv_sems, token = refs[:nw], refs[nw:2 * nw], refs[2 * nw], refs[2 * nw + 1], refs[-1]
        for cp in _direct_reduce_copies(srcs, lands, send_sems, recv_sems):
            cp.start()
        token[...] = jnp.zeros_like(token)

    lands = [pltpu.with_memory_space_constraint(lax.empty((N_DEV - 1, *g.shape[2:]), g.dtype), pltpu.HBM) for g in grads]
    grads = [pltpu.with_memory_space_constraint(g, pltpu.HBM) for g in grads]
    out = pl.pallas_call(
        body, name=name,
        out_shape=(pltpu.SemaphoreType.DMA((7 * nw,)), pltpu.SemaphoreType.DMA((7 * nw,)),
                   *[pltpu.HBM(g.shape, g.dtype) for g in grads], *[pltpu.HBM(t.shape, t.dtype) for t in lands],
                   jax.ShapeDtypeStruct((8, 128), F32)),
        in_specs=[HBM_SPEC] * (2 * nw), out_specs=(SEM_SPEC, SEM_SPEC, *[HBM_SPEC] * (2 * nw), VMEM_SPEC),
        input_output_aliases={i: 2 + i for i in range(2 * nw)},
        compiler_params=pltpu.CompilerParams(has_side_effects=DATAFLOW_EFFECT))(*grads, *lands)
    return out[0], out[1], out[2:2 + nw], out[2 + nw:2 + 2 * nw], out[-1]


def _direct_reduce_wait(send_sems, recv_sems, grads, lands, after, *, name):
    nw = len(grads)

    def body(*refs):
        srcs, lands_in, send_sems, recv_sems = refs[:nw], refs[nw:2 * nw], refs[2 * nw], refs[2 * nw + 1]
        cps = _direct_reduce_copies(srcs, lands_in, send_sems, recv_sems)
        for cp in cps:
            cp.wait_send()
        for cp in cps:
            cp.wait_recv()

    out = pl.pallas_call(
        body, name=name, out_shape=(*[pltpu.HBM(g.shape, g.dtype) for g in grads], *[pltpu.HBM(t.shape, t.dtype) for t in lands]),
        in_specs=[HBM_SPEC] * (2 * nw) + [SEM_SPEC, SEM_SPEC, ANY], out_specs=tuple([HBM_SPEC] * (2 * nw)),
        input_output_aliases={i: i for i in range(2 * nw)},
        compiler_params=pltpu.CompilerParams(has_side_effects=DATAFLOW_EFFECT))(*grads, *lands, send_sems, recv_sems, after)
    return out[nw:]


def _direct_reduce_add(grad, landed, chip, core, *, name):
    _, r, n = grad.shape
    half = r // 2
    tr = _row_tile(half)
    nb = half // tr

    def body(chip_ref, core_ref, g_ref, t_ref, o_ref):
        acc = g_ref[0]
        for k in range(N_DEV - 1):
            acc = acc + t_ref[k].astype(F32)
        o_ref[...] = acc

    return pl.pallas_call(
        body, name=name,
        grid_spec=pltpu.PrefetchScalarGridSpec(
            num_scalar_prefetch=2, grid=(nb,),
            in_specs=[pl.BlockSpec((1, tr, n), lambda i, chip_ref, core_ref: (chip_ref[0], core_ref[0] * nb + i, 0)),
                      pl.BlockSpec((N_DEV - 1, tr, n), lambda i, chip_ref, core_ref: (0, i, 0))],
            out_specs=pl.BlockSpec((tr, n), lambda i, chip_ref, core_ref: (i, 0))),
        out_shape=jax.ShapeDtypeStruct((half, n), F32), compiler_params=_params("parallel"))(chip, core, grad, landed)


def _share_halves(halves, *, name):
    nw = len(halves)

    def body(*refs):
        srcs, outs, (send_sems, recv_sems) = refs[:nw], refs[nw:2 * nw], refs[2 * nw:]
        _, _, _, sibling, _ = _place()
        cps = [pltpu.make_async_remote_copy(src_ref=src_ref, dst_ref=out_ref, send_sem=send_sems.at[w], recv_sem=recv_sems.at[w],
                                            device_id=sibling, device_id_type=MESH)
               for w, (src_ref, out_ref) in enumerate(zip(srcs, outs))]
        for cp in cps:
            cp.start()
        for cp in cps:
            cp.wait()

    return pl.pallas_call(
        body, name=name, out_shape=[jax.ShapeDtypeStruct(h.shape, h.dtype) for h in halves],
        in_specs=[ANY] * nw, out_specs=[ANY] * nw,
        scratch_shapes=[pltpu.SemaphoreType.DMA((nw,)), pltpu.SemaphoreType.DMA((nw,))])(*halves)


def _row_tile(rows, limit=256):
    return next(t for t in range(limit, 15, -16) if rows % t == 0)


def _sum_devices(gathered, *, name):
    _, m, n = gathered.shape

    def body(g_ref, tot_ref, loss_ref):
        tot = g_ref[0]
        for dev in range(1, N_DEV):
            tot = tot + g_ref[dev]
        tot_ref[...] = tot
        loss_ref[...] = jnp.full((8, n), (0.5 / D_MODEL) * jnp.sum(tot[0:8]), F32)

    return pl.pallas_call(body, name=name, in_specs=[VMEM_SPEC], out_specs=[VMEM_SPEC, VMEM_SPEC],
                          out_shape=[jax.ShapeDtypeStruct((m, n), F32), jax.ShapeDtypeStruct((8, n), F32)])(gathered)


def _ada_mod(cond_all, w_ada_shard, *, name):
    tn = 512

    def body(a_ref, b_ref, o_ref):
        o_ref[...] = _nn(a_ref[...], b_ref[...], precision=HIGHEST)

    return pl.pallas_call(
        body, name=name, grid=(w_ada_shard.shape[1] // tn,),
        in_specs=[pl.BlockSpec(cond_all.shape, lambda j: (0, 0)), pl.BlockSpec((D_MODEL, tn), lambda j: (0, j))],
        out_specs=pl.BlockSpec((N_DEV, tn), lambda j: (0, j)),
        out_shape=jax.ShapeDtypeStruct((N_DEV, w_ada_shard.shape[1]), F32), compiler_params=_params("parallel"))(cond_all, w_ada_shard)


def _ada_grad(cond_all, dmod_cols, *, name):
    tm = 256

    def body(a_ref, b_ref, o_ref):
        o_ref[...] = lax.dot_general(a_ref[...], b_ref[...], (((0,), (0,)), ((), ())), precision=HIGHEST,
                                     preferred_element_type=F32)

    return pl.pallas_call(
        body, name=name, grid=(D_MODEL // tm,),
        in_specs=[pl.BlockSpec((N_DEV, tm), lambda i: (0, i)), pl.BlockSpec(dmod_cols.shape, lambda i: (0, 0))],
        out_specs=pl.BlockSpec((tm, dmod_cols.shape[1]), lambda i: (i, 0)),
        out_shape=jax.ShapeDtypeStruct((D_MODEL, dmod_cols.shape[1]), F32), compiler_params=_params("parallel"))(cond_all, dmod_cols)


def _silu_rows(c8, *, name):
    def body(c_ref, o_ref):
        cv = c_ref[...]
        o_ref[...] = cv * _sigmoid(cv)

    return pl.pallas_call(body, name=name, in_specs=[VMEM_SPEC], out_specs=VMEM_SPEC,
                          out_shape=jax.ShapeDtypeStruct(c8.shape, F32))(c8)


def _rows128(t, rows=None):
    flat = t.reshape(-1, 128)
    return flat if rows is None else jnp.pad(flat, ((0, rows - flat.shape[0]), (0, 0)))


def _from_col_shards(shards, r, n):
    return shards.reshape(N_CHIP, r, n).transpose(1, 0, 2).reshape(r, N_CHIP * n)


def kernel(x, c, w_ada, b_ada, norm1_g, w_in, gla_w_gate, gla_b_gate, gla_norm_g, q_norm_g, k_norm_g, w_out, norm2_g, w_up, conv_w, conv_b, w_down, loss_target, m_w_ada, m_b_ada, m_norm1_g, m_w_in, m_gla_w_gate, m_gla_b_gate, m_gla_norm_g, m_q_norm_g, m_k_norm_g, m_w_out, m_norm2_g, m_w_up, m_conv_w, m_conv_b, m_w_down, v_w_ada, v_b_ada, v_norm1_g, v_w_in, v_gla_w_gate, v_gla_b_gate, v_gla_norm_g, v_q_norm_g, v_k_norm_g, v_w_out, v_norm2_g, v_w_up, v_conv_w, v_conv_b, v_w_down):
    d = D_MODEL
    ax, ay, ac = lax.axis_index("x"), lax.axis_index("y"), lax.axis_index("c")
    chip, dev = 2 * ax + ay, 4 * ax + 2 * ay + ac

    cond = _silu_rows(jnp.broadcast_to(c, (8, d)), name="cond_silu")[0:1]
    small_in = jnp.concatenate([_rows128(cond), _rows128(conv_w[0]), _rows128(gla_w_gate[0])], axis=0)
    small_in = _rows128(small_in, 56)
    got = _all_gather_small(small_in, name="gather_small").reshape(N_DEV, 56, 128)
    cond_all = got[:, 0:8].reshape(N_DEV, d)
    conv_w_full = _from_col_shards(got[0::2, 8:41].reshape(N_CHIP, 3 * 1408 // 128, 128), 3, 1408)
    gate_full = _from_col_shards(got[0::2, 41:49].reshape(N_CHIP, 16 * 64 // 128, 128), GLA_GATE_RANK, 64)
    mod_part = _ada_mod(cond_all, w_ada[0], name="ada_mod")
    mod_got = _all_gather_small(_rows128(mod_part), name="gather_mod").reshape(N_DEV, N_DEV, 1536)
    mod_all = mod_got[0::2].transpose(1, 0, 2).reshape(N_DEV, 6 * d) + b_ada
    mod = lax.dynamic_slice_in_dim(mod_all, dev, 1, axis=0)

    own = [w[0].astype(BF16).reshape(2, w.shape[1] // 2, w.shape[2]) for w in (w_in, w_out, w_up, w_down)]
    with_own = lambda got, mine: [lax.dynamic_update_index_in_dim(t, o, chip, 0) for t, o in zip(got, mine)]
    got_in, got_out = with_own(_gather_weight_shards(own[:2], name="gather_weights"), own[:2])
    w_in_full = got_in.reshape(N_CHIP, d, 772).transpose(1, 0, 2).reshape(d, N_CHIP * 772)
    w_out_full = got_out.reshape(d, d)
    exchanged = mod_all[0:1, 0:1] + got_in[0, 0, 0:1, 0:1].astype(F32)
    send_sems, recv_sems, own_thru, lands, token = _gather_late_start(own[2:], exchanged, name="gather_late_start")
    mod = mod + token[0:1, 0:1]

    def ffn_weights(after):
        mine, landed = _gather_late_wait(send_sems, recv_sems, own_thru, lands, after, name="gather_late_wait")
        got_up, got_down = with_own(landed, mine)
        return got_up.reshape(N_CHIP, d, 1408).transpose(1, 0, 2).reshape(d, 2 * D_FF), got_down.reshape(D_FF, d)

    ffn_reduce, attn_reduce, attn_parts = [], [], []
    halves_of = lambda g: g.reshape(N_CHIP, 2, g.shape[-2] // 2, g.shape[-1])

    def ffn_grads_ready(g_wup_b, g_wdown_b):
        ffn_reduce.extend(_direct_reduce_start([halves_of(g_wup_b), halves_of(g_wdown_b.reshape(N_CHIP, D_FF // N_CHIP, d))],
                                               name="reduce_ffn_start"))
        return ffn_reduce[4]

    def attn_grads_ready(g_wi, g_wo):
        attn_parts.extend([_in_proj_grad_layout(g_wi).reshape(d, N_CHIP, 772).transpose(1, 0, 2), g_wo.reshape(N_CHIP, d // N_CHIP, d)])
        attn_reduce.extend(_direct_reduce_start([halves_of(g.astype(BF16)) for g in attn_parts], name="reduce_attn_start"))
        return attn_reduce[4]

    err2, grad_x, (g_wi, g_wo, g_wup, g_wdown), small = _local_step(
        x[0], loss_target[0], mod, _in_proj_layout(w_in_full), w_out_full, ffn_weights, ffn_grads_ready, attn_grads_ready,
        conv_w_full, conv_b,
        _gate_layout(gate_full), gla_b_gate, gla_norm_g, q_norm_g, k_norm_g, norm1_g, norm2_g)

    pieces = [err2[0], small["dmod"], small["norm1_g"], small["norm2_g"], small["gla_w_gate"].reshape(-1), small["gla_b_gate"],
              small["gla_norm_g"], small["q_norm_g"], small["k_norm_g"], small["conv_w"].reshape(-1), small["conv_b"]]
    sizes = [p.shape[0] for p in pieces]
    at = [sum(sizes[:i]) for i in range(len(sizes) + 1)]
    vec = _rows128(jnp.concatenate(pieces), 288)
    got = _all_gather_small(vec, name="gather_grads").reshape(N_DEV, 288, 128)
    total, loss8 = _sum_devices(got, name="sum_devices")
    total = total.reshape(-1)
    seg = lambda i: total[at[i]:at[i + 1]]
    dmod_all = got.reshape(N_DEV, -1)[:, at[1]:at[2]]
    g_small = dict(
        b_ada=seg(1)[None], norm1_g=seg(2)[None], norm2_g=seg(3)[None],
        gla_w_gate=lax.dynamic_slice_in_dim(seg(4).reshape(GLA_GATE_RANK, 256), chip * 64, 64, axis=1),
        gla_b_gate=seg(5)[None], gla_norm_g=seg(6)[None], q_norm_g=seg(7)[None], k_norm_g=seg(8)[None],
        conv_w=lax.dynamic_slice_in_dim(seg(9).reshape(3, 2 * D_FF), chip * 1408, 1408, axis=1), conv_b=seg(10)[None])
    dmod_cols = lax.dynamic_slice_in_dim(dmod_all.reshape(N_DEV, 6 * d), chip * 1536, 1536, axis=1)
    g_w_ada = _ada_grad(cond_all, dmod_cols, name="ada_grad")

    core_id, chip_id = jnp.reshape(ac, (1,)).astype(jnp.int32), jnp.reshape(chip, (1,)).astype(jnp.int32)
    landed = (_direct_reduce_wait(*attn_reduce[:4], grad_x, name="reduce_attn_wait")
              + _direct_reduce_wait(*ffn_reduce[:4], grad_x, name="reduce_ffn_wait"))
    own = attn_parts + [g_wup, g_wdown.reshape(N_CHIP, D_FF // N_CHIP, d)]
    summed = [_direct_reduce_add(g, t, chip_id, core_id, name=f"reduce_add_{tag}")
              for g, t, tag in zip(own, landed, ("w_in", "w_out", "w_up", "w_down"))]
    others = _share_halves(summed, name="share_pair")

    grads = dict(w_ada=g_w_ada, **g_small, **dict(zip(("w_in", "w_out", "w_up", "w_down"), zip(summed, others))))
    names = ["w_ada", "b_ada", "norm1_g", "w_in", "gla_w_gate", "gla_b_gate", "gla_norm_g", "q_norm_g", "k_norm_g", "w_out",
             "norm2_g", "w_up", "conv_w", "conv_b", "w_down"]
    ws = dict(w_ada=w_ada, b_ada=b_ada, norm1_g=norm1_g, w_in=w_in, gla_w_gate=gla_w_gate, gla_b_gate=gla_b_gate,
              gla_norm_g=gla_norm_g, q_norm_g=q_norm_g, k_norm_g=k_norm_g, w_out=w_out, norm2_g=norm2_g, w_up=w_up,
              conv_w=conv_w, conv_b=conv_b, w_down=w_down)
    ms = dict(w_ada=m_w_ada, b_ada=m_b_ada, norm1_g=m_norm1_g, w_in=m_w_in, gla_w_gate=m_gla_w_gate, gla_b_gate=m_gla_b_gate,
              gla_norm_g=m_gla_norm_g, q_norm_g=m_q_norm_g, k_norm_g=m_k_norm_g, w_out=m_w_out, norm2_g=m_norm2_g, w_up=m_w_up,
              conv_w=m_conv_w, conv_b=m_conv_b, w_down=m_w_down)
    vs = dict(w_ada=v_w_ada, b_ada=v_b_ada, norm1_g=v_norm1_g, w_in=v_w_in, gla_w_gate=v_gla_w_gate, gla_b_gate=v_gla_b_gate,
              gla_norm_g=v_gla_norm_g, q_norm_g=v_q_norm_g, k_norm_g=v_k_norm_g, w_out=v_w_out, norm2_g=v_norm2_g, w_up=v_w_up,
              conv_w=v_conv_w, conv_b=v_conv_b, w_down=v_w_down)
    g_out, d_out, m_out, v_out = [], [], [], []
    for nm in names:
        shape = ws[nm].shape
        flip = (lambda t: t.T) if shape[-1] % 128 and shape[-2] % 128 == 0 else (lambda t: t)
        w2 = flip(ws[nm].reshape(shape[-2:]))
        if isinstance(grads[nm], tuple):
            mine, other = grads[nm]
            dl, mn, vn, g2 = _adamw(w2, (flip(mine), flip(other), core_id), flip(ms[nm].reshape(shape[-2:])),
                                    flip(vs[nm].reshape(shape[-2:])), name=f"adamw_{nm}")
        else:
            g2 = flip(grads[nm].reshape(shape[-2:]))
            dl, mn, vn = _adamw(w2, g2, flip(ms[nm].reshape(shape[-2:])), flip(vs[nm].reshape(shape[-2:])), name=f"adamw_{nm}")
        for outs, t in ((g_out, g2), (d_out, dl), (m_out, mn), (v_out, vn)):
            outs.append(flip(t).reshape(shape))
    return (loss8[0, 0], grad_x[None], *g_out, *d_out, *m_out, *v_out)
```

```python
import functools

import jax
import jax.numpy as jnp
from jax import lax
from jax.experimental import pallas as pl
from jax.experimental.pallas import tpu as pltpu

F32, BF16 = jnp.float32, jnp.bfloat16
HIGHEST = lax.Precision.HIGHEST
MESH = pl.DeviceIdType.MESH

D_MODEL = 1024
GLA_CHUNK = 64
GLA_GATE_TAU = 16.0
GLA_GATE_RANK = 16
HEAD_LANES = 128
ATTN_BLOCK = 128
DILATIONS = (1, 4, 16)
ALIBI_SLOPES = tuple(2.0 ** (-(h + 1)) for h in range(8))
D_FF = 2816
EPS = 1e-6
C_GQ, C_GK, C_GV, C_GR, C_AQ, C_AK, C_AV, C_LR, PROJ_W = 0, 256, 512, 1024, 1536, 2048, 2560, 3072, 3200
ADAM_LR, ADAM_B1, ADAM_B2, ADAM_EPS, ADAM_WD, ADAM_STEP = 0.001, 0.9, 0.999, 1e-08, 0.01, 10
VMEM_LIMIT_BYTES = 56 * 1024 * 1024
ROW_TILE = 512
ADAM_TILE = 256


def _params(*sem):
    return pltpu.CompilerParams(dimension_semantics=sem or None, vmem_limit_bytes=VMEM_LIMIT_BYTES)


def _nt(a, b):
    return lax.dot_general(a, b, (((1,), (1,)), ((), ())), preferred_element_type=F32)


def _tn(a, b):
    return lax.dot_general(a, b, (((0,), (0,)), ((), ())), preferred_element_type=F32)


def _nn(a, b, precision=None):
    return jnp.dot(a, b, preferred_element_type=F32, precision=precision)


def _split3(v):
    hi = v.astype(BF16)
    rest = v - hi.astype(F32)
    mid = rest.astype(BF16)
    return hi, mid, (rest - mid.astype(F32)).astype(BF16)


def _sum_right(v, ones):
    hi, mid, lo = _split3(v)
    return (_nn(lo, ones) + _nn(mid, ones)) + _nn(hi, ones)


def _sum_left(ones, v):
    hi, mid, lo = _split3(v)
    return (_nn(ones, lo) + _nn(ones, mid)) + _nn(ones, hi)


def _fold8(v):
    return v.reshape(v.shape[0] // 8, 8, v.shape[1]).sum(axis=0)


def _spread_total(ref):
    t = ref[...]
    ref[...] = jnp.broadcast_to(jnp.sum(t, axis=-2, keepdims=True), t.shape)


def _sigmoid(x):
    return 1.0 / (1.0 + jnp.exp(-x))


def _mm(a, b, *, ta=False, tb=False, out_dtype=F32, tm, tn, tk, shard_cols=False, also_bf16=False, name):
    (k_a, m) = a.shape if ta else a.shape[::-1]
    (k_b, n) = b.shape[::-1] if tb else b.shape
    assert k_a == k_b and m % tm == 0 and n % tn == 0 and k_a % tk == 0, (name, a.shape, b.shape)
    nk = k_a // tk
    assert nk == 1 or out_dtype == F32, name
    dims = (((0 if ta else 1,), (1 if tb else 0,)), ((), ()))

    def body(a_ref, b_ref, o_ref, *rounded):
        k = pl.program_id(2)
        part = lax.dot_general(a_ref[...].astype(BF16), b_ref[...].astype(BF16), dims, preferred_element_type=F32)
        if nk == 1:
            o_ref[...] = part.astype(out_dtype)
        else:
            @pl.when(k == 0)
            def _():
                o_ref[...] = part

            @pl.when(k > 0)
            def _():
                o_ref[...] += part

        if also_bf16:
            @pl.when(k == nk - 1)
            def _():
                rounded[0][...] = o_ref[...].astype(BF16)

    a_spec = pl.BlockSpec((tk, tm), lambda i, j, k: (k, i)) if ta else pl.BlockSpec((tm, tk), lambda i, j, k: (i, k))
    b_spec = pl.BlockSpec((tn, tk), lambda i, j, k: (j, k)) if tb else pl.BlockSpec((tk, tn), lambda i, j, k: (k, j))
    if shard_cols:
        o_spec, o_shape = pl.BlockSpec((None, tm, tn), lambda i, j, k: (j, i, 0)), (n // tn, m, tn)
    else:
        o_spec, o_shape = pl.BlockSpec((tm, tn), lambda i, j, k: (i, j)), (m, n)
    shapes = [jax.ShapeDtypeStruct(o_shape, out_dtype)] + ([jax.ShapeDtypeStruct(o_shape, BF16)] if also_bf16 else [])
    out = pl.pallas_call(
        body, name=name, grid=(m // tm, n // tn, nk), in_specs=[a_spec, b_spec], out_specs=[o_spec] * len(shapes),
        out_shape=shapes, compiler_params=_params("parallel", "parallel", "arbitrary"))(a, b)
    return out if also_bf16 else out[0]


def _norm_mod_fwd(x, branch, gate, gain, scale, shift, *, name):
    s, d = x.shape
    tm = ROW_TILE
    has_branch = branch is not None

    def body(*refs):
        if has_branch:
            x_ref, br_ref, gate_ref, gain_ref, sc_ref, sh_ref, x1_ref, h_ref, ht_ref = refs
            xv = x_ref[...] + gate_ref[...] * br_ref[...]
            x1_ref[...] = xv
        else:
            x_ref, gain_ref, sc_ref, sh_ref, h_ref, ht_ref = refs
            xv = x_ref[...]
        r = lax.rsqrt(jnp.mean(xv * xv, axis=-1, keepdims=True) + EPS)
        h = (xv * r) * gain_ref[...] * (1.0 + sc_ref[...]) + sh_ref[...]
        h_ref[...] = h.astype(BF16)
        ht_ref[...] = h.T.astype(BF16)

    row = pl.BlockSpec((tm, d), lambda i: (i, 0))
    col = pl.BlockSpec((d, tm), lambda i: (0, i))
    vec = pl.BlockSpec((1, d), lambda i: (0, 0))
    h_shapes = [jax.ShapeDtypeStruct((s, d), BF16), jax.ShapeDtypeStruct((d, s), BF16)]
    if has_branch:
        return pl.pallas_call(
            body, name=name, grid=(s // tm,), in_specs=[row, row, vec, vec, vec, vec], out_specs=[row, row, col],
            out_shape=[jax.ShapeDtypeStruct((s, d), F32)] + h_shapes,
            compiler_params=_params("parallel"))(x, branch, gate, gain, scale, shift)
    h, ht = pl.pallas_call(
        body, name=name, grid=(s // tm,), in_specs=[row, vec, vec, vec], out_specs=[row, col],
        out_shape=h_shapes, compiler_params=_params("parallel"))(x, gain, scale, shift)
    return x, h, ht


def _norm_mod_bwd(x, dh, dres, gain, scale, branch, gate, *, name):
    s, d = x.shape
    tm = ROW_TILE
    has_branch = branch is not None

    def body(*refs):
        if has_branch:
            x_ref, dh_ref, dres_ref, gain_ref, sc_ref, br_ref, gate_ref, dx_ref, dbr_ref, sums_ref = refs
        else:
            x_ref, dh_ref, dres_ref, gain_ref, sc_ref, dx_ref, sums_ref = refs
        i = pl.program_id(0)

        @pl.when(i == 0)
        def _():
            sums_ref[...] = jnp.zeros_like(sums_ref)

        xv, dhv = x_ref[...], dh_ref[...]
        r = lax.rsqrt(jnp.mean(xv * xv, axis=-1, keepdims=True) + EPS)
        xn = xv * r
        dxn = dhv * (gain_ref[...] * (1.0 + sc_ref[...]))
        dx = dres_ref[...] + r * (dxn - xn * jnp.mean(dxn * xn, axis=-1, keepdims=True))
        dx_ref[...] = dx
        sums_ref[0] += _fold8(dhv * xn)
        sums_ref[1] += _fold8(dhv)
        if has_branch:
            dbr_ref[...] = (gate_ref[...] * dx).astype(BF16)
            sums_ref[2] += _fold8(dx * br_ref[...])

        @pl.when(i == s // tm - 1)
        def _():
            _spread_total(sums_ref)

    row = pl.BlockSpec((tm, d), lambda i: (i, 0))
    vec = pl.BlockSpec((1, d), lambda i: (0, 0))
    sums = pl.BlockSpec((3, 8, d), lambda i: (0, 0, 0))
    sums_shape = jax.ShapeDtypeStruct((3, 8, d), F32)
    if has_branch:
        return pl.pallas_call(
            body, name=name, grid=(s // tm,), in_specs=[row, row, row, vec, vec, row, vec], out_specs=[row, row, sums],
            out_shape=[jax.ShapeDtypeStruct((s, d), F32), jax.ShapeDtypeStruct((s, d), BF16), sums_shape],
            compiler_params=_params("arbitrary"))(x, dh, dres, gain, scale, branch, gate)
    dx, sm = pl.pallas_call(
        body, name=name, grid=(s // tm,), in_specs=[row, row, row, vec, vec], out_specs=[row, sums],
        out_shape=[jax.ShapeDtypeStruct((s, d), F32), sums_shape],
        compiler_params=_params("arbitrary"))(x, dh, dres, gain, scale)
    return dx, None, sm


GLA_ROWS = 256


def _gla_block_setup(lr_ref, wg_ref, bg_ref):
    t, c = GLA_ROWS, GLA_CHUNK
    ri = lax.broadcasted_iota(jnp.int32, (t, t), 0)
    ci = lax.broadcasted_iota(jnp.int32, (t, t), 1)
    same = (ri // c) == (ci // c)
    causal, upper = same & (ci <= ri), same & (ci >= ri)
    z = _nn(lr_ref[...].astype(BF16), wg_ref[...]) + bg_ref[...]
    g = (jnp.minimum(z, 0.0) - jnp.log(1.0 + jnp.exp(-jnp.abs(z)))) * (1.0 / GLA_GATE_TAU)
    hi, mid, lo = _split3(g)
    total = lambda ones: (_nn(ones, lo) + _nn(ones, mid)) + _nn(ones, hi)
    return z, total(causal.astype(BF16)), total(same.astype(BF16)), causal, upper


def _chunks(t):
    return [t[i * GLA_CHUNK:(i + 1) * GLA_CHUNK] for i in range(GLA_ROWS // GLA_CHUNK)]


def _gla_fwd(proj, wg, bg, gn, *, name):
    s = proj.shape[0]
    tb, c = GLA_ROWS, GLA_CHUNK
    cb = tb // c

    def body(q_ref, k_ref, v_ref, r_ref, lr_ref, wg_ref, bg_ref, gn_ref, o_ref, y_ref, st_ref, state):
        i = pl.program_id(0)

        @pl.when(i == 0)
        def _():
            state[...] = jnp.zeros_like(state)

        low = lax.broadcasted_iota(jnp.int32, (tb, HEAD_LANES), 1) < 64
        masks = (low, jnp.logical_not(low))
        _, b, b_end, causal, _ = _gla_block_setup(lr_ref, wg_ref, bg_ref)
        pairs = []
        for p in range(2):
            cols = pl.ds(p * HEAD_LANES, HEAD_LANES)
            bp, bep = (t[:, p * HEAD_LANES:(p + 1) * HEAD_LANES] for t in (b, b_end))
            k = k_ref[:, cols]
            q_in = q_ref[:, cols] * 0.125 * jnp.exp(bp)
            k_out = (k * jnp.exp(-bp)).astype(BF16)
            k_end = k * jnp.exp(bep - bp)
            qms = [jnp.where(m, q_in, 0.0).astype(BF16) for m in masks]
            kes = [jnp.where(m, k_end, 0.0).astype(BF16) for m in masks]
            vs = [v_ref[:, pl.ds((2 * p + e) * HEAD_LANES, HEAD_LANES)].astype(BF16) for e in range(2)]
            grow = [_tn(v0, k0) + _tn(v1, k1) for v0, k0, v1, k1 in zip(_chunks(vs[0]), _chunks(kes[0]), _chunks(vs[1]), _chunks(kes[1]))]
            pairs.append((bep, k_out, qms, vs, grow))
        entering = [[], []]
        for p, (bep, _, _, _, grow) in enumerate(pairs):
            st = state[p]
            for ch in range(cb):
                entering[p].append(st)
                st_ref[ch, p] = st
                st = st * jnp.exp(bep[ch * c:ch * c + 1, :]) + grow[ch]
            state[p] = st
        for p, (_, k_out, qms, vs, _) in enumerate(pairs):
            for e in range(2):
                hc = pl.ds((2 * p + e) * HEAD_LANES, HEAD_LANES)
                a = jnp.where(causal, _nt(qms[e], k_out), 0.0).astype(BF16)
                carried = jnp.concatenate([_nt(qc, sc.astype(BF16)) for qc, sc in zip(_chunks(qms[e]), entering[p])], axis=0)
                o = _nn(a, vs[e]) + carried
                o_ref[:, hc] = o
                rr = r_ref[:, hc]
                on = o * lax.rsqrt(jnp.mean(o * o, axis=-1, keepdims=True) + EPS)
                y_ref[:, hc] = (on * gn_ref[...] * (rr * _sigmoid(rr))).astype(BF16)

    def col(width, at):
        return pl.BlockSpec((tb, width), lambda i: (i, at // width))

    full = lambda shape: pl.BlockSpec(shape, lambda i: tuple(0 for _ in shape))
    return pl.pallas_call(
        body, name=name, grid=(s // tb,),
        in_specs=[col(256, C_GQ), col(256, C_GK), col(512, C_GV), col(512, C_GR), col(128, C_LR),
                  full((HEAD_LANES, 256)), full((1, 256)), full((1, HEAD_LANES))],
        out_specs=[pl.BlockSpec((tb, 512), lambda i: (i, 0)), pl.BlockSpec((tb, 512), lambda i: (i, 0)),
                   pl.BlockSpec((cb, 2, HEAD_LANES, HEAD_LANES), lambda i: (i, 0, 0, 0))],
        out_shape=[jax.ShapeDtypeStruct((s, 512), F32), jax.ShapeDtypeStruct((s, 512), BF16),
                   jax.ShapeDtypeStruct((s // c, 2, HEAD_LANES, HEAD_LANES), F32)],
        scratch_shapes=[pltpu.VMEM((2, HEAD_LANES, HEAD_LANES), F32)],
        compiler_params=_params("arbitrary"))(proj, proj, proj, proj, proj, wg, bg, gn)


def _gla_bwd(proj, wg, bg, gn, o_raw, states, dmixed, *, name):
    s = proj.shape[0]
    tb, c = GLA_ROWS, GLA_CHUNK
    cb = tb // c
    nblk, nch = s // tb, s // c

    def body(q_ref, k_ref, v_ref, r_ref, lr_ref, wg_ref, bg_ref, gn_ref, o_ref, st_ref, stn_ref, dy_ref,
             dq_ref, dk_ref, dv_ref, dr_ref, dlr_ref, gwg_ref, sums_ref, dstate):
        i = pl.program_id(0)

        @pl.when(i == 0)
        def _():
            dstate[...] = jnp.zeros_like(dstate)
            gwg_ref[...] = jnp.zeros_like(gwg_ref)
            sums_ref[...] = jnp.zeros_like(sums_ref)

        low = lax.broadcasted_iota(jnp.int32, (tb, HEAD_LANES), 1) < 64
        masks = (low, jnp.logical_not(low))
        z, b, b_end, causal, upper = _gla_block_setup(lr_ref, wg_ref, bg_ref)
        lr_b = lr_ref[...].astype(BF16)
        dlr = jnp.zeros((tb, HEAD_LANES), F32)
        per_chunk = lambda rows, mats, fn: jnp.concatenate([fn(r, m.astype(BF16)) for r, m in zip(_chunks(rows), mats)], axis=0)
        pairs = []
        for p in range(2):
            cols = pl.ds(p * HEAD_LANES, HEAD_LANES)
            sl = slice(p * HEAD_LANES, (p + 1) * HEAD_LANES)
            bp, bep = b[:, sl], b_end[:, sl]
            e_in, e_out, e_end = jnp.exp(bp), jnp.exp(-bp), jnp.exp(bep - bp)
            q = q_ref[:, cols] * 0.125
            k = k_ref[:, cols]
            q_in, k_out, k_end = q * e_in, k * e_out, k * e_end
            qms = [jnp.where(m, q_in, 0.0).astype(BF16) for m in masks]
            kms_out = [jnp.where(m, k_out, 0.0).astype(BF16) for m in masks]
            kms_end = [jnp.where(m, k_end, 0.0).astype(BF16) for m in masks]
            vs, dos = [], []
            for e in range(2):
                hc = pl.ds((2 * p + e) * HEAD_LANES, HEAD_LANES)
                o, rr, dy = o_ref[:, hc], r_ref[:, hc], dy_ref[:, hc]
                sg = _sigmoid(rr)
                rs = lax.rsqrt(jnp.mean(o * o, axis=-1, keepdims=True) + EPS)
                on = o * rs
                t = dy * (rr * sg)
                sums_ref[1, :, hc] += _fold8(t * on)
                dn = t * gn_ref[...]
                dos.append((rs * (dn - on * jnp.mean(dn * on, axis=-1, keepdims=True))).astype(BF16))
                dr_ref[:, hc] = (dy * on * gn_ref[...] * (sg * (1.0 + rr * (1.0 - sg)))).astype(BF16)
                vs.append(v_ref[:, hc].astype(BF16))
            grow = [_tn(d0, q0) + _tn(d1, q1) for d0, q0, d1, q1 in zip(_chunks(dos[0]), _chunks(qms[0]), _chunks(dos[1]), _chunks(qms[1]))]
            pairs.append((bep, e_in, e_out, e_end, q, k, qms, kms_out, kms_end, vs, dos, grow))
        chains = []
        for p in range(2):
            bep, grow = pairs[p][0], pairs[p][-1]
            entering = [st_ref[ch, p] for ch in range(cb)]
            dst, leaving_grad = dstate[p], [None] * cb
            for ch in reversed(range(cb)):
                leaving_grad[ch] = dst
                dst = dst * jnp.exp(bep[ch * c:ch * c + 1, :]) + grow[ch]
            dstate[p] = dst
            chains.append((entering, leaving_grad))
        for p in range(2):
            cols = pl.ds(p * HEAD_LANES, HEAD_LANES)
            sl = slice(p * HEAD_LANES, (p + 1) * HEAD_LANES)
            _, e_in, e_out, e_end, q, k, qms, kms_out, kms_end, vs, dos, _ = pairs[p]
            entering, leaving_grad = chains[p]
            leaving = entering[1:] + [stn_ref[0, p]]
            felt = jnp.concatenate([jnp.broadcast_to(jnp.sum(dg_st * st, axis=0, keepdims=True), (c, HEAD_LANES))
                                    for dg_st, st in zip(leaving_grad, leaving)], axis=0)
            dq_in = jnp.zeros((tb, HEAD_LANES), F32)
            dk_out = jnp.zeros((tb, HEAD_LANES), F32)
            dk_end = jnp.zeros((tb, HEAD_LANES), F32)
            for e in range(2):
                hc = pl.ds((2 * p + e) * HEAD_LANES, HEAD_LANES)
                a = jnp.where(causal, _nt(qms[e], kms_out[e]), 0.0).astype(BF16)
                da = jnp.where(causal, _nt(dos[e], vs[e]), 0.0).astype(BF16)
                dv_ref[:, hc] = (_tn(a, dos[e]) + per_chunk(kms_end[e], leaving_grad, _nt)).astype(BF16)
                dq_in = dq_in + jnp.where(masks[e], per_chunk(dos[e], entering, _nn) + _nn(da, kms_out[e]), 0.0)
                dk_out = dk_out + _tn(da, qms[e])
                dk_end = dk_end + jnp.where(masks[e], per_chunk(vs[e], leaving_grad, _nn), 0.0)
            dq = dq_in * e_in
            dk = dk_out * e_out + dk_end * e_end
            dq_ref[:, cols] = (dq * 0.125).astype(BF16)
            dk_ref[:, cols] = dk.astype(BF16)
            dg = _sum_left(upper.astype(BF16), q * dq - k * dk) + felt
            dz = dg * (1.0 / GLA_GATE_TAU) * _sigmoid(-z[:, sl])
            dz_b = dz.astype(BF16)
            sums_ref[0, :, cols] += _fold8(dz)
            dlr = dlr + _nt(dz_b, wg_ref[:, cols])
            gwg_ref[:, cols] += _tn(lr_b, dz_b)
        dlr_ref[...] = dlr.astype(BF16)

        @pl.when(i == nblk - 1)
        def _():
            _spread_total(sums_ref)

    rev = lambda i: nblk - 1 - i

    def col(width, at):
        return pl.BlockSpec((tb, width), lambda i: (rev(i), at // width))

    full = lambda shape: pl.BlockSpec(shape, lambda i: tuple(0 for _ in shape))
    out_col = lambda width: pl.BlockSpec((tb, width), lambda i: (rev(i), 0))
    return pl.pallas_call(
        body, name=name, grid=(nblk,),
        in_specs=[col(256, C_GQ), col(256, C_GK), col(512, C_GV), col(512, C_GR), col(128, C_LR),
                  full((HEAD_LANES, 256)), full((1, 256)), full((1, HEAD_LANES)),
                  pl.BlockSpec((tb, 512), lambda i: (rev(i), 0)),
                  pl.BlockSpec((cb, 2, HEAD_LANES, HEAD_LANES), lambda i: (rev(i), 0, 0, 0)),
                  pl.BlockSpec((1, 2, HEAD_LANES, HEAD_LANES), lambda i: (jnp.minimum((rev(i) + 1) * cb, nch - 1), 0, 0, 0)),
                  pl.BlockSpec((tb, 512), lambda i: (rev(i), 0))],
        out_specs=[out_col(256), out_col(256), out_col(512), out_col(512), out_col(128),
                   full((HEAD_LANES, 256)), full((2, 8, 512))],
        out_shape=[jax.ShapeDtypeStruct((s, 256), BF16), jax.ShapeDtypeStruct((s, 256), BF16),
                   jax.ShapeDtypeStruct((s, 512), BF16), jax.ShapeDtypeStruct((s, 512), BF16),
                   jax.ShapeDtypeStruct((s, 128), BF16), jax.ShapeDtypeStruct((HEAD_LANES, 256), F32),
                   jax.ShapeDtypeStruct((2, 8, 512), F32)],
        scratch_shapes=[pltpu.VMEM((2, HEAD_LANES, HEAD_LANES), F32)],
        compiler_params=_params("arbitrary"))(proj, proj, proj, proj, proj, wg, bg, gn, o_raw, states, states, dmixed)


def _head_sums(v):
    ri = lax.broadcasted_iota(jnp.int32, (HEAD_LANES, HEAD_LANES), 0) // 64
    ci = lax.broadcasted_iota(jnp.int32, (HEAD_LANES, HEAD_LANES), 1) // 64
    ones = (ri == ci).astype(BF16)
    return jnp.concatenate([_sum_right(v[:, p * HEAD_LANES:(p + 1) * HEAD_LANES], ones) for p in range(4)], axis=1)


def _attn_prep(proj, qg, kg, *, name):
    s = proj.shape[0]
    tm = ROW_TILE

    def body(q_ref, k_ref, qg_ref, kg_ref, qa_ref, ka_ref):
        q, k = q_ref[...], k_ref[...]
        qr = lax.rsqrt(_head_sums(q * q) * (1.0 / 64) + EPS)
        kr = lax.rsqrt(_head_sums(k * k) * (1.0 / 64) + EPS)
        qa_ref[...] = q * qr * qg_ref[...] * 0.125
        ka_ref[...] = k * kr * kg_ref[...]

    col = lambda at: pl.BlockSpec((tm, 512), lambda i: (i, at // 512))
    vec = pl.BlockSpec((1, 512), lambda i: (0, 0))
    out = pl.BlockSpec((tm, 512), lambda i: (i, 0))
    return pl.pallas_call(
        body, name=name, grid=(s // tm,), in_specs=[col(C_AQ), col(C_AK), vec, vec], out_specs=[out] * 2,
        out_shape=[jax.ShapeDtypeStruct((s, 512), F32)] * 2, compiler_params=_params("parallel"))(proj, proj, qg, kg)


FAR = 1e30
LOG2E, LN2 = 1.4426950408889634, 0.6931471805599453


def _attn_distance(first):
    blk = ATTN_BLOCK
    iq = lax.broadcasted_iota(jnp.int32, (2 * blk, 2 * blk), 0) & (blk - 1)
    ik = lax.broadcasted_iota(jnp.int32, (2 * blk, 2 * blk), 1)
    rel = iq + blk - ik
    valid = (rel >= 0) & (rel <= blk) & (jnp.logical_not(first) | (ik >= blk))
    return jnp.where(valid, rel.astype(F32), FAR)


def _stack_heads(t2):
    low = lax.broadcasted_iota(jnp.int32, t2.shape, 1) < 64
    return jnp.concatenate([jnp.where(low, t2, 0.0), jnp.where(low, 0.0, t2)], axis=0).astype(BF16)


def _unstack_heads(t):
    blk = ATTN_BLOCK
    low = lax.broadcasted_iota(jnp.int32, (blk, HEAD_LANES), 1) < 64
    return jnp.where(low, t[0:blk], t[blk:2 * blk])


def _attn_scores(qs, kcat, slopes, dil, dist):
    top = lax.broadcasted_iota(jnp.int32, (2 * ATTN_BLOCK, 1), 0) < ATTN_BLOCK
    return _nt(qs, kcat) - jnp.where(top, slopes[0] * (dil * LOG2E), slopes[1] * (dil * LOG2E)) * dist


def _pair_slopes(p):
    if isinstance(p, int):
        return ALIBI_SLOPES[2 * p], ALIBI_SLOPES[2 * p + 1]
    pick = lambda e: jnp.where(p == 0, ALIBI_SLOPES[e], jnp.where(p == 1, ALIBI_SLOPES[2 + e],
                               jnp.where(p == 2, ALIBI_SLOPES[4 + e], ALIBI_SLOPES[6 + e])))
    return pick(0), pick(1)


ATTN_GROUP = 4


def _each(fn, *lists):
    return [fn(*args) for args in zip(*lists)]


def _attn_group_fwd(q2s, kcats, vcats, slopes, dil, dist):
    qs = _each(lambda q2: _stack_heads(q2 * LOG2E), q2s)
    sc = _each(lambda q, k, sl: _attn_scores(q, k, sl, dil, dist), qs, kcats, slopes)
    m = _each(lambda s: jnp.max(s, axis=-1, keepdims=True), sc)
    pr = _each(lambda s, mx: jnp.exp2(s - mx), sc, m)
    den = _each(lambda p: jnp.sum(p, axis=-1, keepdims=True), pr)
    o = _each(lambda p, v, d: _nn(p.astype(BF16), v) / d, pr, vcats, den)
    lse = _each(lambda mx, d, t: jnp.broadcast_to(mx + jnp.log2(d), t.shape), m, den, o)
    return _each(lambda t, l: (_unstack_heads(t), _unstack_heads(l)), o, lse)


def _attn_group_bwd(q2s, kcats, vcats, do2s, y2s, lse2s, slopes, dil, dist):
    lane = lax.broadcasted_iota(jnp.int32, (ATTN_BLOCK, HEAD_LANES), 1)
    low = lane < 64
    per_head = lambda t, pick: jnp.concatenate([jnp.sum(jnp.where(pick(0), t, 0.0), axis=-1, keepdims=True),
                                                jnp.sum(jnp.where(pick(1), t, 0.0), axis=-1, keepdims=True)], axis=0)
    lse = _each(lambda l: per_head(l, lambda e: lane == 64 * e), lse2s)
    delta = _each(lambda d, y: per_head(d * y, lambda e: low if e == 0 else jnp.logical_not(low)), do2s, y2s)
    qs = _each(lambda q2: _stack_heads(q2 * LOG2E), q2s)
    dos = _each(_stack_heads, do2s)
    sc = _each(lambda q, k, sl: _attn_scores(q, k, sl, dil, dist), qs, kcats, slopes)
    pr = _each(lambda s, l: jnp.exp2(s - l), sc, lse)
    dp = _each(_nt, dos, vcats)
    ds = _each(lambda p, d, dl: (p * (d - dl)).astype(BF16), pr, dp, delta)
    dq = _each(lambda d, k: _unstack_heads(_nn(d, k)), ds, kcats)
    dk = _each(lambda d, q: _tn(d, q) * LN2, ds, qs)
    dv = _each(lambda p, d: _tn(p.astype(BF16), d), pr, dos)
    return list(zip(dq, dk, dv))


def _attn_specs(dil):
    rows = ATTN_BLOCK * dil
    if dil == 1:
        cur = lambda at: pl.BlockSpec((rows, 512), lambda n: (n, at // 512))
        prev = lambda at: pl.BlockSpec((rows, 512), lambda n: (jnp.maximum(n - 1, 0), at // 512))
    else:
        cur = lambda at: pl.BlockSpec((rows, HEAD_LANES), lambda n, p: (n, at // HEAD_LANES + p))
        prev = lambda at: pl.BlockSpec((rows, HEAD_LANES), lambda n, p: (jnp.maximum(n - 1, 0), at // HEAD_LANES + p))
    return cur, prev


def _attn_loop(dil, one_group, p):
    if dil == 1:
        one_group([(slice(None), pl.ds(p * HEAD_LANES, HEAD_LANES), p) for p in range(ATTN_GROUP)])
    else:
        group = min(dil, ATTN_GROUP)

        def step(g, carry):
            one_group([(pl.ds(g * group + j, ATTN_BLOCK, stride=dil), slice(None), p) for j in range(group)])
            return carry

        if dil == group:
            step(0, 0)
        else:
            lax.fori_loop(0, dil // group, step, 0)


def _dil_attn_fwd(qa, ka, proj, dil, *, name):
    s = qa.shape[0]

    def body(q_ref, kp_ref, kc_ref, vp_ref, vc_ref, o_ref, lse_ref):
        dist = _attn_distance(pl.program_id(0) == 0)
        pair = None if dil == 1 else pl.program_id(1)

        def one_group(items):
            both = lambda a, b: [jnp.concatenate([a[rows, cols], b[rows, cols]], axis=0).astype(BF16) for rows, cols, _ in items]
            outs = _attn_group_fwd([q_ref[rows, cols] for rows, cols, _ in items], both(kp_ref, kc_ref), both(vp_ref, vc_ref),
                                   [_pair_slopes(p) for _, _, p in items], dil, dist)
            for (rows, cols, _), (o2, lse2) in zip(items, outs):
                o_ref[rows, cols] = o2
                lse_ref[rows, cols] = lse2

        _attn_loop(dil, one_group, pair)

    cur, prev = _attn_specs(dil)
    grid = (s // ATTN_BLOCK,) if dil == 1 else (s // (ATTN_BLOCK * dil), 4)
    return pl.pallas_call(
        body, name=name, grid=grid, in_specs=[cur(0), prev(0), cur(0), prev(C_AV), cur(C_AV)], out_specs=[cur(0), cur(0)],
        out_shape=[jax.ShapeDtypeStruct((s, 512), F32)] * 2,
        compiler_params=_params(*["parallel"] * len(grid)))(qa, ka, ka, proj, proj)


def _dense_attn_fwd_merge(qa, ka, proj, others, y_gla, *, name):
    s = qa.shape[0]
    blk = ATTN_BLOCK

    def body(q_ref, kp_ref, kc_ref, vp_ref, vc_ref, oa_ref, la_ref, ob_ref, lb_ref, yg_ref, mixed_ref, y_ref, lse_ref):
        dist = _attn_distance(pl.program_id(0) == 0)
        mixed_ref[:, 0:512] = yg_ref[...]

        def one_group(items):
            both = lambda a, b: [jnp.concatenate([a[rows, cols], b[rows, cols]], axis=0).astype(BF16) for rows, cols, _ in items]
            outs = _attn_group_fwd([q_ref[rows, cols] for rows, cols, _ in items], both(kp_ref, kc_ref), both(vp_ref, vc_ref),
                                   [_pair_slopes(p) for _, _, p in items], 1, dist)
            for (_, cols, p), (o2, l2) in zip(items, outs):
                la, lb = la_ref[:, cols], lb_ref[:, cols]
                m = jnp.maximum(jnp.maximum(l2, la), lb)
                w0, wa, wb = jnp.exp2(l2 - m), jnp.exp2(la - m), jnp.exp2(lb - m)
                zs = w0 + wa + wb
                y = (w0 * o2 + wa * oa_ref[:, cols] + wb * ob_ref[:, cols]) / zs
                y_ref[:, cols] = y
                lse_ref[:, cols] = m + jnp.log2(zs)
                mixed_ref[:, pl.ds(512 + p * HEAD_LANES, HEAD_LANES)] = y.astype(BF16)

        _attn_loop(1, one_group, None)

    cur, prev = _attn_specs(1)
    here = pl.BlockSpec((blk, 512), lambda n: (n, 0))
    (oa, la), (ob, lb) = others
    return pl.pallas_call(
        body, name=name, grid=(s // blk,),
        in_specs=[cur(0), prev(0), cur(0), prev(C_AV), cur(C_AV)] + [here] * 5,
        out_specs=[pl.BlockSpec((blk, 1024), lambda n: (n, 0)), here, here],
        out_shape=[jax.ShapeDtypeStruct((s, 1024), BF16), jax.ShapeDtypeStruct((s, 512), F32),
                   jax.ShapeDtypeStruct((s, 512), F32)],
        compiler_params=_params("parallel"))(qa, ka, ka, proj, proj, oa, la, ob, lb, y_gla)


def _dil_attn_bwd(qa, ka, proj, y_att, lse, dmixed, dil, *, name):
    s = qa.shape[0]
    blk, rows_per_step = ATTN_BLOCK, ATTN_BLOCK * dil
    nb = s // rows_per_step
    step_axis = 0 if dil == 1 else 1

    def body(q_ref, kp_ref, kc_ref, vp_ref, vc_ref, y_ref, lse_ref, do_ref, dq_ref, dk_ref, dv_ref, dk_own, dv_own):
        n = pl.program_id(step_axis)
        pair = None if dil == 1 else pl.program_id(0)
        dist = _attn_distance(n == 0)

        @pl.when(n == 0)
        def _():
            dk_own[...] = jnp.zeros_like(dk_own)
            dv_own[...] = jnp.zeros_like(dv_own)

        def one_group(items):
            both = lambda a, b: [jnp.concatenate([a[rows, cols], b[rows, cols]], axis=0).astype(BF16) for rows, cols, _ in items]
            at = lambda ref: [ref[rows, cols] for rows, cols, _ in items]
            outs = _attn_group_bwd(at(q_ref), both(kp_ref, kc_ref), both(vp_ref, vc_ref), at(do_ref), at(y_ref), at(lse_ref),
                                   [_pair_slopes(p) for _, _, p in items], dil, dist)
            for (rows, cols, _), (dq, dk, dv) in zip(items, outs):
                dq_ref[rows, cols] = dq
                dk_ref[rows, cols] = dk_own[rows, cols] + dk[0:blk]
                dv_ref[rows, cols] = dv_own[rows, cols] + dv[0:blk]
                dk_own[rows, cols] = dk[blk:2 * blk]
                dv_own[rows, cols] = dv[blk:2 * blk]

        _attn_loop(dil, one_group, pair)

    width = 512 if dil == 1 else HEAD_LANES

    def spec(at, row_of):
        if dil == 1:
            return pl.BlockSpec((rows_per_step, width), lambda n: (row_of(n), at // width))
        return pl.BlockSpec((rows_per_step, width), lambda p, n: (row_of(n), at // width + p))

    cur = lambda at: spec(at, lambda n: n)
    prev = lambda at: spec(at, lambda n: jnp.maximum(n - 1, 0))
    own = spec(0, lambda n: 0)
    grid = (nb,) if dil == 1 else (4, nb)
    sems = ("arbitrary",) if dil == 1 else ("parallel", "arbitrary")
    dq, dk, dv, dk_last, dv_last = pl.pallas_call(
        body, name=name, grid=grid,
        in_specs=[cur(0), prev(0), cur(0), prev(C_AV), cur(C_AV), cur(0), cur(0), cur(512)],
        out_specs=[cur(0), prev(0), prev(0), own, own],
        out_shape=[jax.ShapeDtypeStruct((s, 512), F32)] * 3 + [jax.ShapeDtypeStruct((rows_per_step, 512), F32)] * 2,
        compiler_params=_params(*sems),
    )(qa, ka, ka, proj, proj, y_att, lse, dmixed)
    return dq, dk.at[s - rows_per_step:].set(dk_last), dv.at[s - rows_per_step:].set(dv_last)


def _attn_post(parts, proj, qg, kg, *, name):
    s = proj.shape[0]
    tm = ROW_TILE
    nblk = s // tm

    def body(*refs):
        ins, (q_ref, k_ref, qg_ref, kg_ref, dq_out, dk_out, dv_out, sums_ref) = refs[:9], refs[9:]
        i = pl.program_id(0)

        @pl.when(i == 0)
        def _():
            sums_ref[...] = jnp.zeros_like(sums_ref)

        dq = (ins[0][...] + ins[3][...]) + ins[6][...]
        dk = (ins[1][...] + ins[4][...]) + ins[7][...]
        dv = (ins[2][...] + ins[5][...]) + ins[8][...]
        dv_out[...] = dv.astype(BF16)
        for row, (x_ref, g_ref, dy, out, post) in enumerate(((q_ref, qg_ref, dq, dq_out, 0.125), (k_ref, kg_ref, dk, dk_out, 1.0))):
            x = x_ref[...]
            rs = lax.rsqrt(_head_sums(x * x) * (1.0 / 64) + EPS)
            xn = x * rs
            dy = dy * post
            sums_ref[row] += _fold8(dy * xn)
            dn = dy * g_ref[...]
            out[...] = (rs * (dn - xn * (_head_sums(dn * xn) * (1.0 / 64)))).astype(BF16)

        @pl.when(i == nblk - 1)
        def _():
            _spread_total(sums_ref)

    here = pl.BlockSpec((tm, 512), lambda i: (i, 0))
    col = lambda at: pl.BlockSpec((tm, 512), lambda i: (i, at // 512))
    vec = pl.BlockSpec((1, 512), lambda i: (0, 0))
    return pl.pallas_call(
        body, name=name, grid=(nblk,), in_specs=[here] * 9 + [col(C_AQ), col(C_AK), vec, vec],
        out_specs=[here, here, here, pl.BlockSpec((2, 8, 512), lambda i: (0, 0, 0))],
        out_shape=[jax.ShapeDtypeStruct((s, 512), BF16)] * 3 + [jax.ShapeDtypeStruct((2, 8, 512), F32)],
        compiler_params=_params("arbitrary"))(*[t for part in parts for t in part], proj, proj, qg, kg)


FFN_TM, FFN_TN = 512, 1408
HALO = 16


def _conv3(u_ref, halo_ref, w_ref, b_ref, first):
    u = u_ref[...].astype(F32)
    ext = jnp.concatenate([jnp.where(first, 0.0, halo_ref[...].astype(F32)), u], axis=0)
    u1 = pltpu.roll(ext, 1, 0)[HALO:]
    u2 = pltpu.roll(ext, 2, 0)[HALO:]
    return b_ref[...] + w_ref[0:1, :] * u2 + w_ref[1:2, :] * u1 + w_ref[2:3, :] * u


def _ffn_specs(tm, tn):
    nj = D_FF // tn
    blk = lambda half: pl.BlockSpec((tm, tn), lambda j, i: (i, j + half * nj))
    halo = lambda half: pl.BlockSpec((HALO, tn), lambda j, i: (jnp.maximum(i * (tm // HALO) - 1, 0), j + half * nj))
    wspec = lambda half: pl.BlockSpec((3, tn), lambda j, i: (0, j + half * nj))
    bspec = lambda half: pl.BlockSpec((1, tn), lambda j, i: (0, j + half * nj))
    return [blk(0), halo(0), blk(1), halo(1), wspec(0), wspec(1), bspec(0), bspec(1)]


def _conv_swiglu_fwd(u, conv_w, conv_b, *, name):
    s = u.shape[0]
    tm, tn = FFN_TM, FFN_TN

    def body(ug_ref, hg_ref, uv_ref, hv_ref, wg_ref, wv_ref, bg_ref, bv_ref, act_ref, uc_ref):
        first = pl.program_id(1) == 0
        cg = _conv3(ug_ref, hg_ref, wg_ref, bg_ref, first)
        cv = _conv3(uv_ref, hv_ref, wv_ref, bv_ref, first)
        act_ref[...] = (cg * _sigmoid(cg) * cv).astype(BF16)
        uc_ref[0] = cg.astype(BF16)
        uc_ref[1] = cv.astype(BF16)

    return pl.pallas_call(
        body, name=name, grid=(D_FF // tn, s // tm), in_specs=_ffn_specs(tm, tn),
        out_specs=[pl.BlockSpec((tm, tn), lambda j, i: (i, j)), pl.BlockSpec((2, tm, tn), lambda j, i: (0, i, j))],
        out_shape=[jax.ShapeDtypeStruct((s, D_FF), BF16), jax.ShapeDtypeStruct((2, s, D_FF), BF16)],
        compiler_params=_params("parallel", "parallel"))(u, u, u, u, conv_w, conv_w, conv_b, conv_b)


def _swiglu_bwd(uc, dact, *, name):
    _, s, _ = uc.shape
    tm, tn = FFN_TM, FFN_TN

    def body(uc_ref, da_ref, duc_ref, sums_ref):
        i = pl.program_id(1)

        @pl.when(i == 0)
        def _():
            sums_ref[...] = jnp.zeros_like(sums_ref)

        cg, cv, da = uc_ref[0].astype(F32), uc_ref[1].astype(F32), da_ref[...].astype(F32)
        sg = _sigmoid(cg)
        dg = da * cv * (sg * (1.0 + cg * (1.0 - sg)))
        dv = da * (cg * sg)
        duc_ref[0] = dg.astype(BF16)
        duc_ref[1] = dv.astype(BF16)
        sums_ref[0] += _fold8(dg)
        sums_ref[1] += _fold8(dv)

        @pl.when(i == s // tm - 1)
        def _():
            _spread_total(sums_ref)

    pair = pl.BlockSpec((2, tm, tn), lambda j, i: (0, i, j))
    return pl.pallas_call(
        body, name=name, grid=(D_FF // tn, s // tm), in_specs=[pair, pl.BlockSpec((tm, tn), lambda j, i: (i, j))],
        out_specs=[pair, pl.BlockSpec((2, 8, tn), lambda j, i: (0, 0, j))],
        out_shape=[jax.ShapeDtypeStruct((2, s, D_FF), BF16), jax.ShapeDtypeStruct((2, 8, D_FF), F32)],
        compiler_params=_params("parallel", "arbitrary"))(uc, dact)


def _conv_bwd(duc, u, conv_w, *, name):
    _, s, _ = duc.shape
    tm, tn = FFN_TM, FFN_TN
    nj, ni = D_FF // tn, s // tm

    def body(d_ref, halo_ref, u_ref, w_ref, du_ref, sums_ref):
        i = pl.program_id(2)

        @pl.when(i == 0)
        def _():
            sums_ref[...] = jnp.zeros_like(sums_ref)

        d = d_ref[0].astype(F32)
        ext = jnp.concatenate([d, jnp.where(i == ni - 1, 0.0, halo_ref[0].astype(F32))], axis=0)
        n = tm + HALO
        d1 = pltpu.roll(ext, n - 1, 0)[:tm]
        d2 = pltpu.roll(ext, n - 2, 0)[:tm]
        du_ref[...] = (w_ref[2:3, :] * d + w_ref[1:2, :] * d1 + w_ref[0:1, :] * d2).astype(BF16)
        uv = u_ref[...].astype(F32)
        for t, shifted in enumerate((d2, d1, d)):
            sums_ref[0, t] += _fold8(shifted * uv)

        @pl.when(i == ni - 1)
        def _():
            _spread_total(sums_ref)

    return pl.pallas_call(
        body, name=name, grid=(2, nj, ni),
        in_specs=[pl.BlockSpec((1, tm, tn), lambda g, j, i: (g, i, j)),
                  pl.BlockSpec((1, HALO, tn), lambda g, j, i: (g, jnp.minimum((i + 1) * (tm // HALO), s // HALO - 1), j)),
                  pl.BlockSpec((tm, tn), lambda g, j, i: (i, g * nj + j)),
                  pl.BlockSpec((3, tn), lambda g, j, i: (0, g * nj + j))],
        out_specs=[pl.BlockSpec((tm, tn), lambda g, j, i: (i, g * nj + j)),
                   pl.BlockSpec((1, 3, 8, tn), lambda g, j, i: (g, 0, 0, j))],
        out_shape=[jax.ShapeDtypeStruct((s, 2 * D_FF), BF16), jax.ShapeDtypeStruct((2, 3, 8, D_FF), F32)],
        compiler_params=_params("parallel", "parallel", "arbitrary"))(duc, duc, u, conv_w)


def _loss_head(x1, ffn, gate, target, *, name):
    s, d = x1.shape
    tm = ROW_TILE

    def body(x_ref, f_ref, g_ref, t_ref, dy_ref, df_ref, sums_ref):
        i = pl.program_id(0)

        @pl.when(i == 0)
        def _():
            sums_ref[...] = jnp.zeros_like(sums_ref)

        f = f_ref[...]
        err = x_ref[...] + g_ref[...] * f - t_ref[...]
        dy = err * (1.0 / d)
        dy_ref[...] = dy
        df_ref[...] = (g_ref[...] * dy).astype(BF16)
        sums_ref[0] += _fold8(dy * f)
        sums_ref[1] += _fold8(err * err)

        @pl.when(i == s // tm - 1)
        def _():
            _spread_total(sums_ref)

    row = pl.BlockSpec((tm, d), lambda i: (i, 0))
    return pl.pallas_call(
        body, name=name, grid=(s // tm,), in_specs=[row, row, pl.BlockSpec((1, d), lambda i: (0, 0)), row],
        out_specs=[row, row, pl.BlockSpec((2, 8, d), lambda i: (0, 0, 0))],
        out_shape=[jax.ShapeDtypeStruct((s, d), F32), jax.ShapeDtypeStruct((s, d), BF16), jax.ShapeDtypeStruct((2, 8, d), F32)],
        compiler_params=_params("arbitrary"))(x1, ffn, gate, target)


def _adamw(w, g, m, v, *, name):
    rows, cols = w.shape
    split = isinstance(g, tuple)
    if rows % 8 == 0 or rows <= ADAM_TILE:
        span = rows // 2 if split else rows
        tm = next((t for t in range(ADAM_TILE, 7, -8) if span % t == 0), span)
        shape, at, steps, per_half = (tm, cols), (lambda i: (i, 0)), rows // tm, span // tm
    else:
        shape, at, steps, per_half = (rows, ADAM_TILE), (lambda i: (0, i)), cols // ADAM_TILE, cols // ADAM_TILE // 2

    def update(gv, w_ref, m_ref, v_ref, d_ref, mo_ref, vo_ref):
        mn = ADAM_B1 * m_ref[...] + (1.0 - ADAM_B1) * gv
        vn = ADAM_B2 * v_ref[...] + (1.0 - ADAM_B2) * (gv * gv)
        m_hat = mn / (1.0 - ADAM_B1 ** ADAM_STEP)
        v_hat = vn / (1.0 - ADAM_B2 ** ADAM_STEP)
        d_ref[...] = -ADAM_LR * (m_hat / (jnp.sqrt(v_hat) + ADAM_EPS) + ADAM_WD * w_ref[...])
        mo_ref[...] = mn
        vo_ref[...] = vn

    out_shape = [jax.ShapeDtypeStruct((rows, cols), F32)] * (4 if split else 3)
    if not split:
        def body(w_ref, g_ref, m_ref, v_ref, d_ref, mo_ref, vo_ref):
            update(g_ref[...], w_ref, m_ref, v_ref, d_ref, mo_ref, vo_ref)

        blk = pl.BlockSpec(shape, at)
        return pl.pallas_call(body, name=name, grid=(steps,), in_specs=[blk] * 4, out_specs=[blk] * 3, out_shape=out_shape,
                              compiler_params=_params("parallel"))(w, g, m, v)

    mine, other, core = g

    def body(core_ref, w_ref, mine_ref, other_ref, m_ref, v_ref, d_ref, mo_ref, vo_ref, g_ref):
        gv = jnp.where(pl.program_id(0) // per_half == core_ref[0], mine_ref[...], other_ref[...])
        g_ref[...] = gv
        update(gv, w_ref, m_ref, v_ref, d_ref, mo_ref, vo_ref)

    blk = pl.BlockSpec(shape, lambda i, core_ref: at(i))
    half = pl.BlockSpec(shape, lambda i, core_ref: at(i % per_half))
    return pl.pallas_call(
        body, name=name, out_shape=out_shape, compiler_params=_params("parallel"),
        grid_spec=pltpu.PrefetchScalarGridSpec(num_scalar_prefetch=1, grid=(steps,), in_specs=[blk, half, half, blk, blk],
                                               out_specs=[blk] * 4))(core, w, mine, other, m, v)


def _colsum(t):
    return t[..., 0, :]


def _in_proj_layout(w_in):
    pad = jnp.zeros((w_in.shape[0], PROJ_W - C_LR - GLA_GATE_RANK), w_in.dtype)
    return jnp.concatenate([w_in[:, :1536], w_in[:, 1552:], w_in[:, 1536:1552], pad], axis=1)


def _in_proj_grad_layout(g):
    return jnp.concatenate([g[:, :1536], g[:, C_LR:C_LR + GLA_GATE_RANK], g[:, 1536:C_LR]], axis=1)


def _gate_layout(gla_w_gate):
    return jnp.pad(gla_w_gate, ((0, HEAD_LANES - GLA_GATE_RANK), (0, 0))).astype(BF16)


def _local_step(x, target, mod, wi, wo, ffn_weights, ffn_grads_ready, attn_grads_ready, conv_w, conv_b, wg, bg, gn, qg, kg, n1g, n2g):
    d = D_MODEL
    sh1, sc1, g1, sh2, sc2, g2 = [mod[:, i * d:(i + 1) * d] for i in range(6)]
    qg8, kg8 = jnp.tile(qg, (1, 8)), jnp.tile(kg, (1, 8))

    _, h1, h1_t = _norm_mod_fwd(x, None, None, n1g, sc1, sh1, name="norm1_fwd")
    proj = _mm(h1, wi, tm=1024, tn=PROJ_W, tk=d, name="in_proj")
    o_raw, y_gla, states = _gla_fwd(proj, wg, bg, gn, name="gla_fwd")
    qa, ka = _attn_prep(proj, qg8, kg8, name="attn_prep")
    sparse = [_dil_attn_fwd(qa, ka, proj, dil, name=f"attn_fwd_d{dil}") for dil in DILATIONS[1:]]
    mixed, y_att, lse = _dense_attn_fwd_merge(qa, ka, proj, sparse, y_gla, name="attn_fwd_d1_merge")
    attn_out = _mm(mixed, wo, tm=1024, tn=d, tk=d, name="out_proj")
    x1, h2, h2_t = _norm_mod_fwd(x, attn_out, g1, n2g, sc2, sh2, name="norm2_fwd")
    wup, wdown = ffn_weights(h2)
    u = _mm(h2, wup, out_dtype=BF16, tm=1024, tn=D_FF, tk=d, name="up_proj")
    act, uc = _conv_swiglu_fwd(u, conv_w, conv_b, name="conv_swiglu_fwd")
    ffn = _mm(act, wdown, tm=1024, tn=d, tk=D_FF, name="down_proj")
    dy, dffn, head_sums = _loss_head(x1, ffn, g2, target, name="loss_head")

    dact = _mm(dffn, wdown, tb=True, out_dtype=BF16, tm=1024, tn=D_FF, tk=d, name="down_proj_dx")
    g_wdown, g_wdown_b = _mm(act, dffn, ta=True, tm=1408, tn=d, tk=2048, also_bf16=True, name="down_proj_dw")
    duc, bias_sums = _swiglu_bwd(uc, dact, name="swiglu_bwd")
    du, tap_sums = _conv_bwd(duc, u, conv_w, name="conv_bwd")
    dh2 = _mm(du, wup, tb=True, tm=1024, tn=d, tk=D_FF, name="up_proj_dx")
    g_wup, g_wup_b = _mm(h2_t, du, tm=d, tn=1408, tk=2048, shard_cols=True, also_bf16=True, name="up_proj_dw")
    token = ffn_grads_ready(g_wup_b, g_wdown_b)
    g1_late = g1 if token is None else g1 + token[0:1, 0:1]
    dx1, dao, n2_sums = _norm_mod_bwd(x1, dh2, dy, n2g, sc2, attn_out, g1_late, name="norm2_bwd")

    dmixed = _mm(dao, wo, tb=True, tm=1024, tn=d, tk=d, name="out_proj_dx")
    g_wo = _mm(mixed, dao, ta=True, tm=d, tn=d, tk=1024, name="out_proj_dw")
    dgq, dgk, dgv, dgr, dlr, g_wg, gla_sums = _gla_bwd(proj, wg, bg, gn, o_raw, states, dmixed, name="gla_bwd")
    parts = [_dil_attn_bwd(qa, ka, proj, y_att, lse, dmixed, dil, name=f"attn_bwd_d{dil}") for dil in DILATIONS]
    daq, dak, dav, qk_sums = _attn_post(parts, proj, qg8, kg8, name="attn_post")
    dproj = jnp.concatenate([dgq, dgk, dgv, dgr, daq, dak, dav, dlr], axis=1)
    g_wi = _mm(h1_t, dproj, tm=512, tn=PROJ_W, tk=2048, name="in_proj_dw")
    token = attn_grads_ready(g_wi, g_wo)
    wi_late = wi if token is None else wi + token[0:1, 0:1].astype(BF16)
    dh1 = _mm(dproj, wi_late, tb=True, tm=1024, tn=d, tk=PROJ_W, name="in_proj_dx")
    grad_x, _, n1_sums = _norm_mod_bwd(x, dh1, dx1, n1g, sc1, None, None, name="norm1_bwd")

    n1, n2, hs, taps, cb = _colsum(n1_sums), _colsum(n2_sums), _colsum(head_sums), _colsum(tap_sums), _colsum(bias_sums)
    gs, qs = _colsum(gla_sums), _colsum(qk_sums)
    dmod = jnp.concatenate([n1[1], n1[0] * n1g[0], n2[2], n2[1], n2[0] * n2g[0], hs[0]])
    small = dict(
        dmod=dmod,
        norm1_g=n1[0] * (1.0 + sc1[0]), norm2_g=n2[0] * (1.0 + sc2[0]),
        gla_w_gate=g_wg[:GLA_GATE_RANK], gla_b_gate=gs[0, :256], gla_norm_g=gs[1].reshape(4, 128).sum(axis=0),
        q_norm_g=qs[0].reshape(8, 64).sum(axis=0), k_norm_g=qs[1].reshape(8, 64).sum(axis=0),
        conv_w=jnp.concatenate([taps[0], taps[1]], axis=1), conv_b=jnp.concatenate([cb[0], cb[1]]),
    )
    return head_sums[1], grad_x, (g_wi, g_wo, g_wup, g_wdown), small


N_DEV, N_CHIP = 8, 4
ANY = pl.BlockSpec(memory_space=pl.ANY)
VMEM_SPEC = pl.BlockSpec(memory_space=pltpu.VMEM)


def _place():
    x, y, c = lax.axis_index("x"), lax.axis_index("y"), lax.axis_index("c")
    other_chips = [(1 - x, y), (x, 1 - y), (1 - x, 1 - y)]
    return x, y, c, (x, y, 1 - c), other_chips


def _all_gather_small(v, *, name):
    m, n = v.shape

    def body(v_ref, out_ref, send_sems, recv_sems, local_sem):
        x, y, c, sibling, chips = _place()
        me = (x, y, c)

        def rows(px, py, pc):
            return out_ref.at[pl.ds((4 * px + 2 * py + pc) * m, m), :]

        def copy(k, block, to, src=None):
            return pltpu.make_async_remote_copy(
                src_ref=rows(*block) if src is None else src, dst_ref=rows(*block), send_sem=send_sems.at[k],
                recv_sem=recv_sems.at[k], device_id=to, device_id_type=MESH)

        mine = pltpu.make_async_copy(v_ref, rows(*me), local_sem)
        mine.start()
        first = [copy(0, me, sibling, src=v_ref)]
        first += [copy(1 + j, me, (*chip, c), src=v_ref) for j, chip in enumerate(chips)]
        for cp in first:
            cp.start()
        passed = [copy(4 + j, (*chip, c), sibling) for j, chip in enumerate(chips)]
        for j, chip in enumerate(chips):
            copy(1 + j, (*chip, c), me).wait_recv()
            passed[j].start()
        copy(0, sibling, me).wait_recv()
        for j, chip in enumerate(chips):
            copy(4 + j, (*chip, 1 - c), me).wait_recv()
        for cp in first + passed:
            cp.wait_send()
        mine.wait()

    return pl.pallas_call(
        body, name=name, out_shape=jax.ShapeDtypeStruct((N_DEV * m, n), v.dtype), in_specs=[VMEM_SPEC], out_specs=VMEM_SPEC,
        scratch_shapes=[pltpu.SemaphoreType.DMA((7,)), pltpu.SemaphoreType.DMA((7,)), pltpu.SemaphoreType.DMA],
    )(v)


def _gather_weight_shards(shards, *, name):
    nw = len(shards)

    def body(*refs):
        srcs, outs, (send_sems, recv_sems) = refs[:nw], refs[nw:2 * nw], refs[2 * nw:]
        x, y, c, sibling, chips = _place()
        index = lambda chip: 2 * chip[0] + chip[1]

        def copy(w, k, src, dst, to):
            return pltpu.make_async_remote_copy(src_ref=src, dst_ref=dst, send_sem=send_sems.at[6 * w + k],
                                                recv_sem=recv_sems.at[6 * w + k], device_id=to, device_id_type=MESH)

        sent = []
        for w, (src_ref, out_ref) in enumerate(zip(srcs, outs)):
            for k, chip in enumerate(chips):
                sent.append(copy(w, k, src_ref.at[c], out_ref.at[2 * x + y, c], (*chip, c)))
                sent[-1].start()
        for w, out_ref in enumerate(outs):
            for k, chip in enumerate(chips):
                landed = out_ref.at[index(chip), c]
                copy(w, k, landed, landed, (*chip, c)).wait_recv()
                sent.append(copy(w, 3 + k, landed, landed, sibling))
                sent[-1].start()
        for w, out_ref in enumerate(outs):
            for k, chip in enumerate(chips):
                passed_on = out_ref.at[index(chip), 1 - c]
                copy(w, 3 + k, passed_on, passed_on, sibling).wait_recv()
        for cp in sent:
            cp.wait_send()

    return pl.pallas_call(
        body, name=name, out_shape=[jax.ShapeDtypeStruct((N_CHIP, *s.shape), s.dtype) for s in shards],
        in_specs=[ANY] * nw, out_specs=[ANY] * nw,
        scratch_shapes=[pltpu.SemaphoreType.DMA((6 * nw,)), pltpu.SemaphoreType.DMA((6 * nw,))],
    )(*shards)


HBM_SPEC = pl.BlockSpec(memory_space=pltpu.HBM)
SEM_SPEC = pl.BlockSpec(memory_space=pltpu.SEMAPHORE)
DATAFLOW_EFFECT = pltpu.SideEffectType.DATAFLOW_SIDE_EFFECTING


def _late_copies(srcs, lands, send_sems, recv_sems):
    x, y, c, _, chips = _place()
    return [pltpu.make_async_remote_copy(
        src_ref=src.at[c], dst_ref=land.at[2 * x + y, c], send_sem=send_sems.at[6 * w + 2 * r + core],
        recv_sem=recv_sems.at[6 * w + 2 * r + c], device_id=(*chip, core), device_id_type=MESH)
        for w, (src, land) in enumerate(zip(srcs, lands)) for r, chip in enumerate(chips) for core in range(2)]


def _gather_late_start(own, after, *, name):
    nw = len(own)

    def body(*refs):
        srcs, lands, send_sems, recv_sems, token = refs[:nw], refs[nw:2 * nw], refs[2 * nw + 1], refs[2 * nw + 2], refs[-1]
        for cp in _late_copies(srcs, lands, send_sems, recv_sems):
            cp.start()
        token[...] = jnp.zeros_like(token)

    lands = [pltpu.with_memory_space_constraint(lax.empty((N_CHIP, *s.shape), s.dtype), pltpu.HBM) for s in own]
    own = [pltpu.with_memory_space_constraint(s, pltpu.HBM) for s in own]
    out = pl.pallas_call(
        body, name=name,
        out_shape=(pltpu.SemaphoreType.DMA((6 * nw,)), pltpu.SemaphoreType.DMA((6 * nw,)),
                   *[pltpu.HBM(s.shape, s.dtype) for s in own], *[pltpu.HBM(s.shape, s.dtype) for s in lands],
                   jax.ShapeDtypeStruct((8, 128), F32)),
        in_specs=[HBM_SPEC] * (2 * nw) + [ANY], out_specs=(SEM_SPEC, SEM_SPEC, *[HBM_SPEC] * (2 * nw), VMEM_SPEC),
        input_output_aliases={i: 2 + i for i in range(2 * nw)},
        compiler_params=pltpu.CompilerParams(has_side_effects=DATAFLOW_EFFECT))(*own, *lands, after)
    return out[0], out[1], out[2:2 + nw], out[2 + nw:2 + 2 * nw], out[-1]


def _gather_late_wait(send_sems, recv_sems, own, lands, after, *, name):
    nw = len(own)

    def body(*refs):
        srcs, lands_in, send_sems, recv_sems = refs[:nw], refs[nw:2 * nw], refs[2 * nw], refs[2 * nw + 1]
        x, y, c, _, chips = _place()
        for cp in _late_copies(srcs, lands_in, send_sems, recv_sems):
            cp.wait_send()
        for w, (src, land) in enumerate(zip(srcs, lands_in)):
            for r, chip in enumerate(chips):
                for core in range(2):
                    pltpu.make_async_remote_copy(
                        src_ref=src.at[c], dst_ref=land.at[2 * chip[0] + chip[1], core], send_sem=send_sems.at[6 * w + 2 * r + core],
                        recv_sem=recv_sems.at[6 * w + 2 * r + core], device_id=(*chip, core), device_id_type=MESH).wait_recv()

    out = pl.pallas_call(
        body, name=name, out_shape=(*[pltpu.HBM(s.shape, s.dtype) for s in own], *[pltpu.HBM(s.shape, s.dtype) for s in lands]),
        in_specs=[HBM_SPEC] * (2 * nw) + [SEM_SPEC, SEM_SPEC, ANY], out_specs=tuple([HBM_SPEC] * (2 * nw)),
        input_output_aliases={i: i for i in range(2 * nw)},
        compiler_params=pltpu.CompilerParams(has_side_effects=DATAFLOW_EFFECT))(*own, *lands, send_sems, recv_sems, after)
    return out[:nw], out[nw:]


def _direct_reduce_copies(srcs, lands, send_sems, recv_sems):
    x, y, c, _, _ = _place()
    cps = []
    for w, (src, land) in enumerate(zip(srcs, lands)):
        for rel in range(1, N_DEV):
            tx, ty, tc = (1 - x if rel & 4 else x), (1 - y if rel & 2 else y), (1 - c if rel & 1 else c)
            cps.append(pltpu.make_async_remote_copy(
                src_ref=src.at[2 * tx + ty, tc], dst_ref=land.at[rel - 1], send_sem=send_sems.at[7 * w + rel - 1],
                recv_sem=recv_sems.at[7 * w + rel - 1], device_id=(tx, ty, tc), device_id_type=MESH))
    return cps


def _direct_reduce_start(grads, *, name):
    nw = len(grads)

    def body(*refs):
        srcs, lands, send_sems, recv_sems, token = refs[:nw], refs[nw:2 * nw], refs[2 * nw], refs[2 * nw + 1], refs[-1]
        for cp in _direct_reduce_copies(srcs, lands, send_sems, recv_sems):
            cp.start()
        token[...] = jnp.zeros_like(token)

    lands = [pltpu.with_memory_space_constraint(lax.empty((N_DEV - 1, *g.shape[2:]), g.dtype), pltpu.HBM) for g in grads]
    grads = [pltpu.with_memory_space_constraint(g, pltpu.HBM) for g in grads]
    out = pl.pallas_call(
        body, name=name,
        out_shape=(pltpu.SemaphoreType.DMA((7 * nw,)), pltpu.SemaphoreType.DMA((7 * nw,)),
                   *[pltpu.HBM(g.shape, g.dtype) for g in grads], *[pltpu.HBM(t.shape, t.dtype) for t in lands],
                   jax.ShapeDtypeStruct((8, 128), F32)),
        in_specs=[HBM_SPEC] * (2 * nw), out_specs=(SEM_SPEC, SEM_SPEC, *[HBM_SPEC] * (2 * nw), VMEM_SPEC),
        input_output_aliases={i: 2 + i for i in range(2 * nw)},
        compiler_params=pltpu.CompilerParams(has_side_effects=DATAFLOW_EFFECT))(*grads, *lands)
    return out[0], out[1], out[2:2 + nw], out[2 + nw:2 + 2 * nw], out[-1]


def _direct_reduce_wait(send_sems, recv_sems, grads, lands, after, *, name):
    nw = len(grads)

    def body(*refs):
        srcs, lands_in, send_sems, recv_sems = refs[:nw], refs[nw:2 * nw], refs[2 * nw], refs[2 * nw + 1]
        cps = _direct_reduce_copies(srcs, lands_in, send_sems, recv_sems)
        for cp in cps:
            cp.wait_send()
        for cp in cps:
            cp.wait_recv()

    out = pl.pallas_call(
        body, name=name, out_shape=(*[pltpu.HBM(g.shape, g.dtype) for g in grads], *[pltpu.HBM(t.shape, t.dtype) for t in lands]),
        in_specs=[HBM_SPEC] * (2 * nw) + [SEM_SPEC, SEM_SPEC, ANY], out_specs=tuple([HBM_SPEC] * (2 * nw)),
        input_output_aliases={i: i for i in range(2 * nw)},
        compiler_params=pltpu.CompilerParams(has_side_effects=DATAFLOW_EFFECT))(*grads, *lands, send_sems, recv_sems, after)
    return out[nw:]


def _direct_reduce_add(grad, landed, chip, core, *, name):
    _, r, n = grad.shape
    half = r // 2
    tr = _row_tile(half)
    nb = half // tr

    def body(chip_ref, core_ref, g_ref, t_ref, o_ref):
        acc = g_ref[0]
        for k in range(N_DEV - 1):
            acc = acc + t_ref[k].astype(F32)
        o_ref[...] = acc

    return pl.pallas_call(
        body, name=name,
        grid_spec=pltpu.PrefetchScalarGridSpec(
            num_scalar_prefetch=2, grid=(nb,),
            in_specs=[pl.BlockSpec((1, tr, n), lambda i, chip_ref, core_ref: (chip_ref[0], core_ref[0] * nb + i, 0)),
                      pl.BlockSpec((N_DEV - 1, tr, n), lambda i, chip_ref, core_ref: (0, i, 0))],
            out_specs=pl.BlockSpec((tr, n), lambda i, chip_ref, core_ref: (i, 0))),
        out_shape=jax.ShapeDtypeStruct((half, n), F32), compiler_params=_params("parallel"))(chip, core, grad, landed)


def _share_halves(halves, *, name):
    nw = len(halves)

    def body(*refs):
        srcs, outs, (send_sems, recv_sems) = refs[:nw], refs[nw:2 * nw], refs[2 * nw:]
        _, _, _, sibling, _ = _place()
        cps = [pltpu.make_async_remote_copy(src_ref=src_ref, dst_ref=out_ref, send_sem=send_sems.at[w], recv_sem=recv_sems.at[w],
                                            device_id=sibling, device_id_type=MESH)
               for w, (src_ref, out_ref) in enumerate(zip(srcs, outs))]
        for cp in cps:
            cp.start()
        for cp in cps:
            cp.wait()

    return pl.pallas_call(
        body, name=name, out_shape=[jax.ShapeDtypeStruct(h.shape, h.dtype) for h in halves],
        in_specs=[ANY] * nw, out_specs=[ANY] * nw,
        scratch_shapes=[pltpu.SemaphoreType.DMA((nw,)), pltpu.SemaphoreType.DMA((nw,))])(*halves)


def _row_tile(rows, limit=256):
    return next(t for t in range(limit, 15, -16) if rows % t == 0)


def _sum_devices(gathered, *, name):
    _, m, n = gathered.shape

    def body(g_ref, tot_ref, loss_ref):
        tot = g_ref[0]
        for dev in range(1, N_DEV):
            tot = tot + g_ref[dev]
        tot_ref[...] = tot
        loss_ref[...] = jnp.full((8, n), (0.5 / D_MODEL) * jnp.sum(tot[0:8]), F32)

    return pl.pallas_call(body, name=name, in_specs=[VMEM_SPEC], out_specs=[VMEM_SPEC, VMEM_SPEC],
                          out_shape=[jax.ShapeDtypeStruct((m, n), F32), jax.ShapeDtypeStruct((8, n), F32)])(gathered)


def _ada_mod(cond_all, w_ada_shard, *, name):
    tn = 512

    def body(a_ref, b_ref, o_ref):
        o_ref[...] = _nn(a_ref[...], b_ref[...], precision=HIGHEST)

    return pl.pallas_call(
        body, name=name, grid=(w_ada_shard.shape[1] // tn,),
        in_specs=[pl.BlockSpec(cond_all.shape, lambda j: (0, 0)), pl.BlockSpec((D_MODEL, tn), lambda j: (0, j))],
        out_specs=pl.BlockSpec((N_DEV, tn), lambda j: (0, j)),
        out_shape=jax.ShapeDtypeStruct((N_DEV, w_ada_shard.shape[1]), F32), compiler_params=_params("parallel"))(cond_all, w_ada_shard)


def _ada_grad(cond_all, dmod_cols, *, name):
    tm = 256

    def body(a_ref, b_ref, o_ref):
        o_ref[...] = lax.dot_general(a_ref[...], b_ref[...], (((0,), (0,)), ((), ())), precision=HIGHEST,
                                     preferred_element_type=F32)

    return pl.pallas_call(
        body, name=name, grid=(D_MODEL // tm,),
        in_specs=[pl.BlockSpec((N_DEV, tm), lambda i: (0, i)), pl.BlockSpec(dmod_cols.shape, lambda i: (0, 0))],
        out_specs=pl.BlockSpec((tm, dmod_cols.shape[1]), lambda i: (i, 0)),
        out_shape=jax.ShapeDtypeStruct((D_MODEL, dmod_cols.shape[1]), F32), compiler_params=_params("parallel"))(cond_all, dmod_cols)


def _silu_rows(c8, *, name):
    def body(c_ref, o_ref):
        cv = c_ref[...]
        o_ref[...] = cv * _sigmoid(cv)

    return pl.pallas_call(body, name=name, in_specs=[VMEM_SPEC], out_specs=VMEM_SPEC,
                          out_shape=jax.ShapeDtypeStruct(c8.shape, F32))(c8)


def _rows128(t, rows=None):
    flat = t.reshape(-1, 128)
    return flat if rows is None else jnp.pad(flat, ((0, rows - flat.shape[0]), (0, 0)))


def _from_col_shards(shards, r, n):
    return shards.reshape(N_CHIP, r, n).transpose(1, 0, 2).reshape(r, N_CHIP * n)


def kernel(x, c, w_ada, b_ada, norm1_g, w_in, gla_w_gate, gla_b_gate, gla_norm_g, q_norm_g, k_norm_g, w_out, norm2_g, w_up, conv_w, conv_b, w_down, loss_target, m_w_ada, m_b_ada, m_norm1_g, m_w_in, m_gla_w_gate, m_gla_b_gate, m_gla_norm_g, m_q_norm_g, m_k_norm_g, m_w_out, m_norm2_g, m_w_up, m_conv_w, m_conv_b, m_w_down, v_w_ada, v_b_ada, v_norm1_g, v_w_in, v_gla_w_gate, v_gla_b_gate, v_gla_norm_g, v_q_norm_g, v_k_norm_g, v_w_out, v_norm2_g, v_w_up, v_conv_w, v_conv_b, v_w_down):
    d = D_MODEL
    ax, ay, ac = lax.axis_index("x"), lax.axis_index("y"), lax.axis_index("c")
    chip, dev = 2 * ax + ay, 4 * ax + 2 * ay + ac

    cond = _silu_rows(jnp.broadcast_to(c, (8, d)), name="cond_silu")[0:1]
    small_in = jnp.concatenate([_rows128(cond), _rows128(conv_w[0]), _rows128(gla_w_gate[0])], axis=0)
    small_in = _rows128(small_in, 56)
    got = _all_gather_small(small_in, name="gather_small").reshape(N_DEV, 56, 128)
    cond_all = got[:, 0:8].reshape(N_DEV, d)
    conv_w_full = _from_col_shards(got[0::2, 8:41].reshape(N_CHIP, 3 * 1408 // 128, 128), 3, 1408)
    gate_full = _from_col_shards(got[0::2, 41:49].reshape(N_CHIP, 16 * 64 // 128, 128), GLA_GATE_RANK, 64)
    mod_part = _ada_mod(cond_all, w_ada[0], name="ada_mod")
    mod_got = _all_gather_small(_rows128(mod_part), name="gather_mod").reshape(N_DEV, N_DEV, 1536)
    mod_all = mod_got[0::2].transpose(1, 0, 2).reshape(N_DEV, 6 * d) + b_ada
    mod = lax.dynamic_slice_in_dim(mod_all, dev, 1, axis=0)

    own = [w[0].astype(BF16).reshape(2, w.shape[1] // 2, w.shape[2]) for w in (w_in, w_out, w_up, w_down)]
    with_own = lambda got, mine: [lax.dynamic_update_index_in_dim(t, o, chip, 0) for t, o in zip(got, mine)]
    got_in, got_out = with_own(_gather_weight_shards(own[:2], name="gather_weights"), own[:2])
    w_in_full = got_in.reshape(N_CHIP, d, 772).transpose(1, 0, 2).reshape(d, N_CHIP * 772)
    w_out_full = got_out.reshape(d, d)
    exchanged = mod_all[0:1, 0:1] + got_in[0, 0, 0:1, 0:1].astype(F32)
    send_sems, recv_sems, own_thru, lands, token = _gather_late_start(own[2:], exchanged, name="gather_late_start")
    mod = mod + token[0:1, 0:1]

    def ffn_weights(after):
        mine, landed = _gather_late_wait(send_sems, recv_sems, own_thru, lands, after, name="gather_late_wait")
        got_up, got_down = with_own(landed, mine)
        return got_up.reshape(N_CHIP, d, 1408).transpose(1, 0, 2).reshape(d, 2 * D_FF), got_down.reshape(D_FF, d)

    ffn_reduce, attn_reduce, attn_parts = [], [], []
    halves_of = lambda g: g.reshape(N_CHIP, 2, g.shape[-2] // 2, g.shape[-1])

    def ffn_grads_ready(g_wup_b, g_wdown_b):
        ffn_reduce.extend(_direct_reduce_start([halves_of(g_wup_b), halves_of(g_wdown_b.reshape(N_CHIP, D_FF // N_CHIP, d))],
                                               name="reduce_ffn_start"))
        return ffn_reduce[4]

    def attn_grads_ready(g_wi, g_wo):
        attn_parts.extend([_in_proj_grad_layout(g_wi).reshape(d, N_CHIP, 772).transpose(1, 0, 2), g_wo.reshape(N_CHIP, d // N_CHIP, d)])
        attn_reduce.extend(_direct_reduce_start([halves_of(g.astype(BF16)) for g in attn_parts], name="reduce_attn_start"))
        return attn_reduce[4]

    err2, grad_x, (g_wi, g_wo, g_wup, g_wdown), small = _local_step(
        x[0], loss_target[0], mod, _in_proj_layout(w_in_full), w_out_full, ffn_weights, ffn_grads_ready, attn_grads_ready,
        conv_w_full, conv_b,
        _gate_layout(gate_full), gla_b_gate, gla_norm_g, q_norm_g, k_norm_g, norm1_g, norm2_g)

    pieces = [err2[0], small["dmod"], small["norm1_g"], small["norm2_g"], small["gla_w_gate"].reshape(-1), small["gla_b_gate"],
              small["gla_norm_g"], small["q_norm_g"], small["k_norm_g"], small["conv_w"].reshape(-1), small["conv_b"]]
    sizes = [p.shape[0] for p in pieces]
    at = [sum(sizes[:i]) for i in range(len(sizes) + 1)]
    vec = _rows128(jnp.concatenate(pieces), 288)
    got = _all_gather_small(vec, name="gather_grads").reshape(N_DEV, 288, 128)
    total, loss8 = _sum_devices(got, name="sum_devices")
    total = total.reshape(-1)
    seg = lambda i: total[at[i]:at[i + 1]]
    dmod_all = got.reshape(N_DEV, -1)[:, at[1]:at[2]]
    g_small = dict(
        b_ada=seg(1)[None], norm1_g=seg(2)[None], norm2_g=seg(3)[None],
        gla_w_gate=lax.dynamic_slice_in_dim(seg(4).reshape(GLA_GATE_RANK, 256), chip * 64, 64, axis=1),
        gla_b_gate=seg(5)[None], gla_norm_g=seg(6)[None], q_norm_g=seg(7)[None], k_norm_g=seg(8)[None],
        conv_w=lax.dynamic_slice_in_dim(seg(9).reshape(3, 2 * D_FF), chip * 1408, 1408, axis=1), conv_b=seg(10)[None])
    dmod_cols = lax.dynamic_slice_in_dim(dmod_all.reshape(N_DEV, 6 * d), chip * 1536, 1536, axis=1)
    g_w_ada = _ada_grad(cond_all, dmod_cols, name="ada_grad")

    core_id, chip_id = jnp.reshape(ac, (1,)).astype(jnp.int32), jnp.reshape(chip, (1,)).astype(jnp.int32)
    landed = (_direct_reduce_wait(*attn_reduce[:4], grad_x, name="reduce_attn_wait")
              + _direct_reduce_wait(*ffn_reduce[:4], grad_x, name="reduce_ffn_wait"))
    own = attn_parts + [g_wup, g_wdown.reshape(N_CHIP, D_FF // N_CHIP, d)]
    summed = [_direct_reduce_add(g, t, chip_id, core_id, name=f"reduce_add_{tag}")
              for g, t, tag in zip(own, landed, ("w_in", "w_out", "w_up", "w_down"))]
    others = _share_halves(summed, name="share_pair")

    grads = dict(w_ada=g_w_ada, **g_small, **dict(zip(("w_in", "w_out", "w_up", "w_down"), zip(summed, others))))
    names = ["w_ada", "b_ada", "norm1_g", "w_in", "gla_w_gate", "gla_b_gate", "gla_norm_g", "q_norm_g", "k_norm_g", "w_out",
             "norm2_g", "w_up", "conv_w", "conv_b", "w_down"]
    ws = dict(w_ada=w_ada, b_ada=b_ada, norm1_g=norm1_g, w_in=w_in, gla_w_gate=gla_w_gate, gla_b_gate=gla_b_gate,
              gla_norm_g=gla_norm_g, q_norm_g=q_norm_g, k_norm_g=k_norm_g, w_out=w_out, norm2_g=norm2_g, w_up=w_up,
              conv_w=conv_w, conv_b=conv_b, w_down=w_down)
    ms = dict(w_ada=m_w_ada, b_ada=m_b_ada, norm1_g=m_norm1_g, w_in=m_w_in, gla_w_gate=m_gla_w_gate, gla_b_gate=m_gla_b_gate,
              gla_norm_g=m_gla_norm_g, q_norm_g=m_q_norm_g, k_norm_g=m_k_norm_g, w_out=m_w_out, norm2_g=m_norm2_g, w_up=m_w_up,
              conv_w=m_conv_w, conv_b=m_conv_b, w_down=m_w_down)
    vs = dict(w_ada=v_w_ada, b_ada=v_b_ada, norm1_g=v_norm1_g, w_in=v_w_in, gla_w_gate=v_gla_w_gate, gla_b_gate=v_gla_b_gate,
              gla_norm_g=v_gla_norm_g, q_norm_g=v_q_norm_g, k_norm_g=v_k_norm_g, w_out=v_w_out, norm2_g=v_norm2_g, w_up=v_w_up,
              conv_w=v_conv_w, conv_b=v_conv_b, w_down=v_w_down)
    g_out, d_out, m_out, v_out = [], [], [], []
    for nm in names:
        shape = ws[nm].shape
        flip = (lambda t: t.T) if shape[-1] % 128 and shape[-2] % 128 == 0 else (lambda t: t)
        w2 = flip(ws[nm].reshape(shape[-2:]))
        if isinstance(grads[nm], tuple):
            mine, other = grads[nm]
            dl, mn, vn, g2 = _adamw(w2, (flip(mine), flip(other), core_id), flip(ms[nm].reshape(shape[-2:])),
                                    flip(vs[nm].reshape(shape[-2:])), name=f"adamw_{nm}")
        else:
            g2 = flip(grads[nm].reshape(shape[-2:]))
            dl, mn, vn = _adamw(w2, g2, flip(ms[nm].reshape(shape[-2:])), flip(vs[nm].reshape(shape[-2:])), name=f"adamw_{nm}")
        for outs, t in ((g_out, g2), (d_out, dl), (m_out, mn), (v_out, vn)):
            outs.append(flip(t).reshape(shape))
    return (loss8[0, 0], grad_x[None], *g_out, *d_out, *m_out, *v_out)
```

```python
import functools

import jax
import jax.numpy as jnp
from jax import lax
from jax.experimental import pallas as pl
from jax.experimental.pallas import tpu as pltpu

F32, BF16 = jnp.float32, jnp.bfloat16
HIGHEST = lax.Precision.HIGHEST
MESH = pl.DeviceIdType.MESH

D_MODEL = 1024
GLA_CHUNK = 64
GLA_GATE_TAU = 16.0
GLA_GATE_RANK = 16
HEAD_LANES = 128
ATTN_BLOCK = 128
DILATIONS = (1, 4, 16)
ALIBI_SLOPES = tuple(2.0 ** (-(h + 1)) for h in range(8))
D_FF = 2816
EPS = 1e-6
C_GQ, C_GK, C_GV, C_GR, C_AQ, C_AK, C_AV, C_LR, PROJ_W = 0, 256, 512, 1024, 1536, 2048, 2560, 3072, 3200
ADAM_LR, ADAM_B1, ADAM_B2, ADAM_EPS, ADAM_WD, ADAM_STEP = 0.001, 0.9, 0.999, 1e-08, 0.01, 10
VMEM_LIMIT_BYTES = 56 * 1024 * 1024
ROW_TILE = 512
ADAM_TILE = 256


def _params(*sem):
    return pltpu.CompilerParams(dimension_semantics=sem or None, vmem_limit_bytes=VMEM_LIMIT_BYTES)


def _nt(a, b):
    return lax.dot_general(a, b, (((1,), (1,)), ((), ())), preferred_element_type=F32)


def _tn(a, b):
    return lax.dot_general(a, b, (((0,), (0,)), ((), ())), preferred_element_type=F32)


def _nn(a, b, precision=None):
    return jnp.dot(a, b, preferred_element_type=F32, precision=precision)


def _split3(v):
    hi = v.astype(BF16)
    rest = v - hi.astype(F32)
    mid = rest.astype(BF16)
    return hi, mid, (rest - mid.astype(F32)).astype(BF16)


def _sum_right(v, ones):
    hi, mid, lo = _split3(v)
    return (_nn(lo, ones) + _nn(mid, ones)) + _nn(hi, ones)


def _sum_left(ones, v):
    hi, mid, lo = _split3(v)
    return (_nn(ones, lo) + _nn(ones, mid)) + _nn(ones, hi)


def _fold8(v):
    return v.reshape(v.shape[0] // 8, 8, v.shape[1]).sum(axis=0)


def _spread_total(ref):
    t = ref[...]
    ref[...] = jnp.broadcast_to(jnp.sum(t, axis=-2, keepdims=True), t.shape)


def _sigmoid(x):
    return 1.0 / (1.0 + jnp.exp(-x))


def _mm(a, b, *, ta=False, tb=False, out_dtype=F32, tm, tn, tk, shard_cols=False, also_bf16=False, name):
    (k_a, m) = a.shape if ta else a.shape[::-1]
    (k_b, n) = b.shape[::-1] if tb else b.shape
    assert k_a == k_b and m % tm == 0 and n % tn == 0 and k_a % tk == 0, (name, a.shape, b.shape)
    nk = k_a // tk
    assert nk == 1 or out_dtype == F32, name
    dims = (((0 if ta else 1,), (1 if tb else 0,)), ((), ()))

    def body(a_ref, b_ref, o_ref, *rounded):
        k = pl.program_id(2)
        part = lax.dot_general(a_ref[...].astype(BF16), b_ref[...].astype(BF16), dims, preferred_element_type=F32)
        if nk == 1:
            o_ref[...] = part.astype(out_dtype)
        else:
            @pl.when(k == 0)
            def _():
                o_ref[...] = part

            @pl.when(k > 0)
            def _():
                o_ref[...] += part

        if also_bf16:
            @pl.when(k == nk - 1)
            def _():
                rounded[0][...] = o_ref[...].astype(BF16)

    a_spec = pl.BlockSpec((tk, tm), lambda i, j, k: (k, i)) if ta else pl.BlockSpec((tm, tk), lambda i, j, k: (i, k))
    b_spec = pl.BlockSpec((tn, tk), lambda i, j, k: (j, k)) if tb else pl.BlockSpec((tk, tn), lambda i, j, k: (k, j))
    if shard_cols:
        o_spec, o_shape = pl.BlockSpec((None, tm, tn), lambda i, j, k: (j, i, 0)), (n // tn, m, tn)
    else:
        o_spec, o_shape = pl.BlockSpec((tm, tn), lambda i, j, k: (i, j)), (m, n)
    shapes = [jax.ShapeDtypeStruct(o_shape, out_dtype)] + ([jax.ShapeDtypeStruct(o_shape, BF16)] if also_bf16 else [])
    out = pl.pallas_call(
        body, name=name, grid=(m // tm, n // tn, nk), in_specs=[a_spec, b_spec], out_specs=[o_spec] * len(shapes),
        out_shape=shapes, compiler_params=_params("parallel", "parallel", "arbitrary"))(a, b)
    return out if also_bf16 else out[0]


def _norm_mod_fwd(x, branch, gate, gain, scale, shift, *, name):
    s, d = x.shape
    tm = 2 * ROW_TILE
    has_branch = branch is not None

    def body(*refs):
        if has_branch:
            x_ref, br_ref, gate_ref, gain_ref, sc_ref, sh_ref, x1_ref, h_ref, ht_ref = refs
            xv = x_ref[...] + gate_ref[...] * br_ref[...]
            x1_ref[...] = xv
        else:
            x_ref, gain_ref, sc_ref, sh_ref, h_ref, ht_ref = refs
            xv = x_ref[...]
        r = lax.rsqrt(jnp.mean(xv * xv, axis=-1, keepdims=True) + EPS)
        h = (xv * r) * gain_ref[...] * (1.0 + sc_ref[...]) + sh_ref[...]
        h_ref[...] = h.astype(BF16)
        ht_ref[...] = h.T.astype(BF16)

    row = pl.BlockSpec((tm, d), lambda i: (i, 0))
    col = pl.BlockSpec((d, tm), lambda i: (0, i))
    vec = pl.BlockSpec((1, d), lambda i: (0, 0))
    h_shapes = [jax.ShapeDtypeStruct((s, d), BF16), jax.ShapeDtypeStruct((d, s), BF16)]
    if has_branch:
        return pl.pallas_call(
            body, name=name, grid=(s // tm,), in_specs=[row, row, vec, vec, vec, vec], out_specs=[row, row, col],
            out_shape=[jax.ShapeDtypeStruct((s, d), F32)] + h_shapes,
            compiler_params=_params("parallel"))(x, branch, gate, gain, scale, shift)
    h, ht = pl.pallas_call(
        body, name=name, grid=(s // tm,), in_specs=[row, vec, vec, vec], out_specs=[row, col],
        out_shape=h_shapes, compiler_params=_params("parallel"))(x, gain, scale, shift)
    return x, h, ht


def _norm_mod_bwd(x, dh, dres, gain, scale, branch, gate, *, name):
    s, d = x.shape
    tm = ROW_TILE
    has_branch = branch is not None

    def body(*refs):
        if has_branch:
            x_ref, dh_ref, dres_ref, gain_ref, sc_ref, br_ref, gate_ref, dx_ref, dbr_ref, sums_ref = refs
        else:
            x_ref, dh_ref, dres_ref, gain_ref, sc_ref, dx_ref, sums_ref = refs
        i = pl.program_id(0)

        @pl.when(i == 0)
        def _():
            sums_ref[...] = jnp.zeros_like(sums_ref)

        xv, dhv = x_ref[...], dh_ref[...]
        r = lax.rsqrt(jnp.mean(xv * xv, axis=-1, keepdims=True) + EPS)
        xn = xv * r
        dxn = dhv * (gain_ref[...] * (1.0 + sc_ref[...]))
        dx = dres_ref[...] + r * (dxn - xn * jnp.mean(dxn * xn, axis=-1, keepdims=True))
        dx_ref[...] = dx
        sums_ref[0] += _fold8(dhv * xn)
        sums_ref[1] += _fold8(dhv)
        if has_branch:
            dbr_ref[...] = (gate_ref[...] * dx).astype(BF16)
            sums_ref[2] += _fold8(dx * br_ref[...])

        @pl.when(i == s // tm - 1)
        def _():
            _spread_total(sums_ref)

    row = pl.BlockSpec((tm, d), lambda i: (i, 0))
    vec = pl.BlockSpec((1, d), lambda i: (0, 0))
    sums = pl.BlockSpec((3, 8, d), lambda i: (0, 0, 0))
    sums_shape = jax.ShapeDtypeStruct((3, 8, d), F32)
    if has_branch:
        return pl.pallas_call(
            body, name=name, grid=(s // tm,), in_specs=[row, row, row, vec, vec, row, vec], out_specs=[row, row, sums],
            out_shape=[jax.ShapeDtypeStruct((s, d), F32), jax.ShapeDtypeStruct((s, d), BF16), sums_shape],
            compiler_params=_params("arbitrary"))(x, dh, dres, gain, scale, branch, gate)
    dx, sm = pl.pallas_call(
        body, name=name, grid=(s // tm,), in_specs=[row, row, row, vec, vec], out_specs=[row, sums],
        out_shape=[jax.ShapeDtypeStruct((s, d), F32), sums_shape],
        compiler_params=_params("arbitrary"))(x, dh, dres, gain, scale)
    return dx, None, sm


GLA_ROWS = 256


def _gla_block_setup(lr_ref, wg_ref, bg_ref):
    t, c = GLA_ROWS, GLA_CHUNK
    ri = lax.broadcasted_iota(jnp.int32, (t, t), 0)
    ci = lax.broadcasted_iota(jnp.int32, (t, t), 1)
    same = (ri // c) == (ci // c)
    causal, upper = same & (ci <= ri), same & (ci >= ri)
    z = _nn(lr_ref[...].astype(BF16), wg_ref[...]) + bg_ref[...]
    g = (jnp.minimum(z, 0.0) - jnp.log(1.0 + jnp.exp(-jnp.abs(z)))) * (1.0 / GLA_GATE_TAU)
    hi, mid, lo = _split3(g)
    total = lambda ones: (_nn(ones, lo) + _nn(ones, mid)) + _nn(ones, hi)
    return z, total(causal.astype(BF16)), total(same.astype(BF16)), causal, upper


def _chunks(t):
    return [t[i * GLA_CHUNK:(i + 1) * GLA_CHUNK] for i in range(GLA_ROWS // GLA_CHUNK)]


def _gla_fwd(proj, wg, bg, gn, *, name):
    s = proj.shape[0]
    tb, c = GLA_ROWS, GLA_CHUNK
    cb = tb // c

    def body(q_ref, k_ref, v_ref, r_ref, lr_ref, wg_ref, bg_ref, gn_ref, o_ref, y_ref, st_ref, state):
        i = pl.program_id(0)

        @pl.when(i == 0)
        def _():
            state[...] = jnp.zeros_like(state)

        low = lax.broadcasted_iota(jnp.int32, (tb, HEAD_LANES), 1) < 64
        masks = (low, jnp.logical_not(low))
        _, b, b_end, causal, _ = _gla_block_setup(lr_ref, wg_ref, bg_ref)
        pairs = []
        for p in range(2):
            cols = pl.ds(p * HEAD_LANES, HEAD_LANES)
            bp, bep = (t[:, p * HEAD_LANES:(p + 1) * HEAD_LANES] for t in (b, b_end))
            k = k_ref[:, cols]
            q_in = q_ref[:, cols] * 0.125 * jnp.exp(bp)
            k_out = (k * jnp.exp(-bp)).astype(BF16)
            k_end = k * jnp.exp(bep - bp)
            qms = [jnp.where(m, q_in, 0.0).astype(BF16) for m in masks]
            kes = [jnp.where(m, k_end, 0.0).astype(BF16) for m in masks]
            vs = [v_ref[:, pl.ds((2 * p + e) * HEAD_LANES, HEAD_LANES)].astype(BF16) for e in range(2)]
            grow = [_tn(v0, k0) + _tn(v1, k1) for v0, k0, v1, k1 in zip(_chunks(vs[0]), _chunks(kes[0]), _chunks(vs[1]), _chunks(kes[1]))]
            pairs.append((bep, k_out, qms, vs, grow))
        entering = [[], []]
        for p, (bep, _, _, _, grow) in enumerate(pairs):
            st = state[p]
            for ch in range(cb):
                entering[p].append(st)
                st_ref[ch, p] = st
                st = st * jnp.exp(bep[ch * c:ch * c + 1, :]) + grow[ch]
            state[p] = st
        for p, (_, k_out, qms, vs, _) in enumerate(pairs):
            for e in range(2):
                hc = pl.ds((2 * p + e) * HEAD_LANES, HEAD_LANES)
                a = jnp.where(causal, _nt(qms[e], k_out), 0.0).astype(BF16)
                carried = jnp.concatenate([_nt(qc, sc.astype(BF16)) for qc, sc in zip(_chunks(qms[e]), entering[p])], axis=0)
                o = _nn(a, vs[e]) + carried
                o_ref[:, hc] = o
                rr = r_ref[:, hc]
                on = o * lax.rsqrt(jnp.mean(o * o, axis=-1, keepdims=True) + EPS)
                y_ref[:, hc] = (on * gn_ref[...] * (rr * _sigmoid(rr))).astype(BF16)

    def col(width, at):
        return pl.BlockSpec((tb, width), lambda i: (i, at // width))

    full = lambda shape: pl.BlockSpec(shape, lambda i: tuple(0 for _ in shape))
    return pl.pallas_call(
        body, name=name, grid=(s // tb,),
        in_specs=[col(256, C_GQ), col(256, C_GK), col(512, C_GV), col(512, C_GR), col(128, C_LR),
                  full((HEAD_LANES, 256)), full((1, 256)), full((1, HEAD_LANES))],
        out_specs=[pl.BlockSpec((tb, 512), lambda i: (i, 0)), pl.BlockSpec((tb, 512), lambda i: (i, 0)),
                   pl.BlockSpec((cb, 2, HEAD_LANES, HEAD_LANES), lambda i: (i, 0, 0, 0))],
        out_shape=[jax.ShapeDtypeStruct((s, 512), F32), jax.ShapeDtypeStruct((s, 512), BF16),
                   jax.ShapeDtypeStruct((s // c, 2, HEAD_LANES, HEAD_LANES), F32)],
        scratch_shapes=[pltpu.VMEM((2, HEAD_LANES, HEAD_LANES), F32)],
        compiler_params=_params("arbitrary"))(proj, proj, proj, proj, proj, wg, bg, gn)


def _gla_bwd(proj, wg, bg, gn, o_raw, states, dmixed, *, name):
    s = proj.shape[0]
    tb, c = GLA_ROWS, GLA_CHUNK
    cb = tb // c
    nblk, nch = s // tb, s // c

    def body(q_ref, k_ref, v_ref, r_ref, lr_ref, wg_ref, bg_ref, gn_ref, o_ref, st_ref, stn_ref, dy_ref,
             dq_ref, dk_ref, dv_ref, dr_ref, dlr_ref, gwg_ref, sums_ref, dstate):
        i = pl.program_id(0)

        @pl.when(i == 0)
        def _():
            dstate[...] = jnp.zeros_like(dstate)
            gwg_ref[...] = jnp.zeros_like(gwg_ref)
            sums_ref[...] = jnp.zeros_like(sums_ref)

        low = lax.broadcasted_iota(jnp.int32, (tb, HEAD_LANES), 1) < 64
        masks = (low, jnp.logical_not(low))
        z, b, b_end, causal, upper = _gla_block_setup(lr_ref, wg_ref, bg_ref)
        lr_b = lr_ref[...].astype(BF16)
        dlr = jnp.zeros((tb, HEAD_LANES), F32)
        per_chunk = lambda rows, mats, fn: jnp.concatenate([fn(r, m.astype(BF16)) for r, m in zip(_chunks(rows), mats)], axis=0)
        pairs = []
        for p in range(2):
            cols = pl.ds(p * HEAD_LANES, HEAD_LANES)
            sl = slice(p * HEAD_LANES, (p + 1) * HEAD_LANES)
            bp, bep = b[:, sl], b_end[:, sl]
            e_in, e_out, e_end = jnp.exp(bp), jnp.exp(-bp), jnp.exp(bep - bp)
            q = q_ref[:, cols] * 0.125
            k = k_ref[:, cols]
            q_in, k_out, k_end = q * e_in, k * e_out, k * e_end
            qms = [jnp.where(m, q_in, 0.0).astype(BF16) for m in masks]
            kms_out = [jnp.where(m, k_out, 0.0).astype(BF16) for m in masks]
            kms_end = [jnp.where(m, k_end, 0.0).astype(BF16) for m in masks]
            vs, dos = [], []
            for e in range(2):
                hc = pl.ds((2 * p + e) * HEAD_LANES, HEAD_LANES)
                o, rr, dy = o_ref[:, hc], r_ref[:, hc], dy_ref[:, hc]
                sg = _sigmoid(rr)
                rs = lax.rsqrt(jnp.mean(o * o, axis=-1, keepdims=True) + EPS)
                on = o * rs
                t = dy * (rr * sg)
                sums_ref[1, :, hc] += _fold8(t * on)
                dn = t * gn_ref[...]
                dos.append((rs * (dn - on * jnp.mean(dn * on, axis=-1, keepdims=True))).astype(BF16))
                dr_ref[:, hc] = (dy * on * gn_ref[...] * (sg * (1.0 + rr * (1.0 - sg)))).astype(BF16)
                vs.append(v_ref[:, hc].astype(BF16))
            grow = [_tn(d0, q0) + _tn(d1, q1) for d0, q0, d1, q1 in zip(_chunks(dos[0]), _chunks(qms[0]), _chunks(dos[1]), _chunks(qms[1]))]
            pairs.append((bep, e_in, e_out, e_end, q, k, qms, kms_out, kms_end, vs, dos, grow))
        chains = []
        for p in range(2):
            bep, grow = pairs[p][0], pairs[p][-1]
            entering = [st_ref[ch, p] for ch in range(cb)]
            dst, leaving_grad = dstate[p], [None] * cb
            for ch in reversed(range(cb)):
                leaving_grad[ch] = dst
                dst = dst * jnp.exp(bep[ch * c:ch * c + 1, :]) + grow[ch]
            dstate[p] = dst
            chains.append((entering, leaving_grad))
        for p in range(2):
            cols = pl.ds(p * HEAD_LANES, HEAD_LANES)
            sl = slice(p * HEAD_LANES, (p + 1) * HEAD_LANES)
            _, e_in, e_out, e_end, q, k, qms, kms_out, kms_end, vs, dos, _ = pairs[p]
            entering, leaving_grad = chains[p]
            leaving = entering[1:] + [stn_ref[0, p]]
            felt = jnp.concatenate([jnp.broadcast_to(jnp.sum(dg_st * st, axis=0, keepdims=True), (c, HEAD_LANES))
                                    for dg_st, st in zip(leaving_grad, leaving)], axis=0)
            dq_in = jnp.zeros((tb, HEAD_LANES), F32)
            dk_out = jnp.zeros((tb, HEAD_LANES), F32)
            dk_end = jnp.zeros((tb, HEAD_LANES), F32)
            for e in range(2):
                hc = pl.ds((2 * p + e) * HEAD_LANES, HEAD_LANES)
                a = jnp.where(causal, _nt(qms[e], kms_out[e]), 0.0).astype(BF16)
                da = jnp.where(causal, _nt(dos[e], vs[e]), 0.0).astype(BF16)
                dv_ref[:, hc] = (_tn(a, dos[e]) + per_chunk(kms_end[e], leaving_grad, _nt)).astype(BF16)
                dq_in = dq_in + jnp.where(masks[e], per_chunk(dos[e], entering, _nn) + _nn(da, kms_out[e]), 0.0)
                dk_out = dk_out + _tn(da, qms[e])
                dk_end = dk_end + jnp.where(masks[e], per_chunk(vs[e], leaving_grad, _nn), 0.0)
            dq = dq_in * e_in
            dk = dk_out * e_out + dk_end * e_end
            dq_ref[:, cols] = (dq * 0.125).astype(BF16)
            dk_ref[:, cols] = dk.astype(BF16)
            dg = _sum_left(upper.astype(BF16), q * dq - k * dk) + felt
            dz = dg * (1.0 / GLA_GATE_TAU) * _sigmoid(-z[:, sl])
            dz_b = dz.astype(BF16)
            sums_ref[0, :, cols] += _fold8(dz)
            dlr = dlr + _nt(dz_b, wg_ref[:, cols])
            gwg_ref[:, cols] += _tn(lr_b, dz_b)
        dlr_ref[...] = dlr.astype(BF16)

        @pl.when(i == nblk - 1)
        def _():
            _spread_total(sums_ref)

    rev = lambda i: nblk - 1 - i

    def col(width, at):
        return pl.BlockSpec((tb, width), lambda i: (rev(i), at // width))

    full = lambda shape: pl.BlockSpec(shape, lambda i: tuple(0 for _ in shape))
    out_col = lambda width: pl.BlockSpec((tb, width), lambda i: (rev(i), 0))
    return pl.pallas_call(
        body, name=name, grid=(nblk,),
        in_specs=[col(256, C_GQ), col(256, C_GK), col(512, C_GV), col(512, C_GR), col(128, C_LR),
                  full((HEAD_LANES, 256)), full((1, 256)), full((1, HEAD_LANES)),
                  pl.BlockSpec((tb, 512), lambda i: (rev(i), 0)),
                  pl.BlockSpec((cb, 2, HEAD_LANES, HEAD_LANES), lambda i: (rev(i), 0, 0, 0)),
                  pl.BlockSpec((1, 2, HEAD_LANES, HEAD_LANES), lambda i: (jnp.minimum((rev(i) + 1) * cb, nch - 1), 0, 0, 0)),
                  pl.BlockSpec((tb, 512), lambda i: (rev(i), 0))],
        out_specs=[out_col(256), out_col(256), out_col(512), out_col(512), out_col(128),
                   full((HEAD_LANES, 256)), full((2, 8, 512))],
        out_shape=[jax.ShapeDtypeStruct((s, 256), BF16), jax.ShapeDtypeStruct((s, 256), BF16),
                   jax.ShapeDtypeStruct((s, 512), BF16), jax.ShapeDtypeStruct((s, 512), BF16),
                   jax.ShapeDtypeStruct((s, 128), BF16), jax.ShapeDtypeStruct((HEAD_LANES, 256), F32),
                   jax.ShapeDtypeStruct((2, 8, 512), F32)],
        scratch_shapes=[pltpu.VMEM((2, HEAD_LANES, HEAD_LANES), F32)],
        compiler_params=_params("arbitrary"))(proj, proj, proj, proj, proj, wg, bg, gn, o_raw, states, states, dmixed)


def _head_sums(v):
    ri = lax.broadcasted_iota(jnp.int32, (HEAD_LANES, HEAD_LANES), 0) // 64
    ci = lax.broadcasted_iota(jnp.int32, (HEAD_LANES, HEAD_LANES), 1) // 64
    ones = (ri == ci).astype(BF16)
    return jnp.concatenate([_sum_right(v[:, p * HEAD_LANES:(p + 1) * HEAD_LANES], ones) for p in range(4)], axis=1)


def _attn_prep(proj, qg, kg, *, name):
    s = proj.shape[0]
    tm = 2 * ROW_TILE

    def body(q_ref, k_ref, qg_ref, kg_ref, qa_ref, ka_ref):
        q, k = q_ref[...], k_ref[...]
        qr = lax.rsqrt(_head_sums(q * q) * (1.0 / 64) + EPS)
        kr = lax.rsqrt(_head_sums(k * k) * (1.0 / 64) + EPS)
        qa_ref[...] = q * qr * qg_ref[...] * 0.125
        ka_ref[...] = k * kr * kg_ref[...]

    col = lambda at: pl.BlockSpec((tm, 512), lambda i: (i, at // 512))
    vec = pl.BlockSpec((1, 512), lambda i: (0, 0))
    out = pl.BlockSpec((tm, 512), lambda i: (i, 0))
    return pl.pallas_call(
        body, name=name, grid=(s // tm,), in_specs=[col(C_AQ), col(C_AK), vec, vec], out_specs=[out] * 2,
        out_shape=[jax.ShapeDtypeStruct((s, 512), F32)] * 2, compiler_params=_params("parallel"))(proj, proj, qg, kg)


FAR = 1e30
LOG2E, LN2 = 1.4426950408889634, 0.6931471805599453


def _attn_distance(first):
    blk = ATTN_BLOCK
    iq = lax.broadcasted_iota(jnp.int32, (2 * blk, 2 * blk), 0) & (blk - 1)
    ik = lax.broadcasted_iota(jnp.int32, (2 * blk, 2 * blk), 1)
    rel = iq + blk - ik
    valid = (rel >= 0) & (rel <= blk) & (jnp.logical_not(first) | (ik >= blk))
    return jnp.where(valid, rel.astype(F32), FAR)


def _stack_heads(t2):
    low = lax.broadcasted_iota(jnp.int32, t2.shape, 1) < 64
    return jnp.concatenate([jnp.where(low, t2, 0.0), jnp.where(low, 0.0, t2)], axis=0).astype(BF16)


def _unstack_heads(t):
    blk = ATTN_BLOCK
    low = lax.broadcasted_iota(jnp.int32, (blk, HEAD_LANES), 1) < 64
    return jnp.where(low, t[0:blk], t[blk:2 * blk])


def _attn_scores(qs, kcat, slopes, dil, dist):
    top = lax.broadcasted_iota(jnp.int32, (2 * ATTN_BLOCK, 1), 0) < ATTN_BLOCK
    return _nt(qs, kcat) - jnp.where(top, slopes[0] * (dil * LOG2E), slopes[1] * (dil * LOG2E)) * dist


def _pair_slopes(p):
    if isinstance(p, int):
        return ALIBI_SLOPES[2 * p], ALIBI_SLOPES[2 * p + 1]
    pick = lambda e: jnp.where(p == 0, ALIBI_SLOPES[e], jnp.where(p == 1, ALIBI_SLOPES[2 + e],
                               jnp.where(p == 2, ALIBI_SLOPES[4 + e], ALIBI_SLOPES[6 + e])))
    return pick(0), pick(1)


ATTN_GROUP = 4


def _each(fn, *lists):
    return [fn(*args) for args in zip(*lists)]


def _attn_group_fwd(q2s, kcats, vcats, slopes, dil, dist):
    qs = _each(lambda q2: _stack_heads(q2 * LOG2E), q2s)
    sc = _each(lambda q, k, sl: _attn_scores(q, k, sl, dil, dist), qs, kcats, slopes)
    m = _each(lambda s: jnp.max(s, axis=-1, keepdims=True), sc)
    pr = _each(lambda s, mx: jnp.exp2(s - mx), sc, m)
    den = _each(lambda p: jnp.sum(p, axis=-1, keepdims=True), pr)
    o = _each(lambda p, v, d: _nn(p.astype(BF16), v) / d, pr, vcats, den)
    lse = _each(lambda mx, d, t: jnp.broadcast_to(mx + jnp.log2(d), t.shape), m, den, o)
    return _each(lambda t, l: (_unstack_heads(t), _unstack_heads(l)), o, lse)


def _attn_group_bwd(q2s, kcats, vcats, do2s, y2s, lse2s, slopes, dil, dist):
    lane = lax.broadcasted_iota(jnp.int32, (ATTN_BLOCK, HEAD_LANES), 1)
    low = lane < 64
    per_head = lambda t, pick: jnp.concatenate([jnp.sum(jnp.where(pick(0), t, 0.0), axis=-1, keepdims=True),
                                                jnp.sum(jnp.where(pick(1), t, 0.0), axis=-1, keepdims=True)], axis=0)
    lse = _each(lambda l: per_head(l, lambda e: lane == 64 * e), lse2s)
    delta = _each(lambda d, y: per_head(d * y, lambda e: low if e == 0 else jnp.logical_not(low)), do2s, y2s)
    qs = _each(lambda q2: _stack_heads(q2 * LOG2E), q2s)
    dos = _each(_stack_heads, do2s)
    sc = _each(lambda q, k, sl: _attn_scores(q, k, sl, dil, dist), qs, kcats, slopes)
    pr = _each(lambda s, l: jnp.exp2(s - l), sc, lse)
    dp = _each(_nt, dos, vcats)
    ds = _each(lambda p, d, dl: (p * (d - dl)).astype(BF16), pr, dp, delta)
    dq = _each(lambda d, k: _unstack_heads(_nn(d, k)), ds, kcats)
    dk = _each(lambda d, q: _tn(d, q) * LN2, ds, qs)
    dv = _each(lambda p, d: _tn(p.astype(BF16), d), pr, dos)
    return list(zip(dq, dk, dv))


def _attn_specs(dil):
    rows = ATTN_BLOCK * dil
    if dil == 1:
        cur = lambda at: pl.BlockSpec((rows, 512), lambda n: (n, at // 512))
        prev = lambda at: pl.BlockSpec((rows, 512), lambda n: (jnp.maximum(n - 1, 0), at // 512))
    else:
        cur = lambda at: pl.BlockSpec((rows, HEAD_LANES), lambda n, p: (n, at // HEAD_LANES + p))
        prev = lambda at: pl.BlockSpec((rows, HEAD_LANES), lambda n, p: (jnp.maximum(n - 1, 0), at // HEAD_LANES + p))
    return cur, prev


def _attn_loop(dil, one_group, p):
    if dil == 1:
        one_group([(slice(None), pl.ds(p * HEAD_LANES, HEAD_LANES), p) for p in range(ATTN_GROUP)])
    else:
        group = min(dil, ATTN_GROUP)

        def step(g, carry):
            one_group([(pl.ds(g * group + j, ATTN_BLOCK, stride=dil), slice(None), p) for j in range(group)])
            return carry

        if dil == group:
            step(0, 0)
        else:
            lax.fori_loop(0, dil // group, step, 0)


def _dil_attn_fwd(qa, ka, proj, dil, *, name):
    s = qa.shape[0]

    def body(q_ref, kp_ref, kc_ref, vp_ref, vc_ref, o_ref, lse_ref):
        dist = _attn_distance(pl.program_id(0) == 0)
        pair = None if dil == 1 else pl.program_id(1)

        def one_group(items):
            both = lambda a, b: [jnp.concatenate([a[rows, cols], b[rows, cols]], axis=0).astype(BF16) for rows, cols, _ in items]
            outs = _attn_group_fwd([q_ref[rows, cols] for rows, cols, _ in items], both(kp_ref, kc_ref), both(vp_ref, vc_ref),
                                   [_pair_slopes(p) for _, _, p in items], dil, dist)
            for (rows, cols, _), (o2, lse2) in zip(items, outs):
                o_ref[rows, cols] = o2
                lse_ref[rows, cols] = lse2

        _attn_loop(dil, one_group, pair)

    cur, prev = _attn_specs(dil)
    grid = (s // ATTN_BLOCK,) if dil == 1 else (s // (ATTN_BLOCK * dil), 4)
    return pl.pallas_call(
        body, name=name, grid=grid, in_specs=[cur(0), prev(0), cur(0), prev(C_AV), cur(C_AV)], out_specs=[cur(0), cur(0)],
        out_shape=[jax.ShapeDtypeStruct((s, 512), F32)] * 2,
        compiler_params=_params(*["parallel"] * len(grid)))(qa, ka, ka, proj, proj)


def _dense_attn_fwd_merge(qa, ka, proj, others, y_gla, *, name):
    s = qa.shape[0]
    blk = ATTN_BLOCK

    def body(q_ref, kp_ref, kc_ref, vp_ref, vc_ref, oa_ref, la_ref, ob_ref, lb_ref, yg_ref, mixed_ref, y_ref, lse_ref):
        dist = _attn_distance(pl.program_id(0) == 0)
        mixed_ref[:, 0:512] = yg_ref[...]

        def one_group(items):
            both = lambda a, b: [jnp.concatenate([a[rows, cols], b[rows, cols]], axis=0).astype(BF16) for rows, cols, _ in items]
            outs = _attn_group_fwd([q_ref[rows, cols] for rows, cols, _ in items], both(kp_ref, kc_ref), both(vp_ref, vc_ref),
                                   [_pair_slopes(p) for _, _, p in items], 1, dist)
            for (_, cols, p), (o2, l2) in zip(items, outs):
                la, lb = la_ref[:, cols], lb_ref[:, cols]
                m = jnp.maximum(jnp.maximum(l2, la), lb)
                w0, wa, wb = jnp.exp2(l2 - m), jnp.exp2(la - m), jnp.exp2(lb - m)
                zs = w0 + wa + wb
                y = (w0 * o2 + wa * oa_ref[:, cols] + wb * ob_ref[:, cols]) / zs
                y_ref[:, cols] = y
                lse_ref[:, cols] = m + jnp.log2(zs)
                mixed_ref[:, pl.ds(512 + p * HEAD_LANES, HEAD_LANES)] = y.astype(BF16)

        _attn_loop(1, one_group, None)

    cur, prev = _attn_specs(1)
    here = pl.BlockSpec((blk, 512), lambda n: (n, 0))
    (oa, la), (ob, lb) = others
    return pl.pallas_call(
        body, name=name, grid=(s // blk,),
        in_specs=[cur(0), prev(0), cur(0), prev(C_AV), cur(C_AV)] + [here] * 5,
        out_specs=[pl.BlockSpec((blk, 1024), lambda n: (n, 0)), here, here],
        out_shape=[jax.ShapeDtypeStruct((s, 1024), BF16), jax.ShapeDtypeStruct((s, 512), F32),
                   jax.ShapeDtypeStruct((s, 512), F32)],
        compiler_params=_params("parallel"))(qa, ka, ka, proj, proj, oa, la, ob, lb, y_gla)


def _dil_attn_bwd(qa, ka, proj, y_att, lse, dmixed, dil, *, name):
    s = qa.shape[0]
    blk, rows_per_step = ATTN_BLOCK, ATTN_BLOCK * dil
    nb = s // rows_per_step
    step_axis = 0 if dil == 1 else 1

    def body(q_ref, kp_ref, kc_ref, vp_ref, vc_ref, y_ref, lse_ref, do_ref, dq_ref, dk_ref, dv_ref, dk_own, dv_own):
        n = pl.program_id(step_axis)
        pair = None if dil == 1 else pl.program_id(0)
        dist = _attn_distance(n == 0)

        @pl.when(n == 0)
        def _():
            dk_own[...] = jnp.zeros_like(dk_own)
            dv_own[...] = jnp.zeros_like(dv_own)

        def one_group(items):
            both = lambda a, b: [jnp.concatenate([a[rows, cols], b[rows, cols]], axis=0).astype(BF16) for rows, cols, _ in items]
            at = lambda ref: [ref[rows, cols] for rows, cols, _ in items]
            outs = _attn_group_bwd(at(q_ref), both(kp_ref, kc_ref), both(vp_ref, vc_ref), at(do_ref), at(y_ref), at(lse_ref),
                                   [_pair_slopes(p) for _, _, p in items], dil, dist)
            for (rows, cols, _), (dq, dk, dv) in zip(items, outs):
                dq_ref[rows, cols] = dq
                dk_ref[rows, cols] = dk_own[rows, cols] + dk[0:blk]
                dv_ref[rows, cols] = dv_own[rows, cols] + dv[0:blk]
                dk_own[rows, cols] = dk[blk:2 * blk]
                dv_own[rows, cols] = dv[blk:2 * blk]

        _attn_loop(dil, one_group, pair)

    width = 512 if dil == 1 else HEAD_LANES

    def spec(at, row_of):
        if dil == 1:
            return pl.BlockSpec((rows_per_step, width), lambda n: (row_of(n), at // width))
        return pl.BlockSpec((rows_per_step, width), lambda p, n: (row_of(n), at // width + p))

    cur = lambda at: spec(at, lambda n: n)
    prev = lambda at: spec(at, lambda n: jnp.maximum(n - 1, 0))
    own = spec(0, lambda n: 0)
    grid = (nb,) if dil == 1 else (4, nb)
    sems = ("arbitrary",) if dil == 1 else ("parallel", "arbitrary")
    dq, dk, dv, dk_last, dv_last = pl.pallas_call(
        body, name=name, grid=grid,
        in_specs=[cur(0), prev(0), cur(0), prev(C_AV), cur(C_AV), cur(0), cur(0), cur(512)],
        out_specs=[cur(0), prev(0), prev(0), own, own],
        out_shape=[jax.ShapeDtypeStruct((s, 512), F32)] * 3 + [jax.ShapeDtypeStruct((rows_per_step, 512), F32)] * 2,
        compiler_params=_params(*sems),
    )(qa, ka, ka, proj, proj, y_att, lse, dmixed)
    return dq, dk.at[s - rows_per_step:].set(dk_last), dv.at[s - rows_per_step:].set(dv_last)


def _attn_post(parts, proj, qg, kg, *, name):
    s = proj.shape[0]
    tm = ROW_TILE
    nblk = s // tm

    def body(*refs):
        ins, (q_ref, k_ref, qg_ref, kg_ref, dq_out, dk_out, dv_out, sums_ref) = refs[:9], refs[9:]
        i = pl.program_id(0)

        @pl.when(i == 0)
        def _():
            sums_ref[...] = jnp.zeros_like(sums_ref)

        dq = (ins[0][...] + ins[3][...]) + ins[6][...]
        dk = (ins[1][...] + ins[4][...]) + ins[7][...]
        dv = (ins[2][...] + ins[5][...]) + ins[8][...]
        dv_out[...] = dv.astype(BF16)
        for row, (x_ref, g_ref, dy, out, post) in enumerate(((q_ref, qg_ref, dq, dq_out, 0.125), (k_ref, kg_ref, dk, dk_out, 1.0))):
            x = x_ref[...]
            rs = lax.rsqrt(_head_sums(x * x) * (1.0 / 64) + EPS)
            xn = x * rs
            dy = dy * post
            sums_ref[row] += _fold8(dy * xn)
            dn = dy * g_ref[...]
            out[...] = (rs * (dn - xn * (_head_sums(dn * xn) * (1.0 / 64)))).astype(BF16)

        @pl.when(i == nblk - 1)
        def _():
            _spread_total(sums_ref)

    here = pl.BlockSpec((tm, 512), lambda i: (i, 0))
    col = lambda at: pl.BlockSpec((tm, 512), lambda i: (i, at // 512))
    vec = pl.BlockSpec((1, 512), lambda i: (0, 0))
    return pl.pallas_call(
        body, name=name, grid=(nblk,), in_specs=[here] * 9 + [col(C_AQ), col(C_AK), vec, vec],
        out_specs=[here, here, here, pl.BlockSpec((2, 8, 512), lambda i: (0, 0, 0))],
        out_shape=[jax.ShapeDtypeStruct((s, 512), BF16)] * 3 + [jax.ShapeDtypeStruct((2, 8, 512), F32)],
        compiler_params=_params("arbitrary"))(*[t for part in parts for t in part], proj, proj, qg, kg)


FFN_TM, FFN_TN = 512, 1408
HALO = 16


def _conv3(u_ref, halo_ref, w_ref, b_ref, first):
    u = u_ref[...].astype(F32)
    ext = jnp.concatenate([jnp.where(first, 0.0, halo_ref[...].astype(F32)), u], axis=0)
    u1 = pltpu.roll(ext, 1, 0)[HALO:]
    u2 = pltpu.roll(ext, 2, 0)[HALO:]
    return b_ref[...] + w_ref[0:1, :] * u2 + w_ref[1:2, :] * u1 + w_ref[2:3, :] * u


def _ffn_specs(tm, tn):
    nj = D_FF // tn
    blk = lambda half: pl.BlockSpec((tm, tn), lambda j, i: (i, j + half * nj))
    halo = lambda half: pl.BlockSpec((HALO, tn), lambda j, i: (jnp.maximum(i * (tm // HALO) - 1, 0), j + half * nj))
    wspec = lambda half: pl.BlockSpec((3, tn), lambda j, i: (0, j + half * nj))
    bspec = lambda half: pl.BlockSpec((1, tn), lambda j, i: (0, j + half * nj))
    return [blk(0), halo(0), blk(1), halo(1), wspec(0), wspec(1), bspec(0), bspec(1)]


def _conv_swiglu_fwd(u, conv_w, conv_b, *, name):
    s = u.shape[0]
    tm, tn = FFN_TM, FFN_TN

    def body(ug_ref, hg_ref, uv_ref, hv_ref, wg_ref, wv_ref, bg_ref, bv_ref, act_ref, uc_ref):
        first = pl.program_id(1) == 0
        cg = _conv3(ug_ref, hg_ref, wg_ref, bg_ref, first)
        cv = _conv3(uv_ref, hv_ref, wv_ref, bv_ref, first)
        act_ref[...] = (cg * _sigmoid(cg) * cv).astype(BF16)
        uc_ref[0] = cg.astype(BF16)
        uc_ref[1] = cv.astype(BF16)

    return pl.pallas_call(
        body, name=name, grid=(D_FF // tn, s // tm), in_specs=_ffn_specs(tm, tn),
        out_specs=[pl.BlockSpec((tm, tn), lambda j, i: (i, j)), pl.BlockSpec((2, tm, tn), lambda j, i: (0, i, j))],
        out_shape=[jax.ShapeDtypeStruct((s, D_FF), BF16), jax.ShapeDtypeStruct((2, s, D_FF), BF16)],
        compiler_params=_params("parallel", "parallel"))(u, u, u, u, conv_w, conv_w, conv_b, conv_b)


def _swiglu_bwd(uc, dact, *, name):
    _, s, _ = uc.shape
    tm, tn = FFN_TM, FFN_TN

    def body(uc_ref, da_ref, duc_ref, sums_ref):
        i = pl.program_id(1)

        @pl.when(i == 0)
        def _():
            sums_ref[...] = jnp.zeros_like(sums_ref)

        cg, cv, da = uc_ref[0].astype(F32), uc_ref[1].astype(F32), da_ref[...].astype(F32)
        sg = _sigmoid(cg)
        dg = da * cv * (sg * (1.0 + cg * (1.0 - sg)))
        dv = da * (cg * sg)
        duc_ref[0] = dg.astype(BF16)
        duc_ref[1] = dv.astype(BF16)
        sums_ref[0] += _fold8(dg)
        sums_ref[1] += _fold8(dv)

        @pl.when(i == s // tm - 1)
        def _():
            _spread_total(sums_ref)

    pair = pl.BlockSpec((2, tm, tn), lambda j, i: (0, i, j))
    return pl.pallas_call(
        body, name=name, grid=(D_FF // tn, s // tm), in_specs=[pair, pl.BlockSpec((tm, tn), lambda j, i: (i, j))],
        out_specs=[pair, pl.BlockSpec((2, 8, tn), lambda j, i: (0, 0, j))],
        out_shape=[jax.ShapeDtypeStruct((2, s, D_FF), BF16), jax.ShapeDtypeStruct((2, 8, D_FF), F32)],
        compiler_params=_params("parallel", "arbitrary"))(uc, dact)


def _conv_bwd(duc, u, conv_w, *, name):
    _, s, _ = duc.shape
    tm, tn = FFN_TM, FFN_TN
    nj, ni = D_FF // tn, s // tm

    def body(d_ref, halo_ref, u_ref, w_ref, du_ref, sums_ref):
        i = pl.program_id(2)

        @pl.when(i == 0)
        def _():
            sums_ref[...] = jnp.zeros_like(sums_ref)

        d = d_ref[0].astype(F32)
        ext = jnp.concatenate([d, jnp.where(i == ni - 1, 0.0, halo_ref[0].astype(F32))], axis=0)
        n = tm + HALO
        d1 = pltpu.roll(ext, n - 1, 0)[:tm]
        d2 = pltpu.roll(ext, n - 2, 0)[:tm]
        du_ref[...] = (w_ref[2:3, :] * d + w_ref[1:2, :] * d1 + w_ref[0:1, :] * d2).astype(BF16)
        uv = u_ref[...].astype(F32)
        for t, shifted in enumerate((d2, d1, d)):
            sums_ref[0, t] += _fold8(shifted * uv)

        @pl.when(i == ni - 1)
        def _():
            _spread_total(sums_ref)

    return pl.pallas_call(
        body, name=name, grid=(2, nj, ni),
        in_specs=[pl.BlockSpec((1, tm, tn), lambda g, j, i: (g, i, j)),
                  pl.BlockSpec((1, HALO, tn), lambda g, j, i: (g, jnp.minimum((i + 1) * (tm // HALO), s // HALO - 1), j)),
                  pl.BlockSpec((tm, tn), lambda g, j, i: (i, g * nj + j)),
                  pl.BlockSpec((3, tn), lambda g, j, i: (0, g * nj + j))],
        out_specs=[pl.BlockSpec((tm, tn), lambda g, j, i: (i, g * nj + j)),
                   pl.BlockSpec((1, 3, 8, tn), lambda g, j, i: (g, 0, 0, j))],
        out_shape=[jax.ShapeDtypeStruct((s, 2 * D_FF), BF16), jax.ShapeDtypeStruct((2, 3, 8, D_FF), F32)],
        compiler_params=_params("parallel", "parallel", "arbitrary"))(duc, duc, u, conv_w)


def _loss_head(x1, ffn, gate, target, *, name):
    s, d = x1.shape
    tm = ROW_TILE

    def body(x_ref, f_ref, g_ref, t_ref, dy_ref, df_ref, sums_ref):
        i = pl.program_id(0)

        @pl.when(i == 0)
        def _():
            sums_ref[...] = jnp.zeros_like(sums_ref)

        f = f_ref[...]
        err = x_ref[...] + g_ref[...] * f - t_ref[...]
        dy = err * (1.0 / d)
        dy_ref[...] = dy
        df_ref[...] = (g_ref[...] * dy).astype(BF16)
        sums_ref[0] += _fold8(dy * f)
        sums_ref[1] += _fold8(err * err)

        @pl.when(i == s // tm - 1)
        def _():
            _spread_total(sums_ref)

    row = pl.BlockSpec((tm, d), lambda i: (i, 0))
    return pl.pallas_call(
        body, name=name, grid=(s // tm,), in_specs=[row, row, pl.BlockSpec((1, d), lambda i: (0, 0)), row],
        out_specs=[row, row, pl.BlockSpec((2, 8, d), lambda i: (0, 0, 0))],
        out_shape=[jax.ShapeDtypeStruct((s, d), F32), jax.ShapeDtypeStruct((s, d), BF16), jax.ShapeDtypeStruct((2, 8, d), F32)],
        compiler_params=_params("arbitrary"))(x1, ffn, gate, target)


def _adamw(w, g, m, v, *, name):
    rows, cols = w.shape
    split = isinstance(g, tuple)
    if rows % 8 == 0 or rows <= ADAM_TILE:
        span = rows // 2 if split else rows
        tm = next((t for t in range(ADAM_TILE, 7, -8) if span % t == 0), span)
        shape, at, steps, per_half = (tm, cols), (lambda i: (i, 0)), rows // tm, span // tm
    else:
        shape, at, steps, per_half = (rows, ADAM_TILE), (lambda i: (0, i)), cols // ADAM_TILE, cols // ADAM_TILE // 2

    def update(gv, w_ref, m_ref, v_ref, d_ref, mo_ref, vo_ref):
        mn = ADAM_B1 * m_ref[...] + (1.0 - ADAM_B1) * gv
        vn = ADAM_B2 * v_ref[...] + (1.0 - ADAM_B2) * (gv * gv)
        m_hat = mn / (1.0 - ADAM_B1 ** ADAM_STEP)
        v_hat = vn / (1.0 - ADAM_B2 ** ADAM_STEP)
        d_ref[...] = -ADAM_LR * (m_hat / (jnp.sqrt(v_hat) + ADAM_EPS) + ADAM_WD * w_ref[...])
        mo_ref[...] = mn
        vo_ref[...] = vn

    out_shape = [jax.ShapeDtypeStruct((rows, cols), F32)] * (4 if split else 3)
    if not split:
        def body(w_ref, g_ref, m_ref, v_ref, d_ref, mo_ref, vo_ref):
            update(g_ref[...], w_ref, m_ref, v_ref, d_ref, mo_ref, vo_ref)

        blk = pl.BlockSpec(shape, at)
        return pl.pallas_call(body, name=name, grid=(steps,), in_specs=[blk] * 4, out_specs=[blk] * 3, out_shape=out_shape,
                              compiler_params=_params("parallel"))(w, g, m, v)

    mine, other, core = g

    def body(core_ref, w_ref, mine_ref, other_ref, m_ref, v_ref, d_ref, mo_ref, vo_ref, g_ref):
        gv = jnp.where(pl.program_id(0) // per_half == core_ref[0], mine_ref[...], other_ref[...])
        g_ref[...] = gv
        update(gv, w_ref, m_ref, v_ref, d_ref, mo_ref, vo_ref)

    blk = pl.BlockSpec(shape, lambda i, core_ref: at(i))
    half = pl.BlockSpec(shape, lambda i, core_ref: at(i % per_half))
    return pl.pallas_call(
        body, name=name, out_shape=out_shape, compiler_params=_params("parallel"),
        grid_spec=pltpu.PrefetchScalarGridSpec(num_scalar_prefetch=1, grid=(steps,), in_specs=[blk, half, half, blk, blk],
                                               out_specs=[blk] * 4))(core, w, mine, other, m, v)


def _colsum(t):
    return t[..., 0, :]


def _in_proj_layout(w_in):
    pad = jnp.zeros((w_in.shape[0], PROJ_W - C_LR - GLA_GATE_RANK), w_in.dtype)
    return jnp.concatenate([w_in[:, :1536], w_in[:, 1552:], w_in[:, 1536:1552], pad], axis=1)


def _in_proj_grad_layout(g):
    return jnp.concatenate([g[:, :1536], g[:, C_LR:C_LR + GLA_GATE_RANK], g[:, 1536:C_LR]], axis=1)


def _gate_layout(gla_w_gate):
    return jnp.pad(gla_w_gate, ((0, HEAD_LANES - GLA_GATE_RANK), (0, 0))).astype(BF16)


def _local_step(x, target, mod, wi, wo, ffn_weights, ffn_grads_ready, attn_grads_ready, conv_w, conv_b, wg, bg, gn, qg, kg, n1g, n2g):
    d = D_MODEL
    sh1, sc1, g1, sh2, sc2, g2 = [mod[:, i * d:(i + 1) * d] for i in range(6)]
    qg8, kg8 = jnp.tile(qg, (1, 8)), jnp.tile(kg, (1, 8))

    _, h1, h1_t = _norm_mod_fwd(x, None, None, n1g, sc1, sh1, name="norm1_fwd")
    proj = _mm(h1, wi, tm=1024, tn=PROJ_W, tk=d, name="in_proj")
    o_raw, y_gla, states = _gla_fwd(proj, wg, bg, gn, name="gla_fwd")
    qa, ka = _attn_prep(proj, qg8, kg8, name="attn_prep")
    sparse = [_dil_attn_fwd(qa, ka, proj, dil, name=f"attn_fwd_d{dil}") for dil in DILATIONS[1:]]
    mixed, y_att, lse = _dense_attn_fwd_merge(qa, ka, proj, sparse, y_gla, name="attn_fwd_d1_merge")
    attn_out = _mm(mixed, wo, tm=1024, tn=d, tk=d, name="out_proj")
    x1, h2, h2_t = _norm_mod_fwd(x, attn_out, g1, n2g, sc2, sh2, name="norm2_fwd")
    wup, wdown = ffn_weights(h2)
    u = _mm(h2, wup, out_dtype=BF16, tm=1024, tn=D_FF, tk=d, name="up_proj")
    act, uc = _conv_swiglu_fwd(u, conv_w, conv_b, name="conv_swiglu_fwd")
    ffn = _mm(act, wdown, tm=1024, tn=d, tk=D_FF, name="down_proj")
    dy, dffn, head_sums = _loss_head(x1, ffn, g2, target, name="loss_head")

    dact = _mm(dffn, wdown, tb=True, out_dtype=BF16, tm=1024, tn=D_FF, tk=d, name="down_proj_dx")
    g_wdown, g_wdown_b = _mm(act, dffn, ta=True, tm=1408, tn=d, tk=2048, also_bf16=True, name="down_proj_dw")
    duc, bias_sums = _swiglu_bwd(uc, dact, name="swiglu_bwd")
    du, tap_sums = _conv_bwd(duc, u, conv_w, name="conv_bwd")
    dh2 = _mm(du, wup, tb=True, tm=1024, tn=d, tk=D_FF, name="up_proj_dx")
    g_wup, g_wup_b = _mm(h2_t, du, tm=d, tn=1408, tk=2048, shard_cols=True, also_bf16=True, name="up_proj_dw")
    token = ffn_grads_ready(g_wup_b, g_wdown_b)
    g1_late = g1 if token is None else g1 + token[0:1, 0:1]
    dx1, dao, n2_sums = _norm_mod_bwd(x1, dh2, dy, n2g, sc2, attn_out, g1_late, name="norm2_bwd")

    dmixed = _mm(dao, wo, tb=True, tm=1024, tn=d, tk=d, name="out_proj_dx")
    g_wo = _mm(mixed, dao, ta=True, tm=d, tn=d, tk=1024, name="out_proj_dw")
    dgq, dgk, dgv, dgr, dlr, g_wg, gla_sums = _gla_bwd(proj, wg, bg, gn, o_raw, states, dmixed, name="gla_bwd")
    parts = [_dil_attn_bwd(qa, ka, proj, y_att, lse, dmixed, dil, name=f"attn_bwd_d{dil}") for dil in DILATIONS]
    daq, dak, dav, qk_sums = _attn_post(parts, proj, qg8, kg8, name="attn_post")
    dproj = jnp.concatenate([dgq, dgk, dgv, dgr, daq, dak, dav, dlr], axis=1)
    g_wi = _mm(h1_t, dproj, tm=512, tn=PROJ_W, tk=2048, name="in_proj_dw")
    token = attn_grads_ready(g_wi, g_wo)
    wi_late = wi if token is None else wi + token[0:1, 0:1].astype(BF16)
    dh1 = _mm(dproj, wi_late, tb=True, tm=1024, tn=d, tk=PROJ_W, name="in_proj_dx")
    grad_x, _, n1_sums = _norm_mod_bwd(x, dh1, dx1, n1g, sc1, None, None, name="norm1_bwd")

    n1, n2, hs, taps, cb = _colsum(n1_sums), _colsum(n2_sums), _colsum(head_sums), _colsum(tap_sums), _colsum(bias_sums)
    gs, qs = _colsum(gla_sums), _colsum(qk_sums)
    dmod = jnp.concatenate([n1[1], n1[0] * n1g[0], n2[2], n2[1], n2[0] * n2g[0], hs[0]])
    small = dict(
        dmod=dmod,
        norm1_g=n1[0] * (1.0 + sc1[0]), norm2_g=n2[0] * (1.0 + sc2[0]),
        gla_w_gate=g_wg[:GLA_GATE_RANK], gla_b_gate=gs[0, :256], gla_norm_g=gs[1].reshape(4, 128).sum(axis=0),
        q_norm_g=qs[0].reshape(8, 64).sum(axis=0), k_norm_g=qs[1].reshape(8, 64).sum(axis=0),
        conv_w=jnp.concatenate([taps[0], taps[1]], axis=1), conv_b=jnp.concatenate([cb[0], cb[1]]),
    )
    return head_sums[1], grad_x, (g_wi, g_wo, g_wup, g_wdown), small


N_DEV, N_CHIP = 8, 4
ANY = pl.BlockSpec(memory_space=pl.ANY)
VMEM_SPEC = pl.BlockSpec(memory_space=pltpu.VMEM)


def _place():
    x, y, c = lax.axis_index("x"), lax.axis_index("y"), lax.axis_index("c")
    other_chips = [(1 - x, y), (x, 1 - y), (1 - x, 1 - y)]
    return x, y, c, (x, y, 1 - c), other_chips


def _all_gather_small(v, *, name):
    m, n = v.shape

    def body(v_ref, out_ref, send_sems, recv_sems, local_sem):
        x, y, c, sibling, chips = _place()
        me = (x, y, c)

        def rows(px, py, pc):
            return out_ref.at[pl.ds((4 * px + 2 * py + pc) * m, m), :]

        def copy(k, block, to, src=None):
            return pltpu.make_async_remote_copy(
                src_ref=rows(*block) if src is None else src, dst_ref=rows(*block), send_sem=send_sems.at[k],
                recv_sem=recv_sems.at[k], device_id=to, device_id_type=MESH)

        mine = pltpu.make_async_copy(v_ref, rows(*me), local_sem)
        mine.start()
        first = [copy(0, me, sibling, src=v_ref)]
        first += [copy(1 + j, me, (*chip, c), src=v_ref) for j, chip in enumerate(chips)]
        for cp in first:
            cp.start()
        passed = [copy(4 + j, (*chip, c), sibling) for j, chip in enumerate(chips)]
        for j, chip in enumerate(chips):
            copy(1 + j, (*chip, c), me).wait_recv()
            passed[j].start()
        copy(0, sibling, me).wait_recv()
        for j, chip in enumerate(chips):
            copy(4 + j, (*chip, 1 - c), me).wait_recv()
        for cp in first + passed:
            cp.wait_send()
        mine.wait()

    return pl.pallas_call(
        body, name=name, out_shape=jax.ShapeDtypeStruct((N_DEV * m, n), v.dtype), in_specs=[VMEM_SPEC], out_specs=VMEM_SPEC,
        scratch_shapes=[pltpu.SemaphoreType.DMA((7,)), pltpu.SemaphoreType.DMA((7,)), pltpu.SemaphoreType.DMA],
    )(v)


def _gather_weight_shards(shards, *, name):
    nw = len(shards)

    def body(*refs):
        srcs, outs, (send_sems, recv_sems) = refs[:nw], refs[nw:2 * nw], refs[2 * nw:]
        x, y, c, sibling, chips = _place()
        index = lambda chip: 2 * chip[0] + chip[1]

        def copy(w, k, src, dst, to):
            return pltpu.make_async_remote_copy(src_ref=src, dst_ref=dst, send_sem=send_sems.at[6 * w + k],
                                                recv_sem=recv_sems.at[6 * w + k], device_id=to, device_id_type=MESH)

        sent = []
        for w, (src_ref, out_ref) in enumerate(zip(srcs, outs)):
            for k, chip in enumerate(chips):
                sent.append(copy(w, k, src_ref.at[c], out_ref.at[2 * x + y, c], (*chip, c)))
                sent[-1].start()
        for w, out_ref in enumerate(outs):
            for k, chip in enumerate(chips):
                landed = out_ref.at[index(chip), c]
                copy(w, k, landed, landed, (*chip, c)).wait_recv()
                sent.append(copy(w, 3 + k, landed, landed, sibling))
                sent[-1].start()
        for w, out_ref in enumerate(outs):
            for k, chip in enumerate(chips):
                passed_on = out_ref.at[index(chip), 1 - c]
                copy(w, 3 + k, passed_on, passed_on, sibling).wait_recv()
        for cp in sent:
            cp.wait_send()

    return pl.pallas_call(
        body, name=name, out_shape=[jax.ShapeDtypeStruct((N_CHIP, *s.shape), s.dtype) for s in shards],
        in_specs=[ANY] * nw, out_specs=[ANY] * nw,
        scratch_shapes=[pltpu.SemaphoreType.DMA((6 * nw,)), pltpu.SemaphoreType.DMA((6 * nw,))],
    )(*shards)


HBM_SPEC = pl.BlockSpec(memory_space=pltpu.HBM)
SEM_SPEC = pl.BlockSpec(memory_space=pltpu.SEMAPHORE)
DATAFLOW_EFFECT = pltpu.SideEffectType.DATAFLOW_SIDE_EFFECTING


def _late_copies(srcs, lands, send_sems, recv_sems):
    x, y, c, _, chips = _place()
    return [pltpu.make_async_remote_copy(
        src_ref=src.at[c], dst_ref=land.at[2 * x + y, c], send_sem=send_sems.at[6 * w + 2 * r + core],
        recv_sem=recv_sems.at[6 * w + 2 * r + c], device_id=(*chip, core), device_id_type=MESH)
        for w, (src, land) in enumerate(zip(srcs, lands)) for r, chip in enumerate(chips) for core in range(2)]


def _gather_late_start(own, after, *, name):
    nw = len(own)

    def body(*refs):
        srcs, lands, send_sems, recv_sems, token = refs[:nw], refs[nw:2 * nw], refs[2 * nw + 1], refs[2 * nw + 2], refs[-1]
        for cp in _late_copies(srcs, lands, send_sems, recv_sems):
            cp.start()
        token[...] = jnp.zeros_like(token)

    lands = [pltpu.with_memory_space_constraint(lax.empty((N_CHIP, *s.shape), s.dtype), pltpu.HBM) for s in own]
    own = [pltpu.with_memory_space_constraint(s, pltpu.HBM) for s in own]
    out = pl.pallas_call(
        body, name=name,
        out_shape=(pltpu.SemaphoreType.DMA((6 * nw,)), pltpu.SemaphoreType.DMA((6 * nw,)),
                   *[pltpu.HBM(s.shape, s.dtype) for s in own], *[pltpu.HBM(s.shape, s.dtype) for s in lands],
                   jax.ShapeDtypeStruct((8, 128), F32)),
        in_specs=[HBM_SPEC] * (2 * nw) + [ANY], out_specs=(SEM_SPEC, SEM_SPEC, *[HBM_SPEC] * (2 * nw), VMEM_SPEC),
        input_output_aliases={i: 2 + i for i in range(2 * nw)},
        compiler_params=pltpu.CompilerParams(has_side_effects=DATAFLOW_EFFECT))(*own, *lands, after)
    return out[0], out[1], out[2:2 + nw], out[2 + nw:2 + 2 * nw], out[-1]


def _gather_late_wait(send_sems, recv_sems, own, lands, after, *, name):
    nw = len(own)

    def body(*refs):
        srcs, lands_in, send_sems, recv_sems = refs[:nw], refs[nw:2 * nw], refs[2 * nw], refs[2 * nw + 1]
        x, y, c, _, chips = _place()
        for cp in _late_copies(srcs, lands_in, send_sems, recv_sems):
            cp.wait_send()
        for w, (src, land) in enumerate(zip(srcs, lands_in)):
            for r, chip in enumerate(chips):
                for core in range(2):
                    pltpu.make_async_remote_copy(
                        src_ref=src.at[c], dst_ref=land.at[2 * chip[0] + chip[1], core], send_sem=send_sems.at[6 * w + 2 * r + core],
                        recv_sem=recv_sems.at[6 * w + 2 * r + core], device_id=(*chip, core), device_id_type=MESH).wait_recv()

    out = pl.pallas_call(
        body, name=name, out_shape=(*[pltpu.HBM(s.shape, s.dtype) for s in own], *[pltpu.HBM(s.shape, s.dtype) for s in lands]),
        in_specs=[HBM_SPEC] * (2 * nw) + [SEM_SPEC, SEM_SPEC, ANY], out_specs=tuple([HBM_SPEC] * (2 * nw)),
        input_output_aliases={i: i for i in range(2 * nw)},
        compiler_params=pltpu.CompilerParams(has_side_effects=DATAFLOW_EFFECT))(*own, *lands, send_sems, recv_sems, after)
    return out[:nw], out[nw:]


def _direct_reduce_copies(srcs, lands, send_sems, recv_sems):
    x, y, c, _, _ = _place()
    cps = []
    for w, (src, land) in enumerate(zip(srcs, lands)):
        for rel in range(1, N_DEV):
            tx, ty, tc = (1 - x if rel & 4 else x), (1 - y if rel & 2 else y), (1 - c if rel & 1 else c)
            cps.append(pltpu.make_async_remote_copy(
                src_ref=src.at[2 * tx + ty, tc], dst_ref=land.at[rel - 1], send_sem=send_sems.at[7 * w + rel - 1],
                recv_sem=recv_sems.at[7 * w + rel - 1], device_id=(tx, ty, tc), device_id_type=MESH))
    return cps


def _direct_reduce_start(grads, *, name):
    nw = len(grads)

    def body(*refs):
        srcs, lands, send_sems, recv_sems, token = refs[:nw], refs[nw:2 * nw], refs[2 * nw], refs[2 * nw + 1], refs[-1]
        for cp in _direct_reduce_copies(srcs, lands, send_sems, recv_sems):
            cp.start()
        token[...] = jnp.zeros_like(token)

    lands = [pltpu.with_memory_space_constraint(lax.empty((N_DEV - 1, *g.shape[2:]), g.dtype), pltpu.HBM) for g in grads]
    grads = [pltpu.with_memory_space_constraint(g, pltpu.HBM) for g in grads]
    out = pl.pallas_call(
        body, name=name,
        out_shape=(pltpu.SemaphoreType.DMA((7 * nw,)), pltpu.SemaphoreType.DMA((7 * nw,)),
                   *[pltpu.HBM(g.shape, g.dtype) for g in grads], *[pltpu.HBM(t.shape, t.dtype) for t in lands],
                   jax.ShapeDtypeStruct((8, 128), F32)),
        in_specs=[HBM_SPEC] * (2 * nw), out_specs=(SEM_SPEC, SEM_SPEC, *[HBM_SPEC] * (2 * nw), VMEM_SPEC),
        input_output_aliases={i: 2 + i for i in range(2 * nw)},
        compiler_params=pltpu.CompilerParams(has_side_effects=DATAFLOW_EFFECT))(*grads, *lands)
    return out[0], out[1], out[2:2 + nw], out[2 + nw:2 + 2 * nw], out[-1]


def _direct_reduce_wait(send_sems, recv_sems, grads, lands, after, *, name):
    nw = len(grads)

    def body(*refs):
        srcs, lands_in, send_sems, recv_sems = refs[:nw], refs[nw:2 * nw], refs[2 * nw], refs[2 * nw + 1]
        cps = _direct_reduce_copies(srcs, lands_in, send_sems, recv_sems)
        for cp in cps:
            cp.wait_send()
        for cp in cps:
            cp.wait_recv()

    out = pl.pallas_call(
        body, name=name, out_shape=(*[pltpu.HBM(g.shape, g.dtype) for g in grads], *[pltpu.HBM(t.shape, t.dtype) for t in lands]),
        in_specs=[HBM_SPEC] * (2 * nw) + [SEM_SPEC, SEM_SPEC, ANY], out_specs=tuple([HBM_SPEC] * (2 * nw)),
        input_output_aliases={i: i for i in range(2 * nw)},
        compiler_params=pltpu.CompilerParams(has_side_effects=DATAFLOW_EFFECT))(*grads, *lands, send_sems, recv_sems, after)
    return out[nw:]


def _direct_reduce_add(grad, landed, chip, core, *, name):
    _, r, n = grad.shape
    half = r // 2
    tr = _row_tile(half)
    nb = half // tr

    def body(chip_ref, core_ref, g_ref, t_ref, o_ref):
        acc = g_ref[0]
        for k in range(N_DEV - 1):
            acc = acc + t_ref[k].astype(F32)
        o_ref[...] = acc

    return pl.pallas_call(
        body, name=name,
        grid_spec=pltpu.PrefetchScalarGridSpec(
            num_scalar_prefetch=2, grid=(nb,),
            in_specs=[pl.BlockSpec((1, tr, n), lambda i, chip_ref, core_ref: (chip_ref[0], core_ref[0] * nb + i, 0)),
                      pl.BlockSpec((N_DEV - 1, tr, n), lambda i, chip_ref, core_ref: (0, i, 0))],
            out_specs=pl.BlockSpec((tr, n), lambda i, chip_ref, core_ref: (i, 0))),
        out_shape=jax.ShapeDtypeStruct((half, n), F32), compiler_params=_params("parallel"))(chip, core, grad, landed)


def _share_halves(halves, *, name):
    nw = len(halves)

    def body(*refs):
        srcs, outs, (send_sems, recv_sems) = refs[:nw], refs[nw:2 * nw], refs[2 * nw:]
        _, _, _, sibling, _ = _place()
        cps = [pltpu.make_async_remote_copy(src_ref=src_ref, dst_ref=out_ref, send_sem=send_sems.at[w], recv_sem=recv_sems.at[w],
                                            device_id=sibling, device_id_type=MESH)
               for w, (src_ref, out_ref) in enumerate(zip(srcs, outs))]
        for cp in cps:
            cp.start()
        for cp in cps:
            cp.wait()

    return pl.pallas_call(
        body, name=name, out_shape=[jax.ShapeDtypeStruct(h.shape, h.dtype) for h in halves],
        in_specs=[ANY] * nw, out_specs=[ANY] * nw,
        scratch_shapes=[pltpu.SemaphoreType.DMA((nw,)), pltpu.SemaphoreType.DMA((nw,))])(*halves)


def _row_tile(rows, limit=256):
    return next(t for t in range(limit, 15, -16) if rows % t == 0)


def _sum_devices(gathered, *, name):
    _, m, n = gathered.shape

    def body(g_ref, tot_ref, loss_ref):
        tot = g_ref[0]
        for dev in range(1, N_DEV):
            tot = tot + g_ref[dev]
        tot_ref[...] = tot
        loss_ref[...] = jnp.full((8, n), (0.5 / D_MODEL) * jnp.sum(tot[0:8]), F32)

    return pl.pallas_call(body, name=name, in_specs=[VMEM_SPEC], out_specs=[VMEM_SPEC, VMEM_SPEC],
                          out_shape=[jax.ShapeDtypeStruct((m, n), F32), jax.ShapeDtypeStruct((8, n), F32)])(gathered)


def _ada_mod(cond_all, w_ada_shard, *, name):
    tn = 512

    def body(a_ref, b_ref, o_ref):
        o_ref[...] = _nn(a_ref[...], b_ref[...], precision=HIGHEST)

    return pl.pallas_call(
        body, name=name, grid=(w_ada_shard.shape[1] // tn,),
        in_specs=[pl.BlockSpec(cond_all.shape, lambda j: (0, 0)), pl.BlockSpec((D_MODEL, tn), lambda j: (0, j))],
        out_specs=pl.BlockSpec((N_DEV, tn), lambda j: (0, j)),
        out_shape=jax.ShapeDtypeStruct((N_DEV, w_ada_shard.shape[1]), F32), compiler_params=_params("parallel"))(cond_all, w_ada_shard)


def _ada_grad(cond_all, dmod_cols, *, name):
    tm = 256

    def body(a_ref, b_ref, o_ref):
        o_ref[...] = lax.dot_general(a_ref[...], b_ref[...], (((0,), (0,)), ((), ())), precision=HIGHEST,
                                     preferred_element_type=F32)

    return pl.pallas_call(
        body, name=name, grid=(D_MODEL // tm,),
        in_specs=[pl.BlockSpec((N_DEV, tm), lambda i: (0, i)), pl.BlockSpec(dmod_cols.shape, lambda i: (0, 0))],
        out_specs=pl.BlockSpec((tm, dmod_cols.shape[1]), lambda i: (i, 0)),
        out_shape=jax.ShapeDtypeStruct((D_MODEL, dmod_cols.shape[1]), F32), compiler_params=_params("parallel"))(cond_all, dmod_cols)


def _silu_rows(c8, *, name):
    def body(c_ref, o_ref):
        cv = c_ref[...]
        o_ref[...] = cv * _sigmoid(cv)

    return pl.pallas_call(body, name=name, in_specs=[VMEM_SPEC], out_specs=VMEM_SPEC,
                          out_shape=jax.ShapeDtypeStruct(c8.shape, F32))(c8)


def _rows128(t, rows=None):
    flat = t.reshape(-1, 128)
    return flat if rows is None else jnp.pad(flat, ((0, rows - flat.shape[0]), (0, 0)))


def _from_col_shards(shards, r, n):
    return shards.reshape(N_CHIP, r, n).transpose(1, 0, 2).reshape(r, N_CHIP * n)


def kernel(x, c, w_ada, b_ada, norm1_g, w_in, gla_w_gate, gla_b_gate, gla_norm_g, q_norm_g, k_norm_g, w_out, norm2_g, w_up, conv_w, conv_b, w_down, loss_target, m_w_ada, m_b_ada, m_norm1_g, m_w_in, m_gla_w_gate, m_gla_b_gate, m_gla_norm_g, m_q_norm_g, m_k_norm_g, m_w_out, m_norm2_g, m_w_up, m_conv_w, m_conv_b, m_w_down, v_w_ada, v_b_ada, v_norm1_g, v_w_in, v_gla_w_gate, v_gla_b_gate, v_gla_norm_g, v_q_norm_g, v_k_norm_g, v_w_out, v_norm2_g, v_w_up, v_conv_w, v_conv_b, v_w_down):
    d = D_MODEL
    ax, ay, ac = lax.axis_index("x"), lax.axis_index("y"), lax.axis_index("c")
    chip, dev = 2 * ax + ay, 4 * ax + 2 * ay + ac

    cond = _silu_rows(jnp.broadcast_to(c, (8, d)), name="cond_silu")[0:1]
    small_in = jnp.concatenate([_rows128(cond), _rows128(conv_w[0]), _rows128(gla_w_gate[0])], axis=0)
    small_in = _rows128(small_in, 56)
    got = _all_gather_small(small_in, name="gather_small").reshape(N_DEV, 56, 128)
    cond_all = got[:, 0:8].reshape(N_DEV, d)
    conv_w_full = _from_col_shards(got[0::2, 8:41].reshape(N_CHIP, 3 * 1408 // 128, 128), 3, 1408)
    gate_full = _from_col_shards(got[0::2, 41:49].reshape(N_CHIP, 16 * 64 // 128, 128), GLA_GATE_RANK, 64)
    mod_part = _ada_mod(cond_all, w_ada[0], name="ada_mod")
    mod_got = _all_gather_small(_rows128(mod_part), name="gather_mod").reshape(N_DEV, N_DEV, 1536)
    mod_all = mod_got[0::2].transpose(1, 0, 2).reshape(N_DEV, 6 * d) + b_ada
    mod = lax.dynamic_slice_in_dim(mod_all, dev, 1, axis=0)

    own = [w[0].astype(BF16).reshape(2, w.shape[1] // 2, w.shape[2]) for w in (w_in, w_out, w_up, w_down)]
    with_own = lambda got, mine: [lax.dynamic_update_index_in_dim(t, o, chip, 0) for t, o in zip(got, mine)]
    got_in, got_out = with_own(_gather_weight_shards(own[:2], name="gather_weights"), own[:2])
    w_in_full = got_in.reshape(N_CHIP, d, 772).transpose(1, 0, 2).reshape(d, N_CHIP * 772)
    w_out_full = got_out.reshape(d, d)
    exchanged = mod_all[0:1, 0:1] + got_in[0, 0, 0:1, 0:1].astype(F32)
    send_sems, recv_sems, own_thru, lands, token = _gather_late_start(own[2:], exchanged, name="gather_late_start")
    mod = mod + token[0:1, 0:1]

    def ffn_weights(after):
        mine, landed = _gather_late_wait(send_sems, recv_sems, own_thru, lands, after, name="gather_late_wait")
        got_up, got_down = with_own(landed, mine)
        return got_up.reshape(N_CHIP, d, 1408).transpose(1, 0, 2).reshape(d, 2 * D_FF), got_down.reshape(D_FF, d)

    ffn_reduce, attn_reduce, attn_parts = [], [], []
    halves_of = lambda g: g.reshape(N_CHIP, 2, g.shape[-2] // 2, g.shape[-1])

    def ffn_grads_ready(g_wup_b, g_wdown_b):
        ffn_reduce.extend(_direct_reduce_start([halves_of(g_wup_b), halves_of(g_wdown_b.reshape(N_CHIP, D_FF // N_CHIP, d))],
                                               name="reduce_ffn_start"))
        return ffn_reduce[4]

    def attn_grads_ready(g_wi, g_wo):
        attn_parts.extend([_in_proj_grad_layout(g_wi).reshape(d, N_CHIP, 772).transpose(1, 0, 2), g_wo.reshape(N_CHIP, d // N_CHIP, d)])
        attn_reduce.extend(_direct_reduce_start([halves_of(g.astype(BF16)) for g in attn_parts], name="reduce_attn_start"))
        return attn_reduce[4]

    err2, grad_x, (g_wi, g_wo, g_wup, g_wdown), small = _local_step(
        x[0], loss_target[0], mod, _in_proj_layout(w_in_full), w_out_full, ffn_weights, ffn_grads_ready, attn_grads_ready,
        conv_w_full, conv_b,
        _gate_layout(gate_full), gla_b_gate, gla_norm_g, q_norm_g, k_norm_g, norm1_g, norm2_g)

    pieces = [err2[0], small["dmod"], small["norm1_g"], small["norm2_g"], small["gla_w_gate"].reshape(-1), small["gla_b_gate"],
              small["gla_norm_g"], small["q_norm_g"], small["k_norm_g"], small["conv_w"].reshape(-1), small["conv_b"]]
    sizes = [p.shape[0] for p in pieces]
    at = [sum(sizes[:i]) for i in range(len(sizes) + 1)]
    vec = _rows128(jnp.concatenate(pieces), 288)
    got = _all_gather_small(vec, name="gather_grads").reshape(N_DEV, 288, 128)
    total, loss8 = _sum_devices(got, name="sum_devices")
    total = total.reshape(-1)
    seg = lambda i: total[at[i]:at[i + 1]]
    dmod_all = got.reshape(N_DEV, -1)[:, at[1]:at[2]]
    g_small = dict(
        b_ada=seg(1)[None], norm1_g=seg(2)[None], norm2_g=seg(3)[None],
        gla_w_gate=lax.dynamic_slice_in_dim(seg(4).reshape(GLA_GATE_RANK, 256), chip * 64, 64, axis=1),
        gla_b_gate=seg(5)[None], gla_norm_g=seg(6)[None], q_norm_g=seg(7)[None], k_norm_g=seg(8)[None],
        conv_w=lax.dynamic_slice_in_dim(seg(9).reshape(3, 2 * D_FF), chip * 1408, 1408, axis=1), conv_b=seg(10)[None])
    dmod_cols = lax.dynamic_slice_in_dim(dmod_all.reshape(N_DEV, 6 * d), chip * 1536, 1536, axis=1)
    g_w_ada = _ada_grad(cond_all, dmod_cols, name="ada_grad")

    core_id, chip_id = jnp.reshape(ac, (1,)).astype(jnp.int32), jnp.reshape(chip, (1,)).astype(jnp.int32)
    landed = (_direct_reduce_wait(*attn_reduce[:4], grad_x, name="reduce_attn_wait")
              + _direct_reduce_wait(*ffn_reduce[:4], grad_x, name="reduce_ffn_wait"))
    own = attn_parts + [g_wup, g_wdown.reshape(N_CHIP, D_FF // N_CHIP, d)]
    summed = [_direct_reduce_add(g, t, chip_id, core_id, name=f"reduce_add_{tag}")
              for g, t, tag in zip(own, landed, ("w_in", "w_out", "w_up", "w_down"))]
    others = _share_halves(summed, name="share_pair")

    grads = dict(w_ada=g_w_ada, **g_small, **dict(zip(("w_in", "w_out", "w_up", "w_down"), zip(summed, others))))
    names = ["w_ada", "b_ada", "norm1_g", "w_in", "gla_w_gate", "gla_b_gate", "gla_norm_g", "q_norm_g", "k_norm_g", "w_out",
             "norm2_g", "w_up", "conv_w", "conv_b", "w_down"]
    ws = dict(w_ada=w_ada, b_ada=b_ada, norm1_g=norm1_g, w_in=w_in, gla_w_gate=gla_w_gate, gla_b_gate=gla_b_gate,
              gla_norm_g=gla_norm_g, q_norm_g=q_norm_g, k_norm_g=k_norm_g, w_out=w_out, norm2_g=norm2_g, w_up=w_up,
              conv_w=conv_w, conv_b=conv_b, w_down=w_down)
    ms = dict(w_ada=m_w_ada, b_ada=m_b_ada, norm1_g=m_norm1_g, w_in=m_w_in, gla_w_gate=m_gla_w_gate, gla_b_gate=m_gla_b_gate,
              gla_norm_g=m_gla_norm_g, q_norm_g=m_q_norm_g, k_norm_g=m_k_norm_g, w_out=m_w_out, norm2_g=m_norm2_g, w_up=m_w_up,
              conv_w=m_conv_w, conv_b=m_conv_b, w_down=m_w_down)
    vs = dict(w_ada=v_w_ada, b_ada=v_b_ada, norm1_g=v_norm1_g, w_in=v_w_in, gla_w_gate=v_gla_w_gate, gla_b_gate=v_gla_b_gate,
              gla_norm_g=v_gla_norm_g, q_norm_g=v_q_norm_g, k_norm_g=v_k_norm_g, w_out=v_w_out, norm2_g=v_norm2_g, w_up=v_w_up,
              conv_w=v_conv_w, conv_b=v_conv_b, w_down=v_w_down)
    g_out, d_out, m_out, v_out = [], [], [], []
    for nm in names:
        shape = ws[nm].shape
        flip = (lambda t: t.T) if shape[-1] % 128 and shape[-2] % 128 == 0 else (lambda t: t)
        w2 = flip(ws[nm].reshape(shape[-2:]))
        if isinstance(grads[nm], tuple):
            mine, other = grads[nm]
            dl, mn, vn, g2 = _adamw(w2, (flip(mine), flip(other), core_id), flip(ms[nm].reshape(shape[-2:])),
                                    flip(vs[nm].reshape(shape[-2:])), name=f"adamw_{nm}")
        else:
            g2 = flip(grads[nm].reshape(shape[-2:]))
            dl, mn, vn = _adamw(w2, g2, flip(ms[nm].reshape(shape[-2:])), flip(vs[nm].reshape(shape[-2:])), name=f"adamw_{nm}")
        for outs, t in ((g_out, g2), (d_out, dl), (m_out, mn), (v_out, vn)):
            outs.append(flip(t).reshape(shape))
    return (loss8[0, 0], grad_x[None], *g_out, *d_out, *m_out, *v_out)
```

```python
import functools

import jax
import jax.numpy as jnp
from jax import lax
from jax.experimental import pallas as pl
from jax.experimental.pallas import tpu as pltpu

F32, BF16 = jnp.float32, jnp.bfloat16
HIGHEST = lax.Precision.HIGHEST
MESH = pl.DeviceIdType.MESH

D_MODEL = 1024
GLA_CHUNK = 64
GLA_GATE_TAU = 16.0
GLA_GATE_RANK = 16
HEAD_LANES = 128
ATTN_BLOCK = 128
DILATIONS = (1, 4, 16)
ALIBI_SLOPES = tuple(2.0 ** (-(h + 1)) for h in range(8))
D_FF = 2816
EPS = 1e-6
C_GQ, C_GK, C_GV, C_GR, C_AQ, C_AK, C_AV, C_LR, PROJ_W = 0, 256, 512, 1024, 1536, 2048, 2560, 3072, 3200
ADAM_LR, ADAM_B1, ADAM_B2, ADAM_EPS, ADAM_WD, ADAM_STEP = 0.001, 0.9, 0.999, 1e-08, 0.01, 10
VMEM_LIMIT_BYTES = 56 * 1024 * 1024
ROW_TILE = 512
ADAM_TILE = 256


def _params(*sem):
    return pltpu.CompilerParams(dimension_semantics=sem or None, vmem_limit_bytes=VMEM_LIMIT_BYTES)


def _nt(a, b):
    return lax.dot_general(a, b, (((1,), (1,)), ((), ())), preferred_element_type=F32)


def _tn(a, b):
    return lax.dot_general(a, b, (((0,), (0,)), ((), ())), preferred_element_type=F32)


def _nn(a, b, precision=None):
    return jnp.dot(a, b, preferred_element_type=F32, precision=precision)


def _split3(v):
    hi = v.astype(BF16)
    rest = v - hi.astype(F32)
    mid = rest.astype(BF16)
    return hi, mid, (rest - mid.astype(F32)).astype(BF16)


def _sum_right(v, ones):
    hi, mid, lo = _split3(v)
    return (_nn(lo, ones) + _nn(mid, ones)) + _nn(hi, ones)


def _sum_left(ones, v):
    hi, mid, lo = _split3(v)
    return (_nn(ones, lo) + _nn(ones, mid)) + _nn(ones, hi)


def _fold8(v):
    return v.reshape(v.shape[0] // 8, 8, v.shape[1]).sum(axis=0)


def _spread_total(ref):
    t = ref[...]
    ref[...] = jnp.broadcast_to(jnp.sum(t, axis=-2, keepdims=True), t.shape)


def _sigmoid(x):
    return 1.0 / (1.0 + jnp.exp(-x))


def _mm(a, b, *, ta=False, tb=False, out_dtype=F32, tm, tn, tk, shard_cols=False, also_bf16=False, name):
    (k_a, m) = a.shape if ta else a.shape[::-1]
    (k_b, n) = b.shape[::-1] if tb else b.shape
    assert k_a == k_b and m % tm == 0 and n % tn == 0 and k_a % tk == 0, (name, a.shape, b.shape)
    nk = k_a // tk
    assert nk == 1 or out_dtype == F32, name
    dims = (((0 if ta else 1,), (1 if tb else 0,)), ((), ()))

    def body(a_ref, b_ref, o_ref, *rounded):
        k = pl.program_id(2)
        part = lax.dot_general(a_ref[...].astype(BF16), b_ref[...].astype(BF16), dims, preferred_element_type=F32)
        if nk == 1:
            o_ref[...] = part.astype(out_dtype)
        else:
            @pl.when(k == 0)
            def _():
                o_ref[...] = part

            @pl.when(k > 0)
            def _():
                o_ref[...] += part

        if also_bf16:
            @pl.when(k == nk - 1)
            def _():
                rounded[0][...] = o_ref[...].astype(BF16)

    a_spec = pl.BlockSpec((tk, tm), lambda i, j, k: (k, i)) if ta else pl.BlockSpec((tm, tk), lambda i, j, k: (i, k))
    b_spec = pl.BlockSpec((tn, tk), lambda i, j, k: (j, k)) if tb else pl.BlockSpec((tk, tn), lambda i, j, k: (k, j))
    if shard_cols:
        o_spec, o_shape = pl.BlockSpec((None, tm, tn), lambda i, j, k: (j, i, 0)), (n // tn, m, tn)
    else:
        o_spec, o_shape = pl.BlockSpec((tm, tn), lambda i, j, k: (i, j)), (m, n)
    shapes = [jax.ShapeDtypeStruct(o_shape, out_dtype)] + ([jax.ShapeDtypeStruct(o_shape, BF16)] if also_bf16 else [])
    out = pl.pallas_call(
        body, name=name, grid=(m // tm, n // tn, nk), in_specs=[a_spec, b_spec], out_specs=[o_spec] * len(shapes),
        out_shape=shapes, compiler_params=_params("parallel", "parallel", "arbitrary"))(a, b)
    return out if also_bf16 else out[0]


def _norm_mod_fwd(x, branch, gate, gain, scale, shift, *, name):
    s, d = x.shape
    tm = 2 * ROW_TILE
    has_branch = branch is not None

    def body(*refs):
        if has_branch:
            x_ref, br_ref, gate_ref, gain_ref, sc_ref, sh_ref, x1_ref, h_ref, ht_ref = refs
            xv = x_ref[...] + gate_ref[...] * br_ref[...]
            x1_ref[...] = xv
        else:
            x_ref, gain_ref, sc_ref, sh_ref, h_ref, ht_ref = refs
            xv = x_ref[...]
        r = lax.rsqrt(jnp.mean(xv * xv, axis=-1, keepdims=True) + EPS)
        h = (xv * r) * gain_ref[...] * (1.0 + sc_ref[...]) + sh_ref[...]
        h_ref[...] = h.astype(BF16)
        ht_ref[...] = h.T.astype(BF16)

    row = pl.BlockSpec((tm, d), lambda i: (i, 0))
    col = pl.BlockSpec((d, tm), lambda i: (0, i))
    vec = pl.BlockSpec((1, d), lambda i: (0, 0))
    h_shapes = [jax.ShapeDtypeStruct((s, d), BF16), jax.ShapeDtypeStruct((d, s), BF16)]
    if has_branch:
        return pl.pallas_call(
            body, name=name, grid=(s // tm,), in_specs=[row, row, vec, vec, vec, vec], out_specs=[row, row, col],
            out_shape=[jax.ShapeDtypeStruct((s, d), F32)] + h_shapes,
            compiler_params=_params("parallel"))(x, branch, gate, gain, scale, shift)
    h, ht = pl.pallas_call(
        body, name=name, grid=(s // tm,), in_specs=[row, vec, vec, vec], out_specs=[row, col],
        out_shape=h_shapes, compiler_params=_params("parallel"))(x, gain, scale, shift)
    return x, h, ht


def _norm_mod_bwd(x, dh, dres, gain, scale, branch, gate, *, name):
    s, d = x.shape
    has_branch = branch is not None
    tm = ROW_TILE if has_branch else 2 * ROW_TILE

    def body(*refs):
        if has_branch:
            x_ref, dh_ref, dres_ref, gain_ref, sc_ref, br_ref, gate_ref, dx_ref, dbr_ref, sums_ref = refs
        else:
            x_ref, dh_ref, dres_ref, gain_ref, sc_ref, dx_ref, sums_ref = refs
        i = pl.program_id(0)

        @pl.when(i == 0)
        def _():
            sums_ref[...] = jnp.zeros_like(sums_ref)

        xv, dhv = x_ref[...], dh_ref[...]
        r = lax.rsqrt(jnp.mean(xv * xv, axis=-1, keepdims=True) + EPS)
        xn = xv * r
        dxn = dhv * (gain_ref[...] * (1.0 + sc_ref[...]))
        dx = dres_ref[...] + r * (dxn - xn * jnp.mean(dxn * xn, axis=-1, keepdims=True))
        dx_ref[...] = dx
        sums_ref[0] += _fold8(dhv * xn)
        sums_ref[1] += _fold8(dhv)
        if has_branch:
            dbr_ref[...] = (gate_ref[...] * dx).astype(BF16)
            sums_ref[2] += _fold8(dx * br_ref[...])

        @pl.when(i == s // tm - 1)
        def _():
            _spread_total(sums_ref)

    row = pl.BlockSpec((tm, d), lambda i: (i, 0))
    vec = pl.BlockSpec((1, d), lambda i: (0, 0))
    sums = pl.BlockSpec((3, 8, d), lambda i: (0, 0, 0))
    sums_shape = jax.ShapeDtypeStruct((3, 8, d), F32)
    if has_branch:
        return pl.pallas_call(
            body, name=name, grid=(s // tm,), in_specs=[row, row, row, vec, vec, row, vec], out_specs=[row, row, sums],
            out_shape=[jax.ShapeDtypeStruct((s, d), F32), jax.ShapeDtypeStruct((s, d), BF16), sums_shape],
            compiler_params=_params("arbitrary"))(x, dh, dres, gain, scale, branch, gate)
    dx, sm = pl.pallas_call(
        body, name=name, grid=(s // tm,), in_specs=[row, row, row, vec, vec], out_specs=[row, sums],
        out_shape=[jax.ShapeDtypeStruct((s, d), F32), sums_shape],
        compiler_params=_params("arbitrary"))(x, dh, dres, gain, scale)
    return dx, None, sm


GLA_ROWS = 256


def _gla_block_setup(lr_ref, wg_ref, bg_ref):
    t, c = GLA_ROWS, GLA_CHUNK
    ri = lax.broadcasted_iota(jnp.int32, (t, t), 0)
    ci = lax.broadcasted_iota(jnp.int32, (t, t), 1)
    same = (ri // c) == (ci // c)
    causal, upper = same & (ci <= ri), same & (ci >= ri)
    z = _nn(lr_ref[...].astype(BF16), wg_ref[...]) + bg_ref[...]
    g = (jnp.minimum(z, 0.0) - jnp.log(1.0 + jnp.exp(-jnp.abs(z)))) * (1.0 / GLA_GATE_TAU)
    hi, mid, lo = _split3(g)
    total = lambda ones: (_nn(ones, lo) + _nn(ones, mid)) + _nn(ones, hi)
    return z, total(causal.astype(BF16)), total(same.astype(BF16)), causal, upper


def _chunks(t):
    return [t[i * GLA_CHUNK:(i + 1) * GLA_CHUNK] for i in range(GLA_ROWS // GLA_CHUNK)]


def _gla_fwd(proj, wg, bg, gn, *, name):
    s = proj.shape[0]
    tb, c = GLA_ROWS, GLA_CHUNK
    cb = tb // c

    def body(q_ref, k_ref, v_ref, r_ref, lr_ref, wg_ref, bg_ref, gn_ref, o_ref, y_ref, st_ref, state):
        i = pl.program_id(0)

        @pl.when(i == 0)
        def _():
            state[...] = jnp.zeros_like(state)

        low = lax.broadcasted_iota(jnp.int32, (tb, HEAD_LANES), 1) < 64
        masks = (low, jnp.logical_not(low))
        _, b, b_end, causal, _ = _gla_block_setup(lr_ref, wg_ref, bg_ref)
        pairs = []
        for p in range(2):
            cols = pl.ds(p * HEAD_LANES, HEAD_LANES)
            bp, bep = (t[:, p * HEAD_LANES:(p + 1) * HEAD_LANES] for t in (b, b_end))
            k = k_ref[:, cols]
            q_in = q_ref[:, cols] * 0.125 * jnp.exp(bp)
            k_out = (k * jnp.exp(-bp)).astype(BF16)
            k_end = k * jnp.exp(bep - bp)
            qms = [jnp.where(m, q_in, 0.0).astype(BF16) for m in masks]
            kes = [jnp.where(m, k_end, 0.0).astype(BF16) for m in masks]
            vs = [v_ref[:, pl.ds((2 * p + e) * HEAD_LANES, HEAD_LANES)].astype(BF16) for e in range(2)]
            grow = [_tn(v0, k0) + _tn(v1, k1) for v0, k0, v1, k1 in zip(_chunks(vs[0]), _chunks(kes[0]), _chunks(vs[1]), _chunks(kes[1]))]
            pairs.append((bep, k_out, qms, vs, grow))
        entering = [[], []]
        for p, (bep, _, _, _, grow) in enumerate(pairs):
            st = state[p]
            for ch in range(cb):
                entering[p].append(st)
                st_ref[ch, p] = st
                st = st * jnp.exp(bep[ch * c:ch * c + 1, :]) + grow[ch]
            state[p] = st
        for p, (_, k_out, qms, vs, _) in enumerate(pairs):
            for e in range(2):
                hc = pl.ds((2 * p + e) * HEAD_LANES, HEAD_LANES)
                a = jnp.where(causal, _nt(qms[e], k_out), 0.0).astype(BF16)
                carried = jnp.concatenate([_nt(qc, sc.astype(BF16)) for qc, sc in zip(_chunks(qms[e]), entering[p])], axis=0)
                o = _nn(a, vs[e]) + carried
                o_ref[:, hc] = o
                rr = r_ref[:, hc]
                on = o * lax.rsqrt(jnp.mean(o * o, axis=-1, keepdims=True) + EPS)
                y_ref[:, hc] = (on * gn_ref[...] * (rr * _sigmoid(rr))).astype(BF16)

    def col(width, at):
        return pl.BlockSpec((tb, width), lambda i: (i, at // width))

    full = lambda shape: pl.BlockSpec(shape, lambda i: tuple(0 for _ in shape))
    return pl.pallas_call(
        body, name=name, grid=(s // tb,),
        in_specs=[col(256, C_GQ), col(256, C_GK), col(512, C_GV), col(512, C_GR), col(128, C_LR),
                  full((HEAD_LANES, 256)), full((1, 256)), full((1, HEAD_LANES))],
        out_specs=[pl.BlockSpec((tb, 512), lambda i: (i, 0)), pl.BlockSpec((tb, 512), lambda i: (i, 0)),
                   pl.BlockSpec((cb, 2, HEAD_LANES, HEAD_LANES), lambda i: (i, 0, 0, 0))],
        out_shape=[jax.ShapeDtypeStruct((s, 512), F32), jax.ShapeDtypeStruct((s, 512), BF16),
                   jax.ShapeDtypeStruct((s // c, 2, HEAD_LANES, HEAD_LANES), F32)],
        scratch_shapes=[pltpu.VMEM((2, HEAD_LANES, HEAD_LANES), F32)],
        compiler_params=_params("arbitrary"))(proj, proj, proj, proj, proj, wg, bg, gn)


def _gla_bwd(proj, wg, bg, gn, o_raw, states, dmixed, *, name):
    s = proj.shape[0]
    tb, c = GLA_ROWS, GLA_CHUNK
    cb = tb // c
    nblk, nch = s // tb, s // c

    def body(q_ref, k_ref, v_ref, r_ref, lr_ref, wg_ref, bg_ref, gn_ref, o_ref, st_ref, stn_ref, dy_ref,
             dq_ref, dk_ref, dv_ref, dr_ref, dlr_ref, gwg_ref, sums_ref, dstate):
        i = pl.program_id(0)

        @pl.when(i == 0)
        def _():
            dstate[...] = jnp.zeros_like(dstate)
            gwg_ref[...] = jnp.zeros_like(gwg_ref)
            sums_ref[...] = jnp.zeros_like(sums_ref)

        low = lax.broadcasted_iota(jnp.int32, (tb, HEAD_LANES), 1) < 64
        masks = (low, jnp.logical_not(low))
        z, b, b_end, causal, upper = _gla_block_setup(lr_ref, wg_ref, bg_ref)
        lr_b = lr_ref[...].astype(BF16)
        dlr = jnp.zeros((tb, HEAD_LANES), F32)
        per_chunk = lambda rows, mats, fn: jnp.concatenate([fn(r, m.astype(BF16)) for r, m in zip(_chunks(rows), mats)], axis=0)
        pairs = []
        for p in range(2):
            cols = pl.ds(p * HEAD_LANES, HEAD_LANES)
            sl = slice(p * HEAD_LANES, (p + 1) * HEAD_LANES)
            bp, bep = b[:, sl], b_end[:, sl]
            e_in, e_out, e_end = jnp.exp(bp), jnp.exp(-bp), jnp.exp(bep - bp)
            q = q_ref[:, cols] * 0.125
            k = k_ref[:, cols]
            q_in, k_out, k_end = q * e_in, k * e_out, k * e_end
            qms = [jnp.where(m, q_in, 0.0).astype(BF16) for m in masks]
            kms_out = [jnp.where(m, k_out, 0.0).astype(BF16) for m in masks]
            kms_end = [jnp.where(m, k_end, 0.0).astype(BF16) for m in masks]
            vs, dos = [], []
            for e in range(2):
                hc = pl.ds((2 * p + e) * HEAD_LANES, HEAD_LANES)
                o, rr, dy = o_ref[:, hc], r_ref[:, hc], dy_ref[:, hc]
                sg = _sigmoid(rr)
                rs = lax.rsqrt(jnp.mean(o * o, axis=-1, keepdims=True) + EPS)
                on = o * rs
                t = dy * (rr * sg)
                sums_ref[1, :, hc] += _fold8(t * on)
                dn = t * gn_ref[...]
                dos.append((rs * (dn - on * jnp.mean(dn * on, axis=-1, keepdims=True))).astype(BF16))
                dr_ref[:, hc] = (dy * on * gn_ref[...] * (sg * (1.0 + rr * (1.0 - sg)))).astype(BF16)
                vs.append(v_ref[:, hc].astype(BF16))
            grow = [_tn(d0, q0) + _tn(d1, q1) for d0, q0, d1, q1 in zip(_chunks(dos[0]), _chunks(qms[0]), _chunks(dos[1]), _chunks(qms[1]))]
            pairs.append((bep, e_in, e_out, e_end, q, k, qms, kms_out, kms_end, vs, dos, grow))
        chains = []
        for p in range(2):
            bep, grow = pairs[p][0], pairs[p][-1]
            entering = [st_ref[ch, p] for ch in range(cb)]
            dst, leaving_grad = dstate[p], [None] * cb
            for ch in reversed(range(cb)):
                leaving_grad[ch] = dst
                dst = dst * jnp.exp(bep[ch * c:ch * c + 1, :]) + grow[ch]
            dstate[p] = dst
            chains.append((entering, leaving_grad))
        for p in range(2):
            cols = pl.ds(p * HEAD_LANES, HEAD_LANES)
            sl = slice(p * HEAD_LANES, (p + 1) * HEAD_LANES)
            _, e_in, e_out, e_end, q, k, qms, kms_out, kms_end, vs, dos, _ = pairs[p]
            entering, leaving_grad = chains[p]
            leaving = entering[1:] + [stn_ref[0, p]]
            felt = jnp.concatenate([jnp.broadcast_to(jnp.sum(dg_st * st, axis=0, keepdims=True), (c, HEAD_LANES))
                                    for dg_st, st in zip(leaving_grad, leaving)], axis=0)
            dq_in = jnp.zeros((tb, HEAD_LANES), F32)
            dk_out = jnp.zeros((tb, HEAD_LANES), F32)
            dk_end = jnp.zeros((tb, HEAD_LANES), F32)
            for e in range(2):
                hc = pl.ds((2 * p + e) * HEAD_LANES, HEAD_LANES)
                a = jnp.where(causal, _nt(qms[e], kms_out[e]), 0.0).astype(BF16)
                da = jnp.where(causal, _nt(dos[e], vs[e]), 0.0).astype(BF16)
                dv_ref[:, hc] = (_tn(a, dos[e]) + per_chunk(kms_end[e], leaving_grad, _nt)).astype(BF16)
                dq_in = dq_in + jnp.where(masks[e], per_chunk(dos[e], entering, _nn) + _nn(da, kms_out[e]), 0.0)
                dk_out = dk_out + _tn(da, qms[e])
                dk_end = dk_end + jnp.where(masks[e], per_chunk(vs[e], leaving_grad, _nn), 0.0)
            dq = dq_in * e_in
            dk = dk_out * e_out + dk_end * e_end
            dq_ref[:, cols] = (dq * 0.125).astype(BF16)
            dk_ref[:, cols] = dk.astype(BF16)
            dg = _sum_left(upper.astype(BF16), q * dq - k * dk) + felt
            dz = dg * (1.0 / GLA_GATE_TAU) * _sigmoid(-z[:, sl])
            dz_b = dz.astype(BF16)
            sums_ref[0, :, cols] += _fold8(dz)
            dlr = dlr + _nt(dz_b, wg_ref[:, cols])
            gwg_ref[:, cols] += _tn(lr_b, dz_b)
        dlr_ref[...] = dlr.astype(BF16)

        @pl.when(i == nblk - 1)
        def _():
            _spread_total(sums_ref)

    rev = lambda i: nblk - 1 - i

    def col(width, at):
        return pl.BlockSpec((tb, width), lambda i: (rev(i), at // width))

    full = lambda shape: pl.BlockSpec(shape, lambda i: tuple(0 for _ in shape))
    out_col = lambda width: pl.BlockSpec((tb, width), lambda i: (rev(i), 0))
    return pl.pallas_call(
        body, name=name, grid=(nblk,),
        in_specs=[col(256, C_GQ), col(256, C_GK), col(512, C_GV), col(512, C_GR), col(128, C_LR),
                  full((HEAD_LANES, 256)), full((1, 256)), full((1, HEAD_LANES)),
                  pl.BlockSpec((tb, 512), lambda i: (rev(i), 0)),
                  pl.BlockSpec((cb, 2, HEAD_LANES, HEAD_LANES), lambda i: (rev(i), 0, 0, 0)),
                  pl.BlockSpec((1, 2, HEAD_LANES, HEAD_LANES), lambda i: (jnp.minimum((rev(i) + 1) * cb, nch - 1), 0, 0, 0)),
                  pl.BlockSpec((tb, 512), lambda i: (rev(i), 0))],
        out_specs=[out_col(256), out_col(256), out_col(512), out_col(512), out_col(128),
                   full((HEAD_LANES, 256)), full((2, 8, 512))],
        out_shape=[jax.ShapeDtypeStruct((s, 256), BF16), jax.ShapeDtypeStruct((s, 256), BF16),
                   jax.ShapeDtypeStruct((s, 512), BF16), jax.ShapeDtypeStruct((s, 512), BF16),
                   jax.ShapeDtypeStruct((s, 128), BF16), jax.ShapeDtypeStruct((HEAD_LANES, 256), F32),
                   jax.ShapeDtypeStruct((2, 8, 512), F32)],
        scratch_shapes=[pltpu.VMEM((2, HEAD_LANES, HEAD_LANES), F32)],
        compiler_params=_params("arbitrary"))(proj, proj, proj, proj, proj, wg, bg, gn, o_raw, states, states, dmixed)


def _head_sums(v):
    ri = lax.broadcasted_iota(jnp.int32, (HEAD_LANES, HEAD_LANES), 0) // 64
    ci = lax.broadcasted_iota(jnp.int32, (HEAD_LANES, HEAD_LANES), 1) // 64
    ones = (ri == ci).astype(BF16)
    return jnp.concatenate([_sum_right(v[:, p * HEAD_LANES:(p + 1) * HEAD_LANES], ones) for p in range(4)], axis=1)


def _attn_prep(proj, qg, kg, *, name):
    s = proj.shape[0]
    tm = 2 * ROW_TILE

    def body(q_ref, k_ref, qg_ref, kg_ref, qa_ref, ka_ref):
        q, k = q_ref[...], k_ref[...]
        qr = lax.rsqrt(_head_sums(q * q) * (1.0 / 64) + EPS)
        kr = lax.rsqrt(_head_sums(k * k) * (1.0 / 64) + EPS)
        qa_ref[...] = q * qr * qg_ref[...] * 0.125
        ka_ref[...] = k * kr * kg_ref[...]

    col = lambda at: pl.BlockSpec((tm, 512), lambda i: (i, at // 512))
    vec = pl.BlockSpec((1, 512), lambda i: (0, 0))
    out = pl.BlockSpec((tm, 512), lambda i: (i, 0))
    return pl.pallas_call(
        body, name=name, grid=(s // tm,), in_specs=[col(C_AQ), col(C_AK), vec, vec], out_specs=[out] * 2,
        out_shape=[jax.ShapeDtypeStruct((s, 512), F32)] * 2, compiler_params=_params("parallel"))(proj, proj, qg, kg)


FAR = 1e30
LOG2E, LN2 = 1.4426950408889634, 0.6931471805599453


def _attn_distance(first):
    blk = ATTN_BLOCK
    iq = lax.broadcasted_iota(jnp.int32, (2 * blk, 2 * blk), 0) & (blk - 1)
    ik = lax.broadcasted_iota(jnp.int32, (2 * blk, 2 * blk), 1)
    rel = iq + blk - ik
    valid = (rel >= 0) & (rel <= blk) & (jnp.logical_not(first) | (ik >= blk))
    return jnp.where(valid, rel.astype(F32), FAR)


def _stack_heads(t2):
    low = lax.broadcasted_iota(jnp.int32, t2.shape, 1) < 64
    return jnp.concatenate([jnp.where(low, t2, 0.0), jnp.where(low, 0.0, t2)], axis=0).astype(BF16)


def _unstack_heads(t):
    blk = ATTN_BLOCK
    low = lax.broadcasted_iota(jnp.int32, (blk, HEAD_LANES), 1) < 64
    return jnp.where(low, t[0:blk], t[blk:2 * blk])


def _attn_scores(qs, kcat, slopes, dil, dist):
    top = lax.broadcasted_iota(jnp.int32, (2 * ATTN_BLOCK, 1), 0) < ATTN_BLOCK
    return _nt(qs, kcat) - jnp.where(top, slopes[0] * (dil * LOG2E), slopes[1] * (dil * LOG2E)) * dist


def _pair_slopes(p):
    if isinstance(p, int):
        return ALIBI_SLOPES[2 * p], ALIBI_SLOPES[2 * p + 1]
    pick = lambda e: jnp.where(p == 0, ALIBI_SLOPES[e], jnp.where(p == 1, ALIBI_SLOPES[2 + e],
                               jnp.where(p == 2, ALIBI_SLOPES[4 + e], ALIBI_SLOPES[6 + e])))
    return pick(0), pick(1)


ATTN_GROUP = 4


def _each(fn, *lists):
    return [fn(*args) for args in zip(*lists)]


def _attn_group_fwd(q2s, kcats, vcats, slopes, dil, dist):
    qs = _each(lambda q2: _stack_heads(q2 * LOG2E), q2s)
    sc = _each(lambda q, k, sl: _attn_scores(q, k, sl, dil, dist), qs, kcats, slopes)
    m = _each(lambda s: jnp.max(s, axis=-1, keepdims=True), sc)
    pr = _each(lambda s, mx: jnp.exp2(s - mx), sc, m)
    den = _each(lambda p: jnp.sum(p, axis=-1, keepdims=True), pr)
    o = _each(lambda p, v, d: _nn(p.astype(BF16), v) / d, pr, vcats, den)
    lse = _each(lambda mx, d, t: jnp.broadcast_to(mx + jnp.log2(d), t.shape), m, den, o)
    return _each(lambda t, l: (_unstack_heads(t), _unstack_heads(l)), o, lse)


def _attn_group_bwd(q2s, kcats, vcats, do2s, y2s, lse2s, slopes, dil, dist):
    lane = lax.broadcasted_iota(jnp.int32, (ATTN_BLOCK, HEAD_LANES), 1)
    low = lane < 64
    per_head = lambda t, pick: jnp.concatenate([jnp.sum(jnp.where(pick(0), t, 0.0), axis=-1, keepdims=True),
                                                jnp.sum(jnp.where(pick(1), t, 0.0), axis=-1, keepdims=True)], axis=0)
    lse = _each(lambda l: per_head(l, lambda e: lane == 64 * e), lse2s)
    delta = _each(lambda d, y: per_head(d * y, lambda e: low if e == 0 else jnp.logical_not(low)), do2s, y2s)
    qs = _each(lambda q2: _stack_heads(q2 * LOG2E), q2s)
    dos = _each(_stack_heads, do2s)
    sc = _each(lambda q, k, sl: _attn_scores(q, k, sl, dil, dist), qs, kcats, slopes)
    pr = _each(lambda s, l: jnp.exp2(s - l), sc, lse)
    dp = _each(_nt, dos, vcats)
    ds = _each(lambda p, d, dl: (p * (d - dl)).astype(BF16), pr, dp, delta)
    dq = _each(lambda d, k: _unstack_heads(_nn(d, k)), ds, kcats)
    dk = _each(lambda d, q: _tn(d, q) * LN2, ds, qs)
    dv = _each(lambda p, d: _tn(p.astype(BF16), d), pr, dos)
    return list(zip(dq, dk, dv))


def _attn_specs(dil):
    rows = ATTN_BLOCK * dil
    if dil == 1:
        cur = lambda at: pl.BlockSpec((rows, 512), lambda n: (n, at // 512))
        prev = lambda at: pl.BlockSpec((rows, 512), lambda n: (jnp.maximum(n - 1, 0), at // 512))
    else:
        cur = lambda at: pl.BlockSpec((rows, HEAD_LANES), lambda n, p: (n, at // HEAD_LANES + p))
        prev = lambda at: pl.BlockSpec((rows, HEAD_LANES), lambda n, p: (jnp.maximum(n - 1, 0), at // HEAD_LANES + p))
    return cur, prev


def _attn_loop(dil, one_group, p):
    if dil == 1:
        one_group([(slice(None), pl.ds(p * HEAD_LANES, HEAD_LANES), p) for p in range(ATTN_GROUP)])
    else:
        group = min(dil, ATTN_GROUP)

        def step(g, carry):
            one_group([(pl.ds(g * group + j, ATTN_BLOCK, stride=dil), slice(None), p) for j in range(group)])
            return carry

        if dil == group:
            step(0, 0)
        else:
            lax.fori_loop(0, dil // group, step, 0)


def _dil_attn_fwd(qa, ka, proj, dil, *, name):
    s = qa.shape[0]

    def body(q_ref, kp_ref, kc_ref, vp_ref, vc_ref, o_ref, lse_ref):
        dist = _attn_distance(pl.program_id(0) == 0)
        pair = None if dil == 1 else pl.program_id(1)

        def one_group(items):
            both = lambda a, b: [jnp.concatenate([a[rows, cols], b[rows, cols]], axis=0).astype(BF16) for rows, cols, _ in items]
            outs = _attn_group_fwd([q_ref[rows, cols] for rows, cols, _ in items], both(kp_ref, kc_ref), both(vp_ref, vc_ref),
                                   [_pair_slopes(p) for _, _, p in items], dil, dist)
            for (rows, cols, _), (o2, lse2) in zip(items, outs):
                o_ref[rows, cols] = o2
                lse_ref[rows, cols] = lse2

        _attn_loop(dil, one_group, pair)

    cur, prev = _attn_specs(dil)
    grid = (s // ATTN_BLOCK,) if dil == 1 else (s // (ATTN_BLOCK * dil), 4)
    return pl.pallas_call(
        body, name=name, grid=grid, in_specs=[cur(0), prev(0), cur(0), prev(C_AV), cur(C_AV)], out_specs=[cur(0), cur(0)],
        out_shape=[jax.ShapeDtypeStruct((s, 512), F32)] * 2,
        compiler_params=_params(*["parallel"] * len(grid)))(qa, ka, ka, proj, proj)


def _dense_attn_fwd_merge(qa, ka, proj, others, y_gla, *, name):
    s = qa.shape[0]
    blk = ATTN_BLOCK

    def body(q_ref, kp_ref, kc_ref, vp_ref, vc_ref, oa_ref, la_ref, ob_ref, lb_ref, yg_ref, mixed_ref, y_ref, lse_ref):
        dist = _attn_distance(pl.program_id(0) == 0)
        mixed_ref[:, 0:512] = yg_ref[...]

        def one_group(items):
            both = lambda a, b: [jnp.concatenate([a[rows, cols], b[rows, cols]], axis=0).astype(BF16) for rows, cols, _ in items]
            outs = _attn_group_fwd([q_ref[rows, cols] for rows, cols, _ in items], both(kp_ref, kc_ref), both(vp_ref, vc_ref),
                                   [_pair_slopes(p) for _, _, p in items], 1, dist)
            for (_, cols, p), (o2, l2) in zip(items, outs):
                la, lb = la_ref[:, cols], lb_ref[:, cols]
                m = jnp.maximum(jnp.maximum(l2, la), lb)
                w0, wa, wb = jnp.exp2(l2 - m), jnp.exp2(la - m), jnp.exp2(lb - m)
                zs = w0 + wa + wb
                y = (w0 * o2 + wa * oa_ref[:, cols] + wb * ob_ref[:, cols]) / zs
                y_ref[:, cols] = y
                lse_ref[:, cols] = m + jnp.log2(zs)
                mixed_ref[:, pl.ds(512 + p * HEAD_LANES, HEAD_LANES)] = y.astype(BF16)

        _attn_loop(1, one_group, None)

    cur, prev = _attn_specs(1)
    here = pl.BlockSpec((blk, 512), lambda n: (n, 0))
    (oa, la), (ob, lb) = others
    return pl.pallas_call(
        body, name=name, grid=(s // blk,),
        in_specs=[cur(0), prev(0), cur(0), prev(C_AV), cur(C_AV)] + [here] * 5,
        out_specs=[pl.BlockSpec((blk, 1024), lambda n: (n, 0)), here, here],
        out_shape=[jax.ShapeDtypeStruct((s, 1024), BF16), jax.ShapeDtypeStruct((s, 512), F32),
                   jax.ShapeDtypeStruct((s, 512), F32)],
        compiler_params=_params("parallel"))(qa, ka, ka, proj, proj, oa, la, ob, lb, y_gla)


def _dil_attn_bwd(qa, ka, proj, y_att, lse, dmixed, dil, *, name):
    s = qa.shape[0]
    blk, rows_per_step = ATTN_BLOCK, ATTN_BLOCK * dil
    nb = s // rows_per_step
    step_axis = 0 if dil == 1 else 1

    def body(q_ref, kp_ref, kc_ref, vp_ref, vc_ref, y_ref, lse_ref, do_ref, dq_ref, dk_ref, dv_ref, dk_own, dv_own):
        n = pl.program_id(step_axis)
        pair = None if dil == 1 else pl.program_id(0)
        dist = _attn_distance(n == 0)

        @pl.when(n == 0)
        def _():
            dk_own[...] = jnp.zeros_like(dk_own)
            dv_own[...] = jnp.zeros_like(dv_own)

        def one_group(items):
            both = lambda a, b: [jnp.concatenate([a[rows, cols], b[rows, cols]], axis=0).astype(BF16) for rows, cols, _ in items]
            at = lambda ref: [ref[rows, cols] for rows, cols, _ in items]
            outs = _attn_group_bwd(at(q_ref), both(kp_ref, kc_ref), both(vp_ref, vc_ref), at(do_ref), at(y_ref), at(lse_ref),
                                   [_pair_slopes(p) for _, _, p in items], dil, dist)
            for (rows, cols, _), (dq, dk, dv) in zip(items, outs):
                dq_ref[rows, cols] = dq
                dk_ref[rows, cols] = dk_own[rows, cols] + dk[0:blk]
                dv_ref[rows, cols] = dv_own[rows, cols] + dv[0:blk]
                dk_own[rows, cols] = dk[blk:2 * blk]
                dv_own[rows, cols] = dv[blk:2 * blk]

        _attn_loop(dil, one_group, pair)

    width = 512 if dil == 1 else HEAD_LANES

    def spec(at, row_of):
        if dil == 1:
            return pl.BlockSpec((rows_per_step, width), lambda n: (row_of(n), at // width))
        return pl.BlockSpec((rows_per_step, width), lambda p, n: (row_of(n), at // width + p))

    cur = lambda at: spec(at, lambda n: n)
    prev = lambda at: spec(at, lambda n: jnp.maximum(n - 1, 0))
    own = spec(0, lambda n: 0)
    grid = (nb,) if dil == 1 else (4, nb)
    sems = ("arbitrary",) if dil == 1 else ("parallel", "arbitrary")
    dq, dk, dv, dk_last, dv_last = pl.pallas_call(
        body, name=name, grid=grid,
        in_specs=[cur(0), prev(0), cur(0), prev(C_AV), cur(C_AV), cur(0), cur(0), cur(512)],
        out_specs=[cur(0), prev(0), prev(0), own, own],
        out_shape=[jax.ShapeDtypeStruct((s, 512), F32)] * 3 + [jax.ShapeDtypeStruct((rows_per_step, 512), F32)] * 2,
        compiler_params=_params(*sems),
    )(qa, ka, ka, proj, proj, y_att, lse, dmixed)
    return dq, dk.at[s - rows_per_step:].set(dk_last), dv.at[s - rows_per_step:].set(dv_last)


def _attn_post(parts, proj, qg, kg, *, name):
    s = proj.shape[0]
    tm = ROW_TILE
    nblk = s // tm

    def body(*refs):
        ins, (q_ref, k_ref, qg_ref, kg_ref, dq_out, dk_out, dv_out, sums_ref) = refs[:9], refs[9:]
        i = pl.program_id(0)

        @pl.when(i == 0)
        def _():
            sums_ref[...] = jnp.zeros_like(sums_ref)

        dq = (ins[0][...] + ins[3][...]) + ins[6][...]
        dk = (ins[1][...] + ins[4][...]) + ins[7][...]
        dv = (ins[2][...] + ins[5][...]) + ins[8][...]
        dv_out[...] = dv.astype(BF16)
        for row, (x_ref, g_ref, dy, out, post) in enumerate(((q_ref, qg_ref, dq, dq_out, 0.125), (k_ref, kg_ref, dk, dk_out, 1.0))):
            x = x_ref[...]
            rs = lax.rsqrt(_head_sums(x * x) * (1.0 / 64) + EPS)
            xn = x * rs
            dy = dy * post
            sums_ref[row] += _fold8(dy * xn)
            dn = dy * g_ref[...]
            out[...] = (rs * (dn - xn * (_head_sums(dn * xn) * (1.0 / 64)))).astype(BF16)

        @pl.when(i == nblk - 1)
        def _():
            _spread_total(sums_ref)

    here = pl.BlockSpec((tm, 512), lambda i: (i, 0))
    col = lambda at: pl.BlockSpec((tm, 512), lambda i: (i, at // 512))
    vec = pl.BlockSpec((1, 512), lambda i: (0, 0))
    return pl.pallas_call(
        body, name=name, grid=(nblk,), in_specs=[here] * 9 + [col(C_AQ), col(C_AK), vec, vec],
        out_specs=[here, here, here, pl.BlockSpec((2, 8, 512), lambda i: (0, 0, 0))],
        out_shape=[jax.ShapeDtypeStruct((s, 512), BF16)] * 3 + [jax.ShapeDtypeStruct((2, 8, 512), F32)],
        compiler_params=_params("arbitrary"))(*[t for part in parts for t in part], proj, proj, qg, kg)


FFN_TM, FFN_TN = 512, 1408
HALO = 16


def _conv3(u_ref, halo_ref, w_ref, b_ref, first):
    u = u_ref[...].astype(F32)
    ext = jnp.concatenate([jnp.where(first, 0.0, halo_ref[...].astype(F32)), u], axis=0)
    u1 = pltpu.roll(ext, 1, 0)[HALO:]
    u2 = pltpu.roll(ext, 2, 0)[HALO:]
    return b_ref[...] + w_ref[0:1, :] * u2 + w_ref[1:2, :] * u1 + w_ref[2:3, :] * u


def _ffn_specs(tm, tn):
    nj = D_FF // tn
    blk = lambda half: pl.BlockSpec((tm, tn), lambda j, i: (i, j + half * nj))
    halo = lambda half: pl.BlockSpec((HALO, tn), lambda j, i: (jnp.maximum(i * (tm // HALO) - 1, 0), j + half * nj))
    wspec = lambda half: pl.BlockSpec((3, tn), lambda j, i: (0, j + half * nj))
    bspec = lambda half: pl.BlockSpec((1, tn), lambda j, i: (0, j + half * nj))
    return [blk(0), halo(0), blk(1), halo(1), wspec(0), wspec(1), bspec(0), bspec(1)]


def _conv_swiglu_fwd(u, conv_w, conv_b, *, name):
    s = u.shape[0]
    tm, tn = FFN_TM, FFN_TN

    def body(ug_ref, hg_ref, uv_ref, hv_ref, wg_ref, wv_ref, bg_ref, bv_ref, act_ref, uc_ref):
        first = pl.program_id(1) == 0
        cg = _conv3(ug_ref, hg_ref, wg_ref, bg_ref, first)
        cv = _conv3(uv_ref, hv_ref, wv_ref, bv_ref, first)
        act_ref[...] = (cg * _sigmoid(cg) * cv).astype(BF16)
        uc_ref[0] = cg.astype(BF16)
        uc_ref[1] = cv.astype(BF16)

    return pl.pallas_call(
        body, name=name, grid=(D_FF // tn, s // tm), in_specs=_ffn_specs(tm, tn),
        out_specs=[pl.BlockSpec((tm, tn), lambda j, i: (i, j)), pl.BlockSpec((2, tm, tn), lambda j, i: (0, i, j))],
        out_shape=[jax.ShapeDtypeStruct((s, D_FF), BF16), jax.ShapeDtypeStruct((2, s, D_FF), BF16)],
        compiler_params=_params("parallel", "parallel"))(u, u, u, u, conv_w, conv_w, conv_b, conv_b)


def _swiglu_bwd(uc, dact, *, name):
    _, s, _ = uc.shape
    tm, tn = FFN_TM, FFN_TN

    def body(uc_ref, da_ref, duc_ref, sums_ref):
        i = pl.program_id(1)

        @pl.when(i == 0)
        def _():
            sums_ref[...] = jnp.zeros_like(sums_ref)

        cg, cv, da = uc_ref[0].astype(F32), uc_ref[1].astype(F32), da_ref[...].astype(F32)
        sg = _sigmoid(cg)
        dg = da * cv * (sg * (1.0 + cg * (1.0 - sg)))
        dv = da * (cg * sg)
        duc_ref[0] = dg.astype(BF16)
        duc_ref[1] = dv.astype(BF16)
        sums_ref[0] += _fold8(dg)
        sums_ref[1] += _fold8(dv)

        @pl.when(i == s // tm - 1)
        def _():
            _spread_total(sums_ref)

    pair = pl.BlockSpec((2, tm, tn), lambda j, i: (0, i, j))
    return pl.pallas_call(
        body, name=name, grid=(D_FF // tn, s // tm), in_specs=[pair, pl.BlockSpec((tm, tn), lambda j, i: (i, j))],
        out_specs=[pair, pl.BlockSpec((2, 8, tn), lambda j, i: (0, 0, j))],
        out_shape=[jax.ShapeDtypeStruct((2, s, D_FF), BF16), jax.ShapeDtypeStruct((2, 8, D_FF), F32)],
        compiler_params=_params("parallel", "arbitrary"))(uc, dact)


def _conv_bwd(duc, u, conv_w, *, name):
    _, s, _ = duc.shape
    tm, tn = FFN_TM, FFN_TN
    nj, ni = D_FF // tn, s // tm

    def body(d_ref, halo_ref, u_ref, w_ref, du_ref, sums_ref):
        i = pl.program_id(2)

        @pl.when(i == 0)
        def _():
            sums_ref[...] = jnp.zeros_like(sums_ref)

        d = d_ref[0].astype(F32)
        ext = jnp.concatenate([d, jnp.where(i == ni - 1, 0.0, halo_ref[0].astype(F32))], axis=0)
        n = tm + HALO
        d1 = pltpu.roll(ext, n - 1, 0)[:tm]
        d2 = pltpu.roll(ext, n - 2, 0)[:tm]
        du_ref[...] = (w_ref[2:3, :] * d + w_ref[1:2, :] * d1 + w_ref[0:1, :] * d2).astype(BF16)
        uv = u_ref[...].astype(F32)
        for t, shifted in enumerate((d2, d1, d)):
            sums_ref[0, t] += _fold8(shifted * uv)

        @pl.when(i == ni - 1)
        def _():
            _spread_total(sums_ref)

    return pl.pallas_call(
        body, name=name, grid=(2, nj, ni),
        in_specs=[pl.BlockSpec((1, tm, tn), lambda g, j, i: (g, i, j)),
                  pl.BlockSpec((1, HALO, tn), lambda g, j, i: (g, jnp.minimum((i + 1) * (tm // HALO), s // HALO - 1), j)),
                  pl.BlockSpec((tm, tn), lambda g, j, i: (i, g * nj + j)),
                  pl.BlockSpec((3, tn), lambda g, j, i: (0, g * nj + j))],
        out_specs=[pl.BlockSpec((tm, tn), lambda g, j, i: (i, g * nj + j)),
                   pl.BlockSpec((1, 3, 8, tn), lambda g, j, i: (g, 0, 0, j))],
        out_shape=[jax.ShapeDtypeStruct((s, 2 * D_FF), BF16), jax.ShapeDtypeStruct((2, 3, 8, D_FF), F32)],
        compiler_params=_params("parallel", "parallel", "arbitrary"))(duc, duc, u, conv_w)


def _loss_head(x1, ffn, gate, target, *, name):
    s, d = x1.shape
    tm = ROW_TILE

    def body(x_ref, f_ref, g_ref, t_ref, dy_ref, df_ref, sums_ref):
        i = pl.program_id(0)

        @pl.when(i == 0)
        def _():
            sums_ref[...] = jnp.zeros_like(sums_ref)

        f = f_ref[...]
        err = x_ref[...] + g_ref[...] * f - t_ref[...]
        dy = err * (1.0 / d)
        dy_ref[...] = dy
        df_ref[...] = (g_ref[...] * dy).astype(BF16)
        sums_ref[0] += _fold8(dy * f)
        sums_ref[1] += _fold8(err * err)

        @pl.when(i == s // tm - 1)
        def _():
            _spread_total(sums_ref)

    row = pl.BlockSpec((tm, d), lambda i: (i, 0))
    return pl.pallas_call(
        body, name=name, grid=(s // tm,), in_specs=[row, row, pl.BlockSpec((1, d), lambda i: (0, 0)), row],
        out_specs=[row, row, pl.BlockSpec((2, 8, d), lambda i: (0, 0, 0))],
        out_shape=[jax.ShapeDtypeStruct((s, d), F32), jax.ShapeDtypeStruct((s, d), BF16), jax.ShapeDtypeStruct((2, 8, d), F32)],
        compiler_params=_params("arbitrary"))(x1, ffn, gate, target)


def _adamw(w, g, m, v, *, name):
    rows, cols = w.shape
    split = isinstance(g, tuple)
    if rows % 8 == 0 or rows <= ADAM_TILE:
        span = rows // 2 if split else rows
        tm = next((t for t in range(ADAM_TILE, 7, -8) if span % t == 0), span)
        shape, at, steps, per_half = (tm, cols), (lambda i: (i, 0)), rows // tm, span // tm
    else:
        shape, at, steps, per_half = (rows, ADAM_TILE), (lambda i: (0, i)), cols // ADAM_TILE, cols // ADAM_TILE // 2

    def update(gv, w_ref, m_ref, v_ref, d_ref, mo_ref, vo_ref):
        mn = ADAM_B1 * m_ref[...] + (1.0 - ADAM_B1) * gv
        vn = ADAM_B2 * v_ref[...] + (1.0 - ADAM_B2) * (gv * gv)
        m_hat = mn / (1.0 - ADAM_B1 ** ADAM_STEP)
        v_hat = vn / (1.0 - ADAM_B2 ** ADAM_STEP)
        d_ref[...] = -ADAM_LR * (m_hat / (jnp.sqrt(v_hat) + ADAM_EPS) + ADAM_WD * w_ref[...])
        mo_ref[...] = mn
        vo_ref[...] = vn

    out_shape = [jax.ShapeDtypeStruct((rows, cols), F32)] * (4 if split else 3)
    if not split:
        def body(w_ref, g_ref, m_ref, v_ref, d_ref, mo_ref, vo_ref):
            update(g_ref[...], w_ref, m_ref, v_ref, d_ref, mo_ref, vo_ref)

        blk = pl.BlockSpec(shape, at)
        return pl.pallas_call(body, name=name, grid=(steps,), in_specs=[blk] * 4, out_specs=[blk] * 3, out_shape=out_shape,
                              compiler_params=_params("parallel"))(w, g, m, v)

    mine, other, core = g

    def body(core_ref, w_ref, mine_ref, other_ref, m_ref, v_ref, d_ref, mo_ref, vo_ref, g_ref):
        gv = jnp.where(pl.program_id(0) // per_half == core_ref[0], mine_ref[...], other_ref[...])
        g_ref[...] = gv
        update(gv, w_ref, m_ref, v_ref, d_ref, mo_ref, vo_ref)

    blk = pl.BlockSpec(shape, lambda i, core_ref: at(i))
    half = pl.BlockSpec(shape, lambda i, core_ref: at(i % per_half))
    return pl.pallas_call(
        body, name=name, out_shape=out_shape, compiler_params=_params("parallel"),
        grid_spec=pltpu.PrefetchScalarGridSpec(num_scalar_prefetch=1, grid=(steps,), in_specs=[blk, half, half, blk, blk],
                                               out_specs=[blk] * 4))(core, w, mine, other, m, v)


def _colsum(t):
    return t[..., 0, :]


def _in_proj_layout(w_in):
    pad = jnp.zeros((w_in.shape[0], PROJ_W - C_LR - GLA_GATE_RANK), w_in.dtype)
    return jnp.concatenate([w_in[:, :1536], w_in[:, 1552:], w_in[:, 1536:1552], pad], axis=1)


def _in_proj_grad_layout(g):
    return jnp.concatenate([g[:, :1536], g[:, C_LR:C_LR + GLA_GATE_RANK], g[:, 1536:C_LR]], axis=1)


def _gate_layout(gla_w_gate):
    return jnp.pad(gla_w_gate, ((0, HEAD_LANES - GLA_GATE_RANK), (0, 0))).astype(BF16)


def _local_step(x, target, mod, wi, wo, ffn_weights, ffn_grads_ready, attn_grads_ready, conv_w, conv_b, wg, bg, gn, qg, kg, n1g, n2g):
    d = D_MODEL
    sh1, sc1, g1, sh2, sc2, g2 = [mod[:, i * d:(i + 1) * d] for i in range(6)]
    qg8, kg8 = jnp.tile(qg, (1, 8)), jnp.tile(kg, (1, 8))

    _, h1, h1_t = _norm_mod_fwd(x, None, None, n1g, sc1, sh1, name="norm1_fwd")
    proj = _mm(h1, wi, tm=1024, tn=PROJ_W, tk=d, name="in_proj")
    o_raw, y_gla, states = _gla_fwd(proj, wg, bg, gn, name="gla_fwd")
    qa, ka = _attn_prep(proj, qg8, kg8, name="attn_prep")
    sparse = [_dil_attn_fwd(qa, ka, proj, dil, name=f"attn_fwd_d{dil}") for dil in DILATIONS[1:]]
    mixed, y_att, lse = _dense_attn_fwd_merge(qa, ka, proj, sparse, y_gla, name="attn_fwd_d1_merge")
    attn_out = _mm(mixed, wo, tm=1024, tn=d, tk=d, name="out_proj")
    x1, h2, h2_t = _norm_mod_fwd(x, attn_out, g1, n2g, sc2, sh2, name="norm2_fwd")
    wup, wdown = ffn_weights(h2)
    u = _mm(h2, wup, out_dtype=BF16, tm=1024, tn=D_FF, tk=d, name="up_proj")
    act, uc = _conv_swiglu_fwd(u, conv_w, conv_b, name="conv_swiglu_fwd")
    ffn = _mm(act, wdown, tm=1024, tn=d, tk=D_FF, name="down_proj")
    dy, dffn, head_sums = _loss_head(x1, ffn, g2, target, name="loss_head")

    dact = _mm(dffn, wdown, tb=True, out_dtype=BF16, tm=1024, tn=D_FF, tk=d, name="down_proj_dx")
    g_wdown, g_wdown_b = _mm(act, dffn, ta=True, tm=1408, tn=d, tk=2048, also_bf16=True, name="down_proj_dw")
    duc, bias_sums = _swiglu_bwd(uc, dact, name="swiglu_bwd")
    du, tap_sums = _conv_bwd(duc, u, conv_w, name="conv_bwd")
    dh2 = _mm(du, wup, tb=True, tm=1024, tn=d, tk=D_FF, name="up_proj_dx")
    g_wup, g_wup_b = _mm(h2_t, du, tm=d, tn=1408, tk=2048, shard_cols=True, also_bf16=True, name="up_proj_dw")
    token = ffn_grads_ready(g_wup_b, g_wdown_b)
    g1_late = g1 if token is None else g1 + token[0:1, 0:1]
    dx1, dao, n2_sums = _norm_mod_bwd(x1, dh2, dy, n2g, sc2, attn_out, g1_late, name="norm2_bwd")

    dmixed = _mm(dao, wo, tb=True, tm=1024, tn=d, tk=d, name="out_proj_dx")
    g_wo = _mm(mixed, dao, ta=True, tm=d, tn=d, tk=1024, name="out_proj_dw")
    dgq, dgk, dgv, dgr, dlr, g_wg, gla_sums = _gla_bwd(proj, wg, bg, gn, o_raw, states, dmixed, name="gla_bwd")
    parts = [_dil_attn_bwd(qa, ka, proj, y_att, lse, dmixed, dil, name=f"attn_bwd_d{dil}") for dil in DILATIONS]
    daq, dak, dav, qk_sums = _attn_post(parts, proj, qg8, kg8, name="attn_post")
    dproj = jnp.concatenate([dgq, dgk, dgv, dgr, daq, dak, dav, dlr], axis=1)
    g_wi = _mm(h1_t, dproj, tm=512, tn=PROJ_W, tk=2048, name="in_proj_dw")
    token = attn_grads_ready(g_wi, g_wo)
    wi_late = wi if token is None else wi + token[0:1, 0:1].astype(BF16)
    dh1 = _mm(dproj, wi_late, tb=True, tm=1024, tn=d, tk=PROJ_W, name="in_proj_dx")
    grad_x, _, n1_sums = _norm_mod_bwd(x, dh1, dx1, n1g, sc1, None, None, name="norm1_bwd")

    n1, n2, hs, taps, cb = _colsum(n1_sums), _colsum(n2_sums), _colsum(head_sums), _colsum(tap_sums), _colsum(bias_sums)
    gs, qs = _colsum(gla_sums), _colsum(qk_sums)
    dmod = jnp.concatenate([n1[1], n1[0] * n1g[0], n2[2], n2[1], n2[0] * n2g[0], hs[0]])
    small = dict(
        dmod=dmod,
        norm1_g=n1[0] * (1.0 + sc1[0]), norm2_g=n2[0] * (1.0 + sc2[0]),
        gla_w_gate=g_wg[:GLA_GATE_RANK], gla_b_gate=gs[0, :256], gla_norm_g=gs[1].reshape(4, 128).sum(axis=0),
        q_norm_g=qs[0].reshape(8, 64).sum(axis=0), k_norm_g=qs[1].reshape(8, 64).sum(axis=0),
        conv_w=jnp.concatenate([taps[0], taps[1]], axis=1), conv_b=jnp.concatenate([cb[0], cb[1]]),
    )
    return head_sums[1], grad_x, (g_wi, g_wo, g_wup, g_wdown), small


N_DEV, N_CHIP = 8, 4
ANY = pl.BlockSpec(memory_space=pl.ANY)
VMEM_SPEC = pl.BlockSpec(memory_space=pltpu.VMEM)


def _place():
    x, y, c = lax.axis_index("x"), lax.axis_index("y"), lax.axis_index("c")
    other_chips = [(1 - x, y), (x, 1 - y), (1 - x, 1 - y)]
    return x, y, c, (x, y, 1 - c), other_chips


def _all_gather_small(v, *, name):
    m, n = v.shape

    def body(v_ref, out_ref, send_sems, recv_sems, local_sem):
        x, y, c, sibling, chips = _place()
        me = (x, y, c)

        def rows(px, py, pc):
            return out_ref.at[pl.ds((4 * px + 2 * py + pc) * m, m), :]

        def copy(k, block, to, src=None):
            return pltpu.make_async_remote_copy(
                src_ref=rows(*block) if src is None else src, dst_ref=rows(*block), send_sem=send_sems.at[k],
                recv_sem=recv_sems.at[k], device_id=to, device_id_type=MESH)

        mine = pltpu.make_async_copy(v_ref, rows(*me), local_sem)
        mine.start()
        first = [copy(0, me, sibling, src=v_ref)]
        first += [copy(1 + j, me, (*chip, c), src=v_ref) for j, chip in enumerate(chips)]
        for cp in first:
            cp.start()
        passed = [copy(4 + j, (*chip, c), sibling) for j, chip in enumerate(chips)]
        for j, chip in enumerate(chips):
            copy(1 + j, (*chip, c), me).wait_recv()
            passed[j].start()
        copy(0, sibling, me).wait_recv()
        for j, chip in enumerate(chips):
            copy(4 + j, (*chip, 1 - c), me).wait_recv()
        for cp in first + passed:
            cp.wait_send()
        mine.wait()

    return pl.pallas_call(
        body, name=name, out_shape=jax.ShapeDtypeStruct((N_DEV * m, n), v.dtype), in_specs=[VMEM_SPEC], out_specs=VMEM_SPEC,
        scratch_shapes=[pltpu.SemaphoreType.DMA((7,)), pltpu.SemaphoreType.DMA((7,)), pltpu.SemaphoreType.DMA],
    )(v)


def _gather_weight_shards(shards, *, name):
    nw = len(shards)

    def body(*refs):
        srcs, outs, (send_sems, recv_sems) = refs[:nw], refs[nw:2 * nw], refs[2 * nw:]
        x, y, c, sibling, chips = _place()
        index = lambda chip: 2 * chip[0] + chip[1]

        def copy(w, k, src, dst, to):
            return pltpu.make_async_remote_copy(src_ref=src, dst_ref=dst, send_sem=send_sems.at[6 * w + k],
                                                recv_sem=recv_sems.at[6 * w + k], device_id=to, device_id_type=MESH)

        sent = []
        for w, (src_ref, out_ref) in enumerate(zip(srcs, outs)):
            for k, chip in enumerate(chips):
                sent.append(copy(w, k, src_ref.at[c], out_ref.at[2 * x + y, c], (*chip, c)))
                sent[-1].start()
        for w, out_ref in enumerate(outs):
            for k, chip in enumerate(chips):
                landed = out_ref.at[index(chip), c]
                copy(w, k, landed, landed, (*chip, c)).wait_recv()
                sent.append(copy(w, 3 + k, landed, landed, sibling))
                sent[-1].start()
        for w, out_ref in enumerate(outs):
            for k, chip in enumerate(chips):
                passed_on = out_ref.at[index(chip), 1 - c]
                copy(w, 3 + k, passed_on, passed_on, sibling).wait_recv()
        for cp in sent:
            cp.wait_send()

    return pl.pallas_call(
        body, name=name, out_shape=[jax.ShapeDtypeStruct((N_CHIP, *s.shape), s.dtype) for s in shards],
        in_specs=[ANY] * nw, out_specs=[ANY] * nw,
        scratch_shapes=[pltpu.SemaphoreType.DMA((6 * nw,)), pltpu.SemaphoreType.DMA((6 * nw,))],
    )(*shards)


HBM_SPEC = pl.BlockSpec(memory_space=pltpu.HBM)
SEM_SPEC = pl.BlockSpec(memory_space=pltpu.SEMAPHORE)
DATAFLOW_EFFECT = pltpu.SideEffectType.DATAFLOW_SIDE_EFFECTING


def _late_copies(srcs, lands, send_sems, recv_sems):
    x, y, c, _, chips = _place()
    return [pltpu.make_async_remote_copy(
        src_ref=src.at[c], dst_ref=land.at[2 * x + y, c], send_sem=send_sems.at[6 * w + 2 * r + core],
        recv_sem=recv_sems.at[6 * w + 2 * r + c], device_id=(*chip, core), device_id_type=MESH)
        for w, (src, land) in enumerate(zip(srcs, lands)) for r, chip in enumerate(chips) for core in range(2)]


def _gather_late_start(own, after, *, name):
    nw = len(own)

    def body(*refs):
        srcs, lands, send_sems, recv_sems, token = refs[:nw], refs[nw:2 * nw], refs[2 * nw + 1], refs[2 * nw + 2], refs[-1]
        for cp in _late_copies(srcs, lands, send_sems, recv_sems):
            cp.start()
        token[...] = jnp.zeros_like(token)

    lands = [pltpu.with_memory_space_constraint(lax.empty((N_CHIP, *s.shape), s.dtype), pltpu.HBM) for s in own]
    own = [pltpu.with_memory_space_constraint(s, pltpu.HBM) for s in own]
    out = pl.pallas_call(
        body, name=name,
        out_shape=(pltpu.SemaphoreType.DMA((6 * nw,)), pltpu.SemaphoreType.DMA((6 * nw,)),
                   *[pltpu.HBM(s.shape, s.dtype) for s in own], *[pltpu.HBM(s.shape, s.dtype) for s in lands],
                   jax.ShapeDtypeStruct((8, 128), F32)),
        in_specs=[HBM_SPEC] * (2 * nw) + [ANY], out_specs=(SEM_SPEC, SEM_SPEC, *[HBM_SPEC] * (2 * nw), VMEM_SPEC),
        input_output_aliases={i: 2 + i for i in range(2 * nw)},
        compiler_params=pltpu.CompilerParams(has_side_effects=DATAFLOW_EFFECT))(*own, *lands, after)
    return out[0], out[1], out[2:2 + nw], out[2 + nw:2 + 2 * nw], out[-1]


def _gather_late_wait(send_sems, recv_sems, own, lands, after, *, name):
    nw = len(own)

    def body(*refs):
        srcs, lands_in, send_sems, recv_sems = refs[:nw], refs[nw:2 * nw], refs[2 * nw], refs[2 * nw + 1]
        x, y, c, _, chips = _place()
        for cp in _late_copies(srcs, lands_in, send_sems, recv_sems):
            cp.wait_send()
        for w, (src, land) in enumerate(zip(srcs, lands_in)):
            for r, chip in enumerate(chips):
                for core in range(2):
                    pltpu.make_async_remote_copy(
                        src_ref=src.at[c], dst_ref=land.at[2 * chip[0] + chip[1], core], send_sem=send_sems.at[6 * w + 2 * r + core],
                        recv_sem=recv_sems.at[6 * w + 2 * r + core], device_id=(*chip, core), device_id_type=MESH).wait_recv()

    out = pl.pallas_call(
        body, name=name, out_shape=(*[pltpu.HBM(s.shape, s.dtype) for s in own], *[pltpu.HBM(s.shape, s.dtype) for s in lands]),
        in_specs=[HBM_SPEC] * (2 * nw) + [SEM_SPEC, SEM_SPEC, ANY], out_specs=tuple([HBM_SPEC] * (2 * nw)),
        input_output_aliases={i: i for i in range(2 * nw)},
        compiler_params=pltpu.CompilerParams(has_side_effects=DATAFLOW_EFFECT))(*own, *lands, send_sems, recv_sems, after)
    return out[:nw], out[nw:]


def _direct_reduce_copies(srcs, lands, send_sems, recv_sems):
    x, y, c, _, _ = _place()
    cps = []
    for w, (src, land) in enumerate(zip(srcs, lands)):
        for rel in range(1, N_DEV):
            tx, ty, tc = (1 - x if rel & 4 else x), (1 - y if rel & 2 else y), (1 - c if rel & 1 else c)
            cps.append(pltpu.make_async_remote_copy(
                src_ref=src.at[2 * tx + ty, tc], dst_ref=land.at[rel - 1], send_sem=send_sems.at[7 * w + rel - 1],
                recv_sem=recv_sems.at[7 * w + rel - 1], device_id=(tx, ty, tc), device_id_type=MESH))
    return cps


def _direct_reduce_start(grads, *, name):
    nw = len(grads)

    def body(*refs):
        srcs, lands, send_sems, recv_sems, token = refs[:nw], refs[nw:2 * nw], refs[2 * nw], refs[2 * nw + 1], refs[-1]
        for cp in _direct_reduce_copies(srcs, lands, send_sems, recv_sems):
            cp.start()
        token[...] = jnp.zeros_like(token)

    lands = [pltpu.with_memory_space_constraint(lax.empty((N_DEV - 1, *g.shape[2:]), g.dtype), pltpu.HBM) for g in grads]
    grads = [pltpu.with_memory_space_constraint(g, pltpu.HBM) for g in grads]
    out = pl.pallas_call(
        body, name=name,
        out_shape=(pltpu.SemaphoreType.DMA((7 * nw,)), pltpu.SemaphoreType.DMA((7 * nw,)),
                   *[pltpu.HBM(g.shape, g.dtype) for g in grads], *[pltpu.HBM(t.shape, t.dtype) for t in lands],
                   jax.ShapeDtypeStruct((8, 128), F32)),
        in_specs=[HBM_SPEC] * (2 * nw), out_specs=(SEM_SPEC, SEM_SPEC, *[HBM_SPEC] * (2 * nw), VMEM_SPEC),
        input_output_aliases={i: 2 + i for i in range(2 * nw)},
        compiler_params=pltpu.CompilerParams(has_side_effects=DATAFLOW_EFFECT))(*grads, *lands)
    return out[0], out[1], out[2:2 + nw], out[2 + nw:2 + 2 * nw], out[-1]


def _direct_reduce_wait(send_sems, recv_sems, grads, lands, after, *, name):
    nw = len(grads)

    def body(*refs):
        srcs, lands_in, send_sems, recv_sems = refs[:nw], refs[nw:2 * nw], refs[2 * nw], refs[2 * nw + 1]
        cps = _direct_reduce_copies(srcs, lands_in, send_sems, recv_sems)
        for cp in cps:
            cp.wait_send()
        for cp in cps:
            cp.wait_recv()

    out = pl.pallas_call(
        body, name=name, out_shape=(*[pltpu.HBM(g.shape, g.dtype) for g in grads], *[pltpu.HBM(t.shape, t.dtype) for t in lands]),
        in_specs=[HBM_SPEC] * (2 * nw) + [SEM_SPEC, SEM_SPEC, ANY], out_specs=tuple([HBM_SPEC] * (2 * nw)),
        input_output_aliases={i: i for i in range(2 * nw)},
        compiler_params=pltpu.CompilerParams(has_side_effects=DATAFLOW_EFFECT))(*grads, *lands, send_sems, recv_sems, after)
    return out[nw:]


def _direct_reduce_add(grad, landed, chip, core, *, name):
    _, r, n = grad.shape
    half = r // 2
    tr = _row_tile(half)
    nb = half // tr

    def body(chip_ref, core_ref, g_ref, t_ref, o_ref):
        acc = g_ref[0]
        for k in range(N_DEV - 1):
            acc = acc + t_ref[k].astype(F32)
        o_ref[...] = acc

    return pl.pallas_call(
        body, name=name,
        grid_spec=pltpu.PrefetchScalarGridSpec(
            num_scalar_prefetch=2, grid=(nb,),
            in_specs=[pl.BlockSpec((1, tr, n), lambda i, chip_ref, core_ref: (chip_ref[0], core_ref[0] * nb + i, 0)),
                      pl.BlockSpec((N_DEV - 1, tr, n), lambda i, chip_ref, core_ref: (0, i, 0))],
            out_specs=pl.BlockSpec((tr, n), lambda i, chip_ref, core_ref: (i, 0))),
        out_shape=jax.ShapeDtypeStruct((half, n), F32), compiler_params=_params("parallel"))(chip, core, grad, landed)


def _share_halves(halves, *, name):
    nw = len(halves)

    def body(*refs):
        srcs, outs, (send_sems, recv_sems) = refs[:nw], refs[nw:2 * nw], refs[2 * nw:]
        _, _, _, sibling, _ = _place()
        cps = [pltpu.make_async_remote_copy(src_ref=src_ref, dst_ref=out_ref, send_sem=send_sems.at[w], recv_sem=recv_sems.at[w],
                                            device_id=sibling, device_id_type=MESH)
               for w, (src_ref, out_ref) in enumerate(zip(srcs, outs))]
        for cp in cps:
            cp.start()
        for cp in cps:
            cp.wait()

    return pl.pallas_call(
        body, name=name, out_shape=[jax.ShapeDtypeStruct(h.shape, h.dtype) for h in halves],
        in_specs=[ANY] * nw, out_specs=[ANY] * nw,
        scratch_shapes=[pltpu.SemaphoreType.DMA((nw,)), pltpu.SemaphoreType.DMA((nw,))])(*halves)


def _row_tile(rows, limit=256):
    return next(t for t in range(limit, 15, -16) if rows % t == 0)


def _sum_devices(gathered, *, name):
    _, m, n = gathered.shape

    def body(g_ref, tot_ref, loss_ref):
        tot = g_ref[0]
        for dev in range(1, N_DEV):
            tot = tot + g_ref[dev]
        tot_ref[...] = tot
        loss_ref[...] = jnp.full((8, n), (0.5 / D_MODEL) * jnp.sum(tot[0:8]), F32)

    return pl.pallas_call(body, name=name, in_specs=[VMEM_SPEC], out_specs=[VMEM_SPEC, VMEM_SPEC],
                          out_shape=[jax.ShapeDtypeStruct((m, n), F32), jax.ShapeDtypeStruct((8, n), F32)])(gathered)


def _ada_mod(cond_all, w_ada_shard, *, name):
    tn = 512

    def body(a_ref, b_ref, o_ref):
        o_ref[...] = _nn(a_ref[...], b_ref[...], precision=HIGHEST)

    return pl.pallas_call(
        body, name=name, grid=(w_ada_shard.shape[1] // tn,),
        in_specs=[pl.BlockSpec(cond_all.shape, lambda j: (0, 0)), pl.BlockSpec((D_MODEL, tn), lambda j: (0, j))],
        out_specs=pl.BlockSpec((N_DEV, tn), lambda j: (0, j)),
        out_shape=jax.ShapeDtypeStruct((N_DEV, w_ada_shard.shape[1]), F32), compiler_params=_params("parallel"))(cond_all, w_ada_shard)


def _ada_grad(cond_all, dmod_cols, *, name):
    tm = 256

    def body(a_ref, b_ref, o_ref):
        o_ref[...] = lax.dot_general(a_ref[...], b_ref[...], (((0,), (0,)), ((), ())), precision=HIGHEST,
                                     preferred_element_type=F32)

    return pl.pallas_call(
        body, name=name, grid=(D_MODEL // tm,),
        in_specs=[pl.BlockSpec((N_DEV, tm), lambda i: (0, i)), pl.BlockSpec(dmod_cols.shape, lambda i: (0, 0))],
        out_specs=pl.BlockSpec((tm, dmod_cols.shape[1]), lambda i: (i, 0)),
        out_shape=jax.ShapeDtypeStruct((D_MODEL, dmod_cols.shape[1]), F32), compiler_params=_params("parallel"))(cond_all, dmod_cols)


def _silu_rows(c8, *, name):
    def body(c_ref, o_ref):
        cv = c_ref[...]
        o_ref[...] = cv * _sigmoid(cv)

    return pl.pallas_call(body, name=name, in_specs=[VMEM_SPEC], out_specs=VMEM_SPEC,
                          out_shape=jax.ShapeDtypeStruct(c8.shape, F32))(c8)


def _rows128(t, rows=None):
    flat = t.reshape(-1, 128)
    return flat if rows is None else jnp.pad(flat, ((0, rows - flat.shape[0]), (0, 0)))


def _from_col_shards(shards, r, n):
    return shards.reshape(N_CHIP, r, n).transpose(1, 0, 2).reshape(r, N_CHIP * n)


def kernel(x, c, w_ada, b_ada, norm1_g, w_in, gla_w_gate, gla_b_gate, gla_norm_g, q_norm_g, k_norm_g, w_out, norm2_g, w_up, conv_w, conv_b, w_down, loss_target, m_w_ada, m_b_ada, m_norm1_g, m_w_in, m_gla_w_gate, m_gla_b_gate, m_gla_norm_g, m_q_norm_g, m_k_norm_g, m_w_out, m_norm2_g, m_w_up, m_conv_w, m_conv_b, m_w_down, v_w_ada, v_b_ada, v_norm1_g, v_w_in, v_gla_w_gate, v_gla_b_gate, v_gla_norm_g, v_q_norm_g, v_k_norm_g, v_w_out, v_norm2_g, v_w_up, v_conv_w, v_conv_b, v_w_down):
    d = D_MODEL
    ax, ay, ac = lax.axis_index("x"), lax.axis_index("y"), lax.axis_index("c")
    chip, dev = 2 * ax + ay, 4 * ax + 2 * ay + ac

    cond = _silu_rows(jnp.broadcast_to(c, (8, d)), name="cond_silu")[0:1]
    small_in = jnp.concatenate([_rows128(cond), _rows128(conv_w[0]), _rows128(gla_w_gate[0])], axis=0)
    small_in = _rows128(small_in, 56)
    got = _all_gather_small(small_in, name="gather_small").reshape(N_DEV, 56, 128)
    cond_all = got[:, 0:8].reshape(N_DEV, d)
    conv_w_full = _from_col_shards(got[0::2, 8:41].reshape(N_CHIP, 3 * 1408 // 128, 128), 3, 1408)
    gate_full = _from_col_shards(got[0::2, 41:49].reshape(N_CHIP, 16 * 64 // 128, 128), GLA_GATE_RANK, 64)
    mod_part = _ada_mod(cond_all, w_ada[0], name="ada_mod")
    mod_got = _all_gather_small(_rows128(mod_part), name="gather_mod").reshape(N_DEV, N_DEV, 1536)
    mod_all = mod_got[0::2].transpose(1, 0, 2).reshape(N_DEV, 6 * d) + b_ada
    mod = lax.dynamic_slice_in_dim(mod_all, dev, 1, axis=0)

    own = [w[0].astype(BF16).reshape(2, w.shape[1] // 2, w.shape[2]) for w in (w_in, w_out, w_up, w_down)]
    with_own = lambda got, mine: [lax.dynamic_update_index_in_dim(t, o, chip, 0) for t, o in zip(got, mine)]
    got_in, got_out = with_own(_gather_weight_shards(own[:2], name="gather_weights"), own[:2])
    w_in_full = got_in.reshape(N_CHIP, d, 772).transpose(1, 0, 2).reshape(d, N_CHIP * 772)
    w_out_full = got_out.reshape(d, d)
    exchanged = mod_all[0:1, 0:1] + got_in[0, 0, 0:1, 0:1].astype(F32)
    send_sems, recv_sems, own_thru, lands, token = _gather_late_start(own[2:], exchanged, name="gather_late_start")
    mod = mod + token[0:1, 0:1]

    def ffn_weights(after):
        mine, landed = _gather_late_wait(send_sems, recv_sems, own_thru, lands, after, name="gather_late_wait")
        got_up, got_down = with_own(landed, mine)
        return got_up.reshape(N_CHIP, d, 1408).transpose(1, 0, 2).reshape(d, 2 * D_FF), got_down.reshape(D_FF, d)

    ffn_reduce, attn_reduce, attn_parts = [], [], []
    halves_of = lambda g: g.reshape(N_CHIP, 2, g.shape[-2] // 2, g.shape[-1])

    def ffn_grads_ready(g_wup_b, g_wdown_b):
        ffn_reduce.extend(_direct_reduce_start([halves_of(g_wup_b), halves_of(g_wdown_b.reshape(N_CHIP, D_FF // N_CHIP, d))],
                                               name="reduce_ffn_start"))
        return ffn_reduce[4]

    def attn_grads_ready(g_wi, g_wo):
        attn_parts.extend([_in_proj_grad_layout(g_wi).reshape(d, N_CHIP, 772).transpose(1, 0, 2), g_wo.reshape(N_CHIP, d // N_CHIP, d)])
        attn_reduce.extend(_direct_reduce_start([halves_of(g.astype(BF16)) for g in attn_parts], name="reduce_attn_start"))
        return attn_reduce[4]

    err2, grad_x, (g_wi, g_wo, g_wup, g_wdown), small = _local_step(
        x[0], loss_target[0], mod, _in_proj_layout(w_in_full), w_out_full, ffn_weights, ffn_grads_ready, attn_grads_ready,
        conv_w_full, conv_b,
        _gate_layout(gate_full), gla_b_gate, gla_norm_g, q_norm_g, k_norm_g, norm1_g, norm2_g)

    pieces = [err2[0], small["dmod"], small["norm1_g"], small["norm2_g"], small["gla_w_gate"].reshape(-1), small["gla_b_gate"],
              small["gla_norm_g"], small["q_norm_g"], small["k_norm_g"], small["conv_w"].reshape(-1), small["conv_b"]]
    sizes = [p.shape[0] for p in pieces]
    at = [sum(sizes[:i]) for i in range(len(sizes) + 1)]
    vec = _rows128(jnp.concatenate(pieces), 288)
    got = _all_gather_small(vec, name="gather_grads").reshape(N_DEV, 288, 128)
    total, loss8 = _sum_devices(got, name="sum_devices")
    total = total.reshape(-1)
    seg = lambda i: total[at[i]:at[i + 1]]
    dmod_all = got.reshape(N_DEV, -1)[:, at[1]:at[2]]
    g_small = dict(
        b_ada=seg(1)[None], norm1_g=seg(2)[None], norm2_g=seg(3)[None],
        gla_w_gate=lax.dynamic_slice_in_dim(seg(4).reshape(GLA_GATE_RANK, 256), chip * 64, 64, axis=1),
        gla_b_gate=seg(5)[None], gla_norm_g=seg(6)[None], q_norm_g=seg(7)[None], k_norm_g=seg(8)[None],
        conv_w=lax.dynamic_slice_in_dim(seg(9).reshape(3, 2 * D_FF), chip * 1408, 1408, axis=1), conv_b=seg(10)[None])
    dmod_cols = lax.dynamic_slice_in_dim(dmod_all.reshape(N_DEV, 6 * d), chip * 1536, 1536, axis=1)
    g_w_ada = _ada_grad(cond_all, dmod_cols, name="ada_grad")

    core_id, chip_id = jnp.reshape(ac, (1,)).astype(jnp.int32), jnp.reshape(chip, (1,)).astype(jnp.int32)
    landed = (_direct_reduce_wait(*attn_reduce[:4], grad_x, name="reduce_attn_wait")
              + _direct_reduce_wait(*ffn_reduce[:4], grad_x, name="reduce_ffn_wait"))
    own = attn_parts + [g_wup, g_wdown.reshape(N_CHIP, D_FF // N_CHIP, d)]
    summed = [_direct_reduce_add(g, t, chip_id, core_id, name=f"reduce_add_{tag}")
              for g, t, tag in zip(own, landed, ("w_in", "w_out", "w_up", "w_down"))]
    others = _share_halves(summed, name="share_pair")

    grads = dict(w_ada=g_w_ada, **g_small, **dict(zip(("w_in", "w_out", "w_up", "w_down"), zip(summed, others))))
    names = ["w_ada", "b_ada", "norm1_g", "w_in", "gla_w_gate", "gla_b_gate", "gla_norm_g", "q_norm_g", "k_norm_g", "w_out",
             "norm2_g", "w_up", "conv_w", "conv_b", "w_down"]
    ws = dict(w_ada=w_ada, b_ada=b_ada, norm1_g=norm1_g, w_in=w_in, gla_w_gate=gla_w_gate, gla_b_gate=gla_b_gate,
              gla_norm_g=gla_norm_g, q_norm_g=q_norm_g, k_norm_g=k_norm_g, w_out=w_out, norm2_g=norm2_g, w_up=w_up,
              conv_w=conv_w, conv_b=conv_b, w_down=w_down)
    ms = dict(w_ada=m_w_ada, b_ada=m_b_ada, norm1_g=m_norm1_g, w_in=m_w_in, gla_w_gate=m_gla_w_gate, gla_b_gate=m_gla_b_gate,
              gla_norm_g=m_gla_norm_g, q_norm_g=m_q_norm_g, k_norm_g=m_k_norm_g, w_out=m_w_out, norm2_g=m_norm2_g, w_up=m_w_up,
              conv_w=m_conv_w, conv_b=m_conv_b, w_down=m_w_down)
    vs = dict(w_ada=v_w_ada, b_ada=v_b_ada, norm1_g=v_norm1_g, w_in=v_w_in, gla_w_gate=v_gla_w_gate, gla_b_gate=v_gla_b_gate,
              gla_norm_g=v_gla_norm_g, q_norm_g=v_q_norm_g, k_norm_g=v_k_norm_g, w_out=v_w_out, norm2_g=v_norm2_g, w_up=v_w_up,
              conv_w=v_conv_w, conv_b=v_conv_b, w_down=v_w_down)
    g_out, d_out, m_out, v_out = [], [], [], []
    for nm in names:
        shape = ws[nm].shape
        flip = (lambda t: t.T) if shape[-1] % 128 and shape[-2] % 128 == 0 else (lambda t: t)
        w2 = flip(ws[nm].reshape(shape[-2:]))
        if isinstance(grads[nm], tuple):
            mine, other = grads[nm]
            dl, mn, vn, g2 = _adamw(w2, (flip(mine), flip(other), core_id), flip(ms[nm].reshape(shape[-2:])),
                                    flip(vs[nm].reshape(shape[-2:])), name=f"adamw_{nm}")
        else:
            g2 = flip(grads[nm].reshape(shape[-2:]))
            dl, mn, vn = _adamw(w2, g2, flip(ms[nm].reshape(shape[-2:])), flip(vs[nm].reshape(shape[-2:])), name=f"adamw_{nm}")
        for outs, t in ((g_out, g2), (d_out, dl), (m_out, mn), (v_out, vn)):
            outs.append(flip(t).reshape(shape))
    return (loss8[0, 0], grad_x[None], *g_out, *d_out, *m_out, *v_out)
```

```python
import functools

import jax
import jax.numpy as jnp
from jax import lax
from jax.experimental import pallas as pl
from jax.experimental.pallas import tpu as pltpu

F32, BF16 = jnp.float32, jnp.bfloat16
HIGHEST = lax.Precision.HIGHEST
MESH = pl.DeviceIdType.MESH

D_MODEL = 1024
GLA_CHUNK = 64
GLA_GATE_TAU = 16.0
GLA_GATE_RANK = 16
HEAD_LANES = 128
ATTN_BLOCK = 128
DILATIONS = (1, 4, 16)
ALIBI_SLOPES = tuple(2.0 ** (-(h + 1)) for h in range(8))
D_FF = 2816
EPS = 1e-6
C_GQ, C_GK, C_GV, C_GR, C_AQ, C_AK, C_AV, C_LR, PROJ_W = 0, 256, 512, 1024, 1536, 2048, 2560, 3072, 3200
ADAM_LR, ADAM_B1, ADAM_B2, ADAM_EPS, ADAM_WD, ADAM_STEP = 0.001, 0.9, 0.999, 1e-08, 0.01, 10
VMEM_LIMIT_BYTES = 56 * 1024 * 1024
ROW_TILE = 512
ADAM_TILE = 256


def _params(*sem):
    return pltpu.CompilerParams(dimension_semantics=sem or None, vmem_limit_bytes=VMEM_LIMIT_BYTES)


def _nt(a, b):
    return lax.dot_general(a, b, (((1,), (1,)), ((), ())), preferred_element_type=F32)


def _tn(a, b):
    return lax.dot_general(a, b, (((0,), (0,)), ((), ())), preferred_element_type=F32)


def _nn(a, b, precision=None):
    return jnp.dot(a, b, preferred_element_type=F32, precision=precision)


def _split3(v):
    hi = v.astype(BF16)
    rest = v - hi.astype(F32)
    mid = rest.astype(BF16)
    return hi, mid, (rest - mid.astype(F32)).astype(BF16)


def _sum_right(v, ones):
    hi, mid, lo = _split3(v)
    return (_nn(lo, ones) + _nn(mid, ones)) + _nn(hi, ones)


def _sum_left(ones, v):
    hi, mid, lo = _split3(v)
    return (_nn(ones, lo) + _nn(ones, mid)) + _nn(ones, hi)


def _fold8(v):
    return v.reshape(v.shape[0] // 8, 8, v.shape[1]).sum(axis=0)


def _spread_total(ref):
    t = ref[...]
    ref[...] = jnp.broadcast_to(jnp.sum(t, axis=-2, keepdims=True), t.shape)


def _sigmoid(x):
    return 1.0 / (1.0 + jnp.exp(-x))


def _mm(a, b, *, ta=False, tb=False, out_dtype=F32, tm, tn, tk, shard_cols=False, also_bf16=False, name):
    (k_a, m) = a.shape if ta else a.shape[::-1]
    (k_b, n) = b.shape[::-1] if tb else b.shape
    assert k_a == k_b and m % tm == 0 and n % tn == 0 and k_a % tk == 0, (name, a.shape, b.shape)
    nk = k_a // tk
    assert nk == 1 or out_dtype == F32, name
    dims = (((0 if ta else 1,), (1 if tb else 0,)), ((), ()))

    def body(a_ref, b_ref, o_ref, *rounded):
        k = pl.program_id(2)
        part = lax.dot_general(a_ref[...].astype(BF16), b_ref[...].astype(BF16), dims, preferred_element_type=F32)
        if nk == 1:
            o_ref[...] = part.astype(out_dtype)
        else:
            @pl.when(k == 0)
            def _():
                o_ref[...] = part

            @pl.when(k > 0)
            def _():
                o_ref[...] += part

        if also_bf16:
            @pl.when(k == nk - 1)
            def _():
                rounded[0][...] = o_ref[...].astype(BF16)

    a_spec = pl.BlockSpec((tk, tm), lambda i, j, k: (k, i)) if ta else pl.BlockSpec((tm, tk), lambda i, j, k: (i, k))
    b_spec = pl.BlockSpec((tn, tk), lambda i, j, k: (j, k)) if tb else pl.BlockSpec((tk, tn), lambda i, j, k: (k, j))
    if shard_cols:
        o_spec, o_shape = pl.BlockSpec((None, tm, tn), lambda i, j, k: (j, i, 0)), (n // tn, m, tn)
    else:
        o_spec, o_shape = pl.BlockSpec((tm, tn), lambda i, j, k: (i, j)), (m, n)
    shapes = [jax.ShapeDtypeStruct(o_shape, out_dtype)] + ([jax.ShapeDtypeStruct(o_shape, BF16)] if also_bf16 else [])
    out = pl.pallas_call(
        body, name=name, grid=(m // tm, n // tn, nk), in_specs=[a_spec, b_spec], out_specs=[o_spec] * len(shapes),
        out_shape=shapes, compiler_params=_params("parallel", "parallel", "arbitrary"))(a, b)
    return out if also_bf16 else out[0]


def _norm_mod_fwd(x, branch, gate, gain, scale, shift, *, name):
    s, d = x.shape
    tm = 2 * ROW_TILE
    has_branch = branch is not None

    def body(*refs):
        if has_branch:
            x_ref, br_ref, gate_ref, gain_ref, sc_ref, sh_ref, x1_ref, h_ref, ht_ref = refs
            xv = x_ref[...] + gate_ref[...] * br_ref[...]
            x1_ref[...] = xv
        else:
            x_ref, gain_ref, sc_ref, sh_ref, h_ref, ht_ref = refs
            xv = x_ref[...]
        r = lax.rsqrt(jnp.mean(xv * xv, axis=-1, keepdims=True) + EPS)
        h = (xv * r) * gain_ref[...] * (1.0 + sc_ref[...]) + sh_ref[...]
        h_ref[...] = h.astype(BF16)
        ht_ref[...] = h.T.astype(BF16)

    row = pl.BlockSpec((tm, d), lambda i: (i, 0))
    col = pl.BlockSpec((d, tm), lambda i: (0, i))
    vec = pl.BlockSpec((1, d), lambda i: (0, 0))
    h_shapes = [jax.ShapeDtypeStruct((s, d), BF16), jax.ShapeDtypeStruct((d, s), BF16)]
    if has_branch:
        return pl.pallas_call(
            body, name=name, grid=(s // tm,), in_specs=[row, row, vec, vec, vec, vec], out_specs=[row, row, col],
            out_shape=[jax.ShapeDtypeStruct((s, d), F32)] + h_shapes,
            compiler_params=_params("parallel"))(x, branch, gate, gain, scale, shift)
    h, ht = pl.pallas_call(
        body, name=name, grid=(s // tm,), in_specs=[row, vec, vec, vec], out_specs=[row, col],
        out_shape=h_shapes, compiler_params=_params("parallel"))(x, gain, scale, shift)
    return x, h, ht


def _norm_mod_bwd(x, dh, dres, gain, scale, branch, gate, *, name):
    s, d = x.shape
    tm = ROW_TILE
    has_branch = branch is not None

    def body(*refs):
        if has_branch:
            x_ref, dh_ref, dres_ref, gain_ref, sc_ref, br_ref, gate_ref, dx_ref, dbr_ref, sums_ref = refs
        else:
            x_ref, dh_ref, dres_ref, gain_ref, sc_ref, dx_ref, sums_ref = refs
        i = pl.program_id(0)

        @pl.when(i == 0)
        def _():
            sums_ref[...] = jnp.zeros_like(sums_ref)

        xv, dhv = x_ref[...], dh_ref[...]
        r = lax.rsqrt(jnp.mean(xv * xv, axis=-1, keepdims=True) + EPS)
        xn = xv * r
        dxn = dhv * (gain_ref[...] * (1.0 + sc_ref[...]))
        dx = dres_ref[...] + r * (dxn - xn * jnp.mean(dxn * xn, axis=-1, keepdims=True))
        dx_ref[...] = dx
        sums_ref[0] += _fold8(dhv * xn)
        sums_ref[1] += _fold8(dhv)
        if has_branch:
            dbr_ref[...] = (gate_ref[...] * dx).astype(BF16)
            sums_ref[2] += _fold8(dx * br_ref[...])

        @pl.when(i == s // tm - 1)
        def _():
            _spread_total(sums_ref)

    row = pl.BlockSpec((tm, d), lambda i: (i, 0))
    vec = pl.BlockSpec((1, d), lambda i: (0, 0))
    sums = pl.BlockSpec((3, 8, d), lambda i: (0, 0, 0))
    sums_shape = jax.ShapeDtypeStruct((3, 8, d), F32)
    if has_branch:
        return pl.pallas_call(
            body, name=name, grid=(s // tm,), in_specs=[row, row, row, vec, vec, row, vec], out_specs=[row, row, sums],
            out_shape=[jax.ShapeDtypeStruct((s, d), F32), jax.ShapeDtypeStruct((s, d), BF16), sums_shape],
            compiler_params=_params("arbitrary"))(x, dh, dres, gain, scale, branch, gate)
    dx, sm = pl.pallas_call(
        body, name=name, grid=(s // tm,), in_specs=[row, row, row, vec, vec], out_specs=[row, sums],
        out_shape=[jax.ShapeDtypeStruct((s, d), F32), sums_shape],
        compiler_params=_params("arbitrary"))(x, dh, dres, gain, scale)
    return dx, None, sm


GLA_ROWS = 256


def _gla_block_setup(lr_ref, wg_ref, bg_ref):
    t, c = GLA_ROWS, GLA_CHUNK
    ri = lax.broadcasted_iota(jnp.int32, (t, t), 0)
    ci = lax.broadcasted_iota(jnp.int32, (t, t), 1)
    same = (ri // c) == (ci // c)
    causal, upper = same & (ci <= ri), same & (ci >= ri)
    z = _nn(lr_ref[...].astype(BF16), wg_ref[...]) + bg_ref[...]
    g = (jnp.minimum(z, 0.0) - jnp.log(1.0 + jnp.exp(-jnp.abs(z)))) * (1.0 / GLA_GATE_TAU)
    hi, mid, lo = _split3(g)
    total = lambda ones: (_nn(ones, lo) + _nn(ones, mid)) + _nn(ones, hi)
    return z, total(causal.astype(BF16)), total(same.astype(BF16)), causal, upper


def _chunks(t):
    return [t[i * GLA_CHUNK:(i + 1) * GLA_CHUNK] for i in range(GLA_ROWS // GLA_CHUNK)]


def _gla_fwd(proj, wg, bg, gn, *, name):
    s = proj.shape[0]
    tb, c = GLA_ROWS, GLA_CHUNK
    cb = tb // c

    def body(q_ref, k_ref, v_ref, r_ref, lr_ref, wg_ref, bg_ref, gn_ref, o_ref, y_ref, st_ref, state):
        i = pl.program_id(0)

        @pl.when(i == 0)
        def _():
            state[...] = jnp.zeros_like(state)

        low = lax.broadcasted_iota(jnp.int32, (tb, HEAD_LANES), 1) < 64
        masks = (low, jnp.logical_not(low))
        _, b, b_end, causal, _ = _gla_block_setup(lr_ref, wg_ref, bg_ref)
        pairs = []
        for p in range(2):
            cols = pl.ds(p * HEAD_LANES, HEAD_LANES)
            bp, bep = (t[:, p * HEAD_LANES:(p + 1) * HEAD_LANES] for t in (b, b_end))
            k = k_ref[:, cols]
            q_in = q_ref[:, cols] * 0.125 * jnp.exp(bp)
            k_out = (k * jnp.exp(-bp)).astype(BF16)
            k_end = k * jnp.exp(bep - bp)
            qms = [jnp.where(m, q_in, 0.0).astype(BF16) for m in masks]
            kes = [jnp.where(m, k_end, 0.0).astype(BF16) for m in masks]
            vs = [v_ref[:, pl.ds((2 * p + e) * HEAD_LANES, HEAD_LANES)].astype(BF16) for e in range(2)]
            grow = [_tn(v0, k0) + _tn(v1, k1) for v0, k0, v1, k1 in zip(_chunks(vs[0]), _chunks(kes[0]), _chunks(vs[1]), _chunks(kes[1]))]
            pairs.append((bep, k_out, qms, vs, grow))
        entering = [[], []]
        for p, (bep, _, _, _, grow) in enumerate(pairs):
            st = state[p]
            for ch in range(cb):
                entering[p].append(st)
                st_ref[ch, p] = st
                st = st * jnp.exp(bep[ch * c:ch * c + 1, :]) + grow[ch]
            state[p] = st
        for p, (_, k_out, qms, vs, _) in enumerate(pairs):
            for e in range(2):
                hc = pl.ds((2 * p + e) * HEAD_LANES, HEAD_LANES)
                a = jnp.where(causal, _nt(qms[e], k_out), 0.0).astype(BF16)
                carried = jnp.concatenate([_nt(qc, sc.astype(BF16)) for qc, sc in zip(_chunks(qms[e]), entering[p])], axis=0)
                o = _nn(a, vs[e]) + carried
                o_ref[:, hc] = o
                rr = r_ref[:, hc]
                on = o * lax.rsqrt(jnp.mean(o * o, axis=-1, keepdims=True) + EPS)
                y_ref[:, hc] = (on * gn_ref[...] * (rr * _sigmoid(rr))).astype(BF16)

    def col(width, at):
        return pl.BlockSpec((tb, width), lambda i: (i, at // width))

    full = lambda shape: pl.BlockSpec(shape, lambda i: tuple(0 for _ in shape))
    return pl.pallas_call(
        body, name=name, grid=(s // tb,),
        in_specs=[col(256, C_GQ), col(256, C_GK), col(512, C_GV), col(512, C_GR), col(128, C_LR),
                  full((HEAD_LANES, 256)), full((1, 256)), full((1, HEAD_LANES))],
        out_specs=[pl.BlockSpec((tb, 512), lambda i: (i, 0)), pl.BlockSpec((tb, 512), lambda i: (i, 0)),
                   pl.BlockSpec((cb, 2, HEAD_LANES, HEAD_LANES), lambda i: (i, 0, 0, 0))],
        out_shape=[jax.ShapeDtypeStruct((s, 512), F32), jax.ShapeDtypeStruct((s, 512), BF16),
                   jax.ShapeDtypeStruct((s // c, 2, HEAD_LANES, HEAD_LANES), F32)],
        scratch_shapes=[pltpu.VMEM((2, HEAD_LANES, HEAD_LANES), F32)],
        compiler_params=_params("arbitrary"))(proj, proj, proj, proj, proj, wg, bg, gn)


def _gla_bwd(proj, wg, bg, gn, o_raw, states, dmixed, *, name):
    s = proj.shape[0]
    tb, c = GLA_ROWS, GLA_CHUNK
    cb = tb // c
    nblk, nch = s // tb, s // c

    def body(q_ref, k_ref, v_ref, r_ref, lr_ref, wg_ref, bg_ref, gn_ref, o_ref, st_ref, stn_ref, dy_ref,
             dq_ref, dk_ref, dv_ref, dr_ref, dlr_ref, gwg_ref, sums_ref, dstate):
        i = pl.program_id(0)

        @pl.when(i == 0)
        def _():
            dstate[...] = jnp.zeros_like(dstate)
            gwg_ref[...] = jnp.zeros_like(gwg_ref)
            sums_ref[...] = jnp.zeros_like(sums_ref)

        low = lax.broadcasted_iota(jnp.int32, (tb, HEAD_LANES), 1) < 64
        masks = (low, jnp.logical_not(low))
        z, b, b_end, causal, upper = _gla_block_setup(lr_ref, wg_ref, bg_ref)
        lr_b = lr_ref[...].astype(BF16)
        dlr = jnp.zeros((tb, HEAD_LANES), F32)
        per_chunk = lambda rows, mats, fn: jnp.concatenate([fn(r, m.astype(BF16)) for r, m in zip(_chunks(rows), mats)], axis=0)
        pairs = []
        for p in range(2):
            cols = pl.ds(p * HEAD_LANES, HEAD_LANES)
            sl = slice(p * HEAD_LANES, (p + 1) * HEAD_LANES)
            bp, bep = b[:, sl], b_end[:, sl]
            e_in, e_out, e_end = jnp.exp(bp), jnp.exp(-bp), jnp.exp(bep - bp)
            q = q_ref[:, cols] * 0.125
            k = k_ref[:, cols]
            q_in, k_out, k_end = q * e_in, k * e_out, k * e_end
            qms = [jnp.where(m, q_in, 0.0).astype(BF16) for m in masks]
            kms_out = [jnp.where(m, k_out, 0.0).astype(BF16) for m in masks]
            kms_end = [jnp.where(m, k_end, 0.0).astype(BF16) for m in masks]
            vs, dos = [], []
            for e in range(2):
                hc = pl.ds((2 * p + e) * HEAD_LANES, HEAD_LANES)
                o, rr, dy = o_ref[:, hc], r_ref[:, hc], dy_ref[:, hc]
                sg = _sigmoid(rr)
                rs = lax.rsqrt(jnp.mean(o * o, axis=-1, keepdims=True) + EPS)
                on = o * rs
                t = dy * (rr * sg)
                sums_ref[1, :, hc] += _fold8(t * on)
                dn = t * gn_ref[...]
                dos.append((rs * (dn - on * jnp.mean(dn * on, axis=-1, keepdims=True))).astype(BF16))
                dr_ref[:, hc] = (dy * on * gn_ref[...] * (sg * (1.0 + rr * (1.0 - sg)))).astype(BF16)
                vs.append(v_ref[:, hc].astype(BF16))
            grow = [_tn(d0, q0) + _tn(d1, q1) for d0, q0, d1, q1 in zip(_chunks(dos[0]), _chunks(qms[0]), _chunks(dos[1]), _chunks(qms[1]))]
            pairs.append((bep, e_in, e_out, e_end, q, k, qms, kms_out, kms_end, vs, dos, grow))
        chains = []
        for p in range(2):
            bep, grow = pairs[p][0], pairs[p][-1]
            entering = [st_ref[ch, p] for ch in range(cb)]
            dst, leaving_grad = dstate[p], [None] * cb
            for ch in reversed(range(cb)):
                leaving_grad[ch] = dst
                dst = dst * jnp.exp(bep[ch * c:ch * c + 1, :]) + grow[ch]
            dstate[p] = dst
            chains.append((entering, leaving_grad))
        for p in range(2):
            cols = pl.ds(p * HEAD_LANES, HEAD_LANES)
            sl = slice(p * HEAD_LANES, (p + 1) * HEAD_LANES)
            _, e_in, e_out, e_end, q, k, qms, kms_out, kms_end, vs, dos, _ = pairs[p]
            entering, leaving_grad = chains[p]
            leaving = entering[1:] + [stn_ref[0, p]]
            felt = jnp.concatenate([jnp.broadcast_to(jnp.sum(dg_st * st, axis=0, keepdims=True), (c, HEAD_LANES))
                                    for dg_st, st in zip(leaving_grad, leaving)], axis=0)
            dq_in = jnp.zeros((tb, HEAD_LANES), F32)
            dk_out = jnp.zeros((tb, HEAD_LANES), F32)
            dk_end = jnp.zeros((tb, HEAD_LANES), F32)
            for e in range(2):
                hc = pl.ds((2 * p + e) * HEAD_LANES, HEAD_LANES)
                a = jnp.where(causal, _nt(qms[e], kms_out[e]), 0.0).astype(BF16)
                da = jnp.where(causal, _nt(dos[e], vs[e]), 0.0).astype(BF16)
                dv_ref[:, hc] = (_tn(a, dos[e]) + per_chunk(kms_end[e], leaving_grad, _nt)).astype(BF16)
                dq_in = dq_in + jnp.where(masks[e], per_chunk(dos[e], entering, _nn) + _nn(da, kms_out[e]), 0.0)
                dk_out = dk_out + _tn(da, qms[e])
                dk_end = dk_end + jnp.where(masks[e], per_chunk(vs[e], leaving_grad, _nn), 0.0)
            dq = dq_in * e_in
            dk = dk_out * e_out + dk_end * e_end
            dq_ref[:, cols] = (dq * 0.125).astype(BF16)
            dk_ref[:, cols] = dk.astype(BF16)
            dg = _sum_left(upper.astype(BF16), q * dq - k * dk) + felt
            dz = dg * (1.0 / GLA_GATE_TAU) * _sigmoid(-z[:, sl])
            dz_b = dz.astype(BF16)
            sums_ref[0, :, cols] += _fold8(dz)
            dlr = dlr + _nt(dz_b, wg_ref[:, cols])
            gwg_ref[:, cols] += _tn(lr_b, dz_b)
        dlr_ref[...] = dlr.astype(BF16)

        @pl.when(i == nblk - 1)
        def _():
            _spread_total(sums_ref)

    rev = lambda i: nblk - 1 - i

    def col(width, at):
        return pl.BlockSpec((tb, width), lambda i: (rev(i), at // width))

    full = lambda shape: pl.BlockSpec(shape, lambda i: tuple(0 for _ in shape))
    out_col = lambda width: pl.BlockSpec((tb, width), lambda i: (rev(i), 0))
    return pl.pallas_call(
        body, name=name, grid=(nblk,),
        in_specs=[col(256, C_GQ), col(256, C_GK), col(512, C_GV), col(512, C_GR), col(128, C_LR),
                  full((HEAD_LANES, 256)), full((1, 256)), full((1, HEAD_LANES)),
                  pl.BlockSpec((tb, 512), lambda i: (rev(i), 0)),
                  pl.BlockSpec((cb, 2, HEAD_LANES, HEAD_LANES), lambda i: (rev(i), 0, 0, 0)),
                  pl.BlockSpec((1, 2, HEAD_LANES, HEAD_LANES), lambda i: (jnp.minimum((rev(i) + 1) * cb, nch - 1), 0, 0, 0)),
                  pl.BlockSpec((tb, 512), lambda i: (rev(i), 0))],
        out_specs=[out_col(256), out_col(256), out_col(512), out_col(512), out_col(128),
                   full((HEAD_LANES, 256)), full((2, 8, 512))],
        out_shape=[jax.ShapeDtypeStruct((s, 256), BF16), jax.ShapeDtypeStruct((s, 256), BF16),
                   jax.ShapeDtypeStruct((s, 512), BF16), jax.ShapeDtypeStruct((s, 512), BF16),
                   jax.ShapeDtypeStruct((s, 128), BF16), jax.ShapeDtypeStruct((HEAD_LANES, 256), F32),
                   jax.ShapeDtypeStruct((2, 8, 512), F32)],
        scratch_shapes=[pltpu.VMEM((2, HEAD_LANES, HEAD_LANES), F32)],
        compiler_params=_params("arbitrary"))(proj, proj, proj, proj, proj, wg, bg, gn, o_raw, states, states, dmixed)


def _head_sums(v):
    ri = lax.broadcasted_iota(jnp.int32, (HEAD_LANES, HEAD_LANES), 0) // 64
    ci = lax.broadcasted_iota(jnp.int32, (HEAD_LANES, HEAD_LANES), 1) // 64
    ones = (ri == ci).astype(BF16)
    return jnp.concatenate([_sum_right(v[:, p * HEAD_LANES:(p + 1) * HEAD_LANES], ones) for p in range(4)], axis=1)


def _attn_prep(proj, qg, kg, *, name):
    s = proj.shape[0]
    tm = 2 * ROW_TILE

    def body(q_ref, k_ref, qg_ref, kg_ref, qa_ref, ka_ref):
        q, k = q_ref[...], k_ref[...]
        qr = lax.rsqrt(_head_sums(q * q) * (1.0 / 64) + EPS)
        kr = lax.rsqrt(_head_sums(k * k) * (1.0 / 64) + EPS)
        qa_ref[...] = q * qr * qg_ref[...] * 0.125
        ka_ref[...] = k * kr * kg_ref[...]

    col = lambda at: pl.BlockSpec((tm, 512), lambda i: (i, at // 512))
    vec = pl.BlockSpec((1, 512), lambda i: (0, 0))
    out = pl.BlockSpec((tm, 512), lambda i: (i, 0))
    return pl.pallas_call(
        body, name=name, grid=(s // tm,), in_specs=[col(C_AQ), col(C_AK), vec, vec], out_specs=[out] * 2,
        out_shape=[jax.ShapeDtypeStruct((s, 512), F32)] * 2, compiler_params=_params("parallel"))(proj, proj, qg, kg)


FAR = 1e30
LOG2E, LN2 = 1.4426950408889634, 0.6931471805599453


def _attn_distance(first):
    blk = ATTN_BLOCK
    iq = lax.broadcasted_iota(jnp.int32, (2 * blk, 2 * blk), 0) & (blk - 1)
    ik = lax.broadcasted_iota(jnp.int32, (2 * blk, 2 * blk), 1)
    rel = iq + blk - ik
    valid = (rel >= 0) & (rel <= blk) & (jnp.logical_not(first) | (ik >= blk))
    return jnp.where(valid, rel.astype(F32), FAR)


def _stack_heads(t2):
    low = lax.broadcasted_iota(jnp.int32, t2.shape, 1) < 64
    return jnp.concatenate([jnp.where(low, t2, 0.0), jnp.where(low, 0.0, t2)], axis=0).astype(BF16)


def _unstack_heads(t):
    blk = t.shape[0] // 2
    low = lax.broadcasted_iota(jnp.int32, (blk, HEAD_LANES), 1) < 64
    return jnp.where(low, t[0:blk], t[blk:2 * blk])


def _attn_scores(qs, kcat, slopes, dil, dist):
    top = lax.broadcasted_iota(jnp.int32, (2 * ATTN_BLOCK, 1), 0) < ATTN_BLOCK
    return _nt(qs, kcat) - jnp.where(top, slopes[0] * (dil * LOG2E), slopes[1] * (dil * LOG2E)) * dist


def _pair_slopes(p):
    if isinstance(p, int):
        return ALIBI_SLOPES[2 * p], ALIBI_SLOPES[2 * p + 1]
    pick = lambda e: jnp.where(p == 0, ALIBI_SLOPES[e], jnp.where(p == 1, ALIBI_SLOPES[2 + e],
                               jnp.where(p == 2, ALIBI_SLOPES[4 + e], ALIBI_SLOPES[6 + e])))
    return pick(0), pick(1)


ATTN_GROUP = 4


def _each(fn, *lists):
    return [fn(*args) for args in zip(*lists)]


BAND = 64


def _band_distance(first, h):
    iq = lax.broadcasted_iota(jnp.int32, (2 * BAND, BAND + ATTN_BLOCK), 0) & (BAND - 1)
    ik = lax.broadcasted_iota(jnp.int32, (2 * BAND, BAND + ATTN_BLOCK), 1)
    rel = iq + ATTN_BLOCK - ik
    valid = (rel >= 0) & (rel <= ATTN_BLOCK) & (jnp.logical_not(first) | (ik + h * BAND >= ATTN_BLOCK))
    return jnp.where(valid, rel.astype(F32), FAR)


def _attn_group_fwd(q2s, kcats, vcats, slopes, dil, first):
    dists = [_band_distance(first, h) for h in range(2)]
    top = lax.broadcasted_iota(jnp.int32, (2 * BAND, 1), 0) < BAND
    subs = [(i, h) for i in range(len(q2s)) for h in range(2)]
    keys = lambda t, h: t[h * BAND:h * BAND + BAND + ATTN_BLOCK]
    qs = [_stack_heads(q2s[i][h * BAND:(h + 1) * BAND] * LOG2E) for i, h in subs]
    sc = [_nt(q, keys(kcats[i], h)) - jnp.where(top, slopes[i][0] * (dil * LOG2E), slopes[i][1] * (dil * LOG2E)) * dists[h]
          for q, (i, h) in zip(qs, subs)]
    m = _each(lambda s: jnp.max(s, axis=-1, keepdims=True), sc)
    pr = _each(lambda s, mx: jnp.exp2(s - mx), sc, m)
    den = _each(lambda p: jnp.sum(p, axis=-1, keepdims=True), pr)
    o = [_nn(p.astype(BF16), keys(vcats[i], h)) / d for p, d, (i, h) in zip(pr, den, subs)]
    lse = _each(lambda mx, d, t: jnp.broadcast_to(mx + jnp.log2(d), t.shape), m, den, o)
    whole = lambda parts, i: jnp.concatenate([_unstack_heads(parts[2 * i]), _unstack_heads(parts[2 * i + 1])], axis=0)
    return [(whole(o, i), whole(lse, i)) for i in range(len(q2s))]


def _attn_group_bwd(q2s, kcats, vcats, do2s, y2s, lse2s, slopes, dil, dist):
    lane = lax.broadcasted_iota(jnp.int32, (ATTN_BLOCK, HEAD_LANES), 1)
    low = lane < 64
    per_head = lambda t, pick: jnp.concatenate([jnp.sum(jnp.where(pick(0), t, 0.0), axis=-1, keepdims=True),
                                                jnp.sum(jnp.where(pick(1), t, 0.0), axis=-1, keepdims=True)], axis=0)
    lse = _each(lambda l: per_head(l, lambda e: lane == 64 * e), lse2s)
    delta = _each(lambda d, y: per_head(d * y, lambda e: low if e == 0 else jnp.logical_not(low)), do2s, y2s)
    qs = _each(lambda q2: _stack_heads(q2 * LOG2E), q2s)
    dos = _each(_stack_heads, do2s)
    sc = _each(lambda q, k, sl: _attn_scores(q, k, sl, dil, dist), qs, kcats, slopes)
    pr = _each(lambda s, l: jnp.exp2(s - l), sc, lse)
    dp = _each(_nt, dos, vcats)
    ds = _each(lambda p, d, dl: (p * (d - dl)).astype(BF16), pr, dp, delta)
    dq = _each(lambda d, k: _unstack_heads(_nn(d, k)), ds, kcats)
    dk = _each(lambda d, q: _tn(d, q) * LN2, ds, qs)
    dv = _each(lambda p, d: _tn(p.astype(BF16), d), pr, dos)
    return list(zip(dq, dk, dv))


def _attn_specs(dil):
    rows = ATTN_BLOCK * dil
    if dil == 1:
        cur = lambda at: pl.BlockSpec((rows, 512), lambda n: (n, at // 512))
        prev = lambda at: pl.BlockSpec((rows, 512), lambda n: (jnp.maximum(n - 1, 0), at // 512))
    else:
        cur = lambda at: pl.BlockSpec((rows, HEAD_LANES), lambda n, p: (n, at // HEAD_LANES + p))
        prev = lambda at: pl.BlockSpec((rows, HEAD_LANES), lambda n, p: (jnp.maximum(n - 1, 0), at // HEAD_LANES + p))
    return cur, prev


def _attn_loop(dil, one_group, p):
    if dil == 1:
        one_group([(slice(None), pl.ds(p * HEAD_LANES, HEAD_LANES), p) for p in range(ATTN_GROUP)])
    else:
        group = min(dil, ATTN_GROUP)

        def step(g, carry):
            one_group([(pl.ds(g * group + j, ATTN_BLOCK, stride=dil), slice(None), p) for j in range(group)])
            return carry

        if dil == group:
            step(0, 0)
        else:
            lax.fori_loop(0, dil // group, step, 0)


def _dil_attn_fwd(qa, ka, proj, dil, *, name):
    s = qa.shape[0]

    def body(q_ref, kp_ref, kc_ref, vp_ref, vc_ref, o_ref, lse_ref):
        first = pl.program_id(0) == 0
        pair = None if dil == 1 else pl.program_id(1)

        def one_group(items):
            both = lambda a, b: [jnp.concatenate([a[rows, cols], b[rows, cols]], axis=0).astype(BF16) for rows, cols, _ in items]
            outs = _attn_group_fwd([q_ref[rows, cols] for rows, cols, _ in items], both(kp_ref, kc_ref), both(vp_ref, vc_ref),
                                   [_pair_slopes(p) for _, _, p in items], dil, first)
            for (rows, cols, _), (o2, lse2) in zip(items, outs):
                o_ref[rows, cols] = o2
                lse_ref[rows, cols] = lse2

        _attn_loop(dil, one_group, pair)

    cur, prev = _attn_specs(dil)
    grid = (s // ATTN_BLOCK,) if dil == 1 else (s // (ATTN_BLOCK * dil), 4)
    return pl.pallas_call(
        body, name=name, grid=grid, in_specs=[cur(0), prev(0), cur(0), prev(C_AV), cur(C_AV)], out_specs=[cur(0), cur(0)],
        out_shape=[jax.ShapeDtypeStruct((s, 512), F32)] * 2,
        compiler_params=_params(*["parallel"] * len(grid)))(qa, ka, ka, proj, proj)


def _dense_attn_fwd_merge(qa, ka, proj, others, y_gla, *, name):
    s = qa.shape[0]
    blk = ATTN_BLOCK

    def body(q_ref, kp_ref, kc_ref, vp_ref, vc_ref, oa_ref, la_ref, ob_ref, lb_ref, yg_ref, mixed_ref, y_ref, lse_ref):
        first = pl.program_id(0) == 0
        mixed_ref[:, 0:512] = yg_ref[...]

        def one_group(items):
            both = lambda a, b: [jnp.concatenate([a[rows, cols], b[rows, cols]], axis=0).astype(BF16) for rows, cols, _ in items]
            outs = _attn_group_fwd([q_ref[rows, cols] for rows, cols, _ in items], both(kp_ref, kc_ref), both(vp_ref, vc_ref),
                                   [_pair_slopes(p) for _, _, p in items], 1, first)
            for (_, cols, p), (o2, l2) in zip(items, outs):
                la, lb = la_ref[:, cols], lb_ref[:, cols]
                m = jnp.maximum(jnp.maximum(l2, la), lb)
                w0, wa, wb = jnp.exp2(l2 - m), jnp.exp2(la - m), jnp.exp2(lb - m)
                zs = w0 + wa + wb
                y = (w0 * o2 + wa * oa_ref[:, cols] + wb * ob_ref[:, cols]) / zs
                y_ref[:, cols] = y
                lse_ref[:, cols] = m + jnp.log2(zs)
                mixed_ref[:, pl.ds(512 + p * HEAD_LANES, HEAD_LANES)] = y.astype(BF16)

        _attn_loop(1, one_group, None)

    cur, prev = _attn_specs(1)
    here = pl.BlockSpec((blk, 512), lambda n: (n, 0))
    (oa, la), (ob, lb) = others
    return pl.pallas_call(
        body, name=name, grid=(s // blk,),
        in_specs=[cur(0), prev(0), cur(0), prev(C_AV), cur(C_AV)] + [here] * 5,
        out_specs=[pl.BlockSpec((blk, 1024), lambda n: (n, 0)), here, here],
        out_shape=[jax.ShapeDtypeStruct((s, 1024), BF16), jax.ShapeDtypeStruct((s, 512), F32),
                   jax.ShapeDtypeStruct((s, 512), F32)],
        compiler_params=_params("parallel"))(qa, ka, ka, proj, proj, oa, la, ob, lb, y_gla)


def _dil_attn_bwd(qa, ka, proj, y_att, lse, dmixed, dil, *, name):
    s = qa.shape[0]
    blk, rows_per_step = ATTN_BLOCK, ATTN_BLOCK * dil
    nb = s // rows_per_step
    step_axis = 0 if dil == 1 else 1

    def body(q_ref, kp_ref, kc_ref, vp_ref, vc_ref, y_ref, lse_ref, do_ref, dq_ref, dk_ref, dv_ref, dk_own, dv_own):
        n = pl.program_id(step_axis)
        pair = None if dil == 1 else pl.program_id(0)
        dist = _attn_distance(n == 0)

        @pl.when(n == 0)
        def _():
            dk_own[...] = jnp.zeros_like(dk_own)
            dv_own[...] = jnp.zeros_like(dv_own)

        def one_group(items):
            both = lambda a, b: [jnp.concatenate([a[rows, cols], b[rows, cols]], axis=0).astype(BF16) for rows, cols, _ in items]
            at = lambda ref: [ref[rows, cols] for rows, cols, _ in items]
            outs = _attn_group_bwd(at(q_ref), both(kp_ref, kc_ref), both(vp_ref, vc_ref), at(do_ref), at(y_ref), at(lse_ref),
                                   [_pair_slopes(p) for _, _, p in items], dil, dist)
            for (rows, cols, _), (dq, dk, dv) in zip(items, outs):
                dq_ref[rows, cols] = dq
                dk_ref[rows, cols] = dk_own[rows, cols] + dk[0:blk]
                dv_ref[rows, cols] = dv_own[rows, cols] + dv[0:blk]
                dk_own[rows, cols] = dk[blk:2 * blk]
                dv_own[rows, cols] = dv[blk:2 * blk]

        _attn_loop(dil, one_group, pair)

    width = 512 if dil == 1 else HEAD_LANES

    def spec(at, row_of):
        if dil == 1:
            return pl.BlockSpec((rows_per_step, width), lambda n: (row_of(n), at // width))
        return pl.BlockSpec((rows_per_step, width), lambda p, n: (row_of(n), at // width + p))

    cur = lambda at: spec(at, lambda n: n)
    prev = lambda at: spec(at, lambda n: jnp.maximum(n - 1, 0))
    own = spec(0, lambda n: 0)
    grid = (nb,) if dil == 1 else (4, nb)
    sems = ("arbitrary",) if dil == 1 else ("parallel", "arbitrary")
    dq, dk, dv, dk_last, dv_last = pl.pallas_call(
        body, name=name, grid=grid,
        in_specs=[cur(0), prev(0), cur(0), prev(C_AV), cur(C_AV), cur(0), cur(0), cur(512)],
        out_specs=[cur(0), prev(0), prev(0), own, own],
        out_shape=[jax.ShapeDtypeStruct((s, 512), F32)] * 3 + [jax.ShapeDtypeStruct((rows_per_step, 512), F32)] * 2,
        compiler_params=_params(*sems),
    )(qa, ka, ka, proj, proj, y_att, lse, dmixed)
    return dq, dk.at[s - rows_per_step:].set(dk_last), dv.at[s - rows_per_step:].set(dv_last)


def _attn_post(parts, proj, qg, kg, *, name):
    s = proj.shape[0]
    tm = ROW_TILE
    nblk = s // tm

    def body(*refs):
        ins, (q_ref, k_ref, qg_ref, kg_ref, dq_out, dk_out, dv_out, sums_ref) = refs[:9], refs[9:]
        i = pl.program_id(0)

        @pl.when(i == 0)
        def _():
            sums_ref[...] = jnp.zeros_like(sums_ref)

        dq = (ins[0][...] + ins[3][...]) + ins[6][...]
        dk = (ins[1][...] + ins[4][...]) + ins[7][...]
        dv = (ins[2][...] + ins[5][...]) + ins[8][...]
        dv_out[...] = dv.astype(BF16)
        for row, (x_ref, g_ref, dy, out, post) in enumerate(((q_ref, qg_ref, dq, dq_out, 0.125), (k_ref, kg_ref, dk, dk_out, 1.0))):
            x = x_ref[...]
            rs = lax.rsqrt(_head_sums(x * x) * (1.0 / 64) + EPS)
            xn = x * rs
            dy = dy * post
            sums_ref[row] += _fold8(dy * xn)
            dn = dy * g_ref[...]
            out[...] = (rs * (dn - xn * (_head_sums(dn * xn) * (1.0 / 64)))).astype(BF16)

        @pl.when(i == nblk - 1)
        def _():
            _spread_total(sums_ref)

    here = pl.BlockSpec((tm, 512), lambda i: (i, 0))
    col = lambda at: pl.BlockSpec((tm, 512), lambda i: (i, at // 512))
    vec = pl.BlockSpec((1, 512), lambda i: (0, 0))
    return pl.pallas_call(
        body, name=name, grid=(nblk,), in_specs=[here] * 9 + [col(C_AQ), col(C_AK), vec, vec],
        out_specs=[here, here, here, pl.BlockSpec((2, 8, 512), lambda i: (0, 0, 0))],
        out_shape=[jax.ShapeDtypeStruct((s, 512), BF16)] * 3 + [jax.ShapeDtypeStruct((2, 8, 512), F32)],
        compiler_params=_params("arbitrary"))(*[t for part in parts for t in part], proj, proj, qg, kg)


FFN_TM, FFN_TN = 512, 1408
HALO = 16


def _conv3(u_ref, halo_ref, w_ref, b_ref, first):
    u = u_ref[...].astype(F32)
    ext = jnp.concatenate([jnp.where(first, 0.0, halo_ref[...].astype(F32)), u], axis=0)
    u1 = pltpu.roll(ext, 1, 0)[HALO:]
    u2 = pltpu.roll(ext, 2, 0)[HALO:]
    return b_ref[...] + w_ref[0:1, :] * u2 + w_ref[1:2, :] * u1 + w_ref[2:3, :] * u


def _ffn_specs(tm, tn):
    nj = D_FF // tn
    blk = lambda half: pl.BlockSpec((tm, tn), lambda j, i: (i, j + half * nj))
    halo = lambda half: pl.BlockSpec((HALO, tn), lambda j, i: (jnp.maximum(i * (tm // HALO) - 1, 0), j + half * nj))
    wspec = lambda half: pl.BlockSpec((3, tn), lambda j, i: (0, j + half * nj))
    bspec = lambda half: pl.BlockSpec((1, tn), lambda j, i: (0, j + half * nj))
    return [blk(0), halo(0), blk(1), halo(1), wspec(0), wspec(1), bspec(0), bspec(1)]


def _conv_swiglu_fwd(u, conv_w, conv_b, *, name):
    s = u.shape[0]
    tm, tn = FFN_TM, FFN_TN

    def body(ug_ref, hg_ref, uv_ref, hv_ref, wg_ref, wv_ref, bg_ref, bv_ref, act_ref, uc_ref):
        first = pl.program_id(1) == 0
        cg = _conv3(ug_ref, hg_ref, wg_ref, bg_ref, first)
        cv = _conv3(uv_ref, hv_ref, wv_ref, bv_ref, first)
        act_ref[...] = (cg * _sigmoid(cg) * cv).astype(BF16)
        uc_ref[0] = cg.astype(BF16)
        uc_ref[1] = cv.astype(BF16)

    return pl.pallas_call(
        body, name=name, grid=(D_FF // tn, s // tm), in_specs=_ffn_specs(tm, tn),
        out_specs=[pl.BlockSpec((tm, tn), lambda j, i: (i, j)), pl.BlockSpec((2, tm, tn), lambda j, i: (0, i, j))],
        out_shape=[jax.ShapeDtypeStruct((s, D_FF), BF16), jax.ShapeDtypeStruct((2, s, D_FF), BF16)],
        compiler_params=_params("parallel", "parallel"))(u, u, u, u, conv_w, conv_w, conv_b, conv_b)


def _swiglu_bwd(uc, dact, *, name):
    _, s, _ = uc.shape
    tm, tn = FFN_TM, FFN_TN

    def body(uc_ref, da_ref, duc_ref, sums_ref):
        i = pl.program_id(1)

        @pl.when(i == 0)
        def _():
            sums_ref[...] = jnp.zeros_like(sums_ref)

        cg, cv, da = uc_ref[0].astype(F32), uc_ref[1].astype(F32), da_ref[...].astype(F32)
        sg = _sigmoid(cg)
        dg = da * cv * (sg * (1.0 + cg * (1.0 - sg)))
        dv = da * (cg * sg)
        duc_ref[0] = dg.astype(BF16)
        duc_ref[1] = dv.astype(BF16)
        sums_ref[0] += _fold8(dg)
        sums_ref[1] += _fold8(dv)

        @pl.when(i == s // tm - 1)
        def _():
            _spread_total(sums_ref)

    pair = pl.BlockSpec((2, tm, tn), lambda j, i: (0, i, j))
    return pl.pallas_call(
        body, name=name, grid=(D_FF // tn, s // tm), in_specs=[pair, pl.BlockSpec((tm, tn), lambda j, i: (i, j))],
        out_specs=[pair, pl.BlockSpec((2, 8, tn), lambda j, i: (0, 0, j))],
        out_shape=[jax.ShapeDtypeStruct((2, s, D_FF), BF16), jax.ShapeDtypeStruct((2, 8, D_FF), F32)],
        compiler_params=_params("parallel", "arbitrary"))(uc, dact)


def _conv_bwd(duc, u, conv_w, *, name):
    _, s, _ = duc.shape
    tm, tn = FFN_TM, FFN_TN
    nj, ni = D_FF // tn, s // tm

    def body(d_ref, halo_ref, u_ref, w_ref, du_ref, sums_ref):
        i = pl.program_id(2)

        @pl.when(i == 0)
        def _():
            sums_ref[...] = jnp.zeros_like(sums_ref)

        d = d_ref[0].astype(F32)
        ext = jnp.concatenate([d, jnp.where(i == ni - 1, 0.0, halo_ref[0].astype(F32))], axis=0)
        n = tm + HALO
        d1 = pltpu.roll(ext, n - 1, 0)[:tm]
        d2 = pltpu.roll(ext, n - 2, 0)[:tm]
        du_ref[...] = (w_ref[2:3, :] * d + w_ref[1:2, :] * d1 + w_ref[0:1, :] * d2).astype(BF16)
        uv = u_ref[...].astype(F32)
        for t, shifted in enumerate((d2, d1, d)):
            sums_ref[0, t] += _fold8(shifted * uv)

        @pl.when(i == ni - 1)
        def _():
            _spread_total(sums_ref)

    return pl.pallas_call(
        body, name=name, grid=(2, nj, ni),
        in_specs=[pl.BlockSpec((1, tm, tn), lambda g, j, i: (g, i, j)),
                  pl.BlockSpec((1, HALO, tn), lambda g, j, i: (g, jnp.minimum((i + 1) * (tm // HALO), s // HALO - 1), j)),
                  pl.BlockSpec((tm, tn), lambda g, j, i: (i, g * nj + j)),
                  pl.BlockSpec((3, tn), lambda g, j, i: (0, g * nj + j))],
        out_specs=[pl.BlockSpec((tm, tn), lambda g, j, i: (i, g * nj + j)),
                   pl.BlockSpec((1, 3, 8, tn), lambda g, j, i: (g, 0, 0, j))],
        out_shape=[jax.ShapeDtypeStruct((s, 2 * D_FF), BF16), jax.ShapeDtypeStruct((2, 3, 8, D_FF), F32)],
        compiler_params=_params("parallel", "parallel", "arbitrary"))(duc, duc, u, conv_w)


def _loss_head(x1, ffn, gate, target, *, name):
    s, d = x1.shape
    tm = ROW_TILE

    def body(x_ref, f_ref, g_ref, t_ref, dy_ref, df_ref, sums_ref):
        i = pl.program_id(0)

        @pl.when(i == 0)
        def _():
            sums_ref[...] = jnp.zeros_like(sums_ref)

        f = f_ref[...]
        err = x_ref[...] + g_ref[...] * f - t_ref[...]
        dy = err * (1.0 / d)
        dy_ref[...] = dy
        df_ref[...] = (g_ref[...] * dy).astype(BF16)
        sums_ref[0] += _fold8(dy * f)
        sums_ref[1] += _fold8(err * err)

        @pl.when(i == s // tm - 1)
        def _():
            _spread_total(sums_ref)

    row = pl.BlockSpec((tm, d), lambda i: (i, 0))
    return pl.pallas_call(
        body, name=name, grid=(s // tm,), in_specs=[row, row, pl.BlockSpec((1, d), lambda i: (0, 0)), row],
        out_specs=[row, row, pl.BlockSpec((2, 8, d), lambda i: (0, 0, 0))],
        out_shape=[jax.ShapeDtypeStruct((s, d), F32), jax.ShapeDtypeStruct((s, d), BF16), jax.ShapeDtypeStruct((2, 8, d), F32)],
        compiler_params=_params("arbitrary"))(x1, ffn, gate, target)


def _adamw(w, g, m, v, *, name):
    rows, cols = w.shape
    split = isinstance(g, tuple)
    if rows % 8 == 0 or rows <= ADAM_TILE:
        span = rows // 2 if split else rows
        tm = next((t for t in range(ADAM_TILE, 7, -8) if span % t == 0), span)
        shape, at, steps, per_half = (tm, cols), (lambda i: (i, 0)), rows // tm, span // tm
    else:
        shape, at, steps, per_half = (rows, ADAM_TILE), (lambda i: (0, i)), cols // ADAM_TILE, cols // ADAM_TILE // 2

    def update(gv, w_ref, m_ref, v_ref, d_ref, mo_ref, vo_ref):
        mn = ADAM_B1 * m_ref[...] + (1.0 - ADAM_B1) * gv
        vn = ADAM_B2 * v_ref[...] + (1.0 - ADAM_B2) * (gv * gv)
        m_hat = mn / (1.0 - ADAM_B1 ** ADAM_STEP)
        v_hat = vn / (1.0 - ADAM_B2 ** ADAM_STEP)
        d_ref[...] = -ADAM_LR * (m_hat / (jnp.sqrt(v_hat) + ADAM_EPS) + ADAM_WD * w_ref[...])
        mo_ref[...] = mn
        vo_ref[...] = vn

    out_shape = [jax.ShapeDtypeStruct((rows, cols), F32)] * (4 if split else 3)
    if not split:
        def body(w_ref, g_ref, m_ref, v_ref, d_ref, mo_ref, vo_ref):
            update(g_ref[...], w_ref, m_ref, v_ref, d_ref, mo_ref, vo_ref)

        blk = pl.BlockSpec(shape, at)
        return pl.pallas_call(body, name=name, grid=(steps,), in_specs=[blk] * 4, out_specs=[blk] * 3, out_shape=out_shape,
                              compiler_params=_params("parallel"))(w, g, m, v)

    mine, other, core = g

    def body(core_ref, w_ref, mine_ref, other_ref, m_ref, v_ref, d_ref, mo_ref, vo_ref, g_ref):
        gv = jnp.where(pl.program_id(0) // per_half == core_ref[0], mine_ref[...], other_ref[...])
        g_ref[...] = gv
        update(gv, w_ref, m_ref, v_ref, d_ref, mo_ref, vo_ref)

    blk = pl.BlockSpec(shape, lambda i, core_ref: at(i))
    half = pl.BlockSpec(shape, lambda i, core_ref: at(i % per_half))
    return pl.pallas_call(
        body, name=name, out_shape=out_shape, compiler_params=_params("parallel"),
        grid_spec=pltpu.PrefetchScalarGridSpec(num_scalar_prefetch=1, grid=(steps,), in_specs=[blk, half, half, blk, blk],
                                               out_specs=[blk] * 4))(core, w, mine, other, m, v)


def _colsum(t):
    return t[..., 0, :]


def _in_proj_layout(w_in):
    pad = jnp.zeros((w_in.shape[0], PROJ_W - C_LR - GLA_GATE_RANK), w_in.dtype)
    return jnp.concatenate([w_in[:, :1536], w_in[:, 1552:], w_in[:, 1536:1552], pad], axis=1)


def _in_proj_grad_layout(g):
    return jnp.concatenate([g[:, :1536], g[:, C_LR:C_LR + GLA_GATE_RANK], g[:, 1536:C_LR]], axis=1)


def _gate_layout(gla_w_gate):
    return jnp.pad(gla_w_gate, ((0, HEAD_LANES - GLA_GATE_RANK), (0, 0))).astype(BF16)


def _local_step(x, target, mod, wi, wo, ffn_weights, ffn_grads_ready, attn_grads_ready, conv_w, conv_b, wg, bg, gn, qg, kg, n1g, n2g):
    d = D_MODEL
    sh1, sc1, g1, sh2, sc2, g2 = [mod[:, i * d:(i + 1) * d] for i in range(6)]
    qg8, kg8 = jnp.tile(qg, (1, 8)), jnp.tile(kg, (1, 8))

    _, h1, h1_t = _norm_mod_fwd(x, None, None, n1g, sc1, sh1, name="norm1_fwd")
    proj = _mm(h1, wi, tm=1024, tn=PROJ_W, tk=d, name="in_proj")
    o_raw, y_gla, states = _gla_fwd(proj, wg, bg, gn, name="gla_fwd")
    qa, ka = _attn_prep(proj, qg8, kg8, name="attn_prep")
    sparse = [_dil_attn_fwd(qa, ka, proj, dil, name=f"attn_fwd_d{dil}") for dil in DILATIONS[1:]]
    mixed, y_att, lse = _dense_attn_fwd_merge(qa, ka, proj, sparse, y_gla, name="attn_fwd_d1_merge")
    attn_out = _mm(mixed, wo, tm=1024, tn=d, tk=d, name="out_proj")
    x1, h2, h2_t = _norm_mod_fwd(x, attn_out, g1, n2g, sc2, sh2, name="norm2_fwd")
    wup, wdown = ffn_weights(h2)
    u = _mm(h2, wup, out_dtype=BF16, tm=1024, tn=D_FF, tk=d, name="up_proj")
    act, uc = _conv_swiglu_fwd(u, conv_w, conv_b, name="conv_swiglu_fwd")
    ffn = _mm(act, wdown, tm=1024, tn=d, tk=D_FF, name="down_proj")
    dy, dffn, head_sums = _loss_head(x1, ffn, g2, target, name="loss_head")

    dact = _mm(dffn, wdown, tb=True, out_dtype=BF16, tm=1024, tn=D_FF, tk=d, name="down_proj_dx")
    g_wdown, g_wdown_b = _mm(act, dffn, ta=True, tm=1408, tn=d, tk=2048, also_bf16=True, name="down_proj_dw")
    duc, bias_sums = _swiglu_bwd(uc, dact, name="swiglu_bwd")
    du, tap_sums = _conv_bwd(duc, u, conv_w, name="conv_bwd")
    dh2 = _mm(du, wup, tb=True, tm=1024, tn=d, tk=D_FF, name="up_proj_dx")
    g_wup, g_wup_b = _mm(h2_t, du, tm=d, tn=1408, tk=2048, shard_cols=True, also_bf16=True, name="up_proj_dw")
    token = ffn_grads_ready(g_wup_b, g_wdown_b)
    g1_late = g1 if token is None else g1 + token[0:1, 0:1]
    dx1, dao, n2_sums = _norm_mod_bwd(x1, dh2, dy, n2g, sc2, attn_out, g1_late, name="norm2_bwd")

    dmixed = _mm(dao, wo, tb=True, tm=1024, tn=d, tk=d, name="out_proj_dx")
    g_wo = _mm(mixed, dao, ta=True, tm=d, tn=d, tk=1024, name="out_proj_dw")
    dgq, dgk, dgv, dgr, dlr, g_wg, gla_sums = _gla_bwd(proj, wg, bg, gn, o_raw, states, dmixed, name="gla_bwd")
    parts = [_dil_attn_bwd(qa, ka, proj, y_att, lse, dmixed, dil, name=f"attn_bwd_d{dil}") for dil in DILATIONS]
    daq, dak, dav, qk_sums = _attn_post(parts, proj, qg8, kg8, name="attn_post")
    dproj = jnp.concatenate([dgq, dgk, dgv, dgr, daq, dak, dav, dlr], axis=1)
    g_wi = _mm(h1_t, dproj, tm=512, tn=PROJ_W, tk=2048, name="in_proj_dw")
    token = attn_grads_ready(g_wi, g_wo)
    wi_late = wi if token is None else wi + token[0:1, 0:1].astype(BF16)
    dh1 = _mm(dproj, wi_late, tb=True, tm=1024, tn=d, tk=PROJ_W, name="in_proj_dx")
    grad_x, _, n1_sums = _norm_mod_bwd(x, dh1, dx1, n1g, sc1, None, None, name="norm1_bwd")

    n1, n2, hs, taps, cb = _colsum(n1_sums), _colsum(n2_sums), _colsum(head_sums), _colsum(tap_sums), _colsum(bias_sums)
    gs, qs = _colsum(gla_sums), _colsum(qk_sums)
    dmod = jnp.concatenate([n1[1], n1[0] * n1g[0], n2[2], n2[1], n2[0] * n2g[0], hs[0]])
    small = dict(
        dmod=dmod,
        norm1_g=n1[0] * (1.0 + sc1[0]), norm2_g=n2[0] * (1.0 + sc2[0]),
        gla_w_gate=g_wg[:GLA_GATE_RANK], gla_b_gate=gs[0, :256], gla_norm_g=gs[1].reshape(4, 128).sum(axis=0),
        q_norm_g=qs[0].reshape(8, 64).sum(axis=0), k_norm_g=qs[1].reshape(8, 64).sum(axis=0),
        conv_w=jnp.concatenate([taps[0], taps[1]], axis=1), conv_b=jnp.concatenate([cb[0], cb[1]]),
    )
    return head_sums[1], grad_x, (g_wi, g_wo, g_wup, g_wdown), small


N_DEV, N_CHIP = 8, 4
ANY = pl.BlockSpec(memory_space=pl.ANY)
VMEM_SPEC = pl.BlockSpec(memory_space=pltpu.VMEM)


def _place():
    x, y, c = lax.axis_index("x"), lax.axis_index("y"), lax.axis_index("c")
    other_chips = [(1 - x, y), (x, 1 - y), (1 - x, 1 - y)]
    return x, y, c, (x, y, 1 - c), other_chips


def _all_gather_small(v, *, name):
    m, n = v.shape

    def body(v_ref, out_ref, send_sems, recv_sems, local_sem):
        x, y, c, sibling, chips = _place()
        me = (x, y, c)

        def rows(px, py, pc):
            return out_ref.at[pl.ds((4 * px + 2 * py + pc) * m, m), :]

        def copy(k, block, to, src=None):
            return pltpu.make_async_remote_copy(
                src_ref=rows(*block) if src is None else src, dst_ref=rows(*block), send_sem=send_sems.at[k],
                recv_sem=recv_sems.at[k], device_id=to, device_id_type=MESH)

        mine = pltpu.make_async_copy(v_ref, rows(*me), local_sem)
        mine.start()
        first = [copy(0, me, sibling, src=v_ref)]
        first += [copy(1 + j, me, (*chip, c), src=v_ref) for j, chip in enumerate(chips)]
        for cp in first:
            cp.start()
        passed = [copy(4 + j, (*chip, c), sibling) for j, chip in enumerate(chips)]
        for j, chip in enumerate(chips):
            copy(1 + j, (*chip, c), me).wait_recv()
            passed[j].start()
        copy(0, sibling, me).wait_recv()
        for j, chip in enumerate(chips):
            copy(4 + j, (*chip, 1 - c), me).wait_recv()
        for cp in first + passed:
            cp.wait_send()
        mine.wait()

    return pl.pallas_call(
        body, name=name, out_shape=jax.ShapeDtypeStruct((N_DEV * m, n), v.dtype), in_specs=[VMEM_SPEC], out_specs=VMEM_SPEC,
        scratch_shapes=[pltpu.SemaphoreType.DMA((7,)), pltpu.SemaphoreType.DMA((7,)), pltpu.SemaphoreType.DMA],
    )(v)


def _gather_weight_shards(shards, *, name):
    nw = len(shards)

    def body(*refs):
        srcs, outs, (send_sems, recv_sems) = refs[:nw], refs[nw:2 * nw], refs[2 * nw:]
        x, y, c, sibling, chips = _place()
        index = lambda chip: 2 * chip[0] + chip[1]

        def copy(w, k, src, dst, to):
            return pltpu.make_async_remote_copy(src_ref=src, dst_ref=dst, send_sem=send_sems.at[6 * w + k],
                                                recv_sem=recv_sems.at[6 * w + k], device_id=to, device_id_type=MESH)

        sent = []
        for w, (src_ref, out_ref) in enumerate(zip(srcs, outs)):
            for k, chip in enumerate(chips):
                sent.append(copy(w, k, src_ref.at[c], out_ref.at[2 * x + y, c], (*chip, c)))
                sent[-1].start()
        for w, out_ref in enumerate(outs):
            for k, chip in enumerate(chips):
                landed = out_ref.at[index(chip), c]
                copy(w, k, landed, landed, (*chip, c)).wait_recv()
                sent.append(copy(w, 3 + k, landed, landed, sibling))
                sent[-1].start()
        for w, out_ref in enumerate(outs):
            for k, chip in enumerate(chips):
                passed_on = out_ref.at[index(chip), 1 - c]
                copy(w, 3 + k, passed_on, passed_on, sibling).wait_recv()
        for cp in sent:
            cp.wait_send()

    return pl.pallas_call(
        body, name=name, out_shape=[jax.ShapeDtypeStruct((N_CHIP, *s.shape), s.dtype) for s in shards],
        in_specs=[ANY] * nw, out_specs=[ANY] * nw,
        scratch_shapes=[pltpu.SemaphoreType.DMA((6 * nw,)), pltpu.SemaphoreType.DMA((6 * nw,))],
    )(*shards)


HBM_SPEC = pl.BlockSpec(memory_space=pltpu.HBM)
SEM_SPEC = pl.BlockSpec(memory_space=pltpu.SEMAPHORE)
DATAFLOW_EFFECT = pltpu.SideEffectType.DATAFLOW_SIDE_EFFECTING


def _late_copies(srcs, lands, send_sems, recv_sems):
    x, y, c, _, chips = _place()
    return [pltpu.make_async_remote_copy(
        src_ref=src.at[c], dst_ref=land.at[2 * x + y, c], send_sem=send_sems.at[6 * w + 2 * r + core],
        recv_sem=recv_sems.at[6 * w + 2 * r + c], device_id=(*chip, core), device_id_type=MESH)
        for w, (src, land) in enumerate(zip(srcs, lands)) for r, chip in enumerate(chips) for core in range(2)]


def _gather_late_start(own, after, *, name):
    nw = len(own)

    def body(*refs):
        srcs, lands, send_sems, recv_sems, token = refs[:nw], refs[nw:2 * nw], refs[2 * nw + 1], refs[2 * nw + 2], refs[-1]
        for cp in _late_copies(srcs, lands, send_sems, recv_sems):
            cp.start()
        token[...] = jnp.zeros_like(token)

    lands = [pltpu.with_memory_space_constraint(lax.empty((N_CHIP, *s.shape), s.dtype), pltpu.HBM) for s in own]
    own = [pltpu.with_memory_space_constraint(s, pltpu.HBM) for s in own]
    out = pl.pallas_call(
        body, name=name,
        out_shape=(pltpu.SemaphoreType.DMA((6 * nw,)), pltpu.SemaphoreType.DMA((6 * nw,)),
                   *[pltpu.HBM(s.shape, s.dtype) for s in own], *[pltpu.HBM(s.shape, s.dtype) for s in lands],
                   jax.ShapeDtypeStruct((8, 128), F32)),
        in_specs=[HBM_SPEC] * (2 * nw) + [ANY], out_specs=(SEM_SPEC, SEM_SPEC, *[HBM_SPEC] * (2 * nw), VMEM_SPEC),
        input_output_aliases={i: 2 + i for i in range(2 * nw)},
        compiler_params=pltpu.CompilerParams(has_side_effects=DATAFLOW_EFFECT))(*own, *lands, after)
    return out[0], out[1], out[2:2 + nw], out[2 + nw:2 + 2 * nw], out[-1]


def _gather_late_wait(send_sems, recv_sems, own, lands, after, *, name):
    nw = len(own)

    def body(*refs):
        srcs, lands_in, send_sems, recv_sems = refs[:nw], refs[nw:2 * nw], refs[2 * nw], refs[2 * nw + 1]
        x, y, c, _, chips = _place()
        for cp in _late_copies(srcs, lands_in, send_sems, recv_sems):
            cp.wait_send()
        for w, (src, land) in enumerate(zip(srcs, lands_in)):
            for r, chip in enumerate(chips):
                for core in range(2):
                    pltpu.make_async_remote_copy(
                        src_ref=src.at[c], dst_ref=land.at[2 * chip[0] + chip[1], core], send_sem=send_sems.at[6 * w + 2 * r + core],
                        recv_sem=recv_sems.at[6 * w + 2 * r + core], device_id=(*chip, core), device_id_type=MESH).wait_recv()

    out = pl.pallas_call(
        body, name=name, out_shape=(*[pltpu.HBM(s.shape, s.dtype) for s in own], *[pltpu.HBM(s.shape, s.dtype) for s in lands]),
        in_specs=[HBM_SPEC] * (2 * nw) + [SEM_SPEC, SEM_SPEC, ANY], out_specs=tuple([HBM_SPEC] * (2 * nw)),
        input_output_aliases={i: i for i in range(2 * nw)},
        compiler_params=pltpu.CompilerParams(has_side_effects=DATAFLOW_EFFECT))(*own, *lands, send_sems, recv_sems, after)
    return out[:nw], out[nw:]


def _direct_reduce_copies(srcs, lands, send_sems, recv_sems):
    x, y, c, _, _ = _place()
    cps = []
    for w, (src, land) in enumerate(zip(srcs, lands)):
        for rel in range(1, N_DEV):
            tx, ty, tc = (1 - x if rel & 4 else x), (1 - y if rel & 2 else y), (1 - c if rel & 1 else c)
            cps.append(pltpu.make_async_remote_copy(
                src_ref=src.at[2 * tx + ty, tc], dst_ref=land.at[rel - 1], send_sem=send_sems.at[7 * w + rel - 1],
                recv_sem=recv_sems.at[7 * w + rel - 1], device_id=(tx, ty, tc), device_id_type=MESH))
    return cps


def _direct_reduce_start(grads, *, name):
    nw = len(grads)

    def body(*refs):
        srcs, lands, send_sems, recv_sems, token = refs[:nw], refs[nw:2 * nw], refs[2 * nw], refs[2 * nw + 1], refs[-1]
        for cp in _direct_reduce_copies(srcs, lands, send_sems, recv_sems):
            cp.start()
        token[...] = jnp.zeros_like(token)

    lands = [pltpu.with_memory_space_constraint(lax.empty((N_DEV - 1, *g.shape[2:]), g.dtype), pltpu.HBM) for g in grads]
    grads = [pltpu.with_memory_space_constraint(g, pltpu.HBM) for g in grads]
    out = pl.pallas_call(
        body, name=name,
        out_shape=(pltpu.SemaphoreType.DMA((7 * nw,)), pltpu.SemaphoreType.DMA((7 * nw,)),
                   *[pltpu.HBM(g.shape, g.dtype) for g in grads], *[pltpu.HBM(t.shape, t.dtype) for t in lands],
                   jax.ShapeDtypeStruct((8, 128), F32)),
        in_specs=[HBM_SPEC] * (2 * nw), out_specs=(SEM_SPEC, SEM_SPEC, *[HBM_SPEC] * (2 * nw), VMEM_SPEC),
        input_output_aliases={i: 2 + i for i in range(2 * nw)},
        compiler_params=pltpu.CompilerParams(has_side_effects=DATAFLOW_EFFECT))(*grads, *lands)
    return out[0], out[1], out[2:2 + nw], out[2 + nw:2 + 2 * nw], out[-1]


def _direct_reduce_wait(send_sems, recv_sems, grads, lands, after, *, name):
    nw = len(grads)

    def body(*refs):
        srcs, lands_in, send_sems, recv_sems = refs[:nw], refs[nw:2 * nw], refs[2 * nw], refs[2 * nw + 1]
        cps = _direct_reduce_copies(srcs, lands_in, send_sems, recv_sems)
        for cp in cps:
            cp.wait_send()
        for cp in cps:
            cp.wait_recv()

    out = pl.pallas_call(
        body, name=name, out_shape=(*[pltpu.HBM(g.shape, g.dtype) for g in grads], *[pltpu.HBM(t.shape, t.dtype) for t in lands]),
        in_specs=[HBM_SPEC] * (2 * nw) + [SEM_SPEC, SEM_SPEC, ANY], out_specs=tuple([HBM_SPEC] * (2 * nw)),
        input_output_aliases={i: i for i in range(2 * nw)},
        compiler_params=pltpu.CompilerParams(has_side_effects=DATAFLOW_EFFECT))(*grads, *lands, send_sems, recv_sems, after)
    return out[nw:]


def _direct_reduce_add(grad, landed, chip, core, *, name):
    _, r, n = grad.shape
    half = r // 2
    tr = _row_tile(half)
    nb = half // tr

    def body(chip_ref, core_ref, g_ref, t_ref, o_ref):
        acc = g_ref[0]
        for k in range(N_DEV - 1):
            acc = acc + t_ref[k].astype(F32)
        o_ref[...] = acc

    return pl.pallas_call(
        body, name=name,
        grid_spec=pltpu.PrefetchScalarGridSpec(
            num_scalar_prefetch=2, grid=(nb,),
            in_specs=[pl.BlockSpec((1, tr, n), lambda i, chip_ref, core_ref: (chip_ref[0], core_ref[0] * nb + i, 0)),
                      pl.BlockSpec((N_DEV - 1, tr, n), lambda i, chip_ref, core_ref: (0, i, 0))],
            out_specs=pl.BlockSpec((tr, n), lambda i, chip_ref, core_ref: (i, 0))),
        out_shape=jax.ShapeDtypeStruct((half, n), F32), compiler_params=_params("parallel"))(chip, core, grad, landed)


def _share_halves(halves, *, name):
    nw = len(halves)

    def body(*refs):
        srcs, outs, (send_sems, recv_sems) = refs[:nw], refs[nw:2 * nw], refs[2 * nw:]
        _, _, _, sibling, _ = _place()
        cps = [pltpu.make_async_remote_copy(src_ref=src_ref, dst_ref=out_ref, send_sem=send_sems.at[w], recv_sem=recv_sems.at[w],
                                            device_id=sibling, device_id_type=MESH)
               for w, (src_ref, out_ref) in enumerate(zip(srcs, outs))]
        for cp in cps:
            cp.start()
        for cp in cps:
            cp.wait()

    return pl.pallas_call(
        body, name=name, out_shape=[jax.ShapeDtypeStruct(h.shape, h.dtype) for h in halves],
        in_specs=[ANY] * nw, out_specs=[ANY] * nw,
        scratch_shapes=[pltpu.SemaphoreType.DMA((nw,)), pltpu.SemaphoreType.DMA((nw,))])(*halves)


def _row_tile(rows, limit=256):
    return next(t for t in range(limit, 15, -16) if rows % t == 0)


def _sum_devices(gathered, *, name):
    _, m, n = gathered.shape

    def body(g_ref, tot_ref, loss_ref):
        tot = g_ref[0]
        for dev in range(1, N_DEV):
            tot = tot + g_ref[dev]
        tot_ref[...] = tot
        loss_ref[...] = jnp.full((8, n), (0.5 / D_MODEL) * jnp.sum(tot[0:8]), F32)

    return pl.pallas_call(body, name=name, in_specs=[VMEM_SPEC], out_specs=[VMEM_SPEC, VMEM_SPEC],
                          out_shape=[jax.ShapeDtypeStruct((m, n), F32), jax.ShapeDtypeStruct((8, n), F32)])(gathered)


def _ada_mod(cond_all, w_ada_shard, *, name):
    tn = 512

    def body(a_ref, b_ref, o_ref):
        o_ref[...] = _nn(a_ref[...], b_ref[...], precision=HIGHEST)

    return pl.pallas_call(
        body, name=name, grid=(w_ada_shard.shape[1] // tn,),
        in_specs=[pl.BlockSpec(cond_all.shape, lambda j: (0, 0)), pl.BlockSpec((D_MODEL, tn), lambda j: (0, j))],
        out_specs=pl.BlockSpec((N_DEV, tn), lambda j: (0, j)),
        out_shape=jax.ShapeDtypeStruct((N_DEV, w_ada_shard.shape[1]), F32), compiler_params=_params("parallel"))(cond_all, w_ada_shard)


def _ada_grad(cond_all, dmod_cols, *, name):
    tm = 256

    def body(a_ref, b_ref, o_ref):
        o_ref[...] = lax.dot_general(a_ref[...], b_ref[...], (((0,), (0,)), ((), ())), precision=HIGHEST,
                                     preferred_element_type=F32)

    return pl.pallas_call(
        body, name=name, grid=(D_MODEL // tm,),
        in_specs=[pl.BlockSpec((N_DEV, tm), lambda i: (0, i)), pl.BlockSpec(dmod_cols.shape, lambda i: (0, 0))],
        out_specs=pl.BlockSpec((tm, dmod_cols.shape[1]), lambda i: (i, 0)),
        out_shape=jax.ShapeDtypeStruct((D_MODEL, dmod_cols.shape[1]), F32), compiler_params=_params("parallel"))(cond_all, dmod_cols)


def _silu_rows(c8, *, name):
    def body(c_ref, o_ref):
        cv = c_ref[...]
        o_ref[...] = cv * _sigmoid(cv)

    return pl.pallas_call(body, name=name, in_specs=[VMEM_SPEC], out_specs=VMEM_SPEC,
                          out_shape=jax.ShapeDtypeStruct(c8.shape, F32))(c8)


def _rows128(t, rows=None):
    flat = t.reshape(-1, 128)
    return flat if rows is None else jnp.pad(flat, ((0, rows - flat.shape[0]), (0, 0)))


def _from_col_shards(shards, r, n):
    return shards.reshape(N_CHIP, r, n).transpose(1, 0, 2).reshape(r, N_CHIP * n)


def kernel(x, c, w_ada, b_ada, norm1_g, w_in, gla_w_gate, gla_b_gate, gla_norm_g, q_norm_g, k_norm_g, w_out, norm2_g, w_up, conv_w, conv_b, w_down, loss_target, m_w_ada, m_b_ada, m_norm1_g, m_w_in, m_gla_w_gate, m_gla_b_gate, m_gla_norm_g, m_q_norm_g, m_k_norm_g, m_w_out, m_norm2_g, m_w_up, m_conv_w, m_conv_b, m_w_down, v_w_ada, v_b_ada, v_norm1_g, v_w_in, v_gla_w_gate, v_gla_b_gate, v_gla_norm_g, v_q_norm_g, v_k_norm_g, v_w_out, v_norm2_g, v_w_up, v_conv_w, v_conv_b, v_w_down):
    d = D_MODEL
    ax, ay, ac = lax.axis_index("x"), lax.axis_index("y"), lax.axis_index("c")
    chip, dev = 2 * ax + ay, 4 * ax + 2 * ay + ac

    cond = _silu_rows(jnp.broadcast_to(c, (8, d)), name="cond_silu")[0:1]
    small_in = jnp.concatenate([_rows128(cond), _rows128(conv_w[0]), _rows128(gla_w_gate[0])], axis=0)
    small_in = _rows128(small_in, 56)
    got = _all_gather_small(small_in, name="gather_small").reshape(N_DEV, 56, 128)
    cond_all = got[:, 0:8].reshape(N_DEV, d)
    conv_w_full = _from_col_shards(got[0::2, 8:41].reshape(N_CHIP, 3 * 1408 // 128, 128), 3, 1408)
    gate_full = _from_col_shards(got[0::2, 41:49].reshape(N_CHIP, 16 * 64 // 128, 128), GLA_GATE_RANK, 64)
    mod_part = _ada_mod(cond_all, w_ada[0], name="ada_mod")
    mod_got = _all_gather_small(_rows128(mod_part), name="gather_mod").reshape(N_DEV, N_DEV, 1536)
    mod_all = mod_got[0::2].transpose(1, 0, 2).reshape(N_DEV, 6 * d) + b_ada
    mod = lax.dynamic_slice_in_dim(mod_all, dev, 1, axis=0)

    own = [w[0].astype(BF16).reshape(2, w.shape[1] // 2, w.shape[2]) for w in (w_in, w_out, w_up, w_down)]
    with_own = lambda got, mine: [lax.dynamic_update_index_in_dim(t, o, chip, 0) for t, o in zip(got, mine)]
    got_in, got_out = with_own(_gather_weight_shards(own[:2], name="gather_weights"), own[:2])
    w_in_full = got_in.reshape(N_CHIP, d, 772).transpose(1, 0, 2).reshape(d, N_CHIP * 772)
    w_out_full = got_out.reshape(d, d)
    exchanged = mod_all[0:1, 0:1] + got_in[0, 0, 0:1, 0:1].astype(F32)
    send_sems, recv_sems, own_thru, lands, token = _gather_late_start(own[2:], exchanged, name="gather_late_start")
    mod = mod + token[0:1, 0:1]

    def ffn_weights(after):
        mine, landed = _gather_late_wait(send_sems, recv_sems, own_thru, lands, after, name="gather_late_wait")
        got_up, got_down = with_own(landed, mine)
        return got_up.reshape(N_CHIP, d, 1408).transpose(1, 0, 2).reshape(d, 2 * D_FF), got_down.reshape(D_FF, d)

    ffn_reduce, attn_reduce, attn_parts = [], [], []
    halves_of = lambda g: g.reshape(N_CHIP, 2, g.shape[-2] // 2, g.shape[-1])

    def ffn_grads_ready(g_wup_b, g_wdown_b):
        ffn_reduce.extend(_direct_reduce_start([halves_of(g_wup_b), halves_of(g_wdown_b.reshape(N_CHIP, D_FF // N_CHIP, d))],
                                               name="reduce_ffn_start"))
        return ffn_reduce[4]

    def attn_grads_ready(g_wi, g_wo):
        attn_parts.extend([_in_proj_grad_layout(g_wi).reshape(d, N_CHIP, 772).transpose(1, 0, 2), g_wo.reshape(N_CHIP, d // N_CHIP, d)])
        attn_reduce.extend(_direct_reduce_start([halves_of(g.astype(BF16)) for g in attn_parts], name="reduce_attn_start"))
        return attn_reduce[4]

    err2, grad_x, (g_wi, g_wo, g_wup, g_wdown), small = _local_step(
        x[0], loss_target[0], mod, _in_proj_layout(w_in_full), w_out_full, ffn_weights, ffn_grads_ready, attn_grads_ready,
        conv_w_full, conv_b,
        _gate_layout(gate_full), gla_b_gate, gla_norm_g, q_norm_g, k_norm_g, norm1_g, norm2_g)

    pieces = [err2[0], small["dmod"], small["norm1_g"], small["norm2_g"], small["gla_w_gate"].reshape(-1), small["gla_b_gate"],
              small["gla_norm_g"], small["q_norm_g"], small["k_norm_g"], small["conv_w"].reshape(-1), small["conv_b"]]
    sizes = [p.shape[0] for p in pieces]
    at = [sum(sizes[:i]) for i in range(len(sizes) + 1)]
    vec = _rows128(jnp.concatenate(pieces), 288)
    got = _all_gather_small(vec, name="gather_grads").reshape(N_DEV, 288, 128)
    total, loss8 = _sum_devices(got, name="sum_devices")
    total = total.reshape(-1)
    seg = lambda i: total[at[i]:at[i + 1]]
    dmod_all = got.reshape(N_DEV, -1)[:, at[1]:at[2]]
    g_small = dict(
        b_ada=seg(1)[None], norm1_g=seg(2)[None], norm2_g=seg(3)[None],
        gla_w_gate=lax.dynamic_slice_in_dim(seg(4).reshape(GLA_GATE_RANK, 256), chip * 64, 64, axis=1),
        gla_b_gate=seg(5)[None], gla_norm_g=seg(6)[None], q_norm_g=seg(7)[None], k_norm_g=seg(8)[None],
        conv_w=lax.dynamic_slice_in_dim(seg(9).reshape(3, 2 * D_FF), chip * 1408, 1408, axis=1), conv_b=seg(10)[None])
    dmod_cols = lax.dynamic_slice_in_dim(dmod_all.reshape(N_DEV, 6 * d), chip * 1536, 1536, axis=1)
    g_w_ada = _ada_grad(cond_all, dmod_cols, name="ada_grad")

    core_id, chip_id = jnp.reshape(ac, (1,)).astype(jnp.int32), jnp.reshape(chip, (1,)).astype(jnp.int32)
    landed = (_direct_reduce_wait(*attn_reduce[:4], grad_x, name="reduce_attn_wait")
              + _direct_reduce_wait(*ffn_reduce[:4], grad_x, name="reduce_ffn_wait"))
    own = attn_parts + [g_wup, g_wdown.reshape(N_CHIP, D_FF // N_CHIP, d)]
    summed = [_direct_reduce_add(g, t, chip_id, core_id, name=f"reduce_add_{tag}")
              for g, t, tag in zip(own, landed, ("w_in", "w_out", "w_up", "w_down"))]
    others = _share_halves(summed, name="share_pair")

    grads = dict(w_ada=g_w_ada, **g_small, **dict(zip(("w_in", "w_out", "w_up", "w_down"), zip(summed, others))))
    names = ["w_ada", "b_ada", "norm1_g", "w_in", "gla_w_gate", "gla_b_gate", "gla_norm_g", "q_norm_g", "k_norm_g", "w_out",
             "norm2_g", "w_up", "conv_w", "conv_b", "w_down"]
    ws = dict(w_ada=w_ada, b_ada=b_ada, norm1_g=norm1_g, w_in=w_in, gla_w_gate=gla_w_gate, gla_b_gate=gla_b_gate,
              gla_norm_g=gla_norm_g, q_norm_g=q_norm_g, k_norm_g=k_norm_g, w_out=w_out, norm2_g=norm2_g, w_up=w_up,
              conv_w=conv_w, conv_b=conv_b, w_down=w_down)
    ms = dict(w_ada=m_w_ada, b_ada=m_b_ada, norm1_g=m_norm1_g, w_in=m_w_in, gla_w_gate=m_gla_w_gate, gla_b_gate=m_gla_b_gate,
              gla_norm_g=m_gla_norm_g, q_norm_g=m_q_norm_g, k_norm_g=m_k_norm_g, w_out=m_w_out, norm2_g=m_norm2_g, w_up=m_w_up,
              conv_w=m_conv_w, conv_b=m_conv_b, w_down=m_w_down)
    vs = dict(w_ada=v_w_ada, b_ada=v_b_ada, norm1_g=v_norm1_g, w_in=v_w_in, gla_w_gate=v_gla_w_gate, gla_b_gate=v_gla_b_gate,
              gla_norm_g=v_gla_norm_g, q_norm_g=v_q_norm_g, k_norm_g=v_k_norm_g, w_out=v_w_out, norm2_g=v_norm2_g, w_up=v_w_up,
              conv_w=v_conv_w, conv_b=v_conv_b, w_down=v_w_down)
    g_out, d_out, m_out, v_out = [], [], [], []
    for nm in names:
        shape = ws[nm].shape
        flip = (lambda t: t.T) if shape[-1] % 128 and shape[-2] % 128 == 0 else (lambda t: t)
        w2 = flip(ws[nm].reshape(shape[-2:]))
        if isinstance(grads[nm], tuple):
            mine, other = grads[nm]
            dl, mn, vn, g2 = _adamw(w2, (flip(mine), flip(other), core_id), flip(ms[nm].reshape(shape[-2:])),
                                    flip(vs[nm].reshape(shape[-2:])), name=f"adamw_{nm}")
        else:
            g2 = flip(grads[nm].reshape(shape[-2:]))
            dl, mn, vn = _adamw(w2, g2, flip(ms[nm].reshape(shape[-2:])), flip(vs[nm].reshape(shape[-2:])), name=f"adamw_{nm}")
        for outs, t in ((g_out, g2), (d_out, dl), (m_out, mn), (v_out, vn)):
            outs.append(flip(t).reshape(shape))
    return (loss8[0, 0], grad_x[None], *g_out, *d_out, *m_out, *v_out)
```

```python
import functools

import jax
import jax.numpy as jnp
from jax import lax
from jax.experimental import pallas as pl
from jax.experimental.pallas import tpu as pltpu

F32, BF16 = jnp.float32, jnp.bfloat16
HIGHEST = lax.Precision.HIGHEST
MESH = pl.DeviceIdType.MESH

D_MODEL = 1024
GLA_CHUNK = 64
GLA_GATE_TAU = 16.0
GLA_GATE_RANK = 16
HEAD_LANES = 128
ATTN_BLOCK = 128
DILATIONS = (1, 4, 16)
ALIBI_SLOPES = tuple(2.0 ** (-(h + 1)) for h in range(8))
D_FF = 2816
EPS = 1e-6
C_GQ, C_GK, C_GV, C_GR, C_AQ, C_AK, C_AV, C_LR, PROJ_W = 0, 256, 512, 1024, 1536, 2048, 2560, 3072, 3200
ADAM_LR, ADAM_B1, ADAM_B2, ADAM_EPS, ADAM_WD, ADAM_STEP = 0.001, 0.9, 0.999, 1e-08, 0.01, 10
VMEM_LIMIT_BYTES = 56 * 1024 * 1024
ROW_TILE = 512
ADAM_TILE = 256


def _params(*sem):
    return pltpu.CompilerParams(dimension_semantics=sem or None, vmem_limit_bytes=VMEM_LIMIT_BYTES)


def _nt(a, b):
    return lax.dot_general(a, b, (((1,), (1,)), ((), ())), preferred_element_type=F32)


def _tn(a, b):
    return lax.dot_general(a, b, (((0,), (0,)), ((), ())), preferred_element_type=F32)


def _nn(a, b, precision=None):
    return jnp.dot(a, b, preferred_element_type=F32, precision=precision)


def _split3(v):
    hi = v.astype(BF16)
    rest = v - hi.astype(F32)
    mid = rest.astype(BF16)
    return hi, mid, (rest - mid.astype(F32)).astype(BF16)


def _sum_right(v, ones):
    hi, mid, lo = _split3(v)
    return (_nn(lo, ones) + _nn(mid, ones)) + _nn(hi, ones)


def _sum_left(ones, v):
    hi, mid, lo = _split3(v)
    return (_nn(ones, lo) + _nn(ones, mid)) + _nn(ones, hi)


def _fold8(v):
    return v.reshape(v.shape[0] // 8, 8, v.shape[1]).sum(axis=0)


def _spread_total(ref):
    t = ref[...]
    ref[...] = jnp.broadcast_to(jnp.sum(t, axis=-2, keepdims=True), t.shape)


def _sigmoid(x):
    return 1.0 / (1.0 + jnp.exp(-x))


def _mm(a, b, *, ta=False, tb=False, out_dtype=F32, tm, tn, tk, shard_cols=False, also_bf16=False, name):
    (k_a, m) = a.shape if ta else a.shape[::-1]
    (k_b, n) = b.shape[::-1] if tb else b.shape
    assert k_a == k_b and m % tm == 0 and n % tn == 0 and k_a % tk == 0, (name, a.shape, b.shape)
    nk = k_a // tk
    assert nk == 1 or out_dtype == F32, name
    dims = (((0 if ta else 1,), (1 if tb else 0,)), ((), ()))

    def body(a_ref, b_ref, o_ref, *rounded):
        k = pl.program_id(2)
        part = lax.dot_general(a_ref[...].astype(BF16), b_ref[...].astype(BF16), dims, preferred_element_type=F32)
        if nk == 1:
            o_ref[...] = part.astype(out_dtype)
        else:
            @pl.when(k == 0)
            def _():
                o_ref[...] = part

            @pl.when(k > 0)
            def _():
                o_ref[...] += part

        if also_bf16:
            @pl.when(k == nk - 1)
            def _():
                rounded[0][...] = o_ref[...].astype(BF16)

    a_spec = pl.BlockSpec((tk, tm), lambda i, j, k: (k, i)) if ta else pl.BlockSpec((tm, tk), lambda i, j, k: (i, k))
    b_spec = pl.BlockSpec((tn, tk), lambda i, j, k: (j, k)) if tb else pl.BlockSpec((tk, tn), lambda i, j, k: (k, j))
    if shard_cols:
        o_spec, o_shape = pl.BlockSpec((None, tm, tn), lambda i, j, k: (j, i, 0)), (n // tn, m, tn)
    else:
        o_spec, o_shape = pl.BlockSpec((tm, tn), lambda i, j, k: (i, j)), (m, n)
    shapes = [jax.ShapeDtypeStruct(o_shape, out_dtype)] + ([jax.ShapeDtypeStruct(o_shape, BF16)] if also_bf16 else [])
    out = pl.pallas_call(
        body, name=name, grid=(m // tm, n // tn, nk), in_specs=[a_spec, b_spec], out_specs=[o_spec] * len(shapes),
        out_shape=shapes, compiler_params=_params("parallel", "parallel", "arbitrary"))(a, b)
    return out if also_bf16 else out[0]


def _norm_mod_fwd(x, branch, gate, gain, scale, shift, *, name):
    s, d = x.shape
    tm = 2 * ROW_TILE
    has_branch = branch is not None

    def body(*refs):
        if has_branch:
            x_ref, br_ref, gate_ref, gain_ref, sc_ref, sh_ref, x1_ref, h_ref, ht_ref = refs
            xv = x_ref[...] + gate_ref[...] * br_ref[...]
            x1_ref[...] = xv
        else:
            x_ref, gain_ref, sc_ref, sh_ref, h_ref, ht_ref = refs
            xv = x_ref[...]
        r = lax.rsqrt(jnp.mean(xv * xv, axis=-1, keepdims=True) + EPS)
        h = (xv * r) * gain_ref[...] * (1.0 + sc_ref[...]) + sh_ref[...]
        h_ref[...] = h.astype(BF16)
        ht_ref[...] = h.T.astype(BF16)

    row = pl.BlockSpec((tm, d), lambda i: (i, 0))
    col = pl.BlockSpec((d, tm), lambda i: (0, i))
    vec = pl.BlockSpec((1, d), lambda i: (0, 0))
    h_shapes = [jax.ShapeDtypeStruct((s, d), BF16), jax.ShapeDtypeStruct((d, s), BF16)]
    if has_branch:
        return pl.pallas_call(
            body, name=name, grid=(s // tm,), in_specs=[row, row, vec, vec, vec, vec], out_specs=[row, row, col],
            out_shape=[jax.ShapeDtypeStruct((s, d), F32)] + h_shapes,
            compiler_params=_params("parallel"))(x, branch, gate, gain, scale, shift)
    h, ht = pl.pallas_call(
        body, name=name, grid=(s // tm,), in_specs=[row, vec, vec, vec], out_specs=[row, col],
        out_shape=h_shapes, compiler_params=_params("parallel"))(x, gain, scale, shift)
    return x, h, ht


def _norm_mod_bwd(x, dh, dres, gain, scale, branch, gate, *, name):
    s, d = x.shape
    tm = ROW_TILE
    has_branch = branch is not None

    def body(*refs):
        if has_branch:
            x_ref, dh_ref, dres_ref, gain_ref, sc_ref, br_ref, gate_ref, dx_ref, dbr_ref, sums_ref = refs
        else:
            x_ref, dh_ref, dres_ref, gain_ref, sc_ref, dx_ref, sums_ref = refs
        i = pl.program_id(0)

        @pl.when(i == 0)
        def _():
            sums_ref[...] = jnp.zeros_like(sums_ref)

        xv, dhv = x_ref[...], dh_ref[...]
        r = lax.rsqrt(jnp.mean(xv * xv, axis=-1, keepdims=True) + EPS)
        xn = xv * r
        dxn = dhv * (gain_ref[...] * (1.0 + sc_ref[...]))
        dx = dres_ref[...] + r * (dxn - xn * jnp.mean(dxn * xn, axis=-1, keepdims=True))
        dx_ref[...] = dx
        sums_ref[0] += _fold8(dhv * xn)
        sums_ref[1] += _fold8(dhv)
        if has_branch:
            dbr_ref[...] = (gate_ref[...] * dx).astype(BF16)
            sums_ref[2] += _fold8(dx * br_ref[...])

        @pl.when(i == s // tm - 1)
        def _():
            _spread_total(sums_ref)

    row = pl.BlockSpec((tm, d), lambda i: (i, 0))
    vec = pl.BlockSpec((1, d), lambda i: (0, 0))
    sums = pl.BlockSpec((3, 8, d), lambda i: (0, 0, 0))
    sums_shape = jax.ShapeDtypeStruct((3, 8, d), F32)
    if has_branch:
        return pl.pallas_call(
            body, name=name, grid=(s // tm,), in_specs=[row, row, row, vec, vec, row, vec], out_specs=[row, row, sums],
            out_shape=[jax.ShapeDtypeStruct((s, d), F32), jax.ShapeDtypeStruct((s, d), BF16), sums_shape],
            compiler_params=_params("arbitrary"))(x, dh, dres, gain, scale, branch, gate)
    dx, sm = pl.pallas_call(
        body, name=name, grid=(s // tm,), in_specs=[row, row, row, vec, vec], out_specs=[row, sums],
        out_shape=[jax.ShapeDtypeStruct((s, d), F32), sums_shape],
        compiler_params=_params("arbitrary"))(x, dh, dres, gain, scale)
    return dx, None, sm


GLA_ROWS = 256


def _gla_block_setup(lr_ref, wg_ref, bg_ref):
    t, c = GLA_ROWS, GLA_CHUNK
    ri = lax.broadcasted_iota(jnp.int32, (t, t), 0)
    ci = lax.broadcasted_iota(jnp.int32, (t, t), 1)
    same = (ri // c) == (ci // c)
    causal, upper = same & (ci <= ri), same & (ci >= ri)
    z = _nn(lr_ref[...].astype(BF16), wg_ref[...]) + bg_ref[...]
    g = (jnp.minimum(z, 0.0) - jnp.log(1.0 + jnp.exp(-jnp.abs(z)))) * (1.0 / GLA_GATE_TAU)
    hi, mid, lo = _split3(g)
    total = lambda ones: (_nn(ones, lo) + _nn(ones, mid)) + _nn(ones, hi)
    return z, total(causal.astype(BF16)), total(same.astype(BF16)), causal, upper


def _chunks(t):
    return [t[i * GLA_CHUNK:(i + 1) * GLA_CHUNK] for i in range(GLA_ROWS // GLA_CHUNK)]


def _gla_fwd(proj, wg, bg, gn, *, name):
    s = proj.shape[0]
    tb, c = GLA_ROWS, GLA_CHUNK
    cb = tb // c

    def body(q_ref, k_ref, v_ref, r_ref, lr_ref, wg_ref, bg_ref, gn_ref, o_ref, y_ref, st_ref, state):
        i = pl.program_id(0)

        @pl.when(i == 0)
        def _():
            state[...] = jnp.zeros_like(state)

        low = lax.broadcasted_iota(jnp.int32, (tb, HEAD_LANES), 1) < 64
        masks = (low, jnp.logical_not(low))
        _, b, b_end, causal, _ = _gla_block_setup(lr_ref, wg_ref, bg_ref)
        pairs = []
        for p in range(2):
            cols = pl.ds(p * HEAD_LANES, HEAD_LANES)
            bp, bep = (t[:, p * HEAD_LANES:(p + 1) * HEAD_LANES] for t in (b, b_end))
            k = k_ref[:, cols]
            q_in = q_ref[:, cols] * 0.125 * jnp.exp(bp)
            k_out = (k * jnp.exp(-bp)).astype(BF16)
            k_end = k * jnp.exp(bep - bp)
            qms = [jnp.where(m, q_in, 0.0).astype(BF16) for m in masks]
            kes = [jnp.where(m, k_end, 0.0).astype(BF16) for m in masks]
            vs = [v_ref[:, pl.ds((2 * p + e) * HEAD_LANES, HEAD_LANES)].astype(BF16) for e in range(2)]
            grow = [_tn(v0, k0) + _tn(v1, k1) for v0, k0, v1, k1 in zip(_chunks(vs[0]), _chunks(kes[0]), _chunks(vs[1]), _chunks(kes[1]))]
            pairs.append((bep, k_out, qms, vs, grow))
        entering = [[], []]
        for p, (bep, _, _, _, grow) in enumerate(pairs):
            st = state[p]
            for ch in range(cb):
                entering[p].append(st)
                st_ref[ch, p] = st
                st = st * jnp.exp(bep[ch * c:ch * c + 1, :]) + grow[ch]
            state[p] = st
        for p, (_, k_out, qms, vs, _) in enumerate(pairs):
            for e in range(2):
                hc = pl.ds((2 * p + e) * HEAD_LANES, HEAD_LANES)
                a = jnp.where(causal, _nt(qms[e], k_out), 0.0).astype(BF16)
                carried = jnp.concatenate([_nt(qc, sc.astype(BF16)) for qc, sc in zip(_chunks(qms[e]), entering[p])], axis=0)
                o = _nn(a, vs[e]) + carried
                o_ref[:, hc] = o
                rr = r_ref[:, hc]
                on = o * lax.rsqrt(jnp.mean(o * o, axis=-1, keepdims=True) + EPS)
                y_ref[:, hc] = (on * gn_ref[...] * (rr * _sigmoid(rr))).astype(BF16)

    def col(width, at):
        return pl.BlockSpec((tb, width), lambda i: (i, at // width))

    full = lambda shape: pl.BlockSpec(shape, lambda i: tuple(0 for _ in shape))
    return pl.pallas_call(
        body, name=name, grid=(s // tb,),
        in_specs=[col(256, C_GQ), col(256, C_GK), col(512, C_GV), col(512, C_GR), col(128, C_LR),
                  full((HEAD_LANES, 256)), full((1, 256)), full((1, HEAD_LANES))],
        out_specs=[pl.BlockSpec((tb, 512), lambda i: (i, 0)), pl.BlockSpec((tb, 512), lambda i: (i, 0)),
                   pl.BlockSpec((cb, 2, HEAD_LANES, HEAD_LANES), lambda i: (i, 0, 0, 0))],
        out_shape=[jax.ShapeDtypeStruct((s, 512), F32), jax.ShapeDtypeStruct((s, 512), BF16),
                   jax.ShapeDtypeStruct((s // c, 2, HEAD_LANES, HEAD_LANES), F32)],
        scratch_shapes=[pltpu.VMEM((2, HEAD_LANES, HEAD_LANES), F32)],
        compiler_params=_params("arbitrary"))(proj, proj, proj, proj, proj, wg, bg, gn)


def _gla_bwd(proj, wg, bg, gn, o_raw, states, dmixed, *, name):
    s = proj.shape[0]
    tb, c = GLA_ROWS, GLA_CHUNK
    cb = tb // c
    nblk, nch = s // tb, s // c

    def body(q_ref, k_ref, v_ref, r_ref, lr_ref, wg_ref, bg_ref, gn_ref, o_ref, st_ref, stn_ref, dy_ref,
             dq_ref, dk_ref, dv_ref, dr_ref, dlr_ref, gwg_ref, sums_ref, dstate):
        i = pl.program_id(0)

        @pl.when(i == 0)
        def _():
            dstate[...] = jnp.zeros_like(dstate)
            gwg_ref[...] = jnp.zeros_like(gwg_ref)
            sums_ref[...] = jnp.zeros_like(sums_ref)

        low = lax.broadcasted_iota(jnp.int32, (tb, HEAD_LANES), 1) < 64
        masks = (low, jnp.logical_not(low))
        z, b, b_end, causal, upper = _gla_block_setup(lr_ref, wg_ref, bg_ref)
        lr_b = lr_ref[...].astype(BF16)
        dlr = jnp.zeros((tb, HEAD_LANES), F32)
        per_chunk = lambda rows, mats, fn: jnp.concatenate([fn(r, m.astype(BF16)) for r, m in zip(_chunks(rows), mats)], axis=0)
        pairs = []
        for p in range(2):
            cols = pl.ds(p * HEAD_LANES, HEAD_LANES)
            sl = slice(p * HEAD_LANES, (p + 1) * HEAD_LANES)
            bp, bep = b[:, sl], b_end[:, sl]
            e_in, e_out, e_end = jnp.exp(bp), jnp.exp(-bp), jnp.exp(bep - bp)
            q = q_ref[:, cols] * 0.125
            k = k_ref[:, cols]
            q_in, k_out, k_end = q * e_in, k * e_out, k * e_end
            qms = [jnp.where(m, q_in, 0.0).astype(BF16) for m in masks]
            kms_out = [jnp.where(m, k_out, 0.0).astype(BF16) for m in masks]
            kms_end = [jnp.where(m, k_end, 0.0).astype(BF16) for m in masks]
            vs, dos = [], []
            for e in range(2):
                hc = pl.ds((2 * p + e) * HEAD_LANES, HEAD_LANES)
                o, rr, dy = o_ref[:, hc], r_ref[:, hc], dy_ref[:, hc]
                sg = _sigmoid(rr)
                rs = lax.rsqrt(jnp.mean(o * o, axis=-1, keepdims=True) + EPS)
                on = o * rs
                t = dy * (rr * sg)
                sums_ref[1, :, hc] += _fold8(t * on)
                dn = t * gn_ref[...]
                dos.append((rs * (dn - on * jnp.mean(dn * on, axis=-1, keepdims=True))).astype(BF16))
                dr_ref[:, hc] = (dy * on * gn_ref[...] * (sg * (1.0 + rr * (1.0 - sg)))).astype(BF16)
                vs.append(v_ref[:, hc].astype(BF16))
            grow = [_tn(d0, q0) + _tn(d1, q1) for d0, q0, d1, q1 in zip(_chunks(dos[0]), _chunks(qms[0]), _chunks(dos[1]), _chunks(qms[1]))]
            pairs.append((bep, e_in, e_out, e_end, q, k, qms, kms_out, kms_end, vs, dos, grow))
        chains = []
        for p in range(2):
            bep, grow = pairs[p][0], pairs[p][-1]
            entering = [st_ref[ch, p] for ch in range(cb)]
            dst, leaving_grad = dstate[p], [None] * cb
            for ch in reversed(range(cb)):
                leaving_grad[ch] = dst
                dst = dst * jnp.exp(bep[ch * c:ch * c + 1, :]) + grow[ch]
            dstate[p] = dst
            chains.append((entering, leaving_grad))
        for p in range(2):
            cols = pl.ds(p * HEAD_LANES, HEAD_LANES)
            sl = slice(p * HEAD_LANES, (p + 1) * HEAD_LANES)
            _, e_in, e_out, e_end, q, k, qms, kms_out, kms_end, vs, dos, _ = pairs[p]
            entering, leaving_grad = chains[p]
            leaving = entering[1:] + [stn_ref[0, p]]
            felt = jnp.concatenate([jnp.broadcast_to(jnp.sum(dg_st * st, axis=0, keepdims=True), (c, HEAD_LANES))
                                    for dg_st, st in zip(leaving_grad, leaving)], axis=0)
            dq_in = jnp.zeros((tb, HEAD_LANES), F32)
            dk_out = jnp.zeros((tb, HEAD_LANES), F32)
            dk_end = jnp.zeros((tb, HEAD_LANES), F32)
            for e in range(2):
                hc = pl.ds((2 * p + e) * HEAD_LANES, HEAD_LANES)
                a = jnp.where(causal, _nt(qms[e], kms_out[e]), 0.0).astype(BF16)
                da = jnp.where(causal, _nt(dos[e], vs[e]), 0.0).astype(BF16)
                dv_ref[:, hc] = (_tn(a, dos[e]) + per_chunk(kms_end[e], leaving_grad, _nt)).astype(BF16)
                dq_in = dq_in + jnp.where(masks[e], per_chunk(dos[e], entering, _nn) + _nn(da, kms_out[e]), 0.0)
                dk_out = dk_out + _tn(da, qms[e])
                dk_end = dk_end + jnp.where(masks[e], per_chunk(vs[e], leaving_grad, _nn), 0.0)
            dq = dq_in * e_in
            dk = dk_out * e_out + dk_end * e_end
            dq_ref[:, cols] = (dq * 0.125).astype(BF16)
            dk_ref[:, cols] = dk.astype(BF16)
            dg = _sum_left(upper.astype(BF16), q * dq - k * dk) + felt
            dz = dg * (1.0 / GLA_GATE_TAU) * _sigmoid(-z[:, sl])
            dz_b = dz.astype(BF16)
            sums_ref[0, :, cols] += _fold8(dz)
            dlr = dlr + _nt(dz_b, wg_ref[:, cols])
            gwg_ref[:, cols] += _tn(lr_b, dz_b)
        dlr_ref[...] = dlr.astype(BF16)

        @pl.when(i == nblk - 1)
        def _():
            _spread_total(sums_ref)

    rev = lambda i: nblk - 1 - i

    def col(width, at):
        return pl.BlockSpec((tb, width), lambda i: (rev(i), at // width))

    full = lambda shape: pl.BlockSpec(shape, lambda i: tuple(0 for _ in shape))
    out_col = lambda width: pl.BlockSpec((tb, width), lambda i: (rev(i), 0))
    return pl.pallas_call(
        body, name=name, grid=(nblk,),
        in_specs=[col(256, C_GQ), col(256, C_GK), col(512, C_GV), col(512, C_GR), col(128, C_LR),
                  full((HEAD_LANES, 256)), full((1, 256)), full((1, HEAD_LANES)),
                  pl.BlockSpec((tb, 512), lambda i: (rev(i), 0)),
                  pl.BlockSpec((cb, 2, HEAD_LANES, HEAD_LANES), lambda i: (rev(i), 0, 0, 0)),
                  pl.BlockSpec((1, 2, HEAD_LANES, HEAD_LANES), lambda i: (jnp.minimum((rev(i) + 1) * cb, nch - 1), 0, 0, 0)),
                  pl.BlockSpec((tb, 512), lambda i: (rev(i), 0))],
        out_specs=[out_col(256), out_col(256), out_col(512), out_col(512), out_col(128),
                   full((HEAD_LANES, 256)), full((2, 8, 512))],
        out_shape=[jax.ShapeDtypeStruct((s, 256), BF16), jax.ShapeDtypeStruct((s, 256), BF16),
                   jax.ShapeDtypeStruct((s, 512), BF16), jax.ShapeDtypeStruct((s, 512), BF16),
                   jax.ShapeDtypeStruct((s, 128), BF16), jax.ShapeDtypeStruct((HEAD_LANES, 256), F32),
                   jax.ShapeDtypeStruct((2, 8, 512), F32)],
        scratch_shapes=[pltpu.VMEM((2, HEAD_LANES, HEAD_LANES), F32)],
        compiler_params=_params("arbitrary"))(proj, proj, proj, proj, proj, wg, bg, gn, o_raw, states, states, dmixed)


def _head_sums(v):
    ri = lax.broadcasted_iota(jnp.int32, (HEAD_LANES, HEAD_LANES), 0) // 64
    ci = lax.broadcasted_iota(jnp.int32, (HEAD_LANES, HEAD_LANES), 1) // 64
    ones = (ri == ci).astype(BF16)
    return jnp.concatenate([_sum_right(v[:, p * HEAD_LANES:(p + 1) * HEAD_LANES], ones) for p in range(4)], axis=1)


def _attn_prep(proj, qg, kg, *, name):
    s = proj.shape[0]
    tm = 2 * ROW_TILE

    def body(q_ref, k_ref, qg_ref, kg_ref, qa_ref, ka_ref):
        q, k = q_ref[...], k_ref[...]
        qr = lax.rsqrt(_head_sums(q * q) * (1.0 / 64) + EPS)
        kr = lax.rsqrt(_head_sums(k * k) * (1.0 / 64) + EPS)
        qa_ref[...] = q * qr * qg_ref[...] * 0.125
        ka_ref[...] = k * kr * kg_ref[...]

    col = lambda at: pl.BlockSpec((tm, 512), lambda i: (i, at // 512))
    vec = pl.BlockSpec((1, 512), lambda i: (0, 0))
    out = pl.BlockSpec((tm, 512), lambda i: (i, 0))
    return pl.pallas_call(
        body, name=name, grid=(s // tm,), in_specs=[col(C_AQ), col(C_AK), vec, vec], out_specs=[out] * 2,
        out_shape=[jax.ShapeDtypeStruct((s, 512), F32)] * 2, compiler_params=_params("parallel"))(proj, proj, qg, kg)


FAR = 1e30
LOG2E, LN2 = 1.4426950408889634, 0.6931471805599453


def _attn_distance(first):
    blk = ATTN_BLOCK
    iq = lax.broadcasted_iota(jnp.int32, (2 * blk, 2 * blk), 0) & (blk - 1)
    ik = lax.broadcasted_iota(jnp.int32, (2 * blk, 2 * blk), 1)
    rel = iq + blk - ik
    valid = (rel >= 0) & (rel <= blk) & (jnp.logical_not(first) | (ik >= blk))
    return jnp.where(valid, rel.astype(F32), FAR)


def _stack_heads(t2):
    low = lax.broadcasted_iota(jnp.int32, t2.shape, 1) < 64
    return jnp.concatenate([jnp.where(low, t2, 0.0), jnp.where(low, 0.0, t2)], axis=0).astype(BF16)


def _unstack_heads(t):
    blk = ATTN_BLOCK
    low = lax.broadcasted_iota(jnp.int32, (blk, HEAD_LANES), 1) < 64
    return jnp.where(low, t[0:blk], t[blk:2 * blk])


def _attn_scores(qs, kcat, slopes, dil, dist):
    top = lax.broadcasted_iota(jnp.int32, (2 * ATTN_BLOCK, 1), 0) < ATTN_BLOCK
    return _nt(qs, kcat) - jnp.where(top, slopes[0] * (dil * LOG2E), slopes[1] * (dil * LOG2E)) * dist


def _pair_slopes(p):
    if isinstance(p, int):
        return ALIBI_SLOPES[2 * p], ALIBI_SLOPES[2 * p + 1]
    pick = lambda e: jnp.where(p == 0, ALIBI_SLOPES[e], jnp.where(p == 1, ALIBI_SLOPES[2 + e],
                               jnp.where(p == 2, ALIBI_SLOPES[4 + e], ALIBI_SLOPES[6 + e])))
    return pick(0), pick(1)


ATTN_GROUP = 4


def _each(fn, *lists):
    return [fn(*args) for args in zip(*lists)]


def _attn_group_fwd(q2s, kcats, vcats, slopes, dil, dist):
    qs = _each(lambda q2: _stack_heads(q2 * LOG2E), q2s)
    sc = _each(lambda q, k, sl: _attn_scores(q, k, sl, dil, dist), qs, kcats, slopes)
    m = _each(lambda s: jnp.max(s, axis=-1, keepdims=True), sc)
    pr = _each(lambda s, mx: jnp.exp2(s - mx), sc, m)
    den = _each(lambda p: jnp.sum(p, axis=-1, keepdims=True), pr)
    o = _each(lambda p, v, d: _nn(p.astype(BF16), v) / d, pr, vcats, den)
    lse = _each(lambda mx, d, t: jnp.broadcast_to(mx + jnp.log2(d), t.shape), m, den, o)
    return _each(lambda t, l: (_unstack_heads(t), _unstack_heads(l)), o, lse)


def _attn_group_bwd(q2s, kcats, vcats, do2s, y2s, lse2s, slopes, dil, dist):
    lane = lax.broadcasted_iota(jnp.int32, (ATTN_BLOCK, HEAD_LANES), 1)
    low = lane < 64
    per_head = lambda t, pick: jnp.concatenate([jnp.sum(jnp.where(pick(0), t, 0.0), axis=-1, keepdims=True),
                                                jnp.sum(jnp.where(pick(1), t, 0.0), axis=-1, keepdims=True)], axis=0)
    lse = _each(lambda l: per_head(l, lambda e: lane == 64 * e), lse2s)
    delta = _each(lambda d, y: per_head(d * y, lambda e: low if e == 0 else jnp.logical_not(low)), do2s, y2s)
    qs = _each(lambda q2: _stack_heads(q2 * LOG2E), q2s)
    dos = _each(_stack_heads, do2s)
    sc = _each(lambda q, k, sl: _attn_scores(q, k, sl, dil, dist), qs, kcats, slopes)
    pr = _each(lambda s, l: jnp.exp2(s - l), sc, lse)
    dp = _each(_nt, dos, vcats)
    ds = _each(lambda p, d, dl: (p * (d - dl)).astype(BF16), pr, dp, delta)
    dq = _each(lambda d, k: _unstack_heads(_nn(d, k)), ds, kcats)
    dk = _each(lambda d, q: _tn(d, q) * LN2, ds, qs)
    dv = _each(lambda p, d: _tn(p.astype(BF16), d), pr, dos)
    return list(zip(dq, dk, dv))


def _attn_specs(dil):
    rows = ATTN_BLOCK * dil
    if dil == 1:
        cur = lambda at: pl.BlockSpec((rows, 512), lambda n: (n, at // 512))
        prev = lambda at: pl.BlockSpec((rows, 512), lambda n: (jnp.maximum(n - 1, 0), at // 512))
    else:
        cur = lambda at: pl.BlockSpec((rows, HEAD_LANES), lambda n, p: (n, at // HEAD_LANES + p))
        prev = lambda at: pl.BlockSpec((rows, HEAD_LANES), lambda n, p: (jnp.maximum(n - 1, 0), at // HEAD_LANES + p))
    return cur, prev


def _attn_loop(dil, one_group, p):
    if dil == 1:
        one_group([(slice(None), pl.ds(p * HEAD_LANES, HEAD_LANES), p) for p in range(ATTN_GROUP)])
    else:
        group = min(dil, ATTN_GROUP)

        def step(g, carry):
            one_group([(pl.ds(g * group + j, ATTN_BLOCK, stride=dil), slice(None), p) for j in range(group)])
            return carry

        if dil == group:
            step(0, 0)
        else:
            lax.fori_loop(0, dil // group, step, 0)


def _dil_attn_fwd(qa, ka, proj, dil, *, name):
    s = qa.shape[0]

    def body(q_ref, kp_ref, kc_ref, vp_ref, vc_ref, o_ref, lse_ref):
        dist = _attn_distance(pl.program_id(0) == 0)
        pair = None if dil == 1 else pl.program_id(1)

        def one_group(items):
            both = lambda a, b: [jnp.concatenate([a[rows, cols], b[rows, cols]], axis=0).astype(BF16) for rows, cols, _ in items]
            outs = _attn_group_fwd([q_ref[rows, cols] for rows, cols, _ in items], both(kp_ref, kc_ref), both(vp_ref, vc_ref),
                                   [_pair_slopes(p) for _, _, p in items], dil, dist)
            for (rows, cols, _), (o2, lse2) in zip(items, outs):
                o_ref[rows, cols] = o2
                lse_ref[rows, cols] = lse2

        _attn_loop(dil, one_group, pair)

    cur, prev = _attn_specs(dil)
    grid = (s // ATTN_BLOCK,) if dil == 1 else (s // (ATTN_BLOCK * dil), 4)
    return pl.pallas_call(
        body, name=name, grid=grid, in_specs=[cur(0), prev(0), cur(0), prev(C_AV), cur(C_AV)], out_specs=[cur(0), cur(0)],
        out_shape=[jax.ShapeDtypeStruct((s, 512), F32)] * 2,
        compiler_params=_params(*["parallel"] * len(grid)))(qa, ka, ka, proj, proj)


def _dense_attn_fwd_merge(qa, ka, proj, others, y_gla, *, name):
    s = qa.shape[0]
    blk = ATTN_BLOCK

    def body(q_ref, kp_ref, kc_ref, vp_ref, vc_ref, oa_ref, la_ref, ob_ref, lb_ref, yg_ref, mixed_ref, y_ref, lse_ref):
        dist = _attn_distance(pl.program_id(0) == 0)
        mixed_ref[:, 0:512] = yg_ref[...]

        def one_group(items):
            both = lambda a, b: [jnp.concatenate([a[rows, cols], b[rows, cols]], axis=0).astype(BF16) for rows, cols, _ in items]
            outs = _attn_group_fwd([q_ref[rows, cols] for rows, cols, _ in items], both(kp_ref, kc_ref), both(vp_ref, vc_ref),
                                   [_pair_slopes(p) for _, _, p in items], 1, dist)
            for (_, cols, p), (o2, l2) in zip(items, outs):
                la, lb = la_ref[:, cols], lb_ref[:, cols]
                m = jnp.maximum(jnp.maximum(l2, la), lb)
                w0, wa, wb = jnp.exp2(l2 - m), jnp.exp2(la - m), jnp.exp2(lb - m)
                zs = w0 + wa + wb
                y = (w0 * o2 + wa * oa_ref[:, cols] + wb * ob_ref[:, cols]) / zs
                y_ref[:, cols] = y
                lse_ref[:, cols] = m + jnp.log2(zs)
                mixed_ref[:, pl.ds(512 + p * HEAD_LANES, HEAD_LANES)] = y.astype(BF16)

        _attn_loop(1, one_group, None)

    cur, prev = _attn_specs(1)
    here = pl.BlockSpec((blk, 512), lambda n: (n, 0))
    (oa, la), (ob, lb) = others
    return pl.pallas_call(
        body, name=name, grid=(s // blk,),
        in_specs=[cur(0), prev(0), cur(0), prev(C_AV), cur(C_AV)] + [here] * 5,
        out_specs=[pl.BlockSpec((blk, 1024), lambda n: (n, 0)), here, here],
        out_shape=[jax.ShapeDtypeStruct((s, 1024), BF16), jax.ShapeDtypeStruct((s, 512), F32),
                   jax.ShapeDtypeStruct((s, 512), F32)],
        compiler_params=_params("parallel"))(qa, ka, ka, proj, proj, oa, la, ob, lb, y_gla)


def _dil_attn_bwd(qa, ka, proj, y_att, lse, dmixed, dil, *, name):
    s = qa.shape[0]
    blk, rows_per_step = ATTN_BLOCK, ATTN_BLOCK * dil
    nb = s // rows_per_step
    step_axis = 0 if dil == 1 else 1

    def body(q_ref, kp_ref, kc_ref, vp_ref, vc_ref, y_ref, lse_ref, do_ref, dq_ref, dk_ref, dv_ref, dk_own, dv_own):
        n = pl.program_id(step_axis)
        pair = None if dil == 1 else pl.program_id(0)
        dist = _attn_distance(n == 0)

        @pl.when(n == 0)
        def _():
            dk_own[...] = jnp.zeros_like(dk_own)
            dv_own[...] = jnp.zeros_like(dv_own)

        def one_group(items):
            both = lambda a, b: [jnp.concatenate([a[rows, cols], b[rows, cols]], axis=0).astype(BF16) for rows, cols, _ in items]
            at = lambda ref: [ref[rows, cols] for rows, cols, _ in items]
            outs = _attn_group_bwd(at(q_ref), both(kp_ref, kc_ref), both(vp_ref, vc_ref), at(do_ref), at(y_ref), at(lse_ref),
                                   [_pair_slopes(p) for _, _, p in items], dil, dist)
            for (rows, cols, _), (dq, dk, dv) in zip(items, outs):
                dq_ref[rows, cols] = dq
                dk_ref[rows, cols] = dk_own[rows, cols] + dk[0:blk]
                dv_ref[rows, cols] = dv_own[rows, cols] + dv[0:blk]
                dk_own[rows, cols] = dk[blk:2 * blk]
                dv_own[rows, cols] = dv[blk:2 * blk]

        _attn_loop(dil, one_group, pair)

    width = 512 if dil == 1 else HEAD_LANES

    def spec(at, row_of):
        if dil == 1:
            return pl.BlockSpec((rows_per_step, width), lambda n: (row_of(n), at // width))
        return pl.BlockSpec((rows_per_step, width), lambda p, n: (row_of(n), at // width + p))

    cur = lambda at: spec(at, lambda n: n)
    prev = lambda at: spec(at, lambda n: jnp.maximum(n - 1, 0))
    own = spec(0, lambda n: 0)
    grid = (nb,) if dil == 1 else (4, nb)
    sems = ("arbitrary",) if dil == 1 else ("parallel", "arbitrary")
    dq, dk, dv, dk_last, dv_last = pl.pallas_call(
        body, name=name, grid=grid,
        in_specs=[cur(0), prev(0), cur(0), prev(C_AV), cur(C_AV), cur(0), cur(0), cur(512)],
        out_specs=[cur(0), prev(0), prev(0), own, own],
        out_shape=[jax.ShapeDtypeStruct((s, 512), F32)] * 3 + [jax.ShapeDtypeStruct((rows_per_step, 512), F32)] * 2,
        compiler_params=_params(*sems),
    )(qa, ka, ka, proj, proj, y_att, lse, dmixed)
    return dq, dk.at[s - rows_per_step:].set(dk_last), dv.at[s - rows_per_step:].set(dv_last)


def _attn_post(parts, proj, qg, kg, *, name):
    s = proj.shape[0]
    tm = ROW_TILE
    nblk = s // tm

    def body(*refs):
        ins, (q_ref, k_ref, qg_ref, kg_ref, dq_out, dk_out, dv_out, sums_ref) = refs[:9], refs[9:]
        i = pl.program_id(0)

        @pl.when(i == 0)
        def _():
            sums_ref[...] = jnp.zeros_like(sums_ref)

        dq = (ins[0][...] + ins[3][...]) + ins[6][...]
        dk = (ins[1][...] + ins[4][...]) + ins[7][...]
        dv = (ins[2][...] + ins[5][...]) + ins[8][...]
        dv_out[...] = dv.astype(BF16)
        for row, (x_ref, g_ref, dy, out, post) in enumerate(((q_ref, qg_ref, dq, dq_out, 0.125), (k_ref, kg_ref, dk, dk_out, 1.0))):
            x = x_ref[...]
            rs = lax.rsqrt(_head_sums(x * x) * (1.0 / 64) + EPS)
            xn = x * rs
            dy = dy * post
            sums_ref[row] += _fold8(dy * xn)
            dn = dy * g_ref[...]
            out[...] = (rs * (dn - xn * (_head_sums(dn * xn) * (1.0 / 64)))).astype(BF16)

        @pl.when(i == nblk - 1)
        def _():
            _spread_total(sums_ref)

    here = pl.BlockSpec((tm, 512), lambda i: (i, 0))
    col = lambda at: pl.BlockSpec((tm, 512), lambda i: (i, at // 512))
    vec = pl.BlockSpec((1, 512), lambda i: (0, 0))
    return pl.pallas_call(
        body, name=name, grid=(nblk,), in_specs=[here] * 9 + [col(C_AQ), col(C_AK), vec, vec],
        out_specs=[here, here, here, pl.BlockSpec((2, 8, 512), lambda i: (0, 0, 0))],
        out_shape=[jax.ShapeDtypeStruct((s, 512), BF16)] * 3 + [jax.ShapeDtypeStruct((2, 8, 512), F32)],
        compiler_params=_params("arbitrary"))(*[t for part in parts for t in part], proj, proj, qg, kg)


FFN_TM, FFN_TN = 512, 1408
HALO = 16


def _conv3(u_ref, halo_ref, w_ref, b_ref, first):
    u = u_ref[...].astype(F32)
    ext = jnp.concatenate([jnp.where(first, 0.0, halo_ref[...].astype(F32)), u], axis=0)
    u1 = pltpu.roll(ext, 1, 0)[HALO:]
    u2 = pltpu.roll(ext, 2, 0)[HALO:]
    return b_ref[...] + w_ref[0:1, :] * u2 + w_ref[1:2, :] * u1 + w_ref[2:3, :] * u


def _ffn_specs(tm, tn):
    nj = D_FF // tn
    blk = lambda half: pl.BlockSpec((tm, tn), lambda j, i: (i, j + half * nj))
    halo = lambda half: pl.BlockSpec((HALO, tn), lambda j, i: (jnp.maximum(i * (tm // HALO) - 1, 0), j + half * nj))
    wspec = lambda half: pl.BlockSpec((3, tn), lambda j, i: (0, j + half * nj))
    bspec = lambda half: pl.BlockSpec((1, tn), lambda j, i: (0, j + half * nj))
    return [blk(0), halo(0), blk(1), halo(1), wspec(0), wspec(1), bspec(0), bspec(1)]


def _conv_swiglu_fwd(u, conv_w, conv_b, *, name):
    s = u.shape[0]
    tm, tn = FFN_TM, FFN_TN

    def body(ug_ref, hg_ref, uv_ref, hv_ref, wg_ref, wv_ref, bg_ref, bv_ref, act_ref, uc_ref):
        first = pl.program_id(1) == 0
        cg = _conv3(ug_ref, hg_ref, wg_ref, bg_ref, first)
        cv = _conv3(uv_ref, hv_ref, wv_ref, bv_ref, first)
        act_ref[...] = (cg * _sigmoid(cg) * cv).astype(BF16)
        uc_ref[0] = cg.astype(BF16)
        uc_ref[1] = cv.astype(BF16)

    return pl.pallas_call(
        body, name=name, grid=(D_FF // tn, s // tm), in_specs=_ffn_specs(tm, tn),
        out_specs=[pl.BlockSpec((tm, tn), lambda j, i: (i, j)), pl.BlockSpec((2, tm, tn), lambda j, i: (0, i, j))],
        out_shape=[jax.ShapeDtypeStruct((s, D_FF), BF16), jax.ShapeDtypeStruct((2, s, D_FF), BF16)],
        compiler_params=_params("parallel", "parallel"))(u, u, u, u, conv_w, conv_w, conv_b, conv_b)


def _swiglu_bwd(uc, dact, *, name):
    _, s, _ = uc.shape
    tm, tn = FFN_TM, FFN_TN

    def body(uc_ref, da_ref, duc_ref, sums_ref):
        i = pl.program_id(1)

        @pl.when(i == 0)
        def _():
            sums_ref[...] = jnp.zeros_like(sums_ref)

        cg, cv, da = uc_ref[0].astype(F32), uc_ref[1].astype(F32), da_ref[...].astype(F32)
        sg = _sigmoid(cg)
        dg = da * cv * (sg * (1.0 + cg * (1.0 - sg)))
        dv = da * (cg * sg)
        duc_ref[0] = dg.astype(BF16)
        duc_ref[1] = dv.astype(BF16)
        sums_ref[0] += _fold8(dg)
        sums_ref[1] += _fold8(dv)

        @pl.when(i == s // tm - 1)
        def _():
            _spread_total(sums_ref)

    pair = pl.BlockSpec((2, tm, tn), lambda j, i: (0, i, j))
    return pl.pallas_call(
        body, name=name, grid=(D_FF // tn, s // tm), in_specs=[pair, pl.BlockSpec((tm, tn), lambda j, i: (i, j))],
        out_specs=[pair, pl.BlockSpec((2, 8, tn), lambda j, i: (0, 0, j))],
        out_shape=[jax.ShapeDtypeStruct((2, s, D_FF), BF16), jax.ShapeDtypeStruct((2, 8, D_FF), F32)],
        compiler_params=_params("parallel", "arbitrary"))(uc, dact)


def _conv_bwd(duc, u, conv_w, *, name):
    _, s, _ = duc.shape
    tm, tn = FFN_TM, FFN_TN
    nj, ni = D_FF // tn, s // tm

    def body(d_ref, halo_ref, u_ref, w_ref, du_ref, sums_ref):
        i = pl.program_id(2)

        @pl.when(i == 0)
        def _():
            sums_ref[...] = jnp.zeros_like(sums_ref)

        d = d_ref[0].astype(F32)
        ext = jnp.concatenate([d, jnp.where(i == ni - 1, 0.0, halo_ref[0].astype(F32))], axis=0)
        n = tm + HALO
        d1 = pltpu.roll(ext, n - 1, 0)[:tm]
        d2 = pltpu.roll(ext, n - 2, 0)[:tm]
        du_ref[...] = (w_ref[2:3, :] * d + w_ref[1:2, :] * d1 + w_ref[0:1, :] * d2).astype(BF16)
        uv = u_ref[...].astype(F32)
        for t, shifted in enumerate((d2, d1, d)):
            sums_ref[0, t] += _fold8(shifted * uv)

        @pl.when(i == ni - 1)
        def _():
            _spread_total(sums_ref)

    return pl.pallas_call(
        body, name=name, grid=(2, nj, ni),
        in_specs=[pl.BlockSpec((1, tm, tn), lambda g, j, i: (g, i, j)),
                  pl.BlockSpec((1, HALO, tn), lambda g, j, i: (g, jnp.minimum((i + 1) * (tm // HALO), s // HALO - 1), j)),
                  pl.BlockSpec((tm, tn), lambda g, j, i: (i, g * nj + j)),
                  pl.BlockSpec((3, tn), lambda g, j, i: (0, g * nj + j))],
        out_specs=[pl.BlockSpec((tm, tn), lambda g, j, i: (i, g * nj + j)),
                   pl.BlockSpec((1, 3, 8, tn), lambda g, j, i: (g, 0, 0, j))],
        out_shape=[jax.ShapeDtypeStruct((s, 2 * D_FF), BF16), jax.ShapeDtypeStruct((2, 3, 8, D_FF), F32)],
        compiler_params=_params("parallel", "parallel", "arbitrary"))(duc, duc, u, conv_w)


def _loss_head(x1, ffn, gate, target, *, name):
    s, d = x1.shape
    tm = 2 * ROW_TILE

    def body(x_ref, f_ref, g_ref, t_ref, dy_ref, df_ref, sums_ref):
        i = pl.program_id(0)

        @pl.when(i == 0)
        def _():
            sums_ref[...] = jnp.zeros_like(sums_ref)

        f = f_ref[...]
        err = x_ref[...] + g_ref[...] * f - t_ref[...]
        dy = err * (1.0 / d)
        dy_ref[...] = dy
        df_ref[...] = (g_ref[...] * dy).astype(BF16)
        sums_ref[0] += _fold8(dy * f)
        sums_ref[1] += _fold8(err * err)

        @pl.when(i == s // tm - 1)
        def _():
            _spread_total(sums_ref)

    row = pl.BlockSpec((tm, d), lambda i: (i, 0))
    return pl.pallas_call(
        body, name=name, grid=(s // tm,), in_specs=[row, row, pl.BlockSpec((1, d), lambda i: (0, 0)), row],
        out_specs=[row, row, pl.BlockSpec((2, 8, d), lambda i: (0, 0, 0))],
        out_shape=[jax.ShapeDtypeStruct((s, d), F32), jax.ShapeDtypeStruct((s, d), BF16), jax.ShapeDtypeStruct((2, 8, d), F32)],
        compiler_params=_params("arbitrary"))(x1, ffn, gate, target)


def _adamw(w, g, m, v, *, name):
    rows, cols = w.shape
    split = isinstance(g, tuple)
    if rows % 8 == 0 or rows <= ADAM_TILE:
        span = rows // 2 if split else rows
        tm = next((t for t in range(ADAM_TILE, 7, -8) if span % t == 0), span)
        shape, at, steps, per_half = (tm, cols), (lambda i: (i, 0)), rows // tm, span // tm
    else:
        shape, at, steps, per_half = (rows, ADAM_TILE), (lambda i: (0, i)), cols // ADAM_TILE, cols // ADAM_TILE // 2

    def update(gv, w_ref, m_ref, v_ref, d_ref, mo_ref, vo_ref):
        mn = ADAM_B1 * m_ref[...] + (1.0 - ADAM_B1) * gv
        vn = ADAM_B2 * v_ref[...] + (1.0 - ADAM_B2) * (gv * gv)
        m_hat = mn / (1.0 - ADAM_B1 ** ADAM_STEP)
        v_hat = vn / (1.0 - ADAM_B2 ** ADAM_STEP)
        d_ref[...] = -ADAM_LR * (m_hat / (jnp.sqrt(v_hat) + ADAM_EPS) + ADAM_WD * w_ref[...])
        mo_ref[...] = mn
        vo_ref[...] = vn

    out_shape = [jax.ShapeDtypeStruct((rows, cols), F32)] * (4 if split else 3)
    if not split:
        def body(w_ref, g_ref, m_ref, v_ref, d_ref, mo_ref, vo_ref):
            update(g_ref[...], w_ref, m_ref, v_ref, d_ref, mo_ref, vo_ref)

        blk = pl.BlockSpec(shape, at)
        return pl.pallas_call(body, name=name, grid=(steps,), in_specs=[blk] * 4, out_specs=[blk] * 3, out_shape=out_shape,
                              compiler_params=_params("parallel"))(w, g, m, v)

    mine, other, core = g

    def body(core_ref, w_ref, mine_ref, other_ref, m_ref, v_ref, d_ref, mo_ref, vo_ref, g_ref):
        gv = jnp.where(pl.program_id(0) // per_half == core_ref[0], mine_ref[...], other_ref[...])
        g_ref[...] = gv
        update(gv, w_ref, m_ref, v_ref, d_ref, mo_ref, vo_ref)

    blk = pl.BlockSpec(shape, lambda i, core_ref: at(i))
    half = pl.BlockSpec(shape, lambda i, core_ref: at(i % per_half))
    return pl.pallas_call(
        body, name=name, out_shape=out_shape, compiler_params=_params("parallel"),
        grid_spec=pltpu.PrefetchScalarGridSpec(num_scalar_prefetch=1, grid=(steps,), in_specs=[blk, half, half, blk, blk],
                                               out_specs=[blk] * 4))(core, w, mine, other, m, v)


def _colsum(t):
    return t[..., 0, :]


def _in_proj_layout(w_in):
    pad = jnp.zeros((w_in.shape[0], PROJ_W - C_LR - GLA_GATE_RANK), w_in.dtype)
    return jnp.concatenate([w_in[:, :1536], w_in[:, 1552:], w_in[:, 1536:1552], pad], axis=1)


def _in_proj_grad_layout(g):
    return jnp.concatenate([g[:, :1536], g[:, C_LR:C_LR + GLA_GATE_RANK], g[:, 1536:C_LR]], axis=1)


def _gate_layout(gla_w_gate):
    return jnp.pad(gla_w_gate, ((0, HEAD_LANES - GLA_GATE_RANK), (0, 0))).astype(BF16)


def _local_step(x, target, mod, wi, wo, ffn_weights, ffn_grads_ready, attn_grads_ready, conv_w, conv_b, wg, bg, gn, qg, kg, n1g, n2g):
    d = D_MODEL
    sh1, sc1, g1, sh2, sc2, g2 = [mod[:, i * d:(i + 1) * d] for i in range(6)]
    qg8, kg8 = jnp.tile(qg, (1, 8)), jnp.tile(kg, (1, 8))

    _, h1, h1_t = _norm_mod_fwd(x, None, None, n1g, sc1, sh1, name="norm1_fwd")
    proj = _mm(h1, wi, tm=1024, tn=PROJ_W, tk=d, name="in_proj")
    o_raw, y_gla, states = _gla_fwd(proj, wg, bg, gn, name="gla_fwd")
    qa, ka = _attn_prep(proj, qg8, kg8, name="attn_prep")
    sparse = [_dil_attn_fwd(qa, ka, proj, dil, name=f"attn_fwd_d{dil}") for dil in DILATIONS[1:]]
    mixed, y_att, lse = _dense_attn_fwd_merge(qa, ka, proj, sparse, y_gla, name="attn_fwd_d1_merge")
    attn_out = _mm(mixed, wo, tm=1024, tn=d, tk=d, name="out_proj")
    x1, h2, h2_t = _norm_mod_fwd(x, attn_out, g1, n2g, sc2, sh2, name="norm2_fwd")
    wup, wdown = ffn_weights(h2)
    u = _mm(h2, wup, out_dtype=BF16, tm=1024, tn=D_FF, tk=d, name="up_proj")
    act, uc = _conv_swiglu_fwd(u, conv_w, conv_b, name="conv_swiglu_fwd")
    ffn = _mm(act, wdown, tm=1024, tn=d, tk=D_FF, name="down_proj")
    dy, dffn, head_sums = _loss_head(x1, ffn, g2, target, name="loss_head")

    dact = _mm(dffn, wdown, tb=True, out_dtype=BF16, tm=1024, tn=D_FF, tk=d, name="down_proj_dx")
    g_wdown, g_wdown_b = _mm(act, dffn, ta=True, tm=1408, tn=d, tk=2048, also_bf16=True, name="down_proj_dw")
    duc, bias_sums = _swiglu_bwd(uc, dact, name="swiglu_bwd")
    du, tap_sums = _conv_bwd(duc, u, conv_w, name="conv_bwd")
    dh2 = _mm(du, wup, tb=True, tm=1024, tn=d, tk=D_FF, name="up_proj_dx")
    g_wup, g_wup_b = _mm(h2_t, du, tm=d, tn=1408, tk=2048, shard_cols=True, also_bf16=True, name="up_proj_dw")
    token = ffn_grads_ready(g_wup_b, g_wdown_b)
    g1_late = g1 if token is None else g1 + token[0:1, 0:1]
    dx1, dao, n2_sums = _norm_mod_bwd(x1, dh2, dy, n2g, sc2, attn_out, g1_late, name="norm2_bwd")

    dmixed = _mm(dao, wo, tb=True, tm=1024, tn=d, tk=d, name="out_proj_dx")
    g_wo = _mm(mixed, dao, ta=True, tm=d, tn=d, tk=1024, name="out_proj_dw")
    dgq, dgk, dgv, dgr, dlr, g_wg, gla_sums = _gla_bwd(proj, wg, bg, gn, o_raw, states, dmixed, name="gla_bwd")
    parts = [_dil_attn_bwd(qa, ka, proj, y_att, lse, dmixed, dil, name=f"attn_bwd_d{dil}") for dil in DILATIONS]
    daq, dak, dav, qk_sums = _attn_post(parts, proj, qg8, kg8, name="attn_post")
    dproj = jnp.concatenate([dgq, dgk, dgv, dgr, daq, dak, dav, dlr], axis=1)
    g_wi = _mm(h1_t, dproj, tm=512, tn=PROJ_W, tk=2048, name="in_proj_dw")
    token = attn_grads_ready(g_wi, g_wo)
    wi_late = wi if token is None else wi + token[0:1, 0:1].astype(BF16)
    dh1 = _mm(dproj, wi_late, tb=True, tm=1024, tn=d, tk=PROJ_W, name="in_proj_dx")
    grad_x, _, n1_sums = _norm_mod_bwd(x, dh1, dx1, n1g, sc1, None, None, name="norm1_bwd")

    n1, n2, hs, taps, cb = _colsum(n1_sums), _colsum(n2_sums), _colsum(head_sums), _colsum(tap_sums), _colsum(bias_sums)
    gs, qs = _colsum(gla_sums), _colsum(qk_sums)
    dmod = jnp.concatenate([n1[1], n1[0] * n1g[0], n2[2], n2[1], n2[0] * n2g[0], hs[0]])
    small = dict(
        dmod=dmod,
        norm1_g=n1[0] * (1.0 + sc1[0]), norm2_g=n2[0] * (1.0 + sc2[0]),
        gla_w_gate=g_wg[:GLA_GATE_RANK], gla_b_gate=gs[0, :256], gla_norm_g=gs[1].reshape(4, 128).sum(axis=0),
        q_norm_g=qs[0].reshape(8, 64).sum(axis=0), k_norm_g=qs[1].reshape(8, 64).sum(axis=0),
        conv_w=jnp.concatenate([taps[0], taps[1]], axis=1), conv_b=jnp.concatenate([cb[0], cb[1]]),
    )
    return head_sums[1], grad_x, (g_wi, g_wo, g_wup, g_wdown), small


N_DEV, N_CHIP = 8, 4
ANY = pl.BlockSpec(memory_space=pl.ANY)
VMEM_SPEC = pl.BlockSpec(memory_space=pltpu.VMEM)


def _place():
    x, y, c = lax.axis_index("x"), lax.axis_index("y"), lax.axis_index("c")
    other_chips = [(1 - x, y), (x, 1 - y), (1 - x, 1 - y)]
    return x, y, c, (x, y, 1 - c), other_chips


def _all_gather_small(v, *, name):
    m, n = v.shape

    def body(v_ref, out_ref, send_sems, recv_sems, local_sem):
        x, y, c, sibling, chips = _place()
        me = (x, y, c)

        def rows(px, py, pc):
            return out_ref.at[pl.ds((4 * px + 2 * py + pc) * m, m), :]

        def copy(k, block, to, src=None):
            return pltpu.make_async_remote_copy(
                src_ref=rows(*block) if src is None else src, dst_ref=rows(*block), send_sem=send_sems.at[k],
                recv_sem=recv_sems.at[k], device_id=to, device_id_type=MESH)

        mine = pltpu.make_async_copy(v_ref, rows(*me), local_sem)
        mine.start()
        first = [copy(0, me, sibling, src=v_ref)]
        first += [copy(1 + j, me, (*chip, c), src=v_ref) for j, chip in enumerate(chips)]
        for cp in first:
            cp.start()
        passed = [copy(4 + j, (*chip, c), sibling) for j, chip in enumerate(chips)]
        for j, chip in enumerate(chips):
            copy(1 + j, (*chip, c), me).wait_recv()
            passed[j].start()
        copy(0, sibling, me).wait_recv()
        for j, chip in enumerate(chips):
            copy(4 + j, (*chip, 1 - c), me).wait_recv()
        for cp in first + passed:
            cp.wait_send()
        mine.wait()

    return pl.pallas_call(
        body, name=name, out_shape=jax.ShapeDtypeStruct((N_DEV * m, n), v.dtype), in_specs=[VMEM_SPEC], out_specs=VMEM_SPEC,
        scratch_shapes=[pltpu.SemaphoreType.DMA((7,)), pltpu.SemaphoreType.DMA((7,)), pltpu.SemaphoreType.DMA],
    )(v)


def _gather_weight_shards(shards, *, name):
    nw = len(shards)

    def body(*refs):
        srcs, outs, (send_sems, recv_sems) = refs[:nw], refs[nw:2 * nw], refs[2 * nw:]
        x, y, c, sibling, chips = _place()
        index = lambda chip: 2 * chip[0] + chip[1]

        def copy(w, k, src, dst, to):
            return pltpu.make_async_remote_copy(src_ref=src, dst_ref=dst, send_sem=send_sems.at[6 * w + k],
                                                recv_sem=recv_sems.at[6 * w + k], device_id=to, device_id_type=MESH)

        sent = []
        for w, (src_ref, out_ref) in enumerate(zip(srcs, outs)):
            for k, chip in enumerate(chips):
                sent.append(copy(w, k, src_ref.at[c], out_ref.at[2 * x + y, c], (*chip, c)))
                sent[-1].start()
        for w, out_ref in enumerate(outs):
            for k, chip in enumerate(chips):
                landed = out_ref.at[index(chip), c]
                copy(w, k, landed, landed, (*chip, c)).wait_recv()
                sent.append(copy(w, 3 + k, landed, landed, sibling))
                sent[-1].start()
        for w, out_ref in enumerate(outs):
            for k, chip in enumerate(chips):
                passed_on = out_ref.at[index(chip), 1 - c]
                copy(w, 3 + k, passed_on, passed_on, sibling).wait_recv()
        for cp in sent:
            cp.wait_send()

    return pl.pallas_call(
        body, name=name, out_shape=[jax.ShapeDtypeStruct((N_CHIP, *s.shape), s.dtype) for s in shards],
        in_specs=[ANY] * nw, out_specs=[ANY] * nw,
        scratch_shapes=[pltpu.SemaphoreType.DMA((6 * nw,)), pltpu.SemaphoreType.DMA((6 * nw,))],
    )(*shards)


HBM_SPEC = pl.BlockSpec(memory_space=pltpu.HBM)
SEM_SPEC = pl.BlockSpec(memory_space=pltpu.SEMAPHORE)
DATAFLOW_EFFECT = pltpu.SideEffectType.DATAFLOW_SIDE_EFFECTING


def _late_copies(srcs, lands, send_sems, recv_sems):
    x, y, c, _, chips = _place()
    return [pltpu.make_async_remote_copy(
        src_ref=src.at[c], dst_ref=land.at[2 * x + y, c], send_sem=send_sems.at[6 * w + 2 * r + core],
        recv_sem=recv_sems.at[6 * w + 2 * r + c], device_id=(*chip, core), device_id_type=MESH)
        for w, (src, land) in enumerate(zip(srcs, lands)) for r, chip in enumerate(chips) for core in range(2)]


def _gather_late_start(own, after, *, name):
    nw = len(own)

    def body(*refs):
        srcs, lands, send_sems, recv_sems, token = refs[:nw], refs[nw:2 * nw], refs[2 * nw + 1], refs[2 * nw + 2], refs[-1]
        for cp in _late_copies(srcs, lands, send_sems, recv_sems):
            cp.start()
        token[...] = jnp.zeros_like(token)

    lands = [pltpu.with_memory_space_constraint(lax.empty((N_CHIP, *s.shape), s.dtype), pltpu.HBM) for s in own]
    own = [pltpu.with_memory_space_constraint(s, pltpu.HBM) for s in own]
    out = pl.pallas_call(
        body, name=name,
        out_shape=(pltpu.SemaphoreType.DMA((6 * nw,)), pltpu.SemaphoreType.DMA((6 * nw,)),
                   *[pltpu.HBM(s.shape, s.dtype) for s in own], *[pltpu.HBM(s.shape, s.dtype) for s in lands],
                   jax.ShapeDtypeStruct((8, 128), F32)),
        in_specs=[HBM_SPEC] * (2 * nw) + [ANY], out_specs=(SEM_SPEC, SEM_SPEC, *[HBM_SPEC] * (2 * nw), VMEM_SPEC),
        input_output_aliases={i: 2 + i for i in range(2 * nw)},
        compiler_params=pltpu.CompilerParams(has_side_effects=DATAFLOW_EFFECT))(*own, *lands, after)
    return out[0], out[1], out[2:2 + nw], out[2 + nw:2 + 2 * nw], out[-1]


def _gather_late_wait(send_sems, recv_sems, own, lands, after, *, name):
    nw = len(own)

    def body(*refs):
        srcs, lands_in, send_sems, recv_sems = refs[:nw], refs[nw:2 * nw], refs[2 * nw], refs[2 * nw + 1]
        x, y, c, _, chips = _place()
        for cp in _late_copies(srcs, lands_in, send_sems, recv_sems):
            cp.wait_send()
        for w, (src, land) in enumerate(zip(srcs, lands_in)):
            for r, chip in enumerate(chips):
                for core in range(2):
                    pltpu.make_async_remote_copy(
                        src_ref=src.at[c], dst_ref=land.at[2 * chip[0] + chip[1], core], send_sem=send_sems.at[6 * w + 2 * r + core],
                        recv_sem=recv_sems.at[6 * w + 2 * r + core], device_id=(*chip, core), device_id_type=MESH).wait_recv()

    out = pl.pallas_call(
        body, name=name, out_shape=(*[pltpu.HBM(s.shape, s.dtype) for s in own], *[pltpu.HBM(s.shape, s.dtype) for s in lands]),
        in_specs=[HBM_SPEC] * (2 * nw) + [SEM_SPEC, SEM_SPEC, ANY], out_specs=tuple([HBM_SPEC] * (2 * nw)),
        input_output_aliases={i: i for i in range(2 * nw)},
        compiler_params=pltpu.CompilerParams(has_side_effects=DATAFLOW_EFFECT))(*own, *lands, send_sems, recv_sems, after)
    return out[:nw], out[nw:]


def _direct_reduce_copies(srcs, lands, send_sems, recv_sems):
    x, y, c, _, _ = _place()
    cps = []
    for w, (src, land) in enumerate(zip(srcs, lands)):
        for rel in range(1, N_DEV):
            tx, ty, tc = (1 - x if rel & 4 else x), (1 - y if rel & 2 else y), (1 - c if rel & 1 else c)
            cps.append(pltpu.make_async_remote_copy(
                src_ref=src.at[2 * tx + ty, tc], dst_ref=land.at[rel - 1], send_sem=send_sems.at[7 * w + rel - 1],
                recv_sem=recv_sems.at[7 * w + rel - 1], device_id=(tx, ty, tc), device_id_type=MESH))
    return cps


def _direct_reduce_start(grads, *, name):
    nw = len(grads)

    def body(*refs):
        srcs, lands, send_sems, recv_sems, token = refs[:nw], refs[nw:2 * nw], refs[2 * nw], refs[2 * nw + 1], refs[-1]
        for cp in _direct_reduce_copies(srcs, lands, send_sems, recv_sems):
            cp.start()
        token[...] = jnp.zeros_like(token)

    lands = [pltpu.with_memory_space_constraint(lax.empty((N_DEV - 1, *g.shape[2:]), g.dtype), pltpu.HBM) for g in grads]
    grads = [pltpu.with_memory_space_constraint(g, pltpu.HBM) for g in grads]
    out = pl.pallas_call(
        body, name=name,
        out_shape=(pltpu.SemaphoreType.DMA((7 * nw,)), pltpu.SemaphoreType.DMA((7 * nw,)),
                   *[pltpu.HBM(g.shape, g.dtype) for g in grads], *[pltpu.HBM(t.shape, t.dtype) for t in lands],
                   jax.ShapeDtypeStruct((8, 128), F32)),
        in_specs=[HBM_SPEC] * (2 * nw), out_specs=(SEM_SPEC, SEM_SPEC, *[HBM_SPEC] * (2 * nw), VMEM_SPEC),
        input_output_aliases={i: 2 + i for i in range(2 * nw)},
        compiler_params=pltpu.CompilerParams(has_side_effects=DATAFLOW_EFFECT))(*grads, *lands)
    return out[0], out[1], out[2:2 + nw], out[2 + nw:2 + 2 * nw], out[-1]


def _direct_reduce_wait(send_sems, recv_sems, grads, lands, after, *, name):
    nw = len(grads)

    def body(*refs):
        srcs, lands_in, send_sems, recv_sems = refs[:nw], refs[nw:2 * nw], refs[2 * nw], refs[2 * nw + 1]
        cps = _direct_reduce_copies(srcs, lands_in, send_sems, recv_sems)
        for cp in cps:
            cp.wait_send()
        for cp in cps:
            cp.wait_recv()

    out = pl.pallas_call(
        body, name=name, out_shape=(*[pltpu.HBM(g.shape, g.dtype) for g in grads], *[pltpu.HBM(t.shape, t.dtype) for t in lands]),
        in_specs=[HBM_SPEC] * (2 * nw) + [SEM_SPEC, SEM_SPEC, ANY], out_specs=tuple([HBM_SPEC] * (2 * nw)),
        input_output_aliases={i: i for i in range(2 * nw)},
        compiler_params=pltpu.CompilerParams(has_side_effects=DATAFLOW_EFFECT))(*grads, *lands, send_sems, recv_sems, after)
    return out[nw:]


def _direct_reduce_add(grad, landed, chip, core, *, name):
    _, r, n = grad.shape
    half = r // 2
    tr = _row_tile(half)
    nb = half // tr

    def body(chip_ref, core_ref, g_ref, t_ref, o_ref):
        acc = g_ref[0]
        for k in range(N_DEV - 1):
            acc = acc + t_ref[k].astype(F32)
        o_ref[...] = acc

    return pl.pallas_call(
        body, name=name,
        grid_spec=pltpu.PrefetchScalarGridSpec(
            num_scalar_prefetch=2, grid=(nb,),
            in_specs=[pl.BlockSpec((1, tr, n), lambda i, chip_ref, core_ref: (chip_ref[0], core_ref[0] * nb + i, 0)),
                      pl.BlockSpec((N_DEV - 1, tr, n), lambda i, chip_ref, core_ref: (0, i, 0))],
            out_specs=pl.BlockSpec((tr, n), lambda i, chip_ref, core_ref: (i, 0))),
        out_shape=jax.ShapeDtypeStruct((half, n), F32), compiler_params=_params("parallel"))(chip, core, grad, landed)


def _share_halves(halves, *, name):
    nw = len(halves)

    def body(*refs):
        srcs, outs, (send_sems, recv_sems) = refs[:nw], refs[nw:2 * nw], refs[2 * nw:]
        _, _, _, sibling, _ = _place()
        cps = [pltpu.make_async_remote_copy(src_ref=src_ref, dst_ref=out_ref, send_sem=send_sems.at[w], recv_sem=recv_sems.at[w],
                                            device_id=sibling, device_id_type=MESH)
               for w, (src_ref, out_ref) in enumerate(zip(srcs, outs))]
        for cp in cps:
            cp.start()
        for cp in cps:
            cp.wait()

    return pl.pallas_call(
        body, name=name, out_shape=[jax.ShapeDtypeStruct(h.shape, h.dtype) for h in halves],
        in_specs=[ANY] * nw, out_specs=[ANY] * nw,
        scratch_shapes=[pltpu.SemaphoreType.DMA((nw,)), pltpu.SemaphoreType.DMA((nw,))])(*halves)


def _row_tile(rows, limit=256):
    return next(t for t in range(limit, 15, -16) if rows % t == 0)


def _sum_devices(gathered, *, name):
    _, m, n = gathered.shape

    def body(g_ref, tot_ref, loss_ref):
        tot = g_ref[0]
        for dev in range(1, N_DEV):
            tot = tot + g_ref[dev]
        tot_ref[...] = tot
        loss_ref[...] = jnp.full((8, n), (0.5 / D_MODEL) * jnp.sum(tot[0:8]), F32)

    return pl.pallas_call(body, name=name, in_specs=[VMEM_SPEC], out_specs=[VMEM_SPEC, VMEM_SPEC],
                          out_shape=[jax.ShapeDtypeStruct((m, n), F32), jax.ShapeDtypeStruct((8, n), F32)])(gathered)


def _ada_mod(cond_all, w_ada_shard, *, name):
    tn = 512

    def body(a_ref, b_ref, o_ref):
        o_ref[...] = _nn(a_ref[...], b_ref[...], precision=HIGHEST)

    return pl.pallas_call(
        body, name=name, grid=(w_ada_shard.shape[1] // tn,),
        in_specs=[pl.BlockSpec(cond_all.shape, lambda j: (0, 0)), pl.BlockSpec((D_MODEL, tn), lambda j: (0, j))],
        out_specs=pl.BlockSpec((N_DEV, tn), lambda j: (0, j)),
        out_shape=jax.ShapeDtypeStruct((N_DEV, w_ada_shard.shape[1]), F32), compiler_params=_params("parallel"))(cond_all, w_ada_shard)


def _ada_grad(cond_all, dmod_cols, *, name):
    tm = 256

    def body(a_ref, b_ref, o_ref):
        o_ref[...] = lax.dot_general(a_ref[...], b_ref[...], (((0,), (0,)), ((), ())), precision=HIGHEST,
                                     preferred_element_type=F32)

    return pl.pallas_call(
        body, name=name, grid=(D_MODEL // tm,),
        in_specs=[pl.BlockSpec((N_DEV, tm), lambda i: (0, i)), pl.BlockSpec(dmod_cols.shape, lambda i: (0, 0))],
        out_specs=pl.BlockSpec((tm, dmod_cols.shape[1]), lambda i: (i, 0)),
        out_shape=jax.ShapeDtypeStruct((D_MODEL, dmod_cols.shape[1]), F32), compiler_params=_params("parallel"))(cond_all, dmod_cols)


def _silu_rows(c8, *, name):
    def body(c_ref, o_ref):
        cv = c_ref[...]
        o_ref[...] = cv * _sigmoid(cv)

    return pl.pallas_call(body, name=name, in_specs=[VMEM_SPEC], out_specs=VMEM_SPEC,
                          out_shape=jax.ShapeDtypeStruct(c8.shape, F32))(c8)


def _rows128(t, rows=None):
    flat = t.reshape(-1, 128)
    return flat if rows is None else jnp.pad(flat, ((0, rows - flat.shape[0]), (0, 0)))


def _from_col_shards(shards, r, n):
    return shards.reshape(N_CHIP, r, n).transpose(1, 0, 2).reshape(r, N_CHIP * n)


def kernel(x, c, w_ada, b_ada, norm1_g, w_in, gla_w_gate, gla_b_gate, gla_norm_g, q_norm_g, k_norm_g, w_out, norm2_g, w_up, conv_w, conv_b, w_down, loss_target, m_w_ada, m_b_ada, m_norm1_g, m_w_in, m_gla_w_gate, m_gla_b_gate, m_gla_norm_g, m_q_norm_g, m_k_norm_g, m_w_out, m_norm2_g, m_w_up, m_conv_w, m_conv_b, m_w_down, v_w_ada, v_b_ada, v_norm1_g, v_w_in, v_gla_w_gate, v_gla_b_gate, v_gla_norm_g, v_q_norm_g, v_k_norm_g, v_w_out, v_norm2_g, v_w_up, v_conv_w, v_conv_b, v_w_down):
    d = D_MODEL
    ax, ay, ac = lax.axis_index("x"), lax.axis_index("y"), lax.axis_index("c")
    chip, dev = 2 * ax + ay, 4 * ax + 2 * ay + ac

    cond = _silu_rows(jnp.broadcast_to(c, (8, d)), name="cond_silu")[0:1]
    small_in = jnp.concatenate([_rows128(cond), _rows128(conv_w[0]), _rows128(gla_w_gate[0])], axis=0)
    small_in = _rows128(small_in, 56)
    got = _all_gather_small(small_in, name="gather_small").reshape(N_DEV, 56, 128)
    cond_all = got[:, 0:8].reshape(N_DEV, d)
    conv_w_full = _from_col_shards(got[0::2, 8:41].reshape(N_CHIP, 3 * 1408 // 128, 128), 3, 1408)
    gate_full = _from_col_shards(got[0::2, 41:49].reshape(N_CHIP, 16 * 64 // 128, 128), GLA_GATE_RANK, 64)
    mod_part = _ada_mod(cond_all, w_ada[0], name="ada_mod")
    mod_got = _all_gather_small(_rows128(mod_part), name="gather_mod").reshape(N_DEV, N_DEV, 1536)
    mod_all = mod_got[0::2].transpose(1, 0, 2).reshape(N_DEV, 6 * d) + b_ada
    mod = lax.dynamic_slice_in_dim(mod_all, dev, 1, axis=0)

    own = [w[0].astype(BF16).reshape(2, w.shape[1] // 2, w.shape[2]) for w in (w_in, w_out, w_up, w_down)]
    with_own = lambda got, mine: [lax.dynamic_update_index_in_dim(t, o, chip, 0) for t, o in zip(got, mine)]
    got_in, got_out = with_own(_gather_weight_shards(own[:2], name="gather_weights"), own[:2])
    w_in_full = got_in.reshape(N_CHIP, d, 772).transpose(1, 0, 2).reshape(d, N_CHIP * 772)
    w_out_full = got_out.reshape(d, d)
    exchanged = mod_all[0:1, 0:1] + got_in[0, 0, 0:1, 0:1].astype(F32)
    send_sems, recv_sems, own_thru, lands, token = _gather_late_start(own[2:], exchanged, name="gather_late_start")
    mod = mod + token[0:1, 0:1]

    def ffn_weights(after):
        mine, landed = _gather_late_wait(send_sems, recv_sems, own_thru, lands, after, name="gather_late_wait")
        got_up, got_down = with_own(landed, mine)
        return got_up.reshape(N_CHIP, d, 1408).transpose(1, 0, 2).reshape(d, 2 * D_FF), got_down.reshape(D_FF, d)

    ffn_reduce, attn_reduce, attn_parts = [], [], []
    halves_of = lambda g: g.reshape(N_CHIP, 2, g.shape[-2] // 2, g.shape[-1])

    def ffn_grads_ready(g_wup_b, g_wdown_b):
        ffn_reduce.extend(_direct_reduce_start([halves_of(g_wup_b), halves_of(g_wdown_b.reshape(N_CHIP, D_FF // N_CHIP, d))],
                                               name="reduce_ffn_start"))
        return ffn_reduce[4]

    def attn_grads_ready(g_wi, g_wo):
        attn_parts.extend([_in_proj_grad_layout(g_wi).reshape(d, N_CHIP, 772).transpose(1, 0, 2), g_wo.reshape(N_CHIP, d // N_CHIP, d)])
        attn_reduce.extend(_direct_reduce_start([halves_of(g.astype(BF16)) for g in attn_parts], name="reduce_attn_start"))
        return attn_reduce[4]

    err2, grad_x, (g_wi, g_wo, g_wup, g_wdown), small = _local_step(
        x[0], loss_target[0], mod, _in_proj_layout(w_in_full), w_out_full, ffn_weights, ffn_grads_ready, attn_grads_ready,
        conv_w_full, conv_b,
        _gate_layout(gate_full), gla_b_gate, gla_norm_g, q_norm_g, k_norm_g, norm1_g, norm2_g)

    pieces = [err2[0], small["dmod"], small["norm1_g"], small["norm2_g"], small["gla_w_gate"].reshape(-1), small["gla_b_gate"],
              small["gla_norm_g"], small["q_norm_g"], small["k_norm_g"], small["conv_w"].reshape(-1), small["conv_b"]]
    sizes = [p.shape[0] for p in pieces]
    at = [sum(sizes[:i]) for i in range(len(sizes) + 1)]
    vec = _rows128(jnp.concatenate(pieces), 288)
    got = _all_gather_small(vec, name="gather_grads").reshape(N_DEV, 288, 128)
    total, loss8 = _sum_devices(got, name="sum_devices")
    total = total.reshape(-1)
    seg = lambda i: total[at[i]:at[i + 1]]
    dmod_all = got.reshape(N_DEV, -1)[:, at[1]:at[2]]
    g_small = dict(
        b_ada=seg(1)[None], norm1_g=seg(2)[None], norm2_g=seg(3)[None],
        gla_w_gate=lax.dynamic_slice_in_dim(seg(4).reshape(GLA_GATE_RANK, 256), chip * 64, 64, axis=1),
        gla_b_gate=seg(5)[None], gla_norm_g=seg(6)[None], q_norm_g=seg(7)[None], k_norm_g=seg(8)[None],
        conv_w=lax.dynamic_slice_in_dim(seg(9).reshape(3, 2 * D_FF), chip * 1408, 1408, axis=1), conv_b=seg(10)[None])
    dmod_cols = lax.dynamic_slice_in_dim(dmod_all.reshape(N_DEV, 6 * d), chip * 1536, 1536, axis=1)
    g_w_ada = _ada_grad(cond_all, dmod_cols, name="ada_grad")

    core_id, chip_id = jnp.reshape(ac, (1,)).astype(jnp.int32), jnp.reshape(chip, (1,)).astype(jnp.int32)
    landed = (_direct_reduce_wait(*attn_reduce[:4], grad_x, name="reduce_attn_wait")
              + _direct_reduce_wait(*ffn_reduce[:4], grad_x, name="reduce_ffn_wait"))
    own = attn_parts + [g_wup, g_wdown.reshape(N_CHIP, D_FF // N_CHIP, d)]
    summed = [_direct_reduce_add(g, t, chip_id, core_id, name=f"reduce_add_{tag}")
              for g, t, tag in zip(own, landed, ("w_in", "w_out", "w_up", "w_down"))]
    others = _share_halves(summed, name="share_pair")

    grads = dict(w_ada=g_w_ada, **g_small, **dict(zip(("w_in", "w_out", "w_up", "w_down"), zip(summed, others))))
    names = ["w_ada", "b_ada", "norm1_g", "w_in", "gla_w_gate", "gla_b_gate", "gla_norm_g", "q_norm_g", "k_norm_g", "w_out",
             "norm2_g", "w_up", "conv_w", "conv_b", "w_down"]
    ws = dict(w_ada=w_ada, b_ada=b_ada, norm1_g=norm1_g, w_in=w_in, gla_w_gate=gla_w_gate, gla_b_gate=gla_b_gate,
              gla_norm_g=gla_norm_g, q_norm_g=q_norm_g, k_norm_g=k_norm_g, w_out=w_out, norm2_g=norm2_g, w_up=w_up,
              conv_w=conv_w, conv_b=conv_b, w_down=w_down)
    ms = dict(w_ada=m_w_ada, b_ada=m_b_ada, norm1_g=m_norm1_g, w_in=m_w_in, gla_w_gate=m_gla_w_gate, gla_b_gate=m_gla_b_gate,
              gla_norm_g=m_gla_norm_g, q_norm_g=m_q_norm_g, k_norm_g=m_k_norm_g, w_out=m_w_out, norm2_g=m_norm2_g, w_up=m_w_up,
              conv_w=m_conv_w, conv_b=m_conv_b, w_down=m_w_down)
    vs = dict(w_ada=v_w_ada, b_ada=v_b_ada, norm1_g=v_norm1_g, w_in=v_w_in, gla_w_gate=v_gla_w_gate, gla_b_gate=v_gla_b_gate,
              gla_norm_g=v_gla_norm_g, q_norm_g=v_q_norm_g, k_norm_g=v_k_norm_g, w_out=v_w_out, norm2_g=v_norm2_g, w_up=v_w_up,
              conv_w=v_conv_w, conv_b=v_conv_b, w_down=v_w_down)
    g_out, d_out, m_out, v_out = [], [], [], []
    for nm in names:
        shape = ws[nm].shape
        flip = (lambda t: t.T) if shape[-1] % 128 and shape[-2] % 128 == 0 else (lambda t: t)
        w2 = flip(ws[nm].reshape(shape[-2:]))
        if isinstance(grads[nm], tuple):
            mine, other = grads[nm]
            dl, mn, vn, g2 = _adamw(w2, (flip(mine), flip(other), core_id), flip(ms[nm].reshape(shape[-2:])),
                                    flip(vs[nm].reshape(shape[-2:])), name=f"adamw_{nm}")
        else:
            g2 = flip(grads[nm].reshape(shape[-2:]))
            dl, mn, vn = _adamw(w2, g2, flip(ms[nm].reshape(shape[-2:])), flip(vs[nm].reshape(shape[-2:])), name=f"adamw_{nm}")
        for outs, t in ((g_out, g2), (d_out, dl), (m_out, mn), (v_out, vn)):
            outs.append(flip(t).reshape(shape))
    return (loss8[0, 0], grad_x[None], *g_out, *d_out, *m_out, *v_out)
```
